```python
import jax, jax.numpy as jnp
from jax import lax
import numpy as np

D_MODEL = 1024
BATCH = 16
SEQ = 2048
DEPTH = 2

ATT_HEAD_DIM = 64
ATT_HEADS = 8
ATT_KV_HEADS = 2
ATT_GROUP = ATT_HEADS // ATT_KV_HEADS
ATT_WIDTH = ATT_HEADS * ATT_HEAD_DIM
ATT_KV_WIDTH = ATT_KV_HEADS * ATT_HEAD_DIM
WINDOW = 128
ATT_BLOCK = 128
ROPE_DIM = ATT_HEAD_DIM // 4
ROPE_THETA = 500000.0
MLSTM_HEADS = 4
MLSTM_HEAD_DIM = 128
MLSTM_WIDTH = MLSTM_HEADS * MLSTM_HEAD_DIM
MLSTM_CHUNK = 128
MLSTM_N_GATES = 4 * MLSTM_HEADS
CONV_K = 3
D_FF = ((8 * D_MODEL // 3 + 255) // 256) * 256
NORM_EPS = 1e-6
SPLIT_SIZES = (ATT_WIDTH, ATT_KV_WIDTH, ATT_KV_WIDTH, MLSTM_WIDTH, MLSTM_WIDTH, MLSTM_WIDTH, MLSTM_WIDTH, MLSTM_N_GATES, 2 * D_MODEL)
IN_WIDTH = sum(SPLIT_SIZES)

kernel_name = 'hybrid_bidir_swa_mlstm_macaron'


def rms_norm(x, g):
    xf = x.astype(jnp.float32)
    y = xf * lax.rsqrt(jnp.mean(xf * xf, axis=-1, keepdims=True) + NORM_EPS)
    return (y * g.astype(jnp.float32)).astype(x.dtype)


def swiglu(h, w_gate, w_up, w_down):
    return (jax.nn.silu(h @ w_gate) * (h @ w_up)) @ w_down


def partial_rope(t, positions):
    half = ROPE_DIM // 2
    inv_freq = jnp.power(jnp.float32(ROPE_THETA), -jnp.arange(half, dtype=jnp.float32) * (2.0 / ROPE_DIM))
    ang = positions.astype(jnp.float32)[:, :, None] * inv_freq
    cos = jnp.cos(ang)[:, :, None, :]
    sin = jnp.sin(ang)[:, :, None, :]
    tr = t[..., :ROPE_DIM].astype(jnp.float32)
    t1, t2 = tr[..., :half], tr[..., half:]
    rot = jnp.concatenate([t1 * cos - t2 * sin, t2 * cos + t1 * sin], axis=-1)
    return jnp.concatenate([rot.astype(t.dtype), t[..., ROPE_DIM:]], axis=-1)


def windowed_gqa_with_sink(q, k, v, sink):
    B, S, _, dh = q.shape
    wb = ATT_BLOCK
    nb = S // wb
    f32 = jnp.float32
    qb = q.astype(f32).reshape(B, nb, wb, ATT_KV_HEADS, ATT_GROUP, dh)
    pad = ((0, 0), (wb, wb), (0, 0), (0, 0))
    kp = jnp.pad(k.astype(f32), pad).reshape(B, nb + 2, wb, ATT_KV_HEADS, dh)
    vp = jnp.pad(v.astype(f32), pad).reshape(B, nb + 2, wb, ATT_KV_HEADS, dh)
    kb = jnp.concatenate([kp[:, :-2], kp[:, 1:-1], kp[:, 2:]], axis=2)
    vb = jnp.concatenate([vp[:, :-2], vp[:, 1:-1], vp[:, 2:]], axis=2)
    s = jnp.einsum('bnqhgd,bnkhd->bnhgqk', qb, kb) * (dh ** -0.5)
    qi = jnp.arange(nb)[:, None, None] * wb + jnp.arange(wb)[None, :, None]
    kj = jnp.arange(nb)[:, None, None] * wb - wb + jnp.arange(3 * wb)[None, None, :]
    valid = (jnp.abs(qi - kj) <= WINDOW) & (kj >= 0) & (kj < S)
    s = jnp.where(valid[None, :, None, None], s, -jnp.inf)
    sink_l = sink.astype(f32).reshape(1, 1, ATT_KV_HEADS, ATT_GROUP, 1, 1)
    m = jnp.maximum(jnp.max(s, axis=-1, keepdims=True), sink_l)
    p = jnp.exp(s - m)
    den = jnp.sum(p, axis=-1, keepdims=True) + jnp.exp(sink_l - m)
    o = jnp.einsum('bnhgqk,bnkhd->bnqhgd', p / den, vb)
    return o.reshape(B, S, ATT_HEADS * dh).astype(q.dtype)


def centred_depthwise_conv(u, w, b):
    S = u.shape[1]
    pad = CONV_K // 2
    up = jnp.pad(u, ((0, 0), (pad, pad), (0, 0)))
    out = up[:, 0:S] * w[0]
    for j in range(1, CONV_K):
        out = out + up[:, j:j + S] * w[j]
    return out + b


def mlstm_chunkwise(q, k, v, log_i, log_f):
    B, H, S, dk = q.shape
    dv = v.shape[-1]
    L = MLSTM_CHUNK
    nc = S // L
    q = q.reshape(B, H, nc, L, dk)
    k = k.reshape(B, H, nc, L, dk)
    v = v.reshape(B, H, nc, L, dv)
    li = log_i.reshape(B, H, nc, L)
    b = jnp.cumsum(log_f.reshape(B, H, nc, L), axis=-1)
    b_tot = b[..., -1]
    a = b_tot[..., None] - b + li
    a_max = jnp.max(a, axis=-1)
    w = jnp.exp(a - a_max[..., None])
    kw = k * w[..., None]
    C_loc = jnp.einsum('bhcsk,bhcsv->bhckv', kw, v)
    n_loc = jnp.sum(kw, axis=3)

    def step(carry, inp):
        C, n, m = carry
        C_l, n_l, am, bt = inp
        m_new = jnp.maximum(bt + m, am)
        s_p = jnp.exp(bt + m - m_new)
        s_l = jnp.exp(am - m_new)
        C_new = s_p[..., None, None] * C + s_l[..., None, None] * C_l
        n_new = s_p[..., None] * n + s_l[..., None] * n_l
        return (C_new, n_new, m_new), (C, n, m)

    init = (jnp.zeros((B, H, dk, dv), jnp.float32), jnp.zeros((B, H, dk), jnp.float32), jnp.zeros((B, H), jnp.float32))
    xs = (jnp.moveaxis(C_loc, 2, 0), jnp.moveaxis(n_loc, 2, 0), jnp.moveaxis(a_max, 2, 0), jnp.moveaxis(b_tot, 2, 0))
    _, (C_in, n_in, m_in) = lax.scan(step, init, xs)
    C_in = jnp.moveaxis(C_in, 0, 2)
    n_in = jnp.moveaxis(n_in, 0, 2)
    m_in = jnp.moveaxis(m_in, 0, 2)

    D = b[..., :, None] - b[..., None, :] + li[..., None, :]
    tri = jnp.tril(jnp.ones((L, L), dtype=bool))
    D = jnp.where(tri, D, -jnp.inf)
    inter = b + m_in[..., None]
    m_t = jnp.maximum(inter, jnp.max(D, axis=-1))
    P = jnp.exp(D - m_t[..., None])
    sc = jnp.einsum('bhctd,bhcsd->bhcts', q, k) * P
    scale_in = jnp.exp(inter - m_t)
    num = jnp.einsum('bhcts,bhcsv->bhctv', sc, v) + scale_in[..., None] * jnp.einsum('bhctk,bhckv->bhctv', q, C_in)
    den = jnp.sum(sc, axis=-1) + scale_in * jnp.einsum('bhctk,bhck->bhct', q, n_in)
    h = num / jnp.maximum(jnp.abs(den), jnp.exp(-m_t))[..., None]
    return h.reshape(B, H, S, dv)


def bidirectional_mlstm(q, k, v, o_pre, gate_pre, gate_bias, norm_g):
    B, S, _ = q.shape
    f32 = jnp.float32

    def heads(t):
        return t.astype(f32).reshape(B, S, MLSTM_HEADS, MLSTM_HEAD_DIM).transpose(0, 2, 1, 3)

    qh = heads(q)
    kh = heads(k) * (MLSTM_HEAD_DIM ** -0.5)
    vh = heads(v)
    g = (gate_pre.astype(f32) + gate_bias.astype(f32)).reshape(B, S, 4, MLSTM_HEADS).transpose(2, 0, 3, 1)
    h_fwd = mlstm_chunkwise(qh, kh, vh, g[0], jax.nn.log_sigmoid(g[1]))

    def flip(t):
        return jnp.flip(t, axis=2)

    h_bwd = flip(mlstm_chunkwise(flip(qh), flip(kh), flip(vh), flip(g[2]), jax.nn.log_sigmoid(flip(g[3]))))
    h = h_fwd + h_bwd
    mu = jnp.mean(h, axis=-1, keepdims=True)
    var = jnp.mean(jnp.square(h - mu), axis=-1, keepdims=True)
    h = (h - mu) * lax.rsqrt(var + NORM_EPS)
    h = h.transpose(0, 2, 1, 3).reshape(B, S, MLSTM_WIDTH) * norm_g.astype(f32)
    return (jax.nn.sigmoid(o_pre.astype(f32)) * h).astype(q.dtype)


def _fwd_setup_inputs(seed: int = 0) -> dict:
    key = jax.random.key(seed)
    ks = jax.random.split(key, 26)
    f32 = jnp.float32
    L = DEPTH

    def dense(k, shape, fan_in):
        return jax.random.normal(k, shape, f32) * (fan_in ** -0.5)

    def gain(k, shape):
        return 1.0 + 0.02 * jax.random.normal(k, shape, f32)

    x = jax.random.normal(ks[0], (BATCH, SEQ, D_MODEL), f32)
    positions = jnp.arange(SEQ, dtype=jnp.int32)[None, :] + jax.random.randint(ks[1], (BATCH, 1), 0, 1024, dtype=jnp.int32)
    forget_base = jnp.linspace(3.0, 6.0, MLSTM_HEADS, dtype=f32)
    is_forget = jnp.array([0.0, 1.0, 0.0, 1.0], f32)
    mlstm_gate_bias = (0.1 * jax.random.normal(ks[8], (L, 4, MLSTM_HEADS), f32) + is_forget[None, :, None] * forget_base[None, None, :]).reshape(L, MLSTM_N_GATES)
    return {
        'x': x,
        'positions': positions,
        'ffn1_norm': gain(ks[2], (L, D_MODEL)),
        'ffn1_w_gate': dense(ks[3], (L, D_MODEL, D_FF), D_MODEL),
        'ffn1_w_up': dense(ks[4], (L, D_MODEL, D_FF), D_MODEL),
        'ffn1_w_down': dense(ks[5], (L, D_FF, D_MODEL), D_FF),
        'mix_norm': gain(ks[6], (L, D_MODEL)),
        'w_in': dense(ks[7], (L, D_MODEL, IN_WIDTH), D_MODEL),
        'mlstm_gate_bias': mlstm_gate_bias,
        'attn_q_norm': gain(ks[9], (L, ATT_HEAD_DIM)),
        'attn_k_norm': gain(ks[10], (L, ATT_HEAD_DIM)),
        'attn_sink': 0.5 * jax.random.normal(ks[11], (L, ATT_HEADS), f32),
        'mlstm_conv_w': dense(ks[12], (L, CONV_K, 2 * MLSTM_WIDTH), CONV_K),
        'mlstm_conv_b': 0.02 * jax.random.normal(ks[13], (L, 2 * MLSTM_WIDTH), f32),
        'mlstm_out_norm': gain(ks[14], (L, MLSTM_WIDTH)),
        'w_branch_attn': dense(ks[15], (L, ATT_WIDTH, D_MODEL), ATT_WIDTH),
        'w_branch_mlstm': dense(ks[16], (L, MLSTM_WIDTH, D_MODEL), MLSTM_WIDTH),
        'w_out': dense(ks[17], (L, D_MODEL, D_MODEL), D_MODEL),
        'ffn2_norm': gain(ks[18], (L, D_MODEL)),
        'ffn2_w_gate': dense(ks[19], (L, D_MODEL, D_FF), D_MODEL),
        'ffn2_w_up': dense(ks[20], (L, D_MODEL, D_FF), D_MODEL),
        'ffn2_w_down': dense(ks[21], (L, D_FF, D_MODEL), D_FF),
        'block_out_norm': gain(ks[22], (L, D_MODEL)),
    }


def _fwd_reference(x, positions, ffn1_norm, ffn1_w_gate, ffn1_w_up, ffn1_w_down, mix_norm, w_in, mlstm_gate_bias, attn_q_norm, attn_k_norm, attn_sink, mlstm_conv_w, mlstm_conv_b, mlstm_out_norm, w_branch_attn, w_branch_mlstm, w_out, ffn2_norm, ffn2_w_gate, ffn2_w_up, ffn2_w_down, block_out_norm):
    B, S, _ = x.shape
    split_idx = np.cumsum(SPLIT_SIZES)[:-1].tolist()
    for l in range(DEPTH):
        x = x + 0.5 * swiglu(rms_norm(x, ffn1_norm[l]), ffn1_w_gate[l], ffn1_w_up[l], ffn1_w_down[l])

        h = rms_norm(x, mix_norm[l])
        proj = h @ w_in[l]
        qa, ka, va, qm, km, vm, om, gm, gmerge = jnp.split(proj, split_idx, axis=-1)

        qa = rms_norm(qa.reshape(B, S, ATT_HEADS, ATT_HEAD_DIM), attn_q_norm[l])
        ka = rms_norm(ka.reshape(B, S, ATT_KV_HEADS, ATT_HEAD_DIM), attn_k_norm[l])
        qa = partial_rope(qa, positions)
        ka = partial_rope(ka, positions)
        va = va.reshape(B, S, ATT_KV_HEADS, ATT_HEAD_DIM)
        y_a = windowed_gqa_with_sink(qa, ka, va, attn_sink[l])

        qk = jax.nn.silu(centred_depthwise_conv(jnp.concatenate([qm, km], axis=-1), mlstm_conv_w[l], mlstm_conv_b[l]))
        qm, km = jnp.split(qk, 2, axis=-1)
        y_m = bidirectional_mlstm(qm, km, vm, om, gm, mlstm_gate_bias[l], mlstm_out_norm[l])

        g_a, g_m = jnp.split(jax.nn.sigmoid(gmerge), 2, axis=-1)
        merged = g_a * (y_a @ w_branch_attn[l]) + g_m * (y_m @ w_branch_mlstm[l])
        x = x + merged @ w_out[l]

        x = x + 0.5 * swiglu(rms_norm(x, ffn2_norm[l]), ffn2_w_gate[l], ffn2_w_up[l], ffn2_w_down[l])
        x = rms_norm(x, block_out_norm[l])
    return x


import jax as _jax
import jax.numpy as _jnp

TWIN_FORMAT = 'train_step'
FWD_PARAMS = ['x', 'positions', 'ffn1_norm', 'ffn1_w_gate', 'ffn1_w_up', 'ffn1_w_down', 'mix_norm', 'w_in', 'mlstm_gate_bias', 'attn_q_norm', 'attn_k_norm', 'attn_sink', 'mlstm_conv_w', 'mlstm_conv_b', 'mlstm_out_norm', 'w_branch_attn', 'w_branch_mlstm', 'w_out', 'ffn2_norm', 'ffn2_w_gate', 'ffn2_w_up', 'ffn2_w_down', 'block_out_norm']
TWIN_WEIGHTS = ['ffn1_norm', 'ffn1_w_gate', 'ffn1_w_up', 'ffn1_w_down', 'mix_norm', 'w_in', 'mlstm_gate_bias', 'attn_q_norm', 'attn_k_norm', 'attn_sink', 'mlstm_conv_w', 'mlstm_conv_b', 'mlstm_out_norm', 'w_branch_attn', 'w_branch_mlstm', 'w_out', 'ffn2_norm', 'ffn2_w_gate', 'ffn2_w_up', 'ffn2_w_down', 'block_out_norm']
TWIN_DIFF_INPUT = 'x'
TWIN_INPUTS = ['x', 'positions', 'ffn1_norm', 'ffn1_w_gate', 'ffn1_w_up', 'ffn1_w_down', 'mix_norm', 'w_in', 'mlstm_gate_bias', 'attn_q_norm', 'attn_k_norm', 'attn_sink', 'mlstm_conv_w', 'mlstm_conv_b', 'mlstm_out_norm', 'w_branch_attn', 'w_branch_mlstm', 'w_out', 'ffn2_norm', 'ffn2_w_gate', 'ffn2_w_up', 'ffn2_w_down', 'block_out_norm', 'loss_target', 'm_ffn1_norm', 'm_ffn1_w_gate', 'm_ffn1_w_up', 'm_ffn1_w_down', 'm_mix_norm', 'm_w_in', 'm_mlstm_gate_bias', 'm_attn_q_norm', 'm_attn_k_norm', 'm_attn_sink', 'm_mlstm_conv_w', 'm_mlstm_conv_b', 'm_mlstm_out_norm', 'm_w_branch_attn', 'm_w_branch_mlstm', 'm_w_out', 'm_ffn2_norm', 'm_ffn2_w_gate', 'm_ffn2_w_up', 'm_ffn2_w_down', 'm_block_out_norm', 'v_ffn1_norm', 'v_ffn1_w_gate', 'v_ffn1_w_up', 'v_ffn1_w_down', 'v_mix_norm', 'v_w_in', 'v_mlstm_gate_bias', 'v_attn_q_norm', 'v_attn_k_norm', 'v_attn_sink', 'v_mlstm_conv_w', 'v_mlstm_conv_b', 'v_mlstm_out_norm', 'v_w_branch_attn', 'v_w_branch_mlstm', 'v_w_out', 'v_ffn2_norm', 'v_ffn2_w_gate', 'v_ffn2_w_up', 'v_ffn2_w_down', 'v_block_out_norm']
TWIN_OUTPUTS = ['loss', 'grad_x', 'grad_ffn1_norm', 'grad_ffn1_w_gate', 'grad_ffn1_w_up', 'grad_ffn1_w_down', 'grad_mix_norm', 'grad_w_in', 'grad_mlstm_gate_bias', 'grad_attn_q_norm', 'grad_attn_k_norm', 'grad_attn_sink', 'grad_mlstm_conv_w', 'grad_mlstm_conv_b', 'grad_mlstm_out_norm', 'grad_w_branch_attn', 'grad_w_branch_mlstm', 'grad_w_out', 'grad_ffn2_norm', 'grad_ffn2_w_gate', 'grad_ffn2_w_up', 'grad_ffn2_w_down', 'grad_block_out_norm', 'delta_ffn1_norm', 'delta_ffn1_w_gate', 'delta_ffn1_w_up', 'delta_ffn1_w_down', 'delta_mix_norm', 'delta_w_in', 'delta_mlstm_gate_bias', 'delta_attn_q_norm', 'delta_attn_k_norm', 'delta_attn_sink', 'delta_mlstm_conv_w', 'delta_mlstm_conv_b', 'delta_mlstm_out_norm', 'delta_w_branch_attn', 'delta_w_branch_mlstm', 'delta_w_out', 'delta_ffn2_norm', 'delta_ffn2_w_gate', 'delta_ffn2_w_up', 'delta_ffn2_w_down', 'delta_block_out_norm', 'new_m_ffn1_norm', 'new_m_ffn1_w_gate', 'new_m_ffn1_w_up', 'new_m_ffn1_w_down', 'new_m_mix_norm', 'new_m_w_in', 'new_m_mlstm_gate_bias', 'new_m_attn_q_norm', 'new_m_attn_k_norm', 'new_m_attn_sink', 'new_m_mlstm_conv_w', 'new_m_mlstm_conv_b', 'new_m_mlstm_out_norm', 'new_m_w_branch_attn', 'new_m_w_branch_mlstm', 'new_m_w_out', 'new_m_ffn2_norm', 'new_m_ffn2_w_gate', 'new_m_ffn2_w_up', 'new_m_ffn2_w_down', 'new_m_block_out_norm', 'new_v_ffn1_norm', 'new_v_ffn1_w_gate', 'new_v_ffn1_w_up', 'new_v_ffn1_w_down', 'new_v_mix_norm', 'new_v_w_in', 'new_v_mlstm_gate_bias', 'new_v_attn_q_norm', 'new_v_attn_k_norm', 'new_v_attn_sink', 'new_v_mlstm_conv_w', 'new_v_mlstm_conv_b', 'new_v_mlstm_out_norm', 'new_v_w_branch_attn', 'new_v_w_branch_mlstm', 'new_v_w_out', 'new_v_ffn2_norm', 'new_v_ffn2_w_gate', 'new_v_ffn2_w_up', 'new_v_ffn2_w_down', 'new_v_block_out_norm']
TWIN_LEAF_KINDS = {'loss': 'loss', 'grad_x': 'grad_x', 'grad_ffn1_norm': 'grad_w', 'grad_ffn1_w_gate': 'grad_w', 'grad_ffn1_w_up': 'grad_w', 'grad_ffn1_w_down': 'grad_w', 'grad_mix_norm': 'grad_w', 'grad_w_in': 'grad_w', 'grad_mlstm_gate_bias': 'grad_w', 'grad_attn_q_norm': 'grad_w', 'grad_attn_k_norm': 'grad_w', 'grad_attn_sink': 'grad_w', 'grad_mlstm_conv_w': 'grad_w', 'grad_mlstm_conv_b': 'grad_w', 'grad_mlstm_out_norm': 'grad_w', 'grad_w_branch_attn': 'grad_w', 'grad_w_branch_mlstm': 'grad_w', 'grad_w_out': 'grad_w', 'grad_ffn2_norm': 'grad_w', 'grad_ffn2_w_gate': 'grad_w', 'grad_ffn2_w_up': 'grad_w', 'grad_ffn2_w_down': 'grad_w', 'grad_block_out_norm': 'grad_w', 'delta_ffn1_norm': 'delta_w', 'delta_ffn1_w_gate': 'delta_w', 'delta_ffn1_w_up': 'delta_w', 'delta_ffn1_w_down': 'delta_w', 'delta_mix_norm': 'delta_w', 'delta_w_in': 'delta_w', 'delta_mlstm_gate_bias': 'delta_w', 'delta_attn_q_norm': 'delta_w', 'delta_attn_k_norm': 'delta_w', 'delta_attn_sink': 'delta_w', 'delta_mlstm_conv_w': 'delta_w', 'delta_mlstm_conv_b': 'delta_w', 'delta_mlstm_out_norm': 'delta_w', 'delta_w_branch_attn': 'delta_w', 'delta_w_branch_mlstm': 'delta_w', 'delta_w_out': 'delta_w', 'delta_ffn2_norm': 'delta_w', 'delta_ffn2_w_gate': 'delta_w', 'delta_ffn2_w_up': 'delta_w', 'delta_ffn2_w_down': 'delta_w', 'delta_block_out_norm': 'delta_w', 'new_m_ffn1_norm': 'new_m', 'new_m_ffn1_w_gate': 'new_m', 'new_m_ffn1_w_up': 'new_m', 'new_m_ffn1_w_down': 'new_m', 'new_m_mix_norm': 'new_m', 'new_m_w_in': 'new_m', 'new_m_mlstm_gate_bias': 'new_m', 'new_m_attn_q_norm': 'new_m', 'new_m_attn_k_norm': 'new_m', 'new_m_attn_sink': 'new_m', 'new_m_mlstm_conv_w': 'new_m', 'new_m_mlstm_conv_b': 'new_m', 'new_m_mlstm_out_norm': 'new_m', 'new_m_w_branch_attn': 'new_m', 'new_m_w_branch_mlstm': 'new_m', 'new_m_w_out': 'new_m', 'new_m_ffn2_norm': 'new_m', 'new_m_ffn2_w_gate': 'new_m', 'new_m_ffn2_w_up': 'new_m', 'new_m_ffn2_w_down': 'new_m', 'new_m_block_out_norm': 'new_m', 'new_v_ffn1_norm': 'new_v', 'new_v_ffn1_w_gate': 'new_v', 'new_v_ffn1_w_up': 'new_v', 'new_v_ffn1_w_down': 'new_v', 'new_v_mix_norm': 'new_v', 'new_v_w_in': 'new_v', 'new_v_mlstm_gate_bias': 'new_v', 'new_v_attn_q_norm': 'new_v', 'new_v_attn_k_norm': 'new_v', 'new_v_attn_sink': 'new_v', 'new_v_mlstm_conv_w': 'new_v', 'new_v_mlstm_conv_b': 'new_v', 'new_v_mlstm_out_norm': 'new_v', 'new_v_w_branch_attn': 'new_v', 'new_v_w_branch_mlstm': 'new_v', 'new_v_w_out': 'new_v', 'new_v_ffn2_norm': 'new_v', 'new_v_ffn2_w_gate': 'new_v', 'new_v_ffn2_w_up': 'new_v', 'new_v_ffn2_w_down': 'new_v', 'new_v_block_out_norm': 'new_v'}


def _forward(args):
    return _fwd_reference(*[args[k] for k in FWD_PARAMS])


def _output_shape():
    out = _jax.eval_shape(lambda: _forward(_fwd_setup_inputs(0)))
    return out.shape, out.dtype

N_MICROBATCH = 1
ADAM_LR = 0.001
ADAM_B1 = 0.9
ADAM_B2 = 0.999
ADAM_EPS = 1e-08
ADAM_WD = 0.01
ADAM_STEP = 10
PER_EXAMPLE_BATCH_AXIS = {'x': 0, 'positions': 0, 'loss_target': 0}
SHARED_INPUTS = []
_WEIGHT_DTYPES = {'ffn1_norm': _jnp.float32, 'ffn1_w_gate': _jnp.float32, 'ffn1_w_up': _jnp.float32, 'ffn1_w_down': _jnp.float32, 'mix_norm': _jnp.float32, 'w_in': _jnp.float32, 'mlstm_gate_bias': _jnp.float32, 'attn_q_norm': _jnp.float32, 'attn_k_norm': _jnp.float32, 'attn_sink': _jnp.float32, 'mlstm_conv_w': _jnp.float32, 'mlstm_conv_b': _jnp.float32, 'mlstm_out_norm': _jnp.float32, 'w_branch_attn': _jnp.float32, 'w_branch_mlstm': _jnp.float32, 'w_out': _jnp.float32, 'ffn2_norm': _jnp.float32, 'ffn2_w_gate': _jnp.float32, 'ffn2_w_up': _jnp.float32, 'ffn2_w_down': _jnp.float32, 'block_out_norm': _jnp.float32}
MOMENT_SCALE = {'ffn1_norm': 8.915685e-02, 'ffn1_w_gate': 3.763412e-02, 'ffn1_w_up': 3.644856e-02, 'ffn1_w_down': 6.041268e-02, 'mix_norm': 1.008342e-01, 'w_in': 4.464584e-02, 'mlstm_gate_bias': 3.754009e-01, 'attn_q_norm': 4.488552e-02, 'attn_k_norm': 4.654703e-02, 'attn_sink': 8.569514e-04, 'mlstm_conv_w': 5.659292e-02, 'mlstm_conv_b': 5.802910e-02, 'mlstm_out_norm': 8.138975e-02, 'w_branch_attn': 1.503922e-02, 'w_branch_mlstm': 5.722469e-02, 'w_out': 5.924459e-02, 'ffn2_norm': 7.412170e-02, 'ffn2_w_gate': 3.120938e-02, 'ffn2_w_up': 3.025215e-02, 'ffn2_w_down': 5.024147e-02, 'block_out_norm': 2.256902e+01}


def _to_microbatches(a, axis):
    t = _jnp.moveaxis(a, axis, 0)
    t = t.reshape((N_MICROBATCH, t.shape[0] // N_MICROBATCH) + t.shape[1:])
    return _jnp.moveaxis(t, 1, axis + 1)


def setup_inputs(seed: int = 0) -> dict:
    inp = _fwd_setup_inputs(seed)
    key = _jax.random.fold_in(_jax.random.key(seed), 7919)
    shape, _ = _output_shape()
    out = dict(inp)
    out["loss_target"] = _jax.random.normal(_jax.random.fold_in(key, 0), shape, _jnp.float32)
    for i, name in enumerate(TWIN_WEIGHTS):
        w = inp[name].astype(_jnp.float32)
        if MOMENT_SCALE is None:
            s = _jnp.sqrt(_jnp.mean(_jnp.square(w)) + 1e-30)
        else:
            s = MOMENT_SCALE[name]
        km, kv = _jax.random.split(_jax.random.fold_in(key, i + 1))
        out[name] = w
        out["m_" + name] = s * _jax.random.normal(km, w.shape, _jnp.float32)
        out["v_" + name] = (s * s) * _jax.random.uniform(kv, w.shape, _jnp.float32, 0.5, 1.5)
    if N_MICROBATCH > 1:
        for name, axis in PER_EXAMPLE_BATCH_AXIS.items():
            out[name] = _to_microbatches(out[name], axis)
    return {'x': out['x'], 'positions': out['positions'], 'ffn1_norm': out['ffn1_norm'], 'ffn1_w_gate': out['ffn1_w_gate'], 'ffn1_w_up': out['ffn1_w_up'], 'ffn1_w_down': out['ffn1_w_down'], 'mix_norm': out['mix_norm'], 'w_in': out['w_in'], 'mlstm_gate_bias': out['mlstm_gate_bias'], 'attn_q_norm': out['attn_q_norm'], 'attn_k_norm': out['attn_k_norm'], 'attn_sink': out['attn_sink'], 'mlstm_conv_w': out['mlstm_conv_w'], 'mlstm_conv_b': out['mlstm_conv_b'], 'mlstm_out_norm': out['mlstm_out_norm'], 'w_branch_attn': out['w_branch_attn'], 'w_branch_mlstm': out['w_branch_mlstm'], 'w_out': out['w_out'], 'ffn2_norm': out['ffn2_norm'], 'ffn2_w_gate': out['ffn2_w_gate'], 'ffn2_w_up': out['ffn2_w_up'], 'ffn2_w_down': out['ffn2_w_down'], 'block_out_norm': out['block_out_norm'], 'loss_target': out['loss_target'], 'm_ffn1_norm': out['m_ffn1_norm'], 'm_ffn1_w_gate': out['m_ffn1_w_gate'], 'm_ffn1_w_up': out['m_ffn1_w_up'], 'm_ffn1_w_down': out['m_ffn1_w_down'], 'm_mix_norm': out['m_mix_norm'], 'm_w_in': out['m_w_in'], 'm_mlstm_gate_bias': out['m_mlstm_gate_bias'], 'm_attn_q_norm': out['m_attn_q_norm'], 'm_attn_k_norm': out['m_attn_k_norm'], 'm_attn_sink': out['m_attn_sink'], 'm_mlstm_conv_w': out['m_mlstm_conv_w'], 'm_mlstm_conv_b': out['m_mlstm_conv_b'], 'm_mlstm_out_norm': out['m_mlstm_out_norm'], 'm_w_branch_attn': out['m_w_branch_attn'], 'm_w_branch_mlstm': out['m_w_branch_mlstm'], 'm_w_out': out['m_w_out'], 'm_ffn2_norm': out['m_ffn2_norm'], 'm_ffn2_w_gate': out['m_ffn2_w_gate'], 'm_ffn2_w_up': out['m_ffn2_w_up'], 'm_ffn2_w_down': out['m_ffn2_w_down'], 'm_block_out_norm': out['m_block_out_norm'], 'v_ffn1_norm': out['v_ffn1_norm'], 'v_ffn1_w_gate': out['v_ffn1_w_gate'], 'v_ffn1_w_up': out['v_ffn1_w_up'], 'v_ffn1_w_down': out['v_ffn1_w_down'], 'v_mix_norm': out['v_mix_norm'], 'v_w_in': out['v_w_in'], 'v_mlstm_gate_bias': out['v_mlstm_gate_bias'], 'v_attn_q_norm': out['v_attn_q_norm'], 'v_attn_k_norm': out['v_attn_k_norm'], 'v_attn_sink': out['v_attn_sink'], 'v_mlstm_conv_w': out['v_mlstm_conv_w'], 'v_mlstm_conv_b': out['v_mlstm_conv_b'], 'v_mlstm_out_norm': out['v_mlstm_out_norm'], 'v_w_branch_attn': out['v_w_branch_attn'], 'v_w_branch_mlstm': out['v_w_branch_mlstm'], 'v_w_out': out['v_w_out'], 'v_ffn2_norm': out['v_ffn2_norm'], 'v_ffn2_w_gate': out['v_ffn2_w_gate'], 'v_ffn2_w_up': out['v_ffn2_w_up'], 'v_ffn2_w_down': out['v_ffn2_w_down'], 'v_block_out_norm': out['v_block_out_norm']}


def _loss(weights, diff, rest, loss_target):
    with _jax.named_scope("forward"):
        args = {**rest, TWIN_DIFF_INPUT: diff, **{k: w.astype(_WEIGHT_DTYPES[k]) for k, w in weights.items()}}
        y = _forward(args)
    with _jax.named_scope("loss_head"):
        err = _jnp.square(y.astype(_jnp.float32) - loss_target)
        return 0.5 * _jnp.sum(_jnp.mean(err, axis=-1)) if err.ndim else 0.5 * err


def _adamw(w, g, m, v):
    m = ADAM_B1 * m + (1.0 - ADAM_B1) * g
    v = ADAM_B2 * v + (1.0 - ADAM_B2) * _jnp.square(g)
    m_hat = m / (1.0 - ADAM_B1 ** ADAM_STEP)
    v_hat = v / (1.0 - ADAM_B2 ** ADAM_STEP)
    delta = -ADAM_LR * (m_hat / (_jnp.sqrt(v_hat) + ADAM_EPS) + ADAM_WD * w)
    return delta, m, v


def reference(x, positions, ffn1_norm, ffn1_w_gate, ffn1_w_up, ffn1_w_down, mix_norm, w_in, mlstm_gate_bias, attn_q_norm, attn_k_norm, attn_sink, mlstm_conv_w, mlstm_conv_b, mlstm_out_norm, w_branch_attn, w_branch_mlstm, w_out, ffn2_norm, ffn2_w_gate, ffn2_w_up, ffn2_w_down, block_out_norm, loss_target, m_ffn1_norm, m_ffn1_w_gate, m_ffn1_w_up, m_ffn1_w_down, m_mix_norm, m_w_in, m_mlstm_gate_bias, m_attn_q_norm, m_attn_k_norm, m_attn_sink, m_mlstm_conv_w, m_mlstm_conv_b, m_mlstm_out_norm, m_w_branch_attn, m_w_branch_mlstm, m_w_out, m_ffn2_norm, m_ffn2_w_gate, m_ffn2_w_up, m_ffn2_w_down, m_block_out_norm, v_ffn1_norm, v_ffn1_w_gate, v_ffn1_w_up, v_ffn1_w_down, v_mix_norm, v_w_in, v_mlstm_gate_bias, v_attn_q_norm, v_attn_k_norm, v_attn_sink, v_mlstm_conv_w, v_mlstm_conv_b, v_mlstm_out_norm, v_w_branch_attn, v_w_branch_mlstm, v_w_out, v_ffn2_norm, v_ffn2_w_gate, v_ffn2_w_up, v_ffn2_w_down, v_block_out_norm):
    given = dict(x=x, positions=positions, ffn1_norm=ffn1_norm, ffn1_w_gate=ffn1_w_gate, ffn1_w_up=ffn1_w_up, ffn1_w_down=ffn1_w_down, mix_norm=mix_norm, w_in=w_in, mlstm_gate_bias=mlstm_gate_bias, attn_q_norm=attn_q_norm, attn_k_norm=attn_k_norm, attn_sink=attn_sink, mlstm_conv_w=mlstm_conv_w, mlstm_conv_b=mlstm_conv_b, mlstm_out_norm=mlstm_out_norm, w_branch_attn=w_branch_attn, w_branch_mlstm=w_branch_mlstm, w_out=w_out, ffn2_norm=ffn2_norm, ffn2_w_gate=ffn2_w_gate, ffn2_w_up=ffn2_w_up, ffn2_w_down=ffn2_w_down, block_out_norm=block_out_norm, loss_target=loss_target, m_ffn1_norm=m_ffn1_norm, m_ffn1_w_gate=m_ffn1_w_gate, m_ffn1_w_up=m_ffn1_w_up, m_ffn1_w_down=m_ffn1_w_down, m_mix_norm=m_mix_norm, m_w_in=m_w_in, m_mlstm_gate_bias=m_mlstm_gate_bias, m_attn_q_norm=m_attn_q_norm, m_attn_k_norm=m_attn_k_norm, m_attn_sink=m_attn_sink, m_mlstm_conv_w=m_mlstm_conv_w, m_mlstm_conv_b=m_mlstm_conv_b, m_mlstm_out_norm=m_mlstm_out_norm, m_w_branch_attn=m_w_branch_attn, m_w_branch_mlstm=m_w_branch_mlstm, m_w_out=m_w_out, m_ffn2_norm=m_ffn2_norm, m_ffn2_w_gate=m_ffn2_w_gate, m_ffn2_w_up=m_ffn2_w_up, m_ffn2_w_down=m_ffn2_w_down, m_block_out_norm=m_block_out_norm, v_ffn1_norm=v_ffn1_norm, v_ffn1_w_gate=v_ffn1_w_gate, v_ffn1_w_up=v_ffn1_w_up, v_ffn1_w_down=v_ffn1_w_down, v_mix_norm=v_mix_norm, v_w_in=v_w_in, v_mlstm_gate_bias=v_mlstm_gate_bias, v_attn_q_norm=v_attn_q_norm, v_attn_k_norm=v_attn_k_norm, v_attn_sink=v_attn_sink, v_mlstm_conv_w=v_mlstm_conv_w, v_mlstm_conv_b=v_mlstm_conv_b, v_mlstm_out_norm=v_mlstm_out_norm, v_w_branch_attn=v_w_branch_attn, v_w_branch_mlstm=v_w_branch_mlstm, v_w_out=v_w_out, v_ffn2_norm=v_ffn2_norm, v_ffn2_w_gate=v_ffn2_w_gate, v_ffn2_w_up=v_ffn2_w_up, v_ffn2_w_down=v_ffn2_w_down, v_block_out_norm=v_block_out_norm)
    weights = {n: given[n] for n in TWIN_WEIGHTS}
    shared = {n: given[n] for n in SHARED_INPUTS}
    per_example = {n: given[n] for n in ['x', 'positions']}
    grad_fn = _jax.value_and_grad(_loss, argnums=(0, 1))

    def one_microbatch(ex, loss_target):
        ex = dict(ex)
        diff = ex.pop(TWIN_DIFF_INPUT)
        return grad_fn(weights, diff, {**shared, **ex}, loss_target)

    if N_MICROBATCH == 1:
        loss, (grad_w, grad_x) = one_microbatch(per_example, given["loss_target"])
    else:
        def body(carry, xs):
            loss_sum, grad_sum = carry
            l_k, (gw_k, gx_k) = one_microbatch(xs[0], xs[1])
            with _jax.named_scope("update"):
                return (loss_sum + l_k, _jax.tree.map(_jnp.add, grad_sum, gw_k)), gx_k

        init = (_jnp.zeros((), _jnp.float32), _jax.tree.map(_jnp.zeros_like, weights))
        (loss, grad_w), grad_x = _jax.lax.scan(body, init, (per_example, given["loss_target"]))
    with _jax.named_scope("update"):
        delta_w, new_m, new_v = {}, {}, {}
        for n in TWIN_WEIGHTS:
            delta_w[n], new_m[n], new_v[n] = _adamw(weights[n], grad_w[n], given["m_" + n], given["v_" + n])
    return (loss, grad_x, *[grad_w[n] for n in TWIN_WEIGHTS], *[delta_w[n] for n in TWIN_WEIGHTS],
            *[new_m[n] for n in TWIN_WEIGHTS], *[new_v[n] for n in TWIN_WEIGHTS])
```

```python
import functools

import numpy as np
import jax
import jax.numpy as jnp
from jax import lax
from jax.experimental import pallas as pl
from jax.experimental.pallas import tpu as pltpu

F32 = jnp.float32
BF16 = jnp.bfloat16

D_MODEL = 1024
D_FF = 2816
ATT_HEAD_DIM = 64
ATT_HEADS = 8
ATT_KV_HEADS = 2
ATT_GROUP = ATT_HEADS // ATT_KV_HEADS
ATT_WIDTH = ATT_HEADS * ATT_HEAD_DIM
ATT_KV_WIDTH = ATT_KV_HEADS * ATT_HEAD_DIM
WINDOW = 128
ATT_BLOCK = 128
ROPE_DIM = 16
ROPE_THETA = 500000.0
MLSTM_HEADS = 4
MLSTM_HEAD_DIM = 128
MLSTM_WIDTH = MLSTM_HEADS * MLSTM_HEAD_DIM
MLSTM_CHUNK = 128
MLSTM_N_GATES = 4 * MLSTM_HEADS
NORM_EPS = 1e-6
IN_WIDTH = 4880
DEPTH = 2
N_DEV = 8

ADAM_LR = 0.001
ADAM_B1 = 0.9
ADAM_B2 = 0.999
ADAM_EPS = 1e-08
ADAM_WD = 0.01
ADAM_STEP = 10

LANES = 128
C_GMERGE = 0
C_QK = 2048
C_VM = 3072
C_OM = 3584
C_QA = 4096
C_KA = 4608
C_VA = 4736
C_GATES = 4864
IN_PAD = 4992

VMEM_LIMIT = 48 * 1024 * 1024

MESH = pl.DeviceIdType.MESH


def _cparams(sem):
    return pltpu.CompilerParams(dimension_semantics=sem, vmem_limit_bytes=VMEM_LIMIT)


def _first_divisor(n, cands):
    for c in cands:
        if n % c == 0:
            return c
    return n


_NN = ((1,), (0,))
_NT = ((1,), (1,))
_TN = ((0,), (0,))


def _mm(a, b, dims):
    return lax.dot_general(a.astype(BF16), b.astype(BF16), (dims, ((), ())), preferred_element_type=F32)


@jax.custom_vjp
def mm_nn(a, b):
    return _mm(a, b, _NN)


def _mm_nn_fwd(a, b):
    return _mm(a, b, _NN), (a, b)


def _mm_nn_bwd(res, g):
    a, b = res
    return _mm(g, b, _NT).astype(a.dtype), _mm(a, g, _TN).astype(b.dtype)


mm_nn.defvjp(_mm_nn_fwd, _mm_nn_bwd)


@jax.custom_vjp
def mm_nt(a, b):
    return _mm(a, b, _NT)


def _mm_nt_fwd(a, b):
    return _mm(a, b, _NT), (a, b)


def _mm_nt_bwd(res, g):
    a, b = res
    return _mm(g, b, _NN).astype(a.dtype), _mm(g, a, _TN).astype(b.dtype)


mm_nt.defvjp(_mm_nt_fwd, _mm_nt_bwd)


@jax.custom_vjp
def mm_tn(a, b):
    return _mm(a, b, _TN)


def _mm_tn_fwd(a, b):
    return _mm(a, b, _TN), (a, b)


def _mm_tn_bwd(res, g):
    a, b = res
    return _mm(b, g, _NT).astype(a.dtype), _mm(a, g, _NN).astype(b.dtype)


mm_tn.defvjp(_mm_tn_fwd, _mm_tn_bwd)


def _matmul(name, a, b, mode, out_dtype=F32, res=None, scale=1.0):
    if mode == "nn":
        (M, K), (K2, N) = a.shape, b.shape
    elif mode == "nt":
        (M, K), (N, K2) = a.shape, b.shape
    else:
        (K, M), (K2, N) = a.shape, b.shape
    assert K == K2, (name, a.shape, b.shape)
    tm = _first_divisor(M, (1024, 1408, 512, 384, 256, 128))
    tn = _first_divisor(N, (1024, 512, 384, 256, 128))
    tk = _first_divisor(K, (1024, 1408, 1664, 512, 256, 128))
    nk = K // tk
    if mode == "tn":
        a_spec = pl.BlockSpec((tk, tm), lambda i, j, k: (k, i))
    else:
        a_spec = pl.BlockSpec((tm, tk), lambda i, j, k: (i, k))
    if mode == "nt":
        b_spec = pl.BlockSpec((tn, tk), lambda i, j, k: (j, k))
    else:
        b_spec = pl.BlockSpec((tk, tn), lambda i, j, k: (k, j))
    o_spec = pl.BlockSpec((tm, tn), lambda i, j, k: (i, j))
    dims = {"nn": _NN, "nt": _NT, "tn": _TN}[mode]
    has_res = res is not None

    def body(*refs):
        if has_res:
            a_ref, b_ref, r_ref, o_ref, acc = refs
        else:
            a_ref, b_ref, o_ref, acc = refs
        k = pl.program_id(2)

        @pl.when(k == 0)
        def _():
            acc[...] = jnp.zeros_like(acc)

        acc[...] += _mm(a_ref[...], b_ref[...], dims)

        @pl.when(k == nk - 1)
        def _():
            out = acc[...]
            if scale != 1.0:
                out = out * scale
            if has_res:
                out = r_ref[...].astype(F32) + out
            o_ref[...] = out.astype(out_dtype)

    in_specs = [a_spec, b_spec] + ([o_spec] if has_res else [])
    args = (a, b) + ((res,) if has_res else ())
    return pl.pallas_call(
        body, name=name, grid=(M // tm, N // tn, nk), in_specs=in_specs, out_specs=o_spec,
        out_shape=jax.ShapeDtypeStruct((M, N), out_dtype), scratch_shapes=[pltpu.VMEM((tm, tn), F32)],
        compiler_params=_cparams(("parallel", "parallel", "arbitrary")),
    )(*args)


class _In:
    def __init__(self, arr, width=None, base=0, split=False, rows=True):
        self.arr, self.base, self.split, self.rows = arr, base, split, rows
        self.width = arr.shape[1] if width is None else width


class _Out:
    def __init__(self, cols, dtype=F32, width=None, split=False, rows=True, nrows=1):
        self.cols, self.dtype, self.split, self.rows, self.nrows = cols, dtype, split, rows, nrows
        self.width = cols if width is None else width


def _rowwise(name, fn, ins, outs, n_rows, br, ncol=1):
    br = min(br, n_rows)
    assert n_rows % br == 0, (name, n_rows, br)
    nrow_blocks = n_rows // br

    def in_spec(d):
        nb = br if d.rows else d.arr.shape[0]
        if d.rows and d.split:
            im = lambda j, i, base=d.base: (i, base + j)
        elif d.rows:
            im = lambda j, i, base=d.base: (i, base)
        elif d.split:
            im = lambda j, i, base=d.base: (0, base + j)
        else:
            im = lambda j, i, base=d.base: (0, base)
        return pl.BlockSpec((nb, d.width), im)

    def out_spec(d):
        nb = br if d.rows else d.nrows
        if d.rows and d.split:
            im = lambda j, i: (i, j)
        elif d.rows:
            im = lambda j, i: (i, 0)
        elif d.split:
            im = lambda j, i: (0, j)
        else:
            im = lambda j, i: (0, 0)
        return pl.BlockSpec((nb, d.width), im)

    n_in = len(ins)

    def body(*refs):
        i = pl.program_id(1)
        vals = [r[...] for r in refs[:n_in]]
        res = fn(*vals)
        if not isinstance(res, (tuple, list)):
            res = (res,)
        for d, ref, val in zip(outs, refs[n_in:], res):
            if d.rows:
                ref[...] = val.astype(d.dtype)
            else:
                @pl.when(i == 0)
                def _(ref=ref):
                    ref[...] = jnp.zeros_like(ref)

                ref[...] += val.astype(d.dtype)

    out_shape = [jax.ShapeDtypeStruct((n_rows if d.rows else d.nrows, d.cols), d.dtype) for d in outs]
    res = pl.pallas_call(
        body, name=name, grid=(ncol, nrow_blocks), in_specs=[in_spec(d) for d in ins],
        out_specs=[out_spec(d) for d in outs], out_shape=out_shape,
        compiler_params=_cparams(("parallel", "arbitrary")),
    )(*[d.arr for d in ins])
    return res


def _rms(x, g):
    return x * lax.rsqrt(jnp.mean(x * x, axis=-1, keepdims=True) + NORM_EPS) * g


def _sigmoid(x):
    return 1.0 / (1.0 + jnp.exp(-x))


def _silu(x):
    return x * _sigmoid(x)


def _log_sigmoid(x):
    return jnp.minimum(x, 0.0) - jnp.log(1.0 + jnp.exp(-jnp.abs(x)))


def _rope_tables(pos, inv_freq_row):
    ang = pos.astype(F32) * inv_freq_row
    return jnp.cos(ang), jnp.sin(ang)


def _qk_prep(t, g, cos, sin, rot_mat):
    y = _rms(t, g)
    rot = lax.dot_general(y, rot_mat, (_NN, ((), ())), precision=lax.Precision.HIGHEST, preferred_element_type=F32)
    return y * cos + rot * sin


def _attn_head(q, kb, vb, sink, valid):
    s = mm_nt(q, kb) * (ATT_HEAD_DIM ** -0.5)
    s = jnp.where(valid, s, -jnp.inf)
    m = jnp.maximum(jnp.max(s, axis=-1, keepdims=True), sink)
    p = jnp.exp(s - m)
    den = jnp.sum(p, axis=-1, keepdims=True) + jnp.exp(sink - m)
    return mm_nn(p / den, vb)


def _mlstm_chunk(q, k, v, li, lf_pre, C, n, m, incl, incl_t, eye):
    k = k * (MLSTM_HEAD_DIM ** -0.5)
    lf = _log_sigmoid(lf_pre)
    lf_row = jnp.sum(eye * lf, axis=0, keepdims=True)
    li_row = jnp.sum(eye * li, axis=0, keepdims=True)
    b = jnp.sum(incl * lf_row, axis=1, keepdims=True)
    b_row = jnp.sum(incl_t * lf, axis=0, keepdims=True)
    b_tot = jnp.sum(lf, axis=0, keepdims=True)
    a = b_tot - b + li
    a_max = jnp.max(a, axis=0, keepdims=True)
    kw = k * jnp.exp(a - a_max)
    c_loc = mm_tn(kw, v)
    n_loc = jnp.sum(kw, axis=0, keepdims=True)

    dmat = jnp.where(incl > 0.5, b - b_row + li_row, -jnp.inf)
    inter = b + m
    m_t = jnp.maximum(inter, jnp.max(dmat, axis=1, keepdims=True))
    sc = mm_nt(q, k) * jnp.exp(dmat - m_t)
    scale_in = jnp.exp(inter - m_t)
    num = mm_nn(sc, v) + scale_in * mm_nn(q, C)
    den = jnp.sum(sc, axis=1, keepdims=True) + scale_in * jnp.sum(q * n, axis=1, keepdims=True)
    h = num / jnp.maximum(jnp.abs(den), jnp.exp(-m_t))

    m_new = jnp.maximum(b_tot + m, a_max)
    s_p = jnp.exp(b_tot + m - m_new)
    s_l = jnp.exp(a_max - m_new)
    return h, s_p * C + s_l * c_loc, s_p * n + s_l * n_loc, m_new


def _mlstm_combine(hf, hb, o_pre, g):
    h = hf + hb
    mu = jnp.mean(h, axis=-1, keepdims=True)
    var = jnp.mean(jnp.square(h - mu), axis=-1, keepdims=True)
    return _sigmoid(o_pre) * ((h - mu) * lax.rsqrt(var + NORM_EPS) * g)


def _merge(ga, gm, za, zm):
    return _sigmoid(ga) * za + _sigmoid(gm) * zm


def _attn_mask(n, seq):
    qi = n * ATT_BLOCK + lax.broadcasted_iota(jnp.int32, (ATT_BLOCK, 3 * ATT_BLOCK), 0)
    kj = (n - 1) * ATT_BLOCK + lax.broadcasted_iota(jnp.int32, (ATT_BLOCK, 3 * ATT_BLOCK), 1)
    return (jnp.abs(qi - kj) <= WINDOW) & (kj >= 0) & (kj < seq)


def _attn_specs(nq):
    q_spec = pl.BlockSpec((1, ATT_GROUP, ATT_BLOCK, ATT_HEAD_DIM), lambda h, b, n: (b, h, n, 0))

    def kv_spec(off):
        return pl.BlockSpec((1, 1, ATT_BLOCK, ATT_HEAD_DIM),
                            lambda h, b, n: (b, h, jnp.clip(n + off, 0, nq - 1), 0))

    sink_spec = pl.BlockSpec((1, ATT_GROUP, 1, 1), lambda h, b, n: (h, 0, 0, 0))
    return q_spec, kv_spec, sink_spec


def _attn_fwd(q, k, v, sink):
    B, _, S, _ = q.shape
    nq = S // ATT_BLOCK
    q_spec, kv_spec, sink_spec = _attn_specs(nq)

    def body(q_ref, kp, kc, kn, vp, vc, vn, s_ref, o_ref):
        valid = _attn_mask(pl.program_id(2), S)
        kb = jnp.concatenate([kp[0, 0], kc[0, 0], kn[0, 0]], axis=0)
        vb = jnp.concatenate([vp[0, 0], vc[0, 0], vn[0, 0]], axis=0)
        for g in range(ATT_GROUP):
            o_ref[0, g] = _attn_head(q_ref[0, g], kb, vb, s_ref[0, g], valid).astype(BF16)

    return pl.pallas_call(
        body, name="attn_fwd", grid=(ATT_KV_HEADS, B, nq),
        in_specs=[q_spec, kv_spec(-1), kv_spec(0), kv_spec(1), kv_spec(-1), kv_spec(0), kv_spec(1), sink_spec],
        out_specs=q_spec, out_shape=jax.ShapeDtypeStruct(q.shape, BF16),
        compiler_params=_cparams(("parallel", "parallel", "arbitrary")),
    )(q, k, k, k, v, v, v, sink)


def _attn_bwd(q, k, v, sink, dy):
    B, _, S, _ = q.shape
    nq = S // ATT_BLOCK
    q_spec, kv_spec, sink_spec = _attn_specs(nq)
    kv_full = pl.BlockSpec((1, 1, S, ATT_HEAD_DIM), lambda h, b, n: (b, h, 0, 0))

    def body(q_ref, kp, kc, kn, vp, vc, vn, s_ref, dy_ref, dq_ref, dk_ref, dv_ref, ds_ref):
        b, n = pl.program_id(1), pl.program_id(2)
        valid = _attn_mask(n, S)
        kb = jnp.concatenate([kp[0, 0], kc[0, 0], kn[0, 0]], axis=0)
        vb = jnp.concatenate([vp[0, 0], vc[0, 0], vn[0, 0]], axis=0)

        @pl.when(n == 0)
        def _():
            dk_ref[...] = jnp.zeros_like(dk_ref)
            dv_ref[...] = jnp.zeros_like(dv_ref)

        @pl.when((n == 0) & (b == 0))
        def _():
            ds_ref[...] = jnp.zeros_like(ds_ref)

        dkb = jnp.zeros_like(kb)
        dvb = jnp.zeros_like(vb)
        for g in range(ATT_GROUP):
            _, vjp = jax.vjp(functools.partial(_attn_head, valid=valid), q_ref[0, g], kb, vb, s_ref[0, g])
            dq, dk_g, dv_g, dsink = vjp(dy_ref[0, g])
            dq_ref[0, g] = dq
            ds_ref[0, g] += dsink
            dkb += dk_g
            dvb += dv_g
        for j, off in enumerate((-1, 0, 1)):
            start = pl.multiple_of(jnp.clip(n + off, 0, nq - 1) * ATT_BLOCK, ATT_BLOCK)
            rows = pl.ds(start, ATT_BLOCK)
            dk_ref[0, 0, rows, :] += dkb[j * ATT_BLOCK:(j + 1) * ATT_BLOCK]
            dv_ref[0, 0, rows, :] += dvb[j * ATT_BLOCK:(j + 1) * ATT_BLOCK]

    return pl.pallas_call(
        body, name="attn_bwd", grid=(ATT_KV_HEADS, B, nq),
        in_specs=[q_spec, kv_spec(-1), kv_spec(0), kv_spec(1), kv_spec(-1), kv_spec(0), kv_spec(1), sink_spec, q_spec],
        out_specs=[q_spec, kv_full, kv_full, sink_spec],
        out_shape=[jax.ShapeDtypeStruct(q.shape, F32), jax.ShapeDtypeStruct(k.shape, F32),
                   jax.ShapeDtypeStruct(v.shape, F32), jax.ShapeDtypeStruct(sink.shape, F32)],
        compiler_params=_cparams(("arbitrary", "arbitrary", "arbitrary")),
    )(q, k, k, k, v, v, v, sink, dy)


CONV_COLS = 256


def _conv_taps(u, seq):
    row = lax.broadcasted_iota(jnp.int32, u.shape, 0)
    prev = jnp.where(row == 0, 0.0, pltpu.roll(u, 1, axis=0))
    nxt = jnp.where(row == seq - 1, 0.0, pltpu.roll(u, seq - 1, axis=0))
    return prev, nxt


def _conv_fwd(proj3, w8):
    B, S, _ = proj3.shape
    ncb = 2 * MLSTM_WIDTH // CONV_COLS

    def body(u_ref, w_ref, o_ref):
        u = u_ref[0]
        prev, nxt = _conv_taps(u, S)
        o_ref[0] = _silu(prev * w_ref[0:1, :] + u * w_ref[1:2, :] + nxt * w_ref[2:3, :] + w_ref[3:4, :])

    return pl.pallas_call(
        body, name="conv_fwd", grid=(B, ncb),
        in_specs=[pl.BlockSpec((1, S, CONV_COLS), lambda b, c: (b, 0, C_QK // CONV_COLS + c)),
                  pl.BlockSpec((8, CONV_COLS), lambda b, c: (0, c))],
        out_specs=pl.BlockSpec((1, S, CONV_COLS), lambda b, c: (b, 0, c)),
        out_shape=jax.ShapeDtypeStruct((B, S, 2 * MLSTM_WIDTH), F32),
        compiler_params=_cparams(("parallel", "parallel")),
    )(proj3, w8)


def _conv_bwd(proj3, w8, dout_f, dout_b):
    B, S, _ = proj3.shape
    ncb = 2 * MLSTM_WIDTH // CONV_COLS

    def body(u_ref, w_ref, df_ref, db_ref, du_ref, dw_ref):
        b = pl.program_id(1)
        u = u_ref[0]
        prev, nxt = _conv_taps(u, S)
        w0, w1, w2 = w_ref[0:1, :], w_ref[1:2, :], w_ref[2:3, :]
        pre = prev * w0 + u * w1 + nxt * w2 + w_ref[3:4, :]
        sig = _sigmoid(pre)
        dpre = (df_ref[0] + db_ref[0]) * (sig * (1.0 + pre * (1.0 - sig)))
        dprev, dnxt = _conv_taps(dpre, S)
        du_ref[0] = dnxt * w0 + dpre * w1 + dprev * w2

        @pl.when(b == 0)
        def _():
            dw_ref[...] = jnp.zeros_like(dw_ref)

        dw_ref[0:1, :] += jnp.sum(dpre * prev, axis=0, keepdims=True)
        dw_ref[1:2, :] += jnp.sum(dpre * u, axis=0, keepdims=True)
        dw_ref[2:3, :] += jnp.sum(dpre * nxt, axis=0, keepdims=True)
        dw_ref[3:4, :] += jnp.sum(dpre, axis=0, keepdims=True)

    blk = pl.BlockSpec((1, S, CONV_COLS), lambda c, b: (b, 0, c))
    return pl.pallas_call(
        body, name="conv_bwd", grid=(ncb, B),
        in_specs=[pl.BlockSpec((1, S, CONV_COLS), lambda c, b: (b, 0, C_QK // CONV_COLS + c)),
                  pl.BlockSpec((8, CONV_COLS), lambda c, b: (0, c)), blk, blk],
        out_specs=[blk, pl.BlockSpec((8, CONV_COLS), lambda c, b: (0, c))],
        out_shape=[jax.ShapeDtypeStruct((B, S, 2 * MLSTM_WIDTH), F32), jax.ShapeDtypeStruct((8, 2 * MLSTM_WIDTH), F32)],
        compiler_params=_cparams(("parallel", "arbitrary")),
    )(proj3, w8, dout_f, dout_b)


def _chunk_masks(direction):
    t = lax.broadcasted_iota(jnp.int32, (MLSTM_CHUNK, MLSTM_CHUNK), 0)
    s = lax.broadcasted_iota(jnp.int32, (MLSTM_CHUNK, MLSTM_CHUNK), 1)
    le, ge = (s <= t).astype(F32), (s >= t).astype(F32)
    eye = (s == t).astype(F32)
    return (le, ge, eye) if direction == 0 else (ge, le, eye)


def _gate_cols(gates, direction, head):
    lane = lax.broadcasted_iota(jnp.int32, gates.shape, 1)
    sel_i = (lane == (2 * direction) * MLSTM_HEADS + head).astype(F32)
    sel_f = (lane == (2 * direction + 1) * MLSTM_HEADS + head).astype(F32)
    return sel_i, sel_f


def _mlstm_fwd(qk, proj3, bias):
    B, S, _ = qk.shape
    nc = S // MLSTM_CHUNK
    H, L, DH = MLSTM_HEADS, MLSTM_CHUNK, MLSTM_HEAD_DIM

    def chunk_of(d, c):
        return c if d == 0 else nc - 1 - c

    def body(qkf, qkb, vf, vb, gf, gb, bias_ref, hf, hb, csf, csb, nsf, nsb, msf, msb, c_st, n_st, m_st):
        c, h = pl.program_id(1), pl.program_id(2)

        @pl.when(c == 0)
        def _():
            for d in range(2):
                c_st[d, h] = jnp.zeros((DH, DH), F32)
                n_st[d, h] = jnp.zeros((1, DH), F32)
                m_st[d, h] = jnp.zeros((1, DH), F32)

        for d, (qk_ref, v_ref, g_ref, h_ref, cs, ns, ms) in enumerate(
                ((qkf, vf, gf, hf, csf, nsf, msf), (qkb, vb, gb, hb, csb, nsb, msb))):
            incl, incl_t, eye = _chunk_masks(d)
            gates = g_ref[0] + bias_ref[...]
            sel_i, sel_f = _gate_cols(gates, d, h)
            li = jnp.sum(gates * sel_i, axis=1, keepdims=True)
            lf_pre = jnp.sum(gates * sel_f, axis=1, keepdims=True)
            c_in, n_in, m_in = c_st[d, h], n_st[d, h], m_st[d, h]
            cs[0, 0, 0], ns[0, 0, 0], ms[0, 0, 0] = c_in, n_in, m_in
            hh, c_new, n_new, m_new = _mlstm_chunk(
                qk_ref[0, :, :DH], qk_ref[0, :, DH:], v_ref[0], li, lf_pre, c_in, n_in,
                jnp.max(m_in, axis=1, keepdims=True), incl, incl_t, eye)
            h_ref[0] = hh
            c_st[d, h], n_st[d, h] = c_new, n_new
            m_st[d, h] = jnp.broadcast_to(m_new, (1, DH))

    def tok_spec(width, base, d, per_head):
        return pl.BlockSpec((1, L, width), lambda b, c, h: (b, chunk_of(d, c), base + (h if per_head else 0)))

    def st_spec(shape, d):
        return pl.BlockSpec((1, 1, 1) + shape, lambda b, c, h: (b, chunk_of(d, c), h, 0, 0))

    in_specs = [tok_spec(2 * DH, 0, 0, True), tok_spec(2 * DH, 0, 1, True),
                tok_spec(DH, C_VM // DH, 0, True), tok_spec(DH, C_VM // DH, 1, True),
                tok_spec(LANES, C_GATES // LANES, 0, False), tok_spec(LANES, C_GATES // LANES, 1, False),
                pl.BlockSpec((1, LANES), lambda b, c, h: (0, 0))]
    out_specs = [tok_spec(DH, 0, 0, True), tok_spec(DH, 0, 1, True),
                 st_spec((DH, DH), 0), st_spec((DH, DH), 1), st_spec((1, DH), 0), st_spec((1, DH), 1),
                 st_spec((1, DH), 0), st_spec((1, DH), 1)]
    hs = jax.ShapeDtypeStruct((B, S, H * DH), F32)
    cs = jax.ShapeDtypeStruct((B, nc, H, DH, DH), F32)
    vs = jax.ShapeDtypeStruct((B, nc, H, 1, DH), F32)
    return pl.pallas_call(
        body, name="mlstm_fwd", grid=(B, nc, H), in_specs=in_specs, out_specs=out_specs,
        out_shape=[hs, hs, cs, cs, vs, vs, vs, vs],
        scratch_shapes=[pltpu.VMEM((2, H, DH, DH), F32), pltpu.VMEM((2, H, 1, DH), F32), pltpu.VMEM((2, H, 1, DH), F32)],
        compiler_params=_cparams(("parallel", "arbitrary", "arbitrary")),
    )(qk, qk, proj3, proj3, proj3, proj3, bias)


def _mlstm_bwd(qk, proj3, bias, states, dh):
    B, S, _ = qk.shape
    nc = S // MLSTM_CHUNK
    H, L, DH = MLSTM_HEADS, MLSTM_CHUNK, MLSTM_HEAD_DIM

    def chunk_of(d, c):
        return nc - 1 - c if d == 0 else c

    def body(qkf, qkb, vf, vb, gf, gb, bias_ref, csf, csb, nsf, nsb, msf, msb, dhf, dhb,
             dqkf, dqkb, dvf, dvb, dgf, dgb, dc_st, dn_st, dm_st):
        c, h = pl.program_id(1), pl.program_id(2)

        @pl.when(c == 0)
        def _():
            for d in range(2):
                dc_st[d, h] = jnp.zeros((DH, DH), F32)
                dn_st[d, h] = jnp.zeros((1, DH), F32)
                dm_st[d, h] = jnp.zeros((1, DH), F32)

        @pl.when(h == 0)
        def _():
            dgf[...] = jnp.zeros_like(dgf)
            dgb[...] = jnp.zeros_like(dgb)

        for d, (qk_ref, v_ref, g_ref, cs, ns, ms, dh_ref, dqk_ref, dv_ref, dg_ref) in enumerate(
                ((qkf, vf, gf, csf, nsf, msf, dhf, dqkf, dvf, dgf), (qkb, vb, gb, csb, nsb, msb, dhb, dqkb, dvb, dgb))):
            incl, incl_t, eye = _chunk_masks(d)
            gates = g_ref[0] + bias_ref[...]
            sel_i, sel_f = _gate_cols(gates, d, h)
            li = jnp.sum(gates * sel_i, axis=1, keepdims=True)
            lf_pre = jnp.sum(gates * sel_f, axis=1, keepdims=True)
            m_in = jnp.max(ms[0, 0, 0], axis=1, keepdims=True)
            _, vjp = jax.vjp(
                functools.partial(_mlstm_chunk, incl=incl, incl_t=incl_t, eye=eye),
                qk_ref[0, :, :DH], qk_ref[0, :, DH:], v_ref[0], li, lf_pre, cs[0, 0, 0], ns[0, 0, 0], m_in)
            dm_out = jnp.max(dm_st[d, h], axis=1, keepdims=True)
            dq, dk, dv, dli, dlf, dc, dn, dm = vjp((dh_ref[0], dc_st[d, h], dn_st[d, h], dm_out))
            dqk_ref[0, :, :DH] = dq
            dqk_ref[0, :, DH:] = dk
            dv_ref[0] = dv
            dg_ref[0] += dli * sel_i + dlf * sel_f
            dc_st[d, h], dn_st[d, h] = dc, dn
            dm_st[d, h] = jnp.broadcast_to(dm, (1, DH))

    def tok_spec(width, base, d, per_head):
        return pl.BlockSpec((1, L, width), lambda b, c, h: (b, chunk_of(d, c), base + (h if per_head else 0)))

    def st_spec(shape, d):
        return pl.BlockSpec((1, 1, 1) + shape, lambda b, c, h: (b, chunk_of(d, c), h, 0, 0))

    in_specs = [tok_spec(2 * DH, 0, 0, True), tok_spec(2 * DH, 0, 1, True),
                tok_spec(DH, C_VM // DH, 0, True), tok_spec(DH, C_VM // DH, 1, True),
                tok_spec(LANES, C_GATES // LANES, 0, False), tok_spec(LANES, C_GATES // LANES, 1, False),
                pl.BlockSpec((1, LANES), lambda b, c, h: (0, 0)),
                st_spec((DH, DH), 0), st_spec((DH, DH), 1), st_spec((1, DH), 0), st_spec((1, DH), 1),
                st_spec((1, DH), 0), st_spec((1, DH), 1), tok_spec(DH, 0, 0, True), tok_spec(DH, 0, 1, True)]
    out_specs = [tok_spec(2 * DH, 0, 0, True), tok_spec(2 * DH, 0, 1, True), tok_spec(DH, 0, 0, True), tok_spec(DH, 0, 1, True),
                 tok_spec(LANES, 0, 0, False), tok_spec(LANES, 0, 1, False)]
    qks = jax.ShapeDtypeStruct((B, S, 2 * H * DH), F32)
    vs = jax.ShapeDtypeStruct((B, S, H * DH), F32)
    gs = jax.ShapeDtypeStruct((B, S, LANES), F32)
    csf, csb, nsf, nsb, msf, msb = states
    return pl.pallas_call(
        body, name="mlstm_bwd", grid=(B, nc, H), in_specs=in_specs, out_specs=out_specs,
        out_shape=[qks, qks, vs, vs, gs, gs],
        scratch_shapes=[pltpu.VMEM((2, H, DH, DH), F32), pltpu.VMEM((2, H, 1, DH), F32), pltpu.VMEM((2, H, 1, DH), F32)],
        compiler_params=_cparams(("parallel", "arbitrary", "arbitrary")),
    )(qk, qk, proj3, proj3, proj3, proj3, bias, csf, csb, nsf, nsb, msf, msb, dh, dh)


ROW_BLOCK = 256
FF_COLS = 256


def _rms_fwd(name, x, g):
    T = x.shape[0]
    return _rowwise(name, lambda xv, gv: _rms(xv, gv), [_In(x), _In(g, rows=False)], [_Out(D_MODEL, BF16)], T, ROW_BLOCK)[0]


def _rms_bwd(name, x, g, dh, dres):
    T = x.shape[0]

    def fn(xv, gv, dhv, drv):
        _, vjp = jax.vjp(_rms, xv, gv)
        dx, dg = vjp(dhv)
        return drv + dx, dg

    return _rowwise(name, fn, [_In(x), _In(g, rows=False), _In(dh), _In(dres)],
                    [_Out(D_MODEL), _Out(D_MODEL, rows=False)], T, ROW_BLOCK)


def _ffn_fwd(tag, x, g, wg, wu, wd):
    T = x.shape[0]
    h = _rms_fwd(tag + "_norm", x, g)
    gate = _matmul(tag + "_gate", h, wg, "nn")
    up = _matmul(tag + "_up", h, wu, "nn")
    act = _rowwise(tag + "_act", lambda a, b: _silu(a) * b,
                   [_In(gate, FF_COLS, split=True), _In(up, FF_COLS, split=True)],
                   [_Out(D_FF, BF16, FF_COLS, split=True)], T, 1024, ncol=D_FF // FF_COLS)[0]
    out = _matmul(tag + "_down", act, wd, "nn", res=x, scale=0.5)
    return out, (x, h, gate, up, act)


def _ffn_bwd(tag, saved, g, wg, wu, wd, dx):
    x, h, gate, up, act = saved
    T = x.shape[0]
    dact = _matmul(tag + "_dact", dx, wd, "nt", scale=0.5)
    dwd = _matmul(tag + "_dwd", act, dx, "tn", scale=0.5)

    def fn(a, b, da):
        _, vjp = jax.vjp(lambda p, q: _silu(p) * q, a, b)
        return vjp(da)

    dgate, dup = _rowwise(tag + "_dactfn", fn,
                          [_In(gate, FF_COLS, split=True), _In(up, FF_COLS, split=True), _In(dact, FF_COLS, split=True)],
                          [_Out(D_FF, BF16, FF_COLS, split=True), _Out(D_FF, BF16, FF_COLS, split=True)],
                          T, 1024, ncol=D_FF // FF_COLS)
    dh = _matmul(tag + "_dh1", dgate, wg, "nt")
    dh = _matmul(tag + "_dh2", dup, wu, "nt", res=dh)
    dwg = _matmul(tag + "_dwg", h, dgate, "tn")
    dwu = _matmul(tag + "_dwu", h, dup, "tn")
    dx_new, dg = _rms_bwd(tag + "_dnorm", x, g, dh, dx)
    return dx_new, dg, dwg, dwu, dwd


def _rope_consts():
    half = ROPE_DIM // 2
    inv_freq = jnp.power(jnp.float32(ROPE_THETA), -jnp.arange(half, dtype=F32) * (2.0 / ROPE_DIM))
    row = jnp.zeros((1, ATT_HEAD_DIM), F32).at[0, :ROPE_DIM].set(jnp.concatenate([inv_freq, inv_freq]))
    rot = np.zeros((ATT_HEAD_DIM, ATT_HEAD_DIM), np.float32)
    for i in range(half):
        rot[half + i, i] = -1.0
        rot[i, half + i] = 1.0
    return row, jnp.asarray(rot)


def _prep_fwd(name, t, g, pos, inv_freq_row, rot):
    R = t.shape[0]

    def fn(tv, gv, pv, fv, rv):
        cos, sin = _rope_tables(pv, fv)
        return _qk_prep(tv, gv, cos, sin, rv)

    return _rowwise(name, fn, [_In(t), _In(g, rows=False), _In(pos), _In(inv_freq_row, rows=False), _In(rot, rows=False)],
                    [_Out(ATT_HEAD_DIM)], R, 1024)[0]


def _prep_bwd(name, t, g, pos, inv_freq_row, rot, dout):
    R = t.shape[0]

    def fn(tv, gv, pv, fv, rv, dv):
        cos, sin = _rope_tables(pv, fv)
        _, vjp = jax.vjp(lambda a, b: _qk_prep(a, b, cos, sin, rv), tv, gv)
        return vjp(dv)

    return _rowwise(name, fn, [_In(t), _In(g, rows=False), _In(pos), _In(inv_freq_row, rows=False), _In(rot, rows=False), _In(dout)],
                    [_Out(ATT_HEAD_DIM), _Out(ATT_HEAD_DIM, rows=False)], R, 1024)


def _to_heads(t, B, S, nh):
    return t.reshape(B, S, nh, ATT_HEAD_DIM).transpose(0, 2, 1, 3)


def _from_heads(t):
    B, nh, S, _ = t.shape
    return t.transpose(0, 2, 1, 3).reshape(B * S, nh * ATT_HEAD_DIM)


def _mix_fwd(x, pos_q, pos_k, B, S, p):
    T = B * S
    h = _rms_fwd("mix_norm", x, p["mix_norm"])
    proj = _matmul("mix_proj", h, p["w_in"], "nn")
    proj3 = proj.reshape(B, S, IN_PAD)
    inv_freq_row, rot = _rope_consts()
    qa = proj[:, C_QA:C_QA + ATT_WIDTH].reshape(T * ATT_HEADS, ATT_HEAD_DIM)
    ka = proj[:, C_KA:C_KA + ATT_KV_WIDTH].reshape(T * ATT_KV_HEADS, ATT_HEAD_DIM)
    q_r = _prep_fwd("q_prep", qa, p["attn_q_norm"], pos_q, inv_freq_row, rot)
    k_r = _prep_fwd("k_prep", ka, p["attn_k_norm"], pos_k, inv_freq_row, rot)
    qh = _to_heads(q_r, B, S, ATT_HEADS)
    kh = _to_heads(k_r, B, S, ATT_KV_HEADS)
    vh = _to_heads(proj[:, C_VA:C_VA + ATT_KV_WIDTH], B, S, ATT_KV_HEADS)
    sink = p["attn_sink"].reshape(ATT_KV_HEADS, ATT_GROUP, 1, 1)
    y_a = _from_heads(_attn_fwd(qh, kh, vh, sink))

    qk_c = _conv_fwd(proj3, p["conv_w8"])
    hf, hb, *states = _mlstm_fwd(qk_c, proj3, p["gate_bias"])
    hf2, hb2 = hf.reshape(T, MLSTM_WIDTH), hb.reshape(T, MLSTM_WIDTH)
    DH = MLSTM_HEAD_DIM
    y_m = _rowwise("mlstm_out", _mlstm_combine,
                   [_In(hf2, DH, split=True), _In(hb2, DH, split=True), _In(proj, DH, C_OM // DH, split=True),
                    _In(p["mlstm_out_norm"], DH, split=True, rows=False)],
                   [_Out(MLSTM_WIDTH, BF16, DH, split=True)], T, 1024, ncol=MLSTM_HEADS)[0]

    za = _matmul("branch_a", y_a, p["w_branch_attn"], "nn")
    zm = _matmul("branch_m", y_m, p["w_branch_mlstm"], "nn")
    W = 512
    merged = _rowwise("merge", _merge,
                      [_In(proj, W, C_GMERGE // W, split=True), _In(proj, W, (C_GMERGE + D_MODEL) // W, split=True),
                       _In(za, W, split=True), _In(zm, W, split=True)],
                      [_Out(D_MODEL, BF16, W, split=True)], T, 512, ncol=D_MODEL // W)[0]
    out = _matmul("mix_out", merged, p["w_out"], "nn", res=x)
    saved = dict(x=x, h=h, proj=proj, qa=qa, ka=ka, qh=qh, kh=kh, vh=vh, sink=sink, y_a=y_a, qk_c=qk_c, hf=hf2, hb=hb2,
                 states=states, y_m=y_m, za=za, zm=zm, merged=merged)
    return out, saved


def _mix_bwd(sv, pos_q, pos_k, B, S, p, dx):
    T = B * S
    DH = MLSTM_HEAD_DIM
    proj = sv["proj"]
    proj3 = proj.reshape(B, S, IN_PAD)
    inv_freq_row, rot = _rope_consts()
    g = {}
    dmerged = _matmul("mix_dmerged", dx, p["w_out"], "nt")
    g["w_out"] = _matmul("mix_dwout", sv["merged"], dx, "tn")
    W = 512

    def merge_bwd(ga, gm, za, zm, dm):
        _, vjp = jax.vjp(_merge, ga, gm, za, zm)
        return vjp(dm)

    dga, dgm, dza, dzm = _rowwise(
        "merge_bwd", merge_bwd,
        [_In(proj, W, C_GMERGE // W, split=True), _In(proj, W, (C_GMERGE + D_MODEL) // W, split=True),
         _In(sv["za"], W, split=True), _In(sv["zm"], W, split=True), _In(dmerged, W, split=True)],
        [_Out(D_MODEL, F32, W, split=True), _Out(D_MODEL, F32, W, split=True),
         _Out(D_MODEL, BF16, W, split=True), _Out(D_MODEL, BF16, W, split=True)], T, 512, ncol=D_MODEL // W)
    dya = _matmul("branch_a_dx", dza, p["w_branch_attn"], "nt")
    g["w_branch_attn"] = _matmul("branch_a_dw", sv["y_a"], dza, "tn")
    dym = _matmul("branch_m_dx", dzm, p["w_branch_mlstm"], "nt")
    g["w_branch_mlstm"] = _matmul("branch_m_dw", sv["y_m"], dzm, "tn")

    def combine_bwd(hf, hb, o_pre, gn, dy):
        _, vjp = jax.vjp(_mlstm_combine, hf, hb, o_pre, gn)
        dhf, _, do, dg = vjp(dy)
        return dhf, do, dg

    dh, dom, g["mlstm_out_norm"] = _rowwise(
        "mlstm_out_bwd", combine_bwd,
        [_In(sv["hf"], DH, split=True), _In(sv["hb"], DH, split=True), _In(proj, DH, C_OM // DH, split=True),
         _In(p["mlstm_out_norm"], DH, split=True, rows=False), _In(dym, DH, split=True)],
        [_Out(MLSTM_WIDTH, F32, DH, split=True), _Out(MLSTM_WIDTH, F32, DH, split=True),
         _Out(MLSTM_WIDTH, F32, DH, split=True, rows=False)], T, 1024, ncol=MLSTM_HEADS)
    dqk_f, dqk_b, dv_f, dv_b, dg_f, dg_b = _mlstm_bwd(sv["qk_c"], proj3, p["gate_bias"], sv["states"],
                                                       dh.reshape(B, S, MLSTM_WIDTH))
    dgates, dvm, g["gate_bias"] = _rowwise(
        "mlstm_dsum", lambda a, b, c, d: (a + b, c + d, jnp.sum(a + b, axis=0, keepdims=True)),
        [_In(dg_f.reshape(T, LANES)), _In(dg_b.reshape(T, LANES)), _In(dv_f.reshape(T, MLSTM_WIDTH)), _In(dv_b.reshape(T, MLSTM_WIDTH))],
        [_Out(LANES), _Out(MLSTM_WIDTH), _Out(LANES, rows=False)], T, 1024)
    dqk, g["conv_w8"] = _conv_bwd(proj3, p["conv_w8"], dqk_f, dqk_b)

    dyh = _to_heads(dya, B, S, ATT_HEADS)
    dqh, dkh, dvh, dsink = _attn_bwd(sv["qh"], sv["kh"], sv["vh"], sv["sink"], dyh)
    g["attn_sink"] = dsink.reshape(1, ATT_HEADS)
    dq_r = dqh.transpose(0, 2, 1, 3).reshape(T * ATT_HEADS, ATT_HEAD_DIM)
    dk_r = dkh.transpose(0, 2, 1, 3).reshape(T * ATT_KV_HEADS, ATT_HEAD_DIM)
    dva = _from_heads(dvh)
    dqa, g["attn_q_norm"] = _prep_bwd("q_prep_bwd", sv["qa"], p["attn_q_norm"], pos_q, inv_freq_row, rot, dq_r)
    dka, g["attn_k_norm"] = _prep_bwd("k_prep_bwd", sv["ka"], p["attn_k_norm"], pos_k, inv_freq_row, rot, dk_r)

    dproj = jnp.concatenate(
        [dga.astype(BF16), dgm.astype(BF16), dqk.reshape(T, 2 * MLSTM_WIDTH).astype(BF16), dvm.astype(BF16), dom.astype(BF16),
         dqa.reshape(T, ATT_WIDTH).astype(BF16), dka.reshape(T, ATT_KV_WIDTH).astype(BF16), dva.astype(BF16),
         dgates.astype(BF16)], axis=1)
    dh2 = _matmul("mix_dh", dproj, p["w_in"], "nt")
    g["w_in"] = _matmul("mix_dwin", sv["h"], dproj, "tn")
    dx_new, g["mix_norm"] = _rms_bwd("mix_dnorm", sv["x"], p["mix_norm"], dh2, dx)
    return dx_new, g


def _loss_and_grad(x, g, target):
    T = x.shape[0]

    def loss_fn(xv, gv, tv):
        err = jnp.square(_rms(xv, gv) - tv)
        return 0.5 * jnp.sum(jnp.mean(err, axis=-1, keepdims=True), axis=0, keepdims=True)

    def fn(xv, gv, tv):
        val, vjp = jax.vjp(lambda a, b: loss_fn(a, b, tv), xv, gv)
        dx, dg = vjp(jnp.ones((1, 1), F32))
        return val, dx, dg

    return _rowwise("loss_head", fn, [_In(x), _In(g, rows=False), _In(target)],
                    [_Out(1, rows=False), _Out(D_MODEL), _Out(D_MODEL, rows=False)], T, ROW_BLOCK)


def _block_norm_fwd(x, g):
    T = x.shape[0]
    return _rowwise("block_norm", _rms, [_In(x), _In(g, rows=False)], [_Out(D_MODEL)], T, ROW_BLOCK)[0]


def _block_norm_bwd(x, g, dy):
    T = x.shape[0]

    def fn(xv, gv, dv):
        _, vjp = jax.vjp(_rms, xv, gv)
        return vjp(dv)

    return _rowwise("block_norm_bwd", fn, [_In(x), _In(g, rows=False), _In(dy)],
                    [_Out(D_MODEL), _Out(D_MODEL, rows=False)], T, ROW_BLOCK)


def _qk_perm_cols(t, axis):
    q, k = jnp.split(t, 2, axis=axis)
    parts = []
    for h in range(MLSTM_HEADS):
        sl = [slice(None)] * t.ndim
        sl[axis] = slice(h * MLSTM_HEAD_DIM, (h + 1) * MLSTM_HEAD_DIM)
        parts += [q[tuple(sl)], k[tuple(sl)]]
    return jnp.concatenate(parts, axis=axis)


def _qk_unperm_cols(t, axis):
    qs, ks = [], []
    for h in range(MLSTM_HEADS):
        sl = [slice(None)] * t.ndim
        sl[axis] = slice(2 * h * MLSTM_HEAD_DIM, (2 * h + 1) * MLSTM_HEAD_DIM)
        qs.append(t[tuple(sl)])
        sl[axis] = slice((2 * h + 1) * MLSTM_HEAD_DIM, (2 * h + 2) * MLSTM_HEAD_DIM)
        ks.append(t[tuple(sl)])
    return jnp.concatenate(qs + ks, axis=axis)


def _w_in_arrange(w):
    qa, ka, va, qm, km, vm, om, gm, gmerge = jnp.split(w, np.cumsum(
        (ATT_WIDTH, ATT_KV_WIDTH, ATT_KV_WIDTH, MLSTM_WIDTH, MLSTM_WIDTH, MLSTM_WIDTH, MLSTM_WIDTH, MLSTM_N_GATES))[:].tolist(), axis=1)
    qk = _qk_perm_cols(jnp.concatenate([qm, km], axis=1), 1)
    pad = jnp.zeros((w.shape[0], LANES - MLSTM_N_GATES), w.dtype)
    return jnp.concatenate([gmerge, qk, vm, om, qa, ka, va, gm, pad], axis=1)


def _w_in_restore(w):
    gmerge = w[:, C_GMERGE:C_GMERGE + 2 * D_MODEL]
    qk = _qk_unperm_cols(w[:, C_QK:C_QK + 2 * MLSTM_WIDTH], 1)
    vm, om = w[:, C_VM:C_VM + MLSTM_WIDTH], w[:, C_OM:C_OM + MLSTM_WIDTH]
    qa, ka, va = w[:, C_QA:C_QA + ATT_WIDTH], w[:, C_KA:C_KA + ATT_KV_WIDTH], w[:, C_VA:C_VA + ATT_KV_WIDTH]
    gm = w[:, C_GATES:C_GATES + MLSTM_N_GATES]
    return jnp.concatenate([qa, ka, va, qk, vm, om, gm, gmerge], axis=1)


BIG = ("ffn1_w_gate", "ffn1_w_up", "ffn1_w_down", "w_in", "mlstm_conv_w", "w_branch_attn", "w_branch_mlstm", "w_out",
       "ffn2_w_gate", "ffn2_w_up", "ffn2_w_down")
SHARD_AXIS = {"ffn1_w_gate": 2, "ffn1_w_up": 2, "ffn1_w_down": 1, "w_in": 2, "mlstm_conv_w": 2, "w_branch_attn": 2,
              "w_branch_mlstm": 2, "w_out": 1, "ffn2_w_gate": 2, "ffn2_w_up": 2, "ffn2_w_down": 1}
SMALL = ("ffn1_norm", "mix_norm", "mlstm_gate_bias", "attn_q_norm", "attn_k_norm", "attn_sink", "mlstm_conv_b",
         "mlstm_out_norm", "ffn2_norm", "block_out_norm")
WEIGHTS = ("ffn1_norm", "ffn1_w_gate", "ffn1_w_up", "ffn1_w_down", "mix_norm", "w_in", "mlstm_gate_bias", "attn_q_norm",
           "attn_k_norm", "attn_sink", "mlstm_conv_w", "mlstm_conv_b", "mlstm_out_norm", "w_branch_attn", "w_branch_mlstm",
           "w_out", "ffn2_norm", "ffn2_w_gate", "ffn2_w_up", "ffn2_w_down", "block_out_norm")
PACK_COLS = 1024


def _padded_rows(n_elems):
    return -(-n_elems // PACK_COLS)


def _pack_flat(arrs, dtype, row_multiple):
    parts = []
    for a in arrs:
        flat = a.reshape(-1).astype(dtype)
        pad = _padded_rows(flat.shape[0]) * PACK_COLS - flat.shape[0]
        parts.append(jnp.pad(flat, (0, pad)) if pad else flat)
    flat = jnp.concatenate(parts)
    rows = flat.shape[0] // PACK_COLS
    extra = (-rows) % row_multiple
    if extra:
        flat = jnp.pad(flat, (0, extra * PACK_COLS))
    return flat.reshape(-1, PACK_COLS)


def _unpack_flat(buf, shapes, lead=()):
    flat = buf.reshape(lead + (-1,))
    out, off = [], 0
    for s in shapes:
        n = int(np.prod(s))
        out.append(flat[..., off:off + n].reshape(lead + tuple(s)))
        off += _padded_rows(n) * PACK_COLS
    return out


def _full_from_gathered(name, gathered):
    ax = SHARD_AXIS[name]
    parts = [gathered[j] for j in range(N_DEV)]
    return jnp.concatenate(parts, axis=ax)


def _shards_from_full(name, full):
    ax = SHARD_AXIS[name]
    return jnp.stack(jnp.split(full, N_DEV, axis=ax), axis=0)


ANY = pl.BlockSpec(memory_space=pl.ANY)


def _mesh_pos():
    return lax.axis_index("x"), lax.axis_index("y"), lax.axis_index("c")


def _all_gather(name, shard, vmem=False):
    R, C = shard.shape
    space = pl.BlockSpec(memory_space=pltpu.VMEM) if vmem else ANY

    def body(x_ref, out_ref, send_sems, recv_sems, local_sem):
        x, y, c = _mesh_pos()
        me, sibling = (x, y, c), (x, y, 1 - c)
        chips = [(1 - x, y), (x, 1 - y), (1 - x, 1 - y)]

        def slot(px, py, pc):
            return out_ref.at[4 * px + 2 * py + pc]

        def copy(k, block, to, src=None):
            return pltpu.make_async_remote_copy(
                src_ref=slot(*block) if src is None else src, dst_ref=slot(*block),
                send_sem=send_sems.at[k], recv_sem=recv_sems.at[k], device_id=to, device_id_type=MESH)

        mine = pltpu.make_async_copy(x_ref, slot(*me), local_sem)
        mine.start()
        first = [copy(0, me, sibling, src=x_ref)]
        first += [copy(1 + j, me, (*chip, c), src=x_ref) for j, chip in enumerate(chips)]
        for cp in first:
            cp.start()
        passed = [copy(4 + j, (*chip, c), sibling) for j, chip in enumerate(chips)]
        for j, chip in enumerate(chips):
            copy(1 + j, (*chip, c), me).wait_recv()
            passed[j].start()
        copy(0, sibling, me).wait_recv()
        for j, chip in enumerate(chips):
            copy(4 + j, (*chip, 1 - c), me).wait_recv()
        for cp in first + passed:
            cp.wait_send()
        mine.wait()

    return pl.pallas_call(
        body, name=name, out_shape=jax.ShapeDtypeStruct((N_DEV, R, C), shard.dtype),
        in_specs=[space], out_specs=space,
        scratch_shapes=[pltpu.SemaphoreType.DMA((7,)), pltpu.SemaphoreType.DMA((7,)), pltpu.SemaphoreType.DMA],
    )(shard)


def _pair_exchange(name, blocks):
    _, R, C = blocks.shape

    def body(g_ref, land_ref, send_sems, recv_sems):
        x, y, c = _mesh_pos()
        copies = []
        for chip in range(4):
            copies.append(pltpu.make_async_remote_copy(
                src_ref=g_ref.at[2 * chip + (1 - c)], dst_ref=land_ref.at[chip],
                send_sem=send_sems.at[chip], recv_sem=recv_sems.at[chip], device_id=(x, y, 1 - c), device_id_type=MESH))
        for cp in copies:
            cp.start()
        for cp in copies:
            cp.wait_recv()
        for cp in copies:
            cp.wait_send()

    return pl.pallas_call(
        body, name=name, out_shape=jax.ShapeDtypeStruct((4, R, C), blocks.dtype), in_specs=[ANY], out_specs=ANY,
        scratch_shapes=[pltpu.SemaphoreType.DMA((4,)), pltpu.SemaphoreType.DMA((4,))],
    )(blocks)


def _pair_sum(name, blocks, landed, out_dtype):
    _, R, C = blocks.shape
    br = _first_divisor(R, (512, 256, 128, 64, 32, 16, 8))

    def body(mine_ref, sib_ref, o_ref):
        o_ref[0] = (mine_ref[0] + sib_ref[0]).astype(out_dtype)

    c = lax.axis_index("c")
    return pl.pallas_call(
        lambda c_ref, a, b, o: body(a, b, o), name=name,
        grid_spec=pltpu.PrefetchScalarGridSpec(
            num_scalar_prefetch=1, grid=(4, R // br),
            in_specs=[pl.BlockSpec((1, br, C), lambda k, i, c_ref: (2 * k + c_ref[0], i, 0)),
                      pl.BlockSpec((1, br, C), lambda k, i, c_ref: (k, i, 0))],
            out_specs=pl.BlockSpec((1, br, C), lambda k, i, c_ref: (k, i, 0))),
        out_shape=jax.ShapeDtypeStruct((4, R, C), out_dtype),
        compiler_params=_cparams(("parallel", "parallel")),
    )(c.reshape(1).astype(jnp.int32), blocks, landed)


def _chip_exchange(name, sums):
    _, R, C = sums.shape

    def body(s_ref, land_ref, send_sems, recv_sems, local_sem):
        x, y, c = _mesh_pos()
        my_chip = 2 * x + y
        mine = pltpu.make_async_copy(s_ref.at[my_chip], land_ref.at[my_chip], local_sem)
        mine.start()
        chips = [(1 - x, y), (x, 1 - y), (1 - x, 1 - y)]
        copies = []
        for j, (px, py) in enumerate(chips):
            copies.append(pltpu.make_async_remote_copy(
                src_ref=s_ref.at[2 * px + py], dst_ref=land_ref.at[my_chip],
                send_sem=send_sems.at[j], recv_sem=recv_sems.at[j], device_id=(px, py, c), device_id_type=MESH))
        for cp in copies:
            cp.start()
        for j, (px, py) in enumerate(chips):
            pltpu.make_async_remote_copy(
                src_ref=s_ref.at[my_chip], dst_ref=land_ref.at[2 * px + py],
                send_sem=send_sems.at[j], recv_sem=recv_sems.at[j], device_id=(px, py, c), device_id_type=MESH).wait_recv()
        for cp in copies:
            cp.wait_send()
        mine.wait()

    return pl.pallas_call(
        body, name=name, out_shape=jax.ShapeDtypeStruct((4, R, C), sums.dtype), in_specs=[ANY], out_specs=ANY,
        scratch_shapes=[pltpu.SemaphoreType.DMA((3,)), pltpu.SemaphoreType.DMA((3,)), pltpu.SemaphoreType.DMA],
    )(sums)


def _sum_slots(name, slots, n):
    _, R, C = slots.shape
    br = _first_divisor(R, (256, 128, 64, 32, 16, 8))

    def body(s_ref, o_ref):
        acc = s_ref[0].astype(F32)
        for k in range(1, n):
            acc = acc + s_ref[k].astype(F32)
        o_ref[...] = acc

    return pl.pallas_call(
        body, name=name, grid=(R // br,), in_specs=[pl.BlockSpec((n, br, C), lambda i: (0, i, 0))],
        out_specs=pl.BlockSpec((br, C), lambda i: (i, 0)), out_shape=jax.ShapeDtypeStruct((R, C), F32),
        compiler_params=_cparams(("parallel",)),
    )(slots)


def _reduce_scatter(tag, blocks):
    landed = _pair_exchange(tag + "_pair", blocks)
    sums = _pair_sum(tag + "_pairsum", blocks, landed, F32)
    got = _chip_exchange(tag + "_chips", sums)
    return _sum_slots(tag + "_sum", got, 4)


def _adamw_math(w, g, m, v):
    m = ADAM_B1 * m + (1.0 - ADAM_B1) * g
    v = ADAM_B2 * v + (1.0 - ADAM_B2) * jnp.square(g)
    m_hat = m / (1.0 - ADAM_B1 ** ADAM_STEP)
    v_hat = v / (1.0 - ADAM_B2 ** ADAM_STEP)
    delta = -ADAM_LR * (m_hat / (jnp.sqrt(v_hat) + ADAM_EPS) + ADAM_WD * w)
    return delta, m, v


def _adamw(name, w, g, m, v):
    shape = w.shape
    cols = shape[-1]
    rows = int(np.prod(shape[:-1]))
    br = _first_divisor(rows, (512, 352, 256, 128, 64, 32, 16, 8))
    args = [_In(a.reshape(rows, cols)) for a in (w, g, m, v)]
    outs = _rowwise(name, _adamw_math, args, [_Out(cols), _Out(cols), _Out(cols)], rows, br)
    return [o.reshape(shape) for o in outs]


def _layer_params(full, small, l):
    p = {}
    for n in ("ffn1_w_gate", "ffn1_w_up", "ffn1_w_down", "w_branch_attn", "w_branch_mlstm", "w_out",
              "ffn2_w_gate", "ffn2_w_up", "ffn2_w_down"):
        p[n] = full[n][l]
    p["w_in"] = _w_in_arrange(full["w_in"][l])
    for n in ("ffn1_norm", "mix_norm", "ffn2_norm", "block_out_norm", "mlstm_out_norm", "attn_q_norm", "attn_k_norm"):
        p[n] = small[n][l][None, :]
    p["attn_sink"] = small["attn_sink"][l]
    p["gate_bias"] = jnp.pad(small["mlstm_gate_bias"][l], (0, LANES - MLSTM_N_GATES))[None, :]
    conv_w = _qk_perm_cols(full["mlstm_conv_w"][l].astype(F32), 1)
    conv_b = _qk_perm_cols(small["mlstm_conv_b"][l][None, :], 1)
    p["conv_w8"] = jnp.concatenate([conv_w, conv_b, jnp.zeros((4, 2 * MLSTM_WIDTH), F32)], axis=0)
    return p


def _local_step(x, positions, target, full, small):
    B, S, _ = x.shape
    T = B * S
    pos = positions.reshape(T, 1)
    pos_q = jnp.repeat(pos, ATT_HEADS, axis=0)
    pos_k = jnp.repeat(pos, ATT_KV_HEADS, axis=0)
    params = [_layer_params(full, small, l) for l in range(DEPTH)]
    xs = x.reshape(T, D_MODEL)
    tgt = target.reshape(T, D_MODEL)

    saved = []
    for l, p in enumerate(params):
        x1, s1 = _ffn_fwd("ffn1", xs, p["ffn1_norm"], p["ffn1_w_gate"], p["ffn1_w_up"], p["ffn1_w_down"])
        x2, s2 = _mix_fwd(x1, pos_q, pos_k, B, S, p)
        x3, s3 = _ffn_fwd("ffn2", x2, p["ffn2_norm"], p["ffn2_w_gate"], p["ffn2_w_up"], p["ffn2_w_down"])
        saved.append((s1, s2, s3, x3))
        if l + 1 < DEPTH:
            xs = _block_norm_fwd(x3, p["block_out_norm"])

    big = {n: [None] * DEPTH for n in BIG}
    sm = {n: [None] * DEPTH for n in SMALL}
    loss = None
    dx = None
    for l in reversed(range(DEPTH)):
        p = params[l]
        s1, s2, s3, x3 = saved[l]
        if l == DEPTH - 1:
            loss, dx, dgn = _loss_and_grad(x3, p["block_out_norm"], tgt)
        else:
            dx, dgn = _block_norm_bwd(x3, p["block_out_norm"], dx)
        sm["block_out_norm"][l] = dgn[0]
        dx, dg, dwg, dwu, dwd = _ffn_bwd("ffn2", s3, p["ffn2_norm"], p["ffn2_w_gate"], p["ffn2_w_up"], p["ffn2_w_down"], dx)
        sm["ffn2_norm"][l], big["ffn2_w_gate"][l], big["ffn2_w_up"][l], big["ffn2_w_down"][l] = dg[0], dwg, dwu, dwd
        dx, g = _mix_bwd(s2, pos_q, pos_k, B, S, p, dx)
        big["w_out"][l], big["w_branch_attn"][l], big["w_branch_mlstm"][l] = g["w_out"], g["w_branch_attn"], g["w_branch_mlstm"]
        big["w_in"][l] = _w_in_restore(g["w_in"])
        dconv = _qk_unperm_cols(g["conv_w8"], 1)
        big["mlstm_conv_w"][l] = dconv[0:3]
        sm["mlstm_conv_b"][l] = dconv[3]
        sm["mix_norm"][l] = g["mix_norm"][0]
        sm["mlstm_gate_bias"][l] = g["gate_bias"][0, :MLSTM_N_GATES]
        sm["attn_q_norm"][l], sm["attn_k_norm"][l] = g["attn_q_norm"][0], g["attn_k_norm"][0]
        sm["attn_sink"][l] = g["attn_sink"][0]
        sm["mlstm_out_norm"][l] = g["mlstm_out_norm"][0]
        dx, dg, dwg, dwu, dwd = _ffn_bwd("ffn1", s1, p["ffn1_norm"], p["ffn1_w_gate"], p["ffn1_w_up"], p["ffn1_w_down"], dx)
        sm["ffn1_norm"][l], big["ffn1_w_gate"][l], big["ffn1_w_up"][l], big["ffn1_w_down"][l] = dg[0], dwg, dwu, dwd
    big = {n: jnp.stack(v, axis=0) for n, v in big.items()}
    sm = {n: jnp.stack(v, axis=0) for n, v in sm.items()}
    return loss, dx.reshape(B, S, D_MODEL), big, sm


def kernel(x, positions, ffn1_norm, ffn1_w_gate, ffn1_w_up, ffn1_w_down, mix_norm, w_in, mlstm_gate_bias, attn_q_norm, attn_k_norm, attn_sink, mlstm_conv_w, mlstm_conv_b, mlstm_out_norm, w_branch_attn, w_branch_mlstm, w_out, ffn2_norm, ffn2_w_gate, ffn2_w_up, ffn2_w_down, block_out_norm, loss_target, m_ffn1_norm, m_ffn1_w_gate, m_ffn1_w_up, m_ffn1_w_down, m_mix_norm, m_w_in, m_mlstm_gate_bias, m_attn_q_norm, m_attn_k_norm, m_attn_sink, m_mlstm_conv_w, m_mlstm_conv_b, m_mlstm_out_norm, m_w_branch_attn, m_w_branch_mlstm, m_w_out, m_ffn2_norm, m_ffn2_w_gate, m_ffn2_w_up, m_ffn2_w_down, m_block_out_norm, v_ffn1_norm, v_ffn1_w_gate, v_ffn1_w_up, v_ffn1_w_down, v_mix_norm, v_w_in, v_mlstm_gate_bias, v_attn_q_norm, v_attn_k_norm, v_attn_sink, v_mlstm_conv_w, v_mlstm_conv_b, v_mlstm_out_norm, v_w_branch_attn, v_w_branch_mlstm, v_w_out, v_ffn2_norm, v_ffn2_w_gate, v_ffn2_w_up, v_ffn2_w_down, v_block_out_norm):
    args = locals()
    w = {n: args[n] for n in WEIGHTS}
    m = {n: args["m_" + n] for n in WEIGHTS}
    v = {n: args["v_" + n] for n in WEIGHTS}

    shard_shapes = [w[n].shape for n in BIG]
    matmul_w = [n for n in BIG if n != "mlstm_conv_w"]
    packed = _pack_flat([w[n] for n in matmul_w], BF16, 16)
    gathered = _all_gather("weights_all_gather", packed)
    parts = _unpack_flat(gathered, [w[n].shape for n in matmul_w], lead=(N_DEV,))
    full = {n: _full_from_gathered(n, part) for n, part in zip(matmul_w, parts)}
    conv_all = _all_gather("conv_all_gather", _pack_flat([w["mlstm_conv_w"]], F32, 8), vmem=True)
    full["mlstm_conv_w"] = _full_from_gathered(
        "mlstm_conv_w", _unpack_flat(conv_all, [w["mlstm_conv_w"].shape], lead=(N_DEV,))[0])
    small = {n: w[n] for n in SMALL}

    loss, grad_x, big_g, small_g = _local_step(x, positions, loss_target, full, small)

    blocks = jnp.stack([_pack_flat([_shards_from_full(n, big_g[n])[j] for n in BIG], F32, 8) for j in range(N_DEV)], axis=0)
    reduced = _reduce_scatter("grads", blocks)
    grads = dict(zip(BIG, _unpack_flat(reduced, shard_shapes)))

    small_shapes = [w[n].shape for n in SMALL] + [(1, 1)]
    small_packed = _pack_flat([small_g[n] for n in SMALL] + [loss], F32, 8)
    small_all = _all_gather("small_all_gather", small_packed, vmem=True)
    small_sum = _sum_slots("small_sum", small_all, N_DEV)
    *small_grads, loss_total = _unpack_flat(small_sum, small_shapes)
    grads.update(dict(zip(SMALL, small_grads)))

    deltas, new_m, new_v = {}, {}, {}
    for n in BIG:
        deltas[n], new_m[n], new_v[n] = _adamw("adamw_" + n, w[n], grads[n], m[n], v[n])
    sw, sg, smm, sv = (_pack_flat([d[n] for n in SMALL], F32, 8) for d in (w, grads, m, v))
    sd, snm, snv = _adamw("adamw_small", sw, sg, smm, sv)
    shapes = [w[n].shape for n in SMALL]
    for d, buf in ((deltas, sd), (new_m, snm), (new_v, snv)):
        d.update(dict(zip(SMALL, _unpack_flat(buf, shapes))))

    return (loss_total.reshape(()), grad_x, *[grads[n] for n in WEIGHTS], *[deltas[n] for n in WEIGHTS],
            *[new_m[n] for n in WEIGHTS], *[new_v[n] for n in WEIGHTS])
```

```python
import functools

import numpy as np
import jax
import jax.numpy as jnp
from jax import lax
from jax.experimental import pallas as pl
from jax.experimental.pallas import tpu as pltpu

F32 = jnp.float32
BF16 = jnp.bfloat16

D_MODEL = 1024
D_FF = 2816
ATT_HEAD_DIM = 64
ATT_HEADS = 8
ATT_KV_HEADS = 2
ATT_GROUP = ATT_HEADS // ATT_KV_HEADS
ATT_WIDTH = ATT_HEADS * ATT_HEAD_DIM
ATT_KV_WIDTH = ATT_KV_HEADS * ATT_HEAD_DIM
WINDOW = 128
ATT_BLOCK = 128
ROPE_DIM = 16
ROPE_THETA = 500000.0
MLSTM_HEADS = 4
MLSTM_HEAD_DIM = 128
MLSTM_WIDTH = MLSTM_HEADS * MLSTM_HEAD_DIM
MLSTM_CHUNK = 128
MLSTM_N_GATES = 4 * MLSTM_HEADS
NORM_EPS = 1e-6
IN_WIDTH = 4880
DEPTH = 2
N_DEV = 8

ADAM_LR = 0.001
ADAM_B1 = 0.9
ADAM_B2 = 0.999
ADAM_EPS = 1e-08
ADAM_WD = 0.01
ADAM_STEP = 10

LANES = 128
C_GMERGE = 0
C_QK = 2048
C_VM = 3072
C_OM = 3584
C_QA = 4096
C_KA = 4608
C_VA = 4736
C_GATES = 4864
IN_PAD = 4992

VMEM_LIMIT = 48 * 1024 * 1024

MESH = pl.DeviceIdType.MESH


def _cparams(sem):
    return pltpu.CompilerParams(dimension_semantics=sem, vmem_limit_bytes=VMEM_LIMIT)


def _first_divisor(n, cands):
    for c in cands:
        if n % c == 0:
            return c
    return n


_NN = ((1,), (0,))
_NT = ((1,), (1,))
_TN = ((0,), (0,))


def _mm(a, b, dims):
    return lax.dot_general(a.astype(BF16), b.astype(BF16), (dims, ((), ())), preferred_element_type=F32)


@jax.custom_vjp
def mm_nn(a, b):
    return _mm(a, b, _NN)


def _mm_nn_fwd(a, b):
    return _mm(a, b, _NN), (a, b)


def _mm_nn_bwd(res, g):
    a, b = res
    return _mm(g, b, _NT).astype(a.dtype), _mm(a, g, _TN).astype(b.dtype)


mm_nn.defvjp(_mm_nn_fwd, _mm_nn_bwd)


@jax.custom_vjp
def mm_nt(a, b):
    return _mm(a, b, _NT)


def _mm_nt_fwd(a, b):
    return _mm(a, b, _NT), (a, b)


def _mm_nt_bwd(res, g):
    a, b = res
    return _mm(g, b, _NN).astype(a.dtype), _mm(g, a, _TN).astype(b.dtype)


mm_nt.defvjp(_mm_nt_fwd, _mm_nt_bwd)


@jax.custom_vjp
def mm_tn(a, b):
    return _mm(a, b, _TN)


def _mm_tn_fwd(a, b):
    return _mm(a, b, _TN), (a, b)


def _mm_tn_bwd(res, g):
    a, b = res
    return _mm(b, g, _NT).astype(a.dtype), _mm(a, g, _NN).astype(b.dtype)


mm_tn.defvjp(_mm_tn_fwd, _mm_tn_bwd)


def _matmul(name, a, b, mode, out_dtype=F32, res=None, scale=1.0, bl=None):
    b_shape = b.shape if bl is None else b.shape[1:]
    if mode == "nn":
        (M, K), (K2, N) = a.shape, b_shape
    elif mode == "nt":
        (M, K), (N, K2) = a.shape, b_shape
    else:
        (K, M), (K2, N) = a.shape, b_shape
    assert K == K2, (name, a.shape, b.shape)
    tm = _first_divisor(M, (1024, 1408, 512, 384, 256, 128))
    tn = _first_divisor(N, (1024, 512, 384, 256, 128))
    tk = _first_divisor(K, (1024, 1408, 1664, 512, 256, 128))
    nk = K // tk
    if mode == "tn":
        a_spec = pl.BlockSpec((tk, tm), lambda i, j, k: (k, i))
    else:
        a_spec = pl.BlockSpec((tm, tk), lambda i, j, k: (i, k))
    if mode == "nt":
        b_blk, b_idx = (tn, tk), (lambda i, j, k: (j, k))
    else:
        b_blk, b_idx = (tk, tn), (lambda i, j, k: (k, j))
    if bl is None:
        b_spec = pl.BlockSpec(b_blk, b_idx)
    else:
        b_spec = pl.BlockSpec((None,) + b_blk, lambda i, j, k: (bl,) + b_idx(i, j, k))
    o_spec = pl.BlockSpec((tm, tn), lambda i, j, k: (i, j))
    dims = {"nn": _NN, "nt": _NT, "tn": _TN}[mode]
    has_res = res is not None

    def body(*refs):
        if has_res:
            a_ref, b_ref, r_ref, o_ref, acc = refs
        else:
            a_ref, b_ref, o_ref, acc = refs
        k = pl.program_id(2)

        @pl.when(k == 0)
        def _():
            acc[...] = jnp.zeros_like(acc)

        acc[...] += _mm(a_ref[...], b_ref[...], dims)

        @pl.when(k == nk - 1)
        def _():
            out = acc[...]
            if scale != 1.0:
                out = out * scale
            if has_res:
                out = r_ref[...].astype(F32) + out
            o_ref[...] = out.astype(out_dtype)

    in_specs = [a_spec, b_spec] + ([o_spec] if has_res else [])
    args = (a, b) + ((res,) if has_res else ())
    return pl.pallas_call(
        body, name=name, grid=(M // tm, N // tn, nk), in_specs=in_specs, out_specs=o_spec,
        out_shape=jax.ShapeDtypeStruct((M, N), out_dtype), scratch_shapes=[pltpu.VMEM((tm, tn), F32)],
        compiler_params=_cparams(("parallel", "parallel", "arbitrary")),
    )(*args)


class _In:
    def __init__(self, arr, width=None, base=0, split=False, rows=True):
        self.arr, self.base, self.split, self.rows = arr, base, split, rows
        self.width = arr.shape[1] if width is None else width


class _Out:
    def __init__(self, cols, dtype=F32, width=None, split=False, rows=True, nrows=1):
        self.cols, self.dtype, self.split, self.rows, self.nrows = cols, dtype, split, rows, nrows
        self.width = cols if width is None else width


def _rowwise(name, fn, ins, outs, n_rows, br, ncol=1):
    br = min(br, n_rows)
    assert n_rows % br == 0, (name, n_rows, br)
    nrow_blocks = n_rows // br

    def in_spec(d):
        nb = br if d.rows else d.arr.shape[0]
        if d.rows and d.split:
            im = lambda j, i, base=d.base: (i, base + j)
        elif d.rows:
            im = lambda j, i, base=d.base: (i, base)
        elif d.split:
            im = lambda j, i, base=d.base: (0, base + j)
        else:
            im = lambda j, i, base=d.base: (0, base)
        return pl.BlockSpec((nb, d.width), im)

    def out_spec(d):
        nb = br if d.rows else d.nrows
        if d.rows and d.split:
            im = lambda j, i: (i, j)
        elif d.rows:
            im = lambda j, i: (i, 0)
        elif d.split:
            im = lambda j, i: (0, j)
        else:
            im = lambda j, i: (0, 0)
        return pl.BlockSpec((nb, d.width), im)

    n_in = len(ins)

    def body(*refs):
        i = pl.program_id(1)
        vals = [r[...] for r in refs[:n_in]]
        res = fn(*vals)
        if not isinstance(res, (tuple, list)):
            res = (res,)
        for d, ref, val in zip(outs, refs[n_in:], res):
            if d.rows:
                ref[...] = val.astype(d.dtype)
            else:
                @pl.when(i == 0)
                def _(ref=ref):
                    ref[...] = jnp.zeros_like(ref)

                ref[...] += val.astype(d.dtype)

    out_shape = [jax.ShapeDtypeStruct((n_rows if d.rows else d.nrows, d.cols), d.dtype) for d in outs]
    res = pl.pallas_call(
        body, name=name, grid=(ncol, nrow_blocks), in_specs=[in_spec(d) for d in ins],
        out_specs=[out_spec(d) for d in outs], out_shape=out_shape,
        compiler_params=_cparams(("parallel", "arbitrary")),
    )(*[d.arr for d in ins])
    return res


def _rms(x, g):
    return x * lax.rsqrt(jnp.mean(x * x, axis=-1, keepdims=True) + NORM_EPS) * g


def _sigmoid(x):
    return 1.0 / (1.0 + jnp.exp(-x))


def _silu(x):
    return x * _sigmoid(x)


def _log_sigmoid(x):
    return jnp.minimum(x, 0.0) - jnp.log(1.0 + jnp.exp(-jnp.abs(x)))


def _rope_tables(pos, inv_freq_row):
    ang = pos.astype(F32) * inv_freq_row
    return jnp.cos(ang), jnp.sin(ang)


def _qk_prep(t, g, cos, sin, rot_mat):
    y = _rms(t, g)
    rot = lax.dot_general(y, rot_mat, (_NN, ((), ())), precision=lax.Precision.HIGHEST, preferred_element_type=F32)
    return y * cos + rot * sin


def _attn_head(q, kb, vb, sink, valid):
    s = mm_nt(q, kb) * (ATT_HEAD_DIM ** -0.5)
    s = jnp.where(valid, s, -jnp.inf)
    m = jnp.maximum(jnp.max(s, axis=-1, keepdims=True), sink)
    p = jnp.exp(s - m)
    den = jnp.sum(p, axis=-1, keepdims=True) + jnp.exp(sink - m)
    return mm_nn(p / den, vb)


def _mlstm_chunk(q, k, v, li, lf_pre, C, n, m, incl, incl_t, eye):
    k = k * (MLSTM_HEAD_DIM ** -0.5)
    lf = _log_sigmoid(lf_pre)
    lf_row = jnp.sum(eye * lf, axis=0, keepdims=True)
    li_row = jnp.sum(eye * li, axis=0, keepdims=True)
    b = jnp.sum(incl * lf_row, axis=1, keepdims=True)
    b_row = jnp.sum(incl_t * lf, axis=0, keepdims=True)
    b_tot = jnp.sum(lf, axis=0, keepdims=True)
    a = b_tot - b + li
    a_max = jnp.max(a, axis=0, keepdims=True)
    kw = k * jnp.exp(a - a_max)
    c_loc = mm_tn(kw, v)
    n_loc = jnp.sum(kw, axis=0, keepdims=True)

    dmat = jnp.where(incl > 0.5, b - b_row + li_row, -jnp.inf)
    inter = b + m
    m_t = jnp.maximum(inter, jnp.max(dmat, axis=1, keepdims=True))
    sc = mm_nt(q, k) * jnp.exp(dmat - m_t)
    scale_in = jnp.exp(inter - m_t)
    num = mm_nn(sc, v) + scale_in * mm_nn(q, C)
    den = jnp.sum(sc, axis=1, keepdims=True) + scale_in * jnp.sum(q * n, axis=1, keepdims=True)
    h = num / jnp.maximum(jnp.abs(den), jnp.exp(-m_t))

    m_new = jnp.maximum(b_tot + m, a_max)
    s_p = jnp.exp(b_tot + m - m_new)
    s_l = jnp.exp(a_max - m_new)
    return h, s_p * C + s_l * c_loc, s_p * n + s_l * n_loc, m_new


def _mlstm_combine(hf, hb, o_pre, g):
    h = hf + hb
    mu = jnp.mean(h, axis=-1, keepdims=True)
    var = jnp.mean(jnp.square(h - mu), axis=-1, keepdims=True)
    return _sigmoid(o_pre) * ((h - mu) * lax.rsqrt(var + NORM_EPS) * g)


def _merge(ga, gm, za, zm):
    return _sigmoid(ga) * za + _sigmoid(gm) * zm


def _attn_mask(n, seq):
    qi = n * ATT_BLOCK + lax.broadcasted_iota(jnp.int32, (ATT_BLOCK, 3 * ATT_BLOCK), 0)
    kj = (n - 1) * ATT_BLOCK + lax.broadcasted_iota(jnp.int32, (ATT_BLOCK, 3 * ATT_BLOCK), 1)
    return (jnp.abs(qi - kj) <= WINDOW) & (kj >= 0) & (kj < seq)


def _attn_specs(nq):
    q_spec = pl.BlockSpec((1, ATT_GROUP, ATT_BLOCK, ATT_HEAD_DIM), lambda h, b, n: (b, h, n, 0))

    def kv_spec(off):
        return pl.BlockSpec((1, 1, ATT_BLOCK, ATT_HEAD_DIM),
                            lambda h, b, n: (b, h, jnp.clip(n + off, 0, nq - 1), 0))

    sink_spec = pl.BlockSpec((1, ATT_GROUP, 1, 1), lambda h, b, n: (h, 0, 0, 0))
    return q_spec, kv_spec, sink_spec


def _attn_fwd(q, k, v, sink):
    B, _, S, _ = q.shape
    nq = S // ATT_BLOCK
    q_spec, kv_spec, sink_spec = _attn_specs(nq)

    def body(q_ref, kp, kc, kn, vp, vc, vn, s_ref, o_ref):
        valid = _attn_mask(pl.program_id(2), S)
        kb = jnp.concatenate([kp[0, 0], kc[0, 0], kn[0, 0]], axis=0)
        vb = jnp.concatenate([vp[0, 0], vc[0, 0], vn[0, 0]], axis=0)
        for g in range(ATT_GROUP):
            o_ref[0, g] = _attn_head(q_ref[0, g], kb, vb, s_ref[0, g], valid).astype(BF16)

    return pl.pallas_call(
        body, name="attn_fwd", grid=(ATT_KV_HEADS, B, nq),
        in_specs=[q_spec, kv_spec(-1), kv_spec(0), kv_spec(1), kv_spec(-1), kv_spec(0), kv_spec(1), sink_spec],
        out_specs=q_spec, out_shape=jax.ShapeDtypeStruct(q.shape, BF16),
        compiler_params=_cparams(("parallel", "parallel", "arbitrary")),
    )(q, k, k, k, v, v, v, sink)


def _attn_bwd(q, k, v, sink, dy):
    B, _, S, _ = q.shape
    nq = S // ATT_BLOCK
    q_spec, kv_spec, sink_spec = _attn_specs(nq)
    kv_full = pl.BlockSpec((1, 1, S, ATT_HEAD_DIM), lambda h, b, n: (b, h, 0, 0))

    def body(q_ref, kp, kc, kn, vp, vc, vn, s_ref, dy_ref, dq_ref, dk_ref, dv_ref, ds_ref):
        b, n = pl.program_id(1), pl.program_id(2)
        valid = _attn_mask(n, S)
        kb = jnp.concatenate([kp[0, 0], kc[0, 0], kn[0, 0]], axis=0)
        vb = jnp.concatenate([vp[0, 0], vc[0, 0], vn[0, 0]], axis=0)

        @pl.when(n == 0)
        def _():
            dk_ref[...] = jnp.zeros_like(dk_ref)
            dv_ref[...] = jnp.zeros_like(dv_ref)

        @pl.when((n == 0) & (b == 0))
        def _():
            ds_ref[...] = jnp.zeros_like(ds_ref)

        dkb = jnp.zeros_like(kb)
        dvb = jnp.zeros_like(vb)
        for g in range(ATT_GROUP):
            _, vjp = jax.vjp(functools.partial(_attn_head, valid=valid), q_ref[0, g], kb, vb, s_ref[0, g])
            dq, dk_g, dv_g, dsink = vjp(dy_ref[0, g])
            dq_ref[0, g] = dq
            ds_ref[0, g] += dsink
            dkb += dk_g
            dvb += dv_g
        for j, off in enumerate((-1, 0, 1)):
            start = pl.multiple_of(jnp.clip(n + off, 0, nq - 1) * ATT_BLOCK, ATT_BLOCK)
            rows = pl.ds(start, ATT_BLOCK)
            dk_ref[0, 0, rows, :] += dkb[j * ATT_BLOCK:(j + 1) * ATT_BLOCK]
            dv_ref[0, 0, rows, :] += dvb[j * ATT_BLOCK:(j + 1) * ATT_BLOCK]

    return pl.pallas_call(
        body, name="attn_bwd", grid=(ATT_KV_HEADS, B, nq),
        in_specs=[q_spec, kv_spec(-1), kv_spec(0), kv_spec(1), kv_spec(-1), kv_spec(0), kv_spec(1), sink_spec, q_spec],
        out_specs=[q_spec, kv_full, kv_full, sink_spec],
        out_shape=[jax.ShapeDtypeStruct(q.shape, F32), jax.ShapeDtypeStruct(k.shape, F32),
                   jax.ShapeDtypeStruct(v.shape, F32), jax.ShapeDtypeStruct(sink.shape, F32)],
        compiler_params=_cparams(("arbitrary", "arbitrary", "arbitrary")),
    )(q, k, k, k, v, v, v, sink, dy)


CONV_COLS = 256


def _conv_taps(u, seq):
    row = lax.broadcasted_iota(jnp.int32, u.shape, 0)
    prev = jnp.where(row == 0, 0.0, pltpu.roll(u, 1, axis=0))
    nxt = jnp.where(row == seq - 1, 0.0, pltpu.roll(u, seq - 1, axis=0))
    return prev, nxt


def _conv_fwd(proj3, w8):
    B, S, _ = proj3.shape
    ncb = 2 * MLSTM_WIDTH // CONV_COLS

    def body(u_ref, w_ref, o_ref):
        u = u_ref[0]
        prev, nxt = _conv_taps(u, S)
        o_ref[0] = _silu(prev * w_ref[0:1, :] + u * w_ref[1:2, :] + nxt * w_ref[2:3, :] + w_ref[3:4, :])

    return pl.pallas_call(
        body, name="conv_fwd", grid=(B, ncb),
        in_specs=[pl.BlockSpec((1, S, CONV_COLS), lambda b, c: (b, 0, C_QK // CONV_COLS + c)),
                  pl.BlockSpec((8, CONV_COLS), lambda b, c: (0, c))],
        out_specs=pl.BlockSpec((1, S, CONV_COLS), lambda b, c: (b, 0, c)),
        out_shape=jax.ShapeDtypeStruct((B, S, 2 * MLSTM_WIDTH), F32),
        compiler_params=_cparams(("parallel", "parallel")),
    )(proj3, w8)


def _conv_bwd(proj3, w8, dout_f, dout_b):
    B, S, _ = proj3.shape
    ncb = 2 * MLSTM_WIDTH // CONV_COLS

    def body(u_ref, w_ref, df_ref, db_ref, du_ref, dw_ref):
        b = pl.program_id(1)
        u = u_ref[0]
        prev, nxt = _conv_taps(u, S)
        w0, w1, w2 = w_ref[0:1, :], w_ref[1:2, :], w_ref[2:3, :]
        pre = prev * w0 + u * w1 + nxt * w2 + w_ref[3:4, :]
        sig = _sigmoid(pre)
        dpre = (df_ref[0] + db_ref[0]) * (sig * (1.0 + pre * (1.0 - sig)))
        dprev, dnxt = _conv_taps(dpre, S)
        du_ref[0] = dnxt * w0 + dpre * w1 + dprev * w2

        @pl.when(b == 0)
        def _():
            dw_ref[...] = jnp.zeros_like(dw_ref)

        dw_ref[0:1, :] += jnp.sum(dpre * prev, axis=0, keepdims=True)
        dw_ref[1:2, :] += jnp.sum(dpre * u, axis=0, keepdims=True)
        dw_ref[2:3, :] += jnp.sum(dpre * nxt, axis=0, keepdims=True)
        dw_ref[3:4, :] += jnp.sum(dpre, axis=0, keepdims=True)

    blk = pl.BlockSpec((1, S, CONV_COLS), lambda c, b: (b, 0, c))
    return pl.pallas_call(
        body, name="conv_bwd", grid=(ncb, B),
        in_specs=[pl.BlockSpec((1, S, CONV_COLS), lambda c, b: (b, 0, C_QK // CONV_COLS + c)),
                  pl.BlockSpec((8, CONV_COLS), lambda c, b: (0, c)), blk, blk],
        out_specs=[blk, pl.BlockSpec((8, CONV_COLS), lambda c, b: (0, c))],
        out_shape=[jax.ShapeDtypeStruct((B, S, 2 * MLSTM_WIDTH), F32), jax.ShapeDtypeStruct((8, 2 * MLSTM_WIDTH), F32)],
        compiler_params=_cparams(("parallel", "arbitrary")),
    )(proj3, w8, dout_f, dout_b)


def _chunk_masks(direction):
    t = lax.broadcasted_iota(jnp.int32, (MLSTM_CHUNK, MLSTM_CHUNK), 0)
    s = lax.broadcasted_iota(jnp.int32, (MLSTM_CHUNK, MLSTM_CHUNK), 1)
    le, ge = (s <= t).astype(F32), (s >= t).astype(F32)
    eye = (s == t).astype(F32)
    return (le, ge, eye) if direction == 0 else (ge, le, eye)


def _gate_cols(gates, direction, head):
    lane = lax.broadcasted_iota(jnp.int32, gates.shape, 1)
    sel_i = (lane == (2 * direction) * MLSTM_HEADS + head).astype(F32)
    sel_f = (lane == (2 * direction + 1) * MLSTM_HEADS + head).astype(F32)
    return sel_i, sel_f


def _mlstm_fwd(qk, proj3, bias):
    B, S, _ = qk.shape
    nc = S // MLSTM_CHUNK
    H, L, DH = MLSTM_HEADS, MLSTM_CHUNK, MLSTM_HEAD_DIM

    def chunk_of(d, c):
        return c if d == 0 else nc - 1 - c

    def body(qkf, qkb, vf, vb, gf, gb, bias_ref, hf, hb, csf, csb, nsf, nsb, msf, msb, c_st, n_st, m_st):
        c, h = pl.program_id(1), pl.program_id(2)

        @pl.when(c == 0)
        def _():
            for d in range(2):
                c_st[d, h] = jnp.zeros((DH, DH), F32)
                n_st[d, h] = jnp.zeros((1, DH), F32)
                m_st[d, h] = jnp.zeros((1, DH), F32)

        for d, (qk_ref, v_ref, g_ref, h_ref, cs, ns, ms) in enumerate(
                ((qkf, vf, gf, hf, csf, nsf, msf), (qkb, vb, gb, hb, csb, nsb, msb))):
            incl, incl_t, eye = _chunk_masks(d)
            gates = g_ref[0] + bias_ref[...]
            sel_i, sel_f = _gate_cols(gates, d, h)
            li = jnp.sum(gates * sel_i, axis=1, keepdims=True)
            lf_pre = jnp.sum(gates * sel_f, axis=1, keepdims=True)
            c_in, n_in, m_in = c_st[d, h], n_st[d, h], m_st[d, h]
            cs[0, 0, 0], ns[0, 0, 0], ms[0, 0, 0] = c_in, n_in, m_in
            hh, c_new, n_new, m_new = _mlstm_chunk(
                qk_ref[0, :, :DH], qk_ref[0, :, DH:], v_ref[0], li, lf_pre, c_in, n_in,
                jnp.max(m_in, axis=1, keepdims=True), incl, incl_t, eye)
            h_ref[0] = hh
            c_st[d, h], n_st[d, h] = c_new, n_new
            m_st[d, h] = jnp.broadcast_to(m_new, (1, DH))

    def tok_spec(width, base, d, per_head):
        return pl.BlockSpec((1, L, width), lambda b, c, h: (b, chunk_of(d, c), base + (h if per_head else 0)))

    def st_spec(shape, d):
        return pl.BlockSpec((1, 1, 1) + shape, lambda b, c, h: (b, chunk_of(d, c), h, 0, 0))

    in_specs = [tok_spec(2 * DH, 0, 0, True), tok_spec(2 * DH, 0, 1, True),
                tok_spec(DH, C_VM // DH, 0, True), tok_spec(DH, C_VM // DH, 1, True),
                tok_spec(LANES, C_GATES // LANES, 0, False), tok_spec(LANES, C_GATES // LANES, 1, False),
                pl.BlockSpec((1, LANES), lambda b, c, h: (0, 0))]
    out_specs = [tok_spec(DH, 0, 0, True), tok_spec(DH, 0, 1, True),
                 st_spec((DH, DH), 0), st_spec((DH, DH), 1), st_spec((1, DH), 0), st_spec((1, DH), 1),
                 st_spec((1, DH), 0), st_spec((1, DH), 1)]
    hs = jax.ShapeDtypeStruct((B, S, H * DH), F32)
    cs = jax.ShapeDtypeStruct((B, nc, H, DH, DH), F32)
    vs = jax.ShapeDtypeStruct((B, nc, H, 1, DH), F32)
    return pl.pallas_call(
        body, name="mlstm_fwd", grid=(B, nc, H), in_specs=in_specs, out_specs=out_specs,
        out_shape=[hs, hs, cs, cs, vs, vs, vs, vs],
        scratch_shapes=[pltpu.VMEM((2, H, DH, DH), F32), pltpu.VMEM((2, H, 1, DH), F32), pltpu.VMEM((2, H, 1, DH), F32)],
        compiler_params=_cparams(("parallel", "arbitrary", "arbitrary")),
    )(qk, qk, proj3, proj3, proj3, proj3, bias)


def _mlstm_bwd(qk, proj3, bias, states, dh):
    B, S, _ = qk.shape
    nc = S // MLSTM_CHUNK
    H, L, DH = MLSTM_HEADS, MLSTM_CHUNK, MLSTM_HEAD_DIM

    def chunk_of(d, c):
        return nc - 1 - c if d == 0 else c

    def body(qkf, qkb, vf, vb, gf, gb, bias_ref, csf, csb, nsf, nsb, msf, msb, dhf, dhb,
             dqkf, dqkb, dvf, dvb, dgf, dgb, dc_st, dn_st, dm_st):
        c, h = pl.program_id(1), pl.program_id(2)

        @pl.when(c == 0)
        def _():
            for d in range(2):
                dc_st[d, h] = jnp.zeros((DH, DH), F32)
                dn_st[d, h] = jnp.zeros((1, DH), F32)
                dm_st[d, h] = jnp.zeros((1, DH), F32)

        @pl.when(h == 0)
        def _():
            dgf[...] = jnp.zeros_like(dgf)
            dgb[...] = jnp.zeros_like(dgb)

        for d, (qk_ref, v_ref, g_ref, cs, ns, ms, dh_ref, dqk_ref, dv_ref, dg_ref) in enumerate(
                ((qkf, vf, gf, csf, nsf, msf, dhf, dqkf, dvf, dgf), (qkb, vb, gb, csb, nsb, msb, dhb, dqkb, dvb, dgb))):
            incl, incl_t, eye = _chunk_masks(d)
            gates = g_ref[0] + bias_ref[...]
            sel_i, sel_f = _gate_cols(gates, d, h)
            li = jnp.sum(gates * sel_i, axis=1, keepdims=True)
            lf_pre = jnp.sum(gates * sel_f, axis=1, keepdims=True)
            m_in = jnp.max(ms[0, 0, 0], axis=1, keepdims=True)
            _, vjp = jax.vjp(
                functools.partial(_mlstm_chunk, incl=incl, incl_t=incl_t, eye=eye),
                qk_ref[0, :, :DH], qk_ref[0, :, DH:], v_ref[0], li, lf_pre, cs[0, 0, 0], ns[0, 0, 0], m_in)
            dm_out = jnp.max(dm_st[d, h], axis=1, keepdims=True)
            dq, dk, dv, dli, dlf, dc, dn, dm = vjp((dh_ref[0], dc_st[d, h], dn_st[d, h], dm_out))
            dqk_ref[0, :, :DH] = dq
            dqk_ref[0, :, DH:] = dk
            dv_ref[0] = dv
            dg_ref[0] += dli * sel_i + dlf * sel_f
            dc_st[d, h], dn_st[d, h] = dc, dn
            dm_st[d, h] = jnp.broadcast_to(dm, (1, DH))

    def tok_spec(width, base, d, per_head):
        return pl.BlockSpec((1, L, width), lambda b, c, h: (b, chunk_of(d, c), base + (h if per_head else 0)))

    def st_spec(shape, d):
        return pl.BlockSpec((1, 1, 1) + shape, lambda b, c, h: (b, chunk_of(d, c), h, 0, 0))

    in_specs = [tok_spec(2 * DH, 0, 0, True), tok_spec(2 * DH, 0, 1, True),
                tok_spec(DH, C_VM // DH, 0, True), tok_spec(DH, C_VM // DH, 1, True),
                tok_spec(LANES, C_GATES // LANES, 0, False), tok_spec(LANES, C_GATES // LANES, 1, False),
                pl.BlockSpec((1, LANES), lambda b, c, h: (0, 0)),
                st_spec((DH, DH), 0), st_spec((DH, DH), 1), st_spec((1, DH), 0), st_spec((1, DH), 1),
                st_spec((1, DH), 0), st_spec((1, DH), 1), tok_spec(DH, 0, 0, True), tok_spec(DH, 0, 1, True)]
    out_specs = [tok_spec(2 * DH, 0, 0, True), tok_spec(2 * DH, 0, 1, True), tok_spec(DH, 0, 0, True), tok_spec(DH, 0, 1, True),
                 tok_spec(LANES, 0, 0, False), tok_spec(LANES, 0, 1, False)]
    qks = jax.ShapeDtypeStruct((B, S, 2 * H * DH), F32)
    vs = jax.ShapeDtypeStruct((B, S, H * DH), F32)
    gs = jax.ShapeDtypeStruct((B, S, LANES), F32)
    csf, csb, nsf, nsb, msf, msb = states
    return pl.pallas_call(
        body, name="mlstm_bwd", grid=(B, nc, H), in_specs=in_specs, out_specs=out_specs,
        out_shape=[qks, qks, vs, vs, gs, gs],
        scratch_shapes=[pltpu.VMEM((2, H, DH, DH), F32), pltpu.VMEM((2, H, 1, DH), F32), pltpu.VMEM((2, H, 1, DH), F32)],
        compiler_params=_cparams(("parallel", "arbitrary", "arbitrary")),
    )(qk, qk, proj3, proj3, proj3, proj3, bias, csf, csb, nsf, nsb, msf, msb, dh, dh)


ROW_BLOCK = 256
FF_COLS = 512
FF_SHARD = D_FF // N_DEV
FF_SHARD_PAD = 384
FF_PAD = N_DEV * FF_SHARD_PAD


def _rms_fwd(name, x, g):
    T = x.shape[0]
    return _rowwise(name, lambda xv, gv: _rms(xv, gv), [_In(x), _In(g, rows=False)], [_Out(D_MODEL, BF16)], T, ROW_BLOCK)[0]


def _rms_bwd(name, x, g, dh, dres):
    T = x.shape[0]

    def fn(xv, gv, dhv, drv):
        _, vjp = jax.vjp(_rms, xv, gv)
        dx, dg = vjp(dhv)
        return drv + dx, dg

    return _rowwise(name, fn, [_In(x), _In(g, rows=False), _In(dh), _In(dres)],
                    [_Out(D_MODEL), _Out(D_MODEL, rows=False)], T, ROW_BLOCK)


def _mmw(name, a, w, mode, **kw):
    return _matmul(name, a, w[0], mode, bl=w[1], **kw)


def _ffn_fwd(tag, x, g, wg, wu, wd):
    T = x.shape[0]
    h = _rms_fwd(tag + "_norm", x, g)
    gate = _mmw(tag + "_gate", h, wg, "nn")
    up = _mmw(tag + "_up", h, wu, "nn")
    act = _rowwise(tag + "_act", lambda a, b: _silu(a) * b,
                   [_In(gate, FF_COLS, split=True), _In(up, FF_COLS, split=True)],
                   [_Out(FF_PAD, BF16, FF_COLS, split=True)], T, 1024, ncol=FF_PAD // FF_COLS)[0]
    out = _mmw(tag + "_down", act, wd, "nn", res=x, scale=0.5)
    return out, (x, h, gate, up, act)


def _ffn_bwd(tag, saved, g, wg, wu, wd, dx):
    x, h, gate, up, act = saved
    T = x.shape[0]
    dact = _mmw(tag + "_dact", dx, wd, "nt", scale=0.5)
    dwd = _matmul(tag + "_dwd", act, dx, "tn", scale=0.5)

    def fn(a, b, da):
        _, vjp = jax.vjp(lambda p, q: _silu(p) * q, a, b)
        return vjp(da)

    dgate, dup = _rowwise(tag + "_dactfn", fn,
                          [_In(gate, FF_COLS, split=True), _In(up, FF_COLS, split=True), _In(dact, FF_COLS, split=True)],
                          [_Out(FF_PAD, BF16, FF_COLS, split=True), _Out(FF_PAD, BF16, FF_COLS, split=True)],
                          T, 1024, ncol=FF_PAD // FF_COLS)
    dh = _mmw(tag + "_dh1", dgate, wg, "nt")
    dh = _mmw(tag + "_dh2", dup, wu, "nt", res=dh)
    dwg = _matmul(tag + "_dwg", h, dgate, "tn")
    dwu = _matmul(tag + "_dwu", h, dup, "tn")
    dx_new, dg = _rms_bwd(tag + "_dnorm", x, g, dh, dx)
    return dx_new, dg, dwg, dwu, dwd


def _rope_consts():
    half = ROPE_DIM // 2
    inv_freq = jnp.power(jnp.float32(ROPE_THETA), -jnp.arange(half, dtype=F32) * (2.0 / ROPE_DIM))
    row = jnp.zeros((1, ATT_HEAD_DIM), F32).at[0, :ROPE_DIM].set(jnp.concatenate([inv_freq, inv_freq]))
    rot = np.zeros((ATT_HEAD_DIM, ATT_HEAD_DIM), np.float32)
    for i in range(half):
        rot[half + i, i] = -1.0
        rot[i, half + i] = 1.0
    return row, jnp.asarray(rot)


def _prep_fwd(name, t, g, pos, inv_freq_row, rot):
    R = t.shape[0]

    def fn(tv, gv, pv, fv, rv):
        cos, sin = _rope_tables(pv, fv)
        return _qk_prep(tv, gv, cos, sin, rv)

    return _rowwise(name, fn, [_In(t), _In(g, rows=False), _In(pos), _In(inv_freq_row, rows=False), _In(rot, rows=False)],
                    [_Out(ATT_HEAD_DIM)], R, 1024)[0]


def _prep_bwd(name, t, g, pos, inv_freq_row, rot, dout):
    R = t.shape[0]

    def fn(tv, gv, pv, fv, rv, dv):
        cos, sin = _rope_tables(pv, fv)
        _, vjp = jax.vjp(lambda a, b: _qk_prep(a, b, cos, sin, rv), tv, gv)
        return vjp(dv)

    return _rowwise(name, fn, [_In(t), _In(g, rows=False), _In(pos), _In(inv_freq_row, rows=False), _In(rot, rows=False), _In(dout)],
                    [_Out(ATT_HEAD_DIM), _Out(ATT_HEAD_DIM, rows=False)], R, 1024)


def _to_heads(t, B, S, nh):
    return t.reshape(B, S, nh, ATT_HEAD_DIM).transpose(0, 2, 1, 3)


def _from_heads(t):
    B, nh, S, _ = t.shape
    return t.transpose(0, 2, 1, 3).reshape(B * S, nh * ATT_HEAD_DIM)


def _mix_fwd(x, pos_q, pos_k, B, S, p):
    T = B * S
    h = _rms_fwd("mix_norm", x, p["mix_norm"])
    proj = _matmul("mix_proj", h, p["w_in"], "nn")
    proj3 = proj.reshape(B, S, IN_PAD)
    inv_freq_row, rot = _rope_consts()
    qa = proj[:, C_QA:C_QA + ATT_WIDTH].reshape(T * ATT_HEADS, ATT_HEAD_DIM)
    ka = proj[:, C_KA:C_KA + ATT_KV_WIDTH].reshape(T * ATT_KV_HEADS, ATT_HEAD_DIM)
    q_r = _prep_fwd("q_prep", qa, p["attn_q_norm"], pos_q, inv_freq_row, rot)
    k_r = _prep_fwd("k_prep", ka, p["attn_k_norm"], pos_k, inv_freq_row, rot)
    qh = _to_heads(q_r, B, S, ATT_HEADS)
    kh = _to_heads(k_r, B, S, ATT_KV_HEADS)
    vh = _to_heads(proj[:, C_VA:C_VA + ATT_KV_WIDTH], B, S, ATT_KV_HEADS)
    sink = p["attn_sink"].reshape(ATT_KV_HEADS, ATT_GROUP, 1, 1)
    y_a = _from_heads(_attn_fwd(qh, kh, vh, sink))

    qk_c = _conv_fwd(proj3, p["conv_w8"])
    hf, hb, *states = _mlstm_fwd(qk_c, proj3, p["gate_bias"])
    hf2, hb2 = hf.reshape(T, MLSTM_WIDTH), hb.reshape(T, MLSTM_WIDTH)
    DH = MLSTM_HEAD_DIM
    y_m = _rowwise("mlstm_out", _mlstm_combine,
                   [_In(hf2, DH, split=True), _In(hb2, DH, split=True), _In(proj, DH, C_OM // DH, split=True),
                    _In(p["mlstm_out_norm"], DH, split=True, rows=False)],
                   [_Out(MLSTM_WIDTH, BF16, DH, split=True)], T, 1024, ncol=MLSTM_HEADS)[0]

    za = _mmw("branch_a", y_a, p["w_branch_attn"], "nn")
    zm = _mmw("branch_m", y_m, p["w_branch_mlstm"], "nn")
    W = 512
    merged = _rowwise("merge", _merge,
                      [_In(proj, W, C_GMERGE // W, split=True), _In(proj, W, (C_GMERGE + D_MODEL) // W, split=True),
                       _In(za, W, split=True), _In(zm, W, split=True)],
                      [_Out(D_MODEL, BF16, W, split=True)], T, 512, ncol=D_MODEL // W)[0]
    out = _mmw("mix_out", merged, p["w_out"], "nn", res=x)
    saved = dict(x=x, h=h, proj=proj, qa=qa, ka=ka, qh=qh, kh=kh, vh=vh, sink=sink, y_a=y_a, qk_c=qk_c, hf=hf2, hb=hb2,
                 states=states, y_m=y_m, za=za, zm=zm, merged=merged)
    return out, saved


def _mix_bwd(sv, pos_q, pos_k, B, S, p, dx):
    T = B * S
    DH = MLSTM_HEAD_DIM
    proj = sv["proj"]
    proj3 = proj.reshape(B, S, IN_PAD)
    inv_freq_row, rot = _rope_consts()
    g = {}
    dmerged = _mmw("mix_dmerged", dx, p["w_out"], "nt")
    g["w_out"] = _matmul("mix_dwout", sv["merged"], dx, "tn")
    W = 512

    def merge_bwd(ga, gm, za, zm, dm):
        _, vjp = jax.vjp(_merge, ga, gm, za, zm)
        return vjp(dm)

    dga, dgm, dza, dzm = _rowwise(
        "merge_bwd", merge_bwd,
        [_In(proj, W, C_GMERGE // W, split=True), _In(proj, W, (C_GMERGE + D_MODEL) // W, split=True),
         _In(sv["za"], W, split=True), _In(sv["zm"], W, split=True), _In(dmerged, W, split=True)],
        [_Out(D_MODEL, F32, W, split=True), _Out(D_MODEL, F32, W, split=True),
         _Out(D_MODEL, BF16, W, split=True), _Out(D_MODEL, BF16, W, split=True)], T, 512, ncol=D_MODEL // W)
    dya = _mmw("branch_a_dx", dza, p["w_branch_attn"], "nt")
    g["w_branch_attn"] = _matmul("branch_a_dw", sv["y_a"], dza, "tn")
    dym = _mmw("branch_m_dx", dzm, p["w_branch_mlstm"], "nt")
    g["w_branch_mlstm"] = _matmul("branch_m_dw", sv["y_m"], dzm, "tn")

    def combine_bwd(hf, hb, o_pre, gn, dy):
        _, vjp = jax.vjp(_mlstm_combine, hf, hb, o_pre, gn)
        dhf, _, do, dg = vjp(dy)
        return dhf, do, dg

    dh, dom, g["mlstm_out_norm"] = _rowwise(
        "mlstm_out_bwd", combine_bwd,
        [_In(sv["hf"], DH, split=True), _In(sv["hb"], DH, split=True), _In(proj, DH, C_OM // DH, split=True),
         _In(p["mlstm_out_norm"], DH, split=True, rows=False), _In(dym, DH, split=True)],
        [_Out(MLSTM_WIDTH, F32, DH, split=True), _Out(MLSTM_WIDTH, F32, DH, split=True),
         _Out(MLSTM_WIDTH, F32, DH, split=True, rows=False)], T, 1024, ncol=MLSTM_HEADS)
    dqk_f, dqk_b, dv_f, dv_b, dg_f, dg_b = _mlstm_bwd(sv["qk_c"], proj3, p["gate_bias"], sv["states"],
                                                       dh.reshape(B, S, MLSTM_WIDTH))
    dgates, dvm, g["gate_bias"] = _rowwise(
        "mlstm_dsum", lambda a, b, c, d: (a + b, c + d, jnp.sum(a + b, axis=0, keepdims=True)),
        [_In(dg_f.reshape(T, LANES)), _In(dg_b.reshape(T, LANES)), _In(dv_f.reshape(T, MLSTM_WIDTH)), _In(dv_b.reshape(T, MLSTM_WIDTH))],
        [_Out(LANES), _Out(MLSTM_WIDTH), _Out(LANES, rows=False)], T, 1024)
    dqk, g["conv_w8"] = _conv_bwd(proj3, p["conv_w8"], dqk_f, dqk_b)

    dyh = _to_heads(dya, B, S, ATT_HEADS)
    dqh, dkh, dvh, dsink = _attn_bwd(sv["qh"], sv["kh"], sv["vh"], sv["sink"], dyh)
    g["attn_sink"] = dsink.reshape(1, ATT_HEADS)
    dq_r = dqh.transpose(0, 2, 1, 3).reshape(T * ATT_HEADS, ATT_HEAD_DIM)
    dk_r = dkh.transpose(0, 2, 1, 3).reshape(T * ATT_KV_HEADS, ATT_HEAD_DIM)
    dva = _from_heads(dvh)
    dqa, g["attn_q_norm"] = _prep_bwd("q_prep_bwd", sv["qa"], p["attn_q_norm"], pos_q, inv_freq_row, rot, dq_r)
    dka, g["attn_k_norm"] = _prep_bwd("k_prep_bwd", sv["ka"], p["attn_k_norm"], pos_k, inv_freq_row, rot, dk_r)

    dproj = jnp.concatenate(
        [dga.astype(BF16), dgm.astype(BF16), dqk.reshape(T, 2 * MLSTM_WIDTH).astype(BF16), dvm.astype(BF16), dom.astype(BF16),
         dqa.reshape(T, ATT_WIDTH).astype(BF16), dka.reshape(T, ATT_KV_WIDTH).astype(BF16), dva.astype(BF16),
         dgates.astype(BF16)], axis=1)
    dh2 = _matmul("mix_dh", dproj, p["w_in"], "nt")
    g["w_in"] = _matmul("mix_dwin", sv["h"], dproj, "tn")
    dx_new, g["mix_norm"] = _rms_bwd("mix_dnorm", sv["x"], p["mix_norm"], dh2, dx)
    return dx_new, g


def _loss_and_grad(x, g, target):
    T = x.shape[0]

    def loss_fn(xv, gv, tv):
        err = jnp.square(_rms(xv, gv) - tv)
        return 0.5 * jnp.sum(jnp.mean(err, axis=-1, keepdims=True), axis=0, keepdims=True)

    def fn(xv, gv, tv):
        val, vjp = jax.vjp(lambda a, b: loss_fn(a, b, tv), xv, gv)
        dx, dg = vjp(jnp.ones((1, 1), F32))
        return val, dx, dg

    return _rowwise("loss_head", fn, [_In(x), _In(g, rows=False), _In(target)],
                    [_Out(1, rows=False), _Out(D_MODEL), _Out(D_MODEL, rows=False)], T, ROW_BLOCK)


def _block_norm_fwd(x, g):
    T = x.shape[0]
    return _rowwise("block_norm", _rms, [_In(x), _In(g, rows=False)], [_Out(D_MODEL)], T, ROW_BLOCK)[0]


def _block_norm_bwd(x, g, dy):
    T = x.shape[0]

    def fn(xv, gv, dv):
        _, vjp = jax.vjp(_rms, xv, gv)
        return vjp(dv)

    return _rowwise("block_norm_bwd", fn, [_In(x), _In(g, rows=False), _In(dy)],
                    [_Out(D_MODEL), _Out(D_MODEL, rows=False)], T, ROW_BLOCK)


def _qk_perm_cols(t, axis):
    q, k = jnp.split(t, 2, axis=axis)
    parts = []
    for h in range(MLSTM_HEADS):
        sl = [slice(None)] * t.ndim
        sl[axis] = slice(h * MLSTM_HEAD_DIM, (h + 1) * MLSTM_HEAD_DIM)
        parts += [q[tuple(sl)], k[tuple(sl)]]
    return jnp.concatenate(parts, axis=axis)


def _qk_unperm_cols(t, axis):
    qs, ks = [], []
    for h in range(MLSTM_HEADS):
        sl = [slice(None)] * t.ndim
        sl[axis] = slice(2 * h * MLSTM_HEAD_DIM, (2 * h + 1) * MLSTM_HEAD_DIM)
        qs.append(t[tuple(sl)])
        sl[axis] = slice((2 * h + 1) * MLSTM_HEAD_DIM, (2 * h + 2) * MLSTM_HEAD_DIM)
        ks.append(t[tuple(sl)])
    return jnp.concatenate(qs + ks, axis=axis)


def _w_in_arrange(w):
    qa, ka, va, qm, km, vm, om, gm, gmerge = jnp.split(w, np.cumsum(
        (ATT_WIDTH, ATT_KV_WIDTH, ATT_KV_WIDTH, MLSTM_WIDTH, MLSTM_WIDTH, MLSTM_WIDTH, MLSTM_WIDTH, MLSTM_N_GATES))[:].tolist(), axis=1)
    qk = _qk_perm_cols(jnp.concatenate([qm, km], axis=1), 1)
    pad = jnp.zeros((w.shape[0], LANES - MLSTM_N_GATES), w.dtype)
    return jnp.concatenate([gmerge, qk, vm, om, qa, ka, va, gm, pad], axis=1)


def _w_in_restore(w):
    gmerge = w[:, C_GMERGE:C_GMERGE + 2 * D_MODEL]
    qk = _qk_unperm_cols(w[:, C_QK:C_QK + 2 * MLSTM_WIDTH], 1)
    vm, om = w[:, C_VM:C_VM + MLSTM_WIDTH], w[:, C_OM:C_OM + MLSTM_WIDTH]
    qa, ka, va = w[:, C_QA:C_QA + ATT_WIDTH], w[:, C_KA:C_KA + ATT_KV_WIDTH], w[:, C_VA:C_VA + ATT_KV_WIDTH]
    gm = w[:, C_GATES:C_GATES + MLSTM_N_GATES]
    return jnp.concatenate([qa, ka, va, qk, vm, om, gm, gmerge], axis=1)


BIG = ("ffn1_w_gate", "ffn1_w_up", "ffn1_w_down", "w_in", "mlstm_conv_w", "w_branch_attn", "w_branch_mlstm", "w_out",
       "ffn2_w_gate", "ffn2_w_up", "ffn2_w_down")
MATMUL_W = tuple(n for n in BIG if n != "mlstm_conv_w")
SMALL = ("ffn1_norm", "mix_norm", "mlstm_gate_bias", "attn_q_norm", "attn_k_norm", "attn_sink", "mlstm_conv_b",
         "mlstm_out_norm", "ffn2_norm", "block_out_norm")
WEIGHTS = ("ffn1_norm", "ffn1_w_gate", "ffn1_w_up", "ffn1_w_down", "mix_norm", "w_in", "mlstm_gate_bias", "attn_q_norm",
           "attn_k_norm", "attn_sink", "mlstm_conv_w", "mlstm_conv_b", "mlstm_out_norm", "w_branch_attn", "w_branch_mlstm",
           "w_out", "ffn2_norm", "ffn2_w_gate", "ffn2_w_up", "ffn2_w_down", "block_out_norm")
PACK_COLS = 1024


def _padded_rows(n_elems):
    return -(-n_elems // PACK_COLS)


def _pack_flat(arrs, dtype, row_multiple):
    parts = []
    for a in arrs:
        flat = a.reshape(-1).astype(dtype)
        pad = _padded_rows(flat.shape[0]) * PACK_COLS - flat.shape[0]
        parts.append(jnp.pad(flat, (0, pad)) if pad else flat)
    flat = jnp.concatenate(parts)
    rows = flat.shape[0] // PACK_COLS
    extra = (-rows) % row_multiple
    if extra:
        flat = jnp.pad(flat, (0, extra * PACK_COLS))
    return flat.reshape(-1, PACK_COLS)


def _unpack_flat(buf, shapes, lead=()):
    flat = buf.reshape(lead + (-1,))
    out, off = [], 0
    for s in shapes:
        n = int(np.prod(s))
        out.append(flat[..., off:off + n].reshape(lead + tuple(s)))
        off += _padded_rows(n) * PACK_COLS
    return out


class _Lay:
    def __init__(self, shard, axis, width):
        self.shard, self.axis, self.width = shard, axis, width
        self.padded = tuple(width if a == axis else s for a, s in enumerate(shard))
        self.whole = tuple(N_DEV * width if a == axis else s for a, s in enumerate(shard))

    def pad(self, t, lead=0):
        extra = self.width - self.shard[self.axis]
        if not extra:
            return t
        cfg = [(0, 0)] * t.ndim
        cfg[lead + self.axis] = (0, extra)
        return jnp.pad(t, cfg)

    def unpad(self, t, lead=0):
        idx = [slice(None)] * t.ndim
        idx[lead + self.axis] = slice(0, self.shard[self.axis])
        return t[tuple(idx)]


_FF_COL = _Lay((D_MODEL, FF_SHARD), 1, FF_SHARD_PAD)
_FF_ROW = _Lay((FF_SHARD, D_MODEL), 0, FF_SHARD_PAD)
LAYOUTS = {
    "ffn1_w_gate": _FF_COL, "ffn1_w_up": _FF_COL, "ffn1_w_down": _FF_ROW,
    "ffn2_w_gate": _FF_COL, "ffn2_w_up": _FF_COL, "ffn2_w_down": _FF_ROW,
    "w_in": _Lay((D_MODEL, IN_WIDTH // N_DEV), 0, D_MODEL),
    "mlstm_conv_w": _Lay((3, 2 * MLSTM_WIDTH // N_DEV), 1, 2 * MLSTM_WIDTH // N_DEV),
    "w_branch_attn": _Lay((ATT_WIDTH, D_MODEL // N_DEV), 1, D_MODEL // N_DEV),
    "w_branch_mlstm": _Lay((MLSTM_WIDTH, D_MODEL // N_DEV), 1, D_MODEL // N_DEV),
    "w_out": _Lay((D_MODEL // N_DEV, D_MODEL), 0, D_MODEL // N_DEV),
}


def _window(ref, axis, j, width):
    idx = [slice(None)] * len(ref.shape)
    idx[axis] = pl.ds(pl.multiple_of(j * width, width), width)
    return ref.at[tuple(idx)]


ANY = pl.BlockSpec(memory_space=pl.ANY)


def _mesh_pos():
    return lax.axis_index("x"), lax.axis_index("y"), lax.axis_index("c")


def _all_gather(name, shard, vmem=False):
    R, C = shard.shape
    space = pl.BlockSpec(memory_space=pltpu.VMEM) if vmem else ANY

    def body(x_ref, out_ref, send_sems, recv_sems, local_sem):
        x, y, c = _mesh_pos()
        me, sibling = (x, y, c), (x, y, 1 - c)
        chips = [(1 - x, y), (x, 1 - y), (1 - x, 1 - y)]

        def slot(px, py, pc):
            return out_ref.at[4 * px + 2 * py + pc]

        def copy(k, block, to, src=None):
            return pltpu.make_async_remote_copy(
                src_ref=slot(*block) if src is None else src, dst_ref=slot(*block),
                send_sem=send_sems.at[k], recv_sem=recv_sems.at[k], device_id=to, device_id_type=MESH)

        mine = pltpu.make_async_copy(x_ref, slot(*me), local_sem)
        mine.start()
        first = [copy(0, me, sibling, src=x_ref)]
        first += [copy(1 + j, me, (*chip, c), src=x_ref) for j, chip in enumerate(chips)]
        for cp in first:
            cp.start()
        passed = [copy(4 + j, (*chip, c), sibling) for j, chip in enumerate(chips)]
        for j, chip in enumerate(chips):
            copy(1 + j, (*chip, c), me).wait_recv()
            passed[j].start()
        copy(0, sibling, me).wait_recv()
        for j, chip in enumerate(chips):
            copy(4 + j, (*chip, 1 - c), me).wait_recv()
        for cp in first + passed:
            cp.wait_send()
        mine.wait()

    return pl.pallas_call(
        body, name=name, out_shape=jax.ShapeDtypeStruct((N_DEV, R, C), shard.dtype),
        in_specs=[space], out_specs=space,
        scratch_shapes=[pltpu.SemaphoreType.DMA((7,)), pltpu.SemaphoreType.DMA((7,)), pltpu.SemaphoreType.DMA],
    )(shard)


def _gather_weights(name, shards, lays):
    nt = len(shards)

    def body(*refs):
        x_refs, out_refs = refs[:nt], refs[nt:2 * nt]
        send_sems, recv_sems, local_sems = refs[2 * nt:]
        x, y, c = _mesh_pos()
        me, sibling = (x, y, c), (x, y, 1 - c)
        chips = [(1 - x, y), (x, 1 - y), (1 - x, 1 - y)]

        def slot(t, px, py, pc):
            return _window(out_refs[t], 1 + lays[t].axis, 4 * px + 2 * py + pc, lays[t].width)

        def copy(t, k, block, to, src=None):
            return pltpu.make_async_remote_copy(
                src_ref=slot(t, *block) if src is None else src, dst_ref=slot(t, *block),
                send_sem=send_sems.at[k * nt + t], recv_sem=recv_sems.at[k * nt + t], device_id=to, device_id_type=MESH)

        mine = [pltpu.make_async_copy(x_refs[t], slot(t, *me), local_sems.at[t]) for t in range(nt)]
        for cp in mine:
            cp.start()
        first = []
        for t in range(nt):
            first.append(copy(t, 0, me, sibling, src=x_refs[t]))
            first += [copy(t, 1 + j, me, (*chip, c), src=x_refs[t]) for j, chip in enumerate(chips)]
        for cp in first:
            cp.start()
        passed = []
        for j, chip in enumerate(chips):
            for t in range(nt):
                copy(t, 1 + j, (*chip, c), me).wait_recv()
                fwd = copy(t, 4 + j, (*chip, c), sibling)
                fwd.start()
                passed.append(fwd)
        for t in range(nt):
            copy(t, 0, sibling, me).wait_recv()
            for j, chip in enumerate(chips):
                copy(t, 4 + j, (*chip, 1 - c), me).wait_recv()
        for cp in first + passed:
            cp.wait_send()
        for cp in mine:
            cp.wait()

    out_shape = [jax.ShapeDtypeStruct((DEPTH,) + lay.whole, s.dtype) for s, lay in zip(shards, lays)]
    return pl.pallas_call(
        body, name=name, out_shape=out_shape, in_specs=[ANY] * nt, out_specs=[ANY] * nt,
        scratch_shapes=[pltpu.SemaphoreType.DMA((7 * nt,)), pltpu.SemaphoreType.DMA((7 * nt,)), pltpu.SemaphoreType.DMA((nt,))],
    )(*shards)


def _pair_exchange(name, grads, lays):
    nt = len(grads)

    def body(*refs):
        g_refs, land_refs = refs[:nt], refs[nt:2 * nt]
        send_sems, recv_sems = refs[2 * nt:]
        x, y, c = _mesh_pos()
        copies = []
        for t in range(nt):
            for chip in range(4):
                copies.append(pltpu.make_async_remote_copy(
                    src_ref=_window(g_refs[t], lays[t].axis, 2 * chip + (1 - c), lays[t].width), dst_ref=land_refs[t].at[chip],
                    send_sem=send_sems.at[4 * t + chip], recv_sem=recv_sems.at[4 * t + chip],
                    device_id=(x, y, 1 - c), device_id_type=MESH))
        for cp in copies:
            cp.start()
        for cp in copies:
            cp.wait_recv()
        for cp in copies:
            cp.wait_send()

    out_shape = [jax.ShapeDtypeStruct((4,) + lay.padded, g.dtype) for g, lay in zip(grads, lays)]
    return pl.pallas_call(
        body, name=name, out_shape=out_shape, in_specs=[ANY] * nt, out_specs=[ANY] * nt,
        scratch_shapes=[pltpu.SemaphoreType.DMA((4 * nt,)), pltpu.SemaphoreType.DMA((4 * nt,))],
    )(*grads)


def _pair_sum(name, whole, landed, lay, out_dtype):
    R, C = lay.padded
    br = _first_divisor(R, (512, 384, 256, 128, 64, 32, 16, 8))
    nb = R // br
    if lay.axis == 0:
        mine_spec = pl.BlockSpec((br, C), lambda k, i, c_ref: ((2 * k + c_ref[0]) * nb + i, 0))
    else:
        mine_spec = pl.BlockSpec((br, C), lambda k, i, c_ref: (i, 2 * k + c_ref[0]))

    def body(c_ref, mine_ref, sib_ref, o_ref):
        o_ref[0] = (mine_ref[...] + sib_ref[0]).astype(out_dtype)

    c = lax.axis_index("c")
    return pl.pallas_call(
        body, name=name,
        grid_spec=pltpu.PrefetchScalarGridSpec(
            num_scalar_prefetch=1, grid=(4, nb),
            in_specs=[mine_spec, pl.BlockSpec((1, br, C), lambda k, i, c_ref: (k, i, 0))],
            out_specs=pl.BlockSpec((1, br, C), lambda k, i, c_ref: (k, i, 0))),
        out_shape=jax.ShapeDtypeStruct((4, R, C), out_dtype),
        compiler_params=_cparams(("parallel", "parallel")),
    )(c.reshape(1).astype(jnp.int32), whole, landed)


def _chip_exchange(name, sums):
    nt = len(sums)

    def body(*refs):
        s_refs, land_refs = refs[:nt], refs[nt:2 * nt]
        send_sems, recv_sems, local_sems = refs[2 * nt:]
        x, y, c = _mesh_pos()
        my_chip = 2 * x + y
        mine = [pltpu.make_async_copy(s_refs[t].at[my_chip], land_refs[t].at[my_chip], local_sems.at[t]) for t in range(nt)]
        for cp in mine:
            cp.start()
        chips = [(1 - x, y), (x, 1 - y), (1 - x, 1 - y)]
        copies = []
        for t in range(nt):
            for j, (px, py) in enumerate(chips):
                copies.append(pltpu.make_async_remote_copy(
                    src_ref=s_refs[t].at[2 * px + py], dst_ref=land_refs[t].at[my_chip],
                    send_sem=send_sems.at[3 * t + j], recv_sem=recv_sems.at[3 * t + j],
                    device_id=(px, py, c), device_id_type=MESH))
        for cp in copies:
            cp.start()
        for t in range(nt):
            for j, (px, py) in enumerate(chips):
                pltpu.make_async_remote_copy(
                    src_ref=s_refs[t].at[my_chip], dst_ref=land_refs[t].at[2 * px + py],
                    send_sem=send_sems.at[3 * t + j], recv_sem=recv_sems.at[3 * t + j],
                    device_id=(px, py, c), device_id_type=MESH).wait_recv()
        for cp in copies:
            cp.wait_send()
        for cp in mine:
            cp.wait()

    return pl.pallas_call(
        body, name=name, out_shape=[jax.ShapeDtypeStruct(s.shape, s.dtype) for s in sums],
        in_specs=[ANY] * nt, out_specs=[ANY] * nt,
        scratch_shapes=[pltpu.SemaphoreType.DMA((3 * nt,)), pltpu.SemaphoreType.DMA((3 * nt,)), pltpu.SemaphoreType.DMA((nt,))],
    )(*sums)


def _sum_slots(name, slots, n):
    _, R, C = slots.shape
    br = _first_divisor(R, (512, 384, 256, 128, 64, 32, 16, 8))

    def body(s_ref, o_ref):
        acc = s_ref[0].astype(F32)
        for k in range(1, n):
            acc = acc + s_ref[k].astype(F32)
        o_ref[...] = acc

    return pl.pallas_call(
        body, name=name, grid=(R // br,), in_specs=[pl.BlockSpec((n, br, C), lambda i: (0, i, 0))],
        out_specs=pl.BlockSpec((br, C), lambda i: (i, 0)), out_shape=jax.ShapeDtypeStruct((R, C), F32),
        compiler_params=_cparams(("parallel",)),
    )(slots)


def _reduce_scatter(tag, names, grads):
    lays = [LAYOUTS[n] for n in names]
    landed = _pair_exchange(tag + "_pair", grads, lays)
    sums = [_pair_sum(f"{tag}_pairsum_{n}", g, ld, lay, BF16) for n, g, ld, lay in zip(names, grads, landed, lays)]
    got = _chip_exchange(tag + "_chips", sums)
    return [_sum_slots(f"{tag}_sum_{n}", s, 4) for n, s in zip(names, got)]


def _adamw_math(w, g, m, v):
    m = ADAM_B1 * m + (1.0 - ADAM_B1) * g
    v = ADAM_B2 * v + (1.0 - ADAM_B2) * jnp.square(g)
    m_hat = m / (1.0 - ADAM_B1 ** ADAM_STEP)
    v_hat = v / (1.0 - ADAM_B2 ** ADAM_STEP)
    delta = -ADAM_LR * (m_hat / (jnp.sqrt(v_hat) + ADAM_EPS) + ADAM_WD * w)
    return delta, m, v


def _adamw(name, w, g, m, v):
    shape = w.shape
    cols = shape[-1]
    rows = int(np.prod(shape[:-1]))
    br = _first_divisor(rows, (512, 352, 256, 128, 64, 32, 16, 8))
    args = [_In(a.reshape(rows, cols)) for a in (w, g, m, v)]
    outs = _rowwise(name, _adamw_math, args, [_Out(cols), _Out(cols), _Out(cols)], rows, br)
    return [o.reshape(shape) for o in outs]


def _layer_params(full, small, l):
    p = {}
    for n in ("ffn1_w_gate", "ffn1_w_up", "ffn1_w_down", "w_branch_attn", "w_branch_mlstm", "w_out",
              "ffn2_w_gate", "ffn2_w_up", "ffn2_w_down"):
        p[n] = (full[n], l)
    w_in = full["w_in"][l].reshape(N_DEV, D_MODEL, IN_WIDTH // N_DEV).transpose(1, 0, 2).reshape(D_MODEL, IN_WIDTH)
    p["w_in"] = _w_in_arrange(w_in)
    for n in ("ffn1_norm", "mix_norm", "ffn2_norm", "block_out_norm", "mlstm_out_norm", "attn_q_norm", "attn_k_norm"):
        p[n] = small[n][l][None, :]
    p["attn_sink"] = small["attn_sink"][l]
    p["gate_bias"] = jnp.pad(small["mlstm_gate_bias"][l], (0, LANES - MLSTM_N_GATES))[None, :]
    conv_w = _qk_perm_cols(full["mlstm_conv_w"][l].astype(F32), 1)
    conv_b = _qk_perm_cols(small["mlstm_conv_b"][l][None, :], 1)
    p["conv_w8"] = jnp.concatenate([conv_w, conv_b, jnp.zeros((4, 2 * MLSTM_WIDTH), F32)], axis=0)
    return p


def _local_step(x, positions, target, full, small):
    B, S, _ = x.shape
    T = B * S
    pos = positions.reshape(T, 1)
    pos_q = jnp.repeat(pos, ATT_HEADS, axis=0)
    pos_k = jnp.repeat(pos, ATT_KV_HEADS, axis=0)
    params = [_layer_params(full, small, l) for l in range(DEPTH)]
    xs = x.reshape(T, D_MODEL)
    tgt = target.reshape(T, D_MODEL)

    saved = []
    for l, p in enumerate(params):
        x1, s1 = _ffn_fwd("ffn1", xs, p["ffn1_norm"], p["ffn1_w_gate"], p["ffn1_w_up"], p["ffn1_w_down"])
        x2, s2 = _mix_fwd(x1, pos_q, pos_k, B, S, p)
        x3, s3 = _ffn_fwd("ffn2", x2, p["ffn2_norm"], p["ffn2_w_gate"], p["ffn2_w_up"], p["ffn2_w_down"])
        saved.append((s1, s2, s3, x3))
        if l + 1 < DEPTH:
            xs = _block_norm_fwd(x3, p["block_out_norm"])

    big = {n: [None] * DEPTH for n in MATMUL_W}
    sm = {n: [None] * DEPTH for n in SMALL + ("mlstm_conv_w",)}
    loss = None
    dx = None
    for l in reversed(range(DEPTH)):
        p = params[l]
        s1, s2, s3, x3 = saved[l]
        if l == DEPTH - 1:
            loss, dx, dgn = _loss_and_grad(x3, p["block_out_norm"], tgt)
        else:
            dx, dgn = _block_norm_bwd(x3, p["block_out_norm"], dx)
        sm["block_out_norm"][l] = dgn[0]
        dx, dg, dwg, dwu, dwd = _ffn_bwd("ffn2", s3, p["ffn2_norm"], p["ffn2_w_gate"], p["ffn2_w_up"], p["ffn2_w_down"], dx)
        sm["ffn2_norm"][l], big["ffn2_w_gate"][l], big["ffn2_w_up"][l], big["ffn2_w_down"][l] = dg[0], dwg, dwu, dwd
        dx, g = _mix_bwd(s2, pos_q, pos_k, B, S, p, dx)
        big["w_out"][l], big["w_branch_attn"][l], big["w_branch_mlstm"][l] = g["w_out"], g["w_branch_attn"], g["w_branch_mlstm"]
        big["w_in"][l] = _w_in_restore(g["w_in"]).reshape(D_MODEL, N_DEV, IN_WIDTH // N_DEV).transpose(1, 0, 2).reshape(
            N_DEV * D_MODEL, IN_WIDTH // N_DEV)
        dconv = _qk_unperm_cols(g["conv_w8"], 1)
        sm["mlstm_conv_w"][l] = dconv[0:3]
        sm["mlstm_conv_b"][l] = dconv[3]
        sm["mix_norm"][l] = g["mix_norm"][0]
        sm["mlstm_gate_bias"][l] = g["gate_bias"][0, :MLSTM_N_GATES]
        sm["attn_q_norm"][l], sm["attn_k_norm"][l] = g["attn_q_norm"][0], g["attn_k_norm"][0]
        sm["attn_sink"][l] = g["attn_sink"][0]
        sm["mlstm_out_norm"][l] = g["mlstm_out_norm"][0]
        dx, dg, dwg, dwu, dwd = _ffn_bwd("ffn1", s1, p["ffn1_norm"], p["ffn1_w_gate"], p["ffn1_w_up"], p["ffn1_w_down"], dx)
        sm["ffn1_norm"][l], big["ffn1_w_gate"][l], big["ffn1_w_up"][l], big["ffn1_w_down"][l] = dg[0], dwg, dwu, dwd
    sm = {n: jnp.stack(v, axis=0) for n, v in sm.items()}
    return loss, dx.reshape(B, S, D_MODEL), big, sm


def kernel(x, positions, ffn1_norm, ffn1_w_gate, ffn1_w_up, ffn1_w_down, mix_norm, w_in, mlstm_gate_bias, attn_q_norm, attn_k_norm, attn_sink, mlstm_conv_w, mlstm_conv_b, mlstm_out_norm, w_branch_attn, w_branch_mlstm, w_out, ffn2_norm, ffn2_w_gate, ffn2_w_up, ffn2_w_down, block_out_norm, loss_target, m_ffn1_norm, m_ffn1_w_gate, m_ffn1_w_up, m_ffn1_w_down, m_mix_norm, m_w_in, m_mlstm_gate_bias, m_attn_q_norm, m_attn_k_norm, m_attn_sink, m_mlstm_conv_w, m_mlstm_conv_b, m_mlstm_out_norm, m_w_branch_attn, m_w_branch_mlstm, m_w_out, m_ffn2_norm, m_ffn2_w_gate, m_ffn2_w_up, m_ffn2_w_down, m_block_out_norm, v_ffn1_norm, v_ffn1_w_gate, v_ffn1_w_up, v_ffn1_w_down, v_mix_norm, v_w_in, v_mlstm_gate_bias, v_attn_q_norm, v_attn_k_norm, v_attn_sink, v_mlstm_conv_w, v_mlstm_conv_b, v_mlstm_out_norm, v_w_branch_attn, v_w_branch_mlstm, v_w_out, v_ffn2_norm, v_ffn2_w_gate, v_ffn2_w_up, v_ffn2_w_down, v_block_out_norm):
    args = locals()
    w = {n: args[n] for n in WEIGHTS}
    m = {n: args["m_" + n] for n in WEIGHTS}
    v = {n: args["v_" + n] for n in WEIGHTS}

    lays = [LAYOUTS[n] for n in MATMUL_W]
    shards = [lay.pad(w[n].astype(BF16), lead=1) for n, lay in zip(MATMUL_W, lays)]
    full = dict(zip(MATMUL_W, _gather_weights("weights_all_gather", shards, lays)))
    conv_shape = w["mlstm_conv_w"].shape
    conv_all = _all_gather("conv_all_gather", _pack_flat([w["mlstm_conv_w"]], F32, 8), vmem=True)
    conv_parts = _unpack_flat(conv_all, [conv_shape], lead=(N_DEV,))[0]
    full["mlstm_conv_w"] = jnp.concatenate([conv_parts[j] for j in range(N_DEV)], axis=2)
    small = {n: w[n] for n in SMALL}

    loss, grad_x, big_g, small_g = _local_step(x, positions, loss_target, full, small)

    totals = [_reduce_scatter("grads", MATMUL_W, [big_g[n][l] for n in MATMUL_W]) for l in range(DEPTH)]
    grads = {n: jnp.stack([lay.unpad(totals[l][i]) for l in range(DEPTH)], axis=0)
             for i, (n, lay) in enumerate(zip(MATMUL_W, lays))}

    small_names = SMALL + ("mlstm_conv_w",)
    small_shapes = [small_g[n].shape for n in small_names] + [(1, 1)]
    small_packed = _pack_flat([small_g[n] for n in small_names] + [loss], F32, 8)
    small_all = _all_gather("small_all_gather", small_packed, vmem=True)
    small_sum = _sum_slots("small_sum", small_all, N_DEV)
    *small_grads, loss_total = _unpack_flat(small_sum, small_shapes)
    grads.update(dict(zip(small_names, small_grads)))
    x_pos, y_pos, c_pos = _mesh_pos()
    grads["mlstm_conv_w"] = lax.dynamic_slice_in_dim(
        grads["mlstm_conv_w"], (4 * x_pos + 2 * y_pos + c_pos) * conv_shape[2], conv_shape[2], axis=2)

    deltas, new_m, new_v = {}, {}, {}
    for n in BIG:
        deltas[n], new_m[n], new_v[n] = _adamw("adamw_" + n, w[n], grads[n], m[n], v[n])
    sw, sg, smm, sv = (_pack_flat([d[n] for n in SMALL], F32, 8) for d in (w, grads, m, v))
    sd, snm, snv = _adamw("adamw_small", sw, sg, smm, sv)
    shapes = [w[n].shape for n in SMALL]
    for d, buf in ((deltas, sd), (new_m, snm), (new_v, snv)):
        d.update(dict(zip(SMALL, _unpack_flat(buf, shapes))))

    return (loss_total.reshape(()), grad_x, *[grads[n] for n in WEIGHTS], *[deltas[n] for n in WEIGHTS],
            *[new_m[n] for n in WEIGHTS], *[new_v[n] for n in WEIGHTS])
```

```python
import functools

import numpy as np
import jax
import jax.numpy as jnp
from jax import lax
from jax.experimental import pallas as pl
from jax.experimental.pallas import tpu as pltpu

F32 = jnp.float32
BF16 = jnp.bfloat16

D_MODEL = 1024
D_FF = 2816
ATT_HEAD_DIM = 64
ATT_HEADS = 8
ATT_KV_HEADS = 2
ATT_GROUP = ATT_HEADS // ATT_KV_HEADS
ATT_WIDTH = ATT_HEADS * ATT_HEAD_DIM
ATT_KV_WIDTH = ATT_KV_HEADS * ATT_HEAD_DIM
WINDOW = 128
ATT_BLOCK = 128
ROPE_DIM = 16
ROPE_THETA = 500000.0
MLSTM_HEADS = 4
MLSTM_HEAD_DIM = 128
MLSTM_WIDTH = MLSTM_HEADS * MLSTM_HEAD_DIM
MLSTM_CHUNK = 128
MLSTM_N_GATES = 4 * MLSTM_HEADS
NORM_EPS = 1e-6
IN_WIDTH = 4880
DEPTH = 2
N_DEV = 8

ADAM_LR = 0.001
ADAM_B1 = 0.9
ADAM_B2 = 0.999
ADAM_EPS = 1e-08
ADAM_WD = 0.01
ADAM_STEP = 10

LANES = 128
C_GMERGE = 0
C_QK = 2048
C_VM = 3072
C_OM = 3584
C_QA = 4096
C_KA = 4608
C_VA = 4736
C_GATES = 4864
IN_PAD = 4992

VMEM_LIMIT = 48 * 1024 * 1024

MESH = pl.DeviceIdType.MESH


def _cparams(sem):
    return pltpu.CompilerParams(dimension_semantics=sem, vmem_limit_bytes=VMEM_LIMIT)


def _first_divisor(n, cands):
    for c in cands:
        if n % c == 0:
            return c
    return n


_NN = ((1,), (0,))
_NT = ((1,), (1,))
_TN = ((0,), (0,))


def _mm(a, b, dims):
    return lax.dot_general(a.astype(BF16), b.astype(BF16), (dims, ((), ())), preferred_element_type=F32)


@jax.custom_vjp
def mm_nn(a, b):
    return _mm(a, b, _NN)


def _mm_nn_fwd(a, b):
    return _mm(a, b, _NN), (a, b)


def _mm_nn_bwd(res, g):
    a, b = res
    return _mm(g, b, _NT).astype(a.dtype), _mm(a, g, _TN).astype(b.dtype)


mm_nn.defvjp(_mm_nn_fwd, _mm_nn_bwd)


@jax.custom_vjp
def mm_nt(a, b):
    return _mm(a, b, _NT)


def _mm_nt_fwd(a, b):
    return _mm(a, b, _NT), (a, b)


def _mm_nt_bwd(res, g):
    a, b = res
    return _mm(g, b, _NN).astype(a.dtype), _mm(g, a, _TN).astype(b.dtype)


mm_nt.defvjp(_mm_nt_fwd, _mm_nt_bwd)


@jax.custom_vjp
def mm_tn(a, b):
    return _mm(a, b, _TN)


def _mm_tn_fwd(a, b):
    return _mm(a, b, _TN), (a, b)


def _mm_tn_bwd(res, g):
    a, b = res
    return _mm(b, g, _NT).astype(a.dtype), _mm(a, g, _NN).astype(b.dtype)


mm_tn.defvjp(_mm_tn_fwd, _mm_tn_bwd)


def _matmul(name, a, b, mode, out_dtype=F32, res=None, scale=1.0, bl=None):
    b_shape = b.shape if bl is None else b.shape[1:]
    if mode == "nn":
        (M, K), (K2, N) = a.shape, b_shape
    elif mode == "nt":
        (M, K), (N, K2) = a.shape, b_shape
    else:
        (K, M), (K2, N) = a.shape, b_shape
    assert K == K2, (name, a.shape, b.shape)
    tm = _first_divisor(M, (1024, 1408, 512, 384, 256, 128))
    tn = _first_divisor(N, (1024, 512, 384, 256, 128))
    tk = _first_divisor(K, (1024, 1408, 1664, 512, 256, 128))
    nk = K // tk
    if mode == "tn":
        a_spec = pl.BlockSpec((tk, tm), lambda i, j, k: (k, i))
    else:
        a_spec = pl.BlockSpec((tm, tk), lambda i, j, k: (i, k))
    if mode == "nt":
        b_blk, b_idx = (tn, tk), (lambda i, j, k: (j, k))
    else:
        b_blk, b_idx = (tk, tn), (lambda i, j, k: (k, j))
    if bl is None:
        b_spec = pl.BlockSpec(b_blk, b_idx)
    else:
        b_spec = pl.BlockSpec((None,) + b_blk, lambda i, j, k: (bl,) + b_idx(i, j, k))
    o_spec = pl.BlockSpec((tm, tn), lambda i, j, k: (i, j))
    dims = {"nn": _NN, "nt": _NT, "tn": _TN}[mode]
    has_res = res is not None

    def body(*refs):
        if has_res:
            a_ref, b_ref, r_ref, o_ref, acc = refs
        else:
            a_ref, b_ref, o_ref, acc = refs
        k = pl.program_id(2)

        @pl.when(k == 0)
        def _():
            acc[...] = jnp.zeros_like(acc)

        acc[...] += _mm(a_ref[...], b_ref[...], dims)

        @pl.when(k == nk - 1)
        def _():
            out = acc[...]
            if scale != 1.0:
                out = out * scale
            if has_res:
                out = r_ref[...].astype(F32) + out
            o_ref[...] = out.astype(out_dtype)

    in_specs = [a_spec, b_spec] + ([o_spec] if has_res else [])
    args = (a, b) + ((res,) if has_res else ())
    return pl.pallas_call(
        body, name=name, grid=(M // tm, N // tn, nk), in_specs=in_specs, out_specs=o_spec,
        out_shape=jax.ShapeDtypeStruct((M, N), out_dtype), scratch_shapes=[pltpu.VMEM((tm, tn), F32)],
        compiler_params=_cparams(("parallel", "parallel", "arbitrary")),
    )(*args)


class _In:
    def __init__(self, arr, width=None, base=0, split=False, rows=True):
        self.arr, self.base, self.split, self.rows = arr, base, split, rows
        self.width = arr.shape[1] if width is None else width


class _Out:
    def __init__(self, cols, dtype=F32, width=None, split=False, rows=True, nrows=1):
        self.cols, self.dtype, self.split, self.rows, self.nrows = cols, dtype, split, rows, nrows
        self.width = cols if width is None else width


def _rowwise(name, fn, ins, outs, n_rows, br, ncol=1):
    br = min(br, n_rows)
    assert n_rows % br == 0, (name, n_rows, br)
    nrow_blocks = n_rows // br

    def in_spec(d):
        nb = br if d.rows else d.arr.shape[0]
        if d.rows and d.split:
            im = lambda j, i, base=d.base: (i, base + j)
        elif d.rows:
            im = lambda j, i, base=d.base: (i, base)
        elif d.split:
            im = lambda j, i, base=d.base: (0, base + j)
        else:
            im = lambda j, i, base=d.base: (0, base)
        return pl.BlockSpec((nb, d.width), im)

    def out_spec(d):
        nb = br if d.rows else d.nrows
        if d.rows and d.split:
            im = lambda j, i: (i, j)
        elif d.rows:
            im = lambda j, i: (i, 0)
        elif d.split:
            im = lambda j, i: (0, j)
        else:
            im = lambda j, i: (0, 0)
        return pl.BlockSpec((nb, d.width), im)

    n_in = len(ins)

    def body(*refs):
        i = pl.program_id(1)
        vals = [r[...] for r in refs[:n_in]]
        res = fn(*vals)
        if not isinstance(res, (tuple, list)):
            res = (res,)
        for d, ref, val in zip(outs, refs[n_in:], res):
            if d.rows:
                ref[...] = val.astype(d.dtype)
            else:
                @pl.when(i == 0)
                def _(ref=ref):
                    ref[...] = jnp.zeros_like(ref)

                ref[...] += val.astype(d.dtype)

    out_shape = [jax.ShapeDtypeStruct((n_rows if d.rows else d.nrows, d.cols), d.dtype) for d in outs]
    res = pl.pallas_call(
        body, name=name, grid=(ncol, nrow_blocks), in_specs=[in_spec(d) for d in ins],
        out_specs=[out_spec(d) for d in outs], out_shape=out_shape,
        compiler_params=_cparams(("parallel", "arbitrary")),
    )(*[d.arr for d in ins])
    return res


def _rms(x, g):
    return x * lax.rsqrt(jnp.mean(x * x, axis=-1, keepdims=True) + NORM_EPS) * g


def _sigmoid(x):
    return 1.0 / (1.0 + jnp.exp(-x))


def _silu(x):
    return x * _sigmoid(x)


def _log_sigmoid(x):
    return jnp.minimum(x, 0.0) - jnp.log(1.0 + jnp.exp(-jnp.abs(x)))


def _rope_tables(pos, inv_freq_row):
    ang = pos.astype(F32) * inv_freq_row
    return jnp.cos(ang), jnp.sin(ang)


def _qk_prep(t, g, cos, sin, rot_mat):
    y = _rms(t, g)
    rot = lax.dot_general(y, rot_mat, (_NN, ((), ())), precision=lax.Precision.HIGHEST, preferred_element_type=F32)
    return y * cos + rot * sin


def _attn_head(q, kb, vb, sink, valid):
    s = mm_nt(q, kb) * (ATT_HEAD_DIM ** -0.5)
    s = jnp.where(valid, s, -jnp.inf)
    m = jnp.maximum(jnp.max(s, axis=-1, keepdims=True), sink)
    p = jnp.exp(s - m)
    den = jnp.sum(p, axis=-1, keepdims=True) + jnp.exp(sink - m)
    return mm_nn(p / den, vb)


def _mlstm_chunk(q, k, v, li, lf_pre, C, n, m, incl, incl_t, eye):
    k = k * (MLSTM_HEAD_DIM ** -0.5)
    lf = _log_sigmoid(lf_pre)
    lf_row = jnp.sum(eye * lf, axis=0, keepdims=True)
    li_row = jnp.sum(eye * li, axis=0, keepdims=True)
    b = jnp.sum(incl * lf_row, axis=1, keepdims=True)
    b_row = jnp.sum(incl_t * lf, axis=0, keepdims=True)
    b_tot = jnp.sum(lf, axis=0, keepdims=True)
    a = b_tot - b + li
    a_max = jnp.max(a, axis=0, keepdims=True)
    kw = k * jnp.exp(a - a_max)
    c_loc = mm_tn(kw, v)
    n_loc = jnp.sum(kw, axis=0, keepdims=True)

    dmat = jnp.where(incl > 0.5, b - b_row + li_row, -jnp.inf)
    inter = b + m
    m_t = jnp.maximum(inter, jnp.max(dmat, axis=1, keepdims=True))
    sc = mm_nt(q, k) * jnp.exp(dmat - m_t)
    scale_in = jnp.exp(inter - m_t)
    num = mm_nn(sc, v) + scale_in * mm_nn(q, C)
    den = jnp.sum(sc, axis=1, keepdims=True) + scale_in * jnp.sum(q * n, axis=1, keepdims=True)
    h = num / jnp.maximum(jnp.abs(den), jnp.exp(-m_t))

    m_new = jnp.maximum(b_tot + m, a_max)
    s_p = jnp.exp(b_tot + m - m_new)
    s_l = jnp.exp(a_max - m_new)
    return h, s_p * C + s_l * c_loc, s_p * n + s_l * n_loc, m_new


def _mlstm_combine(hf, hb, o_pre, g):
    h = hf + hb
    mu = jnp.mean(h, axis=-1, keepdims=True)
    var = jnp.mean(jnp.square(h - mu), axis=-1, keepdims=True)
    return _sigmoid(o_pre) * ((h - mu) * lax.rsqrt(var + NORM_EPS) * g)


def _merge(ga, gm, za, zm):
    return _sigmoid(ga) * za + _sigmoid(gm) * zm


def _attn_mask(n, seq):
    qi = n * ATT_BLOCK + lax.broadcasted_iota(jnp.int32, (ATT_BLOCK, 3 * ATT_BLOCK), 0)
    kj = (n - 1) * ATT_BLOCK + lax.broadcasted_iota(jnp.int32, (ATT_BLOCK, 3 * ATT_BLOCK), 1)
    return (jnp.abs(qi - kj) <= WINDOW) & (kj >= 0) & (kj < seq)


def _attn_specs(nq):
    q_spec = pl.BlockSpec((1, ATT_GROUP, ATT_BLOCK, ATT_HEAD_DIM), lambda h, b, n: (b, h, n, 0))

    def kv_spec(off):
        return pl.BlockSpec((1, 1, ATT_BLOCK, ATT_HEAD_DIM),
                            lambda h, b, n: (b, h, jnp.clip(n + off, 0, nq - 1), 0))

    sink_spec = pl.BlockSpec((1, ATT_GROUP, 1, 1), lambda h, b, n: (h, 0, 0, 0))
    return q_spec, kv_spec, sink_spec


def _attn_fwd(q, k, v, sink):
    B, _, S, _ = q.shape
    nq = S // ATT_BLOCK
    q_spec, kv_spec, sink_spec = _attn_specs(nq)

    def body(q_ref, kp, kc, kn, vp, vc, vn, s_ref, o_ref):
        valid = _attn_mask(pl.program_id(2), S)
        kb = jnp.concatenate([kp[0, 0], kc[0, 0], kn[0, 0]], axis=0)
        vb = jnp.concatenate([vp[0, 0], vc[0, 0], vn[0, 0]], axis=0)
        for g in range(ATT_GROUP):
            o_ref[0, g] = _attn_head(q_ref[0, g], kb, vb, s_ref[0, g], valid).astype(BF16)

    return pl.pallas_call(
        body, name="attn_fwd", grid=(ATT_KV_HEADS, B, nq),
        in_specs=[q_spec, kv_spec(-1), kv_spec(0), kv_spec(1), kv_spec(-1), kv_spec(0), kv_spec(1), sink_spec],
        out_specs=q_spec, out_shape=jax.ShapeDtypeStruct(q.shape, BF16),
        compiler_params=_cparams(("parallel", "parallel", "arbitrary")),
    )(q, k, k, k, v, v, v, sink)


def _attn_bwd(q, k, v, sink, dy):
    B, _, S, _ = q.shape
    nq = S // ATT_BLOCK
    q_spec, kv_spec, sink_spec = _attn_specs(nq)
    kv_full = pl.BlockSpec((1, 1, S, ATT_HEAD_DIM), lambda h, b, n: (b, h, 0, 0))

    def body(q_ref, kp, kc, kn, vp, vc, vn, s_ref, dy_ref, dq_ref, dk_ref, dv_ref, ds_ref):
        b, n = pl.program_id(1), pl.program_id(2)
        valid = _attn_mask(n, S)
        kb = jnp.concatenate([kp[0, 0], kc[0, 0], kn[0, 0]], axis=0)
        vb = jnp.concatenate([vp[0, 0], vc[0, 0], vn[0, 0]], axis=0)

        @pl.when(n == 0)
        def _():
            dk_ref[...] = jnp.zeros_like(dk_ref)
            dv_ref[...] = jnp.zeros_like(dv_ref)

        @pl.when((n == 0) & (b == 0))
        def _():
            ds_ref[...] = jnp.zeros_like(ds_ref)

        dkb = jnp.zeros_like(kb)
        dvb = jnp.zeros_like(vb)
        for g in range(ATT_GROUP):
            _, vjp = jax.vjp(functools.partial(_attn_head, valid=valid), q_ref[0, g], kb, vb, s_ref[0, g])
            dq, dk_g, dv_g, dsink = vjp(dy_ref[0, g])
            dq_ref[0, g] = dq
            ds_ref[0, g] += dsink
            dkb += dk_g
            dvb += dv_g
        for j, off in enumerate((-1, 0, 1)):
            start = pl.multiple_of(jnp.clip(n + off, 0, nq - 1) * ATT_BLOCK, ATT_BLOCK)
            rows = pl.ds(start, ATT_BLOCK)
            dk_ref[0, 0, rows, :] += dkb[j * ATT_BLOCK:(j + 1) * ATT_BLOCK]
            dv_ref[0, 0, rows, :] += dvb[j * ATT_BLOCK:(j + 1) * ATT_BLOCK]

    return pl.pallas_call(
        body, name="attn_bwd", grid=(ATT_KV_HEADS, B, nq),
        in_specs=[q_spec, kv_spec(-1), kv_spec(0), kv_spec(1), kv_spec(-1), kv_spec(0), kv_spec(1), sink_spec, q_spec],
        out_specs=[q_spec, kv_full, kv_full, sink_spec],
        out_shape=[jax.ShapeDtypeStruct(q.shape, F32), jax.ShapeDtypeStruct(k.shape, F32),
                   jax.ShapeDtypeStruct(v.shape, F32), jax.ShapeDtypeStruct(sink.shape, F32)],
        compiler_params=_cparams(("arbitrary", "arbitrary", "arbitrary")),
    )(q, k, k, k, v, v, v, sink, dy)


CONV_COLS = 256


def _conv_taps(u, seq):
    row = lax.broadcasted_iota(jnp.int32, u.shape, 0)
    prev = jnp.where(row == 0, 0.0, pltpu.roll(u, 1, axis=0))
    nxt = jnp.where(row == seq - 1, 0.0, pltpu.roll(u, seq - 1, axis=0))
    return prev, nxt


def _conv_fwd(proj3, w8):
    B, S, _ = proj3.shape
    ncb = 2 * MLSTM_WIDTH // CONV_COLS

    def body(u_ref, w_ref, o_ref):
        u = u_ref[0]
        prev, nxt = _conv_taps(u, S)
        o_ref[0] = _silu(prev * w_ref[0:1, :] + u * w_ref[1:2, :] + nxt * w_ref[2:3, :] + w_ref[3:4, :])

    return pl.pallas_call(
        body, name="conv_fwd", grid=(B, ncb),
        in_specs=[pl.BlockSpec((1, S, CONV_COLS), lambda b, c: (b, 0, C_QK // CONV_COLS + c)),
                  pl.BlockSpec((8, CONV_COLS), lambda b, c: (0, c))],
        out_specs=pl.BlockSpec((1, S, CONV_COLS), lambda b, c: (b, 0, c)),
        out_shape=jax.ShapeDtypeStruct((B, S, 2 * MLSTM_WIDTH), F32),
        compiler_params=_cparams(("parallel", "parallel")),
    )(proj3, w8)


def _conv_bwd(proj3, w8, dout_f, dout_b):
    B, S, _ = proj3.shape
    ncb = 2 * MLSTM_WIDTH // CONV_COLS

    def body(u_ref, w_ref, df_ref, db_ref, du_ref, dw_ref):
        b = pl.program_id(1)
        u = u_ref[0]
        prev, nxt = _conv_taps(u, S)
        w0, w1, w2 = w_ref[0:1, :], w_ref[1:2, :], w_ref[2:3, :]
        pre = prev * w0 + u * w1 + nxt * w2 + w_ref[3:4, :]
        sig = _sigmoid(pre)
        dpre = (df_ref[0] + db_ref[0]) * (sig * (1.0 + pre * (1.0 - sig)))
        dprev, dnxt = _conv_taps(dpre, S)
        du_ref[0] = dnxt * w0 + dpre * w1 + dprev * w2

        @pl.when(b == 0)
        def _():
            dw_ref[...] = jnp.zeros_like(dw_ref)

        dw_ref[0:1, :] += jnp.sum(dpre * prev, axis=0, keepdims=True)
        dw_ref[1:2, :] += jnp.sum(dpre * u, axis=0, keepdims=True)
        dw_ref[2:3, :] += jnp.sum(dpre * nxt, axis=0, keepdims=True)
        dw_ref[3:4, :] += jnp.sum(dpre, axis=0, keepdims=True)

    blk = pl.BlockSpec((1, S, CONV_COLS), lambda c, b: (b, 0, c))
    return pl.pallas_call(
        body, name="conv_bwd", grid=(ncb, B),
        in_specs=[pl.BlockSpec((1, S, CONV_COLS), lambda c, b: (b, 0, C_QK // CONV_COLS + c)),
                  pl.BlockSpec((8, CONV_COLS), lambda c, b: (0, c)), blk, blk],
        out_specs=[blk, pl.BlockSpec((8, CONV_COLS), lambda c, b: (0, c))],
        out_shape=[jax.ShapeDtypeStruct((B, S, 2 * MLSTM_WIDTH), F32), jax.ShapeDtypeStruct((8, 2 * MLSTM_WIDTH), F32)],
        compiler_params=_cparams(("parallel", "arbitrary")),
    )(proj3, w8, dout_f, dout_b)


def _chunk_masks(direction):
    t = lax.broadcasted_iota(jnp.int32, (MLSTM_CHUNK, MLSTM_CHUNK), 0)
    s = lax.broadcasted_iota(jnp.int32, (MLSTM_CHUNK, MLSTM_CHUNK), 1)
    le, ge = (s <= t).astype(F32), (s >= t).astype(F32)
    eye = (s == t).astype(F32)
    return (le, ge, eye) if direction == 0 else (ge, le, eye)


def _gate_cols(gates, direction, head):
    lane = lax.broadcasted_iota(jnp.int32, gates.shape, 1)
    sel_i = (lane == (2 * direction) * MLSTM_HEADS + head).astype(F32)
    sel_f = (lane == (2 * direction + 1) * MLSTM_HEADS + head).astype(F32)
    return sel_i, sel_f


def _mlstm_fwd(qk, proj3, bias):
    B, S, _ = qk.shape
    nc = S // MLSTM_CHUNK
    H, L, DH = MLSTM_HEADS, MLSTM_CHUNK, MLSTM_HEAD_DIM

    def chunk_of(d, c):
        return c if d == 0 else nc - 1 - c

    def body(qkf, qkb, vf, vb, gf, gb, bias_ref, hf, hb, csf, csb, nsf, nsb, msf, msb, c_st, n_st, m_st):
        c, h = pl.program_id(1), pl.program_id(2)

        @pl.when(c == 0)
        def _():
            for d in range(2):
                c_st[d, h] = jnp.zeros((DH, DH), F32)
                n_st[d, h] = jnp.zeros((1, DH), F32)
                m_st[d, h] = jnp.zeros((1, DH), F32)

        for d, (qk_ref, v_ref, g_ref, h_ref, cs, ns, ms) in enumerate(
                ((qkf, vf, gf, hf, csf, nsf, msf), (qkb, vb, gb, hb, csb, nsb, msb))):
            incl, incl_t, eye = _chunk_masks(d)
            gates = g_ref[0] + bias_ref[...]
            sel_i, sel_f = _gate_cols(gates, d, h)
            li = jnp.sum(gates * sel_i, axis=1, keepdims=True)
            lf_pre = jnp.sum(gates * sel_f, axis=1, keepdims=True)
            c_in, n_in, m_in = c_st[d, h], n_st[d, h], m_st[d, h]
            cs[0, 0, 0], ns[0, 0, 0], ms[0, 0, 0] = c_in, n_in, m_in
            hh, c_new, n_new, m_new = _mlstm_chunk(
                qk_ref[0, :, :DH], qk_ref[0, :, DH:], v_ref[0], li, lf_pre, c_in, n_in,
                jnp.max(m_in, axis=1, keepdims=True), incl, incl_t, eye)
            h_ref[0] = hh
            c_st[d, h], n_st[d, h] = c_new, n_new
            m_st[d, h] = jnp.broadcast_to(m_new, (1, DH))

    def tok_spec(width, base, d, per_head):
        return pl.BlockSpec((1, L, width), lambda b, c, h: (b, chunk_of(d, c), base + (h if per_head else 0)))

    def st_spec(shape, d):
        return pl.BlockSpec((1, 1, 1) + shape, lambda b, c, h: (b, chunk_of(d, c), h, 0, 0))

    in_specs = [tok_spec(2 * DH, 0, 0, True), tok_spec(2 * DH, 0, 1, True),
                tok_spec(DH, C_VM // DH, 0, True), tok_spec(DH, C_VM // DH, 1, True),
                tok_spec(LANES, C_GATES // LANES, 0, False), tok_spec(LANES, C_GATES // LANES, 1, False),
                pl.BlockSpec((1, LANES), lambda b, c, h: (0, 0))]
    out_specs = [tok_spec(DH, 0, 0, True), tok_spec(DH, 0, 1, True),
                 st_spec((DH, DH), 0), st_spec((DH, DH), 1), st_spec((1, DH), 0), st_spec((1, DH), 1),
                 st_spec((1, DH), 0), st_spec((1, DH), 1)]
    hs = jax.ShapeDtypeStruct((B, S, H * DH), F32)
    cs = jax.ShapeDtypeStruct((B, nc, H, DH, DH), F32)
    vs = jax.ShapeDtypeStruct((B, nc, H, 1, DH), F32)
    return pl.pallas_call(
        body, name="mlstm_fwd", grid=(B, nc, H), in_specs=in_specs, out_specs=out_specs,
        out_shape=[hs, hs, cs, cs, vs, vs, vs, vs],
        scratch_shapes=[pltpu.VMEM((2, H, DH, DH), F32), pltpu.VMEM((2, H, 1, DH), F32), pltpu.VMEM((2, H, 1, DH), F32)],
        compiler_params=_cparams(("parallel", "arbitrary", "arbitrary")),
    )(qk, qk, proj3, proj3, proj3, proj3, bias)


def _mlstm_bwd(qk, proj3, bias, states, dh):
    B, S, _ = qk.shape
    nc = S // MLSTM_CHUNK
    H, L, DH = MLSTM_HEADS, MLSTM_CHUNK, MLSTM_HEAD_DIM

    def chunk_of(d, c):
        return nc - 1 - c if d == 0 else c

    def body(qkf, qkb, vf, vb, gf, gb, bias_ref, csf, csb, nsf, nsb, msf, msb, dhf, dhb,
             dqkf, dqkb, dvf, dvb, dgf, dgb, dc_st, dn_st, dm_st):
        c, h = pl.program_id(1), pl.program_id(2)

        @pl.when(c == 0)
        def _():
            for d in range(2):
                dc_st[d, h] = jnp.zeros((DH, DH), F32)
                dn_st[d, h] = jnp.zeros((1, DH), F32)
                dm_st[d, h] = jnp.zeros((1, DH), F32)

        @pl.when(h == 0)
        def _():
            dgf[...] = jnp.zeros_like(dgf)
            dgb[...] = jnp.zeros_like(dgb)

        for d, (qk_ref, v_ref, g_ref, cs, ns, ms, dh_ref, dqk_ref, dv_ref, dg_ref) in enumerate(
                ((qkf, vf, gf, csf, nsf, msf, dhf, dqkf, dvf, dgf), (qkb, vb, gb, csb, nsb, msb, dhb, dqkb, dvb, dgb))):
            incl, incl_t, eye = _chunk_masks(d)
            gates = g_ref[0] + bias_ref[...]
            sel_i, sel_f = _gate_cols(gates, d, h)
            li = jnp.sum(gates * sel_i, axis=1, keepdims=True)
            lf_pre = jnp.sum(gates * sel_f, axis=1, keepdims=True)
            m_in = jnp.max(ms[0, 0, 0], axis=1, keepdims=True)
            _, vjp = jax.vjp(
                functools.partial(_mlstm_chunk, incl=incl, incl_t=incl_t, eye=eye),
                qk_ref[0, :, :DH], qk_ref[0, :, DH:], v_ref[0], li, lf_pre, cs[0, 0, 0], ns[0, 0, 0], m_in)
            dm_out = jnp.max(dm_st[d, h], axis=1, keepdims=True)
            dq, dk, dv, dli, dlf, dc, dn, dm = vjp((dh_ref[0], dc_st[d, h], dn_st[d, h], dm_out))
            dqk_ref[0, :, :DH] = dq
            dqk_ref[0, :, DH:] = dk
            dv_ref[0] = dv
            dg_ref[0] += dli * sel_i + dlf * sel_f
            dc_st[d, h], dn_st[d, h] = dc, dn
            dm_st[d, h] = jnp.broadcast_to(dm, (1, DH))

    def tok_spec(width, base, d, per_head):
        return pl.BlockSpec((1, L, width), lambda b, c, h: (b, chunk_of(d, c), base + (h if per_head else 0)))

    def st_spec(shape, d):
        return pl.BlockSpec((1, 1, 1) + shape, lambda b, c, h: (b, chunk_of(d, c), h, 0, 0))

    in_specs = [tok_spec(2 * DH, 0, 0, True), tok_spec(2 * DH, 0, 1, True),
                tok_spec(DH, C_VM // DH, 0, True), tok_spec(DH, C_VM // DH, 1, True),
                tok_spec(LANES, C_GATES // LANES, 0, False), tok_spec(LANES, C_GATES // LANES, 1, False),
                pl.BlockSpec((1, LANES), lambda b, c, h: (0, 0)),
                st_spec((DH, DH), 0), st_spec((DH, DH), 1), st_spec((1, DH), 0), st_spec((1, DH), 1),
                st_spec((1, DH), 0), st_spec((1, DH), 1), tok_spec(DH, 0, 0, True), tok_spec(DH, 0, 1, True)]
    out_specs = [tok_spec(2 * DH, 0, 0, True), tok_spec(2 * DH, 0, 1, True), tok_spec(DH, 0, 0, True), tok_spec(DH, 0, 1, True),
                 tok_spec(LANES, 0, 0, False), tok_spec(LANES, 0, 1, False)]
    qks = jax.ShapeDtypeStruct((B, S, 2 * H * DH), F32)
    vs = jax.ShapeDtypeStruct((B, S, H * DH), F32)
    gs = jax.ShapeDtypeStruct((B, S, LANES), F32)
    csf, csb, nsf, nsb, msf, msb = states
    return pl.pallas_call(
        body, name="mlstm_bwd", grid=(B, nc, H), in_specs=in_specs, out_specs=out_specs,
        out_shape=[qks, qks, vs, vs, gs, gs],
        scratch_shapes=[pltpu.VMEM((2, H, DH, DH), F32), pltpu.VMEM((2, H, 1, DH), F32), pltpu.VMEM((2, H, 1, DH), F32)],
        compiler_params=_cparams(("parallel", "arbitrary", "arbitrary")),
    )(qk, qk, proj3, proj3, proj3, proj3, bias, csf, csb, nsf, nsb, msf, msb, dh, dh)


ROW_BLOCK = 256
FF_COLS = 512
FF_SHARD = D_FF // N_DEV
FF_SHARD_PAD = 384
FF_PAD = N_DEV * FF_SHARD_PAD


def _rms_fwd(name, x, g):
    T = x.shape[0]
    return _rowwise(name, lambda xv, gv: _rms(xv, gv), [_In(x), _In(g, rows=False)], [_Out(D_MODEL, BF16)], T, ROW_BLOCK)[0]


def _rms_bwd(name, x, g, dh, dres):
    T = x.shape[0]

    def fn(xv, gv, dhv, drv):
        _, vjp = jax.vjp(_rms, xv, gv)
        dx, dg = vjp(dhv)
        return drv + dx, dg

    return _rowwise(name, fn, [_In(x), _In(g, rows=False), _In(dh), _In(dres)],
                    [_Out(D_MODEL), _Out(D_MODEL, rows=False)], T, ROW_BLOCK)


def _mmw(name, a, w, mode, **kw):
    if isinstance(w, tuple):
        return _matmul(name, a, w[0], mode, bl=w[1], **kw)
    return _matmul(name, a, w, mode, **kw)


def _ffn_fwd(tag, x, g, wg, wu, wd):
    T = x.shape[0]
    h = _rms_fwd(tag + "_norm", x, g)
    gate = _mmw(tag + "_gate", h, wg, "nn")
    up = _mmw(tag + "_up", h, wu, "nn")
    act = _rowwise(tag + "_act", lambda a, b: _silu(a) * b,
                   [_In(gate, FF_COLS, split=True), _In(up, FF_COLS, split=True)],
                   [_Out(FF_PAD, BF16, FF_COLS, split=True)], T, 1024, ncol=FF_PAD // FF_COLS)[0]
    out = _mmw(tag + "_down", act, wd, "nn", res=x, scale=0.5)
    return out, (x, h, gate, up, act)


def _ffn_bwd(tag, saved, g, wg, wu, wd, dx):
    x, h, gate, up, act = saved
    T = x.shape[0]
    dact = _mmw(tag + "_dact", dx, wd, "nt", scale=0.5)
    dwd = _matmul(tag + "_dwd", act, dx, "tn", scale=0.5)

    def fn(a, b, da):
        _, vjp = jax.vjp(lambda p, q: _silu(p) * q, a, b)
        return vjp(da)

    dgate, dup = _rowwise(tag + "_dactfn", fn,
                          [_In(gate, FF_COLS, split=True), _In(up, FF_COLS, split=True), _In(dact, FF_COLS, split=True)],
                          [_Out(FF_PAD, BF16, FF_COLS, split=True), _Out(FF_PAD, BF16, FF_COLS, split=True)],
                          T, 1024, ncol=FF_PAD // FF_COLS)
    dh = _mmw(tag + "_dh1", dgate, wg, "nt")
    dh = _mmw(tag + "_dh2", dup, wu, "nt", res=dh)
    dwg = _matmul(tag + "_dwg", h, dgate, "tn")
    dwu = _matmul(tag + "_dwu", h, dup, "tn")
    dx_new, dg = _rms_bwd(tag + "_dnorm", x, g, dh, dx)
    return dx_new, dg, dwg, dwu, dwd


def _rope_consts():
    half = ROPE_DIM // 2
    inv_freq = jnp.power(jnp.float32(ROPE_THETA), -jnp.arange(half, dtype=F32) * (2.0 / ROPE_DIM))
    row = jnp.zeros((1, ATT_HEAD_DIM), F32).at[0, :ROPE_DIM].set(jnp.concatenate([inv_freq, inv_freq]))
    rot = np.zeros((ATT_HEAD_DIM, ATT_HEAD_DIM), np.float32)
    for i in range(half):
        rot[half + i, i] = -1.0
        rot[i, half + i] = 1.0
    return row, jnp.asarray(rot)


def _prep_fwd(name, t, g, pos, inv_freq_row, rot):
    R = t.shape[0]

    def fn(tv, gv, pv, fv, rv):
        cos, sin = _rope_tables(pv, fv)
        return _qk_prep(tv, gv, cos, sin, rv)

    return _rowwise(name, fn, [_In(t), _In(g, rows=False), _In(pos), _In(inv_freq_row, rows=False), _In(rot, rows=False)],
                    [_Out(ATT_HEAD_DIM)], R, 1024)[0]


def _prep_bwd(name, t, g, pos, inv_freq_row, rot, dout):
    R = t.shape[0]

    def fn(tv, gv, pv, fv, rv, dv):
        cos, sin = _rope_tables(pv, fv)
        _, vjp = jax.vjp(lambda a, b: _qk_prep(a, b, cos, sin, rv), tv, gv)
        return vjp(dv)

    return _rowwise(name, fn, [_In(t), _In(g, rows=False), _In(pos), _In(inv_freq_row, rows=False), _In(rot, rows=False), _In(dout)],
                    [_Out(ATT_HEAD_DIM), _Out(ATT_HEAD_DIM, rows=False)], R, 1024)


def _to_heads(t, B, S, nh):
    return t.reshape(B, S, nh, ATT_HEAD_DIM).transpose(0, 2, 1, 3)


def _from_heads(t):
    B, nh, S, _ = t.shape
    return t.transpose(0, 2, 1, 3).reshape(B * S, nh * ATT_HEAD_DIM)


def _mix_fwd(x, pos_q, pos_k, B, S, p):
    T = B * S
    h = _rms_fwd("mix_norm", x, p["mix_norm"])
    proj = _matmul("mix_proj", h, p["w_in"], "nn")
    proj3 = proj.reshape(B, S, IN_PAD)
    inv_freq_row, rot = _rope_consts()
    qa = proj[:, C_QA:C_QA + ATT_WIDTH].reshape(T * ATT_HEADS, ATT_HEAD_DIM)
    ka = proj[:, C_KA:C_KA + ATT_KV_WIDTH].reshape(T * ATT_KV_HEADS, ATT_HEAD_DIM)
    q_r = _prep_fwd("q_prep", qa, p["attn_q_norm"], pos_q, inv_freq_row, rot)
    k_r = _prep_fwd("k_prep", ka, p["attn_k_norm"], pos_k, inv_freq_row, rot)
    qh = _to_heads(q_r, B, S, ATT_HEADS)
    kh = _to_heads(k_r, B, S, ATT_KV_HEADS)
    vh = _to_heads(proj[:, C_VA:C_VA + ATT_KV_WIDTH], B, S, ATT_KV_HEADS)
    sink = p["attn_sink"].reshape(ATT_KV_HEADS, ATT_GROUP, 1, 1)
    y_a = _from_heads(_attn_fwd(qh, kh, vh, sink))

    qk_c = _conv_fwd(proj3, p["conv_w8"])
    hf, hb, *states = _mlstm_fwd(qk_c, proj3, p["gate_bias"])
    hf2, hb2 = hf.reshape(T, MLSTM_WIDTH), hb.reshape(T, MLSTM_WIDTH)
    DH = MLSTM_HEAD_DIM
    y_m = _rowwise("mlstm_out", _mlstm_combine,
                   [_In(hf2, DH, split=True), _In(hb2, DH, split=True), _In(proj, DH, C_OM // DH, split=True),
                    _In(p["mlstm_out_norm"], DH, split=True, rows=False)],
                   [_Out(MLSTM_WIDTH, BF16, DH, split=True)], T, 1024, ncol=MLSTM_HEADS)[0]

    za = _mmw("branch_a", y_a, p["w_branch_attn"], "nn")
    zm = _mmw("branch_m", y_m, p["w_branch_mlstm"], "nn")
    W = 512
    merged = _rowwise("merge", _merge,
                      [_In(proj, W, C_GMERGE // W, split=True), _In(proj, W, (C_GMERGE + D_MODEL) // W, split=True),
                       _In(za, W, split=True), _In(zm, W, split=True)],
                      [_Out(D_MODEL, BF16, W, split=True)], T, 512, ncol=D_MODEL // W)[0]
    out = _mmw("mix_out", merged, p["w_out"], "nn", res=x)
    saved = dict(x=x, h=h, proj=proj, qa=qa, ka=ka, qh=qh, kh=kh, vh=vh, sink=sink, y_a=y_a, qk_c=qk_c, hf=hf2, hb=hb2,
                 states=states, y_m=y_m, za=za, zm=zm, merged=merged)
    return out, saved


def _mix_bwd(sv, pos_q, pos_k, B, S, p, dx):
    T = B * S
    DH = MLSTM_HEAD_DIM
    proj = sv["proj"]
    proj3 = proj.reshape(B, S, IN_PAD)
    inv_freq_row, rot = _rope_consts()
    g = {}
    dmerged = _mmw("mix_dmerged", dx, p["w_out"], "nt")
    g["w_out"] = _matmul("mix_dwout", sv["merged"], dx, "tn")
    W = 512

    def merge_bwd(ga, gm, za, zm, dm):
        _, vjp = jax.vjp(_merge, ga, gm, za, zm)
        return vjp(dm)

    dga, dgm, dza, dzm = _rowwise(
        "merge_bwd", merge_bwd,
        [_In(proj, W, C_GMERGE // W, split=True), _In(proj, W, (C_GMERGE + D_MODEL) // W, split=True),
         _In(sv["za"], W, split=True), _In(sv["zm"], W, split=True), _In(dmerged, W, split=True)],
        [_Out(D_MODEL, F32, W, split=True), _Out(D_MODEL, F32, W, split=True),
         _Out(D_MODEL, BF16, W, split=True), _Out(D_MODEL, BF16, W, split=True)], T, 512, ncol=D_MODEL // W)
    dya = _mmw("branch_a_dx", dza, p["w_branch_attn"], "nt")
    g["w_branch_attn"] = _matmul("branch_a_dw", sv["y_a"], dza, "tn")
    dym = _mmw("branch_m_dx", dzm, p["w_branch_mlstm"], "nt")
    g["w_branch_mlstm"] = _matmul("branch_m_dw", sv["y_m"], dzm, "tn")

    def combine_bwd(hf, hb, o_pre, gn, dy):
        _, vjp = jax.vjp(_mlstm_combine, hf, hb, o_pre, gn)
        dhf, _, do, dg = vjp(dy)
        return dhf, do, dg

    dh, dom, g["mlstm_out_norm"] = _rowwise(
        "mlstm_out_bwd", combine_bwd,
        [_In(sv["hf"], DH, split=True), _In(sv["hb"], DH, split=True), _In(proj, DH, C_OM // DH, split=True),
         _In(p["mlstm_out_norm"], DH, split=True, rows=False), _In(dym, DH, split=True)],
        [_Out(MLSTM_WIDTH, F32, DH, split=True), _Out(MLSTM_WIDTH, F32, DH, split=True),
         _Out(MLSTM_WIDTH, F32, DH, split=True, rows=False)], T, 1024, ncol=MLSTM_HEADS)
    dqk_f, dqk_b, dv_f, dv_b, dg_f, dg_b = _mlstm_bwd(sv["qk_c"], proj3, p["gate_bias"], sv["states"],
                                                       dh.reshape(B, S, MLSTM_WIDTH))
    dgates, dvm, g["gate_bias"] = _rowwise(
        "mlstm_dsum", lambda a, b, c, d: (a + b, c + d, jnp.sum(a + b, axis=0, keepdims=True)),
        [_In(dg_f.reshape(T, LANES)), _In(dg_b.reshape(T, LANES)), _In(dv_f.reshape(T, MLSTM_WIDTH)), _In(dv_b.reshape(T, MLSTM_WIDTH))],
        [_Out(LANES), _Out(MLSTM_WIDTH), _Out(LANES, rows=False)], T, 1024)
    dqk, g["conv_w8"] = _conv_bwd(proj3, p["conv_w8"], dqk_f, dqk_b)

    dyh = _to_heads(dya, B, S, ATT_HEADS)
    dqh, dkh, dvh, dsink = _attn_bwd(sv["qh"], sv["kh"], sv["vh"], sv["sink"], dyh)
    g["attn_sink"] = dsink.reshape(1, ATT_HEADS)
    dq_r = dqh.transpose(0, 2, 1, 3).reshape(T * ATT_HEADS, ATT_HEAD_DIM)
    dk_r = dkh.transpose(0, 2, 1, 3).reshape(T * ATT_KV_HEADS, ATT_HEAD_DIM)
    dva = _from_heads(dvh)
    dqa, g["attn_q_norm"] = _prep_bwd("q_prep_bwd", sv["qa"], p["attn_q_norm"], pos_q, inv_freq_row, rot, dq_r)
    dka, g["attn_k_norm"] = _prep_bwd("k_prep_bwd", sv["ka"], p["attn_k_norm"], pos_k, inv_freq_row, rot, dk_r)

    dproj = jnp.concatenate(
        [dga.astype(BF16), dgm.astype(BF16), dqk.reshape(T, 2 * MLSTM_WIDTH).astype(BF16), dvm.astype(BF16), dom.astype(BF16),
         dqa.reshape(T, ATT_WIDTH).astype(BF16), dka.reshape(T, ATT_KV_WIDTH).astype(BF16), dva.astype(BF16),
         dgates.astype(BF16)], axis=1)
    dh2 = _matmul("mix_dh", dproj, p["w_in"], "nt")
    g["w_in"] = _matmul("mix_dwin", sv["h"], dproj, "tn")
    dx_new, g["mix_norm"] = _rms_bwd("mix_dnorm", sv["x"], p["mix_norm"], dh2, dx)
    return dx_new, g


def _loss_and_grad(x, g, target):
    T = x.shape[0]

    def loss_fn(xv, gv, tv):
        err = jnp.square(_rms(xv, gv) - tv)
        return 0.5 * jnp.sum(jnp.mean(err, axis=-1, keepdims=True), axis=0, keepdims=True)

    def fn(xv, gv, tv):
        val, vjp = jax.vjp(lambda a, b: loss_fn(a, b, tv), xv, gv)
        dx, dg = vjp(jnp.ones((1, 1), F32))
        return val, dx, dg

    return _rowwise("loss_head", fn, [_In(x), _In(g, rows=False), _In(target)],
                    [_Out(1, rows=False), _Out(D_MODEL), _Out(D_MODEL, rows=False)], T, ROW_BLOCK)


def _block_norm_fwd(x, g):
    T = x.shape[0]
    return _rowwise("block_norm", _rms, [_In(x), _In(g, rows=False)], [_Out(D_MODEL)], T, ROW_BLOCK)[0]


def _block_norm_bwd(x, g, dy):
    T = x.shape[0]

    def fn(xv, gv, dv):
        _, vjp = jax.vjp(_rms, xv, gv)
        return vjp(dv)

    return _rowwise("block_norm_bwd", fn, [_In(x), _In(g, rows=False), _In(dy)],
                    [_Out(D_MODEL), _Out(D_MODEL, rows=False)], T, ROW_BLOCK)


def _qk_perm_cols(t, axis):
    q, k = jnp.split(t, 2, axis=axis)
    parts = []
    for h in range(MLSTM_HEADS):
        sl = [slice(None)] * t.ndim
        sl[axis] = slice(h * MLSTM_HEAD_DIM, (h + 1) * MLSTM_HEAD_DIM)
        parts += [q[tuple(sl)], k[tuple(sl)]]
    return jnp.concatenate(parts, axis=axis)


def _qk_unperm_cols(t, axis):
    qs, ks = [], []
    for h in range(MLSTM_HEADS):
        sl = [slice(None)] * t.ndim
        sl[axis] = slice(2 * h * MLSTM_HEAD_DIM, (2 * h + 1) * MLSTM_HEAD_DIM)
        qs.append(t[tuple(sl)])
        sl[axis] = slice((2 * h + 1) * MLSTM_HEAD_DIM, (2 * h + 2) * MLSTM_HEAD_DIM)
        ks.append(t[tuple(sl)])
    return jnp.concatenate(qs + ks, axis=axis)


def _w_in_arrange(w):
    qa, ka, va, qm, km, vm, om, gm, gmerge = jnp.split(w, np.cumsum(
        (ATT_WIDTH, ATT_KV_WIDTH, ATT_KV_WIDTH, MLSTM_WIDTH, MLSTM_WIDTH, MLSTM_WIDTH, MLSTM_WIDTH, MLSTM_N_GATES))[:].tolist(), axis=1)
    qk = _qk_perm_cols(jnp.concatenate([qm, km], axis=1), 1)
    pad = jnp.zeros((w.shape[0], LANES - MLSTM_N_GATES), w.dtype)
    return jnp.concatenate([gmerge, qk, vm, om, qa, ka, va, gm, pad], axis=1)


def _w_in_restore(w):
    gmerge = w[:, C_GMERGE:C_GMERGE + 2 * D_MODEL]
    qk = _qk_unperm_cols(w[:, C_QK:C_QK + 2 * MLSTM_WIDTH], 1)
    vm, om = w[:, C_VM:C_VM + MLSTM_WIDTH], w[:, C_OM:C_OM + MLSTM_WIDTH]
    qa, ka, va = w[:, C_QA:C_QA + ATT_WIDTH], w[:, C_KA:C_KA + ATT_KV_WIDTH], w[:, C_VA:C_VA + ATT_KV_WIDTH]
    gm = w[:, C_GATES:C_GATES + MLSTM_N_GATES]
    return jnp.concatenate([qa, ka, va, qk, vm, om, gm, gmerge], axis=1)


BIG = ("ffn1_w_gate", "ffn1_w_up", "ffn1_w_down", "w_in", "mlstm_conv_w", "w_branch_attn", "w_branch_mlstm", "w_out",
       "ffn2_w_gate", "ffn2_w_up", "ffn2_w_down")
MATMUL_W = tuple(n for n in BIG if n != "mlstm_conv_w")
SMALL = ("ffn1_norm", "mix_norm", "mlstm_gate_bias", "attn_q_norm", "attn_k_norm", "attn_sink", "mlstm_conv_b",
         "mlstm_out_norm", "ffn2_norm", "block_out_norm")
WEIGHTS = ("ffn1_norm", "ffn1_w_gate", "ffn1_w_up", "ffn1_w_down", "mix_norm", "w_in", "mlstm_gate_bias", "attn_q_norm",
           "attn_k_norm", "attn_sink", "mlstm_conv_w", "mlstm_conv_b", "mlstm_out_norm", "w_branch_attn", "w_branch_mlstm",
           "w_out", "ffn2_norm", "ffn2_w_gate", "ffn2_w_up", "ffn2_w_down", "block_out_norm")
PACK_COLS = 1024


def _padded_rows(n_elems):
    return -(-n_elems // PACK_COLS)


def _pack_flat(arrs, dtype, row_multiple):
    parts = []
    for a in arrs:
        flat = a.reshape(-1).astype(dtype)
        pad = _padded_rows(flat.shape[0]) * PACK_COLS - flat.shape[0]
        parts.append(jnp.pad(flat, (0, pad)) if pad else flat)
    flat = jnp.concatenate(parts)
    rows = flat.shape[0] // PACK_COLS
    extra = (-rows) % row_multiple
    if extra:
        flat = jnp.pad(flat, (0, extra * PACK_COLS))
    return flat.reshape(-1, PACK_COLS)


def _unpack_flat(buf, shapes, lead=()):
    flat = buf.reshape(lead + (-1,))
    out, off = [], 0
    for s in shapes:
        n = int(np.prod(s))
        out.append(flat[..., off:off + n].reshape(lead + tuple(s)))
        off += _padded_rows(n) * PACK_COLS
    return out


class _Lay:
    def __init__(self, shard, axis, width):
        self.shard, self.axis, self.width = shard, axis, width
        self.padded = tuple(width if a == axis else s for a, s in enumerate(shard))
        self.whole = tuple(N_DEV * width if a == axis else s for a, s in enumerate(shard))

    def pad(self, t, lead=0):
        extra = self.width - self.shard[self.axis]
        if not extra:
            return t
        cfg = [(0, 0)] * t.ndim
        cfg[lead + self.axis] = (0, extra)
        return jnp.pad(t, cfg)

    def unpad(self, t, lead=0):
        idx = [slice(None)] * t.ndim
        idx[lead + self.axis] = slice(0, self.shard[self.axis])
        return t[tuple(idx)]


_FF_COL = _Lay((D_MODEL, FF_SHARD), 1, FF_SHARD_PAD)
_FF_ROW = _Lay((FF_SHARD, D_MODEL), 0, FF_SHARD_PAD)
LAYOUTS = {
    "ffn1_w_gate": _FF_COL, "ffn1_w_up": _FF_COL, "ffn1_w_down": _FF_ROW,
    "ffn2_w_gate": _FF_COL, "ffn2_w_up": _FF_COL, "ffn2_w_down": _FF_ROW,
    "w_in": _Lay((D_MODEL, IN_WIDTH // N_DEV), 0, D_MODEL),
    "mlstm_conv_w": _Lay((3, 2 * MLSTM_WIDTH // N_DEV), 1, 2 * MLSTM_WIDTH // N_DEV),
    "w_branch_attn": _Lay((ATT_WIDTH, D_MODEL // N_DEV), 1, D_MODEL // N_DEV),
    "w_branch_mlstm": _Lay((MLSTM_WIDTH, D_MODEL // N_DEV), 1, D_MODEL // N_DEV),
    "w_out": _Lay((D_MODEL // N_DEV, D_MODEL), 0, D_MODEL // N_DEV),
}


def _window(ref, axis, j, width):
    idx = [slice(None)] * len(ref.shape)
    idx[axis] = pl.ds(pl.multiple_of(j * width, width), width)
    return ref.at[tuple(idx)]


ANY = pl.BlockSpec(memory_space=pl.ANY)


def _mesh_pos():
    return lax.axis_index("x"), lax.axis_index("y"), lax.axis_index("c")


def _all_gather(name, shard, vmem=False):
    R, C = shard.shape
    space = pl.BlockSpec(memory_space=pltpu.VMEM) if vmem else ANY

    def body(x_ref, out_ref, send_sems, recv_sems, local_sem):
        x, y, c = _mesh_pos()
        me, sibling = (x, y, c), (x, y, 1 - c)
        chips = [(1 - x, y), (x, 1 - y), (1 - x, 1 - y)]

        def slot(px, py, pc):
            return out_ref.at[4 * px + 2 * py + pc]

        def copy(k, block, to, src=None):
            return pltpu.make_async_remote_copy(
                src_ref=slot(*block) if src is None else src, dst_ref=slot(*block),
                send_sem=send_sems.at[k], recv_sem=recv_sems.at[k], device_id=to, device_id_type=MESH)

        mine = pltpu.make_async_copy(x_ref, slot(*me), local_sem)
        mine.start()
        first = [copy(0, me, sibling, src=x_ref)]
        first += [copy(1 + j, me, (*chip, c), src=x_ref) for j, chip in enumerate(chips)]
        for cp in first:
            cp.start()
        passed = [copy(4 + j, (*chip, c), sibling) for j, chip in enumerate(chips)]
        for j, chip in enumerate(chips):
            copy(1 + j, (*chip, c), me).wait_recv()
            passed[j].start()
        copy(0, sibling, me).wait_recv()
        for j, chip in enumerate(chips):
            copy(4 + j, (*chip, 1 - c), me).wait_recv()
        for cp in first + passed:
            cp.wait_send()
        mine.wait()

    return pl.pallas_call(
        body, name=name, out_shape=jax.ShapeDtypeStruct((N_DEV, R, C), shard.dtype),
        in_specs=[space], out_specs=space,
        scratch_shapes=[pltpu.SemaphoreType.DMA((7,)), pltpu.SemaphoreType.DMA((7,)), pltpu.SemaphoreType.DMA],
    )(shard)


HBM = pl.BlockSpec(memory_space=pltpu.HBM)
SEM = pl.BlockSpec(memory_space=pltpu.SEMAPHORE)
SPLIT_COPY = pltpu.CompilerParams(has_side_effects=pltpu.SideEffectType.DATAFLOW_SIDE_EFFECTING)
N_PEERS = N_DEV - 1


def _peers(x, y, c):
    return [(x, y, 1 - c), (1 - x, y, c), (x, 1 - y, c), (1 - x, 1 - y, c),
            (1 - x, y, 1 - c), (x, 1 - y, 1 - c), (1 - x, 1 - y, 1 - c)]


def _dev_index(pos):
    return 4 * pos[0] + 2 * pos[1] + pos[2]


def _gather_start(name, shards, lays, groups):
    nt, ng = len(shards), len(groups)

    def body(*refs):
        x_refs, land_refs = refs[:nt], refs[nt:2 * nt]
        sems = refs[2 * nt:2 * nt + 2 * ng]
        local_sems = refs[-1]
        pos = _mesh_pos()
        me = _dev_index(pos)
        mine = [pltpu.make_async_copy(x_refs[t], _window(land_refs[t], lays[t].axis, me, lays[t].width), local_sems.at[t])
                for t in range(nt)]
        for cp in mine:
            cp.start()
        for g, tens in enumerate(groups):
            for i, t in enumerate(tens):
                for k, peer in enumerate(_peers(*pos)):
                    pltpu.make_async_remote_copy(
                        src_ref=x_refs[t], dst_ref=_window(land_refs[t], lays[t].axis, me, lays[t].width),
                        send_sem=sems[2 * g].at[N_PEERS * i + k], recv_sem=sems[2 * g + 1].at[N_PEERS * i + k],
                        device_id=peer, device_id_type=MESH).start()
        for cp in mine:
            cp.wait()

    sem_shapes = []
    for tens in groups:
        sem_shapes += [pltpu.SemaphoreType.DMA((N_PEERS * len(tens),))] * 2
    thru = [pltpu.HBM(s.shape, s.dtype) for s in shards] + [pltpu.HBM(lay.whole, s.dtype) for s, lay in zip(shards, lays)]
    args = [pltpu.with_memory_space_constraint(s, pltpu.HBM) for s in shards]
    args += [pltpu.with_memory_space_constraint(lax.empty(lay.whole, s.dtype), pltpu.HBM) for s, lay in zip(shards, lays)]
    res = pl.pallas_call(
        body, name=name, out_shape=tuple(sem_shapes + thru), in_specs=[HBM] * (2 * nt),
        out_specs=tuple([SEM] * (2 * ng) + [HBM] * (2 * nt)),
        input_output_aliases={t: 2 * ng + t for t in range(2 * nt)},
        scratch_shapes=[pltpu.SemaphoreType.DMA((nt,))], compiler_params=SPLIT_COPY,
    )(*args)
    sems = [(res[2 * g], res[2 * g + 1]) for g in range(ng)]
    return sems, list(res[2 * ng:2 * ng + nt]), list(res[2 * ng + nt:])


def _gather_wait(name, sems, shards, lands, lays, after):
    nt = len(shards)
    send_sems, recv_sems = sems

    def body(*refs):
        x_refs, land_refs = refs[:nt], refs[nt:2 * nt]
        send_ref, recv_ref = refs[2 * nt], refs[2 * nt + 1]
        pos = _mesh_pos()
        for t in range(nt):
            for k, peer in enumerate(_peers(*pos)):
                cp = pltpu.make_async_remote_copy(
                    src_ref=x_refs[t], dst_ref=_window(land_refs[t], lays[t].axis, _dev_index(peer), lays[t].width),
                    send_sem=send_ref.at[N_PEERS * t + k], recv_sem=recv_ref.at[N_PEERS * t + k],
                    device_id=peer, device_id_type=MESH)
                cp.wait_send()
                cp.wait_recv()

    thru = [pltpu.HBM(s.shape, s.dtype) for s in shards] + [pltpu.HBM(ld.shape, ld.dtype) for ld in lands]
    res = pl.pallas_call(
        body, name=name, out_shape=tuple(thru), in_specs=[HBM] * (2 * nt) + [SEM, SEM, ANY],
        out_specs=tuple([HBM] * (2 * nt)), input_output_aliases={t: t for t in range(2 * nt)},
        compiler_params=SPLIT_COPY,
    )(*shards, *lands, send_sems, recv_sems, after)
    return list(res[nt:])


def _pair_exchange(name, grads, lays):
    nt = len(grads)

    def body(*refs):
        g_refs, land_refs = refs[:nt], refs[nt:2 * nt]
        send_sems, recv_sems = refs[2 * nt:]
        x, y, c = _mesh_pos()
        copies = []
        for t in range(nt):
            for chip in range(4):
                copies.append(pltpu.make_async_remote_copy(
                    src_ref=_window(g_refs[t], lays[t].axis, 2 * chip + (1 - c), lays[t].width), dst_ref=land_refs[t].at[chip],
                    send_sem=send_sems.at[4 * t + chip], recv_sem=recv_sems.at[4 * t + chip],
                    device_id=(x, y, 1 - c), device_id_type=MESH))
        for cp in copies:
            cp.start()
        for cp in copies:
            cp.wait_recv()
        for cp in copies:
            cp.wait_send()

    out_shape = [jax.ShapeDtypeStruct((4,) + lay.padded, g.dtype) for g, lay in zip(grads, lays)]
    return pl.pallas_call(
        body, name=name, out_shape=out_shape, in_specs=[ANY] * nt, out_specs=[ANY] * nt,
        scratch_shapes=[pltpu.SemaphoreType.DMA((4 * nt,)), pltpu.SemaphoreType.DMA((4 * nt,))],
    )(*grads)


def _pair_sum(name, whole, landed, lay, out_dtype):
    R, C = lay.padded
    br = _first_divisor(R, (512, 384, 256, 128, 64, 32, 16, 8))
    nb = R // br
    if lay.axis == 0:
        mine_spec = pl.BlockSpec((br, C), lambda k, i, c_ref: ((2 * k + c_ref[0]) * nb + i, 0))
    else:
        mine_spec = pl.BlockSpec((br, C), lambda k, i, c_ref: (i, 2 * k + c_ref[0]))

    def body(c_ref, mine_ref, sib_ref, o_ref):
        o_ref[0] = (mine_ref[...] + sib_ref[0]).astype(out_dtype)

    c = lax.axis_index("c")
    return pl.pallas_call(
        body, name=name,
        grid_spec=pltpu.PrefetchScalarGridSpec(
            num_scalar_prefetch=1, grid=(4, nb),
            in_specs=[mine_spec, pl.BlockSpec((1, br, C), lambda k, i, c_ref: (k, i, 0))],
            out_specs=pl.BlockSpec((1, br, C), lambda k, i, c_ref: (k, i, 0))),
        out_shape=jax.ShapeDtypeStruct((4, R, C), out_dtype),
        compiler_params=_cparams(("parallel", "parallel")),
    )(c.reshape(1).astype(jnp.int32), whole, landed)


def _chip_exchange(name, sums):
    nt = len(sums)

    def body(*refs):
        s_refs, land_refs = refs[:nt], refs[nt:2 * nt]
        send_sems, recv_sems, local_sems = refs[2 * nt:]
        x, y, c = _mesh_pos()
        my_chip = 2 * x + y
        mine = [pltpu.make_async_copy(s_refs[t].at[my_chip], land_refs[t].at[my_chip], local_sems.at[t]) for t in range(nt)]
        for cp in mine:
            cp.start()
        chips = [(1 - x, y), (x, 1 - y), (1 - x, 1 - y)]
        copies = []
        for t in range(nt):
            for j, (px, py) in enumerate(chips):
                copies.append(pltpu.make_async_remote_copy(
                    src_ref=s_refs[t].at[2 * px + py], dst_ref=land_refs[t].at[my_chip],
                    send_sem=send_sems.at[3 * t + j], recv_sem=recv_sems.at[3 * t + j],
                    device_id=(px, py, c), device_id_type=MESH))
        for cp in copies:
            cp.start()
        for t in range(nt):
            for j, (px, py) in enumerate(chips):
                pltpu.make_async_remote_copy(
                    src_ref=s_refs[t].at[my_chip], dst_ref=land_refs[t].at[2 * px + py],
                    send_sem=send_sems.at[3 * t + j], recv_sem=recv_sems.at[3 * t + j],
                    device_id=(px, py, c), device_id_type=MESH).wait_recv()
        for cp in copies:
            cp.wait_send()
        for cp in mine:
            cp.wait()

    return pl.pallas_call(
        body, name=name, out_shape=[jax.ShapeDtypeStruct(s.shape, s.dtype) for s in sums],
        in_specs=[ANY] * nt, out_specs=[ANY] * nt,
        scratch_shapes=[pltpu.SemaphoreType.DMA((3 * nt,)), pltpu.SemaphoreType.DMA((3 * nt,)), pltpu.SemaphoreType.DMA((nt,))],
    )(*sums)


def _chip_start(name, sums):
    nt = len(sums)

    def body(*refs):
        s_refs, land_refs = refs[:nt], refs[nt:2 * nt]
        send_sems, recv_sems = refs[2 * nt], refs[2 * nt + 1]
        local_sems = refs[-1]
        x, y, c = _mesh_pos()
        my_chip = 2 * x + y
        mine = [pltpu.make_async_copy(s_refs[t].at[my_chip], land_refs[t].at[my_chip], local_sems.at[t]) for t in range(nt)]
        for cp in mine:
            cp.start()
        for t in range(nt):
            for j, (px, py) in enumerate([(1 - x, y), (x, 1 - y), (1 - x, 1 - y)]):
                pltpu.make_async_remote_copy(
                    src_ref=s_refs[t].at[2 * px + py], dst_ref=land_refs[t].at[my_chip],
                    send_sem=send_sems.at[3 * t + j], recv_sem=recv_sems.at[3 * t + j],
                    device_id=(px, py, c), device_id_type=MESH).start()
        for cp in mine:
            cp.wait()

    thru = [pltpu.HBM(s.shape, s.dtype) for s in sums] * 2
    args = [pltpu.with_memory_space_constraint(s, pltpu.HBM) for s in sums]
    args += [pltpu.with_memory_space_constraint(lax.empty(s.shape, s.dtype), pltpu.HBM) for s in sums]
    res = pl.pallas_call(
        body, name=name, out_shape=tuple([pltpu.SemaphoreType.DMA((3 * nt,))] * 2 + thru), in_specs=[HBM] * (2 * nt),
        out_specs=tuple([SEM, SEM] + [HBM] * (2 * nt)), input_output_aliases={t: 2 + t for t in range(2 * nt)},
        scratch_shapes=[pltpu.SemaphoreType.DMA((nt,))], compiler_params=SPLIT_COPY,
    )(*args)
    return (res[0], res[1]), list(res[2:2 + nt]), list(res[2 + nt:])


def _chip_wait(name, sems, sums, lands, after):
    nt = len(sums)

    def body(*refs):
        s_refs, land_refs = refs[:nt], refs[nt:2 * nt]
        send_sems, recv_sems = refs[2 * nt], refs[2 * nt + 1]
        x, y, c = _mesh_pos()
        my_chip = 2 * x + y
        for t in range(nt):
            for j, (px, py) in enumerate([(1 - x, y), (x, 1 - y), (1 - x, 1 - y)]):
                cp = pltpu.make_async_remote_copy(
                    src_ref=s_refs[t].at[my_chip], dst_ref=land_refs[t].at[2 * px + py],
                    send_sem=send_sems.at[3 * t + j], recv_sem=recv_sems.at[3 * t + j],
                    device_id=(px, py, c), device_id_type=MESH)
                cp.wait_send()
                cp.wait_recv()

    thru = [pltpu.HBM(s.shape, s.dtype) for s in sums] * 2
    res = pl.pallas_call(
        body, name=name, out_shape=tuple(thru), in_specs=[HBM] * (2 * nt) + [SEM, SEM, ANY],
        out_specs=tuple([HBM] * (2 * nt)), input_output_aliases={t: t for t in range(2 * nt)},
        compiler_params=SPLIT_COPY,
    )(*sums, *lands, sems[0], sems[1], after)
    return list(res[nt:])


def _sum_slots(name, slots, n):
    _, R, C = slots.shape
    br = _first_divisor(R, (512, 384, 256, 128, 64, 32, 16, 8))

    def body(s_ref, o_ref):
        acc = s_ref[0].astype(F32)
        for k in range(1, n):
            acc = acc + s_ref[k].astype(F32)
        o_ref[...] = acc

    return pl.pallas_call(
        body, name=name, grid=(R // br,), in_specs=[pl.BlockSpec((n, br, C), lambda i: (0, i, 0))],
        out_specs=pl.BlockSpec((br, C), lambda i: (i, 0)), out_shape=jax.ShapeDtypeStruct((R, C), F32),
        compiler_params=_cparams(("parallel",)),
    )(slots)


def _reduce_scatter_start(tag, names, grads):
    lays = [LAYOUTS[n] for n in names]
    landed = _pair_exchange("grads_pair_" + names[0], grads, lays)
    sums = [_pair_sum("grads_pairsum_" + n, g, ld, lay, BF16) for n, g, ld, lay in zip(names, grads, landed, lays)]
    sems, sums, lands = _chip_start(tag + "_chips_start", sums)
    return tag, names, sems, sums, lands


def _reduce_scatter_finish(pending, after):
    tag, names, sems, sums, lands = pending
    got = _chip_wait(tag + "_chips_wait", sems, sums, lands, after)
    return [_sum_slots("grads_sum_" + n, s, 4) for n, s in zip(names, got)]


def _adamw_math(w, g, m, v):
    m = ADAM_B1 * m + (1.0 - ADAM_B1) * g
    v = ADAM_B2 * v + (1.0 - ADAM_B2) * jnp.square(g)
    m_hat = m / (1.0 - ADAM_B1 ** ADAM_STEP)
    v_hat = v / (1.0 - ADAM_B2 ** ADAM_STEP)
    delta = -ADAM_LR * (m_hat / (jnp.sqrt(v_hat) + ADAM_EPS) + ADAM_WD * w)
    return delta, m, v


def _adamw(name, w, g, m, v):
    shape = w.shape
    cols = shape[-1]
    rows = int(np.prod(shape[:-1]))
    br = _first_divisor(rows, (512, 352, 256, 128, 64, 32, 16, 8))
    args = [_In(a.reshape(rows, cols)) for a in (w, g, m, v)]
    outs = _rowwise(name, _adamw_math, args, [_Out(cols), _Out(cols), _Out(cols)], rows, br)
    return [o.reshape(shape) for o in outs]


GROUPS = {"ffn1": ("ffn1_w_gate", "ffn1_w_up", "ffn1_w_down"),
          "mix": ("w_in", "w_branch_attn", "w_branch_mlstm", "w_out"),
          "ffn2": ("ffn2_w_gate", "ffn2_w_up", "ffn2_w_down")}


def _small_params(small, conv_w, l):
    p = {}
    for n in ("ffn1_norm", "mix_norm", "ffn2_norm", "block_out_norm", "mlstm_out_norm", "attn_q_norm", "attn_k_norm"):
        p[n] = small[n][l][None, :]
    p["attn_sink"] = small["attn_sink"][l]
    p["gate_bias"] = jnp.pad(small["mlstm_gate_bias"][l], (0, LANES - MLSTM_N_GATES))[None, :]
    taps = _qk_perm_cols(conv_w[l], 1)
    conv_b = _qk_perm_cols(small["mlstm_conv_b"][l][None, :], 1)
    p["conv_w8"] = jnp.concatenate([taps, conv_b, jnp.zeros((4, 2 * MLSTM_WIDTH), F32)], axis=0)
    return p


def _w_in_from_slots(slots):
    w_in = slots.reshape(N_DEV, D_MODEL, IN_WIDTH // N_DEV).transpose(1, 0, 2).reshape(D_MODEL, IN_WIDTH)
    return _w_in_arrange(w_in)


def _w_in_to_slots(g):
    return _w_in_restore(g).reshape(D_MODEL, N_DEV, IN_WIDTH // N_DEV).transpose(1, 0, 2).reshape(
        N_DEV * D_MODEL, IN_WIDTH // N_DEV)


def _local_step(x, positions, target, weights_of, small, conv_w, on_grads):
    B, S, _ = x.shape
    T = B * S
    pos = positions.reshape(T, 1)
    pos_q = jnp.repeat(pos, ATT_HEADS, axis=0)
    pos_k = jnp.repeat(pos, ATT_KV_HEADS, axis=0)
    params = [_small_params(small, conv_w, l) for l in range(DEPTH)]
    xs = x.reshape(T, D_MODEL)
    tgt = target.reshape(T, D_MODEL)

    saved = []
    for l, p in enumerate(params):
        p.update(weights_of(l, "ffn1", xs))
        x1, s1 = _ffn_fwd("ffn1", xs, p["ffn1_norm"], p["ffn1_w_gate"], p["ffn1_w_up"], p["ffn1_w_down"])
        p.update(weights_of(l, "mix", x1))
        p["w_in"] = _w_in_from_slots(p["w_in"])
        x2, s2 = _mix_fwd(x1, pos_q, pos_k, B, S, p)
        p.update(weights_of(l, "ffn2", x2))
        x3, s3 = _ffn_fwd("ffn2", x2, p["ffn2_norm"], p["ffn2_w_gate"], p["ffn2_w_up"], p["ffn2_w_down"])
        saved.append((s1, s2, s3, x3))
        if l + 1 < DEPTH:
            xs = _block_norm_fwd(x3, p["block_out_norm"])

    sm = {n: [None] * DEPTH for n in SMALL + ("mlstm_conv_w",)}
    loss = None
    dx = None
    for l in reversed(range(DEPTH)):
        p = params[l]
        s1, s2, s3, x3 = saved[l]
        if l == DEPTH - 1:
            loss, dx, dgn = _loss_and_grad(x3, p["block_out_norm"], tgt)
        else:
            dx, dgn = _block_norm_bwd(x3, p["block_out_norm"], dx)
        sm["block_out_norm"][l] = dgn[0]
        dx, dg, dwg, dwu, dwd = _ffn_bwd("ffn2", s3, p["ffn2_norm"], p["ffn2_w_gate"], p["ffn2_w_up"], p["ffn2_w_down"], dx)
        sm["ffn2_norm"][l] = dg[0]
        on_grads(l, "ffn2", {"ffn2_w_gate": dwg, "ffn2_w_up": dwu, "ffn2_w_down": dwd}, dx)
        dx, g = _mix_bwd(s2, pos_q, pos_k, B, S, p, dx)
        on_grads(l, "mix", {"w_in": _w_in_to_slots(g["w_in"]), "w_branch_attn": g["w_branch_attn"],
                            "w_branch_mlstm": g["w_branch_mlstm"], "w_out": g["w_out"]}, dx)
        dconv = _qk_unperm_cols(g["conv_w8"], 1)
        sm["mlstm_conv_w"][l] = dconv[0:3]
        sm["mlstm_conv_b"][l] = dconv[3]
        sm["mix_norm"][l] = g["mix_norm"][0]
        sm["mlstm_gate_bias"][l] = g["gate_bias"][0, :MLSTM_N_GATES]
        sm["attn_q_norm"][l], sm["attn_k_norm"][l] = g["attn_q_norm"][0], g["attn_k_norm"][0]
        sm["attn_sink"][l] = g["attn_sink"][0]
        sm["mlstm_out_norm"][l] = g["mlstm_out_norm"][0]
        dx, dg, dwg, dwu, dwd = _ffn_bwd("ffn1", s1, p["ffn1_norm"], p["ffn1_w_gate"], p["ffn1_w_up"], p["ffn1_w_down"], dx)
        sm["ffn1_norm"][l] = dg[0]
        on_grads(l, "ffn1", {"ffn1_w_gate": dwg, "ffn1_w_up": dwu, "ffn1_w_down": dwd}, dx)
    sm = {n: jnp.stack(v, axis=0) for n, v in sm.items()}
    return loss, dx.reshape(B, S, D_MODEL), sm


def kernel(x, positions, ffn1_norm, ffn1_w_gate, ffn1_w_up, ffn1_w_down, mix_norm, w_in, mlstm_gate_bias, attn_q_norm, attn_k_norm, attn_sink, mlstm_conv_w, mlstm_conv_b, mlstm_out_norm, w_branch_attn, w_branch_mlstm, w_out, ffn2_norm, ffn2_w_gate, ffn2_w_up, ffn2_w_down, block_out_norm, loss_target, m_ffn1_norm, m_ffn1_w_gate, m_ffn1_w_up, m_ffn1_w_down, m_mix_norm, m_w_in, m_mlstm_gate_bias, m_attn_q_norm, m_attn_k_norm, m_attn_sink, m_mlstm_conv_w, m_mlstm_conv_b, m_mlstm_out_norm, m_w_branch_attn, m_w_branch_mlstm, m_w_out, m_ffn2_norm, m_ffn2_w_gate, m_ffn2_w_up, m_ffn2_w_down, m_block_out_norm, v_ffn1_norm, v_ffn1_w_gate, v_ffn1_w_up, v_ffn1_w_down, v_mix_norm, v_w_in, v_mlstm_gate_bias, v_attn_q_norm, v_attn_k_norm, v_attn_sink, v_mlstm_conv_w, v_mlstm_conv_b, v_mlstm_out_norm, v_w_branch_attn, v_w_branch_mlstm, v_w_out, v_ffn2_norm, v_ffn2_w_gate, v_ffn2_w_up, v_ffn2_w_down, v_block_out_norm):
    args = locals()
    w = {n: args[n] for n in WEIGHTS}
    m = {n: args["m_" + n] for n in WEIGHTS}
    v = {n: args["v_" + n] for n in WEIGHTS}

    order = [(l, grp) for l in range(DEPTH) for grp in ("ffn1", "mix", "ffn2")]
    keys = [(l, n) for l, grp in order for n in GROUPS[grp]]
    lays = [LAYOUTS[n] for _, n in keys]
    shards = [lay.pad(w[n][l].astype(BF16)) for (l, n), lay in zip(keys, lays)]
    group_idx, at = {}, 0
    for l, grp in order:
        group_idx[(l, grp)] = list(range(at, at + len(GROUPS[grp])))
        at += len(GROUPS[grp])
    sems, shards, lands = _gather_start("weights_gather_start", shards, lays, [group_idx[k] for k in order])

    def weights_of(l, grp, after):
        idx = group_idx[(l, grp)]
        whole = _gather_wait(f"weights_gather_wait_{l}_{grp}", sems[order.index((l, grp))], [shards[i] for i in idx],
                             [lands[i] for i in idx], [lays[i] for i in idx], after)
        return dict(zip(GROUPS[grp], whole))

    conv_shape = w["mlstm_conv_w"].shape
    conv_all = _all_gather("conv_all_gather", _pack_flat([w["mlstm_conv_w"]], F32, 8), vmem=True)
    conv_parts = _unpack_flat(conv_all, [conv_shape], lead=(N_DEV,))[0]
    conv_w = jnp.concatenate([conv_parts[j] for j in range(N_DEV)], axis=2)
    small = {n: w[n] for n in SMALL}

    totals, pending = {}, []

    def finish(after):
        tag, names = pending[0][0], pending[0][1]
        for n, t in zip(names, _reduce_scatter_finish(pending.pop(0), after)):
            totals[(tag, n)] = t

    def on_grads(l, grp, g, after):
        if pending:
            finish(after)
        names = GROUPS[grp]
        pending.append(_reduce_scatter_start(f"grads_{l}_{grp}", names, [g[n] for n in names]))

    loss, grad_x, small_g = _local_step(x, positions, loss_target, weights_of, small, conv_w, on_grads)
    finish(pending[0][3][0])
    grads = {}
    for grp, names in GROUPS.items():
        for n in names:
            grads[n] = jnp.stack([LAYOUTS[n].unpad(totals[(f"grads_{l}_{grp}", n)]) for l in range(DEPTH)], axis=0)

    small_names = SMALL + ("mlstm_conv_w",)
    small_shapes = [small_g[n].shape for n in small_names] + [(1, 1)]
    small_packed = _pack_flat([small_g[n] for n in small_names] + [loss], F32, 8)
    small_all = _all_gather("small_all_gather", small_packed, vmem=True)
    small_sum = _sum_slots("small_sum", small_all, N_DEV)
    *small_grads, loss_total = _unpack_flat(small_sum, small_shapes)
    grads.update(dict(zip(small_names, small_grads)))
    x_pos, y_pos, c_pos = _mesh_pos()
    grads["mlstm_conv_w"] = lax.dynamic_slice_in_dim(
        grads["mlstm_conv_w"], (4 * x_pos + 2 * y_pos + c_pos) * conv_shape[2], conv_shape[2], axis=2)

    deltas, new_m, new_v = {}, {}, {}
    for n in BIG:
        deltas[n], new_m[n], new_v[n] = _adamw("adamw_" + n, w[n], grads[n], m[n], v[n])
    sw, sg, smm, sv = (_pack_flat([d[n] for n in SMALL], F32, 8) for d in (w, grads, m, v))
    sd, snm, snv = _adamw("adamw_small", sw, sg, smm, sv)
    shapes = [w[n].shape for n in SMALL]
    for d, buf in ((deltas, sd), (new_m, snm), (new_v, snv)):
        d.update(dict(zip(SMALL, _unpack_flat(buf, shapes))))

    return (loss_total.reshape(()), grad_x, *[grads[n] for n in WEIGHTS], *[deltas[n] for n in WEIGHTS],
            *[new_m[n] for n in WEIGHTS], *[new_v[n] for n in WEIGHTS])
```

```python
import functools

import numpy as np
import jax
import jax.numpy as jnp
from jax import lax
from jax.experimental import pallas as pl
from jax.experimental.pallas import tpu as pltpu

F32 = jnp.float32
BF16 = jnp.bfloat16

D_MODEL = 1024
D_FF = 2816
ATT_HEAD_DIM = 64
ATT_HEADS = 8
ATT_KV_HEADS = 2
ATT_GROUP = ATT_HEADS // ATT_KV_HEADS
ATT_WIDTH = ATT_HEADS * ATT_HEAD_DIM
ATT_KV_WIDTH = ATT_KV_HEADS * ATT_HEAD_DIM
WINDOW = 128
ATT_BLOCK = 128
ROPE_DIM = 16
ROPE_THETA = 500000.0
MLSTM_HEADS = 4
MLSTM_HEAD_DIM = 128
MLSTM_WIDTH = MLSTM_HEADS * MLSTM_HEAD_DIM
MLSTM_CHUNK = 128
MLSTM_N_GATES = 4 * MLSTM_HEADS
NORM_EPS = 1e-6
IN_WIDTH = 4880
DEPTH = 2
N_DEV = 8

ADAM_LR = 0.001
ADAM_B1 = 0.9
ADAM_B2 = 0.999
ADAM_EPS = 1e-08
ADAM_WD = 0.01
ADAM_STEP = 10

LANES = 128
C_GMERGE = 0
C_QK = 2048
C_VM = 3072
C_OM = 3584
C_QA = 4096
C_KA = 4608
C_VA = 4736
C_GATES = 4864
IN_PAD = 4992

VMEM_LIMIT = 48 * 1024 * 1024

MESH = pl.DeviceIdType.MESH


def _cparams(sem):
    return pltpu.CompilerParams(dimension_semantics=sem, vmem_limit_bytes=VMEM_LIMIT)


def _first_divisor(n, cands):
    for c in cands:
        if n % c == 0:
            return c
    return n


_NN = ((1,), (0,))
_NT = ((1,), (1,))
_TN = ((0,), (0,))


def _mm(a, b, dims):
    return lax.dot_general(a.astype(BF16), b.astype(BF16), (dims, ((), ())), preferred_element_type=F32)


@jax.custom_vjp
def mm_nn(a, b):
    return _mm(a, b, _NN)


def _mm_nn_fwd(a, b):
    return _mm(a, b, _NN), (a, b)


def _mm_nn_bwd(res, g):
    a, b = res
    return _mm(g, b, _NT).astype(a.dtype), _mm(a, g, _TN).astype(b.dtype)


mm_nn.defvjp(_mm_nn_fwd, _mm_nn_bwd)


@jax.custom_vjp
def mm_nt(a, b):
    return _mm(a, b, _NT)


def _mm_nt_fwd(a, b):
    return _mm(a, b, _NT), (a, b)


def _mm_nt_bwd(res, g):
    a, b = res
    return _mm(g, b, _NN).astype(a.dtype), _mm(g, a, _TN).astype(b.dtype)


mm_nt.defvjp(_mm_nt_fwd, _mm_nt_bwd)


@jax.custom_vjp
def mm_tn(a, b):
    return _mm(a, b, _TN)


def _mm_tn_fwd(a, b):
    return _mm(a, b, _TN), (a, b)


def _mm_tn_bwd(res, g):
    a, b = res
    return _mm(b, g, _NT).astype(a.dtype), _mm(a, g, _NN).astype(b.dtype)


mm_tn.defvjp(_mm_tn_fwd, _mm_tn_bwd)


def _matmul(name, a, b, mode, out_dtype=F32, res=None, scale=1.0, bl=None):
    b_shape = b.shape if bl is None else b.shape[1:]
    if mode == "nn":
        (M, K), (K2, N) = a.shape, b_shape
    elif mode == "nt":
        (M, K), (N, K2) = a.shape, b_shape
    else:
        (K, M), (K2, N) = a.shape, b_shape
    assert K == K2, (name, a.shape, b.shape)
    tm = _first_divisor(M, (1024, 1408, 512, 384, 256, 128))
    tn = _first_divisor(N, (1024, 512, 384, 256, 128))
    tk = _first_divisor(K, (1024, 1408, 1664, 512, 256, 128))
    nk = K // tk
    if mode == "tn":
        a_spec = pl.BlockSpec((tk, tm), lambda i, j, k: (k, i))
    else:
        a_spec = pl.BlockSpec((tm, tk), lambda i, j, k: (i, k))
    if mode == "nt":
        b_blk, b_idx = (tn, tk), (lambda i, j, k: (j, k))
    else:
        b_blk, b_idx = (tk, tn), (lambda i, j, k: (k, j))
    if bl is None:
        b_spec = pl.BlockSpec(b_blk, b_idx)
    else:
        b_spec = pl.BlockSpec((None,) + b_blk, lambda i, j, k: (bl,) + b_idx(i, j, k))
    o_spec = pl.BlockSpec((tm, tn), lambda i, j, k: (i, j))
    dims = {"nn": _NN, "nt": _NT, "tn": _TN}[mode]
    has_res = res is not None

    def body(*refs):
        if has_res:
            a_ref, b_ref, r_ref, o_ref, acc = refs
        else:
            a_ref, b_ref, o_ref, acc = refs
        k = pl.program_id(2)

        @pl.when(k == 0)
        def _():
            acc[...] = jnp.zeros_like(acc)

        acc[...] += _mm(a_ref[...], b_ref[...], dims)

        @pl.when(k == nk - 1)
        def _():
            out = acc[...]
            if scale != 1.0:
                out = out * scale
            if has_res:
                out = r_ref[...].astype(F32) + out
            o_ref[...] = out.astype(out_dtype)

    in_specs = [a_spec, b_spec] + ([o_spec] if has_res else [])
    args = (a, b) + ((res,) if has_res else ())
    return pl.pallas_call(
        body, name=name, grid=(M // tm, N // tn, nk), in_specs=in_specs, out_specs=o_spec,
        out_shape=jax.ShapeDtypeStruct((M, N), out_dtype), scratch_shapes=[pltpu.VMEM((tm, tn), F32)],
        compiler_params=_cparams(("parallel", "parallel", "arbitrary")),
    )(*args)


class _In:
    def __init__(self, arr, width=None, base=0, split=False, rows=True):
        self.arr, self.base, self.split, self.rows = arr, base, split, rows
        self.width = arr.shape[1] if width is None else width


class _Out:
    def __init__(self, cols, dtype=F32, width=None, split=False, rows=True, nrows=1):
        self.cols, self.dtype, self.split, self.rows, self.nrows = cols, dtype, split, rows, nrows
        self.width = cols if width is None else width


def _rowwise(name, fn, ins, outs, n_rows, br, ncol=1):
    br = min(br, n_rows)
    assert n_rows % br == 0, (name, n_rows, br)
    nrow_blocks = n_rows // br

    def in_spec(d):
        nb = br if d.rows else d.arr.shape[0]
        if d.rows and d.split:
            im = lambda j, i, base=d.base: (i, base + j)
        elif d.rows:
            im = lambda j, i, base=d.base: (i, base)
        elif d.split:
            im = lambda j, i, base=d.base: (0, base + j)
        else:
            im = lambda j, i, base=d.base: (0, base)
        return pl.BlockSpec((nb, d.width), im)

    def out_spec(d):
        nb = br if d.rows else d.nrows
        if d.rows and d.split:
            im = lambda j, i: (i, j)
        elif d.rows:
            im = lambda j, i: (i, 0)
        elif d.split:
            im = lambda j, i: (0, j)
        else:
            im = lambda j, i: (0, 0)
        return pl.BlockSpec((nb, d.width), im)

    n_in = len(ins)

    def body(*refs):
        i = pl.program_id(1)
        vals = [r[...] for r in refs[:n_in]]
        res = fn(*vals)
        if not isinstance(res, (tuple, list)):
            res = (res,)
        for d, ref, val in zip(outs, refs[n_in:], res):
            if d.rows:
                ref[...] = val.astype(d.dtype)
            else:
                @pl.when(i == 0)
                def _(ref=ref):
                    ref[...] = jnp.zeros_like(ref)

                ref[...] += val.astype(d.dtype)

    out_shape = [jax.ShapeDtypeStruct((n_rows if d.rows else d.nrows, d.cols), d.dtype) for d in outs]
    res = pl.pallas_call(
        body, name=name, grid=(ncol, nrow_blocks), in_specs=[in_spec(d) for d in ins],
        out_specs=[out_spec(d) for d in outs], out_shape=out_shape,
        compiler_params=_cparams(("parallel", "arbitrary")),
    )(*[d.arr for d in ins])
    return res


def _rms(x, g):
    return x * lax.rsqrt(jnp.mean(x * x, axis=-1, keepdims=True) + NORM_EPS) * g


def _sigmoid(x):
    return 1.0 / (1.0 + jnp.exp(-x))


def _silu(x):
    return x * _sigmoid(x)


def _log_sigmoid(x):
    return jnp.minimum(x, 0.0) - jnp.log(1.0 + jnp.exp(-jnp.abs(x)))


def _rope_tables(pos, inv_freq_row):
    ang = pos.astype(F32) * inv_freq_row
    return jnp.cos(ang), jnp.sin(ang)


def _qk_prep(t, g, cos, sin, rot_mat):
    y = _rms(t, g)
    rot = lax.dot_general(y, rot_mat, (_NN, ((), ())), precision=lax.Precision.HIGHEST, preferred_element_type=F32)
    return y * cos + rot * sin


def _attn_head(q, kb, vb, sink, valid):
    s = mm_nt(q, kb) * (ATT_HEAD_DIM ** -0.5)
    s = jnp.where(valid, s, -jnp.inf)
    m = jnp.maximum(jnp.max(s, axis=-1, keepdims=True), sink)
    p = jnp.exp(s - m)
    den = jnp.sum(p, axis=-1, keepdims=True) + jnp.exp(sink - m)
    return mm_nn(p / den, vb)


def _mlstm_chunk(q, k, v, li, lf_pre, C, n, m, incl, incl_t, eye):
    k = k * (MLSTM_HEAD_DIM ** -0.5)
    lf = _log_sigmoid(lf_pre)
    lf_row = jnp.sum(eye * lf, axis=0, keepdims=True)
    li_row = jnp.sum(eye * li, axis=0, keepdims=True)
    b = jnp.sum(incl * lf_row, axis=1, keepdims=True)
    b_row = jnp.sum(incl_t * lf, axis=0, keepdims=True)
    b_tot = jnp.sum(lf, axis=0, keepdims=True)
    a = b_tot - b + li
    a_max = jnp.max(a, axis=0, keepdims=True)
    kw = k * jnp.exp(a - a_max)
    c_loc = mm_tn(kw, v)
    n_loc = jnp.sum(kw, axis=0, keepdims=True)

    dmat = jnp.where(incl > 0.5, b - b_row + li_row, -jnp.inf)
    inter = b + m
    m_t = jnp.maximum(inter, jnp.max(dmat, axis=1, keepdims=True))
    sc = mm_nt(q, k) * jnp.exp(dmat - m_t)
    scale_in = jnp.exp(inter - m_t)
    num = mm_nn(sc, v) + scale_in * mm_nn(q, C)
    den = jnp.sum(sc, axis=1, keepdims=True) + scale_in * jnp.sum(q * n, axis=1, keepdims=True)
    h = num / jnp.maximum(jnp.abs(den), jnp.exp(-m_t))

    m_new = jnp.maximum(b_tot + m, a_max)
    s_p = jnp.exp(b_tot + m - m_new)
    s_l = jnp.exp(a_max - m_new)
    return h, s_p * C + s_l * c_loc, s_p * n + s_l * n_loc, m_new


def _mlstm_combine(hf, hb, o_pre, g):
    h = hf + hb
    mu = jnp.mean(h, axis=-1, keepdims=True)
    var = jnp.mean(jnp.square(h - mu), axis=-1, keepdims=True)
    return _sigmoid(o_pre) * ((h - mu) * lax.rsqrt(var + NORM_EPS) * g)


def _merge(ga, gm, za, zm):
    return _sigmoid(ga) * za + _sigmoid(gm) * zm


def _attn_mask(n, seq):
    qi = n * ATT_BLOCK + lax.broadcasted_iota(jnp.int32, (ATT_BLOCK, 3 * ATT_BLOCK), 0)
    kj = (n - 1) * ATT_BLOCK + lax.broadcasted_iota(jnp.int32, (ATT_BLOCK, 3 * ATT_BLOCK), 1)
    return (jnp.abs(qi - kj) <= WINDOW) & (kj >= 0) & (kj < seq)


def _attn_specs(nq):
    q_spec = pl.BlockSpec((1, ATT_GROUP, ATT_BLOCK, ATT_HEAD_DIM), lambda h, b, n: (b, h, n, 0))

    def kv_spec(off):
        return pl.BlockSpec((1, 1, ATT_BLOCK, ATT_HEAD_DIM),
                            lambda h, b, n: (b, h, jnp.clip(n + off, 0, nq - 1), 0))

    sink_spec = pl.BlockSpec((1, ATT_GROUP, 1, 1), lambda h, b, n: (h, 0, 0, 0))
    return q_spec, kv_spec, sink_spec


def _attn_fwd(q, k, v, sink):
    B, _, S, _ = q.shape
    nq = S // ATT_BLOCK
    q_spec, kv_spec, sink_spec = _attn_specs(nq)

    def body(q_ref, kp, kc, kn, vp, vc, vn, s_ref, o_ref):
        valid = _attn_mask(pl.program_id(2), S)
        kb = jnp.concatenate([kp[0, 0], kc[0, 0], kn[0, 0]], axis=0)
        vb = jnp.concatenate([vp[0, 0], vc[0, 0], vn[0, 0]], axis=0)
        for g in range(ATT_GROUP):
            o_ref[0, g] = _attn_head(q_ref[0, g], kb, vb, s_ref[0, g], valid).astype(BF16)

    return pl.pallas_call(
        body, name="attn_fwd", grid=(ATT_KV_HEADS, B, nq),
        in_specs=[q_spec, kv_spec(-1), kv_spec(0), kv_spec(1), kv_spec(-1), kv_spec(0), kv_spec(1), sink_spec],
        out_specs=q_spec, out_shape=jax.ShapeDtypeStruct(q.shape, BF16),
        compiler_params=_cparams(("parallel", "parallel", "arbitrary")),
    )(q, k, k, k, v, v, v, sink)


def _attn_bwd(q, k, v, sink, dy):
    B, _, S, _ = q.shape
    nq = S // ATT_BLOCK
    q_spec, kv_spec, sink_spec = _attn_specs(nq)
    kv_full = pl.BlockSpec((1, 1, S, ATT_HEAD_DIM), lambda h, b, n: (b, h, 0, 0))

    def body(q_ref, kp, kc, kn, vp, vc, vn, s_ref, dy_ref, dq_ref, dk_ref, dv_ref, ds_ref):
        b, n = pl.program_id(1), pl.program_id(2)
        valid = _attn_mask(n, S)
        kb = jnp.concatenate([kp[0, 0], kc[0, 0], kn[0, 0]], axis=0)
        vb = jnp.concatenate([vp[0, 0], vc[0, 0], vn[0, 0]], axis=0)

        @pl.when(n == 0)
        def _():
            dk_ref[...] = jnp.zeros_like(dk_ref)
            dv_ref[...] = jnp.zeros_like(dv_ref)

        @pl.when((n == 0) & (b == 0))
        def _():
            ds_ref[...] = jnp.zeros_like(ds_ref)

        dkb = jnp.zeros_like(kb)
        dvb = jnp.zeros_like(vb)
        for g in range(ATT_GROUP):
            _, vjp = jax.vjp(functools.partial(_attn_head, valid=valid), q_ref[0, g], kb, vb, s_ref[0, g])
            dq, dk_g, dv_g, dsink = vjp(dy_ref[0, g])
            dq_ref[0, g] = dq
            ds_ref[0, g] += dsink
            dkb += dk_g
            dvb += dv_g
        for j, off in enumerate((-1, 0, 1)):
            start = pl.multiple_of(jnp.clip(n + off, 0, nq - 1) * ATT_BLOCK, ATT_BLOCK)
            rows = pl.ds(start, ATT_BLOCK)
            dk_ref[0, 0, rows, :] += dkb[j * ATT_BLOCK:(j + 1) * ATT_BLOCK]
            dv_ref[0, 0, rows, :] += dvb[j * ATT_BLOCK:(j + 1) * ATT_BLOCK]

    return pl.pallas_call(
        body, name="attn_bwd", grid=(ATT_KV_HEADS, B, nq),
        in_specs=[q_spec, kv_spec(-1), kv_spec(0), kv_spec(1), kv_spec(-1), kv_spec(0), kv_spec(1), sink_spec, q_spec],
        out_specs=[q_spec, kv_full, kv_full, sink_spec],
        out_shape=[jax.ShapeDtypeStruct(q.shape, F32), jax.ShapeDtypeStruct(k.shape, F32),
                   jax.ShapeDtypeStruct(v.shape, F32), jax.ShapeDtypeStruct(sink.shape, F32)],
        compiler_params=_cparams(("arbitrary", "arbitrary", "arbitrary")),
    )(q, k, k, k, v, v, v, sink, dy)


CONV_COLS = 256


def _conv_taps(u, seq):
    row = lax.broadcasted_iota(jnp.int32, u.shape, 0)
    prev = jnp.where(row == 0, 0.0, pltpu.roll(u, 1, axis=0))
    nxt = jnp.where(row == seq - 1, 0.0, pltpu.roll(u, seq - 1, axis=0))
    return prev, nxt


def _conv_fwd(proj3, w8):
    B, S, _ = proj3.shape
    ncb = 2 * MLSTM_WIDTH // CONV_COLS

    def body(u_ref, w_ref, o_ref):
        u = u_ref[0]
        prev, nxt = _conv_taps(u, S)
        o_ref[0] = _silu(prev * w_ref[0:1, :] + u * w_ref[1:2, :] + nxt * w_ref[2:3, :] + w_ref[3:4, :])

    return pl.pallas_call(
        body, name="conv_fwd", grid=(B, ncb),
        in_specs=[pl.BlockSpec((1, S, CONV_COLS), lambda b, c: (b, 0, C_QK // CONV_COLS + c)),
                  pl.BlockSpec((8, CONV_COLS), lambda b, c: (0, c))],
        out_specs=pl.BlockSpec((1, S, CONV_COLS), lambda b, c: (b, 0, c)),
        out_shape=jax.ShapeDtypeStruct((B, S, 2 * MLSTM_WIDTH), F32),
        compiler_params=_cparams(("parallel", "parallel")),
    )(proj3, w8)


def _conv_bwd(proj3, w8, dout_f, dout_b):
    B, S, _ = proj3.shape
    ncb = 2 * MLSTM_WIDTH // CONV_COLS

    def body(u_ref, w_ref, df_ref, db_ref, du_ref, dw_ref):
        b = pl.program_id(1)
        u = u_ref[0]
        prev, nxt = _conv_taps(u, S)
        w0, w1, w2 = w_ref[0:1, :], w_ref[1:2, :], w_ref[2:3, :]
        pre = prev * w0 + u * w1 + nxt * w2 + w_ref[3:4, :]
        sig = _sigmoid(pre)
        dpre = (df_ref[0] + db_ref[0]) * (sig * (1.0 + pre * (1.0 - sig)))
        dprev, dnxt = _conv_taps(dpre, S)
        du_ref[0] = dnxt * w0 + dpre * w1 + dprev * w2

        @pl.when(b == 0)
        def _():
            dw_ref[...] = jnp.zeros_like(dw_ref)

        dw_ref[0:1, :] += jnp.sum(dpre * prev, axis=0, keepdims=True)
        dw_ref[1:2, :] += jnp.sum(dpre * u, axis=0, keepdims=True)
        dw_ref[2:3, :] += jnp.sum(dpre * nxt, axis=0, keepdims=True)
        dw_ref[3:4, :] += jnp.sum(dpre, axis=0, keepdims=True)

    blk = pl.BlockSpec((1, S, CONV_COLS), lambda c, b: (b, 0, c))
    return pl.pallas_call(
        body, name="conv_bwd", grid=(ncb, B),
        in_specs=[pl.BlockSpec((1, S, CONV_COLS), lambda c, b: (b, 0, C_QK // CONV_COLS + c)),
                  pl.BlockSpec((8, CONV_COLS), lambda c, b: (0, c)), blk, blk],
        out_specs=[blk, pl.BlockSpec((8, CONV_COLS), lambda c, b: (0, c))],
        out_shape=[jax.ShapeDtypeStruct((B, S, 2 * MLSTM_WIDTH), F32), jax.ShapeDtypeStruct((8, 2 * MLSTM_WIDTH), F32)],
        compiler_params=_cparams(("parallel", "arbitrary")),
    )(proj3, w8, dout_f, dout_b)


def _chunk_masks(direction):
    t = lax.broadcasted_iota(jnp.int32, (MLSTM_CHUNK, MLSTM_CHUNK), 0)
    s = lax.broadcasted_iota(jnp.int32, (MLSTM_CHUNK, MLSTM_CHUNK), 1)
    le, ge = (s <= t).astype(F32), (s >= t).astype(F32)
    eye = (s == t).astype(F32)
    return (le, ge, eye) if direction == 0 else (ge, le, eye)


def _gate_cols(gates, direction, head):
    lane = lax.broadcasted_iota(jnp.int32, gates.shape, 1)
    sel_i = (lane == (2 * direction) * MLSTM_HEADS + head).astype(F32)
    sel_f = (lane == (2 * direction + 1) * MLSTM_HEADS + head).astype(F32)
    return sel_i, sel_f


def _mlstm_fwd(qk, proj3, bias):
    B, S, _ = qk.shape
    nc = S // MLSTM_CHUNK
    H, L, DH = MLSTM_HEADS, MLSTM_CHUNK, MLSTM_HEAD_DIM

    def chunk_of(d, c):
        return c if d == 0 else nc - 1 - c

    def body(qkf, qkb, vf, vb, gf, gb, bias_ref, hf, hb, csf, csb, nsf, nsb, msf, msb, c_st, n_st, m_st):
        c, h = pl.program_id(1), pl.program_id(2)

        @pl.when(c == 0)
        def _():
            for d in range(2):
                c_st[d, h] = jnp.zeros((DH, DH), F32)
                n_st[d, h] = jnp.zeros((1, DH), F32)
                m_st[d, h] = jnp.zeros((1, DH), F32)

        for d, (qk_ref, v_ref, g_ref, h_ref, cs, ns, ms) in enumerate(
                ((qkf, vf, gf, hf, csf, nsf, msf), (qkb, vb, gb, hb, csb, nsb, msb))):
            incl, incl_t, eye = _chunk_masks(d)
            gates = g_ref[0] + bias_ref[...]
            sel_i, sel_f = _gate_cols(gates, d, h)
            li = jnp.sum(gates * sel_i, axis=1, keepdims=True)
            lf_pre = jnp.sum(gates * sel_f, axis=1, keepdims=True)
            c_in, n_in, m_in = c_st[d, h], n_st[d, h], m_st[d, h]
            cs[0, 0, 0], ns[0, 0, 0], ms[0, 0, 0] = c_in, n_in, m_in
            hh, c_new, n_new, m_new = _mlstm_chunk(
                qk_ref[0, :, :DH], qk_ref[0, :, DH:], v_ref[0], li, lf_pre, c_in, n_in,
                jnp.max(m_in, axis=1, keepdims=True), incl, incl_t, eye)
            h_ref[0] = hh
            c_st[d, h], n_st[d, h] = c_new, n_new
            m_st[d, h] = jnp.broadcast_to(m_new, (1, DH))

    def tok_spec(width, base, d, per_head):
        return pl.BlockSpec((1, L, width), lambda b, c, h: (b, chunk_of(d, c), base + (h if per_head else 0)))

    def st_spec(shape, d):
        return pl.BlockSpec((1, 1, 1) + shape, lambda b, c, h: (b, chunk_of(d, c), h, 0, 0))

    in_specs = [tok_spec(2 * DH, 0, 0, True), tok_spec(2 * DH, 0, 1, True),
                tok_spec(DH, C_VM // DH, 0, True), tok_spec(DH, C_VM // DH, 1, True),
                tok_spec(LANES, C_GATES // LANES, 0, False), tok_spec(LANES, C_GATES // LANES, 1, False),
                pl.BlockSpec((1, LANES), lambda b, c, h: (0, 0))]
    out_specs = [tok_spec(DH, 0, 0, True), tok_spec(DH, 0, 1, True),
                 st_spec((DH, DH), 0), st_spec((DH, DH), 1), st_spec((1, DH), 0), st_spec((1, DH), 1),
                 st_spec((1, DH), 0), st_spec((1, DH), 1)]
    hs = jax.ShapeDtypeStruct((B, S, H * DH), F32)
    cs = jax.ShapeDtypeStruct((B, nc, H, DH, DH), F32)
    vs = jax.ShapeDtypeStruct((B, nc, H, 1, DH), F32)
    return pl.pallas_call(
        body, name="mlstm_fwd", grid=(B, nc, H), in_specs=in_specs, out_specs=out_specs,
        out_shape=[hs, hs, cs, cs, vs, vs, vs, vs],
        scratch_shapes=[pltpu.VMEM((2, H, DH, DH), F32), pltpu.VMEM((2, H, 1, DH), F32), pltpu.VMEM((2, H, 1, DH), F32)],
        compiler_params=_cparams(("parallel", "arbitrary", "arbitrary")),
    )(qk, qk, proj3, proj3, proj3, proj3, bias)


def _mlstm_bwd(qk, proj3, bias, states, dh):
    B, S, _ = qk.shape
    nc = S // MLSTM_CHUNK
    H, L, DH = MLSTM_HEADS, MLSTM_CHUNK, MLSTM_HEAD_DIM

    def chunk_of(d, c):
        return nc - 1 - c if d == 0 else c

    def body(qkf, qkb, vf, vb, gf, gb, bias_ref, csf, csb, nsf, nsb, msf, msb, dhf, dhb,
             dqkf, dqkb, dvf, dvb, dgf, dgb, dc_st, dn_st, dm_st):
        c, h = pl.program_id(1), pl.program_id(2)

        @pl.when(c == 0)
        def _():
            for d in range(2):
                dc_st[d, h] = jnp.zeros((DH, DH), F32)
                dn_st[d, h] = jnp.zeros((1, DH), F32)
                dm_st[d, h] = jnp.zeros((1, DH), F32)

        @pl.when(h == 0)
        def _():
            dgf[...] = jnp.zeros_like(dgf)
            dgb[...] = jnp.zeros_like(dgb)

        for d, (qk_ref, v_ref, g_ref, cs, ns, ms, dh_ref, dqk_ref, dv_ref, dg_ref) in enumerate(
                ((qkf, vf, gf, csf, nsf, msf, dhf, dqkf, dvf, dgf), (qkb, vb, gb, csb, nsb, msb, dhb, dqkb, dvb, dgb))):
            incl, incl_t, eye = _chunk_masks(d)
            gates = g_ref[0] + bias_ref[...]
            sel_i, sel_f = _gate_cols(gates, d, h)
            li = jnp.sum(gates * sel_i, axis=1, keepdims=True)
            lf_pre = jnp.sum(gates * sel_f, axis=1, keepdims=True)
            m_in = jnp.max(ms[0, 0, 0], axis=1, keepdims=True)
            _, vjp = jax.vjp(
                functools.partial(_mlstm_chunk, incl=incl, incl_t=incl_t, eye=eye),
                qk_ref[0, :, :DH], qk_ref[0, :, DH:], v_ref[0], li, lf_pre, cs[0, 0, 0], ns[0, 0, 0], m_in)
            dm_out = jnp.max(dm_st[d, h], axis=1, keepdims=True)
            dq, dk, dv, dli, dlf, dc, dn, dm = vjp((dh_ref[0], dc_st[d, h], dn_st[d, h], dm_out))
            dqk_ref[0, :, :DH] = dq
            dqk_ref[0, :, DH:] = dk
            dv_ref[0] = dv
            dg_ref[0] += dli * sel_i + dlf * sel_f
            dc_st[d, h], dn_st[d, h] = dc, dn
            dm_st[d, h] = jnp.broadcast_to(dm, (1, DH))

    def tok_spec(width, base, d, per_head):
        return pl.BlockSpec((1, L, width), lambda b, c, h: (b, chunk_of(d, c), base + (h if per_head else 0)))

    def st_spec(shape, d):
        return pl.BlockSpec((1, 1, 1) + shape, lambda b, c, h: (b, chunk_of(d, c), h, 0, 0))

    in_specs = [tok_spec(2 * DH, 0, 0, True), tok_spec(2 * DH, 0, 1, True),
                tok_spec(DH, C_VM // DH, 0, True), tok_spec(DH, C_VM // DH, 1, True),
                tok_spec(LANES, C_GATES // LANES, 0, False), tok_spec(LANES, C_GATES // LANES, 1, False),
                pl.BlockSpec((1, LANES), lambda b, c, h: (0, 0)),
                st_spec((DH, DH), 0), st_spec((DH, DH), 1), st_spec((1, DH), 0), st_spec((1, DH), 1),
                st_spec((1, DH), 0), st_spec((1, DH), 1), tok_spec(DH, 0, 0, True), tok_spec(DH, 0, 1, True)]
    out_specs = [tok_spec(2 * DH, 0, 0, True), tok_spec(2 * DH, 0, 1, True), tok_spec(DH, 0, 0, True), tok_spec(DH, 0, 1, True),
                 tok_spec(LANES, 0, 0, False), tok_spec(LANES, 0, 1, False)]
    qks = jax.ShapeDtypeStruct((B, S, 2 * H * DH), F32)
    vs = jax.ShapeDtypeStruct((B, S, H * DH), F32)
    gs = jax.ShapeDtypeStruct((B, S, LANES), F32)
    csf, csb, nsf, nsb, msf, msb = states
    return pl.pallas_call(
        body, name="mlstm_bwd", grid=(B, nc, H), in_specs=in_specs, out_specs=out_specs,
        out_shape=[qks, qks, vs, vs, gs, gs],
        scratch_shapes=[pltpu.VMEM((2, H, DH, DH), F32), pltpu.VMEM((2, H, 1, DH), F32), pltpu.VMEM((2, H, 1, DH), F32)],
        compiler_params=_cparams(("parallel", "arbitrary", "arbitrary")),
    )(qk, qk, proj3, proj3, proj3, proj3, bias, csf, csb, nsf, nsb, msf, msb, dh, dh)


ROW_BLOCK = 256
FF_COLS = 512
FF_SHARD = D_FF // N_DEV
FF_SHARD_PAD = 384
FF_PAD = N_DEV * FF_SHARD_PAD


def _rms_fwd(name, x, g):
    T = x.shape[0]
    return _rowwise(name, lambda xv, gv: _rms(xv, gv), [_In(x), _In(g, rows=False)], [_Out(D_MODEL, BF16)], T, ROW_BLOCK)[0]


def _rms_bwd(name, x, g, dh, dres):
    T = x.shape[0]

    def fn(xv, gv, dhv, drv):
        _, vjp = jax.vjp(_rms, xv, gv)
        dx, dg = vjp(dhv)
        return drv + dx, dg

    return _rowwise(name, fn, [_In(x), _In(g, rows=False), _In(dh), _In(dres)],
                    [_Out(D_MODEL), _Out(D_MODEL, rows=False)], T, ROW_BLOCK)


def _mmw(name, a, w, mode, **kw):
    if isinstance(w, tuple):
        return _matmul(name, a, w[0], mode, bl=w[1], **kw)
    return _matmul(name, a, w, mode, **kw)


def _ffn_fwd(tag, x, g, wg, wu, wd):
    T = x.shape[0]
    h = _rms_fwd(tag + "_norm", x, g)
    gate = _mmw(tag + "_gate", h, wg, "nn")
    up = _mmw(tag + "_up", h, wu, "nn")
    act = _rowwise(tag + "_act", lambda a, b: _silu(a) * b,
                   [_In(gate, FF_COLS, split=True), _In(up, FF_COLS, split=True)],
                   [_Out(FF_PAD, BF16, FF_COLS, split=True)], T, 1024, ncol=FF_PAD // FF_COLS)[0]
    out = _mmw(tag + "_down", act, wd, "nn", res=x, scale=0.5)
    return out, (x, h, gate, up, act)


def _ffn_bwd(tag, saved, g, wg, wu, wd, dx):
    x, h, gate, up, act = saved
    T = x.shape[0]
    dact = _mmw(tag + "_dact", dx, wd, "nt", scale=0.5)
    dwd = _matmul(tag + "_dwd", act, dx, "tn", scale=0.5)

    def fn(a, b, da):
        _, vjp = jax.vjp(lambda p, q: _silu(p) * q, a, b)
        return vjp(da)

    dgate, dup = _rowwise(tag + "_dactfn", fn,
                          [_In(gate, FF_COLS, split=True), _In(up, FF_COLS, split=True), _In(dact, FF_COLS, split=True)],
                          [_Out(FF_PAD, BF16, FF_COLS, split=True), _Out(FF_PAD, BF16, FF_COLS, split=True)],
                          T, 1024, ncol=FF_PAD // FF_COLS)
    dh = _mmw(tag + "_dh1", dgate, wg, "nt")
    dh = _mmw(tag + "_dh2", dup, wu, "nt", res=dh)
    dwg = _matmul(tag + "_dwg", h, dgate, "tn")
    dwu = _matmul(tag + "_dwu", h, dup, "tn")
    dx_new, dg = _rms_bwd(tag + "_dnorm", x, g, dh, dx)
    return dx_new, dg, dwg, dwu, dwd


def _rope_consts():
    half = ROPE_DIM // 2
    inv_freq = jnp.power(jnp.float32(ROPE_THETA), -jnp.arange(half, dtype=F32) * (2.0 / ROPE_DIM))
    row = jnp.zeros((1, ATT_HEAD_DIM), F32).at[0, :ROPE_DIM].set(jnp.concatenate([inv_freq, inv_freq]))
    rot = np.zeros((ATT_HEAD_DIM, ATT_HEAD_DIM), np.float32)
    for i in range(half):
        rot[half + i, i] = -1.0
        rot[i, half + i] = 1.0
    return row, jnp.asarray(rot)


def _prep_fwd(name, t, g, pos, inv_freq_row, rot):
    R = t.shape[0]

    def fn(tv, gv, pv, fv, rv):
        cos, sin = _rope_tables(pv, fv)
        return _qk_prep(tv, gv, cos, sin, rv)

    return _rowwise(name, fn, [_In(t), _In(g, rows=False), _In(pos), _In(inv_freq_row, rows=False), _In(rot, rows=False)],
                    [_Out(ATT_HEAD_DIM)], R, 1024)[0]


def _prep_bwd(name, t, g, pos, inv_freq_row, rot, dout):
    R = t.shape[0]

    def fn(tv, gv, pv, fv, rv, dv):
        cos, sin = _rope_tables(pv, fv)
        _, vjp = jax.vjp(lambda a, b: _qk_prep(a, b, cos, sin, rv), tv, gv)
        return vjp(dv)

    return _rowwise(name, fn, [_In(t), _In(g, rows=False), _In(pos), _In(inv_freq_row, rows=False), _In(rot, rows=False), _In(dout)],
                    [_Out(ATT_HEAD_DIM), _Out(ATT_HEAD_DIM, rows=False)], R, 1024)


def _to_heads(t, B, S, nh):
    return t.reshape(B, S, nh, ATT_HEAD_DIM).transpose(0, 2, 1, 3)


def _from_heads(t):
    B, nh, S, _ = t.shape
    return t.transpose(0, 2, 1, 3).reshape(B * S, nh * ATT_HEAD_DIM)


def _mix_fwd(x, pos_q, pos_k, B, S, p):
    T = B * S
    h = _rms_fwd("mix_norm", x, p["mix_norm"])
    proj = _matmul("mix_proj", h, p["w_in"], "nn")
    proj3 = proj.reshape(B, S, IN_PAD)
    inv_freq_row, rot = _rope_consts()
    qa = proj[:, C_QA:C_QA + ATT_WIDTH].reshape(T * ATT_HEADS, ATT_HEAD_DIM)
    ka = proj[:, C_KA:C_KA + ATT_KV_WIDTH].reshape(T * ATT_KV_HEADS, ATT_HEAD_DIM)
    q_r = _prep_fwd("q_prep", qa, p["attn_q_norm"], pos_q, inv_freq_row, rot)
    k_r = _prep_fwd("k_prep", ka, p["attn_k_norm"], pos_k, inv_freq_row, rot)
    qh = _to_heads(q_r, B, S, ATT_HEADS)
    kh = _to_heads(k_r, B, S, ATT_KV_HEADS)
    vh = _to_heads(proj[:, C_VA:C_VA + ATT_KV_WIDTH], B, S, ATT_KV_HEADS)
    sink = p["attn_sink"].reshape(ATT_KV_HEADS, ATT_GROUP, 1, 1)
    y_a = _from_heads(_attn_fwd(qh, kh, vh, sink))

    qk_c = _conv_fwd(proj3, p["conv_w8"])
    hf, hb, *states = _mlstm_fwd(qk_c, proj3, p["gate_bias"])
    hf2, hb2 = hf.reshape(T, MLSTM_WIDTH), hb.reshape(T, MLSTM_WIDTH)
    DH = MLSTM_HEAD_DIM
    y_m = _rowwise("mlstm_out", _mlstm_combine,
                   [_In(hf2, DH, split=True), _In(hb2, DH, split=True), _In(proj, DH, C_OM // DH, split=True),
                    _In(p["mlstm_out_norm"], DH, split=True, rows=False)],
                   [_Out(MLSTM_WIDTH, BF16, DH, split=True)], T, 1024, ncol=MLSTM_HEADS)[0]

    za = _mmw("branch_a", y_a, p["w_branch_attn"], "nn")
    zm = _mmw("branch_m", y_m, p["w_branch_mlstm"], "nn")
    W = 512
    merged = _rowwise("merge", _merge,
                      [_In(proj, W, C_GMERGE // W, split=True), _In(proj, W, (C_GMERGE + D_MODEL) // W, split=True),
                       _In(za, W, split=True), _In(zm, W, split=True)],
                      [_Out(D_MODEL, BF16, W, split=True)], T, 512, ncol=D_MODEL // W)[0]
    out = _mmw("mix_out", merged, p["w_out"], "nn", res=x)
    saved = dict(x=x, h=h, proj=proj, qa=qa, ka=ka, qh=qh, kh=kh, vh=vh, sink=sink, y_a=y_a, qk_c=qk_c, hf=hf2, hb=hb2,
                 states=states, y_m=y_m, za=za, zm=zm, merged=merged)
    return out, saved


def _mix_bwd(sv, pos_q, pos_k, B, S, p, dx):
    T = B * S
    DH = MLSTM_HEAD_DIM
    proj = sv["proj"]
    proj3 = proj.reshape(B, S, IN_PAD)
    inv_freq_row, rot = _rope_consts()
    g = {}
    dmerged = _mmw("mix_dmerged", dx, p["w_out"], "nt")
    g["w_out"] = _matmul("mix_dwout", sv["merged"], dx, "tn")
    W = 512

    def merge_bwd(ga, gm, za, zm, dm):
        _, vjp = jax.vjp(_merge, ga, gm, za, zm)
        return vjp(dm)

    dga, dgm, dza, dzm = _rowwise(
        "merge_bwd", merge_bwd,
        [_In(proj, W, C_GMERGE // W, split=True), _In(proj, W, (C_GMERGE + D_MODEL) // W, split=True),
         _In(sv["za"], W, split=True), _In(sv["zm"], W, split=True), _In(dmerged, W, split=True)],
        [_Out(D_MODEL, F32, W, split=True), _Out(D_MODEL, F32, W, split=True),
         _Out(D_MODEL, BF16, W, split=True), _Out(D_MODEL, BF16, W, split=True)], T, 512, ncol=D_MODEL // W)
    dya = _mmw("branch_a_dx", dza, p["w_branch_attn"], "nt")
    g["w_branch_attn"] = _matmul("branch_a_dw", sv["y_a"], dza, "tn")
    dym = _mmw("branch_m_dx", dzm, p["w_branch_mlstm"], "nt")
    g["w_branch_mlstm"] = _matmul("branch_m_dw", sv["y_m"], dzm, "tn")

    def combine_bwd(hf, hb, o_pre, gn, dy):
        _, vjp = jax.vjp(_mlstm_combine, hf, hb, o_pre, gn)
        dhf, _, do, dg = vjp(dy)
        return dhf, do, dg

    dh, dom, g["mlstm_out_norm"] = _rowwise(
        "mlstm_out_bwd", combine_bwd,
        [_In(sv["hf"], DH, split=True), _In(sv["hb"], DH, split=True), _In(proj, DH, C_OM // DH, split=True),
         _In(p["mlstm_out_norm"], DH, split=True, rows=False), _In(dym, DH, split=True)],
        [_Out(MLSTM_WIDTH, F32, DH, split=True), _Out(MLSTM_WIDTH, F32, DH, split=True),
         _Out(MLSTM_WIDTH, F32, DH, split=True, rows=False)], T, 1024, ncol=MLSTM_HEADS)
    dqk_f, dqk_b, dv_f, dv_b, dg_f, dg_b = _mlstm_bwd(sv["qk_c"], proj3, p["gate_bias"], sv["states"],
                                                       dh.reshape(B, S, MLSTM_WIDTH))
    dgates, dvm, g["gate_bias"] = _rowwise(
        "mlstm_dsum", lambda a, b, c, d: (a + b, c + d, jnp.sum(a + b, axis=0, keepdims=True)),
        [_In(dg_f.reshape(T, LANES)), _In(dg_b.reshape(T, LANES)), _In(dv_f.reshape(T, MLSTM_WIDTH)), _In(dv_b.reshape(T, MLSTM_WIDTH))],
        [_Out(LANES), _Out(MLSTM_WIDTH), _Out(LANES, rows=False)], T, 1024)
    dqk, g["conv_w8"] = _conv_bwd(proj3, p["conv_w8"], dqk_f, dqk_b)

    dyh = _to_heads(dya, B, S, ATT_HEADS)
    dqh, dkh, dvh, dsink = _attn_bwd(sv["qh"], sv["kh"], sv["vh"], sv["sink"], dyh)
    g["attn_sink"] = dsink.reshape(1, ATT_HEADS)
    dq_r = dqh.transpose(0, 2, 1, 3).reshape(T * ATT_HEADS, ATT_HEAD_DIM)
    dk_r = dkh.transpose(0, 2, 1, 3).reshape(T * ATT_KV_HEADS, ATT_HEAD_DIM)
    dva = _from_heads(dvh)
    dqa, g["attn_q_norm"] = _prep_bwd("q_prep_bwd", sv["qa"], p["attn_q_norm"], pos_q, inv_freq_row, rot, dq_r)
    dka, g["attn_k_norm"] = _prep_bwd("k_prep_bwd", sv["ka"], p["attn_k_norm"], pos_k, inv_freq_row, rot, dk_r)

    dproj = jnp.concatenate(
        [dga.astype(BF16), dgm.astype(BF16), dqk.reshape(T, 2 * MLSTM_WIDTH).astype(BF16), dvm.astype(BF16), dom.astype(BF16),
         dqa.reshape(T, ATT_WIDTH).astype(BF16), dka.reshape(T, ATT_KV_WIDTH).astype(BF16), dva.astype(BF16),
         dgates.astype(BF16)], axis=1)
    dh2 = _matmul("mix_dh", dproj, p["w_in"], "nt")
    g["w_in"] = _matmul("mix_dwin", sv["h"], dproj, "tn")
    dx_new, g["mix_norm"] = _rms_bwd("mix_dnorm", sv["x"], p["mix_norm"], dh2, dx)
    return dx_new, g


def _loss_and_grad(x, g, target):
    T = x.shape[0]

    def loss_fn(xv, gv, tv):
        err = jnp.square(_rms(xv, gv) - tv)
        return 0.5 * jnp.sum(jnp.mean(err, axis=-1, keepdims=True), axis=0, keepdims=True)

    def fn(xv, gv, tv):
        val, vjp = jax.vjp(lambda a, b: loss_fn(a, b, tv), xv, gv)
        dx, dg = vjp(jnp.ones((1, 1), F32))
        return val, dx, dg

    return _rowwise("loss_head", fn, [_In(x), _In(g, rows=False), _In(target)],
                    [_Out(1, rows=False), _Out(D_MODEL), _Out(D_MODEL, rows=False)], T, ROW_BLOCK)


def _block_norm_fwd(x, g):
    T = x.shape[0]
    return _rowwise("block_norm", _rms, [_In(x), _In(g, rows=False)], [_Out(D_MODEL)], T, ROW_BLOCK)[0]


def _block_norm_bwd(x, g, dy):
    T = x.shape[0]

    def fn(xv, gv, dv):
        _, vjp = jax.vjp(_rms, xv, gv)
        return vjp(dv)

    return _rowwise("block_norm_bwd", fn, [_In(x), _In(g, rows=False), _In(dy)],
                    [_Out(D_MODEL), _Out(D_MODEL, rows=False)], T, ROW_BLOCK)


def _qk_perm_cols(t, axis):
    q, k = jnp.split(t, 2, axis=axis)
    parts = []
    for h in range(MLSTM_HEADS):
        sl = [slice(None)] * t.ndim
        sl[axis] = slice(h * MLSTM_HEAD_DIM, (h + 1) * MLSTM_HEAD_DIM)
        parts += [q[tuple(sl)], k[tuple(sl)]]
    return jnp.concatenate(parts, axis=axis)


def _qk_unperm_cols(t, axis):
    qs, ks = [], []
    for h in range(MLSTM_HEADS):
        sl = [slice(None)] * t.ndim
        sl[axis] = slice(2 * h * MLSTM_HEAD_DIM, (2 * h + 1) * MLSTM_HEAD_DIM)
        qs.append(t[tuple(sl)])
        sl[axis] = slice((2 * h + 1) * MLSTM_HEAD_DIM, (2 * h + 2) * MLSTM_HEAD_DIM)
        ks.append(t[tuple(sl)])
    return jnp.concatenate(qs + ks, axis=axis)


def _w_in_arrange(w):
    qa, ka, va, qm, km, vm, om, gm, gmerge = jnp.split(w, np.cumsum(
        (ATT_WIDTH, ATT_KV_WIDTH, ATT_KV_WIDTH, MLSTM_WIDTH, MLSTM_WIDTH, MLSTM_WIDTH, MLSTM_WIDTH, MLSTM_N_GATES))[:].tolist(), axis=1)
    qk = _qk_perm_cols(jnp.concatenate([qm, km], axis=1), 1)
    pad = jnp.zeros((w.shape[0], LANES - MLSTM_N_GATES), w.dtype)
    return jnp.concatenate([gmerge, qk, vm, om, qa, ka, va, gm, pad], axis=1)


def _w_in_restore(w):
    gmerge = w[:, C_GMERGE:C_GMERGE + 2 * D_MODEL]
    qk = _qk_unperm_cols(w[:, C_QK:C_QK + 2 * MLSTM_WIDTH], 1)
    vm, om = w[:, C_VM:C_VM + MLSTM_WIDTH], w[:, C_OM:C_OM + MLSTM_WIDTH]
    qa, ka, va = w[:, C_QA:C_QA + ATT_WIDTH], w[:, C_KA:C_KA + ATT_KV_WIDTH], w[:, C_VA:C_VA + ATT_KV_WIDTH]
    gm = w[:, C_GATES:C_GATES + MLSTM_N_GATES]
    return jnp.concatenate([qa, ka, va, qk, vm, om, gm, gmerge], axis=1)


BIG = ("ffn1_w_gate", "ffn1_w_up", "ffn1_w_down", "w_in", "mlstm_conv_w", "w_branch_attn", "w_branch_mlstm", "w_out",
       "ffn2_w_gate", "ffn2_w_up", "ffn2_w_down")
MATMUL_W = tuple(n for n in BIG if n != "mlstm_conv_w")
SMALL = ("ffn1_norm", "mix_norm", "mlstm_gate_bias", "attn_q_norm", "attn_k_norm", "attn_sink", "mlstm_conv_b",
         "mlstm_out_norm", "ffn2_norm", "block_out_norm")
WEIGHTS = ("ffn1_norm", "ffn1_w_gate", "ffn1_w_up", "ffn1_w_down", "mix_norm", "w_in", "mlstm_gate_bias", "attn_q_norm",
           "attn_k_norm", "attn_sink", "mlstm_conv_w", "mlstm_conv_b", "mlstm_out_norm", "w_branch_attn", "w_branch_mlstm",
           "w_out", "ffn2_norm", "ffn2_w_gate", "ffn2_w_up", "ffn2_w_down", "block_out_norm")
PACK_COLS = 1024


def _padded_rows(n_elems):
    return -(-n_elems // PACK_COLS)


def _pack_flat(arrs, dtype, row_multiple):
    parts = []
    for a in arrs:
        flat = a.reshape(-1).astype(dtype)
        pad = _padded_rows(flat.shape[0]) * PACK_COLS - flat.shape[0]
        parts.append(jnp.pad(flat, (0, pad)) if pad else flat)
    flat = jnp.concatenate(parts)
    rows = flat.shape[0] // PACK_COLS
    extra = (-rows) % row_multiple
    if extra:
        flat = jnp.pad(flat, (0, extra * PACK_COLS))
    return flat.reshape(-1, PACK_COLS)


def _unpack_flat(buf, shapes, lead=()):
    flat = buf.reshape(lead + (-1,))
    out, off = [], 0
    for s in shapes:
        n = int(np.prod(s))
        out.append(flat[..., off:off + n].reshape(lead + tuple(s)))
        off += _padded_rows(n) * PACK_COLS
    return out


class _Lay:
    def __init__(self, shard, axis, width):
        self.shard, self.axis, self.width = shard, axis, width
        self.padded = tuple(width if a == axis else s for a, s in enumerate(shard))
        self.whole = tuple(N_DEV * width if a == axis else s for a, s in enumerate(shard))

    def pad(self, t, lead=0):
        extra = self.width - self.shard[self.axis]
        if not extra:
            return t
        cfg = [(0, 0)] * t.ndim
        cfg[lead + self.axis] = (0, extra)
        return jnp.pad(t, cfg)

    def unpad(self, t, lead=0):
        idx = [slice(None)] * t.ndim
        idx[lead + self.axis] = slice(0, self.shard[self.axis])
        return t[tuple(idx)]


_FF_COL = _Lay((D_MODEL, FF_SHARD), 1, FF_SHARD_PAD)
_FF_ROW = _Lay((FF_SHARD, D_MODEL), 0, FF_SHARD_PAD)
LAYOUTS = {
    "ffn1_w_gate": _FF_COL, "ffn1_w_up": _FF_COL, "ffn1_w_down": _FF_ROW,
    "ffn2_w_gate": _FF_COL, "ffn2_w_up": _FF_COL, "ffn2_w_down": _FF_ROW,
    "w_in": _Lay((D_MODEL, IN_WIDTH // N_DEV), 0, D_MODEL),
    "mlstm_conv_w": _Lay((3, 2 * MLSTM_WIDTH // N_DEV), 1, 2 * MLSTM_WIDTH // N_DEV),
    "w_branch_attn": _Lay((ATT_WIDTH, D_MODEL // N_DEV), 1, D_MODEL // N_DEV),
    "w_branch_mlstm": _Lay((MLSTM_WIDTH, D_MODEL // N_DEV), 1, D_MODEL // N_DEV),
    "w_out": _Lay((D_MODEL // N_DEV, D_MODEL), 0, D_MODEL // N_DEV),
}


def _window(ref, axis, j, width):
    idx = [slice(None)] * len(ref.shape)
    idx[axis] = pl.ds(pl.multiple_of(j * width, width), width)
    return ref.at[tuple(idx)]


ANY = pl.BlockSpec(memory_space=pl.ANY)


def _mesh_pos():
    return lax.axis_index("x"), lax.axis_index("y"), lax.axis_index("c")


def _all_gather(name, shard, vmem=False):
    R, C = shard.shape
    space = pl.BlockSpec(memory_space=pltpu.VMEM) if vmem else ANY

    def body(x_ref, out_ref, send_sems, recv_sems, local_sem):
        x, y, c = _mesh_pos()
        me, sibling = (x, y, c), (x, y, 1 - c)
        chips = [(1 - x, y), (x, 1 - y), (1 - x, 1 - y)]

        def slot(px, py, pc):
            return out_ref.at[4 * px + 2 * py + pc]

        def copy(k, block, to, src=None):
            return pltpu.make_async_remote_copy(
                src_ref=slot(*block) if src is None else src, dst_ref=slot(*block),
                send_sem=send_sems.at[k], recv_sem=recv_sems.at[k], device_id=to, device_id_type=MESH)

        mine = pltpu.make_async_copy(x_ref, slot(*me), local_sem)
        mine.start()
        first = [copy(0, me, sibling, src=x_ref)]
        first += [copy(1 + j, me, (*chip, c), src=x_ref) for j, chip in enumerate(chips)]
        for cp in first:
            cp.start()
        passed = [copy(4 + j, (*chip, c), sibling) for j, chip in enumerate(chips)]
        for j, chip in enumerate(chips):
            copy(1 + j, (*chip, c), me).wait_recv()
            passed[j].start()
        copy(0, sibling, me).wait_recv()
        for j, chip in enumerate(chips):
            copy(4 + j, (*chip, 1 - c), me).wait_recv()
        for cp in first + passed:
            cp.wait_send()
        mine.wait()

    return pl.pallas_call(
        body, name=name, out_shape=jax.ShapeDtypeStruct((N_DEV, R, C), shard.dtype),
        in_specs=[space], out_specs=space,
        scratch_shapes=[pltpu.SemaphoreType.DMA((7,)), pltpu.SemaphoreType.DMA((7,)), pltpu.SemaphoreType.DMA],
    )(shard)


HBM = pl.BlockSpec(memory_space=pltpu.HBM)
SEM = pl.BlockSpec(memory_space=pltpu.SEMAPHORE)
SPLIT_COPY = pltpu.CompilerParams(has_side_effects=pltpu.SideEffectType.DATAFLOW_SIDE_EFFECTING)
N_PEERS = N_DEV - 1


def _peers(x, y, c):
    return [(x, y, 1 - c), (1 - x, y, c), (x, 1 - y, c), (1 - x, 1 - y, c),
            (1 - x, y, 1 - c), (x, 1 - y, 1 - c), (1 - x, 1 - y, 1 - c)]


def _dev_index(pos):
    return 4 * pos[0] + 2 * pos[1] + pos[2]


def _gather_start(name, shards, lays, groups, after):
    nt, ng = len(shards), len(groups)

    def body(*refs):
        x_refs, land_refs = refs[:nt], refs[nt:2 * nt]
        sems = refs[2 * nt + 1:2 * nt + 1 + 2 * ng]
        pos = _mesh_pos()
        me = _dev_index(pos)
        for g, tens in enumerate(groups):
            for i, t in enumerate(tens):
                for k, peer in enumerate(_peers(*pos)):
                    pltpu.make_async_remote_copy(
                        src_ref=x_refs[t], dst_ref=_window(land_refs[t], lays[t].axis, me, lays[t].width),
                        send_sem=sems[2 * g].at[N_PEERS * i + k], recv_sem=sems[2 * g + 1].at[N_PEERS * i + k],
                        device_id=peer, device_id_type=MESH).start()

    sem_shapes = []
    for tens in groups:
        sem_shapes += [pltpu.SemaphoreType.DMA((N_PEERS * len(tens),))] * 2
    thru = [pltpu.HBM(s.shape, s.dtype) for s in shards] + [pltpu.HBM(lay.whole, s.dtype) for s, lay in zip(shards, lays)]
    args = [pltpu.with_memory_space_constraint(s, pltpu.HBM) for s in shards]
    args += [pltpu.with_memory_space_constraint(lax.empty(lay.whole, s.dtype), pltpu.HBM) for s, lay in zip(shards, lays)]
    res = pl.pallas_call(
        body, name=name, out_shape=tuple(sem_shapes + thru), in_specs=[HBM] * (2 * nt) + [ANY],
        out_specs=tuple([SEM] * (2 * ng) + [HBM] * (2 * nt)),
        input_output_aliases={t: 2 * ng + t for t in range(2 * nt)}, compiler_params=SPLIT_COPY,
    )(*args, after)
    sems = [(res[2 * g], res[2 * g + 1]) for g in range(ng)]
    return sems, list(res[2 * ng:2 * ng + nt]), list(res[2 * ng + nt:])


def _gather_wait(name, sems, shards, lands, lays, after):
    nt = len(shards)
    send_sems, recv_sems = sems

    def body(*refs):
        x_refs, land_refs = refs[:nt], refs[nt:2 * nt]
        send_ref, recv_ref = refs[2 * nt], refs[2 * nt + 1]
        local_sems = refs[-1]
        pos = _mesh_pos()
        mine = [pltpu.make_async_copy(x_refs[t], _window(land_refs[t], lays[t].axis, _dev_index(pos), lays[t].width),
                                      local_sems.at[t]) for t in range(nt)]
        for cp in mine:
            cp.start()
        for t in range(nt):
            for k, peer in enumerate(_peers(*pos)):
                cp = pltpu.make_async_remote_copy(
                    src_ref=x_refs[t], dst_ref=_window(land_refs[t], lays[t].axis, _dev_index(peer), lays[t].width),
                    send_sem=send_ref.at[N_PEERS * t + k], recv_sem=recv_ref.at[N_PEERS * t + k],
                    device_id=peer, device_id_type=MESH)
                cp.wait_send()
                cp.wait_recv()
        for cp in mine:
            cp.wait()

    thru = [pltpu.HBM(s.shape, s.dtype) for s in shards] + [pltpu.HBM(ld.shape, ld.dtype) for ld in lands]
    res = pl.pallas_call(
        body, name=name, out_shape=tuple(thru), in_specs=[HBM] * (2 * nt) + [SEM, SEM, ANY],
        out_specs=tuple([HBM] * (2 * nt)), input_output_aliases={t: t for t in range(2 * nt)},
        scratch_shapes=[pltpu.SemaphoreType.DMA((nt,))], compiler_params=SPLIT_COPY,
    )(*shards, *lands, send_sems, recv_sems, after)
    return list(res[nt:])


def _pair_exchange(name, grads, lays):
    nt = len(grads)

    def body(*refs):
        g_refs, land_refs = refs[:nt], refs[nt:2 * nt]
        send_sems, recv_sems = refs[2 * nt:]
        x, y, c = _mesh_pos()
        copies = []
        for t in range(nt):
            for chip in range(4):
                copies.append(pltpu.make_async_remote_copy(
                    src_ref=_window(g_refs[t], lays[t].axis, 2 * chip + (1 - c), lays[t].width), dst_ref=land_refs[t].at[chip],
                    send_sem=send_sems.at[4 * t + chip], recv_sem=recv_sems.at[4 * t + chip],
                    device_id=(x, y, 1 - c), device_id_type=MESH))
        for cp in copies:
            cp.start()
        for cp in copies:
            cp.wait_recv()
        for cp in copies:
            cp.wait_send()

    out_shape = [jax.ShapeDtypeStruct((4,) + lay.padded, g.dtype) for g, lay in zip(grads, lays)]
    return pl.pallas_call(
        body, name=name, out_shape=out_shape, in_specs=[ANY] * nt, out_specs=[ANY] * nt,
        scratch_shapes=[pltpu.SemaphoreType.DMA((4 * nt,)), pltpu.SemaphoreType.DMA((4 * nt,))],
    )(*grads)


def _pair_sum(name, whole, landed, lay, out_dtype):
    R, C = lay.padded
    br = _first_divisor(R, (512, 384, 256, 128, 64, 32, 16, 8))
    nb = R // br
    if lay.axis == 0:
        mine_spec = pl.BlockSpec((br, C), lambda k, i, c_ref: ((2 * k + c_ref[0]) * nb + i, 0))
    else:
        mine_spec = pl.BlockSpec((br, C), lambda k, i, c_ref: (i, 2 * k + c_ref[0]))

    def body(c_ref, mine_ref, sib_ref, o_ref):
        o_ref[0] = (mine_ref[...] + sib_ref[0]).astype(out_dtype)

    c = lax.axis_index("c")
    return pl.pallas_call(
        body, name=name,
        grid_spec=pltpu.PrefetchScalarGridSpec(
            num_scalar_prefetch=1, grid=(4, nb),
            in_specs=[mine_spec, pl.BlockSpec((1, br, C), lambda k, i, c_ref: (k, i, 0))],
            out_specs=pl.BlockSpec((1, br, C), lambda k, i, c_ref: (k, i, 0))),
        out_shape=jax.ShapeDtypeStruct((4, R, C), out_dtype),
        compiler_params=_cparams(("parallel", "parallel")),
    )(c.reshape(1).astype(jnp.int32), whole, landed)


def _chip_exchange(name, sums):
    nt = len(sums)

    def body(*refs):
        s_refs, land_refs = refs[:nt], refs[nt:2 * nt]
        send_sems, recv_sems, local_sems = refs[2 * nt:]
        x, y, c = _mesh_pos()
        my_chip = 2 * x + y
        mine = [pltpu.make_async_copy(s_refs[t].at[my_chip], land_refs[t].at[my_chip], local_sems.at[t]) for t in range(nt)]
        for cp in mine:
            cp.start()
        chips = [(1 - x, y), (x, 1 - y), (1 - x, 1 - y)]
        copies = []
        for t in range(nt):
            for j, (px, py) in enumerate(chips):
                copies.append(pltpu.make_async_remote_copy(
                    src_ref=s_refs[t].at[2 * px + py], dst_ref=land_refs[t].at[my_chip],
                    send_sem=send_sems.at[3 * t + j], recv_sem=recv_sems.at[3 * t + j],
                    device_id=(px, py, c), device_id_type=MESH))
        for cp in copies:
            cp.start()
        for t in range(nt):
            for j, (px, py) in enumerate(chips):
                pltpu.make_async_remote_copy(
                    src_ref=s_refs[t].at[my_chip], dst_ref=land_refs[t].at[2 * px + py],
                    send_sem=send_sems.at[3 * t + j], recv_sem=recv_sems.at[3 * t + j],
                    device_id=(px, py, c), device_id_type=MESH).wait_recv()
        for cp in copies:
            cp.wait_send()
        for cp in mine:
            cp.wait()

    return pl.pallas_call(
        body, name=name, out_shape=[jax.ShapeDtypeStruct(s.shape, s.dtype) for s in sums],
        in_specs=[ANY] * nt, out_specs=[ANY] * nt,
        scratch_shapes=[pltpu.SemaphoreType.DMA((3 * nt,)), pltpu.SemaphoreType.DMA((3 * nt,)), pltpu.SemaphoreType.DMA((nt,))],
    )(*sums)


def _chip_start(name, sums):
    nt = len(sums)

    def body(*refs):
        s_refs, land_refs = refs[:nt], refs[nt:2 * nt]
        send_sems, recv_sems = refs[2 * nt], refs[2 * nt + 1]
        x, y, c = _mesh_pos()
        my_chip = 2 * x + y
        for t in range(nt):
            for j, (px, py) in enumerate([(1 - x, y), (x, 1 - y), (1 - x, 1 - y)]):
                pltpu.make_async_remote_copy(
                    src_ref=s_refs[t].at[2 * px + py], dst_ref=land_refs[t].at[my_chip],
                    send_sem=send_sems.at[3 * t + j], recv_sem=recv_sems.at[3 * t + j],
                    device_id=(px, py, c), device_id_type=MESH).start()

    thru = [pltpu.HBM(s.shape, s.dtype) for s in sums] * 2
    args = [pltpu.with_memory_space_constraint(s, pltpu.HBM) for s in sums]
    args += [pltpu.with_memory_space_constraint(lax.empty(s.shape, s.dtype), pltpu.HBM) for s in sums]
    res = pl.pallas_call(
        body, name=name, out_shape=tuple([pltpu.SemaphoreType.DMA((3 * nt,))] * 2 + thru), in_specs=[HBM] * (2 * nt),
        out_specs=tuple([SEM, SEM] + [HBM] * (2 * nt)), input_output_aliases={t: 2 + t for t in range(2 * nt)},
        compiler_params=SPLIT_COPY,
    )(*args)
    return (res[0], res[1]), list(res[2:2 + nt]), list(res[2 + nt:])


def _chip_wait(name, sems, sums, lands, after):
    nt = len(sums)

    def body(*refs):
        s_refs, land_refs = refs[:nt], refs[nt:2 * nt]
        send_sems, recv_sems = refs[2 * nt], refs[2 * nt + 1]
        local_sems = refs[-1]
        x, y, c = _mesh_pos()
        my_chip = 2 * x + y
        mine = [pltpu.make_async_copy(s_refs[t].at[my_chip], land_refs[t].at[my_chip], local_sems.at[t]) for t in range(nt)]
        for cp in mine:
            cp.start()
        for t in range(nt):
            for j, (px, py) in enumerate([(1 - x, y), (x, 1 - y), (1 - x, 1 - y)]):
                cp = pltpu.make_async_remote_copy(
                    src_ref=s_refs[t].at[my_chip], dst_ref=land_refs[t].at[2 * px + py],
                    send_sem=send_sems.at[3 * t + j], recv_sem=recv_sems.at[3 * t + j],
                    device_id=(px, py, c), device_id_type=MESH)
                cp.wait_send()
                cp.wait_recv()
        for cp in mine:
            cp.wait()

    thru = [pltpu.HBM(s.shape, s.dtype) for s in sums] * 2
    res = pl.pallas_call(
        body, name=name, out_shape=tuple(thru), in_specs=[HBM] * (2 * nt) + [SEM, SEM, ANY],
        out_specs=tuple([HBM] * (2 * nt)), input_output_aliases={t: t for t in range(2 * nt)},
        scratch_shapes=[pltpu.SemaphoreType.DMA((nt,))], compiler_params=SPLIT_COPY,
    )(*sums, *lands, sems[0], sems[1], after)
    return list(res[nt:])


def _sum_slots(name, slots, n):
    _, R, C = slots.shape
    br = _first_divisor(R, (512, 384, 256, 128, 64, 32, 16, 8))

    def body(s_ref, o_ref):
        acc = s_ref[0].astype(F32)
        for k in range(1, n):
            acc = acc + s_ref[k].astype(F32)
        o_ref[...] = acc

    return pl.pallas_call(
        body, name=name, grid=(R // br,), in_specs=[pl.BlockSpec((n, br, C), lambda i: (0, i, 0))],
        out_specs=pl.BlockSpec((br, C), lambda i: (i, 0)), out_shape=jax.ShapeDtypeStruct((R, C), F32),
        compiler_params=_cparams(("parallel",)),
    )(slots)


def _reduce_scatter_start(tag, names, grads):
    lays = [LAYOUTS[n] for n in names]
    landed = _pair_exchange("grads_pair_" + names[0], grads, lays)
    sums = [_pair_sum("grads_pairsum_" + n, g, ld, lay, BF16) for n, g, ld, lay in zip(names, grads, landed, lays)]
    sems, sums, lands = _chip_start(tag + "_chips_start", sums)
    return tag, names, sems, sums, lands


def _reduce_scatter_finish(pending, after):
    tag, names, sems, sums, lands = pending
    got = _chip_wait(tag + "_chips_wait", sems, sums, lands, after)
    return [_sum_slots("grads_sum_" + n, s, 4) for n, s in zip(names, got)]


def _adamw_math(w, g, m, v):
    m = ADAM_B1 * m + (1.0 - ADAM_B1) * g
    v = ADAM_B2 * v + (1.0 - ADAM_B2) * jnp.square(g)
    m_hat = m / (1.0 - ADAM_B1 ** ADAM_STEP)
    v_hat = v / (1.0 - ADAM_B2 ** ADAM_STEP)
    delta = -ADAM_LR * (m_hat / (jnp.sqrt(v_hat) + ADAM_EPS) + ADAM_WD * w)
    return delta, m, v


def _adamw(name, w, g, m, v):
    shape = w.shape
    cols = shape[-1]
    rows = int(np.prod(shape[:-1]))
    br = _first_divisor(rows, (512, 352, 256, 128, 64, 32, 16, 8))
    args = [_In(a.reshape(rows, cols)) for a in (w, g, m, v)]
    outs = _rowwise(name, _adamw_math, args, [_Out(cols), _Out(cols), _Out(cols)], rows, br)
    return [o.reshape(shape) for o in outs]


GROUPS = {"ffn1": ("ffn1_w_gate", "ffn1_w_up", "ffn1_w_down"),
          "mix": ("w_in", "w_branch_attn", "w_branch_mlstm", "w_out"),
          "ffn2": ("ffn2_w_gate", "ffn2_w_up", "ffn2_w_down")}


def _small_params(small, conv_w, l):
    p = {}
    for n in ("ffn1_norm", "mix_norm", "ffn2_norm", "block_out_norm", "mlstm_out_norm", "attn_q_norm", "attn_k_norm"):
        p[n] = small[n][l][None, :]
    p["attn_sink"] = small["attn_sink"][l]
    p["gate_bias"] = jnp.pad(small["mlstm_gate_bias"][l], (0, LANES - MLSTM_N_GATES))[None, :]
    taps = _qk_perm_cols(conv_w[l], 1)
    conv_b = _qk_perm_cols(small["mlstm_conv_b"][l][None, :], 1)
    p["conv_w8"] = jnp.concatenate([taps, conv_b, jnp.zeros((4, 2 * MLSTM_WIDTH), F32)], axis=0)
    return p


def _w_in_from_slots(slots):
    w_in = slots.reshape(N_DEV, D_MODEL, IN_WIDTH // N_DEV).transpose(1, 0, 2).reshape(D_MODEL, IN_WIDTH)
    return _w_in_arrange(w_in)


def _w_in_to_slots(g):
    return _w_in_restore(g).reshape(D_MODEL, N_DEV, IN_WIDTH // N_DEV).transpose(1, 0, 2).reshape(
        N_DEV * D_MODEL, IN_WIDTH // N_DEV)


def _local_step(x, positions, target, weights_of, small, conv_w, on_grads):
    B, S, _ = x.shape
    T = B * S
    pos = positions.reshape(T, 1)
    pos_q = jnp.repeat(pos, ATT_HEADS, axis=0)
    pos_k = jnp.repeat(pos, ATT_KV_HEADS, axis=0)
    params = [_small_params(small, conv_w, l) for l in range(DEPTH)]
    xs = x.reshape(T, D_MODEL)
    tgt = target.reshape(T, D_MODEL)

    saved = []
    for l, p in enumerate(params):
        p.update(weights_of(l, "ffn1", xs))
        x1, s1 = _ffn_fwd("ffn1", xs, p["ffn1_norm"], p["ffn1_w_gate"], p["ffn1_w_up"], p["ffn1_w_down"])
        p.update(weights_of(l, "mix", x1))
        p["w_in"] = _w_in_from_slots(p["w_in"])
        x2, s2 = _mix_fwd(x1, pos_q, pos_k, B, S, p)
        p.update(weights_of(l, "ffn2", x2))
        x3, s3 = _ffn_fwd("ffn2", x2, p["ffn2_norm"], p["ffn2_w_gate"], p["ffn2_w_up"], p["ffn2_w_down"])
        saved.append((s1, s2, s3, x3))
        if l + 1 < DEPTH:
            xs = _block_norm_fwd(x3, p["block_out_norm"])

    sm = {n: [None] * DEPTH for n in SMALL + ("mlstm_conv_w",)}
    loss = None
    dx = None
    for l in reversed(range(DEPTH)):
        p = params[l]
        s1, s2, s3, x3 = saved[l]
        if l == DEPTH - 1:
            loss, dx, dgn = _loss_and_grad(x3, p["block_out_norm"], tgt)
        else:
            dx, dgn = _block_norm_bwd(x3, p["block_out_norm"], dx)
        sm["block_out_norm"][l] = dgn[0]
        dx, dg, dwg, dwu, dwd = _ffn_bwd("ffn2", s3, p["ffn2_norm"], p["ffn2_w_gate"], p["ffn2_w_up"], p["ffn2_w_down"], dx)
        sm["ffn2_norm"][l] = dg[0]
        on_grads(l, "ffn2", {"ffn2_w_gate": dwg, "ffn2_w_up": dwu, "ffn2_w_down": dwd}, dx)
        dx, g = _mix_bwd(s2, pos_q, pos_k, B, S, p, dx)
        on_grads(l, "mix", {"w_in": _w_in_to_slots(g["w_in"]), "w_branch_attn": g["w_branch_attn"],
                            "w_branch_mlstm": g["w_branch_mlstm"], "w_out": g["w_out"]}, dx)
        dconv = _qk_unperm_cols(g["conv_w8"], 1)
        sm["mlstm_conv_w"][l] = dconv[0:3]
        sm["mlstm_conv_b"][l] = dconv[3]
        sm["mix_norm"][l] = g["mix_norm"][0]
        sm["mlstm_gate_bias"][l] = g["gate_bias"][0, :MLSTM_N_GATES]
        sm["attn_q_norm"][l], sm["attn_k_norm"][l] = g["attn_q_norm"][0], g["attn_k_norm"][0]
        sm["attn_sink"][l] = g["attn_sink"][0]
        sm["mlstm_out_norm"][l] = g["mlstm_out_norm"][0]
        dx, dg, dwg, dwu, dwd = _ffn_bwd("ffn1", s1, p["ffn1_norm"], p["ffn1_w_gate"], p["ffn1_w_up"], p["ffn1_w_down"], dx)
        sm["ffn1_norm"][l] = dg[0]
        on_grads(l, "ffn1", {"ffn1_w_gate": dwg, "ffn1_w_up": dwu, "ffn1_w_down": dwd}, dx)
    sm = {n: jnp.stack(v, axis=0) for n, v in sm.items()}
    return loss, dx.reshape(B, S, D_MODEL), sm


def kernel(x, positions, ffn1_norm, ffn1_w_gate, ffn1_w_up, ffn1_w_down, mix_norm, w_in, mlstm_gate_bias, attn_q_norm, attn_k_norm, attn_sink, mlstm_conv_w, mlstm_conv_b, mlstm_out_norm, w_branch_attn, w_branch_mlstm, w_out, ffn2_norm, ffn2_w_gate, ffn2_w_up, ffn2_w_down, block_out_norm, loss_target, m_ffn1_norm, m_ffn1_w_gate, m_ffn1_w_up, m_ffn1_w_down, m_mix_norm, m_w_in, m_mlstm_gate_bias, m_attn_q_norm, m_attn_k_norm, m_attn_sink, m_mlstm_conv_w, m_mlstm_conv_b, m_mlstm_out_norm, m_w_branch_attn, m_w_branch_mlstm, m_w_out, m_ffn2_norm, m_ffn2_w_gate, m_ffn2_w_up, m_ffn2_w_down, m_block_out_norm, v_ffn1_norm, v_ffn1_w_gate, v_ffn1_w_up, v_ffn1_w_down, v_mix_norm, v_w_in, v_mlstm_gate_bias, v_attn_q_norm, v_attn_k_norm, v_attn_sink, v_mlstm_conv_w, v_mlstm_conv_b, v_mlstm_out_norm, v_w_branch_attn, v_w_branch_mlstm, v_w_out, v_ffn2_norm, v_ffn2_w_gate, v_ffn2_w_up, v_ffn2_w_down, v_block_out_norm):
    args = locals()
    w = {n: args[n] for n in WEIGHTS}
    m = {n: args["m_" + n] for n in WEIGHTS}
    v = {n: args["v_" + n] for n in WEIGHTS}

    order = [(l, grp) for l in range(DEPTH) for grp in ("ffn1", "mix", "ffn2")]
    keys = [(l, n) for l, grp in order for n in GROUPS[grp]]
    lays = [LAYOUTS[n] for _, n in keys]
    shards = [lay.pad(w[n][l].astype(BF16)) for (l, n), lay in zip(keys, lays)]
    group_idx, at = {}, 0
    for l, grp in order:
        group_idx[(l, grp)] = list(range(at, at + len(GROUPS[grp])))
        at += len(GROUPS[grp])
    conv_shape = w["mlstm_conv_w"].shape
    conv_all = _all_gather("conv_all_gather", _pack_flat([w["mlstm_conv_w"]], F32, 8), vmem=True)
    conv_parts = _unpack_flat(conv_all, [conv_shape], lead=(N_DEV,))[0]
    conv_w = jnp.concatenate([conv_parts[j] for j in range(N_DEV)], axis=2)
    small = {n: w[n] for n in SMALL}

    sems, shards, lands = _gather_start("weights_gather_start", shards, lays, [group_idx[k] for k in order], conv_all)

    def weights_of(l, grp, after):
        idx = group_idx[(l, grp)]
        whole = _gather_wait(f"weights_gather_wait_{l}_{grp}", sems[order.index((l, grp))], [shards[i] for i in idx],
                             [lands[i] for i in idx], [lays[i] for i in idx], after)
        return dict(zip(GROUPS[grp], whole))

    totals, pending = {}, []

    def finish(after):
        tag, names = pending[0][0], pending[0][1]
        for n, t in zip(names, _reduce_scatter_finish(pending.pop(0), after)):
            totals[(tag, n)] = t

    def on_grads(l, grp, g, after):
        if pending:
            finish(after)
        names = GROUPS[grp]
        pending.append(_reduce_scatter_start(f"grads_{l}_{grp}", names, [g[n] for n in names]))

    loss, grad_x, small_g = _local_step(x, positions, loss_target, weights_of, small, conv_w, on_grads)
    finish(pending[0][3][0])
    grads = {}
    for grp, names in GROUPS.items():
        for n in names:
            grads[n] = jnp.stack([LAYOUTS[n].unpad(totals[(f"grads_{l}_{grp}", n)]) for l in range(DEPTH)], axis=0)

    small_names = SMALL + ("mlstm_conv_w",)
    small_shapes = [small_g[n].shape for n in small_names] + [(1, 1)]
    small_packed = _pack_flat([small_g[n] for n in small_names] + [loss], F32, 8)
    small_all = _all_gather("small_all_gather", small_packed, vmem=True)
    small_sum = _sum_slots("small_sum", small_all, N_DEV)
    *small_grads, loss_total = _unpack_flat(small_sum, small_shapes)
    grads.update(dict(zip(small_names, small_grads)))
    x_pos, y_pos, c_pos = _mesh_pos()
    grads["mlstm_conv_w"] = lax.dynamic_slice_in_dim(
        grads["mlstm_conv_w"], (4 * x_pos + 2 * y_pos + c_pos) * conv_shape[2], conv_shape[2], axis=2)

    deltas, new_m, new_v = {}, {}, {}
    for n in BIG:
        deltas[n], new_m[n], new_v[n] = _adamw("adamw_" + n, w[n], grads[n], m[n], v[n])
    sw, sg, smm, sv = (_pack_flat([d[n] for n in SMALL], F32, 8) for d in (w, grads, m, v))
    sd, snm, snv = _adamw("adamw_small", sw, sg, smm, sv)
    shapes = [w[n].shape for n in SMALL]
    for d, buf in ((deltas, sd), (new_m, snm), (new_v, snv)):
        d.update(dict(zip(SMALL, _unpack_flat(buf, shapes))))

    return (loss_total.reshape(()), grad_x, *[grads[n] for n in WEIGHTS], *[deltas[n] for n in WEIGHTS],
            *[new_m[n] for n in WEIGHTS], *[new_v[n] for n in WEIGHTS])
```

```python
import functools

import numpy as np
import jax
import jax.numpy as jnp
from jax import lax
from jax.experimental import pallas as pl
from jax.experimental.pallas import tpu as pltpu

F32 = jnp.float32
BF16 = jnp.bfloat16

D_MODEL = 1024
D_FF = 2816
ATT_HEAD_DIM = 64
ATT_HEADS = 8
ATT_KV_HEADS = 2
ATT_GROUP = ATT_HEADS // ATT_KV_HEADS
ATT_WIDTH = ATT_HEADS * ATT_HEAD_DIM
ATT_KV_WIDTH = ATT_KV_HEADS * ATT_HEAD_DIM
WINDOW = 128
ATT_BLOCK = 128
ROPE_DIM = 16
ROPE_THETA = 500000.0
MLSTM_HEADS = 4
MLSTM_HEAD_DIM = 128
MLSTM_WIDTH = MLSTM_HEADS * MLSTM_HEAD_DIM
MLSTM_CHUNK = 128
MLSTM_N_GATES = 4 * MLSTM_HEADS
NORM_EPS = 1e-6
IN_WIDTH = 4880
DEPTH = 2
N_DEV = 8

ADAM_LR = 0.001
ADAM_B1 = 0.9
ADAM_B2 = 0.999
ADAM_EPS = 1e-08
ADAM_WD = 0.01
ADAM_STEP = 10

LANES = 128
C_GMERGE = 0
C_QK = 2048
C_VM = 3072
C_OM = 3584
C_QA = 4096
C_KA = 4608
C_VA = 4736
C_GATES = 4864
IN_PAD = 4992

VMEM_LIMIT = 48 * 1024 * 1024

MESH = pl.DeviceIdType.MESH


def _cparams(sem):
    return pltpu.CompilerParams(dimension_semantics=sem, vmem_limit_bytes=VMEM_LIMIT)


def _first_divisor(n, cands):
    for c in cands:
        if n % c == 0:
            return c
    return n


_NN = ((1,), (0,))
_NT = ((1,), (1,))
_TN = ((0,), (0,))


def _mm(a, b, dims):
    return lax.dot_general(a.astype(BF16), b.astype(BF16), (dims, ((), ())), preferred_element_type=F32)


@jax.custom_vjp
def mm_nn(a, b):
    return _mm(a, b, _NN)


def _mm_nn_fwd(a, b):
    return _mm(a, b, _NN), (a, b)


def _mm_nn_bwd(res, g):
    a, b = res
    return _mm(g, b, _NT).astype(a.dtype), _mm(a, g, _TN).astype(b.dtype)


mm_nn.defvjp(_mm_nn_fwd, _mm_nn_bwd)


@jax.custom_vjp
def mm_nt(a, b):
    return _mm(a, b, _NT)


def _mm_nt_fwd(a, b):
    return _mm(a, b, _NT), (a, b)


def _mm_nt_bwd(res, g):
    a, b = res
    return _mm(g, b, _NN).astype(a.dtype), _mm(g, a, _TN).astype(b.dtype)


mm_nt.defvjp(_mm_nt_fwd, _mm_nt_bwd)


@jax.custom_vjp
def mm_tn(a, b):
    return _mm(a, b, _TN)


def _mm_tn_fwd(a, b):
    return _mm(a, b, _TN), (a, b)


def _mm_tn_bwd(res, g):
    a, b = res
    return _mm(b, g, _NT).astype(a.dtype), _mm(a, g, _NN).astype(b.dtype)


mm_tn.defvjp(_mm_tn_fwd, _mm_tn_bwd)


def _matmul(name, a, b, mode, out_dtype=F32, res=None, scale=1.0, bl=None):
    b_shape = b.shape if bl is None else b.shape[1:]
    if mode == "nn":
        (M, K), (K2, N) = a.shape, b_shape
    elif mode == "nt":
        (M, K), (N, K2) = a.shape, b_shape
    else:
        (K, M), (K2, N) = a.shape, b_shape
    assert K == K2, (name, a.shape, b.shape)
    tm = _first_divisor(M, (1024, 1408, 512, 384, 256, 128))
    tn = _first_divisor(N, (1024, 512, 384, 256, 128))
    tk = _first_divisor(K, (1024, 1408, 1664, 512, 256, 128))
    nk = K // tk
    if mode == "tn":
        a_spec = pl.BlockSpec((tk, tm), lambda i, j, k: (k, i))
    else:
        a_spec = pl.BlockSpec((tm, tk), lambda i, j, k: (i, k))
    if mode == "nt":
        b_blk, b_idx = (tn, tk), (lambda i, j, k: (j, k))
    else:
        b_blk, b_idx = (tk, tn), (lambda i, j, k: (k, j))
    if bl is None:
        b_spec = pl.BlockSpec(b_blk, b_idx)
    else:
        b_spec = pl.BlockSpec((None,) + b_blk, lambda i, j, k: (bl,) + b_idx(i, j, k))
    o_spec = pl.BlockSpec((tm, tn), lambda i, j, k: (i, j))
    dims = {"nn": _NN, "nt": _NT, "tn": _TN}[mode]
    has_res = res is not None

    def body(*refs):
        if has_res:
            a_ref, b_ref, r_ref, o_ref, acc = refs
        else:
            a_ref, b_ref, o_ref, acc = refs
        k = pl.program_id(2)

        @pl.when(k == 0)
        def _():
            acc[...] = jnp.zeros_like(acc)

        acc[...] += _mm(a_ref[...], b_ref[...], dims)

        @pl.when(k == nk - 1)
        def _():
            out = acc[...]
            if scale != 1.0:
                out = out * scale
            if has_res:
                out = r_ref[...].astype(F32) + out
            o_ref[...] = out.astype(out_dtype)

    in_specs = [a_spec, b_spec] + ([o_spec] if has_res else [])
    args = (a, b) + ((res,) if has_res else ())
    return pl.pallas_call(
        body, name=name, grid=(M // tm, N // tn, nk), in_specs=in_specs, out_specs=o_spec,
        out_shape=jax.ShapeDtypeStruct((M, N), out_dtype), scratch_shapes=[pltpu.VMEM((tm, tn), F32)],
        compiler_params=_cparams(("parallel", "parallel", "arbitrary")),
    )(*args)


class _In:
    def __init__(self, arr, width=None, base=0, split=False, rows=True):
        self.arr, self.base, self.split, self.rows = arr, base, split, rows
        self.width = arr.shape[1] if width is None else width


class _Out:
    def __init__(self, cols, dtype=F32, width=None, split=False, rows=True, nrows=1):
        self.cols, self.dtype, self.split, self.rows, self.nrows = cols, dtype, split, rows, nrows
        self.width = cols if width is None else width


def _rowwise(name, fn, ins, outs, n_rows, br, ncol=1):
    br = min(br, n_rows)
    assert n_rows % br == 0, (name, n_rows, br)
    nrow_blocks = n_rows // br

    def in_spec(d):
        nb = br if d.rows else d.arr.shape[0]
        if d.rows and d.split:
            im = lambda j, i, base=d.base: (i, base + j)
        elif d.rows:
            im = lambda j, i, base=d.base: (i, base)
        elif d.split:
            im = lambda j, i, base=d.base: (0, base + j)
        else:
            im = lambda j, i, base=d.base: (0, base)
        return pl.BlockSpec((nb, d.width), im)

    def out_spec(d):
        nb = br if d.rows else d.nrows
        if d.rows and d.split:
            im = lambda j, i: (i, j)
        elif d.rows:
            im = lambda j, i: (i, 0)
        elif d.split:
            im = lambda j, i: (0, j)
        else:
            im = lambda j, i: (0, 0)
        return pl.BlockSpec((nb, d.width), im)

    n_in = len(ins)

    def body(*refs):
        i = pl.program_id(1)
        vals = [r[...] for r in refs[:n_in]]
        res = fn(*vals)
        if not isinstance(res, (tuple, list)):
            res = (res,)
        for d, ref, val in zip(outs, refs[n_in:], res):
            if d.rows:
                ref[...] = val.astype(d.dtype)
            else:
                @pl.when(i == 0)
                def _(ref=ref):
                    ref[...] = jnp.zeros_like(ref)

                ref[...] += val.astype(d.dtype)

    out_shape = [jax.ShapeDtypeStruct((n_rows if d.rows else d.nrows, d.cols), d.dtype) for d in outs]
    res = pl.pallas_call(
        body, name=name, grid=(ncol, nrow_blocks), in_specs=[in_spec(d) for d in ins],
        out_specs=[out_spec(d) for d in outs], out_shape=out_shape,
        compiler_params=_cparams(("parallel", "arbitrary")),
    )(*[d.arr for d in ins])
    return res


def _rms(x, g):
    return x * lax.rsqrt(jnp.mean(x * x, axis=-1, keepdims=True) + NORM_EPS) * g


def _sigmoid(x):
    return 1.0 / (1.0 + jnp.exp(-x))


def _silu(x):
    return x * _sigmoid(x)


def _log_sigmoid(x):
    return jnp.minimum(x, 0.0) - jnp.log(1.0 + jnp.exp(-jnp.abs(x)))


def _rope_tables(pos, inv_freq_row):
    ang = pos.astype(F32) * inv_freq_row
    return jnp.cos(ang), jnp.sin(ang)


def _qk_prep(t, g, cos, sin, rot_mat):
    y = _rms(t, g)
    rot = lax.dot_general(y, rot_mat, (_NN, ((), ())), precision=lax.Precision.HIGHEST, preferred_element_type=F32)
    return y * cos + rot * sin


def _attn_head(q, kb, vb, sink, valid):
    s = mm_nt(q, kb) * (ATT_HEAD_DIM ** -0.5)
    s = jnp.where(valid, s, -jnp.inf)
    m = jnp.maximum(jnp.max(s, axis=-1, keepdims=True), sink)
    p = jnp.exp(s - m)
    den = jnp.sum(p, axis=-1, keepdims=True) + jnp.exp(sink - m)
    return mm_nn(p / den, vb)


def _mlstm_chunk(q, k, v, li, lf_pre, C, n, m, incl, incl_t, eye):
    k = k * (MLSTM_HEAD_DIM ** -0.5)
    lf = _log_sigmoid(lf_pre)
    lf_row = jnp.sum(eye * lf, axis=0, keepdims=True)
    li_row = jnp.sum(eye * li, axis=0, keepdims=True)
    b = jnp.sum(incl * lf_row, axis=1, keepdims=True)
    b_row = jnp.sum(incl_t * lf, axis=0, keepdims=True)
    b_tot = jnp.sum(lf, axis=0, keepdims=True)
    a = b_tot - b + li
    a_max = jnp.max(a, axis=0, keepdims=True)
    kw = k * jnp.exp(a - a_max)
    c_loc = mm_tn(kw, v)
    n_loc = jnp.sum(kw, axis=0, keepdims=True)

    dmat = jnp.where(incl > 0.5, b - b_row + li_row, -jnp.inf)
    inter = b + m
    m_t = jnp.maximum(inter, jnp.max(dmat, axis=1, keepdims=True))
    sc = mm_nt(q, k) * jnp.exp(dmat - m_t)
    scale_in = jnp.exp(inter - m_t)
    num = mm_nn(sc, v) + scale_in * mm_nn(q, C)
    den = jnp.sum(sc, axis=1, keepdims=True) + scale_in * jnp.sum(q * n, axis=1, keepdims=True)
    h = num / jnp.maximum(jnp.abs(den), jnp.exp(-m_t))

    m_new = jnp.maximum(b_tot + m, a_max)
    s_p = jnp.exp(b_tot + m - m_new)
    s_l = jnp.exp(a_max - m_new)
    return h, s_p * C + s_l * c_loc, s_p * n + s_l * n_loc, m_new


def _mlstm_combine(hf, hb, o_pre, g):
    h = hf + hb
    mu = jnp.mean(h, axis=-1, keepdims=True)
    var = jnp.mean(jnp.square(h - mu), axis=-1, keepdims=True)
    return _sigmoid(o_pre) * ((h - mu) * lax.rsqrt(var + NORM_EPS) * g)


def _merge(ga, gm, za, zm):
    return _sigmoid(ga) * za + _sigmoid(gm) * zm


def _attn_mask(n, seq):
    qi = n * ATT_BLOCK + lax.broadcasted_iota(jnp.int32, (ATT_BLOCK, 3 * ATT_BLOCK), 0)
    kj = (n - 1) * ATT_BLOCK + lax.broadcasted_iota(jnp.int32, (ATT_BLOCK, 3 * ATT_BLOCK), 1)
    return (jnp.abs(qi - kj) <= WINDOW) & (kj >= 0) & (kj < seq)


def _attn_specs(nq):
    q_spec = pl.BlockSpec((1, ATT_GROUP, ATT_BLOCK, ATT_HEAD_DIM), lambda h, b, n: (b, h, n, 0))

    def kv_spec(off):
        return pl.BlockSpec((1, 1, ATT_BLOCK, ATT_HEAD_DIM),
                            lambda h, b, n: (b, h, jnp.clip(n + off, 0, nq - 1), 0))

    sink_spec = pl.BlockSpec((1, ATT_GROUP, 1, 1), lambda h, b, n: (h, 0, 0, 0))
    return q_spec, kv_spec, sink_spec


def _attn_fwd(q, k, v, sink):
    B, _, S, _ = q.shape
    nq = S // ATT_BLOCK
    q_spec, kv_spec, sink_spec = _attn_specs(nq)

    def body(q_ref, kp, kc, kn, vp, vc, vn, s_ref, o_ref):
        valid = _attn_mask(pl.program_id(2), S)
        kb = jnp.concatenate([kp[0, 0], kc[0, 0], kn[0, 0]], axis=0)
        vb = jnp.concatenate([vp[0, 0], vc[0, 0], vn[0, 0]], axis=0)
        for g in range(ATT_GROUP):
            o_ref[0, g] = _attn_head(q_ref[0, g], kb, vb, s_ref[0, g], valid).astype(BF16)

    return pl.pallas_call(
        body, name="attn_fwd", grid=(ATT_KV_HEADS, B, nq),
        in_specs=[q_spec, kv_spec(-1), kv_spec(0), kv_spec(1), kv_spec(-1), kv_spec(0), kv_spec(1), sink_spec],
        out_specs=q_spec, out_shape=jax.ShapeDtypeStruct(q.shape, BF16),
        compiler_params=_cparams(("parallel", "parallel", "arbitrary")),
    )(q, k, k, k, v, v, v, sink)


def _attn_bwd(q, k, v, sink, dy):
    B, _, S, _ = q.shape
    nq = S // ATT_BLOCK
    q_spec, kv_spec, sink_spec = _attn_specs(nq)
    kv_full = pl.BlockSpec((1, 1, S, ATT_HEAD_DIM), lambda h, b, n: (b, h, 0, 0))

    def body(q_ref, kp, kc, kn, vp, vc, vn, s_ref, dy_ref, dq_ref, dk_ref, dv_ref, ds_ref):
        b, n = pl.program_id(1), pl.program_id(2)
        valid = _attn_mask(n, S)
        kb = jnp.concatenate([kp[0, 0], kc[0, 0], kn[0, 0]], axis=0)
        vb = jnp.concatenate([vp[0, 0], vc[0, 0], vn[0, 0]], axis=0)

        @pl.when(n == 0)
        def _():
            dk_ref[...] = jnp.zeros_like(dk_ref)
            dv_ref[...] = jnp.zeros_like(dv_ref)

        @pl.when((n == 0) & (b == 0))
        def _():
            ds_ref[...] = jnp.zeros_like(ds_ref)

        dkb = jnp.zeros_like(kb)
        dvb = jnp.zeros_like(vb)
        for g in range(ATT_GROUP):
            _, vjp = jax.vjp(functools.partial(_attn_head, valid=valid), q_ref[0, g], kb, vb, s_ref[0, g])
            dq, dk_g, dv_g, dsink = vjp(dy_ref[0, g])
            dq_ref[0, g] = dq
            ds_ref[0, g] += dsink
            dkb += dk_g
            dvb += dv_g
        for j, off in enumerate((-1, 0, 1)):
            start = pl.multiple_of(jnp.clip(n + off, 0, nq - 1) * ATT_BLOCK, ATT_BLOCK)
            rows = pl.ds(start, ATT_BLOCK)
            dk_ref[0, 0, rows, :] += dkb[j * ATT_BLOCK:(j + 1) * ATT_BLOCK]
            dv_ref[0, 0, rows, :] += dvb[j * ATT_BLOCK:(j + 1) * ATT_BLOCK]

    return pl.pallas_call(
        body, name="attn_bwd", grid=(ATT_KV_HEADS, B, nq),
        in_specs=[q_spec, kv_spec(-1), kv_spec(0), kv_spec(1), kv_spec(-1), kv_spec(0), kv_spec(1), sink_spec, q_spec],
        out_specs=[q_spec, kv_full, kv_full, sink_spec],
        out_shape=[jax.ShapeDtypeStruct(q.shape, F32), jax.ShapeDtypeStruct(k.shape, F32),
                   jax.ShapeDtypeStruct(v.shape, F32), jax.ShapeDtypeStruct(sink.shape, F32)],
        compiler_params=_cparams(("arbitrary", "arbitrary", "arbitrary")),
    )(q, k, k, k, v, v, v, sink, dy)


CONV_COLS = 256


def _conv_taps(u, seq):
    row = lax.broadcasted_iota(jnp.int32, u.shape, 0)
    prev = jnp.where(row == 0, 0.0, pltpu.roll(u, 1, axis=0))
    nxt = jnp.where(row == seq - 1, 0.0, pltpu.roll(u, seq - 1, axis=0))
    return prev, nxt


def _conv_fwd(proj3, w8):
    B, S, _ = proj3.shape
    ncb = 2 * MLSTM_WIDTH // CONV_COLS

    def body(u_ref, w_ref, o_ref):
        u = u_ref[0]
        prev, nxt = _conv_taps(u, S)
        o_ref[0] = _silu(prev * w_ref[0:1, :] + u * w_ref[1:2, :] + nxt * w_ref[2:3, :] + w_ref[3:4, :])

    return pl.pallas_call(
        body, name="conv_fwd", grid=(B, ncb),
        in_specs=[pl.BlockSpec((1, S, CONV_COLS), lambda b, c: (b, 0, C_QK // CONV_COLS + c)),
                  pl.BlockSpec((8, CONV_COLS), lambda b, c: (0, c))],
        out_specs=pl.BlockSpec((1, S, CONV_COLS), lambda b, c: (b, 0, c)),
        out_shape=jax.ShapeDtypeStruct((B, S, 2 * MLSTM_WIDTH), F32),
        compiler_params=_cparams(("parallel", "parallel")),
    )(proj3, w8)


def _conv_bwd(proj3, w8, dout_f, dout_b):
    B, S, _ = proj3.shape
    ncb = 2 * MLSTM_WIDTH // CONV_COLS

    def body(u_ref, w_ref, df_ref, db_ref, du_ref, dw_ref):
        b = pl.program_id(1)
        u = u_ref[0]
        prev, nxt = _conv_taps(u, S)
        w0, w1, w2 = w_ref[0:1, :], w_ref[1:2, :], w_ref[2:3, :]
        pre = prev * w0 + u * w1 + nxt * w2 + w_ref[3:4, :]
        sig = _sigmoid(pre)
        dpre = (df_ref[0] + db_ref[0]) * (sig * (1.0 + pre * (1.0 - sig)))
        dprev, dnxt = _conv_taps(dpre, S)
        du_ref[0] = dnxt * w0 + dpre * w1 + dprev * w2

        @pl.when(b == 0)
        def _():
            dw_ref[...] = jnp.zeros_like(dw_ref)

        dw_ref[0:1, :] += jnp.sum(dpre * prev, axis=0, keepdims=True)
        dw_ref[1:2, :] += jnp.sum(dpre * u, axis=0, keepdims=True)
        dw_ref[2:3, :] += jnp.sum(dpre * nxt, axis=0, keepdims=True)
        dw_ref[3:4, :] += jnp.sum(dpre, axis=0, keepdims=True)

    blk = pl.BlockSpec((1, S, CONV_COLS), lambda c, b: (b, 0, c))
    return pl.pallas_call(
        body, name="conv_bwd", grid=(ncb, B),
        in_specs=[pl.BlockSpec((1, S, CONV_COLS), lambda c, b: (b, 0, C_QK // CONV_COLS + c)),
                  pl.BlockSpec((8, CONV_COLS), lambda c, b: (0, c)), blk, blk],
        out_specs=[blk, pl.BlockSpec((8, CONV_COLS), lambda c, b: (0, c))],
        out_shape=[jax.ShapeDtypeStruct((B, S, 2 * MLSTM_WIDTH), F32), jax.ShapeDtypeStruct((8, 2 * MLSTM_WIDTH), F32)],
        compiler_params=_cparams(("parallel", "arbitrary")),
    )(proj3, w8, dout_f, dout_b)


def _chunk_masks(direction):
    t = lax.broadcasted_iota(jnp.int32, (MLSTM_CHUNK, MLSTM_CHUNK), 0)
    s = lax.broadcasted_iota(jnp.int32, (MLSTM_CHUNK, MLSTM_CHUNK), 1)
    le, ge = (s <= t).astype(F32), (s >= t).astype(F32)
    eye = (s == t).astype(F32)
    return (le, ge, eye) if direction == 0 else (ge, le, eye)


def _gate_cols(gates, direction, head):
    lane = lax.broadcasted_iota(jnp.int32, gates.shape, 1)
    sel_i = (lane == (2 * direction) * MLSTM_HEADS + head).astype(F32)
    sel_f = (lane == (2 * direction + 1) * MLSTM_HEADS + head).astype(F32)
    return sel_i, sel_f


def _mlstm_fwd(qk, proj3, bias):
    B, S, _ = qk.shape
    nc = S // MLSTM_CHUNK
    H, L, DH = MLSTM_HEADS, MLSTM_CHUNK, MLSTM_HEAD_DIM

    def chunk_of(d, c):
        return c if d == 0 else nc - 1 - c

    def body(qkf, qkb, vf, vb, gf, gb, bias_ref, hf, hb, csf, csb, nsf, nsb, msf, msb, c_st, n_st, m_st):
        c, h = pl.program_id(1), pl.program_id(2)

        @pl.when(c == 0)
        def _():
            for d in range(2):
                c_st[d, h] = jnp.zeros((DH, DH), F32)
                n_st[d, h] = jnp.zeros((1, DH), F32)
                m_st[d, h] = jnp.zeros((1, DH), F32)

        for d, (qk_ref, v_ref, g_ref, h_ref, cs, ns, ms) in enumerate(
                ((qkf, vf, gf, hf, csf, nsf, msf), (qkb, vb, gb, hb, csb, nsb, msb))):
            incl, incl_t, eye = _chunk_masks(d)
            gates = g_ref[0] + bias_ref[...]
            sel_i, sel_f = _gate_cols(gates, d, h)
            li = jnp.sum(gates * sel_i, axis=1, keepdims=True)
            lf_pre = jnp.sum(gates * sel_f, axis=1, keepdims=True)
            c_in, n_in, m_in = c_st[d, h], n_st[d, h], m_st[d, h]
            cs[0, 0, 0], ns[0, 0, 0], ms[0, 0, 0] = c_in, n_in, m_in
            hh, c_new, n_new, m_new = _mlstm_chunk(
                qk_ref[0, :, :DH], qk_ref[0, :, DH:], v_ref[0], li, lf_pre, c_in, n_in,
                jnp.max(m_in, axis=1, keepdims=True), incl, incl_t, eye)
            h_ref[0] = hh
            c_st[d, h], n_st[d, h] = c_new, n_new
            m_st[d, h] = jnp.broadcast_to(m_new, (1, DH))

    def tok_spec(width, base, d, per_head):
        return pl.BlockSpec((1, L, width), lambda b, c, h: (b, chunk_of(d, c), base + (h if per_head else 0)))

    def st_spec(shape, d):
        return pl.BlockSpec((1, 1, 1) + shape, lambda b, c, h: (b, chunk_of(d, c), h, 0, 0))

    in_specs = [tok_spec(2 * DH, 0, 0, True), tok_spec(2 * DH, 0, 1, True),
                tok_spec(DH, C_VM // DH, 0, True), tok_spec(DH, C_VM // DH, 1, True),
                tok_spec(LANES, C_GATES // LANES, 0, False), tok_spec(LANES, C_GATES // LANES, 1, False),
                pl.BlockSpec((1, LANES), lambda b, c, h: (0, 0))]
    out_specs = [tok_spec(DH, 0, 0, True), tok_spec(DH, 0, 1, True),
                 st_spec((DH, DH), 0), st_spec((DH, DH), 1), st_spec((1, DH), 0), st_spec((1, DH), 1),
                 st_spec((1, DH), 0), st_spec((1, DH), 1)]
    hs = jax.ShapeDtypeStruct((B, S, H * DH), F32)
    cs = jax.ShapeDtypeStruct((B, nc, H, DH, DH), F32)
    vs = jax.ShapeDtypeStruct((B, nc, H, 1, DH), F32)
    return pl.pallas_call(
        body, name="mlstm_fwd", grid=(B, nc, H), in_specs=in_specs, out_specs=out_specs,
        out_shape=[hs, hs, cs, cs, vs, vs, vs, vs],
        scratch_shapes=[pltpu.VMEM((2, H, DH, DH), F32), pltpu.VMEM((2, H, 1, DH), F32), pltpu.VMEM((2, H, 1, DH), F32)],
        compiler_params=_cparams(("parallel", "arbitrary", "arbitrary")),
    )(qk, qk, proj3, proj3, proj3, proj3, bias)


def _mlstm_bwd(qk, proj3, bias, states, dh):
    B, S, _ = qk.shape
    nc = S // MLSTM_CHUNK
    H, L, DH = MLSTM_HEADS, MLSTM_CHUNK, MLSTM_HEAD_DIM

    def chunk_of(d, c):
        return nc - 1 - c if d == 0 else c

    def body(qkf, qkb, vf, vb, gf, gb, bias_ref, csf, csb, nsf, nsb, msf, msb, dhf, dhb,
             dqkf, dqkb, dvf, dvb, dgf, dgb, dc_st, dn_st, dm_st):
        c, h = pl.program_id(1), pl.program_id(2)

        @pl.when(c == 0)
        def _():
            for d in range(2):
                dc_st[d, h] = jnp.zeros((DH, DH), F32)
                dn_st[d, h] = jnp.zeros((1, DH), F32)
                dm_st[d, h] = jnp.zeros((1, DH), F32)

        @pl.when(h == 0)
        def _():
            dgf[...] = jnp.zeros_like(dgf)
            dgb[...] = jnp.zeros_like(dgb)

        for d, (qk_ref, v_ref, g_ref, cs, ns, ms, dh_ref, dqk_ref, dv_ref, dg_ref) in enumerate(
                ((qkf, vf, gf, csf, nsf, msf, dhf, dqkf, dvf, dgf), (qkb, vb, gb, csb, nsb, msb, dhb, dqkb, dvb, dgb))):
            incl, incl_t, eye = _chunk_masks(d)
            gates = g_ref[0] + bias_ref[...]
            sel_i, sel_f = _gate_cols(gates, d, h)
            li = jnp.sum(gates * sel_i, axis=1, keepdims=True)
            lf_pre = jnp.sum(gates * sel_f, axis=1, keepdims=True)
            m_in = jnp.max(ms[0, 0, 0], axis=1, keepdims=True)
            _, vjp = jax.vjp(
                functools.partial(_mlstm_chunk, incl=incl, incl_t=incl_t, eye=eye),
                qk_ref[0, :, :DH], qk_ref[0, :, DH:], v_ref[0], li, lf_pre, cs[0, 0, 0], ns[0, 0, 0], m_in)
            dm_out = jnp.max(dm_st[d, h], axis=1, keepdims=True)
            dq, dk, dv, dli, dlf, dc, dn, dm = vjp((dh_ref[0], dc_st[d, h], dn_st[d, h], dm_out))
            dqk_ref[0, :, :DH] = dq
            dqk_ref[0, :, DH:] = dk
            dv_ref[0] = dv
            dg_ref[0] += dli * sel_i + dlf * sel_f
            dc_st[d, h], dn_st[d, h] = dc, dn
            dm_st[d, h] = jnp.broadcast_to(dm, (1, DH))

    def tok_spec(width, base, d, per_head):
        return pl.BlockSpec((1, L, width), lambda b, c, h: (b, chunk_of(d, c), base + (h if per_head else 0)))

    def st_spec(shape, d):
        return pl.BlockSpec((1, 1, 1) + shape, lambda b, c, h: (b, chunk_of(d, c), h, 0, 0))

    in_specs = [tok_spec(2 * DH, 0, 0, True), tok_spec(2 * DH, 0, 1, True),
                tok_spec(DH, C_VM // DH, 0, True), tok_spec(DH, C_VM // DH, 1, True),
                tok_spec(LANES, C_GATES // LANES, 0, False), tok_spec(LANES, C_GATES // LANES, 1, False),
                pl.BlockSpec((1, LANES), lambda b, c, h: (0, 0)),
                st_spec((DH, DH), 0), st_spec((DH, DH), 1), st_spec((1, DH), 0), st_spec((1, DH), 1),
                st_spec((1, DH), 0), st_spec((1, DH), 1), tok_spec(DH, 0, 0, True), tok_spec(DH, 0, 1, True)]
    out_specs = [tok_spec(2 * DH, 0, 0, True), tok_spec(2 * DH, 0, 1, True), tok_spec(DH, 0, 0, True), tok_spec(DH, 0, 1, True),
                 tok_spec(LANES, 0, 0, False), tok_spec(LANES, 0, 1, False)]
    qks = jax.ShapeDtypeStruct((B, S, 2 * H * DH), F32)
    vs = jax.ShapeDtypeStruct((B, S, H * DH), F32)
    gs = jax.ShapeDtypeStruct((B, S, LANES), F32)
    csf, csb, nsf, nsb, msf, msb = states
    return pl.pallas_call(
        body, name="mlstm_bwd", grid=(B, nc, H), in_specs=in_specs, out_specs=out_specs,
        out_shape=[qks, qks, vs, vs, gs, gs],
        scratch_shapes=[pltpu.VMEM((2, H, DH, DH), F32), pltpu.VMEM((2, H, 1, DH), F32), pltpu.VMEM((2, H, 1, DH), F32)],
        compiler_params=_cparams(("parallel", "arbitrary", "arbitrary")),
    )(qk, qk, proj3, proj3, proj3, proj3, bias, csf, csb, nsf, nsb, msf, msb, dh, dh)


ROW_BLOCK = 256
FF_COLS = 512
FF_SHARD = D_FF // N_DEV
FF_SHARD_PAD = 384
FF_PAD = N_DEV * FF_SHARD_PAD


def _rms_fwd(name, x, g):
    T = x.shape[0]
    return _rowwise(name, lambda xv, gv: _rms(xv, gv), [_In(x), _In(g, rows=False)], [_Out(D_MODEL, BF16)], T, ROW_BLOCK)[0]


def _rms_bwd(name, x, g, dh, dres):
    T = x.shape[0]

    def fn(xv, gv, dhv, drv):
        _, vjp = jax.vjp(_rms, xv, gv)
        dx, dg = vjp(dhv)
        return drv + dx, dg

    return _rowwise(name, fn, [_In(x), _In(g, rows=False), _In(dh), _In(dres)],
                    [_Out(D_MODEL), _Out(D_MODEL, rows=False)], T, ROW_BLOCK)


def _mmw(name, a, w, mode, **kw):
    if isinstance(w, tuple):
        return _matmul(name, a, w[0], mode, bl=w[1], **kw)
    return _matmul(name, a, w, mode, **kw)


def _ffn_fwd(tag, x, g, wg, wu, wd):
    T = x.shape[0]
    h = _rms_fwd(tag + "_norm", x, g)
    gate = _mmw(tag + "_gate", h, wg, "nn")
    up = _mmw(tag + "_up", h, wu, "nn")
    act = _rowwise(tag + "_act", lambda a, b: _silu(a) * b,
                   [_In(gate, FF_COLS, split=True), _In(up, FF_COLS, split=True)],
                   [_Out(FF_PAD, BF16, FF_COLS, split=True)], T, 1024, ncol=FF_PAD // FF_COLS)[0]
    out = _mmw(tag + "_down", act, wd, "nn", res=x, scale=0.5)
    return out, (x, h, gate, up, act)


def _ffn_bwd(tag, saved, g, wg, wu, wd, dx):
    x, h, gate, up, act = saved
    T = x.shape[0]
    dact = _mmw(tag + "_dact", dx, wd, "nt", scale=0.5)
    dwd = _matmul(tag + "_dwd", act, dx, "tn", scale=0.5)

    def fn(a, b, da):
        _, vjp = jax.vjp(lambda p, q: _silu(p) * q, a, b)
        return vjp(da)

    dgate, dup = _rowwise(tag + "_dactfn", fn,
                          [_In(gate, FF_COLS, split=True), _In(up, FF_COLS, split=True), _In(dact, FF_COLS, split=True)],
                          [_Out(FF_PAD, BF16, FF_COLS, split=True), _Out(FF_PAD, BF16, FF_COLS, split=True)],
                          T, 1024, ncol=FF_PAD // FF_COLS)
    dh = _mmw(tag + "_dh1", dgate, wg, "nt")
    dh = _mmw(tag + "_dh2", dup, wu, "nt", res=dh)
    dwg = _matmul(tag + "_dwg", h, dgate, "tn")
    dwu = _matmul(tag + "_dwu", h, dup, "tn")
    dx_new, dg = _rms_bwd(tag + "_dnorm", x, g, dh, dx)
    return dx_new, dg, dwg, dwu, dwd


def _rope_consts():
    half = ROPE_DIM // 2
    inv_freq = jnp.power(jnp.float32(ROPE_THETA), -jnp.arange(half, dtype=F32) * (2.0 / ROPE_DIM))
    row = jnp.zeros((1, ATT_HEAD_DIM), F32).at[0, :ROPE_DIM].set(jnp.concatenate([inv_freq, inv_freq]))
    rot = np.zeros((ATT_HEAD_DIM, ATT_HEAD_DIM), np.float32)
    for i in range(half):
        rot[half + i, i] = -1.0
        rot[i, half + i] = 1.0
    return row, jnp.asarray(rot)


def _prep_fwd(name, t, g, pos, inv_freq_row, rot):
    R = t.shape[0]

    def fn(tv, gv, pv, fv, rv):
        cos, sin = _rope_tables(pv, fv)
        return _qk_prep(tv, gv, cos, sin, rv)

    return _rowwise(name, fn, [_In(t), _In(g, rows=False), _In(pos), _In(inv_freq_row, rows=False), _In(rot, rows=False)],
                    [_Out(ATT_HEAD_DIM)], R, 1024)[0]


def _prep_bwd(name, t, g, pos, inv_freq_row, rot, dout):
    R = t.shape[0]

    def fn(tv, gv, pv, fv, rv, dv):
        cos, sin = _rope_tables(pv, fv)
        _, vjp = jax.vjp(lambda a, b: _qk_prep(a, b, cos, sin, rv), tv, gv)
        return vjp(dv)

    return _rowwise(name, fn, [_In(t), _In(g, rows=False), _In(pos), _In(inv_freq_row, rows=False), _In(rot, rows=False), _In(dout)],
                    [_Out(ATT_HEAD_DIM), _Out(ATT_HEAD_DIM, rows=False)], R, 1024)


def _to_heads(t, B, S, nh):
    return t.reshape(B, S, nh, ATT_HEAD_DIM).transpose(0, 2, 1, 3)


def _from_heads(t):
    B, nh, S, _ = t.shape
    return t.transpose(0, 2, 1, 3).reshape(B * S, nh * ATT_HEAD_DIM)


def _mix_fwd(x, pos_q, pos_k, B, S, p):
    T = B * S
    h = _rms_fwd("mix_norm", x, p["mix_norm"])
    proj = _matmul("mix_proj", h, p["w_in"], "nn")
    proj3 = proj.reshape(B, S, IN_PAD)
    inv_freq_row, rot = _rope_consts()
    qa = proj[:, C_QA:C_QA + ATT_WIDTH].reshape(T * ATT_HEADS, ATT_HEAD_DIM)
    ka = proj[:, C_KA:C_KA + ATT_KV_WIDTH].reshape(T * ATT_KV_HEADS, ATT_HEAD_DIM)
    q_r = _prep_fwd("q_prep", qa, p["attn_q_norm"], pos_q, inv_freq_row, rot)
    k_r = _prep_fwd("k_prep", ka, p["attn_k_norm"], pos_k, inv_freq_row, rot)
    qh = _to_heads(q_r, B, S, ATT_HEADS)
    kh = _to_heads(k_r, B, S, ATT_KV_HEADS)
    vh = _to_heads(proj[:, C_VA:C_VA + ATT_KV_WIDTH], B, S, ATT_KV_HEADS)
    sink = p["attn_sink"].reshape(ATT_KV_HEADS, ATT_GROUP, 1, 1)
    y_a = _from_heads(_attn_fwd(qh, kh, vh, sink))

    qk_c = _conv_fwd(proj3, p["conv_w8"])
    hf, hb, *states = _mlstm_fwd(qk_c, proj3, p["gate_bias"])
    hf2, hb2 = hf.reshape(T, MLSTM_WIDTH), hb.reshape(T, MLSTM_WIDTH)
    DH = MLSTM_HEAD_DIM
    y_m = _rowwise("mlstm_out", _mlstm_combine,
                   [_In(hf2, DH, split=True), _In(hb2, DH, split=True), _In(proj, DH, C_OM // DH, split=True),
                    _In(p["mlstm_out_norm"], DH, split=True, rows=False)],
                   [_Out(MLSTM_WIDTH, BF16, DH, split=True)], T, 1024, ncol=MLSTM_HEADS)[0]

    za = _mmw("branch_a", y_a, p["w_branch_attn"], "nn")
    zm = _mmw("branch_m", y_m, p["w_branch_mlstm"], "nn")
    W = 512
    merged = _rowwise("merge", _merge,
                      [_In(proj, W, C_GMERGE // W, split=True), _In(proj, W, (C_GMERGE + D_MODEL) // W, split=True),
                       _In(za, W, split=True), _In(zm, W, split=True)],
                      [_Out(D_MODEL, BF16, W, split=True)], T, 512, ncol=D_MODEL // W)[0]
    out = _mmw("mix_out", merged, p["w_out"], "nn", res=x)
    saved = dict(x=x, h=h, proj=proj, qa=qa, ka=ka, qh=qh, kh=kh, vh=vh, sink=sink, y_a=y_a, qk_c=qk_c, hf=hf2, hb=hb2,
                 states=states, y_m=y_m, za=za, zm=zm, merged=merged)
    return out, saved


def _mix_bwd(sv, pos_q, pos_k, B, S, p, dx):
    T = B * S
    DH = MLSTM_HEAD_DIM
    proj = sv["proj"]
    proj3 = proj.reshape(B, S, IN_PAD)
    inv_freq_row, rot = _rope_consts()
    g = {}
    dmerged = _mmw("mix_dmerged", dx, p["w_out"], "nt")
    g["w_out"] = _matmul("mix_dwout", sv["merged"], dx, "tn")
    W = 512

    def merge_bwd(ga, gm, za, zm, dm):
        _, vjp = jax.vjp(_merge, ga, gm, za, zm)
        return vjp(dm)

    dga, dgm, dza, dzm = _rowwise(
        "merge_bwd", merge_bwd,
        [_In(proj, W, C_GMERGE // W, split=True), _In(proj, W, (C_GMERGE + D_MODEL) // W, split=True),
         _In(sv["za"], W, split=True), _In(sv["zm"], W, split=True), _In(dmerged, W, split=True)],
        [_Out(D_MODEL, F32, W, split=True), _Out(D_MODEL, F32, W, split=True),
         _Out(D_MODEL, BF16, W, split=True), _Out(D_MODEL, BF16, W, split=True)], T, 512, ncol=D_MODEL // W)
    dya = _mmw("branch_a_dx", dza, p["w_branch_attn"], "nt")
    g["w_branch_attn"] = _matmul("branch_a_dw", sv["y_a"], dza, "tn")
    dym = _mmw("branch_m_dx", dzm, p["w_branch_mlstm"], "nt")
    g["w_branch_mlstm"] = _matmul("branch_m_dw", sv["y_m"], dzm, "tn")

    def combine_bwd(hf, hb, o_pre, gn, dy):
        _, vjp = jax.vjp(_mlstm_combine, hf, hb, o_pre, gn)
        dhf, _, do, dg = vjp(dy)
        return dhf, do, dg

    dh, dom, g["mlstm_out_norm"] = _rowwise(
        "mlstm_out_bwd", combine_bwd,
        [_In(sv["hf"], DH, split=True), _In(sv["hb"], DH, split=True), _In(proj, DH, C_OM // DH, split=True),
         _In(p["mlstm_out_norm"], DH, split=True, rows=False), _In(dym, DH, split=True)],
        [_Out(MLSTM_WIDTH, F32, DH, split=True), _Out(MLSTM_WIDTH, F32, DH, split=True),
         _Out(MLSTM_WIDTH, F32, DH, split=True, rows=False)], T, 1024, ncol=MLSTM_HEADS)
    dqk_f, dqk_b, dv_f, dv_b, dg_f, dg_b = _mlstm_bwd(sv["qk_c"], proj3, p["gate_bias"], sv["states"],
                                                       dh.reshape(B, S, MLSTM_WIDTH))
    dgates, dvm, g["gate_bias"] = _rowwise(
        "mlstm_dsum", lambda a, b, c, d: (a + b, c + d, jnp.sum(a + b, axis=0, keepdims=True)),
        [_In(dg_f.reshape(T, LANES)), _In(dg_b.reshape(T, LANES)), _In(dv_f.reshape(T, MLSTM_WIDTH)), _In(dv_b.reshape(T, MLSTM_WIDTH))],
        [_Out(LANES), _Out(MLSTM_WIDTH), _Out(LANES, rows=False)], T, 1024)
    dqk, g["conv_w8"] = _conv_bwd(proj3, p["conv_w8"], dqk_f, dqk_b)

    dyh = _to_heads(dya, B, S, ATT_HEADS)
    dqh, dkh, dvh, dsink = _attn_bwd(sv["qh"], sv["kh"], sv["vh"], sv["sink"], dyh)
    g["attn_sink"] = dsink.reshape(1, ATT_HEADS)
    dq_r = dqh.transpose(0, 2, 1, 3).reshape(T * ATT_HEADS, ATT_HEAD_DIM)
    dk_r = dkh.transpose(0, 2, 1, 3).reshape(T * ATT_KV_HEADS, ATT_HEAD_DIM)
    dva = _from_heads(dvh)
    dqa, g["attn_q_norm"] = _prep_bwd("q_prep_bwd", sv["qa"], p["attn_q_norm"], pos_q, inv_freq_row, rot, dq_r)
    dka, g["attn_k_norm"] = _prep_bwd("k_prep_bwd", sv["ka"], p["attn_k_norm"], pos_k, inv_freq_row, rot, dk_r)

    dproj = jnp.concatenate(
        [dga.astype(BF16), dgm.astype(BF16), dqk.reshape(T, 2 * MLSTM_WIDTH).astype(BF16), dvm.astype(BF16), dom.astype(BF16),
         dqa.reshape(T, ATT_WIDTH).astype(BF16), dka.reshape(T, ATT_KV_WIDTH).astype(BF16), dva.astype(BF16),
         dgates.astype(BF16)], axis=1)
    dh2 = _matmul("mix_dh", dproj, p["w_in"], "nt")
    g["w_in"] = _matmul("mix_dwin", sv["h"], dproj, "tn")
    dx_new, g["mix_norm"] = _rms_bwd("mix_dnorm", sv["x"], p["mix_norm"], dh2, dx)
    return dx_new, g


def _loss_and_grad(x, g, target):
    T = x.shape[0]

    def loss_fn(xv, gv, tv):
        err = jnp.square(_rms(xv, gv) - tv)
        return 0.5 * jnp.sum(jnp.mean(err, axis=-1, keepdims=True), axis=0, keepdims=True)

    def fn(xv, gv, tv):
        val, vjp = jax.vjp(lambda a, b: loss_fn(a, b, tv), xv, gv)
        dx, dg = vjp(jnp.ones((1, 1), F32))
        return val, dx, dg

    return _rowwise("loss_head", fn, [_In(x), _In(g, rows=False), _In(target)],
                    [_Out(1, rows=False), _Out(D_MODEL), _Out(D_MODEL, rows=False)], T, ROW_BLOCK)


def _block_norm_fwd(x, g):
    T = x.shape[0]
    return _rowwise("block_norm", _rms, [_In(x), _In(g, rows=False)], [_Out(D_MODEL)], T, ROW_BLOCK)[0]


def _block_norm_bwd(x, g, dy):
    T = x.shape[0]

    def fn(xv, gv, dv):
        _, vjp = jax.vjp(_rms, xv, gv)
        return vjp(dv)

    return _rowwise("block_norm_bwd", fn, [_In(x), _In(g, rows=False), _In(dy)],
                    [_Out(D_MODEL), _Out(D_MODEL, rows=False)], T, ROW_BLOCK)


def _qk_perm_cols(t, axis):
    q, k = jnp.split(t, 2, axis=axis)
    parts = []
    for h in range(MLSTM_HEADS):
        sl = [slice(None)] * t.ndim
        sl[axis] = slice(h * MLSTM_HEAD_DIM, (h + 1) * MLSTM_HEAD_DIM)
        parts += [q[tuple(sl)], k[tuple(sl)]]
    return jnp.concatenate(parts, axis=axis)


def _qk_unperm_cols(t, axis):
    qs, ks = [], []
    for h in range(MLSTM_HEADS):
        sl = [slice(None)] * t.ndim
        sl[axis] = slice(2 * h * MLSTM_HEAD_DIM, (2 * h + 1) * MLSTM_HEAD_DIM)
        qs.append(t[tuple(sl)])
        sl[axis] = slice((2 * h + 1) * MLSTM_HEAD_DIM, (2 * h + 2) * MLSTM_HEAD_DIM)
        ks.append(t[tuple(sl)])
    return jnp.concatenate(qs + ks, axis=axis)


def _w_in_arrange(w):
    qa, ka, va, qm, km, vm, om, gm, gmerge = jnp.split(w, np.cumsum(
        (ATT_WIDTH, ATT_KV_WIDTH, ATT_KV_WIDTH, MLSTM_WIDTH, MLSTM_WIDTH, MLSTM_WIDTH, MLSTM_WIDTH, MLSTM_N_GATES))[:].tolist(), axis=1)
    qk = _qk_perm_cols(jnp.concatenate([qm, km], axis=1), 1)
    pad = jnp.zeros((w.shape[0], LANES - MLSTM_N_GATES), w.dtype)
    return jnp.concatenate([gmerge, qk, vm, om, qa, ka, va, gm, pad], axis=1)


def _w_in_restore(w):
    gmerge = w[:, C_GMERGE:C_GMERGE + 2 * D_MODEL]
    qk = _qk_unperm_cols(w[:, C_QK:C_QK + 2 * MLSTM_WIDTH], 1)
    vm, om = w[:, C_VM:C_VM + MLSTM_WIDTH], w[:, C_OM:C_OM + MLSTM_WIDTH]
    qa, ka, va = w[:, C_QA:C_QA + ATT_WIDTH], w[:, C_KA:C_KA + ATT_KV_WIDTH], w[:, C_VA:C_VA + ATT_KV_WIDTH]
    gm = w[:, C_GATES:C_GATES + MLSTM_N_GATES]
    return jnp.concatenate([qa, ka, va, qk, vm, om, gm, gmerge], axis=1)


BIG = ("ffn1_w_gate", "ffn1_w_up", "ffn1_w_down", "w_in", "mlstm_conv_w", "w_branch_attn", "w_branch_mlstm", "w_out",
       "ffn2_w_gate", "ffn2_w_up", "ffn2_w_down")
MATMUL_W = tuple(n for n in BIG if n != "mlstm_conv_w")
SMALL = ("ffn1_norm", "mix_norm", "mlstm_gate_bias", "attn_q_norm", "attn_k_norm", "attn_sink", "mlstm_conv_b",
         "mlstm_out_norm", "ffn2_norm", "block_out_norm")
WEIGHTS = ("ffn1_norm", "ffn1_w_gate", "ffn1_w_up", "ffn1_w_down", "mix_norm", "w_in", "mlstm_gate_bias", "attn_q_norm",
           "attn_k_norm", "attn_sink", "mlstm_conv_w", "mlstm_conv_b", "mlstm_out_norm", "w_branch_attn", "w_branch_mlstm",
           "w_out", "ffn2_norm", "ffn2_w_gate", "ffn2_w_up", "ffn2_w_down", "block_out_norm")
PACK_COLS = 1024


def _padded_rows(n_elems):
    return -(-n_elems // PACK_COLS)


def _pack_flat(arrs, dtype, row_multiple):
    parts = []
    for a in arrs:
        flat = a.reshape(-1).astype(dtype)
        pad = _padded_rows(flat.shape[0]) * PACK_COLS - flat.shape[0]
        parts.append(jnp.pad(flat, (0, pad)) if pad else flat)
    flat = jnp.concatenate(parts)
    rows = flat.shape[0] // PACK_COLS
    extra = (-rows) % row_multiple
    if extra:
        flat = jnp.pad(flat, (0, extra * PACK_COLS))
    return flat.reshape(-1, PACK_COLS)


def _unpack_flat(buf, shapes, lead=()):
    flat = buf.reshape(lead + (-1,))
    out, off = [], 0
    for s in shapes:
        n = int(np.prod(s))
        out.append(flat[..., off:off + n].reshape(lead + tuple(s)))
        off += _padded_rows(n) * PACK_COLS
    return out


class _Lay:
    def __init__(self, shard, axis, width):
        self.shard, self.axis, self.width = shard, axis, width
        self.padded = tuple(width if a == axis else s for a, s in enumerate(shard))
        self.whole = tuple(N_DEV * width if a == axis else s for a, s in enumerate(shard))

    def pad(self, t, lead=0):
        extra = self.width - self.shard[self.axis]
        if not extra:
            return t
        cfg = [(0, 0)] * t.ndim
        cfg[lead + self.axis] = (0, extra)
        return jnp.pad(t, cfg)

    def unpad(self, t, lead=0):
        idx = [slice(None)] * t.ndim
        idx[lead + self.axis] = slice(0, self.shard[self.axis])
        return t[tuple(idx)]


_FF_COL = _Lay((D_MODEL, FF_SHARD), 1, FF_SHARD_PAD)
_FF_ROW = _Lay((FF_SHARD, D_MODEL), 0, FF_SHARD_PAD)
LAYOUTS = {
    "ffn1_w_gate": _FF_COL, "ffn1_w_up": _FF_COL, "ffn1_w_down": _FF_ROW,
    "ffn2_w_gate": _FF_COL, "ffn2_w_up": _FF_COL, "ffn2_w_down": _FF_ROW,
    "w_in": _Lay((D_MODEL, IN_WIDTH // N_DEV), 0, D_MODEL),
    "mlstm_conv_w": _Lay((3, 2 * MLSTM_WIDTH // N_DEV), 1, 2 * MLSTM_WIDTH // N_DEV),
    "w_branch_attn": _Lay((ATT_WIDTH, D_MODEL // N_DEV), 1, D_MODEL // N_DEV),
    "w_branch_mlstm": _Lay((MLSTM_WIDTH, D_MODEL // N_DEV), 1, D_MODEL // N_DEV),
    "w_out": _Lay((D_MODEL // N_DEV, D_MODEL), 0, D_MODEL // N_DEV),
}


def _window(ref, axis, j, width):
    idx = [slice(None)] * len(ref.shape)
    idx[axis] = pl.ds(pl.multiple_of(j * width, width), width)
    return ref.at[tuple(idx)]


ANY = pl.BlockSpec(memory_space=pl.ANY)


def _mesh_pos():
    return lax.axis_index("x"), lax.axis_index("y"), lax.axis_index("c")


def _all_gather(name, shard, vmem=False):
    R, C = shard.shape
    space = pl.BlockSpec(memory_space=pltpu.VMEM) if vmem else ANY

    def body(x_ref, out_ref, send_sems, recv_sems, local_sem):
        x, y, c = _mesh_pos()
        me, sibling = (x, y, c), (x, y, 1 - c)
        chips = [(1 - x, y), (x, 1 - y), (1 - x, 1 - y)]

        def slot(px, py, pc):
            return out_ref.at[4 * px + 2 * py + pc]

        def copy(k, block, to, src=None):
            return pltpu.make_async_remote_copy(
                src_ref=slot(*block) if src is None else src, dst_ref=slot(*block),
                send_sem=send_sems.at[k], recv_sem=recv_sems.at[k], device_id=to, device_id_type=MESH)

        mine = pltpu.make_async_copy(x_ref, slot(*me), local_sem)
        mine.start()
        first = [copy(0, me, sibling, src=x_ref)]
        first += [copy(1 + j, me, (*chip, c), src=x_ref) for j, chip in enumerate(chips)]
        for cp in first:
            cp.start()
        passed = [copy(4 + j, (*chip, c), sibling) for j, chip in enumerate(chips)]
        for j, chip in enumerate(chips):
            copy(1 + j, (*chip, c), me).wait_recv()
            passed[j].start()
        copy(0, sibling, me).wait_recv()
        for j, chip in enumerate(chips):
            copy(4 + j, (*chip, 1 - c), me).wait_recv()
        for cp in first + passed:
            cp.wait_send()
        mine.wait()

    return pl.pallas_call(
        body, name=name, out_shape=jax.ShapeDtypeStruct((N_DEV, R, C), shard.dtype),
        in_specs=[space], out_specs=space,
        scratch_shapes=[pltpu.SemaphoreType.DMA((7,)), pltpu.SemaphoreType.DMA((7,)), pltpu.SemaphoreType.DMA],
    )(shard)


HBM = pl.BlockSpec(memory_space=pltpu.HBM)
SEM = pl.BlockSpec(memory_space=pltpu.SEMAPHORE)
SPLIT_COPY = pltpu.CompilerParams(has_side_effects=pltpu.SideEffectType.DATAFLOW_SIDE_EFFECTING)
N_PEERS = N_DEV - 1


def _peers(x, y, c):
    return [(x, y, 1 - c), (1 - x, y, c), (x, 1 - y, c), (1 - x, 1 - y, c),
            (1 - x, y, 1 - c), (x, 1 - y, 1 - c), (1 - x, 1 - y, 1 - c)]


def _dev_index(pos):
    return 4 * pos[0] + 2 * pos[1] + pos[2]


def _place_own(name, shards, lays):
    nt = len(shards)
    me = _dev_index(_mesh_pos())

    def body(me_ref, *refs):
        for x_ref, o_ref in zip(refs[:nt], refs[nt:]):
            o_ref[...] = x_ref[...]

    def window_spec(lay):
        if lay.axis == 0:
            return pl.BlockSpec(lay.padded, lambda i, me_ref: (me_ref[0], 0))
        return pl.BlockSpec(lay.padded, lambda i, me_ref: (0, me_ref[0]))

    return pl.pallas_call(
        body, name=name,
        grid_spec=pltpu.PrefetchScalarGridSpec(
            num_scalar_prefetch=1, grid=(1,),
            in_specs=[pl.BlockSpec(lay.padded, lambda i, me_ref: (0, 0)) for lay in lays],
            out_specs=[window_spec(lay) for lay in lays]),
        out_shape=[jax.ShapeDtypeStruct(lay.whole, s.dtype) for s, lay in zip(shards, lays)],
        compiler_params=_cparams(("arbitrary",)),
    )(me.reshape(1).astype(jnp.int32), *shards)


def _gather_start(name, shards, lands, lays, groups, after):
    nt, ng = len(shards), len(groups)

    def body(*refs):
        x_refs, land_refs = refs[:nt], refs[nt:2 * nt]
        sems = refs[2 * nt + 1:2 * nt + 1 + 2 * ng]
        pos = _mesh_pos()
        me = _dev_index(pos)
        for g, tens in enumerate(groups):
            for i, t in enumerate(tens):
                for k, peer in enumerate(_peers(*pos)):
                    pltpu.make_async_remote_copy(
                        src_ref=x_refs[t], dst_ref=_window(land_refs[t], lays[t].axis, me, lays[t].width),
                        send_sem=sems[2 * g].at[N_PEERS * i + k], recv_sem=sems[2 * g + 1].at[N_PEERS * i + k],
                        device_id=peer, device_id_type=MESH).start()

    sem_shapes = []
    for tens in groups:
        sem_shapes += [pltpu.SemaphoreType.DMA((N_PEERS * len(tens),))] * 2
    thru = [pltpu.HBM(s.shape, s.dtype) for s in shards] + [pltpu.HBM(lay.whole, s.dtype) for s, lay in zip(shards, lays)]
    args = [pltpu.with_memory_space_constraint(s, pltpu.HBM) for s in shards]
    args += [pltpu.with_memory_space_constraint(ld, pltpu.HBM) for ld in lands]
    res = pl.pallas_call(
        body, name=name, out_shape=tuple(sem_shapes + thru), in_specs=[HBM] * (2 * nt) + [ANY],
        out_specs=tuple([SEM] * (2 * ng) + [HBM] * (2 * nt)),
        input_output_aliases={t: 2 * ng + t for t in range(2 * nt)}, compiler_params=SPLIT_COPY,
    )(*args, after)
    sems = [(res[2 * g], res[2 * g + 1]) for g in range(ng)]
    return sems, list(res[2 * ng:2 * ng + nt]), list(res[2 * ng + nt:])


def _gather_wait(name, sems, shards, lands, lays, after):
    nt = len(shards)
    send_sems, recv_sems = sems

    def body(*refs):
        x_refs, land_refs = refs[:nt], refs[nt:2 * nt]
        send_ref, recv_ref = refs[2 * nt], refs[2 * nt + 1]
        pos = _mesh_pos()
        for t in range(nt):
            for k, peer in enumerate(_peers(*pos)):
                cp = pltpu.make_async_remote_copy(
                    src_ref=x_refs[t], dst_ref=_window(land_refs[t], lays[t].axis, _dev_index(peer), lays[t].width),
                    send_sem=send_ref.at[N_PEERS * t + k], recv_sem=recv_ref.at[N_PEERS * t + k],
                    device_id=peer, device_id_type=MESH)
                cp.wait_send()
                cp.wait_recv()

    thru = [pltpu.HBM(s.shape, s.dtype) for s in shards] + [pltpu.HBM(ld.shape, ld.dtype) for ld in lands]
    res = pl.pallas_call(
        body, name=name, out_shape=tuple(thru), in_specs=[HBM] * (2 * nt) + [SEM, SEM, ANY],
        out_specs=tuple([HBM] * (2 * nt)), input_output_aliases={t: t for t in range(2 * nt)},
        compiler_params=SPLIT_COPY,
    )(*shards, *lands, send_sems, recv_sems, after)
    return list(res[nt:])


def _pair_exchange(name, grads, lays):
    nt = len(grads)

    def body(*refs):
        g_refs, land_refs = refs[:nt], refs[nt:2 * nt]
        send_sems, recv_sems = refs[2 * nt:]
        x, y, c = _mesh_pos()
        copies = []
        for t in range(nt):
            for chip in range(4):
                copies.append(pltpu.make_async_remote_copy(
                    src_ref=_window(g_refs[t], lays[t].axis, 2 * chip + (1 - c), lays[t].width), dst_ref=land_refs[t].at[chip],
                    send_sem=send_sems.at[4 * t + chip], recv_sem=recv_sems.at[4 * t + chip],
                    device_id=(x, y, 1 - c), device_id_type=MESH))
        for cp in copies:
            cp.start()
        for cp in copies:
            cp.wait_recv()
        for cp in copies:
            cp.wait_send()

    out_shape = [jax.ShapeDtypeStruct((4,) + lay.padded, g.dtype) for g, lay in zip(grads, lays)]
    return pl.pallas_call(
        body, name=name, out_shape=out_shape, in_specs=[ANY] * nt, out_specs=[ANY] * nt,
        scratch_shapes=[pltpu.SemaphoreType.DMA((4 * nt,)), pltpu.SemaphoreType.DMA((4 * nt,))],
    )(*grads)


def _pair_sum(name, whole, landed, lay, out_dtype):
    R, C = lay.padded
    br = _first_divisor(R, (512, 384, 256, 128, 64, 32, 16, 8))
    nb = R // br
    if lay.axis == 0:
        mine_spec = pl.BlockSpec((br, C), lambda k, i, c_ref: ((2 * k + c_ref[0]) * nb + i, 0))
    else:
        mine_spec = pl.BlockSpec((br, C), lambda k, i, c_ref: (i, 2 * k + c_ref[0]))

    def body(c_ref, mine_ref, sib_ref, o_ref):
        o_ref[0] = (mine_ref[...] + sib_ref[0]).astype(out_dtype)

    c = lax.axis_index("c")
    return pl.pallas_call(
        body, name=name,
        grid_spec=pltpu.PrefetchScalarGridSpec(
            num_scalar_prefetch=1, grid=(4, nb),
            in_specs=[mine_spec, pl.BlockSpec((1, br, C), lambda k, i, c_ref: (k, i, 0))],
            out_specs=pl.BlockSpec((1, br, C), lambda k, i, c_ref: (k, i, 0))),
        out_shape=jax.ShapeDtypeStruct((4, R, C), out_dtype),
        compiler_params=_cparams(("parallel", "parallel")),
    )(c.reshape(1).astype(jnp.int32), whole, landed)


def _chip_exchange(name, sums):
    nt = len(sums)

    def body(*refs):
        s_refs, land_refs = refs[:nt], refs[nt:2 * nt]
        send_sems, recv_sems, local_sems = refs[2 * nt:]
        x, y, c = _mesh_pos()
        my_chip = 2 * x + y
        mine = [pltpu.make_async_copy(s_refs[t].at[my_chip], land_refs[t].at[my_chip], local_sems.at[t]) for t in range(nt)]
        for cp in mine:
            cp.start()
        chips = [(1 - x, y), (x, 1 - y), (1 - x, 1 - y)]
        copies = []
        for t in range(nt):
            for j, (px, py) in enumerate(chips):
                copies.append(pltpu.make_async_remote_copy(
                    src_ref=s_refs[t].at[2 * px + py], dst_ref=land_refs[t].at[my_chip],
                    send_sem=send_sems.at[3 * t + j], recv_sem=recv_sems.at[3 * t + j],
                    device_id=(px, py, c), device_id_type=MESH))
        for cp in copies:
            cp.start()
        for t in range(nt):
            for j, (px, py) in enumerate(chips):
                pltpu.make_async_remote_copy(
                    src_ref=s_refs[t].at[my_chip], dst_ref=land_refs[t].at[2 * px + py],
                    send_sem=send_sems.at[3 * t + j], recv_sem=recv_sems.at[3 * t + j],
                    device_id=(px, py, c), device_id_type=MESH).wait_recv()
        for cp in copies:
            cp.wait_send()
        for cp in mine:
            cp.wait()

    return pl.pallas_call(
        body, name=name, out_shape=[jax.ShapeDtypeStruct(s.shape, s.dtype) for s in sums],
        in_specs=[ANY] * nt, out_specs=[ANY] * nt,
        scratch_shapes=[pltpu.SemaphoreType.DMA((3 * nt,)), pltpu.SemaphoreType.DMA((3 * nt,)), pltpu.SemaphoreType.DMA((nt,))],
    )(*sums)


def _chip_start(name, sums):
    nt = len(sums)

    def body(*refs):
        s_refs, land_refs = refs[:nt], refs[nt:2 * nt]
        send_sems, recv_sems = refs[2 * nt], refs[2 * nt + 1]
        x, y, c = _mesh_pos()
        my_chip = 2 * x + y
        for t in range(nt):
            for j, (px, py) in enumerate([(1 - x, y), (x, 1 - y), (1 - x, 1 - y)]):
                pltpu.make_async_remote_copy(
                    src_ref=s_refs[t].at[2 * px + py], dst_ref=land_refs[t].at[my_chip],
                    send_sem=send_sems.at[3 * t + j], recv_sem=recv_sems.at[3 * t + j],
                    device_id=(px, py, c), device_id_type=MESH).start()

    thru = [pltpu.HBM(s.shape, s.dtype) for s in sums] * 2
    args = [pltpu.with_memory_space_constraint(s, pltpu.HBM) for s in sums]
    args += [pltpu.with_memory_space_constraint(lax.empty(s.shape, s.dtype), pltpu.HBM) for s in sums]
    res = pl.pallas_call(
        body, name=name, out_shape=tuple([pltpu.SemaphoreType.DMA((3 * nt,))] * 2 + thru), in_specs=[HBM] * (2 * nt),
        out_specs=tuple([SEM, SEM] + [HBM] * (2 * nt)), input_output_aliases={t: 2 + t for t in range(2 * nt)},
        compiler_params=SPLIT_COPY,
    )(*args)
    return (res[0], res[1]), list(res[2:2 + nt]), list(res[2 + nt:])


def _chip_wait(name, sems, sums, lands, after):
    nt = len(sums)

    def body(*refs):
        s_refs, land_refs = refs[:nt], refs[nt:2 * nt]
        send_sems, recv_sems = refs[2 * nt], refs[2 * nt + 1]
        x, y, c = _mesh_pos()
        my_chip = 2 * x + y
        for t in range(nt):
            for j, (px, py) in enumerate([(1 - x, y), (x, 1 - y), (1 - x, 1 - y)]):
                cp = pltpu.make_async_remote_copy(
                    src_ref=s_refs[t].at[my_chip], dst_ref=land_refs[t].at[2 * px + py],
                    send_sem=send_sems.at[3 * t + j], recv_sem=recv_sems.at[3 * t + j],
                    device_id=(px, py, c), device_id_type=MESH)
                cp.wait_send()
                cp.wait_recv()

    thru = [pltpu.HBM(s.shape, s.dtype) for s in sums] * 2
    res = pl.pallas_call(
        body, name=name, out_shape=tuple(thru), in_specs=[HBM] * (2 * nt) + [SEM, SEM, ANY],
        out_specs=tuple([HBM] * (2 * nt)), input_output_aliases={t: t for t in range(2 * nt)},
        compiler_params=SPLIT_COPY,
    )(*sums, *lands, sems[0], sems[1], after)
    return list(res[:nt]), list(res[nt:])


def _sum_chips(name, own, landed):
    _, R, C = own.shape
    br = _first_divisor(R, (512, 384, 256, 128, 64, 32, 16, 8))
    x, y, _ = _mesh_pos()
    slots = jnp.stack([2 * x + y, 2 * (1 - x) + y, 2 * x + (1 - y), 2 * (1 - x) + (1 - y)]).astype(jnp.int32)

    def body(slot_ref, mine_ref, a_ref, b_ref, c_ref, o_ref):
        o_ref[...] = ((mine_ref[0].astype(F32) + a_ref[0].astype(F32)) + b_ref[0].astype(F32)) + c_ref[0].astype(F32)

    def slot_spec(j):
        return pl.BlockSpec((1, br, C), lambda i, slot_ref: (slot_ref[j], i, 0))

    return pl.pallas_call(
        body, name=name,
        grid_spec=pltpu.PrefetchScalarGridSpec(
            num_scalar_prefetch=1, grid=(R // br,), in_specs=[slot_spec(0), slot_spec(1), slot_spec(2), slot_spec(3)],
            out_specs=pl.BlockSpec((br, C), lambda i, slot_ref: (i, 0))),
        out_shape=jax.ShapeDtypeStruct((R, C), F32), compiler_params=_cparams(("parallel",)),
    )(slots, own, landed, landed, landed)


def _sum_slots(name, slots, n):
    _, R, C = slots.shape
    br = _first_divisor(R, (512, 384, 256, 128, 64, 32, 16, 8))

    def body(s_ref, o_ref):
        acc = s_ref[0].astype(F32)
        for k in range(1, n):
            acc = acc + s_ref[k].astype(F32)
        o_ref[...] = acc

    return pl.pallas_call(
        body, name=name, grid=(R // br,), in_specs=[pl.BlockSpec((n, br, C), lambda i: (0, i, 0))],
        out_specs=pl.BlockSpec((br, C), lambda i: (i, 0)), out_shape=jax.ShapeDtypeStruct((R, C), F32),
        compiler_params=_cparams(("parallel",)),
    )(slots)


def _reduce_scatter_start(tag, names, grads):
    lays = [LAYOUTS[n] for n in names]
    landed = _pair_exchange("grads_pair_" + names[0], grads, lays)
    sums = [_pair_sum("grads_pairsum_" + n, g, ld, lay, BF16) for n, g, ld, lay in zip(names, grads, landed, lays)]
    sems, sums, lands = _chip_start(tag + "_chips_start", sums)
    return tag, names, sems, sums, lands


def _reduce_scatter_finish(pending, after):
    tag, names, sems, sums, lands = pending
    own, got = _chip_wait(tag + "_chips_wait", sems, sums, lands, after)
    return [_sum_chips("grads_sum_" + n, o, s) for n, o, s in zip(names, own, got)]


def _adamw_math(w, g, m, v):
    m = ADAM_B1 * m + (1.0 - ADAM_B1) * g
    v = ADAM_B2 * v + (1.0 - ADAM_B2) * jnp.square(g)
    m_hat = m / (1.0 - ADAM_B1 ** ADAM_STEP)
    v_hat = v / (1.0 - ADAM_B2 ** ADAM_STEP)
    delta = -ADAM_LR * (m_hat / (jnp.sqrt(v_hat) + ADAM_EPS) + ADAM_WD * w)
    return delta, m, v


def _adamw(name, w, g, m, v):
    shape = w.shape
    cols = shape[-1]
    rows = int(np.prod(shape[:-1]))
    br = _first_divisor(rows, (512, 352, 256, 128, 64, 32, 16, 8))
    args = [_In(a.reshape(rows, cols)) for a in (w, g, m, v)]
    outs = _rowwise(name, _adamw_math, args, [_Out(cols), _Out(cols), _Out(cols)], rows, br)
    return [o.reshape(shape) for o in outs]


GROUPS = {"ffn1": ("ffn1_w_gate", "ffn1_w_up", "ffn1_w_down"),
          "mix": ("w_in", "w_branch_attn", "w_branch_mlstm", "w_out"),
          "ffn2": ("ffn2_w_gate", "ffn2_w_up", "ffn2_w_down")}


def _small_params(small, conv_w, l):
    p = {}
    for n in ("ffn1_norm", "mix_norm", "ffn2_norm", "block_out_norm", "mlstm_out_norm", "attn_q_norm", "attn_k_norm"):
        p[n] = small[n][l][None, :]
    p["attn_sink"] = small["attn_sink"][l]
    p["gate_bias"] = jnp.pad(small["mlstm_gate_bias"][l], (0, LANES - MLSTM_N_GATES))[None, :]
    taps = _qk_perm_cols(conv_w[l], 1)
    conv_b = _qk_perm_cols(small["mlstm_conv_b"][l][None, :], 1)
    p["conv_w8"] = jnp.concatenate([taps, conv_b, jnp.zeros((4, 2 * MLSTM_WIDTH), F32)], axis=0)
    return p


def _w_in_from_slots(slots):
    w_in = slots.reshape(N_DEV, D_MODEL, IN_WIDTH // N_DEV).transpose(1, 0, 2).reshape(D_MODEL, IN_WIDTH)
    return _w_in_arrange(w_in)


def _w_in_to_slots(g):
    return _w_in_restore(g).reshape(D_MODEL, N_DEV, IN_WIDTH // N_DEV).transpose(1, 0, 2).reshape(
        N_DEV * D_MODEL, IN_WIDTH // N_DEV)


def _local_step(x, positions, target, weights_of, small, conv_w, on_grads):
    B, S, _ = x.shape
    T = B * S
    pos = positions.reshape(T, 1)
    pos_q = jnp.repeat(pos, ATT_HEADS, axis=0)
    pos_k = jnp.repeat(pos, ATT_KV_HEADS, axis=0)
    params = [_small_params(small, conv_w, l) for l in range(DEPTH)]
    xs = x.reshape(T, D_MODEL)
    tgt = target.reshape(T, D_MODEL)

    saved = []
    for l, p in enumerate(params):
        p.update(weights_of(l, "ffn1", xs))
        x1, s1 = _ffn_fwd("ffn1", xs, p["ffn1_norm"], p["ffn1_w_gate"], p["ffn1_w_up"], p["ffn1_w_down"])
        p.update(weights_of(l, "mix", x1))
        p["w_in"] = _w_in_from_slots(p["w_in"])
        x2, s2 = _mix_fwd(x1, pos_q, pos_k, B, S, p)
        p.update(weights_of(l, "ffn2", x2))
        x3, s3 = _ffn_fwd("ffn2", x2, p["ffn2_norm"], p["ffn2_w_gate"], p["ffn2_w_up"], p["ffn2_w_down"])
        saved.append((s1, s2, s3, x3))
        if l + 1 < DEPTH:
            xs = _block_norm_fwd(x3, p["block_out_norm"])

    sm = {n: [None] * DEPTH for n in SMALL + ("mlstm_conv_w",)}
    loss = None
    dx = None
    for l in reversed(range(DEPTH)):
        p = params[l]
        s1, s2, s3, x3 = saved[l]
        if l == DEPTH - 1:
            loss, dx, dgn = _loss_and_grad(x3, p["block_out_norm"], tgt)
        else:
            dx, dgn = _block_norm_bwd(x3, p["block_out_norm"], dx)
        sm["block_out_norm"][l] = dgn[0]
        dx, dg, dwg, dwu, dwd = _ffn_bwd("ffn2", s3, p["ffn2_norm"], p["ffn2_w_gate"], p["ffn2_w_up"], p["ffn2_w_down"], dx)
        sm["ffn2_norm"][l] = dg[0]
        on_grads(l, "ffn2", {"ffn2_w_gate": dwg, "ffn2_w_up": dwu, "ffn2_w_down": dwd}, dx)
        dx, g = _mix_bwd(s2, pos_q, pos_k, B, S, p, dx)
        on_grads(l, "mix", {"w_in": _w_in_to_slots(g["w_in"]), "w_branch_attn": g["w_branch_attn"],
                            "w_branch_mlstm": g["w_branch_mlstm"], "w_out": g["w_out"]}, dx)
        dconv = _qk_unperm_cols(g["conv_w8"], 1)
        sm["mlstm_conv_w"][l] = dconv[0:3]
        sm["mlstm_conv_b"][l] = dconv[3]
        sm["mix_norm"][l] = g["mix_norm"][0]
        sm["mlstm_gate_bias"][l] = g["gate_bias"][0, :MLSTM_N_GATES]
        sm["attn_q_norm"][l], sm["attn_k_norm"][l] = g["attn_q_norm"][0], g["attn_k_norm"][0]
        sm["attn_sink"][l] = g["attn_sink"][0]
        sm["mlstm_out_norm"][l] = g["mlstm_out_norm"][0]
        dx, dg, dwg, dwu, dwd = _ffn_bwd("ffn1", s1, p["ffn1_norm"], p["ffn1_w_gate"], p["ffn1_w_up"], p["ffn1_w_down"], dx)
        sm["ffn1_norm"][l] = dg[0]
        on_grads(l, "ffn1", {"ffn1_w_gate": dwg, "ffn1_w_up": dwu, "ffn1_w_down": dwd}, dx)
    sm = {n: jnp.stack(v, axis=0) for n, v in sm.items()}
    return loss, dx.reshape(B, S, D_MODEL), sm


def kernel(x, positions, ffn1_norm, ffn1_w_gate, ffn1_w_up, ffn1_w_down, mix_norm, w_in, mlstm_gate_bias, attn_q_norm, attn_k_norm, attn_sink, mlstm_conv_w, mlstm_conv_b, mlstm_out_norm, w_branch_attn, w_branch_mlstm, w_out, ffn2_norm, ffn2_w_gate, ffn2_w_up, ffn2_w_down, block_out_norm, loss_target, m_ffn1_norm, m_ffn1_w_gate, m_ffn1_w_up, m_ffn1_w_down, m_mix_norm, m_w_in, m_mlstm_gate_bias, m_attn_q_norm, m_attn_k_norm, m_attn_sink, m_mlstm_conv_w, m_mlstm_conv_b, m_mlstm_out_norm, m_w_branch_attn, m_w_branch_mlstm, m_w_out, m_ffn2_norm, m_ffn2_w_gate, m_ffn2_w_up, m_ffn2_w_down, m_block_out_norm, v_ffn1_norm, v_ffn1_w_gate, v_ffn1_w_up, v_ffn1_w_down, v_mix_norm, v_w_in, v_mlstm_gate_bias, v_attn_q_norm, v_attn_k_norm, v_attn_sink, v_mlstm_conv_w, v_mlstm_conv_b, v_mlstm_out_norm, v_w_branch_attn, v_w_branch_mlstm, v_w_out, v_ffn2_norm, v_ffn2_w_gate, v_ffn2_w_up, v_ffn2_w_down, v_block_out_norm):
    args = locals()
    w = {n: args[n] for n in WEIGHTS}
    m = {n: args["m_" + n] for n in WEIGHTS}
    v = {n: args["v_" + n] for n in WEIGHTS}

    order = [(l, grp) for l in range(DEPTH) for grp in ("ffn1", "mix", "ffn2")]
    keys = [(l, n) for l, grp in order for n in GROUPS[grp]]
    lays = [LAYOUTS[n] for _, n in keys]
    shards = [lay.pad(w[n][l].astype(BF16)) for (l, n), lay in zip(keys, lays)]
    group_idx, at = {}, 0
    for l, grp in order:
        group_idx[(l, grp)] = list(range(at, at + len(GROUPS[grp])))
        at += len(GROUPS[grp])
    conv_shape = w["mlstm_conv_w"].shape
    conv_all = _all_gather("conv_all_gather", _pack_flat([w["mlstm_conv_w"]], F32, 8), vmem=True)
    conv_parts = _unpack_flat(conv_all, [conv_shape], lead=(N_DEV,))[0]
    conv_w = jnp.concatenate([conv_parts[j] for j in range(N_DEV)], axis=2)
    small = {n: w[n] for n in SMALL}

    lands = []
    for l, grp in order:
        idx = group_idx[(l, grp)]
        lands += _place_own("weights_place_" + grp, [shards[i] for i in idx], [lays[i] for i in idx])
    sems, shards, lands = _gather_start("weights_gather_start", shards, lands, lays, [group_idx[k] for k in order], conv_all)

    def weights_of(l, grp, after):
        idx = group_idx[(l, grp)]
        whole = _gather_wait(f"weights_gather_wait_{l}_{grp}", sems[order.index((l, grp))], [shards[i] for i in idx],
                             [lands[i] for i in idx], [lays[i] for i in idx], after)
        return dict(zip(GROUPS[grp], whole))

    totals, pending = {}, []

    def finish(after):
        tag, names = pending[0][0], pending[0][1]
        for n, t in zip(names, _reduce_scatter_finish(pending.pop(0), after)):
            totals[(tag, n)] = t

    def on_grads(l, grp, g, after):
        if pending:
            finish(after)
        names = GROUPS[grp]
        pending.append(_reduce_scatter_start(f"grads_{l}_{grp}", names, [g[n] for n in names]))

    loss, grad_x, small_g = _local_step(x, positions, loss_target, weights_of, small, conv_w, on_grads)
    finish(pending[0][3][0])
    grads = {}
    for grp, names in GROUPS.items():
        for n in names:
            grads[n] = jnp.stack([LAYOUTS[n].unpad(totals[(f"grads_{l}_{grp}", n)]) for l in range(DEPTH)], axis=0)

    small_names = SMALL + ("mlstm_conv_w",)
    small_shapes = [small_g[n].shape for n in small_names] + [(1, 1)]
    small_packed = _pack_flat([small_g[n] for n in small_names] + [loss], F32, 8)
    small_all = _all_gather("small_all_gather", small_packed, vmem=True)
    small_sum = _sum_slots("small_sum", small_all, N_DEV)
    *small_grads, loss_total = _unpack_flat(small_sum, small_shapes)
    grads.update(dict(zip(small_names, small_grads)))
    x_pos, y_pos, c_pos = _mesh_pos()
    grads["mlstm_conv_w"] = lax.dynamic_slice_in_dim(
        grads["mlstm_conv_w"], (4 * x_pos + 2 * y_pos + c_pos) * conv_shape[2], conv_shape[2], axis=2)

    deltas, new_m, new_v = {}, {}, {}
    for n in BIG:
        deltas[n], new_m[n], new_v[n] = _adamw("adamw_" + n, w[n], grads[n], m[n], v[n])
    sw, sg, smm, sv = (_pack_flat([d[n] for n in SMALL], F32, 8) for d in (w, grads, m, v))
    sd, snm, snv = _adamw("adamw_small", sw, sg, smm, sv)
    shapes = [w[n].shape for n in SMALL]
    for d, buf in ((deltas, sd), (new_m, snm), (new_v, snv)):
        d.update(dict(zip(SMALL, _unpack_flat(buf, shapes))))

    return (loss_total.reshape(()), grad_x, *[grads[n] for n in WEIGHTS], *[deltas[n] for n in WEIGHTS],
            *[new_m[n] for n in WEIGHTS], *[new_v[n] for n in WEIGHTS])
```

```python
import functools

import numpy as np
import jax
import jax.numpy as jnp
from jax import lax
from jax.experimental import pallas as pl
from jax.experimental.pallas import tpu as pltpu

F32 = jnp.float32
BF16 = jnp.bfloat16

D_MODEL = 1024
D_FF = 2816
ATT_HEAD_DIM = 64
ATT_HEADS = 8
ATT_KV_HEADS = 2
ATT_GROUP = ATT_HEADS // ATT_KV_HEADS
ATT_WIDTH = ATT_HEADS * ATT_HEAD_DIM
ATT_KV_WIDTH = ATT_KV_HEADS * ATT_HEAD_DIM
WINDOW = 128
ATT_BLOCK = 128
ROPE_DIM = 16
ROPE_THETA = 500000.0
MLSTM_HEADS = 4
MLSTM_HEAD_DIM = 128
MLSTM_WIDTH = MLSTM_HEADS * MLSTM_HEAD_DIM
MLSTM_CHUNK = 128
MLSTM_N_GATES = 4 * MLSTM_HEADS
NORM_EPS = 1e-6
IN_WIDTH = 4880
DEPTH = 2
N_DEV = 8

ADAM_LR = 0.001
ADAM_B1 = 0.9
ADAM_B2 = 0.999
ADAM_EPS = 1e-08
ADAM_WD = 0.01
ADAM_STEP = 10

LANES = 128
C_GMERGE = 0
C_QK = 2048
C_VM = 3072
C_OM = 3584
C_QA = 4096
C_KA = 4608
C_VA = 4736
C_GATES = 4864
IN_PAD = 4992

VMEM_LIMIT = 48 * 1024 * 1024

MESH = pl.DeviceIdType.MESH


def _cparams(sem):
    return pltpu.CompilerParams(dimension_semantics=sem, vmem_limit_bytes=VMEM_LIMIT)


def _first_divisor(n, cands):
    for c in cands:
        if n % c == 0:
            return c
    return n


_NN = ((1,), (0,))
_NT = ((1,), (1,))
_TN = ((0,), (0,))


def _mm(a, b, dims):
    return lax.dot_general(a.astype(BF16), b.astype(BF16), (dims, ((), ())), preferred_element_type=F32)


@jax.custom_vjp
def mm_nn(a, b):
    return _mm(a, b, _NN)


def _mm_nn_fwd(a, b):
    return _mm(a, b, _NN), (a, b)


def _mm_nn_bwd(res, g):
    a, b = res
    return _mm(g, b, _NT).astype(a.dtype), _mm(a, g, _TN).astype(b.dtype)


mm_nn.defvjp(_mm_nn_fwd, _mm_nn_bwd)


@jax.custom_vjp
def mm_nt(a, b):
    return _mm(a, b, _NT)


def _mm_nt_fwd(a, b):
    return _mm(a, b, _NT), (a, b)


def _mm_nt_bwd(res, g):
    a, b = res
    return _mm(g, b, _NN).astype(a.dtype), _mm(g, a, _TN).astype(b.dtype)


mm_nt.defvjp(_mm_nt_fwd, _mm_nt_bwd)


@jax.custom_vjp
def mm_tn(a, b):
    return _mm(a, b, _TN)


def _mm_tn_fwd(a, b):
    return _mm(a, b, _TN), (a, b)


def _mm_tn_bwd(res, g):
    a, b = res
    return _mm(b, g, _NT).astype(a.dtype), _mm(a, g, _NN).astype(b.dtype)


mm_tn.defvjp(_mm_tn_fwd, _mm_tn_bwd)


def _matmul(name, a, b, mode, out_dtype=F32, res=None, scale=1.0, bl=None, dep=None):
    b_shape = b.shape if bl is None else b.shape[1:]
    if mode == "nn":
        (M, K), (K2, N) = a.shape, b_shape
    elif mode == "nt":
        (M, K), (N, K2) = a.shape, b_shape
    else:
        (K, M), (K2, N) = a.shape, b_shape
    assert K == K2, (name, a.shape, b.shape)
    tm = _first_divisor(M, (1024, 1408, 512, 384, 256, 128))
    tn = _first_divisor(N, (1024, 512, 384, 256, 128))
    tk = _first_divisor(K, (1024, 1408, 1664, 512, 256, 128))
    nk = K // tk
    if mode == "tn":
        a_spec = pl.BlockSpec((tk, tm), lambda i, j, k: (k, i))
    else:
        a_spec = pl.BlockSpec((tm, tk), lambda i, j, k: (i, k))
    if mode == "nt":
        b_blk, b_idx = (tn, tk), (lambda i, j, k: (j, k))
    else:
        b_blk, b_idx = (tk, tn), (lambda i, j, k: (k, j))
    if bl is None:
        b_spec = pl.BlockSpec(b_blk, b_idx)
    else:
        b_spec = pl.BlockSpec((None,) + b_blk, lambda i, j, k: (bl,) + b_idx(i, j, k))
    o_spec = pl.BlockSpec((tm, tn), lambda i, j, k: (i, j))
    dims = {"nn": _NN, "nt": _NT, "tn": _TN}[mode]
    has_res = res is not None

    def body(*refs):
        a_ref, b_ref = refs[:2]
        r_ref = refs[2] if has_res else None
        o_ref, acc = refs[-2:]
        k = pl.program_id(2)

        @pl.when(k == 0)
        def _():
            acc[...] = jnp.zeros_like(acc)

        acc[...] += _mm(a_ref[...], b_ref[...], dims)

        @pl.when(k == nk - 1)
        def _():
            out = acc[...]
            if scale != 1.0:
                out = out * scale
            if has_res:
                out = r_ref[...].astype(F32) + out
            o_ref[...] = out.astype(out_dtype)

    in_specs = [a_spec, b_spec] + ([o_spec] if has_res else [])
    args = (a, b) + ((res,) if has_res else ())
    if dep is not None:
        in_specs.append(pl.BlockSpec(memory_space=pl.ANY))
        args += (dep,)
    return pl.pallas_call(
        body, name=name, grid=(M // tm, N // tn, nk), in_specs=in_specs, out_specs=o_spec,
        out_shape=jax.ShapeDtypeStruct((M, N), out_dtype), scratch_shapes=[pltpu.VMEM((tm, tn), F32)],
        compiler_params=_cparams(("parallel", "parallel", "arbitrary")),
    )(*args)


class _In:
    def __init__(self, arr, width=None, base=0, split=False, rows=True):
        self.arr, self.base, self.split, self.rows = arr, base, split, rows
        self.width = arr.shape[1] if width is None else width


class _Out:
    def __init__(self, cols, dtype=F32, width=None, split=False, rows=True, nrows=1):
        self.cols, self.dtype, self.split, self.rows, self.nrows = cols, dtype, split, rows, nrows
        self.width = cols if width is None else width


def _rowwise(name, fn, ins, outs, n_rows, br, ncol=1):
    br = min(br, n_rows)
    assert n_rows % br == 0, (name, n_rows, br)
    nrow_blocks = n_rows // br

    def in_spec(d):
        nb = br if d.rows else d.arr.shape[0]
        if d.rows and d.split:
            im = lambda j, i, base=d.base: (i, base + j)
        elif d.rows:
            im = lambda j, i, base=d.base: (i, base)
        elif d.split:
            im = lambda j, i, base=d.base: (0, base + j)
        else:
            im = lambda j, i, base=d.base: (0, base)
        return pl.BlockSpec((nb, d.width), im)

    def out_spec(d):
        nb = br if d.rows else d.nrows
        if d.rows and d.split:
            im = lambda j, i: (i, j)
        elif d.rows:
            im = lambda j, i: (i, 0)
        elif d.split:
            im = lambda j, i: (0, j)
        else:
            im = lambda j, i: (0, 0)
        return pl.BlockSpec((nb, d.width), im)

    n_in = len(ins)

    def body(*refs):
        i = pl.program_id(1)
        vals = [r[...] for r in refs[:n_in]]
        res = fn(*vals)
        if not isinstance(res, (tuple, list)):
            res = (res,)
        for d, ref, val in zip(outs, refs[n_in:], res):
            if d.rows:
                ref[...] = val.astype(d.dtype)
            else:
                @pl.when(i == 0)
                def _(ref=ref):
                    ref[...] = jnp.zeros_like(ref)

                ref[...] += val.astype(d.dtype)

    out_shape = [jax.ShapeDtypeStruct((n_rows if d.rows else d.nrows, d.cols), d.dtype) for d in outs]
    res = pl.pallas_call(
        body, name=name, grid=(ncol, nrow_blocks), in_specs=[in_spec(d) for d in ins],
        out_specs=[out_spec(d) for d in outs], out_shape=out_shape,
        compiler_params=_cparams(("parallel", "arbitrary")),
    )(*[d.arr for d in ins])
    return res


def _rms(x, g):
    return x * lax.rsqrt(jnp.mean(x * x, axis=-1, keepdims=True) + NORM_EPS) * g


def _sigmoid(x):
    return 1.0 / (1.0 + jnp.exp(-x))


def _silu(x):
    return x * _sigmoid(x)


def _log_sigmoid(x):
    return jnp.minimum(x, 0.0) - jnp.log(1.0 + jnp.exp(-jnp.abs(x)))


def _rope_tables(pos, inv_freq_row):
    ang = pos.astype(F32) * inv_freq_row
    return jnp.cos(ang), jnp.sin(ang)


def _qk_prep(t, g, cos, sin, rot_mat):
    y = _rms(t, g)
    rot = lax.dot_general(y, rot_mat, (_NN, ((), ())), precision=lax.Precision.HIGHEST, preferred_element_type=F32)
    return y * cos + rot * sin


def _attn_head(q, kb, vb, sink, valid):
    s = mm_nt(q, kb) * (ATT_HEAD_DIM ** -0.5)
    s = jnp.where(valid, s, -jnp.inf)
    m = jnp.maximum(jnp.max(s, axis=-1, keepdims=True), sink)
    p = jnp.exp(s - m)
    den = jnp.sum(p, axis=-1, keepdims=True) + jnp.exp(sink - m)
    return mm_nn(p / den, vb)


def _mlstm_chunk(q, k, v, li, lf_pre, C, n, m, incl, incl_t, eye):
    k = k * (MLSTM_HEAD_DIM ** -0.5)
    lf = _log_sigmoid(lf_pre)
    lf_row = jnp.sum(eye * lf, axis=0, keepdims=True)
    li_row = jnp.sum(eye * li, axis=0, keepdims=True)
    b = jnp.sum(incl * lf_row, axis=1, keepdims=True)
    b_row = jnp.sum(incl_t * lf, axis=0, keepdims=True)
    b_tot = jnp.sum(lf, axis=0, keepdims=True)
    a = b_tot - b + li
    a_max = jnp.max(a, axis=0, keepdims=True)
    kw = k * jnp.exp(a - a_max)
    c_loc = mm_tn(kw, v)
    n_loc = jnp.sum(kw, axis=0, keepdims=True)

    dmat = jnp.where(incl > 0.5, b - b_row + li_row, -jnp.inf)
    inter = b + m
    m_t = jnp.maximum(inter, jnp.max(dmat, axis=1, keepdims=True))
    sc = mm_nt(q, k) * jnp.exp(dmat - m_t)
    scale_in = jnp.exp(inter - m_t)
    num = mm_nn(sc, v) + scale_in * mm_nn(q, C)
    den = jnp.sum(sc, axis=1, keepdims=True) + scale_in * jnp.sum(q * n, axis=1, keepdims=True)
    h = num / jnp.maximum(jnp.abs(den), jnp.exp(-m_t))

    m_new = jnp.maximum(b_tot + m, a_max)
    s_p = jnp.exp(b_tot + m - m_new)
    s_l = jnp.exp(a_max - m_new)
    return h, s_p * C + s_l * c_loc, s_p * n + s_l * n_loc, m_new


def _mlstm_combine(hf, hb, o_pre, g):
    h = hf + hb
    mu = jnp.mean(h, axis=-1, keepdims=True)
    var = jnp.mean(jnp.square(h - mu), axis=-1, keepdims=True)
    return _sigmoid(o_pre) * ((h - mu) * lax.rsqrt(var + NORM_EPS) * g)


def _merge(ga, gm, za, zm):
    return _sigmoid(ga) * za + _sigmoid(gm) * zm


def _attn_mask(n, seq):
    qi = n * ATT_BLOCK + lax.broadcasted_iota(jnp.int32, (ATT_BLOCK, 3 * ATT_BLOCK), 0)
    kj = (n - 1) * ATT_BLOCK + lax.broadcasted_iota(jnp.int32, (ATT_BLOCK, 3 * ATT_BLOCK), 1)
    return (jnp.abs(qi - kj) <= WINDOW) & (kj >= 0) & (kj < seq)


def _attn_specs(nq):
    q_spec = pl.BlockSpec((1, ATT_GROUP, ATT_BLOCK, ATT_HEAD_DIM), lambda h, b, n: (b, h, n, 0))

    def kv_spec(off):
        return pl.BlockSpec((1, 1, ATT_BLOCK, ATT_HEAD_DIM),
                            lambda h, b, n: (b, h, jnp.clip(n + off, 0, nq - 1), 0))

    sink_spec = pl.BlockSpec((1, ATT_GROUP, 1, 1), lambda h, b, n: (h, 0, 0, 0))
    return q_spec, kv_spec, sink_spec


def _attn_fwd(q, k, v, sink):
    B, _, S, _ = q.shape
    nq = S // ATT_BLOCK
    q_spec, kv_spec, sink_spec = _attn_specs(nq)

    def body(q_ref, kp, kc, kn, vp, vc, vn, s_ref, o_ref):
        valid = _attn_mask(pl.program_id(2), S)
        kb = jnp.concatenate([kp[0, 0], kc[0, 0], kn[0, 0]], axis=0)
        vb = jnp.concatenate([vp[0, 0], vc[0, 0], vn[0, 0]], axis=0)
        for g in range(ATT_GROUP):
            o_ref[0, g] = _attn_head(q_ref[0, g], kb, vb, s_ref[0, g], valid).astype(BF16)

    return pl.pallas_call(
        body, name="attn_fwd", grid=(ATT_KV_HEADS, B, nq),
        in_specs=[q_spec, kv_spec(-1), kv_spec(0), kv_spec(1), kv_spec(-1), kv_spec(0), kv_spec(1), sink_spec],
        out_specs=q_spec, out_shape=jax.ShapeDtypeStruct(q.shape, BF16),
        compiler_params=_cparams(("parallel", "parallel", "arbitrary")),
    )(q, k, k, k, v, v, v, sink)


def _attn_bwd(q, k, v, sink, dy):
    B, _, S, _ = q.shape
    nq = S // ATT_BLOCK
    q_spec, kv_spec, sink_spec = _attn_specs(nq)
    kv_full = pl.BlockSpec((1, 1, S, ATT_HEAD_DIM), lambda h, b, n: (b, h, 0, 0))

    def body(q_ref, kp, kc, kn, vp, vc, vn, s_ref, dy_ref, dq_ref, dk_ref, dv_ref, ds_ref):
        b, n = pl.program_id(1), pl.program_id(2)
        valid = _attn_mask(n, S)
        kb = jnp.concatenate([kp[0, 0], kc[0, 0], kn[0, 0]], axis=0)
        vb = jnp.concatenate([vp[0, 0], vc[0, 0], vn[0, 0]], axis=0)

        @pl.when(n == 0)
        def _():
            dk_ref[...] = jnp.zeros_like(dk_ref)
            dv_ref[...] = jnp.zeros_like(dv_ref)

        @pl.when((n == 0) & (b == 0))
        def _():
            ds_ref[...] = jnp.zeros_like(ds_ref)

        dkb = jnp.zeros_like(kb)
        dvb = jnp.zeros_like(vb)
        for g in range(ATT_GROUP):
            _, vjp = jax.vjp(functools.partial(_attn_head, valid=valid), q_ref[0, g], kb, vb, s_ref[0, g])
            dq, dk_g, dv_g, dsink = vjp(dy_ref[0, g])
            dq_ref[0, g] = dq
            ds_ref[0, g] += dsink
            dkb += dk_g
            dvb += dv_g
        for j, off in enumerate((-1, 0, 1)):
            start = pl.multiple_of(jnp.clip(n + off, 0, nq - 1) * ATT_BLOCK, ATT_BLOCK)
            rows = pl.ds(start, ATT_BLOCK)
            dk_ref[0, 0, rows, :] += dkb[j * ATT_BLOCK:(j + 1) * ATT_BLOCK]
            dv_ref[0, 0, rows, :] += dvb[j * ATT_BLOCK:(j + 1) * ATT_BLOCK]

    return pl.pallas_call(
        body, name="attn_bwd", grid=(ATT_KV_HEADS, B, nq),
        in_specs=[q_spec, kv_spec(-1), kv_spec(0), kv_spec(1), kv_spec(-1), kv_spec(0), kv_spec(1), sink_spec, q_spec],
        out_specs=[q_spec, kv_full, kv_full, sink_spec],
        out_shape=[jax.ShapeDtypeStruct(q.shape, F32), jax.ShapeDtypeStruct(k.shape, F32),
                   jax.ShapeDtypeStruct(v.shape, F32), jax.ShapeDtypeStruct(sink.shape, F32)],
        compiler_params=_cparams(("arbitrary", "arbitrary", "arbitrary")),
    )(q, k, k, k, v, v, v, sink, dy)


CONV_COLS = 256


def _conv_taps(u, seq):
    row = lax.broadcasted_iota(jnp.int32, u.shape, 0)
    prev = jnp.where(row == 0, 0.0, pltpu.roll(u, 1, axis=0))
    nxt = jnp.where(row == seq - 1, 0.0, pltpu.roll(u, seq - 1, axis=0))
    return prev, nxt


def _conv_fwd(proj3, w8):
    B, S, _ = proj3.shape
    ncb = 2 * MLSTM_WIDTH // CONV_COLS

    def body(u_ref, w_ref, o_ref):
        u = u_ref[0]
        prev, nxt = _conv_taps(u, S)
        o_ref[0] = _silu(prev * w_ref[0:1, :] + u * w_ref[1:2, :] + nxt * w_ref[2:3, :] + w_ref[3:4, :])

    return pl.pallas_call(
        body, name="conv_fwd", grid=(B, ncb),
        in_specs=[pl.BlockSpec((1, S, CONV_COLS), lambda b, c: (b, 0, C_QK // CONV_COLS + c)),
                  pl.BlockSpec((8, CONV_COLS), lambda b, c: (0, c))],
        out_specs=pl.BlockSpec((1, S, CONV_COLS), lambda b, c: (b, 0, c)),
        out_shape=jax.ShapeDtypeStruct((B, S, 2 * MLSTM_WIDTH), F32),
        compiler_params=_cparams(("parallel", "parallel")),
    )(proj3, w8)


def _conv_bwd(proj3, w8, dout_f, dout_b):
    B, S, _ = proj3.shape
    ncb = 2 * MLSTM_WIDTH // CONV_COLS

    def body(u_ref, w_ref, df_ref, db_ref, du_ref, dw_ref):
        b = pl.program_id(1)
        u = u_ref[0]
        prev, nxt = _conv_taps(u, S)
        w0, w1, w2 = w_ref[0:1, :], w_ref[1:2, :], w_ref[2:3, :]
        pre = prev * w0 + u * w1 + nxt * w2 + w_ref[3:4, :]
        sig = _sigmoid(pre)
        dpre = (df_ref[0] + db_ref[0]) * (sig * (1.0 + pre * (1.0 - sig)))
        dprev, dnxt = _conv_taps(dpre, S)
        du_ref[0] = dnxt * w0 + dpre * w1 + dprev * w2

        @pl.when(b == 0)
        def _():
            dw_ref[...] = jnp.zeros_like(dw_ref)

        dw_ref[0:1, :] += jnp.sum(dpre * prev, axis=0, keepdims=True)
        dw_ref[1:2, :] += jnp.sum(dpre * u, axis=0, keepdims=True)
        dw_ref[2:3, :] += jnp.sum(dpre * nxt, axis=0, keepdims=True)
        dw_ref[3:4, :] += jnp.sum(dpre, axis=0, keepdims=True)

    blk = pl.BlockSpec((1, S, CONV_COLS), lambda c, b: (b, 0, c))
    return pl.pallas_call(
        body, name="conv_bwd", grid=(ncb, B),
        in_specs=[pl.BlockSpec((1, S, CONV_COLS), lambda c, b: (b, 0, C_QK // CONV_COLS + c)),
                  pl.BlockSpec((8, CONV_COLS), lambda c, b: (0, c)), blk, blk],
        out_specs=[blk, pl.BlockSpec((8, CONV_COLS), lambda c, b: (0, c))],
        out_shape=[jax.ShapeDtypeStruct((B, S, 2 * MLSTM_WIDTH), F32), jax.ShapeDtypeStruct((8, 2 * MLSTM_WIDTH), F32)],
        compiler_params=_cparams(("parallel", "arbitrary")),
    )(proj3, w8, dout_f, dout_b)


def _chunk_masks(direction):
    t = lax.broadcasted_iota(jnp.int32, (MLSTM_CHUNK, MLSTM_CHUNK), 0)
    s = lax.broadcasted_iota(jnp.int32, (MLSTM_CHUNK, MLSTM_CHUNK), 1)
    le, ge = (s <= t).astype(F32), (s >= t).astype(F32)
    eye = (s == t).astype(F32)
    return (le, ge, eye) if direction == 0 else (ge, le, eye)


def _gate_cols(gates, direction, head):
    lane = lax.broadcasted_iota(jnp.int32, gates.shape, 1)
    sel_i = (lane == (2 * direction) * MLSTM_HEADS + head).astype(F32)
    sel_f = (lane == (2 * direction + 1) * MLSTM_HEADS + head).astype(F32)
    return sel_i, sel_f


def _mlstm_fwd(qk, proj3, bias):
    B, S, _ = qk.shape
    nc = S // MLSTM_CHUNK
    H, L, DH = MLSTM_HEADS, MLSTM_CHUNK, MLSTM_HEAD_DIM

    def chunk_of(d, c):
        return c if d == 0 else nc - 1 - c

    def body(qkf, qkb, vf, vb, gf, gb, bias_ref, hf, hb, csf, csb, nsf, nsb, msf, msb, c_st, n_st, m_st):
        c, h = pl.program_id(1), pl.program_id(2)

        @pl.when(c == 0)
        def _():
            for d in range(2):
                c_st[d, h] = jnp.zeros((DH, DH), F32)
                n_st[d, h] = jnp.zeros((1, DH), F32)
                m_st[d, h] = jnp.zeros((1, DH), F32)

        for d, (qk_ref, v_ref, g_ref, h_ref, cs, ns, ms) in enumerate(
                ((qkf, vf, gf, hf, csf, nsf, msf), (qkb, vb, gb, hb, csb, nsb, msb))):
            incl, incl_t, eye = _chunk_masks(d)
            gates = g_ref[0] + bias_ref[...]
            sel_i, sel_f = _gate_cols(gates, d, h)
            li = jnp.sum(gates * sel_i, axis=1, keepdims=True)
            lf_pre = jnp.sum(gates * sel_f, axis=1, keepdims=True)
            c_in, n_in, m_in = c_st[d, h], n_st[d, h], m_st[d, h]
            cs[0, 0, 0], ns[0, 0, 0], ms[0, 0, 0] = c_in, n_in, m_in
            hh, c_new, n_new, m_new = _mlstm_chunk(
                qk_ref[0, :, :DH], qk_ref[0, :, DH:], v_ref[0], li, lf_pre, c_in, n_in,
                jnp.max(m_in, axis=1, keepdims=True), incl, incl_t, eye)
            h_ref[0] = hh
            c_st[d, h], n_st[d, h] = c_new, n_new
            m_st[d, h] = jnp.broadcast_to(m_new, (1, DH))

    def tok_spec(width, base, d, per_head):
        return pl.BlockSpec((1, L, width), lambda b, c, h: (b, chunk_of(d, c), base + (h if per_head else 0)))

    def st_spec(shape, d):
        return pl.BlockSpec((1, 1, 1) + shape, lambda b, c, h: (b, chunk_of(d, c), h, 0, 0))

    in_specs = [tok_spec(2 * DH, 0, 0, True), tok_spec(2 * DH, 0, 1, True),
                tok_spec(DH, C_VM // DH, 0, True), tok_spec(DH, C_VM // DH, 1, True),
                tok_spec(LANES, C_GATES // LANES, 0, False), tok_spec(LANES, C_GATES // LANES, 1, False),
                pl.BlockSpec((1, LANES), lambda b, c, h: (0, 0))]
    out_specs = [tok_spec(DH, 0, 0, True), tok_spec(DH, 0, 1, True),
                 st_spec((DH, DH), 0), st_spec((DH, DH), 1), st_spec((1, DH), 0), st_spec((1, DH), 1),
                 st_spec((1, DH), 0), st_spec((1, DH), 1)]
    hs = jax.ShapeDtypeStruct((B, S, H * DH), F32)
    cs = jax.ShapeDtypeStruct((B, nc, H, DH, DH), F32)
    vs = jax.ShapeDtypeStruct((B, nc, H, 1, DH), F32)
    return pl.pallas_call(
        body, name="mlstm_fwd", grid=(B, nc, H), in_specs=in_specs, out_specs=out_specs,
        out_shape=[hs, hs, cs, cs, vs, vs, vs, vs],
        scratch_shapes=[pltpu.VMEM((2, H, DH, DH), F32), pltpu.VMEM((2, H, 1, DH), F32), pltpu.VMEM((2, H, 1, DH), F32)],
        compiler_params=_cparams(("parallel", "arbitrary", "arbitrary")),
    )(qk, qk, proj3, proj3, proj3, proj3, bias)


def _mlstm_bwd(qk, proj3, bias, states, dh):
    B, S, _ = qk.shape
    nc = S // MLSTM_CHUNK
    H, L, DH = MLSTM_HEADS, MLSTM_CHUNK, MLSTM_HEAD_DIM

    def chunk_of(d, c):
        return nc - 1 - c if d == 0 else c

    def body(qkf, qkb, vf, vb, gf, gb, bias_ref, csf, csb, nsf, nsb, msf, msb, dhf, dhb,
             dqkf, dqkb, dvf, dvb, dgf, dgb, dc_st, dn_st, dm_st):
        c, h = pl.program_id(1), pl.program_id(2)

        @pl.when(c == 0)
        def _():
            for d in range(2):
                dc_st[d, h] = jnp.zeros((DH, DH), F32)
                dn_st[d, h] = jnp.zeros((1, DH), F32)
                dm_st[d, h] = jnp.zeros((1, DH), F32)

        @pl.when(h == 0)
        def _():
            dgf[...] = jnp.zeros_like(dgf)
            dgb[...] = jnp.zeros_like(dgb)

        for d, (qk_ref, v_ref, g_ref, cs, ns, ms, dh_ref, dqk_ref, dv_ref, dg_ref) in enumerate(
                ((qkf, vf, gf, csf, nsf, msf, dhf, dqkf, dvf, dgf), (qkb, vb, gb, csb, nsb, msb, dhb, dqkb, dvb, dgb))):
            incl, incl_t, eye = _chunk_masks(d)
            gates = g_ref[0] + bias_ref[...]
            sel_i, sel_f = _gate_cols(gates, d, h)
            li = jnp.sum(gates * sel_i, axis=1, keepdims=True)
            lf_pre = jnp.sum(gates * sel_f, axis=1, keepdims=True)
            m_in = jnp.max(ms[0, 0, 0], axis=1, keepdims=True)
            _, vjp = jax.vjp(
                functools.partial(_mlstm_chunk, incl=incl, incl_t=incl_t, eye=eye),
                qk_ref[0, :, :DH], qk_ref[0, :, DH:], v_ref[0], li, lf_pre, cs[0, 0, 0], ns[0, 0, 0], m_in)
            dm_out = jnp.max(dm_st[d, h], axis=1, keepdims=True)
            dq, dk, dv, dli, dlf, dc, dn, dm = vjp((dh_ref[0], dc_st[d, h], dn_st[d, h], dm_out))
            dqk_ref[0, :, :DH] = dq
            dqk_ref[0, :, DH:] = dk
            dv_ref[0] = dv
            dg_ref[0] += dli * sel_i + dlf * sel_f
            dc_st[d, h], dn_st[d, h] = dc, dn
            dm_st[d, h] = jnp.broadcast_to(dm, (1, DH))

    def tok_spec(width, base, d, per_head):
        return pl.BlockSpec((1, L, width), lambda b, c, h: (b, chunk_of(d, c), base + (h if per_head else 0)))

    def st_spec(shape, d):
        return pl.BlockSpec((1, 1, 1) + shape, lambda b, c, h: (b, chunk_of(d, c), h, 0, 0))

    in_specs = [tok_spec(2 * DH, 0, 0, True), tok_spec(2 * DH, 0, 1, True),
                tok_spec(DH, C_VM // DH, 0, True), tok_spec(DH, C_VM // DH, 1, True),
                tok_spec(LANES, C_GATES // LANES, 0, False), tok_spec(LANES, C_GATES // LANES, 1, False),
                pl.BlockSpec((1, LANES), lambda b, c, h: (0, 0)),
                st_spec((DH, DH), 0), st_spec((DH, DH), 1), st_spec((1, DH), 0), st_spec((1, DH), 1),
                st_spec((1, DH), 0), st_spec((1, DH), 1), tok_spec(DH, 0, 0, True), tok_spec(DH, 0, 1, True)]
    out_specs = [tok_spec(2 * DH, 0, 0, True), tok_spec(2 * DH, 0, 1, True), tok_spec(DH, 0, 0, True), tok_spec(DH, 0, 1, True),
                 tok_spec(LANES, 0, 0, False), tok_spec(LANES, 0, 1, False)]
    qks = jax.ShapeDtypeStruct((B, S, 2 * H * DH), F32)
    vs = jax.ShapeDtypeStruct((B, S, H * DH), F32)
    gs = jax.ShapeDtypeStruct((B, S, LANES), F32)
    csf, csb, nsf, nsb, msf, msb = states
    return pl.pallas_call(
        body, name="mlstm_bwd", grid=(B, nc, H), in_specs=in_specs, out_specs=out_specs,
        out_shape=[qks, qks, vs, vs, gs, gs],
        scratch_shapes=[pltpu.VMEM((2, H, DH, DH), F32), pltpu.VMEM((2, H, 1, DH), F32), pltpu.VMEM((2, H, 1, DH), F32)],
        compiler_params=_cparams(("parallel", "arbitrary", "arbitrary")),
    )(qk, qk, proj3, proj3, proj3, proj3, bias, csf, csb, nsf, nsb, msf, msb, dh, dh)


ROW_BLOCK = 256
FF_COLS = 512
FF_SHARD = D_FF // N_DEV
FF_SHARD_PAD = 384
FF_PAD = N_DEV * FF_SHARD_PAD


def _rms_fwd(name, x, g):
    T = x.shape[0]
    return _rowwise(name, lambda xv, gv: _rms(xv, gv), [_In(x), _In(g, rows=False)], [_Out(D_MODEL, BF16)], T, ROW_BLOCK)[0]


def _rms_bwd(name, x, g, dh, dres):
    T = x.shape[0]

    def fn(xv, gv, dhv, drv):
        _, vjp = jax.vjp(_rms, xv, gv)
        dx, dg = vjp(dhv)
        return drv + dx, dg

    return _rowwise(name, fn, [_In(x), _In(g, rows=False), _In(dh), _In(dres)],
                    [_Out(D_MODEL), _Out(D_MODEL, rows=False)], T, ROW_BLOCK)


def _mmw(name, a, w, mode, **kw):
    if isinstance(w, tuple):
        return _matmul(name, a, w[0], mode, bl=w[1], **kw)
    return _matmul(name, a, w, mode, **kw)


def _ffn_fwd(tag, x, g, wg, wu, wd):
    T = x.shape[0]
    h = _rms_fwd(tag + "_norm", x, g)
    gate = _mmw(tag + "_gate", h, wg, "nn")
    up = _mmw(tag + "_up", h, wu, "nn")
    act = _rowwise(tag + "_act", lambda a, b: _silu(a) * b,
                   [_In(gate, FF_COLS, split=True), _In(up, FF_COLS, split=True)],
                   [_Out(FF_PAD, BF16, FF_COLS, split=True)], T, 1024, ncol=FF_PAD // FF_COLS)[0]
    if callable(wd):
        wd = wd(act)
    out = _mmw(tag + "_down", act, wd, "nn", res=x, scale=0.5)
    return out, (x, h, gate, up, act), wd


def _ffn_bwd(tag, saved, g, wg, wu, wd, dx, on_dw):
    x, h, gate, up, act = saved
    T = x.shape[0]
    dact = _mmw(tag + "_dact", dx, wd, "nt", scale=0.5)
    dwd = _matmul(tag + "_dwd", act, dx, "tn", scale=0.5)

    def fn(a, b, da):
        _, vjp = jax.vjp(lambda p, q: _silu(p) * q, a, b)
        return vjp(da)

    dgate, dup = _rowwise(tag + "_dactfn", fn,
                          [_In(gate, FF_COLS, split=True), _In(up, FF_COLS, split=True), _In(dact, FF_COLS, split=True)],
                          [_Out(FF_PAD, BF16, FF_COLS, split=True), _Out(FF_PAD, BF16, FF_COLS, split=True)],
                          T, 1024, ncol=FF_PAD // FF_COLS)
    dwg = _matmul(tag + "_dwg", h, dgate, "tn")
    dwu = _matmul(tag + "_dwu", h, dup, "tn")
    token = on_dw({tag + "_w_gate": dwg, tag + "_w_up": dwu, tag + "_w_down": dwd}, dwu)
    dh = _mmw(tag + "_dh1", dgate, wg, "nt", dep=token)
    dh = _mmw(tag + "_dh2", dup, wu, "nt", res=dh)
    dx_new, dg = _rms_bwd(tag + "_dnorm", x, g, dh, dx)
    return dx_new, dg


def _rope_consts():
    half = ROPE_DIM // 2
    inv_freq = jnp.power(jnp.float32(ROPE_THETA), -jnp.arange(half, dtype=F32) * (2.0 / ROPE_DIM))
    row = jnp.zeros((1, ATT_HEAD_DIM), F32).at[0, :ROPE_DIM].set(jnp.concatenate([inv_freq, inv_freq]))
    rot = np.zeros((ATT_HEAD_DIM, ATT_HEAD_DIM), np.float32)
    for i in range(half):
        rot[half + i, i] = -1.0
        rot[i, half + i] = 1.0
    return row, jnp.asarray(rot)


def _prep_fwd(name, t, g, pos, inv_freq_row, rot):
    R = t.shape[0]

    def fn(tv, gv, pv, fv, rv):
        cos, sin = _rope_tables(pv, fv)
        return _qk_prep(tv, gv, cos, sin, rv)

    return _rowwise(name, fn, [_In(t), _In(g, rows=False), _In(pos), _In(inv_freq_row, rows=False), _In(rot, rows=False)],
                    [_Out(ATT_HEAD_DIM)], R, 1024)[0]


def _prep_bwd(name, t, g, pos, inv_freq_row, rot, dout):
    R = t.shape[0]

    def fn(tv, gv, pv, fv, rv, dv):
        cos, sin = _rope_tables(pv, fv)
        _, vjp = jax.vjp(lambda a, b: _qk_prep(a, b, cos, sin, rv), tv, gv)
        return vjp(dv)

    return _rowwise(name, fn, [_In(t), _In(g, rows=False), _In(pos), _In(inv_freq_row, rows=False), _In(rot, rows=False), _In(dout)],
                    [_Out(ATT_HEAD_DIM), _Out(ATT_HEAD_DIM, rows=False)], R, 1024)


def _to_heads(t, B, S, nh):
    return t.reshape(B, S, nh, ATT_HEAD_DIM).transpose(0, 2, 1, 3)


def _from_heads(t):
    B, nh, S, _ = t.shape
    return t.transpose(0, 2, 1, 3).reshape(B * S, nh * ATT_HEAD_DIM)


def _mix_fwd(x, pos_q, pos_k, B, S, p):
    T = B * S
    h = _rms_fwd("mix_norm", x, p["mix_norm"])
    proj = _matmul("mix_proj", h, p["w_in"], "nn")
    proj3 = proj.reshape(B, S, IN_PAD)
    inv_freq_row, rot = _rope_consts()
    qa = proj[:, C_QA:C_QA + ATT_WIDTH].reshape(T * ATT_HEADS, ATT_HEAD_DIM)
    ka = proj[:, C_KA:C_KA + ATT_KV_WIDTH].reshape(T * ATT_KV_HEADS, ATT_HEAD_DIM)
    q_r = _prep_fwd("q_prep", qa, p["attn_q_norm"], pos_q, inv_freq_row, rot)
    k_r = _prep_fwd("k_prep", ka, p["attn_k_norm"], pos_k, inv_freq_row, rot)
    qh = _to_heads(q_r, B, S, ATT_HEADS)
    kh = _to_heads(k_r, B, S, ATT_KV_HEADS)
    vh = _to_heads(proj[:, C_VA:C_VA + ATT_KV_WIDTH], B, S, ATT_KV_HEADS)
    sink = p["attn_sink"].reshape(ATT_KV_HEADS, ATT_GROUP, 1, 1)
    y_a = _from_heads(_attn_fwd(qh, kh, vh, sink))

    qk_c = _conv_fwd(proj3, p["conv_w8"])
    hf, hb, *states = _mlstm_fwd(qk_c, proj3, p["gate_bias"])
    hf2, hb2 = hf.reshape(T, MLSTM_WIDTH), hb.reshape(T, MLSTM_WIDTH)
    DH = MLSTM_HEAD_DIM
    y_m = _rowwise("mlstm_out", _mlstm_combine,
                   [_In(hf2, DH, split=True), _In(hb2, DH, split=True), _In(proj, DH, C_OM // DH, split=True),
                    _In(p["mlstm_out_norm"], DH, split=True, rows=False)],
                   [_Out(MLSTM_WIDTH, BF16, DH, split=True)], T, 1024, ncol=MLSTM_HEADS)[0]

    za = _mmw("branch_a", y_a, p["w_branch_attn"], "nn")
    zm = _mmw("branch_m", y_m, p["w_branch_mlstm"], "nn")
    W = 512
    merged = _rowwise("merge", _merge,
                      [_In(proj, W, C_GMERGE // W, split=True), _In(proj, W, (C_GMERGE + D_MODEL) // W, split=True),
                       _In(za, W, split=True), _In(zm, W, split=True)],
                      [_Out(D_MODEL, BF16, W, split=True)], T, 512, ncol=D_MODEL // W)[0]
    out = _mmw("mix_out", merged, p["w_out"], "nn", res=x)
    saved = dict(x=x, h=h, proj=proj, qa=qa, ka=ka, qh=qh, kh=kh, vh=vh, sink=sink, y_a=y_a, qk_c=qk_c, hf=hf2, hb=hb2,
                 states=states, y_m=y_m, za=za, zm=zm, merged=merged)
    return out, saved


def _mix_bwd(sv, pos_q, pos_k, B, S, p, dx, on_dw):
    T = B * S
    DH = MLSTM_HEAD_DIM
    proj = sv["proj"]
    proj3 = proj.reshape(B, S, IN_PAD)
    inv_freq_row, rot = _rope_consts()
    g = {}
    dmerged = _mmw("mix_dmerged", dx, p["w_out"], "nt")
    g["w_out"] = _matmul("mix_dwout", sv["merged"], dx, "tn")
    W = 512

    def merge_bwd(ga, gm, za, zm, dm):
        _, vjp = jax.vjp(_merge, ga, gm, za, zm)
        return vjp(dm)

    dga, dgm, dza, dzm = _rowwise(
        "merge_bwd", merge_bwd,
        [_In(proj, W, C_GMERGE // W, split=True), _In(proj, W, (C_GMERGE + D_MODEL) // W, split=True),
         _In(sv["za"], W, split=True), _In(sv["zm"], W, split=True), _In(dmerged, W, split=True)],
        [_Out(D_MODEL, F32, W, split=True), _Out(D_MODEL, F32, W, split=True),
         _Out(D_MODEL, BF16, W, split=True), _Out(D_MODEL, BF16, W, split=True)], T, 512, ncol=D_MODEL // W)
    dya = _mmw("branch_a_dx", dza, p["w_branch_attn"], "nt")
    g["w_branch_attn"] = _matmul("branch_a_dw", sv["y_a"], dza, "tn")
    dym = _mmw("branch_m_dx", dzm, p["w_branch_mlstm"], "nt")
    g["w_branch_mlstm"] = _matmul("branch_m_dw", sv["y_m"], dzm, "tn")

    def combine_bwd(hf, hb, o_pre, gn, dy):
        _, vjp = jax.vjp(_mlstm_combine, hf, hb, o_pre, gn)
        dhf, _, do, dg = vjp(dy)
        return dhf, do, dg

    dh, dom, g["mlstm_out_norm"] = _rowwise(
        "mlstm_out_bwd", combine_bwd,
        [_In(sv["hf"], DH, split=True), _In(sv["hb"], DH, split=True), _In(proj, DH, C_OM // DH, split=True),
         _In(p["mlstm_out_norm"], DH, split=True, rows=False), _In(dym, DH, split=True)],
        [_Out(MLSTM_WIDTH, F32, DH, split=True), _Out(MLSTM_WIDTH, F32, DH, split=True),
         _Out(MLSTM_WIDTH, F32, DH, split=True, rows=False)], T, 1024, ncol=MLSTM_HEADS)
    dqk_f, dqk_b, dv_f, dv_b, dg_f, dg_b = _mlstm_bwd(sv["qk_c"], proj3, p["gate_bias"], sv["states"],
                                                       dh.reshape(B, S, MLSTM_WIDTH))
    dgates, dvm, g["gate_bias"] = _rowwise(
        "mlstm_dsum", lambda a, b, c, d: (a + b, c + d, jnp.sum(a + b, axis=0, keepdims=True)),
        [_In(dg_f.reshape(T, LANES)), _In(dg_b.reshape(T, LANES)), _In(dv_f.reshape(T, MLSTM_WIDTH)), _In(dv_b.reshape(T, MLSTM_WIDTH))],
        [_Out(LANES), _Out(MLSTM_WIDTH), _Out(LANES, rows=False)], T, 1024)
    dqk, g["conv_w8"] = _conv_bwd(proj3, p["conv_w8"], dqk_f, dqk_b)

    dyh = _to_heads(dya, B, S, ATT_HEADS)
    dqh, dkh, dvh, dsink = _attn_bwd(sv["qh"], sv["kh"], sv["vh"], sv["sink"], dyh)
    g["attn_sink"] = dsink.reshape(1, ATT_HEADS)
    dq_r = dqh.transpose(0, 2, 1, 3).reshape(T * ATT_HEADS, ATT_HEAD_DIM)
    dk_r = dkh.transpose(0, 2, 1, 3).reshape(T * ATT_KV_HEADS, ATT_HEAD_DIM)
    dva = _from_heads(dvh)
    dqa, g["attn_q_norm"] = _prep_bwd("q_prep_bwd", sv["qa"], p["attn_q_norm"], pos_q, inv_freq_row, rot, dq_r)
    dka, g["attn_k_norm"] = _prep_bwd("k_prep_bwd", sv["ka"], p["attn_k_norm"], pos_k, inv_freq_row, rot, dk_r)

    dproj = jnp.concatenate(
        [dga.astype(BF16), dgm.astype(BF16), dqk.reshape(T, 2 * MLSTM_WIDTH).astype(BF16), dvm.astype(BF16), dom.astype(BF16),
         dqa.reshape(T, ATT_WIDTH).astype(BF16), dka.reshape(T, ATT_KV_WIDTH).astype(BF16), dva.astype(BF16),
         dgates.astype(BF16)], axis=1)
    dwin = _matmul("mix_dwin", sv["h"], dproj, "tn")
    token = on_dw({"w_in": _w_in_to_slots(dwin), "w_branch_attn": g.pop("w_branch_attn"),
                   "w_branch_mlstm": g.pop("w_branch_mlstm"), "w_out": g.pop("w_out")}, dwin)
    dh2 = _matmul("mix_dh", dproj, p["w_in"], "nt", dep=token)
    dx_new, g["mix_norm"] = _rms_bwd("mix_dnorm", sv["x"], p["mix_norm"], dh2, dx)
    return dx_new, g


def _loss_and_grad(x, g, target):
    T = x.shape[0]

    def loss_fn(xv, gv, tv):
        err = jnp.square(_rms(xv, gv) - tv)
        return 0.5 * jnp.sum(jnp.mean(err, axis=-1, keepdims=True), axis=0, keepdims=True)

    def fn(xv, gv, tv):
        val, vjp = jax.vjp(lambda a, b: loss_fn(a, b, tv), xv, gv)
        dx, dg = vjp(jnp.ones((1, 1), F32))
        return val, dx, dg

    return _rowwise("loss_head", fn, [_In(x), _In(g, rows=False), _In(target)],
                    [_Out(1, rows=False), _Out(D_MODEL), _Out(D_MODEL, rows=False)], T, ROW_BLOCK)


def _block_norm_fwd(x, g):
    T = x.shape[0]
    return _rowwise("block_norm", _rms, [_In(x), _In(g, rows=False)], [_Out(D_MODEL)], T, ROW_BLOCK)[0]


def _block_norm_bwd(x, g, dy):
    T = x.shape[0]

    def fn(xv, gv, dv):
        _, vjp = jax.vjp(_rms, xv, gv)
        return vjp(dv)

    return _rowwise("block_norm_bwd", fn, [_In(x), _In(g, rows=False), _In(dy)],
                    [_Out(D_MODEL), _Out(D_MODEL, rows=False)], T, ROW_BLOCK)


def _qk_perm_cols(t, axis):
    q, k = jnp.split(t, 2, axis=axis)
    parts = []
    for h in range(MLSTM_HEADS):
        sl = [slice(None)] * t.ndim
        sl[axis] = slice(h * MLSTM_HEAD_DIM, (h + 1) * MLSTM_HEAD_DIM)
        parts += [q[tuple(sl)], k[tuple(sl)]]
    return jnp.concatenate(parts, axis=axis)


def _qk_unperm_cols(t, axis):
    qs, ks = [], []
    for h in range(MLSTM_HEADS):
        sl = [slice(None)] * t.ndim
        sl[axis] = slice(2 * h * MLSTM_HEAD_DIM, (2 * h + 1) * MLSTM_HEAD_DIM)
        qs.append(t[tuple(sl)])
        sl[axis] = slice((2 * h + 1) * MLSTM_HEAD_DIM, (2 * h + 2) * MLSTM_HEAD_DIM)
        ks.append(t[tuple(sl)])
    return jnp.concatenate(qs + ks, axis=axis)


def _w_in_arrange(w):
    qa, ka, va, qm, km, vm, om, gm, gmerge = jnp.split(w, np.cumsum(
        (ATT_WIDTH, ATT_KV_WIDTH, ATT_KV_WIDTH, MLSTM_WIDTH, MLSTM_WIDTH, MLSTM_WIDTH, MLSTM_WIDTH, MLSTM_N_GATES))[:].tolist(), axis=1)
    qk = _qk_perm_cols(jnp.concatenate([qm, km], axis=1), 1)
    pad = jnp.zeros((w.shape[0], LANES - MLSTM_N_GATES), w.dtype)
    return jnp.concatenate([gmerge, qk, vm, om, qa, ka, va, gm, pad], axis=1)


def _w_in_restore(w):
    gmerge = w[:, C_GMERGE:C_GMERGE + 2 * D_MODEL]
    qk = _qk_unperm_cols(w[:, C_QK:C_QK + 2 * MLSTM_WIDTH], 1)
    vm, om = w[:, C_VM:C_VM + MLSTM_WIDTH], w[:, C_OM:C_OM + MLSTM_WIDTH]
    qa, ka, va = w[:, C_QA:C_QA + ATT_WIDTH], w[:, C_KA:C_KA + ATT_KV_WIDTH], w[:, C_VA:C_VA + ATT_KV_WIDTH]
    gm = w[:, C_GATES:C_GATES + MLSTM_N_GATES]
    return jnp.concatenate([qa, ka, va, qk, vm, om, gm, gmerge], axis=1)


BIG = ("ffn1_w_gate", "ffn1_w_up", "ffn1_w_down", "w_in", "mlstm_conv_w", "w_branch_attn", "w_branch_mlstm", "w_out",
       "ffn2_w_gate", "ffn2_w_up", "ffn2_w_down")
MATMUL_W = tuple(n for n in BIG if n != "mlstm_conv_w")
SMALL = ("ffn1_norm", "mix_norm", "mlstm_gate_bias", "attn_q_norm", "attn_k_norm", "attn_sink", "mlstm_conv_b",
         "mlstm_out_norm", "ffn2_norm", "block_out_norm")
WEIGHTS = ("ffn1_norm", "ffn1_w_gate", "ffn1_w_up", "ffn1_w_down", "mix_norm", "w_in", "mlstm_gate_bias", "attn_q_norm",
           "attn_k_norm", "attn_sink", "mlstm_conv_w", "mlstm_conv_b", "mlstm_out_norm", "w_branch_attn", "w_branch_mlstm",
           "w_out", "ffn2_norm", "ffn2_w_gate", "ffn2_w_up", "ffn2_w_down", "block_out_norm")
PACK_COLS = 1024


def _padded_rows(n_elems):
    return -(-n_elems // PACK_COLS)


def _pack_flat(arrs, dtype, row_multiple):
    parts = []
    for a in arrs:
        flat = a.reshape(-1).astype(dtype)
        pad = _padded_rows(flat.shape[0]) * PACK_COLS - flat.shape[0]
        parts.append(jnp.pad(flat, (0, pad)) if pad else flat)
    flat = jnp.concatenate(parts)
    rows = flat.shape[0] // PACK_COLS
    extra = (-rows) % row_multiple
    if extra:
        flat = jnp.pad(flat, (0, extra * PACK_COLS))
    return flat.reshape(-1, PACK_COLS)


def _unpack_flat(buf, shapes, lead=()):
    flat = buf.reshape(lead + (-1,))
    out, off = [], 0
    for s in shapes:
        n = int(np.prod(s))
        out.append(flat[..., off:off + n].reshape(lead + tuple(s)))
        off += _padded_rows(n) * PACK_COLS
    return out


class _Lay:
    def __init__(self, shard, axis, width):
        self.shard, self.axis, self.width = shard, axis, width
        self.padded = tuple(width if a == axis else s for a, s in enumerate(shard))
        self.whole = tuple(N_DEV * width if a == axis else s for a, s in enumerate(shard))

    def pad(self, t, lead=0):
        extra = self.width - self.shard[self.axis]
        if not extra:
            return t
        cfg = [(0, 0)] * t.ndim
        cfg[lead + self.axis] = (0, extra)
        return jnp.pad(t, cfg)

    def unpad(self, t, lead=0):
        idx = [slice(None)] * t.ndim
        idx[lead + self.axis] = slice(0, self.shard[self.axis])
        return t[tuple(idx)]


_FF_COL = _Lay((D_MODEL, FF_SHARD), 1, FF_SHARD_PAD)
_FF_ROW = _Lay((FF_SHARD, D_MODEL), 0, FF_SHARD_PAD)
LAYOUTS = {
    "ffn1_w_gate": _FF_COL, "ffn1_w_up": _FF_COL, "ffn1_w_down": _FF_ROW,
    "ffn2_w_gate": _FF_COL, "ffn2_w_up": _FF_COL, "ffn2_w_down": _FF_ROW,
    "w_in": _Lay((D_MODEL, IN_WIDTH // N_DEV), 0, D_MODEL),
    "mlstm_conv_w": _Lay((3, 2 * MLSTM_WIDTH // N_DEV), 1, 2 * MLSTM_WIDTH // N_DEV),
    "w_branch_attn": _Lay((ATT_WIDTH, D_MODEL // N_DEV), 1, D_MODEL // N_DEV),
    "w_branch_mlstm": _Lay((MLSTM_WIDTH, D_MODEL // N_DEV), 1, D_MODEL // N_DEV),
    "w_out": _Lay((D_MODEL // N_DEV, D_MODEL), 0, D_MODEL // N_DEV),
}


def _window(ref, axis, j, width):
    idx = [slice(None)] * len(ref.shape)
    idx[axis] = pl.ds(pl.multiple_of(j * width, width), width)
    return ref.at[tuple(idx)]


ANY = pl.BlockSpec(memory_space=pl.ANY)


def _mesh_pos():
    return lax.axis_index("x"), lax.axis_index("y"), lax.axis_index("c")


def _all_gather(name, shard, vmem=False):
    R, C = shard.shape
    space = pl.BlockSpec(memory_space=pltpu.VMEM) if vmem else ANY

    def body(x_ref, out_ref, send_sems, recv_sems, local_sem):
        x, y, c = _mesh_pos()
        me, sibling = (x, y, c), (x, y, 1 - c)
        chips = [(1 - x, y), (x, 1 - y), (1 - x, 1 - y)]

        def slot(px, py, pc):
            return out_ref.at[4 * px + 2 * py + pc]

        def copy(k, block, to, src=None):
            return pltpu.make_async_remote_copy(
                src_ref=slot(*block) if src is None else src, dst_ref=slot(*block),
                send_sem=send_sems.at[k], recv_sem=recv_sems.at[k], device_id=to, device_id_type=MESH)

        mine = pltpu.make_async_copy(x_ref, slot(*me), local_sem)
        mine.start()
        first = [copy(0, me, sibling, src=x_ref)]
        first += [copy(1 + j, me, (*chip, c), src=x_ref) for j, chip in enumerate(chips)]
        for cp in first:
            cp.start()
        passed = [copy(4 + j, (*chip, c), sibling) for j, chip in enumerate(chips)]
        for j, chip in enumerate(chips):
            copy(1 + j, (*chip, c), me).wait_recv()
            passed[j].start()
        copy(0, sibling, me).wait_recv()
        for j, chip in enumerate(chips):
            copy(4 + j, (*chip, 1 - c), me).wait_recv()
        for cp in first + passed:
            cp.wait_send()
        mine.wait()

    return pl.pallas_call(
        body, name=name, out_shape=jax.ShapeDtypeStruct((N_DEV, R, C), shard.dtype),
        in_specs=[space], out_specs=space,
        scratch_shapes=[pltpu.SemaphoreType.DMA((7,)), pltpu.SemaphoreType.DMA((7,)), pltpu.SemaphoreType.DMA],
    )(shard)


HBM = pl.BlockSpec(memory_space=pltpu.HBM)
SEM = pl.BlockSpec(memory_space=pltpu.SEMAPHORE)
SPLIT_COPY = pltpu.CompilerParams(has_side_effects=pltpu.SideEffectType.DATAFLOW_SIDE_EFFECTING)
N_PEERS = N_DEV - 1


def _peers(x, y, c):
    return [(x, y, 1 - c), (1 - x, y, c), (x, 1 - y, c), (1 - x, 1 - y, c),
            (1 - x, y, 1 - c), (x, 1 - y, 1 - c), (1 - x, 1 - y, 1 - c)]


def _dev_index(pos):
    return 4 * pos[0] + 2 * pos[1] + pos[2]


def _place_own(name, shards, lays):
    nt = len(shards)
    me = _dev_index(_mesh_pos())

    def body(me_ref, *refs):
        for x_ref, o_ref in zip(refs[:nt], refs[nt:]):
            o_ref[...] = x_ref[...]

    def window_spec(lay):
        if lay.axis == 0:
            return pl.BlockSpec(lay.padded, lambda i, me_ref: (me_ref[0], 0))
        return pl.BlockSpec(lay.padded, lambda i, me_ref: (0, me_ref[0]))

    return pl.pallas_call(
        body, name=name,
        grid_spec=pltpu.PrefetchScalarGridSpec(
            num_scalar_prefetch=1, grid=(1,),
            in_specs=[pl.BlockSpec(lay.padded, lambda i, me_ref: (0, 0)) for lay in lays],
            out_specs=[window_spec(lay) for lay in lays]),
        out_shape=[jax.ShapeDtypeStruct(lay.whole, s.dtype) for s, lay in zip(shards, lays)],
        compiler_params=_cparams(("arbitrary",)),
    )(me.reshape(1).astype(jnp.int32), *shards)


def _gather_start(name, shards, lands, lays, groups, after):
    nt, ng = len(shards), len(groups)

    def body(*refs):
        x_refs, land_refs = refs[:nt], refs[nt:2 * nt]
        sems = refs[2 * nt + 1:2 * nt + 1 + 2 * ng]
        pos = _mesh_pos()
        me = _dev_index(pos)
        for g, tens in enumerate(groups):
            for i, t in enumerate(tens):
                for k, peer in enumerate(_peers(*pos)):
                    pltpu.make_async_remote_copy(
                        src_ref=x_refs[t], dst_ref=_window(land_refs[t], lays[t].axis, me, lays[t].width),
                        send_sem=sems[2 * g].at[N_PEERS * i + k], recv_sem=sems[2 * g + 1].at[N_PEERS * i + k],
                        device_id=peer, device_id_type=MESH).start()

    sem_shapes = []
    for tens in groups:
        sem_shapes += [pltpu.SemaphoreType.DMA((N_PEERS * len(tens),))] * 2
    thru = [pltpu.HBM(s.shape, s.dtype) for s in shards] + [pltpu.HBM(lay.whole, s.dtype) for s, lay in zip(shards, lays)]
    args = [pltpu.with_memory_space_constraint(s, pltpu.HBM) for s in shards]
    args += [pltpu.with_memory_space_constraint(ld, pltpu.HBM) for ld in lands]
    res = pl.pallas_call(
        body, name=name, out_shape=tuple(sem_shapes + thru), in_specs=[HBM] * (2 * nt) + [ANY],
        out_specs=tuple([SEM] * (2 * ng) + [HBM] * (2 * nt)),
        input_output_aliases={t: 2 * ng + t for t in range(2 * nt)}, compiler_params=SPLIT_COPY,
    )(*args, after)
    sems = [(res[2 * g], res[2 * g + 1]) for g in range(ng)]
    return sems, list(res[2 * ng:2 * ng + nt]), list(res[2 * ng + nt:])


def _gather_wait(name, sems, shards, lands, lays, after):
    nt = len(shards)
    send_sems, recv_sems = sems

    def body(*refs):
        x_refs, land_refs = refs[:nt], refs[nt:2 * nt]
        send_ref, recv_ref = refs[2 * nt], refs[2 * nt + 1]
        pos = _mesh_pos()
        for t in range(nt):
            for k, peer in enumerate(_peers(*pos)):
                cp = pltpu.make_async_remote_copy(
                    src_ref=x_refs[t], dst_ref=_window(land_refs[t], lays[t].axis, _dev_index(peer), lays[t].width),
                    send_sem=send_ref.at[N_PEERS * t + k], recv_sem=recv_ref.at[N_PEERS * t + k],
                    device_id=peer, device_id_type=MESH)
                cp.wait_send()
                cp.wait_recv()

    thru = [pltpu.HBM(s.shape, s.dtype) for s in shards] + [pltpu.HBM(ld.shape, ld.dtype) for ld in lands]
    res = pl.pallas_call(
        body, name=name, out_shape=tuple(thru), in_specs=[HBM] * (2 * nt) + [SEM, SEM, ANY],
        out_specs=tuple([HBM] * (2 * nt)), input_output_aliases={t: t for t in range(2 * nt)},
        compiler_params=SPLIT_COPY,
    )(*shards, *lands, send_sems, recv_sems, after)
    return list(res[nt:])


def _pair_exchange(name, grads, lays):
    nt = len(grads)

    def body(*refs):
        g_refs, land_refs = refs[:nt], refs[nt:2 * nt]
        send_sems, recv_sems = refs[2 * nt:]
        x, y, c = _mesh_pos()
        copies = []
        for t in range(nt):
            for chip in range(4):
                copies.append(pltpu.make_async_remote_copy(
                    src_ref=_window(g_refs[t], lays[t].axis, 2 * chip + (1 - c), lays[t].width), dst_ref=land_refs[t].at[chip],
                    send_sem=send_sems.at[4 * t + chip], recv_sem=recv_sems.at[4 * t + chip],
                    device_id=(x, y, 1 - c), device_id_type=MESH))
        for cp in copies:
            cp.start()
        for cp in copies:
            cp.wait_recv()
        for cp in copies:
            cp.wait_send()

    out_shape = [jax.ShapeDtypeStruct((4,) + lay.padded, g.dtype) for g, lay in zip(grads, lays)]
    return pl.pallas_call(
        body, name=name, out_shape=out_shape, in_specs=[ANY] * nt, out_specs=[ANY] * nt,
        scratch_shapes=[pltpu.SemaphoreType.DMA((4 * nt,)), pltpu.SemaphoreType.DMA((4 * nt,))],
    )(*grads)


def _pair_sum(name, whole, landed, lay, out_dtype):
    R, C = lay.padded
    br = _first_divisor(R, (512, 384, 256, 128, 64, 32, 16, 8))
    nb = R // br
    if lay.axis == 0:
        mine_spec = pl.BlockSpec((br, C), lambda k, i, c_ref: ((2 * k + c_ref[0]) * nb + i, 0))
    else:
        mine_spec = pl.BlockSpec((br, C), lambda k, i, c_ref: (i, 2 * k + c_ref[0]))

    def body(c_ref, mine_ref, sib_ref, o_ref):
        o_ref[0] = (mine_ref[...] + sib_ref[0]).astype(out_dtype)

    c = lax.axis_index("c")
    return pl.pallas_call(
        body, name=name,
        grid_spec=pltpu.PrefetchScalarGridSpec(
            num_scalar_prefetch=1, grid=(4, nb),
            in_specs=[mine_spec, pl.BlockSpec((1, br, C), lambda k, i, c_ref: (k, i, 0))],
            out_specs=pl.BlockSpec((1, br, C), lambda k, i, c_ref: (k, i, 0))),
        out_shape=jax.ShapeDtypeStruct((4, R, C), out_dtype),
        compiler_params=_cparams(("parallel", "parallel")),
    )(c.reshape(1).astype(jnp.int32), whole, landed)


def _chip_exchange(name, sums):
    nt = len(sums)

    def body(*refs):
        s_refs, land_refs = refs[:nt], refs[nt:2 * nt]
        send_sems, recv_sems, local_sems = refs[2 * nt:]
        x, y, c = _mesh_pos()
        my_chip = 2 * x + y
        mine = [pltpu.make_async_copy(s_refs[t].at[my_chip], land_refs[t].at[my_chip], local_sems.at[t]) for t in range(nt)]
        for cp in mine:
            cp.start()
        chips = [(1 - x, y), (x, 1 - y), (1 - x, 1 - y)]
        copies = []
        for t in range(nt):
            for j, (px, py) in enumerate(chips):
                copies.append(pltpu.make_async_remote_copy(
                    src_ref=s_refs[t].at[2 * px + py], dst_ref=land_refs[t].at[my_chip],
                    send_sem=send_sems.at[3 * t + j], recv_sem=recv_sems.at[3 * t + j],
                    device_id=(px, py, c), device_id_type=MESH))
        for cp in copies:
            cp.start()
        for t in range(nt):
            for j, (px, py) in enumerate(chips):
                pltpu.make_async_remote_copy(
                    src_ref=s_refs[t].at[my_chip], dst_ref=land_refs[t].at[2 * px + py],
                    send_sem=send_sems.at[3 * t + j], recv_sem=recv_sems.at[3 * t + j],
                    device_id=(px, py, c), device_id_type=MESH).wait_recv()
        for cp in copies:
            cp.wait_send()
        for cp in mine:
            cp.wait()

    return pl.pallas_call(
        body, name=name, out_shape=[jax.ShapeDtypeStruct(s.shape, s.dtype) for s in sums],
        in_specs=[ANY] * nt, out_specs=[ANY] * nt,
        scratch_shapes=[pltpu.SemaphoreType.DMA((3 * nt,)), pltpu.SemaphoreType.DMA((3 * nt,)), pltpu.SemaphoreType.DMA((nt,))],
    )(*sums)


def _chip_start(name, sums):
    nt = len(sums)

    def body(*refs):
        s_refs, land_refs = refs[:nt], refs[nt:2 * nt]
        send_sems, recv_sems = refs[2 * nt], refs[2 * nt + 1]
        x, y, c = _mesh_pos()
        my_chip = 2 * x + y
        for t in range(nt):
            for j, (px, py) in enumerate([(1 - x, y), (x, 1 - y), (1 - x, 1 - y)]):
                pltpu.make_async_remote_copy(
                    src_ref=s_refs[t].at[2 * px + py], dst_ref=land_refs[t].at[my_chip],
                    send_sem=send_sems.at[3 * t + j], recv_sem=recv_sems.at[3 * t + j],
                    device_id=(px, py, c), device_id_type=MESH).start()

    thru = [pltpu.HBM(s.shape, s.dtype) for s in sums] * 2
    args = [pltpu.with_memory_space_constraint(s, pltpu.HBM) for s in sums]
    args += [pltpu.with_memory_space_constraint(lax.empty(s.shape, s.dtype), pltpu.HBM) for s in sums]
    res = pl.pallas_call(
        body, name=name, out_shape=tuple([pltpu.SemaphoreType.DMA((3 * nt,))] * 2 + thru), in_specs=[HBM] * (2 * nt),
        out_specs=tuple([SEM, SEM] + [HBM] * (2 * nt)), input_output_aliases={t: 2 + t for t in range(2 * nt)},
        compiler_params=SPLIT_COPY,
    )(*args)
    return (res[0], res[1]), list(res[2:2 + nt]), list(res[2 + nt:])


def _chip_wait(name, sems, sums, lands, after):
    nt = len(sums)

    def body(*refs):
        s_refs, land_refs = refs[:nt], refs[nt:2 * nt]
        send_sems, recv_sems = refs[2 * nt], refs[2 * nt + 1]
        x, y, c = _mesh_pos()
        my_chip = 2 * x + y
        for t in range(nt):
            for j, (px, py) in enumerate([(1 - x, y), (x, 1 - y), (1 - x, 1 - y)]):
                cp = pltpu.make_async_remote_copy(
                    src_ref=s_refs[t].at[my_chip], dst_ref=land_refs[t].at[2 * px + py],
                    send_sem=send_sems.at[3 * t + j], recv_sem=recv_sems.at[3 * t + j],
                    device_id=(px, py, c), device_id_type=MESH)
                cp.wait_send()
                cp.wait_recv()

    thru = [pltpu.HBM(s.shape, s.dtype) for s in sums] * 2
    res = pl.pallas_call(
        body, name=name, out_shape=tuple(thru), in_specs=[HBM] * (2 * nt) + [SEM, SEM, ANY],
        out_specs=tuple([HBM] * (2 * nt)), input_output_aliases={t: t for t in range(2 * nt)},
        compiler_params=SPLIT_COPY,
    )(*sums, *lands, sems[0], sems[1], after)
    return list(res[:nt]), list(res[nt:])


def _sum_chips(name, own, landed):
    _, R, C = own.shape
    br = _first_divisor(R, (512, 384, 256, 128, 64, 32, 16, 8))
    x, y, _ = _mesh_pos()
    slots = jnp.stack([2 * x + y, 2 * (1 - x) + y, 2 * x + (1 - y), 2 * (1 - x) + (1 - y)]).astype(jnp.int32)

    def body(slot_ref, mine_ref, a_ref, b_ref, c_ref, o_ref):
        o_ref[...] = ((mine_ref[0].astype(F32) + a_ref[0].astype(F32)) + b_ref[0].astype(F32)) + c_ref[0].astype(F32)

    def slot_spec(j):
        return pl.BlockSpec((1, br, C), lambda i, slot_ref: (slot_ref[j], i, 0))

    return pl.pallas_call(
        body, name=name,
        grid_spec=pltpu.PrefetchScalarGridSpec(
            num_scalar_prefetch=1, grid=(R // br,), in_specs=[slot_spec(0), slot_spec(1), slot_spec(2), slot_spec(3)],
            out_specs=pl.BlockSpec((br, C), lambda i, slot_ref: (i, 0))),
        out_shape=jax.ShapeDtypeStruct((R, C), F32), compiler_params=_cparams(("parallel",)),
    )(slots, own, landed, landed, landed)


def _sum_slots(name, slots, n):
    _, R, C = slots.shape
    br = _first_divisor(R, (512, 384, 256, 128, 64, 32, 16, 8))

    def body(s_ref, o_ref):
        acc = s_ref[0].astype(F32)
        for k in range(1, n):
            acc = acc + s_ref[k].astype(F32)
        o_ref[...] = acc

    return pl.pallas_call(
        body, name=name, grid=(R // br,), in_specs=[pl.BlockSpec((n, br, C), lambda i: (0, i, 0))],
        out_specs=pl.BlockSpec((br, C), lambda i: (i, 0)), out_shape=jax.ShapeDtypeStruct((R, C), F32),
        compiler_params=_cparams(("parallel",)),
    )(slots)


def _reduce_scatter_start(tag, names, grads):
    lays = [LAYOUTS[n] for n in names]
    landed = _pair_exchange("grads_pair_" + names[0], grads, lays)
    sums = [_pair_sum("grads_pairsum_" + n, g, ld, lay, BF16) for n, g, ld, lay in zip(names, grads, landed, lays)]
    sems, sums, lands = _chip_start(tag + "_chips_start", sums)
    return tag, names, sems, sums, lands


def _reduce_scatter_finish(pending, after):
    tag, names, sems, sums, lands = pending
    own, got = _chip_wait(tag + "_chips_wait", sems, sums, lands, after)
    return [_sum_chips("grads_sum_" + n, o, s) for n, o, s in zip(names, own, got)]


def _adamw_math(w, g, m, v):
    m = ADAM_B1 * m + (1.0 - ADAM_B1) * g
    v = ADAM_B2 * v + (1.0 - ADAM_B2) * jnp.square(g)
    m_hat = m / (1.0 - ADAM_B1 ** ADAM_STEP)
    v_hat = v / (1.0 - ADAM_B2 ** ADAM_STEP)
    delta = -ADAM_LR * (m_hat / (jnp.sqrt(v_hat) + ADAM_EPS) + ADAM_WD * w)
    return delta, m, v


def _adamw(name, w, g, m, v):
    shape = w.shape
    cols = shape[-1]
    rows = int(np.prod(shape[:-1]))
    br = _first_divisor(rows, (512, 352, 256, 128, 64, 32, 16, 8))
    args = [_In(a.reshape(rows, cols)) for a in (w, g, m, v)]
    outs = _rowwise(name, _adamw_math, args, [_Out(cols), _Out(cols), _Out(cols)], rows, br)
    return [o.reshape(shape) for o in outs]


GROUPS = {"ffn1": ("ffn1_w_gate", "ffn1_w_up", "ffn1_w_down"),
          "mix": ("w_in", "w_branch_attn", "w_branch_mlstm", "w_out"),
          "ffn2": ("ffn2_w_gate", "ffn2_w_up", "ffn2_w_down")}
GATHER_GROUPS = {"ffn1_in": ("ffn1_w_gate", "ffn1_w_up"), "ffn1_out": ("ffn1_w_down",),
                 "mix": ("w_in", "w_branch_attn", "w_branch_mlstm", "w_out"),
                 "ffn2_in": ("ffn2_w_gate", "ffn2_w_up"), "ffn2_out": ("ffn2_w_down",)}


def _small_params(small, conv_w, l):
    p = {}
    for n in ("ffn1_norm", "mix_norm", "ffn2_norm", "block_out_norm", "mlstm_out_norm", "attn_q_norm", "attn_k_norm"):
        p[n] = small[n][l][None, :]
    p["attn_sink"] = small["attn_sink"][l]
    p["gate_bias"] = jnp.pad(small["mlstm_gate_bias"][l], (0, LANES - MLSTM_N_GATES))[None, :]
    taps = _qk_perm_cols(conv_w[l], 1)
    conv_b = _qk_perm_cols(small["mlstm_conv_b"][l][None, :], 1)
    p["conv_w8"] = jnp.concatenate([taps, conv_b, jnp.zeros((4, 2 * MLSTM_WIDTH), F32)], axis=0)
    return p


def _w_in_from_slots(slots):
    w_in = slots.reshape(N_DEV, D_MODEL, IN_WIDTH // N_DEV).transpose(1, 0, 2).reshape(D_MODEL, IN_WIDTH)
    return _w_in_arrange(w_in)


def _w_in_to_slots(g):
    return _w_in_restore(g).reshape(D_MODEL, N_DEV, IN_WIDTH // N_DEV).transpose(1, 0, 2).reshape(
        N_DEV * D_MODEL, IN_WIDTH // N_DEV)


def _local_step(x, positions, target, weights_of, small, conv_w, on_grads):
    B, S, _ = x.shape
    T = B * S
    pos = positions.reshape(T, 1)
    pos_q = jnp.repeat(pos, ATT_HEADS, axis=0)
    pos_k = jnp.repeat(pos, ATT_KV_HEADS, axis=0)
    params = [_small_params(small, conv_w, l) for l in range(DEPTH)]
    xs = x.reshape(T, D_MODEL)
    tgt = target.reshape(T, D_MODEL)

    saved = []
    for l, p in enumerate(params):
        p.update(weights_of(l, "ffn1_in", xs))
        x1, s1, p["ffn1_w_down"] = _ffn_fwd("ffn1", xs, p["ffn1_norm"], p["ffn1_w_gate"], p["ffn1_w_up"],
                                            lambda after, l=l: weights_of(l, "ffn1_out", after)["ffn1_w_down"])
        p.update(weights_of(l, "mix", x1))
        p["w_in"] = _w_in_from_slots(p["w_in"])
        x2, s2 = _mix_fwd(x1, pos_q, pos_k, B, S, p)
        p.update(weights_of(l, "ffn2_in", x2))
        x3, s3, p["ffn2_w_down"] = _ffn_fwd("ffn2", x2, p["ffn2_norm"], p["ffn2_w_gate"], p["ffn2_w_up"],
                                            lambda after, l=l: weights_of(l, "ffn2_out", after)["ffn2_w_down"])
        saved.append((s1, s2, s3, x3))
        if l + 1 < DEPTH:
            xs = _block_norm_fwd(x3, p["block_out_norm"])

    sm = {n: [None] * DEPTH for n in SMALL + ("mlstm_conv_w",)}
    loss = None
    dx = None
    for l in reversed(range(DEPTH)):
        p = params[l]
        s1, s2, s3, x3 = saved[l]
        if l == DEPTH - 1:
            loss, dx, dgn = _loss_and_grad(x3, p["block_out_norm"], tgt)
        else:
            dx, dgn = _block_norm_bwd(x3, p["block_out_norm"], dx)
        sm["block_out_norm"][l] = dgn[0]
        dx, dg = _ffn_bwd("ffn2", s3, p["ffn2_norm"], p["ffn2_w_gate"], p["ffn2_w_up"], p["ffn2_w_down"], dx,
                          functools.partial(on_grads, l, "ffn2"))
        sm["ffn2_norm"][l] = dg[0]
        dx, g = _mix_bwd(s2, pos_q, pos_k, B, S, p, dx, functools.partial(on_grads, l, "mix"))
        dconv = _qk_unperm_cols(g["conv_w8"], 1)
        sm["mlstm_conv_w"][l] = dconv[0:3]
        sm["mlstm_conv_b"][l] = dconv[3]
        sm["mix_norm"][l] = g["mix_norm"][0]
        sm["mlstm_gate_bias"][l] = g["gate_bias"][0, :MLSTM_N_GATES]
        sm["attn_q_norm"][l], sm["attn_k_norm"][l] = g["attn_q_norm"][0], g["attn_k_norm"][0]
        sm["attn_sink"][l] = g["attn_sink"][0]
        sm["mlstm_out_norm"][l] = g["mlstm_out_norm"][0]
        dx, dg = _ffn_bwd("ffn1", s1, p["ffn1_norm"], p["ffn1_w_gate"], p["ffn1_w_up"], p["ffn1_w_down"], dx,
                          functools.partial(on_grads, l, "ffn1"))
        sm["ffn1_norm"][l] = dg[0]
    sm = {n: jnp.stack(v, axis=0) for n, v in sm.items()}
    return loss, dx.reshape(B, S, D_MODEL), sm


def kernel(x, positions, ffn1_norm, ffn1_w_gate, ffn1_w_up, ffn1_w_down, mix_norm, w_in, mlstm_gate_bias, attn_q_norm, attn_k_norm, attn_sink, mlstm_conv_w, mlstm_conv_b, mlstm_out_norm, w_branch_attn, w_branch_mlstm, w_out, ffn2_norm, ffn2_w_gate, ffn2_w_up, ffn2_w_down, block_out_norm, loss_target, m_ffn1_norm, m_ffn1_w_gate, m_ffn1_w_up, m_ffn1_w_down, m_mix_norm, m_w_in, m_mlstm_gate_bias, m_attn_q_norm, m_attn_k_norm, m_attn_sink, m_mlstm_conv_w, m_mlstm_conv_b, m_mlstm_out_norm, m_w_branch_attn, m_w_branch_mlstm, m_w_out, m_ffn2_norm, m_ffn2_w_gate, m_ffn2_w_up, m_ffn2_w_down, m_block_out_norm, v_ffn1_norm, v_ffn1_w_gate, v_ffn1_w_up, v_ffn1_w_down, v_mix_norm, v_w_in, v_mlstm_gate_bias, v_attn_q_norm, v_attn_k_norm, v_attn_sink, v_mlstm_conv_w, v_mlstm_conv_b, v_mlstm_out_norm, v_w_branch_attn, v_w_branch_mlstm, v_w_out, v_ffn2_norm, v_ffn2_w_gate, v_ffn2_w_up, v_ffn2_w_down, v_block_out_norm):
    args = locals()
    w = {n: args[n] for n in WEIGHTS}
    m = {n: args["m_" + n] for n in WEIGHTS}
    v = {n: args["v_" + n] for n in WEIGHTS}

    order = [(l, grp) for l in range(DEPTH) for grp in GATHER_GROUPS]
    keys = [(l, n) for l, grp in order for n in GATHER_GROUPS[grp]]
    lays = [LAYOUTS[n] for _, n in keys]
    shards = [lay.pad(w[n][l].astype(BF16)) for (l, n), lay in zip(keys, lays)]
    group_idx, at = {}, 0
    for l, grp in order:
        group_idx[(l, grp)] = list(range(at, at + len(GATHER_GROUPS[grp])))
        at += len(GATHER_GROUPS[grp])
    conv_shape = w["mlstm_conv_w"].shape
    conv_all = _all_gather("conv_all_gather", _pack_flat([w["mlstm_conv_w"]], F32, 8), vmem=True)
    conv_parts = _unpack_flat(conv_all, [conv_shape], lead=(N_DEV,))[0]
    conv_w = jnp.concatenate([conv_parts[j] for j in range(N_DEV)], axis=2)
    small = {n: w[n] for n in SMALL}

    lands = []
    for l, grp in order:
        idx = group_idx[(l, grp)]
        lands += _place_own("weights_place_" + grp, [shards[i] for i in idx], [lays[i] for i in idx])
    sems, shards, lands = _gather_start("weights_gather_start", shards, lands, lays, [group_idx[k] for k in order], conv_all)

    def weights_of(l, grp, after):
        idx = group_idx[(l, grp)]
        whole = _gather_wait(f"weights_gather_wait_{l}_{grp}", sems[order.index((l, grp))], [shards[i] for i in idx],
                             [lands[i] for i in idx], [lays[i] for i in idx], after)
        return dict(zip(GATHER_GROUPS[grp], whole))

    totals, pending = {}, []

    def finish(after):
        tag, names = pending[0][0], pending[0][1]
        for n, t in zip(names, _reduce_scatter_finish(pending.pop(0), after)):
            totals[(tag, n)] = t

    def on_grads(l, grp, g, after):
        if pending:
            finish(after)
        names = GROUPS[grp]
        pending.append(_reduce_scatter_start(f"grads_{l}_{grp}", names, [g[n] for n in names]))
        return pending[-1][3][0]

    loss, grad_x, small_g = _local_step(x, positions, loss_target, weights_of, small, conv_w, on_grads)
    finish(grad_x)
    grads = {}
    for grp, names in GROUPS.items():
        for n in names:
            grads[n] = jnp.stack([LAYOUTS[n].unpad(totals[(f"grads_{l}_{grp}", n)]) for l in range(DEPTH)], axis=0)

    small_names = SMALL + ("mlstm_conv_w",)
    small_shapes = [small_g[n].shape for n in small_names] + [(1, 1)]
    small_packed = _pack_flat([small_g[n] for n in small_names] + [loss], F32, 8)
    small_all = _all_gather("small_all_gather", small_packed, vmem=True)
    small_sum = _sum_slots("small_sum", small_all, N_DEV)
    *small_grads, loss_total = _unpack_flat(small_sum, small_shapes)
    grads.update(dict(zip(small_names, small_grads)))
    x_pos, y_pos, c_pos = _mesh_pos()
    grads["mlstm_conv_w"] = lax.dynamic_slice_in_dim(
        grads["mlstm_conv_w"], (4 * x_pos + 2 * y_pos + c_pos) * conv_shape[2], conv_shape[2], axis=2)

    deltas, new_m, new_v = {}, {}, {}
    for n in BIG:
        deltas[n], new_m[n], new_v[n] = _adamw("adamw_" + n, w[n], grads[n], m[n], v[n])
    sw, sg, smm, sv = (_pack_flat([d[n] for n in SMALL], F32, 8) for d in (w, grads, m, v))
    sd, snm, snv = _adamw("adamw_small", sw, sg, smm, sv)
    shapes = [w[n].shape for n in SMALL]
    for d, buf in ((deltas, sd), (new_m, snm), (new_v, snv)):
        d.update(dict(zip(SMALL, _unpack_flat(buf, shapes))))

    return (loss_total.reshape(()), grad_x, *[grads[n] for n in WEIGHTS], *[deltas[n] for n in WEIGHTS],
            *[new_m[n] for n in WEIGHTS], *[new_v[n] for n in WEIGHTS])
```

```python
import functools

import numpy as np
import jax
import jax.numpy as jnp
from jax import lax
from jax.experimental import pallas as pl
from jax.experimental.pallas import tpu as pltpu

F32 = jnp.float32
BF16 = jnp.bfloat16

D_MODEL = 1024
D_FF = 2816
ATT_HEAD_DIM = 64
ATT_HEADS = 8
ATT_KV_HEADS = 2
ATT_GROUP = ATT_HEADS // ATT_KV_HEADS
ATT_WIDTH = ATT_HEADS * ATT_HEAD_DIM
ATT_KV_WIDTH = ATT_KV_HEADS * ATT_HEAD_DIM
WINDOW = 128
ATT_BLOCK = 128
ROPE_DIM = 16
ROPE_THETA = 500000.0
MLSTM_HEADS = 4
MLSTM_HEAD_DIM = 128
MLSTM_WIDTH = MLSTM_HEADS * MLSTM_HEAD_DIM
MLSTM_CHUNK = 128
MLSTM_N_GATES = 4 * MLSTM_HEADS
NORM_EPS = 1e-6
IN_WIDTH = 4880
DEPTH = 2
N_DEV = 8

ADAM_LR = 0.001
ADAM_B1 = 0.9
ADAM_B2 = 0.999
ADAM_EPS = 1e-08
ADAM_WD = 0.01
ADAM_STEP = 10

LANES = 128
C_GMERGE = 0
C_QK = 2048
C_VM = 3072
C_OM = 3584
C_QA = 4096
C_KA = 4608
C_VA = 4736
C_GATES = 4864
IN_PAD = 4992

VMEM_LIMIT = 48 * 1024 * 1024

MESH = pl.DeviceIdType.MESH


def _cparams(sem):
    return pltpu.CompilerParams(dimension_semantics=sem, vmem_limit_bytes=VMEM_LIMIT)


def _first_divisor(n, cands):
    for c in cands:
        if n % c == 0:
            return c
    return n


_NN = ((1,), (0,))
_NT = ((1,), (1,))
_TN = ((0,), (0,))


def _mm(a, b, dims):
    return lax.dot_general(a.astype(BF16), b.astype(BF16), (dims, ((), ())), preferred_element_type=F32)


@jax.custom_vjp
def mm_nn(a, b):
    return _mm(a, b, _NN)


def _mm_nn_fwd(a, b):
    return _mm(a, b, _NN), (a, b)


def _mm_nn_bwd(res, g):
    a, b = res
    return _mm(g, b, _NT).astype(a.dtype), _mm(a, g, _TN).astype(b.dtype)


mm_nn.defvjp(_mm_nn_fwd, _mm_nn_bwd)


@jax.custom_vjp
def mm_nt(a, b):
    return _mm(a, b, _NT)


def _mm_nt_fwd(a, b):
    return _mm(a, b, _NT), (a, b)


def _mm_nt_bwd(res, g):
    a, b = res
    return _mm(g, b, _NN).astype(a.dtype), _mm(g, a, _TN).astype(b.dtype)


mm_nt.defvjp(_mm_nt_fwd, _mm_nt_bwd)


@jax.custom_vjp
def mm_tn(a, b):
    return _mm(a, b, _TN)


def _mm_tn_fwd(a, b):
    return _mm(a, b, _TN), (a, b)


def _mm_tn_bwd(res, g):
    a, b = res
    return _mm(b, g, _NT).astype(a.dtype), _mm(a, g, _NN).astype(b.dtype)


mm_tn.defvjp(_mm_tn_fwd, _mm_tn_bwd)


def _matmul(name, a, b, mode, out_dtype=F32, res=None, scale=1.0, bl=None, dep=None):
    b_shape = b.shape if bl is None else b.shape[1:]
    if mode == "nn":
        (M, K), (K2, N) = a.shape, b_shape
    elif mode == "nt":
        (M, K), (N, K2) = a.shape, b_shape
    else:
        (K, M), (K2, N) = a.shape, b_shape
    assert K == K2, (name, a.shape, b.shape)
    tm = _first_divisor(M, (1024, 1408, 512, 384, 256, 128))
    tn = _first_divisor(N, (1024, 512, 384, 256, 128))
    tk = _first_divisor(K, (1024, 1408, 1664, 512, 256, 128))
    nk = K // tk
    if mode == "tn":
        a_spec = pl.BlockSpec((tk, tm), lambda i, j, k: (k, i))
    else:
        a_spec = pl.BlockSpec((tm, tk), lambda i, j, k: (i, k))
    if mode == "nt":
        b_blk, b_idx = (tn, tk), (lambda i, j, k: (j, k))
    else:
        b_blk, b_idx = (tk, tn), (lambda i, j, k: (k, j))
    if bl is None:
        b_spec = pl.BlockSpec(b_blk, b_idx)
    else:
        b_spec = pl.BlockSpec((None,) + b_blk, lambda i, j, k: (bl,) + b_idx(i, j, k))
    o_spec = pl.BlockSpec((tm, tn), lambda i, j, k: (i, j))
    dims = {"nn": _NN, "nt": _NT, "tn": _TN}[mode]
    has_res = res is not None

    def body(*refs):
        a_ref, b_ref = refs[:2]
        r_ref = refs[2] if has_res else None
        o_ref, acc = refs[-2:]
        k = pl.program_id(2)

        @pl.when(k == 0)
        def _():
            acc[...] = jnp.zeros_like(acc)

        acc[...] += _mm(a_ref[...], b_ref[...], dims)

        @pl.when(k == nk - 1)
        def _():
            out = acc[...]
            if scale != 1.0:
                out = out * scale
            if has_res:
                out = r_ref[...].astype(F32) + out
            o_ref[...] = out.astype(out_dtype)

    in_specs = [a_spec, b_spec] + ([o_spec] if has_res else [])
    args = (a, b) + ((res,) if has_res else ())
    if dep is not None:
        in_specs.append(pl.BlockSpec(memory_space=pl.ANY))
        args += (dep,)
    return pl.pallas_call(
        body, name=name, grid=(M // tm, N // tn, nk), in_specs=in_specs, out_specs=o_spec,
        out_shape=jax.ShapeDtypeStruct((M, N), out_dtype), scratch_shapes=[pltpu.VMEM((tm, tn), F32)],
        compiler_params=_cparams(("parallel", "parallel", "arbitrary")),
    )(*args)


class _In:
    def __init__(self, arr, width=None, base=0, split=False, rows=True):
        self.arr, self.base, self.split, self.rows = arr, base, split, rows
        self.width = arr.shape[1] if width is None else width


class _Out:
    def __init__(self, cols, dtype=F32, width=None, split=False, rows=True, nrows=1):
        self.cols, self.dtype, self.split, self.rows, self.nrows = cols, dtype, split, rows, nrows
        self.width = cols if width is None else width


def _rowwise(name, fn, ins, outs, n_rows, br, ncol=1):
    br = min(br, n_rows)
    assert n_rows % br == 0, (name, n_rows, br)
    nrow_blocks = n_rows // br

    def in_spec(d):
        nb = br if d.rows else d.arr.shape[0]
        if d.rows and d.split:
            im = lambda j, i, base=d.base: (i, base + j)
        elif d.rows:
            im = lambda j, i, base=d.base: (i, base)
        elif d.split:
            im = lambda j, i, base=d.base: (0, base + j)
        else:
            im = lambda j, i, base=d.base: (0, base)
        return pl.BlockSpec((nb, d.width), im)

    def out_spec(d):
        nb = br if d.rows else d.nrows
        if d.rows and d.split:
            im = lambda j, i: (i, j)
        elif d.rows:
            im = lambda j, i: (i, 0)
        elif d.split:
            im = lambda j, i: (0, j)
        else:
            im = lambda j, i: (0, 0)
        return pl.BlockSpec((nb, d.width), im)

    n_in = len(ins)

    def body(*refs):
        i = pl.program_id(1)
        vals = [r[...] for r in refs[:n_in]]
        res = fn(*vals)
        if not isinstance(res, (tuple, list)):
            res = (res,)
        for d, ref, val in zip(outs, refs[n_in:], res):
            if d.rows:
                ref[...] = val.astype(d.dtype)
            else:
                @pl.when(i == 0)
                def _(ref=ref):
                    ref[...] = jnp.zeros_like(ref)

                ref[...] += val.astype(d.dtype)

    out_shape = [jax.ShapeDtypeStruct((n_rows if d.rows else d.nrows, d.cols), d.dtype) for d in outs]
    res = pl.pallas_call(
        body, name=name, grid=(ncol, nrow_blocks), in_specs=[in_spec(d) for d in ins],
        out_specs=[out_spec(d) for d in outs], out_shape=out_shape,
        compiler_params=_cparams(("parallel", "arbitrary")),
    )(*[d.arr for d in ins])
    return res


def _rms(x, g):
    return x * lax.rsqrt(jnp.mean(x * x, axis=-1, keepdims=True) + NORM_EPS) * g


def _sigmoid(x):
    return 1.0 / (1.0 + jnp.exp(-x))


def _silu(x):
    return x * _sigmoid(x)


def _log_sigmoid(x):
    return jnp.minimum(x, 0.0) - jnp.log(1.0 + jnp.exp(-jnp.abs(x)))


def _rope_tables(pos, inv_freq_row):
    ang = pos.astype(F32) * inv_freq_row
    return jnp.cos(ang), jnp.sin(ang)


def _head_sums_impl(v):
    w = v.shape[-1]
    shift = ATT_HEAD_DIM.bit_length() - 1
    r = lax.shift_right_logical(lax.broadcasted_iota(jnp.int32, (w, w), 0), shift)
    c = lax.shift_right_logical(lax.broadcasted_iota(jnp.int32, (w, w), 1), shift)
    ones = (r == c).astype(BF16)
    hi = v.astype(BF16)
    lo = (v - hi.astype(F32)).astype(BF16)
    dn = (_NN, ((), ()))
    return (lax.dot_general(hi, ones, dn, preferred_element_type=F32)
            + lax.dot_general(lo, ones, dn, preferred_element_type=F32))


@jax.custom_vjp
def _head_sums(v):
    return _head_sums_impl(v)


_head_sums.defvjp(lambda v: (_head_sums_impl(v), None), lambda _, g: (_head_sums_impl(g),))


def _rotate_half_impl(y):
    w = y.shape[-1]
    half = ROPE_DIM // 2
    lane = lax.broadcasted_iota(jnp.int32, y.shape, 1) & (ATT_HEAD_DIM - 1)
    above = pltpu.roll(y, w - half, axis=1)
    below = pltpu.roll(y, half, axis=1)
    return jnp.where(lane < half, -above, jnp.where(lane < ROPE_DIM, below, 0.0))


@jax.custom_vjp
def _rotate_half(y):
    return _rotate_half_impl(y)


_rotate_half.defvjp(lambda y: (_rotate_half_impl(y), None), lambda _, g: (-_rotate_half_impl(g),))


def _qk_prep(t, g, cos, sin):
    reps = t.shape[-1] // cos.shape[-1]
    if reps > 1:
        cos, sin = jnp.tile(cos, (1, reps)), jnp.tile(sin, (1, reps))
    y = t * lax.rsqrt(_head_sums(t * t) * (1.0 / ATT_HEAD_DIM) + NORM_EPS) * g
    return y * cos + _rotate_half(y) * sin


def _attn_head(q, kb, vb, sink, valid):
    s = mm_nt(q, kb) * (ATT_HEAD_DIM ** -0.5)
    s = jnp.where(valid, s, -jnp.inf)
    m = jnp.maximum(jnp.max(s, axis=-1, keepdims=True), sink)
    p = jnp.exp(s - m)
    den = jnp.sum(p, axis=-1, keepdims=True) + jnp.exp(sink - m)
    return mm_nn(p / den, vb)


def _mlstm_chunk(q, k, v, li, lf_pre, C, n, m, incl, incl_t, eye):
    k = k * (MLSTM_HEAD_DIM ** -0.5)
    lf = _log_sigmoid(lf_pre)
    lf_row = jnp.sum(eye * lf, axis=0, keepdims=True)
    li_row = jnp.sum(eye * li, axis=0, keepdims=True)
    b = jnp.sum(incl * lf_row, axis=1, keepdims=True)
    b_row = jnp.sum(incl_t * lf, axis=0, keepdims=True)
    b_tot = jnp.sum(lf, axis=0, keepdims=True)
    a = b_tot - b + li
    a_max = jnp.max(a, axis=0, keepdims=True)
    kw = k * jnp.exp(a - a_max)
    c_loc = mm_tn(kw, v)
    n_loc = jnp.sum(kw, axis=0, keepdims=True)

    dmat = jnp.where(incl > 0.5, b - b_row + li_row, -jnp.inf)
    inter = b + m
    m_t = jnp.maximum(inter, jnp.max(dmat, axis=1, keepdims=True))
    sc = mm_nt(q, k) * jnp.exp(dmat - m_t)
    scale_in = jnp.exp(inter - m_t)
    num = mm_nn(sc, v) + scale_in * mm_nn(q, C)
    den = jnp.sum(sc, axis=1, keepdims=True) + scale_in * jnp.sum(q * n, axis=1, keepdims=True)
    h = num / jnp.maximum(jnp.abs(den), jnp.exp(-m_t))

    m_new = jnp.maximum(b_tot + m, a_max)
    s_p = jnp.exp(b_tot + m - m_new)
    s_l = jnp.exp(a_max - m_new)
    return h, s_p * C + s_l * c_loc, s_p * n + s_l * n_loc, m_new


def _mlstm_combine(hf, hb, o_pre, g):
    h = hf + hb
    mu = jnp.mean(h, axis=-1, keepdims=True)
    var = jnp.mean(jnp.square(h - mu), axis=-1, keepdims=True)
    return _sigmoid(o_pre) * ((h - mu) * lax.rsqrt(var + NORM_EPS) * g)


def _merge(ga, gm, za, zm):
    return _sigmoid(ga) * za + _sigmoid(gm) * zm


def _attn_mask(n, seq):
    qi = n * ATT_BLOCK + lax.broadcasted_iota(jnp.int32, (ATT_BLOCK, 3 * ATT_BLOCK), 0)
    kj = (n - 1) * ATT_BLOCK + lax.broadcasted_iota(jnp.int32, (ATT_BLOCK, 3 * ATT_BLOCK), 1)
    return (jnp.abs(qi - kj) <= WINDOW) & (kj >= 0) & (kj < seq)


def _attn_specs(nq):
    q_spec = pl.BlockSpec((1, ATT_HEADS, ATT_BLOCK, ATT_HEAD_DIM), lambda b, n: (b, 0, n, 0))

    def kv_spec(off):
        return pl.BlockSpec((1, ATT_KV_HEADS, ATT_BLOCK, ATT_HEAD_DIM),
                            lambda b, n: (b, 0, jnp.clip(n + off, 0, nq - 1), 0))

    sink_spec = pl.BlockSpec((ATT_KV_HEADS, ATT_GROUP, 1, 1), lambda b, n: (0, 0, 0, 0))
    return q_spec, kv_spec, sink_spec


def _attn_fwd(q, k, v, sink):
    B, _, S, _ = q.shape
    nq = S // ATT_BLOCK
    q_spec, kv_spec, sink_spec = _attn_specs(nq)

    def body(q_ref, kp, kc, kn, vp, vc, vn, s_ref, o_ref):
        valid = _attn_mask(pl.program_id(1), S)
        for h in range(ATT_KV_HEADS):
            kb = jnp.concatenate([kp[0, h], kc[0, h], kn[0, h]], axis=0)
            vb = jnp.concatenate([vp[0, h], vc[0, h], vn[0, h]], axis=0)
            for g in range(ATT_GROUP):
                o_ref[0, h * ATT_GROUP + g] = _attn_head(q_ref[0, h * ATT_GROUP + g], kb, vb, s_ref[h, g], valid).astype(BF16)

    return pl.pallas_call(
        body, name="attn_fwd", grid=(B, nq),
        in_specs=[q_spec, kv_spec(-1), kv_spec(0), kv_spec(1), kv_spec(-1), kv_spec(0), kv_spec(1), sink_spec],
        out_specs=q_spec, out_shape=jax.ShapeDtypeStruct(q.shape, BF16),
        compiler_params=_cparams(("parallel", "arbitrary")),
    )(q, k, k, k, v, v, v, sink)


def _attn_bwd(q, k, v, sink, dy):
    B, _, S, _ = q.shape
    nq = S // ATT_BLOCK
    q_spec, kv_spec, sink_spec = _attn_specs(nq)
    kv_full = pl.BlockSpec((1, ATT_KV_HEADS, S, ATT_HEAD_DIM), lambda b, n: (b, 0, 0, 0))

    def body(q_ref, kp, kc, kn, vp, vc, vn, s_ref, dy_ref, dq_ref, dk_ref, dv_ref, ds_ref):
        b, n = pl.program_id(0), pl.program_id(1)
        valid = _attn_mask(n, S)

        @pl.when(n == 0)
        def _():
            dk_ref[...] = jnp.zeros_like(dk_ref)
            dv_ref[...] = jnp.zeros_like(dv_ref)

        @pl.when((n == 0) & (b == 0))
        def _():
            ds_ref[...] = jnp.zeros_like(ds_ref)

        for h in range(ATT_KV_HEADS):
            kb = jnp.concatenate([kp[0, h], kc[0, h], kn[0, h]], axis=0)
            vb = jnp.concatenate([vp[0, h], vc[0, h], vn[0, h]], axis=0)
            dkb = jnp.zeros_like(kb)
            dvb = jnp.zeros_like(vb)
            for g in range(ATT_GROUP):
                hq = h * ATT_GROUP + g
                _, vjp = jax.vjp(functools.partial(_attn_head, valid=valid), q_ref[0, hq], kb, vb, s_ref[h, g])
                dq, dk_g, dv_g, dsink = vjp(dy_ref[0, hq])
                dq_ref[0, hq] = dq
                ds_ref[h, g] += dsink
                dkb += dk_g
                dvb += dv_g
            for j, off in enumerate((-1, 0, 1)):
                start = pl.multiple_of(jnp.clip(n + off, 0, nq - 1) * ATT_BLOCK, ATT_BLOCK)
                rows = pl.ds(start, ATT_BLOCK)
                dk_ref[0, h, rows, :] += dkb[j * ATT_BLOCK:(j + 1) * ATT_BLOCK]
                dv_ref[0, h, rows, :] += dvb[j * ATT_BLOCK:(j + 1) * ATT_BLOCK]

    return pl.pallas_call(
        body, name="attn_bwd", grid=(B, nq),
        in_specs=[q_spec, kv_spec(-1), kv_spec(0), kv_spec(1), kv_spec(-1), kv_spec(0), kv_spec(1), sink_spec, q_spec],
        out_specs=[q_spec, kv_full, kv_full, sink_spec],
        out_shape=[jax.ShapeDtypeStruct(q.shape, F32), jax.ShapeDtypeStruct(k.shape, F32),
                   jax.ShapeDtypeStruct(v.shape, F32), jax.ShapeDtypeStruct(sink.shape, F32)],
        compiler_params=_cparams(("arbitrary", "arbitrary")),
    )(q, k, k, k, v, v, v, sink, dy)


CONV_COLS = 256


def _conv_taps(u, seq):
    row = lax.broadcasted_iota(jnp.int32, u.shape, 0)
    prev = jnp.where(row == 0, 0.0, pltpu.roll(u, 1, axis=0))
    nxt = jnp.where(row == seq - 1, 0.0, pltpu.roll(u, seq - 1, axis=0))
    return prev, nxt


def _conv_fwd(proj3, w8):
    B, S, _ = proj3.shape
    ncb = 2 * MLSTM_WIDTH // CONV_COLS

    def body(u_ref, w_ref, o_ref):
        u = u_ref[0]
        prev, nxt = _conv_taps(u, S)
        o_ref[0] = _silu(prev * w_ref[0:1, :] + u * w_ref[1:2, :] + nxt * w_ref[2:3, :] + w_ref[3:4, :])

    return pl.pallas_call(
        body, name="conv_fwd", grid=(B, ncb),
        in_specs=[pl.BlockSpec((1, S, CONV_COLS), lambda b, c: (b, 0, C_QK // CONV_COLS + c)),
                  pl.BlockSpec((8, CONV_COLS), lambda b, c: (0, c))],
        out_specs=pl.BlockSpec((1, S, CONV_COLS), lambda b, c: (b, 0, c)),
        out_shape=jax.ShapeDtypeStruct((B, S, 2 * MLSTM_WIDTH), F32),
        compiler_params=_cparams(("parallel", "parallel")),
    )(proj3, w8)


def _conv_bwd(proj3, w8, dout_f, dout_b):
    B, S, _ = proj3.shape
    ncb = 2 * MLSTM_WIDTH // CONV_COLS

    def body(u_ref, w_ref, df_ref, db_ref, du_ref, dw_ref):
        b = pl.program_id(1)
        u = u_ref[0]
        prev, nxt = _conv_taps(u, S)
        w0, w1, w2 = w_ref[0:1, :], w_ref[1:2, :], w_ref[2:3, :]
        pre = prev * w0 + u * w1 + nxt * w2 + w_ref[3:4, :]
        sig = _sigmoid(pre)
        dpre = (df_ref[0] + db_ref[0]) * (sig * (1.0 + pre * (1.0 - sig)))
        dprev, dnxt = _conv_taps(dpre, S)
        du_ref[0] = (dnxt * w0 + dpre * w1 + dprev * w2).astype(BF16)

        @pl.when(b == 0)
        def _():
            dw_ref[...] = jnp.zeros_like(dw_ref)

        dw_ref[0:1, :] += jnp.sum(dpre * prev, axis=0, keepdims=True)
        dw_ref[1:2, :] += jnp.sum(dpre * u, axis=0, keepdims=True)
        dw_ref[2:3, :] += jnp.sum(dpre * nxt, axis=0, keepdims=True)
        dw_ref[3:4, :] += jnp.sum(dpre, axis=0, keepdims=True)

    blk = pl.BlockSpec((1, S, CONV_COLS), lambda c, b: (b, 0, c))
    return pl.pallas_call(
        body, name="conv_bwd", grid=(ncb, B),
        in_specs=[pl.BlockSpec((1, S, CONV_COLS), lambda c, b: (b, 0, C_QK // CONV_COLS + c)),
                  pl.BlockSpec((8, CONV_COLS), lambda c, b: (0, c)), blk, blk],
        out_specs=[blk, pl.BlockSpec((8, CONV_COLS), lambda c, b: (0, c))],
        out_shape=[jax.ShapeDtypeStruct((B, S, 2 * MLSTM_WIDTH), BF16), jax.ShapeDtypeStruct((8, 2 * MLSTM_WIDTH), F32)],
        compiler_params=_cparams(("parallel", "arbitrary")),
    )(proj3, w8, dout_f, dout_b)


MLSTM_HEADS_PER_STEP = 4


def _chunk_masks(direction):
    t = lax.broadcasted_iota(jnp.int32, (MLSTM_CHUNK, MLSTM_CHUNK), 0)
    s = lax.broadcasted_iota(jnp.int32, (MLSTM_CHUNK, MLSTM_CHUNK), 1)
    le, ge = (s <= t).astype(F32), (s >= t).astype(F32)
    eye = (s == t).astype(F32)
    return (le, ge, eye) if direction == 0 else (ge, le, eye)


def _gate_cols(gates, direction, head):
    lane = lax.broadcasted_iota(jnp.int32, gates.shape, 1)
    sel_i = (lane == (2 * direction) * MLSTM_HEADS + head).astype(F32)
    sel_f = (lane == (2 * direction + 1) * MLSTM_HEADS + head).astype(F32)
    return sel_i, sel_f


def _mlstm_fwd(qk, proj3, bias):
    B, S, _ = qk.shape
    nc = S // MLSTM_CHUNK
    H, L, DH = MLSTM_HEADS, MLSTM_CHUNK, MLSTM_HEAD_DIM

    def chunk_of(d, c):
        return c if d == 0 else nc - 1 - c

    HS = MLSTM_HEADS_PER_STEP

    def body(qkf, qkb, vf, vb, gf, gb, bias_ref, hf, hb, csf, csb, nsf, nsb, msf, msb, c_st, n_st, m_st):
        c, hg = pl.program_id(1), pl.program_id(2)

        @pl.when(c == 0)
        def _():
            for d in range(2):
                for j in range(HS):
                    c_st[d, hg * HS + j] = jnp.zeros((DH, DH), F32)
                    n_st[d, hg * HS + j] = jnp.zeros((1, DH), F32)
                    m_st[d, hg * HS + j] = jnp.zeros((1, DH), F32)

        for d, (qk_ref, v_ref, g_ref, h_ref, cs, ns, ms) in enumerate(
                ((qkf, vf, gf, hf, csf, nsf, msf), (qkb, vb, gb, hb, csb, nsb, msb))):
            incl, incl_t, eye = _chunk_masks(d)
            gates = g_ref[0] + bias_ref[...]
            for j in range(HS):
                h = hg * HS + j
                sel_i, sel_f = _gate_cols(gates, d, h)
                li = jnp.sum(gates * sel_i, axis=1, keepdims=True)
                lf_pre = jnp.sum(gates * sel_f, axis=1, keepdims=True)
                c_in, n_in, m_in = c_st[d, h], n_st[d, h], m_st[d, h]
                cs[0, 0, j], ns[0, 0, j], ms[0, 0, j] = c_in, n_in, m_in
                hh, c_new, n_new, m_new = _mlstm_chunk(
                    qk_ref[0, :, 2 * j * DH:(2 * j + 1) * DH], qk_ref[0, :, (2 * j + 1) * DH:(2 * j + 2) * DH],
                    v_ref[0, :, j * DH:(j + 1) * DH], li, lf_pre, c_in, n_in,
                    jnp.max(m_in, axis=1, keepdims=True), incl, incl_t, eye)
                h_ref[0, :, j * DH:(j + 1) * DH] = hh
                c_st[d, h], n_st[d, h] = c_new, n_new
                m_st[d, h] = jnp.broadcast_to(m_new, (1, DH))

    def tok_spec(width, base, d, per_head):
        return pl.BlockSpec((1, L, width), lambda b, c, h: (b, chunk_of(d, c), base + (h if per_head else 0)))

    def st_spec(shape, d):
        return pl.BlockSpec((1, 1, HS) + shape, lambda b, c, h: (b, chunk_of(d, c), h, 0, 0))

    in_specs = [tok_spec(2 * HS * DH, 0, 0, True), tok_spec(2 * HS * DH, 0, 1, True),
                tok_spec(HS * DH, C_VM // (HS * DH), 0, True), tok_spec(HS * DH, C_VM // (HS * DH), 1, True),
                tok_spec(LANES, C_GATES // LANES, 0, False), tok_spec(LANES, C_GATES // LANES, 1, False),
                pl.BlockSpec((1, LANES), lambda b, c, h: (0, 0))]
    out_specs = [tok_spec(HS * DH, 0, 0, True), tok_spec(HS * DH, 0, 1, True),
                 st_spec((DH, DH), 0), st_spec((DH, DH), 1), st_spec((1, DH), 0), st_spec((1, DH), 1),
                 st_spec((1, DH), 0), st_spec((1, DH), 1)]
    hs = jax.ShapeDtypeStruct((B, S, H * DH), F32)
    cs = jax.ShapeDtypeStruct((B, nc, H, DH, DH), F32)
    vs = jax.ShapeDtypeStruct((B, nc, H, 1, DH), F32)
    return pl.pallas_call(
        body, name="mlstm_fwd", grid=(B, nc, H // HS), in_specs=in_specs, out_specs=out_specs,
        out_shape=[hs, hs, cs, cs, vs, vs, vs, vs],
        scratch_shapes=[pltpu.VMEM((2, H, DH, DH), F32), pltpu.VMEM((2, H, 1, DH), F32), pltpu.VMEM((2, H, 1, DH), F32)],
        compiler_params=_cparams(("parallel", "arbitrary", "arbitrary")),
    )(qk, qk, proj3, proj3, proj3, proj3, bias)


def _mlstm_bwd(qk, proj3, bias, states, dh):
    B, S, _ = qk.shape
    nc = S // MLSTM_CHUNK
    H, L, DH = MLSTM_HEADS, MLSTM_CHUNK, MLSTM_HEAD_DIM

    def chunk_of(d, c):
        return nc - 1 - c if d == 0 else c

    HS = MLSTM_HEADS_PER_STEP

    def body(qkf, qkb, vf, vb, gf, gb, bias_ref, csf, csb, nsf, nsb, msf, msb, dhf, dhb,
             dqkf, dqkb, dvf, dvb, dgf, dgb, dc_st, dn_st, dm_st):
        c, hg = pl.program_id(1), pl.program_id(2)

        @pl.when(c == 0)
        def _():
            for d in range(2):
                for j in range(HS):
                    dc_st[d, hg * HS + j] = jnp.zeros((DH, DH), F32)
                    dn_st[d, hg * HS + j] = jnp.zeros((1, DH), F32)
                    dm_st[d, hg * HS + j] = jnp.zeros((1, DH), F32)

        @pl.when(hg == 0)
        def _():
            dgf[...] = jnp.zeros_like(dgf)
            dgb[...] = jnp.zeros_like(dgb)

        for d, (qk_ref, v_ref, g_ref, cs, ns, ms, dh_ref, dqk_ref, dv_ref, dg_ref) in enumerate(
                ((qkf, vf, gf, csf, nsf, msf, dhf, dqkf, dvf, dgf), (qkb, vb, gb, csb, nsb, msb, dhb, dqkb, dvb, dgb))):
            incl, incl_t, eye = _chunk_masks(d)
            gates = g_ref[0] + bias_ref[...]
            dgates = jnp.zeros_like(gates)
            for j in range(HS):
                h = hg * HS + j
                sel_i, sel_f = _gate_cols(gates, d, h)
                li = jnp.sum(gates * sel_i, axis=1, keepdims=True)
                lf_pre = jnp.sum(gates * sel_f, axis=1, keepdims=True)
                m_in = jnp.max(ms[0, 0, j], axis=1, keepdims=True)
                _, vjp = jax.vjp(
                    functools.partial(_mlstm_chunk, incl=incl, incl_t=incl_t, eye=eye),
                    qk_ref[0, :, 2 * j * DH:(2 * j + 1) * DH], qk_ref[0, :, (2 * j + 1) * DH:(2 * j + 2) * DH],
                    v_ref[0, :, j * DH:(j + 1) * DH], li, lf_pre, cs[0, 0, j], ns[0, 0, j], m_in)
                dm_out = jnp.max(dm_st[d, h], axis=1, keepdims=True)
                dq, dk, dv, dli, dlf, dc, dn, dm = vjp((dh_ref[0, :, j * DH:(j + 1) * DH], dc_st[d, h], dn_st[d, h], dm_out))
                dqk_ref[0, :, 2 * j * DH:(2 * j + 1) * DH] = dq
                dqk_ref[0, :, (2 * j + 1) * DH:(2 * j + 2) * DH] = dk
                dv_ref[0, :, j * DH:(j + 1) * DH] = dv
                dgates += dli * sel_i + dlf * sel_f
                dc_st[d, h], dn_st[d, h] = dc, dn
                dm_st[d, h] = jnp.broadcast_to(dm, (1, DH))
            dg_ref[0] += dgates

    def tok_spec(width, base, d, per_head):
        return pl.BlockSpec((1, L, width), lambda b, c, h: (b, chunk_of(d, c), base + (h if per_head else 0)))

    def st_spec(shape, d):
        return pl.BlockSpec((1, 1, HS) + shape, lambda b, c, h: (b, chunk_of(d, c), h, 0, 0))

    in_specs = [tok_spec(2 * HS * DH, 0, 0, True), tok_spec(2 * HS * DH, 0, 1, True),
                tok_spec(HS * DH, C_VM // (HS * DH), 0, True), tok_spec(HS * DH, C_VM // (HS * DH), 1, True),
                tok_spec(LANES, C_GATES // LANES, 0, False), tok_spec(LANES, C_GATES // LANES, 1, False),
                pl.BlockSpec((1, LANES), lambda b, c, h: (0, 0)),
                st_spec((DH, DH), 0), st_spec((DH, DH), 1), st_spec((1, DH), 0), st_spec((1, DH), 1),
                st_spec((1, DH), 0), st_spec((1, DH), 1), tok_spec(HS * DH, 0, 0, True), tok_spec(HS * DH, 0, 1, True)]
    out_specs = [tok_spec(2 * HS * DH, 0, 0, True), tok_spec(2 * HS * DH, 0, 1, True),
                 tok_spec(HS * DH, 0, 0, True), tok_spec(HS * DH, 0, 1, True),
                 tok_spec(LANES, 0, 0, False), tok_spec(LANES, 0, 1, False)]
    qks = jax.ShapeDtypeStruct((B, S, 2 * H * DH), F32)
    vs = jax.ShapeDtypeStruct((B, S, H * DH), F32)
    gs = jax.ShapeDtypeStruct((B, S, LANES), F32)
    csf, csb, nsf, nsb, msf, msb = states
    return pl.pallas_call(
        body, name="mlstm_bwd", grid=(B, nc, H // HS), in_specs=in_specs, out_specs=out_specs,
        out_shape=[qks, qks, vs, vs, gs, gs],
        scratch_shapes=[pltpu.VMEM((2, H, DH, DH), F32), pltpu.VMEM((2, H, 1, DH), F32), pltpu.VMEM((2, H, 1, DH), F32)],
        compiler_params=_cparams(("parallel", "arbitrary", "arbitrary")),
    )(qk, qk, proj3, proj3, proj3, proj3, bias, csf, csb, nsf, nsb, msf, msb, dh, dh)


ROW_BLOCK = 256
FF_COLS = 512
FF_SHARD = D_FF // N_DEV
FF_SHARD_PAD = 384
FF_PAD = N_DEV * FF_SHARD_PAD


def _rms_fwd(name, x, g):
    T = x.shape[0]
    return _rowwise(name, lambda xv, gv: _rms(xv, gv), [_In(x), _In(g, rows=False)], [_Out(D_MODEL, BF16)], T, ROW_BLOCK)[0]


def _rms_bwd(name, x, g, dh, dres):
    T = x.shape[0]

    def fn(xv, gv, dhv, drv):
        _, vjp = jax.vjp(_rms, xv, gv)
        dx, dg = vjp(dhv)
        return drv + dx, dg

    return _rowwise(name, fn, [_In(x), _In(g, rows=False), _In(dh), _In(dres)],
                    [_Out(D_MODEL), _Out(D_MODEL, rows=False)], T, ROW_BLOCK)


def _mmw(name, a, w, mode, **kw):
    if isinstance(w, tuple):
        return _matmul(name, a, w[0], mode, bl=w[1], **kw)
    return _matmul(name, a, w, mode, **kw)


def _ffn_fwd(tag, x, g, wg, wu, wd):
    T = x.shape[0]
    h = _rms_fwd(tag + "_norm", x, g)
    gate = _mmw(tag + "_gate", h, wg, "nn")
    up = _mmw(tag + "_up", h, wu, "nn")
    act = _rowwise(tag + "_act", lambda a, b: _silu(a) * b,
                   [_In(gate, FF_COLS, split=True), _In(up, FF_COLS, split=True)],
                   [_Out(FF_PAD, BF16, FF_COLS, split=True)], T, 1024, ncol=FF_PAD // FF_COLS)[0]
    if callable(wd):
        wd = wd(act)
    out = _mmw(tag + "_down", act, wd, "nn", res=x, scale=0.5)
    return out, (x, h, gate, up, act), wd


def _ffn_bwd(tag, saved, g, wg, wu, wd, dx, on_dw):
    x, h, gate, up, act = saved
    T = x.shape[0]
    dact = _mmw(tag + "_dact", dx, wd, "nt", scale=0.5)
    dwd = _matmul(tag + "_dwd", act, dx, "tn", scale=0.5)

    def fn(a, b, da):
        _, vjp = jax.vjp(lambda p, q: _silu(p) * q, a, b)
        return vjp(da)

    dgate, dup = _rowwise(tag + "_dactfn", fn,
                          [_In(gate, FF_COLS, split=True), _In(up, FF_COLS, split=True), _In(dact, FF_COLS, split=True)],
                          [_Out(FF_PAD, BF16, FF_COLS, split=True), _Out(FF_PAD, BF16, FF_COLS, split=True)],
                          T, 1024, ncol=FF_PAD // FF_COLS)
    dwg = _matmul(tag + "_dwg", h, dgate, "tn")
    dwu = _matmul(tag + "_dwu", h, dup, "tn")
    token = on_dw({tag + "_w_gate": dwg, tag + "_w_up": dwu, tag + "_w_down": dwd}, dwu)
    dh = _mmw(tag + "_dh1", dgate, wg, "nt", dep=token)
    dh = _mmw(tag + "_dh2", dup, wu, "nt", res=dh)
    dx_new, dg = _rms_bwd(tag + "_dnorm", x, g, dh, dx)
    return dx_new, dg


def _rope_cos_sin(positions):
    half = ROPE_DIM // 2
    inv_freq = jnp.power(jnp.float32(ROPE_THETA), -jnp.arange(half, dtype=F32) * (2.0 / ROPE_DIM))
    head = jnp.zeros((ATT_HEAD_DIM,), F32).at[:ROPE_DIM].set(jnp.concatenate([inv_freq, inv_freq]))
    row = jnp.tile(head, LANES // ATT_HEAD_DIM)[None, :]
    T = positions.shape[0]
    return _rowwise("rope_tables", _rope_tables, [_In(positions), _In(row, rows=False)], [_Out(LANES), _Out(LANES)], T, 1024)


def _prep_fwd(name, src, width, base, g, cos, sin):
    return _rowwise(name, _qk_prep, [_In(src, width, base), _In(g, rows=False), _In(cos), _In(sin)],
                    [_Out(width)], src.shape[0], 512)[0]


def _prep_bwd(name, src, width, base, g, cos, sin, dout):
    def fn(tv, gv, cv, sv, dv):
        _, vjp = jax.vjp(lambda a, b: _qk_prep(a, b, cv, sv), tv, gv)
        return vjp(dv)

    return _rowwise(name, fn, [_In(src, width, base), _In(g, rows=False), _In(cos), _In(sin), _In(dout)],
                    [_Out(width, BF16), _Out(width, rows=False)], src.shape[0], 512)


def _to_heads(t, B, S, nh):
    return t.reshape(B, S, nh, ATT_HEAD_DIM).transpose(0, 2, 1, 3)


def _from_heads(t):
    B, nh, S, _ = t.shape
    return t.transpose(0, 2, 1, 3).reshape(B * S, nh * ATT_HEAD_DIM)


def _mix_fwd(x, cos, sin, B, S, p):
    T = B * S
    h = _rms_fwd("mix_norm", x, p["mix_norm"])
    proj = _matmul("mix_proj", h, p["w_in"], "nn")
    proj3 = proj.reshape(B, S, IN_PAD)
    q_gain = jnp.tile(p["attn_q_norm"], (1, ATT_HEADS))
    k_gain = jnp.tile(p["attn_k_norm"], (1, ATT_KV_HEADS))
    q_r = _prep_fwd("q_prep", proj, ATT_WIDTH, C_QA // ATT_WIDTH, q_gain, cos, sin)
    k_r = _prep_fwd("k_prep", proj, ATT_KV_WIDTH, C_KA // ATT_KV_WIDTH, k_gain, cos, sin)
    qh = _to_heads(q_r, B, S, ATT_HEADS)
    kh = _to_heads(k_r, B, S, ATT_KV_HEADS)
    vh = _to_heads(proj[:, C_VA:C_VA + ATT_KV_WIDTH], B, S, ATT_KV_HEADS)
    sink = p["attn_sink"].reshape(ATT_KV_HEADS, ATT_GROUP, 1, 1)
    y_a = _from_heads(_attn_fwd(qh, kh, vh, sink))

    qk_c = _conv_fwd(proj3, p["conv_w8"])
    hf, hb, *states = _mlstm_fwd(qk_c, proj3, p["gate_bias"])
    hf2, hb2 = hf.reshape(T, MLSTM_WIDTH), hb.reshape(T, MLSTM_WIDTH)
    DH = MLSTM_HEAD_DIM
    y_m = _rowwise("mlstm_out", _mlstm_combine,
                   [_In(hf2, DH, split=True), _In(hb2, DH, split=True), _In(proj, DH, C_OM // DH, split=True),
                    _In(p["mlstm_out_norm"], DH, split=True, rows=False)],
                   [_Out(MLSTM_WIDTH, BF16, DH, split=True)], T, 1024, ncol=MLSTM_HEADS)[0]

    za = _mmw("branch_a", y_a, p["w_branch_attn"], "nn")
    zm = _mmw("branch_m", y_m, p["w_branch_mlstm"], "nn")
    W = 512
    merged = _rowwise("merge", _merge,
                      [_In(proj, W, C_GMERGE // W, split=True), _In(proj, W, (C_GMERGE + D_MODEL) // W, split=True),
                       _In(za, W, split=True), _In(zm, W, split=True)],
                      [_Out(D_MODEL, BF16, W, split=True)], T, 512, ncol=D_MODEL // W)[0]
    out = _mmw("mix_out", merged, p["w_out"], "nn", res=x)
    saved = dict(x=x, h=h, proj=proj, q_gain=q_gain, k_gain=k_gain, qh=qh, kh=kh, vh=vh, sink=sink, y_a=y_a, qk_c=qk_c,
                 hf=hf2, hb=hb2, states=states, y_m=y_m, za=za, zm=zm, merged=merged)
    return out, saved


def _mix_bwd(sv, cos, sin, B, S, p, dx, on_dw):
    T = B * S
    DH = MLSTM_HEAD_DIM
    proj = sv["proj"]
    proj3 = proj.reshape(B, S, IN_PAD)
    g = {}
    dmerged = _mmw("mix_dmerged", dx, p["w_out"], "nt")
    g["w_out"] = _matmul("mix_dwout", sv["merged"], dx, "tn")
    W = 512

    def merge_bwd(ga, gm, za, zm, dm):
        _, vjp = jax.vjp(_merge, ga, gm, za, zm)
        return vjp(dm)

    dga, dgm, dza, dzm = _rowwise(
        "merge_bwd", merge_bwd,
        [_In(proj, W, C_GMERGE // W, split=True), _In(proj, W, (C_GMERGE + D_MODEL) // W, split=True),
         _In(sv["za"], W, split=True), _In(sv["zm"], W, split=True), _In(dmerged, W, split=True)],
        [_Out(D_MODEL, BF16, W, split=True), _Out(D_MODEL, BF16, W, split=True),
         _Out(D_MODEL, BF16, W, split=True), _Out(D_MODEL, BF16, W, split=True)], T, 512, ncol=D_MODEL // W)
    dya = _mmw("branch_a_dx", dza, p["w_branch_attn"], "nt")
    g["w_branch_attn"] = _matmul("branch_a_dw", sv["y_a"], dza, "tn")
    dym = _mmw("branch_m_dx", dzm, p["w_branch_mlstm"], "nt")
    g["w_branch_mlstm"] = _matmul("branch_m_dw", sv["y_m"], dzm, "tn")

    def combine_bwd(hf, hb, o_pre, gn, dy):
        _, vjp = jax.vjp(_mlstm_combine, hf, hb, o_pre, gn)
        dhf, _, do, dg = vjp(dy)
        return dhf, do, dg

    dh, dom, g["mlstm_out_norm"] = _rowwise(
        "mlstm_out_bwd", combine_bwd,
        [_In(sv["hf"], DH, split=True), _In(sv["hb"], DH, split=True), _In(proj, DH, C_OM // DH, split=True),
         _In(p["mlstm_out_norm"], DH, split=True, rows=False), _In(dym, DH, split=True)],
        [_Out(MLSTM_WIDTH, F32, DH, split=True), _Out(MLSTM_WIDTH, BF16, DH, split=True),
         _Out(MLSTM_WIDTH, F32, DH, split=True, rows=False)], T, 1024, ncol=MLSTM_HEADS)
    dqk_f, dqk_b, dv_f, dv_b, dg_f, dg_b = _mlstm_bwd(sv["qk_c"], proj3, p["gate_bias"], sv["states"],
                                                       dh.reshape(B, S, MLSTM_WIDTH))
    dgates, dvm, g["gate_bias"] = _rowwise(
        "mlstm_dsum", lambda a, b, c, d: (a + b, c + d, jnp.sum(a + b, axis=0, keepdims=True)),
        [_In(dg_f.reshape(T, LANES)), _In(dg_b.reshape(T, LANES)), _In(dv_f.reshape(T, MLSTM_WIDTH)), _In(dv_b.reshape(T, MLSTM_WIDTH))],
        [_Out(LANES, BF16), _Out(MLSTM_WIDTH, BF16), _Out(LANES, rows=False)], T, 1024)
    dqk, g["conv_w8"] = _conv_bwd(proj3, p["conv_w8"], dqk_f, dqk_b)

    dyh = _to_heads(dya, B, S, ATT_HEADS)
    dqh, dkh, dvh, dsink = _attn_bwd(sv["qh"], sv["kh"], sv["vh"], sv["sink"], dyh)
    g["attn_sink"] = dsink.reshape(1, ATT_HEADS)
    dva = _from_heads(dvh)
    dqa, dq_gain = _prep_bwd("q_prep_bwd", proj, ATT_WIDTH, C_QA // ATT_WIDTH, sv["q_gain"], cos, sin, _from_heads(dqh))
    dka, dk_gain = _prep_bwd("k_prep_bwd", proj, ATT_KV_WIDTH, C_KA // ATT_KV_WIDTH, sv["k_gain"], cos, sin, _from_heads(dkh))
    g["attn_q_norm"] = jnp.sum(dq_gain.reshape(ATT_HEADS, ATT_HEAD_DIM), axis=0, keepdims=True)
    g["attn_k_norm"] = jnp.sum(dk_gain.reshape(ATT_KV_HEADS, ATT_HEAD_DIM), axis=0, keepdims=True)

    dproj = jnp.concatenate(
        [dga, dgm, dqk.reshape(T, 2 * MLSTM_WIDTH), dvm, dom, dqa, dka, dva.astype(BF16), dgates], axis=1)
    dwin = _matmul("mix_dwin", sv["h"], dproj, "tn")
    token = on_dw({"w_in": _w_in_to_slots(dwin), "w_branch_attn": g.pop("w_branch_attn"),
                   "w_branch_mlstm": g.pop("w_branch_mlstm"), "w_out": g.pop("w_out")}, dwin)
    dh2 = _matmul("mix_dh", dproj, p["w_in"], "nt", dep=token)
    dx_new, g["mix_norm"] = _rms_bwd("mix_dnorm", sv["x"], p["mix_norm"], dh2, dx)
    return dx_new, g


def _loss_and_grad(x, g, target):
    T = x.shape[0]

    def loss_fn(xv, gv, tv):
        err = jnp.square(_rms(xv, gv) - tv)
        return 0.5 * jnp.sum(jnp.mean(err, axis=-1, keepdims=True), axis=0, keepdims=True)

    def fn(xv, gv, tv):
        val, vjp = jax.vjp(lambda a, b: loss_fn(a, b, tv), xv, gv)
        dx, dg = vjp(jnp.ones((1, 1), F32))
        return val, dx, dg

    return _rowwise("loss_head", fn, [_In(x), _In(g, rows=False), _In(target)],
                    [_Out(1, rows=False), _Out(D_MODEL), _Out(D_MODEL, rows=False)], T, ROW_BLOCK)


def _block_norm_fwd(x, g):
    T = x.shape[0]
    return _rowwise("block_norm", _rms, [_In(x), _In(g, rows=False)], [_Out(D_MODEL)], T, ROW_BLOCK)[0]


def _block_norm_bwd(x, g, dy):
    T = x.shape[0]

    def fn(xv, gv, dv):
        _, vjp = jax.vjp(_rms, xv, gv)
        return vjp(dv)

    return _rowwise("block_norm_bwd", fn, [_In(x), _In(g, rows=False), _In(dy)],
                    [_Out(D_MODEL), _Out(D_MODEL, rows=False)], T, ROW_BLOCK)


def _qk_perm_cols(t, axis):
    q, k = jnp.split(t, 2, axis=axis)
    parts = []
    for h in range(MLSTM_HEADS):
        sl = [slice(None)] * t.ndim
        sl[axis] = slice(h * MLSTM_HEAD_DIM, (h + 1) * MLSTM_HEAD_DIM)
        parts += [q[tuple(sl)], k[tuple(sl)]]
    return jnp.concatenate(parts, axis=axis)


def _qk_unperm_cols(t, axis):
    qs, ks = [], []
    for h in range(MLSTM_HEADS):
        sl = [slice(None)] * t.ndim
        sl[axis] = slice(2 * h * MLSTM_HEAD_DIM, (2 * h + 1) * MLSTM_HEAD_DIM)
        qs.append(t[tuple(sl)])
        sl[axis] = slice((2 * h + 1) * MLSTM_HEAD_DIM, (2 * h + 2) * MLSTM_HEAD_DIM)
        ks.append(t[tuple(sl)])
    return jnp.concatenate(qs + ks, axis=axis)


def _w_in_arrange(w):
    qa, ka, va, qm, km, vm, om, gm, gmerge = jnp.split(w, np.cumsum(
        (ATT_WIDTH, ATT_KV_WIDTH, ATT_KV_WIDTH, MLSTM_WIDTH, MLSTM_WIDTH, MLSTM_WIDTH, MLSTM_WIDTH, MLSTM_N_GATES))[:].tolist(), axis=1)
    qk = _qk_perm_cols(jnp.concatenate([qm, km], axis=1), 1)
    pad = jnp.zeros((w.shape[0], LANES - MLSTM_N_GATES), w.dtype)
    return jnp.concatenate([gmerge, qk, vm, om, qa, ka, va, gm, pad], axis=1)


def _w_in_restore(w):
    gmerge = w[:, C_GMERGE:C_GMERGE + 2 * D_MODEL]
    qk = _qk_unperm_cols(w[:, C_QK:C_QK + 2 * MLSTM_WIDTH], 1)
    vm, om = w[:, C_VM:C_VM + MLSTM_WIDTH], w[:, C_OM:C_OM + MLSTM_WIDTH]
    qa, ka, va = w[:, C_QA:C_QA + ATT_WIDTH], w[:, C_KA:C_KA + ATT_KV_WIDTH], w[:, C_VA:C_VA + ATT_KV_WIDTH]
    gm = w[:, C_GATES:C_GATES + MLSTM_N_GATES]
    return jnp.concatenate([qa, ka, va, qk, vm, om, gm, gmerge], axis=1)


BIG = ("ffn1_w_gate", "ffn1_w_up", "ffn1_w_down", "w_in", "mlstm_conv_w", "w_branch_attn", "w_branch_mlstm", "w_out",
       "ffn2_w_gate", "ffn2_w_up", "ffn2_w_down")
MATMUL_W = tuple(n for n in BIG if n != "mlstm_conv_w")
SMALL = ("ffn1_norm", "mix_norm", "mlstm_gate_bias", "attn_q_norm", "attn_k_norm", "attn_sink", "mlstm_conv_b",
         "mlstm_out_norm", "ffn2_norm", "block_out_norm")
WEIGHTS = ("ffn1_norm", "ffn1_w_gate", "ffn1_w_up", "ffn1_w_down", "mix_norm", "w_in", "mlstm_gate_bias", "attn_q_norm",
           "attn_k_norm", "attn_sink", "mlstm_conv_w", "mlstm_conv_b", "mlstm_out_norm", "w_branch_attn", "w_branch_mlstm",
           "w_out", "ffn2_norm", "ffn2_w_gate", "ffn2_w_up", "ffn2_w_down", "block_out_norm")
PACK_COLS = 1024


def _padded_rows(n_elems):
    return -(-n_elems // PACK_COLS)


def _pack_flat(arrs, dtype, row_multiple):
    parts = []
    for a in arrs:
        flat = a.reshape(-1).astype(dtype)
        pad = _padded_rows(flat.shape[0]) * PACK_COLS - flat.shape[0]
        parts.append(jnp.pad(flat, (0, pad)) if pad else flat)
    flat = jnp.concatenate(parts)
    rows = flat.shape[0] // PACK_COLS
    extra = (-rows) % row_multiple
    if extra:
        flat = jnp.pad(flat, (0, extra * PACK_COLS))
    return flat.reshape(-1, PACK_COLS)


def _unpack_flat(buf, shapes, lead=()):
    flat = buf.reshape(lead + (-1,))
    out, off = [], 0
    for s in shapes:
        n = int(np.prod(s))
        out.append(flat[..., off:off + n].reshape(lead + tuple(s)))
        off += _padded_rows(n) * PACK_COLS
    return out


class _Lay:
    def __init__(self, shard, axis, width):
        self.shard, self.axis, self.width = shard, axis, width
        self.padded = tuple(width if a == axis else s for a, s in enumerate(shard))
        self.whole = tuple(N_DEV * width if a == axis else s for a, s in enumerate(shard))

    def pad(self, t, lead=0):
        extra = self.width - self.shard[self.axis]
        if not extra:
            return t
        cfg = [(0, 0)] * t.ndim
        cfg[lead + self.axis] = (0, extra)
        return jnp.pad(t, cfg)

    def unpad(self, t, lead=0):
        idx = [slice(None)] * t.ndim
        idx[lead + self.axis] = slice(0, self.shard[self.axis])
        return t[tuple(idx)]


_FF_COL = _Lay((D_MODEL, FF_SHARD), 1, FF_SHARD_PAD)
_FF_ROW = _Lay((FF_SHARD, D_MODEL), 0, FF_SHARD_PAD)
LAYOUTS = {
    "ffn1_w_gate": _FF_COL, "ffn1_w_up": _FF_COL, "ffn1_w_down": _FF_ROW,
    "ffn2_w_gate": _FF_COL, "ffn2_w_up": _FF_COL, "ffn2_w_down": _FF_ROW,
    "w_in": _Lay((D_MODEL, IN_WIDTH // N_DEV), 0, D_MODEL),
    "mlstm_conv_w": _Lay((3, 2 * MLSTM_WIDTH // N_DEV), 1, 2 * MLSTM_WIDTH // N_DEV),
    "w_branch_attn": _Lay((ATT_WIDTH, D_MODEL // N_DEV), 1, D_MODEL // N_DEV),
    "w_branch_mlstm": _Lay((MLSTM_WIDTH, D_MODEL // N_DEV), 1, D_MODEL // N_DEV),
    "w_out": _Lay((D_MODEL // N_DEV, D_MODEL), 0, D_MODEL // N_DEV),
}


def _window(ref, axis, j, width):
    idx = [slice(None)] * len(ref.shape)
    idx[axis] = pl.ds(pl.multiple_of(j * width, width), width)
    return ref.at[tuple(idx)]


ANY = pl.BlockSpec(memory_space=pl.ANY)


def _mesh_pos():
    return lax.axis_index("x"), lax.axis_index("y"), lax.axis_index("c")


def _all_gather(name, shard, vmem=False):
    R, C = shard.shape
    space = pl.BlockSpec(memory_space=pltpu.VMEM) if vmem else ANY

    def body(x_ref, out_ref, send_sems, recv_sems, local_sem):
        x, y, c = _mesh_pos()
        me, sibling = (x, y, c), (x, y, 1 - c)
        chips = [(1 - x, y), (x, 1 - y), (1 - x, 1 - y)]

        def slot(px, py, pc):
            return out_ref.at[4 * px + 2 * py + pc]

        def copy(k, block, to, src=None):
            return pltpu.make_async_remote_copy(
                src_ref=slot(*block) if src is None else src, dst_ref=slot(*block),
                send_sem=send_sems.at[k], recv_sem=recv_sems.at[k], device_id=to, device_id_type=MESH)

        mine = pltpu.make_async_copy(x_ref, slot(*me), local_sem)
        mine.start()
        first = [copy(0, me, sibling, src=x_ref)]
        first += [copy(1 + j, me, (*chip, c), src=x_ref) for j, chip in enumerate(chips)]
        for cp in first:
            cp.start()
        passed = [copy(4 + j, (*chip, c), sibling) for j, chip in enumerate(chips)]
        for j, chip in enumerate(chips):
            copy(1 + j, (*chip, c), me).wait_recv()
            passed[j].start()
        copy(0, sibling, me).wait_recv()
        for j, chip in enumerate(chips):
            copy(4 + j, (*chip, 1 - c), me).wait_recv()
        for cp in first + passed:
            cp.wait_send()
        mine.wait()

    return pl.pallas_call(
        body, name=name, out_shape=jax.ShapeDtypeStruct((N_DEV, R, C), shard.dtype),
        in_specs=[space], out_specs=space,
        scratch_shapes=[pltpu.SemaphoreType.DMA((7,)), pltpu.SemaphoreType.DMA((7,)), pltpu.SemaphoreType.DMA],
    )(shard)


HBM = pl.BlockSpec(memory_space=pltpu.HBM)
SEM = pl.BlockSpec(memory_space=pltpu.SEMAPHORE)
SPLIT_COPY = pltpu.CompilerParams(has_side_effects=pltpu.SideEffectType.DATAFLOW_SIDE_EFFECTING)
N_PEERS = N_DEV - 1


def _peers(x, y, c):
    return [(x, y, 1 - c), (1 - x, y, c), (x, 1 - y, c), (1 - x, 1 - y, c),
            (1 - x, y, 1 - c), (x, 1 - y, 1 - c), (1 - x, 1 - y, 1 - c)]


def _dev_index(pos):
    return 4 * pos[0] + 2 * pos[1] + pos[2]


def _place_own(name, shards, lays):
    nt = len(shards)
    me = _dev_index(_mesh_pos())

    def body(me_ref, *refs):
        for x_ref, o_ref in zip(refs[:nt], refs[nt:]):
            o_ref[...] = x_ref[...]

    def window_spec(lay):
        if lay.axis == 0:
            return pl.BlockSpec(lay.padded, lambda i, me_ref: (me_ref[0], 0))
        return pl.BlockSpec(lay.padded, lambda i, me_ref: (0, me_ref[0]))

    return pl.pallas_call(
        body, name=name,
        grid_spec=pltpu.PrefetchScalarGridSpec(
            num_scalar_prefetch=1, grid=(1,),
            in_specs=[pl.BlockSpec(lay.padded, lambda i, me_ref: (0, 0)) for lay in lays],
            out_specs=[window_spec(lay) for lay in lays]),
        out_shape=[jax.ShapeDtypeStruct(lay.whole, s.dtype) for s, lay in zip(shards, lays)],
        compiler_params=_cparams(("arbitrary",)),
    )(me.reshape(1).astype(jnp.int32), *shards)


def _gather_start(name, shards, lands, lays, groups, after):
    nt, ng = len(shards), len(groups)

    def body(*refs):
        x_refs, land_refs = refs[:nt], refs[nt:2 * nt]
        sems = refs[2 * nt + 1:2 * nt + 1 + 2 * ng]
        pos = _mesh_pos()
        me = _dev_index(pos)
        for g, tens in enumerate(groups):
            for i, t in enumerate(tens):
                for k, peer in enumerate(_peers(*pos)):
                    pltpu.make_async_remote_copy(
                        src_ref=x_refs[t], dst_ref=_window(land_refs[t], lays[t].axis, me, lays[t].width),
                        send_sem=sems[2 * g].at[N_PEERS * i + k], recv_sem=sems[2 * g + 1].at[N_PEERS * i + k],
                        device_id=peer, device_id_type=MESH).start()

    sem_shapes = []
    for tens in groups:
        sem_shapes += [pltpu.SemaphoreType.DMA((N_PEERS * len(tens),))] * 2
    thru = [pltpu.HBM(s.shape, s.dtype) for s in shards] + [pltpu.HBM(lay.whole, s.dtype) for s, lay in zip(shards, lays)]
    args = [pltpu.with_memory_space_constraint(s, pltpu.HBM) for s in shards]
    args += [pltpu.with_memory_space_constraint(ld, pltpu.HBM) for ld in lands]
    res = pl.pallas_call(
        body, name=name, out_shape=tuple(sem_shapes + thru), in_specs=[HBM] * (2 * nt) + [ANY],
        out_specs=tuple([SEM] * (2 * ng) + [HBM] * (2 * nt)),
        input_output_aliases={t: 2 * ng + t for t in range(2 * nt)}, compiler_params=SPLIT_COPY,
    )(*args, after)
    sems = [(res[2 * g], res[2 * g + 1]) for g in range(ng)]
    return sems, list(res[2 * ng:2 * ng + nt]), list(res[2 * ng + nt:])


def _gather_wait(name, sems, shards, lands, lays, after):
    nt = len(shards)
    send_sems, recv_sems = sems

    def body(*refs):
        x_refs, land_refs = refs[:nt], refs[nt:2 * nt]
        send_ref, recv_ref = refs[2 * nt], refs[2 * nt + 1]
        pos = _mesh_pos()
        for t in range(nt):
            for k, peer in enumerate(_peers(*pos)):
                cp = pltpu.make_async_remote_copy(
                    src_ref=x_refs[t], dst_ref=_window(land_refs[t], lays[t].axis, _dev_index(peer), lays[t].width),
                    send_sem=send_ref.at[N_PEERS * t + k], recv_sem=recv_ref.at[N_PEERS * t + k],
                    device_id=peer, device_id_type=MESH)
                cp.wait_send()
                cp.wait_recv()

    thru = [pltpu.HBM(s.shape, s.dtype) for s in shards] + [pltpu.HBM(ld.shape, ld.dtype) for ld in lands]
    res = pl.pallas_call(
        body, name=name, out_shape=tuple(thru), in_specs=[HBM] * (2 * nt) + [SEM, SEM, ANY],
        out_specs=tuple([HBM] * (2 * nt)), input_output_aliases={t: t for t in range(2 * nt)},
        compiler_params=SPLIT_COPY,
    )(*shards, *lands, send_sems, recv_sems, after)
    return list(res[nt:])


def _pair_exchange(name, grads, lays):
    nt = len(grads)

    def body(*refs):
        g_refs, land_refs = refs[:nt], refs[nt:2 * nt]
        send_sems, recv_sems = refs[2 * nt:]
        x, y, c = _mesh_pos()
        copies = []
        for t in range(nt):
            for chip in range(4):
                copies.append(pltpu.make_async_remote_copy(
                    src_ref=_window(g_refs[t], lays[t].axis, 2 * chip + (1 - c), lays[t].width), dst_ref=land_refs[t].at[chip],
                    send_sem=send_sems.at[4 * t + chip], recv_sem=recv_sems.at[4 * t + chip],
                    device_id=(x, y, 1 - c), device_id_type=MESH))
        for cp in copies:
            cp.start()
        for cp in copies:
            cp.wait_recv()
        for cp in copies:
            cp.wait_send()

    out_shape = [jax.ShapeDtypeStruct((4,) + lay.padded, g.dtype) for g, lay in zip(grads, lays)]
    return pl.pallas_call(
        body, name=name, out_shape=out_shape, in_specs=[ANY] * nt, out_specs=[ANY] * nt,
        scratch_shapes=[pltpu.SemaphoreType.DMA((4 * nt,)), pltpu.SemaphoreType.DMA((4 * nt,))],
    )(*grads)


def _pair_sum(name, whole, landed, lay, out_dtype):
    R, C = lay.padded
    br = _first_divisor(R, (512, 384, 256, 128, 64, 32, 16, 8))
    nb = R // br
    if lay.axis == 0:
        mine_spec = pl.BlockSpec((br, C), lambda k, i, c_ref: ((2 * k + c_ref[0]) * nb + i, 0))
    else:
        mine_spec = pl.BlockSpec((br, C), lambda k, i, c_ref: (i, 2 * k + c_ref[0]))

    def body(c_ref, mine_ref, sib_ref, o_ref):
        o_ref[0] = (mine_ref[...] + sib_ref[0]).astype(out_dtype)

    c = lax.axis_index("c")
    return pl.pallas_call(
        body, name=name,
        grid_spec=pltpu.PrefetchScalarGridSpec(
            num_scalar_prefetch=1, grid=(4, nb),
            in_specs=[mine_spec, pl.BlockSpec((1, br, C), lambda k, i, c_ref: (k, i, 0))],
            out_specs=pl.BlockSpec((1, br, C), lambda k, i, c_ref: (k, i, 0))),
        out_shape=jax.ShapeDtypeStruct((4, R, C), out_dtype),
        compiler_params=_cparams(("parallel", "parallel")),
    )(c.reshape(1).astype(jnp.int32), whole, landed)


def _chip_exchange(name, sums):
    nt = len(sums)

    def body(*refs):
        s_refs, land_refs = refs[:nt], refs[nt:2 * nt]
        send_sems, recv_sems, local_sems = refs[2 * nt:]
        x, y, c = _mesh_pos()
        my_chip = 2 * x + y
        mine = [pltpu.make_async_copy(s_refs[t].at[my_chip], land_refs[t].at[my_chip], local_sems.at[t]) for t in range(nt)]
        for cp in mine:
            cp.start()
        chips = [(1 - x, y), (x, 1 - y), (1 - x, 1 - y)]
        copies = []
        for t in range(nt):
            for j, (px, py) in enumerate(chips):
                copies.append(pltpu.make_async_remote_copy(
                    src_ref=s_refs[t].at[2 * px + py], dst_ref=land_refs[t].at[my_chip],
                    send_sem=send_sems.at[3 * t + j], recv_sem=recv_sems.at[3 * t + j],
                    device_id=(px, py, c), device_id_type=MESH))
        for cp in copies:
            cp.start()
        for t in range(nt):
            for j, (px, py) in enumerate(chips):
                pltpu.make_async_remote_copy(
                    src_ref=s_refs[t].at[my_chip], dst_ref=land_refs[t].at[2 * px + py],
                    send_sem=send_sems.at[3 * t + j], recv_sem=recv_sems.at[3 * t + j],
                    device_id=(px, py, c), device_id_type=MESH).wait_recv()
        for cp in copies:
            cp.wait_send()
        for cp in mine:
            cp.wait()

    return pl.pallas_call(
        body, name=name, out_shape=[jax.ShapeDtypeStruct(s.shape, s.dtype) for s in sums],
        in_specs=[ANY] * nt, out_specs=[ANY] * nt,
        scratch_shapes=[pltpu.SemaphoreType.DMA((3 * nt,)), pltpu.SemaphoreType.DMA((3 * nt,)), pltpu.SemaphoreType.DMA((nt,))],
    )(*sums)


def _chip_start(name, sums):
    nt = len(sums)

    def body(*refs):
        s_refs, land_refs = refs[:nt], refs[nt:2 * nt]
        send_sems, recv_sems = refs[2 * nt], refs[2 * nt + 1]
        x, y, c = _mesh_pos()
        my_chip = 2 * x + y
        for t in range(nt):
            for j, (px, py) in enumerate([(1 - x, y), (x, 1 - y), (1 - x, 1 - y)]):
                pltpu.make_async_remote_copy(
                    src_ref=s_refs[t].at[2 * px + py], dst_ref=land_refs[t].at[my_chip],
                    send_sem=send_sems.at[3 * t + j], recv_sem=recv_sems.at[3 * t + j],
                    device_id=(px, py, c), device_id_type=MESH).start()

    thru = [pltpu.HBM(s.shape, s.dtype) for s in sums] * 2
    args = [pltpu.with_memory_space_constraint(s, pltpu.HBM) for s in sums]
    args += [pltpu.with_memory_space_constraint(lax.empty(s.shape, s.dtype), pltpu.HBM) for s in sums]
    res = pl.pallas_call(
        body, name=name, out_shape=tuple([pltpu.SemaphoreType.DMA((3 * nt,))] * 2 + thru), in_specs=[HBM] * (2 * nt),
        out_specs=tuple([SEM, SEM] + [HBM] * (2 * nt)), input_output_aliases={t: 2 + t for t in range(2 * nt)},
        compiler_params=SPLIT_COPY,
    )(*args)
    return (res[0], res[1]), list(res[2:2 + nt]), list(res[2 + nt:])


def _chip_wait(name, sems, sums, lands, after):
    nt = len(sums)

    def body(*refs):
        s_refs, land_refs = refs[:nt], refs[nt:2 * nt]
        send_sems, recv_sems = refs[2 * nt], refs[2 * nt + 1]
        x, y, c = _mesh_pos()
        my_chip = 2 * x + y
        for t in range(nt):
            for j, (px, py) in enumerate([(1 - x, y), (x, 1 - y), (1 - x, 1 - y)]):
                cp = pltpu.make_async_remote_copy(
                    src_ref=s_refs[t].at[my_chip], dst_ref=land_refs[t].at[2 * px + py],
                    send_sem=send_sems.at[3 * t + j], recv_sem=recv_sems.at[3 * t + j],
                    device_id=(px, py, c), device_id_type=MESH)
                cp.wait_send()
                cp.wait_recv()

    thru = [pltpu.HBM(s.shape, s.dtype) for s in sums] * 2
    res = pl.pallas_call(
        body, name=name, out_shape=tuple(thru), in_specs=[HBM] * (2 * nt) + [SEM, SEM, ANY],
        out_specs=tuple([HBM] * (2 * nt)), input_output_aliases={t: t for t in range(2 * nt)},
        compiler_params=SPLIT_COPY,
    )(*sums, *lands, sems[0], sems[1], after)
    return list(res[:nt]), list(res[nt:])


def _sum_chips(name, own, landed):
    _, R, C = own.shape
    br = _first_divisor(R, (512, 384, 256, 128, 64, 32, 16, 8))
    x, y, _ = _mesh_pos()
    slots = jnp.stack([2 * x + y, 2 * (1 - x) + y, 2 * x + (1 - y), 2 * (1 - x) + (1 - y)]).astype(jnp.int32)

    def body(slot_ref, mine_ref, a_ref, b_ref, c_ref, o_ref):
        o_ref[...] = ((mine_ref[0].astype(F32) + a_ref[0].astype(F32)) + b_ref[0].astype(F32)) + c_ref[0].astype(F32)

    def slot_spec(j):
        return pl.BlockSpec((1, br, C), lambda i, slot_ref: (slot_ref[j], i, 0))

    return pl.pallas_call(
        body, name=name,
        grid_spec=pltpu.PrefetchScalarGridSpec(
            num_scalar_prefetch=1, grid=(R // br,), in_specs=[slot_spec(0), slot_spec(1), slot_spec(2), slot_spec(3)],
            out_specs=pl.BlockSpec((br, C), lambda i, slot_ref: (i, 0))),
        out_shape=jax.ShapeDtypeStruct((R, C), F32), compiler_params=_cparams(("parallel",)),
    )(slots, own, landed, landed, landed)


def _sum_slots(name, slots, n):
    _, R, C = slots.shape
    br = _first_divisor(R, (512, 384, 256, 128, 64, 32, 16, 8))

    def body(s_ref, o_ref):
        acc = s_ref[0].astype(F32)
        for k in range(1, n):
            acc = acc + s_ref[k].astype(F32)
        o_ref[...] = acc

    return pl.pallas_call(
        body, name=name, grid=(R // br,), in_specs=[pl.BlockSpec((n, br, C), lambda i: (0, i, 0))],
        out_specs=pl.BlockSpec((br, C), lambda i: (i, 0)), out_shape=jax.ShapeDtypeStruct((R, C), F32),
        compiler_params=_cparams(("parallel",)),
    )(slots)


def _reduce_scatter_start(tag, names, grads):
    lays = [LAYOUTS[n] for n in names]
    landed = _pair_exchange("grads_pair_" + names[0], grads, lays)
    sums = [_pair_sum("grads_pairsum_" + n, g, ld, lay, BF16) for n, g, ld, lay in zip(names, grads, landed, lays)]
    sems, sums, lands = _chip_start(tag + "_chips_start", sums)
    return tag, names, sems, sums, lands


def _reduce_scatter_finish(pending, after):
    tag, names, sems, sums, lands = pending
    own, got = _chip_wait(tag + "_chips_wait", sems, sums, lands, after)
    return [_sum_chips("grads_sum_" + n, o, s) for n, o, s in zip(names, own, got)]


def _adamw_math(w, g, m, v):
    m = ADAM_B1 * m + (1.0 - ADAM_B1) * g
    v = ADAM_B2 * v + (1.0 - ADAM_B2) * jnp.square(g)
    m_hat = m / (1.0 - ADAM_B1 ** ADAM_STEP)
    v_hat = v / (1.0 - ADAM_B2 ** ADAM_STEP)
    delta = -ADAM_LR * (m_hat / (jnp.sqrt(v_hat) + ADAM_EPS) + ADAM_WD * w)
    return delta, m, v


def _adamw(name, w, g, m, v):
    shape = w.shape
    cols = shape[-1]
    rows = int(np.prod(shape[:-1]))
    br = _first_divisor(rows, (512, 352, 256, 128, 64, 32, 16, 8))
    args = [_In(a.reshape(rows, cols)) for a in (w, g, m, v)]
    outs = _rowwise(name, _adamw_math, args, [_Out(cols), _Out(cols), _Out(cols)], rows, br)
    return [o.reshape(shape) for o in outs]


GROUPS = {"ffn1": ("ffn1_w_gate", "ffn1_w_up", "ffn1_w_down"),
          "mix": ("w_in", "w_branch_attn", "w_branch_mlstm", "w_out"),
          "ffn2": ("ffn2_w_gate", "ffn2_w_up", "ffn2_w_down")}
GATHER_GROUPS = {"ffn1_in": ("ffn1_w_gate", "ffn1_w_up"), "ffn1_out": ("ffn1_w_down",),
                 "mix": ("w_in", "w_branch_attn", "w_branch_mlstm", "w_out"),
                 "ffn2_in": ("ffn2_w_gate", "ffn2_w_up"), "ffn2_out": ("ffn2_w_down",)}


def _small_params(small, conv_w, l):
    p = {}
    for n in ("ffn1_norm", "mix_norm", "ffn2_norm", "block_out_norm", "mlstm_out_norm", "attn_q_norm", "attn_k_norm"):
        p[n] = small[n][l][None, :]
    p["attn_sink"] = small["attn_sink"][l]
    p["gate_bias"] = jnp.pad(small["mlstm_gate_bias"][l], (0, LANES - MLSTM_N_GATES))[None, :]
    taps = _qk_perm_cols(conv_w[l], 1)
    conv_b = _qk_perm_cols(small["mlstm_conv_b"][l][None, :], 1)
    p["conv_w8"] = jnp.concatenate([taps, conv_b, jnp.zeros((4, 2 * MLSTM_WIDTH), F32)], axis=0)
    return p


def _w_in_from_slots(slots):
    w_in = slots.reshape(N_DEV, D_MODEL, IN_WIDTH // N_DEV).transpose(1, 0, 2).reshape(D_MODEL, IN_WIDTH)
    return _w_in_arrange(w_in)


def _w_in_to_slots(g):
    return _w_in_restore(g).reshape(D_MODEL, N_DEV, IN_WIDTH // N_DEV).transpose(1, 0, 2).reshape(
        N_DEV * D_MODEL, IN_WIDTH // N_DEV)


def _local_step(x, positions, target, weights_of, small, conv_w, on_grads):
    B, S, _ = x.shape
    T = B * S
    cos, sin = _rope_cos_sin(positions.reshape(T, 1))
    params = [_small_params(small, conv_w, l) for l in range(DEPTH)]
    xs = x.reshape(T, D_MODEL)
    tgt = target.reshape(T, D_MODEL)

    saved = []
    for l, p in enumerate(params):
        p.update(weights_of(l, "ffn1_in", xs))
        x1, s1, p["ffn1_w_down"] = _ffn_fwd("ffn1", xs, p["ffn1_norm"], p["ffn1_w_gate"], p["ffn1_w_up"],
                                            lambda after, l=l: weights_of(l, "ffn1_out", after)["ffn1_w_down"])
        p.update(weights_of(l, "mix", x1))
        p["w_in"] = _w_in_from_slots(p["w_in"])
        x2, s2 = _mix_fwd(x1, cos, sin, B, S, p)
        p.update(weights_of(l, "ffn2_in", x2))
        x3, s3, p["ffn2_w_down"] = _ffn_fwd("ffn2", x2, p["ffn2_norm"], p["ffn2_w_gate"], p["ffn2_w_up"],
                                            lambda after, l=l: weights_of(l, "ffn2_out", after)["ffn2_w_down"])
        saved.append((s1, s2, s3, x3))
        if l + 1 < DEPTH:
            xs = _block_norm_fwd(x3, p["block_out_norm"])

    sm = {n: [None] * DEPTH for n in SMALL + ("mlstm_conv_w",)}
    loss = None
    dx = None
    for l in reversed(range(DEPTH)):
        p = params[l]
        s1, s2, s3, x3 = saved[l]
        if l == DEPTH - 1:
            loss, dx, dgn = _loss_and_grad(x3, p["block_out_norm"], tgt)
        else:
            dx, dgn = _block_norm_bwd(x3, p["block_out_norm"], dx)
        sm["block_out_norm"][l] = dgn[0]
        dx, dg = _ffn_bwd("ffn2", s3, p["ffn2_norm"], p["ffn2_w_gate"], p["ffn2_w_up"], p["ffn2_w_down"], dx,
                          functools.partial(on_grads, l, "ffn2"))
        sm["ffn2_norm"][l] = dg[0]
        dx, g = _mix_bwd(s2, cos, sin, B, S, p, dx, functools.partial(on_grads, l, "mix"))
        dconv = _qk_unperm_cols(g["conv_w8"], 1)
        sm["mlstm_conv_w"][l] = dconv[0:3]
        sm["mlstm_conv_b"][l] = dconv[3]
        sm["mix_norm"][l] = g["mix_norm"][0]
        sm["mlstm_gate_bias"][l] = g["gate_bias"][0, :MLSTM_N_GATES]
        sm["attn_q_norm"][l], sm["attn_k_norm"][l] = g["attn_q_norm"][0], g["attn_k_norm"][0]
        sm["attn_sink"][l] = g["attn_sink"][0]
        sm["mlstm_out_norm"][l] = g["mlstm_out_norm"][0]
        dx, dg = _ffn_bwd("ffn1", s1, p["ffn1_norm"], p["ffn1_w_gate"], p["ffn1_w_up"], p["ffn1_w_down"], dx,
                          functools.partial(on_grads, l, "ffn1"))
        sm["ffn1_norm"][l] = dg[0]
    sm = {n: jnp.stack(v, axis=0) for n, v in sm.items()}
    return loss, dx.reshape(B, S, D_MODEL), sm


def kernel(x, positions, ffn1_norm, ffn1_w_gate, ffn1_w_up, ffn1_w_down, mix_norm, w_in, mlstm_gate_bias, attn_q_norm, attn_k_norm, attn_sink, mlstm_conv_w, mlstm_conv_b, mlstm_out_norm, w_branch_attn, w_branch_mlstm, w_out, ffn2_norm, ffn2_w_gate, ffn2_w_up, ffn2_w_down, block_out_norm, loss_target, m_ffn1_norm, m_ffn1_w_gate, m_ffn1_w_up, m_ffn1_w_down, m_mix_norm, m_w_in, m_mlstm_gate_bias, m_attn_q_norm, m_attn_k_norm, m_attn_sink, m_mlstm_conv_w, m_mlstm_conv_b, m_mlstm_out_norm, m_w_branch_attn, m_w_branch_mlstm, m_w_out, m_ffn2_norm, m_ffn2_w_gate, m_ffn2_w_up, m_ffn2_w_down, m_block_out_norm, v_ffn1_norm, v_ffn1_w_gate, v_ffn1_w_up, v_ffn1_w_down, v_mix_norm, v_w_in, v_mlstm_gate_bias, v_attn_q_norm, v_attn_k_norm, v_attn_sink, v_mlstm_conv_w, v_mlstm_conv_b, v_mlstm_out_norm, v_w_branch_attn, v_w_branch_mlstm, v_w_out, v_ffn2_norm, v_ffn2_w_gate, v_ffn2_w_up, v_ffn2_w_down, v_block_out_norm):
    args = locals()
    w = {n: args[n] for n in WEIGHTS}
    m = {n: args["m_" + n] for n in WEIGHTS}
    v = {n: args["v_" + n] for n in WEIGHTS}

    order = [(l, grp) for l in range(DEPTH) for grp in GATHER_GROUPS]
    keys = [(l, n) for l, grp in order for n in GATHER_GROUPS[grp]]
    lays = [LAYOUTS[n] for _, n in keys]
    shards = [lay.pad(w[n][l].astype(BF16)) for (l, n), lay in zip(keys, lays)]
    group_idx, at = {}, 0
    for l, grp in order:
        group_idx[(l, grp)] = list(range(at, at + len(GATHER_GROUPS[grp])))
        at += len(GATHER_GROUPS[grp])
    conv_shape = w["mlstm_conv_w"].shape
    conv_all = _all_gather("conv_all_gather", _pack_flat([w["mlstm_conv_w"]], F32, 8), vmem=True)
    conv_parts = _unpack_flat(conv_all, [conv_shape], lead=(N_DEV,))[0]
    conv_w = jnp.concatenate([conv_parts[j] for j in range(N_DEV)], axis=2)
    small = {n: w[n] for n in SMALL}

    lands = []
    for l, grp in order:
        idx = group_idx[(l, grp)]
        lands += _place_own("weights_place_" + grp, [shards[i] for i in idx], [lays[i] for i in idx])
    sems, shards, lands = _gather_start("weights_gather_start", shards, lands, lays, [group_idx[k] for k in order], conv_all)

    def weights_of(l, grp, after):
        idx = group_idx[(l, grp)]
        whole = _gather_wait(f"weights_gather_wait_{l}_{grp}", sems[order.index((l, grp))], [shards[i] for i in idx],
                             [lands[i] for i in idx], [lays[i] for i in idx], after)
        return dict(zip(GATHER_GROUPS[grp], whole))

    totals, pending = {}, []

    def finish(after):
        tag, names = pending[0][0], pending[0][1]
        for n, t in zip(names, _reduce_scatter_finish(pending.pop(0), after)):
            totals[(tag, n)] = t

    def on_grads(l, grp, g, after):
        if pending:
            finish(after)
        names = GROUPS[grp]
        pending.append(_reduce_scatter_start(f"grads_{l}_{grp}", names, [g[n] for n in names]))
        return pending[-1][3][0]

    loss, grad_x, small_g = _local_step(x, positions, loss_target, weights_of, small, conv_w, on_grads)
    finish(grad_x)
    grads = {}
    for grp, names in GROUPS.items():
        for n in names:
            grads[n] = jnp.stack([LAYOUTS[n].unpad(totals[(f"grads_{l}_{grp}", n)]) for l in range(DEPTH)], axis=0)

    small_names = SMALL + ("mlstm_conv_w",)
    small_shapes = [small_g[n].shape for n in small_names] + [(1, 1)]
    small_packed = _pack_flat([small_g[n] for n in small_names] + [loss], F32, 8)
    small_all = _all_gather("small_all_gather", small_packed, vmem=True)
    small_sum = _sum_slots("small_sum", small_all, N_DEV)
    *small_grads, loss_total = _unpack_flat(small_sum, small_shapes)
    grads.update(dict(zip(small_names, small_grads)))
    x_pos, y_pos, c_pos = _mesh_pos()
    grads["mlstm_conv_w"] = lax.dynamic_slice_in_dim(
        grads["mlstm_conv_w"], (4 * x_pos + 2 * y_pos + c_pos) * conv_shape[2], conv_shape[2], axis=2)

    deltas, new_m, new_v = {}, {}, {}
    for n in BIG:
        deltas[n], new_m[n], new_v[n] = _adamw("adamw_" + n, w[n], grads[n], m[n], v[n])
    sw, sg, smm, sv = (_pack_flat([d[n] for n in SMALL], F32, 8) for d in (w, grads, m, v))
    sd, snm, snv = _adamw("adamw_small", sw, sg, smm, sv)
    shapes = [w[n].shape for n in SMALL]
    for d, buf in ((deltas, sd), (new_m, snm), (new_v, snv)):
        d.update(dict(zip(SMALL, _unpack_flat(buf, shapes))))

    return (loss_total.reshape(()), grad_x, *[grads[n] for n in WEIGHTS], *[deltas[n] for n in WEIGHTS],
            *[new_m[n] for n in WEIGHTS], *[new_v[n] for n in WEIGHTS])
```

```python
import functools

import numpy as np
import jax
import jax.numpy as jnp
from jax import lax
from jax.experimental import pallas as pl
from jax.experimental.pallas import tpu as pltpu

F32 = jnp.float32
BF16 = jnp.bfloat16

D_MODEL = 1024
D_FF = 2816
ATT_HEAD_DIM = 64
ATT_HEADS = 8
ATT_KV_HEADS = 2
ATT_GROUP = ATT_HEADS // ATT_KV_HEADS
ATT_WIDTH = ATT_HEADS * ATT_HEAD_DIM
ATT_KV_WIDTH = ATT_KV_HEADS * ATT_HEAD_DIM
WINDOW = 128
ATT_BLOCK = 128
ROPE_DIM = 16
ROPE_THETA = 500000.0
MLSTM_HEADS = 4
MLSTM_HEAD_DIM = 128
MLSTM_WIDTH = MLSTM_HEADS * MLSTM_HEAD_DIM
MLSTM_CHUNK = 128
MLSTM_N_GATES = 4 * MLSTM_HEADS
NORM_EPS = 1e-6
IN_WIDTH = 4880
DEPTH = 2
N_DEV = 8

ADAM_LR = 0.001
ADAM_B1 = 0.9
ADAM_B2 = 0.999
ADAM_EPS = 1e-08
ADAM_WD = 0.01
ADAM_STEP = 10

LANES = 128
C_GMERGE = 0
C_QK = 2048
C_VM = 3072
C_OM = 3584
C_QA = 4096
C_KA = 4608
C_VA = 4736
C_GATES = 4864
IN_PAD = 4992

VMEM_LIMIT = 48 * 1024 * 1024

MESH = pl.DeviceIdType.MESH


def _cparams(sem):
    return pltpu.CompilerParams(dimension_semantics=sem, vmem_limit_bytes=VMEM_LIMIT)


def _first_divisor(n, cands):
    for c in cands:
        if n % c == 0:
            return c
    return n


_NN = ((1,), (0,))
_NT = ((1,), (1,))
_TN = ((0,), (0,))


def _mm(a, b, dims):
    return lax.dot_general(a.astype(BF16), b.astype(BF16), (dims, ((), ())), preferred_element_type=F32)


@jax.custom_vjp
def mm_nn(a, b):
    return _mm(a, b, _NN)


def _mm_nn_fwd(a, b):
    return _mm(a, b, _NN), (a, b)


def _mm_nn_bwd(res, g):
    a, b = res
    return _mm(g, b, _NT).astype(a.dtype), _mm(a, g, _TN).astype(b.dtype)


mm_nn.defvjp(_mm_nn_fwd, _mm_nn_bwd)


@jax.custom_vjp
def mm_nt(a, b):
    return _mm(a, b, _NT)


def _mm_nt_fwd(a, b):
    return _mm(a, b, _NT), (a, b)


def _mm_nt_bwd(res, g):
    a, b = res
    return _mm(g, b, _NN).astype(a.dtype), _mm(g, a, _TN).astype(b.dtype)


mm_nt.defvjp(_mm_nt_fwd, _mm_nt_bwd)


@jax.custom_vjp
def mm_tn(a, b):
    return _mm(a, b, _TN)


def _mm_tn_fwd(a, b):
    return _mm(a, b, _TN), (a, b)


def _mm_tn_bwd(res, g):
    a, b = res
    return _mm(b, g, _NT).astype(a.dtype), _mm(a, g, _NN).astype(b.dtype)


mm_tn.defvjp(_mm_tn_fwd, _mm_tn_bwd)


def _matmul(name, a, b, mode, out_dtype=F32, res=None, scale=1.0, bl=None, dep=None):
    b_shape = b.shape if bl is None else b.shape[1:]
    if mode == "nn":
        (M, K), (K2, N) = a.shape, b_shape
    elif mode == "nt":
        (M, K), (N, K2) = a.shape, b_shape
    else:
        (K, M), (K2, N) = a.shape, b_shape
    assert K == K2, (name, a.shape, b.shape)
    tm = _first_divisor(M, (1024, 512, 384, 256, 128))
    tn = _first_divisor(N, (1024, 1664, 512, 384, 256, 128))
    tk = _first_divisor(K, (1024, 1664, 512, 256, 128))
    nk = K // tk
    if mode == "tn":
        a_spec = pl.BlockSpec((tk, tm), lambda i, j, k: (k, i))
    else:
        a_spec = pl.BlockSpec((tm, tk), lambda i, j, k: (i, k))
    if mode == "nt":
        b_blk, b_idx = (tn, tk), (lambda i, j, k: (j, k))
    else:
        b_blk, b_idx = (tk, tn), (lambda i, j, k: (k, j))
    if bl is None:
        b_spec = pl.BlockSpec(b_blk, b_idx)
    else:
        b_spec = pl.BlockSpec((None,) + b_blk, lambda i, j, k: (bl,) + b_idx(i, j, k))
    o_spec = pl.BlockSpec((tm, tn), lambda i, j, k: (i, j))
    dims = {"nn": _NN, "nt": _NT, "tn": _TN}[mode]
    has_res = res is not None

    def body(*refs):
        a_ref, b_ref = refs[:2]
        r_ref = refs[2] if has_res else None

        def finish(out):
            if scale != 1.0:
                out = out * scale
            if has_res:
                out = r_ref[...].astype(F32) + out
            o_ref[...] = out.astype(out_dtype)

        if nk == 1:
            o_ref = refs[-1]
            finish(_mm(a_ref[...], b_ref[...], dims))
            return
        o_ref, acc = refs[-2:]
        k = pl.program_id(2)

        @pl.when(k == 0)
        def _():
            acc[...] = jnp.zeros_like(acc)

        acc[...] += _mm(a_ref[...], b_ref[...], dims)

        @pl.when(k == nk - 1)
        def _():
            finish(acc[...])

    in_specs = [a_spec, b_spec] + ([o_spec] if has_res else [])
    args = (a, b) + ((res,) if has_res else ())
    if dep is not None:
        in_specs.append(pl.BlockSpec(memory_space=pl.ANY))
        args += (dep,)
    return pl.pallas_call(
        body, name=name, grid=(M // tm, N // tn, nk), in_specs=in_specs, out_specs=o_spec,
        out_shape=jax.ShapeDtypeStruct((M, N), out_dtype),
        scratch_shapes=[pltpu.VMEM((tm, tn), F32)] if nk > 1 else [],
        compiler_params=_cparams(("parallel", "parallel", "arbitrary")),
    )(*args)


class _In:
    def __init__(self, arr, width=None, base=0, split=False, rows=True):
        self.arr, self.base, self.split, self.rows = arr, base, split, rows
        self.width = arr.shape[1] if width is None else width


class _Out:
    def __init__(self, cols, dtype=F32, width=None, split=False, rows=True, nrows=1):
        self.cols, self.dtype, self.split, self.rows, self.nrows = cols, dtype, split, rows, nrows
        self.width = cols if width is None else width


def _rowwise(name, fn, ins, outs, n_rows, br, ncol=1):
    br = min(br, n_rows)
    assert n_rows % br == 0, (name, n_rows, br)
    nrow_blocks = n_rows // br

    def in_spec(d):
        nb = br if d.rows else d.arr.shape[0]
        if d.rows and d.split:
            im = lambda j, i, base=d.base: (i, base + j)
        elif d.rows:
            im = lambda j, i, base=d.base: (i, base)
        elif d.split:
            im = lambda j, i, base=d.base: (0, base + j)
        else:
            im = lambda j, i, base=d.base: (0, base)
        return pl.BlockSpec((nb, d.width), im)

    def out_spec(d):
        nb = br if d.rows else d.nrows
        if d.rows and d.split:
            im = lambda j, i: (i, j)
        elif d.rows:
            im = lambda j, i: (i, 0)
        elif d.split:
            im = lambda j, i: (0, j)
        else:
            im = lambda j, i: (0, 0)
        return pl.BlockSpec((nb, d.width), im)

    n_in = len(ins)

    def body(*refs):
        i = pl.program_id(1)
        vals = [r[...] for r in refs[:n_in]]
        res = fn(*vals)
        if not isinstance(res, (tuple, list)):
            res = (res,)
        for d, ref, val in zip(outs, refs[n_in:], res):
            if d.rows:
                ref[...] = val.astype(d.dtype)
            else:
                @pl.when(i == 0)
                def _(ref=ref):
                    ref[...] = jnp.zeros_like(ref)

                ref[...] += val.astype(d.dtype)

    out_shape = [jax.ShapeDtypeStruct((n_rows if d.rows else d.nrows, d.cols), d.dtype) for d in outs]
    res = pl.pallas_call(
        body, name=name, grid=(ncol, nrow_blocks), in_specs=[in_spec(d) for d in ins],
        out_specs=[out_spec(d) for d in outs], out_shape=out_shape,
        compiler_params=_cparams(("parallel", "arbitrary")),
    )(*[d.arr for d in ins])
    return res


def _rms(x, g):
    return x * lax.rsqrt(jnp.mean(x * x, axis=-1, keepdims=True) + NORM_EPS) * g


def _sigmoid(x):
    return 1.0 / (1.0 + jnp.exp(-x))


def _silu(x):
    return x * _sigmoid(x)


def _log_sigmoid(x):
    return jnp.minimum(x, 0.0) - jnp.log(1.0 + jnp.exp(-jnp.abs(x)))


def _rope_tables(pos, inv_freq_row):
    ang = pos.astype(F32) * inv_freq_row
    return jnp.cos(ang), jnp.sin(ang)


def _head_sums_impl(v):
    w = v.shape[-1]
    shift = ATT_HEAD_DIM.bit_length() - 1
    r = lax.shift_right_logical(lax.broadcasted_iota(jnp.int32, (w, w), 0), shift)
    c = lax.shift_right_logical(lax.broadcasted_iota(jnp.int32, (w, w), 1), shift)
    ones = (r == c).astype(BF16)
    hi = v.astype(BF16)
    lo = (v - hi.astype(F32)).astype(BF16)
    dn = (_NN, ((), ()))
    return (lax.dot_general(hi, ones, dn, preferred_element_type=F32)
            + lax.dot_general(lo, ones, dn, preferred_element_type=F32))


@jax.custom_vjp
def _head_sums(v):
    return _head_sums_impl(v)


_head_sums.defvjp(lambda v: (_head_sums_impl(v), None), lambda _, g: (_head_sums_impl(g),))


def _rotate_half_impl(y):
    w = y.shape[-1]
    half = ROPE_DIM // 2
    lane = lax.broadcasted_iota(jnp.int32, y.shape, 1) & (ATT_HEAD_DIM - 1)
    above = pltpu.roll(y, w - half, axis=1)
    below = pltpu.roll(y, half, axis=1)
    return jnp.where(lane < half, -above, jnp.where(lane < ROPE_DIM, below, 0.0))


@jax.custom_vjp
def _rotate_half(y):
    return _rotate_half_impl(y)


_rotate_half.defvjp(lambda y: (_rotate_half_impl(y), None), lambda _, g: (-_rotate_half_impl(g),))


def _qk_prep(t, g, cos, sin):
    reps = t.shape[-1] // cos.shape[-1]
    if reps > 1:
        cos, sin = jnp.tile(cos, (1, reps)), jnp.tile(sin, (1, reps))
    y = t * lax.rsqrt(_head_sums(t * t) * (1.0 / ATT_HEAD_DIM) + NORM_EPS) * g
    return y * cos + _rotate_half(y) * sin


def _attn_head(q, kb, vb, sink, valid):
    s = mm_nt(q, kb) * (ATT_HEAD_DIM ** -0.5)
    s = jnp.where(valid, s, -jnp.inf)
    m = jnp.maximum(jnp.max(s, axis=-1, keepdims=True), sink)
    p = jnp.exp(s - m)
    den = jnp.sum(p, axis=-1, keepdims=True) + jnp.exp(sink - m)
    return mm_nn(p / den, vb)


def _mlstm_chunk(q, k, v, li, lf_pre, C, n, m, incl, incl_t, eye):
    k = k * (MLSTM_HEAD_DIM ** -0.5)
    lf = _log_sigmoid(lf_pre)
    lf_row = jnp.sum(eye * lf, axis=0, keepdims=True)
    li_row = jnp.sum(eye * li, axis=0, keepdims=True)
    b = jnp.sum(incl * lf_row, axis=1, keepdims=True)
    b_row = jnp.sum(incl_t * lf, axis=0, keepdims=True)
    b_tot = jnp.sum(lf, axis=0, keepdims=True)
    a = b_tot - b + li
    a_max = jnp.max(a, axis=0, keepdims=True)
    kw = k * jnp.exp(a - a_max)
    c_loc = mm_tn(kw, v)
    n_loc = jnp.sum(kw, axis=0, keepdims=True)

    dmat = jnp.where(incl > 0.5, b - b_row + li_row, -jnp.inf)
    inter = b + m
    m_t = jnp.maximum(inter, jnp.max(dmat, axis=1, keepdims=True))
    sc = mm_nt(q, k) * jnp.exp(dmat - m_t)
    scale_in = jnp.exp(inter - m_t)
    num = mm_nn(sc, v) + scale_in * mm_nn(q, C)
    den = jnp.sum(sc, axis=1, keepdims=True) + scale_in * jnp.sum(q * n, axis=1, keepdims=True)
    h = num / jnp.maximum(jnp.abs(den), jnp.exp(-m_t))

    m_new = jnp.maximum(b_tot + m, a_max)
    s_p = jnp.exp(b_tot + m - m_new)
    s_l = jnp.exp(a_max - m_new)
    return h, s_p * C + s_l * c_loc, s_p * n + s_l * n_loc, m_new


def _mlstm_combine(hf, hb, o_pre, g):
    h = hf + hb
    mu = jnp.mean(h, axis=-1, keepdims=True)
    var = jnp.mean(jnp.square(h - mu), axis=-1, keepdims=True)
    return _sigmoid(o_pre) * ((h - mu) * lax.rsqrt(var + NORM_EPS) * g)


def _merge(ga, gm, za, zm):
    return _sigmoid(ga) * za + _sigmoid(gm) * zm


def _attn_mask(n, seq):
    qi = n * ATT_BLOCK + lax.broadcasted_iota(jnp.int32, (ATT_BLOCK, 3 * ATT_BLOCK), 0)
    kj = (n - 1) * ATT_BLOCK + lax.broadcasted_iota(jnp.int32, (ATT_BLOCK, 3 * ATT_BLOCK), 1)
    return (jnp.abs(qi - kj) <= WINDOW) & (kj >= 0) & (kj < seq)


def _attn_specs(nq):
    q_spec = pl.BlockSpec((1, ATT_HEADS, ATT_BLOCK, ATT_HEAD_DIM), lambda b, n: (b, 0, n, 0))

    def kv_spec(off):
        return pl.BlockSpec((1, ATT_KV_HEADS, ATT_BLOCK, ATT_HEAD_DIM),
                            lambda b, n: (b, 0, jnp.clip(n + off, 0, nq - 1), 0))

    sink_spec = pl.BlockSpec((ATT_KV_HEADS, ATT_GROUP, 1, 1), lambda b, n: (0, 0, 0, 0))
    return q_spec, kv_spec, sink_spec


def _attn_fwd(q, k, v, sink):
    B, _, S, _ = q.shape
    nq = S // ATT_BLOCK
    q_spec, kv_spec, sink_spec = _attn_specs(nq)

    def body(q_ref, kp, kc, kn, vp, vc, vn, s_ref, o_ref):
        valid = _attn_mask(pl.program_id(1), S)
        for h in range(ATT_KV_HEADS):
            kb = jnp.concatenate([kp[0, h], kc[0, h], kn[0, h]], axis=0)
            vb = jnp.concatenate([vp[0, h], vc[0, h], vn[0, h]], axis=0)
            for g in range(ATT_GROUP):
                o_ref[0, h * ATT_GROUP + g] = _attn_head(q_ref[0, h * ATT_GROUP + g], kb, vb, s_ref[h, g], valid).astype(BF16)

    return pl.pallas_call(
        body, name="attn_fwd", grid=(B, nq),
        in_specs=[q_spec, kv_spec(-1), kv_spec(0), kv_spec(1), kv_spec(-1), kv_spec(0), kv_spec(1), sink_spec],
        out_specs=q_spec, out_shape=jax.ShapeDtypeStruct(q.shape, BF16),
        compiler_params=_cparams(("parallel", "arbitrary")),
    )(q, k, k, k, v, v, v, sink)


def _attn_bwd(q, k, v, sink, dy):
    B, _, S, _ = q.shape
    nq = S // ATT_BLOCK
    q_spec, kv_spec, sink_spec = _attn_specs(nq)
    kv_full = pl.BlockSpec((1, ATT_KV_HEADS, S, ATT_HEAD_DIM), lambda b, n: (b, 0, 0, 0))

    def body(q_ref, kp, kc, kn, vp, vc, vn, s_ref, dy_ref, dq_ref, dk_ref, dv_ref, ds_ref):
        b, n = pl.program_id(0), pl.program_id(1)
        valid = _attn_mask(n, S)

        @pl.when(n == 0)
        def _():
            dk_ref[...] = jnp.zeros_like(dk_ref)
            dv_ref[...] = jnp.zeros_like(dv_ref)

        @pl.when((n == 0) & (b == 0))
        def _():
            ds_ref[...] = jnp.zeros_like(ds_ref)

        for h in range(ATT_KV_HEADS):
            kb = jnp.concatenate([kp[0, h], kc[0, h], kn[0, h]], axis=0)
            vb = jnp.concatenate([vp[0, h], vc[0, h], vn[0, h]], axis=0)
            dkb = jnp.zeros_like(kb)
            dvb = jnp.zeros_like(vb)
            for g in range(ATT_GROUP):
                hq = h * ATT_GROUP + g
                _, vjp = jax.vjp(functools.partial(_attn_head, valid=valid), q_ref[0, hq], kb, vb, s_ref[h, g])
                dq, dk_g, dv_g, dsink = vjp(dy_ref[0, hq])
                dq_ref[0, hq] = dq
                ds_ref[h, g] += dsink
                dkb += dk_g
                dvb += dv_g
            for j, off in enumerate((-1, 0, 1)):
                start = pl.multiple_of(jnp.clip(n + off, 0, nq - 1) * ATT_BLOCK, ATT_BLOCK)
                rows = pl.ds(start, ATT_BLOCK)
                dk_ref[0, h, rows, :] += dkb[j * ATT_BLOCK:(j + 1) * ATT_BLOCK]
                dv_ref[0, h, rows, :] += dvb[j * ATT_BLOCK:(j + 1) * ATT_BLOCK]

    return pl.pallas_call(
        body, name="attn_bwd", grid=(B, nq),
        in_specs=[q_spec, kv_spec(-1), kv_spec(0), kv_spec(1), kv_spec(-1), kv_spec(0), kv_spec(1), sink_spec, q_spec],
        out_specs=[q_spec, kv_full, kv_full, sink_spec],
        out_shape=[jax.ShapeDtypeStruct(q.shape, F32), jax.ShapeDtypeStruct(k.shape, F32),
                   jax.ShapeDtypeStruct(v.shape, F32), jax.ShapeDtypeStruct(sink.shape, F32)],
        compiler_params=_cparams(("arbitrary", "arbitrary")),
    )(q, k, k, k, v, v, v, sink, dy)


CONV_COLS = 256


def _conv_taps(u, seq):
    row = lax.broadcasted_iota(jnp.int32, u.shape, 0)
    prev = jnp.where(row == 0, 0.0, pltpu.roll(u, 1, axis=0))
    nxt = jnp.where(row == seq - 1, 0.0, pltpu.roll(u, seq - 1, axis=0))
    return prev, nxt


def _conv_fwd(proj3, w8):
    B, S, _ = proj3.shape
    ncb = 2 * MLSTM_WIDTH // CONV_COLS

    def body(u_ref, w_ref, o_ref):
        u = u_ref[0]
        prev, nxt = _conv_taps(u, S)
        o_ref[0] = _silu(prev * w_ref[0:1, :] + u * w_ref[1:2, :] + nxt * w_ref[2:3, :] + w_ref[3:4, :])

    return pl.pallas_call(
        body, name="conv_fwd", grid=(B, ncb),
        in_specs=[pl.BlockSpec((1, S, CONV_COLS), lambda b, c: (b, 0, C_QK // CONV_COLS + c)),
                  pl.BlockSpec((8, CONV_COLS), lambda b, c: (0, c))],
        out_specs=pl.BlockSpec((1, S, CONV_COLS), lambda b, c: (b, 0, c)),
        out_shape=jax.ShapeDtypeStruct((B, S, 2 * MLSTM_WIDTH), F32),
        compiler_params=_cparams(("parallel", "parallel")),
    )(proj3, w8)


def _conv_bwd(proj3, w8, dout_f, dout_b):
    B, S, _ = proj3.shape
    ncb = 2 * MLSTM_WIDTH // CONV_COLS

    def body(u_ref, w_ref, df_ref, db_ref, du_ref, dw_ref):
        b = pl.program_id(1)
        u = u_ref[0]
        prev, nxt = _conv_taps(u, S)
        w0, w1, w2 = w_ref[0:1, :], w_ref[1:2, :], w_ref[2:3, :]
        pre = prev * w0 + u * w1 + nxt * w2 + w_ref[3:4, :]
        sig = _sigmoid(pre)
        dpre = (df_ref[0] + db_ref[0]) * (sig * (1.0 + pre * (1.0 - sig)))
        dprev, dnxt = _conv_taps(dpre, S)
        du_ref[0] = (dnxt * w0 + dpre * w1 + dprev * w2).astype(BF16)

        @pl.when(b == 0)
        def _():
            dw_ref[...] = jnp.zeros_like(dw_ref)

        dw_ref[0:1, :] += jnp.sum(dpre * prev, axis=0, keepdims=True)
        dw_ref[1:2, :] += jnp.sum(dpre * u, axis=0, keepdims=True)
        dw_ref[2:3, :] += jnp.sum(dpre * nxt, axis=0, keepdims=True)
        dw_ref[3:4, :] += jnp.sum(dpre, axis=0, keepdims=True)

    blk = pl.BlockSpec((1, S, CONV_COLS), lambda c, b: (b, 0, c))
    return pl.pallas_call(
        body, name="conv_bwd", grid=(ncb, B),
        in_specs=[pl.BlockSpec((1, S, CONV_COLS), lambda c, b: (b, 0, C_QK // CONV_COLS + c)),
                  pl.BlockSpec((8, CONV_COLS), lambda c, b: (0, c)), blk, blk],
        out_specs=[blk, pl.BlockSpec((8, CONV_COLS), lambda c, b: (0, c))],
        out_shape=[jax.ShapeDtypeStruct((B, S, 2 * MLSTM_WIDTH), BF16), jax.ShapeDtypeStruct((8, 2 * MLSTM_WIDTH), F32)],
        compiler_params=_cparams(("parallel", "arbitrary")),
    )(proj3, w8, dout_f, dout_b)


MLSTM_HEADS_PER_STEP = 4


def _chunk_masks(direction):
    t = lax.broadcasted_iota(jnp.int32, (MLSTM_CHUNK, MLSTM_CHUNK), 0)
    s = lax.broadcasted_iota(jnp.int32, (MLSTM_CHUNK, MLSTM_CHUNK), 1)
    le, ge = (s <= t).astype(F32), (s >= t).astype(F32)
    eye = (s == t).astype(F32)
    return (le, ge, eye) if direction == 0 else (ge, le, eye)


def _gate_cols(gates, direction, head):
    lane = lax.broadcasted_iota(jnp.int32, gates.shape, 1)
    sel_i = (lane == (2 * direction) * MLSTM_HEADS + head).astype(F32)
    sel_f = (lane == (2 * direction + 1) * MLSTM_HEADS + head).astype(F32)
    return sel_i, sel_f


def _mlstm_fwd(qk, proj3, bias):
    B, S, _ = qk.shape
    nc = S // MLSTM_CHUNK
    H, L, DH = MLSTM_HEADS, MLSTM_CHUNK, MLSTM_HEAD_DIM

    def chunk_of(d, c):
        return c if d == 0 else nc - 1 - c

    HS = MLSTM_HEADS_PER_STEP

    def body(qkf, qkb, vf, vb, gf, gb, bias_ref, hf, hb, csf, csb, nsf, nsb, msf, msb, c_st, n_st, m_st):
        c, hg = pl.program_id(1), pl.program_id(2)

        @pl.when(c == 0)
        def _():
            for d in range(2):
                for j in range(HS):
                    c_st[d, hg * HS + j] = jnp.zeros((DH, DH), F32)
                    n_st[d, hg * HS + j] = jnp.zeros((1, DH), F32)
                    m_st[d, hg * HS + j] = jnp.zeros((1, DH), F32)

        for d, (qk_ref, v_ref, g_ref, h_ref, cs, ns, ms) in enumerate(
                ((qkf, vf, gf, hf, csf, nsf, msf), (qkb, vb, gb, hb, csb, nsb, msb))):
            incl, incl_t, eye = _chunk_masks(d)
            gates = g_ref[0] + bias_ref[...]
            for j in range(HS):
                h = hg * HS + j
                sel_i, sel_f = _gate_cols(gates, d, h)
                li = jnp.sum(gates * sel_i, axis=1, keepdims=True)
                lf_pre = jnp.sum(gates * sel_f, axis=1, keepdims=True)
                c_in, n_in, m_in = c_st[d, h], n_st[d, h], m_st[d, h]
                cs[0, 0, j], ns[0, 0, j], ms[0, 0, j] = c_in, n_in, m_in
                hh, c_new, n_new, m_new = _mlstm_chunk(
                    qk_ref[0, :, 2 * j * DH:(2 * j + 1) * DH], qk_ref[0, :, (2 * j + 1) * DH:(2 * j + 2) * DH],
                    v_ref[0, :, j * DH:(j + 1) * DH], li, lf_pre, c_in, n_in,
                    jnp.max(m_in, axis=1, keepdims=True), incl, incl_t, eye)
                h_ref[0, :, j * DH:(j + 1) * DH] = hh
                c_st[d, h], n_st[d, h] = c_new, n_new
                m_st[d, h] = jnp.broadcast_to(m_new, (1, DH))

    def tok_spec(width, base, d, per_head):
        return pl.BlockSpec((1, L, width), lambda b, c, h: (b, chunk_of(d, c), base + (h if per_head else 0)))

    def st_spec(shape, d):
        return pl.BlockSpec((1, 1, HS) + shape, lambda b, c, h: (b, chunk_of(d, c), h, 0, 0))

    in_specs = [tok_spec(2 * HS * DH, 0, 0, True), tok_spec(2 * HS * DH, 0, 1, True),
                tok_spec(HS * DH, C_VM // (HS * DH), 0, True), tok_spec(HS * DH, C_VM // (HS * DH), 1, True),
                tok_spec(LANES, C_GATES // LANES, 0, False), tok_spec(LANES, C_GATES // LANES, 1, False),
                pl.BlockSpec((1, LANES), lambda b, c, h: (0, 0))]
    out_specs = [tok_spec(HS * DH, 0, 0, True), tok_spec(HS * DH, 0, 1, True),
                 st_spec((DH, DH), 0), st_spec((DH, DH), 1), st_spec((1, DH), 0), st_spec((1, DH), 1),
                 st_spec((1, DH), 0), st_spec((1, DH), 1)]
    hs = jax.ShapeDtypeStruct((B, S, H * DH), F32)
    cs = jax.ShapeDtypeStruct((B, nc, H, DH, DH), F32)
    vs = jax.ShapeDtypeStruct((B, nc, H, 1, DH), F32)
    return pl.pallas_call(
        body, name="mlstm_fwd", grid=(B, nc, H // HS), in_specs=in_specs, out_specs=out_specs,
        out_shape=[hs, hs, cs, cs, vs, vs, vs, vs],
        scratch_shapes=[pltpu.VMEM((2, H, DH, DH), F32), pltpu.VMEM((2, H, 1, DH), F32), pltpu.VMEM((2, H, 1, DH), F32)],
        compiler_params=_cparams(("parallel", "arbitrary", "arbitrary")),
    )(qk, qk, proj3, proj3, proj3, proj3, bias)


def _mlstm_bwd(qk, proj3, bias, states, dh):
    B, S, _ = qk.shape
    nc = S // MLSTM_CHUNK
    H, L, DH = MLSTM_HEADS, MLSTM_CHUNK, MLSTM_HEAD_DIM

    def chunk_of(d, c):
        return nc - 1 - c if d == 0 else c

    HS = MLSTM_HEADS_PER_STEP

    def body(qkf, qkb, vf, vb, gf, gb, bias_ref, csf, csb, nsf, nsb, msf, msb, dhf, dhb,
             dqkf, dqkb, dvf, dvb, dgf, dgb, dc_st, dn_st, dm_st):
        c, hg = pl.program_id(1), pl.program_id(2)

        @pl.when(c == 0)
        def _():
            for d in range(2):
                for j in range(HS):
                    dc_st[d, hg * HS + j] = jnp.zeros((DH, DH), F32)
                    dn_st[d, hg * HS + j] = jnp.zeros((1, DH), F32)
                    dm_st[d, hg * HS + j] = jnp.zeros((1, DH), F32)

        @pl.when(hg == 0)
        def _():
            dgf[...] = jnp.zeros_like(dgf)
            dgb[...] = jnp.zeros_like(dgb)

        for d, (qk_ref, v_ref, g_ref, cs, ns, ms, dh_ref, dqk_ref, dv_ref, dg_ref) in enumerate(
                ((qkf, vf, gf, csf, nsf, msf, dhf, dqkf, dvf, dgf), (qkb, vb, gb, csb, nsb, msb, dhb, dqkb, dvb, dgb))):
            incl, incl_t, eye = _chunk_masks(d)
            gates = g_ref[0] + bias_ref[...]
            dgates = jnp.zeros_like(gates)
            for j in range(HS):
                h = hg * HS + j
                sel_i, sel_f = _gate_cols(gates, d, h)
                li = jnp.sum(gates * sel_i, axis=1, keepdims=True)
                lf_pre = jnp.sum(gates * sel_f, axis=1, keepdims=True)
                m_in = jnp.max(ms[0, 0, j], axis=1, keepdims=True)
                _, vjp = jax.vjp(
                    functools.partial(_mlstm_chunk, incl=incl, incl_t=incl_t, eye=eye),
                    qk_ref[0, :, 2 * j * DH:(2 * j + 1) * DH], qk_ref[0, :, (2 * j + 1) * DH:(2 * j + 2) * DH],
                    v_ref[0, :, j * DH:(j + 1) * DH], li, lf_pre, cs[0, 0, j], ns[0, 0, j], m_in)
                dm_out = jnp.max(dm_st[d, h], axis=1, keepdims=True)
                dq, dk, dv, dli, dlf, dc, dn, dm = vjp((dh_ref[0, :, j * DH:(j + 1) * DH], dc_st[d, h], dn_st[d, h], dm_out))
                dqk_ref[0, :, 2 * j * DH:(2 * j + 1) * DH] = dq
                dqk_ref[0, :, (2 * j + 1) * DH:(2 * j + 2) * DH] = dk
                dv_ref[0, :, j * DH:(j + 1) * DH] = dv
                dgates += dli * sel_i + dlf * sel_f
                dc_st[d, h], dn_st[d, h] = dc, dn
                dm_st[d, h] = jnp.broadcast_to(dm, (1, DH))
            dg_ref[0] += dgates

    def tok_spec(width, base, d, per_head):
        return pl.BlockSpec((1, L, width), lambda b, c, h: (b, chunk_of(d, c), base + (h if per_head else 0)))

    def st_spec(shape, d):
        return pl.BlockSpec((1, 1, HS) + shape, lambda b, c, h: (b, chunk_of(d, c), h, 0, 0))

    in_specs = [tok_spec(2 * HS * DH, 0, 0, True), tok_spec(2 * HS * DH, 0, 1, True),
                tok_spec(HS * DH, C_VM // (HS * DH), 0, True), tok_spec(HS * DH, C_VM // (HS * DH), 1, True),
                tok_spec(LANES, C_GATES // LANES, 0, False), tok_spec(LANES, C_GATES // LANES, 1, False),
                pl.BlockSpec((1, LANES), lambda b, c, h: (0, 0)),
                st_spec((DH, DH), 0), st_spec((DH, DH), 1), st_spec((1, DH), 0), st_spec((1, DH), 1),
                st_spec((1, DH), 0), st_spec((1, DH), 1), tok_spec(HS * DH, 0, 0, True), tok_spec(HS * DH, 0, 1, True)]
    out_specs = [tok_spec(2 * HS * DH, 0, 0, True), tok_spec(2 * HS * DH, 0, 1, True),
                 tok_spec(HS * DH, 0, 0, True), tok_spec(HS * DH, 0, 1, True),
                 tok_spec(LANES, 0, 0, False), tok_spec(LANES, 0, 1, False)]
    qks = jax.ShapeDtypeStruct((B, S, 2 * H * DH), F32)
    vs = jax.ShapeDtypeStruct((B, S, H * DH), F32)
    gs = jax.ShapeDtypeStruct((B, S, LANES), F32)
    csf, csb, nsf, nsb, msf, msb = states
    return pl.pallas_call(
        body, name="mlstm_bwd", grid=(B, nc, H // HS), in_specs=in_specs, out_specs=out_specs,
        out_shape=[qks, qks, vs, vs, gs, gs],
        scratch_shapes=[pltpu.VMEM((2, H, DH, DH), F32), pltpu.VMEM((2, H, 1, DH), F32), pltpu.VMEM((2, H, 1, DH), F32)],
        compiler_params=_cparams(("parallel", "arbitrary", "arbitrary")),
    )(qk, qk, proj3, proj3, proj3, proj3, bias, csf, csb, nsf, nsb, msf, msb, dh, dh)


ROW_BLOCK = 256
FF_COLS = 512
FF_SHARD = D_FF // N_DEV
FF_SHARD_PAD = 384
FF_PAD = N_DEV * FF_SHARD_PAD


def _rms_fwd(name, x, g):
    T = x.shape[0]
    return _rowwise(name, lambda xv, gv: _rms(xv, gv), [_In(x), _In(g, rows=False)], [_Out(D_MODEL, BF16)], T, ROW_BLOCK)[0]


def _rms_bwd(name, x, g, dh, dres):
    T = x.shape[0]

    def fn(xv, gv, dhv, drv):
        _, vjp = jax.vjp(_rms, xv, gv)
        dx, dg = vjp(dhv)
        return drv + dx, dg

    return _rowwise(name, fn, [_In(x), _In(g, rows=False), _In(dh), _In(dres)],
                    [_Out(D_MODEL), _Out(D_MODEL, rows=False)], T, ROW_BLOCK)


def _mmw(name, a, w, mode, **kw):
    if isinstance(w, tuple):
        return _matmul(name, a, w[0], mode, bl=w[1], **kw)
    return _matmul(name, a, w, mode, **kw)


def _swiglu(gate, up):
    return _silu(gate) * up


def _ffn_in(name, h, wg, wu):
    (M, K), N = h.shape, wg.shape[1]
    tm, tn = _first_divisor(M, (1024, 512, 256, 128)), FF_COLS

    def body(h_ref, wg_ref, wu_ref, g_ref, u_ref, a_ref):
        hv = h_ref[...]
        gate = _mm(hv, wg_ref[...], _NN)
        up = _mm(hv, wu_ref[...], _NN)
        g_ref[...], u_ref[...] = gate, up
        a_ref[...] = _swiglu(gate, up).astype(BF16)

    w_spec = pl.BlockSpec((K, tn), lambda i, j: (0, j))
    o_spec = pl.BlockSpec((tm, tn), lambda i, j: (i, j))
    return pl.pallas_call(
        body, name=name, grid=(M // tm, N // tn), in_specs=[pl.BlockSpec((tm, K), lambda i, j: (i, 0)), w_spec, w_spec],
        out_specs=[o_spec, o_spec, o_spec],
        out_shape=[jax.ShapeDtypeStruct((M, N), F32), jax.ShapeDtypeStruct((M, N), F32), jax.ShapeDtypeStruct((M, N), BF16)],
        compiler_params=_cparams(("parallel", "parallel")),
    )(h, wg, wu)


def _ffn_dact(name, dx, wd, gate, up):
    (M, K), N = dx.shape, wd.shape[0]
    tm, tn = _first_divisor(M, (1024, 512, 256, 128)), FF_COLS

    def body(dx_ref, wd_ref, g_ref, u_ref, dg_ref, du_ref):
        dact = _mm(dx_ref[...], wd_ref[...], _NT) * 0.5
        _, vjp = jax.vjp(_swiglu, g_ref[...], u_ref[...])
        dgate, dup = vjp(dact)
        dg_ref[...], du_ref[...] = dgate.astype(BF16), dup.astype(BF16)

    o_spec = pl.BlockSpec((tm, tn), lambda i, j: (i, j))
    return pl.pallas_call(
        body, name=name, grid=(M // tm, N // tn),
        in_specs=[pl.BlockSpec((tm, K), lambda i, j: (i, 0)), pl.BlockSpec((tn, K), lambda i, j: (j, 0)), o_spec, o_spec],
        out_specs=[o_spec, o_spec],
        out_shape=[jax.ShapeDtypeStruct((M, N), BF16), jax.ShapeDtypeStruct((M, N), BF16)],
        compiler_params=_cparams(("parallel", "parallel")),
    )(dx, wd, gate, up)


def _ffn_fwd(tag, x, g, wg, wu, wd):
    h = _rms_fwd(tag + "_norm", x, g)
    gate, up, act = _ffn_in(tag + "_in", h, wg, wu)
    if callable(wd):
        wd = wd(act)
    out = _mmw(tag + "_down", act, wd, "nn", res=x, scale=0.5)
    return out, (x, h, gate, up, act), wd


def _ffn_bwd(tag, saved, g, wg, wu, wd, dx, on_dw):
    x, h, gate, up, act = saved
    dgate, dup = _ffn_dact(tag + "_dact", dx, wd, gate, up)
    dwd = _matmul(tag + "_dwd", act, dx, "tn", scale=0.5)
    dwg = _matmul(tag + "_dwg", h, dgate, "tn")
    dwu = _matmul(tag + "_dwu", h, dup, "tn")
    token = on_dw({tag + "_w_gate": dwg, tag + "_w_up": dwu, tag + "_w_down": dwd}, dwu)
    dh = _mmw(tag + "_dh1", dgate, wg, "nt", dep=token)
    dh = _mmw(tag + "_dh2", dup, wu, "nt", res=dh)
    dx_new, dg = _rms_bwd(tag + "_dnorm", x, g, dh, dx)
    return dx_new, dg


def _rope_cos_sin(positions):
    half = ROPE_DIM // 2
    inv_freq = jnp.power(jnp.float32(ROPE_THETA), -jnp.arange(half, dtype=F32) * (2.0 / ROPE_DIM))
    head = jnp.zeros((ATT_HEAD_DIM,), F32).at[:ROPE_DIM].set(jnp.concatenate([inv_freq, inv_freq]))
    row = jnp.tile(head, LANES // ATT_HEAD_DIM)[None, :]
    T = positions.shape[0]
    return _rowwise("rope_tables", _rope_tables, [_In(positions), _In(row, rows=False)], [_Out(LANES), _Out(LANES)], T, 1024)


def _prep_fwd(name, src, width, base, g, cos, sin):
    return _rowwise(name, _qk_prep, [_In(src, width, base), _In(g, rows=False), _In(cos), _In(sin)],
                    [_Out(width)], src.shape[0], 512)[0]


def _prep_bwd(name, src, width, base, g, cos, sin, dout):
    def fn(tv, gv, cv, sv, dv):
        _, vjp = jax.vjp(lambda a, b: _qk_prep(a, b, cv, sv), tv, gv)
        return vjp(dv)

    return _rowwise(name, fn, [_In(src, width, base), _In(g, rows=False), _In(cos), _In(sin), _In(dout)],
                    [_Out(width, BF16), _Out(width, rows=False)], src.shape[0], 512)


def _to_heads(t, B, S, nh):
    return t.reshape(B, S, nh, ATT_HEAD_DIM).transpose(0, 2, 1, 3)


def _from_heads(t):
    B, nh, S, _ = t.shape
    return t.transpose(0, 2, 1, 3).reshape(B * S, nh * ATT_HEAD_DIM)


def _mix_fwd(x, cos, sin, B, S, p):
    T = B * S
    h = _rms_fwd("mix_norm", x, p["mix_norm"])
    proj = _matmul("mix_proj", h, p["w_in"], "nn")
    proj3 = proj.reshape(B, S, IN_PAD)
    q_gain = jnp.tile(p["attn_q_norm"], (1, ATT_HEADS))
    k_gain = jnp.tile(p["attn_k_norm"], (1, ATT_KV_HEADS))
    q_r = _prep_fwd("q_prep", proj, ATT_WIDTH, C_QA // ATT_WIDTH, q_gain, cos, sin)
    k_r = _prep_fwd("k_prep", proj, ATT_KV_WIDTH, C_KA // ATT_KV_WIDTH, k_gain, cos, sin)
    qh = _to_heads(q_r, B, S, ATT_HEADS)
    kh = _to_heads(k_r, B, S, ATT_KV_HEADS)
    vh = _to_heads(proj[:, C_VA:C_VA + ATT_KV_WIDTH], B, S, ATT_KV_HEADS)
    sink = p["attn_sink"].reshape(ATT_KV_HEADS, ATT_GROUP, 1, 1)
    y_a = _from_heads(_attn_fwd(qh, kh, vh, sink))

    qk_c = _conv_fwd(proj3, p["conv_w8"])
    hf, hb, *states = _mlstm_fwd(qk_c, proj3, p["gate_bias"])
    hf2, hb2 = hf.reshape(T, MLSTM_WIDTH), hb.reshape(T, MLSTM_WIDTH)
    DH = MLSTM_HEAD_DIM
    y_m = _rowwise("mlstm_out", _mlstm_combine,
                   [_In(hf2, DH, split=True), _In(hb2, DH, split=True), _In(proj, DH, C_OM // DH, split=True),
                    _In(p["mlstm_out_norm"], DH, split=True, rows=False)],
                   [_Out(MLSTM_WIDTH, BF16, DH, split=True)], T, 1024, ncol=MLSTM_HEADS)[0]

    za = _mmw("branch_a", y_a, p["w_branch_attn"], "nn")
    zm = _mmw("branch_m", y_m, p["w_branch_mlstm"], "nn")
    W = 512
    merged = _rowwise("merge", _merge,
                      [_In(proj, W, C_GMERGE // W, split=True), _In(proj, W, (C_GMERGE + D_MODEL) // W, split=True),
                       _In(za, W, split=True), _In(zm, W, split=True)],
                      [_Out(D_MODEL, BF16, W, split=True)], T, 512, ncol=D_MODEL // W)[0]
    out = _mmw("mix_out", merged, p["w_out"], "nn", res=x)
    saved = dict(x=x, h=h, proj=proj, q_gain=q_gain, k_gain=k_gain, qh=qh, kh=kh, vh=vh, sink=sink, y_a=y_a, qk_c=qk_c,
                 hf=hf2, hb=hb2, states=states, y_m=y_m, za=za, zm=zm, merged=merged)
    return out, saved


def _mix_bwd(sv, cos, sin, B, S, p, dx, on_dw):
    T = B * S
    DH = MLSTM_HEAD_DIM
    proj = sv["proj"]
    proj3 = proj.reshape(B, S, IN_PAD)
    g = {}
    dmerged = _mmw("mix_dmerged", dx, p["w_out"], "nt")
    g["w_out"] = _matmul("mix_dwout", sv["merged"], dx, "tn")
    W = 512

    def merge_bwd(ga, gm, za, zm, dm):
        _, vjp = jax.vjp(_merge, ga, gm, za, zm)
        return vjp(dm)

    dga, dgm, dza, dzm = _rowwise(
        "merge_bwd", merge_bwd,
        [_In(proj, W, C_GMERGE // W, split=True), _In(proj, W, (C_GMERGE + D_MODEL) // W, split=True),
         _In(sv["za"], W, split=True), _In(sv["zm"], W, split=True), _In(dmerged, W, split=True)],
        [_Out(D_MODEL, BF16, W, split=True), _Out(D_MODEL, BF16, W, split=True),
         _Out(D_MODEL, BF16, W, split=True), _Out(D_MODEL, BF16, W, split=True)], T, 512, ncol=D_MODEL // W)
    dya = _mmw("branch_a_dx", dza, p["w_branch_attn"], "nt")
    g["w_branch_attn"] = _matmul("branch_a_dw", sv["y_a"], dza, "tn")
    dym = _mmw("branch_m_dx", dzm, p["w_branch_mlstm"], "nt")
    g["w_branch_mlstm"] = _matmul("branch_m_dw", sv["y_m"], dzm, "tn")

    def combine_bwd(hf, hb, o_pre, gn, dy):
        _, vjp = jax.vjp(_mlstm_combine, hf, hb, o_pre, gn)
        dhf, _, do, dg = vjp(dy)
        return dhf, do, dg

    dh, dom, g["mlstm_out_norm"] = _rowwise(
        "mlstm_out_bwd", combine_bwd,
        [_In(sv["hf"], DH, split=True), _In(sv["hb"], DH, split=True), _In(proj, DH, C_OM // DH, split=True),
         _In(p["mlstm_out_norm"], DH, split=True, rows=False), _In(dym, DH, split=True)],
        [_Out(MLSTM_WIDTH, F32, DH, split=True), _Out(MLSTM_WIDTH, BF16, DH, split=True),
         _Out(MLSTM_WIDTH, F32, DH, split=True, rows=False)], T, 1024, ncol=MLSTM_HEADS)
    dqk_f, dqk_b, dv_f, dv_b, dg_f, dg_b = _mlstm_bwd(sv["qk_c"], proj3, p["gate_bias"], sv["states"],
                                                       dh.reshape(B, S, MLSTM_WIDTH))
    dgates, dvm, g["gate_bias"] = _rowwise(
        "mlstm_dsum", lambda a, b, c, d: (a + b, c + d, jnp.sum(a + b, axis=0, keepdims=True)),
        [_In(dg_f.reshape(T, LANES)), _In(dg_b.reshape(T, LANES)), _In(dv_f.reshape(T, MLSTM_WIDTH)), _In(dv_b.reshape(T, MLSTM_WIDTH))],
        [_Out(LANES, BF16), _Out(MLSTM_WIDTH, BF16), _Out(LANES, rows=False)], T, 1024)
    dqk, g["conv_w8"] = _conv_bwd(proj3, p["conv_w8"], dqk_f, dqk_b)

    dyh = _to_heads(dya, B, S, ATT_HEADS)
    dqh, dkh, dvh, dsink = _attn_bwd(sv["qh"], sv["kh"], sv["vh"], sv["sink"], dyh)
    g["attn_sink"] = dsink.reshape(1, ATT_HEADS)
    dva = _from_heads(dvh)
    dqa, dq_gain = _prep_bwd("q_prep_bwd", proj, ATT_WIDTH, C_QA // ATT_WIDTH, sv["q_gain"], cos, sin, _from_heads(dqh))
    dka, dk_gain = _prep_bwd("k_prep_bwd", proj, ATT_KV_WIDTH, C_KA // ATT_KV_WIDTH, sv["k_gain"], cos, sin, _from_heads(dkh))
    g["attn_q_norm"] = jnp.sum(dq_gain.reshape(ATT_HEADS, ATT_HEAD_DIM), axis=0, keepdims=True)
    g["attn_k_norm"] = jnp.sum(dk_gain.reshape(ATT_KV_HEADS, ATT_HEAD_DIM), axis=0, keepdims=True)

    dproj = jnp.concatenate(
        [dga, dgm, dqk.reshape(T, 2 * MLSTM_WIDTH), dvm, dom, dqa, dka, dva.astype(BF16), dgates], axis=1)
    dwin = _matmul("mix_dwin", sv["h"], dproj, "tn")
    token = on_dw({"w_in": _w_in_to_slots(dwin), "w_branch_attn": g.pop("w_branch_attn"),
                   "w_branch_mlstm": g.pop("w_branch_mlstm"), "w_out": g.pop("w_out")}, dwin)
    dh2 = _matmul("mix_dh", dproj, p["w_in"], "nt", dep=token)
    dx_new, g["mix_norm"] = _rms_bwd("mix_dnorm", sv["x"], p["mix_norm"], dh2, dx)
    return dx_new, g


def _loss_and_grad(x, g, target):
    T = x.shape[0]

    def loss_fn(xv, gv, tv):
        err = jnp.square(_rms(xv, gv) - tv)
        return 0.5 * jnp.sum(jnp.mean(err, axis=-1, keepdims=True), axis=0, keepdims=True)

    def fn(xv, gv, tv):
        val, vjp = jax.vjp(lambda a, b: loss_fn(a, b, tv), xv, gv)
        dx, dg = vjp(jnp.ones((1, 1), F32))
        return val, dx, dg

    return _rowwise("loss_head", fn, [_In(x), _In(g, rows=False), _In(target)],
                    [_Out(1, rows=False), _Out(D_MODEL), _Out(D_MODEL, rows=False)], T, ROW_BLOCK)


def _block_norm_fwd(x, g):
    T = x.shape[0]
    return _rowwise("block_norm", _rms, [_In(x), _In(g, rows=False)], [_Out(D_MODEL)], T, ROW_BLOCK)[0]


def _block_norm_bwd(x, g, dy):
    T = x.shape[0]

    def fn(xv, gv, dv):
        _, vjp = jax.vjp(_rms, xv, gv)
        return vjp(dv)

    return _rowwise("block_norm_bwd", fn, [_In(x), _In(g, rows=False), _In(dy)],
                    [_Out(D_MODEL), _Out(D_MODEL, rows=False)], T, ROW_BLOCK)


def _qk_perm_cols(t, axis):
    q, k = jnp.split(t, 2, axis=axis)
    parts = []
    for h in range(MLSTM_HEADS):
        sl = [slice(None)] * t.ndim
        sl[axis] = slice(h * MLSTM_HEAD_DIM, (h + 1) * MLSTM_HEAD_DIM)
        parts += [q[tuple(sl)], k[tuple(sl)]]
    return jnp.concatenate(parts, axis=axis)


def _qk_unperm_cols(t, axis):
    qs, ks = [], []
    for h in range(MLSTM_HEADS):
        sl = [slice(None)] * t.ndim
        sl[axis] = slice(2 * h * MLSTM_HEAD_DIM, (2 * h + 1) * MLSTM_HEAD_DIM)
        qs.append(t[tuple(sl)])
        sl[axis] = slice((2 * h + 1) * MLSTM_HEAD_DIM, (2 * h + 2) * MLSTM_HEAD_DIM)
        ks.append(t[tuple(sl)])
    return jnp.concatenate(qs + ks, axis=axis)


def _w_in_arrange(w):
    qa, ka, va, qm, km, vm, om, gm, gmerge = jnp.split(w, np.cumsum(
        (ATT_WIDTH, ATT_KV_WIDTH, ATT_KV_WIDTH, MLSTM_WIDTH, MLSTM_WIDTH, MLSTM_WIDTH, MLSTM_WIDTH, MLSTM_N_GATES))[:].tolist(), axis=1)
    qk = _qk_perm_cols(jnp.concatenate([qm, km], axis=1), 1)
    pad = jnp.zeros((w.shape[0], LANES - MLSTM_N_GATES), w.dtype)
    return jnp.concatenate([gmerge, qk, vm, om, qa, ka, va, gm, pad], axis=1)


def _w_in_restore(w):
    gmerge = w[:, C_GMERGE:C_GMERGE + 2 * D_MODEL]
    qk = _qk_unperm_cols(w[:, C_QK:C_QK + 2 * MLSTM_WIDTH], 1)
    vm, om = w[:, C_VM:C_VM + MLSTM_WIDTH], w[:, C_OM:C_OM + MLSTM_WIDTH]
    qa, ka, va = w[:, C_QA:C_QA + ATT_WIDTH], w[:, C_KA:C_KA + ATT_KV_WIDTH], w[:, C_VA:C_VA + ATT_KV_WIDTH]
    gm = w[:, C_GATES:C_GATES + MLSTM_N_GATES]
    return jnp.concatenate([qa, ka, va, qk, vm, om, gm, gmerge], axis=1)


BIG = ("ffn1_w_gate", "ffn1_w_up", "ffn1_w_down", "w_in", "mlstm_conv_w", "w_branch_attn", "w_branch_mlstm", "w_out",
       "ffn2_w_gate", "ffn2_w_up", "ffn2_w_down")
MATMUL_W = tuple(n for n in BIG if n != "mlstm_conv_w")
SMALL = ("ffn1_norm", "mix_norm", "mlstm_gate_bias", "attn_q_norm", "attn_k_norm", "attn_sink", "mlstm_conv_b",
         "mlstm_out_norm", "ffn2_norm", "block_out_norm")
WEIGHTS = ("ffn1_norm", "ffn1_w_gate", "ffn1_w_up", "ffn1_w_down", "mix_norm", "w_in", "mlstm_gate_bias", "attn_q_norm",
           "attn_k_norm", "attn_sink", "mlstm_conv_w", "mlstm_conv_b", "mlstm_out_norm", "w_branch_attn", "w_branch_mlstm",
           "w_out", "ffn2_norm", "ffn2_w_gate", "ffn2_w_up", "ffn2_w_down", "block_out_norm")
PACK_COLS = 1024


def _padded_rows(n_elems):
    return -(-n_elems // PACK_COLS)


def _pack_flat(arrs, dtype, row_multiple):
    parts = []
    for a in arrs:
        flat = a.reshape(-1).astype(dtype)
        pad = _padded_rows(flat.shape[0]) * PACK_COLS - flat.shape[0]
        parts.append(jnp.pad(flat, (0, pad)) if pad else flat)
    flat = jnp.concatenate(parts)
    rows = flat.shape[0] // PACK_COLS
    extra = (-rows) % row_multiple
    if extra:
        flat = jnp.pad(flat, (0, extra * PACK_COLS))
    return flat.reshape(-1, PACK_COLS)


def _unpack_flat(buf, shapes, lead=()):
    flat = buf.reshape(lead + (-1,))
    out, off = [], 0
    for s in shapes:
        n = int(np.prod(s))
        out.append(flat[..., off:off + n].reshape(lead + tuple(s)))
        off += _padded_rows(n) * PACK_COLS
    return out


class _Lay:
    def __init__(self, shard, axis, width):
        self.shard, self.axis, self.width = shard, axis, width
        self.padded = tuple(width if a == axis else s for a, s in enumerate(shard))
        self.whole = tuple(N_DEV * width if a == axis else s for a, s in enumerate(shard))

    def pad(self, t, lead=0):
        extra = self.width - self.shard[self.axis]
        if not extra:
            return t
        cfg = [(0, 0)] * t.ndim
        cfg[lead + self.axis] = (0, extra)
        return jnp.pad(t, cfg)

    def unpad(self, t, lead=0):
        idx = [slice(None)] * t.ndim
        idx[lead + self.axis] = slice(0, self.shard[self.axis])
        return t[tuple(idx)]


_FF_COL = _Lay((D_MODEL, FF_SHARD), 1, FF_SHARD_PAD)
_FF_ROW = _Lay((FF_SHARD, D_MODEL), 0, FF_SHARD_PAD)
LAYOUTS = {
    "ffn1_w_gate": _FF_COL, "ffn1_w_up": _FF_COL, "ffn1_w_down": _FF_ROW,
    "ffn2_w_gate": _FF_COL, "ffn2_w_up": _FF_COL, "ffn2_w_down": _FF_ROW,
    "w_in": _Lay((D_MODEL, IN_WIDTH // N_DEV), 0, D_MODEL),
    "mlstm_conv_w": _Lay((3, 2 * MLSTM_WIDTH // N_DEV), 1, 2 * MLSTM_WIDTH // N_DEV),
    "w_branch_attn": _Lay((ATT_WIDTH, D_MODEL // N_DEV), 1, D_MODEL // N_DEV),
    "w_branch_mlstm": _Lay((MLSTM_WIDTH, D_MODEL // N_DEV), 1, D_MODEL // N_DEV),
    "w_out": _Lay((D_MODEL // N_DEV, D_MODEL), 0, D_MODEL // N_DEV),
}


def _window(ref, axis, j, width):
    idx = [slice(None)] * len(ref.shape)
    idx[axis] = pl.ds(pl.multiple_of(j * width, width), width)
    return ref.at[tuple(idx)]


ANY = pl.BlockSpec(memory_space=pl.ANY)


def _mesh_pos():
    return lax.axis_index("x"), lax.axis_index("y"), lax.axis_index("c")


def _all_gather(name, shard, vmem=False):
    R, C = shard.shape
    space = pl.BlockSpec(memory_space=pltpu.VMEM) if vmem else ANY

    def body(x_ref, out_ref, send_sems, recv_sems, local_sem):
        x, y, c = _mesh_pos()
        me, sibling = (x, y, c), (x, y, 1 - c)
        chips = [(1 - x, y), (x, 1 - y), (1 - x, 1 - y)]

        def slot(px, py, pc):
            return out_ref.at[4 * px + 2 * py + pc]

        def copy(k, block, to, src=None):
            return pltpu.make_async_remote_copy(
                src_ref=slot(*block) if src is None else src, dst_ref=slot(*block),
                send_sem=send_sems.at[k], recv_sem=recv_sems.at[k], device_id=to, device_id_type=MESH)

        mine = pltpu.make_async_copy(x_ref, slot(*me), local_sem)
        mine.start()
        first = [copy(0, me, sibling, src=x_ref)]
        first += [copy(1 + j, me, (*chip, c), src=x_ref) for j, chip in enumerate(chips)]
        for cp in first:
            cp.start()
        passed = [copy(4 + j, (*chip, c), sibling) for j, chip in enumerate(chips)]
        for j, chip in enumerate(chips):
            copy(1 + j, (*chip, c), me).wait_recv()
            passed[j].start()
        copy(0, sibling, me).wait_recv()
        for j, chip in enumerate(chips):
            copy(4 + j, (*chip, 1 - c), me).wait_recv()
        for cp in first + passed:
            cp.wait_send()
        mine.wait()

    return pl.pallas_call(
        body, name=name, out_shape=jax.ShapeDtypeStruct((N_DEV, R, C), shard.dtype),
        in_specs=[space], out_specs=space,
        scratch_shapes=[pltpu.SemaphoreType.DMA((7,)), pltpu.SemaphoreType.DMA((7,)), pltpu.SemaphoreType.DMA],
    )(shard)


HBM = pl.BlockSpec(memory_space=pltpu.HBM)
SEM = pl.BlockSpec(memory_space=pltpu.SEMAPHORE)
SPLIT_COPY = pltpu.CompilerParams(has_side_effects=pltpu.SideEffectType.DATAFLOW_SIDE_EFFECTING)
N_PEERS = N_DEV - 1


def _peers(x, y, c):
    return [(x, y, 1 - c), (1 - x, y, c), (x, 1 - y, c), (1 - x, 1 - y, c),
            (1 - x, y, 1 - c), (x, 1 - y, 1 - c), (1 - x, 1 - y, 1 - c)]


def _dev_index(pos):
    return 4 * pos[0] + 2 * pos[1] + pos[2]


def _place_own(name, shards, lays):
    nt = len(shards)
    me = _dev_index(_mesh_pos())

    def body(me_ref, *refs):
        for x_ref, o_ref in zip(refs[:nt], refs[nt:]):
            o_ref[...] = x_ref[...]

    def window_spec(lay):
        if lay.axis == 0:
            return pl.BlockSpec(lay.padded, lambda i, me_ref: (me_ref[0], 0))
        return pl.BlockSpec(lay.padded, lambda i, me_ref: (0, me_ref[0]))

    return pl.pallas_call(
        body, name=name,
        grid_spec=pltpu.PrefetchScalarGridSpec(
            num_scalar_prefetch=1, grid=(1,),
            in_specs=[pl.BlockSpec(lay.padded, lambda i, me_ref: (0, 0)) for lay in lays],
            out_specs=[window_spec(lay) for lay in lays]),
        out_shape=[jax.ShapeDtypeStruct(lay.whole, s.dtype) for s, lay in zip(shards, lays)],
        compiler_params=_cparams(("arbitrary",)),
    )(me.reshape(1).astype(jnp.int32), *shards)


def _gather_start(name, shards, lands, lays, groups, after):
    nt, ng = len(shards), len(groups)

    def body(*refs):
        x_refs, land_refs = refs[:nt], refs[nt:2 * nt]
        sems = refs[2 * nt + 1:2 * nt + 1 + 2 * ng]
        pos = _mesh_pos()
        me = _dev_index(pos)
        for g, tens in enumerate(groups):
            for i, t in enumerate(tens):
                for k, peer in enumerate(_peers(*pos)):
                    pltpu.make_async_remote_copy(
                        src_ref=x_refs[t], dst_ref=_window(land_refs[t], lays[t].axis, me, lays[t].width),
                        send_sem=sems[2 * g].at[N_PEERS * i + k], recv_sem=sems[2 * g + 1].at[N_PEERS * i + k],
                        device_id=peer, device_id_type=MESH).start()

    sem_shapes = []
    for tens in groups:
        sem_shapes += [pltpu.SemaphoreType.DMA((N_PEERS * len(tens),))] * 2
    thru = [pltpu.HBM(s.shape, s.dtype) for s in shards] + [pltpu.HBM(lay.whole, s.dtype) for s, lay in zip(shards, lays)]
    args = [pltpu.with_memory_space_constraint(s, pltpu.HBM) for s in shards]
    args += [pltpu.with_memory_space_constraint(ld, pltpu.HBM) for ld in lands]
    res = pl.pallas_call(
        body, name=name, out_shape=tuple(sem_shapes + thru), in_specs=[HBM] * (2 * nt) + [ANY],
        out_specs=tuple([SEM] * (2 * ng) + [HBM] * (2 * nt)),
        input_output_aliases={t: 2 * ng + t for t in range(2 * nt)}, compiler_params=SPLIT_COPY,
    )(*args, after)
    sems = [(res[2 * g], res[2 * g + 1]) for g in range(ng)]
    return sems, list(res[2 * ng:2 * ng + nt]), list(res[2 * ng + nt:])


def _gather_wait(name, sems, shards, lands, lays, after):
    nt = len(shards)
    send_sems, recv_sems = sems

    def body(*refs):
        x_refs, land_refs = refs[:nt], refs[nt:2 * nt]
        send_ref, recv_ref = refs[2 * nt], refs[2 * nt + 1]
        pos = _mesh_pos()
        for t in range(nt):
            for k, peer in enumerate(_peers(*pos)):
                cp = pltpu.make_async_remote_copy(
                    src_ref=x_refs[t], dst_ref=_window(land_refs[t], lays[t].axis, _dev_index(peer), lays[t].width),
                    send_sem=send_ref.at[N_PEERS * t + k], recv_sem=recv_ref.at[N_PEERS * t + k],
                    device_id=peer, device_id_type=MESH)
                cp.wait_send()
                cp.wait_recv()

    thru = [pltpu.HBM(s.shape, s.dtype) for s in shards] + [pltpu.HBM(ld.shape, ld.dtype) for ld in lands]
    res = pl.pallas_call(
        body, name=name, out_shape=tuple(thru), in_specs=[HBM] * (2 * nt) + [SEM, SEM, ANY],
        out_specs=tuple([HBM] * (2 * nt)), input_output_aliases={t: t for t in range(2 * nt)},
        compiler_params=SPLIT_COPY,
    )(*shards, *lands, send_sems, recv_sems, after)
    return list(res[nt:])


def _pair_exchange(name, grads, lays):
    nt = len(grads)

    def body(*refs):
        g_refs, land_refs = refs[:nt], refs[nt:2 * nt]
        send_sems, recv_sems = refs[2 * nt:]
        x, y, c = _mesh_pos()
        copies = []
        for t in range(nt):
            for chip in range(4):
                copies.append(pltpu.make_async_remote_copy(
                    src_ref=_window(g_refs[t], lays[t].axis, 2 * chip + (1 - c), lays[t].width), dst_ref=land_refs[t].at[chip],
                    send_sem=send_sems.at[4 * t + chip], recv_sem=recv_sems.at[4 * t + chip],
                    device_id=(x, y, 1 - c), device_id_type=MESH))
        for cp in copies:
            cp.start()
        for cp in copies:
            cp.wait_recv()
        for cp in copies:
            cp.wait_send()

    out_shape = [jax.ShapeDtypeStruct((4,) + lay.padded, g.dtype) for g, lay in zip(grads, lays)]
    return pl.pallas_call(
        body, name=name, out_shape=out_shape, in_specs=[ANY] * nt, out_specs=[ANY] * nt,
        scratch_shapes=[pltpu.SemaphoreType.DMA((4 * nt,)), pltpu.SemaphoreType.DMA((4 * nt,))],
    )(*grads)


def _pair_sum(name, whole, landed, lay, out_dtype):
    R, C = lay.padded
    br = _first_divisor(R, (512, 384, 256, 128, 64, 32, 16, 8))
    nb = R // br
    if lay.axis == 0:
        mine_spec = pl.BlockSpec((br, C), lambda k, i, c_ref: ((2 * k + c_ref[0]) * nb + i, 0))
    else:
        mine_spec = pl.BlockSpec((br, C), lambda k, i, c_ref: (i, 2 * k + c_ref[0]))

    def body(c_ref, mine_ref, sib_ref, o_ref):
        o_ref[0] = (mine_ref[...] + sib_ref[0]).astype(out_dtype)

    c = lax.axis_index("c")
    return pl.pallas_call(
        body, name=name,
        grid_spec=pltpu.PrefetchScalarGridSpec(
            num_scalar_prefetch=1, grid=(4, nb),
            in_specs=[mine_spec, pl.BlockSpec((1, br, C), lambda k, i, c_ref: (k, i, 0))],
            out_specs=pl.BlockSpec((1, br, C), lambda k, i, c_ref: (k, i, 0))),
        out_shape=jax.ShapeDtypeStruct((4, R, C), out_dtype),
        compiler_params=_cparams(("parallel", "parallel")),
    )(c.reshape(1).astype(jnp.int32), whole, landed)


def _chip_exchange(name, sums):
    nt = len(sums)

    def body(*refs):
        s_refs, land_refs = refs[:nt], refs[nt:2 * nt]
        send_sems, recv_sems, local_sems = refs[2 * nt:]
        x, y, c = _mesh_pos()
        my_chip = 2 * x + y
        mine = [pltpu.make_async_copy(s_refs[t].at[my_chip], land_refs[t].at[my_chip], local_sems.at[t]) for t in range(nt)]
        for cp in mine:
            cp.start()
        chips = [(1 - x, y), (x, 1 - y), (1 - x, 1 - y)]
        copies = []
        for t in range(nt):
            for j, (px, py) in enumerate(chips):
                copies.append(pltpu.make_async_remote_copy(
                    src_ref=s_refs[t].at[2 * px + py], dst_ref=land_refs[t].at[my_chip],
                    send_sem=send_sems.at[3 * t + j], recv_sem=recv_sems.at[3 * t + j],
                    device_id=(px, py, c), device_id_type=MESH))
        for cp in copies:
            cp.start()
        for t in range(nt):
            for j, (px, py) in enumerate(chips):
                pltpu.make_async_remote_copy(
                    src_ref=s_refs[t].at[my_chip], dst_ref=land_refs[t].at[2 * px + py],
                    send_sem=send_sems.at[3 * t + j], recv_sem=recv_sems.at[3 * t + j],
                    device_id=(px, py, c), device_id_type=MESH).wait_recv()
        for cp in copies:
            cp.wait_send()
        for cp in mine:
            cp.wait()

    return pl.pallas_call(
        body, name=name, out_shape=[jax.ShapeDtypeStruct(s.shape, s.dtype) for s in sums],
        in_specs=[ANY] * nt, out_specs=[ANY] * nt,
        scratch_shapes=[pltpu.SemaphoreType.DMA((3 * nt,)), pltpu.SemaphoreType.DMA((3 * nt,)), pltpu.SemaphoreType.DMA((nt,))],
    )(*sums)


def _chip_start(name, sums):
    nt = len(sums)

    def body(*refs):
        s_refs, land_refs = refs[:nt], refs[nt:2 * nt]
        send_sems, recv_sems = refs[2 * nt], refs[2 * nt + 1]
        x, y, c = _mesh_pos()
        my_chip = 2 * x + y
        for t in range(nt):
            for j, (px, py) in enumerate([(1 - x, y), (x, 1 - y), (1 - x, 1 - y)]):
                pltpu.make_async_remote_copy(
                    src_ref=s_refs[t].at[2 * px + py], dst_ref=land_refs[t].at[my_chip],
                    send_sem=send_sems.at[3 * t + j], recv_sem=recv_sems.at[3 * t + j],
                    device_id=(px, py, c), device_id_type=MESH).start()

    thru = [pltpu.HBM(s.shape, s.dtype) for s in sums] * 2
    args = [pltpu.with_memory_space_constraint(s, pltpu.HBM) for s in sums]
    args += [pltpu.with_memory_space_constraint(lax.empty(s.shape, s.dtype), pltpu.HBM) for s in sums]
    res = pl.pallas_call(
        body, name=name, out_shape=tuple([pltpu.SemaphoreType.DMA((3 * nt,))] * 2 + thru), in_specs=[HBM] * (2 * nt),
        out_specs=tuple([SEM, SEM] + [HBM] * (2 * nt)), input_output_aliases={t: 2 + t for t in range(2 * nt)},
        compiler_params=SPLIT_COPY,
    )(*args)
    return (res[0], res[1]), list(res[2:2 + nt]), list(res[2 + nt:])


def _chip_wait(name, sems, sums, lands, after):
    nt = len(sums)

    def body(*refs):
        s_refs, land_refs = refs[:nt], refs[nt:2 * nt]
        send_sems, recv_sems = refs[2 * nt], refs[2 * nt + 1]
        x, y, c = _mesh_pos()
        my_chip = 2 * x + y
        for t in range(nt):
            for j, (px, py) in enumerate([(1 - x, y), (x, 1 - y), (1 - x, 1 - y)]):
                cp = pltpu.make_async_remote_copy(
                    src_ref=s_refs[t].at[my_chip], dst_ref=land_refs[t].at[2 * px + py],
                    send_sem=send_sems.at[3 * t + j], recv_sem=recv_sems.at[3 * t + j],
                    device_id=(px, py, c), device_id_type=MESH)
                cp.wait_send()
                cp.wait_recv()

    thru = [pltpu.HBM(s.shape, s.dtype) for s in sums] * 2
    res = pl.pallas_call(
        body, name=name, out_shape=tuple(thru), in_specs=[HBM] * (2 * nt) + [SEM, SEM, ANY],
        out_specs=tuple([HBM] * (2 * nt)), input_output_aliases={t: t for t in range(2 * nt)},
        compiler_params=SPLIT_COPY,
    )(*sums, *lands, sems[0], sems[1], after)
    return list(res[:nt]), list(res[nt:])


def _sum_chips(name, own, landed):
    _, R, C = own.shape
    br = _first_divisor(R, (512, 384, 256, 128, 64, 32, 16, 8))
    x, y, _ = _mesh_pos()
    slots = jnp.stack([2 * x + y, 2 * (1 - x) + y, 2 * x + (1 - y), 2 * (1 - x) + (1 - y)]).astype(jnp.int32)

    def body(slot_ref, mine_ref, a_ref, b_ref, c_ref, o_ref):
        o_ref[...] = ((mine_ref[0].astype(F32) + a_ref[0].astype(F32)) + b_ref[0].astype(F32)) + c_ref[0].astype(F32)

    def slot_spec(j):
        return pl.BlockSpec((1, br, C), lambda i, slot_ref: (slot_ref[j], i, 0))

    return pl.pallas_call(
        body, name=name,
        grid_spec=pltpu.PrefetchScalarGridSpec(
            num_scalar_prefetch=1, grid=(R // br,), in_specs=[slot_spec(0), slot_spec(1), slot_spec(2), slot_spec(3)],
            out_specs=pl.BlockSpec((br, C), lambda i, slot_ref: (i, 0))),
        out_shape=jax.ShapeDtypeStruct((R, C), F32), compiler_params=_cparams(("parallel",)),
    )(slots, own, landed, landed, landed)


def _sum_slots(name, slots, n):
    _, R, C = slots.shape
    br = _first_divisor(R, (512, 384, 256, 128, 64, 32, 16, 8))

    def body(s_ref, o_ref):
        acc = s_ref[0].astype(F32)
        for k in range(1, n):
            acc = acc + s_ref[k].astype(F32)
        o_ref[...] = acc

    return pl.pallas_call(
        body, name=name, grid=(R // br,), in_specs=[pl.BlockSpec((n, br, C), lambda i: (0, i, 0))],
        out_specs=pl.BlockSpec((br, C), lambda i: (i, 0)), out_shape=jax.ShapeDtypeStruct((R, C), F32),
        compiler_params=_cparams(("parallel",)),
    )(slots)


def _reduce_scatter_start(tag, names, grads):
    lays = [LAYOUTS[n] for n in names]
    landed = _pair_exchange("grads_pair_" + names[0], grads, lays)
    sums = [_pair_sum("grads_pairsum_" + n, g, ld, lay, BF16) for n, g, ld, lay in zip(names, grads, landed, lays)]
    sems, sums, lands = _chip_start(tag + "_chips_start", sums)
    return tag, names, sems, sums, lands


def _reduce_scatter_finish(pending, after):
    tag, names, sems, sums, lands = pending
    own, got = _chip_wait(tag + "_chips_wait", sems, sums, lands, after)
    return [_sum_chips("grads_sum_" + n, o, s) for n, o, s in zip(names, own, got)]


def _adamw_math(w, g, m, v):
    m = ADAM_B1 * m + (1.0 - ADAM_B1) * g
    v = ADAM_B2 * v + (1.0 - ADAM_B2) * jnp.square(g)
    m_hat = m / (1.0 - ADAM_B1 ** ADAM_STEP)
    v_hat = v / (1.0 - ADAM_B2 ** ADAM_STEP)
    delta = -ADAM_LR * (m_hat / (jnp.sqrt(v_hat) + ADAM_EPS) + ADAM_WD * w)
    return delta, m, v


def _adamw_layers(name, w, totals, m, v):
    _, R, C = w.shape
    br = _first_divisor(R, (512, 176, 128, 64, 32, 16, 8))
    Cp = totals[0].shape[1]

    def body(w_ref, g0_ref, g1_ref, m_ref, v_ref, g_out, d_out, m_out, v_out):
        g = jnp.where(pl.program_id(0) == 0, g0_ref[:, 0:C], g1_ref[:, 0:C])
        delta, m_new, v_new = _adamw_math(w_ref[0], g, m_ref[0], v_ref[0])
        g_out[0], d_out[0], m_out[0], v_out[0] = g, delta, m_new, v_new

    blk = pl.BlockSpec((1, br, C), lambda l, i: (l, i, 0))
    g_spec = pl.BlockSpec((br, Cp), lambda l, i: (i, 0))
    return pl.pallas_call(
        body, name=name, grid=(DEPTH, R // br), in_specs=[blk, g_spec, g_spec, blk, blk], out_specs=[blk] * 4,
        out_shape=[jax.ShapeDtypeStruct(w.shape, F32)] * 4, compiler_params=_cparams(("parallel", "parallel")),
    )(w, totals[0], totals[1], m, v)


def _adamw(name, w, g, m, v):
    shape = w.shape
    cols = shape[-1]
    rows = int(np.prod(shape[:-1]))
    br = _first_divisor(rows, (512, 352, 256, 128, 64, 32, 16, 8))
    args = [_In(a.reshape(rows, cols)) for a in (w, g, m, v)]
    outs = _rowwise(name, _adamw_math, args, [_Out(cols), _Out(cols), _Out(cols)], rows, br)
    return [o.reshape(shape) for o in outs]


GROUPS = {"ffn1": ("ffn1_w_gate", "ffn1_w_up", "ffn1_w_down"),
          "mix": ("w_in", "w_branch_attn", "w_branch_mlstm", "w_out"),
          "ffn2": ("ffn2_w_gate", "ffn2_w_up", "ffn2_w_down")}
GATHER_GROUPS = {"ffn1_in": ("ffn1_w_gate", "ffn1_w_up"), "ffn1_out": ("ffn1_w_down",),
                 "mix": ("w_in", "w_branch_attn", "w_branch_mlstm", "w_out"),
                 "ffn2_in": ("ffn2_w_gate", "ffn2_w_up"), "ffn2_out": ("ffn2_w_down",)}


def _small_params(small, conv_w, l):
    p = {}
    for n in ("ffn1_norm", "mix_norm", "ffn2_norm", "block_out_norm", "mlstm_out_norm", "attn_q_norm", "attn_k_norm"):
        p[n] = small[n][l][None, :]
    p["attn_sink"] = small["attn_sink"][l]
    p["gate_bias"] = jnp.pad(small["mlstm_gate_bias"][l], (0, LANES - MLSTM_N_GATES))[None, :]
    taps = _qk_perm_cols(conv_w[l], 1)
    conv_b = _qk_perm_cols(small["mlstm_conv_b"][l][None, :], 1)
    p["conv_w8"] = jnp.concatenate([taps, conv_b, jnp.zeros((4, 2 * MLSTM_WIDTH), F32)], axis=0)
    return p


def _w_in_from_slots(slots):
    w_in = slots.reshape(N_DEV, D_MODEL, IN_WIDTH // N_DEV).transpose(1, 0, 2).reshape(D_MODEL, IN_WIDTH)
    return _w_in_arrange(w_in)


def _w_in_to_slots(g):
    return _w_in_restore(g).reshape(D_MODEL, N_DEV, IN_WIDTH // N_DEV).transpose(1, 0, 2).reshape(
        N_DEV * D_MODEL, IN_WIDTH // N_DEV)


def _local_step(x, positions, target, weights_of, small, conv_w, on_grads):
    B, S, _ = x.shape
    T = B * S
    cos, sin = _rope_cos_sin(positions.reshape(T, 1))
    params = [_small_params(small, conv_w, l) for l in range(DEPTH)]
    xs = x.reshape(T, D_MODEL)
    tgt = target.reshape(T, D_MODEL)

    saved = []
    for l, p in enumerate(params):
        p.update(weights_of(l, "ffn1_in", xs))
        x1, s1, p["ffn1_w_down"] = _ffn_fwd("ffn1", xs, p["ffn1_norm"], p["ffn1_w_gate"], p["ffn1_w_up"],
                                            lambda after, l=l: weights_of(l, "ffn1_out", after)["ffn1_w_down"])
        p.update(weights_of(l, "mix", x1))
        p["w_in"] = _w_in_from_slots(p["w_in"])
        x2, s2 = _mix_fwd(x1, cos, sin, B, S, p)
        p.update(weights_of(l, "ffn2_in", x2))
        x3, s3, p["ffn2_w_down"] = _ffn_fwd("ffn2", x2, p["ffn2_norm"], p["ffn2_w_gate"], p["ffn2_w_up"],
                                            lambda after, l=l: weights_of(l, "ffn2_out", after)["ffn2_w_down"])
        saved.append((s1, s2, s3, x3))
        if l + 1 < DEPTH:
            xs = _block_norm_fwd(x3, p["block_out_norm"])

    sm = {n: [None] * DEPTH for n in SMALL + ("mlstm_conv_w",)}
    loss = None
    dx = None
    for l in reversed(range(DEPTH)):
        p = params[l]
        s1, s2, s3, x3 = saved[l]
        if l == DEPTH - 1:
            loss, dx, dgn = _loss_and_grad(x3, p["block_out_norm"], tgt)
        else:
            dx, dgn = _block_norm_bwd(x3, p["block_out_norm"], dx)
        sm["block_out_norm"][l] = dgn[0]
        dx, dg = _ffn_bwd("ffn2", s3, p["ffn2_norm"], p["ffn2_w_gate"], p["ffn2_w_up"], p["ffn2_w_down"], dx,
                          functools.partial(on_grads, l, "ffn2"))
        sm["ffn2_norm"][l] = dg[0]
        dx, g = _mix_bwd(s2, cos, sin, B, S, p, dx, functools.partial(on_grads, l, "mix"))
        dconv = _qk_unperm_cols(g["conv_w8"], 1)
        sm["mlstm_conv_w"][l] = dconv[0:3]
        sm["mlstm_conv_b"][l] = dconv[3]
        sm["mix_norm"][l] = g["mix_norm"][0]
        sm["mlstm_gate_bias"][l] = g["gate_bias"][0, :MLSTM_N_GATES]
        sm["attn_q_norm"][l], sm["attn_k_norm"][l] = g["attn_q_norm"][0], g["attn_k_norm"][0]
        sm["attn_sink"][l] = g["attn_sink"][0]
        sm["mlstm_out_norm"][l] = g["mlstm_out_norm"][0]
        dx, dg = _ffn_bwd("ffn1", s1, p["ffn1_norm"], p["ffn1_w_gate"], p["ffn1_w_up"], p["ffn1_w_down"], dx,
                          functools.partial(on_grads, l, "ffn1"))
        sm["ffn1_norm"][l] = dg[0]
    sm = {n: jnp.stack(v, axis=0) for n, v in sm.items()}
    return loss, dx.reshape(B, S, D_MODEL), sm


def kernel(x, positions, ffn1_norm, ffn1_w_gate, ffn1_w_up, ffn1_w_down, mix_norm, w_in, mlstm_gate_bias, attn_q_norm, attn_k_norm, attn_sink, mlstm_conv_w, mlstm_conv_b, mlstm_out_norm, w_branch_attn, w_branch_mlstm, w_out, ffn2_norm, ffn2_w_gate, ffn2_w_up, ffn2_w_down, block_out_norm, loss_target, m_ffn1_norm, m_ffn1_w_gate, m_ffn1_w_up, m_ffn1_w_down, m_mix_norm, m_w_in, m_mlstm_gate_bias, m_attn_q_norm, m_attn_k_norm, m_attn_sink, m_mlstm_conv_w, m_mlstm_conv_b, m_mlstm_out_norm, m_w_branch_attn, m_w_branch_mlstm, m_w_out, m_ffn2_norm, m_ffn2_w_gate, m_ffn2_w_up, m_ffn2_w_down, m_block_out_norm, v_ffn1_norm, v_ffn1_w_gate, v_ffn1_w_up, v_ffn1_w_down, v_mix_norm, v_w_in, v_mlstm_gate_bias, v_attn_q_norm, v_attn_k_norm, v_attn_sink, v_mlstm_conv_w, v_mlstm_conv_b, v_mlstm_out_norm, v_w_branch_attn, v_w_branch_mlstm, v_w_out, v_ffn2_norm, v_ffn2_w_gate, v_ffn2_w_up, v_ffn2_w_down, v_block_out_norm):
    args = locals()
    w = {n: args[n] for n in WEIGHTS}
    m = {n: args["m_" + n] for n in WEIGHTS}
    v = {n: args["v_" + n] for n in WEIGHTS}

    order = [(l, grp) for l in range(DEPTH) for grp in GATHER_GROUPS]
    keys = [(l, n) for l, grp in order for n in GATHER_GROUPS[grp]]
    lays = [LAYOUTS[n] for _, n in keys]
    shards = [lay.pad(w[n][l].astype(BF16)) for (l, n), lay in zip(keys, lays)]
    group_idx, at = {}, 0
    for l, grp in order:
        group_idx[(l, grp)] = list(range(at, at + len(GATHER_GROUPS[grp])))
        at += len(GATHER_GROUPS[grp])
    conv_shape = w["mlstm_conv_w"].shape
    conv_all = _all_gather("conv_all_gather", _pack_flat([w["mlstm_conv_w"]], F32, 8), vmem=True)
    conv_parts = _unpack_flat(conv_all, [conv_shape], lead=(N_DEV,))[0]
    conv_w = jnp.concatenate([conv_parts[j] for j in range(N_DEV)], axis=2)
    small = {n: w[n] for n in SMALL}

    lands = []
    for l, grp in order:
        idx = group_idx[(l, grp)]
        lands += _place_own("weights_place_" + grp, [shards[i] for i in idx], [lays[i] for i in idx])
    sems, shards, lands = _gather_start("weights_gather_start", shards, lands, lays, [group_idx[k] for k in order], conv_all)

    def weights_of(l, grp, after):
        idx = group_idx[(l, grp)]
        whole = _gather_wait(f"weights_gather_wait_{l}_{grp}", sems[order.index((l, grp))], [shards[i] for i in idx],
                             [lands[i] for i in idx], [lays[i] for i in idx], after)
        return dict(zip(GATHER_GROUPS[grp], whole))

    totals, pending = {}, []

    def finish(after):
        tag, names = pending[0][0], pending[0][1]
        for n, t in zip(names, _reduce_scatter_finish(pending.pop(0), after)):
            totals[(tag, n)] = t

    def on_grads(l, grp, g, after):
        if pending:
            finish(after)
        names = GROUPS[grp]
        pending.append(_reduce_scatter_start(f"grads_{l}_{grp}", names, [g[n] for n in names]))
        return pending[-1][3][0]

    loss, grad_x, small_g = _local_step(x, positions, loss_target, weights_of, small, conv_w, on_grads)
    finish(grad_x)
    grads, deltas, new_m, new_v = {}, {}, {}, {}
    for grp, names in GROUPS.items():
        for n in names:
            grads[n], deltas[n], new_m[n], new_v[n] = _adamw_layers(
                "adamw_" + n, w[n], [totals[(f"grads_{l}_{grp}", n)] for l in range(DEPTH)], m[n], v[n])

    small_names = SMALL + ("mlstm_conv_w",)
    small_shapes = [small_g[n].shape for n in small_names] + [(1, 1)]
    small_packed = _pack_flat([small_g[n] for n in small_names] + [loss], F32, 8)
    small_all = _all_gather("small_all_gather", small_packed, vmem=True)
    small_sum = _sum_slots("small_sum", small_all, N_DEV)
    *small_grads, loss_total = _unpack_flat(small_sum, small_shapes)
    grads.update(dict(zip(small_names, small_grads)))
    x_pos, y_pos, c_pos = _mesh_pos()
    grads["mlstm_conv_w"] = lax.dynamic_slice_in_dim(
        grads["mlstm_conv_w"], (4 * x_pos + 2 * y_pos + c_pos) * conv_shape[2], conv_shape[2], axis=2)

    n = "mlstm_conv_w"
    deltas[n], new_m[n], new_v[n] = _adamw("adamw_" + n, w[n], grads[n], m[n], v[n])
    sw, sg, smm, sv = (_pack_flat([d[n] for n in SMALL], F32, 8) for d in (w, grads, m, v))
    sd, snm, snv = _adamw("adamw_small", sw, sg, smm, sv)
    shapes = [w[n].shape for n in SMALL]
    for d, buf in ((deltas, sd), (new_m, snm), (new_v, snv)):
        d.update(dict(zip(SMALL, _unpack_flat(buf, shapes))))

    return (loss_total.reshape(()), grad_x, *[grads[n] for n in WEIGHTS], *[deltas[n] for n in WEIGHTS],
            *[new_m[n] for n in WEIGHTS], *[new_v[n] for n in WEIGHTS])
```

```python
import functools

import numpy as np
import jax
import jax.numpy as jnp
from jax import lax
from jax.experimental import pallas as pl
from jax.experimental.pallas import tpu as pltpu

F32 = jnp.float32
BF16 = jnp.bfloat16

D_MODEL = 1024
D_FF = 2816
ATT_HEAD_DIM = 64
ATT_HEADS = 8
ATT_KV_HEADS = 2
ATT_GROUP = ATT_HEADS // ATT_KV_HEADS
ATT_WIDTH = ATT_HEADS * ATT_HEAD_DIM
ATT_KV_WIDTH = ATT_KV_HEADS * ATT_HEAD_DIM
WINDOW = 128
ATT_BLOCK = 128
ROPE_DIM = 16
ROPE_THETA = 500000.0
MLSTM_HEADS = 4
MLSTM_HEAD_DIM = 128
MLSTM_WIDTH = MLSTM_HEADS * MLSTM_HEAD_DIM
MLSTM_CHUNK = 128
MLSTM_N_GATES = 4 * MLSTM_HEADS
NORM_EPS = 1e-6
IN_WIDTH = 4880
DEPTH = 2
N_DEV = 8

ADAM_LR = 0.001
ADAM_B1 = 0.9
ADAM_B2 = 0.999
ADAM_EPS = 1e-08
ADAM_WD = 0.01
ADAM_STEP = 10

LANES = 128
C_GMERGE = 0
C_QK = 2048
C_VM = 3072
C_OM = 3584
C_QA = 4096
C_KA = 4608
C_VA = 4736
C_GATES = 4864
IN_PAD = 4992

VMEM_LIMIT = 48 * 1024 * 1024

MESH = pl.DeviceIdType.MESH


def _cparams(sem):
    return pltpu.CompilerParams(dimension_semantics=sem, vmem_limit_bytes=VMEM_LIMIT)


def _first_divisor(n, cands):
    for c in cands:
        if n % c == 0:
            return c
    return n


_NN = ((1,), (0,))
_NT = ((1,), (1,))
_TN = ((0,), (0,))


def _mm(a, b, dims):
    return lax.dot_general(a.astype(BF16), b.astype(BF16), (dims, ((), ())), preferred_element_type=F32)


@jax.custom_vjp
def mm_nn(a, b):
    return _mm(a, b, _NN)


def _mm_nn_fwd(a, b):
    return _mm(a, b, _NN), (a, b)


def _mm_nn_bwd(res, g):
    a, b = res
    return _mm(g, b, _NT).astype(a.dtype), _mm(a, g, _TN).astype(b.dtype)


mm_nn.defvjp(_mm_nn_fwd, _mm_nn_bwd)


@jax.custom_vjp
def mm_nt(a, b):
    return _mm(a, b, _NT)


def _mm_nt_fwd(a, b):
    return _mm(a, b, _NT), (a, b)


def _mm_nt_bwd(res, g):
    a, b = res
    return _mm(g, b, _NN).astype(a.dtype), _mm(g, a, _TN).astype(b.dtype)


mm_nt.defvjp(_mm_nt_fwd, _mm_nt_bwd)


@jax.custom_vjp
def mm_tn(a, b):
    return _mm(a, b, _TN)


def _mm_tn_fwd(a, b):
    return _mm(a, b, _TN), (a, b)


def _mm_tn_bwd(res, g):
    a, b = res
    return _mm(b, g, _NT).astype(a.dtype), _mm(a, g, _NN).astype(b.dtype)


mm_tn.defvjp(_mm_tn_fwd, _mm_tn_bwd)


def _matmul(name, a, b, mode, out_dtype=F32, res=None, scale=1.0, bl=None, dep=None):
    b_shape = b.shape if bl is None else b.shape[1:]
    if mode == "nn":
        (M, K), (K2, N) = a.shape, b_shape
    elif mode == "nt":
        (M, K), (N, K2) = a.shape, b_shape
    else:
        (K, M), (K2, N) = a.shape, b_shape
    assert K == K2, (name, a.shape, b.shape)
    tm = _first_divisor(M, (1024, 512, 384, 256, 128))
    tn = _first_divisor(N, (1024, 1664, 512, 384, 256, 128))
    tk = _first_divisor(K, (1024, 1664, 512, 256, 128))
    nk = K // tk
    if mode == "tn":
        a_spec = pl.BlockSpec((tk, tm), lambda i, j, k: (k, i))
    else:
        a_spec = pl.BlockSpec((tm, tk), lambda i, j, k: (i, k))
    if mode == "nt":
        b_blk, b_idx = (tn, tk), (lambda i, j, k: (j, k))
    else:
        b_blk, b_idx = (tk, tn), (lambda i, j, k: (k, j))
    if bl is None:
        b_spec = pl.BlockSpec(b_blk, b_idx)
    else:
        b_spec = pl.BlockSpec((None,) + b_blk, lambda i, j, k: (bl,) + b_idx(i, j, k))
    o_spec = pl.BlockSpec((tm, tn), lambda i, j, k: (i, j))
    dims = {"nn": _NN, "nt": _NT, "tn": _TN}[mode]
    has_res = res is not None

    def body(*refs):
        a_ref, b_ref = refs[:2]
        r_ref = refs[2] if has_res else None

        def finish(out):
            if scale != 1.0:
                out = out * scale
            if has_res:
                out = r_ref[...].astype(F32) + out
            o_ref[...] = out.astype(out_dtype)

        if nk == 1:
            o_ref = refs[-1]
            finish(_mm(a_ref[...], b_ref[...], dims))
            return
        o_ref, acc = refs[-2:]
        k = pl.program_id(2)

        @pl.when(k == 0)
        def _():
            acc[...] = jnp.zeros_like(acc)

        acc[...] += _mm(a_ref[...], b_ref[...], dims)

        @pl.when(k == nk - 1)
        def _():
            finish(acc[...])

    in_specs = [a_spec, b_spec] + ([o_spec] if has_res else [])
    args = (a, b) + ((res,) if has_res else ())
    if dep is not None:
        in_specs.append(pl.BlockSpec(memory_space=pl.ANY))
        args += (dep,)
    return pl.pallas_call(
        body, name=name, grid=(M // tm, N // tn, nk), in_specs=in_specs, out_specs=o_spec,
        out_shape=jax.ShapeDtypeStruct((M, N), out_dtype),
        scratch_shapes=[pltpu.VMEM((tm, tn), F32)] if nk > 1 else [],
        compiler_params=_cparams(("parallel", "parallel", "arbitrary")),
    )(*args)


class _In:
    def __init__(self, arr, width=None, base=0, split=False, rows=True):
        self.arr, self.base, self.split, self.rows = arr, base, split, rows
        self.width = arr.shape[1] if width is None else width


class _Out:
    def __init__(self, cols, dtype=F32, width=None, split=False, rows=True, nrows=1):
        self.cols, self.dtype, self.split, self.rows, self.nrows = cols, dtype, split, rows, nrows
        self.width = cols if width is None else width


def _rowwise(name, fn, ins, outs, n_rows, br, ncol=1):
    br = min(br, n_rows)
    assert n_rows % br == 0, (name, n_rows, br)
    nrow_blocks = n_rows // br

    def in_spec(d):
        nb = br if d.rows else d.arr.shape[0]
        if d.rows and d.split:
            im = lambda j, i, base=d.base: (i, base + j)
        elif d.rows:
            im = lambda j, i, base=d.base: (i, base)
        elif d.split:
            im = lambda j, i, base=d.base: (0, base + j)
        else:
            im = lambda j, i, base=d.base: (0, base)
        return pl.BlockSpec((nb, d.width), im)

    def out_spec(d):
        nb = br if d.rows else d.nrows
        if d.rows and d.split:
            im = lambda j, i: (i, j)
        elif d.rows:
            im = lambda j, i: (i, 0)
        elif d.split:
            im = lambda j, i: (0, j)
        else:
            im = lambda j, i: (0, 0)
        return pl.BlockSpec((nb, d.width), im)

    n_in = len(ins)

    def body(*refs):
        i = pl.program_id(1)
        vals = [r[...] for r in refs[:n_in]]
        res = fn(*vals)
        if not isinstance(res, (tuple, list)):
            res = (res,)
        for d, ref, val in zip(outs, refs[n_in:], res):
            if d.rows:
                ref[...] = val.astype(d.dtype)
            else:
                @pl.when(i == 0)
                def _(ref=ref):
                    ref[...] = jnp.zeros_like(ref)

                ref[...] += val.astype(d.dtype)

    out_shape = [jax.ShapeDtypeStruct((n_rows if d.rows else d.nrows, d.cols), d.dtype) for d in outs]
    res = pl.pallas_call(
        body, name=name, grid=(ncol, nrow_blocks), in_specs=[in_spec(d) for d in ins],
        out_specs=[out_spec(d) for d in outs], out_shape=out_shape,
        compiler_params=_cparams(("parallel", "arbitrary")),
    )(*[d.arr for d in ins])
    return res


def _rms(x, g):
    return x * lax.rsqrt(jnp.mean(x * x, axis=-1, keepdims=True) + NORM_EPS) * g


def _sigmoid(x):
    return 1.0 / (1.0 + jnp.exp(-x))


def _silu(x):
    return x * _sigmoid(x)


def _log_sigmoid(x):
    return jnp.minimum(x, 0.0) - jnp.log(1.0 + jnp.exp(-jnp.abs(x)))


def _rope_tables(pos, inv_freq_row):
    ang = pos.astype(F32) * inv_freq_row
    return jnp.cos(ang), jnp.sin(ang)


def _head_sums_impl(v):
    w = v.shape[-1]
    shift = ATT_HEAD_DIM.bit_length() - 1
    r = lax.shift_right_logical(lax.broadcasted_iota(jnp.int32, (w, w), 0), shift)
    c = lax.shift_right_logical(lax.broadcasted_iota(jnp.int32, (w, w), 1), shift)
    ones = (r == c).astype(BF16)
    hi = v.astype(BF16)
    lo = (v - hi.astype(F32)).astype(BF16)
    dn = (_NN, ((), ()))
    return (lax.dot_general(hi, ones, dn, preferred_element_type=F32)
            + lax.dot_general(lo, ones, dn, preferred_element_type=F32))


@jax.custom_vjp
def _head_sums(v):
    return _head_sums_impl(v)


_head_sums.defvjp(lambda v: (_head_sums_impl(v), None), lambda _, g: (_head_sums_impl(g),))


def _rotate_half_impl(y):
    w = y.shape[-1]
    half = ROPE_DIM // 2
    lane = lax.broadcasted_iota(jnp.int32, y.shape, 1) & (ATT_HEAD_DIM - 1)
    above = pltpu.roll(y, w - half, axis=1)
    below = pltpu.roll(y, half, axis=1)
    return jnp.where(lane < half, -above, jnp.where(lane < ROPE_DIM, below, 0.0))


@jax.custom_vjp
def _rotate_half(y):
    return _rotate_half_impl(y)


_rotate_half.defvjp(lambda y: (_rotate_half_impl(y), None), lambda _, g: (-_rotate_half_impl(g),))


def _qk_prep(t, g, cos, sin):
    reps = t.shape[-1] // cos.shape[-1]
    if reps > 1:
        cos, sin = jnp.tile(cos, (1, reps)), jnp.tile(sin, (1, reps))
    y = t * lax.rsqrt(_head_sums(t * t) * (1.0 / ATT_HEAD_DIM) + NORM_EPS) * g
    return y * cos + _rotate_half(y) * sin


def _attn_head(q, kb, vb, sink, valid):
    s = mm_nt(q, kb) * (ATT_HEAD_DIM ** -0.5)
    s = jnp.where(valid, s, -jnp.inf)
    m = jnp.maximum(jnp.max(s, axis=-1, keepdims=True), sink)
    p = jnp.exp(s - m)
    den = jnp.sum(p, axis=-1, keepdims=True) + jnp.exp(sink - m)
    return mm_nn(p / den, vb)


def _mlstm_chunk(q, k, v, li, lf_pre, C, n, m, incl, incl_t, eye):
    k = k * (MLSTM_HEAD_DIM ** -0.5)
    lf = _log_sigmoid(lf_pre)
    lf_row = jnp.sum(eye * lf, axis=0, keepdims=True)
    li_row = jnp.sum(eye * li, axis=0, keepdims=True)
    b = jnp.sum(incl * lf_row, axis=1, keepdims=True)
    b_row = jnp.sum(incl_t * lf, axis=0, keepdims=True)
    b_tot = jnp.sum(lf, axis=0, keepdims=True)
    a = b_tot - b + li
    a_max = jnp.max(a, axis=0, keepdims=True)
    kw = k * jnp.exp(a - a_max)
    c_loc = mm_tn(kw, v)
    n_loc = jnp.sum(kw, axis=0, keepdims=True)

    dmat = jnp.where(incl > 0.5, b - b_row + li_row, -jnp.inf)
    inter = b + m
    m_t = jnp.maximum(inter, jnp.max(dmat, axis=1, keepdims=True))
    sc = mm_nt(q, k) * jnp.exp(dmat - m_t)
    scale_in = jnp.exp(inter - m_t)
    num = mm_nn(sc, v) + scale_in * mm_nn(q, C)
    den = jnp.sum(sc, axis=1, keepdims=True) + scale_in * jnp.sum(q * n, axis=1, keepdims=True)
    h = num / jnp.maximum(jnp.abs(den), jnp.exp(-m_t))

    m_new = jnp.maximum(b_tot + m, a_max)
    s_p = jnp.exp(b_tot + m - m_new)
    s_l = jnp.exp(a_max - m_new)
    return h, s_p * C + s_l * c_loc, s_p * n + s_l * n_loc, m_new


def _mlstm_combine(hf, hb, o_pre, g):
    h = hf + hb
    mu = jnp.mean(h, axis=-1, keepdims=True)
    var = jnp.mean(jnp.square(h - mu), axis=-1, keepdims=True)
    return _sigmoid(o_pre) * ((h - mu) * lax.rsqrt(var + NORM_EPS) * g)


def _merge(ga, gm, za, zm):
    return _sigmoid(ga) * za + _sigmoid(gm) * zm


def _attn_mask(n, seq):
    qi = n * ATT_BLOCK + lax.broadcasted_iota(jnp.int32, (ATT_BLOCK, 3 * ATT_BLOCK), 0)
    kj = (n - 1) * ATT_BLOCK + lax.broadcasted_iota(jnp.int32, (ATT_BLOCK, 3 * ATT_BLOCK), 1)
    return (jnp.abs(qi - kj) <= WINDOW) & (kj >= 0) & (kj < seq)


def _attn_specs(nq, v_base):
    q_spec = pl.BlockSpec((1, ATT_BLOCK, ATT_WIDTH), lambda b, n: (b, n, 0))

    def kv_spec(off, base=0):
        return pl.BlockSpec((1, ATT_BLOCK, ATT_KV_WIDTH), lambda b, n: (b, jnp.clip(n + off, 0, nq - 1), base))

    sink_spec = pl.BlockSpec((ATT_KV_HEADS, ATT_GROUP, 1, 1), lambda b, n: (0, 0, 0, 0))
    specs = [q_spec, kv_spec(-1), kv_spec(0), kv_spec(1), kv_spec(-1, v_base), kv_spec(0, v_base), kv_spec(1, v_base), sink_spec]
    return q_spec, specs, sink_spec


def _head(h):
    return slice(h * ATT_HEAD_DIM, (h + 1) * ATT_HEAD_DIM)


def _attn_fwd(q, k, proj3, sink):
    B, S, _ = q.shape
    nq = S // ATT_BLOCK
    q_spec, specs, _ = _attn_specs(nq, C_VA // ATT_KV_WIDTH)

    def body(q_ref, kp, kc, kn, vp, vc, vn, s_ref, o_ref):
        valid = _attn_mask(pl.program_id(1), S)
        for h in range(ATT_KV_HEADS):
            kb = jnp.concatenate([kp[0, :, _head(h)], kc[0, :, _head(h)], kn[0, :, _head(h)]], axis=0)
            vb = jnp.concatenate([vp[0, :, _head(h)], vc[0, :, _head(h)], vn[0, :, _head(h)]], axis=0)
            for g in range(ATT_GROUP):
                hq = h * ATT_GROUP + g
                o_ref[0, :, _head(hq)] = _attn_head(q_ref[0, :, _head(hq)], kb, vb, s_ref[h, g], valid).astype(BF16)

    return pl.pallas_call(
        body, name="attn_fwd", grid=(B, nq), in_specs=specs,
        out_specs=q_spec, out_shape=jax.ShapeDtypeStruct(q.shape, BF16),
        compiler_params=_cparams(("parallel", "arbitrary")),
    )(q, k, k, k, proj3, proj3, proj3, sink)


def _attn_bwd(q, k, proj3, sink, dy):
    B, S, _ = q.shape
    nq = S // ATT_BLOCK
    q_spec, specs, sink_spec = _attn_specs(nq, C_VA // ATT_KV_WIDTH)
    kv_full = pl.BlockSpec((1, S, ATT_KV_WIDTH), lambda b, n: (b, 0, 0))

    def body(q_ref, kp, kc, kn, vp, vc, vn, s_ref, dy_ref, dq_ref, dk_ref, dv_ref, ds_ref):
        b, n = pl.program_id(0), pl.program_id(1)
        valid = _attn_mask(n, S)

        @pl.when(n == 0)
        def _():
            dk_ref[...] = jnp.zeros_like(dk_ref)
            dv_ref[...] = jnp.zeros_like(dv_ref)

        @pl.when((n == 0) & (b == 0))
        def _():
            ds_ref[...] = jnp.zeros_like(ds_ref)

        for h in range(ATT_KV_HEADS):
            kb = jnp.concatenate([kp[0, :, _head(h)], kc[0, :, _head(h)], kn[0, :, _head(h)]], axis=0)
            vb = jnp.concatenate([vp[0, :, _head(h)], vc[0, :, _head(h)], vn[0, :, _head(h)]], axis=0)
            dkb = jnp.zeros_like(kb)
            dvb = jnp.zeros_like(vb)
            for g in range(ATT_GROUP):
                hq = h * ATT_GROUP + g
                _, vjp = jax.vjp(functools.partial(_attn_head, valid=valid), q_ref[0, :, _head(hq)], kb, vb, s_ref[h, g])
                dq, dk_g, dv_g, dsink = vjp(dy_ref[0, :, _head(hq)])
                dq_ref[0, :, _head(hq)] = dq.astype(dq_ref.dtype)
                ds_ref[h, g] += dsink
                dkb += dk_g
                dvb += dv_g
            for j, off in enumerate((-1, 0, 1)):
                start = pl.multiple_of(jnp.clip(n + off, 0, nq - 1) * ATT_BLOCK, ATT_BLOCK)
                rows = pl.ds(start, ATT_BLOCK)
                dk_ref[0, rows, _head(h)] += dkb[j * ATT_BLOCK:(j + 1) * ATT_BLOCK]
                dv_ref[0, rows, _head(h)] += dvb[j * ATT_BLOCK:(j + 1) * ATT_BLOCK]

    kv_shape = jax.ShapeDtypeStruct(k.shape, F32)
    return pl.pallas_call(
        body, name="attn_bwd", grid=(B, nq), in_specs=specs + [q_spec],
        out_specs=[q_spec, kv_full, kv_full, sink_spec],
        out_shape=[jax.ShapeDtypeStruct(q.shape, F32), kv_shape, kv_shape, jax.ShapeDtypeStruct(sink.shape, F32)],
        compiler_params=_cparams(("arbitrary", "arbitrary")),
    )(q, k, k, k, proj3, proj3, proj3, sink, dy)


CONV_COLS = 256


def _conv_taps(u, seq):
    row = lax.broadcasted_iota(jnp.int32, u.shape, 0)
    prev = jnp.where(row == 0, 0.0, pltpu.roll(u, 1, axis=0))
    nxt = jnp.where(row == seq - 1, 0.0, pltpu.roll(u, seq - 1, axis=0))
    return prev, nxt


def _conv_fwd(proj3, w8):
    B, S, _ = proj3.shape
    ncb = 2 * MLSTM_WIDTH // CONV_COLS

    def body(u_ref, w_ref, o_ref):
        u = u_ref[0]
        prev, nxt = _conv_taps(u, S)
        o_ref[0] = _silu(prev * w_ref[0:1, :] + u * w_ref[1:2, :] + nxt * w_ref[2:3, :] + w_ref[3:4, :])

    return pl.pallas_call(
        body, name="conv_fwd", grid=(B, ncb),
        in_specs=[pl.BlockSpec((1, S, CONV_COLS), lambda b, c: (b, 0, C_QK // CONV_COLS + c)),
                  pl.BlockSpec((8, CONV_COLS), lambda b, c: (0, c))],
        out_specs=pl.BlockSpec((1, S, CONV_COLS), lambda b, c: (b, 0, c)),
        out_shape=jax.ShapeDtypeStruct((B, S, 2 * MLSTM_WIDTH), F32),
        compiler_params=_cparams(("parallel", "parallel")),
    )(proj3, w8)


def _conv_bwd(proj3, w8, dout_f, dout_b):
    B, S, _ = proj3.shape
    ncb = 2 * MLSTM_WIDTH // CONV_COLS

    def body(u_ref, w_ref, df_ref, db_ref, du_ref, dw_ref):
        b = pl.program_id(1)
        u = u_ref[0]
        prev, nxt = _conv_taps(u, S)
        w0, w1, w2 = w_ref[0:1, :], w_ref[1:2, :], w_ref[2:3, :]
        pre = prev * w0 + u * w1 + nxt * w2 + w_ref[3:4, :]
        sig = _sigmoid(pre)
        dpre = (df_ref[0] + db_ref[0]) * (sig * (1.0 + pre * (1.0 - sig)))
        dprev, dnxt = _conv_taps(dpre, S)
        du_ref[0] = (dnxt * w0 + dpre * w1 + dprev * w2).astype(BF16)

        @pl.when(b == 0)
        def _():
            dw_ref[...] = jnp.zeros_like(dw_ref)

        dw_ref[0:1, :] += jnp.sum(dpre * prev, axis=0, keepdims=True)
        dw_ref[1:2, :] += jnp.sum(dpre * u, axis=0, keepdims=True)
        dw_ref[2:3, :] += jnp.sum(dpre * nxt, axis=0, keepdims=True)
        dw_ref[3:4, :] += jnp.sum(dpre, axis=0, keepdims=True)

    blk = pl.BlockSpec((1, S, CONV_COLS), lambda c, b: (b, 0, c))
    return pl.pallas_call(
        body, name="conv_bwd", grid=(ncb, B),
        in_specs=[pl.BlockSpec((1, S, CONV_COLS), lambda c, b: (b, 0, C_QK // CONV_COLS + c)),
                  pl.BlockSpec((8, CONV_COLS), lambda c, b: (0, c)), blk, blk],
        out_specs=[blk, pl.BlockSpec((8, CONV_COLS), lambda c, b: (0, c))],
        out_shape=[jax.ShapeDtypeStruct((B, S, 2 * MLSTM_WIDTH), BF16), jax.ShapeDtypeStruct((8, 2 * MLSTM_WIDTH), F32)],
        compiler_params=_cparams(("parallel", "arbitrary")),
    )(proj3, w8, dout_f, dout_b)


MLSTM_HEADS_PER_STEP = 4


def _chunk_masks(direction):
    t = lax.broadcasted_iota(jnp.int32, (MLSTM_CHUNK, MLSTM_CHUNK), 0)
    s = lax.broadcasted_iota(jnp.int32, (MLSTM_CHUNK, MLSTM_CHUNK), 1)
    le, ge = (s <= t).astype(F32), (s >= t).astype(F32)
    eye = (s == t).astype(F32)
    return (le, ge, eye) if direction == 0 else (ge, le, eye)


def _gate_cols(gates, direction, head):
    lane = lax.broadcasted_iota(jnp.int32, gates.shape, 1)
    sel_i = (lane == (2 * direction) * MLSTM_HEADS + head).astype(F32)
    sel_f = (lane == (2 * direction + 1) * MLSTM_HEADS + head).astype(F32)
    return sel_i, sel_f


def _mlstm_fwd(qk, proj3, bias):
    B, S, _ = qk.shape
    nc = S // MLSTM_CHUNK
    H, L, DH = MLSTM_HEADS, MLSTM_CHUNK, MLSTM_HEAD_DIM

    def chunk_of(d, c):
        return c if d == 0 else nc - 1 - c

    HS = MLSTM_HEADS_PER_STEP

    def body(qkf, qkb, vf, vb, gf, gb, bias_ref, hf, hb, csf, csb, nsf, nsb, msf, msb, c_st, n_st, m_st):
        c, hg = pl.program_id(1), pl.program_id(2)

        @pl.when(c == 0)
        def _():
            for d in range(2):
                for j in range(HS):
                    c_st[d, hg * HS + j] = jnp.zeros((DH, DH), F32)
                    n_st[d, hg * HS + j] = jnp.zeros((1, DH), F32)
                    m_st[d, hg * HS + j] = jnp.zeros((1, DH), F32)

        for d, (qk_ref, v_ref, g_ref, h_ref, cs, ns, ms) in enumerate(
                ((qkf, vf, gf, hf, csf, nsf, msf), (qkb, vb, gb, hb, csb, nsb, msb))):
            incl, incl_t, eye = _chunk_masks(d)
            gates = g_ref[0] + bias_ref[...]
            for j in range(HS):
                h = hg * HS + j
                sel_i, sel_f = _gate_cols(gates, d, h)
                li = jnp.sum(gates * sel_i, axis=1, keepdims=True)
                lf_pre = jnp.sum(gates * sel_f, axis=1, keepdims=True)
                c_in, n_in, m_in = c_st[d, h], n_st[d, h], m_st[d, h]
                cs[0, 0, j], ns[0, 0, j], ms[0, 0, j] = c_in, n_in, m_in
                hh, c_new, n_new, m_new = _mlstm_chunk(
                    qk_ref[0, :, 2 * j * DH:(2 * j + 1) * DH], qk_ref[0, :, (2 * j + 1) * DH:(2 * j + 2) * DH],
                    v_ref[0, :, j * DH:(j + 1) * DH], li, lf_pre, c_in, n_in,
                    jnp.max(m_in, axis=1, keepdims=True), incl, incl_t, eye)
                h_ref[0, :, j * DH:(j + 1) * DH] = hh
                c_st[d, h], n_st[d, h] = c_new, n_new
                m_st[d, h] = jnp.broadcast_to(m_new, (1, DH))

    def tok_spec(width, base, d, per_head):
        return pl.BlockSpec((1, L, width), lambda b, c, h: (b, chunk_of(d, c), base + (h if per_head else 0)))

    def st_spec(shape, d):
        return pl.BlockSpec((1, 1, HS) + shape, lambda b, c, h: (b, chunk_of(d, c), h, 0, 0))

    in_specs = [tok_spec(2 * HS * DH, 0, 0, True), tok_spec(2 * HS * DH, 0, 1, True),
                tok_spec(HS * DH, C_VM // (HS * DH), 0, True), tok_spec(HS * DH, C_VM // (HS * DH), 1, True),
                tok_spec(LANES, C_GATES // LANES, 0, False), tok_spec(LANES, C_GATES // LANES, 1, False),
                pl.BlockSpec((1, LANES), lambda b, c, h: (0, 0))]
    out_specs = [tok_spec(HS * DH, 0, 0, True), tok_spec(HS * DH, 0, 1, True),
                 st_spec((DH, DH), 0), st_spec((DH, DH), 1), st_spec((1, DH), 0), st_spec((1, DH), 1),
                 st_spec((1, DH), 0), st_spec((1, DH), 1)]
    hs = jax.ShapeDtypeStruct((B, S, H * DH), F32)
    cs = jax.ShapeDtypeStruct((B, nc, H, DH, DH), F32)
    vs = jax.ShapeDtypeStruct((B, nc, H, 1, DH), F32)
    return pl.pallas_call(
        body, name="mlstm_fwd", grid=(B, nc, H // HS), in_specs=in_specs, out_specs=out_specs,
        out_shape=[hs, hs, cs, cs, vs, vs, vs, vs],
        scratch_shapes=[pltpu.VMEM((2, H, DH, DH), F32), pltpu.VMEM((2, H, 1, DH), F32), pltpu.VMEM((2, H, 1, DH), F32)],
        compiler_params=_cparams(("parallel", "arbitrary", "arbitrary")),
    )(qk, qk, proj3, proj3, proj3, proj3, bias)


def _mlstm_bwd(qk, proj3, bias, states, dh):
    B, S, _ = qk.shape
    nc = S // MLSTM_CHUNK
    H, L, DH = MLSTM_HEADS, MLSTM_CHUNK, MLSTM_HEAD_DIM

    def chunk_of(d, c):
        return nc - 1 - c if d == 0 else c

    HS = MLSTM_HEADS_PER_STEP

    def body(qkf, qkb, vf, vb, gf, gb, bias_ref, csf, csb, nsf, nsb, msf, msb, dhf, dhb,
             dqkf, dqkb, dvf, dvb, dgf, dgb, dc_st, dn_st, dm_st):
        c, hg = pl.program_id(1), pl.program_id(2)

        @pl.when(c == 0)
        def _():
            for d in range(2):
                for j in range(HS):
                    dc_st[d, hg * HS + j] = jnp.zeros((DH, DH), F32)
                    dn_st[d, hg * HS + j] = jnp.zeros((1, DH), F32)
                    dm_st[d, hg * HS + j] = jnp.zeros((1, DH), F32)

        @pl.when(hg == 0)
        def _():
            dgf[...] = jnp.zeros_like(dgf)
            dgb[...] = jnp.zeros_like(dgb)

        for d, (qk_ref, v_ref, g_ref, cs, ns, ms, dh_ref, dqk_ref, dv_ref, dg_ref) in enumerate(
                ((qkf, vf, gf, csf, nsf, msf, dhf, dqkf, dvf, dgf), (qkb, vb, gb, csb, nsb, msb, dhb, dqkb, dvb, dgb))):
            incl, incl_t, eye = _chunk_masks(d)
            gates = g_ref[0] + bias_ref[...]
            dgates = jnp.zeros_like(gates)
            for j in range(HS):
                h = hg * HS + j
                sel_i, sel_f = _gate_cols(gates, d, h)
                li = jnp.sum(gates * sel_i, axis=1, keepdims=True)
                lf_pre = jnp.sum(gates * sel_f, axis=1, keepdims=True)
                m_in = jnp.max(ms[0, 0, j], axis=1, keepdims=True)
                _, vjp = jax.vjp(
                    functools.partial(_mlstm_chunk, incl=incl, incl_t=incl_t, eye=eye),
                    qk_ref[0, :, 2 * j * DH:(2 * j + 1) * DH], qk_ref[0, :, (2 * j + 1) * DH:(2 * j + 2) * DH],
                    v_ref[0, :, j * DH:(j + 1) * DH], li, lf_pre, cs[0, 0, j], ns[0, 0, j], m_in)
                dm_out = jnp.max(dm_st[d, h], axis=1, keepdims=True)
                dq, dk, dv, dli, dlf, dc, dn, dm = vjp((dh_ref[0, :, j * DH:(j + 1) * DH], dc_st[d, h], dn_st[d, h], dm_out))
                dqk_ref[0, :, 2 * j * DH:(2 * j + 1) * DH] = dq
                dqk_ref[0, :, (2 * j + 1) * DH:(2 * j + 2) * DH] = dk
                dv_ref[0, :, j * DH:(j + 1) * DH] = dv
                dgates += dli * sel_i + dlf * sel_f
                dc_st[d, h], dn_st[d, h] = dc, dn
                dm_st[d, h] = jnp.broadcast_to(dm, (1, DH))
            dg_ref[0] += dgates

    def tok_spec(width, base, d, per_head):
        return pl.BlockSpec((1, L, width), lambda b, c, h: (b, chunk_of(d, c), base + (h if per_head else 0)))

    def st_spec(shape, d):
        return pl.BlockSpec((1, 1, HS) + shape, lambda b, c, h: (b, chunk_of(d, c), h, 0, 0))

    in_specs = [tok_spec(2 * HS * DH, 0, 0, True), tok_spec(2 * HS * DH, 0, 1, True),
                tok_spec(HS * DH, C_VM // (HS * DH), 0, True), tok_spec(HS * DH, C_VM // (HS * DH), 1, True),
                tok_spec(LANES, C_GATES // LANES, 0, False), tok_spec(LANES, C_GATES // LANES, 1, False),
                pl.BlockSpec((1, LANES), lambda b, c, h: (0, 0)),
                st_spec((DH, DH), 0), st_spec((DH, DH), 1), st_spec((1, DH), 0), st_spec((1, DH), 1),
                st_spec((1, DH), 0), st_spec((1, DH), 1), tok_spec(HS * DH, 0, 0, True), tok_spec(HS * DH, 0, 1, True)]
    out_specs = [tok_spec(2 * HS * DH, 0, 0, True), tok_spec(2 * HS * DH, 0, 1, True),
                 tok_spec(HS * DH, 0, 0, True), tok_spec(HS * DH, 0, 1, True),
                 tok_spec(LANES, 0, 0, False), tok_spec(LANES, 0, 1, False)]
    qks = jax.ShapeDtypeStruct((B, S, 2 * H * DH), F32)
    vs = jax.ShapeDtypeStruct((B, S, H * DH), F32)
    gs = jax.ShapeDtypeStruct((B, S, LANES), F32)
    csf, csb, nsf, nsb, msf, msb = states
    return pl.pallas_call(
        body, name="mlstm_bwd", grid=(B, nc, H // HS), in_specs=in_specs, out_specs=out_specs,
        out_shape=[qks, qks, vs, vs, gs, gs],
        scratch_shapes=[pltpu.VMEM((2, H, DH, DH), F32), pltpu.VMEM((2, H, 1, DH), F32), pltpu.VMEM((2, H, 1, DH), F32)],
        compiler_params=_cparams(("parallel", "arbitrary", "arbitrary")),
    )(qk, qk, proj3, proj3, proj3, proj3, bias, csf, csb, nsf, nsb, msf, msb, dh, dh)


ROW_BLOCK = 256
FF_COLS = 512
FF_SHARD = D_FF // N_DEV
FF_SHARD_PAD = 384
FF_PAD = N_DEV * FF_SHARD_PAD


def _rms_fwd(name, x, g):
    T = x.shape[0]
    return _rowwise(name, lambda xv, gv: _rms(xv, gv), [_In(x), _In(g, rows=False)], [_Out(D_MODEL, BF16)], T, ROW_BLOCK)[0]


def _rms_bwd(name, x, g, dh, dres):
    T = x.shape[0]

    def fn(xv, gv, dhv, drv):
        _, vjp = jax.vjp(_rms, xv, gv)
        dx, dg = vjp(dhv)
        return drv + dx, dg

    return _rowwise(name, fn, [_In(x), _In(g, rows=False), _In(dh), _In(dres)],
                    [_Out(D_MODEL), _Out(D_MODEL, rows=False)], T, ROW_BLOCK)


def _mmw(name, a, w, mode, **kw):
    if isinstance(w, tuple):
        return _matmul(name, a, w[0], mode, bl=w[1], **kw)
    return _matmul(name, a, w, mode, **kw)


def _swiglu(gate, up):
    return _silu(gate) * up


def _ffn_in(name, h, wg, wu):
    (M, K), N = h.shape, wg.shape[1]
    tm, tn = _first_divisor(M, (1024, 512, 256, 128)), FF_COLS

    def body(h_ref, wg_ref, wu_ref, g_ref, u_ref, a_ref):
        hv = h_ref[...]
        gate = _mm(hv, wg_ref[...], _NN)
        up = _mm(hv, wu_ref[...], _NN)
        g_ref[...], u_ref[...] = gate, up
        a_ref[...] = _swiglu(gate, up).astype(BF16)

    w_spec = pl.BlockSpec((K, tn), lambda i, j: (0, j))
    o_spec = pl.BlockSpec((tm, tn), lambda i, j: (i, j))
    return pl.pallas_call(
        body, name=name, grid=(M // tm, N // tn), in_specs=[pl.BlockSpec((tm, K), lambda i, j: (i, 0)), w_spec, w_spec],
        out_specs=[o_spec, o_spec, o_spec],
        out_shape=[jax.ShapeDtypeStruct((M, N), F32), jax.ShapeDtypeStruct((M, N), F32), jax.ShapeDtypeStruct((M, N), BF16)],
        compiler_params=_cparams(("parallel", "parallel")),
    )(h, wg, wu)


def _ffn_dact(name, dx, wd, gate, up):
    (M, K), N = dx.shape, wd.shape[0]
    tm, tn = _first_divisor(M, (1024, 512, 256, 128)), FF_COLS

    def body(dx_ref, wd_ref, g_ref, u_ref, dg_ref, du_ref):
        dact = _mm(dx_ref[...], wd_ref[...], _NT) * 0.5
        _, vjp = jax.vjp(_swiglu, g_ref[...], u_ref[...])
        dgate, dup = vjp(dact)
        dg_ref[...], du_ref[...] = dgate.astype(BF16), dup.astype(BF16)

    o_spec = pl.BlockSpec((tm, tn), lambda i, j: (i, j))
    return pl.pallas_call(
        body, name=name, grid=(M // tm, N // tn),
        in_specs=[pl.BlockSpec((tm, K), lambda i, j: (i, 0)), pl.BlockSpec((tn, K), lambda i, j: (j, 0)), o_spec, o_spec],
        out_specs=[o_spec, o_spec],
        out_shape=[jax.ShapeDtypeStruct((M, N), BF16), jax.ShapeDtypeStruct((M, N), BF16)],
        compiler_params=_cparams(("parallel", "parallel")),
    )(dx, wd, gate, up)


def _ffn_fwd(tag, x, g, wg, wu, wd):
    h = _rms_fwd(tag + "_norm", x, g)
    gate, up, act = _ffn_in(tag + "_in", h, wg, wu)
    if callable(wd):
        wd = wd(act)
    out = _mmw(tag + "_down", act, wd, "nn", res=x, scale=0.5)
    return out, (x, h, gate, up, act), wd


def _ffn_bwd(tag, saved, g, wg, wu, wd, dx, on_dw):
    x, h, gate, up, act = saved
    dgate, dup = _ffn_dact(tag + "_dact", dx, wd, gate, up)
    dwd = _matmul(tag + "_dwd", act, dx, "tn", scale=0.5)
    dwg = _matmul(tag + "_dwg", h, dgate, "tn")
    dwu = _matmul(tag + "_dwu", h, dup, "tn")
    token = on_dw({tag + "_w_gate": dwg, tag + "_w_up": dwu, tag + "_w_down": dwd}, dwu)
    dh = _mmw(tag + "_dh1", dgate, wg, "nt", dep=token)
    dh = _mmw(tag + "_dh2", dup, wu, "nt", res=dh)
    dx_new, dg = _rms_bwd(tag + "_dnorm", x, g, dh, dx)
    return dx_new, dg


def _rope_cos_sin(positions):
    half = ROPE_DIM // 2
    inv_freq = jnp.power(jnp.float32(ROPE_THETA), -jnp.arange(half, dtype=F32) * (2.0 / ROPE_DIM))
    head = jnp.zeros((ATT_HEAD_DIM,), F32).at[:ROPE_DIM].set(jnp.concatenate([inv_freq, inv_freq]))
    row = jnp.tile(head, LANES // ATT_HEAD_DIM)[None, :]
    T = positions.shape[0]
    return _rowwise("rope_tables", _rope_tables, [_In(positions), _In(row, rows=False)], [_Out(LANES), _Out(LANES)], T, 1024)


def _prep_fwd(name, src, width, base, g, cos, sin):
    return _rowwise(name, _qk_prep, [_In(src, width, base), _In(g, rows=False), _In(cos), _In(sin)],
                    [_Out(width)], src.shape[0], 512)[0]


def _prep_bwd(name, src, width, base, g, cos, sin, dout):
    def fn(tv, gv, cv, sv, dv):
        _, vjp = jax.vjp(lambda a, b: _qk_prep(a, b, cv, sv), tv, gv)
        return vjp(dv)

    return _rowwise(name, fn, [_In(src, width, base), _In(g, rows=False), _In(cos), _In(sin), _In(dout)],
                    [_Out(width, BF16), _Out(width, rows=False)], src.shape[0], 512)


def _to_heads(t, B, S, nh):
    return t.reshape(B, S, nh, ATT_HEAD_DIM).transpose(0, 2, 1, 3)


def _from_heads(t):
    B, nh, S, _ = t.shape
    return t.transpose(0, 2, 1, 3).reshape(B * S, nh * ATT_HEAD_DIM)


def _mix_fwd(x, cos, sin, B, S, p):
    T = B * S
    h = _rms_fwd("mix_norm", x, p["mix_norm"])
    proj = _matmul("mix_proj", h, p["w_in"], "nn")
    proj3 = proj.reshape(B, S, IN_PAD)
    q_gain = jnp.tile(p["attn_q_norm"], (1, ATT_HEADS))
    k_gain = jnp.tile(p["attn_k_norm"], (1, ATT_KV_HEADS))
    q_r = _prep_fwd("q_prep", proj, ATT_WIDTH, C_QA // ATT_WIDTH, q_gain, cos, sin)
    k_r = _prep_fwd("k_prep", proj, ATT_KV_WIDTH, C_KA // ATT_KV_WIDTH, k_gain, cos, sin)
    qh = q_r.reshape(B, S, ATT_WIDTH)
    kh = k_r.reshape(B, S, ATT_KV_WIDTH)
    sink = p["attn_sink"].reshape(ATT_KV_HEADS, ATT_GROUP, 1, 1)
    y_a = _attn_fwd(qh, kh, proj3, sink).reshape(T, ATT_WIDTH)

    qk_c = _conv_fwd(proj3, p["conv_w8"])
    hf, hb, *states = _mlstm_fwd(qk_c, proj3, p["gate_bias"])
    hf2, hb2 = hf.reshape(T, MLSTM_WIDTH), hb.reshape(T, MLSTM_WIDTH)
    DH = MLSTM_HEAD_DIM
    y_m = _rowwise("mlstm_out", _mlstm_combine,
                   [_In(hf2, DH, split=True), _In(hb2, DH, split=True), _In(proj, DH, C_OM // DH, split=True),
                    _In(p["mlstm_out_norm"], DH, split=True, rows=False)],
                   [_Out(MLSTM_WIDTH, BF16, DH, split=True)], T, 1024, ncol=MLSTM_HEADS)[0]

    za = _mmw("branch_a", y_a, p["w_branch_attn"], "nn")
    zm = _mmw("branch_m", y_m, p["w_branch_mlstm"], "nn")
    W = 512
    merged = _rowwise("merge", _merge,
                      [_In(proj, W, C_GMERGE // W, split=True), _In(proj, W, (C_GMERGE + D_MODEL) // W, split=True),
                       _In(za, W, split=True), _In(zm, W, split=True)],
                      [_Out(D_MODEL, BF16, W, split=True)], T, 512, ncol=D_MODEL // W)[0]
    out = _mmw("mix_out", merged, p["w_out"], "nn", res=x)
    saved = dict(x=x, h=h, proj=proj, q_gain=q_gain, k_gain=k_gain, qh=qh, kh=kh, sink=sink, y_a=y_a, qk_c=qk_c,
                 hf=hf2, hb=hb2, states=states, y_m=y_m, za=za, zm=zm, merged=merged)
    return out, saved


def _mix_bwd(sv, cos, sin, B, S, p, dx, on_dw):
    T = B * S
    DH = MLSTM_HEAD_DIM
    proj = sv["proj"]
    proj3 = proj.reshape(B, S, IN_PAD)
    g = {}
    dmerged = _mmw("mix_dmerged", dx, p["w_out"], "nt")
    g["w_out"] = _matmul("mix_dwout", sv["merged"], dx, "tn")
    W = 512

    def merge_bwd(ga, gm, za, zm, dm):
        _, vjp = jax.vjp(_merge, ga, gm, za, zm)
        return vjp(dm)

    dga, dgm, dza, dzm = _rowwise(
        "merge_bwd", merge_bwd,
        [_In(proj, W, C_GMERGE // W, split=True), _In(proj, W, (C_GMERGE + D_MODEL) // W, split=True),
         _In(sv["za"], W, split=True), _In(sv["zm"], W, split=True), _In(dmerged, W, split=True)],
        [_Out(D_MODEL, BF16, W, split=True), _Out(D_MODEL, BF16, W, split=True),
         _Out(D_MODEL, BF16, W, split=True), _Out(D_MODEL, BF16, W, split=True)], T, 512, ncol=D_MODEL // W)
    dya = _mmw("branch_a_dx", dza, p["w_branch_attn"], "nt")
    g["w_branch_attn"] = _matmul("branch_a_dw", sv["y_a"], dza, "tn")
    dym = _mmw("branch_m_dx", dzm, p["w_branch_mlstm"], "nt")
    g["w_branch_mlstm"] = _matmul("branch_m_dw", sv["y_m"], dzm, "tn")

    def combine_bwd(hf, hb, o_pre, gn, dy):
        _, vjp = jax.vjp(_mlstm_combine, hf, hb, o_pre, gn)
        dhf, _, do, dg = vjp(dy)
        return dhf, do, dg

    dh, dom, g["mlstm_out_norm"] = _rowwise(
        "mlstm_out_bwd", combine_bwd,
        [_In(sv["hf"], DH, split=True), _In(sv["hb"], DH, split=True), _In(proj, DH, C_OM // DH, split=True),
         _In(p["mlstm_out_norm"], DH, split=True, rows=False), _In(dym, DH, split=True)],
        [_Out(MLSTM_WIDTH, F32, DH, split=True), _Out(MLSTM_WIDTH, BF16, DH, split=True),
         _Out(MLSTM_WIDTH, F32, DH, split=True, rows=False)], T, 1024, ncol=MLSTM_HEADS)
    dqk_f, dqk_b, dv_f, dv_b, dg_f, dg_b = _mlstm_bwd(sv["qk_c"], proj3, p["gate_bias"], sv["states"],
                                                       dh.reshape(B, S, MLSTM_WIDTH))
    dgates, dvm, g["gate_bias"] = _rowwise(
        "mlstm_dsum", lambda a, b, c, d: (a + b, c + d, jnp.sum(a + b, axis=0, keepdims=True)),
        [_In(dg_f.reshape(T, LANES)), _In(dg_b.reshape(T, LANES)), _In(dv_f.reshape(T, MLSTM_WIDTH)), _In(dv_b.reshape(T, MLSTM_WIDTH))],
        [_Out(LANES, BF16), _Out(MLSTM_WIDTH, BF16), _Out(LANES, rows=False)], T, 1024)
    dqk, g["conv_w8"] = _conv_bwd(proj3, p["conv_w8"], dqk_f, dqk_b)

    dqh, dkh, dvh, dsink = _attn_bwd(sv["qh"], sv["kh"], proj3, sv["sink"], dya.reshape(B, S, ATT_WIDTH))
    g["attn_sink"] = dsink.reshape(1, ATT_HEADS)
    dva = dvh.reshape(T, ATT_KV_WIDTH)
    dqa, dq_gain = _prep_bwd("q_prep_bwd", proj, ATT_WIDTH, C_QA // ATT_WIDTH, sv["q_gain"], cos, sin,
                             dqh.reshape(T, ATT_WIDTH))
    dka, dk_gain = _prep_bwd("k_prep_bwd", proj, ATT_KV_WIDTH, C_KA // ATT_KV_WIDTH, sv["k_gain"], cos, sin,
                             dkh.reshape(T, ATT_KV_WIDTH))
    g["attn_q_norm"] = jnp.sum(dq_gain.reshape(ATT_HEADS, ATT_HEAD_DIM), axis=0, keepdims=True)
    g["attn_k_norm"] = jnp.sum(dk_gain.reshape(ATT_KV_HEADS, ATT_HEAD_DIM), axis=0, keepdims=True)

    dproj = jnp.concatenate(
        [dga, dgm, dqk.reshape(T, 2 * MLSTM_WIDTH), dvm, dom, dqa, dka, dva.astype(BF16), dgates], axis=1)
    dwin = _matmul("mix_dwin", sv["h"], dproj, "tn")
    token = on_dw({"w_in": _w_in_to_slots(dwin), "w_branch_attn": g.pop("w_branch_attn"),
                   "w_branch_mlstm": g.pop("w_branch_mlstm"), "w_out": g.pop("w_out")}, dwin)
    dh2 = _matmul("mix_dh", dproj, p["w_in"], "nt", dep=token)
    dx_new, g["mix_norm"] = _rms_bwd("mix_dnorm", sv["x"], p["mix_norm"], dh2, dx)
    return dx_new, g


def _loss_and_grad(x, g, target):
    T = x.shape[0]

    def loss_fn(xv, gv, tv):
        err = jnp.square(_rms(xv, gv) - tv)
        return 0.5 * jnp.sum(jnp.mean(err, axis=-1, keepdims=True), axis=0, keepdims=True)

    def fn(xv, gv, tv):
        val, vjp = jax.vjp(lambda a, b: loss_fn(a, b, tv), xv, gv)
        dx, dg = vjp(jnp.ones((1, 1), F32))
        return val, dx, dg

    return _rowwise("loss_head", fn, [_In(x), _In(g, rows=False), _In(target)],
                    [_Out(1, rows=False), _Out(D_MODEL), _Out(D_MODEL, rows=False)], T, ROW_BLOCK)


def _block_norm_fwd(x, g):
    T = x.shape[0]
    return _rowwise("block_norm", _rms, [_In(x), _In(g, rows=False)], [_Out(D_MODEL)], T, ROW_BLOCK)[0]


def _block_norm_bwd(x, g, dy):
    T = x.shape[0]

    def fn(xv, gv, dv):
        _, vjp = jax.vjp(_rms, xv, gv)
        return vjp(dv)

    return _rowwise("block_norm_bwd", fn, [_In(x), _In(g, rows=False), _In(dy)],
                    [_Out(D_MODEL), _Out(D_MODEL, rows=False)], T, ROW_BLOCK)


def _qk_perm_cols(t, axis):
    q, k = jnp.split(t, 2, axis=axis)
    parts = []
    for h in range(MLSTM_HEADS):
        sl = [slice(None)] * t.ndim
        sl[axis] = slice(h * MLSTM_HEAD_DIM, (h + 1) * MLSTM_HEAD_DIM)
        parts += [q[tuple(sl)], k[tuple(sl)]]
    return jnp.concatenate(parts, axis=axis)


def _qk_unperm_cols(t, axis):
    qs, ks = [], []
    for h in range(MLSTM_HEADS):
        sl = [slice(None)] * t.ndim
        sl[axis] = slice(2 * h * MLSTM_HEAD_DIM, (2 * h + 1) * MLSTM_HEAD_DIM)
        qs.append(t[tuple(sl)])
        sl[axis] = slice((2 * h + 1) * MLSTM_HEAD_DIM, (2 * h + 2) * MLSTM_HEAD_DIM)
        ks.append(t[tuple(sl)])
    return jnp.concatenate(qs + ks, axis=axis)


def _w_in_arrange(w):
    qa, ka, va, qm, km, vm, om, gm, gmerge = jnp.split(w, np.cumsum(
        (ATT_WIDTH, ATT_KV_WIDTH, ATT_KV_WIDTH, MLSTM_WIDTH, MLSTM_WIDTH, MLSTM_WIDTH, MLSTM_WIDTH, MLSTM_N_GATES))[:].tolist(), axis=1)
    qk = _qk_perm_cols(jnp.concatenate([qm, km], axis=1), 1)
    pad = jnp.zeros((w.shape[0], LANES - MLSTM_N_GATES), w.dtype)
    return jnp.concatenate([gmerge, qk, vm, om, qa, ka, va, gm, pad], axis=1)


def _w_in_restore(w):
    gmerge = w[:, C_GMERGE:C_GMERGE + 2 * D_MODEL]
    qk = _qk_unperm_cols(w[:, C_QK:C_QK + 2 * MLSTM_WIDTH], 1)
    vm, om = w[:, C_VM:C_VM + MLSTM_WIDTH], w[:, C_OM:C_OM + MLSTM_WIDTH]
    qa, ka, va = w[:, C_QA:C_QA + ATT_WIDTH], w[:, C_KA:C_KA + ATT_KV_WIDTH], w[:, C_VA:C_VA + ATT_KV_WIDTH]
    gm = w[:, C_GATES:C_GATES + MLSTM_N_GATES]
    return jnp.concatenate([qa, ka, va, qk, vm, om, gm, gmerge], axis=1)


BIG = ("ffn1_w_gate", "ffn1_w_up", "ffn1_w_down", "w_in", "mlstm_conv_w", "w_branch_attn", "w_branch_mlstm", "w_out",
       "ffn2_w_gate", "ffn2_w_up", "ffn2_w_down")
MATMUL_W = tuple(n for n in BIG if n != "mlstm_conv_w")
SMALL = ("ffn1_norm", "mix_norm", "mlstm_gate_bias", "attn_q_norm", "attn_k_norm", "attn_sink", "mlstm_conv_b",
         "mlstm_out_norm", "ffn2_norm", "block_out_norm")
WEIGHTS = ("ffn1_norm", "ffn1_w_gate", "ffn1_w_up", "ffn1_w_down", "mix_norm", "w_in", "mlstm_gate_bias", "attn_q_norm",
           "attn_k_norm", "attn_sink", "mlstm_conv_w", "mlstm_conv_b", "mlstm_out_norm", "w_branch_attn", "w_branch_mlstm",
           "w_out", "ffn2_norm", "ffn2_w_gate", "ffn2_w_up", "ffn2_w_down", "block_out_norm")
PACK_COLS = 1024


def _padded_rows(n_elems):
    return -(-n_elems // PACK_COLS)


def _pack_flat(arrs, dtype, row_multiple):
    parts = []
    for a in arrs:
        flat = a.reshape(-1).astype(dtype)
        pad = _padded_rows(flat.shape[0]) * PACK_COLS - flat.shape[0]
        parts.append(jnp.pad(flat, (0, pad)) if pad else flat)
    flat = jnp.concatenate(parts)
    rows = flat.shape[0] // PACK_COLS
    extra = (-rows) % row_multiple
    if extra:
        flat = jnp.pad(flat, (0, extra * PACK_COLS))
    return flat.reshape(-1, PACK_COLS)


def _unpack_flat(buf, shapes, lead=()):
    flat = buf.reshape(lead + (-1,))
    out, off = [], 0
    for s in shapes:
        n = int(np.prod(s))
        out.append(flat[..., off:off + n].reshape(lead + tuple(s)))
        off += _padded_rows(n) * PACK_COLS
    return out


class _Lay:
    def __init__(self, shard, axis, width):
        self.shard, self.axis, self.width = shard, axis, width
        self.padded = tuple(width if a == axis else s for a, s in enumerate(shard))
        self.whole = tuple(N_DEV * width if a == axis else s for a, s in enumerate(shard))

    def pad(self, t, lead=0):
        extra = self.width - self.shard[self.axis]
        if not extra:
            return t
        cfg = [(0, 0)] * t.ndim
        cfg[lead + self.axis] = (0, extra)
        return jnp.pad(t, cfg)

    def unpad(self, t, lead=0):
        idx = [slice(None)] * t.ndim
        idx[lead + self.axis] = slice(0, self.shard[self.axis])
        return t[tuple(idx)]


_FF_COL = _Lay((D_MODEL, FF_SHARD), 1, FF_SHARD_PAD)
_FF_ROW = _Lay((FF_SHARD, D_MODEL), 0, FF_SHARD_PAD)
LAYOUTS = {
    "ffn1_w_gate": _FF_COL, "ffn1_w_up": _FF_COL, "ffn1_w_down": _FF_ROW,
    "ffn2_w_gate": _FF_COL, "ffn2_w_up": _FF_COL, "ffn2_w_down": _FF_ROW,
    "w_in": _Lay((D_MODEL, IN_WIDTH // N_DEV), 0, D_MODEL),
    "mlstm_conv_w": _Lay((3, 2 * MLSTM_WIDTH // N_DEV), 1, 2 * MLSTM_WIDTH // N_DEV),
    "w_branch_attn": _Lay((ATT_WIDTH, D_MODEL // N_DEV), 1, D_MODEL // N_DEV),
    "w_branch_mlstm": _Lay((MLSTM_WIDTH, D_MODEL // N_DEV), 1, D_MODEL // N_DEV),
    "w_out": _Lay((D_MODEL // N_DEV, D_MODEL), 0, D_MODEL // N_DEV),
}


def _window(ref, axis, j, width):
    idx = [slice(None)] * len(ref.shape)
    idx[axis] = pl.ds(pl.multiple_of(j * width, width), width)
    return ref.at[tuple(idx)]


ANY = pl.BlockSpec(memory_space=pl.ANY)


def _mesh_pos():
    return lax.axis_index("x"), lax.axis_index("y"), lax.axis_index("c")


def _all_gather(name, shard, vmem=False):
    R, C = shard.shape
    space = pl.BlockSpec(memory_space=pltpu.VMEM) if vmem else ANY

    def body(x_ref, out_ref, send_sems, recv_sems, local_sem):
        x, y, c = _mesh_pos()
        me, sibling = (x, y, c), (x, y, 1 - c)
        chips = [(1 - x, y), (x, 1 - y), (1 - x, 1 - y)]

        def slot(px, py, pc):
            return out_ref.at[4 * px + 2 * py + pc]

        def copy(k, block, to, src=None):
            return pltpu.make_async_remote_copy(
                src_ref=slot(*block) if src is None else src, dst_ref=slot(*block),
                send_sem=send_sems.at[k], recv_sem=recv_sems.at[k], device_id=to, device_id_type=MESH)

        mine = pltpu.make_async_copy(x_ref, slot(*me), local_sem)
        mine.start()
        first = [copy(0, me, sibling, src=x_ref)]
        first += [copy(1 + j, me, (*chip, c), src=x_ref) for j, chip in enumerate(chips)]
        for cp in first:
            cp.start()
        passed = [copy(4 + j, (*chip, c), sibling) for j, chip in enumerate(chips)]
        for j, chip in enumerate(chips):
            copy(1 + j, (*chip, c), me).wait_recv()
            passed[j].start()
        copy(0, sibling, me).wait_recv()
        for j, chip in enumerate(chips):
            copy(4 + j, (*chip, 1 - c), me).wait_recv()
        for cp in first + passed:
            cp.wait_send()
        mine.wait()

    return pl.pallas_call(
        body, name=name, out_shape=jax.ShapeDtypeStruct((N_DEV, R, C), shard.dtype),
        in_specs=[space], out_specs=space,
        scratch_shapes=[pltpu.SemaphoreType.DMA((7,)), pltpu.SemaphoreType.DMA((7,)), pltpu.SemaphoreType.DMA],
    )(shard)


HBM = pl.BlockSpec(memory_space=pltpu.HBM)
SEM = pl.BlockSpec(memory_space=pltpu.SEMAPHORE)
SPLIT_COPY = pltpu.CompilerParams(has_side_effects=pltpu.SideEffectType.DATAFLOW_SIDE_EFFECTING)
N_PEERS = N_DEV - 1


def _peers(x, y, c):
    return [(x, y, 1 - c), (1 - x, y, c), (x, 1 - y, c), (1 - x, 1 - y, c),
            (1 - x, y, 1 - c), (x, 1 - y, 1 - c), (1 - x, 1 - y, 1 - c)]


def _dev_index(pos):
    return 4 * pos[0] + 2 * pos[1] + pos[2]


def _place_own(name, shards, lays):
    nt = len(shards)
    me = _dev_index(_mesh_pos())

    def body(me_ref, *refs):
        for x_ref, o_ref in zip(refs[:nt], refs[nt:]):
            o_ref[...] = x_ref[...]

    def window_spec(lay):
        if lay.axis == 0:
            return pl.BlockSpec(lay.padded, lambda i, me_ref: (me_ref[0], 0))
        return pl.BlockSpec(lay.padded, lambda i, me_ref: (0, me_ref[0]))

    return pl.pallas_call(
        body, name=name,
        grid_spec=pltpu.PrefetchScalarGridSpec(
            num_scalar_prefetch=1, grid=(1,),
            in_specs=[pl.BlockSpec(lay.padded, lambda i, me_ref: (0, 0)) for lay in lays],
            out_specs=[window_spec(lay) for lay in lays]),
        out_shape=[jax.ShapeDtypeStruct(lay.whole, s.dtype) for s, lay in zip(shards, lays)],
        compiler_params=_cparams(("arbitrary",)),
    )(me.reshape(1).astype(jnp.int32), *shards)


def _gather_start(name, shards, lands, lays, groups, after):
    nt, ng = len(shards), len(groups)

    def body(*refs):
        x_refs, land_refs = refs[:nt], refs[nt:2 * nt]
        sems = refs[2 * nt + 1:2 * nt + 1 + 2 * ng]
        pos = _mesh_pos()
        me = _dev_index(pos)
        for g, tens in enumerate(groups):
            for i, t in enumerate(tens):
                for k, peer in enumerate(_peers(*pos)):
                    pltpu.make_async_remote_copy(
                        src_ref=x_refs[t], dst_ref=_window(land_refs[t], lays[t].axis, me, lays[t].width),
                        send_sem=sems[2 * g].at[N_PEERS * i + k], recv_sem=sems[2 * g + 1].at[N_PEERS * i + k],
                        device_id=peer, device_id_type=MESH).start()

    sem_shapes = []
    for tens in groups:
        sem_shapes += [pltpu.SemaphoreType.DMA((N_PEERS * len(tens),))] * 2
    thru = [pltpu.HBM(s.shape, s.dtype) for s in shards] + [pltpu.HBM(lay.whole, s.dtype) for s, lay in zip(shards, lays)]
    args = [pltpu.with_memory_space_constraint(s, pltpu.HBM) for s in shards]
    args += [pltpu.with_memory_space_constraint(ld, pltpu.HBM) for ld in lands]
    res = pl.pallas_call(
        body, name=name, out_shape=tuple(sem_shapes + thru), in_specs=[HBM] * (2 * nt) + [ANY],
        out_specs=tuple([SEM] * (2 * ng) + [HBM] * (2 * nt)),
        input_output_aliases={t: 2 * ng + t for t in range(2 * nt)}, compiler_params=SPLIT_COPY,
    )(*args, after)
    sems = [(res[2 * g], res[2 * g + 1]) for g in range(ng)]
    return sems, list(res[2 * ng:2 * ng + nt]), list(res[2 * ng + nt:])


def _gather_wait(name, sems, shards, lands, lays, after):
    nt = len(shards)
    send_sems, recv_sems = sems

    def body(*refs):
        x_refs, land_refs = refs[:nt], refs[nt:2 * nt]
        send_ref, recv_ref = refs[2 * nt], refs[2 * nt + 1]
        pos = _mesh_pos()
        for t in range(nt):
            for k, peer in enumerate(_peers(*pos)):
                cp = pltpu.make_async_remote_copy(
                    src_ref=x_refs[t], dst_ref=_window(land_refs[t], lays[t].axis, _dev_index(peer), lays[t].width),
                    send_sem=send_ref.at[N_PEERS * t + k], recv_sem=recv_ref.at[N_PEERS * t + k],
                    device_id=peer, device_id_type=MESH)
                cp.wait_send()
                cp.wait_recv()

    thru = [pltpu.HBM(s.shape, s.dtype) for s in shards] + [pltpu.HBM(ld.shape, ld.dtype) for ld in lands]
    res = pl.pallas_call(
        body, name=name, out_shape=tuple(thru), in_specs=[HBM] * (2 * nt) + [SEM, SEM, ANY],
        out_specs=tuple([HBM] * (2 * nt)), input_output_aliases={t: t for t in range(2 * nt)},
        compiler_params=SPLIT_COPY,
    )(*shards, *lands, send_sems, recv_sems, after)
    return list(res[nt:])


def _pair_exchange(name, grads, lays):
    nt = len(grads)

    def body(*refs):
        g_refs, land_refs = refs[:nt], refs[nt:2 * nt]
        send_sems, recv_sems = refs[2 * nt:]
        x, y, c = _mesh_pos()
        copies = []
        for t in range(nt):
            for chip in range(4):
                copies.append(pltpu.make_async_remote_copy(
                    src_ref=_window(g_refs[t], lays[t].axis, 2 * chip + (1 - c), lays[t].width), dst_ref=land_refs[t].at[chip],
                    send_sem=send_sems.at[4 * t + chip], recv_sem=recv_sems.at[4 * t + chip],
                    device_id=(x, y, 1 - c), device_id_type=MESH))
        for cp in copies:
            cp.start()
        for cp in copies:
            cp.wait_recv()
        for cp in copies:
            cp.wait_send()

    out_shape = [jax.ShapeDtypeStruct((4,) + lay.padded, g.dtype) for g, lay in zip(grads, lays)]
    return pl.pallas_call(
        body, name=name, out_shape=out_shape, in_specs=[ANY] * nt, out_specs=[ANY] * nt,
        scratch_shapes=[pltpu.SemaphoreType.DMA((4 * nt,)), pltpu.SemaphoreType.DMA((4 * nt,))],
    )(*grads)


def _pair_sum(name, whole, landed, lay, out_dtype):
    R, C = lay.padded
    br = _first_divisor(R, (512, 384, 256, 128, 64, 32, 16, 8))
    nb = R // br
    if lay.axis == 0:
        mine_spec = pl.BlockSpec((br, C), lambda k, i, c_ref: ((2 * k + c_ref[0]) * nb + i, 0))
    else:
        mine_spec = pl.BlockSpec((br, C), lambda k, i, c_ref: (i, 2 * k + c_ref[0]))

    def body(c_ref, mine_ref, sib_ref, o_ref):
        o_ref[0] = (mine_ref[...] + sib_ref[0]).astype(out_dtype)

    c = lax.axis_index("c")
    return pl.pallas_call(
        body, name=name,
        grid_spec=pltpu.PrefetchScalarGridSpec(
            num_scalar_prefetch=1, grid=(4, nb),
            in_specs=[mine_spec, pl.BlockSpec((1, br, C), lambda k, i, c_ref: (k, i, 0))],
            out_specs=pl.BlockSpec((1, br, C), lambda k, i, c_ref: (k, i, 0))),
        out_shape=jax.ShapeDtypeStruct((4, R, C), out_dtype),
        compiler_params=_cparams(("parallel", "parallel")),
    )(c.reshape(1).astype(jnp.int32), whole, landed)


def _chip_exchange(name, sums):
    nt = len(sums)

    def body(*refs):
        s_refs, land_refs = refs[:nt], refs[nt:2 * nt]
        send_sems, recv_sems, local_sems = refs[2 * nt:]
        x, y, c = _mesh_pos()
        my_chip = 2 * x + y
        mine = [pltpu.make_async_copy(s_refs[t].at[my_chip], land_refs[t].at[my_chip], local_sems.at[t]) for t in range(nt)]
        for cp in mine:
            cp.start()
        chips = [(1 - x, y), (x, 1 - y), (1 - x, 1 - y)]
        copies = []
        for t in range(nt):
            for j, (px, py) in enumerate(chips):
                copies.append(pltpu.make_async_remote_copy(
                    src_ref=s_refs[t].at[2 * px + py], dst_ref=land_refs[t].at[my_chip],
                    send_sem=send_sems.at[3 * t + j], recv_sem=recv_sems.at[3 * t + j],
                    device_id=(px, py, c), device_id_type=MESH))
        for cp in copies:
            cp.start()
        for t in range(nt):
            for j, (px, py) in enumerate(chips):
                pltpu.make_async_remote_copy(
                    src_ref=s_refs[t].at[my_chip], dst_ref=land_refs[t].at[2 * px + py],
                    send_sem=send_sems.at[3 * t + j], recv_sem=recv_sems.at[3 * t + j],
                    device_id=(px, py, c), device_id_type=MESH).wait_recv()
        for cp in copies:
            cp.wait_send()
        for cp in mine:
            cp.wait()

    return pl.pallas_call(
        body, name=name, out_shape=[jax.ShapeDtypeStruct(s.shape, s.dtype) for s in sums],
        in_specs=[ANY] * nt, out_specs=[ANY] * nt,
        scratch_shapes=[pltpu.SemaphoreType.DMA((3 * nt,)), pltpu.SemaphoreType.DMA((3 * nt,)), pltpu.SemaphoreType.DMA((nt,))],
    )(*sums)


def _chip_start(name, sums):
    nt = len(sums)

    def body(*refs):
        s_refs, land_refs = refs[:nt], refs[nt:2 * nt]
        send_sems, recv_sems = refs[2 * nt], refs[2 * nt + 1]
        x, y, c = _mesh_pos()
        my_chip = 2 * x + y
        for t in range(nt):
            for j, (px, py) in enumerate([(1 - x, y), (x, 1 - y), (1 - x, 1 - y)]):
                pltpu.make_async_remote_copy(
                    src_ref=s_refs[t].at[2 * px + py], dst_ref=land_refs[t].at[my_chip],
                    send_sem=send_sems.at[3 * t + j], recv_sem=recv_sems.at[3 * t + j],
                    device_id=(px, py, c), device_id_type=MESH).start()

    thru = [pltpu.HBM(s.shape, s.dtype) for s in sums] * 2
    args = [pltpu.with_memory_space_constraint(s, pltpu.HBM) for s in sums]
    args += [pltpu.with_memory_space_constraint(lax.empty(s.shape, s.dtype), pltpu.HBM) for s in sums]
    res = pl.pallas_call(
        body, name=name, out_shape=tuple([pltpu.SemaphoreType.DMA((3 * nt,))] * 2 + thru), in_specs=[HBM] * (2 * nt),
        out_specs=tuple([SEM, SEM] + [HBM] * (2 * nt)), input_output_aliases={t: 2 + t for t in range(2 * nt)},
        compiler_params=SPLIT_COPY,
    )(*args)
    return (res[0], res[1]), list(res[2:2 + nt]), list(res[2 + nt:])


def _chip_wait(name, sems, sums, lands, after):
    nt = len(sums)

    def body(*refs):
        s_refs, land_refs = refs[:nt], refs[nt:2 * nt]
        send_sems, recv_sems = refs[2 * nt], refs[2 * nt + 1]
        x, y, c = _mesh_pos()
        my_chip = 2 * x + y
        for t in range(nt):
            for j, (px, py) in enumerate([(1 - x, y), (x, 1 - y), (1 - x, 1 - y)]):
                cp = pltpu.make_async_remote_copy(
                    src_ref=s_refs[t].at[my_chip], dst_ref=land_refs[t].at[2 * px + py],
                    send_sem=send_sems.at[3 * t + j], recv_sem=recv_sems.at[3 * t + j],
                    device_id=(px, py, c), device_id_type=MESH)
                cp.wait_send()
                cp.wait_recv()

    thru = [pltpu.HBM(s.shape, s.dtype) for s in sums] * 2
    res = pl.pallas_call(
        body, name=name, out_shape=tuple(thru), in_specs=[HBM] * (2 * nt) + [SEM, SEM, ANY],
        out_specs=tuple([HBM] * (2 * nt)), input_output_aliases={t: t for t in range(2 * nt)},
        compiler_params=SPLIT_COPY,
    )(*sums, *lands, sems[0], sems[1], after)
    return list(res[:nt]), list(res[nt:])


def _sum_chips(name, own, landed):
    _, R, C = own.shape
    br = _first_divisor(R, (512, 384, 256, 128, 64, 32, 16, 8))
    x, y, _ = _mesh_pos()
    slots = jnp.stack([2 * x + y, 2 * (1 - x) + y, 2 * x + (1 - y), 2 * (1 - x) + (1 - y)]).astype(jnp.int32)

    def body(slot_ref, mine_ref, a_ref, b_ref, c_ref, o_ref):
        o_ref[...] = ((mine_ref[0].astype(F32) + a_ref[0].astype(F32)) + b_ref[0].astype(F32)) + c_ref[0].astype(F32)

    def slot_spec(j):
        return pl.BlockSpec((1, br, C), lambda i, slot_ref: (slot_ref[j], i, 0))

    return pl.pallas_call(
        body, name=name,
        grid_spec=pltpu.PrefetchScalarGridSpec(
            num_scalar_prefetch=1, grid=(R // br,), in_specs=[slot_spec(0), slot_spec(1), slot_spec(2), slot_spec(3)],
            out_specs=pl.BlockSpec((br, C), lambda i, slot_ref: (i, 0))),
        out_shape=jax.ShapeDtypeStruct((R, C), F32), compiler_params=_cparams(("parallel",)),
    )(slots, own, landed, landed, landed)


def _sum_slots(name, slots, n):
    _, R, C = slots.shape
    br = _first_divisor(R, (512, 384, 256, 128, 64, 32, 16, 8))

    def body(s_ref, o_ref):
        acc = s_ref[0].astype(F32)
        for k in range(1, n):
            acc = acc + s_ref[k].astype(F32)
        o_ref[...] = acc

    return pl.pallas_call(
        body, name=name, grid=(R // br,), in_specs=[pl.BlockSpec((n, br, C), lambda i: (0, i, 0))],
        out_specs=pl.BlockSpec((br, C), lambda i: (i, 0)), out_shape=jax.ShapeDtypeStruct((R, C), F32),
        compiler_params=_cparams(("parallel",)),
    )(slots)


def _reduce_scatter_start(tag, names, grads):
    lays = [LAYOUTS[n] for n in names]
    landed = _pair_exchange("grads_pair_" + names[0], grads, lays)
    sums = [_pair_sum("grads_pairsum_" + n, g, ld, lay, BF16) for n, g, ld, lay in zip(names, grads, landed, lays)]
    sems, sums, lands = _chip_start(tag + "_chips_start", sums)
    return tag, names, sems, sums, lands


def _reduce_scatter_finish(pending, after):
    tag, names, sems, sums, lands = pending
    own, got = _chip_wait(tag + "_chips_wait", sems, sums, lands, after)
    return [_sum_chips("grads_sum_" + n, o, s) for n, o, s in zip(names, own, got)]


def _adamw_math(w, g, m, v):
    m = ADAM_B1 * m + (1.0 - ADAM_B1) * g
    v = ADAM_B2 * v + (1.0 - ADAM_B2) * jnp.square(g)
    m_hat = m / (1.0 - ADAM_B1 ** ADAM_STEP)
    v_hat = v / (1.0 - ADAM_B2 ** ADAM_STEP)
    delta = -ADAM_LR * (m_hat / (jnp.sqrt(v_hat) + ADAM_EPS) + ADAM_WD * w)
    return delta, m, v


def _adamw_layers(name, w, totals, m, v):
    _, R, C = w.shape
    br = _first_divisor(R, (512, 176, 128, 64, 32, 16, 8))
    Cp = totals[0].shape[1]

    def body(w_ref, g0_ref, g1_ref, m_ref, v_ref, g_out, d_out, m_out, v_out):
        g = jnp.where(pl.program_id(0) == 0, g0_ref[:, 0:C], g1_ref[:, 0:C])
        delta, m_new, v_new = _adamw_math(w_ref[0], g, m_ref[0], v_ref[0])
        g_out[0], d_out[0], m_out[0], v_out[0] = g, delta, m_new, v_new

    blk = pl.BlockSpec((1, br, C), lambda l, i: (l, i, 0))
    g_spec = pl.BlockSpec((br, Cp), lambda l, i: (i, 0))
    return pl.pallas_call(
        body, name=name, grid=(DEPTH, R // br), in_specs=[blk, g_spec, g_spec, blk, blk], out_specs=[blk] * 4,
        out_shape=[jax.ShapeDtypeStruct(w.shape, F32)] * 4, compiler_params=_cparams(("parallel", "parallel")),
    )(w, totals[0], totals[1], m, v)


def _adamw(name, w, g, m, v):
    shape = w.shape
    cols = shape[-1]
    rows = int(np.prod(shape[:-1]))
    br = _first_divisor(rows, (512, 352, 256, 128, 64, 32, 16, 8))
    args = [_In(a.reshape(rows, cols)) for a in (w, g, m, v)]
    outs = _rowwise(name, _adamw_math, args, [_Out(cols), _Out(cols), _Out(cols)], rows, br)
    return [o.reshape(shape) for o in outs]


GROUPS = {"ffn1": ("ffn1_w_gate", "ffn1_w_up", "ffn1_w_down"),
          "mix": ("w_in", "w_branch_attn", "w_branch_mlstm", "w_out"),
          "ffn2": ("ffn2_w_gate", "ffn2_w_up", "ffn2_w_down")}
GATHER_GROUPS = {"ffn1_in": ("ffn1_w_gate", "ffn1_w_up"), "ffn1_out": ("ffn1_w_down",),
                 "mix": ("w_in", "w_branch_attn", "w_branch_mlstm", "w_out"),
                 "ffn2_in": ("ffn2_w_gate", "ffn2_w_up"), "ffn2_out": ("ffn2_w_down",)}


def _small_params(small, conv_w, l):
    p = {}
    for n in ("ffn1_norm", "mix_norm", "ffn2_norm", "block_out_norm", "mlstm_out_norm", "attn_q_norm", "attn_k_norm"):
        p[n] = small[n][l][None, :]
    p["attn_sink"] = small["attn_sink"][l]
    p["gate_bias"] = jnp.pad(small["mlstm_gate_bias"][l], (0, LANES - MLSTM_N_GATES))[None, :]
    taps = _qk_perm_cols(conv_w[l], 1)
    conv_b = _qk_perm_cols(small["mlstm_conv_b"][l][None, :], 1)
    p["conv_w8"] = jnp.concatenate([taps, conv_b, jnp.zeros((4, 2 * MLSTM_WIDTH), F32)], axis=0)
    return p


def _w_in_from_slots(slots):
    w_in = slots.reshape(N_DEV, D_MODEL, IN_WIDTH // N_DEV).transpose(1, 0, 2).reshape(D_MODEL, IN_WIDTH)
    return _w_in_arrange(w_in)


def _w_in_to_slots(g):
    return _w_in_restore(g).reshape(D_MODEL, N_DEV, IN_WIDTH // N_DEV).transpose(1, 0, 2).reshape(
        N_DEV * D_MODEL, IN_WIDTH // N_DEV)


def _local_step(x, positions, target, weights_of, small, conv_w, on_grads):
    B, S, _ = x.shape
    T = B * S
    cos, sin = _rope_cos_sin(positions.reshape(T, 1))
    params = [_small_params(small, conv_w, l) for l in range(DEPTH)]
    xs = x.reshape(T, D_MODEL)
    tgt = target.reshape(T, D_MODEL)

    saved = []
    for l, p in enumerate(params):
        p.update(weights_of(l, "ffn1_in", xs))
        x1, s1, p["ffn1_w_down"] = _ffn_fwd("ffn1", xs, p["ffn1_norm"], p["ffn1_w_gate"], p["ffn1_w_up"],
                                            lambda after, l=l: weights_of(l, "ffn1_out", after)["ffn1_w_down"])
        p.update(weights_of(l, "mix", x1))
        p["w_in"] = _w_in_from_slots(p["w_in"])
        x2, s2 = _mix_fwd(x1, cos, sin, B, S, p)
        p.update(weights_of(l, "ffn2_in", x2))
        x3, s3, p["ffn2_w_down"] = _ffn_fwd("ffn2", x2, p["ffn2_norm"], p["ffn2_w_gate"], p["ffn2_w_up"],
                                            lambda after, l=l: weights_of(l, "ffn2_out", after)["ffn2_w_down"])
        saved.append((s1, s2, s3, x3))
        if l + 1 < DEPTH:
            xs = _block_norm_fwd(x3, p["block_out_norm"])

    sm = {n: [None] * DEPTH for n in SMALL + ("mlstm_conv_w",)}
    loss = None
    dx = None
    for l in reversed(range(DEPTH)):
        p = params[l]
        s1, s2, s3, x3 = saved[l]
        if l == DEPTH - 1:
            loss, dx, dgn = _loss_and_grad(x3, p["block_out_norm"], tgt)
        else:
            dx, dgn = _block_norm_bwd(x3, p["block_out_norm"], dx)
        sm["block_out_norm"][l] = dgn[0]
        dx, dg = _ffn_bwd("ffn2", s3, p["ffn2_norm"], p["ffn2_w_gate"], p["ffn2_w_up"], p["ffn2_w_down"], dx,
                          functools.partial(on_grads, l, "ffn2"))
        sm["ffn2_norm"][l] = dg[0]
        dx, g = _mix_bwd(s2, cos, sin, B, S, p, dx, functools.partial(on_grads, l, "mix"))
        dconv = _qk_unperm_cols(g["conv_w8"], 1)
        sm["mlstm_conv_w"][l] = dconv[0:3]
        sm["mlstm_conv_b"][l] = dconv[3]
        sm["mix_norm"][l] = g["mix_norm"][0]
        sm["mlstm_gate_bias"][l] = g["gate_bias"][0, :MLSTM_N_GATES]
        sm["attn_q_norm"][l], sm["attn_k_norm"][l] = g["attn_q_norm"][0], g["attn_k_norm"][0]
        sm["attn_sink"][l] = g["attn_sink"][0]
        sm["mlstm_out_norm"][l] = g["mlstm_out_norm"][0]
        dx, dg = _ffn_bwd("ffn1", s1, p["ffn1_norm"], p["ffn1_w_gate"], p["ffn1_w_up"], p["ffn1_w_down"], dx,
                          functools.partial(on_grads, l, "ffn1"))
        sm["ffn1_norm"][l] = dg[0]
    sm = {n: jnp.stack(v, axis=0) for n, v in sm.items()}
    return loss, dx.reshape(B, S, D_MODEL), sm


def kernel(x, positions, ffn1_norm, ffn1_w_gate, ffn1_w_up, ffn1_w_down, mix_norm, w_in, mlstm_gate_bias, attn_q_norm, attn_k_norm, attn_sink, mlstm_conv_w, mlstm_conv_b, mlstm_out_norm, w_branch_attn, w_branch_mlstm, w_out, ffn2_norm, ffn2_w_gate, ffn2_w_up, ffn2_w_down, block_out_norm, loss_target, m_ffn1_norm, m_ffn1_w_gate, m_ffn1_w_up, m_ffn1_w_down, m_mix_norm, m_w_in, m_mlstm_gate_bias, m_attn_q_norm, m_attn_k_norm, m_attn_sink, m_mlstm_conv_w, m_mlstm_conv_b, m_mlstm_out_norm, m_w_branch_attn, m_w_branch_mlstm, m_w_out, m_ffn2_norm, m_ffn2_w_gate, m_ffn2_w_up, m_ffn2_w_down, m_block_out_norm, v_ffn1_norm, v_ffn1_w_gate, v_ffn1_w_up, v_ffn1_w_down, v_mix_norm, v_w_in, v_mlstm_gate_bias, v_attn_q_norm, v_attn_k_norm, v_attn_sink, v_mlstm_conv_w, v_mlstm_conv_b, v_mlstm_out_norm, v_w_branch_attn, v_w_branch_mlstm, v_w_out, v_ffn2_norm, v_ffn2_w_gate, v_ffn2_w_up, v_ffn2_w_down, v_block_out_norm):
    args = locals()
    w = {n: args[n] for n in WEIGHTS}
    m = {n: args["m_" + n] for n in WEIGHTS}
    v = {n: args["v_" + n] for n in WEIGHTS}

    order = [(l, grp) for l in range(DEPTH) for grp in GATHER_GROUPS]
    keys = [(l, n) for l, grp in order for n in GATHER_GROUPS[grp]]
    lays = [LAYOUTS[n] for _, n in keys]
    shards = [lay.pad(w[n][l].astype(BF16)) for (l, n), lay in zip(keys, lays)]
    group_idx, at = {}, 0
    for l, grp in order:
        group_idx[(l, grp)] = list(range(at, at + len(GATHER_GROUPS[grp])))
        at += len(GATHER_GROUPS[grp])
    conv_shape = w["mlstm_conv_w"].shape
    conv_all = _all_gather("conv_all_gather", _pack_flat([w["mlstm_conv_w"]], F32, 8), vmem=True)
    conv_parts = _unpack_flat(conv_all, [conv_shape], lead=(N_DEV,))[0]
    conv_w = jnp.concatenate([conv_parts[j] for j in range(N_DEV)], axis=2)
    small = {n: w[n] for n in SMALL}

    lands = []
    for l, grp in order:
        idx = group_idx[(l, grp)]
        lands += _place_own("weights_place_" + grp, [shards[i] for i in idx], [lays[i] for i in idx])
    sems, shards, lands = _gather_start("weights_gather_start", shards, lands, lays, [group_idx[k] for k in order], conv_all)

    def weights_of(l, grp, after):
        idx = group_idx[(l, grp)]
        whole = _gather_wait(f"weights_gather_wait_{l}_{grp}", sems[order.index((l, grp))], [shards[i] for i in idx],
                             [lands[i] for i in idx], [lays[i] for i in idx], after)
        return dict(zip(GATHER_GROUPS[grp], whole))

    totals, pending = {}, []

    def finish(after):
        tag, names = pending[0][0], pending[0][1]
        for n, t in zip(names, _reduce_scatter_finish(pending.pop(0), after)):
            totals[(tag, n)] = t

    def on_grads(l, grp, g, after):
        if pending:
            finish(after)
        names = GROUPS[grp]
        pending.append(_reduce_scatter_start(f"grads_{l}_{grp}", names, [g[n] for n in names]))
        return pending[-1][3][0]

    loss, grad_x, small_g = _local_step(x, positions, loss_target, weights_of, small, conv_w, on_grads)
    finish(grad_x)
    grads, deltas, new_m, new_v = {}, {}, {}, {}
    for grp, names in GROUPS.items():
        for n in names:
            grads[n], deltas[n], new_m[n], new_v[n] = _adamw_layers(
                "adamw_" + n, w[n], [totals[(f"grads_{l}_{grp}", n)] for l in range(DEPTH)], m[n], v[n])

    small_names = SMALL + ("mlstm_conv_w",)
    small_shapes = [small_g[n].shape for n in small_names] + [(1, 1)]
    small_packed = _pack_flat([small_g[n] for n in small_names] + [loss], F32, 8)
    small_all = _all_gather("small_all_gather", small_packed, vmem=True)
    small_sum = _sum_slots("small_sum", small_all, N_DEV)
    *small_grads, loss_total = _unpack_flat(small_sum, small_shapes)
    grads.update(dict(zip(small_names, small_grads)))
    x_pos, y_pos, c_pos = _mesh_pos()
    grads["mlstm_conv_w"] = lax.dynamic_slice_in_dim(
        grads["mlstm_conv_w"], (4 * x_pos + 2 * y_pos + c_pos) * conv_shape[2], conv_shape[2], axis=2)

    n = "mlstm_conv_w"
    deltas[n], new_m[n], new_v[n] = _adamw("adamw_" + n, w[n], grads[n], m[n], v[n])
    sw, sg, smm, sv = (_pack_flat([d[n] for n in SMALL], F32, 8) for d in (w, grads, m, v))
    sd, snm, snv = _adamw("adamw_small", sw, sg, smm, sv)
    shapes = [w[n].shape for n in SMALL]
    for d, buf in ((deltas, sd), (new_m, snm), (new_v, snv)):
        d.update(dict(zip(SMALL, _unpack_flat(buf, shapes))))

    return (loss_total.reshape(()), grad_x, *[grads[n] for n in WEIGHTS], *[deltas[n] for n in WEIGHTS],
            *[new_m[n] for n in WEIGHTS], *[new_v[n] for n in WEIGHTS])
```

```python
import functools

import numpy as np
import jax
import jax.numpy as jnp
from jax import lax
from jax.experimental import pallas as pl
from jax.experimental.pallas import tpu as pltpu

F32 = jnp.float32
BF16 = jnp.bfloat16

D_MODEL = 1024
D_FF = 2816
ATT_HEAD_DIM = 64
ATT_HEADS = 8
ATT_KV_HEADS = 2
ATT_GROUP = ATT_HEADS // ATT_KV_HEADS
ATT_WIDTH = ATT_HEADS * ATT_HEAD_DIM
ATT_KV_WIDTH = ATT_KV_HEADS * ATT_HEAD_DIM
WINDOW = 128
ATT_BLOCK = 128
ROPE_DIM = 16
ROPE_THETA = 500000.0
MLSTM_HEADS = 4
MLSTM_HEAD_DIM = 128
MLSTM_WIDTH = MLSTM_HEADS * MLSTM_HEAD_DIM
MLSTM_CHUNK = 128
MLSTM_N_GATES = 4 * MLSTM_HEADS
NORM_EPS = 1e-6
IN_WIDTH = 4880
DEPTH = 2
N_DEV = 8

ADAM_LR = 0.001
ADAM_B1 = 0.9
ADAM_B2 = 0.999
ADAM_EPS = 1e-08
ADAM_WD = 0.01
ADAM_STEP = 10

LANES = 128
C_GMERGE = 0
C_QK = 2048
C_VM = 3072
C_OM = 3584
C_QA = 4096
C_KA = 4608
C_VA = 4736
C_GATES = 4864
IN_PAD = 4992

VMEM_LIMIT = 48 * 1024 * 1024

MESH = pl.DeviceIdType.MESH


def _cparams(sem):
    return pltpu.CompilerParams(dimension_semantics=sem, vmem_limit_bytes=VMEM_LIMIT)


def _first_divisor(n, cands):
    for c in cands:
        if n % c == 0:
            return c
    return n


_NN = ((1,), (0,))
_NT = ((1,), (1,))
_TN = ((0,), (0,))


def _mm(a, b, dims):
    return lax.dot_general(a.astype(BF16), b.astype(BF16), (dims, ((), ())), preferred_element_type=F32)


@jax.custom_vjp
def mm_nn(a, b):
    return _mm(a, b, _NN)


def _mm_nn_fwd(a, b):
    return _mm(a, b, _NN), (a, b)


def _mm_nn_bwd(res, g):
    a, b = res
    return _mm(g, b, _NT).astype(a.dtype), _mm(a, g, _TN).astype(b.dtype)


mm_nn.defvjp(_mm_nn_fwd, _mm_nn_bwd)


@jax.custom_vjp
def mm_nt(a, b):
    return _mm(a, b, _NT)


def _mm_nt_fwd(a, b):
    return _mm(a, b, _NT), (a, b)


def _mm_nt_bwd(res, g):
    a, b = res
    return _mm(g, b, _NN).astype(a.dtype), _mm(g, a, _TN).astype(b.dtype)


mm_nt.defvjp(_mm_nt_fwd, _mm_nt_bwd)


@jax.custom_vjp
def mm_tn(a, b):
    return _mm(a, b, _TN)


def _mm_tn_fwd(a, b):
    return _mm(a, b, _TN), (a, b)


def _mm_tn_bwd(res, g):
    a, b = res
    return _mm(b, g, _NT).astype(a.dtype), _mm(a, g, _NN).astype(b.dtype)


mm_tn.defvjp(_mm_tn_fwd, _mm_tn_bwd)


def _matmul(name, a, b, mode, out_dtype=F32, res=None, scale=1.0, bl=None, dep=None):
    b_shape = b.shape if bl is None else b.shape[1:]
    if mode == "nn":
        (M, K), (K2, N) = a.shape, b_shape
    elif mode == "nt":
        (M, K), (N, K2) = a.shape, b_shape
    else:
        (K, M), (K2, N) = a.shape, b_shape
    assert K == K2, (name, a.shape, b.shape)
    tm = _first_divisor(M, (1024, 512, 384, 256, 128))
    tn = _first_divisor(N, (1024, 1664, 512, 384, 256, 128))
    tk = _first_divisor(K, (1024, 1664, 512, 256, 128))
    nk = K // tk
    if mode == "tn":
        a_spec = pl.BlockSpec((tk, tm), lambda i, j, k: (k, i))
    else:
        a_spec = pl.BlockSpec((tm, tk), lambda i, j, k: (i, k))
    if mode == "nt":
        b_blk, b_idx = (tn, tk), (lambda i, j, k: (j, k))
    else:
        b_blk, b_idx = (tk, tn), (lambda i, j, k: (k, j))
    if bl is None:
        b_spec = pl.BlockSpec(b_blk, b_idx)
    else:
        b_spec = pl.BlockSpec((None,) + b_blk, lambda i, j, k: (bl,) + b_idx(i, j, k))
    o_spec = pl.BlockSpec((tm, tn), lambda i, j, k: (i, j))
    dims = {"nn": _NN, "nt": _NT, "tn": _TN}[mode]
    has_res = res is not None

    def body(*refs):
        a_ref, b_ref = refs[:2]
        r_ref = refs[2] if has_res else None

        def finish(out):
            if scale != 1.0:
                out = out * scale
            if has_res:
                out = r_ref[...].astype(F32) + out
            o_ref[...] = out.astype(out_dtype)

        if nk == 1:
            o_ref = refs[-1]
            finish(_mm(a_ref[...], b_ref[...], dims))
            return
        o_ref, acc = refs[-2:]
        k = pl.program_id(2)

        @pl.when(k == 0)
        def _():
            acc[...] = jnp.zeros_like(acc)

        acc[...] += _mm(a_ref[...], b_ref[...], dims)

        @pl.when(k == nk - 1)
        def _():
            finish(acc[...])

    in_specs = [a_spec, b_spec] + ([o_spec] if has_res else [])
    args = (a, b) + ((res,) if has_res else ())
    if dep is not None:
        in_specs.append(pl.BlockSpec(memory_space=pl.ANY))
        args += (dep,)
    return pl.pallas_call(
        body, name=name, grid=(M // tm, N // tn, nk), in_specs=in_specs, out_specs=o_spec,
        out_shape=jax.ShapeDtypeStruct((M, N), out_dtype),
        scratch_shapes=[pltpu.VMEM((tm, tn), F32)] if nk > 1 else [],
        compiler_params=_cparams(("parallel", "parallel", "arbitrary")),
    )(*args)


class _In:
    def __init__(self, arr, width=None, base=0, split=False, rows=True):
        self.arr, self.base, self.split, self.rows = arr, base, split, rows
        self.width = arr.shape[1] if width is None else width


class _Out:
    def __init__(self, cols, dtype=F32, width=None, split=False, rows=True, nrows=1):
        self.cols, self.dtype, self.split, self.rows, self.nrows = cols, dtype, split, rows, nrows
        self.width = cols if width is None else width


def _rowwise(name, fn, ins, outs, n_rows, br, ncol=1):
    br = min(br, n_rows)
    assert n_rows % br == 0, (name, n_rows, br)
    nrow_blocks = n_rows // br

    def in_spec(d):
        nb = br if d.rows else d.arr.shape[0]
        if d.rows and d.split:
            im = lambda j, i, base=d.base: (i, base + j)
        elif d.rows:
            im = lambda j, i, base=d.base: (i, base)
        elif d.split:
            im = lambda j, i, base=d.base: (0, base + j)
        else:
            im = lambda j, i, base=d.base: (0, base)
        return pl.BlockSpec((nb, d.width), im)

    def out_spec(d):
        nb = br if d.rows else d.nrows
        if d.rows and d.split:
            im = lambda j, i: (i, j)
        elif d.rows:
            im = lambda j, i: (i, 0)
        elif d.split:
            im = lambda j, i: (0, j)
        else:
            im = lambda j, i: (0, 0)
        return pl.BlockSpec((nb, d.width), im)

    n_in = len(ins)

    def body(*refs):
        i = pl.program_id(1)
        vals = [r[...] for r in refs[:n_in]]
        res = fn(*vals)
        if not isinstance(res, (tuple, list)):
            res = (res,)
        for d, ref, val in zip(outs, refs[n_in:], res):
            if d.rows:
                ref[...] = val.astype(d.dtype)
            else:
                @pl.when(i == 0)
                def _(ref=ref):
                    ref[...] = jnp.zeros_like(ref)

                ref[...] += val.astype(d.dtype)

    out_shape = [jax.ShapeDtypeStruct((n_rows if d.rows else d.nrows, d.cols), d.dtype) for d in outs]
    res = pl.pallas_call(
        body, name=name, grid=(ncol, nrow_blocks), in_specs=[in_spec(d) for d in ins],
        out_specs=[out_spec(d) for d in outs], out_shape=out_shape,
        compiler_params=_cparams(("parallel", "arbitrary")),
    )(*[d.arr for d in ins])
    return res


def _rms(x, g):
    return x * lax.rsqrt(jnp.mean(x * x, axis=-1, keepdims=True) + NORM_EPS) * g


def _sigmoid(x):
    return 0.5 * jnp.tanh(0.5 * x) + 0.5


def _silu(x):
    return x * _sigmoid(x)


def _log_sigmoid(x):
    return jnp.minimum(x, 0.0) - jnp.log(1.0 + jnp.exp(-jnp.abs(x)))


def _rope_tables(pos, inv_freq_row):
    ang = pos.astype(F32) * inv_freq_row
    return jnp.cos(ang), jnp.sin(ang)


def _head_sums_impl(v):
    w = v.shape[-1]
    shift = ATT_HEAD_DIM.bit_length() - 1
    r = lax.shift_right_logical(lax.broadcasted_iota(jnp.int32, (w, w), 0), shift)
    c = lax.shift_right_logical(lax.broadcasted_iota(jnp.int32, (w, w), 1), shift)
    ones = (r == c).astype(BF16)
    hi = v.astype(BF16)
    lo = (v - hi.astype(F32)).astype(BF16)
    dn = (_NN, ((), ()))
    return (lax.dot_general(hi, ones, dn, preferred_element_type=F32)
            + lax.dot_general(lo, ones, dn, preferred_element_type=F32))


@jax.custom_vjp
def _head_sums(v):
    return _head_sums_impl(v)


_head_sums.defvjp(lambda v: (_head_sums_impl(v), None), lambda _, g: (_head_sums_impl(g),))


def _rotate_half_impl(y):
    w = y.shape[-1]
    half = ROPE_DIM // 2
    lane = lax.broadcasted_iota(jnp.int32, y.shape, 1) & (ATT_HEAD_DIM - 1)
    above = pltpu.roll(y, w - half, axis=1)
    below = pltpu.roll(y, half, axis=1)
    return jnp.where(lane < half, -above, jnp.where(lane < ROPE_DIM, below, 0.0))


@jax.custom_vjp
def _rotate_half(y):
    return _rotate_half_impl(y)


_rotate_half.defvjp(lambda y: (_rotate_half_impl(y), None), lambda _, g: (-_rotate_half_impl(g),))


def _qk_prep(t, g, cos, sin):
    reps = t.shape[-1] // cos.shape[-1]
    if reps > 1:
        cos, sin = jnp.tile(cos, (1, reps)), jnp.tile(sin, (1, reps))
    y = t * lax.rsqrt(_head_sums(t * t) * (1.0 / ATT_HEAD_DIM) + NORM_EPS) * g
    return y * cos + _rotate_half(y) * sin


def _attn_head(q, kb, vb, sink, valid):
    s = mm_nt(q, kb) * (ATT_HEAD_DIM ** -0.5)
    s = jnp.where(valid, s, -jnp.inf)
    m = jnp.maximum(jnp.max(s, axis=-1, keepdims=True), sink)
    p = jnp.exp(s - m)
    den = jnp.sum(p, axis=-1, keepdims=True) + jnp.exp(sink - m)
    return mm_nn(p * (1.0 / den), vb)


def _mlstm_chunk(q, k, v, li, lf_pre, C, n, m, incl, incl_t, eye):
    k = k * (MLSTM_HEAD_DIM ** -0.5)
    lf = _log_sigmoid(lf_pre)
    lf_row = jnp.sum(eye * lf, axis=0, keepdims=True)
    li_row = jnp.sum(eye * li, axis=0, keepdims=True)
    b = jnp.sum(incl * lf_row, axis=1, keepdims=True)
    b_row = jnp.sum(incl_t * lf, axis=0, keepdims=True)
    b_tot = jnp.sum(lf, axis=0, keepdims=True)
    a = b_tot - b + li
    a_max = jnp.max(a, axis=0, keepdims=True)
    kw = k * jnp.exp(a - a_max)
    c_loc = mm_tn(kw, v)
    n_loc = jnp.sum(kw, axis=0, keepdims=True)

    dmat = jnp.where(incl > 0.5, b - b_row + li_row, -jnp.inf)
    inter = b + m
    m_t = jnp.maximum(inter, jnp.max(dmat, axis=1, keepdims=True))
    sc = mm_nt(q, k) * jnp.exp(dmat - m_t)
    scale_in = jnp.exp(inter - m_t)
    num = mm_nn(sc, v) + scale_in * mm_nn(q, C)
    den = jnp.sum(sc, axis=1, keepdims=True) + scale_in * jnp.sum(q * n, axis=1, keepdims=True)
    h = num * (1.0 / jnp.maximum(jnp.abs(den), jnp.exp(-m_t)))

    m_new = jnp.maximum(b_tot + m, a_max)
    s_p = jnp.exp(b_tot + m - m_new)
    s_l = jnp.exp(a_max - m_new)
    return h, s_p * C + s_l * c_loc, s_p * n + s_l * n_loc, m_new


def _mlstm_combine(hf, hb, o_pre, g):
    h = hf + hb
    mu = jnp.mean(h, axis=-1, keepdims=True)
    var = jnp.mean(jnp.square(h - mu), axis=-1, keepdims=True)
    return _sigmoid(o_pre) * ((h - mu) * lax.rsqrt(var + NORM_EPS) * g)


def _merge(ga, gm, za, zm):
    return _sigmoid(ga) * za + _sigmoid(gm) * zm


def _attn_mask(n, seq):
    qi = n * ATT_BLOCK + lax.broadcasted_iota(jnp.int32, (ATT_BLOCK, 3 * ATT_BLOCK), 0)
    kj = (n - 1) * ATT_BLOCK + lax.broadcasted_iota(jnp.int32, (ATT_BLOCK, 3 * ATT_BLOCK), 1)
    return (jnp.abs(qi - kj) <= WINDOW) & (kj >= 0) & (kj < seq)


def _attn_specs(nq, v_base):
    q_spec = pl.BlockSpec((1, ATT_BLOCK, ATT_WIDTH), lambda b, n: (b, n, 0))

    def kv_spec(off, base=0):
        return pl.BlockSpec((1, ATT_BLOCK, ATT_KV_WIDTH), lambda b, n: (b, jnp.clip(n + off, 0, nq - 1), base))

    sink_spec = pl.BlockSpec((ATT_KV_HEADS, ATT_GROUP, 1, 1), lambda b, n: (0, 0, 0, 0))
    specs = [q_spec, kv_spec(-1), kv_spec(0), kv_spec(1), kv_spec(-1, v_base), kv_spec(0, v_base), kv_spec(1, v_base), sink_spec]
    return q_spec, specs, sink_spec


def _head(h):
    return slice(h * ATT_HEAD_DIM, (h + 1) * ATT_HEAD_DIM)


def _attn_fwd(q, k, proj3, sink):
    B, S, _ = q.shape
    nq = S // ATT_BLOCK
    q_spec, specs, _ = _attn_specs(nq, C_VA // ATT_KV_WIDTH)

    def body(q_ref, kp, kc, kn, vp, vc, vn, s_ref, o_ref):
        valid = _attn_mask(pl.program_id(1), S)
        for h in range(ATT_KV_HEADS):
            kb = jnp.concatenate([kp[0, :, _head(h)], kc[0, :, _head(h)], kn[0, :, _head(h)]], axis=0)
            vb = jnp.concatenate([vp[0, :, _head(h)], vc[0, :, _head(h)], vn[0, :, _head(h)]], axis=0)
            for g in range(ATT_GROUP):
                hq = h * ATT_GROUP + g
                o_ref[0, :, _head(hq)] = _attn_head(q_ref[0, :, _head(hq)], kb, vb, s_ref[h, g], valid).astype(BF16)

    return pl.pallas_call(
        body, name="attn_fwd", grid=(B, nq), in_specs=specs,
        out_specs=q_spec, out_shape=jax.ShapeDtypeStruct(q.shape, BF16),
        compiler_params=_cparams(("parallel", "arbitrary")),
    )(q, k, k, k, proj3, proj3, proj3, sink)


def _attn_bwd(q, k, proj3, sink, dy):
    B, S, _ = q.shape
    nq = S // ATT_BLOCK
    q_spec, specs, sink_spec = _attn_specs(nq, C_VA // ATT_KV_WIDTH)
    kv_full = pl.BlockSpec((1, S, ATT_KV_WIDTH), lambda b, n: (b, 0, 0))

    def body(q_ref, kp, kc, kn, vp, vc, vn, s_ref, dy_ref, dq_ref, dk_ref, dv_ref, ds_ref):
        b, n = pl.program_id(0), pl.program_id(1)
        valid = _attn_mask(n, S)

        @pl.when(n == 0)
        def _():
            dk_ref[...] = jnp.zeros_like(dk_ref)
            dv_ref[...] = jnp.zeros_like(dv_ref)

        @pl.when((n == 0) & (b == 0))
        def _():
            ds_ref[...] = jnp.zeros_like(ds_ref)

        for h in range(ATT_KV_HEADS):
            kb = jnp.concatenate([kp[0, :, _head(h)], kc[0, :, _head(h)], kn[0, :, _head(h)]], axis=0)
            vb = jnp.concatenate([vp[0, :, _head(h)], vc[0, :, _head(h)], vn[0, :, _head(h)]], axis=0)
            dkb = jnp.zeros_like(kb)
            dvb = jnp.zeros_like(vb)
            for g in range(ATT_GROUP):
                hq = h * ATT_GROUP + g
                _, vjp = jax.vjp(functools.partial(_attn_head, valid=valid), q_ref[0, :, _head(hq)], kb, vb, s_ref[h, g])
                dq, dk_g, dv_g, dsink = vjp(dy_ref[0, :, _head(hq)])
                dq_ref[0, :, _head(hq)] = dq.astype(dq_ref.dtype)
                ds_ref[h, g] += dsink
                dkb += dk_g
                dvb += dv_g
            for j, off in enumerate((-1, 0, 1)):
                start = pl.multiple_of(jnp.clip(n + off, 0, nq - 1) * ATT_BLOCK, ATT_BLOCK)
                rows = pl.ds(start, ATT_BLOCK)
                dk_ref[0, rows, _head(h)] += dkb[j * ATT_BLOCK:(j + 1) * ATT_BLOCK]
                dv_ref[0, rows, _head(h)] += dvb[j * ATT_BLOCK:(j + 1) * ATT_BLOCK]

    kv_shape = jax.ShapeDtypeStruct(k.shape, F32)
    return pl.pallas_call(
        body, name="attn_bwd", grid=(B, nq), in_specs=specs + [q_spec],
        out_specs=[q_spec, kv_full, kv_full, sink_spec],
        out_shape=[jax.ShapeDtypeStruct(q.shape, F32), kv_shape, kv_shape, jax.ShapeDtypeStruct(sink.shape, F32)],
        compiler_params=_cparams(("arbitrary", "arbitrary")),
    )(q, k, k, k, proj3, proj3, proj3, sink, dy)


CONV_COLS = 256


def _conv_taps(u, seq):
    row = lax.broadcasted_iota(jnp.int32, u.shape, 0)
    prev = jnp.where(row == 0, 0.0, pltpu.roll(u, 1, axis=0))
    nxt = jnp.where(row == seq - 1, 0.0, pltpu.roll(u, seq - 1, axis=0))
    return prev, nxt


def _conv_fwd(proj3, w8):
    B, S, _ = proj3.shape
    ncb = 2 * MLSTM_WIDTH // CONV_COLS

    def body(u_ref, w_ref, o_ref):
        u = u_ref[0]
        prev, nxt = _conv_taps(u, S)
        o_ref[0] = _silu(prev * w_ref[0:1, :] + u * w_ref[1:2, :] + nxt * w_ref[2:3, :] + w_ref[3:4, :])

    return pl.pallas_call(
        body, name="conv_fwd", grid=(B, ncb),
        in_specs=[pl.BlockSpec((1, S, CONV_COLS), lambda b, c: (b, 0, C_QK // CONV_COLS + c)),
                  pl.BlockSpec((8, CONV_COLS), lambda b, c: (0, c))],
        out_specs=pl.BlockSpec((1, S, CONV_COLS), lambda b, c: (b, 0, c)),
        out_shape=jax.ShapeDtypeStruct((B, S, 2 * MLSTM_WIDTH), F32),
        compiler_params=_cparams(("parallel", "parallel")),
    )(proj3, w8)


def _conv_bwd(proj3, w8, dout_f, dout_b):
    B, S, _ = proj3.shape
    ncb = 2 * MLSTM_WIDTH // CONV_COLS

    def body(u_ref, w_ref, df_ref, db_ref, du_ref, dw_ref):
        b = pl.program_id(1)
        u = u_ref[0]
        prev, nxt = _conv_taps(u, S)
        w0, w1, w2 = w_ref[0:1, :], w_ref[1:2, :], w_ref[2:3, :]
        pre = prev * w0 + u * w1 + nxt * w2 + w_ref[3:4, :]
        sig = _sigmoid(pre)
        dpre = (df_ref[0] + db_ref[0]) * (sig * (1.0 + pre * (1.0 - sig)))
        dprev, dnxt = _conv_taps(dpre, S)
        du_ref[0] = (dnxt * w0 + dpre * w1 + dprev * w2).astype(BF16)

        @pl.when(b == 0)
        def _():
            dw_ref[...] = jnp.zeros_like(dw_ref)

        dw_ref[0:1, :] += jnp.sum(dpre * prev, axis=0, keepdims=True)
        dw_ref[1:2, :] += jnp.sum(dpre * u, axis=0, keepdims=True)
        dw_ref[2:3, :] += jnp.sum(dpre * nxt, axis=0, keepdims=True)
        dw_ref[3:4, :] += jnp.sum(dpre, axis=0, keepdims=True)

    blk = pl.BlockSpec((1, S, CONV_COLS), lambda c, b: (b, 0, c))
    return pl.pallas_call(
        body, name="conv_bwd", grid=(ncb, B),
        in_specs=[pl.BlockSpec((1, S, CONV_COLS), lambda c, b: (b, 0, C_QK // CONV_COLS + c)),
                  pl.BlockSpec((8, CONV_COLS), lambda c, b: (0, c)), blk, blk],
        out_specs=[blk, pl.BlockSpec((8, CONV_COLS), lambda c, b: (0, c))],
        out_shape=[jax.ShapeDtypeStruct((B, S, 2 * MLSTM_WIDTH), BF16), jax.ShapeDtypeStruct((8, 2 * MLSTM_WIDTH), F32)],
        compiler_params=_cparams(("parallel", "arbitrary")),
    )(proj3, w8, dout_f, dout_b)


MLSTM_HEADS_PER_STEP = 4


def _chunk_masks(direction):
    t = lax.broadcasted_iota(jnp.int32, (MLSTM_CHUNK, MLSTM_CHUNK), 0)
    s = lax.broadcasted_iota(jnp.int32, (MLSTM_CHUNK, MLSTM_CHUNK), 1)
    le, ge = (s <= t).astype(F32), (s >= t).astype(F32)
    eye = (s == t).astype(F32)
    return (le, ge, eye) if direction == 0 else (ge, le, eye)


def _gate_cols(gates, direction, head):
    lane = lax.broadcasted_iota(jnp.int32, gates.shape, 1)
    sel_i = (lane == (2 * direction) * MLSTM_HEADS + head).astype(F32)
    sel_f = (lane == (2 * direction + 1) * MLSTM_HEADS + head).astype(F32)
    return sel_i, sel_f


def _mlstm_fwd(qk, proj3, bias):
    B, S, _ = qk.shape
    nc = S // MLSTM_CHUNK
    H, L, DH = MLSTM_HEADS, MLSTM_CHUNK, MLSTM_HEAD_DIM

    def chunk_of(d, c):
        return c if d == 0 else nc - 1 - c

    HS = MLSTM_HEADS_PER_STEP

    def body(qkf, qkb, vf, vb, gf, gb, bias_ref, hf, hb, csf, csb, nsf, nsb, msf, msb, c_st, n_st, m_st):
        c, hg = pl.program_id(1), pl.program_id(2)

        @pl.when(c == 0)
        def _():
            for d in range(2):
                for j in range(HS):
                    c_st[d, hg * HS + j] = jnp.zeros((DH, DH), F32)
                    n_st[d, hg * HS + j] = jnp.zeros((1, DH), F32)
                    m_st[d, hg * HS + j] = jnp.zeros((1, DH), F32)

        for d, (qk_ref, v_ref, g_ref, h_ref, cs, ns, ms) in enumerate(
                ((qkf, vf, gf, hf, csf, nsf, msf), (qkb, vb, gb, hb, csb, nsb, msb))):
            incl, incl_t, eye = _chunk_masks(d)
            gates = g_ref[0] + bias_ref[...]
            for j in range(HS):
                h = hg * HS + j
                sel_i, sel_f = _gate_cols(gates, d, h)
                li = jnp.sum(gates * sel_i, axis=1, keepdims=True)
                lf_pre = jnp.sum(gates * sel_f, axis=1, keepdims=True)
                c_in, n_in, m_in = c_st[d, h], n_st[d, h], m_st[d, h]
                cs[0, 0, j], ns[0, 0, j], ms[0, 0, j] = c_in, n_in, m_in
                hh, c_new, n_new, m_new = _mlstm_chunk(
                    qk_ref[0, :, 2 * j * DH:(2 * j + 1) * DH], qk_ref[0, :, (2 * j + 1) * DH:(2 * j + 2) * DH],
                    v_ref[0, :, j * DH:(j + 1) * DH], li, lf_pre, c_in, n_in,
                    jnp.max(m_in, axis=1, keepdims=True), incl, incl_t, eye)
                h_ref[0, :, j * DH:(j + 1) * DH] = hh
                c_st[d, h], n_st[d, h] = c_new, n_new
                m_st[d, h] = jnp.broadcast_to(m_new, (1, DH))

    def tok_spec(width, base, d, per_head):
        return pl.BlockSpec((1, L, width), lambda b, c, h: (b, chunk_of(d, c), base + (h if per_head else 0)))

    def st_spec(shape, d):
        return pl.BlockSpec((1, 1, HS) + shape, lambda b, c, h: (b, chunk_of(d, c), h, 0, 0))

    in_specs = [tok_spec(2 * HS * DH, 0, 0, True), tok_spec(2 * HS * DH, 0, 1, True),
                tok_spec(HS * DH, C_VM // (HS * DH), 0, True), tok_spec(HS * DH, C_VM // (HS * DH), 1, True),
                tok_spec(LANES, C_GATES // LANES, 0, False), tok_spec(LANES, C_GATES // LANES, 1, False),
                pl.BlockSpec((1, LANES), lambda b, c, h: (0, 0))]
    out_specs = [tok_spec(HS * DH, 0, 0, True), tok_spec(HS * DH, 0, 1, True),
                 st_spec((DH, DH), 0), st_spec((DH, DH), 1), st_spec((1, DH), 0), st_spec((1, DH), 1),
                 st_spec((1, DH), 0), st_spec((1, DH), 1)]
    hs = jax.ShapeDtypeStruct((B, S, H * DH), F32)
    cs = jax.ShapeDtypeStruct((B, nc, H, DH, DH), F32)
    vs = jax.ShapeDtypeStruct((B, nc, H, 1, DH), F32)
    return pl.pallas_call(
        body, name="mlstm_fwd", grid=(B, nc, H // HS), in_specs=in_specs, out_specs=out_specs,
        out_shape=[hs, hs, cs, cs, vs, vs, vs, vs],
        scratch_shapes=[pltpu.VMEM((2, H, DH, DH), F32), pltpu.VMEM((2, H, 1, DH), F32), pltpu.VMEM((2, H, 1, DH), F32)],
        compiler_params=_cparams(("parallel", "arbitrary", "arbitrary")),
    )(qk, qk, proj3, proj3, proj3, proj3, bias)


def _mlstm_bwd(qk, proj3, bias, states, dh):
    B, S, _ = qk.shape
    nc = S // MLSTM_CHUNK
    H, L, DH = MLSTM_HEADS, MLSTM_CHUNK, MLSTM_HEAD_DIM

    def chunk_of(d, c):
        return nc - 1 - c if d == 0 else c

    HS = MLSTM_HEADS_PER_STEP

    def body(qkf, qkb, vf, vb, gf, gb, bias_ref, csf, csb, nsf, nsb, msf, msb, dhf, dhb,
             dqkf, dqkb, dvf, dvb, dgf, dgb, dc_st, dn_st, dm_st):
        c, hg = pl.program_id(1), pl.program_id(2)

        @pl.when(c == 0)
        def _():
            for d in range(2):
                for j in range(HS):
                    dc_st[d, hg * HS + j] = jnp.zeros((DH, DH), F32)
                    dn_st[d, hg * HS + j] = jnp.zeros((1, DH), F32)
                    dm_st[d, hg * HS + j] = jnp.zeros((1, DH), F32)

        @pl.when(hg == 0)
        def _():
            dgf[...] = jnp.zeros_like(dgf)
            dgb[...] = jnp.zeros_like(dgb)

        for d, (qk_ref, v_ref, g_ref, cs, ns, ms, dh_ref, dqk_ref, dv_ref, dg_ref) in enumerate(
                ((qkf, vf, gf, csf, nsf, msf, dhf, dqkf, dvf, dgf), (qkb, vb, gb, csb, nsb, msb, dhb, dqkb, dvb, dgb))):
            incl, incl_t, eye = _chunk_masks(d)
            gates = g_ref[0] + bias_ref[...]
            dgates = jnp.zeros_like(gates)
            for j in range(HS):
                h = hg * HS + j
                sel_i, sel_f = _gate_cols(gates, d, h)
                li = jnp.sum(gates * sel_i, axis=1, keepdims=True)
                lf_pre = jnp.sum(gates * sel_f, axis=1, keepdims=True)
                m_in = jnp.max(ms[0, 0, j], axis=1, keepdims=True)
                _, vjp = jax.vjp(
                    functools.partial(_mlstm_chunk, incl=incl, incl_t=incl_t, eye=eye),
                    qk_ref[0, :, 2 * j * DH:(2 * j + 1) * DH], qk_ref[0, :, (2 * j + 1) * DH:(2 * j + 2) * DH],
                    v_ref[0, :, j * DH:(j + 1) * DH], li, lf_pre, cs[0, 0, j], ns[0, 0, j], m_in)
                dm_out = jnp.max(dm_st[d, h], axis=1, keepdims=True)
                dq, dk, dv, dli, dlf, dc, dn, dm = vjp((dh_ref[0, :, j * DH:(j + 1) * DH], dc_st[d, h], dn_st[d, h], dm_out))
                dqk_ref[0, :, 2 * j * DH:(2 * j + 1) * DH] = dq
                dqk_ref[0, :, (2 * j + 1) * DH:(2 * j + 2) * DH] = dk
                dv_ref[0, :, j * DH:(j + 1) * DH] = dv
                dgates += dli * sel_i + dlf * sel_f
                dc_st[d, h], dn_st[d, h] = dc, dn
                dm_st[d, h] = jnp.broadcast_to(dm, (1, DH))
            dg_ref[0] += dgates

    def tok_spec(width, base, d, per_head):
        return pl.BlockSpec((1, L, width), lambda b, c, h: (b, chunk_of(d, c), base + (h if per_head else 0)))

    def st_spec(shape, d):
        return pl.BlockSpec((1, 1, HS) + shape, lambda b, c, h: (b, chunk_of(d, c), h, 0, 0))

    in_specs = [tok_spec(2 * HS * DH, 0, 0, True), tok_spec(2 * HS * DH, 0, 1, True),
                tok_spec(HS * DH, C_VM // (HS * DH), 0, True), tok_spec(HS * DH, C_VM // (HS * DH), 1, True),
                tok_spec(LANES, C_GATES // LANES, 0, False), tok_spec(LANES, C_GATES // LANES, 1, False),
                pl.BlockSpec((1, LANES), lambda b, c, h: (0, 0)),
                st_spec((DH, DH), 0), st_spec((DH, DH), 1), st_spec((1, DH), 0), st_spec((1, DH), 1),
                st_spec((1, DH), 0), st_spec((1, DH), 1), tok_spec(HS * DH, 0, 0, True), tok_spec(HS * DH, 0, 1, True)]
    out_specs = [tok_spec(2 * HS * DH, 0, 0, True), tok_spec(2 * HS * DH, 0, 1, True),
                 tok_spec(HS * DH, 0, 0, True), tok_spec(HS * DH, 0, 1, True),
                 tok_spec(LANES, 0, 0, False), tok_spec(LANES, 0, 1, False)]
    qks = jax.ShapeDtypeStruct((B, S, 2 * H * DH), F32)
    vs = jax.ShapeDtypeStruct((B, S, H * DH), F32)
    gs = jax.ShapeDtypeStruct((B, S, LANES), F32)
    csf, csb, nsf, nsb, msf, msb = states
    return pl.pallas_call(
        body, name="mlstm_bwd", grid=(B, nc, H // HS), in_specs=in_specs, out_specs=out_specs,
        out_shape=[qks, qks, vs, vs, gs, gs],
        scratch_shapes=[pltpu.VMEM((2, H, DH, DH), F32), pltpu.VMEM((2, H, 1, DH), F32), pltpu.VMEM((2, H, 1, DH), F32)],
        compiler_params=_cparams(("parallel", "arbitrary", "arbitrary")),
    )(qk, qk, proj3, proj3, proj3, proj3, bias, csf, csb, nsf, nsb, msf, msb, dh, dh)


ROW_BLOCK = 256
FF_COLS = 512
FF_SHARD = D_FF // N_DEV
FF_SHARD_PAD = 384
FF_PAD = N_DEV * FF_SHARD_PAD


def _rms_fwd(name, x, g):
    T = x.shape[0]
    return _rowwise(name, lambda xv, gv: _rms(xv, gv), [_In(x), _In(g, rows=False)], [_Out(D_MODEL, BF16)], T, ROW_BLOCK)[0]


def _rms_bwd(name, x, g, dh, dres):
    T = x.shape[0]

    def fn(xv, gv, dhv, drv):
        _, vjp = jax.vjp(_rms, xv, gv)
        dx, dg = vjp(dhv)
        return drv + dx, dg

    return _rowwise(name, fn, [_In(x), _In(g, rows=False), _In(dh), _In(dres)],
                    [_Out(D_MODEL), _Out(D_MODEL, rows=False)], T, ROW_BLOCK)


def _mmw(name, a, w, mode, **kw):
    if isinstance(w, tuple):
        return _matmul(name, a, w[0], mode, bl=w[1], **kw)
    return _matmul(name, a, w, mode, **kw)


def _swiglu(gate, up):
    return _silu(gate) * up


def _ffn_in(name, h, wg, wu):
    (M, K), N = h.shape, wg.shape[1]
    tm, tn = _first_divisor(M, (1024, 512, 256, 128)), FF_COLS

    def body(h_ref, wg_ref, wu_ref, g_ref, u_ref, a_ref):
        hv = h_ref[...]
        gate = _mm(hv, wg_ref[...], _NN)
        up = _mm(hv, wu_ref[...], _NN)
        g_ref[...], u_ref[...] = gate, up
        a_ref[...] = _swiglu(gate, up).astype(BF16)

    w_spec = pl.BlockSpec((K, tn), lambda i, j: (0, j))
    o_spec = pl.BlockSpec((tm, tn), lambda i, j: (i, j))
    return pl.pallas_call(
        body, name=name, grid=(M // tm, N // tn), in_specs=[pl.BlockSpec((tm, K), lambda i, j: (i, 0)), w_spec, w_spec],
        out_specs=[o_spec, o_spec, o_spec],
        out_shape=[jax.ShapeDtypeStruct((M, N), F32), jax.ShapeDtypeStruct((M, N), F32), jax.ShapeDtypeStruct((M, N), BF16)],
        compiler_params=_cparams(("parallel", "parallel")),
    )(h, wg, wu)


def _ffn_dact(name, dx, wd, gate, up):
    (M, K), N = dx.shape, wd.shape[0]
    tm, tn = _first_divisor(M, (1024, 512, 256, 128)), FF_COLS

    def body(dx_ref, wd_ref, g_ref, u_ref, dg_ref, du_ref):
        dact = _mm(dx_ref[...], wd_ref[...], _NT) * 0.5
        _, vjp = jax.vjp(_swiglu, g_ref[...], u_ref[...])
        dgate, dup = vjp(dact)
        dg_ref[...], du_ref[...] = dgate.astype(BF16), dup.astype(BF16)

    o_spec = pl.BlockSpec((tm, tn), lambda i, j: (i, j))
    return pl.pallas_call(
        body, name=name, grid=(M // tm, N // tn),
        in_specs=[pl.BlockSpec((tm, K), lambda i, j: (i, 0)), pl.BlockSpec((tn, K), lambda i, j: (j, 0)), o_spec, o_spec],
        out_specs=[o_spec, o_spec],
        out_shape=[jax.ShapeDtypeStruct((M, N), BF16), jax.ShapeDtypeStruct((M, N), BF16)],
        compiler_params=_cparams(("parallel", "parallel")),
    )(dx, wd, gate, up)


def _ffn_dh(name, dgate, dup, wg, wu, dep):
    (M, K), N = dgate.shape, wg.shape[0]
    tm, tk = _first_divisor(M, (1024, 512, 256, 128)), _first_divisor(K, (1024, 512, 384, 256, 128))
    nk = K // tk

    def body(dg_ref, du_ref, wg_ref, wu_ref, dep_ref, o_ref, acc):
        k = pl.program_id(1)

        @pl.when(k == 0)
        def _():
            acc[...] = jnp.zeros_like(acc)

        acc[...] += _mm(dg_ref[...], wg_ref[...], _NT) + _mm(du_ref[...], wu_ref[...], _NT)

        @pl.when(k == nk - 1)
        def _():
            o_ref[...] = acc[...]

    a_spec = pl.BlockSpec((tm, tk), lambda i, k: (i, k))
    w_spec = pl.BlockSpec((N, tk), lambda i, k: (0, k))
    return pl.pallas_call(
        body, name=name, grid=(M // tm, nk), in_specs=[a_spec, a_spec, w_spec, w_spec, pl.BlockSpec(memory_space=pl.ANY)],
        out_specs=pl.BlockSpec((tm, N), lambda i, k: (i, 0)), out_shape=jax.ShapeDtypeStruct((M, N), F32),
        scratch_shapes=[pltpu.VMEM((tm, N), F32)], compiler_params=_cparams(("parallel", "arbitrary")),
    )(dgate, dup, wg, wu, dep)


def _ffn_fwd(tag, x, g, wg, wu, wd):
    h = _rms_fwd(tag + "_norm", x, g)
    gate, up, act = _ffn_in(tag + "_in", h, wg, wu)
    if callable(wd):
        wd = wd(act)
    out = _mmw(tag + "_down", act, wd, "nn", res=x, scale=0.5)
    return out, (x, h, gate, up, act), wd


def _ffn_bwd(tag, saved, g, wg, wu, wd, dx, on_dw):
    x, h, gate, up, act = saved
    dgate, dup = _ffn_dact(tag + "_dact", dx, wd, gate, up)
    dwd = _matmul(tag + "_dwd", act, dx, "tn", scale=0.5, out_dtype=BF16)
    dwg = _matmul(tag + "_dwg", h, dgate, "tn", out_dtype=BF16)
    dwu = _matmul(tag + "_dwu", h, dup, "tn", out_dtype=BF16)
    token = on_dw({tag + "_w_gate": dwg, tag + "_w_up": dwu, tag + "_w_down": dwd}, dwu)
    dh = _ffn_dh(tag + "_dh", dgate, dup, wg, wu, token)
    dx_new, dg = _rms_bwd(tag + "_dnorm", x, g, dh, dx)
    return dx_new, dg


def _rope_cos_sin(positions):
    half = ROPE_DIM // 2
    inv_freq = jnp.power(jnp.float32(ROPE_THETA), -jnp.arange(half, dtype=F32) * (2.0 / ROPE_DIM))
    head = jnp.zeros((ATT_HEAD_DIM,), F32).at[:ROPE_DIM].set(jnp.concatenate([inv_freq, inv_freq]))
    row = jnp.tile(head, LANES // ATT_HEAD_DIM)[None, :]
    T = positions.shape[0]
    return _rowwise("rope_tables", _rope_tables, [_In(positions), _In(row, rows=False)], [_Out(LANES), _Out(LANES)], T, 1024)


def _prep_fwd(name, src, width, base, g, cos, sin):
    return _rowwise(name, _qk_prep, [_In(src, width, base), _In(g, rows=False), _In(cos), _In(sin)],
                    [_Out(width)], src.shape[0], 512)[0]


def _prep_bwd(name, src, width, base, g, cos, sin, dout):
    def fn(tv, gv, cv, sv, dv):
        _, vjp = jax.vjp(lambda a, b: _qk_prep(a, b, cv, sv), tv, gv)
        return vjp(dv)

    return _rowwise(name, fn, [_In(src, width, base), _In(g, rows=False), _In(cos), _In(sin), _In(dout)],
                    [_Out(width, BF16), _Out(width, rows=False)], src.shape[0], 512)


def _to_heads(t, B, S, nh):
    return t.reshape(B, S, nh, ATT_HEAD_DIM).transpose(0, 2, 1, 3)


def _from_heads(t):
    B, nh, S, _ = t.shape
    return t.transpose(0, 2, 1, 3).reshape(B * S, nh * ATT_HEAD_DIM)


def _mix_fwd(x, cos, sin, B, S, p):
    T = B * S
    h = _rms_fwd("mix_norm", x, p["mix_norm"])
    proj = _matmul("mix_proj", h, p["w_in"], "nn")
    proj3 = proj.reshape(B, S, IN_PAD)
    q_gain = jnp.tile(p["attn_q_norm"], (1, ATT_HEADS))
    k_gain = jnp.tile(p["attn_k_norm"], (1, ATT_KV_HEADS))
    q_r = _prep_fwd("q_prep", proj, ATT_WIDTH, C_QA // ATT_WIDTH, q_gain, cos, sin)
    k_r = _prep_fwd("k_prep", proj, ATT_KV_WIDTH, C_KA // ATT_KV_WIDTH, k_gain, cos, sin)
    qh = q_r.reshape(B, S, ATT_WIDTH)
    kh = k_r.reshape(B, S, ATT_KV_WIDTH)
    sink = p["attn_sink"].reshape(ATT_KV_HEADS, ATT_GROUP, 1, 1)
    y_a = _attn_fwd(qh, kh, proj3, sink).reshape(T, ATT_WIDTH)

    qk_c = _conv_fwd(proj3, p["conv_w8"])
    hf, hb, *states = _mlstm_fwd(qk_c, proj3, p["gate_bias"])
    hf2, hb2 = hf.reshape(T, MLSTM_WIDTH), hb.reshape(T, MLSTM_WIDTH)
    DH = MLSTM_HEAD_DIM
    y_m = _rowwise("mlstm_out", _mlstm_combine,
                   [_In(hf2, DH, split=True), _In(hb2, DH, split=True), _In(proj, DH, C_OM // DH, split=True),
                    _In(p["mlstm_out_norm"], DH, split=True, rows=False)],
                   [_Out(MLSTM_WIDTH, BF16, DH, split=True)], T, 1024, ncol=MLSTM_HEADS)[0]

    za = _mmw("branch_a", y_a, p["w_branch_attn"], "nn")
    zm = _mmw("branch_m", y_m, p["w_branch_mlstm"], "nn")
    W = 512
    merged = _rowwise("merge", _merge,
                      [_In(proj, W, C_GMERGE // W, split=True), _In(proj, W, (C_GMERGE + D_MODEL) // W, split=True),
                       _In(za, W, split=True), _In(zm, W, split=True)],
                      [_Out(D_MODEL, BF16, W, split=True)], T, 512, ncol=D_MODEL // W)[0]
    out = _mmw("mix_out", merged, p["w_out"], "nn", res=x)
    saved = dict(x=x, h=h, proj=proj, q_gain=q_gain, k_gain=k_gain, qh=qh, kh=kh, sink=sink, y_a=y_a, qk_c=qk_c,
                 hf=hf2, hb=hb2, states=states, y_m=y_m, za=za, zm=zm, merged=merged)
    return out, saved


def _mix_bwd(sv, cos, sin, B, S, p, dx, on_dw):
    T = B * S
    DH = MLSTM_HEAD_DIM
    proj = sv["proj"]
    proj3 = proj.reshape(B, S, IN_PAD)
    g = {}
    dmerged = _mmw("mix_dmerged", dx, p["w_out"], "nt")
    g["w_out"] = _matmul("mix_dwout", sv["merged"], dx, "tn", out_dtype=BF16)
    W = 512

    def merge_bwd(ga, gm, za, zm, dm):
        _, vjp = jax.vjp(_merge, ga, gm, za, zm)
        return vjp(dm)

    dga, dgm, dza, dzm = _rowwise(
        "merge_bwd", merge_bwd,
        [_In(proj, W, C_GMERGE // W, split=True), _In(proj, W, (C_GMERGE + D_MODEL) // W, split=True),
         _In(sv["za"], W, split=True), _In(sv["zm"], W, split=True), _In(dmerged, W, split=True)],
        [_Out(D_MODEL, BF16, W, split=True), _Out(D_MODEL, BF16, W, split=True),
         _Out(D_MODEL, BF16, W, split=True), _Out(D_MODEL, BF16, W, split=True)], T, 512, ncol=D_MODEL // W)
    dya = _mmw("branch_a_dx", dza, p["w_branch_attn"], "nt")
    g["w_branch_attn"] = _matmul("branch_a_dw", sv["y_a"], dza, "tn", out_dtype=BF16)
    dym = _mmw("branch_m_dx", dzm, p["w_branch_mlstm"], "nt")
    g["w_branch_mlstm"] = _matmul("branch_m_dw", sv["y_m"], dzm, "tn", out_dtype=BF16)

    def combine_bwd(hf, hb, o_pre, gn, dy):
        _, vjp = jax.vjp(_mlstm_combine, hf, hb, o_pre, gn)
        dhf, _, do, dg = vjp(dy)
        return dhf, do, dg

    dh, dom, g["mlstm_out_norm"] = _rowwise(
        "mlstm_out_bwd", combine_bwd,
        [_In(sv["hf"], DH, split=True), _In(sv["hb"], DH, split=True), _In(proj, DH, C_OM // DH, split=True),
         _In(p["mlstm_out_norm"], DH, split=True, rows=False), _In(dym, DH, split=True)],
        [_Out(MLSTM_WIDTH, F32, DH, split=True), _Out(MLSTM_WIDTH, BF16, DH, split=True),
         _Out(MLSTM_WIDTH, F32, DH, split=True, rows=False)], T, 1024, ncol=MLSTM_HEADS)
    dqk_f, dqk_b, dv_f, dv_b, dg_f, dg_b = _mlstm_bwd(sv["qk_c"], proj3, p["gate_bias"], sv["states"],
                                                       dh.reshape(B, S, MLSTM_WIDTH))
    dgates, dvm, g["gate_bias"] = _rowwise(
        "mlstm_dsum", lambda a, b, c, d: (a + b, c + d, jnp.sum(a + b, axis=0, keepdims=True)),
        [_In(dg_f.reshape(T, LANES)), _In(dg_b.reshape(T, LANES)), _In(dv_f.reshape(T, MLSTM_WIDTH)), _In(dv_b.reshape(T, MLSTM_WIDTH))],
        [_Out(LANES, BF16), _Out(MLSTM_WIDTH, BF16), _Out(LANES, rows=False)], T, 1024)
    dqk, g["conv_w8"] = _conv_bwd(proj3, p["conv_w8"], dqk_f, dqk_b)

    dqh, dkh, dvh, dsink = _attn_bwd(sv["qh"], sv["kh"], proj3, sv["sink"], dya.reshape(B, S, ATT_WIDTH))
    g["attn_sink"] = dsink.reshape(1, ATT_HEADS)
    dva = dvh.reshape(T, ATT_KV_WIDTH)
    dqa, dq_gain = _prep_bwd("q_prep_bwd", proj, ATT_WIDTH, C_QA // ATT_WIDTH, sv["q_gain"], cos, sin,
                             dqh.reshape(T, ATT_WIDTH))
    dka, dk_gain = _prep_bwd("k_prep_bwd", proj, ATT_KV_WIDTH, C_KA // ATT_KV_WIDTH, sv["k_gain"], cos, sin,
                             dkh.reshape(T, ATT_KV_WIDTH))
    g["attn_q_norm"] = jnp.sum(dq_gain.reshape(ATT_HEADS, ATT_HEAD_DIM), axis=0, keepdims=True)
    g["attn_k_norm"] = jnp.sum(dk_gain.reshape(ATT_KV_HEADS, ATT_HEAD_DIM), axis=0, keepdims=True)

    dproj = jnp.concatenate(
        [dga, dgm, dqk.reshape(T, 2 * MLSTM_WIDTH), dvm, dom, dqa, dka, dva.astype(BF16), dgates], axis=1)
    dwin = _matmul("mix_dwin", sv["h"], dproj, "tn", out_dtype=BF16)
    token = on_dw({"w_in": _w_in_to_slots(dwin), "w_branch_attn": g.pop("w_branch_attn"),
                   "w_branch_mlstm": g.pop("w_branch_mlstm"), "w_out": g.pop("w_out")}, dwin)
    dh2 = _matmul("mix_dh", dproj, p["w_in"], "nt", dep=token)
    dx_new, g["mix_norm"] = _rms_bwd("mix_dnorm", sv["x"], p["mix_norm"], dh2, dx)
    return dx_new, g


def _loss_and_grad(x, g, target):
    T = x.shape[0]

    def loss_fn(xv, gv, tv):
        err = jnp.square(_rms(xv, gv) - tv)
        return 0.5 * jnp.sum(jnp.mean(err, axis=-1, keepdims=True), axis=0, keepdims=True)

    def fn(xv, gv, tv):
        val, vjp = jax.vjp(lambda a, b: loss_fn(a, b, tv), xv, gv)
        dx, dg = vjp(jnp.ones((1, 1), F32))
        return val, dx, dg

    return _rowwise("loss_head", fn, [_In(x), _In(g, rows=False), _In(target)],
                    [_Out(1, rows=False), _Out(D_MODEL), _Out(D_MODEL, rows=False)], T, ROW_BLOCK)


def _block_norm_fwd(x, g):
    T = x.shape[0]
    return _rowwise("block_norm", _rms, [_In(x), _In(g, rows=False)], [_Out(D_MODEL)], T, ROW_BLOCK)[0]


def _block_norm_bwd(x, g, dy):
    T = x.shape[0]

    def fn(xv, gv, dv):
        _, vjp = jax.vjp(_rms, xv, gv)
        return vjp(dv)

    return _rowwise("block_norm_bwd", fn, [_In(x), _In(g, rows=False), _In(dy)],
                    [_Out(D_MODEL), _Out(D_MODEL, rows=False)], T, ROW_BLOCK)


def _qk_perm_cols(t, axis):
    q, k = jnp.split(t, 2, axis=axis)
    parts = []
    for h in range(MLSTM_HEADS):
        sl = [slice(None)] * t.ndim
        sl[axis] = slice(h * MLSTM_HEAD_DIM, (h + 1) * MLSTM_HEAD_DIM)
        parts += [q[tuple(sl)], k[tuple(sl)]]
    return jnp.concatenate(parts, axis=axis)


def _qk_unperm_cols(t, axis):
    qs, ks = [], []
    for h in range(MLSTM_HEADS):
        sl = [slice(None)] * t.ndim
        sl[axis] = slice(2 * h * MLSTM_HEAD_DIM, (2 * h + 1) * MLSTM_HEAD_DIM)
        qs.append(t[tuple(sl)])
        sl[axis] = slice((2 * h + 1) * MLSTM_HEAD_DIM, (2 * h + 2) * MLSTM_HEAD_DIM)
        ks.append(t[tuple(sl)])
    return jnp.concatenate(qs + ks, axis=axis)


def _w_in_arrange(w):
    qa, ka, va, qm, km, vm, om, gm, gmerge = jnp.split(w, np.cumsum(
        (ATT_WIDTH, ATT_KV_WIDTH, ATT_KV_WIDTH, MLSTM_WIDTH, MLSTM_WIDTH, MLSTM_WIDTH, MLSTM_WIDTH, MLSTM_N_GATES))[:].tolist(), axis=1)
    qk = _qk_perm_cols(jnp.concatenate([qm, km], axis=1), 1)
    pad = jnp.zeros((w.shape[0], LANES - MLSTM_N_GATES), w.dtype)
    return jnp.concatenate([gmerge, qk, vm, om, qa, ka, va, gm, pad], axis=1)


def _w_in_restore(w):
    gmerge = w[:, C_GMERGE:C_GMERGE + 2 * D_MODEL]
    qk = _qk_unperm_cols(w[:, C_QK:C_QK + 2 * MLSTM_WIDTH], 1)
    vm, om = w[:, C_VM:C_VM + MLSTM_WIDTH], w[:, C_OM:C_OM + MLSTM_WIDTH]
    qa, ka, va = w[:, C_QA:C_QA + ATT_WIDTH], w[:, C_KA:C_KA + ATT_KV_WIDTH], w[:, C_VA:C_VA + ATT_KV_WIDTH]
    gm = w[:, C_GATES:C_GATES + MLSTM_N_GATES]
    return jnp.concatenate([qa, ka, va, qk, vm, om, gm, gmerge], axis=1)


BIG = ("ffn1_w_gate", "ffn1_w_up", "ffn1_w_down", "w_in", "mlstm_conv_w", "w_branch_attn", "w_branch_mlstm", "w_out",
       "ffn2_w_gate", "ffn2_w_up", "ffn2_w_down")
MATMUL_W = tuple(n for n in BIG if n != "mlstm_conv_w")
SMALL = ("ffn1_norm", "mix_norm", "mlstm_gate_bias", "attn_q_norm", "attn_k_norm", "attn_sink", "mlstm_conv_b",
         "mlstm_out_norm", "ffn2_norm", "block_out_norm")
WEIGHTS = ("ffn1_norm", "ffn1_w_gate", "ffn1_w_up", "ffn1_w_down", "mix_norm", "w_in", "mlstm_gate_bias", "attn_q_norm",
           "attn_k_norm", "attn_sink", "mlstm_conv_w", "mlstm_conv_b", "mlstm_out_norm", "w_branch_attn", "w_branch_mlstm",
           "w_out", "ffn2_norm", "ffn2_w_gate", "ffn2_w_up", "ffn2_w_down", "block_out_norm")
PACK_COLS = 1024


def _padded_rows(n_elems):
    return -(-n_elems // PACK_COLS)


def _pack_flat(arrs, dtype, row_multiple):
    parts = []
    for a in arrs:
        flat = a.reshape(-1).astype(dtype)
        pad = _padded_rows(flat.shape[0]) * PACK_COLS - flat.shape[0]
        parts.append(jnp.pad(flat, (0, pad)) if pad else flat)
    flat = jnp.concatenate(parts)
    rows = flat.shape[0] // PACK_COLS
    extra = (-rows) % row_multiple
    if extra:
        flat = jnp.pad(flat, (0, extra * PACK_COLS))
    return flat.reshape(-1, PACK_COLS)


def _unpack_flat(buf, shapes, lead=()):
    flat = buf.reshape(lead + (-1,))
    out, off = [], 0
    for s in shapes:
        n = int(np.prod(s))
        out.append(flat[..., off:off + n].reshape(lead + tuple(s)))
        off += _padded_rows(n) * PACK_COLS
    return out


class _Lay:
    def __init__(self, shard, axis, width):
        self.shard, self.axis, self.width = shard, axis, width
        self.padded = tuple(width if a == axis else s for a, s in enumerate(shard))
        self.whole = tuple(N_DEV * width if a == axis else s for a, s in enumerate(shard))

    def pad(self, t, lead=0):
        extra = self.width - self.shard[self.axis]
        if not extra:
            return t
        cfg = [(0, 0)] * t.ndim
        cfg[lead + self.axis] = (0, extra)
        return jnp.pad(t, cfg)

    def unpad(self, t, lead=0):
        idx = [slice(None)] * t.ndim
        idx[lead + self.axis] = slice(0, self.shard[self.axis])
        return t[tuple(idx)]


_FF_COL = _Lay((D_MODEL, FF_SHARD), 1, FF_SHARD_PAD)
_FF_ROW = _Lay((FF_SHARD, D_MODEL), 0, FF_SHARD_PAD)
LAYOUTS = {
    "ffn1_w_gate": _FF_COL, "ffn1_w_up": _FF_COL, "ffn1_w_down": _FF_ROW,
    "ffn2_w_gate": _FF_COL, "ffn2_w_up": _FF_COL, "ffn2_w_down": _FF_ROW,
    "w_in": _Lay((D_MODEL, IN_WIDTH // N_DEV), 0, D_MODEL),
    "mlstm_conv_w": _Lay((3, 2 * MLSTM_WIDTH // N_DEV), 1, 2 * MLSTM_WIDTH // N_DEV),
    "w_branch_attn": _Lay((ATT_WIDTH, D_MODEL // N_DEV), 1, D_MODEL // N_DEV),
    "w_branch_mlstm": _Lay((MLSTM_WIDTH, D_MODEL // N_DEV), 1, D_MODEL // N_DEV),
    "w_out": _Lay((D_MODEL // N_DEV, D_MODEL), 0, D_MODEL // N_DEV),
}


def _window(ref, axis, j, width):
    idx = [slice(None)] * len(ref.shape)
    idx[axis] = pl.ds(pl.multiple_of(j * width, width), width)
    return ref.at[tuple(idx)]


ANY = pl.BlockSpec(memory_space=pl.ANY)


def _mesh_pos():
    return lax.axis_index("x"), lax.axis_index("y"), lax.axis_index("c")


def _all_gather(name, shard, vmem=False):
    R, C = shard.shape
    space = pl.BlockSpec(memory_space=pltpu.VMEM) if vmem else ANY

    def body(x_ref, out_ref, send_sems, recv_sems, local_sem):
        x, y, c = _mesh_pos()
        me, sibling = (x, y, c), (x, y, 1 - c)
        chips = [(1 - x, y), (x, 1 - y), (1 - x, 1 - y)]

        def slot(px, py, pc):
            return out_ref.at[4 * px + 2 * py + pc]

        def copy(k, block, to, src=None):
            return pltpu.make_async_remote_copy(
                src_ref=slot(*block) if src is None else src, dst_ref=slot(*block),
                send_sem=send_sems.at[k], recv_sem=recv_sems.at[k], device_id=to, device_id_type=MESH)

        mine = pltpu.make_async_copy(x_ref, slot(*me), local_sem)
        mine.start()
        first = [copy(0, me, sibling, src=x_ref)]
        first += [copy(1 + j, me, (*chip, c), src=x_ref) for j, chip in enumerate(chips)]
        for cp in first:
            cp.start()
        passed = [copy(4 + j, (*chip, c), sibling) for j, chip in enumerate(chips)]
        for j, chip in enumerate(chips):
            copy(1 + j, (*chip, c), me).wait_recv()
            passed[j].start()
        copy(0, sibling, me).wait_recv()
        for j, chip in enumerate(chips):
            copy(4 + j, (*chip, 1 - c), me).wait_recv()
        for cp in first + passed:
            cp.wait_send()
        mine.wait()

    return pl.pallas_call(
        body, name=name, out_shape=jax.ShapeDtypeStruct((N_DEV, R, C), shard.dtype),
        in_specs=[space], out_specs=space,
        scratch_shapes=[pltpu.SemaphoreType.DMA((7,)), pltpu.SemaphoreType.DMA((7,)), pltpu.SemaphoreType.DMA],
    )(shard)


HBM = pl.BlockSpec(memory_space=pltpu.HBM)
SEM = pl.BlockSpec(memory_space=pltpu.SEMAPHORE)
SPLIT_COPY = pltpu.CompilerParams(has_side_effects=pltpu.SideEffectType.DATAFLOW_SIDE_EFFECTING)
N_PEERS = N_DEV - 1


def _peers(x, y, c):
    return [(x, y, 1 - c), (1 - x, y, c), (x, 1 - y, c), (1 - x, 1 - y, c),
            (1 - x, y, 1 - c), (x, 1 - y, 1 - c), (1 - x, 1 - y, 1 - c)]


def _dev_index(pos):
    return 4 * pos[0] + 2 * pos[1] + pos[2]


def _place_own(name, shards, lays):
    nt = len(shards)
    me = _dev_index(_mesh_pos())

    def body(me_ref, *refs):
        for x_ref, o_ref in zip(refs[:nt], refs[nt:]):
            o_ref[...] = x_ref[...]

    def window_spec(lay):
        if lay.axis == 0:
            return pl.BlockSpec(lay.padded, lambda i, me_ref: (me_ref[0], 0))
        return pl.BlockSpec(lay.padded, lambda i, me_ref: (0, me_ref[0]))

    return pl.pallas_call(
        body, name=name,
        grid_spec=pltpu.PrefetchScalarGridSpec(
            num_scalar_prefetch=1, grid=(1,),
            in_specs=[pl.BlockSpec(lay.padded, lambda i, me_ref: (0, 0)) for lay in lays],
            out_specs=[window_spec(lay) for lay in lays]),
        out_shape=[jax.ShapeDtypeStruct(lay.whole, s.dtype) for s, lay in zip(shards, lays)],
        compiler_params=_cparams(("arbitrary",)),
    )(me.reshape(1).astype(jnp.int32), *shards)


def _gather_start(name, shards, lands, lays, groups, after):
    nt, ng = len(shards), len(groups)

    def body(*refs):
        x_refs, land_refs = refs[:nt], refs[nt:2 * nt]
        sems = refs[2 * nt + 1:2 * nt + 1 + 2 * ng]
        pos = _mesh_pos()
        me = _dev_index(pos)
        for g, tens in enumerate(groups):
            for i, t in enumerate(tens):
                for k, peer in enumerate(_peers(*pos)):
                    pltpu.make_async_remote_copy(
                        src_ref=x_refs[t], dst_ref=_window(land_refs[t], lays[t].axis, me, lays[t].width),
                        send_sem=sems[2 * g].at[N_PEERS * i + k], recv_sem=sems[2 * g + 1].at[N_PEERS * i + k],
                        device_id=peer, device_id_type=MESH).start()

    sem_shapes = []
    for tens in groups:
        sem_shapes += [pltpu.SemaphoreType.DMA((N_PEERS * len(tens),))] * 2
    thru = [pltpu.HBM(s.shape, s.dtype) for s in shards] + [pltpu.HBM(lay.whole, s.dtype) for s, lay in zip(shards, lays)]
    args = [pltpu.with_memory_space_constraint(s, pltpu.HBM) for s in shards]
    args += [pltpu.with_memory_space_constraint(ld, pltpu.HBM) for ld in lands]
    res = pl.pallas_call(
        body, name=name, out_shape=tuple(sem_shapes + thru), in_specs=[HBM] * (2 * nt) + [ANY],
        out_specs=tuple([SEM] * (2 * ng) + [HBM] * (2 * nt)),
        input_output_aliases={t: 2 * ng + t for t in range(2 * nt)}, compiler_params=SPLIT_COPY,
    )(*args, after)
    sems = [(res[2 * g], res[2 * g + 1]) for g in range(ng)]
    return sems, list(res[2 * ng:2 * ng + nt]), list(res[2 * ng + nt:])


def _gather_wait(name, sems, shards, lands, lays, after):
    nt = len(shards)
    send_sems, recv_sems = sems

    def body(*refs):
        x_refs, land_refs = refs[:nt], refs[nt:2 * nt]
        send_ref, recv_ref = refs[2 * nt], refs[2 * nt + 1]
        pos = _mesh_pos()
        for t in range(nt):
            for k, peer in enumerate(_peers(*pos)):
                cp = pltpu.make_async_remote_copy(
                    src_ref=x_refs[t], dst_ref=_window(land_refs[t], lays[t].axis, _dev_index(peer), lays[t].width),
                    send_sem=send_ref.at[N_PEERS * t + k], recv_sem=recv_ref.at[N_PEERS * t + k],
                    device_id=peer, device_id_type=MESH)
                cp.wait_send()
                cp.wait_recv()

    thru = [pltpu.HBM(s.shape, s.dtype) for s in shards] + [pltpu.HBM(ld.shape, ld.dtype) for ld in lands]
    res = pl.pallas_call(
        body, name=name, out_shape=tuple(thru), in_specs=[HBM] * (2 * nt) + [SEM, SEM, ANY],
        out_specs=tuple([HBM] * (2 * nt)), input_output_aliases={t: t for t in range(2 * nt)},
        compiler_params=SPLIT_COPY,
    )(*shards, *lands, send_sems, recv_sems, after)
    return list(res[nt:])


def _pair_exchange(name, grads, lays):
    nt = len(grads)

    def body(*refs):
        g_refs, land_refs = refs[:nt], refs[nt:2 * nt]
        send_sems, recv_sems = refs[2 * nt:]
        x, y, c = _mesh_pos()
        copies = []
        for t in range(nt):
            for chip in range(4):
                copies.append(pltpu.make_async_remote_copy(
                    src_ref=_window(g_refs[t], lays[t].axis, 2 * chip + (1 - c), lays[t].width), dst_ref=land_refs[t].at[chip],
                    send_sem=send_sems.at[4 * t + chip], recv_sem=recv_sems.at[4 * t + chip],
                    device_id=(x, y, 1 - c), device_id_type=MESH))
        for cp in copies:
            cp.start()
        for cp in copies:
            cp.wait_recv()
        for cp in copies:
            cp.wait_send()

    out_shape = [jax.ShapeDtypeStruct((4,) + lay.padded, g.dtype) for g, lay in zip(grads, lays)]
    return pl.pallas_call(
        body, name=name, out_shape=out_shape, in_specs=[ANY] * nt, out_specs=[ANY] * nt,
        scratch_shapes=[pltpu.SemaphoreType.DMA((4 * nt,)), pltpu.SemaphoreType.DMA((4 * nt,))],
    )(*grads)


def _pair_sum(name, whole, landed, lay, out_dtype):
    R, C = lay.padded
    br = _first_divisor(R, (512, 384, 256, 128, 64, 32, 16, 8))
    nb = R // br
    if lay.axis == 0:
        mine_spec = pl.BlockSpec((br, C), lambda k, i, c_ref: ((2 * k + c_ref[0]) * nb + i, 0))
    else:
        mine_spec = pl.BlockSpec((br, C), lambda k, i, c_ref: (i, 2 * k + c_ref[0]))

    def body(c_ref, mine_ref, sib_ref, o_ref):
        o_ref[0] = (mine_ref[...].astype(F32) + sib_ref[0].astype(F32)).astype(out_dtype)

    c = lax.axis_index("c")
    return pl.pallas_call(
        body, name=name,
        grid_spec=pltpu.PrefetchScalarGridSpec(
            num_scalar_prefetch=1, grid=(4, nb),
            in_specs=[mine_spec, pl.BlockSpec((1, br, C), lambda k, i, c_ref: (k, i, 0))],
            out_specs=pl.BlockSpec((1, br, C), lambda k, i, c_ref: (k, i, 0))),
        out_shape=jax.ShapeDtypeStruct((4, R, C), out_dtype),
        compiler_params=_cparams(("parallel", "parallel")),
    )(c.reshape(1).astype(jnp.int32), whole, landed)


def _chip_exchange(name, sums):
    nt = len(sums)

    def body(*refs):
        s_refs, land_refs = refs[:nt], refs[nt:2 * nt]
        send_sems, recv_sems, local_sems = refs[2 * nt:]
        x, y, c = _mesh_pos()
        my_chip = 2 * x + y
        mine = [pltpu.make_async_copy(s_refs[t].at[my_chip], land_refs[t].at[my_chip], local_sems.at[t]) for t in range(nt)]
        for cp in mine:
            cp.start()
        chips = [(1 - x, y), (x, 1 - y), (1 - x, 1 - y)]
        copies = []
        for t in range(nt):
            for j, (px, py) in enumerate(chips):
                copies.append(pltpu.make_async_remote_copy(
                    src_ref=s_refs[t].at[2 * px + py], dst_ref=land_refs[t].at[my_chip],
                    send_sem=send_sems.at[3 * t + j], recv_sem=recv_sems.at[3 * t + j],
                    device_id=(px, py, c), device_id_type=MESH))
        for cp in copies:
            cp.start()
        for t in range(nt):
            for j, (px, py) in enumerate(chips):
                pltpu.make_async_remote_copy(
                    src_ref=s_refs[t].at[my_chip], dst_ref=land_refs[t].at[2 * px + py],
                    send_sem=send_sems.at[3 * t + j], recv_sem=recv_sems.at[3 * t + j],
                    device_id=(px, py, c), device_id_type=MESH).wait_recv()
        for cp in copies:
            cp.wait_send()
        for cp in mine:
            cp.wait()

    return pl.pallas_call(
        body, name=name, out_shape=[jax.ShapeDtypeStruct(s.shape, s.dtype) for s in sums],
        in_specs=[ANY] * nt, out_specs=[ANY] * nt,
        scratch_shapes=[pltpu.SemaphoreType.DMA((3 * nt,)), pltpu.SemaphoreType.DMA((3 * nt,)), pltpu.SemaphoreType.DMA((nt,))],
    )(*sums)


def _chip_start(name, sums):
    nt = len(sums)

    def body(*refs):
        s_refs, land_refs = refs[:nt], refs[nt:2 * nt]
        send_sems, recv_sems = refs[2 * nt], refs[2 * nt + 1]
        x, y, c = _mesh_pos()
        my_chip = 2 * x + y
        for t in range(nt):
            for j, (px, py) in enumerate([(1 - x, y), (x, 1 - y), (1 - x, 1 - y)]):
                pltpu.make_async_remote_copy(
                    src_ref=s_refs[t].at[2 * px + py], dst_ref=land_refs[t].at[my_chip],
                    send_sem=send_sems.at[3 * t + j], recv_sem=recv_sems.at[3 * t + j],
                    device_id=(px, py, c), device_id_type=MESH).start()

    thru = [pltpu.HBM(s.shape, s.dtype) for s in sums] * 2
    args = [pltpu.with_memory_space_constraint(s, pltpu.HBM) for s in sums]
    args += [pltpu.with_memory_space_constraint(lax.empty(s.shape, s.dtype), pltpu.HBM) for s in sums]
    res = pl.pallas_call(
        body, name=name, out_shape=tuple([pltpu.SemaphoreType.DMA((3 * nt,))] * 2 + thru), in_specs=[HBM] * (2 * nt),
        out_specs=tuple([SEM, SEM] + [HBM] * (2 * nt)), input_output_aliases={t: 2 + t for t in range(2 * nt)},
        compiler_params=SPLIT_COPY,
    )(*args)
    return (res[0], res[1]), list(res[2:2 + nt]), list(res[2 + nt:])


def _chip_wait(name, sems, sums, lands, after):
    nt = len(sums)

    def body(*refs):
        s_refs, land_refs = refs[:nt], refs[nt:2 * nt]
        send_sems, recv_sems = refs[2 * nt], refs[2 * nt + 1]
        x, y, c = _mesh_pos()
        my_chip = 2 * x + y
        for t in range(nt):
            for j, (px, py) in enumerate([(1 - x, y), (x, 1 - y), (1 - x, 1 - y)]):
                cp = pltpu.make_async_remote_copy(
                    src_ref=s_refs[t].at[my_chip], dst_ref=land_refs[t].at[2 * px + py],
                    send_sem=send_sems.at[3 * t + j], recv_sem=recv_sems.at[3 * t + j],
                    device_id=(px, py, c), device_id_type=MESH)
                cp.wait_send()
                cp.wait_recv()

    thru = [pltpu.HBM(s.shape, s.dtype) for s in sums] * 2
    res = pl.pallas_call(
        body, name=name, out_shape=tuple(thru), in_specs=[HBM] * (2 * nt) + [SEM, SEM, ANY],
        out_specs=tuple([HBM] * (2 * nt)), input_output_aliases={t: t for t in range(2 * nt)},
        compiler_params=SPLIT_COPY,
    )(*sums, *lands, sems[0], sems[1], after)
    return list(res[:nt]), list(res[nt:])


def _sum_chips(name, own, landed):
    _, R, C = own.shape
    br = _first_divisor(R, (512, 384, 256, 128, 64, 32, 16, 8))
    x, y, _ = _mesh_pos()
    slots = jnp.stack([2 * x + y, 2 * (1 - x) + y, 2 * x + (1 - y), 2 * (1 - x) + (1 - y)]).astype(jnp.int32)

    def body(slot_ref, mine_ref, a_ref, b_ref, c_ref, o_ref):
        o_ref[...] = ((mine_ref[0].astype(F32) + a_ref[0].astype(F32)) + b_ref[0].astype(F32)) + c_ref[0].astype(F32)

    def slot_spec(j):
        return pl.BlockSpec((1, br, C), lambda i, slot_ref: (slot_ref[j], i, 0))

    return pl.pallas_call(
        body, name=name,
        grid_spec=pltpu.PrefetchScalarGridSpec(
            num_scalar_prefetch=1, grid=(R // br,), in_specs=[slot_spec(0), slot_spec(1), slot_spec(2), slot_spec(3)],
            out_specs=pl.BlockSpec((br, C), lambda i, slot_ref: (i, 0))),
        out_shape=jax.ShapeDtypeStruct((R, C), F32), compiler_params=_cparams(("parallel",)),
    )(slots, own, landed, landed, landed)


def _sum_slots(name, slots, n):
    _, R, C = slots.shape
    br = _first_divisor(R, (512, 384, 256, 128, 64, 32, 16, 8))

    def body(s_ref, o_ref):
        acc = s_ref[0].astype(F32)
        for k in range(1, n):
            acc = acc + s_ref[k].astype(F32)
        o_ref[...] = acc

    return pl.pallas_call(
        body, name=name, grid=(R // br,), in_specs=[pl.BlockSpec((n, br, C), lambda i: (0, i, 0))],
        out_specs=pl.BlockSpec((br, C), lambda i: (i, 0)), out_shape=jax.ShapeDtypeStruct((R, C), F32),
        compiler_params=_cparams(("parallel",)),
    )(slots)


def _reduce_scatter_start(tag, names, grads):
    lays = [LAYOUTS[n] for n in names]
    landed = _pair_exchange("grads_pair_" + names[0], grads, lays)
    sums = [_pair_sum("grads_pairsum_" + n, g, ld, lay, BF16) for n, g, ld, lay in zip(names, grads, landed, lays)]
    sems, sums, lands = _chip_start(tag + "_chips_start", sums)
    return tag, names, sems, sums, lands


def _reduce_scatter_finish(pending, after):
    tag, names, sems, sums, lands = pending
    own, got = _chip_wait(tag + "_chips_wait", sems, sums, lands, after)
    return [_sum_chips("grads_sum_" + n, o, s) for n, o, s in zip(names, own, got)]


def _adamw_math(w, g, m, v):
    m = ADAM_B1 * m + (1.0 - ADAM_B1) * g
    v = ADAM_B2 * v + (1.0 - ADAM_B2) * jnp.square(g)
    m_hat = m / (1.0 - ADAM_B1 ** ADAM_STEP)
    v_hat = v / (1.0 - ADAM_B2 ** ADAM_STEP)
    delta = -ADAM_LR * (m_hat / (jnp.sqrt(v_hat) + ADAM_EPS) + ADAM_WD * w)
    return delta, m, v


def _adamw_layers(name, w, totals, m, v):
    _, R, C = w.shape
    br = _first_divisor(R, (512, 176, 128, 64, 32, 16, 8))
    Cp = totals[0].shape[1]

    def body(w_ref, g0_ref, g1_ref, m_ref, v_ref, g_out, d_out, m_out, v_out):
        g = jnp.where(pl.program_id(0) == 0, g0_ref[:, 0:C], g1_ref[:, 0:C])
        delta, m_new, v_new = _adamw_math(w_ref[0], g, m_ref[0], v_ref[0])
        g_out[0], d_out[0], m_out[0], v_out[0] = g, delta, m_new, v_new

    blk = pl.BlockSpec((1, br, C), lambda l, i: (l, i, 0))
    g_spec = pl.BlockSpec((br, Cp), lambda l, i: (i, 0))
    return pl.pallas_call(
        body, name=name, grid=(DEPTH, R // br), in_specs=[blk, g_spec, g_spec, blk, blk], out_specs=[blk] * 4,
        out_shape=[jax.ShapeDtypeStruct(w.shape, F32)] * 4, compiler_params=_cparams(("parallel", "parallel")),
    )(w, totals[0], totals[1], m, v)


def _adamw(name, w, g, m, v):
    shape = w.shape
    cols = shape[-1]
    rows = int(np.prod(shape[:-1]))
    br = _first_divisor(rows, (512, 352, 256, 128, 64, 32, 16, 8))
    args = [_In(a.reshape(rows, cols)) for a in (w, g, m, v)]
    outs = _rowwise(name, _adamw_math, args, [_Out(cols), _Out(cols), _Out(cols)], rows, br)
    return [o.reshape(shape) for o in outs]


GROUPS = {"ffn1": ("ffn1_w_gate", "ffn1_w_up", "ffn1_w_down"),
          "mix": ("w_in", "w_branch_attn", "w_branch_mlstm", "w_out"),
          "ffn2": ("ffn2_w_gate", "ffn2_w_up", "ffn2_w_down")}
GATHER_GROUPS = {"ffn1_in": ("ffn1_w_gate", "ffn1_w_up"), "ffn1_out": ("ffn1_w_down",),
                 "mix": ("w_in", "w_branch_attn", "w_branch_mlstm", "w_out"),
                 "ffn2_in": ("ffn2_w_gate", "ffn2_w_up"), "ffn2_out": ("ffn2_w_down",)}


def _small_params(small, conv_w, l):
    p = {}
    for n in ("ffn1_norm", "mix_norm", "ffn2_norm", "block_out_norm", "mlstm_out_norm", "attn_q_norm", "attn_k_norm"):
        p[n] = small[n][l][None, :]
    p["attn_sink"] = small["attn_sink"][l]
    p["gate_bias"] = jnp.pad(small["mlstm_gate_bias"][l], (0, LANES - MLSTM_N_GATES))[None, :]
    taps = _qk_perm_cols(conv_w[l], 1)
    conv_b = _qk_perm_cols(small["mlstm_conv_b"][l][None, :], 1)
    p["conv_w8"] = jnp.concatenate([taps, conv_b, jnp.zeros((4, 2 * MLSTM_WIDTH), F32)], axis=0)
    return p


def _w_in_from_slots(slots):
    w_in = slots.reshape(N_DEV, D_MODEL, IN_WIDTH // N_DEV).transpose(1, 0, 2).reshape(D_MODEL, IN_WIDTH)
    return _w_in_arrange(w_in)


def _w_in_to_slots(g):
    return _w_in_restore(g).reshape(D_MODEL, N_DEV, IN_WIDTH // N_DEV).transpose(1, 0, 2).reshape(
        N_DEV * D_MODEL, IN_WIDTH // N_DEV)


def _local_step(x, positions, target, weights_of, small, conv_w, on_grads):
    B, S, _ = x.shape
    T = B * S
    cos, sin = _rope_cos_sin(positions.reshape(T, 1))
    params = [_small_params(small, conv_w, l) for l in range(DEPTH)]
    xs = x.reshape(T, D_MODEL)
    tgt = target.reshape(T, D_MODEL)

    saved = []
    for l, p in enumerate(params):
        p.update(weights_of(l, "ffn1_in", xs))
        x1, s1, p["ffn1_w_down"] = _ffn_fwd("ffn1", xs, p["ffn1_norm"], p["ffn1_w_gate"], p["ffn1_w_up"],
                                            lambda after, l=l: weights_of(l, "ffn1_out", after)["ffn1_w_down"])
        p.update(weights_of(l, "mix", x1))
        p["w_in"] = _w_in_from_slots(p["w_in"])
        x2, s2 = _mix_fwd(x1, cos, sin, B, S, p)
        p.update(weights_of(l, "ffn2_in", x2))
        x3, s3, p["ffn2_w_down"] = _ffn_fwd("ffn2", x2, p["ffn2_norm"], p["ffn2_w_gate"], p["ffn2_w_up"],
                                            lambda after, l=l: weights_of(l, "ffn2_out", after)["ffn2_w_down"])
        saved.append((s1, s2, s3, x3))
        if l + 1 < DEPTH:
            xs = _block_norm_fwd(x3, p["block_out_norm"])

    sm = {n: [None] * DEPTH for n in SMALL + ("mlstm_conv_w",)}
    loss = None
    dx = None
    for l in reversed(range(DEPTH)):
        p = params[l]
        s1, s2, s3, x3 = saved[l]
        if l == DEPTH - 1:
            loss, dx, dgn = _loss_and_grad(x3, p["block_out_norm"], tgt)
        else:
            dx, dgn = _block_norm_bwd(x3, p["block_out_norm"], dx)
        sm["block_out_norm"][l] = dgn[0]
        dx, dg = _ffn_bwd("ffn2", s3, p["ffn2_norm"], p["ffn2_w_gate"], p["ffn2_w_up"], p["ffn2_w_down"], dx,
                          functools.partial(on_grads, l, "ffn2"))
        sm["ffn2_norm"][l] = dg[0]
        dx, g = _mix_bwd(s2, cos, sin, B, S, p, dx, functools.partial(on_grads, l, "mix"))
        dconv = _qk_unperm_cols(g["conv_w8"], 1)
        sm["mlstm_conv_w"][l] = dconv[0:3]
        sm["mlstm_conv_b"][l] = dconv[3]
        sm["mix_norm"][l] = g["mix_norm"][0]
        sm["mlstm_gate_bias"][l] = g["gate_bias"][0, :MLSTM_N_GATES]
        sm["attn_q_norm"][l], sm["attn_k_norm"][l] = g["attn_q_norm"][0], g["attn_k_norm"][0]
        sm["attn_sink"][l] = g["attn_sink"][0]
        sm["mlstm_out_norm"][l] = g["mlstm_out_norm"][0]
        dx, dg = _ffn_bwd("ffn1", s1, p["ffn1_norm"], p["ffn1_w_gate"], p["ffn1_w_up"], p["ffn1_w_down"], dx,
                          functools.partial(on_grads, l, "ffn1"))
        sm["ffn1_norm"][l] = dg[0]
    sm = {n: jnp.stack(v, axis=0) for n, v in sm.items()}
    return loss, dx.reshape(B, S, D_MODEL), sm


def kernel(x, positions, ffn1_norm, ffn1_w_gate, ffn1_w_up, ffn1_w_down, mix_norm, w_in, mlstm_gate_bias, attn_q_norm, attn_k_norm, attn_sink, mlstm_conv_w, mlstm_conv_b, mlstm_out_norm, w_branch_attn, w_branch_mlstm, w_out, ffn2_norm, ffn2_w_gate, ffn2_w_up, ffn2_w_down, block_out_norm, loss_target, m_ffn1_norm, m_ffn1_w_gate, m_ffn1_w_up, m_ffn1_w_down, m_mix_norm, m_w_in, m_mlstm_gate_bias, m_attn_q_norm, m_attn_k_norm, m_attn_sink, m_mlstm_conv_w, m_mlstm_conv_b, m_mlstm_out_norm, m_w_branch_attn, m_w_branch_mlstm, m_w_out, m_ffn2_norm, m_ffn2_w_gate, m_ffn2_w_up, m_ffn2_w_down, m_block_out_norm, v_ffn1_norm, v_ffn1_w_gate, v_ffn1_w_up, v_ffn1_w_down, v_mix_norm, v_w_in, v_mlstm_gate_bias, v_attn_q_norm, v_attn_k_norm, v_attn_sink, v_mlstm_conv_w, v_mlstm_conv_b, v_mlstm_out_norm, v_w_branch_attn, v_w_branch_mlstm, v_w_out, v_ffn2_norm, v_ffn2_w_gate, v_ffn2_w_up, v_ffn2_w_down, v_block_out_norm):
    args = locals()
    w = {n: args[n] for n in WEIGHTS}
    m = {n: args["m_" + n] for n in WEIGHTS}
    v = {n: args["v_" + n] for n in WEIGHTS}

    order = [(l, grp) for l in range(DEPTH) for grp in GATHER_GROUPS]
    keys = [(l, n) for l, grp in order for n in GATHER_GROUPS[grp]]
    lays = [LAYOUTS[n] for _, n in keys]
    shards = [lay.pad(w[n][l].astype(BF16)) for (l, n), lay in zip(keys, lays)]
    group_idx, at = {}, 0
    for l, grp in order:
        group_idx[(l, grp)] = list(range(at, at + len(GATHER_GROUPS[grp])))
        at += len(GATHER_GROUPS[grp])
    conv_shape = w["mlstm_conv_w"].shape
    conv_all = _all_gather("conv_all_gather", _pack_flat([w["mlstm_conv_w"]], F32, 8), vmem=True)
    conv_parts = _unpack_flat(conv_all, [conv_shape], lead=(N_DEV,))[0]
    conv_w = jnp.concatenate([conv_parts[j] for j in range(N_DEV)], axis=2)
    small = {n: w[n] for n in SMALL}

    lands = []
    for l, grp in order:
        idx = group_idx[(l, grp)]
        lands += _place_own("weights_place_" + grp, [shards[i] for i in idx], [lays[i] for i in idx])
    sems, shards, lands = _gather_start("weights_gather_start", shards, lands, lays, [group_idx[k] for k in order], conv_all)

    def weights_of(l, grp, after):
        idx = group_idx[(l, grp)]
        whole = _gather_wait(f"weights_gather_wait_{l}_{grp}", sems[order.index((l, grp))], [shards[i] for i in idx],
                             [lands[i] for i in idx], [lays[i] for i in idx], after)
        return dict(zip(GATHER_GROUPS[grp], whole))

    totals, pending = {}, []

    def finish(after):
        tag, names = pending[0][0], pending[0][1]
        for n, t in zip(names, _reduce_scatter_finish(pending.pop(0), after)):
            totals[(tag, n)] = t

    def on_grads(l, grp, g, after):
        if pending:
            finish(after)
        names = GROUPS[grp]
        pending.append(_reduce_scatter_start(f"grads_{l}_{grp}", names, [g[n] for n in names]))
        return pending[-1][3][0]

    loss, grad_x, small_g = _local_step(x, positions, loss_target, weights_of, small, conv_w, on_grads)
    finish(grad_x)
    grads, deltas, new_m, new_v = {}, {}, {}, {}
    for grp, names in GROUPS.items():
        for n in names:
            grads[n], deltas[n], new_m[n], new_v[n] = _adamw_layers(
                "adamw_" + n, w[n], [totals[(f"grads_{l}_{grp}", n)] for l in range(DEPTH)], m[n], v[n])

    small_names = SMALL + ("mlstm_conv_w",)
    small_shapes = [small_g[n].shape for n in small_names] + [(1, 1)]
    small_packed = _pack_flat([small_g[n] for n in small_names] + [loss], F32, 8)
    small_all = _all_gather("small_all_gather", small_packed, vmem=True)
    small_sum = _sum_slots("small_sum", small_all, N_DEV)
    *small_grads, loss_total = _unpack_flat(small_sum, small_shapes)
    grads.update(dict(zip(small_names, small_grads)))
    x_pos, y_pos, c_pos = _mesh_pos()
    grads["mlstm_conv_w"] = lax.dynamic_slice_in_dim(
        grads["mlstm_conv_w"], (4 * x_pos + 2 * y_pos + c_pos) * conv_shape[2], conv_shape[2], axis=2)

    n = "mlstm_conv_w"
    deltas[n], new_m[n], new_v[n] = _adamw("adamw_" + n, w[n], grads[n], m[n], v[n])
    sw, sg, smm, sv = (_pack_flat([d[n] for n in SMALL], F32, 8) for d in (w, grads, m, v))
    sd, snm, snv = _adamw("adamw_small", sw, sg, smm, sv)
    shapes = [w[n].shape for n in SMALL]
    for d, buf in ((deltas, sd), (new_m, snm), (new_v, snv)):
        d.update(dict(zip(SMALL, _unpack_flat(buf, shapes))))

    return (loss_total.reshape(()), grad_x, *[grads[n] for n in WEIGHTS], *[deltas[n] for n in WEIGHTS],
            *[new_m[n] for n in WEIGHTS], *[new_v[n] for n in WEIGHTS])
```

```python
import functools

import numpy as np
import jax
import jax.numpy as jnp
from jax import lax
from jax.experimental import pallas as pl
from jax.experimental.pallas import tpu as pltpu

F32 = jnp.float32
BF16 = jnp.bfloat16

D_MODEL = 1024
D_FF = 2816
ATT_HEAD_DIM = 64
ATT_HEADS = 8
ATT_KV_HEADS = 2
ATT_GROUP = ATT_HEADS // ATT_KV_HEADS
ATT_WIDTH = ATT_HEADS * ATT_HEAD_DIM
ATT_KV_WIDTH = ATT_KV_HEADS * ATT_HEAD_DIM
WINDOW = 128
ATT_BLOCK = 128
ROPE_DIM = 16
ROPE_THETA = 500000.0
MLSTM_HEADS = 4
MLSTM_HEAD_DIM = 128
MLSTM_WIDTH = MLSTM_HEADS * MLSTM_HEAD_DIM
MLSTM_CHUNK = 128
MLSTM_N_GATES = 4 * MLSTM_HEADS
NORM_EPS = 1e-6
IN_WIDTH = 4880
DEPTH = 2
N_DEV = 8

ADAM_LR = 0.001
ADAM_B1 = 0.9
ADAM_B2 = 0.999
ADAM_EPS = 1e-08
ADAM_WD = 0.01
ADAM_STEP = 10

LANES = 128
C_GMERGE = 0
C_QK = 2048
C_VM = 3072
C_OM = 3584
C_QA = 4096
C_KA = 4608
C_VA = 4736
C_GATES = 4864
IN_PAD = 4992

VMEM_LIMIT = 48 * 1024 * 1024

MESH = pl.DeviceIdType.MESH


def _cparams(sem):
    return pltpu.CompilerParams(dimension_semantics=sem, vmem_limit_bytes=VMEM_LIMIT)


def _first_divisor(n, cands):
    for c in cands:
        if n % c == 0:
            return c
    return n


_NN = ((1,), (0,))
_NT = ((1,), (1,))
_TN = ((0,), (0,))


def _mm(a, b, dims):
    return lax.dot_general(a.astype(BF16), b.astype(BF16), (dims, ((), ())), preferred_element_type=F32)


@jax.custom_vjp
def mm_nn(a, b):
    return _mm(a, b, _NN)


def _mm_nn_fwd(a, b):
    return _mm(a, b, _NN), (a, b)


def _mm_nn_bwd(res, g):
    a, b = res
    return _mm(g, b, _NT).astype(a.dtype), _mm(a, g, _TN).astype(b.dtype)


mm_nn.defvjp(_mm_nn_fwd, _mm_nn_bwd)


@jax.custom_vjp
def mm_nt(a, b):
    return _mm(a, b, _NT)


def _mm_nt_fwd(a, b):
    return _mm(a, b, _NT), (a, b)


def _mm_nt_bwd(res, g):
    a, b = res
    return _mm(g, b, _NN).astype(a.dtype), _mm(g, a, _TN).astype(b.dtype)


mm_nt.defvjp(_mm_nt_fwd, _mm_nt_bwd)


@jax.custom_vjp
def mm_tn(a, b):
    return _mm(a, b, _TN)


def _mm_tn_fwd(a, b):
    return _mm(a, b, _TN), (a, b)


def _mm_tn_bwd(res, g):
    a, b = res
    return _mm(b, g, _NT).astype(a.dtype), _mm(a, g, _NN).astype(b.dtype)


mm_tn.defvjp(_mm_tn_fwd, _mm_tn_bwd)


def _matmul(name, a, b, mode, out_dtype=F32, res=None, scale=1.0, bl=None, dep=None):
    b_shape = b.shape if bl is None else b.shape[1:]
    if mode == "nn":
        (M, K), (K2, N) = a.shape, b_shape
    elif mode == "nt":
        (M, K), (N, K2) = a.shape, b_shape
    else:
        (K, M), (K2, N) = a.shape, b_shape
    assert K == K2, (name, a.shape, b.shape)
    tm = _first_divisor(M, (1024, 512, 384, 256, 128))
    tn = _first_divisor(N, (1024, 1664, 512, 384, 256, 128))
    tk = _first_divisor(K, (1024, 1664, 512, 256, 128))
    nk = K // tk
    if mode == "tn":
        a_spec = pl.BlockSpec((tk, tm), lambda i, j, k: (k, i))
    else:
        a_spec = pl.BlockSpec((tm, tk), lambda i, j, k: (i, k))
    if mode == "nt":
        b_blk, b_idx = (tn, tk), (lambda i, j, k: (j, k))
    else:
        b_blk, b_idx = (tk, tn), (lambda i, j, k: (k, j))
    if bl is None:
        b_spec = pl.BlockSpec(b_blk, b_idx)
    else:
        b_spec = pl.BlockSpec((None,) + b_blk, lambda i, j, k: (bl,) + b_idx(i, j, k))
    o_spec = pl.BlockSpec((tm, tn), lambda i, j, k: (i, j))
    dims = {"nn": _NN, "nt": _NT, "tn": _TN}[mode]
    has_res = res is not None

    def body(*refs):
        a_ref, b_ref = refs[:2]
        r_ref = refs[2] if has_res else None

        def finish(out):
            if scale != 1.0:
                out = out * scale
            if has_res:
                out = r_ref[...].astype(F32) + out
            o_ref[...] = out.astype(out_dtype)

        if nk == 1:
            o_ref = refs[-1]
            finish(_mm(a_ref[...], b_ref[...], dims))
            return
        o_ref, acc = refs[-2:]
        k = pl.program_id(2)

        @pl.when(k == 0)
        def _():
            acc[...] = jnp.zeros_like(acc)

        acc[...] += _mm(a_ref[...], b_ref[...], dims)

        @pl.when(k == nk - 1)
        def _():
            finish(acc[...])

    in_specs = [a_spec, b_spec] + ([o_spec] if has_res else [])
    args = (a, b) + ((res,) if has_res else ())
    if dep is not None:
        in_specs.append(pl.BlockSpec(memory_space=pl.ANY))
        args += (dep,)
    return pl.pallas_call(
        body, name=name, grid=(M // tm, N // tn, nk), in_specs=in_specs, out_specs=o_spec,
        out_shape=jax.ShapeDtypeStruct((M, N), out_dtype),
        scratch_shapes=[pltpu.VMEM((tm, tn), F32)] if nk > 1 else [],
        compiler_params=_cparams(("parallel", "parallel", "arbitrary")),
    )(*args)


class _In:
    def __init__(self, arr, width=None, base=0, split=False, rows=True):
        self.arr, self.base, self.split, self.rows = arr, base, split, rows
        self.width = arr.shape[1] if width is None else width


class _Out:
    def __init__(self, cols, dtype=F32, width=None, split=False, rows=True, nrows=1):
        self.cols, self.dtype, self.split, self.rows, self.nrows = cols, dtype, split, rows, nrows
        self.width = cols if width is None else width


def _rowwise(name, fn, ins, outs, n_rows, br, ncol=1):
    br = min(br, n_rows)
    assert n_rows % br == 0, (name, n_rows, br)
    nrow_blocks = n_rows // br

    def in_spec(d):
        nb = br if d.rows else d.arr.shape[0]
        if d.rows and d.split:
            im = lambda j, i, base=d.base: (i, base + j)
        elif d.rows:
            im = lambda j, i, base=d.base: (i, base)
        elif d.split:
            im = lambda j, i, base=d.base: (0, base + j)
        else:
            im = lambda j, i, base=d.base: (0, base)
        return pl.BlockSpec((nb, d.width), im)

    def out_spec(d):
        nb = br if d.rows else d.nrows
        if d.rows and d.split:
            im = lambda j, i: (i, j)
        elif d.rows:
            im = lambda j, i: (i, 0)
        elif d.split:
            im = lambda j, i: (0, j)
        else:
            im = lambda j, i: (0, 0)
        return pl.BlockSpec((nb, d.width), im)

    n_in = len(ins)

    def body(*refs):
        i = pl.program_id(1)
        vals = [r[...] for r in refs[:n_in]]
        res = fn(*vals)
        if not isinstance(res, (tuple, list)):
            res = (res,)
        for d, ref, val in zip(outs, refs[n_in:], res):
            if d.rows:
                ref[...] = val.astype(d.dtype)
            else:
                @pl.when(i == 0)
                def _(ref=ref):
                    ref[...] = jnp.zeros_like(ref)

                ref[...] += val.astype(d.dtype)

    out_shape = [jax.ShapeDtypeStruct((n_rows if d.rows else d.nrows, d.cols), d.dtype) for d in outs]
    res = pl.pallas_call(
        body, name=name, grid=(ncol, nrow_blocks), in_specs=[in_spec(d) for d in ins],
        out_specs=[out_spec(d) for d in outs], out_shape=out_shape,
        compiler_params=_cparams(("parallel", "arbitrary")),
    )(*[d.arr for d in ins])
    return res


def _rms(x, g):
    return x * lax.rsqrt(jnp.mean(x * x, axis=-1, keepdims=True) + NORM_EPS) * g


def _sigmoid(x):
    return 0.5 * jnp.tanh(0.5 * x) + 0.5


def _silu(x):
    return x * _sigmoid(x)


def _log_sigmoid(x):
    return jnp.minimum(x, 0.0) - jnp.log(1.0 + jnp.exp(-jnp.abs(x)))


def _rope_tables(pos, inv_freq_row):
    ang = pos.astype(F32) * inv_freq_row
    return jnp.cos(ang), jnp.sin(ang)


def _head_sums_impl(v):
    w = v.shape[-1]
    shift = ATT_HEAD_DIM.bit_length() - 1
    r = lax.shift_right_logical(lax.broadcasted_iota(jnp.int32, (w, w), 0), shift)
    c = lax.shift_right_logical(lax.broadcasted_iota(jnp.int32, (w, w), 1), shift)
    ones = (r == c).astype(BF16)
    hi = v.astype(BF16)
    lo = (v - hi.astype(F32)).astype(BF16)
    dn = (_NN, ((), ()))
    return (lax.dot_general(hi, ones, dn, preferred_element_type=F32)
            + lax.dot_general(lo, ones, dn, preferred_element_type=F32))


@jax.custom_vjp
def _head_sums(v):
    return _head_sums_impl(v)


_head_sums.defvjp(lambda v: (_head_sums_impl(v), None), lambda _, g: (_head_sums_impl(g),))


def _rotate_half_impl(y):
    w = y.shape[-1]
    half = ROPE_DIM // 2
    lane = lax.broadcasted_iota(jnp.int32, y.shape, 1) & (ATT_HEAD_DIM - 1)
    above = pltpu.roll(y, w - half, axis=1)
    below = pltpu.roll(y, half, axis=1)
    return jnp.where(lane < half, -above, jnp.where(lane < ROPE_DIM, below, 0.0))


@jax.custom_vjp
def _rotate_half(y):
    return _rotate_half_impl(y)


_rotate_half.defvjp(lambda y: (_rotate_half_impl(y), None), lambda _, g: (-_rotate_half_impl(g),))


def _qk_prep(t, g, cos, sin):
    reps = t.shape[-1] // cos.shape[-1]
    if reps > 1:
        cos, sin = jnp.tile(cos, (1, reps)), jnp.tile(sin, (1, reps))
    y = t * lax.rsqrt(_head_sums(t * t) * (1.0 / ATT_HEAD_DIM) + NORM_EPS) * g
    return y * cos + _rotate_half(y) * sin


def _attn_head(q, kb, vb, sink, valid):
    s = mm_nt(q, kb) * (ATT_HEAD_DIM ** -0.5)
    s = jnp.where(valid, s, -jnp.inf)
    m = jnp.maximum(jnp.max(s, axis=-1, keepdims=True), sink)
    p = jnp.exp(s - m)
    den = jnp.sum(p, axis=-1, keepdims=True) + jnp.exp(sink - m)
    return mm_nn(p * (1.0 / den), vb)


def _mlstm_chunk(q, k, v, li, lf_pre, C, n, m, incl, incl_t, eye):
    k = k * (MLSTM_HEAD_DIM ** -0.5)
    lf = _log_sigmoid(lf_pre)
    lf_row = jnp.sum(eye * lf, axis=0, keepdims=True)
    li_row = jnp.sum(eye * li, axis=0, keepdims=True)
    b = jnp.sum(incl * lf_row, axis=1, keepdims=True)
    b_row = jnp.sum(incl_t * lf, axis=0, keepdims=True)
    b_tot = jnp.sum(lf, axis=0, keepdims=True)
    a = b_tot - b + li
    a_max = jnp.max(a, axis=0, keepdims=True)
    kw = k * jnp.exp(a - a_max)
    c_loc = mm_tn(kw, v)
    n_loc = jnp.sum(kw, axis=0, keepdims=True)

    dmat = jnp.where(incl > 0.5, b - b_row + li_row, -jnp.inf)
    inter = b + m
    m_t = jnp.maximum(inter, jnp.max(dmat, axis=1, keepdims=True))
    sc = mm_nt(q, k) * jnp.exp(dmat - m_t)
    scale_in = jnp.exp(inter - m_t)
    num = mm_nn(sc, v) + scale_in * mm_nn(q, C)
    den = jnp.sum(sc, axis=1, keepdims=True) + scale_in * jnp.sum(q * n, axis=1, keepdims=True)
    h = num * (1.0 / jnp.maximum(jnp.abs(den), jnp.exp(-m_t)))

    m_new = jnp.maximum(b_tot + m, a_max)
    s_p = jnp.exp(b_tot + m - m_new)
    s_l = jnp.exp(a_max - m_new)
    return h, s_p * C + s_l * c_loc, s_p * n + s_l * n_loc, m_new


def _mlstm_combine(hf, hb, o_pre, g):
    h = hf + hb
    mu = jnp.mean(h, axis=-1, keepdims=True)
    var = jnp.mean(jnp.square(h - mu), axis=-1, keepdims=True)
    return _sigmoid(o_pre) * ((h - mu) * lax.rsqrt(var + NORM_EPS) * g)


def _merge(ga, gm, za, zm):
    return _sigmoid(ga) * za + _sigmoid(gm) * zm


def _attn_mask(n, seq):
    shape = (ATT_GROUP * ATT_BLOCK, 3 * ATT_BLOCK)
    qi = n * ATT_BLOCK + (lax.broadcasted_iota(jnp.int32, shape, 0) & (ATT_BLOCK - 1))
    kj = (n - 1) * ATT_BLOCK + lax.broadcasted_iota(jnp.int32, shape, 1)
    return (jnp.abs(qi - kj) <= WINDOW) & (kj >= 0) & (kj < seq)


def _attn_specs(nq, v_base):
    q_spec = pl.BlockSpec((1, ATT_BLOCK, ATT_WIDTH), lambda b, n: (b, n, 0))

    def kv_spec(off, base=0):
        return pl.BlockSpec((1, ATT_BLOCK, ATT_KV_WIDTH), lambda b, n: (b, jnp.clip(n + off, 0, nq - 1), base))

    sink_spec = pl.BlockSpec((ATT_KV_HEADS, ATT_GROUP, 1, 1), lambda b, n: (0, 0, 0, 0))
    specs = [q_spec, kv_spec(-1), kv_spec(0), kv_spec(1), kv_spec(-1, v_base), kv_spec(0, v_base), kv_spec(1, v_base), sink_spec]
    return q_spec, specs, sink_spec


def _head(h):
    return slice(h * ATT_HEAD_DIM, (h + 1) * ATT_HEAD_DIM)


def _group_rows(q_ref, s_ref, h):
    q4 = jnp.concatenate([q_ref[0, :, _head(h * ATT_GROUP + g)] for g in range(ATT_GROUP)], axis=0)
    sink4 = jnp.concatenate([jnp.broadcast_to(s_ref[h, g], (ATT_BLOCK, 1)) for g in range(ATT_GROUP)], axis=0)
    return q4, sink4


def _attn_fwd(q, k, proj3, sink):
    B, S, _ = q.shape
    nq = S // ATT_BLOCK
    q_spec, specs, _ = _attn_specs(nq, C_VA // ATT_KV_WIDTH)

    def body(q_ref, kp, kc, kn, vp, vc, vn, s_ref, o_ref):
        valid = _attn_mask(pl.program_id(1), S)
        for h in range(ATT_KV_HEADS):
            kb = jnp.concatenate([kp[0, :, _head(h)], kc[0, :, _head(h)], kn[0, :, _head(h)]], axis=0)
            vb = jnp.concatenate([vp[0, :, _head(h)], vc[0, :, _head(h)], vn[0, :, _head(h)]], axis=0)
            q4, sink4 = _group_rows(q_ref, s_ref, h)
            o4 = _attn_head(q4, kb, vb, sink4, valid).astype(BF16)
            for g in range(ATT_GROUP):
                o_ref[0, :, _head(h * ATT_GROUP + g)] = o4[g * ATT_BLOCK:(g + 1) * ATT_BLOCK]

    return pl.pallas_call(
        body, name="attn_fwd", grid=(B, nq), in_specs=specs,
        out_specs=q_spec, out_shape=jax.ShapeDtypeStruct(q.shape, BF16),
        compiler_params=_cparams(("parallel", "arbitrary")),
    )(q, k, k, k, proj3, proj3, proj3, sink)


def _attn_bwd(q, k, proj3, sink, dy):
    B, S, _ = q.shape
    nq = S // ATT_BLOCK
    q_spec, specs, sink_spec = _attn_specs(nq, C_VA // ATT_KV_WIDTH)
    kv_full = pl.BlockSpec((1, S, ATT_KV_WIDTH), lambda b, n: (b, 0, 0))

    def body(q_ref, kp, kc, kn, vp, vc, vn, s_ref, dy_ref, dq_ref, dk_ref, dv_ref, ds_ref):
        b, n = pl.program_id(0), pl.program_id(1)
        valid = _attn_mask(n, S)

        @pl.when(n == 0)
        def _():
            dk_ref[...] = jnp.zeros_like(dk_ref)
            dv_ref[...] = jnp.zeros_like(dv_ref)

        @pl.when((n == 0) & (b == 0))
        def _():
            ds_ref[...] = jnp.zeros_like(ds_ref)

        for h in range(ATT_KV_HEADS):
            kb = jnp.concatenate([kp[0, :, _head(h)], kc[0, :, _head(h)], kn[0, :, _head(h)]], axis=0)
            vb = jnp.concatenate([vp[0, :, _head(h)], vc[0, :, _head(h)], vn[0, :, _head(h)]], axis=0)
            q4, sink4 = _group_rows(q_ref, s_ref, h)
            dy4 = jnp.concatenate([dy_ref[0, :, _head(h * ATT_GROUP + g)] for g in range(ATT_GROUP)], axis=0)
            _, vjp = jax.vjp(functools.partial(_attn_head, valid=valid), q4, kb, vb, sink4)
            dq4, dkb, dvb, dsink4 = vjp(dy4)
            for g in range(ATT_GROUP):
                rows = slice(g * ATT_BLOCK, (g + 1) * ATT_BLOCK)
                dq_ref[0, :, _head(h * ATT_GROUP + g)] = dq4[rows]
                ds_ref[h, g] += jnp.sum(dsink4[rows], axis=0, keepdims=True)
            for j, off in enumerate((-1, 0, 1)):
                start = pl.multiple_of(jnp.clip(n + off, 0, nq - 1) * ATT_BLOCK, ATT_BLOCK)
                rows = pl.ds(start, ATT_BLOCK)
                dk_ref[0, rows, _head(h)] += dkb[j * ATT_BLOCK:(j + 1) * ATT_BLOCK]
                dv_ref[0, rows, _head(h)] += dvb[j * ATT_BLOCK:(j + 1) * ATT_BLOCK]

    kv_shape = jax.ShapeDtypeStruct(k.shape, F32)
    return pl.pallas_call(
        body, name="attn_bwd", grid=(B, nq), in_specs=specs + [q_spec],
        out_specs=[q_spec, kv_full, kv_full, sink_spec],
        out_shape=[jax.ShapeDtypeStruct(q.shape, F32), kv_shape, kv_shape, jax.ShapeDtypeStruct(sink.shape, F32)],
        compiler_params=_cparams(("arbitrary", "arbitrary")),
    )(q, k, k, k, proj3, proj3, proj3, sink, dy)


CONV_COLS = 256


def _conv_taps(u, seq):
    row = lax.broadcasted_iota(jnp.int32, u.shape, 0)
    prev = jnp.where(row == 0, 0.0, pltpu.roll(u, 1, axis=0))
    nxt = jnp.where(row == seq - 1, 0.0, pltpu.roll(u, seq - 1, axis=0))
    return prev, nxt


def _conv_fwd(proj3, w8):
    B, S, _ = proj3.shape
    ncb = 2 * MLSTM_WIDTH // CONV_COLS

    def body(u_ref, w_ref, o_ref):
        u = u_ref[0]
        prev, nxt = _conv_taps(u, S)
        o_ref[0] = _silu(prev * w_ref[0:1, :] + u * w_ref[1:2, :] + nxt * w_ref[2:3, :] + w_ref[3:4, :])

    return pl.pallas_call(
        body, name="conv_fwd", grid=(B, ncb),
        in_specs=[pl.BlockSpec((1, S, CONV_COLS), lambda b, c: (b, 0, C_QK // CONV_COLS + c)),
                  pl.BlockSpec((8, CONV_COLS), lambda b, c: (0, c))],
        out_specs=pl.BlockSpec((1, S, CONV_COLS), lambda b, c: (b, 0, c)),
        out_shape=jax.ShapeDtypeStruct((B, S, 2 * MLSTM_WIDTH), F32),
        compiler_params=_cparams(("parallel", "parallel")),
    )(proj3, w8)


def _conv_bwd(proj3, w8, dout_f, dout_b):
    B, S, _ = proj3.shape
    ncb = 2 * MLSTM_WIDTH // CONV_COLS

    def body(u_ref, w_ref, df_ref, db_ref, du_ref, dw_ref):
        b = pl.program_id(1)
        u = u_ref[0]
        prev, nxt = _conv_taps(u, S)
        w0, w1, w2 = w_ref[0:1, :], w_ref[1:2, :], w_ref[2:3, :]
        pre = prev * w0 + u * w1 + nxt * w2 + w_ref[3:4, :]
        sig = _sigmoid(pre)
        dpre = (df_ref[0] + db_ref[0]) * (sig * (1.0 + pre * (1.0 - sig)))
        dprev, dnxt = _conv_taps(dpre, S)
        du_ref[0] = (dnxt * w0 + dpre * w1 + dprev * w2).astype(BF16)

        @pl.when(b == 0)
        def _():
            dw_ref[...] = jnp.zeros_like(dw_ref)

        dw_ref[0:1, :] += jnp.sum(dpre * prev, axis=0, keepdims=True)
        dw_ref[1:2, :] += jnp.sum(dpre * u, axis=0, keepdims=True)
        dw_ref[2:3, :] += jnp.sum(dpre * nxt, axis=0, keepdims=True)
        dw_ref[3:4, :] += jnp.sum(dpre, axis=0, keepdims=True)

    blk = pl.BlockSpec((1, S, CONV_COLS), lambda c, b: (b, 0, c))
    return pl.pallas_call(
        body, name="conv_bwd", grid=(ncb, B),
        in_specs=[pl.BlockSpec((1, S, CONV_COLS), lambda c, b: (b, 0, C_QK // CONV_COLS + c)),
                  pl.BlockSpec((8, CONV_COLS), lambda c, b: (0, c)), blk, blk],
        out_specs=[blk, pl.BlockSpec((8, CONV_COLS), lambda c, b: (0, c))],
        out_shape=[jax.ShapeDtypeStruct((B, S, 2 * MLSTM_WIDTH), BF16), jax.ShapeDtypeStruct((8, 2 * MLSTM_WIDTH), F32)],
        compiler_params=_cparams(("parallel", "arbitrary")),
    )(proj3, w8, dout_f, dout_b)


MLSTM_HEADS_PER_STEP = 4


def _chunk_masks(direction):
    t = lax.broadcasted_iota(jnp.int32, (MLSTM_CHUNK, MLSTM_CHUNK), 0)
    s = lax.broadcasted_iota(jnp.int32, (MLSTM_CHUNK, MLSTM_CHUNK), 1)
    le, ge = (s <= t).astype(F32), (s >= t).astype(F32)
    eye = (s == t).astype(F32)
    return (le, ge, eye) if direction == 0 else (ge, le, eye)


def _gate_cols(gates, direction, head):
    lane = lax.broadcasted_iota(jnp.int32, gates.shape, 1)
    sel_i = (lane == (2 * direction) * MLSTM_HEADS + head).astype(F32)
    sel_f = (lane == (2 * direction + 1) * MLSTM_HEADS + head).astype(F32)
    return sel_i, sel_f


def _mlstm_fwd(qk, proj3, bias):
    B, S, _ = qk.shape
    nc = S // MLSTM_CHUNK
    H, L, DH = MLSTM_HEADS, MLSTM_CHUNK, MLSTM_HEAD_DIM

    def chunk_of(d, c):
        return c if d == 0 else nc - 1 - c

    HS = MLSTM_HEADS_PER_STEP

    def body(qkf, qkb, vf, vb, gf, gb, bias_ref, hf, hb, csf, csb, nsf, nsb, msf, msb, c_st, n_st, m_st):
        c, hg = pl.program_id(1), pl.program_id(2)

        @pl.when(c == 0)
        def _():
            for d in range(2):
                for j in range(HS):
                    c_st[d, hg * HS + j] = jnp.zeros((DH, DH), F32)
                    n_st[d, hg * HS + j] = jnp.zeros((1, DH), F32)
                    m_st[d, hg * HS + j] = jnp.zeros((1, DH), F32)

        for d, (qk_ref, v_ref, g_ref, h_ref, cs, ns, ms) in enumerate(
                ((qkf, vf, gf, hf, csf, nsf, msf), (qkb, vb, gb, hb, csb, nsb, msb))):
            incl, incl_t, eye = _chunk_masks(d)
            gates = g_ref[0] + bias_ref[...]
            for j in range(HS):
                h = hg * HS + j
                sel_i, sel_f = _gate_cols(gates, d, h)
                li = jnp.sum(gates * sel_i, axis=1, keepdims=True)
                lf_pre = jnp.sum(gates * sel_f, axis=1, keepdims=True)
                c_in, n_in, m_in = c_st[d, h], n_st[d, h], m_st[d, h]
                cs[0, 0, j], ns[0, 0, j], ms[0, 0, j] = c_in, n_in, m_in
                hh, c_new, n_new, m_new = _mlstm_chunk(
                    qk_ref[0, :, 2 * j * DH:(2 * j + 1) * DH], qk_ref[0, :, (2 * j + 1) * DH:(2 * j + 2) * DH],
                    v_ref[0, :, j * DH:(j + 1) * DH], li, lf_pre, c_in, n_in,
                    jnp.max(m_in, axis=1, keepdims=True), incl, incl_t, eye)
                h_ref[0, :, j * DH:(j + 1) * DH] = hh
                c_st[d, h], n_st[d, h] = c_new, n_new
                m_st[d, h] = jnp.broadcast_to(m_new, (1, DH))

    def tok_spec(width, base, d, per_head):
        return pl.BlockSpec((1, L, width), lambda b, c, h: (b, chunk_of(d, c), base + (h if per_head else 0)))

    def st_spec(shape, d):
        return pl.BlockSpec((1, 1, HS) + shape, lambda b, c, h: (b, chunk_of(d, c), h, 0, 0))

    in_specs = [tok_spec(2 * HS * DH, 0, 0, True), tok_spec(2 * HS * DH, 0, 1, True),
                tok_spec(HS * DH, C_VM // (HS * DH), 0, True), tok_spec(HS * DH, C_VM // (HS * DH), 1, True),
                tok_spec(LANES, C_GATES // LANES, 0, False), tok_spec(LANES, C_GATES // LANES, 1, False),
                pl.BlockSpec((1, LANES), lambda b, c, h: (0, 0))]
    out_specs = [tok_spec(HS * DH, 0, 0, True), tok_spec(HS * DH, 0, 1, True),
                 st_spec((DH, DH), 0), st_spec((DH, DH), 1), st_spec((1, DH), 0), st_spec((1, DH), 1),
                 st_spec((1, DH), 0), st_spec((1, DH), 1)]
    hs = jax.ShapeDtypeStruct((B, S, H * DH), F32)
    cs = jax.ShapeDtypeStruct((B, nc, H, DH, DH), F32)
    vs = jax.ShapeDtypeStruct((B, nc, H, 1, DH), F32)
    return pl.pallas_call(
        body, name="mlstm_fwd", grid=(B, nc, H // HS), in_specs=in_specs, out_specs=out_specs,
        out_shape=[hs, hs, cs, cs, vs, vs, vs, vs],
        scratch_shapes=[pltpu.VMEM((2, H, DH, DH), F32), pltpu.VMEM((2, H, 1, DH), F32), pltpu.VMEM((2, H, 1, DH), F32)],
        compiler_params=_cparams(("parallel", "arbitrary", "arbitrary")),
    )(qk, qk, proj3, proj3, proj3, proj3, bias)


def _mlstm_bwd(qk, proj3, bias, states, dh):
    B, S, _ = qk.shape
    nc = S // MLSTM_CHUNK
    H, L, DH = MLSTM_HEADS, MLSTM_CHUNK, MLSTM_HEAD_DIM

    def chunk_of(d, c):
        return nc - 1 - c if d == 0 else c

    HS = MLSTM_HEADS_PER_STEP

    def body(qkf, qkb, vf, vb, gf, gb, bias_ref, csf, csb, nsf, nsb, msf, msb, dhf, dhb,
             dqkf, dqkb, dvf, dvb, dgf, dgb, dc_st, dn_st, dm_st):
        c, hg = pl.program_id(1), pl.program_id(2)

        @pl.when(c == 0)
        def _():
            for d in range(2):
                for j in range(HS):
                    dc_st[d, hg * HS + j] = jnp.zeros((DH, DH), F32)
                    dn_st[d, hg * HS + j] = jnp.zeros((1, DH), F32)
                    dm_st[d, hg * HS + j] = jnp.zeros((1, DH), F32)

        @pl.when(hg == 0)
        def _():
            dgf[...] = jnp.zeros_like(dgf)
            dgb[...] = jnp.zeros_like(dgb)

        for d, (qk_ref, v_ref, g_ref, cs, ns, ms, dh_ref, dqk_ref, dv_ref, dg_ref) in enumerate(
                ((qkf, vf, gf, csf, nsf, msf, dhf, dqkf, dvf, dgf), (qkb, vb, gb, csb, nsb, msb, dhb, dqkb, dvb, dgb))):
            incl, incl_t, eye = _chunk_masks(d)
            gates = g_ref[0] + bias_ref[...]
            dgates = jnp.zeros_like(gates)
            for j in range(HS):
                h = hg * HS + j
                sel_i, sel_f = _gate_cols(gates, d, h)
                li = jnp.sum(gates * sel_i, axis=1, keepdims=True)
                lf_pre = jnp.sum(gates * sel_f, axis=1, keepdims=True)
                m_in = jnp.max(ms[0, 0, j], axis=1, keepdims=True)
                _, vjp = jax.vjp(
                    functools.partial(_mlstm_chunk, incl=incl, incl_t=incl_t, eye=eye),
                    qk_ref[0, :, 2 * j * DH:(2 * j + 1) * DH], qk_ref[0, :, (2 * j + 1) * DH:(2 * j + 2) * DH],
                    v_ref[0, :, j * DH:(j + 1) * DH], li, lf_pre, cs[0, 0, j], ns[0, 0, j], m_in)
                dm_out = jnp.max(dm_st[d, h], axis=1, keepdims=True)
                dq, dk, dv, dli, dlf, dc, dn, dm = vjp((dh_ref[0, :, j * DH:(j + 1) * DH], dc_st[d, h], dn_st[d, h], dm_out))
                dqk_ref[0, :, 2 * j * DH:(2 * j + 1) * DH] = dq
                dqk_ref[0, :, (2 * j + 1) * DH:(2 * j + 2) * DH] = dk
                dv_ref[0, :, j * DH:(j + 1) * DH] = dv
                dgates += dli * sel_i + dlf * sel_f
                dc_st[d, h], dn_st[d, h] = dc, dn
                dm_st[d, h] = jnp.broadcast_to(dm, (1, DH))
            dg_ref[0] += dgates

    def tok_spec(width, base, d, per_head):
        return pl.BlockSpec((1, L, width), lambda b, c, h: (b, chunk_of(d, c), base + (h if per_head else 0)))

    def st_spec(shape, d):
        return pl.BlockSpec((1, 1, HS) + shape, lambda b, c, h: (b, chunk_of(d, c), h, 0, 0))

    in_specs = [tok_spec(2 * HS * DH, 0, 0, True), tok_spec(2 * HS * DH, 0, 1, True),
                tok_spec(HS * DH, C_VM // (HS * DH), 0, True), tok_spec(HS * DH, C_VM // (HS * DH), 1, True),
                tok_spec(LANES, C_GATES // LANES, 0, False), tok_spec(LANES, C_GATES // LANES, 1, False),
                pl.BlockSpec((1, LANES), lambda b, c, h: (0, 0)),
                st_spec((DH, DH), 0), st_spec((DH, DH), 1), st_spec((1, DH), 0), st_spec((1, DH), 1),
                st_spec((1, DH), 0), st_spec((1, DH), 1), tok_spec(HS * DH, 0, 0, True), tok_spec(HS * DH, 0, 1, True)]
    out_specs = [tok_spec(2 * HS * DH, 0, 0, True), tok_spec(2 * HS * DH, 0, 1, True),
                 tok_spec(HS * DH, 0, 0, True), tok_spec(HS * DH, 0, 1, True),
                 tok_spec(LANES, 0, 0, False), tok_spec(LANES, 0, 1, False)]
    qks = jax.ShapeDtypeStruct((B, S, 2 * H * DH), F32)
    vs = jax.ShapeDtypeStruct((B, S, H * DH), F32)
    gs = jax.ShapeDtypeStruct((B, S, LANES), F32)
    csf, csb, nsf, nsb, msf, msb = states
    return pl.pallas_call(
        body, name="mlstm_bwd", grid=(B, nc, H // HS), in_specs=in_specs, out_specs=out_specs,
        out_shape=[qks, qks, vs, vs, gs, gs],
        scratch_shapes=[pltpu.VMEM((2, H, DH, DH), F32), pltpu.VMEM((2, H, 1, DH), F32), pltpu.VMEM((2, H, 1, DH), F32)],
        compiler_params=_cparams(("parallel", "arbitrary", "arbitrary")),
    )(qk, qk, proj3, proj3, proj3, proj3, bias, csf, csb, nsf, nsb, msf, msb, dh, dh)


ROW_BLOCK = 256
FF_COLS = 512
FF_SHARD = D_FF // N_DEV
FF_SHARD_PAD = 384
FF_PAD = N_DEV * FF_SHARD_PAD


def _rms_fwd(name, x, g):
    T = x.shape[0]
    return _rowwise(name, lambda xv, gv: _rms(xv, gv), [_In(x), _In(g, rows=False)], [_Out(D_MODEL, BF16)], T, ROW_BLOCK)[0]


def _rms_bwd(name, x, g, dh, dres):
    T = x.shape[0]

    def fn(xv, gv, dhv, drv):
        _, vjp = jax.vjp(_rms, xv, gv)
        dx, dg = vjp(dhv)
        return drv + dx, dg

    return _rowwise(name, fn, [_In(x), _In(g, rows=False), _In(dh), _In(dres)],
                    [_Out(D_MODEL), _Out(D_MODEL, rows=False)], T, ROW_BLOCK)


def _mmw(name, a, w, mode, **kw):
    if isinstance(w, tuple):
        return _matmul(name, a, w[0], mode, bl=w[1], **kw)
    return _matmul(name, a, w, mode, **kw)


def _swiglu(gate, up):
    return _silu(gate) * up


def _ffn_in(name, h, wg, wu):
    (M, K), N = h.shape, wg.shape[1]
    tm, tn = _first_divisor(M, (1024, 512, 256, 128)), FF_COLS

    def body(h_ref, wg_ref, wu_ref, g_ref, u_ref, a_ref):
        hv = h_ref[...]
        gate = _mm(hv, wg_ref[...], _NN)
        up = _mm(hv, wu_ref[...], _NN)
        g_ref[...], u_ref[...] = gate, up
        a_ref[...] = _swiglu(gate, up).astype(BF16)

    w_spec = pl.BlockSpec((K, tn), lambda i, j: (0, j))
    o_spec = pl.BlockSpec((tm, tn), lambda i, j: (i, j))
    return pl.pallas_call(
        body, name=name, grid=(M // tm, N // tn), in_specs=[pl.BlockSpec((tm, K), lambda i, j: (i, 0)), w_spec, w_spec],
        out_specs=[o_spec, o_spec, o_spec],
        out_shape=[jax.ShapeDtypeStruct((M, N), F32), jax.ShapeDtypeStruct((M, N), F32), jax.ShapeDtypeStruct((M, N), BF16)],
        compiler_params=_cparams(("parallel", "parallel")),
    )(h, wg, wu)


def _ffn_dact(name, dx, wd, gate, up):
    (M, K), N = dx.shape, wd.shape[0]
    tm, tn = _first_divisor(M, (1024, 512, 256, 128)), FF_COLS

    def body(dx_ref, wd_ref, g_ref, u_ref, dg_ref, du_ref):
        dact = _mm(dx_ref[...], wd_ref[...], _NT) * 0.5
        _, vjp = jax.vjp(_swiglu, g_ref[...], u_ref[...])
        dgate, dup = vjp(dact)
        dg_ref[...], du_ref[...] = dgate.astype(BF16), dup.astype(BF16)

    o_spec = pl.BlockSpec((tm, tn), lambda i, j: (i, j))
    return pl.pallas_call(
        body, name=name, grid=(M // tm, N // tn),
        in_specs=[pl.BlockSpec((tm, K), lambda i, j: (i, 0)), pl.BlockSpec((tn, K), lambda i, j: (j, 0)), o_spec, o_spec],
        out_specs=[o_spec, o_spec],
        out_shape=[jax.ShapeDtypeStruct((M, N), BF16), jax.ShapeDtypeStruct((M, N), BF16)],
        compiler_params=_cparams(("parallel", "parallel")),
    )(dx, wd, gate, up)


def _ffn_dh(name, dgate, dup, wg, wu, dep):
    (M, K), N = dgate.shape, wg.shape[0]
    tm, tk = _first_divisor(M, (1024, 512, 256, 128)), _first_divisor(K, (1024, 512, 384, 256, 128))
    nk = K // tk

    def body(dg_ref, du_ref, wg_ref, wu_ref, dep_ref, o_ref, acc):
        k = pl.program_id(1)

        @pl.when(k == 0)
        def _():
            acc[...] = jnp.zeros_like(acc)

        acc[...] += _mm(dg_ref[...], wg_ref[...], _NT) + _mm(du_ref[...], wu_ref[...], _NT)

        @pl.when(k == nk - 1)
        def _():
            o_ref[...] = acc[...]

    a_spec = pl.BlockSpec((tm, tk), lambda i, k: (i, k))
    w_spec = pl.BlockSpec((N, tk), lambda i, k: (0, k))
    return pl.pallas_call(
        body, name=name, grid=(M // tm, nk), in_specs=[a_spec, a_spec, w_spec, w_spec, pl.BlockSpec(memory_space=pl.ANY)],
        out_specs=pl.BlockSpec((tm, N), lambda i, k: (i, 0)), out_shape=jax.ShapeDtypeStruct((M, N), F32),
        scratch_shapes=[pltpu.VMEM((tm, N), F32)], compiler_params=_cparams(("parallel", "arbitrary")),
    )(dgate, dup, wg, wu, dep)


def _ffn_fwd(tag, x, g, wg, wu, wd):
    h = _rms_fwd(tag + "_norm", x, g)
    gate, up, act = _ffn_in(tag + "_in", h, wg, wu)
    if callable(wd):
        wd = wd(act)
    out = _mmw(tag + "_down", act, wd, "nn", res=x, scale=0.5)
    return out, (x, h, gate, up, act), wd


def _ffn_bwd(tag, saved, g, wg, wu, wd, dx, on_dw):
    x, h, gate, up, act = saved
    dgate, dup = _ffn_dact(tag + "_dact", dx, wd, gate, up)
    dwd = _matmul(tag + "_dwd", act, dx, "tn", scale=0.5, out_dtype=BF16)
    dwg = _matmul(tag + "_dwg", h, dgate, "tn", out_dtype=BF16)
    dwu = _matmul(tag + "_dwu", h, dup, "tn", out_dtype=BF16)
    token = on_dw({tag + "_w_gate": dwg, tag + "_w_up": dwu, tag + "_w_down": dwd}, dwu)
    dh = _ffn_dh(tag + "_dh", dgate, dup, wg, wu, token)
    dx_new, dg = _rms_bwd(tag + "_dnorm", x, g, dh, dx)
    return dx_new, dg


def _rope_cos_sin(positions):
    half = ROPE_DIM // 2
    inv_freq = jnp.power(jnp.float32(ROPE_THETA), -jnp.arange(half, dtype=F32) * (2.0 / ROPE_DIM))
    head = jnp.zeros((ATT_HEAD_DIM,), F32).at[:ROPE_DIM].set(jnp.concatenate([inv_freq, inv_freq]))
    row = jnp.tile(head, LANES // ATT_HEAD_DIM)[None, :]
    T = positions.shape[0]
    return _rowwise("rope_tables", _rope_tables, [_In(positions), _In(row, rows=False)], [_Out(LANES), _Out(LANES)], T, 1024)


def _prep_fwd(name, src, width, base, g, cos, sin):
    return _rowwise(name, _qk_prep, [_In(src, width, base), _In(g, rows=False), _In(cos), _In(sin)],
                    [_Out(width)], src.shape[0], 512)[0]


def _prep_bwd(name, src, width, base, g, cos, sin, dout):
    def fn(tv, gv, cv, sv, dv):
        _, vjp = jax.vjp(lambda a, b: _qk_prep(a, b, cv, sv), tv, gv)
        return vjp(dv)

    return _rowwise(name, fn, [_In(src, width, base), _In(g, rows=False), _In(cos), _In(sin), _In(dout)],
                    [_Out(width, BF16), _Out(width, rows=False)], src.shape[0], 512)


def _to_heads(t, B, S, nh):
    return t.reshape(B, S, nh, ATT_HEAD_DIM).transpose(0, 2, 1, 3)


def _from_heads(t):
    B, nh, S, _ = t.shape
    return t.transpose(0, 2, 1, 3).reshape(B * S, nh * ATT_HEAD_DIM)


def _mix_fwd(x, cos, sin, B, S, p):
    T = B * S
    h = _rms_fwd("mix_norm", x, p["mix_norm"])
    proj = _matmul("mix_proj", h, p["w_in"], "nn")
    proj3 = proj.reshape(B, S, IN_PAD)
    q_gain = jnp.tile(p["attn_q_norm"], (1, ATT_HEADS))
    k_gain = jnp.tile(p["attn_k_norm"], (1, ATT_KV_HEADS))
    q_r = _prep_fwd("q_prep", proj, ATT_WIDTH, C_QA // ATT_WIDTH, q_gain, cos, sin)
    k_r = _prep_fwd("k_prep", proj, ATT_KV_WIDTH, C_KA // ATT_KV_WIDTH, k_gain, cos, sin)
    qh = q_r.reshape(B, S, ATT_WIDTH)
    kh = k_r.reshape(B, S, ATT_KV_WIDTH)
    sink = p["attn_sink"].reshape(ATT_KV_HEADS, ATT_GROUP, 1, 1)
    y_a = _attn_fwd(qh, kh, proj3, sink).reshape(T, ATT_WIDTH)

    qk_c = _conv_fwd(proj3, p["conv_w8"])
    hf, hb, *states = _mlstm_fwd(qk_c, proj3, p["gate_bias"])
    hf2, hb2 = hf.reshape(T, MLSTM_WIDTH), hb.reshape(T, MLSTM_WIDTH)
    DH = MLSTM_HEAD_DIM
    y_m = _rowwise("mlstm_out", _mlstm_combine,
                   [_In(hf2, DH, split=True), _In(hb2, DH, split=True), _In(proj, DH, C_OM // DH, split=True),
                    _In(p["mlstm_out_norm"], DH, split=True, rows=False)],
                   [_Out(MLSTM_WIDTH, BF16, DH, split=True)], T, 1024, ncol=MLSTM_HEADS)[0]

    za = _mmw("branch_a", y_a, p["w_branch_attn"], "nn")
    zm = _mmw("branch_m", y_m, p["w_branch_mlstm"], "nn")
    W = 512
    merged = _rowwise("merge", _merge,
                      [_In(proj, W, C_GMERGE // W, split=True), _In(proj, W, (C_GMERGE + D_MODEL) // W, split=True),
                       _In(za, W, split=True), _In(zm, W, split=True)],
                      [_Out(D_MODEL, BF16, W, split=True)], T, 512, ncol=D_MODEL // W)[0]
    out = _mmw("mix_out", merged, p["w_out"], "nn", res=x)
    saved = dict(x=x, h=h, proj=proj, q_gain=q_gain, k_gain=k_gain, qh=qh, kh=kh, sink=sink, y_a=y_a, qk_c=qk_c,
                 hf=hf2, hb=hb2, states=states, y_m=y_m, za=za, zm=zm, merged=merged)
    return out, saved


def _mix_bwd(sv, cos, sin, B, S, p, dx, on_dw):
    T = B * S
    DH = MLSTM_HEAD_DIM
    proj = sv["proj"]
    proj3 = proj.reshape(B, S, IN_PAD)
    g = {}
    dmerged = _mmw("mix_dmerged", dx, p["w_out"], "nt")
    g["w_out"] = _matmul("mix_dwout", sv["merged"], dx, "tn", out_dtype=BF16)
    W = 512

    def merge_bwd(ga, gm, za, zm, dm):
        _, vjp = jax.vjp(_merge, ga, gm, za, zm)
        return vjp(dm)

    dga, dgm, dza, dzm = _rowwise(
        "merge_bwd", merge_bwd,
        [_In(proj, W, C_GMERGE // W, split=True), _In(proj, W, (C_GMERGE + D_MODEL) // W, split=True),
         _In(sv["za"], W, split=True), _In(sv["zm"], W, split=True), _In(dmerged, W, split=True)],
        [_Out(D_MODEL, BF16, W, split=True), _Out(D_MODEL, BF16, W, split=True),
         _Out(D_MODEL, BF16, W, split=True), _Out(D_MODEL, BF16, W, split=True)], T, 512, ncol=D_MODEL // W)
    dya = _mmw("branch_a_dx", dza, p["w_branch_attn"], "nt")
    g["w_branch_attn"] = _matmul("branch_a_dw", sv["y_a"], dza, "tn", out_dtype=BF16)
    dym = _mmw("branch_m_dx", dzm, p["w_branch_mlstm"], "nt")
    g["w_branch_mlstm"] = _matmul("branch_m_dw", sv["y_m"], dzm, "tn", out_dtype=BF16)

    def combine_bwd(hf, hb, o_pre, gn, dy):
        _, vjp = jax.vjp(_mlstm_combine, hf, hb, o_pre, gn)
        dhf, _, do, dg = vjp(dy)
        return dhf, do, dg

    dh, dom, g["mlstm_out_norm"] = _rowwise(
        "mlstm_out_bwd", combine_bwd,
        [_In(sv["hf"], DH, split=True), _In(sv["hb"], DH, split=True), _In(proj, DH, C_OM // DH, split=True),
         _In(p["mlstm_out_norm"], DH, split=True, rows=False), _In(dym, DH, split=True)],
        [_Out(MLSTM_WIDTH, F32, DH, split=True), _Out(MLSTM_WIDTH, BF16, DH, split=True),
         _Out(MLSTM_WIDTH, F32, DH, split=True, rows=False)], T, 1024, ncol=MLSTM_HEADS)
    dqk_f, dqk_b, dv_f, dv_b, dg_f, dg_b = _mlstm_bwd(sv["qk_c"], proj3, p["gate_bias"], sv["states"],
                                                       dh.reshape(B, S, MLSTM_WIDTH))
    dgates, dvm, g["gate_bias"] = _rowwise(
        "mlstm_dsum", lambda a, b, c, d: (a + b, c + d, jnp.sum(a + b, axis=0, keepdims=True)),
        [_In(dg_f.reshape(T, LANES)), _In(dg_b.reshape(T, LANES)), _In(dv_f.reshape(T, MLSTM_WIDTH)), _In(dv_b.reshape(T, MLSTM_WIDTH))],
        [_Out(LANES, BF16), _Out(MLSTM_WIDTH, BF16), _Out(LANES, rows=False)], T, 1024)
    dqk, g["conv_w8"] = _conv_bwd(proj3, p["conv_w8"], dqk_f, dqk_b)

    dqh, dkh, dvh, dsink = _attn_bwd(sv["qh"], sv["kh"], proj3, sv["sink"], dya.reshape(B, S, ATT_WIDTH))
    g["attn_sink"] = dsink.reshape(1, ATT_HEADS)
    dva = dvh.reshape(T, ATT_KV_WIDTH)
    dqa, dq_gain = _prep_bwd("q_prep_bwd", proj, ATT_WIDTH, C_QA // ATT_WIDTH, sv["q_gain"], cos, sin,
                             dqh.reshape(T, ATT_WIDTH))
    dka, dk_gain = _prep_bwd("k_prep_bwd", proj, ATT_KV_WIDTH, C_KA // ATT_KV_WIDTH, sv["k_gain"], cos, sin,
                             dkh.reshape(T, ATT_KV_WIDTH))
    g["attn_q_norm"] = jnp.sum(dq_gain.reshape(ATT_HEADS, ATT_HEAD_DIM), axis=0, keepdims=True)
    g["attn_k_norm"] = jnp.sum(dk_gain.reshape(ATT_KV_HEADS, ATT_HEAD_DIM), axis=0, keepdims=True)

    dproj = jnp.concatenate(
        [dga, dgm, dqk.reshape(T, 2 * MLSTM_WIDTH), dvm, dom, dqa, dka, dva.astype(BF16), dgates], axis=1)
    dwin = _matmul("mix_dwin", sv["h"], dproj, "tn", out_dtype=BF16)
    token = on_dw({"w_in": _w_in_to_slots(dwin), "w_branch_attn": g.pop("w_branch_attn"),
                   "w_branch_mlstm": g.pop("w_branch_mlstm"), "w_out": g.pop("w_out")}, dwin)
    dh2 = _matmul("mix_dh", dproj, p["w_in"], "nt", dep=token)
    dx_new, g["mix_norm"] = _rms_bwd("mix_dnorm", sv["x"], p["mix_norm"], dh2, dx)
    return dx_new, g


def _loss_and_grad(x, g, target):
    T = x.shape[0]

    def loss_fn(xv, gv, tv):
        err = jnp.square(_rms(xv, gv) - tv)
        return 0.5 * jnp.sum(jnp.mean(err, axis=-1, keepdims=True), axis=0, keepdims=True)

    def fn(xv, gv, tv):
        val, vjp = jax.vjp(lambda a, b: loss_fn(a, b, tv), xv, gv)
        dx, dg = vjp(jnp.ones((1, 1), F32))
        return val, dx, dg

    return _rowwise("loss_head", fn, [_In(x), _In(g, rows=False), _In(target)],
                    [_Out(1, rows=False), _Out(D_MODEL), _Out(D_MODEL, rows=False)], T, ROW_BLOCK)


def _block_norm_fwd(x, g):
    T = x.shape[0]
    return _rowwise("block_norm", _rms, [_In(x), _In(g, rows=False)], [_Out(D_MODEL)], T, ROW_BLOCK)[0]


def _block_norm_bwd(x, g, dy):
    T = x.shape[0]

    def fn(xv, gv, dv):
        _, vjp = jax.vjp(_rms, xv, gv)
        return vjp(dv)

    return _rowwise("block_norm_bwd", fn, [_In(x), _In(g, rows=False), _In(dy)],
                    [_Out(D_MODEL), _Out(D_MODEL, rows=False)], T, ROW_BLOCK)


def _qk_perm_cols(t, axis):
    q, k = jnp.split(t, 2, axis=axis)
    parts = []
    for h in range(MLSTM_HEADS):
        sl = [slice(None)] * t.ndim
        sl[axis] = slice(h * MLSTM_HEAD_DIM, (h + 1) * MLSTM_HEAD_DIM)
        parts += [q[tuple(sl)], k[tuple(sl)]]
    return jnp.concatenate(parts, axis=axis)


def _qk_unperm_cols(t, axis):
    qs, ks = [], []
    for h in range(MLSTM_HEADS):
        sl = [slice(None)] * t.ndim
        sl[axis] = slice(2 * h * MLSTM_HEAD_DIM, (2 * h + 1) * MLSTM_HEAD_DIM)
        qs.append(t[tuple(sl)])
        sl[axis] = slice((2 * h + 1) * MLSTM_HEAD_DIM, (2 * h + 2) * MLSTM_HEAD_DIM)
        ks.append(t[tuple(sl)])
    return jnp.concatenate(qs + ks, axis=axis)


def _w_in_arrange(w):
    qa, ka, va, qm, km, vm, om, gm, gmerge = jnp.split(w, np.cumsum(
        (ATT_WIDTH, ATT_KV_WIDTH, ATT_KV_WIDTH, MLSTM_WIDTH, MLSTM_WIDTH, MLSTM_WIDTH, MLSTM_WIDTH, MLSTM_N_GATES))[:].tolist(), axis=1)
    qk = _qk_perm_cols(jnp.concatenate([qm, km], axis=1), 1)
    pad = jnp.zeros((w.shape[0], LANES - MLSTM_N_GATES), w.dtype)
    return jnp.concatenate([gmerge, qk, vm, om, qa, ka, va, gm, pad], axis=1)


def _w_in_restore(w):
    gmerge = w[:, C_GMERGE:C_GMERGE + 2 * D_MODEL]
    qk = _qk_unperm_cols(w[:, C_QK:C_QK + 2 * MLSTM_WIDTH], 1)
    vm, om = w[:, C_VM:C_VM + MLSTM_WIDTH], w[:, C_OM:C_OM + MLSTM_WIDTH]
    qa, ka, va = w[:, C_QA:C_QA + ATT_WIDTH], w[:, C_KA:C_KA + ATT_KV_WIDTH], w[:, C_VA:C_VA + ATT_KV_WIDTH]
    gm = w[:, C_GATES:C_GATES + MLSTM_N_GATES]
    return jnp.concatenate([qa, ka, va, qk, vm, om, gm, gmerge], axis=1)


BIG = ("ffn1_w_gate", "ffn1_w_up", "ffn1_w_down", "w_in", "mlstm_conv_w", "w_branch_attn", "w_branch_mlstm", "w_out",
       "ffn2_w_gate", "ffn2_w_up", "ffn2_w_down")
MATMUL_W = tuple(n for n in BIG if n != "mlstm_conv_w")
SMALL = ("ffn1_norm", "mix_norm", "mlstm_gate_bias", "attn_q_norm", "attn_k_norm", "attn_sink", "mlstm_conv_b",
         "mlstm_out_norm", "ffn2_norm", "block_out_norm")
WEIGHTS = ("ffn1_norm", "ffn1_w_gate", "ffn1_w_up", "ffn1_w_down", "mix_norm", "w_in", "mlstm_gate_bias", "attn_q_norm",
           "attn_k_norm", "attn_sink", "mlstm_conv_w", "mlstm_conv_b", "mlstm_out_norm", "w_branch_attn", "w_branch_mlstm",
           "w_out", "ffn2_norm", "ffn2_w_gate", "ffn2_w_up", "ffn2_w_down", "block_out_norm")
PACK_COLS = 1024


def _padded_rows(n_elems):
    return -(-n_elems // PACK_COLS)


def _pack_flat(arrs, dtype, row_multiple):
    parts = []
    for a in arrs:
        flat = a.reshape(-1).astype(dtype)
        pad = _padded_rows(flat.shape[0]) * PACK_COLS - flat.shape[0]
        parts.append(jnp.pad(flat, (0, pad)) if pad else flat)
    flat = jnp.concatenate(parts)
    rows = flat.shape[0] // PACK_COLS
    extra = (-rows) % row_multiple
    if extra:
        flat = jnp.pad(flat, (0, extra * PACK_COLS))
    return flat.reshape(-1, PACK_COLS)


def _unpack_flat(buf, shapes, lead=()):
    flat = buf.reshape(lead + (-1,))
    out, off = [], 0
    for s in shapes:
        n = int(np.prod(s))
        out.append(flat[..., off:off + n].reshape(lead + tuple(s)))
        off += _padded_rows(n) * PACK_COLS
    return out


class _Lay:
    def __init__(self, shard, axis, width):
        self.shard, self.axis, self.width = shard, axis, width
        self.padded = tuple(width if a == axis else s for a, s in enumerate(shard))
        self.whole = tuple(N_DEV * width if a == axis else s for a, s in enumerate(shard))

    def pad(self, t, lead=0):
        extra = self.width - self.shard[self.axis]
        if not extra:
            return t
        cfg = [(0, 0)] * t.ndim
        cfg[lead + self.axis] = (0, extra)
        return jnp.pad(t, cfg)

    def unpad(self, t, lead=0):
        idx = [slice(None)] * t.ndim
        idx[lead + self.axis] = slice(0, self.shard[self.axis])
        return t[tuple(idx)]


_FF_COL = _Lay((D_MODEL, FF_SHARD), 1, FF_SHARD_PAD)
_FF_ROW = _Lay((FF_SHARD, D_MODEL), 0, FF_SHARD_PAD)
LAYOUTS = {
    "ffn1_w_gate": _FF_COL, "ffn1_w_up": _FF_COL, "ffn1_w_down": _FF_ROW,
    "ffn2_w_gate": _FF_COL, "ffn2_w_up": _FF_COL, "ffn2_w_down": _FF_ROW,
    "w_in": _Lay((D_MODEL, IN_WIDTH // N_DEV), 0, D_MODEL),
    "mlstm_conv_w": _Lay((3, 2 * MLSTM_WIDTH // N_DEV), 1, 2 * MLSTM_WIDTH // N_DEV),
    "w_branch_attn": _Lay((ATT_WIDTH, D_MODEL // N_DEV), 1, D_MODEL // N_DEV),
    "w_branch_mlstm": _Lay((MLSTM_WIDTH, D_MODEL // N_DEV), 1, D_MODEL // N_DEV),
    "w_out": _Lay((D_MODEL // N_DEV, D_MODEL), 0, D_MODEL // N_DEV),
}


def _window(ref, axis, j, width):
    idx = [slice(None)] * len(ref.shape)
    idx[axis] = pl.ds(pl.multiple_of(j * width, width), width)
    return ref.at[tuple(idx)]


ANY = pl.BlockSpec(memory_space=pl.ANY)


def _mesh_pos():
    return lax.axis_index("x"), lax.axis_index("y"), lax.axis_index("c")


def _all_gather(name, shard, vmem=False):
    R, C = shard.shape
    space = pl.BlockSpec(memory_space=pltpu.VMEM) if vmem else ANY

    def body(x_ref, out_ref, send_sems, recv_sems, local_sem):
        x, y, c = _mesh_pos()
        me, sibling = (x, y, c), (x, y, 1 - c)
        chips = [(1 - x, y), (x, 1 - y), (1 - x, 1 - y)]

        def slot(px, py, pc):
            return out_ref.at[4 * px + 2 * py + pc]

        def copy(k, block, to, src=None):
            return pltpu.make_async_remote_copy(
                src_ref=slot(*block) if src is None else src, dst_ref=slot(*block),
                send_sem=send_sems.at[k], recv_sem=recv_sems.at[k], device_id=to, device_id_type=MESH)

        mine = pltpu.make_async_copy(x_ref, slot(*me), local_sem)
        mine.start()
        first = [copy(0, me, sibling, src=x_ref)]
        first += [copy(1 + j, me, (*chip, c), src=x_ref) for j, chip in enumerate(chips)]
        for cp in first:
            cp.start()
        passed = [copy(4 + j, (*chip, c), sibling) for j, chip in enumerate(chips)]
        for j, chip in enumerate(chips):
            copy(1 + j, (*chip, c), me).wait_recv()
            passed[j].start()
        copy(0, sibling, me).wait_recv()
        for j, chip in enumerate(chips):
            copy(4 + j, (*chip, 1 - c), me).wait_recv()
        for cp in first + passed:
            cp.wait_send()
        mine.wait()

    return pl.pallas_call(
        body, name=name, out_shape=jax.ShapeDtypeStruct((N_DEV, R, C), shard.dtype),
        in_specs=[space], out_specs=space,
        scratch_shapes=[pltpu.SemaphoreType.DMA((7,)), pltpu.SemaphoreType.DMA((7,)), pltpu.SemaphoreType.DMA],
    )(shard)


HBM = pl.BlockSpec(memory_space=pltpu.HBM)
SEM = pl.BlockSpec(memory_space=pltpu.SEMAPHORE)
SPLIT_COPY = pltpu.CompilerParams(has_side_effects=pltpu.SideEffectType.DATAFLOW_SIDE_EFFECTING)
N_PEERS = N_DEV - 1


def _peers(x, y, c):
    return [(x, y, 1 - c), (1 - x, y, c), (x, 1 - y, c), (1 - x, 1 - y, c),
            (1 - x, y, 1 - c), (x, 1 - y, 1 - c), (1 - x, 1 - y, 1 - c)]


def _dev_index(pos):
    return 4 * pos[0] + 2 * pos[1] + pos[2]


def _place_own(name, shards, lays):
    nt = len(shards)
    me = _dev_index(_mesh_pos())

    def body(me_ref, *refs):
        for x_ref, o_ref in zip(refs[:nt], refs[nt:]):
            o_ref[...] = x_ref[...]

    def window_spec(lay):
        if lay.axis == 0:
            return pl.BlockSpec(lay.padded, lambda i, me_ref: (me_ref[0], 0))
        return pl.BlockSpec(lay.padded, lambda i, me_ref: (0, me_ref[0]))

    return pl.pallas_call(
        body, name=name,
        grid_spec=pltpu.PrefetchScalarGridSpec(
            num_scalar_prefetch=1, grid=(1,),
            in_specs=[pl.BlockSpec(lay.padded, lambda i, me_ref: (0, 0)) for lay in lays],
            out_specs=[window_spec(lay) for lay in lays]),
        out_shape=[jax.ShapeDtypeStruct(lay.whole, s.dtype) for s, lay in zip(shards, lays)],
        compiler_params=_cparams(("arbitrary",)),
    )(me.reshape(1).astype(jnp.int32), *shards)


def _gather_start(name, shards, lands, lays, groups, after):
    nt, ng = len(shards), len(groups)

    def body(*refs):
        x_refs, land_refs = refs[:nt], refs[nt:2 * nt]
        sems = refs[2 * nt + 1:2 * nt + 1 + 2 * ng]
        pos = _mesh_pos()
        me = _dev_index(pos)
        for g, tens in enumerate(groups):
            for i, t in enumerate(tens):
                for k, peer in enumerate(_peers(*pos)):
                    pltpu.make_async_remote_copy(
                        src_ref=x_refs[t], dst_ref=_window(land_refs[t], lays[t].axis, me, lays[t].width),
                        send_sem=sems[2 * g].at[N_PEERS * i + k], recv_sem=sems[2 * g + 1].at[N_PEERS * i + k],
                        device_id=peer, device_id_type=MESH).start()

    sem_shapes = []
    for tens in groups:
        sem_shapes += [pltpu.SemaphoreType.DMA((N_PEERS * len(tens),))] * 2
    thru = [pltpu.HBM(s.shape, s.dtype) for s in shards] + [pltpu.HBM(lay.whole, s.dtype) for s, lay in zip(shards, lays)]
    args = [pltpu.with_memory_space_constraint(s, pltpu.HBM) for s in shards]
    args += [pltpu.with_memory_space_constraint(ld, pltpu.HBM) for ld in lands]
    res = pl.pallas_call(
        body, name=name, out_shape=tuple(sem_shapes + thru), in_specs=[HBM] * (2 * nt) + [ANY],
        out_specs=tuple([SEM] * (2 * ng) + [HBM] * (2 * nt)),
        input_output_aliases={t: 2 * ng + t for t in range(2 * nt)}, compiler_params=SPLIT_COPY,
    )(*args, after)
    sems = [(res[2 * g], res[2 * g + 1]) for g in range(ng)]
    return sems, list(res[2 * ng:2 * ng + nt]), list(res[2 * ng + nt:])


def _gather_wait(name, sems, shards, lands, lays, after):
    nt = len(shards)
    send_sems, recv_sems = sems

    def body(*refs):
        x_refs, land_refs = refs[:nt], refs[nt:2 * nt]
        send_ref, recv_ref = refs[2 * nt], refs[2 * nt + 1]
        pos = _mesh_pos()
        for t in range(nt):
            for k, peer in enumerate(_peers(*pos)):
                cp = pltpu.make_async_remote_copy(
                    src_ref=x_refs[t], dst_ref=_window(land_refs[t], lays[t].axis, _dev_index(peer), lays[t].width),
                    send_sem=send_ref.at[N_PEERS * t + k], recv_sem=recv_ref.at[N_PEERS * t + k],
                    device_id=peer, device_id_type=MESH)
                cp.wait_send()
                cp.wait_recv()

    thru = [pltpu.HBM(s.shape, s.dtype) for s in shards] + [pltpu.HBM(ld.shape, ld.dtype) for ld in lands]
    res = pl.pallas_call(
        body, name=name, out_shape=tuple(thru), in_specs=[HBM] * (2 * nt) + [SEM, SEM, ANY],
        out_specs=tuple([HBM] * (2 * nt)), input_output_aliases={t: t for t in range(2 * nt)},
        compiler_params=SPLIT_COPY,
    )(*shards, *lands, send_sems, recv_sems, after)
    return list(res[nt:])


def _pair_exchange(name, grads, lays):
    nt = len(grads)

    def body(*refs):
        g_refs, land_refs = refs[:nt], refs[nt:2 * nt]
        send_sems, recv_sems = refs[2 * nt:]
        x, y, c = _mesh_pos()
        copies = []
        for t in range(nt):
            for chip in range(4):
                copies.append(pltpu.make_async_remote_copy(
                    src_ref=_window(g_refs[t], lays[t].axis, 2 * chip + (1 - c), lays[t].width), dst_ref=land_refs[t].at[chip],
                    send_sem=send_sems.at[4 * t + chip], recv_sem=recv_sems.at[4 * t + chip],
                    device_id=(x, y, 1 - c), device_id_type=MESH))
        for cp in copies:
            cp.start()
        for cp in copies:
            cp.wait_recv()
        for cp in copies:
            cp.wait_send()

    out_shape = [jax.ShapeDtypeStruct((4,) + lay.padded, g.dtype) for g, lay in zip(grads, lays)]
    return pl.pallas_call(
        body, name=name, out_shape=out_shape, in_specs=[ANY] * nt, out_specs=[ANY] * nt,
        scratch_shapes=[pltpu.SemaphoreType.DMA((4 * nt,)), pltpu.SemaphoreType.DMA((4 * nt,))],
    )(*grads)


def _pair_sum(name, whole, landed, lay, out_dtype):
    R, C = lay.padded
    br = _first_divisor(R, (512, 384, 256, 128, 64, 32, 16, 8))
    nb = R // br
    if lay.axis == 0:
        mine_spec = pl.BlockSpec((br, C), lambda k, i, c_ref: ((2 * k + c_ref[0]) * nb + i, 0))
    else:
        mine_spec = pl.BlockSpec((br, C), lambda k, i, c_ref: (i, 2 * k + c_ref[0]))

    def body(c_ref, mine_ref, sib_ref, o_ref):
        o_ref[0] = (mine_ref[...].astype(F32) + sib_ref[0].astype(F32)).astype(out_dtype)

    c = lax.axis_index("c")
    return pl.pallas_call(
        body, name=name,
        grid_spec=pltpu.PrefetchScalarGridSpec(
            num_scalar_prefetch=1, grid=(4, nb),
            in_specs=[mine_spec, pl.BlockSpec((1, br, C), lambda k, i, c_ref: (k, i, 0))],
            out_specs=pl.BlockSpec((1, br, C), lambda k, i, c_ref: (k, i, 0))),
        out_shape=jax.ShapeDtypeStruct((4, R, C), out_dtype),
        compiler_params=_cparams(("parallel", "parallel")),
    )(c.reshape(1).astype(jnp.int32), whole, landed)


def _chip_exchange(name, sums):
    nt = len(sums)

    def body(*refs):
        s_refs, land_refs = refs[:nt], refs[nt:2 * nt]
        send_sems, recv_sems, local_sems = refs[2 * nt:]
        x, y, c = _mesh_pos()
        my_chip = 2 * x + y
        mine = [pltpu.make_async_copy(s_refs[t].at[my_chip], land_refs[t].at[my_chip], local_sems.at[t]) for t in range(nt)]
        for cp in mine:
            cp.start()
        chips = [(1 - x, y), (x, 1 - y), (1 - x, 1 - y)]
        copies = []
        for t in range(nt):
            for j, (px, py) in enumerate(chips):
                copies.append(pltpu.make_async_remote_copy(
                    src_ref=s_refs[t].at[2 * px + py], dst_ref=land_refs[t].at[my_chip],
                    send_sem=send_sems.at[3 * t + j], recv_sem=recv_sems.at[3 * t + j],
                    device_id=(px, py, c), device_id_type=MESH))
        for cp in copies:
            cp.start()
        for t in range(nt):
            for j, (px, py) in enumerate(chips):
                pltpu.make_async_remote_copy(
                    src_ref=s_refs[t].at[my_chip], dst_ref=land_refs[t].at[2 * px + py],
                    send_sem=send_sems.at[3 * t + j], recv_sem=recv_sems.at[3 * t + j],
                    device_id=(px, py, c), device_id_type=MESH).wait_recv()
        for cp in copies:
            cp.wait_send()
        for cp in mine:
            cp.wait()

    return pl.pallas_call(
        body, name=name, out_shape=[jax.ShapeDtypeStruct(s.shape, s.dtype) for s in sums],
        in_specs=[ANY] * nt, out_specs=[ANY] * nt,
        scratch_shapes=[pltpu.SemaphoreType.DMA((3 * nt,)), pltpu.SemaphoreType.DMA((3 * nt,)), pltpu.SemaphoreType.DMA((nt,))],
    )(*sums)


def _chip_start(name, sums):
    nt = len(sums)

    def body(*refs):
        s_refs, land_refs = refs[:nt], refs[nt:2 * nt]
        send_sems, recv_sems = refs[2 * nt], refs[2 * nt + 1]
        x, y, c = _mesh_pos()
        my_chip = 2 * x + y
        for t in range(nt):
            for j, (px, py) in enumerate([(1 - x, y), (x, 1 - y), (1 - x, 1 - y)]):
                pltpu.make_async_remote_copy(
                    src_ref=s_refs[t].at[2 * px + py], dst_ref=land_refs[t].at[my_chip],
                    send_sem=send_sems.at[3 * t + j], recv_sem=recv_sems.at[3 * t + j],
                    device_id=(px, py, c), device_id_type=MESH).start()

    thru = [pltpu.HBM(s.shape, s.dtype) for s in sums] * 2
    args = [pltpu.with_memory_space_constraint(s, pltpu.HBM) for s in sums]
    args += [pltpu.with_memory_space_constraint(lax.empty(s.shape, s.dtype), pltpu.HBM) for s in sums]
    res = pl.pallas_call(
        body, name=name, out_shape=tuple([pltpu.SemaphoreType.DMA((3 * nt,))] * 2 + thru), in_specs=[HBM] * (2 * nt),
        out_specs=tuple([SEM, SEM] + [HBM] * (2 * nt)), input_output_aliases={t: 2 + t for t in range(2 * nt)},
        compiler_params=SPLIT_COPY,
    )(*args)
    return (res[0], res[1]), list(res[2:2 + nt]), list(res[2 + nt:])


def _chip_wait(name, sems, sums, lands, after):
    nt = len(sums)

    def body(*refs):
        s_refs, land_refs = refs[:nt], refs[nt:2 * nt]
        send_sems, recv_sems = refs[2 * nt], refs[2 * nt + 1]
        x, y, c = _mesh_pos()
        my_chip = 2 * x + y
        for t in range(nt):
            for j, (px, py) in enumerate([(1 - x, y), (x, 1 - y), (1 - x, 1 - y)]):
                cp = pltpu.make_async_remote_copy(
                    src_ref=s_refs[t].at[my_chip], dst_ref=land_refs[t].at[2 * px + py],
                    send_sem=send_sems.at[3 * t + j], recv_sem=recv_sems.at[3 * t + j],
                    device_id=(px, py, c), device_id_type=MESH)
                cp.wait_send()
                cp.wait_recv()

    thru = [pltpu.HBM(s.shape, s.dtype) for s in sums] * 2
    res = pl.pallas_call(
        body, name=name, out_shape=tuple(thru), in_specs=[HBM] * (2 * nt) + [SEM, SEM, ANY],
        out_specs=tuple([HBM] * (2 * nt)), input_output_aliases={t: t for t in range(2 * nt)},
        compiler_params=SPLIT_COPY,
    )(*sums, *lands, sems[0], sems[1], after)
    return list(res[:nt]), list(res[nt:])


def _sum_chips(name, own, landed):
    _, R, C = own.shape
    br = _first_divisor(R, (512, 384, 256, 128, 64, 32, 16, 8))
    x, y, _ = _mesh_pos()
    slots = jnp.stack([2 * x + y, 2 * (1 - x) + y, 2 * x + (1 - y), 2 * (1 - x) + (1 - y)]).astype(jnp.int32)

    def body(slot_ref, mine_ref, a_ref, b_ref, c_ref, o_ref):
        o_ref[...] = ((mine_ref[0].astype(F32) + a_ref[0].astype(F32)) + b_ref[0].astype(F32)) + c_ref[0].astype(F32)

    def slot_spec(j):
        return pl.BlockSpec((1, br, C), lambda i, slot_ref: (slot_ref[j], i, 0))

    return pl.pallas_call(
        body, name=name,
        grid_spec=pltpu.PrefetchScalarGridSpec(
            num_scalar_prefetch=1, grid=(R // br,), in_specs=[slot_spec(0), slot_spec(1), slot_spec(2), slot_spec(3)],
            out_specs=pl.BlockSpec((br, C), lambda i, slot_ref: (i, 0))),
        out_shape=jax.ShapeDtypeStruct((R, C), F32), compiler_params=_cparams(("parallel",)),
    )(slots, own, landed, landed, landed)


def _sum_slots(name, slots, n):
    _, R, C = slots.shape
    br = _first_divisor(R, (512, 384, 256, 128, 64, 32, 16, 8))

    def body(s_ref, o_ref):
        acc = s_ref[0].astype(F32)
        for k in range(1, n):
            acc = acc + s_ref[k].astype(F32)
        o_ref[...] = acc

    return pl.pallas_call(
        body, name=name, grid=(R // br,), in_specs=[pl.BlockSpec((n, br, C), lambda i: (0, i, 0))],
        out_specs=pl.BlockSpec((br, C), lambda i: (i, 0)), out_shape=jax.ShapeDtypeStruct((R, C), F32),
        compiler_params=_cparams(("parallel",)),
    )(slots)


def _reduce_scatter_start(tag, names, grads):
    lays = [LAYOUTS[n] for n in names]
    landed = _pair_exchange("grads_pair_" + names[0], grads, lays)
    sums = [_pair_sum("grads_pairsum_" + n, g, ld, lay, BF16) for n, g, ld, lay in zip(names, grads, landed, lays)]
    sems, sums, lands = _chip_start(tag + "_chips_start", sums)
    return tag, names, sems, sums, lands


def _reduce_scatter_finish(pending, after):
    tag, names, sems, sums, lands = pending
    own, got = _chip_wait(tag + "_chips_wait", sems, sums, lands, after)
    return [_sum_chips("grads_sum_" + n, o, s) for n, o, s in zip(names, own, got)]


def _adamw_math(w, g, m, v):
    m = ADAM_B1 * m + (1.0 - ADAM_B1) * g
    v = ADAM_B2 * v + (1.0 - ADAM_B2) * jnp.square(g)
    m_hat = m / (1.0 - ADAM_B1 ** ADAM_STEP)
    v_hat = v / (1.0 - ADAM_B2 ** ADAM_STEP)
    delta = -ADAM_LR * (m_hat / (jnp.sqrt(v_hat) + ADAM_EPS) + ADAM_WD * w)
    return delta, m, v


def _adamw_layers(name, w, totals, m, v):
    _, R, C = w.shape
    br = _first_divisor(R, (512, 176, 128, 64, 32, 16, 8))
    Cp = totals[0].shape[1]

    def body(w_ref, g0_ref, g1_ref, m_ref, v_ref, g_out, d_out, m_out, v_out):
        g = jnp.where(pl.program_id(0) == 0, g0_ref[:, 0:C], g1_ref[:, 0:C])
        delta, m_new, v_new = _adamw_math(w_ref[0], g, m_ref[0], v_ref[0])
        g_out[0], d_out[0], m_out[0], v_out[0] = g, delta, m_new, v_new

    blk = pl.BlockSpec((1, br, C), lambda l, i: (l, i, 0))
    g_spec = pl.BlockSpec((br, Cp), lambda l, i: (i, 0))
    return pl.pallas_call(
        body, name=name, grid=(DEPTH, R // br), in_specs=[blk, g_spec, g_spec, blk, blk], out_specs=[blk] * 4,
        out_shape=[jax.ShapeDtypeStruct(w.shape, F32)] * 4, compiler_params=_cparams(("parallel", "parallel")),
    )(w, totals[0], totals[1], m, v)


def _adamw(name, w, g, m, v):
    shape = w.shape
    cols = shape[-1]
    rows = int(np.prod(shape[:-1]))
    br = _first_divisor(rows, (512, 352, 256, 128, 64, 32, 16, 8))
    args = [_In(a.reshape(rows, cols)) for a in (w, g, m, v)]
    outs = _rowwise(name, _adamw_math, args, [_Out(cols), _Out(cols), _Out(cols)], rows, br)
    return [o.reshape(shape) for o in outs]


GROUPS = {"ffn1": ("ffn1_w_gate", "ffn1_w_up", "ffn1_w_down"),
          "mix": ("w_in", "w_branch_attn", "w_branch_mlstm", "w_out"),
          "ffn2": ("ffn2_w_gate", "ffn2_w_up", "ffn2_w_down")}
GATHER_GROUPS = {"ffn1_in": ("ffn1_w_gate", "ffn1_w_up"), "ffn1_out": ("ffn1_w_down",),
                 "mix": ("w_in", "w_branch_attn", "w_branch_mlstm", "w_out"),
                 "ffn2_in": ("ffn2_w_gate", "ffn2_w_up"), "ffn2_out": ("ffn2_w_down",)}


def _small_params(small, conv_w, l):
    p = {}
    for n in ("ffn1_norm", "mix_norm", "ffn2_norm", "block_out_norm", "mlstm_out_norm", "attn_q_norm", "attn_k_norm"):
        p[n] = small[n][l][None, :]
    p["attn_sink"] = small["attn_sink"][l]
    p["gate_bias"] = jnp.pad(small["mlstm_gate_bias"][l], (0, LANES - MLSTM_N_GATES))[None, :]
    taps = _qk_perm_cols(conv_w[l], 1)
    conv_b = _qk_perm_cols(small["mlstm_conv_b"][l][None, :], 1)
    p["conv_w8"] = jnp.concatenate([taps, conv_b, jnp.zeros((4, 2 * MLSTM_WIDTH), F32)], axis=0)
    return p


def _w_in_from_slots(slots):
    w_in = slots.reshape(N_DEV, D_MODEL, IN_WIDTH // N_DEV).transpose(1, 0, 2).reshape(D_MODEL, IN_WIDTH)
    return _w_in_arrange(w_in)


def _w_in_to_slots(g):
    return _w_in_restore(g).reshape(D_MODEL, N_DEV, IN_WIDTH // N_DEV).transpose(1, 0, 2).reshape(
        N_DEV * D_MODEL, IN_WIDTH // N_DEV)


def _local_step(x, positions, target, weights_of, small, conv_w, on_grads):
    B, S, _ = x.shape
    T = B * S
    cos, sin = _rope_cos_sin(positions.reshape(T, 1))
    params = [_small_params(small, conv_w, l) for l in range(DEPTH)]
    xs = x.reshape(T, D_MODEL)
    tgt = target.reshape(T, D_MODEL)

    saved = []
    for l, p in enumerate(params):
        p.update(weights_of(l, "ffn1_in", xs))
        x1, s1, p["ffn1_w_down"] = _ffn_fwd("ffn1", xs, p["ffn1_norm"], p["ffn1_w_gate"], p["ffn1_w_up"],
                                            lambda after, l=l: weights_of(l, "ffn1_out", after)["ffn1_w_down"])
        p.update(weights_of(l, "mix", x1))
        p["w_in"] = _w_in_from_slots(p["w_in"])
        x2, s2 = _mix_fwd(x1, cos, sin, B, S, p)
        p.update(weights_of(l, "ffn2_in", x2))
        x3, s3, p["ffn2_w_down"] = _ffn_fwd("ffn2", x2, p["ffn2_norm"], p["ffn2_w_gate"], p["ffn2_w_up"],
                                            lambda after, l=l: weights_of(l, "ffn2_out", after)["ffn2_w_down"])
        saved.append((s1, s2, s3, x3))
        if l + 1 < DEPTH:
            xs = _block_norm_fwd(x3, p["block_out_norm"])

    sm = {n: [None] * DEPTH for n in SMALL + ("mlstm_conv_w",)}
    loss = None
    dx = None
    for l in reversed(range(DEPTH)):
        p = params[l]
        s1, s2, s3, x3 = saved[l]
        if l == DEPTH - 1:
            loss, dx, dgn = _loss_and_grad(x3, p["block_out_norm"], tgt)
        else:
            dx, dgn = _block_norm_bwd(x3, p["block_out_norm"], dx)
        sm["block_out_norm"][l] = dgn[0]
        dx, dg = _ffn_bwd("ffn2", s3, p["ffn2_norm"], p["ffn2_w_gate"], p["ffn2_w_up"], p["ffn2_w_down"], dx,
                          functools.partial(on_grads, l, "ffn2"))
        sm["ffn2_norm"][l] = dg[0]
        dx, g = _mix_bwd(s2, cos, sin, B, S, p, dx, functools.partial(on_grads, l, "mix"))
        dconv = _qk_unperm_cols(g["conv_w8"], 1)
        sm["mlstm_conv_w"][l] = dconv[0:3]
        sm["mlstm_conv_b"][l] = dconv[3]
        sm["mix_norm"][l] = g["mix_norm"][0]
        sm["mlstm_gate_bias"][l] = g["gate_bias"][0, :MLSTM_N_GATES]
        sm["attn_q_norm"][l], sm["attn_k_norm"][l] = g["attn_q_norm"][0], g["attn_k_norm"][0]
        sm["attn_sink"][l] = g["attn_sink"][0]
        sm["mlstm_out_norm"][l] = g["mlstm_out_norm"][0]
        dx, dg = _ffn_bwd("ffn1", s1, p["ffn1_norm"], p["ffn1_w_gate"], p["ffn1_w_up"], p["ffn1_w_down"], dx,
                          functools.partial(on_grads, l, "ffn1"))
        sm["ffn1_norm"][l] = dg[0]
    sm = {n: jnp.stack(v, axis=0) for n, v in sm.items()}
    return loss, dx.reshape(B, S, D_MODEL), sm


def kernel(x, positions, ffn1_norm, ffn1_w_gate, ffn1_w_up, ffn1_w_down, mix_norm, w_in, mlstm_gate_bias, attn_q_norm, attn_k_norm, attn_sink, mlstm_conv_w, mlstm_conv_b, mlstm_out_norm, w_branch_attn, w_branch_mlstm, w_out, ffn2_norm, ffn2_w_gate, ffn2_w_up, ffn2_w_down, block_out_norm, loss_target, m_ffn1_norm, m_ffn1_w_gate, m_ffn1_w_up, m_ffn1_w_down, m_mix_norm, m_w_in, m_mlstm_gate_bias, m_attn_q_norm, m_attn_k_norm, m_attn_sink, m_mlstm_conv_w, m_mlstm_conv_b, m_mlstm_out_norm, m_w_branch_attn, m_w_branch_mlstm, m_w_out, m_ffn2_norm, m_ffn2_w_gate, m_ffn2_w_up, m_ffn2_w_down, m_block_out_norm, v_ffn1_norm, v_ffn1_w_gate, v_ffn1_w_up, v_ffn1_w_down, v_mix_norm, v_w_in, v_mlstm_gate_bias, v_attn_q_norm, v_attn_k_norm, v_attn_sink, v_mlstm_conv_w, v_mlstm_conv_b, v_mlstm_out_norm, v_w_branch_attn, v_w_branch_mlstm, v_w_out, v_ffn2_norm, v_ffn2_w_gate, v_ffn2_w_up, v_ffn2_w_down, v_block_out_norm):
    args = locals()
    w = {n: args[n] for n in WEIGHTS}
    m = {n: args["m_" + n] for n in WEIGHTS}
    v = {n: args["v_" + n] for n in WEIGHTS}

    order = [(l, grp) for l in range(DEPTH) for grp in GATHER_GROUPS]
    keys = [(l, n) for l, grp in order for n in GATHER_GROUPS[grp]]
    lays = [LAYOUTS[n] for _, n in keys]
    shards = [lay.pad(w[n][l].astype(BF16)) for (l, n), lay in zip(keys, lays)]
    group_idx, at = {}, 0
    for l, grp in order:
        group_idx[(l, grp)] = list(range(at, at + len(GATHER_GROUPS[grp])))
        at += len(GATHER_GROUPS[grp])
    conv_shape = w["mlstm_conv_w"].shape
    conv_all = _all_gather("conv_all_gather", _pack_flat([w["mlstm_conv_w"]], F32, 8), vmem=True)
    conv_parts = _unpack_flat(conv_all, [conv_shape], lead=(N_DEV,))[0]
    conv_w = jnp.concatenate([conv_parts[j] for j in range(N_DEV)], axis=2)
    small = {n: w[n] for n in SMALL}

    lands = []
    for l, grp in order:
        idx = group_idx[(l, grp)]
        lands += _place_own("weights_place_" + grp, [shards[i] for i in idx], [lays[i] for i in idx])
    sems, shards, lands = _gather_start("weights_gather_start", shards, lands, lays, [group_idx[k] for k in order], conv_all)

    def weights_of(l, grp, after):
        idx = group_idx[(l, grp)]
        whole = _gather_wait(f"weights_gather_wait_{l}_{grp}", sems[order.index((l, grp))], [shards[i] for i in idx],
                             [lands[i] for i in idx], [lays[i] for i in idx], after)
        return dict(zip(GATHER_GROUPS[grp], whole))

    totals, pending = {}, []

    def finish(after):
        tag, names = pending[0][0], pending[0][1]
        for n, t in zip(names, _reduce_scatter_finish(pending.pop(0), after)):
            totals[(tag, n)] = t

    def on_grads(l, grp, g, after):
        if pending:
            finish(after)
        names = GROUPS[grp]
        pending.append(_reduce_scatter_start(f"grads_{l}_{grp}", names, [g[n] for n in names]))
        return pending[-1][3][0]

    loss, grad_x, small_g = _local_step(x, positions, loss_target, weights_of, small, conv_w, on_grads)
    finish(grad_x)
    grads, deltas, new_m, new_v = {}, {}, {}, {}
    for grp, names in GROUPS.items():
        for n in names:
            grads[n], deltas[n], new_m[n], new_v[n] = _adamw_layers(
                "adamw_" + n, w[n], [totals[(f"grads_{l}_{grp}", n)] for l in range(DEPTH)], m[n], v[n])

    small_names = SMALL + ("mlstm_conv_w",)
    small_shapes = [small_g[n].shape for n in small_names] + [(1, 1)]
    small_packed = _pack_flat([small_g[n] for n in small_names] + [loss], F32, 8)
    small_all = _all_gather("small_all_gather", small_packed, vmem=True)
    small_sum = _sum_slots("small_sum", small_all, N_DEV)
    *small_grads, loss_total = _unpack_flat(small_sum, small_shapes)
    grads.update(dict(zip(small_names, small_grads)))
    x_pos, y_pos, c_pos = _mesh_pos()
    grads["mlstm_conv_w"] = lax.dynamic_slice_in_dim(
        grads["mlstm_conv_w"], (4 * x_pos + 2 * y_pos + c_pos) * conv_shape[2], conv_shape[2], axis=2)

    n = "mlstm_conv_w"
    deltas[n], new_m[n], new_v[n] = _adamw("adamw_" + n, w[n], grads[n], m[n], v[n])
    sw, sg, smm, sv = (_pack_flat([d[n] for n in SMALL], F32, 8) for d in (w, grads, m, v))
    sd, snm, snv = _adamw("adamw_small", sw, sg, smm, sv)
    shapes = [w[n].shape for n in SMALL]
    for d, buf in ((deltas, sd), (new_m, snm), (new_v, snv)):
        d.update(dict(zip(SMALL, _unpack_flat(buf, shapes))))

    return (loss_total.reshape(()), grad_x, *[grads[n] for n in WEIGHTS], *[deltas[n] for n in WEIGHTS],
            *[new_m[n] for n in WEIGHTS], *[new_v[n] for n in WEIGHTS])
```

```python
import functools

import numpy as np
import jax
import jax.numpy as jnp
from jax import lax
from jax.experimental import pallas as pl
from jax.experimental.pallas import tpu as pltpu

F32 = jnp.float32
BF16 = jnp.bfloat16

D_MODEL = 1024
D_FF = 2816
ATT_HEAD_DIM = 64
ATT_HEADS = 8
ATT_KV_HEADS = 2
ATT_GROUP = ATT_HEADS // ATT_KV_HEADS
ATT_WIDTH = ATT_HEADS * ATT_HEAD_DIM
ATT_KV_WIDTH = ATT_KV_HEADS * ATT_HEAD_DIM
WINDOW = 128
ATT_BLOCK = 128
ROPE_DIM = 16
ROPE_THETA = 500000.0
MLSTM_HEADS = 4
MLSTM_HEAD_DIM = 128
MLSTM_WIDTH = MLSTM_HEADS * MLSTM_HEAD_DIM
MLSTM_CHUNK = 128
MLSTM_N_GATES = 4 * MLSTM_HEADS
NORM_EPS = 1e-6
IN_WIDTH = 4880
DEPTH = 2
N_DEV = 8

ADAM_LR = 0.001
ADAM_B1 = 0.9
ADAM_B2 = 0.999
ADAM_EPS = 1e-08
ADAM_WD = 0.01
ADAM_STEP = 10

LANES = 128
C_GMERGE = 0
C_QK = 2048
C_VM = 3072
C_OM = 3584
C_QA = 4096
C_KA = 4608
C_VA = 4736
C_GATES = 4864
IN_PAD = 4992

VMEM_LIMIT = 48 * 1024 * 1024

MESH = pl.DeviceIdType.MESH


def _cparams(sem):
    return pltpu.CompilerParams(dimension_semantics=sem, vmem_limit_bytes=VMEM_LIMIT)


def _first_divisor(n, cands):
    for c in cands:
        if n % c == 0:
            return c
    return n


_NN = ((1,), (0,))
_NT = ((1,), (1,))
_TN = ((0,), (0,))


def _mm(a, b, dims):
    return lax.dot_general(a.astype(BF16), b.astype(BF16), (dims, ((), ())), preferred_element_type=F32)


@jax.custom_vjp
def mm_nn(a, b):
    return _mm(a, b, _NN)


def _mm_nn_fwd(a, b):
    return _mm(a, b, _NN), (a, b)


def _mm_nn_bwd(res, g):
    a, b = res
    return _mm(g, b, _NT).astype(a.dtype), _mm(a, g, _TN).astype(b.dtype)


mm_nn.defvjp(_mm_nn_fwd, _mm_nn_bwd)


@jax.custom_vjp
def mm_nt(a, b):
    return _mm(a, b, _NT)


def _mm_nt_fwd(a, b):
    return _mm(a, b, _NT), (a, b)


def _mm_nt_bwd(res, g):
    a, b = res
    return _mm(g, b, _NN).astype(a.dtype), _mm(g, a, _TN).astype(b.dtype)


mm_nt.defvjp(_mm_nt_fwd, _mm_nt_bwd)


@jax.custom_vjp
def mm_tn(a, b):
    return _mm(a, b, _TN)


def _mm_tn_fwd(a, b):
    return _mm(a, b, _TN), (a, b)


def _mm_tn_bwd(res, g):
    a, b = res
    return _mm(b, g, _NT).astype(a.dtype), _mm(a, g, _NN).astype(b.dtype)


mm_tn.defvjp(_mm_tn_fwd, _mm_tn_bwd)


def _matmul(name, a, b, mode, out_dtype=F32, res=None, scale=1.0, bl=None, dep=None):
    b_shape = b.shape if bl is None else b.shape[1:]
    if mode == "nn":
        (M, K), (K2, N) = a.shape, b_shape
    elif mode == "nt":
        (M, K), (N, K2) = a.shape, b_shape
    else:
        (K, M), (K2, N) = a.shape, b_shape
    assert K == K2, (name, a.shape, b.shape)
    tm = _first_divisor(M, (1024, 512, 384, 256, 128))
    tn = _first_divisor(N, (1024, 1664, 512, 384, 256, 128))
    tk = _first_divisor(K, (1024, 1664, 512, 256, 128))
    nk = K // tk
    if mode == "tn":
        a_spec = pl.BlockSpec((tk, tm), lambda i, j, k: (k, i))
    else:
        a_spec = pl.BlockSpec((tm, tk), lambda i, j, k: (i, k))
    if mode == "nt":
        b_blk, b_idx = (tn, tk), (lambda i, j, k: (j, k))
    else:
        b_blk, b_idx = (tk, tn), (lambda i, j, k: (k, j))
    if bl is None:
        b_spec = pl.BlockSpec(b_blk, b_idx)
    else:
        b_spec = pl.BlockSpec((None,) + b_blk, lambda i, j, k: (bl,) + b_idx(i, j, k))
    o_spec = pl.BlockSpec((tm, tn), lambda i, j, k: (i, j))
    dims = {"nn": _NN, "nt": _NT, "tn": _TN}[mode]
    has_res = res is not None

    def body(*refs):
        a_ref, b_ref = refs[:2]
        r_ref = refs[2] if has_res else None

        def finish(out):
            if scale != 1.0:
                out = out * scale
            if has_res:
                out = r_ref[...].astype(F32) + out
            o_ref[...] = out.astype(out_dtype)

        if nk == 1:
            o_ref = refs[-1]
            finish(_mm(a_ref[...], b_ref[...], dims))
            return
        o_ref, acc = refs[-2:]
        k = pl.program_id(2)

        @pl.when(k == 0)
        def _():
            acc[...] = jnp.zeros_like(acc)

        acc[...] += _mm(a_ref[...], b_ref[...], dims)

        @pl.when(k == nk - 1)
        def _():
            finish(acc[...])

    in_specs = [a_spec, b_spec] + ([o_spec] if has_res else [])
    args = (a, b) + ((res,) if has_res else ())
    if dep is not None:
        in_specs.append(pl.BlockSpec(memory_space=pl.ANY))
        args += (dep,)
    return pl.pallas_call(
        body, name=name, grid=(M // tm, N // tn, nk), in_specs=in_specs, out_specs=o_spec,
        out_shape=jax.ShapeDtypeStruct((M, N), out_dtype),
        scratch_shapes=[pltpu.VMEM((tm, tn), F32)] if nk > 1 else [],
        compiler_params=_cparams(("parallel", "parallel", "arbitrary")),
    )(*args)


class _In:
    def __init__(self, arr, width=None, base=0, split=False, rows=True):
        self.arr, self.base, self.split, self.rows = arr, base, split, rows
        self.width = arr.shape[1] if width is None else width


class _Out:
    def __init__(self, cols, dtype=F32, width=None, split=False, rows=True, nrows=1):
        self.cols, self.dtype, self.split, self.rows, self.nrows = cols, dtype, split, rows, nrows
        self.width = cols if width is None else width


def _rowwise(name, fn, ins, outs, n_rows, br, ncol=1):
    br = min(br, n_rows)
    assert n_rows % br == 0, (name, n_rows, br)
    nrow_blocks = n_rows // br

    def in_spec(d):
        nb = br if d.rows else d.arr.shape[0]
        if d.rows and d.split:
            im = lambda j, i, base=d.base: (i, base + j)
        elif d.rows:
            im = lambda j, i, base=d.base: (i, base)
        elif d.split:
            im = lambda j, i, base=d.base: (0, base + j)
        else:
            im = lambda j, i, base=d.base: (0, base)
        return pl.BlockSpec((nb, d.width), im)

    def out_spec(d):
        nb = br if d.rows else d.nrows
        if d.rows and d.split:
            im = lambda j, i: (i, j)
        elif d.rows:
            im = lambda j, i: (i, 0)
        elif d.split:
            im = lambda j, i: (0, j)
        else:
            im = lambda j, i: (0, 0)
        return pl.BlockSpec((nb, d.width), im)

    n_in = len(ins)

    def body(*refs):
        i = pl.program_id(1)
        vals = [r[...] for r in refs[:n_in]]
        res = fn(*vals)
        if not isinstance(res, (tuple, list)):
            res = (res,)
        for d, ref, val in zip(outs, refs[n_in:], res):
            if d.rows:
                ref[...] = val.astype(d.dtype)
            else:
                @pl.when(i == 0)
                def _(ref=ref):
                    ref[...] = jnp.zeros_like(ref)

                ref[...] += val.astype(d.dtype)

    out_shape = [jax.ShapeDtypeStruct((n_rows if d.rows else d.nrows, d.cols), d.dtype) for d in outs]
    res = pl.pallas_call(
        body, name=name, grid=(ncol, nrow_blocks), in_specs=[in_spec(d) for d in ins],
        out_specs=[out_spec(d) for d in outs], out_shape=out_shape,
        compiler_params=_cparams(("parallel", "arbitrary")),
    )(*[d.arr for d in ins])
    return res


def _rms(x, g):
    return x * lax.rsqrt(jnp.mean(x * x, axis=-1, keepdims=True) + NORM_EPS) * g


def _sigmoid(x):
    return 0.5 * jnp.tanh(0.5 * x) + 0.5


def _silu(x):
    return x * _sigmoid(x)


def _log_sigmoid(x):
    return jnp.minimum(x, 0.0) - jnp.log(1.0 + jnp.exp(-jnp.abs(x)))


def _rope_tables(pos, inv_freq_row):
    ang = pos.astype(F32) * inv_freq_row
    return jnp.cos(ang), jnp.sin(ang)


def _head_sums_impl(v):
    w = v.shape[-1]
    shift = ATT_HEAD_DIM.bit_length() - 1
    r = lax.shift_right_logical(lax.broadcasted_iota(jnp.int32, (w, w), 0), shift)
    c = lax.shift_right_logical(lax.broadcasted_iota(jnp.int32, (w, w), 1), shift)
    ones = (r == c).astype(BF16)
    hi = v.astype(BF16)
    lo = (v - hi.astype(F32)).astype(BF16)
    dn = (_NN, ((), ()))
    return (lax.dot_general(hi, ones, dn, preferred_element_type=F32)
            + lax.dot_general(lo, ones, dn, preferred_element_type=F32))


@jax.custom_vjp
def _head_sums(v):
    return _head_sums_impl(v)


_head_sums.defvjp(lambda v: (_head_sums_impl(v), None), lambda _, g: (_head_sums_impl(g),))


def _rotate_half_impl(y):
    w = y.shape[-1]
    half = ROPE_DIM // 2
    lane = lax.broadcasted_iota(jnp.int32, y.shape, 1) & (ATT_HEAD_DIM - 1)
    above = pltpu.roll(y, w - half, axis=1)
    below = pltpu.roll(y, half, axis=1)
    return jnp.where(lane < half, -above, jnp.where(lane < ROPE_DIM, below, 0.0))


@jax.custom_vjp
def _rotate_half(y):
    return _rotate_half_impl(y)


_rotate_half.defvjp(lambda y: (_rotate_half_impl(y), None), lambda _, g: (-_rotate_half_impl(g),))


def _qk_prep(t, g, cos, sin):
    reps = t.shape[-1] // cos.shape[-1]
    if reps > 1:
        cos, sin = jnp.tile(cos, (1, reps)), jnp.tile(sin, (1, reps))
    y = t * lax.rsqrt(_head_sums(t * t) * (1.0 / ATT_HEAD_DIM) + NORM_EPS) * g
    return y * cos + _rotate_half(y) * sin


def _attn_head(q, kb, vb, sink, valid):
    s = mm_nt(q, kb) * (ATT_HEAD_DIM ** -0.5)
    s = jnp.where(valid, s, -jnp.inf)
    m = jnp.maximum(jnp.max(s, axis=-1, keepdims=True), sink)
    p = jnp.exp(s - m)
    den = jnp.sum(p, axis=-1, keepdims=True) + jnp.exp(sink - m)
    return mm_nn(p * (1.0 / den), vb)


def _mlstm_chunk(q, k, v, li, lf, C, n, m, incl, incl_t, eye):
    k = k * (MLSTM_HEAD_DIM ** -0.5)
    lf_row = jnp.sum(eye * lf, axis=0, keepdims=True)
    li_row = jnp.sum(eye * li, axis=0, keepdims=True)
    b = jnp.sum(incl * lf_row, axis=1, keepdims=True)
    b_row = jnp.sum(incl_t * lf, axis=0, keepdims=True)
    b_tot = jnp.sum(lf, axis=0, keepdims=True)
    a = b_tot - b + li
    a_max = jnp.max(a, axis=0, keepdims=True)
    kw = k * jnp.exp(a - a_max)
    c_loc = mm_tn(kw, v)
    n_loc = jnp.sum(kw, axis=0, keepdims=True)

    dmat = jnp.where(incl > 0.5, b - b_row + li_row, -jnp.inf)
    inter = b + m
    m_t = jnp.maximum(inter, jnp.max(dmat, axis=1, keepdims=True))
    sc = mm_nt(q, k) * jnp.exp(dmat - m_t)
    scale_in = jnp.exp(inter - m_t)
    num = mm_nn(sc, v) + scale_in * mm_nn(q, C)
    den = jnp.sum(sc, axis=1, keepdims=True) + scale_in * jnp.sum(q * n, axis=1, keepdims=True)
    h = num * (1.0 / jnp.maximum(jnp.abs(den), jnp.exp(-m_t)))

    m_new = jnp.maximum(b_tot + m, a_max)
    s_p = jnp.exp(b_tot + m - m_new)
    s_l = jnp.exp(a_max - m_new)
    return h, s_p * C + s_l * c_loc, s_p * n + s_l * n_loc, m_new


def _mlstm_combine(hf, hb, o_pre, g):
    h = hf + hb
    mu = jnp.mean(h, axis=-1, keepdims=True)
    var = jnp.mean(jnp.square(h - mu), axis=-1, keepdims=True)
    return _sigmoid(o_pre) * ((h - mu) * lax.rsqrt(var + NORM_EPS) * g)


def _merge(ga, gm, za, zm):
    return _sigmoid(ga) * za + _sigmoid(gm) * zm


def _attn_mask(n, seq):
    shape = (ATT_GROUP * ATT_BLOCK, 3 * ATT_BLOCK)
    qi = n * ATT_BLOCK + (lax.broadcasted_iota(jnp.int32, shape, 0) & (ATT_BLOCK - 1))
    kj = (n - 1) * ATT_BLOCK + lax.broadcasted_iota(jnp.int32, shape, 1)
    return (jnp.abs(qi - kj) <= WINDOW) & (kj >= 0) & (kj < seq)


def _attn_specs(nq, v_base):
    q_spec = pl.BlockSpec((1, ATT_BLOCK, ATT_WIDTH), lambda b, n: (b, n, 0))

    def kv_spec(off, base=0):
        return pl.BlockSpec((1, ATT_BLOCK, ATT_KV_WIDTH), lambda b, n: (b, jnp.clip(n + off, 0, nq - 1), base))

    sink_spec = pl.BlockSpec((ATT_KV_HEADS, ATT_GROUP, 1, 1), lambda b, n: (0, 0, 0, 0))
    specs = [q_spec, kv_spec(-1), kv_spec(0), kv_spec(1), kv_spec(-1, v_base), kv_spec(0, v_base), kv_spec(1, v_base), sink_spec]
    return q_spec, specs, sink_spec


def _head(h):
    return slice(h * ATT_HEAD_DIM, (h + 1) * ATT_HEAD_DIM)


def _group_rows(q_ref, s_ref, h):
    q4 = jnp.concatenate([q_ref[0, :, _head(h * ATT_GROUP + g)] for g in range(ATT_GROUP)], axis=0)
    sink4 = jnp.concatenate([jnp.broadcast_to(s_ref[h, g], (ATT_BLOCK, 1)) for g in range(ATT_GROUP)], axis=0)
    return q4, sink4


def _attn_fwd(q, k, proj3, sink):
    B, S, _ = q.shape
    nq = S // ATT_BLOCK
    q_spec, specs, _ = _attn_specs(nq, C_VA // ATT_KV_WIDTH)

    def body(q_ref, kp, kc, kn, vp, vc, vn, s_ref, o_ref):
        valid = _attn_mask(pl.program_id(1), S)
        for h in range(ATT_KV_HEADS):
            kb = jnp.concatenate([kp[0, :, _head(h)], kc[0, :, _head(h)], kn[0, :, _head(h)]], axis=0)
            vb = jnp.concatenate([vp[0, :, _head(h)], vc[0, :, _head(h)], vn[0, :, _head(h)]], axis=0)
            q4, sink4 = _group_rows(q_ref, s_ref, h)
            o4 = _attn_head(q4, kb, vb, sink4, valid).astype(BF16)
            for g in range(ATT_GROUP):
                o_ref[0, :, _head(h * ATT_GROUP + g)] = o4[g * ATT_BLOCK:(g + 1) * ATT_BLOCK]

    return pl.pallas_call(
        body, name="attn_fwd", grid=(B, nq), in_specs=specs,
        out_specs=q_spec, out_shape=jax.ShapeDtypeStruct(q.shape, BF16),
        compiler_params=_cparams(("parallel", "arbitrary")),
    )(q, k, k, k, proj3, proj3, proj3, sink)


def _attn_bwd(q, k, proj3, sink, dy):
    B, S, _ = q.shape
    nq = S // ATT_BLOCK
    q_spec, specs, sink_spec = _attn_specs(nq, C_VA // ATT_KV_WIDTH)
    kv_full = pl.BlockSpec((1, S, ATT_KV_WIDTH), lambda b, n: (b, 0, 0))

    def body(q_ref, kp, kc, kn, vp, vc, vn, s_ref, dy_ref, dq_ref, dk_ref, dv_ref, ds_ref):
        b, n = pl.program_id(0), pl.program_id(1)
        valid = _attn_mask(n, S)

        @pl.when(n == 0)
        def _():
            dk_ref[...] = jnp.zeros_like(dk_ref)
            dv_ref[...] = jnp.zeros_like(dv_ref)

        @pl.when((n == 0) & (b == 0))
        def _():
            ds_ref[...] = jnp.zeros_like(ds_ref)

        for h in range(ATT_KV_HEADS):
            kb = jnp.concatenate([kp[0, :, _head(h)], kc[0, :, _head(h)], kn[0, :, _head(h)]], axis=0)
            vb = jnp.concatenate([vp[0, :, _head(h)], vc[0, :, _head(h)], vn[0, :, _head(h)]], axis=0)
            q4, sink4 = _group_rows(q_ref, s_ref, h)
            dy4 = jnp.concatenate([dy_ref[0, :, _head(h * ATT_GROUP + g)] for g in range(ATT_GROUP)], axis=0)
            _, vjp = jax.vjp(functools.partial(_attn_head, valid=valid), q4, kb, vb, sink4)
            dq4, dkb, dvb, dsink4 = vjp(dy4)
            for g in range(ATT_GROUP):
                rows = slice(g * ATT_BLOCK, (g + 1) * ATT_BLOCK)
                dq_ref[0, :, _head(h * ATT_GROUP + g)] = dq4[rows]
                ds_ref[h, g] += jnp.sum(dsink4[rows], axis=0, keepdims=True)
            for j, off in enumerate((-1, 0, 1)):
                start = pl.multiple_of(jnp.clip(n + off, 0, nq - 1) * ATT_BLOCK, ATT_BLOCK)
                rows = pl.ds(start, ATT_BLOCK)
                dk_ref[0, rows, _head(h)] += dkb[j * ATT_BLOCK:(j + 1) * ATT_BLOCK]
                dv_ref[0, rows, _head(h)] += dvb[j * ATT_BLOCK:(j + 1) * ATT_BLOCK]

    kv_shape = jax.ShapeDtypeStruct(k.shape, F32)
    return pl.pallas_call(
        body, name="attn_bwd", grid=(B, nq), in_specs=specs + [q_spec],
        out_specs=[q_spec, kv_full, kv_full, sink_spec],
        out_shape=[jax.ShapeDtypeStruct(q.shape, F32), kv_shape, kv_shape, jax.ShapeDtypeStruct(sink.shape, F32)],
        compiler_params=_cparams(("arbitrary", "arbitrary")),
    )(q, k, k, k, proj3, proj3, proj3, sink, dy)


CONV_COLS = 256


def _conv_taps(u, seq):
    row = lax.broadcasted_iota(jnp.int32, u.shape, 0)
    prev = jnp.where(row == 0, 0.0, pltpu.roll(u, 1, axis=0))
    nxt = jnp.where(row == seq - 1, 0.0, pltpu.roll(u, seq - 1, axis=0))
    return prev, nxt


def _conv_fwd(proj3, w8):
    B, S, _ = proj3.shape
    ncb = 2 * MLSTM_WIDTH // CONV_COLS

    def body(u_ref, w_ref, o_ref):
        u = u_ref[0]
        prev, nxt = _conv_taps(u, S)
        o_ref[0] = _silu(prev * w_ref[0:1, :] + u * w_ref[1:2, :] + nxt * w_ref[2:3, :] + w_ref[3:4, :])

    return pl.pallas_call(
        body, name="conv_fwd", grid=(B, ncb),
        in_specs=[pl.BlockSpec((1, S, CONV_COLS), lambda b, c: (b, 0, C_QK // CONV_COLS + c)),
                  pl.BlockSpec((8, CONV_COLS), lambda b, c: (0, c))],
        out_specs=pl.BlockSpec((1, S, CONV_COLS), lambda b, c: (b, 0, c)),
        out_shape=jax.ShapeDtypeStruct((B, S, 2 * MLSTM_WIDTH), F32),
        compiler_params=_cparams(("parallel", "parallel")),
    )(proj3, w8)


def _conv_bwd(proj3, w8, dout_f, dout_b):
    B, S, _ = proj3.shape
    ncb = 2 * MLSTM_WIDTH // CONV_COLS

    def body(u_ref, w_ref, df_ref, db_ref, du_ref, dw_ref):
        b = pl.program_id(1)
        u = u_ref[0]
        prev, nxt = _conv_taps(u, S)
        w0, w1, w2 = w_ref[0:1, :], w_ref[1:2, :], w_ref[2:3, :]
        pre = prev * w0 + u * w1 + nxt * w2 + w_ref[3:4, :]
        sig = _sigmoid(pre)
        dpre = (df_ref[0] + db_ref[0]) * (sig * (1.0 + pre * (1.0 - sig)))
        dprev, dnxt = _conv_taps(dpre, S)
        du_ref[0] = (dnxt * w0 + dpre * w1 + dprev * w2).astype(BF16)

        @pl.when(b == 0)
        def _():
            dw_ref[...] = jnp.zeros_like(dw_ref)

        dw_ref[0:1, :] += jnp.sum(dpre * prev, axis=0, keepdims=True)
        dw_ref[1:2, :] += jnp.sum(dpre * u, axis=0, keepdims=True)
        dw_ref[2:3, :] += jnp.sum(dpre * nxt, axis=0, keepdims=True)
        dw_ref[3:4, :] += jnp.sum(dpre, axis=0, keepdims=True)

    blk = pl.BlockSpec((1, S, CONV_COLS), lambda c, b: (b, 0, c))
    return pl.pallas_call(
        body, name="conv_bwd", grid=(ncb, B),
        in_specs=[pl.BlockSpec((1, S, CONV_COLS), lambda c, b: (b, 0, C_QK // CONV_COLS + c)),
                  pl.BlockSpec((8, CONV_COLS), lambda c, b: (0, c)), blk, blk],
        out_specs=[blk, pl.BlockSpec((8, CONV_COLS), lambda c, b: (0, c))],
        out_shape=[jax.ShapeDtypeStruct((B, S, 2 * MLSTM_WIDTH), BF16), jax.ShapeDtypeStruct((8, 2 * MLSTM_WIDTH), F32)],
        compiler_params=_cparams(("parallel", "arbitrary")),
    )(proj3, w8, dout_f, dout_b)


MLSTM_HEADS_PER_STEP = 4


def _chunk_masks(direction):
    t = lax.broadcasted_iota(jnp.int32, (MLSTM_CHUNK, MLSTM_CHUNK), 0)
    s = lax.broadcasted_iota(jnp.int32, (MLSTM_CHUNK, MLSTM_CHUNK), 1)
    le, ge = (s <= t).astype(F32), (s >= t).astype(F32)
    eye = (s == t).astype(F32)
    return (le, ge, eye) if direction == 0 else (ge, le, eye)


def _gate_cols(gates, direction, head):
    lane = lax.broadcasted_iota(jnp.int32, gates.shape, 1)
    sel_i = (lane == (2 * direction) * MLSTM_HEADS + head).astype(F32)
    sel_f = (lane == (2 * direction + 1) * MLSTM_HEADS + head).astype(F32)
    return sel_i, sel_f


def _mlstm_fwd(qk, proj3, bias):
    B, S, _ = qk.shape
    nc = S // MLSTM_CHUNK
    H, L, DH = MLSTM_HEADS, MLSTM_CHUNK, MLSTM_HEAD_DIM

    def chunk_of(d, c):
        return c if d == 0 else nc - 1 - c

    HS = MLSTM_HEADS_PER_STEP

    def body(qkf, qkb, vf, vb, gf, gb, bias_ref, hf, hb, csf, csb, nsf, nsb, msf, msb, c_st, n_st, m_st):
        c, hg = pl.program_id(1), pl.program_id(2)

        @pl.when(c == 0)
        def _():
            for d in range(2):
                for j in range(HS):
                    c_st[d, hg * HS + j] = jnp.zeros((DH, DH), F32)
                    n_st[d, hg * HS + j] = jnp.zeros((1, DH), F32)
                    m_st[d, hg * HS + j] = jnp.zeros((1, DH), F32)

        for d, (qk_ref, v_ref, g_ref, h_ref, cs, ns, ms) in enumerate(
                ((qkf, vf, gf, hf, csf, nsf, msf), (qkb, vb, gb, hb, csb, nsb, msb))):
            incl, incl_t, eye = _chunk_masks(d)
            gates = g_ref[0] + bias_ref[...]
            log_f = _log_sigmoid(gates)
            for j in range(HS):
                h = hg * HS + j
                sel_i, sel_f = _gate_cols(gates, d, h)
                li = jnp.sum(gates * sel_i, axis=1, keepdims=True)
                lf = jnp.sum(log_f * sel_f, axis=1, keepdims=True)
                c_in, n_in, m_in = c_st[d, h], n_st[d, h], m_st[d, h]
                cs[0, 0, j], ns[0, 0, j], ms[0, 0, j] = c_in, n_in, m_in
                hh, c_new, n_new, m_new = _mlstm_chunk(
                    qk_ref[0, :, 2 * j * DH:(2 * j + 1) * DH], qk_ref[0, :, (2 * j + 1) * DH:(2 * j + 2) * DH],
                    v_ref[0, :, j * DH:(j + 1) * DH], li, lf, c_in, n_in,
                    jnp.max(m_in, axis=1, keepdims=True), incl, incl_t, eye)
                h_ref[0, :, j * DH:(j + 1) * DH] = hh
                c_st[d, h], n_st[d, h] = c_new, n_new
                m_st[d, h] = jnp.broadcast_to(m_new, (1, DH))

    def tok_spec(width, base, d, per_head):
        return pl.BlockSpec((1, L, width), lambda b, c, h: (b, chunk_of(d, c), base + (h if per_head else 0)))

    def st_spec(shape, d):
        return pl.BlockSpec((1, 1, HS) + shape, lambda b, c, h: (b, chunk_of(d, c), h, 0, 0))

    in_specs = [tok_spec(2 * HS * DH, 0, 0, True), tok_spec(2 * HS * DH, 0, 1, True),
                tok_spec(HS * DH, C_VM // (HS * DH), 0, True), tok_spec(HS * DH, C_VM // (HS * DH), 1, True),
                tok_spec(LANES, C_GATES // LANES, 0, False), tok_spec(LANES, C_GATES // LANES, 1, False),
                pl.BlockSpec((1, LANES), lambda b, c, h: (0, 0))]
    out_specs = [tok_spec(HS * DH, 0, 0, True), tok_spec(HS * DH, 0, 1, True),
                 st_spec((DH, DH), 0), st_spec((DH, DH), 1), st_spec((1, DH), 0), st_spec((1, DH), 1),
                 st_spec((1, DH), 0), st_spec((1, DH), 1)]
    hs = jax.ShapeDtypeStruct((B, S, H * DH), F32)
    cs = jax.ShapeDtypeStruct((B, nc, H, DH, DH), F32)
    vs = jax.ShapeDtypeStruct((B, nc, H, 1, DH), F32)
    return pl.pallas_call(
        body, name="mlstm_fwd", grid=(B, nc, H // HS), in_specs=in_specs, out_specs=out_specs,
        out_shape=[hs, hs, cs, cs, vs, vs, vs, vs],
        scratch_shapes=[pltpu.VMEM((2, H, DH, DH), F32), pltpu.VMEM((2, H, 1, DH), F32), pltpu.VMEM((2, H, 1, DH), F32)],
        compiler_params=_cparams(("parallel", "arbitrary", "arbitrary")),
    )(qk, qk, proj3, proj3, proj3, proj3, bias)


def _mlstm_bwd(qk, proj3, bias, states, dh):
    B, S, _ = qk.shape
    nc = S // MLSTM_CHUNK
    H, L, DH = MLSTM_HEADS, MLSTM_CHUNK, MLSTM_HEAD_DIM

    def chunk_of(d, c):
        return nc - 1 - c if d == 0 else c

    HS = MLSTM_HEADS_PER_STEP

    def body(qkf, qkb, vf, vb, gf, gb, bias_ref, csf, csb, nsf, nsb, msf, msb, dhf, dhb,
             dqkf, dqkb, dvf, dvb, dgf, dgb, dc_st, dn_st, dm_st):
        c, hg = pl.program_id(1), pl.program_id(2)

        @pl.when(c == 0)
        def _():
            for d in range(2):
                for j in range(HS):
                    dc_st[d, hg * HS + j] = jnp.zeros((DH, DH), F32)
                    dn_st[d, hg * HS + j] = jnp.zeros((1, DH), F32)
                    dm_st[d, hg * HS + j] = jnp.zeros((1, DH), F32)

        @pl.when(hg == 0)
        def _():
            dgf[...] = jnp.zeros_like(dgf)
            dgb[...] = jnp.zeros_like(dgb)

        for d, (qk_ref, v_ref, g_ref, cs, ns, ms, dh_ref, dqk_ref, dv_ref, dg_ref) in enumerate(
                ((qkf, vf, gf, csf, nsf, msf, dhf, dqkf, dvf, dgf), (qkb, vb, gb, csb, nsb, msb, dhb, dqkb, dvb, dgb))):
            incl, incl_t, eye = _chunk_masks(d)
            gates = g_ref[0] + bias_ref[...]
            log_f = _log_sigmoid(gates)
            d_li = jnp.zeros_like(gates)
            d_lf = jnp.zeros_like(gates)
            for j in range(HS):
                h = hg * HS + j
                sel_i, sel_f = _gate_cols(gates, d, h)
                li = jnp.sum(gates * sel_i, axis=1, keepdims=True)
                lf = jnp.sum(log_f * sel_f, axis=1, keepdims=True)
                m_in = jnp.max(ms[0, 0, j], axis=1, keepdims=True)
                _, vjp = jax.vjp(
                    functools.partial(_mlstm_chunk, incl=incl, incl_t=incl_t, eye=eye),
                    qk_ref[0, :, 2 * j * DH:(2 * j + 1) * DH], qk_ref[0, :, (2 * j + 1) * DH:(2 * j + 2) * DH],
                    v_ref[0, :, j * DH:(j + 1) * DH], li, lf, cs[0, 0, j], ns[0, 0, j], m_in)
                dm_out = jnp.max(dm_st[d, h], axis=1, keepdims=True)
                dq, dk, dv, dli, dlf, dc, dn, dm = vjp((dh_ref[0, :, j * DH:(j + 1) * DH], dc_st[d, h], dn_st[d, h], dm_out))
                dqk_ref[0, :, 2 * j * DH:(2 * j + 1) * DH] = dq
                dqk_ref[0, :, (2 * j + 1) * DH:(2 * j + 2) * DH] = dk
                dv_ref[0, :, j * DH:(j + 1) * DH] = dv
                d_li += dli * sel_i
                d_lf += dlf * sel_f
                dc_st[d, h], dn_st[d, h] = dc, dn
                dm_st[d, h] = jnp.broadcast_to(dm, (1, DH))
            dg_ref[0] += d_li + d_lf * _sigmoid(-gates)

    def tok_spec(width, base, d, per_head):
        return pl.BlockSpec((1, L, width), lambda b, c, h: (b, chunk_of(d, c), base + (h if per_head else 0)))

    def st_spec(shape, d):
        return pl.BlockSpec((1, 1, HS) + shape, lambda b, c, h: (b, chunk_of(d, c), h, 0, 0))

    in_specs = [tok_spec(2 * HS * DH, 0, 0, True), tok_spec(2 * HS * DH, 0, 1, True),
                tok_spec(HS * DH, C_VM // (HS * DH), 0, True), tok_spec(HS * DH, C_VM // (HS * DH), 1, True),
                tok_spec(LANES, C_GATES // LANES, 0, False), tok_spec(LANES, C_GATES // LANES, 1, False),
                pl.BlockSpec((1, LANES), lambda b, c, h: (0, 0)),
                st_spec((DH, DH), 0), st_spec((DH, DH), 1), st_spec((1, DH), 0), st_spec((1, DH), 1),
                st_spec((1, DH), 0), st_spec((1, DH), 1), tok_spec(HS * DH, 0, 0, True), tok_spec(HS * DH, 0, 1, True)]
    out_specs = [tok_spec(2 * HS * DH, 0, 0, True), tok_spec(2 * HS * DH, 0, 1, True),
                 tok_spec(HS * DH, 0, 0, True), tok_spec(HS * DH, 0, 1, True),
                 tok_spec(LANES, 0, 0, False), tok_spec(LANES, 0, 1, False)]
    qks = jax.ShapeDtypeStruct((B, S, 2 * H * DH), F32)
    vs = jax.ShapeDtypeStruct((B, S, H * DH), F32)
    gs = jax.ShapeDtypeStruct((B, S, LANES), F32)
    csf, csb, nsf, nsb, msf, msb = states
    return pl.pallas_call(
        body, name="mlstm_bwd", grid=(B, nc, H // HS), in_specs=in_specs, out_specs=out_specs,
        out_shape=[qks, qks, vs, vs, gs, gs],
        scratch_shapes=[pltpu.VMEM((2, H, DH, DH), F32), pltpu.VMEM((2, H, 1, DH), F32), pltpu.VMEM((2, H, 1, DH), F32)],
        compiler_params=_cparams(("parallel", "arbitrary", "arbitrary")),
    )(qk, qk, proj3, proj3, proj3, proj3, bias, csf, csb, nsf, nsb, msf, msb, dh, dh)


ROW_BLOCK = 256
FF_COLS = 512
FF_SHARD = D_FF // N_DEV
FF_SHARD_PAD = 384
FF_PAD = N_DEV * FF_SHARD_PAD


def _rms_fwd(name, x, g):
    T = x.shape[0]
    return _rowwise(name, lambda xv, gv: _rms(xv, gv), [_In(x), _In(g, rows=False)], [_Out(D_MODEL, BF16)], T, ROW_BLOCK)[0]


def _rms_bwd(name, x, g, dh, dres):
    T = x.shape[0]

    def fn(xv, gv, dhv, drv):
        _, vjp = jax.vjp(_rms, xv, gv)
        dx, dg = vjp(dhv)
        return drv + dx, dg

    return _rowwise(name, fn, [_In(x), _In(g, rows=False), _In(dh), _In(dres)],
                    [_Out(D_MODEL), _Out(D_MODEL, rows=False)], T, ROW_BLOCK)


def _mmw(name, a, w, mode, **kw):
    if isinstance(w, tuple):
        return _matmul(name, a, w[0], mode, bl=w[1], **kw)
    return _matmul(name, a, w, mode, **kw)


def _swiglu(gate, up):
    return _silu(gate) * up


def _ffn_in(name, h, wg, wu):
    (M, K), N = h.shape, wg.shape[1]
    tm, tn = _first_divisor(M, (1024, 512, 256, 128)), FF_COLS

    def body(h_ref, wg_ref, wu_ref, g_ref, u_ref, a_ref):
        hv = h_ref[...]
        gate = _mm(hv, wg_ref[...], _NN)
        up = _mm(hv, wu_ref[...], _NN)
        g_ref[...], u_ref[...] = gate, up
        a_ref[...] = _swiglu(gate, up).astype(BF16)

    w_spec = pl.BlockSpec((K, tn), lambda i, j: (0, j))
    o_spec = pl.BlockSpec((tm, tn), lambda i, j: (i, j))
    return pl.pallas_call(
        body, name=name, grid=(M // tm, N // tn), in_specs=[pl.BlockSpec((tm, K), lambda i, j: (i, 0)), w_spec, w_spec],
        out_specs=[o_spec, o_spec, o_spec],
        out_shape=[jax.ShapeDtypeStruct((M, N), F32), jax.ShapeDtypeStruct((M, N), F32), jax.ShapeDtypeStruct((M, N), BF16)],
        compiler_params=_cparams(("parallel", "parallel")),
    )(h, wg, wu)


def _ffn_dact(name, dx, wd, gate, up):
    (M, K), N = dx.shape, wd.shape[0]
    tm, tn = _first_divisor(M, (1024, 512, 256, 128)), FF_COLS

    def body(dx_ref, wd_ref, g_ref, u_ref, dg_ref, du_ref):
        dact = _mm(dx_ref[...], wd_ref[...], _NT) * 0.5
        _, vjp = jax.vjp(_swiglu, g_ref[...], u_ref[...])
        dgate, dup = vjp(dact)
        dg_ref[...], du_ref[...] = dgate.astype(BF16), dup.astype(BF16)

    o_spec = pl.BlockSpec((tm, tn), lambda i, j: (i, j))
    return pl.pallas_call(
        body, name=name, grid=(M // tm, N // tn),
        in_specs=[pl.BlockSpec((tm, K), lambda i, j: (i, 0)), pl.BlockSpec((tn, K), lambda i, j: (j, 0)), o_spec, o_spec],
        out_specs=[o_spec, o_spec],
        out_shape=[jax.ShapeDtypeStruct((M, N), BF16), jax.ShapeDtypeStruct((M, N), BF16)],
        compiler_params=_cparams(("parallel", "parallel")),
    )(dx, wd, gate, up)


def _ffn_dh(name, dgate, dup, wg, wu, dep):
    (M, K), N = dgate.shape, wg.shape[0]
    tm, tk = _first_divisor(M, (1024, 512, 256, 128)), _first_divisor(K, (1024, 512, 384, 256, 128))
    nk = K // tk

    def body(dg_ref, du_ref, wg_ref, wu_ref, dep_ref, o_ref, acc):
        k = pl.program_id(1)

        @pl.when(k == 0)
        def _():
            acc[...] = jnp.zeros_like(acc)

        acc[...] += _mm(dg_ref[...], wg_ref[...], _NT) + _mm(du_ref[...], wu_ref[...], _NT)

        @pl.when(k == nk - 1)
        def _():
            o_ref[...] = acc[...]

    a_spec = pl.BlockSpec((tm, tk), lambda i, k: (i, k))
    w_spec = pl.BlockSpec((N, tk), lambda i, k: (0, k))
    return pl.pallas_call(
        body, name=name, grid=(M // tm, nk), in_specs=[a_spec, a_spec, w_spec, w_spec, pl.BlockSpec(memory_space=pl.ANY)],
        out_specs=pl.BlockSpec((tm, N), lambda i, k: (i, 0)), out_shape=jax.ShapeDtypeStruct((M, N), F32),
        scratch_shapes=[pltpu.VMEM((tm, N), F32)], compiler_params=_cparams(("parallel", "arbitrary")),
    )(dgate, dup, wg, wu, dep)


def _ffn_fwd(tag, x, g, wg, wu, wd):
    h = _rms_fwd(tag + "_norm", x, g)
    gate, up, act = _ffn_in(tag + "_in", h, wg, wu)
    if callable(wd):
        wd = wd(act)
    out = _mmw(tag + "_down", act, wd, "nn", res=x, scale=0.5)
    return out, (x, h, gate, up, act), wd


def _ffn_bwd(tag, saved, g, wg, wu, wd, dx, on_dw):
    x, h, gate, up, act = saved
    dgate, dup = _ffn_dact(tag + "_dact", dx, wd, gate, up)
    dwd = _matmul(tag + "_dwd", act, dx, "tn", scale=0.5, out_dtype=BF16)
    dwg = _matmul(tag + "_dwg", h, dgate, "tn", out_dtype=BF16)
    dwu = _matmul(tag + "_dwu", h, dup, "tn", out_dtype=BF16)
    token = on_dw({tag + "_w_gate": dwg, tag + "_w_up": dwu, tag + "_w_down": dwd}, dwu)
    dh = _ffn_dh(tag + "_dh", dgate, dup, wg, wu, token)
    dx_new, dg = _rms_bwd(tag + "_dnorm", x, g, dh, dx)
    return dx_new, dg


def _rope_cos_sin(positions):
    half = ROPE_DIM // 2
    inv_freq = jnp.power(jnp.float32(ROPE_THETA), -jnp.arange(half, dtype=F32) * (2.0 / ROPE_DIM))
    head = jnp.zeros((ATT_HEAD_DIM,), F32).at[:ROPE_DIM].set(jnp.concatenate([inv_freq, inv_freq]))
    row = jnp.tile(head, LANES // ATT_HEAD_DIM)[None, :]
    T = positions.shape[0]
    return _rowwise("rope_tables", _rope_tables, [_In(positions), _In(row, rows=False)], [_Out(LANES), _Out(LANES)], T, 1024)


def _prep_fwd(name, src, width, base, g, cos, sin):
    return _rowwise(name, _qk_prep, [_In(src, width, base), _In(g, rows=False), _In(cos), _In(sin)],
                    [_Out(width)], src.shape[0], 512)[0]


def _prep_bwd(name, src, width, base, g, cos, sin, dout):
    def fn(tv, gv, cv, sv, dv):
        _, vjp = jax.vjp(lambda a, b: _qk_prep(a, b, cv, sv), tv, gv)
        return vjp(dv)

    return _rowwise(name, fn, [_In(src, width, base), _In(g, rows=False), _In(cos), _In(sin), _In(dout)],
                    [_Out(width, BF16), _Out(width, rows=False)], src.shape[0], 512)


def _to_heads(t, B, S, nh):
    return t.reshape(B, S, nh, ATT_HEAD_DIM).transpose(0, 2, 1, 3)


def _from_heads(t):
    B, nh, S, _ = t.shape
    return t.transpose(0, 2, 1, 3).reshape(B * S, nh * ATT_HEAD_DIM)


def _mix_fwd(x, cos, sin, B, S, p):
    T = B * S
    h = _rms_fwd("mix_norm", x, p["mix_norm"])
    proj = _matmul("mix_proj", h, p["w_in"], "nn")
    proj3 = proj.reshape(B, S, IN_PAD)
    q_gain = jnp.tile(p["attn_q_norm"], (1, ATT_HEADS))
    k_gain = jnp.tile(p["attn_k_norm"], (1, ATT_KV_HEADS))
    q_r = _prep_fwd("q_prep", proj, ATT_WIDTH, C_QA // ATT_WIDTH, q_gain, cos, sin)
    k_r = _prep_fwd("k_prep", proj, ATT_KV_WIDTH, C_KA // ATT_KV_WIDTH, k_gain, cos, sin)
    qh = q_r.reshape(B, S, ATT_WIDTH)
    kh = k_r.reshape(B, S, ATT_KV_WIDTH)
    sink = p["attn_sink"].reshape(ATT_KV_HEADS, ATT_GROUP, 1, 1)
    y_a = _attn_fwd(qh, kh, proj3, sink).reshape(T, ATT_WIDTH)

    qk_c = _conv_fwd(proj3, p["conv_w8"])
    hf, hb, *states = _mlstm_fwd(qk_c, proj3, p["gate_bias"])
    hf2, hb2 = hf.reshape(T, MLSTM_WIDTH), hb.reshape(T, MLSTM_WIDTH)
    DH = MLSTM_HEAD_DIM
    y_m = _rowwise("mlstm_out", _mlstm_combine,
                   [_In(hf2, DH, split=True), _In(hb2, DH, split=True), _In(proj, DH, C_OM // DH, split=True),
                    _In(p["mlstm_out_norm"], DH, split=True, rows=False)],
                   [_Out(MLSTM_WIDTH, BF16, DH, split=True)], T, 1024, ncol=MLSTM_HEADS)[0]

    za = _mmw("branch_a", y_a, p["w_branch_attn"], "nn")
    zm = _mmw("branch_m", y_m, p["w_branch_mlstm"], "nn")
    W = 512
    merged = _rowwise("merge", _merge,
                      [_In(proj, W, C_GMERGE // W, split=True), _In(proj, W, (C_GMERGE + D_MODEL) // W, split=True),
                       _In(za, W, split=True), _In(zm, W, split=True)],
                      [_Out(D_MODEL, BF16, W, split=True)], T, 512, ncol=D_MODEL // W)[0]
    out = _mmw("mix_out", merged, p["w_out"], "nn", res=x)
    saved = dict(x=x, h=h, proj=proj, q_gain=q_gain, k_gain=k_gain, qh=qh, kh=kh, sink=sink, y_a=y_a, qk_c=qk_c,
                 hf=hf2, hb=hb2, states=states, y_m=y_m, za=za, zm=zm, merged=merged)
    return out, saved


def _mix_bwd(sv, cos, sin, B, S, p, dx, on_dw):
    T = B * S
    DH = MLSTM_HEAD_DIM
    proj = sv["proj"]
    proj3 = proj.reshape(B, S, IN_PAD)
    g = {}
    dmerged = _mmw("mix_dmerged", dx, p["w_out"], "nt")
    g["w_out"] = _matmul("mix_dwout", sv["merged"], dx, "tn", out_dtype=BF16)
    W = 512

    def merge_bwd(ga, gm, za, zm, dm):
        _, vjp = jax.vjp(_merge, ga, gm, za, zm)
        return vjp(dm)

    dga, dgm, dza, dzm = _rowwise(
        "merge_bwd", merge_bwd,
        [_In(proj, W, C_GMERGE // W, split=True), _In(proj, W, (C_GMERGE + D_MODEL) // W, split=True),
         _In(sv["za"], W, split=True), _In(sv["zm"], W, split=True), _In(dmerged, W, split=True)],
        [_Out(D_MODEL, BF16, W, split=True), _Out(D_MODEL, BF16, W, split=True),
         _Out(D_MODEL, BF16, W, split=True), _Out(D_MODEL, BF16, W, split=True)], T, 512, ncol=D_MODEL // W)
    dya = _mmw("branch_a_dx", dza, p["w_branch_attn"], "nt")
    g["w_branch_attn"] = _matmul("branch_a_dw", sv["y_a"], dza, "tn", out_dtype=BF16)
    dym = _mmw("branch_m_dx", dzm, p["w_branch_mlstm"], "nt")
    g["w_branch_mlstm"] = _matmul("branch_m_dw", sv["y_m"], dzm, "tn", out_dtype=BF16)

    def combine_bwd(hf, hb, o_pre, gn, dy):
        _, vjp = jax.vjp(_mlstm_combine, hf, hb, o_pre, gn)
        dhf, _, do, dg = vjp(dy)
        return dhf, do, dg

    dh, dom, g["mlstm_out_norm"] = _rowwise(
        "mlstm_out_bwd", combine_bwd,
        [_In(sv["hf"], DH, split=True), _In(sv["hb"], DH, split=True), _In(proj, DH, C_OM // DH, split=True),
         _In(p["mlstm_out_norm"], DH, split=True, rows=False), _In(dym, DH, split=True)],
        [_Out(MLSTM_WIDTH, F32, DH, split=True), _Out(MLSTM_WIDTH, BF16, DH, split=True),
         _Out(MLSTM_WIDTH, F32, DH, split=True, rows=False)], T, 1024, ncol=MLSTM_HEADS)
    dqk_f, dqk_b, dv_f, dv_b, dg_f, dg_b = _mlstm_bwd(sv["qk_c"], proj3, p["gate_bias"], sv["states"],
                                                       dh.reshape(B, S, MLSTM_WIDTH))
    dgates, dvm, g["gate_bias"] = _rowwise(
        "mlstm_dsum", lambda a, b, c, d: (a + b, c + d, jnp.sum(a + b, axis=0, keepdims=True)),
        [_In(dg_f.reshape(T, LANES)), _In(dg_b.reshape(T, LANES)), _In(dv_f.reshape(T, MLSTM_WIDTH)), _In(dv_b.reshape(T, MLSTM_WIDTH))],
        [_Out(LANES, BF16), _Out(MLSTM_WIDTH, BF16), _Out(LANES, rows=False)], T, 1024)
    dqk, g["conv_w8"] = _conv_bwd(proj3, p["conv_w8"], dqk_f, dqk_b)

    dqh, dkh, dvh, dsink = _attn_bwd(sv["qh"], sv["kh"], proj3, sv["sink"], dya.reshape(B, S, ATT_WIDTH))
    g["attn_sink"] = dsink.reshape(1, ATT_HEADS)
    dva = dvh.reshape(T, ATT_KV_WIDTH)
    dqa, dq_gain = _prep_bwd("q_prep_bwd", proj, ATT_WIDTH, C_QA // ATT_WIDTH, sv["q_gain"], cos, sin,
                             dqh.reshape(T, ATT_WIDTH))
    dka, dk_gain = _prep_bwd("k_prep_bwd", proj, ATT_KV_WIDTH, C_KA // ATT_KV_WIDTH, sv["k_gain"], cos, sin,
                             dkh.reshape(T, ATT_KV_WIDTH))
    g["attn_q_norm"] = jnp.sum(dq_gain.reshape(ATT_HEADS, ATT_HEAD_DIM), axis=0, keepdims=True)
    g["attn_k_norm"] = jnp.sum(dk_gain.reshape(ATT_KV_HEADS, ATT_HEAD_DIM), axis=0, keepdims=True)

    dproj = jnp.concatenate(
        [dga, dgm, dqk.reshape(T, 2 * MLSTM_WIDTH), dvm, dom, dqa, dka, dva.astype(BF16), dgates], axis=1)
    dwin = _matmul("mix_dwin", sv["h"], dproj, "tn", out_dtype=BF16)
    token = on_dw({"w_in": _w_in_to_slots(dwin), "w_branch_attn": g.pop("w_branch_attn"),
                   "w_branch_mlstm": g.pop("w_branch_mlstm"), "w_out": g.pop("w_out")}, dwin)
    dh2 = _matmul("mix_dh", dproj, p["w_in"], "nt", dep=token)
    dx_new, g["mix_norm"] = _rms_bwd("mix_dnorm", sv["x"], p["mix_norm"], dh2, dx)
    return dx_new, g


def _loss_and_grad(x, g, target):
    T = x.shape[0]

    def loss_fn(xv, gv, tv):
        err = jnp.square(_rms(xv, gv) - tv)
        return 0.5 * jnp.sum(jnp.mean(err, axis=-1, keepdims=True), axis=0, keepdims=True)

    def fn(xv, gv, tv):
        val, vjp = jax.vjp(lambda a, b: loss_fn(a, b, tv), xv, gv)
        dx, dg = vjp(jnp.ones((1, 1), F32))
        return val, dx, dg

    return _rowwise("loss_head", fn, [_In(x), _In(g, rows=False), _In(target)],
                    [_Out(1, rows=False), _Out(D_MODEL), _Out(D_MODEL, rows=False)], T, ROW_BLOCK)


def _block_norm_fwd(x, g):
    T = x.shape[0]
    return _rowwise("block_norm", _rms, [_In(x), _In(g, rows=False)], [_Out(D_MODEL)], T, ROW_BLOCK)[0]


def _block_norm_bwd(x, g, dy):
    T = x.shape[0]

    def fn(xv, gv, dv):
        _, vjp = jax.vjp(_rms, xv, gv)
        return vjp(dv)

    return _rowwise("block_norm_bwd", fn, [_In(x), _In(g, rows=False), _In(dy)],
                    [_Out(D_MODEL), _Out(D_MODEL, rows=False)], T, ROW_BLOCK)


def _qk_perm_cols(t, axis):
    q, k = jnp.split(t, 2, axis=axis)
    parts = []
    for h in range(MLSTM_HEADS):
        sl = [slice(None)] * t.ndim
        sl[axis] = slice(h * MLSTM_HEAD_DIM, (h + 1) * MLSTM_HEAD_DIM)
        parts += [q[tuple(sl)], k[tuple(sl)]]
    return jnp.concatenate(parts, axis=axis)


def _qk_unperm_cols(t, axis):
    qs, ks = [], []
    for h in range(MLSTM_HEADS):
        sl = [slice(None)] * t.ndim
        sl[axis] = slice(2 * h * MLSTM_HEAD_DIM, (2 * h + 1) * MLSTM_HEAD_DIM)
        qs.append(t[tuple(sl)])
        sl[axis] = slice((2 * h + 1) * MLSTM_HEAD_DIM, (2 * h + 2) * MLSTM_HEAD_DIM)
        ks.append(t[tuple(sl)])
    return jnp.concatenate(qs + ks, axis=axis)


def _w_in_arrange(w):
    qa, ka, va, qm, km, vm, om, gm, gmerge = jnp.split(w, np.cumsum(
        (ATT_WIDTH, ATT_KV_WIDTH, ATT_KV_WIDTH, MLSTM_WIDTH, MLSTM_WIDTH, MLSTM_WIDTH, MLSTM_WIDTH, MLSTM_N_GATES))[:].tolist(), axis=1)
    qk = _qk_perm_cols(jnp.concatenate([qm, km], axis=1), 1)
    pad = jnp.zeros((w.shape[0], LANES - MLSTM_N_GATES), w.dtype)
    return jnp.concatenate([gmerge, qk, vm, om, qa, ka, va, gm, pad], axis=1)


def _w_in_restore(w):
    gmerge = w[:, C_GMERGE:C_GMERGE + 2 * D_MODEL]
    qk = _qk_unperm_cols(w[:, C_QK:C_QK + 2 * MLSTM_WIDTH], 1)
    vm, om = w[:, C_VM:C_VM + MLSTM_WIDTH], w[:, C_OM:C_OM + MLSTM_WIDTH]
    qa, ka, va = w[:, C_QA:C_QA + ATT_WIDTH], w[:, C_KA:C_KA + ATT_KV_WIDTH], w[:, C_VA:C_VA + ATT_KV_WIDTH]
    gm = w[:, C_GATES:C_GATES + MLSTM_N_GATES]
    return jnp.concatenate([qa, ka, va, qk, vm, om, gm, gmerge], axis=1)


BIG = ("ffn1_w_gate", "ffn1_w_up", "ffn1_w_down", "w_in", "mlstm_conv_w", "w_branch_attn", "w_branch_mlstm", "w_out",
       "ffn2_w_gate", "ffn2_w_up", "ffn2_w_down")
MATMUL_W = tuple(n for n in BIG if n != "mlstm_conv_w")
SMALL = ("ffn1_norm", "mix_norm", "mlstm_gate_bias", "attn_q_norm", "attn_k_norm", "attn_sink", "mlstm_conv_b",
         "mlstm_out_norm", "ffn2_norm", "block_out_norm")
WEIGHTS = ("ffn1_norm", "ffn1_w_gate", "ffn1_w_up", "ffn1_w_down", "mix_norm", "w_in", "mlstm_gate_bias", "attn_q_norm",
           "attn_k_norm", "attn_sink", "mlstm_conv_w", "mlstm_conv_b", "mlstm_out_norm", "w_branch_attn", "w_branch_mlstm",
           "w_out", "ffn2_norm", "ffn2_w_gate", "ffn2_w_up", "ffn2_w_down", "block_out_norm")
PACK_COLS = 1024


def _padded_rows(n_elems):
    return -(-n_elems // PACK_COLS)


def _pack_flat(arrs, dtype, row_multiple):
    parts = []
    for a in arrs:
        flat = a.reshape(-1).astype(dtype)
        pad = _padded_rows(flat.shape[0]) * PACK_COLS - flat.shape[0]
        parts.append(jnp.pad(flat, (0, pad)) if pad else flat)
    flat = jnp.concatenate(parts)
    rows = flat.shape[0] // PACK_COLS
    extra = (-rows) % row_multiple
    if extra:
        flat = jnp.pad(flat, (0, extra * PACK_COLS))
    return flat.reshape(-1, PACK_COLS)


def _unpack_flat(buf, shapes, lead=()):
    flat = buf.reshape(lead + (-1,))
    out, off = [], 0
    for s in shapes:
        n = int(np.prod(s))
        out.append(flat[..., off:off + n].reshape(lead + tuple(s)))
        off += _padded_rows(n) * PACK_COLS
    return out


class _Lay:
    def __init__(self, shard, axis, width):
        self.shard, self.axis, self.width = shard, axis, width
        self.padded = tuple(width if a == axis else s for a, s in enumerate(shard))
        self.whole = tuple(N_DEV * width if a == axis else s for a, s in enumerate(shard))

    def pad(self, t, lead=0):
        extra = self.width - self.shard[self.axis]
        if not extra:
            return t
        cfg = [(0, 0)] * t.ndim
        cfg[lead + self.axis] = (0, extra)
        return jnp.pad(t, cfg)

    def unpad(self, t, lead=0):
        idx = [slice(None)] * t.ndim
        idx[lead + self.axis] = slice(0, self.shard[self.axis])
        return t[tuple(idx)]


_FF_COL = _Lay((D_MODEL, FF_SHARD), 1, FF_SHARD_PAD)
_FF_ROW = _Lay((FF_SHARD, D_MODEL), 0, FF_SHARD_PAD)
LAYOUTS = {
    "ffn1_w_gate": _FF_COL, "ffn1_w_up": _FF_COL, "ffn1_w_down": _FF_ROW,
    "ffn2_w_gate": _FF_COL, "ffn2_w_up": _FF_COL, "ffn2_w_down": _FF_ROW,
    "w_in": _Lay((D_MODEL, IN_WIDTH // N_DEV), 0, D_MODEL),
    "mlstm_conv_w": _Lay((3, 2 * MLSTM_WIDTH // N_DEV), 1, 2 * MLSTM_WIDTH // N_DEV),
    "w_branch_attn": _Lay((ATT_WIDTH, D_MODEL // N_DEV), 1, D_MODEL // N_DEV),
    "w_branch_mlstm": _Lay((MLSTM_WIDTH, D_MODEL // N_DEV), 1, D_MODEL // N_DEV),
    "w_out": _Lay((D_MODEL // N_DEV, D_MODEL), 0, D_MODEL // N_DEV),
}


def _window(ref, axis, j, width):
    idx = [slice(None)] * len(ref.shape)
    idx[axis] = pl.ds(pl.multiple_of(j * width, width), width)
    return ref.at[tuple(idx)]


ANY = pl.BlockSpec(memory_space=pl.ANY)


def _mesh_pos():
    return lax.axis_index("x"), lax.axis_index("y"), lax.axis_index("c")


def _all_gather(name, shard, vmem=False):
    R, C = shard.shape
    space = pl.BlockSpec(memory_space=pltpu.VMEM) if vmem else ANY

    def body(x_ref, out_ref, send_sems, recv_sems, local_sem):
        x, y, c = _mesh_pos()
        me, sibling = (x, y, c), (x, y, 1 - c)
        chips = [(1 - x, y), (x, 1 - y), (1 - x, 1 - y)]

        def slot(px, py, pc):
            return out_ref.at[4 * px + 2 * py + pc]

        def copy(k, block, to, src=None):
            return pltpu.make_async_remote_copy(
                src_ref=slot(*block) if src is None else src, dst_ref=slot(*block),
                send_sem=send_sems.at[k], recv_sem=recv_sems.at[k], device_id=to, device_id_type=MESH)

        mine = pltpu.make_async_copy(x_ref, slot(*me), local_sem)
        mine.start()
        first = [copy(0, me, sibling, src=x_ref)]
        first += [copy(1 + j, me, (*chip, c), src=x_ref) for j, chip in enumerate(chips)]
        for cp in first:
            cp.start()
        passed = [copy(4 + j, (*chip, c), sibling) for j, chip in enumerate(chips)]
        for j, chip in enumerate(chips):
            copy(1 + j, (*chip, c), me).wait_recv()
            passed[j].start()
        copy(0, sibling, me).wait_recv()
        for j, chip in enumerate(chips):
            copy(4 + j, (*chip, 1 - c), me).wait_recv()
        for cp in first + passed:
            cp.wait_send()
        mine.wait()

    return pl.pallas_call(
        body, name=name, out_shape=jax.ShapeDtypeStruct((N_DEV, R, C), shard.dtype),
        in_specs=[space], out_specs=space,
        scratch_shapes=[pltpu.SemaphoreType.DMA((7,)), pltpu.SemaphoreType.DMA((7,)), pltpu.SemaphoreType.DMA],
    )(shard)


HBM = pl.BlockSpec(memory_space=pltpu.HBM)
SEM = pl.BlockSpec(memory_space=pltpu.SEMAPHORE)
SPLIT_COPY = pltpu.CompilerParams(has_side_effects=pltpu.SideEffectType.DATAFLOW_SIDE_EFFECTING)
N_PEERS = N_DEV - 1


def _peers(x, y, c):
    return [(x, y, 1 - c), (1 - x, y, c), (x, 1 - y, c), (1 - x, 1 - y, c),
            (1 - x, y, 1 - c), (x, 1 - y, 1 - c), (1 - x, 1 - y, 1 - c)]


def _dev_index(pos):
    return 4 * pos[0] + 2 * pos[1] + pos[2]


def _place_own(name, shards, lays):
    nt = len(shards)
    me = _dev_index(_mesh_pos())

    def body(me_ref, *refs):
        for x_ref, o_ref in zip(refs[:nt], refs[nt:]):
            o_ref[...] = x_ref[...]

    def window_spec(lay):
        if lay.axis == 0:
            return pl.BlockSpec(lay.padded, lambda i, me_ref: (me_ref[0], 0))
        return pl.BlockSpec(lay.padded, lambda i, me_ref: (0, me_ref[0]))

    return pl.pallas_call(
        body, name=name,
        grid_spec=pltpu.PrefetchScalarGridSpec(
            num_scalar_prefetch=1, grid=(1,),
            in_specs=[pl.BlockSpec(lay.padded, lambda i, me_ref: (0, 0)) for lay in lays],
            out_specs=[window_spec(lay) for lay in lays]),
        out_shape=[jax.ShapeDtypeStruct(lay.whole, s.dtype) for s, lay in zip(shards, lays)],
        compiler_params=_cparams(("arbitrary",)),
    )(me.reshape(1).astype(jnp.int32), *shards)


def _gather_start(name, shards, lands, lays, groups, after):
    nt, ng = len(shards), len(groups)

    def body(*refs):
        x_refs, land_refs = refs[:nt], refs[nt:2 * nt]
        sems = refs[2 * nt + 1:2 * nt + 1 + 2 * ng]
        pos = _mesh_pos()
        me = _dev_index(pos)
        for g, tens in enumerate(groups):
            for i, t in enumerate(tens):
                for k, peer in enumerate(_peers(*pos)):
                    pltpu.make_async_remote_copy(
                        src_ref=x_refs[t], dst_ref=_window(land_refs[t], lays[t].axis, me, lays[t].width),
                        send_sem=sems[2 * g].at[N_PEERS * i + k], recv_sem=sems[2 * g + 1].at[N_PEERS * i + k],
                        device_id=peer, device_id_type=MESH).start()

    sem_shapes = []
    for tens in groups:
        sem_shapes += [pltpu.SemaphoreType.DMA((N_PEERS * len(tens),))] * 2
    thru = [pltpu.HBM(s.shape, s.dtype) for s in shards] + [pltpu.HBM(lay.whole, s.dtype) for s, lay in zip(shards, lays)]
    args = [pltpu.with_memory_space_constraint(s, pltpu.HBM) for s in shards]
    args += [pltpu.with_memory_space_constraint(ld, pltpu.HBM) for ld in lands]
    res = pl.pallas_call(
        body, name=name, out_shape=tuple(sem_shapes + thru), in_specs=[HBM] * (2 * nt) + [ANY],
        out_specs=tuple([SEM] * (2 * ng) + [HBM] * (2 * nt)),
        input_output_aliases={t: 2 * ng + t for t in range(2 * nt)}, compiler_params=SPLIT_COPY,
    )(*args, after)
    sems = [(res[2 * g], res[2 * g + 1]) for g in range(ng)]
    return sems, list(res[2 * ng:2 * ng + nt]), list(res[2 * ng + nt:])


def _gather_wait(name, sems, shards, lands, lays, after):
    nt = len(shards)
    send_sems, recv_sems = sems

    def body(*refs):
        x_refs, land_refs = refs[:nt], refs[nt:2 * nt]
        send_ref, recv_ref = refs[2 * nt], refs[2 * nt + 1]
        pos = _mesh_pos()
        for t in range(nt):
            for k, peer in enumerate(_peers(*pos)):
                cp = pltpu.make_async_remote_copy(
                    src_ref=x_refs[t], dst_ref=_window(land_refs[t], lays[t].axis, _dev_index(peer), lays[t].width),
                    send_sem=send_ref.at[N_PEERS * t + k], recv_sem=recv_ref.at[N_PEERS * t + k],
                    device_id=peer, device_id_type=MESH)
                cp.wait_send()
                cp.wait_recv()

    thru = [pltpu.HBM(s.shape, s.dtype) for s in shards] + [pltpu.HBM(ld.shape, ld.dtype) for ld in lands]
    res = pl.pallas_call(
        body, name=name, out_shape=tuple(thru), in_specs=[HBM] * (2 * nt) + [SEM, SEM, ANY],
        out_specs=tuple([HBM] * (2 * nt)), input_output_aliases={t: t for t in range(2 * nt)},
        compiler_params=SPLIT_COPY,
    )(*shards, *lands, send_sems, recv_sems, after)
    return list(res[nt:])


def _pair_exchange(name, grads, lays):
    nt = len(grads)

    def body(*refs):
        g_refs, land_refs = refs[:nt], refs[nt:2 * nt]
        send_sems, recv_sems = refs[2 * nt:]
        x, y, c = _mesh_pos()
        copies = []
        for t in range(nt):
            for chip in range(4):
                copies.append(pltpu.make_async_remote_copy(
                    src_ref=_window(g_refs[t], lays[t].axis, 2 * chip + (1 - c), lays[t].width), dst_ref=land_refs[t].at[chip],
                    send_sem=send_sems.at[4 * t + chip], recv_sem=recv_sems.at[4 * t + chip],
                    device_id=(x, y, 1 - c), device_id_type=MESH))
        for cp in copies:
            cp.start()
        for cp in copies:
            cp.wait_recv()
        for cp in copies:
            cp.wait_send()

    out_shape = [jax.ShapeDtypeStruct((4,) + lay.padded, g.dtype) for g, lay in zip(grads, lays)]
    return pl.pallas_call(
        body, name=name, out_shape=out_shape, in_specs=[ANY] * nt, out_specs=[ANY] * nt,
        scratch_shapes=[pltpu.SemaphoreType.DMA((4 * nt,)), pltpu.SemaphoreType.DMA((4 * nt,))],
    )(*grads)


def _pair_sum(name, whole, landed, lay, out_dtype):
    R, C = lay.padded
    br = _first_divisor(R, (512, 384, 256, 128, 64, 32, 16, 8))
    nb = R // br
    if lay.axis == 0:
        mine_spec = pl.BlockSpec((br, C), lambda k, i, c_ref: ((2 * k + c_ref[0]) * nb + i, 0))
    else:
        mine_spec = pl.BlockSpec((br, C), lambda k, i, c_ref: (i, 2 * k + c_ref[0]))

    def body(c_ref, mine_ref, sib_ref, o_ref):
        o_ref[0] = (mine_ref[...].astype(F32) + sib_ref[0].astype(F32)).astype(out_dtype)

    c = lax.axis_index("c")
    return pl.pallas_call(
        body, name=name,
        grid_spec=pltpu.PrefetchScalarGridSpec(
            num_scalar_prefetch=1, grid=(4, nb),
            in_specs=[mine_spec, pl.BlockSpec((1, br, C), lambda k, i, c_ref: (k, i, 0))],
            out_specs=pl.BlockSpec((1, br, C), lambda k, i, c_ref: (k, i, 0))),
        out_shape=jax.ShapeDtypeStruct((4, R, C), out_dtype),
        compiler_params=_cparams(("parallel", "parallel")),
    )(c.reshape(1).astype(jnp.int32), whole, landed)


def _chip_exchange(name, sums):
    nt = len(sums)

    def body(*refs):
        s_refs, land_refs = refs[:nt], refs[nt:2 * nt]
        send_sems, recv_sems, local_sems = refs[2 * nt:]
        x, y, c = _mesh_pos()
        my_chip = 2 * x + y
        mine = [pltpu.make_async_copy(s_refs[t].at[my_chip], land_refs[t].at[my_chip], local_sems.at[t]) for t in range(nt)]
        for cp in mine:
            cp.start()
        chips = [(1 - x, y), (x, 1 - y), (1 - x, 1 - y)]
        copies = []
        for t in range(nt):
            for j, (px, py) in enumerate(chips):
                copies.append(pltpu.make_async_remote_copy(
                    src_ref=s_refs[t].at[2 * px + py], dst_ref=land_refs[t].at[my_chip],
                    send_sem=send_sems.at[3 * t + j], recv_sem=recv_sems.at[3 * t + j],
                    device_id=(px, py, c), device_id_type=MESH))
        for cp in copies:
            cp.start()
        for t in range(nt):
            for j, (px, py) in enumerate(chips):
                pltpu.make_async_remote_copy(
                    src_ref=s_refs[t].at[my_chip], dst_ref=land_refs[t].at[2 * px + py],
                    send_sem=send_sems.at[3 * t + j], recv_sem=recv_sems.at[3 * t + j],
                    device_id=(px, py, c), device_id_type=MESH).wait_recv()
        for cp in copies:
            cp.wait_send()
        for cp in mine:
            cp.wait()

    return pl.pallas_call(
        body, name=name, out_shape=[jax.ShapeDtypeStruct(s.shape, s.dtype) for s in sums],
        in_specs=[ANY] * nt, out_specs=[ANY] * nt,
        scratch_shapes=[pltpu.SemaphoreType.DMA((3 * nt,)), pltpu.SemaphoreType.DMA((3 * nt,)), pltpu.SemaphoreType.DMA((nt,))],
    )(*sums)


def _chip_start(name, sums):
    nt = len(sums)

    def body(*refs):
        s_refs, land_refs = refs[:nt], refs[nt:2 * nt]
        send_sems, recv_sems = refs[2 * nt], refs[2 * nt + 1]
        x, y, c = _mesh_pos()
        my_chip = 2 * x + y
        for t in range(nt):
            for j, (px, py) in enumerate([(1 - x, y), (x, 1 - y), (1 - x, 1 - y)]):
                pltpu.make_async_remote_copy(
                    src_ref=s_refs[t].at[2 * px + py], dst_ref=land_refs[t].at[my_chip],
                    send_sem=send_sems.at[3 * t + j], recv_sem=recv_sems.at[3 * t + j],
                    device_id=(px, py, c), device_id_type=MESH).start()

    thru = [pltpu.HBM(s.shape, s.dtype) for s in sums] * 2
    args = [pltpu.with_memory_space_constraint(s, pltpu.HBM) for s in sums]
    args += [pltpu.with_memory_space_constraint(lax.empty(s.shape, s.dtype), pltpu.HBM) for s in sums]
    res = pl.pallas_call(
        body, name=name, out_shape=tuple([pltpu.SemaphoreType.DMA((3 * nt,))] * 2 + thru), in_specs=[HBM] * (2 * nt),
        out_specs=tuple([SEM, SEM] + [HBM] * (2 * nt)), input_output_aliases={t: 2 + t for t in range(2 * nt)},
        compiler_params=SPLIT_COPY,
    )(*args)
    return (res[0], res[1]), list(res[2:2 + nt]), list(res[2 + nt:])


def _chip_wait(name, sems, sums, lands, after):
    nt = len(sums)

    def body(*refs):
        s_refs, land_refs = refs[:nt], refs[nt:2 * nt]
        send_sems, recv_sems = refs[2 * nt], refs[2 * nt + 1]
        x, y, c = _mesh_pos()
        my_chip = 2 * x + y
        for t in range(nt):
            for j, (px, py) in enumerate([(1 - x, y), (x, 1 - y), (1 - x, 1 - y)]):
                cp = pltpu.make_async_remote_copy(
                    src_ref=s_refs[t].at[my_chip], dst_ref=land_refs[t].at[2 * px + py],
                    send_sem=send_sems.at[3 * t + j], recv_sem=recv_sems.at[3 * t + j],
                    device_id=(px, py, c), device_id_type=MESH)
                cp.wait_send()
                cp.wait_recv()

    thru = [pltpu.HBM(s.shape, s.dtype) for s in sums] * 2
    res = pl.pallas_call(
        body, name=name, out_shape=tuple(thru), in_specs=[HBM] * (2 * nt) + [SEM, SEM, ANY],
        out_specs=tuple([HBM] * (2 * nt)), input_output_aliases={t: t for t in range(2 * nt)},
        compiler_params=SPLIT_COPY,
    )(*sums, *lands, sems[0], sems[1], after)
    return list(res[:nt]), list(res[nt:])


def _sum_chips(name, own, landed):
    _, R, C = own.shape
    br = _first_divisor(R, (512, 384, 256, 128, 64, 32, 16, 8))
    x, y, _ = _mesh_pos()
    slots = jnp.stack([2 * x + y, 2 * (1 - x) + y, 2 * x + (1 - y), 2 * (1 - x) + (1 - y)]).astype(jnp.int32)

    def body(slot_ref, mine_ref, a_ref, b_ref, c_ref, o_ref):
        o_ref[...] = ((mine_ref[0].astype(F32) + a_ref[0].astype(F32)) + b_ref[0].astype(F32)) + c_ref[0].astype(F32)

    def slot_spec(j):
        return pl.BlockSpec((1, br, C), lambda i, slot_ref: (slot_ref[j], i, 0))

    return pl.pallas_call(
        body, name=name,
        grid_spec=pltpu.PrefetchScalarGridSpec(
            num_scalar_prefetch=1, grid=(R // br,), in_specs=[slot_spec(0), slot_spec(1), slot_spec(2), slot_spec(3)],
            out_specs=pl.BlockSpec((br, C), lambda i, slot_ref: (i, 0))),
        out_shape=jax.ShapeDtypeStruct((R, C), F32), compiler_params=_cparams(("parallel",)),
    )(slots, own, landed, landed, landed)


def _sum_slots(name, slots, n):
    _, R, C = slots.shape
    br = _first_divisor(R, (512, 384, 256, 128, 64, 32, 16, 8))

    def body(s_ref, o_ref):
        acc = s_ref[0].astype(F32)
        for k in range(1, n):
            acc = acc + s_ref[k].astype(F32)
        o_ref[...] = acc

    return pl.pallas_call(
        body, name=name, grid=(R // br,), in_specs=[pl.BlockSpec((n, br, C), lambda i: (0, i, 0))],
        out_specs=pl.BlockSpec((br, C), lambda i: (i, 0)), out_shape=jax.ShapeDtypeStruct((R, C), F32),
        compiler_params=_cparams(("parallel",)),
    )(slots)


def _reduce_scatter_start(tag, names, grads):
    lays = [LAYOUTS[n] for n in names]
    landed = _pair_exchange("grads_pair_" + names[0], grads, lays)
    sums = [_pair_sum("grads_pairsum_" + n, g, ld, lay, BF16) for n, g, ld, lay in zip(names, grads, landed, lays)]
    sems, sums, lands = _chip_start(tag + "_chips_start", sums)
    return tag, names, sems, sums, lands


def _reduce_scatter_finish(pending, after):
    tag, names, sems, sums, lands = pending
    own, got = _chip_wait(tag + "_chips_wait", sems, sums, lands, after)
    return [_sum_chips("grads_sum_" + n, o, s) for n, o, s in zip(names, own, got)]


def _adamw_math(w, g, m, v):
    m = ADAM_B1 * m + (1.0 - ADAM_B1) * g
    v = ADAM_B2 * v + (1.0 - ADAM_B2) * jnp.square(g)
    m_hat = m / (1.0 - ADAM_B1 ** ADAM_STEP)
    v_hat = v / (1.0 - ADAM_B2 ** ADAM_STEP)
    delta = -ADAM_LR * (m_hat / (jnp.sqrt(v_hat) + ADAM_EPS) + ADAM_WD * w)
    return delta, m, v


def _adamw_layers(name, w, totals, m, v):
    _, R, C = w.shape
    br = _first_divisor(R, (512, 176, 128, 64, 32, 16, 8))
    Cp = totals[0].shape[1]

    def body(w_ref, g0_ref, g1_ref, m_ref, v_ref, g_out, d_out, m_out, v_out):
        g = jnp.where(pl.program_id(0) == 0, g0_ref[:, 0:C], g1_ref[:, 0:C])
        delta, m_new, v_new = _adamw_math(w_ref[0], g, m_ref[0], v_ref[0])
        g_out[0], d_out[0], m_out[0], v_out[0] = g, delta, m_new, v_new

    blk = pl.BlockSpec((1, br, C), lambda l, i: (l, i, 0))
    g_spec = pl.BlockSpec((br, Cp), lambda l, i: (i, 0))
    return pl.pallas_call(
        body, name=name, grid=(DEPTH, R // br), in_specs=[blk, g_spec, g_spec, blk, blk], out_specs=[blk] * 4,
        out_shape=[jax.ShapeDtypeStruct(w.shape, F32)] * 4, compiler_params=_cparams(("parallel", "parallel")),
    )(w, totals[0], totals[1], m, v)


def _adamw(name, w, g, m, v):
    shape = w.shape
    cols = shape[-1]
    rows = int(np.prod(shape[:-1]))
    br = _first_divisor(rows, (512, 352, 256, 128, 64, 32, 16, 8))
    args = [_In(a.reshape(rows, cols)) for a in (w, g, m, v)]
    outs = _rowwise(name, _adamw_math, args, [_Out(cols), _Out(cols), _Out(cols)], rows, br)
    return [o.reshape(shape) for o in outs]


GROUPS = {"ffn1": ("ffn1_w_gate", "ffn1_w_up", "ffn1_w_down"),
          "mix": ("w_in", "w_branch_attn", "w_branch_mlstm", "w_out"),
          "ffn2": ("ffn2_w_gate", "ffn2_w_up", "ffn2_w_down")}
GATHER_GROUPS = {"ffn1_in": ("ffn1_w_gate", "ffn1_w_up"), "ffn1_out": ("ffn1_w_down",),
                 "mix": ("w_in", "w_branch_attn", "w_branch_mlstm", "w_out"),
                 "ffn2_in": ("ffn2_w_gate", "ffn2_w_up"), "ffn2_out": ("ffn2_w_down",)}


def _small_params(small, conv_w, l):
    p = {}
    for n in ("ffn1_norm", "mix_norm", "ffn2_norm", "block_out_norm", "mlstm_out_norm", "attn_q_norm", "attn_k_norm"):
        p[n] = small[n][l][None, :]
    p["attn_sink"] = small["attn_sink"][l]
    p["gate_bias"] = jnp.pad(small["mlstm_gate_bias"][l], (0, LANES - MLSTM_N_GATES))[None, :]
    taps = _qk_perm_cols(conv_w[l], 1)
    conv_b = _qk_perm_cols(small["mlstm_conv_b"][l][None, :], 1)
    p["conv_w8"] = jnp.concatenate([taps, conv_b, jnp.zeros((4, 2 * MLSTM_WIDTH), F32)], axis=0)
    return p


def _w_in_from_slots(slots):
    w_in = slots.reshape(N_DEV, D_MODEL, IN_WIDTH // N_DEV).transpose(1, 0, 2).reshape(D_MODEL, IN_WIDTH)
    return _w_in_arrange(w_in)


def _w_in_to_slots(g):
    return _w_in_restore(g).reshape(D_MODEL, N_DEV, IN_WIDTH // N_DEV).transpose(1, 0, 2).reshape(
        N_DEV * D_MODEL, IN_WIDTH // N_DEV)


def _local_step(x, positions, target, weights_of, small, conv_w, on_grads):
    B, S, _ = x.shape
    T = B * S
    cos, sin = _rope_cos_sin(positions.reshape(T, 1))
    params = [_small_params(small, conv_w, l) for l in range(DEPTH)]
    xs = x.reshape(T, D_MODEL)
    tgt = target.reshape(T, D_MODEL)

    saved = []
    for l, p in enumerate(params):
        p.update(weights_of(l, "ffn1_in", xs))
        x1, s1, p["ffn1_w_down"] = _ffn_fwd("ffn1", xs, p["ffn1_norm"], p["ffn1_w_gate"], p["ffn1_w_up"],
                                            lambda after, l=l: weights_of(l, "ffn1_out", after)["ffn1_w_down"])
        p.update(weights_of(l, "mix", x1))
        p["w_in"] = _w_in_from_slots(p["w_in"])
        x2, s2 = _mix_fwd(x1, cos, sin, B, S, p)
        p.update(weights_of(l, "ffn2_in", x2))
        x3, s3, p["ffn2_w_down"] = _ffn_fwd("ffn2", x2, p["ffn2_norm"], p["ffn2_w_gate"], p["ffn2_w_up"],
                                            lambda after, l=l: weights_of(l, "ffn2_out", after)["ffn2_w_down"])
        saved.append((s1, s2, s3, x3))
        if l + 1 < DEPTH:
            xs = _block_norm_fwd(x3, p["block_out_norm"])

    sm = {n: [None] * DEPTH for n in SMALL + ("mlstm_conv_w",)}
    loss = None
    dx = None
    for l in reversed(range(DEPTH)):
        p = params[l]
        s1, s2, s3, x3 = saved[l]
        if l == DEPTH - 1:
            loss, dx, dgn = _loss_and_grad(x3, p["block_out_norm"], tgt)
        else:
            dx, dgn = _block_norm_bwd(x3, p["block_out_norm"], dx)
        sm["block_out_norm"][l] = dgn[0]
        dx, dg = _ffn_bwd("ffn2", s3, p["ffn2_norm"], p["ffn2_w_gate"], p["ffn2_w_up"], p["ffn2_w_down"], dx,
                          functools.partial(on_grads, l, "ffn2"))
        sm["ffn2_norm"][l] = dg[0]
        dx, g = _mix_bwd(s2, cos, sin, B, S, p, dx, functools.partial(on_grads, l, "mix"))
        dconv = _qk_unperm_cols(g["conv_w8"], 1)
        sm["mlstm_conv_w"][l] = dconv[0:3]
        sm["mlstm_conv_b"][l] = dconv[3]
        sm["mix_norm"][l] = g["mix_norm"][0]
        sm["mlstm_gate_bias"][l] = g["gate_bias"][0, :MLSTM_N_GATES]
        sm["attn_q_norm"][l], sm["attn_k_norm"][l] = g["attn_q_norm"][0], g["attn_k_norm"][0]
        sm["attn_sink"][l] = g["attn_sink"][0]
        sm["mlstm_out_norm"][l] = g["mlstm_out_norm"][0]
        dx, dg = _ffn_bwd("ffn1", s1, p["ffn1_norm"], p["ffn1_w_gate"], p["ffn1_w_up"], p["ffn1_w_down"], dx,
                          functools.partial(on_grads, l, "ffn1"))
        sm["ffn1_norm"][l] = dg[0]
    sm = {n: jnp.stack(v, axis=0) for n, v in sm.items()}
    return loss, dx.reshape(B, S, D_MODEL), sm


def kernel(x, positions, ffn1_norm, ffn1_w_gate, ffn1_w_up, ffn1_w_down, mix_norm, w_in, mlstm_gate_bias, attn_q_norm, attn_k_norm, attn_sink, mlstm_conv_w, mlstm_conv_b, mlstm_out_norm, w_branch_attn, w_branch_mlstm, w_out, ffn2_norm, ffn2_w_gate, ffn2_w_up, ffn2_w_down, block_out_norm, loss_target, m_ffn1_norm, m_ffn1_w_gate, m_ffn1_w_up, m_ffn1_w_down, m_mix_norm, m_w_in, m_mlstm_gate_bias, m_attn_q_norm, m_attn_k_norm, m_attn_sink, m_mlstm_conv_w, m_mlstm_conv_b, m_mlstm_out_norm, m_w_branch_attn, m_w_branch_mlstm, m_w_out, m_ffn2_norm, m_ffn2_w_gate, m_ffn2_w_up, m_ffn2_w_down, m_block_out_norm, v_ffn1_norm, v_ffn1_w_gate, v_ffn1_w_up, v_ffn1_w_down, v_mix_norm, v_w_in, v_mlstm_gate_bias, v_attn_q_norm, v_attn_k_norm, v_attn_sink, v_mlstm_conv_w, v_mlstm_conv_b, v_mlstm_out_norm, v_w_branch_attn, v_w_branch_mlstm, v_w_out, v_ffn2_norm, v_ffn2_w_gate, v_ffn2_w_up, v_ffn2_w_down, v_block_out_norm):
    args = locals()
    w = {n: args[n] for n in WEIGHTS}
    m = {n: args["m_" + n] for n in WEIGHTS}
    v = {n: args["v_" + n] for n in WEIGHTS}

    order = [(l, grp) for l in range(DEPTH) for grp in GATHER_GROUPS]
    keys = [(l, n) for l, grp in order for n in GATHER_GROUPS[grp]]
    lays = [LAYOUTS[n] for _, n in keys]
    shards = [lay.pad(w[n][l].astype(BF16)) for (l, n), lay in zip(keys, lays)]
    group_idx, at = {}, 0
    for l, grp in order:
        group_idx[(l, grp)] = list(range(at, at + len(GATHER_GROUPS[grp])))
        at += len(GATHER_GROUPS[grp])
    conv_shape = w["mlstm_conv_w"].shape
    conv_all = _all_gather("conv_all_gather", _pack_flat([w["mlstm_conv_w"]], F32, 8), vmem=True)
    conv_parts = _unpack_flat(conv_all, [conv_shape], lead=(N_DEV,))[0]
    conv_w = jnp.concatenate([conv_parts[j] for j in range(N_DEV)], axis=2)
    small = {n: w[n] for n in SMALL}

    lands = []
    for l, grp in order:
        idx = group_idx[(l, grp)]
        lands += _place_own("weights_place_" + grp, [shards[i] for i in idx], [lays[i] for i in idx])
    sems, shards, lands = _gather_start("weights_gather_start", shards, lands, lays, [group_idx[k] for k in order], conv_all)

    def weights_of(l, grp, after):
        idx = group_idx[(l, grp)]
        whole = _gather_wait(f"weights_gather_wait_{l}_{grp}", sems[order.index((l, grp))], [shards[i] for i in idx],
                             [lands[i] for i in idx], [lays[i] for i in idx], after)
        return dict(zip(GATHER_GROUPS[grp], whole))

    totals, pending = {}, []

    def finish(after):
        tag, names = pending[0][0], pending[0][1]
        for n, t in zip(names, _reduce_scatter_finish(pending.pop(0), after)):
            totals[(tag, n)] = t

    def on_grads(l, grp, g, after):
        if pending:
            finish(after)
        names = GROUPS[grp]
        pending.append(_reduce_scatter_start(f"grads_{l}_{grp}", names, [g[n] for n in names]))
        return pending[-1][3][0]

    loss, grad_x, small_g = _local_step(x, positions, loss_target, weights_of, small, conv_w, on_grads)
    finish(grad_x)
    grads, deltas, new_m, new_v = {}, {}, {}, {}
    for grp, names in GROUPS.items():
        for n in names:
            grads[n], deltas[n], new_m[n], new_v[n] = _adamw_layers(
                "adamw_" + n, w[n], [totals[(f"grads_{l}_{grp}", n)] for l in range(DEPTH)], m[n], v[n])

    small_names = SMALL + ("mlstm_conv_w",)
    small_shapes = [small_g[n].shape for n in small_names] + [(1, 1)]
    small_packed = _pack_flat([small_g[n] for n in small_names] + [loss], F32, 8)
    small_all = _all_gather("small_all_gather", small_packed, vmem=True)
    small_sum = _sum_slots("small_sum", small_all, N_DEV)
    *small_grads, loss_total = _unpack_flat(small_sum, small_shapes)
    grads.update(dict(zip(small_names, small_grads)))
    x_pos, y_pos, c_pos = _mesh_pos()
    grads["mlstm_conv_w"] = lax.dynamic_slice_in_dim(
        grads["mlstm_conv_w"], (4 * x_pos + 2 * y_pos + c_pos) * conv_shape[2], conv_shape[2], axis=2)

    n = "mlstm_conv_w"
    deltas[n], new_m[n], new_v[n] = _adamw("adamw_" + n, w[n], grads[n], m[n], v[n])
    sw, sg, smm, sv = (_pack_flat([d[n] for n in SMALL], F32, 8) for d in (w, grads, m, v))
    sd, snm, snv = _adamw("adamw_small", sw, sg, smm, sv)
    shapes = [w[n].shape for n in SMALL]
    for d, buf in ((deltas, sd), (new_m, snm), (new_v, snv)):
        d.update(dict(zip(SMALL, _unpack_flat(buf, shapes))))

    return (loss_total.reshape(()), grad_x, *[grads[n] for n in WEIGHTS], *[deltas[n] for n in WEIGHTS],
            *[new_m[n] for n in WEIGHTS], *[new_v[n] for n in WEIGHTS])
```

```python
import functools

import numpy as np
import jax
import jax.numpy as jnp
from jax import lax
from jax.experimental import pallas as pl
from jax.experimental.pallas import tpu as pltpu

F32 = jnp.float32
BF16 = jnp.bfloat16

D_MODEL = 1024
D_FF = 2816
ATT_HEAD_DIM = 64
ATT_HEADS = 8
ATT_KV_HEADS = 2
ATT_GROUP = ATT_HEADS // ATT_KV_HEADS
ATT_WIDTH = ATT_HEADS * ATT_HEAD_DIM
ATT_KV_WIDTH = ATT_KV_HEADS * ATT_HEAD_DIM
WINDOW = 128
ATT_BLOCK = 128
ROPE_DIM = 16
ROPE_THETA = 500000.0
MLSTM_HEADS = 4
MLSTM_HEAD_DIM = 128
MLSTM_WIDTH = MLSTM_HEADS * MLSTM_HEAD_DIM
MLSTM_CHUNK = 128
MLSTM_N_GATES = 4 * MLSTM_HEADS
NORM_EPS = 1e-6
IN_WIDTH = 4880
DEPTH = 2
N_DEV = 8

ADAM_LR = 0.001
ADAM_B1 = 0.9
ADAM_B2 = 0.999
ADAM_EPS = 1e-08
ADAM_WD = 0.01
ADAM_STEP = 10

LANES = 128
C_GMERGE = 0
C_QK = 2048
C_VM = 3072
C_OM = 3584
C_QA = 4096
C_KA = 4608
C_VA = 4736
C_GATES = 4864
IN_PAD = 4992

VMEM_LIMIT = 48 * 1024 * 1024

MESH = pl.DeviceIdType.MESH


def _cparams(sem):
    return pltpu.CompilerParams(dimension_semantics=sem, vmem_limit_bytes=VMEM_LIMIT)


def _first_divisor(n, cands):
    for c in cands:
        if n % c == 0:
            return c
    return n


_NN = ((1,), (0,))
_NT = ((1,), (1,))
_TN = ((0,), (0,))


def _mm(a, b, dims):
    return lax.dot_general(a.astype(BF16), b.astype(BF16), (dims, ((), ())), preferred_element_type=F32)


@jax.custom_vjp
def mm_nn(a, b):
    return _mm(a, b, _NN)


def _mm_nn_fwd(a, b):
    return _mm(a, b, _NN), (a, b)


def _mm_nn_bwd(res, g):
    a, b = res
    return _mm(g, b, _NT).astype(a.dtype), _mm(a, g, _TN).astype(b.dtype)


mm_nn.defvjp(_mm_nn_fwd, _mm_nn_bwd)


@jax.custom_vjp
def mm_nt(a, b):
    return _mm(a, b, _NT)


def _mm_nt_fwd(a, b):
    return _mm(a, b, _NT), (a, b)


def _mm_nt_bwd(res, g):
    a, b = res
    return _mm(g, b, _NN).astype(a.dtype), _mm(g, a, _TN).astype(b.dtype)


mm_nt.defvjp(_mm_nt_fwd, _mm_nt_bwd)


@jax.custom_vjp
def mm_tn(a, b):
    return _mm(a, b, _TN)


def _mm_tn_fwd(a, b):
    return _mm(a, b, _TN), (a, b)


def _mm_tn_bwd(res, g):
    a, b = res
    return _mm(b, g, _NT).astype(a.dtype), _mm(a, g, _NN).astype(b.dtype)


mm_tn.defvjp(_mm_tn_fwd, _mm_tn_bwd)


def _matmul(name, a, b, mode, out_dtype=F32, res=None, scale=1.0, bl=None, dep=None):
    b_shape = b.shape if bl is None else b.shape[1:]
    if mode == "nn":
        (M, K), (K2, N) = a.shape, b_shape
    elif mode == "nt":
        (M, K), (N, K2) = a.shape, b_shape
    else:
        (K, M), (K2, N) = a.shape, b_shape
    assert K == K2, (name, a.shape, b.shape)
    tm = _first_divisor(M, (1024, 512, 384, 256, 128))
    tn = _first_divisor(N, (1024, 1664, 512, 384, 256, 128))
    tk = _first_divisor(K, (1024, 1664, 512, 256, 128))
    nk = K // tk
    if mode == "tn":
        a_spec = pl.BlockSpec((tk, tm), lambda i, j, k: (k, i))
    else:
        a_spec = pl.BlockSpec((tm, tk), lambda i, j, k: (i, k))
    if mode == "nt":
        b_blk, b_idx = (tn, tk), (lambda i, j, k: (j, k))
    else:
        b_blk, b_idx = (tk, tn), (lambda i, j, k: (k, j))
    if bl is None:
        b_spec = pl.BlockSpec(b_blk, b_idx)
    else:
        b_spec = pl.BlockSpec((None,) + b_blk, lambda i, j, k: (bl,) + b_idx(i, j, k))
    o_spec = pl.BlockSpec((tm, tn), lambda i, j, k: (i, j))
    dims = {"nn": _NN, "nt": _NT, "tn": _TN}[mode]
    has_res = res is not None

    def body(*refs):
        a_ref, b_ref = refs[:2]
        r_ref = refs[2] if has_res else None

        def finish(out):
            if scale != 1.0:
                out = out * scale
            if has_res:
                out = r_ref[...].astype(F32) + out
            o_ref[...] = out.astype(out_dtype)

        if nk == 1:
            o_ref = refs[-1]
            finish(_mm(a_ref[...], b_ref[...], dims))
            return
        o_ref, acc = refs[-2:]
        k = pl.program_id(2)

        @pl.when(k == 0)
        def _():
            acc[...] = jnp.zeros_like(acc)

        acc[...] += _mm(a_ref[...], b_ref[...], dims)

        @pl.when(k == nk - 1)
        def _():
            finish(acc[...])

    in_specs = [a_spec, b_spec] + ([o_spec] if has_res else [])
    args = (a, b) + ((res,) if has_res else ())
    if dep is not None:
        in_specs.append(pl.BlockSpec(memory_space=pl.ANY))
        args += (dep,)
    return pl.pallas_call(
        body, name=name, grid=(M // tm, N // tn, nk), in_specs=in_specs, out_specs=o_spec,
        out_shape=jax.ShapeDtypeStruct((M, N), out_dtype),
        scratch_shapes=[pltpu.VMEM((tm, tn), F32)] if nk > 1 else [],
        compiler_params=_cparams(("parallel", "parallel", "arbitrary")),
    )(*args)


class _In:
    def __init__(self, arr, width=None, base=0, split=False, rows=True):
        self.arr, self.base, self.split, self.rows = arr, base, split, rows
        self.width = arr.shape[1] if width is None else width


class _Out:
    def __init__(self, cols, dtype=F32, width=None, split=False, rows=True, nrows=1):
        self.cols, self.dtype, self.split, self.rows, self.nrows = cols, dtype, split, rows, nrows
        self.width = cols if width is None else width


def _rowwise(name, fn, ins, outs, n_rows, br, ncol=1):
    br = min(br, n_rows)
    assert n_rows % br == 0, (name, n_rows, br)
    nrow_blocks = n_rows // br

    def in_spec(d):
        nb = br if d.rows else d.arr.shape[0]
        if d.rows and d.split:
            im = lambda j, i, base=d.base: (i, base + j)
        elif d.rows:
            im = lambda j, i, base=d.base: (i, base)
        elif d.split:
            im = lambda j, i, base=d.base: (0, base + j)
        else:
            im = lambda j, i, base=d.base: (0, base)
        return pl.BlockSpec((nb, d.width), im)

    def out_spec(d):
        nb = br if d.rows else d.nrows
        if d.rows and d.split:
            im = lambda j, i: (i, j)
        elif d.rows:
            im = lambda j, i: (i, 0)
        elif d.split:
            im = lambda j, i: (0, j)
        else:
            im = lambda j, i: (0, 0)
        return pl.BlockSpec((nb, d.width), im)

    n_in = len(ins)

    def body(*refs):
        i = pl.program_id(1)
        vals = [r[...] for r in refs[:n_in]]
        res = fn(*vals)
        if not isinstance(res, (tuple, list)):
            res = (res,)
        for d, ref, val in zip(outs, refs[n_in:], res):
            if d.rows:
                ref[...] = val.astype(d.dtype)
            else:
                @pl.when(i == 0)
                def _(ref=ref):
                    ref[...] = jnp.zeros_like(ref)

                ref[...] += val.astype(d.dtype)

    out_shape = [jax.ShapeDtypeStruct((n_rows if d.rows else d.nrows, d.cols), d.dtype) for d in outs]
    res = pl.pallas_call(
        body, name=name, grid=(ncol, nrow_blocks), in_specs=[in_spec(d) for d in ins],
        out_specs=[out_spec(d) for d in outs], out_shape=out_shape,
        compiler_params=_cparams(("parallel", "arbitrary")),
    )(*[d.arr for d in ins])
    return res


def _rms(x, g):
    return x * lax.rsqrt(jnp.mean(x * x, axis=-1, keepdims=True) + NORM_EPS) * g


def _sigmoid(x):
    return 0.5 * jnp.tanh(0.5 * x) + 0.5


def _silu(x):
    return x * _sigmoid(x)


def _log_sigmoid(x):
    return jnp.minimum(x, 0.0) - jnp.log(1.0 + jnp.exp(-jnp.abs(x)))


def _rope_tables(pos, inv_freq_row):
    ang = pos.astype(F32) * inv_freq_row
    return jnp.cos(ang), jnp.sin(ang)


def _head_sums_impl(v):
    w = v.shape[-1]
    shift = ATT_HEAD_DIM.bit_length() - 1
    r = lax.shift_right_logical(lax.broadcasted_iota(jnp.int32, (w, w), 0), shift)
    c = lax.shift_right_logical(lax.broadcasted_iota(jnp.int32, (w, w), 1), shift)
    ones = (r == c).astype(BF16)
    hi = v.astype(BF16)
    lo = (v - hi.astype(F32)).astype(BF16)
    dn = (_NN, ((), ()))
    return (lax.dot_general(hi, ones, dn, preferred_element_type=F32)
            + lax.dot_general(lo, ones, dn, preferred_element_type=F32))


@jax.custom_vjp
def _head_sums(v):
    return _head_sums_impl(v)


_head_sums.defvjp(lambda v: (_head_sums_impl(v), None), lambda _, g: (_head_sums_impl(g),))


def _rotate_half_impl(y):
    w = y.shape[-1]
    half = ROPE_DIM // 2
    lane = lax.broadcasted_iota(jnp.int32, y.shape, 1) & (ATT_HEAD_DIM - 1)
    above = pltpu.roll(y, w - half, axis=1)
    below = pltpu.roll(y, half, axis=1)
    return jnp.where(lane < half, -above, jnp.where(lane < ROPE_DIM, below, 0.0))


@jax.custom_vjp
def _rotate_half(y):
    return _rotate_half_impl(y)


_rotate_half.defvjp(lambda y: (_rotate_half_impl(y), None), lambda _, g: (-_rotate_half_impl(g),))


def _qk_prep(t, g, cos, sin):
    reps = t.shape[-1] // cos.shape[-1]
    if reps > 1:
        cos, sin = jnp.tile(cos, (1, reps)), jnp.tile(sin, (1, reps))
    y = t * lax.rsqrt(_head_sums(t * t) * (1.0 / ATT_HEAD_DIM) + NORM_EPS) * g
    return y * cos + _rotate_half(y) * sin


def _attn_head(q, kb, vb, sink, valid):
    s = mm_nt(q, kb) * (ATT_HEAD_DIM ** -0.5)
    s = jnp.where(valid, s, -jnp.inf)
    m = jnp.maximum(jnp.max(s, axis=-1, keepdims=True), sink)
    p = jnp.exp(s - m)
    den = jnp.sum(p, axis=-1, keepdims=True) + jnp.exp(sink - m)
    return mm_nn(p * (1.0 / den), vb)


def _mlstm_chunk(q, k, v, li, lf, C, n, m, incl, incl_t, eye):
    k = k * (MLSTM_HEAD_DIM ** -0.5)
    lf_row = jnp.sum(eye * lf, axis=0, keepdims=True)
    li_row = jnp.sum(eye * li, axis=0, keepdims=True)
    b = jnp.sum(incl * lf_row, axis=1, keepdims=True)
    b_row = jnp.sum(incl_t * lf, axis=0, keepdims=True)
    b_tot = jnp.sum(lf, axis=0, keepdims=True)
    a = b_tot - b + li
    a_max = jnp.max(a, axis=0, keepdims=True)
    kw = k * jnp.exp(a - a_max)
    c_loc = mm_tn(kw, v)
    n_loc = jnp.sum(kw, axis=0, keepdims=True)

    dmat = jnp.where(incl > 0.5, b - b_row + li_row, -jnp.inf)
    inter = b + m
    m_t = jnp.maximum(inter, jnp.max(dmat, axis=1, keepdims=True))
    sc = mm_nt(q, k) * jnp.exp(dmat - m_t)
    scale_in = jnp.exp(inter - m_t)
    num = mm_nn(sc, v) + scale_in * mm_nn(q, C)
    den = jnp.sum(sc, axis=1, keepdims=True) + scale_in * jnp.sum(q * n, axis=1, keepdims=True)
    h = num * (1.0 / jnp.maximum(jnp.abs(den), jnp.exp(-m_t)))

    m_new = jnp.maximum(b_tot + m, a_max)
    s_p = jnp.exp(b_tot + m - m_new)
    s_l = jnp.exp(a_max - m_new)
    return h, s_p * C + s_l * c_loc, s_p * n + s_l * n_loc, m_new


def _mlstm_combine(hf, hb, o_pre, g):
    h = hf + hb
    mu = jnp.mean(h, axis=-1, keepdims=True)
    var = jnp.mean(jnp.square(h - mu), axis=-1, keepdims=True)
    return _sigmoid(o_pre) * ((h - mu) * lax.rsqrt(var + NORM_EPS) * g)


def _merge(ga, gm, za, zm):
    return _sigmoid(ga) * za + _sigmoid(gm) * zm


def _attn_mask(n, seq):
    shape = (ATT_GROUP * ATT_BLOCK, 3 * ATT_BLOCK)
    qi = n * ATT_BLOCK + (lax.broadcasted_iota(jnp.int32, shape, 0) & (ATT_BLOCK - 1))
    kj = (n - 1) * ATT_BLOCK + lax.broadcasted_iota(jnp.int32, shape, 1)
    return (jnp.abs(qi - kj) <= WINDOW) & (kj >= 0) & (kj < seq)


def _attn_specs(nq, v_base):
    q_spec = pl.BlockSpec((1, ATT_BLOCK, ATT_WIDTH), lambda b, n: (b, n, 0))

    def kv_spec(off, base=0):
        return pl.BlockSpec((1, ATT_BLOCK, ATT_KV_WIDTH), lambda b, n: (b, jnp.clip(n + off, 0, nq - 1), base))

    sink_spec = pl.BlockSpec((ATT_KV_HEADS, ATT_GROUP, 1, 1), lambda b, n: (0, 0, 0, 0))
    specs = [q_spec, kv_spec(-1), kv_spec(0), kv_spec(1), kv_spec(-1, v_base), kv_spec(0, v_base), kv_spec(1, v_base), sink_spec]
    return q_spec, specs, sink_spec


def _head(h):
    return slice(h * ATT_HEAD_DIM, (h + 1) * ATT_HEAD_DIM)


def _group_rows(q_ref, s_ref, h):
    q4 = jnp.concatenate([q_ref[0, :, _head(h * ATT_GROUP + g)] for g in range(ATT_GROUP)], axis=0)
    sink4 = jnp.concatenate([jnp.broadcast_to(s_ref[h, g], (ATT_BLOCK, 1)) for g in range(ATT_GROUP)], axis=0)
    return q4, sink4


def _attn_fwd(q, k, proj3, sink):
    B, S, _ = q.shape
    nq = S // ATT_BLOCK
    q_spec, specs, _ = _attn_specs(nq, C_VA // ATT_KV_WIDTH)

    def body(q_ref, kp, kc, kn, vp, vc, vn, s_ref, o_ref):
        valid = _attn_mask(pl.program_id(1), S)
        for h in range(ATT_KV_HEADS):
            kb = jnp.concatenate([kp[0, :, _head(h)], kc[0, :, _head(h)], kn[0, :, _head(h)]], axis=0)
            vb = jnp.concatenate([vp[0, :, _head(h)], vc[0, :, _head(h)], vn[0, :, _head(h)]], axis=0)
            q4, sink4 = _group_rows(q_ref, s_ref, h)
            o4 = _attn_head(q4, kb, vb, sink4, valid).astype(BF16)
            for g in range(ATT_GROUP):
                o_ref[0, :, _head(h * ATT_GROUP + g)] = o4[g * ATT_BLOCK:(g + 1) * ATT_BLOCK]

    return pl.pallas_call(
        body, name="attn_fwd", grid=(B, nq), in_specs=specs,
        out_specs=q_spec, out_shape=jax.ShapeDtypeStruct(q.shape, BF16),
        compiler_params=_cparams(("parallel", "arbitrary")),
    )(q, k, k, k, proj3, proj3, proj3, sink)


def _attn_bwd(q, k, proj3, sink, dy):
    B, S, _ = q.shape
    nq = S // ATT_BLOCK
    q_spec, specs, sink_spec = _attn_specs(nq, C_VA // ATT_KV_WIDTH)
    kv_full = pl.BlockSpec((1, S, ATT_KV_WIDTH), lambda b, n: (b, 0, 0))

    def body(q_ref, kp, kc, kn, vp, vc, vn, s_ref, dy_ref, dq_ref, dk_ref, dv_ref, ds_ref):
        b, n = pl.program_id(0), pl.program_id(1)
        valid = _attn_mask(n, S)

        @pl.when(n == 0)
        def _():
            dk_ref[...] = jnp.zeros_like(dk_ref)
            dv_ref[...] = jnp.zeros_like(dv_ref)

        @pl.when((n == 0) & (b == 0))
        def _():
            ds_ref[...] = jnp.zeros_like(ds_ref)

        for h in range(ATT_KV_HEADS):
            kb = jnp.concatenate([kp[0, :, _head(h)], kc[0, :, _head(h)], kn[0, :, _head(h)]], axis=0)
            vb = jnp.concatenate([vp[0, :, _head(h)], vc[0, :, _head(h)], vn[0, :, _head(h)]], axis=0)
            q4, sink4 = _group_rows(q_ref, s_ref, h)
            dy4 = jnp.concatenate([dy_ref[0, :, _head(h * ATT_GROUP + g)] for g in range(ATT_GROUP)], axis=0)
            _, vjp = jax.vjp(functools.partial(_attn_head, valid=valid), q4, kb, vb, sink4)
            dq4, dkb, dvb, dsink4 = vjp(dy4)
            for g in range(ATT_GROUP):
                rows = slice(g * ATT_BLOCK, (g + 1) * ATT_BLOCK)
                dq_ref[0, :, _head(h * ATT_GROUP + g)] = dq4[rows]
                ds_ref[h, g] += jnp.sum(dsink4[rows], axis=0, keepdims=True)
            for j, off in enumerate((-1, 0, 1)):
                start = pl.multiple_of(jnp.clip(n + off, 0, nq - 1) * ATT_BLOCK, ATT_BLOCK)
                rows = pl.ds(start, ATT_BLOCK)
                dk_ref[0, rows, _head(h)] += dkb[j * ATT_BLOCK:(j + 1) * ATT_BLOCK]
                dv_ref[0, rows, _head(h)] += dvb[j * ATT_BLOCK:(j + 1) * ATT_BLOCK]

    kv_shape = jax.ShapeDtypeStruct(k.shape, F32)
    return pl.pallas_call(
        body, name="attn_bwd", grid=(B, nq), in_specs=specs + [q_spec],
        out_specs=[q_spec, kv_full, kv_full, sink_spec],
        out_shape=[jax.ShapeDtypeStruct(q.shape, F32), kv_shape, kv_shape, jax.ShapeDtypeStruct(sink.shape, F32)],
        compiler_params=_cparams(("arbitrary", "arbitrary")),
    )(q, k, k, k, proj3, proj3, proj3, sink, dy)


CONV_COLS = 256


def _conv_taps(u, seq):
    row = lax.broadcasted_iota(jnp.int32, u.shape, 0)
    prev = jnp.where(row == 0, 0.0, pltpu.roll(u, 1, axis=0))
    nxt = jnp.where(row == seq - 1, 0.0, pltpu.roll(u, seq - 1, axis=0))
    return prev, nxt


def _conv_fwd(proj3, w8):
    B, S, _ = proj3.shape
    ncb = 2 * MLSTM_WIDTH // CONV_COLS

    def body(u_ref, w_ref, o_ref):
        u = u_ref[0]
        prev, nxt = _conv_taps(u, S)
        o_ref[0] = _silu(prev * w_ref[0:1, :] + u * w_ref[1:2, :] + nxt * w_ref[2:3, :] + w_ref[3:4, :])

    return pl.pallas_call(
        body, name="conv_fwd", grid=(B, ncb),
        in_specs=[pl.BlockSpec((1, S, CONV_COLS), lambda b, c: (b, 0, C_QK // CONV_COLS + c)),
                  pl.BlockSpec((8, CONV_COLS), lambda b, c: (0, c))],
        out_specs=pl.BlockSpec((1, S, CONV_COLS), lambda b, c: (b, 0, c)),
        out_shape=jax.ShapeDtypeStruct((B, S, 2 * MLSTM_WIDTH), F32),
        compiler_params=_cparams(("parallel", "parallel")),
    )(proj3, w8)


def _conv_bwd(proj3, w8, dout_f, dout_b):
    B, S, _ = proj3.shape
    ncb = 2 * MLSTM_WIDTH // CONV_COLS

    def body(u_ref, w_ref, df_ref, db_ref, du_ref, dw_ref):
        b = pl.program_id(1)
        u = u_ref[0]
        prev, nxt = _conv_taps(u, S)
        w0, w1, w2 = w_ref[0:1, :], w_ref[1:2, :], w_ref[2:3, :]
        pre = prev * w0 + u * w1 + nxt * w2 + w_ref[3:4, :]
        sig = _sigmoid(pre)
        dpre = (df_ref[0] + db_ref[0]) * (sig * (1.0 + pre * (1.0 - sig)))
        dprev, dnxt = _conv_taps(dpre, S)
        du_ref[0] = (dnxt * w0 + dpre * w1 + dprev * w2).astype(BF16)

        @pl.when(b == 0)
        def _():
            dw_ref[...] = jnp.zeros_like(dw_ref)

        dw_ref[0:1, :] += jnp.sum(dpre * prev, axis=0, keepdims=True)
        dw_ref[1:2, :] += jnp.sum(dpre * u, axis=0, keepdims=True)
        dw_ref[2:3, :] += jnp.sum(dpre * nxt, axis=0, keepdims=True)
        dw_ref[3:4, :] += jnp.sum(dpre, axis=0, keepdims=True)

    blk = pl.BlockSpec((1, S, CONV_COLS), lambda c, b: (b, 0, c))
    return pl.pallas_call(
        body, name="conv_bwd", grid=(ncb, B),
        in_specs=[pl.BlockSpec((1, S, CONV_COLS), lambda c, b: (b, 0, C_QK // CONV_COLS + c)),
                  pl.BlockSpec((8, CONV_COLS), lambda c, b: (0, c)), blk, blk],
        out_specs=[blk, pl.BlockSpec((8, CONV_COLS), lambda c, b: (0, c))],
        out_shape=[jax.ShapeDtypeStruct((B, S, 2 * MLSTM_WIDTH), BF16), jax.ShapeDtypeStruct((8, 2 * MLSTM_WIDTH), F32)],
        compiler_params=_cparams(("parallel", "arbitrary")),
    )(proj3, w8, dout_f, dout_b)


MLSTM_HEADS_PER_STEP = 4


def _chunk_masks(direction):
    t = lax.broadcasted_iota(jnp.int32, (MLSTM_CHUNK, MLSTM_CHUNK), 0)
    s = lax.broadcasted_iota(jnp.int32, (MLSTM_CHUNK, MLSTM_CHUNK), 1)
    le, ge = (s <= t).astype(F32), (s >= t).astype(F32)
    eye = (s == t).astype(F32)
    return (le, ge, eye) if direction == 0 else (ge, le, eye)


def _gate_cols(gates, direction, head):
    lane = lax.broadcasted_iota(jnp.int32, gates.shape, 1)
    sel_i = (lane == (2 * direction) * MLSTM_HEADS + head).astype(F32)
    sel_f = (lane == (2 * direction + 1) * MLSTM_HEADS + head).astype(F32)
    return sel_i, sel_f


def _mlstm_fwd(qk, proj3, bias):
    B, S, _ = qk.shape
    nc = S // MLSTM_CHUNK
    H, L, DH = MLSTM_HEADS, MLSTM_CHUNK, MLSTM_HEAD_DIM

    def chunk_of(d, c):
        return c if d == 0 else nc - 1 - c

    HS = MLSTM_HEADS_PER_STEP

    def body(qkf, qkb, vf, vb, gf, gb, bias_ref, hf, hb, csf, csb, nsf, nsb, msf, msb, c_st, n_st, m_st):
        c, hg = pl.program_id(1), pl.program_id(2)

        @pl.when(c == 0)
        def _():
            for d in range(2):
                for j in range(HS):
                    c_st[d, hg * HS + j] = jnp.zeros((DH, DH), F32)
                    n_st[d, hg * HS + j] = jnp.zeros((1, DH), F32)
                    m_st[d, hg * HS + j] = jnp.zeros((1, DH), F32)

        for d, (qk_ref, v_ref, g_ref, h_ref, cs, ns, ms) in enumerate(
                ((qkf, vf, gf, hf, csf, nsf, msf), (qkb, vb, gb, hb, csb, nsb, msb))):
            incl, incl_t, eye = _chunk_masks(d)
            gates = g_ref[0] + bias_ref[...]
            log_f = _log_sigmoid(gates)
            for j in range(HS):
                h = hg * HS + j
                sel_i, sel_f = _gate_cols(gates, d, h)
                li = jnp.sum(gates * sel_i, axis=1, keepdims=True)
                lf = jnp.sum(log_f * sel_f, axis=1, keepdims=True)
                c_in, n_in, m_in = c_st[d, h], n_st[d, h], m_st[d, h]
                cs[0, 0, j], ns[0, 0, j], ms[0, 0, j] = c_in, n_in, m_in
                hh, c_new, n_new, m_new = _mlstm_chunk(
                    qk_ref[0, :, 2 * j * DH:(2 * j + 1) * DH], qk_ref[0, :, (2 * j + 1) * DH:(2 * j + 2) * DH],
                    v_ref[0, :, j * DH:(j + 1) * DH], li, lf, c_in, n_in,
                    jnp.max(m_in, axis=1, keepdims=True), incl, incl_t, eye)
                h_ref[0, :, j * DH:(j + 1) * DH] = hh
                c_st[d, h], n_st[d, h] = c_new, n_new
                m_st[d, h] = jnp.broadcast_to(m_new, (1, DH))

    def tok_spec(width, base, d, per_head):
        return pl.BlockSpec((1, L, width), lambda b, c, h: (b, chunk_of(d, c), base + (h if per_head else 0)))

    def st_spec(shape, d):
        return pl.BlockSpec((1, 1, HS) + shape, lambda b, c, h: (b, chunk_of(d, c), h, 0, 0))

    in_specs = [tok_spec(2 * HS * DH, 0, 0, True), tok_spec(2 * HS * DH, 0, 1, True),
                tok_spec(HS * DH, C_VM // (HS * DH), 0, True), tok_spec(HS * DH, C_VM // (HS * DH), 1, True),
                tok_spec(LANES, C_GATES // LANES, 0, False), tok_spec(LANES, C_GATES // LANES, 1, False),
                pl.BlockSpec((1, LANES), lambda b, c, h: (0, 0))]
    out_specs = [tok_spec(HS * DH, 0, 0, True), tok_spec(HS * DH, 0, 1, True),
                 st_spec((DH, DH), 0), st_spec((DH, DH), 1), st_spec((1, DH), 0), st_spec((1, DH), 1),
                 st_spec((1, DH), 0), st_spec((1, DH), 1)]
    hs = jax.ShapeDtypeStruct((B, S, H * DH), F32)
    cs = jax.ShapeDtypeStruct((B, nc, H, DH, DH), F32)
    vs = jax.ShapeDtypeStruct((B, nc, H, 1, DH), F32)
    return pl.pallas_call(
        body, name="mlstm_fwd", grid=(B, nc, H // HS), in_specs=in_specs, out_specs=out_specs,
        out_shape=[hs, hs, cs, cs, vs, vs, vs, vs],
        scratch_shapes=[pltpu.VMEM((2, H, DH, DH), F32), pltpu.VMEM((2, H, 1, DH), F32), pltpu.VMEM((2, H, 1, DH), F32)],
        compiler_params=_cparams(("parallel", "arbitrary", "arbitrary")),
    )(qk, qk, proj3, proj3, proj3, proj3, bias)


def _mlstm_bwd(qk, proj3, bias, states, dh):
    B, S, _ = qk.shape
    nc = S // MLSTM_CHUNK
    H, L, DH = MLSTM_HEADS, MLSTM_CHUNK, MLSTM_HEAD_DIM

    def chunk_of(d, c):
        return nc - 1 - c if d == 0 else c

    HS = MLSTM_HEADS_PER_STEP

    def body(qkf, qkb, vf, vb, gf, gb, bias_ref, csf, csb, nsf, nsb, msf, msb, dhf, dhb,
             dqkf, dqkb, dvf, dvb, dgf, dgb, dc_st, dn_st, dm_st):
        c, hg = pl.program_id(1), pl.program_id(2)

        @pl.when(c == 0)
        def _():
            for d in range(2):
                for j in range(HS):
                    dc_st[d, hg * HS + j] = jnp.zeros((DH, DH), F32)
                    dn_st[d, hg * HS + j] = jnp.zeros((1, DH), F32)
                    dm_st[d, hg * HS + j] = jnp.zeros((1, DH), F32)

        @pl.when(hg == 0)
        def _():
            dgf[...] = jnp.zeros_like(dgf)
            dgb[...] = jnp.zeros_like(dgb)

        for d, (qk_ref, v_ref, g_ref, cs, ns, ms, dh_ref, dqk_ref, dv_ref, dg_ref) in enumerate(
                ((qkf, vf, gf, csf, nsf, msf, dhf, dqkf, dvf, dgf), (qkb, vb, gb, csb, nsb, msb, dhb, dqkb, dvb, dgb))):
            incl, incl_t, eye = _chunk_masks(d)
            gates = g_ref[0] + bias_ref[...]
            log_f = _log_sigmoid(gates)
            d_li = jnp.zeros_like(gates)
            d_lf = jnp.zeros_like(gates)
            for j in range(HS):
                h = hg * HS + j
                sel_i, sel_f = _gate_cols(gates, d, h)
                li = jnp.sum(gates * sel_i, axis=1, keepdims=True)
                lf = jnp.sum(log_f * sel_f, axis=1, keepdims=True)
                m_in = jnp.max(ms[0, 0, j], axis=1, keepdims=True)
                _, vjp = jax.vjp(
                    functools.partial(_mlstm_chunk, incl=incl, incl_t=incl_t, eye=eye),
                    qk_ref[0, :, 2 * j * DH:(2 * j + 1) * DH], qk_ref[0, :, (2 * j + 1) * DH:(2 * j + 2) * DH],
                    v_ref[0, :, j * DH:(j + 1) * DH], li, lf, cs[0, 0, j], ns[0, 0, j], m_in)
                dm_out = jnp.max(dm_st[d, h], axis=1, keepdims=True)
                dq, dk, dv, dli, dlf, dc, dn, dm = vjp((dh_ref[0, :, j * DH:(j + 1) * DH], dc_st[d, h], dn_st[d, h], dm_out))
                dqk_ref[0, :, 2 * j * DH:(2 * j + 1) * DH] = dq
                dqk_ref[0, :, (2 * j + 1) * DH:(2 * j + 2) * DH] = dk
                dv_ref[0, :, j * DH:(j + 1) * DH] = dv
                d_li += dli * sel_i
                d_lf += dlf * sel_f
                dc_st[d, h], dn_st[d, h] = dc, dn
                dm_st[d, h] = jnp.broadcast_to(dm, (1, DH))
            dg_ref[0] += d_li + d_lf * _sigmoid(-gates)

    def tok_spec(width, base, d, per_head):
        return pl.BlockSpec((1, L, width), lambda b, c, h: (b, chunk_of(d, c), base + (h if per_head else 0)))

    def st_spec(shape, d):
        return pl.BlockSpec((1, 1, HS) + shape, lambda b, c, h: (b, chunk_of(d, c), h, 0, 0))

    in_specs = [tok_spec(2 * HS * DH, 0, 0, True), tok_spec(2 * HS * DH, 0, 1, True),
                tok_spec(HS * DH, C_VM // (HS * DH), 0, True), tok_spec(HS * DH, C_VM // (HS * DH), 1, True),
                tok_spec(LANES, C_GATES // LANES, 0, False), tok_spec(LANES, C_GATES // LANES, 1, False),
                pl.BlockSpec((1, LANES), lambda b, c, h: (0, 0)),
                st_spec((DH, DH), 0), st_spec((DH, DH), 1), st_spec((1, DH), 0), st_spec((1, DH), 1),
                st_spec((1, DH), 0), st_spec((1, DH), 1), tok_spec(HS * DH, 0, 0, True), tok_spec(HS * DH, 0, 1, True)]
    out_specs = [tok_spec(2 * HS * DH, 0, 0, True), tok_spec(2 * HS * DH, 0, 1, True),
                 tok_spec(HS * DH, 0, 0, True), tok_spec(HS * DH, 0, 1, True),
                 tok_spec(LANES, 0, 0, False), tok_spec(LANES, 0, 1, False)]
    qks = jax.ShapeDtypeStruct((B, S, 2 * H * DH), F32)
    vs = jax.ShapeDtypeStruct((B, S, H * DH), F32)
    gs = jax.ShapeDtypeStruct((B, S, LANES), F32)
    csf, csb, nsf, nsb, msf, msb = states
    return pl.pallas_call(
        body, name="mlstm_bwd", grid=(B, nc, H // HS), in_specs=in_specs, out_specs=out_specs,
        out_shape=[qks, qks, vs, vs, gs, gs],
        scratch_shapes=[pltpu.VMEM((2, H, DH, DH), F32), pltpu.VMEM((2, H, 1, DH), F32), pltpu.VMEM((2, H, 1, DH), F32)],
        compiler_params=_cparams(("parallel", "arbitrary", "arbitrary")),
    )(qk, qk, proj3, proj3, proj3, proj3, bias, csf, csb, nsf, nsb, msf, msb, dh, dh)


ROW_BLOCK = 256
FF_COLS = 512
FF_SHARD = D_FF // N_DEV
FF_SHARD_PAD = 384
FF_PAD = N_DEV * FF_SHARD_PAD


def _rms_fwd(name, x, g):
    T = x.shape[0]
    return _rowwise(name, lambda xv, gv: _rms(xv, gv), [_In(x), _In(g, rows=False)], [_Out(D_MODEL, BF16)], T, ROW_BLOCK)[0]


def _rms_bwd(name, x, g, dh, dres):
    T = x.shape[0]

    def fn(xv, gv, dhv, drv):
        _, vjp = jax.vjp(_rms, xv, gv)
        dx, dg = vjp(dhv)
        return drv + dx, dg

    return _rowwise(name, fn, [_In(x), _In(g, rows=False), _In(dh), _In(dres)],
                    [_Out(D_MODEL), _Out(D_MODEL, rows=False)], T, ROW_BLOCK)


def _mmw(name, a, w, mode, **kw):
    if isinstance(w, tuple):
        return _matmul(name, a, w[0], mode, bl=w[1], **kw)
    return _matmul(name, a, w, mode, **kw)


def _swiglu(gate, up):
    return _silu(gate) * up


def _ffn_in(name, h, wg, wu):
    (M, K), N = h.shape, wg.shape[0]
    tm, tn = _first_divisor(M, (1024, 512, 256, 128)), FF_COLS

    def body(h_ref, wg_ref, wu_ref, g_ref, u_ref, a_ref):
        hv = h_ref[...]
        gate = _mm(hv, wg_ref[...], _NT)
        up = _mm(hv, wu_ref[...], _NT)
        g_ref[...], u_ref[...] = gate, up
        a_ref[...] = _swiglu(gate, up).astype(BF16)

    w_spec = pl.BlockSpec((tn, K), lambda i, j: (j, 0))
    o_spec = pl.BlockSpec((tm, tn), lambda i, j: (i, j))
    return pl.pallas_call(
        body, name=name, grid=(M // tm, N // tn), in_specs=[pl.BlockSpec((tm, K), lambda i, j: (i, 0)), w_spec, w_spec],
        out_specs=[o_spec, o_spec, o_spec],
        out_shape=[jax.ShapeDtypeStruct((M, N), F32), jax.ShapeDtypeStruct((M, N), F32), jax.ShapeDtypeStruct((M, N), BF16)],
        compiler_params=_cparams(("parallel", "parallel")),
    )(h, wg, wu)


def _ffn_dact(name, dx, wd, gate, up):
    (M, K), N = dx.shape, wd.shape[0]
    tm, tn = _first_divisor(M, (1024, 512, 256, 128)), FF_COLS

    def body(dx_ref, wd_ref, g_ref, u_ref, dg_ref, du_ref):
        dact = _mm(dx_ref[...], wd_ref[...], _NT) * 0.5
        _, vjp = jax.vjp(_swiglu, g_ref[...], u_ref[...])
        dgate, dup = vjp(dact)
        dg_ref[...], du_ref[...] = dgate.astype(BF16), dup.astype(BF16)

    o_spec = pl.BlockSpec((tm, tn), lambda i, j: (i, j))
    return pl.pallas_call(
        body, name=name, grid=(M // tm, N // tn),
        in_specs=[pl.BlockSpec((tm, K), lambda i, j: (i, 0)), pl.BlockSpec((tn, K), lambda i, j: (j, 0)), o_spec, o_spec],
        out_specs=[o_spec, o_spec],
        out_shape=[jax.ShapeDtypeStruct((M, N), BF16), jax.ShapeDtypeStruct((M, N), BF16)],
        compiler_params=_cparams(("parallel", "parallel")),
    )(dx, wd, gate, up)


def _ffn_dh(name, dgate, dup, wg, wu, dep):
    (M, K), N = dgate.shape, wg.shape[1]
    tm, tk = _first_divisor(M, (1024, 512, 256, 128)), _first_divisor(K, (1024, 512, 384, 256, 128))
    nk = K // tk

    def body(dg_ref, du_ref, wg_ref, wu_ref, dep_ref, o_ref, acc):
        k = pl.program_id(1)

        @pl.when(k == 0)
        def _():
            acc[...] = jnp.zeros_like(acc)

        acc[...] += _mm(dg_ref[...], wg_ref[...], _NN) + _mm(du_ref[...], wu_ref[...], _NN)

        @pl.when(k == nk - 1)
        def _():
            o_ref[...] = acc[...]

    a_spec = pl.BlockSpec((tm, tk), lambda i, k: (i, k))
    w_spec = pl.BlockSpec((tk, N), lambda i, k: (k, 0))
    return pl.pallas_call(
        body, name=name, grid=(M // tm, nk), in_specs=[a_spec, a_spec, w_spec, w_spec, pl.BlockSpec(memory_space=pl.ANY)],
        out_specs=pl.BlockSpec((tm, N), lambda i, k: (i, 0)), out_shape=jax.ShapeDtypeStruct((M, N), F32),
        scratch_shapes=[pltpu.VMEM((tm, N), F32)], compiler_params=_cparams(("parallel", "arbitrary")),
    )(dgate, dup, wg, wu, dep)


def _ffn_fwd(tag, x, g, wg, wu, wd):
    h = _rms_fwd(tag + "_norm", x, g)
    gate, up, act = _ffn_in(tag + "_in", h, wg, wu)
    if callable(wd):
        wd = wd(act)
    out = _mmw(tag + "_down", act, wd, "nn", res=x, scale=0.5)
    return out, (x, h, gate, up, act), wd


def _ffn_bwd(tag, saved, g, wg, wu, wd, dx, on_dw):
    x, h, gate, up, act = saved
    dgate, dup = _ffn_dact(tag + "_dact", dx, wd, gate, up)
    dwd = _matmul(tag + "_dwd", act, dx, "tn", scale=0.5, out_dtype=BF16)
    dwg = _matmul(tag + "_dwg", dgate, h, "tn", out_dtype=BF16)
    dwu = _matmul(tag + "_dwu", dup, h, "tn", out_dtype=BF16)
    token = on_dw({tag + "_w_gate": dwg, tag + "_w_up": dwu, tag + "_w_down": dwd}, dwu)
    dh = _ffn_dh(tag + "_dh", dgate, dup, wg, wu, token)
    dx_new, dg = _rms_bwd(tag + "_dnorm", x, g, dh, dx)
    return dx_new, dg


def _rope_cos_sin(positions):
    half = ROPE_DIM // 2
    inv_freq = jnp.power(jnp.float32(ROPE_THETA), -jnp.arange(half, dtype=F32) * (2.0 / ROPE_DIM))
    head = jnp.zeros((ATT_HEAD_DIM,), F32).at[:ROPE_DIM].set(jnp.concatenate([inv_freq, inv_freq]))
    row = jnp.tile(head, LANES // ATT_HEAD_DIM)[None, :]
    T = positions.shape[0]
    return _rowwise("rope_tables", _rope_tables, [_In(positions), _In(row, rows=False)], [_Out(LANES), _Out(LANES)], T, 1024)


def _prep_fwd(name, src, width, base, g, cos, sin):
    return _rowwise(name, _qk_prep, [_In(src, width, base), _In(g, rows=False), _In(cos), _In(sin)],
                    [_Out(width)], src.shape[0], 512)[0]


def _prep_bwd(name, src, width, base, g, cos, sin, dout):
    def fn(tv, gv, cv, sv, dv):
        _, vjp = jax.vjp(lambda a, b: _qk_prep(a, b, cv, sv), tv, gv)
        return vjp(dv)

    return _rowwise(name, fn, [_In(src, width, base), _In(g, rows=False), _In(cos), _In(sin), _In(dout)],
                    [_Out(width, BF16), _Out(width, rows=False)], src.shape[0], 512)


def _to_heads(t, B, S, nh):
    return t.reshape(B, S, nh, ATT_HEAD_DIM).transpose(0, 2, 1, 3)


def _from_heads(t):
    B, nh, S, _ = t.shape
    return t.transpose(0, 2, 1, 3).reshape(B * S, nh * ATT_HEAD_DIM)


def _mix_fwd(x, cos, sin, B, S, p):
    T = B * S
    h = _rms_fwd("mix_norm", x, p["mix_norm"])
    proj = _matmul("mix_proj", h, p["w_in"], "nn")
    proj3 = proj.reshape(B, S, IN_PAD)
    q_gain = jnp.tile(p["attn_q_norm"], (1, ATT_HEADS))
    k_gain = jnp.tile(p["attn_k_norm"], (1, ATT_KV_HEADS))
    q_r = _prep_fwd("q_prep", proj, ATT_WIDTH, C_QA // ATT_WIDTH, q_gain, cos, sin)
    k_r = _prep_fwd("k_prep", proj, ATT_KV_WIDTH, C_KA // ATT_KV_WIDTH, k_gain, cos, sin)
    qh = q_r.reshape(B, S, ATT_WIDTH)
    kh = k_r.reshape(B, S, ATT_KV_WIDTH)
    sink = p["attn_sink"].reshape(ATT_KV_HEADS, ATT_GROUP, 1, 1)
    y_a = _attn_fwd(qh, kh, proj3, sink).reshape(T, ATT_WIDTH)

    qk_c = _conv_fwd(proj3, p["conv_w8"])
    hf, hb, *states = _mlstm_fwd(qk_c, proj3, p["gate_bias"])
    hf2, hb2 = hf.reshape(T, MLSTM_WIDTH), hb.reshape(T, MLSTM_WIDTH)
    DH = MLSTM_HEAD_DIM
    y_m = _rowwise("mlstm_out", _mlstm_combine,
                   [_In(hf2, DH, split=True), _In(hb2, DH, split=True), _In(proj, DH, C_OM // DH, split=True),
                    _In(p["mlstm_out_norm"], DH, split=True, rows=False)],
                   [_Out(MLSTM_WIDTH, BF16, DH, split=True)], T, 1024, ncol=MLSTM_HEADS)[0]

    za = _mmw("branch_a", y_a, p["w_branch_attn"], "nn")
    zm = _mmw("branch_m", y_m, p["w_branch_mlstm"], "nn")
    W = 512
    merged = _rowwise("merge", _merge,
                      [_In(proj, W, C_GMERGE // W, split=True), _In(proj, W, (C_GMERGE + D_MODEL) // W, split=True),
                       _In(za, W, split=True), _In(zm, W, split=True)],
                      [_Out(D_MODEL, BF16, W, split=True)], T, 512, ncol=D_MODEL // W)[0]
    out = _mmw("mix_out", merged, p["w_out"], "nn", res=x)
    saved = dict(x=x, h=h, proj=proj, q_gain=q_gain, k_gain=k_gain, qh=qh, kh=kh, sink=sink, y_a=y_a, qk_c=qk_c,
                 hf=hf2, hb=hb2, states=states, y_m=y_m, za=za, zm=zm, merged=merged)
    return out, saved


def _mix_bwd(sv, cos, sin, B, S, p, dx, on_dw):
    T = B * S
    DH = MLSTM_HEAD_DIM
    proj = sv["proj"]
    proj3 = proj.reshape(B, S, IN_PAD)
    g = {}
    dmerged = _mmw("mix_dmerged", dx, p["w_out"], "nt")
    g["w_out"] = _matmul("mix_dwout", sv["merged"], dx, "tn", out_dtype=BF16)
    W = 512

    def merge_bwd(ga, gm, za, zm, dm):
        _, vjp = jax.vjp(_merge, ga, gm, za, zm)
        return vjp(dm)

    dga, dgm, dza, dzm = _rowwise(
        "merge_bwd", merge_bwd,
        [_In(proj, W, C_GMERGE // W, split=True), _In(proj, W, (C_GMERGE + D_MODEL) // W, split=True),
         _In(sv["za"], W, split=True), _In(sv["zm"], W, split=True), _In(dmerged, W, split=True)],
        [_Out(D_MODEL, BF16, W, split=True), _Out(D_MODEL, BF16, W, split=True),
         _Out(D_MODEL, BF16, W, split=True), _Out(D_MODEL, BF16, W, split=True)], T, 512, ncol=D_MODEL // W)
    dya = _mmw("branch_a_dx", dza, p["w_branch_attn"], "nt")
    g["w_branch_attn"] = _matmul("branch_a_dw", sv["y_a"], dza, "tn", out_dtype=BF16)
    dym = _mmw("branch_m_dx", dzm, p["w_branch_mlstm"], "nt")
    g["w_branch_mlstm"] = _matmul("branch_m_dw", sv["y_m"], dzm, "tn", out_dtype=BF16)

    def combine_bwd(hf, hb, o_pre, gn, dy):
        _, vjp = jax.vjp(_mlstm_combine, hf, hb, o_pre, gn)
        dhf, _, do, dg = vjp(dy)
        return dhf, do, dg

    dh, dom, g["mlstm_out_norm"] = _rowwise(
        "mlstm_out_bwd", combine_bwd,
        [_In(sv["hf"], DH, split=True), _In(sv["hb"], DH, split=True), _In(proj, DH, C_OM // DH, split=True),
         _In(p["mlstm_out_norm"], DH, split=True, rows=False), _In(dym, DH, split=True)],
        [_Out(MLSTM_WIDTH, F32, DH, split=True), _Out(MLSTM_WIDTH, BF16, DH, split=True),
         _Out(MLSTM_WIDTH, F32, DH, split=True, rows=False)], T, 1024, ncol=MLSTM_HEADS)
    dqk_f, dqk_b, dv_f, dv_b, dg_f, dg_b = _mlstm_bwd(sv["qk_c"], proj3, p["gate_bias"], sv["states"],
                                                       dh.reshape(B, S, MLSTM_WIDTH))
    dgates, dvm, g["gate_bias"] = _rowwise(
        "mlstm_dsum", lambda a, b, c, d: (a + b, c + d, jnp.sum(a + b, axis=0, keepdims=True)),
        [_In(dg_f.reshape(T, LANES)), _In(dg_b.reshape(T, LANES)), _In(dv_f.reshape(T, MLSTM_WIDTH)), _In(dv_b.reshape(T, MLSTM_WIDTH))],
        [_Out(LANES, BF16), _Out(MLSTM_WIDTH, BF16), _Out(LANES, rows=False)], T, 1024)
    dqk, g["conv_w8"] = _conv_bwd(proj3, p["conv_w8"], dqk_f, dqk_b)

    dqh, dkh, dvh, dsink = _attn_bwd(sv["qh"], sv["kh"], proj3, sv["sink"], dya.reshape(B, S, ATT_WIDTH))
    g["attn_sink"] = dsink.reshape(1, ATT_HEADS)
    dva = dvh.reshape(T, ATT_KV_WIDTH)
    dqa, dq_gain = _prep_bwd("q_prep_bwd", proj, ATT_WIDTH, C_QA // ATT_WIDTH, sv["q_gain"], cos, sin,
                             dqh.reshape(T, ATT_WIDTH))
    dka, dk_gain = _prep_bwd("k_prep_bwd", proj, ATT_KV_WIDTH, C_KA // ATT_KV_WIDTH, sv["k_gain"], cos, sin,
                             dkh.reshape(T, ATT_KV_WIDTH))
    g["attn_q_norm"] = jnp.sum(dq_gain.reshape(ATT_HEADS, ATT_HEAD_DIM), axis=0, keepdims=True)
    g["attn_k_norm"] = jnp.sum(dk_gain.reshape(ATT_KV_HEADS, ATT_HEAD_DIM), axis=0, keepdims=True)

    dproj = jnp.concatenate(
        [dga, dgm, dqk.reshape(T, 2 * MLSTM_WIDTH), dvm, dom, dqa, dka, dva.astype(BF16), dgates], axis=1)
    dwin = _matmul("mix_dwin", sv["h"], dproj, "tn", out_dtype=BF16)
    token = on_dw({"w_in": _w_in_to_slots(dwin), "w_branch_attn": g.pop("w_branch_attn"),
                   "w_branch_mlstm": g.pop("w_branch_mlstm"), "w_out": g.pop("w_out")}, dwin)
    dh2 = _matmul("mix_dh", dproj, p["w_in"], "nt", dep=token)
    dx_new, g["mix_norm"] = _rms_bwd("mix_dnorm", sv["x"], p["mix_norm"], dh2, dx)
    return dx_new, g


def _loss_and_grad(x, g, target):
    T = x.shape[0]

    def loss_fn(xv, gv, tv):
        err = jnp.square(_rms(xv, gv) - tv)
        return 0.5 * jnp.sum(jnp.mean(err, axis=-1, keepdims=True), axis=0, keepdims=True)

    def fn(xv, gv, tv):
        val, vjp = jax.vjp(lambda a, b: loss_fn(a, b, tv), xv, gv)
        dx, dg = vjp(jnp.ones((1, 1), F32))
        return val, dx, dg

    return _rowwise("loss_head", fn, [_In(x), _In(g, rows=False), _In(target)],
                    [_Out(1, rows=False), _Out(D_MODEL), _Out(D_MODEL, rows=False)], T, ROW_BLOCK)


def _block_norm_fwd(x, g):
    T = x.shape[0]
    return _rowwise("block_norm", _rms, [_In(x), _In(g, rows=False)], [_Out(D_MODEL)], T, ROW_BLOCK)[0]


def _block_norm_bwd(x, g, dy):
    T = x.shape[0]

    def fn(xv, gv, dv):
        _, vjp = jax.vjp(_rms, xv, gv)
        return vjp(dv)

    return _rowwise("block_norm_bwd", fn, [_In(x), _In(g, rows=False), _In(dy)],
                    [_Out(D_MODEL), _Out(D_MODEL, rows=False)], T, ROW_BLOCK)


def _qk_perm_cols(t, axis):
    q, k = jnp.split(t, 2, axis=axis)
    parts = []
    for h in range(MLSTM_HEADS):
        sl = [slice(None)] * t.ndim
        sl[axis] = slice(h * MLSTM_HEAD_DIM, (h + 1) * MLSTM_HEAD_DIM)
        parts += [q[tuple(sl)], k[tuple(sl)]]
    return jnp.concatenate(parts, axis=axis)


def _qk_unperm_cols(t, axis):
    qs, ks = [], []
    for h in range(MLSTM_HEADS):
        sl = [slice(None)] * t.ndim
        sl[axis] = slice(2 * h * MLSTM_HEAD_DIM, (2 * h + 1) * MLSTM_HEAD_DIM)
        qs.append(t[tuple(sl)])
        sl[axis] = slice((2 * h + 1) * MLSTM_HEAD_DIM, (2 * h + 2) * MLSTM_HEAD_DIM)
        ks.append(t[tuple(sl)])
    return jnp.concatenate(qs + ks, axis=axis)


def _w_in_arrange(w):
    qa, ka, va, qm, km, vm, om, gm, gmerge = jnp.split(w, np.cumsum(
        (ATT_WIDTH, ATT_KV_WIDTH, ATT_KV_WIDTH, MLSTM_WIDTH, MLSTM_WIDTH, MLSTM_WIDTH, MLSTM_WIDTH, MLSTM_N_GATES))[:].tolist(), axis=1)
    qk = _qk_perm_cols(jnp.concatenate([qm, km], axis=1), 1)
    pad = jnp.zeros((w.shape[0], LANES - MLSTM_N_GATES), w.dtype)
    return jnp.concatenate([gmerge, qk, vm, om, qa, ka, va, gm, pad], axis=1)


def _w_in_restore(w):
    gmerge = w[:, C_GMERGE:C_GMERGE + 2 * D_MODEL]
    qk = _qk_unperm_cols(w[:, C_QK:C_QK + 2 * MLSTM_WIDTH], 1)
    vm, om = w[:, C_VM:C_VM + MLSTM_WIDTH], w[:, C_OM:C_OM + MLSTM_WIDTH]
    qa, ka, va = w[:, C_QA:C_QA + ATT_WIDTH], w[:, C_KA:C_KA + ATT_KV_WIDTH], w[:, C_VA:C_VA + ATT_KV_WIDTH]
    gm = w[:, C_GATES:C_GATES + MLSTM_N_GATES]
    return jnp.concatenate([qa, ka, va, qk, vm, om, gm, gmerge], axis=1)


BIG = ("ffn1_w_gate", "ffn1_w_up", "ffn1_w_down", "w_in", "mlstm_conv_w", "w_branch_attn", "w_branch_mlstm", "w_out",
       "ffn2_w_gate", "ffn2_w_up", "ffn2_w_down")
MATMUL_W = tuple(n for n in BIG if n != "mlstm_conv_w")
SMALL = ("ffn1_norm", "mix_norm", "mlstm_gate_bias", "attn_q_norm", "attn_k_norm", "attn_sink", "mlstm_conv_b",
         "mlstm_out_norm", "ffn2_norm", "block_out_norm")
WEIGHTS = ("ffn1_norm", "ffn1_w_gate", "ffn1_w_up", "ffn1_w_down", "mix_norm", "w_in", "mlstm_gate_bias", "attn_q_norm",
           "attn_k_norm", "attn_sink", "mlstm_conv_w", "mlstm_conv_b", "mlstm_out_norm", "w_branch_attn", "w_branch_mlstm",
           "w_out", "ffn2_norm", "ffn2_w_gate", "ffn2_w_up", "ffn2_w_down", "block_out_norm")
PACK_COLS = 1024


def _padded_rows(n_elems):
    return -(-n_elems // PACK_COLS)


def _pack_flat(arrs, dtype, row_multiple):
    parts = []
    for a in arrs:
        flat = a.reshape(-1).astype(dtype)
        pad = _padded_rows(flat.shape[0]) * PACK_COLS - flat.shape[0]
        parts.append(jnp.pad(flat, (0, pad)) if pad else flat)
    flat = jnp.concatenate(parts)
    rows = flat.shape[0] // PACK_COLS
    extra = (-rows) % row_multiple
    if extra:
        flat = jnp.pad(flat, (0, extra * PACK_COLS))
    return flat.reshape(-1, PACK_COLS)


def _unpack_flat(buf, shapes, lead=()):
    flat = buf.reshape(lead + (-1,))
    out, off = [], 0
    for s in shapes:
        n = int(np.prod(s))
        out.append(flat[..., off:off + n].reshape(lead + tuple(s)))
        off += _padded_rows(n) * PACK_COLS
    return out


class _Lay:
    def __init__(self, shard, axis, width):
        self.shard, self.axis, self.width = shard, axis, width
        self.padded = tuple(width if a == axis else s for a, s in enumerate(shard))
        self.whole = tuple(N_DEV * width if a == axis else s for a, s in enumerate(shard))

    def pad(self, t, lead=0):
        extra = self.width - self.shard[self.axis]
        if not extra:
            return t
        cfg = [(0, 0)] * t.ndim
        cfg[lead + self.axis] = (0, extra)
        return jnp.pad(t, cfg)

    def unpad(self, t, lead=0):
        idx = [slice(None)] * t.ndim
        idx[lead + self.axis] = slice(0, self.shard[self.axis])
        return t[tuple(idx)]


_FF_ROW = _Lay((FF_SHARD, D_MODEL), 0, FF_SHARD_PAD)
TRANSPOSED = ("ffn1_w_gate", "ffn1_w_up", "ffn2_w_gate", "ffn2_w_up")
LAYOUTS = {
    "ffn1_w_gate": _FF_ROW, "ffn1_w_up": _FF_ROW, "ffn1_w_down": _FF_ROW,
    "ffn2_w_gate": _FF_ROW, "ffn2_w_up": _FF_ROW, "ffn2_w_down": _FF_ROW,
    "w_in": _Lay((D_MODEL, IN_WIDTH // N_DEV), 0, D_MODEL),
    "mlstm_conv_w": _Lay((3, 2 * MLSTM_WIDTH // N_DEV), 1, 2 * MLSTM_WIDTH // N_DEV),
    "w_branch_attn": _Lay((ATT_WIDTH, D_MODEL // N_DEV), 1, D_MODEL // N_DEV),
    "w_branch_mlstm": _Lay((MLSTM_WIDTH, D_MODEL // N_DEV), 1, D_MODEL // N_DEV),
    "w_out": _Lay((D_MODEL // N_DEV, D_MODEL), 0, D_MODEL // N_DEV),
}


def _window(ref, axis, j, width):
    idx = [slice(None)] * len(ref.shape)
    idx[axis] = pl.ds(pl.multiple_of(j * width, width), width)
    return ref.at[tuple(idx)]


ANY = pl.BlockSpec(memory_space=pl.ANY)


def _mesh_pos():
    return lax.axis_index("x"), lax.axis_index("y"), lax.axis_index("c")


def _all_gather(name, shard, vmem=False):
    R, C = shard.shape
    space = pl.BlockSpec(memory_space=pltpu.VMEM) if vmem else ANY

    def body(x_ref, out_ref, send_sems, recv_sems, local_sem):
        x, y, c = _mesh_pos()
        me, sibling = (x, y, c), (x, y, 1 - c)
        chips = [(1 - x, y), (x, 1 - y), (1 - x, 1 - y)]

        def slot(px, py, pc):
            return out_ref.at[4 * px + 2 * py + pc]

        def copy(k, block, to, src=None):
            return pltpu.make_async_remote_copy(
                src_ref=slot(*block) if src is None else src, dst_ref=slot(*block),
                send_sem=send_sems.at[k], recv_sem=recv_sems.at[k], device_id=to, device_id_type=MESH)

        mine = pltpu.make_async_copy(x_ref, slot(*me), local_sem)
        mine.start()
        first = [copy(0, me, sibling, src=x_ref)]
        first += [copy(1 + j, me, (*chip, c), src=x_ref) for j, chip in enumerate(chips)]
        for cp in first:
            cp.start()
        passed = [copy(4 + j, (*chip, c), sibling) for j, chip in enumerate(chips)]
        for j, chip in enumerate(chips):
            copy(1 + j, (*chip, c), me).wait_recv()
            passed[j].start()
        copy(0, sibling, me).wait_recv()
        for j, chip in enumerate(chips):
            copy(4 + j, (*chip, 1 - c), me).wait_recv()
        for cp in first + passed:
            cp.wait_send()
        mine.wait()

    return pl.pallas_call(
        body, name=name, out_shape=jax.ShapeDtypeStruct((N_DEV, R, C), shard.dtype),
        in_specs=[space], out_specs=space,
        scratch_shapes=[pltpu.SemaphoreType.DMA((7,)), pltpu.SemaphoreType.DMA((7,)), pltpu.SemaphoreType.DMA],
    )(shard)


HBM = pl.BlockSpec(memory_space=pltpu.HBM)
SEM = pl.BlockSpec(memory_space=pltpu.SEMAPHORE)
SPLIT_COPY = pltpu.CompilerParams(has_side_effects=pltpu.SideEffectType.DATAFLOW_SIDE_EFFECTING)
N_PEERS = N_DEV - 1


def _peers(x, y, c):
    return [(x, y, 1 - c), (1 - x, y, c), (x, 1 - y, c), (1 - x, 1 - y, c),
            (1 - x, y, 1 - c), (x, 1 - y, 1 - c), (1 - x, 1 - y, 1 - c)]


def _dev_index(pos):
    return 4 * pos[0] + 2 * pos[1] + pos[2]


def _place_own(name, shards, lays):
    nt = len(shards)
    me = _dev_index(_mesh_pos())

    def body(me_ref, *refs):
        for x_ref, o_ref in zip(refs[:nt], refs[nt:]):
            o_ref[...] = x_ref[...]

    def window_spec(lay):
        if lay.axis == 0:
            return pl.BlockSpec(lay.padded, lambda i, me_ref: (me_ref[0], 0))
        return pl.BlockSpec(lay.padded, lambda i, me_ref: (0, me_ref[0]))

    return pl.pallas_call(
        body, name=name,
        grid_spec=pltpu.PrefetchScalarGridSpec(
            num_scalar_prefetch=1, grid=(1,),
            in_specs=[pl.BlockSpec(lay.padded, lambda i, me_ref: (0, 0)) for lay in lays],
            out_specs=[window_spec(lay) for lay in lays]),
        out_shape=[jax.ShapeDtypeStruct(lay.whole, s.dtype) for s, lay in zip(shards, lays)],
        compiler_params=_cparams(("arbitrary",)),
    )(me.reshape(1).astype(jnp.int32), *shards)


def _gather_start(name, shards, lands, lays, groups, after):
    nt, ng = len(shards), len(groups)

    def body(*refs):
        x_refs, land_refs = refs[:nt], refs[nt:2 * nt]
        sems = refs[2 * nt + 1:2 * nt + 1 + 2 * ng]
        pos = _mesh_pos()
        me = _dev_index(pos)
        for g, tens in enumerate(groups):
            for i, t in enumerate(tens):
                for k, peer in enumerate(_peers(*pos)):
                    pltpu.make_async_remote_copy(
                        src_ref=x_refs[t], dst_ref=_window(land_refs[t], lays[t].axis, me, lays[t].width),
                        send_sem=sems[2 * g].at[N_PEERS * i + k], recv_sem=sems[2 * g + 1].at[N_PEERS * i + k],
                        device_id=peer, device_id_type=MESH).start()

    sem_shapes = []
    for tens in groups:
        sem_shapes += [pltpu.SemaphoreType.DMA((N_PEERS * len(tens),))] * 2
    thru = [pltpu.HBM(s.shape, s.dtype) for s in shards] + [pltpu.HBM(lay.whole, s.dtype) for s, lay in zip(shards, lays)]
    args = [pltpu.with_memory_space_constraint(s, pltpu.HBM) for s in shards]
    args += [pltpu.with_memory_space_constraint(ld, pltpu.HBM) for ld in lands]
    res = pl.pallas_call(
        body, name=name, out_shape=tuple(sem_shapes + thru), in_specs=[HBM] * (2 * nt) + [ANY],
        out_specs=tuple([SEM] * (2 * ng) + [HBM] * (2 * nt)),
        input_output_aliases={t: 2 * ng + t for t in range(2 * nt)}, compiler_params=SPLIT_COPY,
    )(*args, after)
    sems = [(res[2 * g], res[2 * g + 1]) for g in range(ng)]
    return sems, list(res[2 * ng:2 * ng + nt]), list(res[2 * ng + nt:])


def _gather_wait(name, sems, shards, lands, lays, after):
    nt = len(shards)
    send_sems, recv_sems = sems

    def body(*refs):
        x_refs, land_refs = refs[:nt], refs[nt:2 * nt]
        send_ref, recv_ref = refs[2 * nt], refs[2 * nt + 1]
        pos = _mesh_pos()
        for t in range(nt):
            for k, peer in enumerate(_peers(*pos)):
                cp = pltpu.make_async_remote_copy(
                    src_ref=x_refs[t], dst_ref=_window(land_refs[t], lays[t].axis, _dev_index(peer), lays[t].width),
                    send_sem=send_ref.at[N_PEERS * t + k], recv_sem=recv_ref.at[N_PEERS * t + k],
                    device_id=peer, device_id_type=MESH)
                cp.wait_send()
                cp.wait_recv()

    thru = [pltpu.HBM(s.shape, s.dtype) for s in shards] + [pltpu.HBM(ld.shape, ld.dtype) for ld in lands]
    res = pl.pallas_call(
        body, name=name, out_shape=tuple(thru), in_specs=[HBM] * (2 * nt) + [SEM, SEM, ANY],
        out_specs=tuple([HBM] * (2 * nt)), input_output_aliases={t: t for t in range(2 * nt)},
        compiler_params=SPLIT_COPY,
    )(*shards, *lands, send_sems, recv_sems, after)
    return list(res[nt:])


def _pair_exchange(name, grads, lays):
    nt = len(grads)

    def body(*refs):
        g_refs, land_refs = refs[:nt], refs[nt:2 * nt]
        send_sems, recv_sems = refs[2 * nt:]
        x, y, c = _mesh_pos()
        copies = []
        for t in range(nt):
            for chip in range(4):
                copies.append(pltpu.make_async_remote_copy(
                    src_ref=_window(g_refs[t], lays[t].axis, 2 * chip + (1 - c), lays[t].width), dst_ref=land_refs[t].at[chip],
                    send_sem=send_sems.at[4 * t + chip], recv_sem=recv_sems.at[4 * t + chip],
                    device_id=(x, y, 1 - c), device_id_type=MESH))
        for cp in copies:
            cp.start()
        for cp in copies:
            cp.wait_recv()
        for cp in copies:
            cp.wait_send()

    out_shape = [jax.ShapeDtypeStruct((4,) + lay.padded, g.dtype) for g, lay in zip(grads, lays)]
    return pl.pallas_call(
        body, name=name, out_shape=out_shape, in_specs=[ANY] * nt, out_specs=[ANY] * nt,
        scratch_shapes=[pltpu.SemaphoreType.DMA((4 * nt,)), pltpu.SemaphoreType.DMA((4 * nt,))],
    )(*grads)


def _pair_sum(name, whole, landed, lay, out_dtype):
    R, C = lay.padded
    br = _first_divisor(R, (512, 384, 256, 128, 64, 32, 16, 8))
    nb = R // br
    if lay.axis == 0:
        mine_spec = pl.BlockSpec((br, C), lambda k, i, c_ref: ((2 * k + c_ref[0]) * nb + i, 0))
    else:
        mine_spec = pl.BlockSpec((br, C), lambda k, i, c_ref: (i, 2 * k + c_ref[0]))

    def body(c_ref, mine_ref, sib_ref, o_ref):
        o_ref[0] = (mine_ref[...].astype(F32) + sib_ref[0].astype(F32)).astype(out_dtype)

    c = lax.axis_index("c")
    return pl.pallas_call(
        body, name=name,
        grid_spec=pltpu.PrefetchScalarGridSpec(
            num_scalar_prefetch=1, grid=(4, nb),
            in_specs=[mine_spec, pl.BlockSpec((1, br, C), lambda k, i, c_ref: (k, i, 0))],
            out_specs=pl.BlockSpec((1, br, C), lambda k, i, c_ref: (k, i, 0))),
        out_shape=jax.ShapeDtypeStruct((4, R, C), out_dtype),
        compiler_params=_cparams(("parallel", "parallel")),
    )(c.reshape(1).astype(jnp.int32), whole, landed)


def _chip_exchange(name, sums):
    nt = len(sums)

    def body(*refs):
        s_refs, land_refs = refs[:nt], refs[nt:2 * nt]
        send_sems, recv_sems, local_sems = refs[2 * nt:]
        x, y, c = _mesh_pos()
        my_chip = 2 * x + y
        mine = [pltpu.make_async_copy(s_refs[t].at[my_chip], land_refs[t].at[my_chip], local_sems.at[t]) for t in range(nt)]
        for cp in mine:
            cp.start()
        chips = [(1 - x, y), (x, 1 - y), (1 - x, 1 - y)]
        copies = []
        for t in range(nt):
            for j, (px, py) in enumerate(chips):
                copies.append(pltpu.make_async_remote_copy(
                    src_ref=s_refs[t].at[2 * px + py], dst_ref=land_refs[t].at[my_chip],
                    send_sem=send_sems.at[3 * t + j], recv_sem=recv_sems.at[3 * t + j],
                    device_id=(px, py, c), device_id_type=MESH))
        for cp in copies:
            cp.start()
        for t in range(nt):
            for j, (px, py) in enumerate(chips):
                pltpu.make_async_remote_copy(
                    src_ref=s_refs[t].at[my_chip], dst_ref=land_refs[t].at[2 * px + py],
                    send_sem=send_sems.at[3 * t + j], recv_sem=recv_sems.at[3 * t + j],
                    device_id=(px, py, c), device_id_type=MESH).wait_recv()
        for cp in copies:
            cp.wait_send()
        for cp in mine:
            cp.wait()

    return pl.pallas_call(
        body, name=name, out_shape=[jax.ShapeDtypeStruct(s.shape, s.dtype) for s in sums],
        in_specs=[ANY] * nt, out_specs=[ANY] * nt,
        scratch_shapes=[pltpu.SemaphoreType.DMA((3 * nt,)), pltpu.SemaphoreType.DMA((3 * nt,)), pltpu.SemaphoreType.DMA((nt,))],
    )(*sums)


def _chip_start(name, sums):
    nt = len(sums)

    def body(*refs):
        s_refs, land_refs = refs[:nt], refs[nt:2 * nt]
        send_sems, recv_sems = refs[2 * nt], refs[2 * nt + 1]
        x, y, c = _mesh_pos()
        my_chip = 2 * x + y
        for t in range(nt):
            for j, (px, py) in enumerate([(1 - x, y), (x, 1 - y), (1 - x, 1 - y)]):
                pltpu.make_async_remote_copy(
                    src_ref=s_refs[t].at[2 * px + py], dst_ref=land_refs[t].at[my_chip],
                    send_sem=send_sems.at[3 * t + j], recv_sem=recv_sems.at[3 * t + j],
                    device_id=(px, py, c), device_id_type=MESH).start()

    thru = [pltpu.HBM(s.shape, s.dtype) for s in sums] * 2
    args = [pltpu.with_memory_space_constraint(s, pltpu.HBM) for s in sums]
    args += [pltpu.with_memory_space_constraint(lax.empty(s.shape, s.dtype), pltpu.HBM) for s in sums]
    res = pl.pallas_call(
        body, name=name, out_shape=tuple([pltpu.SemaphoreType.DMA((3 * nt,))] * 2 + thru), in_specs=[HBM] * (2 * nt),
        out_specs=tuple([SEM, SEM] + [HBM] * (2 * nt)), input_output_aliases={t: 2 + t for t in range(2 * nt)},
        compiler_params=SPLIT_COPY,
    )(*args)
    return (res[0], res[1]), list(res[2:2 + nt]), list(res[2 + nt:])


def _chip_wait(name, sems, sums, lands, after):
    nt = len(sums)

    def body(*refs):
        s_refs, land_refs = refs[:nt], refs[nt:2 * nt]
        send_sems, recv_sems = refs[2 * nt], refs[2 * nt + 1]
        x, y, c = _mesh_pos()
        my_chip = 2 * x + y
        for t in range(nt):
            for j, (px, py) in enumerate([(1 - x, y), (x, 1 - y), (1 - x, 1 - y)]):
                cp = pltpu.make_async_remote_copy(
                    src_ref=s_refs[t].at[my_chip], dst_ref=land_refs[t].at[2 * px + py],
                    send_sem=send_sems.at[3 * t + j], recv_sem=recv_sems.at[3 * t + j],
                    device_id=(px, py, c), device_id_type=MESH)
                cp.wait_send()
                cp.wait_recv()

    thru = [pltpu.HBM(s.shape, s.dtype) for s in sums] * 2
    res = pl.pallas_call(
        body, name=name, out_shape=tuple(thru), in_specs=[HBM] * (2 * nt) + [SEM, SEM, ANY],
        out_specs=tuple([HBM] * (2 * nt)), input_output_aliases={t: t for t in range(2 * nt)},
        compiler_params=SPLIT_COPY,
    )(*sums, *lands, sems[0], sems[1], after)
    return list(res[:nt]), list(res[nt:])


def _sum_chips(name, own, landed):
    _, R, C = own.shape
    br = _first_divisor(R, (512, 384, 256, 128, 64, 32, 16, 8))
    x, y, _ = _mesh_pos()
    slots = jnp.stack([2 * x + y, 2 * (1 - x) + y, 2 * x + (1 - y), 2 * (1 - x) + (1 - y)]).astype(jnp.int32)

    def body(slot_ref, mine_ref, a_ref, b_ref, c_ref, o_ref):
        o_ref[...] = ((mine_ref[0].astype(F32) + a_ref[0].astype(F32)) + b_ref[0].astype(F32)) + c_ref[0].astype(F32)

    def slot_spec(j):
        return pl.BlockSpec((1, br, C), lambda i, slot_ref: (slot_ref[j], i, 0))

    return pl.pallas_call(
        body, name=name,
        grid_spec=pltpu.PrefetchScalarGridSpec(
            num_scalar_prefetch=1, grid=(R // br,), in_specs=[slot_spec(0), slot_spec(1), slot_spec(2), slot_spec(3)],
            out_specs=pl.BlockSpec((br, C), lambda i, slot_ref: (i, 0))),
        out_shape=jax.ShapeDtypeStruct((R, C), F32), compiler_params=_cparams(("parallel",)),
    )(slots, own, landed, landed, landed)


def _sum_slots(name, slots, n):
    _, R, C = slots.shape
    br = _first_divisor(R, (512, 384, 256, 128, 64, 32, 16, 8))

    def body(s_ref, o_ref):
        acc = s_ref[0].astype(F32)
        for k in range(1, n):
            acc = acc + s_ref[k].astype(F32)
        o_ref[...] = acc

    return pl.pallas_call(
        body, name=name, grid=(R // br,), in_specs=[pl.BlockSpec((n, br, C), lambda i: (0, i, 0))],
        out_specs=pl.BlockSpec((br, C), lambda i: (i, 0)), out_shape=jax.ShapeDtypeStruct((R, C), F32),
        compiler_params=_cparams(("parallel",)),
    )(slots)


def _reduce_scatter_start(tag, names, grads):
    lays = [LAYOUTS[n] for n in names]
    landed = _pair_exchange("grads_pair_" + names[0], grads, lays)
    sums = [_pair_sum("grads_pairsum_" + n, g, ld, lay, BF16) for n, g, ld, lay in zip(names, grads, landed, lays)]
    sems, sums, lands = _chip_start(tag + "_chips_start", sums)
    return tag, names, sems, sums, lands


def _reduce_scatter_finish(pending, after):
    tag, names, sems, sums, lands = pending
    own, got = _chip_wait(tag + "_chips_wait", sems, sums, lands, after)
    return [_sum_chips("grads_sum_" + n, o, s) for n, o, s in zip(names, own, got)]


def _adamw_math(w, g, m, v):
    m = ADAM_B1 * m + (1.0 - ADAM_B1) * g
    v = ADAM_B2 * v + (1.0 - ADAM_B2) * jnp.square(g)
    m_hat = m / (1.0 - ADAM_B1 ** ADAM_STEP)
    v_hat = v / (1.0 - ADAM_B2 ** ADAM_STEP)
    delta = -ADAM_LR * (m_hat / (jnp.sqrt(v_hat) + ADAM_EPS) + ADAM_WD * w)
    return delta, m, v


def _adamw_layers(name, w, totals, m, v):
    _, R, C = w.shape
    br = _first_divisor(R, (512, 176, 128, 64, 32, 16, 8))
    Cp = totals[0].shape[1]

    def body(w_ref, g0_ref, g1_ref, m_ref, v_ref, g_out, d_out, m_out, v_out):
        g = jnp.where(pl.program_id(0) == 0, g0_ref[:, 0:C], g1_ref[:, 0:C])
        delta, m_new, v_new = _adamw_math(w_ref[0], g, m_ref[0], v_ref[0])
        g_out[0], d_out[0], m_out[0], v_out[0] = g, delta, m_new, v_new

    blk = pl.BlockSpec((1, br, C), lambda l, i: (l, i, 0))
    g_spec = pl.BlockSpec((br, Cp), lambda l, i: (i, 0))
    return pl.pallas_call(
        body, name=name, grid=(DEPTH, R // br), in_specs=[blk, g_spec, g_spec, blk, blk], out_specs=[blk] * 4,
        out_shape=[jax.ShapeDtypeStruct(w.shape, F32)] * 4, compiler_params=_cparams(("parallel", "parallel")),
    )(w, totals[0], totals[1], m, v)


def _adamw(name, w, g, m, v):
    shape = w.shape
    cols = shape[-1]
    rows = int(np.prod(shape[:-1]))
    br = _first_divisor(rows, (512, 352, 256, 128, 64, 32, 16, 8))
    args = [_In(a.reshape(rows, cols)) for a in (w, g, m, v)]
    outs = _rowwise(name, _adamw_math, args, [_Out(cols), _Out(cols), _Out(cols)], rows, br)
    return [o.reshape(shape) for o in outs]


GROUPS = {"ffn1": ("ffn1_w_gate", "ffn1_w_up", "ffn1_w_down"),
          "mix": ("w_in", "w_branch_attn", "w_branch_mlstm", "w_out"),
          "ffn2": ("ffn2_w_gate", "ffn2_w_up", "ffn2_w_down")}
GATHER_GROUPS = {"ffn1_in": ("ffn1_w_gate", "ffn1_w_up"), "ffn1_out": ("ffn1_w_down",),
                 "mix": ("w_in", "w_branch_attn", "w_branch_mlstm", "w_out"),
                 "ffn2_in": ("ffn2_w_gate", "ffn2_w_up"), "ffn2_out": ("ffn2_w_down",)}


def _small_params(small, conv_w, l):
    p = {}
    for n in ("ffn1_norm", "mix_norm", "ffn2_norm", "block_out_norm", "mlstm_out_norm", "attn_q_norm", "attn_k_norm"):
        p[n] = small[n][l][None, :]
    p["attn_sink"] = small["attn_sink"][l]
    p["gate_bias"] = jnp.pad(small["mlstm_gate_bias"][l], (0, LANES - MLSTM_N_GATES))[None, :]
    taps = _qk_perm_cols(conv_w[l], 1)
    conv_b = _qk_perm_cols(small["mlstm_conv_b"][l][None, :], 1)
    p["conv_w8"] = jnp.concatenate([taps, conv_b, jnp.zeros((4, 2 * MLSTM_WIDTH), F32)], axis=0)
    return p


def _w_in_from_slots(slots):
    w_in = slots.reshape(N_DEV, D_MODEL, IN_WIDTH // N_DEV).transpose(1, 0, 2).reshape(D_MODEL, IN_WIDTH)
    return _w_in_arrange(w_in)


def _w_in_to_slots(g):
    return _w_in_restore(g).reshape(D_MODEL, N_DEV, IN_WIDTH // N_DEV).transpose(1, 0, 2).reshape(
        N_DEV * D_MODEL, IN_WIDTH // N_DEV)


def _local_step(x, positions, target, weights_of, small, conv_w, on_grads):
    B, S, _ = x.shape
    T = B * S
    cos, sin = _rope_cos_sin(positions.reshape(T, 1))
    params = [_small_params(small, conv_w, l) for l in range(DEPTH)]
    xs = x.reshape(T, D_MODEL)
    tgt = target.reshape(T, D_MODEL)

    saved = []
    for l, p in enumerate(params):
        p.update(weights_of(l, "ffn1_in", xs))
        x1, s1, p["ffn1_w_down"] = _ffn_fwd("ffn1", xs, p["ffn1_norm"], p["ffn1_w_gate"], p["ffn1_w_up"],
                                            lambda after, l=l: weights_of(l, "ffn1_out", after)["ffn1_w_down"])
        p.update(weights_of(l, "mix", x1))
        p["w_in"] = _w_in_from_slots(p["w_in"])
        x2, s2 = _mix_fwd(x1, cos, sin, B, S, p)
        p.update(weights_of(l, "ffn2_in", x2))
        x3, s3, p["ffn2_w_down"] = _ffn_fwd("ffn2", x2, p["ffn2_norm"], p["ffn2_w_gate"], p["ffn2_w_up"],
                                            lambda after, l=l: weights_of(l, "ffn2_out", after)["ffn2_w_down"])
        saved.append((s1, s2, s3, x3))
        if l + 1 < DEPTH:
            xs = _block_norm_fwd(x3, p["block_out_norm"])

    sm = {n: [None] * DEPTH for n in SMALL + ("mlstm_conv_w",)}
    loss = None
    dx = None
    for l in reversed(range(DEPTH)):
        p = params[l]
        s1, s2, s3, x3 = saved[l]
        if l == DEPTH - 1:
            loss, dx, dgn = _loss_and_grad(x3, p["block_out_norm"], tgt)
        else:
            dx, dgn = _block_norm_bwd(x3, p["block_out_norm"], dx)
        sm["block_out_norm"][l] = dgn[0]
        dx, dg = _ffn_bwd("ffn2", s3, p["ffn2_norm"], p["ffn2_w_gate"], p["ffn2_w_up"], p["ffn2_w_down"], dx,
                          functools.partial(on_grads, l, "ffn2"))
        sm["ffn2_norm"][l] = dg[0]
        dx, g = _mix_bwd(s2, cos, sin, B, S, p, dx, functools.partial(on_grads, l, "mix"))
        dconv = _qk_unperm_cols(g["conv_w8"], 1)
        sm["mlstm_conv_w"][l] = dconv[0:3]
        sm["mlstm_conv_b"][l] = dconv[3]
        sm["mix_norm"][l] = g["mix_norm"][0]
        sm["mlstm_gate_bias"][l] = g["gate_bias"][0, :MLSTM_N_GATES]
        sm["attn_q_norm"][l], sm["attn_k_norm"][l] = g["attn_q_norm"][0], g["attn_k_norm"][0]
        sm["attn_sink"][l] = g["attn_sink"][0]
        sm["mlstm_out_norm"][l] = g["mlstm_out_norm"][0]
        dx, dg = _ffn_bwd("ffn1", s1, p["ffn1_norm"], p["ffn1_w_gate"], p["ffn1_w_up"], p["ffn1_w_down"], dx,
                          functools.partial(on_grads, l, "ffn1"))
        sm["ffn1_norm"][l] = dg[0]
    sm = {n: jnp.stack(v, axis=0) for n, v in sm.items()}
    return loss, dx.reshape(B, S, D_MODEL), sm


def kernel(x, positions, ffn1_norm, ffn1_w_gate, ffn1_w_up, ffn1_w_down, mix_norm, w_in, mlstm_gate_bias, attn_q_norm, attn_k_norm, attn_sink, mlstm_conv_w, mlstm_conv_b, mlstm_out_norm, w_branch_attn, w_branch_mlstm, w_out, ffn2_norm, ffn2_w_gate, ffn2_w_up, ffn2_w_down, block_out_norm, loss_target, m_ffn1_norm, m_ffn1_w_gate, m_ffn1_w_up, m_ffn1_w_down, m_mix_norm, m_w_in, m_mlstm_gate_bias, m_attn_q_norm, m_attn_k_norm, m_attn_sink, m_mlstm_conv_w, m_mlstm_conv_b, m_mlstm_out_norm, m_w_branch_attn, m_w_branch_mlstm, m_w_out, m_ffn2_norm, m_ffn2_w_gate, m_ffn2_w_up, m_ffn2_w_down, m_block_out_norm, v_ffn1_norm, v_ffn1_w_gate, v_ffn1_w_up, v_ffn1_w_down, v_mix_norm, v_w_in, v_mlstm_gate_bias, v_attn_q_norm, v_attn_k_norm, v_attn_sink, v_mlstm_conv_w, v_mlstm_conv_b, v_mlstm_out_norm, v_w_branch_attn, v_w_branch_mlstm, v_w_out, v_ffn2_norm, v_ffn2_w_gate, v_ffn2_w_up, v_ffn2_w_down, v_block_out_norm):
    args = locals()
    def stored(n, t):
        return t.transpose(0, 2, 1) if n in TRANSPOSED else t

    w = {n: stored(n, args[n]) for n in WEIGHTS}
    m = {n: stored(n, args["m_" + n]) for n in WEIGHTS}
    v = {n: stored(n, args["v_" + n]) for n in WEIGHTS}

    order = [(l, grp) for l in range(DEPTH) for grp in GATHER_GROUPS]
    keys = [(l, n) for l, grp in order for n in GATHER_GROUPS[grp]]
    lays = [LAYOUTS[n] for _, n in keys]
    shards = [lay.pad(w[n][l].astype(BF16)) for (l, n), lay in zip(keys, lays)]
    group_idx, at = {}, 0
    for l, grp in order:
        group_idx[(l, grp)] = list(range(at, at + len(GATHER_GROUPS[grp])))
        at += len(GATHER_GROUPS[grp])
    conv_shape = w["mlstm_conv_w"].shape
    conv_all = _all_gather("conv_all_gather", _pack_flat([w["mlstm_conv_w"]], F32, 8), vmem=True)
    conv_parts = _unpack_flat(conv_all, [conv_shape], lead=(N_DEV,))[0]
    conv_w = jnp.concatenate([conv_parts[j] for j in range(N_DEV)], axis=2)
    small = {n: w[n] for n in SMALL}

    lands = []
    for l, grp in order:
        idx = group_idx[(l, grp)]
        lands += _place_own("weights_place_" + grp, [shards[i] for i in idx], [lays[i] for i in idx])
    sems, shards, lands = _gather_start("weights_gather_start", shards, lands, lays, [group_idx[k] for k in order], conv_all)

    def weights_of(l, grp, after):
        idx = group_idx[(l, grp)]
        whole = _gather_wait(f"weights_gather_wait_{l}_{grp}", sems[order.index((l, grp))], [shards[i] for i in idx],
                             [lands[i] for i in idx], [lays[i] for i in idx], after)
        return dict(zip(GATHER_GROUPS[grp], whole))

    totals, pending = {}, []

    def finish(after):
        tag, names = pending[0][0], pending[0][1]
        for n, t in zip(names, _reduce_scatter_finish(pending.pop(0), after)):
            totals[(tag, n)] = t

    def on_grads(l, grp, g, after):
        if pending:
            finish(after)
        names = GROUPS[grp]
        pending.append(_reduce_scatter_start(f"grads_{l}_{grp}", names, [g[n] for n in names]))
        return pending[-1][3][0]

    loss, grad_x, small_g = _local_step(x, positions, loss_target, weights_of, small, conv_w, on_grads)
    finish(grad_x)
    grads, deltas, new_m, new_v = {}, {}, {}, {}
    for grp, names in GROUPS.items():
        for n in names:
            grads[n], deltas[n], new_m[n], new_v[n] = _adamw_layers(
                "adamw_" + n, w[n], [totals[(f"grads_{l}_{grp}", n)] for l in range(DEPTH)], m[n], v[n])

    small_names = SMALL + ("mlstm_conv_w",)
    small_shapes = [small_g[n].shape for n in small_names] + [(1, 1)]
    small_packed = _pack_flat([small_g[n] for n in small_names] + [loss], F32, 8)
    small_all = _all_gather("small_all_gather", small_packed, vmem=True)
    small_sum = _sum_slots("small_sum", small_all, N_DEV)
    *small_grads, loss_total = _unpack_flat(small_sum, small_shapes)
    grads.update(dict(zip(small_names, small_grads)))
    x_pos, y_pos, c_pos = _mesh_pos()
    grads["mlstm_conv_w"] = lax.dynamic_slice_in_dim(
        grads["mlstm_conv_w"], (4 * x_pos + 2 * y_pos + c_pos) * conv_shape[2], conv_shape[2], axis=2)

    n = "mlstm_conv_w"
    deltas[n], new_m[n], new_v[n] = _adamw("adamw_" + n, w[n], grads[n], m[n], v[n])
    sw, sg, smm, sv = (_pack_flat([d[n] for n in SMALL], F32, 8) for d in (w, grads, m, v))
    sd, snm, snv = _adamw("adamw_small", sw, sg, smm, sv)
    shapes = [w[n].shape for n in SMALL]
    for d, buf in ((deltas, sd), (new_m, snm), (new_v, snv)):
        d.update(dict(zip(SMALL, _unpack_flat(buf, shapes))))

    return (loss_total.reshape(()), grad_x, *[stored(n, d[n]) for d in (grads, deltas, new_m, new_v) for n in WEIGHTS])
```

```python
import functools

import numpy as np
import jax
import jax.numpy as jnp
from jax import lax
from jax.experimental import pallas as pl
from jax.experimental.pallas import tpu as pltpu

F32 = jnp.float32
BF16 = jnp.bfloat16

D_MODEL = 1024
D_FF = 2816
ATT_HEAD_DIM = 64
ATT_HEADS = 8
ATT_KV_HEADS = 2
ATT_GROUP = ATT_HEADS // ATT_KV_HEADS
ATT_WIDTH = ATT_HEADS * ATT_HEAD_DIM
ATT_KV_WIDTH = ATT_KV_HEADS * ATT_HEAD_DIM
WINDOW = 128
ATT_BLOCK = 128
ROPE_DIM = 16
ROPE_THETA = 500000.0
MLSTM_HEADS = 4
MLSTM_HEAD_DIM = 128
MLSTM_WIDTH = MLSTM_HEADS * MLSTM_HEAD_DIM
MLSTM_CHUNK = 128
MLSTM_N_GATES = 4 * MLSTM_HEADS
NORM_EPS = 1e-6
IN_WIDTH = 4880
DEPTH = 2
N_DEV = 8

ADAM_LR = 0.001
ADAM_B1 = 0.9
ADAM_B2 = 0.999
ADAM_EPS = 1e-08
ADAM_WD = 0.01
ADAM_STEP = 10

LANES = 128
C_GMERGE = 0
C_QK = 2048
C_VM = 3072
C_OM = 3584
C_QA = 4096
C_KA = 4608
C_VA = 4736
C_GATES = 4864
IN_PAD = 4992

VMEM_LIMIT = 48 * 1024 * 1024

MESH = pl.DeviceIdType.MESH


def _cparams(sem):
    return pltpu.CompilerParams(dimension_semantics=sem, vmem_limit_bytes=VMEM_LIMIT)


def _first_divisor(n, cands):
    for c in cands:
        if n % c == 0:
            return c
    return n


_NN = ((1,), (0,))
_NT = ((1,), (1,))
_TN = ((0,), (0,))


def _mm(a, b, dims):
    return lax.dot_general(a.astype(BF16), b.astype(BF16), (dims, ((), ())), preferred_element_type=F32)


@jax.custom_vjp
def mm_nn(a, b):
    return _mm(a, b, _NN)


def _mm_nn_fwd(a, b):
    return _mm(a, b, _NN), (a, b)


def _mm_nn_bwd(res, g):
    a, b = res
    return _mm(g, b, _NT).astype(a.dtype), _mm(a, g, _TN).astype(b.dtype)


mm_nn.defvjp(_mm_nn_fwd, _mm_nn_bwd)


@jax.custom_vjp
def mm_nt(a, b):
    return _mm(a, b, _NT)


def _mm_nt_fwd(a, b):
    return _mm(a, b, _NT), (a, b)


def _mm_nt_bwd(res, g):
    a, b = res
    return _mm(g, b, _NN).astype(a.dtype), _mm(g, a, _TN).astype(b.dtype)


mm_nt.defvjp(_mm_nt_fwd, _mm_nt_bwd)


@jax.custom_vjp
def mm_tn(a, b):
    return _mm(a, b, _TN)


def _mm_tn_fwd(a, b):
    return _mm(a, b, _TN), (a, b)


def _mm_tn_bwd(res, g):
    a, b = res
    return _mm(b, g, _NT).astype(a.dtype), _mm(a, g, _NN).astype(b.dtype)


mm_tn.defvjp(_mm_tn_fwd, _mm_tn_bwd)


def _matmul(name, a, b, mode, out_dtype=F32, res=None, scale=1.0, bl=None, dep=None):
    b_shape = b.shape if bl is None else b.shape[1:]
    if mode == "nn":
        (M, K), (K2, N) = a.shape, b_shape
    elif mode == "nt":
        (M, K), (N, K2) = a.shape, b_shape
    else:
        (K, M), (K2, N) = a.shape, b_shape
    assert K == K2, (name, a.shape, b.shape)
    tm = _first_divisor(M, (1024, 512, 384, 256, 128))
    tn = _first_divisor(N, (1024, 1664, 512, 384, 256, 128))
    tk = _first_divisor(K, (1024, 1664, 512, 256, 128))
    nk = K // tk
    if mode == "tn":
        a_spec = pl.BlockSpec((tk, tm), lambda i, j, k: (k, i))
    else:
        a_spec = pl.BlockSpec((tm, tk), lambda i, j, k: (i, k))
    if mode == "nt":
        b_blk, b_idx = (tn, tk), (lambda i, j, k: (j, k))
    else:
        b_blk, b_idx = (tk, tn), (lambda i, j, k: (k, j))
    if bl is None:
        b_spec = pl.BlockSpec(b_blk, b_idx)
    else:
        b_spec = pl.BlockSpec((None,) + b_blk, lambda i, j, k: (bl,) + b_idx(i, j, k))
    o_spec = pl.BlockSpec((tm, tn), lambda i, j, k: (i, j))
    dims = {"nn": _NN, "nt": _NT, "tn": _TN}[mode]
    has_res = res is not None

    def body(*refs):
        a_ref, b_ref = refs[:2]
        r_ref = refs[2] if has_res else None

        def finish(out):
            if scale != 1.0:
                out = out * scale
            if has_res:
                out = r_ref[...].astype(F32) + out
            o_ref[...] = out.astype(out_dtype)

        if nk == 1:
            o_ref = refs[-1]
            finish(_mm(a_ref[...], b_ref[...], dims))
            return
        o_ref, acc = refs[-2:]
        k = pl.program_id(2)

        @pl.when(k == 0)
        def _():
            acc[...] = jnp.zeros_like(acc)

        acc[...] += _mm(a_ref[...], b_ref[...], dims)

        @pl.when(k == nk - 1)
        def _():
            finish(acc[...])

    in_specs = [a_spec, b_spec] + ([o_spec] if has_res else [])
    args = (a, b) + ((res,) if has_res else ())
    if dep is not None:
        in_specs.append(pl.BlockSpec(memory_space=pl.ANY))
        args += (dep,)
    return pl.pallas_call(
        body, name=name, grid=(M // tm, N // tn, nk), in_specs=in_specs, out_specs=o_spec,
        out_shape=jax.ShapeDtypeStruct((M, N), out_dtype),
        scratch_shapes=[pltpu.VMEM((tm, tn), F32)] if nk > 1 else [],
        compiler_params=_cparams(("parallel", "parallel", "arbitrary")),
    )(*args)


class _In:
    def __init__(self, arr, width=None, base=0, split=False, rows=True):
        self.arr, self.base, self.split, self.rows = arr, base, split, rows
        self.width = arr.shape[1] if width is None else width


class _Out:
    def __init__(self, cols, dtype=F32, width=None, split=False, rows=True, nrows=1):
        self.cols, self.dtype, self.split, self.rows, self.nrows = cols, dtype, split, rows, nrows
        self.width = cols if width is None else width


def _rowwise(name, fn, ins, outs, n_rows, br, ncol=1):
    br = min(br, n_rows)
    assert n_rows % br == 0, (name, n_rows, br)
    nrow_blocks = n_rows // br

    def in_spec(d):
        nb = br if d.rows else d.arr.shape[0]
        if d.rows and d.split:
            im = lambda j, i, base=d.base: (i, base + j)
        elif d.rows:
            im = lambda j, i, base=d.base: (i, base)
        elif d.split:
            im = lambda j, i, base=d.base: (0, base + j)
        else:
            im = lambda j, i, base=d.base: (0, base)
        return pl.BlockSpec((nb, d.width), im)

    def out_spec(d):
        nb = br if d.rows else d.nrows
        if d.rows and d.split:
            im = lambda j, i: (i, j)
        elif d.rows:
            im = lambda j, i: (i, 0)
        elif d.split:
            im = lambda j, i: (0, j)
        else:
            im = lambda j, i: (0, 0)
        return pl.BlockSpec((nb, d.width), im)

    n_in = len(ins)

    def body(*refs):
        i = pl.program_id(1)
        vals = [r[...] for r in refs[:n_in]]
        res = fn(*vals)
        if not isinstance(res, (tuple, list)):
            res = (res,)
        for d, ref, val in zip(outs, refs[n_in:], res):
            if d.rows:
                ref[...] = val.astype(d.dtype)
            else:
                @pl.when(i == 0)
                def _(ref=ref):
                    ref[...] = jnp.zeros_like(ref)

                ref[...] += val.astype(d.dtype)

    out_shape = [jax.ShapeDtypeStruct((n_rows if d.rows else d.nrows, d.cols), d.dtype) for d in outs]
    res = pl.pallas_call(
        body, name=name, grid=(ncol, nrow_blocks), in_specs=[in_spec(d) for d in ins],
        out_specs=[out_spec(d) for d in outs], out_shape=out_shape,
        compiler_params=_cparams(("parallel", "arbitrary")),
    )(*[d.arr for d in ins])
    return res


def _rms(x, g):
    return x * lax.rsqrt(jnp.mean(x * x, axis=-1, keepdims=True) + NORM_EPS) * g


def _sigmoid(x):
    return 0.5 * jnp.tanh(0.5 * x) + 0.5


def _silu(x):
    return x * _sigmoid(x)


def _log_sigmoid(x):
    return jnp.minimum(x, 0.0) - jnp.log(1.0 + jnp.exp(-jnp.abs(x)))


def _rope_tables(pos, inv_freq_row):
    ang = pos.astype(F32) * inv_freq_row
    return jnp.cos(ang), jnp.sin(ang)


def _head_sums_impl(v):
    w = v.shape[-1]
    shift = ATT_HEAD_DIM.bit_length() - 1
    r = lax.shift_right_logical(lax.broadcasted_iota(jnp.int32, (w, w), 0), shift)
    c = lax.shift_right_logical(lax.broadcasted_iota(jnp.int32, (w, w), 1), shift)
    ones = (r == c).astype(BF16)
    hi = v.astype(BF16)
    lo = (v - hi.astype(F32)).astype(BF16)
    dn = (_NN, ((), ()))
    return (lax.dot_general(hi, ones, dn, preferred_element_type=F32)
            + lax.dot_general(lo, ones, dn, preferred_element_type=F32))


@jax.custom_vjp
def _head_sums(v):
    return _head_sums_impl(v)


_head_sums.defvjp(lambda v: (_head_sums_impl(v), None), lambda _, g: (_head_sums_impl(g),))


def _rotate_half_impl(y):
    w = y.shape[-1]
    half = ROPE_DIM // 2
    lane = lax.broadcasted_iota(jnp.int32, y.shape, 1) & (ATT_HEAD_DIM - 1)
    above = pltpu.roll(y, w - half, axis=1)
    below = pltpu.roll(y, half, axis=1)
    return jnp.where(lane < half, -above, jnp.where(lane < ROPE_DIM, below, 0.0))


@jax.custom_vjp
def _rotate_half(y):
    return _rotate_half_impl(y)


_rotate_half.defvjp(lambda y: (_rotate_half_impl(y), None), lambda _, g: (-_rotate_half_impl(g),))


def _qk_prep(t, g, cos, sin):
    reps = t.shape[-1] // cos.shape[-1]
    if reps > 1:
        cos, sin = jnp.tile(cos, (1, reps)), jnp.tile(sin, (1, reps))
    y = t * lax.rsqrt(_head_sums(t * t) * (1.0 / ATT_HEAD_DIM) + NORM_EPS) * g
    return y * cos + _rotate_half(y) * sin


def _attn_head(q, kb, vb, sink, valid):
    s = mm_nt(q, kb) * (ATT_HEAD_DIM ** -0.5)
    s = jnp.where(valid, s, -jnp.inf)
    m = jnp.maximum(jnp.max(s, axis=-1, keepdims=True), sink)
    p = jnp.exp(s - m)
    den = jnp.sum(p, axis=-1, keepdims=True) + jnp.exp(sink - m)
    return mm_nn(p * (1.0 / den), vb)


def _mlstm_chunk(q, k, v, li, lf, C, n, m, incl, incl_t, eye):
    k = k * (MLSTM_HEAD_DIM ** -0.5)
    lf_row = jnp.sum(eye * lf, axis=0, keepdims=True)
    li_row = jnp.sum(eye * li, axis=0, keepdims=True)
    b = jnp.sum(incl * lf_row, axis=1, keepdims=True)
    b_row = jnp.sum(incl_t * lf, axis=0, keepdims=True)
    b_tot = jnp.sum(lf, axis=0, keepdims=True)
    a = b_tot - b + li
    a_max = jnp.max(a, axis=0, keepdims=True)
    kw = k * jnp.exp(a - a_max)
    c_loc = mm_tn(kw, v)
    n_loc = jnp.sum(kw, axis=0, keepdims=True)

    dmat = jnp.where(incl > 0.5, b - b_row + li_row, -jnp.inf)
    inter = b + m
    m_t = jnp.maximum(inter, jnp.max(dmat, axis=1, keepdims=True))
    sc = mm_nt(q, k) * jnp.exp(dmat - m_t)
    scale_in = jnp.exp(inter - m_t)
    num = mm_nn(sc, v) + scale_in * mm_nn(q, C)
    den = jnp.sum(sc, axis=1, keepdims=True) + scale_in * jnp.sum(q * n, axis=1, keepdims=True)
    h = num * (1.0 / jnp.maximum(jnp.abs(den), jnp.exp(-m_t)))

    m_new = jnp.maximum(b_tot + m, a_max)
    s_p = jnp.exp(b_tot + m - m_new)
    s_l = jnp.exp(a_max - m_new)
    return h, s_p * C + s_l * c_loc, s_p * n + s_l * n_loc, m_new


def _mlstm_combine(hf, hb, o_pre, g):
    h = hf + hb
    mu = jnp.mean(h, axis=-1, keepdims=True)
    var = jnp.mean(jnp.square(h - mu), axis=-1, keepdims=True)
    return _sigmoid(o_pre) * ((h - mu) * lax.rsqrt(var + NORM_EPS) * g)


def _merge(ga, gm, za, zm):
    return _sigmoid(ga) * za + _sigmoid(gm) * zm


def _attn_mask(n, seq):
    shape = (ATT_GROUP * ATT_BLOCK, 3 * ATT_BLOCK)
    qi = n * ATT_BLOCK + (lax.broadcasted_iota(jnp.int32, shape, 0) & (ATT_BLOCK - 1))
    kj = (n - 1) * ATT_BLOCK + lax.broadcasted_iota(jnp.int32, shape, 1)
    return (jnp.abs(qi - kj) <= WINDOW) & (kj >= 0) & (kj < seq)


def _attn_specs(nq, v_base):
    q_spec = pl.BlockSpec((1, ATT_BLOCK, ATT_WIDTH), lambda b, n: (b, n, 0))

    def kv_spec(off, base=0):
        return pl.BlockSpec((1, ATT_BLOCK, ATT_KV_WIDTH), lambda b, n: (b, jnp.clip(n + off, 0, nq - 1), base))

    sink_spec = pl.BlockSpec((ATT_KV_HEADS, ATT_GROUP, 1, 1), lambda b, n: (0, 0, 0, 0))
    specs = [q_spec, kv_spec(-1), kv_spec(0), kv_spec(1), kv_spec(-1, v_base), kv_spec(0, v_base), kv_spec(1, v_base), sink_spec]
    return q_spec, specs, sink_spec


def _head(h):
    return slice(h * ATT_HEAD_DIM, (h + 1) * ATT_HEAD_DIM)


def _group_rows(q_ref, s_ref, h):
    q4 = jnp.concatenate([q_ref[0, :, _head(h * ATT_GROUP + g)] for g in range(ATT_GROUP)], axis=0)
    sink4 = jnp.concatenate([jnp.broadcast_to(s_ref[h, g], (ATT_BLOCK, 1)) for g in range(ATT_GROUP)], axis=0)
    return q4, sink4


def _attn_fwd(q, k, proj3, sink):
    B, S, _ = q.shape
    nq = S // ATT_BLOCK
    q_spec, specs, _ = _attn_specs(nq, C_VA // ATT_KV_WIDTH)

    def body(q_ref, kp, kc, kn, vp, vc, vn, s_ref, o_ref):
        valid = _attn_mask(pl.program_id(1), S)
        for h in range(ATT_KV_HEADS):
            kb = jnp.concatenate([kp[0, :, _head(h)], kc[0, :, _head(h)], kn[0, :, _head(h)]], axis=0)
            vb = jnp.concatenate([vp[0, :, _head(h)], vc[0, :, _head(h)], vn[0, :, _head(h)]], axis=0)
            q4, sink4 = _group_rows(q_ref, s_ref, h)
            o4 = _attn_head(q4, kb, vb, sink4, valid).astype(BF16)
            for g in range(ATT_GROUP):
                o_ref[0, :, _head(h * ATT_GROUP + g)] = o4[g * ATT_BLOCK:(g + 1) * ATT_BLOCK]

    return pl.pallas_call(
        body, name="attn_fwd", grid=(B, nq), in_specs=specs,
        out_specs=q_spec, out_shape=jax.ShapeDtypeStruct(q.shape, BF16),
        compiler_params=_cparams(("parallel", "arbitrary")),
    )(q, k, k, k, proj3, proj3, proj3, sink)


def _attn_bwd(q, k, proj3, sink, dy):
    B, S, _ = q.shape
    nq = S // ATT_BLOCK
    q_spec, specs, sink_spec = _attn_specs(nq, C_VA // ATT_KV_WIDTH)
    kv_full = pl.BlockSpec((1, S, ATT_KV_WIDTH), lambda b, n: (b, 0, 0))

    def body(q_ref, kp, kc, kn, vp, vc, vn, s_ref, dy_ref, dq_ref, dk_ref, dv_ref, ds_ref):
        b, n = pl.program_id(0), pl.program_id(1)
        valid = _attn_mask(n, S)

        @pl.when(n == 0)
        def _():
            dk_ref[...] = jnp.zeros_like(dk_ref)
            dv_ref[...] = jnp.zeros_like(dv_ref)

        @pl.when((n == 0) & (b == 0))
        def _():
            ds_ref[...] = jnp.zeros_like(ds_ref)

        for h in range(ATT_KV_HEADS):
            kb = jnp.concatenate([kp[0, :, _head(h)], kc[0, :, _head(h)], kn[0, :, _head(h)]], axis=0)
            vb = jnp.concatenate([vp[0, :, _head(h)], vc[0, :, _head(h)], vn[0, :, _head(h)]], axis=0)
            q4, sink4 = _group_rows(q_ref, s_ref, h)
            dy4 = jnp.concatenate([dy_ref[0, :, _head(h * ATT_GROUP + g)] for g in range(ATT_GROUP)], axis=0)
            _, vjp = jax.vjp(functools.partial(_attn_head, valid=valid), q4, kb, vb, sink4)
            dq4, dkb, dvb, dsink4 = vjp(dy4)
            for g in range(ATT_GROUP):
                rows = slice(g * ATT_BLOCK, (g + 1) * ATT_BLOCK)
                dq_ref[0, :, _head(h * ATT_GROUP + g)] = dq4[rows]
                ds_ref[h, g] += jnp.sum(dsink4[rows], axis=0, keepdims=True)
            for j, off in enumerate((-1, 0, 1)):
                start = pl.multiple_of(jnp.clip(n + off, 0, nq - 1) * ATT_BLOCK, ATT_BLOCK)
                rows = pl.ds(start, ATT_BLOCK)
                dk_ref[0, rows, _head(h)] += dkb[j * ATT_BLOCK:(j + 1) * ATT_BLOCK]
                dv_ref[0, rows, _head(h)] += dvb[j * ATT_BLOCK:(j + 1) * ATT_BLOCK]

    kv_shape = jax.ShapeDtypeStruct(k.shape, F32)
    return pl.pallas_call(
        body, name="attn_bwd", grid=(B, nq), in_specs=specs + [q_spec],
        out_specs=[q_spec, kv_full, kv_full, sink_spec],
        out_shape=[jax.ShapeDtypeStruct(q.shape, F32), kv_shape, kv_shape, jax.ShapeDtypeStruct(sink.shape, F32)],
        compiler_params=_cparams(("arbitrary", "arbitrary")),
    )(q, k, k, k, proj3, proj3, proj3, sink, dy)


CONV_COLS = 256


def _conv_taps(u, seq):
    row = lax.broadcasted_iota(jnp.int32, u.shape, 0)
    prev = jnp.where(row == 0, 0.0, pltpu.roll(u, 1, axis=0))
    nxt = jnp.where(row == seq - 1, 0.0, pltpu.roll(u, seq - 1, axis=0))
    return prev, nxt


def _conv_fwd(proj3, w8):
    B, S, _ = proj3.shape
    ncb = 2 * MLSTM_WIDTH // CONV_COLS

    def body(u_ref, w_ref, o_ref):
        u = u_ref[0]
        prev, nxt = _conv_taps(u, S)
        o_ref[0] = _silu(prev * w_ref[0:1, :] + u * w_ref[1:2, :] + nxt * w_ref[2:3, :] + w_ref[3:4, :])

    return pl.pallas_call(
        body, name="conv_fwd", grid=(B, ncb),
        in_specs=[pl.BlockSpec((1, S, CONV_COLS), lambda b, c: (b, 0, C_QK // CONV_COLS + c)),
                  pl.BlockSpec((8, CONV_COLS), lambda b, c: (0, c))],
        out_specs=pl.BlockSpec((1, S, CONV_COLS), lambda b, c: (b, 0, c)),
        out_shape=jax.ShapeDtypeStruct((B, S, 2 * MLSTM_WIDTH), F32),
        compiler_params=_cparams(("parallel", "parallel")),
    )(proj3, w8)


def _conv_bwd(proj3, w8, dout_f, dout_b):
    B, S, _ = proj3.shape
    ncb = 2 * MLSTM_WIDTH // CONV_COLS

    def body(u_ref, w_ref, df_ref, db_ref, du_ref, dw_ref):
        b = pl.program_id(1)
        u = u_ref[0]
        prev, nxt = _conv_taps(u, S)
        w0, w1, w2 = w_ref[0:1, :], w_ref[1:2, :], w_ref[2:3, :]
        pre = prev * w0 + u * w1 + nxt * w2 + w_ref[3:4, :]
        sig = _sigmoid(pre)
        dpre = (df_ref[0] + db_ref[0]) * (sig * (1.0 + pre * (1.0 - sig)))
        dprev, dnxt = _conv_taps(dpre, S)
        du_ref[0] = (dnxt * w0 + dpre * w1 + dprev * w2).astype(BF16)

        @pl.when(b == 0)
        def _():
            dw_ref[...] = jnp.zeros_like(dw_ref)

        dw_ref[0:1, :] += jnp.sum(dpre * prev, axis=0, keepdims=True)
        dw_ref[1:2, :] += jnp.sum(dpre * u, axis=0, keepdims=True)
        dw_ref[2:3, :] += jnp.sum(dpre * nxt, axis=0, keepdims=True)
        dw_ref[3:4, :] += jnp.sum(dpre, axis=0, keepdims=True)

    blk = pl.BlockSpec((1, S, CONV_COLS), lambda c, b: (b, 0, c))
    return pl.pallas_call(
        body, name="conv_bwd", grid=(ncb, B),
        in_specs=[pl.BlockSpec((1, S, CONV_COLS), lambda c, b: (b, 0, C_QK // CONV_COLS + c)),
                  pl.BlockSpec((8, CONV_COLS), lambda c, b: (0, c)), blk, blk],
        out_specs=[blk, pl.BlockSpec((8, CONV_COLS), lambda c, b: (0, c))],
        out_shape=[jax.ShapeDtypeStruct((B, S, 2 * MLSTM_WIDTH), BF16), jax.ShapeDtypeStruct((8, 2 * MLSTM_WIDTH), F32)],
        compiler_params=_cparams(("parallel", "arbitrary")),
    )(proj3, w8, dout_f, dout_b)


MLSTM_HEADS_PER_STEP = 4


def _chunk_masks(direction):
    t = lax.broadcasted_iota(jnp.int32, (MLSTM_CHUNK, MLSTM_CHUNK), 0)
    s = lax.broadcasted_iota(jnp.int32, (MLSTM_CHUNK, MLSTM_CHUNK), 1)
    le, ge = (s <= t).astype(F32), (s >= t).astype(F32)
    eye = (s == t).astype(F32)
    return (le, ge, eye) if direction == 0 else (ge, le, eye)


def _gate_cols(gates, direction, head):
    lane = lax.broadcasted_iota(jnp.int32, gates.shape, 1)
    sel_i = (lane == (2 * direction) * MLSTM_HEADS + head).astype(F32)
    sel_f = (lane == (2 * direction + 1) * MLSTM_HEADS + head).astype(F32)
    return sel_i, sel_f


def _mlstm_fwd(qk, proj3, bias):
    B, S, _ = qk.shape
    nc = S // MLSTM_CHUNK
    H, L, DH = MLSTM_HEADS, MLSTM_CHUNK, MLSTM_HEAD_DIM

    def chunk_of(d, c):
        return c if d == 0 else nc - 1 - c

    HS = MLSTM_HEADS_PER_STEP

    def body(qkf, qkb, vf, vb, gf, gb, bias_ref, hf, hb, csf, csb, nsf, nsb, msf, msb, c_st, n_st, m_st):
        c, hg = pl.program_id(1), pl.program_id(2)

        @pl.when(c == 0)
        def _():
            for d in range(2):
                for j in range(HS):
                    c_st[d, hg * HS + j] = jnp.zeros((DH, DH), F32)
                    n_st[d, hg * HS + j] = jnp.zeros((1, DH), F32)
                    m_st[d, hg * HS + j] = jnp.zeros((1, DH), F32)

        for d, (qk_ref, v_ref, g_ref, h_ref, cs, ns, ms) in enumerate(
                ((qkf, vf, gf, hf, csf, nsf, msf), (qkb, vb, gb, hb, csb, nsb, msb))):
            incl, incl_t, eye = _chunk_masks(d)
            gates = g_ref[0] + bias_ref[...]
            log_f = _log_sigmoid(gates)
            for j in range(HS):
                h = hg * HS + j
                sel_i, sel_f = _gate_cols(gates, d, h)
                li = jnp.sum(gates * sel_i, axis=1, keepdims=True)
                lf = jnp.sum(log_f * sel_f, axis=1, keepdims=True)
                c_in, n_in, m_in = c_st[d, h], n_st[d, h], m_st[d, h]
                cs[0, 0, j], ns[0, 0, j], ms[0, 0, j] = c_in, n_in, m_in
                hh, c_new, n_new, m_new = _mlstm_chunk(
                    qk_ref[0, :, 2 * j * DH:(2 * j + 1) * DH], qk_ref[0, :, (2 * j + 1) * DH:(2 * j + 2) * DH],
                    v_ref[0, :, j * DH:(j + 1) * DH], li, lf, c_in, n_in,
                    jnp.max(m_in, axis=1, keepdims=True), incl, incl_t, eye)
                h_ref[0, :, j * DH:(j + 1) * DH] = hh
                c_st[d, h], n_st[d, h] = c_new, n_new
                m_st[d, h] = jnp.broadcast_to(m_new, (1, DH))

    def tok_spec(width, base, d, per_head):
        return pl.BlockSpec((1, L, width), lambda b, c, h: (b, chunk_of(d, c), base + (h if per_head else 0)))

    def st_spec(shape, d):
        return pl.BlockSpec((1, 1, HS) + shape, lambda b, c, h: (b, chunk_of(d, c), h, 0, 0))

    in_specs = [tok_spec(2 * HS * DH, 0, 0, True), tok_spec(2 * HS * DH, 0, 1, True),
                tok_spec(HS * DH, C_VM // (HS * DH), 0, True), tok_spec(HS * DH, C_VM // (HS * DH), 1, True),
                tok_spec(LANES, C_GATES // LANES, 0, False), tok_spec(LANES, C_GATES // LANES, 1, False),
                pl.BlockSpec((1, LANES), lambda b, c, h: (0, 0))]
    out_specs = [tok_spec(HS * DH, 0, 0, True), tok_spec(HS * DH, 0, 1, True),
                 st_spec((DH, DH), 0), st_spec((DH, DH), 1), st_spec((1, DH), 0), st_spec((1, DH), 1),
                 st_spec((1, DH), 0), st_spec((1, DH), 1)]
    hs = jax.ShapeDtypeStruct((B, S, H * DH), F32)
    cs = jax.ShapeDtypeStruct((B, nc, H, DH, DH), F32)
    vs = jax.ShapeDtypeStruct((B, nc, H, 1, DH), F32)
    return pl.pallas_call(
        body, name="mlstm_fwd", grid=(B, nc, H // HS), in_specs=in_specs, out_specs=out_specs,
        out_shape=[hs, hs, cs, cs, vs, vs, vs, vs],
        scratch_shapes=[pltpu.VMEM((2, H, DH, DH), F32), pltpu.VMEM((2, H, 1, DH), F32), pltpu.VMEM((2, H, 1, DH), F32)],
        compiler_params=_cparams(("parallel", "arbitrary", "arbitrary")),
    )(qk, qk, proj3, proj3, proj3, proj3, bias)


def _mlstm_bwd(qk, proj3, bias, states, dh):
    B, S, _ = qk.shape
    nc = S // MLSTM_CHUNK
    H, L, DH = MLSTM_HEADS, MLSTM_CHUNK, MLSTM_HEAD_DIM

    def chunk_of(d, c):
        return nc - 1 - c if d == 0 else c

    HS = MLSTM_HEADS_PER_STEP

    def body(qkf, qkb, vf, vb, gf, gb, bias_ref, csf, csb, nsf, nsb, msf, msb, dhf, dhb,
             dqkf, dqkb, dvf, dvb, dgf, dgb, dc_st, dn_st, dm_st):
        c, hg = pl.program_id(1), pl.program_id(2)

        @pl.when(c == 0)
        def _():
            for d in range(2):
                for j in range(HS):
                    dc_st[d, hg * HS + j] = jnp.zeros((DH, DH), F32)
                    dn_st[d, hg * HS + j] = jnp.zeros((1, DH), F32)
                    dm_st[d, hg * HS + j] = jnp.zeros((1, DH), F32)

        @pl.when(hg == 0)
        def _():
            dgf[...] = jnp.zeros_like(dgf)
            dgb[...] = jnp.zeros_like(dgb)

        for d, (qk_ref, v_ref, g_ref, cs, ns, ms, dh_ref, dqk_ref, dv_ref, dg_ref) in enumerate(
                ((qkf, vf, gf, csf, nsf, msf, dhf, dqkf, dvf, dgf), (qkb, vb, gb, csb, nsb, msb, dhb, dqkb, dvb, dgb))):
            incl, incl_t, eye = _chunk_masks(d)
            gates = g_ref[0] + bias_ref[...]
            log_f = _log_sigmoid(gates)
            d_li = jnp.zeros_like(gates)
            d_lf = jnp.zeros_like(gates)
            for j in range(HS):
                h = hg * HS + j
                sel_i, sel_f = _gate_cols(gates, d, h)
                li = jnp.sum(gates * sel_i, axis=1, keepdims=True)
                lf = jnp.sum(log_f * sel_f, axis=1, keepdims=True)
                m_in = jnp.max(ms[0, 0, j], axis=1, keepdims=True)
                _, vjp = jax.vjp(
                    functools.partial(_mlstm_chunk, incl=incl, incl_t=incl_t, eye=eye),
                    qk_ref[0, :, 2 * j * DH:(2 * j + 1) * DH], qk_ref[0, :, (2 * j + 1) * DH:(2 * j + 2) * DH],
                    v_ref[0, :, j * DH:(j + 1) * DH], li, lf, cs[0, 0, j], ns[0, 0, j], m_in)
                dm_out = jnp.max(dm_st[d, h], axis=1, keepdims=True)
                dq, dk, dv, dli, dlf, dc, dn, dm = vjp((dh_ref[0, :, j * DH:(j + 1) * DH], dc_st[d, h], dn_st[d, h], dm_out))
                dqk_ref[0, :, 2 * j * DH:(2 * j + 1) * DH] = dq
                dqk_ref[0, :, (2 * j + 1) * DH:(2 * j + 2) * DH] = dk
                dv_ref[0, :, j * DH:(j + 1) * DH] = dv
                d_li += dli * sel_i
                d_lf += dlf * sel_f
                dc_st[d, h], dn_st[d, h] = dc, dn
                dm_st[d, h] = jnp.broadcast_to(dm, (1, DH))
            dg_ref[0] += d_li + d_lf * _sigmoid(-gates)

    def tok_spec(width, base, d, per_head):
        return pl.BlockSpec((1, L, width), lambda b, c, h: (b, chunk_of(d, c), base + (h if per_head else 0)))

    def st_spec(shape, d):
        return pl.BlockSpec((1, 1, HS) + shape, lambda b, c, h: (b, chunk_of(d, c), h, 0, 0))

    in_specs = [tok_spec(2 * HS * DH, 0, 0, True), tok_spec(2 * HS * DH, 0, 1, True),
                tok_spec(HS * DH, C_VM // (HS * DH), 0, True), tok_spec(HS * DH, C_VM // (HS * DH), 1, True),
                tok_spec(LANES, C_GATES // LANES, 0, False), tok_spec(LANES, C_GATES // LANES, 1, False),
                pl.BlockSpec((1, LANES), lambda b, c, h: (0, 0)),
                st_spec((DH, DH), 0), st_spec((DH, DH), 1), st_spec((1, DH), 0), st_spec((1, DH), 1),
                st_spec((1, DH), 0), st_spec((1, DH), 1), tok_spec(HS * DH, 0, 0, True), tok_spec(HS * DH, 0, 1, True)]
    out_specs = [tok_spec(2 * HS * DH, 0, 0, True), tok_spec(2 * HS * DH, 0, 1, True),
                 tok_spec(HS * DH, 0, 0, True), tok_spec(HS * DH, 0, 1, True),
                 tok_spec(LANES, 0, 0, False), tok_spec(LANES, 0, 1, False)]
    qks = jax.ShapeDtypeStruct((B, S, 2 * H * DH), F32)
    vs = jax.ShapeDtypeStruct((B, S, H * DH), F32)
    gs = jax.ShapeDtypeStruct((B, S, LANES), F32)
    csf, csb, nsf, nsb, msf, msb = states
    return pl.pallas_call(
        body, name="mlstm_bwd", grid=(B, nc, H // HS), in_specs=in_specs, out_specs=out_specs,
        out_shape=[qks, qks, vs, vs, gs, gs],
        scratch_shapes=[pltpu.VMEM((2, H, DH, DH), F32), pltpu.VMEM((2, H, 1, DH), F32), pltpu.VMEM((2, H, 1, DH), F32)],
        compiler_params=_cparams(("parallel", "arbitrary", "arbitrary")),
    )(qk, qk, proj3, proj3, proj3, proj3, bias, csf, csb, nsf, nsb, msf, msb, dh, dh)


ROW_BLOCK = 256
FF_COLS = 512
FF_SHARD = D_FF // N_DEV
FF_SHARD_PAD = 384
FF_PAD = N_DEV * FF_SHARD_PAD


def _rms_fwd(name, x, g):
    T = x.shape[0]
    return _rowwise(name, lambda xv, gv: _rms(xv, gv), [_In(x), _In(g, rows=False)], [_Out(D_MODEL, BF16)], T, ROW_BLOCK)[0]


def _rms_bwd(name, x, g, dh, dres):
    T = x.shape[0]

    def fn(xv, gv, dhv, drv):
        _, vjp = jax.vjp(_rms, xv, gv)
        dx, dg = vjp(dhv)
        return drv + dx, dg

    return _rowwise(name, fn, [_In(x), _In(g, rows=False), _In(dh), _In(dres)],
                    [_Out(D_MODEL), _Out(D_MODEL, rows=False)], T, ROW_BLOCK)


def _mmw(name, a, w, mode, **kw):
    if isinstance(w, tuple):
        return _matmul(name, a, w[0], mode, bl=w[1], **kw)
    return _matmul(name, a, w, mode, **kw)


def _swiglu(gate, up):
    return _silu(gate) * up


def _ffn_in(name, h, wg, wu):
    (M, K), N = h.shape, wg.shape[0]
    tm, tn = _first_divisor(M, (1024, 512, 256, 128)), FF_COLS

    def body(h_ref, wg_ref, wu_ref, g_ref, u_ref, a_ref):
        hv = h_ref[...]
        gate = _mm(hv, wg_ref[...], _NT)
        up = _mm(hv, wu_ref[...], _NT)
        g_ref[...], u_ref[...] = gate.astype(BF16), up.astype(BF16)
        a_ref[...] = _swiglu(gate, up).astype(BF16)

    w_spec = pl.BlockSpec((tn, K), lambda i, j: (j, 0))
    o_spec = pl.BlockSpec((tm, tn), lambda i, j: (i, j))
    return pl.pallas_call(
        body, name=name, grid=(M // tm, N // tn), in_specs=[pl.BlockSpec((tm, K), lambda i, j: (i, 0)), w_spec, w_spec],
        out_specs=[o_spec, o_spec, o_spec],
        out_shape=[jax.ShapeDtypeStruct((M, N), BF16)] * 3,
        compiler_params=_cparams(("parallel", "parallel")),
    )(h, wg, wu)


def _ffn_dact(name, dx, wd, gate, up):
    (M, K), N = dx.shape, wd.shape[0]
    tm, tn = _first_divisor(M, (1024, 512, 256, 128)), FF_COLS

    def body(dx_ref, wd_ref, g_ref, u_ref, dg_ref, du_ref):
        dact = _mm(dx_ref[...], wd_ref[...], _NT) * 0.5
        _, vjp = jax.vjp(_swiglu, g_ref[...].astype(F32), u_ref[...].astype(F32))
        dgate, dup = vjp(dact)
        dg_ref[...], du_ref[...] = dgate.astype(BF16), dup.astype(BF16)

    o_spec = pl.BlockSpec((tm, tn), lambda i, j: (i, j))
    return pl.pallas_call(
        body, name=name, grid=(M // tm, N // tn),
        in_specs=[pl.BlockSpec((tm, K), lambda i, j: (i, 0)), pl.BlockSpec((tn, K), lambda i, j: (j, 0)), o_spec, o_spec],
        out_specs=[o_spec, o_spec],
        out_shape=[jax.ShapeDtypeStruct((M, N), BF16), jax.ShapeDtypeStruct((M, N), BF16)],
        compiler_params=_cparams(("parallel", "parallel")),
    )(dx, wd, gate, up)


def _ffn_dh(name, dgate, dup, wg, wu, dep, x, gain, dres):
    (M, K), N = dgate.shape, wg.shape[1]
    tm, tk = _first_divisor(M, (512, 256, 128)), _first_divisor(K, (1024, 512, 384, 256, 128))
    nk = K // tk

    def body(dg_ref, du_ref, wg_ref, wu_ref, x_ref, gain_ref, dres_ref, dep_ref, o_ref, dgain_ref, acc):
        i, k = pl.program_id(0), pl.program_id(1)

        @pl.when(k == 0)
        def _():
            acc[...] = jnp.zeros_like(acc)

        acc[...] += _mm(dg_ref[...], wg_ref[...], _NN) + _mm(du_ref[...], wu_ref[...], _NN)

        @pl.when((k == nk - 1) & (i == 0))
        def _():
            dgain_ref[...] = jnp.zeros_like(dgain_ref)

        @pl.when(k == nk - 1)
        def _():
            _, vjp = jax.vjp(_rms, x_ref[...], gain_ref[...])
            dx, dgain = vjp(acc[...])
            o_ref[...] = dres_ref[...] + dx
            dgain_ref[...] += dgain

    a_spec = pl.BlockSpec((tm, tk), lambda i, k: (i, k))
    w_spec = pl.BlockSpec((tk, N), lambda i, k: (k, 0))
    row_spec = pl.BlockSpec((tm, N), lambda i, k: (i, 0))
    gain_spec = pl.BlockSpec((1, N), lambda i, k: (0, 0))
    return pl.pallas_call(
        body, name=name, grid=(M // tm, nk),
        in_specs=[a_spec, a_spec, w_spec, w_spec, row_spec, gain_spec, row_spec, pl.BlockSpec(memory_space=pl.ANY)],
        out_specs=[row_spec, gain_spec],
        out_shape=[jax.ShapeDtypeStruct((M, N), F32), jax.ShapeDtypeStruct((1, N), F32)],
        scratch_shapes=[pltpu.VMEM((tm, N), F32)], compiler_params=_cparams(("arbitrary", "arbitrary")),
    )(dgate, dup, wg, wu, x, gain, dres, dep)


def _ffn_fwd(tag, x, g, wg, wu, wd):
    h = _rms_fwd(tag + "_norm", x, g)
    gate, up, act = _ffn_in(tag + "_in", h, wg, wu)
    if callable(wd):
        wd = wd(act)
    out = _mmw(tag + "_down", act, wd, "nn", res=x, scale=0.5)
    return out, (x, h, gate, up, act), wd


def _ffn_bwd(tag, saved, g, wg, wu, wd, dx, on_dw):
    x, h, gate, up, act = saved
    dgate, dup = _ffn_dact(tag + "_dact", dx, wd, gate, up)
    dwd = _matmul(tag + "_dwd", act, dx, "tn", scale=0.5, out_dtype=BF16)
    dwg = _matmul(tag + "_dwg", dgate, h, "tn", out_dtype=BF16)
    dwu = _matmul(tag + "_dwu", dup, h, "tn", out_dtype=BF16)
    token = on_dw({tag + "_w_gate": dwg, tag + "_w_up": dwu, tag + "_w_down": dwd}, dwu)
    return _ffn_dh(tag + "_dh", dgate, dup, wg, wu, token, x, g, dx)


def _rope_cos_sin(positions):
    half = ROPE_DIM // 2
    inv_freq = jnp.power(jnp.float32(ROPE_THETA), -jnp.arange(half, dtype=F32) * (2.0 / ROPE_DIM))
    head = jnp.zeros((ATT_HEAD_DIM,), F32).at[:ROPE_DIM].set(jnp.concatenate([inv_freq, inv_freq]))
    row = jnp.tile(head, LANES // ATT_HEAD_DIM)[None, :]
    T = positions.shape[0]
    return _rowwise("rope_tables", _rope_tables, [_In(positions), _In(row, rows=False)], [_Out(LANES), _Out(LANES)], T, 1024)


def _prep_fwd(name, src, width, base, g, cos, sin):
    return _rowwise(name, _qk_prep, [_In(src, width, base), _In(g, rows=False), _In(cos), _In(sin)],
                    [_Out(width)], src.shape[0], 512)[0]


def _prep_bwd(name, src, width, base, g, cos, sin, dout):
    def fn(tv, gv, cv, sv, dv):
        _, vjp = jax.vjp(lambda a, b: _qk_prep(a, b, cv, sv), tv, gv)
        return vjp(dv)

    return _rowwise(name, fn, [_In(src, width, base), _In(g, rows=False), _In(cos), _In(sin), _In(dout)],
                    [_Out(width, BF16), _Out(width, rows=False)], src.shape[0], 512)


def _to_heads(t, B, S, nh):
    return t.reshape(B, S, nh, ATT_HEAD_DIM).transpose(0, 2, 1, 3)


def _from_heads(t):
    B, nh, S, _ = t.shape
    return t.transpose(0, 2, 1, 3).reshape(B * S, nh * ATT_HEAD_DIM)


def _mix_fwd(x, cos, sin, B, S, p):
    T = B * S
    h = _rms_fwd("mix_norm", x, p["mix_norm"])
    proj = _matmul("mix_proj", h, p["w_in"], "nn")
    proj3 = proj.reshape(B, S, IN_PAD)
    q_gain = jnp.tile(p["attn_q_norm"], (1, ATT_HEADS))
    k_gain = jnp.tile(p["attn_k_norm"], (1, ATT_KV_HEADS))
    q_r = _prep_fwd("q_prep", proj, ATT_WIDTH, C_QA // ATT_WIDTH, q_gain, cos, sin)
    k_r = _prep_fwd("k_prep", proj, ATT_KV_WIDTH, C_KA // ATT_KV_WIDTH, k_gain, cos, sin)
    qh = q_r.reshape(B, S, ATT_WIDTH)
    kh = k_r.reshape(B, S, ATT_KV_WIDTH)
    sink = p["attn_sink"].reshape(ATT_KV_HEADS, ATT_GROUP, 1, 1)
    y_a = _attn_fwd(qh, kh, proj3, sink).reshape(T, ATT_WIDTH)

    qk_c = _conv_fwd(proj3, p["conv_w8"])
    hf, hb, *states = _mlstm_fwd(qk_c, proj3, p["gate_bias"])
    hf2, hb2 = hf.reshape(T, MLSTM_WIDTH), hb.reshape(T, MLSTM_WIDTH)
    DH = MLSTM_HEAD_DIM
    y_m = _rowwise("mlstm_out", _mlstm_combine,
                   [_In(hf2, DH, split=True), _In(hb2, DH, split=True), _In(proj, DH, C_OM // DH, split=True),
                    _In(p["mlstm_out_norm"], DH, split=True, rows=False)],
                   [_Out(MLSTM_WIDTH, BF16, DH, split=True)], T, 1024, ncol=MLSTM_HEADS)[0]

    za = _mmw("branch_a", y_a, p["w_branch_attn"], "nn")
    zm = _mmw("branch_m", y_m, p["w_branch_mlstm"], "nn")
    W = 512
    merged = _rowwise("merge", _merge,
                      [_In(proj, W, C_GMERGE // W, split=True), _In(proj, W, (C_GMERGE + D_MODEL) // W, split=True),
                       _In(za, W, split=True), _In(zm, W, split=True)],
                      [_Out(D_MODEL, BF16, W, split=True)], T, 512, ncol=D_MODEL // W)[0]
    out = _mmw("mix_out", merged, p["w_out"], "nn", res=x)
    saved = dict(x=x, h=h, proj=proj, q_gain=q_gain, k_gain=k_gain, qh=qh, kh=kh, sink=sink, y_a=y_a, qk_c=qk_c,
                 hf=hf2, hb=hb2, states=states, y_m=y_m, za=za, zm=zm, merged=merged)
    return out, saved


def _mix_bwd(sv, cos, sin, B, S, p, dx, on_dw):
    T = B * S
    DH = MLSTM_HEAD_DIM
    proj = sv["proj"]
    proj3 = proj.reshape(B, S, IN_PAD)
    g = {}
    dmerged = _mmw("mix_dmerged", dx, p["w_out"], "nt")
    g["w_out"] = _matmul("mix_dwout", sv["merged"], dx, "tn", out_dtype=BF16)
    W = 512

    def merge_bwd(ga, gm, za, zm, dm):
        _, vjp = jax.vjp(_merge, ga, gm, za, zm)
        return vjp(dm)

    dga, dgm, dza, dzm = _rowwise(
        "merge_bwd", merge_bwd,
        [_In(proj, W, C_GMERGE // W, split=True), _In(proj, W, (C_GMERGE + D_MODEL) // W, split=True),
         _In(sv["za"], W, split=True), _In(sv["zm"], W, split=True), _In(dmerged, W, split=True)],
        [_Out(D_MODEL, BF16, W, split=True), _Out(D_MODEL, BF16, W, split=True),
         _Out(D_MODEL, BF16, W, split=True), _Out(D_MODEL, BF16, W, split=True)], T, 512, ncol=D_MODEL // W)
    dya = _mmw("branch_a_dx", dza, p["w_branch_attn"], "nt")
    g["w_branch_attn"] = _matmul("branch_a_dw", sv["y_a"], dza, "tn", out_dtype=BF16)
    dym = _mmw("branch_m_dx", dzm, p["w_branch_mlstm"], "nt")
    g["w_branch_mlstm"] = _matmul("branch_m_dw", sv["y_m"], dzm, "tn", out_dtype=BF16)

    def combine_bwd(hf, hb, o_pre, gn, dy):
        _, vjp = jax.vjp(_mlstm_combine, hf, hb, o_pre, gn)
        dhf, _, do, dg = vjp(dy)
        return dhf, do, dg

    dh, dom, g["mlstm_out_norm"] = _rowwise(
        "mlstm_out_bwd", combine_bwd,
        [_In(sv["hf"], DH, split=True), _In(sv["hb"], DH, split=True), _In(proj, DH, C_OM // DH, split=True),
         _In(p["mlstm_out_norm"], DH, split=True, rows=False), _In(dym, DH, split=True)],
        [_Out(MLSTM_WIDTH, F32, DH, split=True), _Out(MLSTM_WIDTH, BF16, DH, split=True),
         _Out(MLSTM_WIDTH, F32, DH, split=True, rows=False)], T, 1024, ncol=MLSTM_HEADS)
    dqk_f, dqk_b, dv_f, dv_b, dg_f, dg_b = _mlstm_bwd(sv["qk_c"], proj3, p["gate_bias"], sv["states"],
                                                       dh.reshape(B, S, MLSTM_WIDTH))
    dgates, dvm, g["gate_bias"] = _rowwise(
        "mlstm_dsum", lambda a, b, c, d: (a + b, c + d, jnp.sum(a + b, axis=0, keepdims=True)),
        [_In(dg_f.reshape(T, LANES)), _In(dg_b.reshape(T, LANES)), _In(dv_f.reshape(T, MLSTM_WIDTH)), _In(dv_b.reshape(T, MLSTM_WIDTH))],
        [_Out(LANES, BF16), _Out(MLSTM_WIDTH, BF16), _Out(LANES, rows=False)], T, 1024)
    dqk, g["conv_w8"] = _conv_bwd(proj3, p["conv_w8"], dqk_f, dqk_b)

    dqh, dkh, dvh, dsink = _attn_bwd(sv["qh"], sv["kh"], proj3, sv["sink"], dya.reshape(B, S, ATT_WIDTH))
    g["attn_sink"] = dsink.reshape(1, ATT_HEADS)
    dva = dvh.reshape(T, ATT_KV_WIDTH)
    dqa, dq_gain = _prep_bwd("q_prep_bwd", proj, ATT_WIDTH, C_QA // ATT_WIDTH, sv["q_gain"], cos, sin,
                             dqh.reshape(T, ATT_WIDTH))
    dka, dk_gain = _prep_bwd("k_prep_bwd", proj, ATT_KV_WIDTH, C_KA // ATT_KV_WIDTH, sv["k_gain"], cos, sin,
                             dkh.reshape(T, ATT_KV_WIDTH))
    g["attn_q_norm"] = jnp.sum(dq_gain.reshape(ATT_HEADS, ATT_HEAD_DIM), axis=0, keepdims=True)
    g["attn_k_norm"] = jnp.sum(dk_gain.reshape(ATT_KV_HEADS, ATT_HEAD_DIM), axis=0, keepdims=True)

    dproj = jnp.concatenate(
        [dga, dgm, dqk.reshape(T, 2 * MLSTM_WIDTH), dvm, dom, dqa, dka, dva.astype(BF16), dgates], axis=1)
    dwin = _matmul("mix_dwin", sv["h"], dproj, "tn", out_dtype=BF16)
    token = on_dw({"w_in": _w_in_to_slots(dwin), "w_branch_attn": g.pop("w_branch_attn"),
                   "w_branch_mlstm": g.pop("w_branch_mlstm"), "w_out": g.pop("w_out")}, dwin)
    dh2 = _matmul("mix_dh", dproj, p["w_in"], "nt", dep=token)
    dx_new, g["mix_norm"] = _rms_bwd("mix_dnorm", sv["x"], p["mix_norm"], dh2, dx)
    return dx_new, g


def _loss_and_grad(x, g, target):
    T = x.shape[0]

    def loss_fn(xv, gv, tv):
        err = jnp.square(_rms(xv, gv) - tv)
        return 0.5 * jnp.sum(jnp.mean(err, axis=-1, keepdims=True), axis=0, keepdims=True)

    def fn(xv, gv, tv):
        val, vjp = jax.vjp(lambda a, b: loss_fn(a, b, tv), xv, gv)
        dx, dg = vjp(jnp.ones((1, 1), F32))
        return val, dx, dg

    return _rowwise("loss_head", fn, [_In(x), _In(g, rows=False), _In(target)],
                    [_Out(1, rows=False), _Out(D_MODEL), _Out(D_MODEL, rows=False)], T, ROW_BLOCK)


def _block_norm_fwd(x, g):
    T = x.shape[0]
    return _rowwise("block_norm", _rms, [_In(x), _In(g, rows=False)], [_Out(D_MODEL)], T, ROW_BLOCK)[0]


def _block_norm_bwd(x, g, dy):
    T = x.shape[0]

    def fn(xv, gv, dv):
        _, vjp = jax.vjp(_rms, xv, gv)
        return vjp(dv)

    return _rowwise("block_norm_bwd", fn, [_In(x), _In(g, rows=False), _In(dy)],
                    [_Out(D_MODEL), _Out(D_MODEL, rows=False)], T, ROW_BLOCK)


def _qk_perm_cols(t, axis):
    q, k = jnp.split(t, 2, axis=axis)
    parts = []
    for h in range(MLSTM_HEADS):
        sl = [slice(None)] * t.ndim
        sl[axis] = slice(h * MLSTM_HEAD_DIM, (h + 1) * MLSTM_HEAD_DIM)
        parts += [q[tuple(sl)], k[tuple(sl)]]
    return jnp.concatenate(parts, axis=axis)


def _qk_unperm_cols(t, axis):
    qs, ks = [], []
    for h in range(MLSTM_HEADS):
        sl = [slice(None)] * t.ndim
        sl[axis] = slice(2 * h * MLSTM_HEAD_DIM, (2 * h + 1) * MLSTM_HEAD_DIM)
        qs.append(t[tuple(sl)])
        sl[axis] = slice((2 * h + 1) * MLSTM_HEAD_DIM, (2 * h + 2) * MLSTM_HEAD_DIM)
        ks.append(t[tuple(sl)])
    return jnp.concatenate(qs + ks, axis=axis)


def _w_in_arrange(w):
    qa, ka, va, qm, km, vm, om, gm, gmerge = jnp.split(w, np.cumsum(
        (ATT_WIDTH, ATT_KV_WIDTH, ATT_KV_WIDTH, MLSTM_WIDTH, MLSTM_WIDTH, MLSTM_WIDTH, MLSTM_WIDTH, MLSTM_N_GATES))[:].tolist(), axis=1)
    qk = _qk_perm_cols(jnp.concatenate([qm, km], axis=1), 1)
    pad = jnp.zeros((w.shape[0], LANES - MLSTM_N_GATES), w.dtype)
    return jnp.concatenate([gmerge, qk, vm, om, qa, ka, va, gm, pad], axis=1)


def _w_in_restore(w):
    gmerge = w[:, C_GMERGE:C_GMERGE + 2 * D_MODEL]
    qk = _qk_unperm_cols(w[:, C_QK:C_QK + 2 * MLSTM_WIDTH], 1)
    vm, om = w[:, C_VM:C_VM + MLSTM_WIDTH], w[:, C_OM:C_OM + MLSTM_WIDTH]
    qa, ka, va = w[:, C_QA:C_QA + ATT_WIDTH], w[:, C_KA:C_KA + ATT_KV_WIDTH], w[:, C_VA:C_VA + ATT_KV_WIDTH]
    gm = w[:, C_GATES:C_GATES + MLSTM_N_GATES]
    return jnp.concatenate([qa, ka, va, qk, vm, om, gm, gmerge], axis=1)


BIG = ("ffn1_w_gate", "ffn1_w_up", "ffn1_w_down", "w_in", "mlstm_conv_w", "w_branch_attn", "w_branch_mlstm", "w_out",
       "ffn2_w_gate", "ffn2_w_up", "ffn2_w_down")
MATMUL_W = tuple(n for n in BIG if n != "mlstm_conv_w")
SMALL = ("ffn1_norm", "mix_norm", "mlstm_gate_bias", "attn_q_norm", "attn_k_norm", "attn_sink", "mlstm_conv_b",
         "mlstm_out_norm", "ffn2_norm", "block_out_norm")
WEIGHTS = ("ffn1_norm", "ffn1_w_gate", "ffn1_w_up", "ffn1_w_down", "mix_norm", "w_in", "mlstm_gate_bias", "attn_q_norm",
           "attn_k_norm", "attn_sink", "mlstm_conv_w", "mlstm_conv_b", "mlstm_out_norm", "w_branch_attn", "w_branch_mlstm",
           "w_out", "ffn2_norm", "ffn2_w_gate", "ffn2_w_up", "ffn2_w_down", "block_out_norm")
PACK_COLS = 1024


def _padded_rows(n_elems):
    return -(-n_elems // PACK_COLS)


def _pack_flat(arrs, dtype, row_multiple):
    parts = []
    for a in arrs:
        flat = a.reshape(-1).astype(dtype)
        pad = _padded_rows(flat.shape[0]) * PACK_COLS - flat.shape[0]
        parts.append(jnp.pad(flat, (0, pad)) if pad else flat)
    flat = jnp.concatenate(parts)
    rows = flat.shape[0] // PACK_COLS
    extra = (-rows) % row_multiple
    if extra:
        flat = jnp.pad(flat, (0, extra * PACK_COLS))
    return flat.reshape(-1, PACK_COLS)


def _unpack_flat(buf, shapes, lead=()):
    flat = buf.reshape(lead + (-1,))
    out, off = [], 0
    for s in shapes:
        n = int(np.prod(s))
        out.append(flat[..., off:off + n].reshape(lead + tuple(s)))
        off += _padded_rows(n) * PACK_COLS
    return out


class _Lay:
    def __init__(self, shard, axis, width):
        self.shard, self.axis, self.width = shard, axis, width
        self.padded = tuple(width if a == axis else s for a, s in enumerate(shard))
        self.whole = tuple(N_DEV * width if a == axis else s for a, s in enumerate(shard))

    def pad(self, t, lead=0):
        extra = self.width - self.shard[self.axis]
        if not extra:
            return t
        cfg = [(0, 0)] * t.ndim
        cfg[lead + self.axis] = (0, extra)
        return jnp.pad(t, cfg)

    def unpad(self, t, lead=0):
        idx = [slice(None)] * t.ndim
        idx[lead + self.axis] = slice(0, self.shard[self.axis])
        return t[tuple(idx)]


_FF_ROW = _Lay((FF_SHARD, D_MODEL), 0, FF_SHARD_PAD)
TRANSPOSED = ("ffn1_w_gate", "ffn1_w_up", "ffn2_w_gate", "ffn2_w_up")
LAYOUTS = {
    "ffn1_w_gate": _FF_ROW, "ffn1_w_up": _FF_ROW, "ffn1_w_down": _FF_ROW,
    "ffn2_w_gate": _FF_ROW, "ffn2_w_up": _FF_ROW, "ffn2_w_down": _FF_ROW,
    "w_in": _Lay((D_MODEL, IN_WIDTH // N_DEV), 0, D_MODEL),
    "mlstm_conv_w": _Lay((3, 2 * MLSTM_WIDTH // N_DEV), 1, 2 * MLSTM_WIDTH // N_DEV),
    "w_branch_attn": _Lay((ATT_WIDTH, D_MODEL // N_DEV), 1, D_MODEL // N_DEV),
    "w_branch_mlstm": _Lay((MLSTM_WIDTH, D_MODEL // N_DEV), 1, D_MODEL // N_DEV),
    "w_out": _Lay((D_MODEL // N_DEV, D_MODEL), 0, D_MODEL // N_DEV),
}


def _window(ref, axis, j, width):
    idx = [slice(None)] * len(ref.shape)
    idx[axis] = pl.ds(pl.multiple_of(j * width, width), width)
    return ref.at[tuple(idx)]


ANY = pl.BlockSpec(memory_space=pl.ANY)


def _mesh_pos():
    return lax.axis_index("x"), lax.axis_index("y"), lax.axis_index("c")


def _all_gather(name, shard, vmem=False):
    R, C = shard.shape
    space = pl.BlockSpec(memory_space=pltpu.VMEM) if vmem else ANY

    def body(x_ref, out_ref, send_sems, recv_sems, local_sem):
        x, y, c = _mesh_pos()
        me, sibling = (x, y, c), (x, y, 1 - c)
        chips = [(1 - x, y), (x, 1 - y), (1 - x, 1 - y)]

        def slot(px, py, pc):
            return out_ref.at[4 * px + 2 * py + pc]

        def copy(k, block, to, src=None):
            return pltpu.make_async_remote_copy(
                src_ref=slot(*block) if src is None else src, dst_ref=slot(*block),
                send_sem=send_sems.at[k], recv_sem=recv_sems.at[k], device_id=to, device_id_type=MESH)

        mine = pltpu.make_async_copy(x_ref, slot(*me), local_sem)
        mine.start()
        first = [copy(0, me, sibling, src=x_ref)]
        first += [copy(1 + j, me, (*chip, c), src=x_ref) for j, chip in enumerate(chips)]
        for cp in first:
            cp.start()
        passed = [copy(4 + j, (*chip, c), sibling) for j, chip in enumerate(chips)]
        for j, chip in enumerate(chips):
            copy(1 + j, (*chip, c), me).wait_recv()
            passed[j].start()
        copy(0, sibling, me).wait_recv()
        for j, chip in enumerate(chips):
            copy(4 + j, (*chip, 1 - c), me).wait_recv()
        for cp in first + passed:
            cp.wait_send()
        mine.wait()

    return pl.pallas_call(
        body, name=name, out_shape=jax.ShapeDtypeStruct((N_DEV, R, C), shard.dtype),
        in_specs=[space], out_specs=space,
        scratch_shapes=[pltpu.SemaphoreType.DMA((7,)), pltpu.SemaphoreType.DMA((7,)), pltpu.SemaphoreType.DMA],
    )(shard)


HBM = pl.BlockSpec(memory_space=pltpu.HBM)
SEM = pl.BlockSpec(memory_space=pltpu.SEMAPHORE)
SPLIT_COPY = pltpu.CompilerParams(has_side_effects=pltpu.SideEffectType.DATAFLOW_SIDE_EFFECTING)
N_PEERS = N_DEV - 1


def _peers(x, y, c):
    return [(x, y, 1 - c), (1 - x, y, c), (x, 1 - y, c), (1 - x, 1 - y, c),
            (1 - x, y, 1 - c), (x, 1 - y, 1 - c), (1 - x, 1 - y, 1 - c)]


def _dev_index(pos):
    return 4 * pos[0] + 2 * pos[1] + pos[2]


def _place_own(name, shards, lays):
    nt = len(shards)
    me = _dev_index(_mesh_pos())

    def body(me_ref, *refs):
        for x_ref, o_ref in zip(refs[:nt], refs[nt:]):
            o_ref[...] = x_ref[...]

    def window_spec(lay):
        if lay.axis == 0:
            return pl.BlockSpec(lay.padded, lambda i, me_ref: (me_ref[0], 0))
        return pl.BlockSpec(lay.padded, lambda i, me_ref: (0, me_ref[0]))

    return pl.pallas_call(
        body, name=name,
        grid_spec=pltpu.PrefetchScalarGridSpec(
            num_scalar_prefetch=1, grid=(1,),
            in_specs=[pl.BlockSpec(lay.padded, lambda i, me_ref: (0, 0)) for lay in lays],
            out_specs=[window_spec(lay) for lay in lays]),
        out_shape=[jax.ShapeDtypeStruct(lay.whole, s.dtype) for s, lay in zip(shards, lays)],
        compiler_params=_cparams(("arbitrary",)),
    )(me.reshape(1).astype(jnp.int32), *shards)


def _gather_start(name, shards, lands, lays, groups, after):
    nt, ng = len(shards), len(groups)

    def body(*refs):
        x_refs, land_refs = refs[:nt], refs[nt:2 * nt]
        sems = refs[2 * nt + 1:2 * nt + 1 + 2 * ng]
        pos = _mesh_pos()
        me = _dev_index(pos)
        for g, tens in enumerate(groups):
            for i, t in enumerate(tens):
                for k, peer in enumerate(_peers(*pos)):
                    pltpu.make_async_remote_copy(
                        src_ref=x_refs[t], dst_ref=_window(land_refs[t], lays[t].axis, me, lays[t].width),
                        send_sem=sems[2 * g].at[N_PEERS * i + k], recv_sem=sems[2 * g + 1].at[N_PEERS * i + k],
                        device_id=peer, device_id_type=MESH).start()

    sem_shapes = []
    for tens in groups:
        sem_shapes += [pltpu.SemaphoreType.DMA((N_PEERS * len(tens),))] * 2
    thru = [pltpu.HBM(s.shape, s.dtype) for s in shards] + [pltpu.HBM(lay.whole, s.dtype) for s, lay in zip(shards, lays)]
    args = [pltpu.with_memory_space_constraint(s, pltpu.HBM) for s in shards]
    args += [pltpu.with_memory_space_constraint(ld, pltpu.HBM) for ld in lands]
    res = pl.pallas_call(
        body, name=name, out_shape=tuple(sem_shapes + thru), in_specs=[HBM] * (2 * nt) + [ANY],
        out_specs=tuple([SEM] * (2 * ng) + [HBM] * (2 * nt)),
        input_output_aliases={t: 2 * ng + t for t in range(2 * nt)}, compiler_params=SPLIT_COPY,
    )(*args, after)
    sems = [(res[2 * g], res[2 * g + 1]) for g in range(ng)]
    return sems, list(res[2 * ng:2 * ng + nt]), list(res[2 * ng + nt:])


def _gather_wait(name, sems, shards, lands, lays, after):
    nt = len(shards)
    send_sems, recv_sems = sems

    def body(*refs):
        x_refs, land_refs = refs[:nt], refs[nt:2 * nt]
        send_ref, recv_ref = refs[2 * nt], refs[2 * nt + 1]
        pos = _mesh_pos()
        for t in range(nt):
            for k, peer in enumerate(_peers(*pos)):
                cp = pltpu.make_async_remote_copy(
                    src_ref=x_refs[t], dst_ref=_window(land_refs[t], lays[t].axis, _dev_index(peer), lays[t].width),
                    send_sem=send_ref.at[N_PEERS * t + k], recv_sem=recv_ref.at[N_PEERS * t + k],
                    device_id=peer, device_id_type=MESH)
                cp.wait_send()
                cp.wait_recv()

    thru = [pltpu.HBM(s.shape, s.dtype) for s in shards] + [pltpu.HBM(ld.shape, ld.dtype) for ld in lands]
    res = pl.pallas_call(
        body, name=name, out_shape=tuple(thru), in_specs=[HBM] * (2 * nt) + [SEM, SEM, ANY],
        out_specs=tuple([HBM] * (2 * nt)), input_output_aliases={t: t for t in range(2 * nt)},
        compiler_params=SPLIT_COPY,
    )(*shards, *lands, send_sems, recv_sems, after)
    return list(res[nt:])


def _pair_exchange(name, grads, lays):
    nt = len(grads)

    def body(*refs):
        g_refs, land_refs = refs[:nt], refs[nt:2 * nt]
        send_sems, recv_sems = refs[2 * nt:]
        x, y, c = _mesh_pos()
        copies = []
        for t in range(nt):
            for chip in range(4):
                copies.append(pltpu.make_async_remote_copy(
                    src_ref=_window(g_refs[t], lays[t].axis, 2 * chip + (1 - c), lays[t].width), dst_ref=land_refs[t].at[chip],
                    send_sem=send_sems.at[4 * t + chip], recv_sem=recv_sems.at[4 * t + chip],
                    device_id=(x, y, 1 - c), device_id_type=MESH))
        for cp in copies:
            cp.start()
        for cp in copies:
            cp.wait_recv()
        for cp in copies:
            cp.wait_send()

    out_shape = [jax.ShapeDtypeStruct((4,) + lay.padded, g.dtype) for g, lay in zip(grads, lays)]
    return pl.pallas_call(
        body, name=name, out_shape=out_shape, in_specs=[ANY] * nt, out_specs=[ANY] * nt,
        scratch_shapes=[pltpu.SemaphoreType.DMA((4 * nt,)), pltpu.SemaphoreType.DMA((4 * nt,))],
    )(*grads)


def _pair_sum(name, whole, landed, lay, out_dtype):
    R, C = lay.padded
    br = _first_divisor(R, (512, 384, 256, 128, 64, 32, 16, 8))
    nb = R // br
    if lay.axis == 0:
        mine_spec = pl.BlockSpec((br, C), lambda k, i, c_ref: ((2 * k + c_ref[0]) * nb + i, 0))
    else:
        mine_spec = pl.BlockSpec((br, C), lambda k, i, c_ref: (i, 2 * k + c_ref[0]))

    def body(c_ref, mine_ref, sib_ref, o_ref):
        o_ref[0] = (mine_ref[...].astype(F32) + sib_ref[0].astype(F32)).astype(out_dtype)

    c = lax.axis_index("c")
    return pl.pallas_call(
        body, name=name,
        grid_spec=pltpu.PrefetchScalarGridSpec(
            num_scalar_prefetch=1, grid=(4, nb),
            in_specs=[mine_spec, pl.BlockSpec((1, br, C), lambda k, i, c_ref: (k, i, 0))],
            out_specs=pl.BlockSpec((1, br, C), lambda k, i, c_ref: (k, i, 0))),
        out_shape=jax.ShapeDtypeStruct((4, R, C), out_dtype),
        compiler_params=_cparams(("parallel", "parallel")),
    )(c.reshape(1).astype(jnp.int32), whole, landed)


def _chip_exchange(name, sums):
    nt = len(sums)

    def body(*refs):
        s_refs, land_refs = refs[:nt], refs[nt:2 * nt]
        send_sems, recv_sems, local_sems = refs[2 * nt:]
        x, y, c = _mesh_pos()
        my_chip = 2 * x + y
        mine = [pltpu.make_async_copy(s_refs[t].at[my_chip], land_refs[t].at[my_chip], local_sems.at[t]) for t in range(nt)]
        for cp in mine:
            cp.start()
        chips = [(1 - x, y), (x, 1 - y), (1 - x, 1 - y)]
        copies = []
        for t in range(nt):
            for j, (px, py) in enumerate(chips):
                copies.append(pltpu.make_async_remote_copy(
                    src_ref=s_refs[t].at[2 * px + py], dst_ref=land_refs[t].at[my_chip],
                    send_sem=send_sems.at[3 * t + j], recv_sem=recv_sems.at[3 * t + j],
                    device_id=(px, py, c), device_id_type=MESH))
        for cp in copies:
            cp.start()
        for t in range(nt):
            for j, (px, py) in enumerate(chips):
                pltpu.make_async_remote_copy(
                    src_ref=s_refs[t].at[my_chip], dst_ref=land_refs[t].at[2 * px + py],
                    send_sem=send_sems.at[3 * t + j], recv_sem=recv_sems.at[3 * t + j],
                    device_id=(px, py, c), device_id_type=MESH).wait_recv()
        for cp in copies:
            cp.wait_send()
        for cp in mine:
            cp.wait()

    return pl.pallas_call(
        body, name=name, out_shape=[jax.ShapeDtypeStruct(s.shape, s.dtype) for s in sums],
        in_specs=[ANY] * nt, out_specs=[ANY] * nt,
        scratch_shapes=[pltpu.SemaphoreType.DMA((3 * nt,)), pltpu.SemaphoreType.DMA((3 * nt,)), pltpu.SemaphoreType.DMA((nt,))],
    )(*sums)


def _chip_start(name, sums):
    nt = len(sums)

    def body(*refs):
        s_refs, land_refs = refs[:nt], refs[nt:2 * nt]
        send_sems, recv_sems = refs[2 * nt], refs[2 * nt + 1]
        x, y, c = _mesh_pos()
        my_chip = 2 * x + y
        for t in range(nt):
            for j, (px, py) in enumerate([(1 - x, y), (x, 1 - y), (1 - x, 1 - y)]):
                pltpu.make_async_remote_copy(
                    src_ref=s_refs[t].at[2 * px + py], dst_ref=land_refs[t].at[my_chip],
                    send_sem=send_sems.at[3 * t + j], recv_sem=recv_sems.at[3 * t + j],
                    device_id=(px, py, c), device_id_type=MESH).start()

    thru = [pltpu.HBM(s.shape, s.dtype) for s in sums] * 2
    args = [pltpu.with_memory_space_constraint(s, pltpu.HBM) for s in sums]
    args += [pltpu.with_memory_space_constraint(lax.empty(s.shape, s.dtype), pltpu.HBM) for s in sums]
    res = pl.pallas_call(
        body, name=name, out_shape=tuple([pltpu.SemaphoreType.DMA((3 * nt,))] * 2 + thru), in_specs=[HBM] * (2 * nt),
        out_specs=tuple([SEM, SEM] + [HBM] * (2 * nt)), input_output_aliases={t: 2 + t for t in range(2 * nt)},
        compiler_params=SPLIT_COPY,
    )(*args)
    return (res[0], res[1]), list(res[2:2 + nt]), list(res[2 + nt:])


def _chip_wait(name, sems, sums, lands, after):
    nt = len(sums)

    def body(*refs):
        s_refs, land_refs = refs[:nt], refs[nt:2 * nt]
        send_sems, recv_sems = refs[2 * nt], refs[2 * nt + 1]
        x, y, c = _mesh_pos()
        my_chip = 2 * x + y
        for t in range(nt):
            for j, (px, py) in enumerate([(1 - x, y), (x, 1 - y), (1 - x, 1 - y)]):
                cp = pltpu.make_async_remote_copy(
                    src_ref=s_refs[t].at[my_chip], dst_ref=land_refs[t].at[2 * px + py],
                    send_sem=send_sems.at[3 * t + j], recv_sem=recv_sems.at[3 * t + j],
                    device_id=(px, py, c), device_id_type=MESH)
                cp.wait_send()
                cp.wait_recv()

    thru = [pltpu.HBM(s.shape, s.dtype) for s in sums] * 2
    res = pl.pallas_call(
        body, name=name, out_shape=tuple(thru), in_specs=[HBM] * (2 * nt) + [SEM, SEM, ANY],
        out_specs=tuple([HBM] * (2 * nt)), input_output_aliases={t: t for t in range(2 * nt)},
        compiler_params=SPLIT_COPY,
    )(*sums, *lands, sems[0], sems[1], after)
    return list(res[:nt]), list(res[nt:])


def _sum_chips(name, own, landed):
    _, R, C = own.shape
    br = _first_divisor(R, (512, 384, 256, 128, 64, 32, 16, 8))
    x, y, _ = _mesh_pos()
    slots = jnp.stack([2 * x + y, 2 * (1 - x) + y, 2 * x + (1 - y), 2 * (1 - x) + (1 - y)]).astype(jnp.int32)

    def body(slot_ref, mine_ref, a_ref, b_ref, c_ref, o_ref):
        o_ref[...] = ((mine_ref[0].astype(F32) + a_ref[0].astype(F32)) + b_ref[0].astype(F32)) + c_ref[0].astype(F32)

    def slot_spec(j):
        return pl.BlockSpec((1, br, C), lambda i, slot_ref: (slot_ref[j], i, 0))

    return pl.pallas_call(
        body, name=name,
        grid_spec=pltpu.PrefetchScalarGridSpec(
            num_scalar_prefetch=1, grid=(R // br,), in_specs=[slot_spec(0), slot_spec(1), slot_spec(2), slot_spec(3)],
            out_specs=pl.BlockSpec((br, C), lambda i, slot_ref: (i, 0))),
        out_shape=jax.ShapeDtypeStruct((R, C), F32), compiler_params=_cparams(("parallel",)),
    )(slots, own, landed, landed, landed)


def _sum_slots(name, slots, n):
    _, R, C = slots.shape
    br = _first_divisor(R, (512, 384, 256, 128, 64, 32, 16, 8))

    def body(s_ref, o_ref):
        acc = s_ref[0].astype(F32)
        for k in range(1, n):
            acc = acc + s_ref[k].astype(F32)
        o_ref[...] = acc

    return pl.pallas_call(
        body, name=name, grid=(R // br,), in_specs=[pl.BlockSpec((n, br, C), lambda i: (0, i, 0))],
        out_specs=pl.BlockSpec((br, C), lambda i: (i, 0)), out_shape=jax.ShapeDtypeStruct((R, C), F32),
        compiler_params=_cparams(("parallel",)),
    )(slots)


def _reduce_scatter_start(tag, names, grads):
    lays = [LAYOUTS[n] for n in names]
    landed = _pair_exchange("grads_pair_" + names[0], grads, lays)
    sums = [_pair_sum("grads_pairsum_" + n, g, ld, lay, BF16) for n, g, ld, lay in zip(names, grads, landed, lays)]
    sems, sums, lands = _chip_start(tag + "_chips_start", sums)
    return tag, names, sems, sums, lands


def _reduce_scatter_finish(pending, after):
    tag, names, sems, sums, lands = pending
    own, got = _chip_wait(tag + "_chips_wait", sems, sums, lands, after)
    return [_sum_chips("grads_sum_" + n, o, s) for n, o, s in zip(names, own, got)]


def _adamw_math(w, g, m, v):
    m = ADAM_B1 * m + (1.0 - ADAM_B1) * g
    v = ADAM_B2 * v + (1.0 - ADAM_B2) * jnp.square(g)
    m_hat = m / (1.0 - ADAM_B1 ** ADAM_STEP)
    v_hat = v / (1.0 - ADAM_B2 ** ADAM_STEP)
    delta = -ADAM_LR * (m_hat / (jnp.sqrt(v_hat) + ADAM_EPS) + ADAM_WD * w)
    return delta, m, v


def _adamw_layers(name, w, totals, m, v):
    _, R, C = w.shape
    br = _first_divisor(R, (512, 176, 128, 64, 32, 16, 8))
    Cp = totals[0].shape[1]

    def body(w_ref, g0_ref, g1_ref, m_ref, v_ref, g_out, d_out, m_out, v_out):
        g = jnp.where(pl.program_id(0) == 0, g0_ref[:, 0:C], g1_ref[:, 0:C])
        delta, m_new, v_new = _adamw_math(w_ref[0], g, m_ref[0], v_ref[0])
        g_out[0], d_out[0], m_out[0], v_out[0] = g, delta, m_new, v_new

    blk = pl.BlockSpec((1, br, C), lambda l, i: (l, i, 0))
    g_spec = pl.BlockSpec((br, Cp), lambda l, i: (i, 0))
    return pl.pallas_call(
        body, name=name, grid=(DEPTH, R // br), in_specs=[blk, g_spec, g_spec, blk, blk], out_specs=[blk] * 4,
        out_shape=[jax.ShapeDtypeStruct(w.shape, F32)] * 4, compiler_params=_cparams(("parallel", "parallel")),
    )(w, totals[0], totals[1], m, v)


def _adamw(name, w, g, m, v):
    shape = w.shape
    cols = shape[-1]
    rows = int(np.prod(shape[:-1]))
    br = _first_divisor(rows, (512, 352, 256, 128, 64, 32, 16, 8))
    args = [_In(a.reshape(rows, cols)) for a in (w, g, m, v)]
    outs = _rowwise(name, _adamw_math, args, [_Out(cols), _Out(cols), _Out(cols)], rows, br)
    return [o.reshape(shape) for o in outs]


GROUPS = {"ffn1": ("ffn1_w_gate", "ffn1_w_up", "ffn1_w_down"),
          "mix": ("w_in", "w_branch_attn", "w_branch_mlstm", "w_out"),
          "ffn2": ("ffn2_w_gate", "ffn2_w_up", "ffn2_w_down")}
GATHER_GROUPS = {"ffn1_in": ("ffn1_w_gate", "ffn1_w_up"), "ffn1_out": ("ffn1_w_down",),
                 "mix": ("w_in", "w_branch_attn", "w_branch_mlstm", "w_out"),
                 "ffn2_in": ("ffn2_w_gate", "ffn2_w_up"), "ffn2_out": ("ffn2_w_down",)}


def _small_params(small, conv_w, l):
    p = {}
    for n in ("ffn1_norm", "mix_norm", "ffn2_norm", "block_out_norm", "mlstm_out_norm", "attn_q_norm", "attn_k_norm"):
        p[n] = small[n][l][None, :]
    p["attn_sink"] = small["attn_sink"][l]
    p["gate_bias"] = jnp.pad(small["mlstm_gate_bias"][l], (0, LANES - MLSTM_N_GATES))[None, :]
    taps = _qk_perm_cols(conv_w[l], 1)
    conv_b = _qk_perm_cols(small["mlstm_conv_b"][l][None, :], 1)
    p["conv_w8"] = jnp.concatenate([taps, conv_b, jnp.zeros((4, 2 * MLSTM_WIDTH), F32)], axis=0)
    return p


def _w_in_from_slots(slots):
    w_in = slots.reshape(N_DEV, D_MODEL, IN_WIDTH // N_DEV).transpose(1, 0, 2).reshape(D_MODEL, IN_WIDTH)
    return _w_in_arrange(w_in)


def _w_in_to_slots(g):
    return _w_in_restore(g).reshape(D_MODEL, N_DEV, IN_WIDTH // N_DEV).transpose(1, 0, 2).reshape(
        N_DEV * D_MODEL, IN_WIDTH // N_DEV)


def _local_step(x, positions, target, weights_of, small, conv_w, on_grads):
    B, S, _ = x.shape
    T = B * S
    cos, sin = _rope_cos_sin(positions.reshape(T, 1))
    params = [_small_params(small, conv_w, l) for l in range(DEPTH)]
    xs = x.reshape(T, D_MODEL)
    tgt = target.reshape(T, D_MODEL)

    saved = []
    for l, p in enumerate(params):
        p.update(weights_of(l, "ffn1_in", xs))
        x1, s1, p["ffn1_w_down"] = _ffn_fwd("ffn1", xs, p["ffn1_norm"], p["ffn1_w_gate"], p["ffn1_w_up"],
                                            lambda after, l=l: weights_of(l, "ffn1_out", after)["ffn1_w_down"])
        p.update(weights_of(l, "mix", x1))
        p["w_in"] = _w_in_from_slots(p["w_in"])
        x2, s2 = _mix_fwd(x1, cos, sin, B, S, p)
        p.update(weights_of(l, "ffn2_in", x2))
        x3, s3, p["ffn2_w_down"] = _ffn_fwd("ffn2", x2, p["ffn2_norm"], p["ffn2_w_gate"], p["ffn2_w_up"],
                                            lambda after, l=l: weights_of(l, "ffn2_out", after)["ffn2_w_down"])
        saved.append((s1, s2, s3, x3))
        if l + 1 < DEPTH:
            xs = _block_norm_fwd(x3, p["block_out_norm"])

    sm = {n: [None] * DEPTH for n in SMALL + ("mlstm_conv_w",)}
    loss = None
    dx = None
    for l in reversed(range(DEPTH)):
        p = params[l]
        s1, s2, s3, x3 = saved[l]
        if l == DEPTH - 1:
            loss, dx, dgn = _loss_and_grad(x3, p["block_out_norm"], tgt)
        else:
            dx, dgn = _block_norm_bwd(x3, p["block_out_norm"], dx)
        sm["block_out_norm"][l] = dgn[0]
        dx, dg = _ffn_bwd("ffn2", s3, p["ffn2_norm"], p["ffn2_w_gate"], p["ffn2_w_up"], p["ffn2_w_down"], dx,
                          functools.partial(on_grads, l, "ffn2"))
        sm["ffn2_norm"][l] = dg[0]
        dx, g = _mix_bwd(s2, cos, sin, B, S, p, dx, functools.partial(on_grads, l, "mix"))
        dconv = _qk_unperm_cols(g["conv_w8"], 1)
        sm["mlstm_conv_w"][l] = dconv[0:3]
        sm["mlstm_conv_b"][l] = dconv[3]
        sm["mix_norm"][l] = g["mix_norm"][0]
        sm["mlstm_gate_bias"][l] = g["gate_bias"][0, :MLSTM_N_GATES]
        sm["attn_q_norm"][l], sm["attn_k_norm"][l] = g["attn_q_norm"][0], g["attn_k_norm"][0]
        sm["attn_sink"][l] = g["attn_sink"][0]
        sm["mlstm_out_norm"][l] = g["mlstm_out_norm"][0]
        dx, dg = _ffn_bwd("ffn1", s1, p["ffn1_norm"], p["ffn1_w_gate"], p["ffn1_w_up"], p["ffn1_w_down"], dx,
                          functools.partial(on_grads, l, "ffn1"))
        sm["ffn1_norm"][l] = dg[0]
    sm = {n: jnp.stack(v, axis=0) for n, v in sm.items()}
    return loss, dx.reshape(B, S, D_MODEL), sm


def kernel(x, positions, ffn1_norm, ffn1_w_gate, ffn1_w_up, ffn1_w_down, mix_norm, w_in, mlstm_gate_bias, attn_q_norm, attn_k_norm, attn_sink, mlstm_conv_w, mlstm_conv_b, mlstm_out_norm, w_branch_attn, w_branch_mlstm, w_out, ffn2_norm, ffn2_w_gate, ffn2_w_up, ffn2_w_down, block_out_norm, loss_target, m_ffn1_norm, m_ffn1_w_gate, m_ffn1_w_up, m_ffn1_w_down, m_mix_norm, m_w_in, m_mlstm_gate_bias, m_attn_q_norm, m_attn_k_norm, m_attn_sink, m_mlstm_conv_w, m_mlstm_conv_b, m_mlstm_out_norm, m_w_branch_attn, m_w_branch_mlstm, m_w_out, m_ffn2_norm, m_ffn2_w_gate, m_ffn2_w_up, m_ffn2_w_down, m_block_out_norm, v_ffn1_norm, v_ffn1_w_gate, v_ffn1_w_up, v_ffn1_w_down, v_mix_norm, v_w_in, v_mlstm_gate_bias, v_attn_q_norm, v_attn_k_norm, v_attn_sink, v_mlstm_conv_w, v_mlstm_conv_b, v_mlstm_out_norm, v_w_branch_attn, v_w_branch_mlstm, v_w_out, v_ffn2_norm, v_ffn2_w_gate, v_ffn2_w_up, v_ffn2_w_down, v_block_out_norm):
    args = locals()
    def stored(n, t):
        return t.transpose(0, 2, 1) if n in TRANSPOSED else t

    w = {n: stored(n, args[n]) for n in WEIGHTS}
    m = {n: stored(n, args["m_" + n]) for n in WEIGHTS}
    v = {n: stored(n, args["v_" + n]) for n in WEIGHTS}

    order = [(l, grp) for l in range(DEPTH) for grp in GATHER_GROUPS]
    keys = [(l, n) for l, grp in order for n in GATHER_GROUPS[grp]]
    lays = [LAYOUTS[n] for _, n in keys]
    shards = [lay.pad(w[n][l].astype(BF16)) for (l, n), lay in zip(keys, lays)]
    group_idx, at = {}, 0
    for l, grp in order:
        group_idx[(l, grp)] = list(range(at, at + len(GATHER_GROUPS[grp])))
        at += len(GATHER_GROUPS[grp])
    conv_shape = w["mlstm_conv_w"].shape
    conv_all = _all_gather("conv_all_gather", _pack_flat([w["mlstm_conv_w"]], F32, 8), vmem=True)
    conv_parts = _unpack_flat(conv_all, [conv_shape], lead=(N_DEV,))[0]
    conv_w = jnp.concatenate([conv_parts[j] for j in range(N_DEV)], axis=2)
    small = {n: w[n] for n in SMALL}

    lands = []
    for l, grp in order:
        idx = group_idx[(l, grp)]
        lands += _place_own("weights_place_" + grp, [shards[i] for i in idx], [lays[i] for i in idx])
    sems, shards, lands = _gather_start("weights_gather_start", shards, lands, lays, [group_idx[k] for k in order], conv_all)

    def weights_of(l, grp, after):
        idx = group_idx[(l, grp)]
        whole = _gather_wait(f"weights_gather_wait_{l}_{grp}", sems[order.index((l, grp))], [shards[i] for i in idx],
                             [lands[i] for i in idx], [lays[i] for i in idx], after)
        return dict(zip(GATHER_GROUPS[grp], whole))

    totals, pending = {}, []

    def finish(after):
        tag, names = pending[0][0], pending[0][1]
        for n, t in zip(names, _reduce_scatter_finish(pending.pop(0), after)):
            totals[(tag, n)] = t

    def on_grads(l, grp, g, after):
        if pending:
            finish(after)
        names = GROUPS[grp]
        pending.append(_reduce_scatter_start(f"grads_{l}_{grp}", names, [g[n] for n in names]))
        return pending[-1][3][0]

    loss, grad_x, small_g = _local_step(x, positions, loss_target, weights_of, small, conv_w, on_grads)
    finish(grad_x)
    grads, deltas, new_m, new_v = {}, {}, {}, {}
    for grp, names in GROUPS.items():
        for n in names:
            grads[n], deltas[n], new_m[n], new_v[n] = _adamw_layers(
                "adamw_" + n, w[n], [totals[(f"grads_{l}_{grp}", n)] for l in range(DEPTH)], m[n], v[n])

    small_names = SMALL + ("mlstm_conv_w",)
    small_shapes = [small_g[n].shape for n in small_names] + [(1, 1)]
    small_packed = _pack_flat([small_g[n] for n in small_names] + [loss], F32, 8)
    small_all = _all_gather("small_all_gather", small_packed, vmem=True)
    small_sum = _sum_slots("small_sum", small_all, N_DEV)
    *small_grads, loss_total = _unpack_flat(small_sum, small_shapes)
    grads.update(dict(zip(small_names, small_grads)))
    x_pos, y_pos, c_pos = _mesh_pos()
    grads["mlstm_conv_w"] = lax.dynamic_slice_in_dim(
        grads["mlstm_conv_w"], (4 * x_pos + 2 * y_pos + c_pos) * conv_shape[2], conv_shape[2], axis=2)

    n = "mlstm_conv_w"
    deltas[n], new_m[n], new_v[n] = _adamw("adamw_" + n, w[n], grads[n], m[n], v[n])
    sw, sg, smm, sv = (_pack_flat([d[n] for n in SMALL], F32, 8) for d in (w, grads, m, v))
    sd, snm, snv = _adamw("adamw_small", sw, sg, smm, sv)
    shapes = [w[n].shape for n in SMALL]
    for d, buf in ((deltas, sd), (new_m, snm), (new_v, snv)):
        d.update(dict(zip(SMALL, _unpack_flat(buf, shapes))))

    return (loss_total.reshape(()), grad_x, *[stored(n, d[n]) for d in (grads, deltas, new_m, new_v) for n in WEIGHTS])
```

```python
import functools

import numpy as np
import jax
import jax.numpy as jnp
from jax import lax
from jax.experimental import pallas as pl
from jax.experimental.pallas import tpu as pltpu

F32 = jnp.float32
BF16 = jnp.bfloat16

D_MODEL = 1024
D_FF = 2816
ATT_HEAD_DIM = 64
ATT_HEADS = 8
ATT_KV_HEADS = 2
ATT_GROUP = ATT_HEADS // ATT_KV_HEADS
ATT_WIDTH = ATT_HEADS * ATT_HEAD_DIM
ATT_KV_WIDTH = ATT_KV_HEADS * ATT_HEAD_DIM
WINDOW = 128
ATT_BLOCK = 128
ROPE_DIM = 16
ROPE_THETA = 500000.0
MLSTM_HEADS = 4
MLSTM_HEAD_DIM = 128
MLSTM_WIDTH = MLSTM_HEADS * MLSTM_HEAD_DIM
MLSTM_CHUNK = 128
MLSTM_N_GATES = 4 * MLSTM_HEADS
NORM_EPS = 1e-6
IN_WIDTH = 4880
DEPTH = 2
N_DEV = 8

ADAM_LR = 0.001
ADAM_B1 = 0.9
ADAM_B2 = 0.999
ADAM_EPS = 1e-08
ADAM_WD = 0.01
ADAM_STEP = 10

LANES = 128
C_GMERGE = 0
C_QK = 2048
C_VM = 3072
C_OM = 3584
C_QA = 4096
C_KA = 4608
C_VA = 4736
C_GATES = 4864
IN_PAD = 4992

VMEM_LIMIT = 48 * 1024 * 1024

MESH = pl.DeviceIdType.MESH


def _cparams(sem):
    return pltpu.CompilerParams(dimension_semantics=sem, vmem_limit_bytes=VMEM_LIMIT)


def _first_divisor(n, cands):
    for c in cands:
        if n % c == 0:
            return c
    return n


_NN = ((1,), (0,))
_NT = ((1,), (1,))
_TN = ((0,), (0,))


def _mm(a, b, dims):
    return lax.dot_general(a.astype(BF16), b.astype(BF16), (dims, ((), ())), preferred_element_type=F32)


@jax.custom_vjp
def mm_nn(a, b):
    return _mm(a, b, _NN)


def _mm_nn_fwd(a, b):
    return _mm(a, b, _NN), (a, b)


def _mm_nn_bwd(res, g):
    a, b = res
    return _mm(g, b, _NT).astype(a.dtype), _mm(a, g, _TN).astype(b.dtype)


mm_nn.defvjp(_mm_nn_fwd, _mm_nn_bwd)


@jax.custom_vjp
def mm_nt(a, b):
    return _mm(a, b, _NT)


def _mm_nt_fwd(a, b):
    return _mm(a, b, _NT), (a, b)


def _mm_nt_bwd(res, g):
    a, b = res
    return _mm(g, b, _NN).astype(a.dtype), _mm(g, a, _TN).astype(b.dtype)


mm_nt.defvjp(_mm_nt_fwd, _mm_nt_bwd)


@jax.custom_vjp
def mm_tn(a, b):
    return _mm(a, b, _TN)


def _mm_tn_fwd(a, b):
    return _mm(a, b, _TN), (a, b)


def _mm_tn_bwd(res, g):
    a, b = res
    return _mm(b, g, _NT).astype(a.dtype), _mm(a, g, _NN).astype(b.dtype)


mm_tn.defvjp(_mm_tn_fwd, _mm_tn_bwd)


def _matmul(name, a, b, mode, out_dtype=F32, res=None, scale=1.0, bl=None, dep=None):
    b_shape = b.shape if bl is None else b.shape[1:]
    if mode == "nn":
        (M, K), (K2, N) = a.shape, b_shape
    elif mode == "nt":
        (M, K), (N, K2) = a.shape, b_shape
    else:
        (K, M), (K2, N) = a.shape, b_shape
    assert K == K2, (name, a.shape, b.shape)
    tm = _first_divisor(M, (1024, 512, 384, 256, 128))
    tn = _first_divisor(N, (1024, 1664, 512, 384, 256, 128))
    tk = _first_divisor(K, (1024, 1664, 512, 256, 128))
    nk = K // tk
    if mode == "tn":
        a_spec = pl.BlockSpec((tk, tm), lambda i, j, k: (k, i))
    else:
        a_spec = pl.BlockSpec((tm, tk), lambda i, j, k: (i, k))
    if mode == "nt":
        b_blk, b_idx = (tn, tk), (lambda i, j, k: (j, k))
    else:
        b_blk, b_idx = (tk, tn), (lambda i, j, k: (k, j))
    if bl is None:
        b_spec = pl.BlockSpec(b_blk, b_idx)
    else:
        b_spec = pl.BlockSpec((None,) + b_blk, lambda i, j, k: (bl,) + b_idx(i, j, k))
    o_spec = pl.BlockSpec((tm, tn), lambda i, j, k: (i, j))
    dims = {"nn": _NN, "nt": _NT, "tn": _TN}[mode]
    has_res = res is not None

    def body(*refs):
        a_ref, b_ref = refs[:2]
        r_ref = refs[2] if has_res else None

        def finish(out):
            if scale != 1.0:
                out = out * scale
            if has_res:
                out = r_ref[...].astype(F32) + out
            o_ref[...] = out.astype(out_dtype)

        if nk == 1:
            o_ref = refs[-1]
            finish(_mm(a_ref[...], b_ref[...], dims))
            return
        o_ref, acc = refs[-2:]
        k = pl.program_id(2)

        @pl.when(k == 0)
        def _():
            acc[...] = jnp.zeros_like(acc)

        acc[...] += _mm(a_ref[...], b_ref[...], dims)

        @pl.when(k == nk - 1)
        def _():
            finish(acc[...])

    in_specs = [a_spec, b_spec] + ([o_spec] if has_res else [])
    args = (a, b) + ((res,) if has_res else ())
    if dep is not None:
        in_specs.append(pl.BlockSpec(memory_space=pl.ANY))
        args += (dep,)
    return pl.pallas_call(
        body, name=name, grid=(M // tm, N // tn, nk), in_specs=in_specs, out_specs=o_spec,
        out_shape=jax.ShapeDtypeStruct((M, N), out_dtype),
        scratch_shapes=[pltpu.VMEM((tm, tn), F32)] if nk > 1 else [],
        compiler_params=_cparams(("parallel", "parallel", "arbitrary")),
    )(*args)


class _In:
    def __init__(self, arr, width=None, base=0, split=False, rows=True):
        self.arr, self.base, self.split, self.rows = arr, base, split, rows
        self.width = arr.shape[1] if width is None else width


class _Out:
    def __init__(self, cols, dtype=F32, width=None, split=False, rows=True, nrows=1):
        self.cols, self.dtype, self.split, self.rows, self.nrows = cols, dtype, split, rows, nrows
        self.width = cols if width is None else width


def _rowwise(name, fn, ins, outs, n_rows, br, ncol=1):
    br = min(br, n_rows)
    assert n_rows % br == 0, (name, n_rows, br)
    nrow_blocks = n_rows // br

    def in_spec(d):
        nb = br if d.rows else d.arr.shape[0]
        if d.rows and d.split:
            im = lambda j, i, base=d.base: (i, base + j)
        elif d.rows:
            im = lambda j, i, base=d.base: (i, base)
        elif d.split:
            im = lambda j, i, base=d.base: (0, base + j)
        else:
            im = lambda j, i, base=d.base: (0, base)
        return pl.BlockSpec((nb, d.width), im)

    def out_spec(d):
        nb = br if d.rows else d.nrows
        if d.rows and d.split:
            im = lambda j, i: (i, j)
        elif d.rows:
            im = lambda j, i: (i, 0)
        elif d.split:
            im = lambda j, i: (0, j)
        else:
            im = lambda j, i: (0, 0)
        return pl.BlockSpec((nb, d.width), im)

    n_in = len(ins)

    def body(*refs):
        i = pl.program_id(1)
        vals = [r[...] for r in refs[:n_in]]
        res = fn(*vals)
        if not isinstance(res, (tuple, list)):
            res = (res,)
        for d, ref, val in zip(outs, refs[n_in:], res):
            if d.rows:
                ref[...] = val.astype(d.dtype)
            else:
                @pl.when(i == 0)
                def _(ref=ref):
                    ref[...] = jnp.zeros_like(ref)

                ref[...] += val.astype(d.dtype)

    out_shape = [jax.ShapeDtypeStruct((n_rows if d.rows else d.nrows, d.cols), d.dtype) for d in outs]
    res = pl.pallas_call(
        body, name=name, grid=(ncol, nrow_blocks), in_specs=[in_spec(d) for d in ins],
        out_specs=[out_spec(d) for d in outs], out_shape=out_shape,
        compiler_params=_cparams(("parallel", "arbitrary")),
    )(*[d.arr for d in ins])
    return res


def _rms(x, g):
    return x * lax.rsqrt(jnp.mean(x * x, axis=-1, keepdims=True) + NORM_EPS) * g


def _sigmoid(x):
    return 0.5 * jnp.tanh(0.5 * x) + 0.5


def _silu(x):
    return x * _sigmoid(x)


def _log_sigmoid(x):
    return jnp.minimum(x, 0.0) - jnp.log(1.0 + jnp.exp(-jnp.abs(x)))


def _rope_tables(pos, inv_freq_row):
    ang = pos.astype(F32) * inv_freq_row
    return jnp.cos(ang), jnp.sin(ang)


def _head_sums_impl(v):
    w = v.shape[-1]
    shift = ATT_HEAD_DIM.bit_length() - 1
    r = lax.shift_right_logical(lax.broadcasted_iota(jnp.int32, (w, w), 0), shift)
    c = lax.shift_right_logical(lax.broadcasted_iota(jnp.int32, (w, w), 1), shift)
    ones = (r == c).astype(BF16)
    hi = v.astype(BF16)
    lo = (v - hi.astype(F32)).astype(BF16)
    dn = (_NN, ((), ()))
    return (lax.dot_general(hi, ones, dn, preferred_element_type=F32)
            + lax.dot_general(lo, ones, dn, preferred_element_type=F32))


@jax.custom_vjp
def _head_sums(v):
    return _head_sums_impl(v)


_head_sums.defvjp(lambda v: (_head_sums_impl(v), None), lambda _, g: (_head_sums_impl(g),))


def _rotate_half_impl(y):
    w = y.shape[-1]
    half = ROPE_DIM // 2
    lane = lax.broadcasted_iota(jnp.int32, y.shape, 1) & (ATT_HEAD_DIM - 1)
    above = pltpu.roll(y, w - half, axis=1)
    below = pltpu.roll(y, half, axis=1)
    return jnp.where(lane < half, -above, jnp.where(lane < ROPE_DIM, below, 0.0))


@jax.custom_vjp
def _rotate_half(y):
    return _rotate_half_impl(y)


_rotate_half.defvjp(lambda y: (_rotate_half_impl(y), None), lambda _, g: (-_rotate_half_impl(g),))


def _qk_prep(t, g, cos, sin):
    reps = t.shape[-1] // cos.shape[-1]
    if reps > 1:
        cos, sin = jnp.tile(cos, (1, reps)), jnp.tile(sin, (1, reps))
    y = t * lax.rsqrt(_head_sums(t * t) * (1.0 / ATT_HEAD_DIM) + NORM_EPS) * g
    return y * cos + _rotate_half(y) * sin


def _attn_head(q, kb, vb, sink, valid):
    s = mm_nt(q, kb) * (ATT_HEAD_DIM ** -0.5)
    s = jnp.where(valid, s, -jnp.inf)
    m = jnp.maximum(jnp.max(s, axis=-1, keepdims=True), sink)
    p = jnp.exp(s - m)
    den = jnp.sum(p, axis=-1, keepdims=True) + jnp.exp(sink - m)
    return mm_nn(p * (1.0 / den), vb)


def _mlstm_chunk(q, k, v, li, lf, C, n, m, incl, incl_t, eye):
    k = k * (MLSTM_HEAD_DIM ** -0.5)
    lf_row = jnp.sum(eye * lf, axis=0, keepdims=True)
    li_row = jnp.sum(eye * li, axis=0, keepdims=True)
    b = jnp.sum(incl * lf_row, axis=1, keepdims=True)
    b_row = jnp.sum(incl_t * lf, axis=0, keepdims=True)
    b_tot = jnp.sum(lf, axis=0, keepdims=True)
    a = b_tot - b + li
    a_max = jnp.max(a, axis=0, keepdims=True)
    kw = k * jnp.exp(a - a_max)
    c_loc = mm_tn(kw, v)
    n_loc = jnp.sum(kw, axis=0, keepdims=True)

    dmat = jnp.where(incl > 0.5, b - b_row + li_row, -jnp.inf)
    inter = b + m
    m_t = jnp.maximum(inter, jnp.max(dmat, axis=1, keepdims=True))
    sc = mm_nt(q, k) * jnp.exp(dmat - m_t)
    scale_in = jnp.exp(inter - m_t)
    num = mm_nn(sc, v) + scale_in * mm_nn(q, C)
    den = jnp.sum(sc, axis=1, keepdims=True) + scale_in * jnp.sum(q * n, axis=1, keepdims=True)
    h = num * (1.0 / jnp.maximum(jnp.abs(den), jnp.exp(-m_t)))

    m_new = jnp.maximum(b_tot + m, a_max)
    s_p = jnp.exp(b_tot + m - m_new)
    s_l = jnp.exp(a_max - m_new)
    return h, s_p * C + s_l * c_loc, s_p * n + s_l * n_loc, m_new


def _mlstm_combine(hf, hb, o_pre, g):
    h = hf + hb
    mu = jnp.mean(h, axis=-1, keepdims=True)
    var = jnp.mean(jnp.square(h - mu), axis=-1, keepdims=True)
    return _sigmoid(o_pre) * ((h - mu) * lax.rsqrt(var + NORM_EPS) * g)


def _merge(ga, gm, za, zm):
    return _sigmoid(ga) * za + _sigmoid(gm) * zm


def _attn_mask(n, seq):
    shape = (ATT_GROUP * ATT_BLOCK, 3 * ATT_BLOCK)
    qi = n * ATT_BLOCK + (lax.broadcasted_iota(jnp.int32, shape, 0) & (ATT_BLOCK - 1))
    kj = (n - 1) * ATT_BLOCK + lax.broadcasted_iota(jnp.int32, shape, 1)
    return (jnp.abs(qi - kj) <= WINDOW) & (kj >= 0) & (kj < seq)


def _attn_specs(nq, v_base):
    q_spec = pl.BlockSpec((1, ATT_BLOCK, ATT_WIDTH), lambda b, n: (b, n, 0))

    def kv_spec(off, base=0):
        return pl.BlockSpec((1, ATT_BLOCK, ATT_KV_WIDTH), lambda b, n: (b, jnp.clip(n + off, 0, nq - 1), base))

    sink_spec = pl.BlockSpec((ATT_KV_HEADS, ATT_GROUP, 1, 1), lambda b, n: (0, 0, 0, 0))
    specs = [q_spec, kv_spec(-1), kv_spec(0), kv_spec(1), kv_spec(-1, v_base), kv_spec(0, v_base), kv_spec(1, v_base), sink_spec]
    return q_spec, specs, sink_spec


def _head(h):
    return slice(h * ATT_HEAD_DIM, (h + 1) * ATT_HEAD_DIM)


def _group_rows(q_ref, s_ref, h):
    q4 = jnp.concatenate([q_ref[0, :, _head(h * ATT_GROUP + g)] for g in range(ATT_GROUP)], axis=0)
    sink4 = jnp.concatenate([jnp.broadcast_to(s_ref[h, g], (ATT_BLOCK, 1)) for g in range(ATT_GROUP)], axis=0)
    return q4, sink4


def _attn_fwd(q, k, proj3, sink):
    B, S, _ = q.shape
    nq = S // ATT_BLOCK
    q_spec, specs, _ = _attn_specs(nq, C_VA // ATT_KV_WIDTH)

    def body(q_ref, kp, kc, kn, vp, vc, vn, s_ref, o_ref):
        valid = _attn_mask(pl.program_id(1), S)
        for h in range(ATT_KV_HEADS):
            kb = jnp.concatenate([kp[0, :, _head(h)], kc[0, :, _head(h)], kn[0, :, _head(h)]], axis=0)
            vb = jnp.concatenate([vp[0, :, _head(h)], vc[0, :, _head(h)], vn[0, :, _head(h)]], axis=0)
            q4, sink4 = _group_rows(q_ref, s_ref, h)
            o4 = _attn_head(q4, kb, vb, sink4, valid).astype(BF16)
            for g in range(ATT_GROUP):
                o_ref[0, :, _head(h * ATT_GROUP + g)] = o4[g * ATT_BLOCK:(g + 1) * ATT_BLOCK]

    return pl.pallas_call(
        body, name="attn_fwd", grid=(B, nq), in_specs=specs,
        out_specs=q_spec, out_shape=jax.ShapeDtypeStruct(q.shape, BF16),
        compiler_params=_cparams(("parallel", "arbitrary")),
    )(q, k, k, k, proj3, proj3, proj3, sink)


def _attn_bwd(q, k, proj3, sink, dy):
    B, S, _ = q.shape
    nq = S // ATT_BLOCK
    q_spec, specs, sink_spec = _attn_specs(nq, C_VA // ATT_KV_WIDTH)
    kv_full = pl.BlockSpec((1, S, ATT_KV_WIDTH), lambda b, n: (b, 0, 0))

    def body(q_ref, kp, kc, kn, vp, vc, vn, s_ref, dy_ref, dq_ref, dk_ref, dv_ref, ds_ref):
        b, n = pl.program_id(0), pl.program_id(1)
        valid = _attn_mask(n, S)

        @pl.when(n == 0)
        def _():
            dk_ref[...] = jnp.zeros_like(dk_ref)
            dv_ref[...] = jnp.zeros_like(dv_ref)

        @pl.when((n == 0) & (b == 0))
        def _():
            ds_ref[...] = jnp.zeros_like(ds_ref)

        for h in range(ATT_KV_HEADS):
            kb = jnp.concatenate([kp[0, :, _head(h)], kc[0, :, _head(h)], kn[0, :, _head(h)]], axis=0)
            vb = jnp.concatenate([vp[0, :, _head(h)], vc[0, :, _head(h)], vn[0, :, _head(h)]], axis=0)
            q4, sink4 = _group_rows(q_ref, s_ref, h)
            dy4 = jnp.concatenate([dy_ref[0, :, _head(h * ATT_GROUP + g)] for g in range(ATT_GROUP)], axis=0)
            _, vjp = jax.vjp(functools.partial(_attn_head, valid=valid), q4, kb, vb, sink4)
            dq4, dkb, dvb, dsink4 = vjp(dy4)
            for g in range(ATT_GROUP):
                rows = slice(g * ATT_BLOCK, (g + 1) * ATT_BLOCK)
                dq_ref[0, :, _head(h * ATT_GROUP + g)] = dq4[rows]
                ds_ref[h, g] += jnp.sum(dsink4[rows], axis=0, keepdims=True)
            for j, off in enumerate((-1, 0, 1)):
                start = pl.multiple_of(jnp.clip(n + off, 0, nq - 1) * ATT_BLOCK, ATT_BLOCK)
                rows = pl.ds(start, ATT_BLOCK)
                dk_ref[0, rows, _head(h)] += dkb[j * ATT_BLOCK:(j + 1) * ATT_BLOCK]
                dv_ref[0, rows, _head(h)] += dvb[j * ATT_BLOCK:(j + 1) * ATT_BLOCK]

    kv_shape = jax.ShapeDtypeStruct(k.shape, F32)
    return pl.pallas_call(
        body, name="attn_bwd", grid=(B, nq), in_specs=specs + [q_spec],
        out_specs=[q_spec, kv_full, kv_full, sink_spec],
        out_shape=[jax.ShapeDtypeStruct(q.shape, F32), kv_shape, kv_shape, jax.ShapeDtypeStruct(sink.shape, F32)],
        compiler_params=_cparams(("arbitrary", "arbitrary")),
    )(q, k, k, k, proj3, proj3, proj3, sink, dy)


CONV_COLS = 256


def _conv_taps(u, seq):
    row = lax.broadcasted_iota(jnp.int32, u.shape, 0)
    prev = jnp.where(row == 0, 0.0, pltpu.roll(u, 1, axis=0))
    nxt = jnp.where(row == seq - 1, 0.0, pltpu.roll(u, seq - 1, axis=0))
    return prev, nxt


def _conv_fwd(proj3, w8):
    B, S, _ = proj3.shape
    ncb = 2 * MLSTM_WIDTH // CONV_COLS

    def body(u_ref, w_ref, o_ref):
        u = u_ref[0]
        prev, nxt = _conv_taps(u, S)
        o_ref[0] = _silu(prev * w_ref[0:1, :] + u * w_ref[1:2, :] + nxt * w_ref[2:3, :] + w_ref[3:4, :])

    return pl.pallas_call(
        body, name="conv_fwd", grid=(B, ncb),
        in_specs=[pl.BlockSpec((1, S, CONV_COLS), lambda b, c: (b, 0, C_QK // CONV_COLS + c)),
                  pl.BlockSpec((8, CONV_COLS), lambda b, c: (0, c))],
        out_specs=pl.BlockSpec((1, S, CONV_COLS), lambda b, c: (b, 0, c)),
        out_shape=jax.ShapeDtypeStruct((B, S, 2 * MLSTM_WIDTH), F32),
        compiler_params=_cparams(("parallel", "parallel")),
    )(proj3, w8)


def _conv_bwd(proj3, w8, dout_f, dout_b):
    B, S, _ = proj3.shape
    ncb = 2 * MLSTM_WIDTH // CONV_COLS

    def body(u_ref, w_ref, df_ref, db_ref, du_ref, dw_ref):
        b = pl.program_id(1)
        u = u_ref[0]
        prev, nxt = _conv_taps(u, S)
        w0, w1, w2 = w_ref[0:1, :], w_ref[1:2, :], w_ref[2:3, :]
        pre = prev * w0 + u * w1 + nxt * w2 + w_ref[3:4, :]
        sig = _sigmoid(pre)
        dpre = (df_ref[0] + db_ref[0]) * (sig * (1.0 + pre * (1.0 - sig)))
        dprev, dnxt = _conv_taps(dpre, S)
        du_ref[0] = (dnxt * w0 + dpre * w1 + dprev * w2).astype(BF16)

        @pl.when(b == 0)
        def _():
            dw_ref[...] = jnp.zeros_like(dw_ref)

        dw_ref[0:1, :] += jnp.sum(dpre * prev, axis=0, keepdims=True)
        dw_ref[1:2, :] += jnp.sum(dpre * u, axis=0, keepdims=True)
        dw_ref[2:3, :] += jnp.sum(dpre * nxt, axis=0, keepdims=True)
        dw_ref[3:4, :] += jnp.sum(dpre, axis=0, keepdims=True)

    blk = pl.BlockSpec((1, S, CONV_COLS), lambda c, b: (b, 0, c))
    return pl.pallas_call(
        body, name="conv_bwd", grid=(ncb, B),
        in_specs=[pl.BlockSpec((1, S, CONV_COLS), lambda c, b: (b, 0, C_QK // CONV_COLS + c)),
                  pl.BlockSpec((8, CONV_COLS), lambda c, b: (0, c)), blk, blk],
        out_specs=[blk, pl.BlockSpec((8, CONV_COLS), lambda c, b: (0, c))],
        out_shape=[jax.ShapeDtypeStruct((B, S, 2 * MLSTM_WIDTH), BF16), jax.ShapeDtypeStruct((8, 2 * MLSTM_WIDTH), F32)],
        compiler_params=_cparams(("parallel", "arbitrary")),
    )(proj3, w8, dout_f, dout_b)


MLSTM_HEADS_PER_STEP = 4


def _chunk_masks(direction):
    t = lax.broadcasted_iota(jnp.int32, (MLSTM_CHUNK, MLSTM_CHUNK), 0)
    s = lax.broadcasted_iota(jnp.int32, (MLSTM_CHUNK, MLSTM_CHUNK), 1)
    le, ge = (s <= t).astype(F32), (s >= t).astype(F32)
    eye = (s == t).astype(F32)
    return (le, ge, eye) if direction == 0 else (ge, le, eye)


def _gate_cols(gates, direction, head):
    lane = lax.broadcasted_iota(jnp.int32, gates.shape, 1)
    sel_i = (lane == (2 * direction) * MLSTM_HEADS + head).astype(F32)
    sel_f = (lane == (2 * direction + 1) * MLSTM_HEADS + head).astype(F32)
    return sel_i, sel_f


def _mlstm_fwd(qk, proj3, bias):
    B, S, _ = qk.shape
    nc = S // MLSTM_CHUNK
    H, L, DH = MLSTM_HEADS, MLSTM_CHUNK, MLSTM_HEAD_DIM

    def chunk_of(d, c):
        return c if d == 0 else nc - 1 - c

    HS = MLSTM_HEADS_PER_STEP

    def body(qkf, qkb, vf, vb, gf, gb, bias_ref, hf, hb, csf, csb, nsf, nsb, msf, msb, c_st, n_st, m_st):
        c, hg = pl.program_id(1), pl.program_id(2)

        @pl.when(c == 0)
        def _():
            for d in range(2):
                for j in range(HS):
                    c_st[d, hg * HS + j] = jnp.zeros((DH, DH), F32)
                    n_st[d, hg * HS + j] = jnp.zeros((1, DH), F32)
                    m_st[d, hg * HS + j] = jnp.zeros((1, DH), F32)

        for d, (qk_ref, v_ref, g_ref, h_ref, cs, ns, ms) in enumerate(
                ((qkf, vf, gf, hf, csf, nsf, msf), (qkb, vb, gb, hb, csb, nsb, msb))):
            incl, incl_t, eye = _chunk_masks(d)
            gates = g_ref[0] + bias_ref[...]
            log_f = _log_sigmoid(gates)
            for j in range(HS):
                h = hg * HS + j
                sel_i, sel_f = _gate_cols(gates, d, h)
                li = jnp.sum(gates * sel_i, axis=1, keepdims=True)
                lf = jnp.sum(log_f * sel_f, axis=1, keepdims=True)
                c_in, n_in, m_in = c_st[d, h], n_st[d, h], m_st[d, h]
                cs[0, 0, j], ns[0, 0, j], ms[0, 0, j] = c_in, n_in, m_in
                hh, c_new, n_new, m_new = _mlstm_chunk(
                    qk_ref[0, :, 2 * j * DH:(2 * j + 1) * DH], qk_ref[0, :, (2 * j + 1) * DH:(2 * j + 2) * DH],
                    v_ref[0, :, j * DH:(j + 1) * DH], li, lf, c_in, n_in,
                    jnp.max(m_in, axis=1, keepdims=True), incl, incl_t, eye)
                h_ref[0, :, j * DH:(j + 1) * DH] = hh
                c_st[d, h], n_st[d, h] = c_new, n_new
                m_st[d, h] = jnp.broadcast_to(m_new, (1, DH))

    def tok_spec(width, base, d, per_head):
        return pl.BlockSpec((1, L, width), lambda b, c, h: (b, chunk_of(d, c), base + (h if per_head else 0)))

    def st_spec(shape, d):
        return pl.BlockSpec((1, 1, HS) + shape, lambda b, c, h: (b, chunk_of(d, c), h, 0, 0))

    in_specs = [tok_spec(2 * HS * DH, 0, 0, True), tok_spec(2 * HS * DH, 0, 1, True),
                tok_spec(HS * DH, C_VM // (HS * DH), 0, True), tok_spec(HS * DH, C_VM // (HS * DH), 1, True),
                tok_spec(LANES, C_GATES // LANES, 0, False), tok_spec(LANES, C_GATES // LANES, 1, False),
                pl.BlockSpec((1, LANES), lambda b, c, h: (0, 0))]
    out_specs = [tok_spec(HS * DH, 0, 0, True), tok_spec(HS * DH, 0, 1, True),
                 st_spec((DH, DH), 0), st_spec((DH, DH), 1), st_spec((1, DH), 0), st_spec((1, DH), 1),
                 st_spec((1, DH), 0), st_spec((1, DH), 1)]
    hs = jax.ShapeDtypeStruct((B, S, H * DH), F32)
    cs = jax.ShapeDtypeStruct((B, nc, H, DH, DH), F32)
    vs = jax.ShapeDtypeStruct((B, nc, H, 1, DH), F32)
    return pl.pallas_call(
        body, name="mlstm_fwd", grid=(B, nc, H // HS), in_specs=in_specs, out_specs=out_specs,
        out_shape=[hs, hs, cs, cs, vs, vs, vs, vs],
        scratch_shapes=[pltpu.VMEM((2, H, DH, DH), F32), pltpu.VMEM((2, H, 1, DH), F32), pltpu.VMEM((2, H, 1, DH), F32)],
        compiler_params=_cparams(("parallel", "arbitrary", "arbitrary")),
    )(qk, qk, proj3, proj3, proj3, proj3, bias)


def _mlstm_bwd(qk, proj3, bias, states, dh):
    B, S, _ = qk.shape
    nc = S // MLSTM_CHUNK
    H, L, DH = MLSTM_HEADS, MLSTM_CHUNK, MLSTM_HEAD_DIM

    def chunk_of(d, c):
        return nc - 1 - c if d == 0 else c

    HS = MLSTM_HEADS_PER_STEP

    def body(qkf, qkb, vf, vb, gf, gb, bias_ref, csf, csb, nsf, nsb, msf, msb, dhf, dhb,
             dqkf, dqkb, dvf, dvb, dgf, dgb, dc_st, dn_st, dm_st):
        c, hg = pl.program_id(1), pl.program_id(2)

        @pl.when(c == 0)
        def _():
            for d in range(2):
                for j in range(HS):
                    dc_st[d, hg * HS + j] = jnp.zeros((DH, DH), F32)
                    dn_st[d, hg * HS + j] = jnp.zeros((1, DH), F32)
                    dm_st[d, hg * HS + j] = jnp.zeros((1, DH), F32)

        @pl.when(hg == 0)
        def _():
            dgf[...] = jnp.zeros_like(dgf)
            dgb[...] = jnp.zeros_like(dgb)

        for d, (qk_ref, v_ref, g_ref, cs, ns, ms, dh_ref, dqk_ref, dv_ref, dg_ref) in enumerate(
                ((qkf, vf, gf, csf, nsf, msf, dhf, dqkf, dvf, dgf), (qkb, vb, gb, csb, nsb, msb, dhb, dqkb, dvb, dgb))):
            incl, incl_t, eye = _chunk_masks(d)
            gates = g_ref[0] + bias_ref[...]
            log_f = _log_sigmoid(gates)
            d_li = jnp.zeros_like(gates)
            d_lf = jnp.zeros_like(gates)
            for j in range(HS):
                h = hg * HS + j
                sel_i, sel_f = _gate_cols(gates, d, h)
                li = jnp.sum(gates * sel_i, axis=1, keepdims=True)
                lf = jnp.sum(log_f * sel_f, axis=1, keepdims=True)
                m_in = jnp.max(ms[0, 0, j], axis=1, keepdims=True)
                _, vjp = jax.vjp(
                    functools.partial(_mlstm_chunk, incl=incl, incl_t=incl_t, eye=eye),
                    qk_ref[0, :, 2 * j * DH:(2 * j + 1) * DH], qk_ref[0, :, (2 * j + 1) * DH:(2 * j + 2) * DH],
                    v_ref[0, :, j * DH:(j + 1) * DH], li, lf, cs[0, 0, j], ns[0, 0, j], m_in)
                dm_out = jnp.max(dm_st[d, h], axis=1, keepdims=True)
                dq, dk, dv, dli, dlf, dc, dn, dm = vjp((dh_ref[0, :, j * DH:(j + 1) * DH], dc_st[d, h], dn_st[d, h], dm_out))
                dqk_ref[0, :, 2 * j * DH:(2 * j + 1) * DH] = dq
                dqk_ref[0, :, (2 * j + 1) * DH:(2 * j + 2) * DH] = dk
                dv_ref[0, :, j * DH:(j + 1) * DH] = dv
                d_li += dli * sel_i
                d_lf += dlf * sel_f
                dc_st[d, h], dn_st[d, h] = dc, dn
                dm_st[d, h] = jnp.broadcast_to(dm, (1, DH))
            dg_ref[0] += d_li + d_lf * _sigmoid(-gates)

    def tok_spec(width, base, d, per_head):
        return pl.BlockSpec((1, L, width), lambda b, c, h: (b, chunk_of(d, c), base + (h if per_head else 0)))

    def st_spec(shape, d):
        return pl.BlockSpec((1, 1, HS) + shape, lambda b, c, h: (b, chunk_of(d, c), h, 0, 0))

    in_specs = [tok_spec(2 * HS * DH, 0, 0, True), tok_spec(2 * HS * DH, 0, 1, True),
                tok_spec(HS * DH, C_VM // (HS * DH), 0, True), tok_spec(HS * DH, C_VM // (HS * DH), 1, True),
                tok_spec(LANES, C_GATES // LANES, 0, False), tok_spec(LANES, C_GATES // LANES, 1, False),
                pl.BlockSpec((1, LANES), lambda b, c, h: (0, 0)),
                st_spec((DH, DH), 0), st_spec((DH, DH), 1), st_spec((1, DH), 0), st_spec((1, DH), 1),
                st_spec((1, DH), 0), st_spec((1, DH), 1), tok_spec(HS * DH, 0, 0, True), tok_spec(HS * DH, 0, 1, True)]
    out_specs = [tok_spec(2 * HS * DH, 0, 0, True), tok_spec(2 * HS * DH, 0, 1, True),
                 tok_spec(HS * DH, 0, 0, True), tok_spec(HS * DH, 0, 1, True),
                 tok_spec(LANES, 0, 0, False), tok_spec(LANES, 0, 1, False)]
    qks = jax.ShapeDtypeStruct((B, S, 2 * H * DH), F32)
    vs = jax.ShapeDtypeStruct((B, S, H * DH), F32)
    gs = jax.ShapeDtypeStruct((B, S, LANES), F32)
    csf, csb, nsf, nsb, msf, msb = states
    return pl.pallas_call(
        body, name="mlstm_bwd", grid=(B, nc, H // HS), in_specs=in_specs, out_specs=out_specs,
        out_shape=[qks, qks, vs, vs, gs, gs],
        scratch_shapes=[pltpu.VMEM((2, H, DH, DH), F32), pltpu.VMEM((2, H, 1, DH), F32), pltpu.VMEM((2, H, 1, DH), F32)],
        compiler_params=_cparams(("parallel", "arbitrary", "arbitrary")),
    )(qk, qk, proj3, proj3, proj3, proj3, bias, csf, csb, nsf, nsb, msf, msb, dh, dh)


ROW_BLOCK = 256
FF_COLS = 512
FF_SHARD = D_FF // N_DEV
FF_SHARD_PAD = 384
FF_PAD = N_DEV * FF_SHARD_PAD


def _rms_fwd(name, x, g):
    T = x.shape[0]
    return _rowwise(name, lambda xv, gv: _rms(xv, gv), [_In(x), _In(g, rows=False)], [_Out(D_MODEL, BF16)], T, ROW_BLOCK)[0]


def _rms_bwd(name, x, g, dh, dres):
    T = x.shape[0]

    def fn(xv, gv, dhv, drv):
        _, vjp = jax.vjp(_rms, xv, gv)
        dx, dg = vjp(dhv)
        return drv + dx, dg

    return _rowwise(name, fn, [_In(x), _In(g, rows=False), _In(dh), _In(dres)],
                    [_Out(D_MODEL), _Out(D_MODEL, rows=False)], T, ROW_BLOCK)


def _mmw(name, a, w, mode, **kw):
    if isinstance(w, tuple):
        return _matmul(name, a, w[0], mode, bl=w[1], **kw)
    return _matmul(name, a, w, mode, **kw)


def _swiglu(gate, up):
    return _silu(gate) * up


def _ffn_in(name, h, wg, wu):
    (M, K), N = h.shape, wg.shape[0]
    tm, tn = _first_divisor(M, (1024, 512, 256, 128)), FF_COLS

    def body(h_ref, wg_ref, wu_ref, g_ref, u_ref, a_ref):
        hv = h_ref[...]
        gate = _mm(hv, wg_ref[...], _NT)
        up = _mm(hv, wu_ref[...], _NT)
        g_ref[...], u_ref[...] = gate.astype(BF16), up.astype(BF16)
        a_ref[...] = _swiglu(gate, up).astype(BF16)

    w_spec = pl.BlockSpec((tn, K), lambda i, j: (j, 0))
    o_spec = pl.BlockSpec((tm, tn), lambda i, j: (i, j))
    return pl.pallas_call(
        body, name=name, grid=(M // tm, N // tn), in_specs=[pl.BlockSpec((tm, K), lambda i, j: (i, 0)), w_spec, w_spec],
        out_specs=[o_spec, o_spec, o_spec],
        out_shape=[jax.ShapeDtypeStruct((M, N), BF16)] * 3,
        compiler_params=_cparams(("parallel", "parallel")),
    )(h, wg, wu)


def _ffn_dact(name, dx, wd, gate, up):
    (M, K), N = dx.shape, wd.shape[0]
    tm, tn = _first_divisor(M, (1024, 512, 256, 128)), FF_COLS

    def body(dx_ref, wd_ref, g_ref, u_ref, dg_ref, du_ref):
        dact = _mm(dx_ref[...], wd_ref[...], _NT) * 0.5
        gate, up = g_ref[...].astype(F32), u_ref[...].astype(F32)
        s = _sigmoid(gate)
        silu = gate * s
        dg_ref[...] = (dact * up * (s + silu * (1.0 - s))).astype(BF16)
        du_ref[...] = (dact * silu).astype(BF16)

    o_spec = pl.BlockSpec((tm, tn), lambda i, j: (i, j))
    return pl.pallas_call(
        body, name=name, grid=(M // tm, N // tn),
        in_specs=[pl.BlockSpec((tm, K), lambda i, j: (i, 0)), pl.BlockSpec((tn, K), lambda i, j: (j, 0)), o_spec, o_spec],
        out_specs=[o_spec, o_spec],
        out_shape=[jax.ShapeDtypeStruct((M, N), BF16), jax.ShapeDtypeStruct((M, N), BF16)],
        compiler_params=_cparams(("parallel", "parallel")),
    )(dx, wd, gate, up)


def _ffn_dh(name, dgate, dup, wg, wu, dep, x, gain, dres):
    (M, K), N = dgate.shape, wg.shape[1]
    tm, tk = _first_divisor(M, (512, 256, 128)), _first_divisor(K, (1024, 512, 384, 256, 128))
    nk = K // tk

    def body(dg_ref, du_ref, wg_ref, wu_ref, x_ref, gain_ref, dres_ref, dep_ref, o_ref, dgain_ref, acc):
        i, k = pl.program_id(0), pl.program_id(1)

        @pl.when(k == 0)
        def _():
            acc[...] = jnp.zeros_like(acc)

        acc[...] += _mm(dg_ref[...], wg_ref[...], _NN) + _mm(du_ref[...], wu_ref[...], _NN)

        @pl.when((k == nk - 1) & (i == 0))
        def _():
            dgain_ref[...] = jnp.zeros_like(dgain_ref)

        @pl.when(k == nk - 1)
        def _():
            _, vjp = jax.vjp(_rms, x_ref[...], gain_ref[...])
            dx, dgain = vjp(acc[...])
            o_ref[...] = dres_ref[...] + dx
            dgain_ref[...] += dgain

    a_spec = pl.BlockSpec((tm, tk), lambda i, k: (i, k))
    w_spec = pl.BlockSpec((tk, N), lambda i, k: (k, 0))
    row_spec = pl.BlockSpec((tm, N), lambda i, k: (i, 0))
    gain_spec = pl.BlockSpec((1, N), lambda i, k: (0, 0))
    return pl.pallas_call(
        body, name=name, grid=(M // tm, nk),
        in_specs=[a_spec, a_spec, w_spec, w_spec, row_spec, gain_spec, row_spec, pl.BlockSpec(memory_space=pl.ANY)],
        out_specs=[row_spec, gain_spec],
        out_shape=[jax.ShapeDtypeStruct((M, N), F32), jax.ShapeDtypeStruct((1, N), F32)],
        scratch_shapes=[pltpu.VMEM((tm, N), F32)], compiler_params=_cparams(("arbitrary", "arbitrary")),
    )(dgate, dup, wg, wu, x, gain, dres, dep)


def _ffn_fwd(tag, x, g, wg, wu, wd):
    h = _rms_fwd(tag + "_norm", x, g)
    gate, up, act = _ffn_in(tag + "_in", h, wg, wu)
    if callable(wd):
        wd = wd(act)
    out = _mmw(tag + "_down", act, wd, "nn", res=x, scale=0.5)
    return out, (x, h, gate, up, act), wd


def _ffn_bwd(tag, saved, g, wg, wu, wd, dx, on_dw):
    x, h, gate, up, act = saved
    dgate, dup = _ffn_dact(tag + "_dact", dx, wd, gate, up)
    dwd = _matmul(tag + "_dwd", act, dx, "tn", scale=0.5, out_dtype=BF16)
    dwg = _matmul(tag + "_dwg", dgate, h, "tn", out_dtype=BF16)
    dwu = _matmul(tag + "_dwu", dup, h, "tn", out_dtype=BF16)
    token = on_dw({tag + "_w_gate": dwg, tag + "_w_up": dwu, tag + "_w_down": dwd}, dwu)
    return _ffn_dh(tag + "_dh", dgate, dup, wg, wu, token, x, g, dx)


def _rope_cos_sin(positions):
    half = ROPE_DIM // 2
    inv_freq = jnp.power(jnp.float32(ROPE_THETA), -jnp.arange(half, dtype=F32) * (2.0 / ROPE_DIM))
    head = jnp.zeros((ATT_HEAD_DIM,), F32).at[:ROPE_DIM].set(jnp.concatenate([inv_freq, inv_freq]))
    row = jnp.tile(head, LANES // ATT_HEAD_DIM)[None, :]
    T = positions.shape[0]
    return _rowwise("rope_tables", _rope_tables, [_In(positions), _In(row, rows=False)], [_Out(LANES), _Out(LANES)], T, 1024)


def _prep_fwd(name, src, width, base, g, cos, sin):
    return _rowwise(name, _qk_prep, [_In(src, width, base), _In(g, rows=False), _In(cos), _In(sin)],
                    [_Out(width)], src.shape[0], 512)[0]


def _prep_bwd(name, src, width, base, g, cos, sin, dout):
    def fn(tv, gv, cv, sv, dv):
        _, vjp = jax.vjp(lambda a, b: _qk_prep(a, b, cv, sv), tv, gv)
        return vjp(dv)

    return _rowwise(name, fn, [_In(src, width, base), _In(g, rows=False), _In(cos), _In(sin), _In(dout)],
                    [_Out(width, BF16), _Out(width, rows=False)], src.shape[0], 512)


def _to_heads(t, B, S, nh):
    return t.reshape(B, S, nh, ATT_HEAD_DIM).transpose(0, 2, 1, 3)


def _from_heads(t):
    B, nh, S, _ = t.shape
    return t.transpose(0, 2, 1, 3).reshape(B * S, nh * ATT_HEAD_DIM)


def _mix_fwd(x, cos, sin, B, S, p):
    T = B * S
    h = _rms_fwd("mix_norm", x, p["mix_norm"])
    proj = _matmul("mix_proj", h, p["w_in"], "nn")
    proj3 = proj.reshape(B, S, IN_PAD)
    q_gain = jnp.tile(p["attn_q_norm"], (1, ATT_HEADS))
    k_gain = jnp.tile(p["attn_k_norm"], (1, ATT_KV_HEADS))
    q_r = _prep_fwd("q_prep", proj, ATT_WIDTH, C_QA // ATT_WIDTH, q_gain, cos, sin)
    k_r = _prep_fwd("k_prep", proj, ATT_KV_WIDTH, C_KA // ATT_KV_WIDTH, k_gain, cos, sin)
    qh = q_r.reshape(B, S, ATT_WIDTH)
    kh = k_r.reshape(B, S, ATT_KV_WIDTH)
    sink = p["attn_sink"].reshape(ATT_KV_HEADS, ATT_GROUP, 1, 1)
    y_a = _attn_fwd(qh, kh, proj3, sink).reshape(T, ATT_WIDTH)

    qk_c = _conv_fwd(proj3, p["conv_w8"])
    hf, hb, *states = _mlstm_fwd(qk_c, proj3, p["gate_bias"])
    hf2, hb2 = hf.reshape(T, MLSTM_WIDTH), hb.reshape(T, MLSTM_WIDTH)
    DH = MLSTM_HEAD_DIM
    y_m = _rowwise("mlstm_out", _mlstm_combine,
                   [_In(hf2, DH, split=True), _In(hb2, DH, split=True), _In(proj, DH, C_OM // DH, split=True),
                    _In(p["mlstm_out_norm"], DH, split=True, rows=False)],
                   [_Out(MLSTM_WIDTH, BF16, DH, split=True)], T, 1024, ncol=MLSTM_HEADS)[0]

    za = _mmw("branch_a", y_a, p["w_branch_attn"], "nn")
    zm = _mmw("branch_m", y_m, p["w_branch_mlstm"], "nn")
    W = 512
    merged = _rowwise("merge", _merge,
                      [_In(proj, W, C_GMERGE // W, split=True), _In(proj, W, (C_GMERGE + D_MODEL) // W, split=True),
                       _In(za, W, split=True), _In(zm, W, split=True)],
                      [_Out(D_MODEL, BF16, W, split=True)], T, 512, ncol=D_MODEL // W)[0]
    out = _mmw("mix_out", merged, p["w_out"], "nn", res=x)
    saved = dict(x=x, h=h, proj=proj, q_gain=q_gain, k_gain=k_gain, qh=qh, kh=kh, sink=sink, y_a=y_a, qk_c=qk_c,
                 hf=hf2, hb=hb2, states=states, y_m=y_m, za=za, zm=zm, merged=merged)
    return out, saved


def _mix_bwd(sv, cos, sin, B, S, p, dx, on_dw):
    T = B * S
    DH = MLSTM_HEAD_DIM
    proj = sv["proj"]
    proj3 = proj.reshape(B, S, IN_PAD)
    g = {}
    dmerged = _mmw("mix_dmerged", dx, p["w_out"], "nt")
    g["w_out"] = _matmul("mix_dwout", sv["merged"], dx, "tn", out_dtype=BF16)
    W = 512

    def merge_bwd(ga, gm, za, zm, dm):
        _, vjp = jax.vjp(_merge, ga, gm, za, zm)
        return vjp(dm)

    dga, dgm, dza, dzm = _rowwise(
        "merge_bwd", merge_bwd,
        [_In(proj, W, C_GMERGE // W, split=True), _In(proj, W, (C_GMERGE + D_MODEL) // W, split=True),
         _In(sv["za"], W, split=True), _In(sv["zm"], W, split=True), _In(dmerged, W, split=True)],
        [_Out(D_MODEL, BF16, W, split=True), _Out(D_MODEL, BF16, W, split=True),
         _Out(D_MODEL, BF16, W, split=True), _Out(D_MODEL, BF16, W, split=True)], T, 512, ncol=D_MODEL // W)
    dya = _mmw("branch_a_dx", dza, p["w_branch_attn"], "nt")
    g["w_branch_attn"] = _matmul("branch_a_dw", sv["y_a"], dza, "tn", out_dtype=BF16)
    dym = _mmw("branch_m_dx", dzm, p["w_branch_mlstm"], "nt")
    g["w_branch_mlstm"] = _matmul("branch_m_dw", sv["y_m"], dzm, "tn", out_dtype=BF16)

    def combine_bwd(hf, hb, o_pre, gn, dy):
        _, vjp = jax.vjp(_mlstm_combine, hf, hb, o_pre, gn)
        dhf, _, do, dg = vjp(dy)
        return dhf, do, dg

    dh, dom, g["mlstm_out_norm"] = _rowwise(
        "mlstm_out_bwd", combine_bwd,
        [_In(sv["hf"], DH, split=True), _In(sv["hb"], DH, split=True), _In(proj, DH, C_OM // DH, split=True),
         _In(p["mlstm_out_norm"], DH, split=True, rows=False), _In(dym, DH, split=True)],
        [_Out(MLSTM_WIDTH, F32, DH, split=True), _Out(MLSTM_WIDTH, BF16, DH, split=True),
         _Out(MLSTM_WIDTH, F32, DH, split=True, rows=False)], T, 1024, ncol=MLSTM_HEADS)
    dqk_f, dqk_b, dv_f, dv_b, dg_f, dg_b = _mlstm_bwd(sv["qk_c"], proj3, p["gate_bias"], sv["states"],
                                                       dh.reshape(B, S, MLSTM_WIDTH))
    dgates, dvm, g["gate_bias"] = _rowwise(
        "mlstm_dsum", lambda a, b, c, d: (a + b, c + d, jnp.sum(a + b, axis=0, keepdims=True)),
        [_In(dg_f.reshape(T, LANES)), _In(dg_b.reshape(T, LANES)), _In(dv_f.reshape(T, MLSTM_WIDTH)), _In(dv_b.reshape(T, MLSTM_WIDTH))],
        [_Out(LANES, BF16), _Out(MLSTM_WIDTH, BF16), _Out(LANES, rows=False)], T, 1024)
    dqk, g["conv_w8"] = _conv_bwd(proj3, p["conv_w8"], dqk_f, dqk_b)

    dqh, dkh, dvh, dsink = _attn_bwd(sv["qh"], sv["kh"], proj3, sv["sink"], dya.reshape(B, S, ATT_WIDTH))
    g["attn_sink"] = dsink.reshape(1, ATT_HEADS)
    dva = dvh.reshape(T, ATT_KV_WIDTH)
    dqa, dq_gain = _prep_bwd("q_prep_bwd", proj, ATT_WIDTH, C_QA // ATT_WIDTH, sv["q_gain"], cos, sin,
                             dqh.reshape(T, ATT_WIDTH))
    dka, dk_gain = _prep_bwd("k_prep_bwd", proj, ATT_KV_WIDTH, C_KA // ATT_KV_WIDTH, sv["k_gain"], cos, sin,
                             dkh.reshape(T, ATT_KV_WIDTH))
    g["attn_q_norm"] = jnp.sum(dq_gain.reshape(ATT_HEADS, ATT_HEAD_DIM), axis=0, keepdims=True)
    g["attn_k_norm"] = jnp.sum(dk_gain.reshape(ATT_KV_HEADS, ATT_HEAD_DIM), axis=0, keepdims=True)

    dproj = jnp.concatenate(
        [dga, dgm, dqk.reshape(T, 2 * MLSTM_WIDTH), dvm, dom, dqa, dka, dva.astype(BF16), dgates], axis=1)
    dwin = _matmul("mix_dwin", sv["h"], dproj, "tn", out_dtype=BF16)
    token = on_dw({"w_in": _w_in_to_slots(dwin), "w_branch_attn": g.pop("w_branch_attn"),
                   "w_branch_mlstm": g.pop("w_branch_mlstm"), "w_out": g.pop("w_out")}, dwin)
    dh2 = _matmul("mix_dh", dproj, p["w_in"], "nt", dep=token)
    dx_new, g["mix_norm"] = _rms_bwd("mix_dnorm", sv["x"], p["mix_norm"], dh2, dx)
    return dx_new, g


def _loss_and_grad(x, g, target):
    T = x.shape[0]

    def loss_fn(xv, gv, tv):
        err = jnp.square(_rms(xv, gv) - tv)
        return 0.5 * jnp.sum(jnp.mean(err, axis=-1, keepdims=True), axis=0, keepdims=True)

    def fn(xv, gv, tv):
        val, vjp = jax.vjp(lambda a, b: loss_fn(a, b, tv), xv, gv)
        dx, dg = vjp(jnp.ones((1, 1), F32))
        return val, dx, dg

    return _rowwise("loss_head", fn, [_In(x), _In(g, rows=False), _In(target)],
                    [_Out(1, rows=False), _Out(D_MODEL), _Out(D_MODEL, rows=False)], T, ROW_BLOCK)


def _block_norm_fwd(x, g):
    T = x.shape[0]
    return _rowwise("block_norm", _rms, [_In(x), _In(g, rows=False)], [_Out(D_MODEL)], T, ROW_BLOCK)[0]


def _block_norm_bwd(x, g, dy):
    T = x.shape[0]

    def fn(xv, gv, dv):
        _, vjp = jax.vjp(_rms, xv, gv)
        return vjp(dv)

    return _rowwise("block_norm_bwd", fn, [_In(x), _In(g, rows=False), _In(dy)],
                    [_Out(D_MODEL), _Out(D_MODEL, rows=False)], T, ROW_BLOCK)


def _qk_perm_cols(t, axis):
    q, k = jnp.split(t, 2, axis=axis)
    parts = []
    for h in range(MLSTM_HEADS):
        sl = [slice(None)] * t.ndim
        sl[axis] = slice(h * MLSTM_HEAD_DIM, (h + 1) * MLSTM_HEAD_DIM)
        parts += [q[tuple(sl)], k[tuple(sl)]]
    return jnp.concatenate(parts, axis=axis)


def _qk_unperm_cols(t, axis):
    qs, ks = [], []
    for h in range(MLSTM_HEADS):
        sl = [slice(None)] * t.ndim
        sl[axis] = slice(2 * h * MLSTM_HEAD_DIM, (2 * h + 1) * MLSTM_HEAD_DIM)
        qs.append(t[tuple(sl)])
        sl[axis] = slice((2 * h + 1) * MLSTM_HEAD_DIM, (2 * h + 2) * MLSTM_HEAD_DIM)
        ks.append(t[tuple(sl)])
    return jnp.concatenate(qs + ks, axis=axis)


def _w_in_arrange(w):
    qa, ka, va, qm, km, vm, om, gm, gmerge = jnp.split(w, np.cumsum(
        (ATT_WIDTH, ATT_KV_WIDTH, ATT_KV_WIDTH, MLSTM_WIDTH, MLSTM_WIDTH, MLSTM_WIDTH, MLSTM_WIDTH, MLSTM_N_GATES))[:].tolist(), axis=1)
    qk = _qk_perm_cols(jnp.concatenate([qm, km], axis=1), 1)
    pad = jnp.zeros((w.shape[0], LANES - MLSTM_N_GATES), w.dtype)
    return jnp.concatenate([gmerge, qk, vm, om, qa, ka, va, gm, pad], axis=1)


def _w_in_restore(w):
    gmerge = w[:, C_GMERGE:C_GMERGE + 2 * D_MODEL]
    qk = _qk_unperm_cols(w[:, C_QK:C_QK + 2 * MLSTM_WIDTH], 1)
    vm, om = w[:, C_VM:C_VM + MLSTM_WIDTH], w[:, C_OM:C_OM + MLSTM_WIDTH]
    qa, ka, va = w[:, C_QA:C_QA + ATT_WIDTH], w[:, C_KA:C_KA + ATT_KV_WIDTH], w[:, C_VA:C_VA + ATT_KV_WIDTH]
    gm = w[:, C_GATES:C_GATES + MLSTM_N_GATES]
    return jnp.concatenate([qa, ka, va, qk, vm, om, gm, gmerge], axis=1)


BIG = ("ffn1_w_gate", "ffn1_w_up", "ffn1_w_down", "w_in", "mlstm_conv_w", "w_branch_attn", "w_branch_mlstm", "w_out",
       "ffn2_w_gate", "ffn2_w_up", "ffn2_w_down")
MATMUL_W = tuple(n for n in BIG if n != "mlstm_conv_w")
SMALL = ("ffn1_norm", "mix_norm", "mlstm_gate_bias", "attn_q_norm", "attn_k_norm", "attn_sink", "mlstm_conv_b",
         "mlstm_out_norm", "ffn2_norm", "block_out_norm")
WEIGHTS = ("ffn1_norm", "ffn1_w_gate", "ffn1_w_up", "ffn1_w_down", "mix_norm", "w_in", "mlstm_gate_bias", "attn_q_norm",
           "attn_k_norm", "attn_sink", "mlstm_conv_w", "mlstm_conv_b", "mlstm_out_norm", "w_branch_attn", "w_branch_mlstm",
           "w_out", "ffn2_norm", "ffn2_w_gate", "ffn2_w_up", "ffn2_w_down", "block_out_norm")
PACK_COLS = 1024


def _padded_rows(n_elems):
    return -(-n_elems // PACK_COLS)


def _pack_flat(arrs, dtype, row_multiple):
    parts = []
    for a in arrs:
        flat = a.reshape(-1).astype(dtype)
        pad = _padded_rows(flat.shape[0]) * PACK_COLS - flat.shape[0]
        parts.append(jnp.pad(flat, (0, pad)) if pad else flat)
    flat = jnp.concatenate(parts)
    rows = flat.shape[0] // PACK_COLS
    extra = (-rows) % row_multiple
    if extra:
        flat = jnp.pad(flat, (0, extra * PACK_COLS))
    return flat.reshape(-1, PACK_COLS)


def _unpack_flat(buf, shapes, lead=()):
    flat = buf.reshape(lead + (-1,))
    out, off = [], 0
    for s in shapes:
        n = int(np.prod(s))
        out.append(flat[..., off:off + n].reshape(lead + tuple(s)))
        off += _padded_rows(n) * PACK_COLS
    return out


class _Lay:
    def __init__(self, shard, axis, width):
        self.shard, self.axis, self.width = shard, axis, width
        self.padded = tuple(width if a == axis else s for a, s in enumerate(shard))
        self.whole = tuple(N_DEV * width if a == axis else s for a, s in enumerate(shard))

    def pad(self, t, lead=0):
        extra = self.width - self.shard[self.axis]
        if not extra:
            return t
        cfg = [(0, 0)] * t.ndim
        cfg[lead + self.axis] = (0, extra)
        return jnp.pad(t, cfg)

    def unpad(self, t, lead=0):
        idx = [slice(None)] * t.ndim
        idx[lead + self.axis] = slice(0, self.shard[self.axis])
        return t[tuple(idx)]


_FF_ROW = _Lay((FF_SHARD, D_MODEL), 0, FF_SHARD_PAD)
TRANSPOSED = ("ffn1_w_gate", "ffn1_w_up", "ffn2_w_gate", "ffn2_w_up")
LAYOUTS = {
    "ffn1_w_gate": _FF_ROW, "ffn1_w_up": _FF_ROW, "ffn1_w_down": _FF_ROW,
    "ffn2_w_gate": _FF_ROW, "ffn2_w_up": _FF_ROW, "ffn2_w_down": _FF_ROW,
    "w_in": _Lay((D_MODEL, IN_WIDTH // N_DEV), 0, D_MODEL),
    "mlstm_conv_w": _Lay((3, 2 * MLSTM_WIDTH // N_DEV), 1, 2 * MLSTM_WIDTH // N_DEV),
    "w_branch_attn": _Lay((ATT_WIDTH, D_MODEL // N_DEV), 1, D_MODEL // N_DEV),
    "w_branch_mlstm": _Lay((MLSTM_WIDTH, D_MODEL // N_DEV), 1, D_MODEL // N_DEV),
    "w_out": _Lay((D_MODEL // N_DEV, D_MODEL), 0, D_MODEL // N_DEV),
}


def _window(ref, axis, j, width):
    idx = [slice(None)] * len(ref.shape)
    idx[axis] = pl.ds(pl.multiple_of(j * width, width), width)
    return ref.at[tuple(idx)]


ANY = pl.BlockSpec(memory_space=pl.ANY)


def _mesh_pos():
    return lax.axis_index("x"), lax.axis_index("y"), lax.axis_index("c")


def _all_gather(name, shard, vmem=False):
    R, C = shard.shape
    space = pl.BlockSpec(memory_space=pltpu.VMEM) if vmem else ANY

    def body(x_ref, out_ref, send_sems, recv_sems, local_sem):
        x, y, c = _mesh_pos()
        me, sibling = (x, y, c), (x, y, 1 - c)
        chips = [(1 - x, y), (x, 1 - y), (1 - x, 1 - y)]

        def slot(px, py, pc):
            return out_ref.at[4 * px + 2 * py + pc]

        def copy(k, block, to, src=None):
            return pltpu.make_async_remote_copy(
                src_ref=slot(*block) if src is None else src, dst_ref=slot(*block),
                send_sem=send_sems.at[k], recv_sem=recv_sems.at[k], device_id=to, device_id_type=MESH)

        mine = pltpu.make_async_copy(x_ref, slot(*me), local_sem)
        mine.start()
        first = [copy(0, me, sibling, src=x_ref)]
        first += [copy(1 + j, me, (*chip, c), src=x_ref) for j, chip in enumerate(chips)]
        for cp in first:
            cp.start()
        passed = [copy(4 + j, (*chip, c), sibling) for j, chip in enumerate(chips)]
        for j, chip in enumerate(chips):
            copy(1 + j, (*chip, c), me).wait_recv()
            passed[j].start()
        copy(0, sibling, me).wait_recv()
        for j, chip in enumerate(chips):
            copy(4 + j, (*chip, 1 - c), me).wait_recv()
        for cp in first + passed:
            cp.wait_send()
        mine.wait()

    return pl.pallas_call(
        body, name=name, out_shape=jax.ShapeDtypeStruct((N_DEV, R, C), shard.dtype),
        in_specs=[space], out_specs=space,
        scratch_shapes=[pltpu.SemaphoreType.DMA((7,)), pltpu.SemaphoreType.DMA((7,)), pltpu.SemaphoreType.DMA],
    )(shard)


HBM = pl.BlockSpec(memory_space=pltpu.HBM)
SEM = pl.BlockSpec(memory_space=pltpu.SEMAPHORE)
SPLIT_COPY = pltpu.CompilerParams(has_side_effects=pltpu.SideEffectType.DATAFLOW_SIDE_EFFECTING)
N_PEERS = N_DEV - 1


def _peers(x, y, c):
    return [(x, y, 1 - c), (1 - x, y, c), (x, 1 - y, c), (1 - x, 1 - y, c),
            (1 - x, y, 1 - c), (x, 1 - y, 1 - c), (1 - x, 1 - y, 1 - c)]


def _dev_index(pos):
    return 4 * pos[0] + 2 * pos[1] + pos[2]


def _place_own(name, shards, lays):
    nt = len(shards)
    me = _dev_index(_mesh_pos())

    def body(me_ref, *refs):
        for x_ref, o_ref in zip(refs[:nt], refs[nt:]):
            o_ref[...] = x_ref[...]

    def window_spec(lay):
        if lay.axis == 0:
            return pl.BlockSpec(lay.padded, lambda i, me_ref: (me_ref[0], 0))
        return pl.BlockSpec(lay.padded, lambda i, me_ref: (0, me_ref[0]))

    return pl.pallas_call(
        body, name=name,
        grid_spec=pltpu.PrefetchScalarGridSpec(
            num_scalar_prefetch=1, grid=(1,),
            in_specs=[pl.BlockSpec(lay.padded, lambda i, me_ref: (0, 0)) for lay in lays],
            out_specs=[window_spec(lay) for lay in lays]),
        out_shape=[jax.ShapeDtypeStruct(lay.whole, s.dtype) for s, lay in zip(shards, lays)],
        compiler_params=_cparams(("arbitrary",)),
    )(me.reshape(1).astype(jnp.int32), *shards)


NEAR_PEERS = 4


def _gather_start(name, shards, lands, lays, groups, n_peers, after):
    nt, ng = len(shards), len(groups)

    def body(*refs):
        x_refs, land_refs = refs[:nt], refs[nt:2 * nt]
        sems = refs[2 * nt + 1:2 * nt + 1 + 2 * ng]
        pos = _mesh_pos()
        me = _dev_index(pos)
        for g, tens in enumerate(groups):
            for i, t in enumerate(tens):
                for k, peer in enumerate(_peers(*pos)[:n_peers[g]]):
                    pltpu.make_async_remote_copy(
                        src_ref=x_refs[t], dst_ref=_window(land_refs[t], lays[t].axis, me, lays[t].width),
                        send_sem=sems[2 * g].at[n_peers[g] * i + k], recv_sem=sems[2 * g + 1].at[n_peers[g] * i + k],
                        device_id=peer, device_id_type=MESH).start()

    sem_shapes = []
    for g, tens in enumerate(groups):
        sem_shapes += [pltpu.SemaphoreType.DMA((n_peers[g] * len(tens),))] * 2
    thru = [pltpu.HBM(s.shape, s.dtype) for s in shards] + [pltpu.HBM(lay.whole, s.dtype) for s, lay in zip(shards, lays)]
    args = [pltpu.with_memory_space_constraint(s, pltpu.HBM) for s in shards]
    args += [pltpu.with_memory_space_constraint(ld, pltpu.HBM) for ld in lands]
    res = pl.pallas_call(
        body, name=name, out_shape=tuple(sem_shapes + thru), in_specs=[HBM] * (2 * nt) + [ANY],
        out_specs=tuple([SEM] * (2 * ng) + [HBM] * (2 * nt)),
        input_output_aliases={t: 2 * ng + t for t in range(2 * nt)}, compiler_params=SPLIT_COPY,
    )(*args, after)
    sems = [(res[2 * g], res[2 * g + 1]) for g in range(ng)]
    return sems, list(res[2 * ng:2 * ng + nt]), list(res[2 * ng + nt:])


def _gather_wait(name, sems, shards, lands, lays, n_peers, after):
    nt = len(shards)
    send_sems, recv_sems = sems

    def body(*refs):
        x_refs, land_refs = refs[:nt], refs[nt:2 * nt]
        send_ref, recv_ref = refs[2 * nt], refs[2 * nt + 1]
        pos = _mesh_pos()
        for t in range(nt):
            for k, peer in enumerate(_peers(*pos)[:n_peers]):
                cp = pltpu.make_async_remote_copy(
                    src_ref=x_refs[t], dst_ref=_window(land_refs[t], lays[t].axis, _dev_index(peer), lays[t].width),
                    send_sem=send_ref.at[n_peers * t + k], recv_sem=recv_ref.at[n_peers * t + k],
                    device_id=peer, device_id_type=MESH)
                cp.wait_send()
                cp.wait_recv()

    thru = [pltpu.HBM(s.shape, s.dtype) for s in shards] + [pltpu.HBM(ld.shape, ld.dtype) for ld in lands]
    res = pl.pallas_call(
        body, name=name, out_shape=tuple(thru), in_specs=[HBM] * (2 * nt) + [SEM, SEM, ANY],
        out_specs=tuple([HBM] * (2 * nt)), input_output_aliases={t: t for t in range(2 * nt)},
        compiler_params=SPLIT_COPY,
    )(*shards, *lands, send_sems, recv_sems, after)
    return list(res[nt:])


def _forward_to_sibling(name, lands, lays):
    nt = len(lands)

    def body(*refs):
        land_refs = refs[nt:2 * nt]
        send_sems, recv_sems = refs[2 * nt:]
        x, y, c = _mesh_pos()
        chips = [(1 - x, y), (x, 1 - y), (1 - x, 1 - y)]

        def copy(t, j, core):
            win = _window(land_refs[t], lays[t].axis, _dev_index((*chips[j], core)), lays[t].width)
            return pltpu.make_async_remote_copy(
                src_ref=win, dst_ref=win, send_sem=send_sems.at[3 * t + j], recv_sem=recv_sems.at[3 * t + j],
                device_id=(x, y, 1 - c), device_id_type=MESH)

        sends = [copy(t, j, c) for t in range(nt) for j in range(3)]
        for cp in sends:
            cp.start()
        for t in range(nt):
            for j in range(3):
                copy(t, j, 1 - c).wait_recv()
        for cp in sends:
            cp.wait_send()

    return pl.pallas_call(
        body, name=name, out_shape=[jax.ShapeDtypeStruct(ld.shape, ld.dtype) for ld in lands],
        in_specs=[ANY] * nt, out_specs=[ANY] * nt, input_output_aliases={t: t for t in range(nt)},
        scratch_shapes=[pltpu.SemaphoreType.DMA((3 * nt,)), pltpu.SemaphoreType.DMA((3 * nt,))],
    )(*lands)


def _pair_exchange(name, grads, lays):
    nt = len(grads)

    def body(*refs):
        g_refs, land_refs = refs[:nt], refs[nt:2 * nt]
        send_sems, recv_sems = refs[2 * nt:]
        x, y, c = _mesh_pos()
        copies = []
        for t in range(nt):
            for chip in range(4):
                copies.append(pltpu.make_async_remote_copy(
                    src_ref=_window(g_refs[t], lays[t].axis, 2 * chip + (1 - c), lays[t].width), dst_ref=land_refs[t].at[chip],
                    send_sem=send_sems.at[4 * t + chip], recv_sem=recv_sems.at[4 * t + chip],
                    device_id=(x, y, 1 - c), device_id_type=MESH))
        for cp in copies:
            cp.start()
        for cp in copies:
            cp.wait_recv()
        for cp in copies:
            cp.wait_send()

    out_shape = [jax.ShapeDtypeStruct((4,) + lay.padded, g.dtype) for g, lay in zip(grads, lays)]
    return pl.pallas_call(
        body, name=name, out_shape=out_shape, in_specs=[ANY] * nt, out_specs=[ANY] * nt,
        scratch_shapes=[pltpu.SemaphoreType.DMA((4 * nt,)), pltpu.SemaphoreType.DMA((4 * nt,))],
    )(*grads)


def _pair_sum(name, whole, landed, lay, out_dtype):
    R, C = lay.padded
    br = _first_divisor(R, (512, 384, 256, 128, 64, 32, 16, 8))
    nb = R // br
    if lay.axis == 0:
        mine_spec = pl.BlockSpec((br, C), lambda k, i, c_ref: ((2 * k + c_ref[0]) * nb + i, 0))
    else:
        mine_spec = pl.BlockSpec((br, C), lambda k, i, c_ref: (i, 2 * k + c_ref[0]))

    def body(c_ref, mine_ref, sib_ref, o_ref):
        o_ref[0] = (mine_ref[...].astype(F32) + sib_ref[0].astype(F32)).astype(out_dtype)

    c = lax.axis_index("c")
    return pl.pallas_call(
        body, name=name,
        grid_spec=pltpu.PrefetchScalarGridSpec(
            num_scalar_prefetch=1, grid=(4, nb),
            in_specs=[mine_spec, pl.BlockSpec((1, br, C), lambda k, i, c_ref: (k, i, 0))],
            out_specs=pl.BlockSpec((1, br, C), lambda k, i, c_ref: (k, i, 0))),
        out_shape=jax.ShapeDtypeStruct((4, R, C), out_dtype),
        compiler_params=_cparams(("parallel", "parallel")),
    )(c.reshape(1).astype(jnp.int32), whole, landed)


def _chip_exchange(name, sums):
    nt = len(sums)

    def body(*refs):
        s_refs, land_refs = refs[:nt], refs[nt:2 * nt]
        send_sems, recv_sems, local_sems = refs[2 * nt:]
        x, y, c = _mesh_pos()
        my_chip = 2 * x + y
        mine = [pltpu.make_async_copy(s_refs[t].at[my_chip], land_refs[t].at[my_chip], local_sems.at[t]) for t in range(nt)]
        for cp in mine:
            cp.start()
        chips = [(1 - x, y), (x, 1 - y), (1 - x, 1 - y)]
        copies = []
        for t in range(nt):
            for j, (px, py) in enumerate(chips):
                copies.append(pltpu.make_async_remote_copy(
                    src_ref=s_refs[t].at[2 * px + py], dst_ref=land_refs[t].at[my_chip],
                    send_sem=send_sems.at[3 * t + j], recv_sem=recv_sems.at[3 * t + j],
                    device_id=(px, py, c), device_id_type=MESH))
        for cp in copies:
            cp.start()
        for t in range(nt):
            for j, (px, py) in enumerate(chips):
                pltpu.make_async_remote_copy(
                    src_ref=s_refs[t].at[my_chip], dst_ref=land_refs[t].at[2 * px + py],
                    send_sem=send_sems.at[3 * t + j], recv_sem=recv_sems.at[3 * t + j],
                    device_id=(px, py, c), device_id_type=MESH).wait_recv()
        for cp in copies:
            cp.wait_send()
        for cp in mine:
            cp.wait()

    return pl.pallas_call(
        body, name=name, out_shape=[jax.ShapeDtypeStruct(s.shape, s.dtype) for s in sums],
        in_specs=[ANY] * nt, out_specs=[ANY] * nt,
        scratch_shapes=[pltpu.SemaphoreType.DMA((3 * nt,)), pltpu.SemaphoreType.DMA((3 * nt,)), pltpu.SemaphoreType.DMA((nt,))],
    )(*sums)


def _chip_start(name, sums):
    nt = len(sums)

    def body(*refs):
        s_refs, land_refs = refs[:nt], refs[nt:2 * nt]
        send_sems, recv_sems = refs[2 * nt], refs[2 * nt + 1]
        x, y, c = _mesh_pos()
        my_chip = 2 * x + y
        for t in range(nt):
            for j, (px, py) in enumerate([(1 - x, y), (x, 1 - y), (1 - x, 1 - y)]):
                pltpu.make_async_remote_copy(
                    src_ref=s_refs[t].at[2 * px + py], dst_ref=land_refs[t].at[my_chip],
                    send_sem=send_sems.at[3 * t + j], recv_sem=recv_sems.at[3 * t + j],
                    device_id=(px, py, c), device_id_type=MESH).start()

    thru = [pltpu.HBM(s.shape, s.dtype) for s in sums] * 2
    args = [pltpu.with_memory_space_constraint(s, pltpu.HBM) for s in sums]
    args += [pltpu.with_memory_space_constraint(lax.empty(s.shape, s.dtype), pltpu.HBM) for s in sums]
    res = pl.pallas_call(
        body, name=name, out_shape=tuple([pltpu.SemaphoreType.DMA((3 * nt,))] * 2 + thru), in_specs=[HBM] * (2 * nt),
        out_specs=tuple([SEM, SEM] + [HBM] * (2 * nt)), input_output_aliases={t: 2 + t for t in range(2 * nt)},
        compiler_params=SPLIT_COPY,
    )(*args)
    return (res[0], res[1]), list(res[2:2 + nt]), list(res[2 + nt:])


def _chip_wait(name, sems, sums, lands, after):
    nt = len(sums)

    def body(*refs):
        s_refs, land_refs = refs[:nt], refs[nt:2 * nt]
        send_sems, recv_sems = refs[2 * nt], refs[2 * nt + 1]
        x, y, c = _mesh_pos()
        my_chip = 2 * x + y
        for t in range(nt):
            for j, (px, py) in enumerate([(1 - x, y), (x, 1 - y), (1 - x, 1 - y)]):
                cp = pltpu.make_async_remote_copy(
                    src_ref=s_refs[t].at[my_chip], dst_ref=land_refs[t].at[2 * px + py],
                    send_sem=send_sems.at[3 * t + j], recv_sem=recv_sems.at[3 * t + j],
                    device_id=(px, py, c), device_id_type=MESH)
                cp.wait_send()
                cp.wait_recv()

    thru = [pltpu.HBM(s.shape, s.dtype) for s in sums] * 2
    res = pl.pallas_call(
        body, name=name, out_shape=tuple(thru), in_specs=[HBM] * (2 * nt) + [SEM, SEM, ANY],
        out_specs=tuple([HBM] * (2 * nt)), input_output_aliases={t: t for t in range(2 * nt)},
        compiler_params=SPLIT_COPY,
    )(*sums, *lands, sems[0], sems[1], after)
    return list(res[:nt]), list(res[nt:])


def _sum_chips(name, own, landed):
    _, R, C = own.shape
    br = _first_divisor(R, (512, 384, 256, 128, 64, 32, 16, 8))
    x, y, _ = _mesh_pos()
    slots = jnp.stack([2 * x + y, 2 * (1 - x) + y, 2 * x + (1 - y), 2 * (1 - x) + (1 - y)]).astype(jnp.int32)

    def body(slot_ref, mine_ref, a_ref, b_ref, c_ref, o_ref):
        o_ref[...] = ((mine_ref[0].astype(F32) + a_ref[0].astype(F32)) + b_ref[0].astype(F32)) + c_ref[0].astype(F32)

    def slot_spec(j):
        return pl.BlockSpec((1, br, C), lambda i, slot_ref: (slot_ref[j], i, 0))

    return pl.pallas_call(
        body, name=name,
        grid_spec=pltpu.PrefetchScalarGridSpec(
            num_scalar_prefetch=1, grid=(R // br,), in_specs=[slot_spec(0), slot_spec(1), slot_spec(2), slot_spec(3)],
            out_specs=pl.BlockSpec((br, C), lambda i, slot_ref: (i, 0))),
        out_shape=jax.ShapeDtypeStruct((R, C), F32), compiler_params=_cparams(("parallel",)),
    )(slots, own, landed, landed, landed)


def _sum_slots(name, slots, n):
    _, R, C = slots.shape
    br = _first_divisor(R, (512, 384, 256, 128, 64, 32, 16, 8))

    def body(s_ref, o_ref):
        acc = s_ref[0].astype(F32)
        for k in range(1, n):
            acc = acc + s_ref[k].astype(F32)
        o_ref[...] = acc

    return pl.pallas_call(
        body, name=name, grid=(R // br,), in_specs=[pl.BlockSpec((n, br, C), lambda i: (0, i, 0))],
        out_specs=pl.BlockSpec((br, C), lambda i: (i, 0)), out_shape=jax.ShapeDtypeStruct((R, C), F32),
        compiler_params=_cparams(("parallel",)),
    )(slots)


def _reduce_scatter_start(tag, names, grads):
    lays = [LAYOUTS[n] for n in names]
    landed = _pair_exchange("grads_pair_" + names[0], grads, lays)
    sums = [_pair_sum("grads_pairsum_" + n, g, ld, lay, BF16) for n, g, ld, lay in zip(names, grads, landed, lays)]
    sems, sums, lands = _chip_start(tag + "_chips_start", sums)
    return tag, names, sems, sums, lands


def _reduce_scatter_finish(pending, after):
    tag, names, sems, sums, lands = pending
    own, got = _chip_wait(tag + "_chips_wait", sems, sums, lands, after)
    return [_sum_chips("grads_sum_" + n, o, s) for n, o, s in zip(names, own, got)]


def _adamw_math(w, g, m, v):
    m = ADAM_B1 * m + (1.0 - ADAM_B1) * g
    v = ADAM_B2 * v + (1.0 - ADAM_B2) * jnp.square(g)
    m_hat = m / (1.0 - ADAM_B1 ** ADAM_STEP)
    v_hat = v / (1.0 - ADAM_B2 ** ADAM_STEP)
    delta = -ADAM_LR * (m_hat / (jnp.sqrt(v_hat) + ADAM_EPS) + ADAM_WD * w)
    return delta, m, v


def _adamw_layers(name, w, totals, m, v):
    _, R, C = w.shape
    br = _first_divisor(R, (512, 176, 128, 64, 32, 16, 8))
    Cp = totals[0].shape[1]

    def body(w_ref, g0_ref, g1_ref, m_ref, v_ref, g_out, d_out, m_out, v_out):
        g = jnp.where(pl.program_id(0) == 0, g0_ref[:, 0:C], g1_ref[:, 0:C])
        delta, m_new, v_new = _adamw_math(w_ref[0], g, m_ref[0], v_ref[0])
        g_out[0], d_out[0], m_out[0], v_out[0] = g, delta, m_new, v_new

    blk = pl.BlockSpec((1, br, C), lambda l, i: (l, i, 0))
    g_spec = pl.BlockSpec((br, Cp), lambda l, i: (i, 0))
    return pl.pallas_call(
        body, name=name, grid=(DEPTH, R // br), in_specs=[blk, g_spec, g_spec, blk, blk], out_specs=[blk] * 4,
        out_shape=[jax.ShapeDtypeStruct(w.shape, F32)] * 4, compiler_params=_cparams(("parallel", "parallel")),
    )(w, totals[0], totals[1], m, v)


def _adamw(name, w, g, m, v):
    shape = w.shape
    cols = shape[-1]
    rows = int(np.prod(shape[:-1]))
    br = _first_divisor(rows, (512, 352, 256, 128, 64, 32, 16, 8))
    args = [_In(a.reshape(rows, cols)) for a in (w, g, m, v)]
    outs = _rowwise(name, _adamw_math, args, [_Out(cols), _Out(cols), _Out(cols)], rows, br)
    return [o.reshape(shape) for o in outs]


GROUPS = {"ffn1": ("ffn1_w_gate", "ffn1_w_up", "ffn1_w_down"),
          "mix": ("w_in", "w_branch_attn", "w_branch_mlstm", "w_out"),
          "ffn2": ("ffn2_w_gate", "ffn2_w_up", "ffn2_w_down")}
GATHER_GROUPS = {"ffn1_in": ("ffn1_w_gate", "ffn1_w_up"), "ffn1_out": ("ffn1_w_down",),
                 "mix": ("w_in", "w_branch_attn", "w_branch_mlstm", "w_out"),
                 "ffn2_in": ("ffn2_w_gate", "ffn2_w_up"), "ffn2_out": ("ffn2_w_down",)}


def _small_params(small, conv_w, l):
    p = {}
    for n in ("ffn1_norm", "mix_norm", "ffn2_norm", "block_out_norm", "mlstm_out_norm", "attn_q_norm", "attn_k_norm"):
        p[n] = small[n][l][None, :]
    p["attn_sink"] = small["attn_sink"][l]
    p["gate_bias"] = jnp.pad(small["mlstm_gate_bias"][l], (0, LANES - MLSTM_N_GATES))[None, :]
    taps = _qk_perm_cols(conv_w[l], 1)
    conv_b = _qk_perm_cols(small["mlstm_conv_b"][l][None, :], 1)
    p["conv_w8"] = jnp.concatenate([taps, conv_b, jnp.zeros((4, 2 * MLSTM_WIDTH), F32)], axis=0)
    return p


def _w_in_from_slots(slots):
    w_in = slots.reshape(N_DEV, D_MODEL, IN_WIDTH // N_DEV).transpose(1, 0, 2).reshape(D_MODEL, IN_WIDTH)
    return _w_in_arrange(w_in)


def _w_in_to_slots(g):
    return _w_in_restore(g).reshape(D_MODEL, N_DEV, IN_WIDTH // N_DEV).transpose(1, 0, 2).reshape(
        N_DEV * D_MODEL, IN_WIDTH // N_DEV)


def _local_step(x, positions, target, weights_of, small, conv_w, on_grads):
    B, S, _ = x.shape
    T = B * S
    cos, sin = _rope_cos_sin(positions.reshape(T, 1))
    params = [_small_params(small, conv_w, l) for l in range(DEPTH)]
    xs = x.reshape(T, D_MODEL)
    tgt = target.reshape(T, D_MODEL)

    saved = []
    for l, p in enumerate(params):
        p.update(weights_of(l, "ffn1_in", xs))
        x1, s1, p["ffn1_w_down"] = _ffn_fwd("ffn1", xs, p["ffn1_norm"], p["ffn1_w_gate"], p["ffn1_w_up"],
                                            lambda after, l=l: weights_of(l, "ffn1_out", after)["ffn1_w_down"])
        p.update(weights_of(l, "mix", x1))
        p["w_in"] = _w_in_from_slots(p["w_in"])
        x2, s2 = _mix_fwd(x1, cos, sin, B, S, p)
        p.update(weights_of(l, "ffn2_in", x2))
        x3, s3, p["ffn2_w_down"] = _ffn_fwd("ffn2", x2, p["ffn2_norm"], p["ffn2_w_gate"], p["ffn2_w_up"],
                                            lambda after, l=l: weights_of(l, "ffn2_out", after)["ffn2_w_down"])
        saved.append((s1, s2, s3, x3))
        if l + 1 < DEPTH:
            xs = _block_norm_fwd(x3, p["block_out_norm"])

    sm = {n: [None] * DEPTH for n in SMALL + ("mlstm_conv_w",)}
    loss = None
    dx = None
    for l in reversed(range(DEPTH)):
        p = params[l]
        s1, s2, s3, x3 = saved[l]
        if l == DEPTH - 1:
            loss, dx, dgn = _loss_and_grad(x3, p["block_out_norm"], tgt)
        else:
            dx, dgn = _block_norm_bwd(x3, p["block_out_norm"], dx)
        sm["block_out_norm"][l] = dgn[0]
        dx, dg = _ffn_bwd("ffn2", s3, p["ffn2_norm"], p["ffn2_w_gate"], p["ffn2_w_up"], p["ffn2_w_down"], dx,
                          functools.partial(on_grads, l, "ffn2"))
        sm["ffn2_norm"][l] = dg[0]
        dx, g = _mix_bwd(s2, cos, sin, B, S, p, dx, functools.partial(on_grads, l, "mix"))
        dconv = _qk_unperm_cols(g["conv_w8"], 1)
        sm["mlstm_conv_w"][l] = dconv[0:3]
        sm["mlstm_conv_b"][l] = dconv[3]
        sm["mix_norm"][l] = g["mix_norm"][0]
        sm["mlstm_gate_bias"][l] = g["gate_bias"][0, :MLSTM_N_GATES]
        sm["attn_q_norm"][l], sm["attn_k_norm"][l] = g["attn_q_norm"][0], g["attn_k_norm"][0]
        sm["attn_sink"][l] = g["attn_sink"][0]
        sm["mlstm_out_norm"][l] = g["mlstm_out_norm"][0]
        dx, dg = _ffn_bwd("ffn1", s1, p["ffn1_norm"], p["ffn1_w_gate"], p["ffn1_w_up"], p["ffn1_w_down"], dx,
                          functools.partial(on_grads, l, "ffn1"))
        sm["ffn1_norm"][l] = dg[0]
    sm = {n: jnp.stack(v, axis=0) for n, v in sm.items()}
    return loss, dx.reshape(B, S, D_MODEL), sm


def kernel(x, positions, ffn1_norm, ffn1_w_gate, ffn1_w_up, ffn1_w_down, mix_norm, w_in, mlstm_gate_bias, attn_q_norm, attn_k_norm, attn_sink, mlstm_conv_w, mlstm_conv_b, mlstm_out_norm, w_branch_attn, w_branch_mlstm, w_out, ffn2_norm, ffn2_w_gate, ffn2_w_up, ffn2_w_down, block_out_norm, loss_target, m_ffn1_norm, m_ffn1_w_gate, m_ffn1_w_up, m_ffn1_w_down, m_mix_norm, m_w_in, m_mlstm_gate_bias, m_attn_q_norm, m_attn_k_norm, m_attn_sink, m_mlstm_conv_w, m_mlstm_conv_b, m_mlstm_out_norm, m_w_branch_attn, m_w_branch_mlstm, m_w_out, m_ffn2_norm, m_ffn2_w_gate, m_ffn2_w_up, m_ffn2_w_down, m_block_out_norm, v_ffn1_norm, v_ffn1_w_gate, v_ffn1_w_up, v_ffn1_w_down, v_mix_norm, v_w_in, v_mlstm_gate_bias, v_attn_q_norm, v_attn_k_norm, v_attn_sink, v_mlstm_conv_w, v_mlstm_conv_b, v_mlstm_out_norm, v_w_branch_attn, v_w_branch_mlstm, v_w_out, v_ffn2_norm, v_ffn2_w_gate, v_ffn2_w_up, v_ffn2_w_down, v_block_out_norm):
    args = locals()
    def stored(n, t):
        return t.transpose(0, 2, 1) if n in TRANSPOSED else t

    w = {n: stored(n, args[n]) for n in WEIGHTS}
    m = {n: stored(n, args["m_" + n]) for n in WEIGHTS}
    v = {n: stored(n, args["v_" + n]) for n in WEIGHTS}

    order = [(l, grp) for l in range(DEPTH) for grp in GATHER_GROUPS]
    keys = [(l, n) for l, grp in order for n in GATHER_GROUPS[grp]]
    lays = [LAYOUTS[n] for _, n in keys]
    shards = [lay.pad(w[n][l].astype(BF16)) for (l, n), lay in zip(keys, lays)]
    group_idx, at = {}, 0
    for l, grp in order:
        group_idx[(l, grp)] = list(range(at, at + len(GATHER_GROUPS[grp])))
        at += len(GATHER_GROUPS[grp])
    conv_shape = w["mlstm_conv_w"].shape
    conv_all = _all_gather("conv_all_gather", _pack_flat([w["mlstm_conv_w"]], F32, 8), vmem=True)
    conv_parts = _unpack_flat(conv_all, [conv_shape], lead=(N_DEV,))[0]
    conv_w = jnp.concatenate([conv_parts[j] for j in range(N_DEV)], axis=2)
    small = {n: w[n] for n in SMALL}

    lands = []
    for l, grp in order:
        idx = group_idx[(l, grp)]
        lands += _place_own("weights_place_" + grp, [shards[i] for i in idx], [lays[i] for i in idx])
    n_peers = [NEAR_PEERS if (l, grp) in ((0, "ffn1_in"), (0, "ffn1_out"), (0, "mix")) else N_PEERS for l, grp in order]
    sems, shards, lands = _gather_start("weights_gather_start", shards, lands, lays, [group_idx[k] for k in order],
                                        n_peers, conv_all)

    def weights_of(l, grp, after):
        idx, g = group_idx[(l, grp)], order.index((l, grp))
        group_lays = [lays[i] for i in idx]
        whole = _gather_wait(f"weights_gather_wait_{l}_{grp}", sems[g], [shards[i] for i in idx],
                             [lands[i] for i in idx], group_lays, n_peers[g], after)
        if n_peers[g] == NEAR_PEERS:
            whole = _forward_to_sibling("weights_forward_" + grp, whole, group_lays)
        return dict(zip(GATHER_GROUPS[grp], whole))

    totals, pending = {}, []

    def finish(after):
        tag, names = pending[0][0], pending[0][1]
        for n, t in zip(names, _reduce_scatter_finish(pending.pop(0), after)):
            totals[(tag, n)] = t

    def on_grads(l, grp, g, after):
        if pending:
            finish(after)
        names = GROUPS[grp]
        pending.append(_reduce_scatter_start(f"grads_{l}_{grp}", names, [g[n] for n in names]))
        return pending[-1][3][0]

    loss, grad_x, small_g = _local_step(x, positions, loss_target, weights_of, small, conv_w, on_grads)
    finish(grad_x)
    grads, deltas, new_m, new_v = {}, {}, {}, {}
    for grp, names in GROUPS.items():
        for n in names:
            grads[n], deltas[n], new_m[n], new_v[n] = _adamw_layers(
                "adamw_" + n, w[n], [totals[(f"grads_{l}_{grp}", n)] for l in range(DEPTH)], m[n], v[n])

    small_names = SMALL + ("mlstm_conv_w",)
    small_shapes = [small_g[n].shape for n in small_names] + [(1, 1)]
    small_packed = _pack_flat([small_g[n] for n in small_names] + [loss], F32, 8)
    small_all = _all_gather("small_all_gather", small_packed, vmem=True)
    small_sum = _sum_slots("small_sum", small_all, N_DEV)
    *small_grads, loss_total = _unpack_flat(small_sum, small_shapes)
    grads.update(dict(zip(small_names, small_grads)))
    x_pos, y_pos, c_pos = _mesh_pos()
    grads["mlstm_conv_w"] = lax.dynamic_slice_in_dim(
        grads["mlstm_conv_w"], (4 * x_pos + 2 * y_pos + c_pos) * conv_shape[2], conv_shape[2], axis=2)

    n = "mlstm_conv_w"
    deltas[n], new_m[n], new_v[n] = _adamw("adamw_" + n, w[n], grads[n], m[n], v[n])
    sw, sg, smm, sv = (_pack_flat([d[n] for n in SMALL], F32, 8) for d in (w, grads, m, v))
    sd, snm, snv = _adamw("adamw_small", sw, sg, smm, sv)
    shapes = [w[n].shape for n in SMALL]
    for d, buf in ((deltas, sd), (new_m, snm), (new_v, snv)):
        d.update(dict(zip(SMALL, _unpack_flat(buf, shapes))))

    return (loss_total.reshape(()), grad_x, *[stored(n, d[n]) for d in (grads, deltas, new_m, new_v) for n in WEIGHTS])
```

```python
import functools

import numpy as np
import jax
import jax.numpy as jnp
from jax import lax
from jax.experimental import pallas as pl
from jax.experimental.pallas import tpu as pltpu

F32 = jnp.float32
BF16 = jnp.bfloat16

D_MODEL = 1024
D_FF = 2816
ATT_HEAD_DIM = 64
ATT_HEADS = 8
ATT_KV_HEADS = 2
ATT_GROUP = ATT_HEADS // ATT_KV_HEADS
ATT_WIDTH = ATT_HEADS * ATT_HEAD_DIM
ATT_KV_WIDTH = ATT_KV_HEADS * ATT_HEAD_DIM
WINDOW = 128
ATT_BLOCK = 128
ROPE_DIM = 16
ROPE_THETA = 500000.0
MLSTM_HEADS = 4
MLSTM_HEAD_DIM = 128
MLSTM_WIDTH = MLSTM_HEADS * MLSTM_HEAD_DIM
MLSTM_CHUNK = 128
MLSTM_N_GATES = 4 * MLSTM_HEADS
NORM_EPS = 1e-6
IN_WIDTH = 4880
DEPTH = 2
N_DEV = 8

ADAM_LR = 0.001
ADAM_B1 = 0.9
ADAM_B2 = 0.999
ADAM_EPS = 1e-08
ADAM_WD = 0.01
ADAM_STEP = 10

LANES = 128
C_GMERGE = 0
C_QK = 2048
C_VM = 3072
C_OM = 3584
C_QA = 4096
C_KA = 4608
C_VA = 4736
C_GATES = 4864
IN_PAD = 4992

VMEM_LIMIT = 48 * 1024 * 1024

MESH = pl.DeviceIdType.MESH


def _cparams(sem):
    return pltpu.CompilerParams(dimension_semantics=sem, vmem_limit_bytes=VMEM_LIMIT)


def _first_divisor(n, cands):
    for c in cands:
        if n % c == 0:
            return c
    return n


_NN = ((1,), (0,))
_NT = ((1,), (1,))
_TN = ((0,), (0,))


def _mm(a, b, dims):
    return lax.dot_general(a.astype(BF16), b.astype(BF16), (dims, ((), ())), preferred_element_type=F32)


@jax.custom_vjp
def mm_nn(a, b):
    return _mm(a, b, _NN)


def _mm_nn_fwd(a, b):
    return _mm(a, b, _NN), (a, b)


def _mm_nn_bwd(res, g):
    a, b = res
    return _mm(g, b, _NT).astype(a.dtype), _mm(a, g, _TN).astype(b.dtype)


mm_nn.defvjp(_mm_nn_fwd, _mm_nn_bwd)


@jax.custom_vjp
def mm_nt(a, b):
    return _mm(a, b, _NT)


def _mm_nt_fwd(a, b):
    return _mm(a, b, _NT), (a, b)


def _mm_nt_bwd(res, g):
    a, b = res
    return _mm(g, b, _NN).astype(a.dtype), _mm(g, a, _TN).astype(b.dtype)


mm_nt.defvjp(_mm_nt_fwd, _mm_nt_bwd)


@jax.custom_vjp
def mm_tn(a, b):
    return _mm(a, b, _TN)


def _mm_tn_fwd(a, b):
    return _mm(a, b, _TN), (a, b)


def _mm_tn_bwd(res, g):
    a, b = res
    return _mm(b, g, _NT).astype(a.dtype), _mm(a, g, _NN).astype(b.dtype)


mm_tn.defvjp(_mm_tn_fwd, _mm_tn_bwd)


def _matmul(name, a, b, mode, out_dtype=F32, res=None, scale=1.0, bl=None, dep=None, whole_k=False):
    b_shape = b.shape if bl is None else b.shape[1:]
    if mode == "nn":
        (M, K), (K2, N) = a.shape, b_shape
    elif mode == "nt":
        (M, K), (N, K2) = a.shape, b_shape
    else:
        (K, M), (K2, N) = a.shape, b_shape
    assert K == K2, (name, a.shape, b.shape)
    tm = _first_divisor(M, (1024, 512, 384, 256, 128))
    tn = _first_divisor(N, (1024, 1664, 512, 384, 256, 128))
    tk = K if whole_k else _first_divisor(K, (1024, 1664, 512, 256, 128))
    if whole_k:
        tn = min(tn, 512)
    nk = K // tk
    if mode == "tn":
        a_spec = pl.BlockSpec((tk, tm), lambda i, j, k: (k, i))
    else:
        a_spec = pl.BlockSpec((tm, tk), lambda i, j, k: (i, k))
    if mode == "nt":
        b_blk, b_idx = (tn, tk), (lambda i, j, k: (j, k))
    else:
        b_blk, b_idx = (tk, tn), (lambda i, j, k: (k, j))
    if bl is None:
        b_spec = pl.BlockSpec(b_blk, b_idx)
    else:
        b_spec = pl.BlockSpec((None,) + b_blk, lambda i, j, k: (bl,) + b_idx(i, j, k))
    o_spec = pl.BlockSpec((tm, tn), lambda i, j, k: (i, j))
    dims = {"nn": _NN, "nt": _NT, "tn": _TN}[mode]
    has_res = res is not None

    def body(*refs):
        a_ref, b_ref = refs[:2]
        r_ref = refs[2] if has_res else None

        def finish(out):
            if scale != 1.0:
                out = out * scale
            if has_res:
                out = r_ref[...].astype(F32) + out
            o_ref[...] = out.astype(out_dtype)

        if nk == 1:
            o_ref = refs[-1]
            finish(_mm(a_ref[...], b_ref[...], dims))
            return
        o_ref, acc = refs[-2:]
        k = pl.program_id(2)

        @pl.when(k == 0)
        def _():
            acc[...] = jnp.zeros_like(acc)

        acc[...] += _mm(a_ref[...], b_ref[...], dims)

        @pl.when(k == nk - 1)
        def _():
            finish(acc[...])

    in_specs = [a_spec, b_spec] + ([o_spec] if has_res else [])
    args = (a, b) + ((res,) if has_res else ())
    if dep is not None:
        in_specs.append(pl.BlockSpec(memory_space=pl.ANY))
        args += (dep,)
    return pl.pallas_call(
        body, name=name, grid=(M // tm, N // tn, nk), in_specs=in_specs, out_specs=o_spec,
        out_shape=jax.ShapeDtypeStruct((M, N), out_dtype),
        scratch_shapes=[pltpu.VMEM((tm, tn), F32)] if nk > 1 else [],
        compiler_params=_cparams(("parallel", "parallel", "arbitrary")),
    )(*args)


class _In:
    def __init__(self, arr, width=None, base=0, split=False, rows=True):
        self.arr, self.base, self.split, self.rows = arr, base, split, rows
        self.width = arr.shape[1] if width is None else width


class _Out:
    def __init__(self, cols, dtype=F32, width=None, split=False, rows=True, nrows=1):
        self.cols, self.dtype, self.split, self.rows, self.nrows = cols, dtype, split, rows, nrows
        self.width = cols if width is None else width


def _rowwise(name, fn, ins, outs, n_rows, br, ncol=1):
    br = min(br, n_rows)
    assert n_rows % br == 0, (name, n_rows, br)
    nrow_blocks = n_rows // br

    def in_spec(d):
        nb = br if d.rows else d.arr.shape[0]
        if d.rows and d.split:
            im = lambda j, i, base=d.base: (i, base + j)
        elif d.rows:
            im = lambda j, i, base=d.base: (i, base)
        elif d.split:
            im = lambda j, i, base=d.base: (0, base + j)
        else:
            im = lambda j, i, base=d.base: (0, base)
        return pl.BlockSpec((nb, d.width), im)

    def out_spec(d):
        nb = br if d.rows else d.nrows
        if d.rows and d.split:
            im = lambda j, i: (i, j)
        elif d.rows:
            im = lambda j, i: (i, 0)
        elif d.split:
            im = lambda j, i: (0, j)
        else:
            im = lambda j, i: (0, 0)
        return pl.BlockSpec((nb, d.width), im)

    n_in = len(ins)

    def body(*refs):
        i = pl.program_id(1)
        vals = [r[...] for r in refs[:n_in]]
        res = fn(*vals)
        if not isinstance(res, (tuple, list)):
            res = (res,)
        for d, ref, val in zip(outs, refs[n_in:], res):
            if d.rows:
                ref[...] = val.astype(d.dtype)
            else:
                @pl.when(i == 0)
                def _(ref=ref):
                    ref[...] = jnp.zeros_like(ref)

                ref[...] += val.astype(d.dtype)

    out_shape = [jax.ShapeDtypeStruct((n_rows if d.rows else d.nrows, d.cols), d.dtype) for d in outs]
    res = pl.pallas_call(
        body, name=name, grid=(ncol, nrow_blocks), in_specs=[in_spec(d) for d in ins],
        out_specs=[out_spec(d) for d in outs], out_shape=out_shape,
        compiler_params=_cparams(("parallel", "arbitrary")),
    )(*[d.arr for d in ins])
    return res


def _rms(x, g):
    return x * lax.rsqrt(jnp.mean(x * x, axis=-1, keepdims=True) + NORM_EPS) * g


def _sigmoid(x):
    return 0.5 * jnp.tanh(0.5 * x) + 0.5


def _silu(x):
    return x * _sigmoid(x)


def _log_sigmoid(x):
    return jnp.minimum(x, 0.0) - jnp.log(1.0 + jnp.exp(-jnp.abs(x)))


def _rope_tables(pos, inv_freq_row):
    ang = pos.astype(F32) * inv_freq_row
    return jnp.cos(ang), jnp.sin(ang)


def _head_sums_impl(v):
    w = v.shape[-1]
    shift = ATT_HEAD_DIM.bit_length() - 1
    r = lax.shift_right_logical(lax.broadcasted_iota(jnp.int32, (w, w), 0), shift)
    c = lax.shift_right_logical(lax.broadcasted_iota(jnp.int32, (w, w), 1), shift)
    ones = (r == c).astype(BF16)
    hi = v.astype(BF16)
    lo = (v - hi.astype(F32)).astype(BF16)
    dn = (_NN, ((), ()))
    return (lax.dot_general(hi, ones, dn, preferred_element_type=F32)
            + lax.dot_general(lo, ones, dn, preferred_element_type=F32))


@jax.custom_vjp
def _head_sums(v):
    return _head_sums_impl(v)


_head_sums.defvjp(lambda v: (_head_sums_impl(v), None), lambda _, g: (_head_sums_impl(g),))


def _rotate_half_impl(y):
    w = y.shape[-1]
    half = ROPE_DIM // 2
    lane = lax.broadcasted_iota(jnp.int32, y.shape, 1) & (ATT_HEAD_DIM - 1)
    above = pltpu.roll(y, w - half, axis=1)
    below = pltpu.roll(y, half, axis=1)
    return jnp.where(lane < half, -above, jnp.where(lane < ROPE_DIM, below, 0.0))


@jax.custom_vjp
def _rotate_half(y):
    return _rotate_half_impl(y)


_rotate_half.defvjp(lambda y: (_rotate_half_impl(y), None), lambda _, g: (-_rotate_half_impl(g),))


def _qk_prep(t, g, cos, sin):
    reps = t.shape[-1] // cos.shape[-1]
    if reps > 1:
        cos, sin = jnp.tile(cos, (1, reps)), jnp.tile(sin, (1, reps))
    y = t * lax.rsqrt(_head_sums(t * t) * (1.0 / ATT_HEAD_DIM) + NORM_EPS) * g
    return y * cos + _rotate_half(y) * sin


def _attn_head(q, kb, vb, sink, valid):
    s = mm_nt(q, kb) * (ATT_HEAD_DIM ** -0.5)
    s = jnp.where(valid, s, -jnp.inf)
    m = jnp.maximum(jnp.max(s, axis=-1, keepdims=True), sink)
    p = jnp.exp(s - m)
    den = jnp.sum(p, axis=-1, keepdims=True) + jnp.exp(sink - m)
    return mm_nn(p * (1.0 / den), vb)


def _mlstm_chunk(q, k, v, li, lf, C, n, m, incl, incl_t, eye):
    k = k * (MLSTM_HEAD_DIM ** -0.5)
    lf_row = jnp.sum(eye * lf, axis=0, keepdims=True)
    li_row = jnp.sum(eye * li, axis=0, keepdims=True)
    b = jnp.sum(incl * lf_row, axis=1, keepdims=True)
    b_row = jnp.sum(incl_t * lf, axis=0, keepdims=True)
    b_tot = jnp.sum(lf, axis=0, keepdims=True)
    a = b_tot - b + li
    a_max = jnp.max(a, axis=0, keepdims=True)
    kw = k * jnp.exp(a - a_max)
    c_loc = mm_tn(kw, v)
    n_loc = jnp.sum(kw, axis=0, keepdims=True)

    dmat = jnp.where(incl > 0.5, b - b_row + li_row, -jnp.inf)
    inter = b + m
    m_t = jnp.maximum(inter, jnp.max(dmat, axis=1, keepdims=True))
    sc = mm_nt(q, k) * jnp.exp(dmat - m_t)
    scale_in = jnp.exp(inter - m_t)
    num = mm_nn(sc, v) + scale_in * mm_nn(q, C)
    den = jnp.sum(sc, axis=1, keepdims=True) + scale_in * jnp.sum(q * n, axis=1, keepdims=True)
    h = num * (1.0 / jnp.maximum(jnp.abs(den), jnp.exp(-m_t)))

    m_new = jnp.maximum(b_tot + m, a_max)
    s_p = jnp.exp(b_tot + m - m_new)
    s_l = jnp.exp(a_max - m_new)
    return h, s_p * C + s_l * c_loc, s_p * n + s_l * n_loc, m_new


def _mlstm_combine(hf, hb, o_pre, g):
    h = hf + hb
    mu = jnp.mean(h, axis=-1, keepdims=True)
    var = jnp.mean(jnp.square(h - mu), axis=-1, keepdims=True)
    return _sigmoid(o_pre) * ((h - mu) * lax.rsqrt(var + NORM_EPS) * g)


def _merge(ga, gm, za, zm):
    return _sigmoid(ga) * za + _sigmoid(gm) * zm


def _attn_mask(n, seq):
    shape = (ATT_GROUP * ATT_BLOCK, 3 * ATT_BLOCK)
    qi = n * ATT_BLOCK + (lax.broadcasted_iota(jnp.int32, shape, 0) & (ATT_BLOCK - 1))
    kj = (n - 1) * ATT_BLOCK + lax.broadcasted_iota(jnp.int32, shape, 1)
    return (jnp.abs(qi - kj) <= WINDOW) & (kj >= 0) & (kj < seq)


def _attn_specs(nq, v_base):
    q_spec = pl.BlockSpec((1, ATT_BLOCK, ATT_WIDTH), lambda b, n: (b, n, 0))

    def kv_spec(off, base=0):
        return pl.BlockSpec((1, ATT_BLOCK, ATT_KV_WIDTH), lambda b, n: (b, jnp.clip(n + off, 0, nq - 1), base))

    sink_spec = pl.BlockSpec((ATT_KV_HEADS, ATT_GROUP, 1, 1), lambda b, n: (0, 0, 0, 0))
    specs = [q_spec, kv_spec(-1), kv_spec(0), kv_spec(1), kv_spec(-1, v_base), kv_spec(0, v_base), kv_spec(1, v_base), sink_spec]
    return q_spec, specs, sink_spec


def _head(h):
    return slice(h * ATT_HEAD_DIM, (h + 1) * ATT_HEAD_DIM)


def _group_rows(q_ref, s_ref, h):
    q4 = jnp.concatenate([q_ref[0, :, _head(h * ATT_GROUP + g)] for g in range(ATT_GROUP)], axis=0)
    sink4 = jnp.concatenate([jnp.broadcast_to(s_ref[h, g], (ATT_BLOCK, 1)) for g in range(ATT_GROUP)], axis=0)
    return q4, sink4


def _attn_fwd(q, k, proj3, sink):
    B, S, _ = q.shape
    nq = S // ATT_BLOCK
    q_spec, specs, _ = _attn_specs(nq, C_VA // ATT_KV_WIDTH)

    def body(q_ref, kp, kc, kn, vp, vc, vn, s_ref, o_ref):
        valid = _attn_mask(pl.program_id(1), S)
        for h in range(ATT_KV_HEADS):
            kb = jnp.concatenate([kp[0, :, _head(h)], kc[0, :, _head(h)], kn[0, :, _head(h)]], axis=0)
            vb = jnp.concatenate([vp[0, :, _head(h)], vc[0, :, _head(h)], vn[0, :, _head(h)]], axis=0)
            q4, sink4 = _group_rows(q_ref, s_ref, h)
            o4 = _attn_head(q4, kb, vb, sink4, valid).astype(BF16)
            for g in range(ATT_GROUP):
                o_ref[0, :, _head(h * ATT_GROUP + g)] = o4[g * ATT_BLOCK:(g + 1) * ATT_BLOCK]

    return pl.pallas_call(
        body, name="attn_fwd", grid=(B, nq), in_specs=specs,
        out_specs=q_spec, out_shape=jax.ShapeDtypeStruct(q.shape, BF16),
        compiler_params=_cparams(("parallel", "arbitrary")),
    )(q, k, k, k, proj3, proj3, proj3, sink)


def _attn_bwd(q, k, proj3, sink, dy):
    B, S, _ = q.shape
    nq = S // ATT_BLOCK
    q_spec, specs, sink_spec = _attn_specs(nq, C_VA // ATT_KV_WIDTH)
    kv_full = pl.BlockSpec((1, S, ATT_KV_WIDTH), lambda b, n: (b, 0, 0))

    def body(q_ref, kp, kc, kn, vp, vc, vn, s_ref, dy_ref, dq_ref, dk_ref, dv_ref, ds_ref):
        b, n = pl.program_id(0), pl.program_id(1)
        valid = _attn_mask(n, S)

        @pl.when(n == 0)
        def _():
            dk_ref[...] = jnp.zeros_like(dk_ref)
            dv_ref[...] = jnp.zeros_like(dv_ref)

        @pl.when((n == 0) & (b == 0))
        def _():
            ds_ref[...] = jnp.zeros_like(ds_ref)

        for h in range(ATT_KV_HEADS):
            kb = jnp.concatenate([kp[0, :, _head(h)], kc[0, :, _head(h)], kn[0, :, _head(h)]], axis=0)
            vb = jnp.concatenate([vp[0, :, _head(h)], vc[0, :, _head(h)], vn[0, :, _head(h)]], axis=0)
            q4, sink4 = _group_rows(q_ref, s_ref, h)
            dy4 = jnp.concatenate([dy_ref[0, :, _head(h * ATT_GROUP + g)] for g in range(ATT_GROUP)], axis=0)
            _, vjp = jax.vjp(functools.partial(_attn_head, valid=valid), q4, kb, vb, sink4)
            dq4, dkb, dvb, dsink4 = vjp(dy4)
            for g in range(ATT_GROUP):
                rows = slice(g * ATT_BLOCK, (g + 1) * ATT_BLOCK)
                dq_ref[0, :, _head(h * ATT_GROUP + g)] = dq4[rows]
                ds_ref[h, g] += jnp.sum(dsink4[rows], axis=0, keepdims=True)
            for j, off in enumerate((-1, 0, 1)):
                start = pl.multiple_of(jnp.clip(n + off, 0, nq - 1) * ATT_BLOCK, ATT_BLOCK)
                rows = pl.ds(start, ATT_BLOCK)
                dk_ref[0, rows, _head(h)] += dkb[j * ATT_BLOCK:(j + 1) * ATT_BLOCK]
                dv_ref[0, rows, _head(h)] += dvb[j * ATT_BLOCK:(j + 1) * ATT_BLOCK]

    kv_shape = jax.ShapeDtypeStruct(k.shape, F32)
    return pl.pallas_call(
        body, name="attn_bwd", grid=(B, nq), in_specs=specs + [q_spec],
        out_specs=[q_spec, kv_full, kv_full, sink_spec],
        out_shape=[jax.ShapeDtypeStruct(q.shape, F32), kv_shape, kv_shape, jax.ShapeDtypeStruct(sink.shape, F32)],
        compiler_params=_cparams(("arbitrary", "arbitrary")),
    )(q, k, k, k, proj3, proj3, proj3, sink, dy)


CONV_COLS = 256


def _conv_taps(u, seq):
    row = lax.broadcasted_iota(jnp.int32, u.shape, 0)
    prev = jnp.where(row == 0, 0.0, pltpu.roll(u, 1, axis=0))
    nxt = jnp.where(row == seq - 1, 0.0, pltpu.roll(u, seq - 1, axis=0))
    return prev, nxt


def _conv_fwd(proj3, w8):
    B, S, _ = proj3.shape
    ncb = 2 * MLSTM_WIDTH // CONV_COLS

    def body(u_ref, w_ref, o_ref):
        u = u_ref[0]
        prev, nxt = _conv_taps(u, S)
        o_ref[0] = _silu(prev * w_ref[0:1, :] + u * w_ref[1:2, :] + nxt * w_ref[2:3, :] + w_ref[3:4, :])

    return pl.pallas_call(
        body, name="conv_fwd", grid=(B, ncb),
        in_specs=[pl.BlockSpec((1, S, CONV_COLS), lambda b, c: (b, 0, C_QK // CONV_COLS + c)),
                  pl.BlockSpec((8, CONV_COLS), lambda b, c: (0, c))],
        out_specs=pl.BlockSpec((1, S, CONV_COLS), lambda b, c: (b, 0, c)),
        out_shape=jax.ShapeDtypeStruct((B, S, 2 * MLSTM_WIDTH), F32),
        compiler_params=_cparams(("parallel", "parallel")),
    )(proj3, w8)


def _conv_bwd(proj3, w8, dout_f, dout_b):
    B, S, _ = proj3.shape
    ncb = 2 * MLSTM_WIDTH // CONV_COLS

    def body(u_ref, w_ref, df_ref, db_ref, du_ref, dw_ref):
        b = pl.program_id(1)
        u = u_ref[0]
        prev, nxt = _conv_taps(u, S)
        w0, w1, w2 = w_ref[0:1, :], w_ref[1:2, :], w_ref[2:3, :]
        pre = prev * w0 + u * w1 + nxt * w2 + w_ref[3:4, :]
        sig = _sigmoid(pre)
        dpre = (df_ref[0] + db_ref[0]) * (sig * (1.0 + pre * (1.0 - sig)))
        dprev, dnxt = _conv_taps(dpre, S)
        du_ref[0] = (dnxt * w0 + dpre * w1 + dprev * w2).astype(BF16)

        @pl.when(b == 0)
        def _():
            dw_ref[...] = jnp.zeros_like(dw_ref)

        dw_ref[0:1, :] += jnp.sum(dpre * prev, axis=0, keepdims=True)
        dw_ref[1:2, :] += jnp.sum(dpre * u, axis=0, keepdims=True)
        dw_ref[2:3, :] += jnp.sum(dpre * nxt, axis=0, keepdims=True)
        dw_ref[3:4, :] += jnp.sum(dpre, axis=0, keepdims=True)

    blk = pl.BlockSpec((1, S, CONV_COLS), lambda c, b: (b, 0, c))
    return pl.pallas_call(
        body, name="conv_bwd", grid=(ncb, B),
        in_specs=[pl.BlockSpec((1, S, CONV_COLS), lambda c, b: (b, 0, C_QK // CONV_COLS + c)),
                  pl.BlockSpec((8, CONV_COLS), lambda c, b: (0, c)), blk, blk],
        out_specs=[blk, pl.BlockSpec((8, CONV_COLS), lambda c, b: (0, c))],
        out_shape=[jax.ShapeDtypeStruct((B, S, 2 * MLSTM_WIDTH), BF16), jax.ShapeDtypeStruct((8, 2 * MLSTM_WIDTH), F32)],
        compiler_params=_cparams(("parallel", "arbitrary")),
    )(proj3, w8, dout_f, dout_b)


MLSTM_HEADS_PER_STEP = 4


def _chunk_masks(direction):
    t = lax.broadcasted_iota(jnp.int32, (MLSTM_CHUNK, MLSTM_CHUNK), 0)
    s = lax.broadcasted_iota(jnp.int32, (MLSTM_CHUNK, MLSTM_CHUNK), 1)
    le, ge = (s <= t).astype(F32), (s >= t).astype(F32)
    eye = (s == t).astype(F32)
    return (le, ge, eye) if direction == 0 else (ge, le, eye)


def _gate_cols(gates, direction, head):
    lane = lax.broadcasted_iota(jnp.int32, gates.shape, 1)
    sel_i = (lane == (2 * direction) * MLSTM_HEADS + head).astype(F32)
    sel_f = (lane == (2 * direction + 1) * MLSTM_HEADS + head).astype(F32)
    return sel_i, sel_f


def _mlstm_fwd(qk, proj3, bias):
    B, S, _ = qk.shape
    nc = S // MLSTM_CHUNK
    H, L, DH = MLSTM_HEADS, MLSTM_CHUNK, MLSTM_HEAD_DIM

    def chunk_of(d, c):
        return c if d == 0 else nc - 1 - c

    HS = MLSTM_HEADS_PER_STEP

    def body(qkf, qkb, vf, vb, gf, gb, bias_ref, hf, hb, csf, csb, nsf, nsb, msf, msb, c_st, n_st, m_st):
        c, hg = pl.program_id(1), pl.program_id(2)

        @pl.when(c == 0)
        def _():
            for d in range(2):
                for j in range(HS):
                    c_st[d, hg * HS + j] = jnp.zeros((DH, DH), F32)
                    n_st[d, hg * HS + j] = jnp.zeros((1, DH), F32)
                    m_st[d, hg * HS + j] = jnp.zeros((1, DH), F32)

        for d, (qk_ref, v_ref, g_ref, h_ref, cs, ns, ms) in enumerate(
                ((qkf, vf, gf, hf, csf, nsf, msf), (qkb, vb, gb, hb, csb, nsb, msb))):
            incl, incl_t, eye = _chunk_masks(d)
            gates = g_ref[0] + bias_ref[...]
            log_f = _log_sigmoid(gates)
            for j in range(HS):
                h = hg * HS + j
                sel_i, sel_f = _gate_cols(gates, d, h)
                li = jnp.sum(gates * sel_i, axis=1, keepdims=True)
                lf = jnp.sum(log_f * sel_f, axis=1, keepdims=True)
                c_in, n_in, m_in = c_st[d, h], n_st[d, h], m_st[d, h]
                cs[0, 0, j], ns[0, 0, j], ms[0, 0, j] = c_in, n_in, m_in
                hh, c_new, n_new, m_new = _mlstm_chunk(
                    qk_ref[0, :, 2 * j * DH:(2 * j + 1) * DH], qk_ref[0, :, (2 * j + 1) * DH:(2 * j + 2) * DH],
                    v_ref[0, :, j * DH:(j + 1) * DH], li, lf, c_in, n_in,
                    jnp.max(m_in, axis=1, keepdims=True), incl, incl_t, eye)
                h_ref[0, :, j * DH:(j + 1) * DH] = hh
                c_st[d, h], n_st[d, h] = c_new, n_new
                m_st[d, h] = jnp.broadcast_to(m_new, (1, DH))

    def tok_spec(width, base, d, per_head):
        return pl.BlockSpec((1, L, width), lambda b, c, h: (b, chunk_of(d, c), base + (h if per_head else 0)))

    def st_spec(shape, d):
        return pl.BlockSpec((1, 1, HS) + shape, lambda b, c, h: (b, chunk_of(d, c), h, 0, 0))

    in_specs = [tok_spec(2 * HS * DH, 0, 0, True), tok_spec(2 * HS * DH, 0, 1, True),
                tok_spec(HS * DH, C_VM // (HS * DH), 0, True), tok_spec(HS * DH, C_VM // (HS * DH), 1, True),
                tok_spec(LANES, C_GATES // LANES, 0, False), tok_spec(LANES, C_GATES // LANES, 1, False),
                pl.BlockSpec((1, LANES), lambda b, c, h: (0, 0))]
    out_specs = [tok_spec(HS * DH, 0, 0, True), tok_spec(HS * DH, 0, 1, True),
                 st_spec((DH, DH), 0), st_spec((DH, DH), 1), st_spec((1, DH), 0), st_spec((1, DH), 1),
                 st_spec((1, DH), 0), st_spec((1, DH), 1)]
    hs = jax.ShapeDtypeStruct((B, S, H * DH), F32)
    cs = jax.ShapeDtypeStruct((B, nc, H, DH, DH), F32)
    vs = jax.ShapeDtypeStruct((B, nc, H, 1, DH), F32)
    return pl.pallas_call(
        body, name="mlstm_fwd", grid=(B, nc, H // HS), in_specs=in_specs, out_specs=out_specs,
        out_shape=[hs, hs, cs, cs, vs, vs, vs, vs],
        scratch_shapes=[pltpu.VMEM((2, H, DH, DH), F32), pltpu.VMEM((2, H, 1, DH), F32), pltpu.VMEM((2, H, 1, DH), F32)],
        compiler_params=_cparams(("parallel", "arbitrary", "arbitrary")),
    )(qk, qk, proj3, proj3, proj3, proj3, bias)


def _mlstm_bwd(qk, proj3, bias, states, dh):
    B, S, _ = qk.shape
    nc = S // MLSTM_CHUNK
    H, L, DH = MLSTM_HEADS, MLSTM_CHUNK, MLSTM_HEAD_DIM

    def chunk_of(d, c):
        return nc - 1 - c if d == 0 else c

    HS = MLSTM_HEADS_PER_STEP

    def body(qkf, qkb, vf, vb, gf, gb, bias_ref, csf, csb, nsf, nsb, msf, msb, dhf, dhb,
             dqkf, dqkb, dvf, dvb, dgf, dgb, dc_st, dn_st, dm_st):
        c, hg = pl.program_id(1), pl.program_id(2)

        @pl.when(c == 0)
        def _():
            for d in range(2):
                for j in range(HS):
                    dc_st[d, hg * HS + j] = jnp.zeros((DH, DH), F32)
                    dn_st[d, hg * HS + j] = jnp.zeros((1, DH), F32)
                    dm_st[d, hg * HS + j] = jnp.zeros((1, DH), F32)

        @pl.when(hg == 0)
        def _():
            dgf[...] = jnp.zeros_like(dgf)
            dgb[...] = jnp.zeros_like(dgb)

        for d, (qk_ref, v_ref, g_ref, cs, ns, ms, dh_ref, dqk_ref, dv_ref, dg_ref) in enumerate(
                ((qkf, vf, gf, csf, nsf, msf, dhf, dqkf, dvf, dgf), (qkb, vb, gb, csb, nsb, msb, dhb, dqkb, dvb, dgb))):
            incl, incl_t, eye = _chunk_masks(d)
            gates = g_ref[0] + bias_ref[...]
            log_f = _log_sigmoid(gates)
            d_li = jnp.zeros_like(gates)
            d_lf = jnp.zeros_like(gates)
            for j in range(HS):
                h = hg * HS + j
                sel_i, sel_f = _gate_cols(gates, d, h)
                li = jnp.sum(gates * sel_i, axis=1, keepdims=True)
                lf = jnp.sum(log_f * sel_f, axis=1, keepdims=True)
                m_in = jnp.max(ms[0, 0, j], axis=1, keepdims=True)
                _, vjp = jax.vjp(
                    functools.partial(_mlstm_chunk, incl=incl, incl_t=incl_t, eye=eye),
                    qk_ref[0, :, 2 * j * DH:(2 * j + 1) * DH], qk_ref[0, :, (2 * j + 1) * DH:(2 * j + 2) * DH],
                    v_ref[0, :, j * DH:(j + 1) * DH], li, lf, cs[0, 0, j], ns[0, 0, j], m_in)
                dm_out = jnp.max(dm_st[d, h], axis=1, keepdims=True)
                dq, dk, dv, dli, dlf, dc, dn, dm = vjp((dh_ref[0, :, j * DH:(j + 1) * DH], dc_st[d, h], dn_st[d, h], dm_out))
                dqk_ref[0, :, 2 * j * DH:(2 * j + 1) * DH] = dq
                dqk_ref[0, :, (2 * j + 1) * DH:(2 * j + 2) * DH] = dk
                dv_ref[0, :, j * DH:(j + 1) * DH] = dv
                d_li += dli * sel_i
                d_lf += dlf * sel_f
                dc_st[d, h], dn_st[d, h] = dc, dn
                dm_st[d, h] = jnp.broadcast_to(dm, (1, DH))
            dg_ref[0] += d_li + d_lf * _sigmoid(-gates)

    def tok_spec(width, base, d, per_head):
        return pl.BlockSpec((1, L, width), lambda b, c, h: (b, chunk_of(d, c), base + (h if per_head else 0)))

    def st_spec(shape, d):
        return pl.BlockSpec((1, 1, HS) + shape, lambda b, c, h: (b, chunk_of(d, c), h, 0, 0))

    in_specs = [tok_spec(2 * HS * DH, 0, 0, True), tok_spec(2 * HS * DH, 0, 1, True),
                tok_spec(HS * DH, C_VM // (HS * DH), 0, True), tok_spec(HS * DH, C_VM // (HS * DH), 1, True),
                tok_spec(LANES, C_GATES // LANES, 0, False), tok_spec(LANES, C_GATES // LANES, 1, False),
                pl.BlockSpec((1, LANES), lambda b, c, h: (0, 0)),
                st_spec((DH, DH), 0), st_spec((DH, DH), 1), st_spec((1, DH), 0), st_spec((1, DH), 1),
                st_spec((1, DH), 0), st_spec((1, DH), 1), tok_spec(HS * DH, 0, 0, True), tok_spec(HS * DH, 0, 1, True)]
    out_specs = [tok_spec(2 * HS * DH, 0, 0, True), tok_spec(2 * HS * DH, 0, 1, True),
                 tok_spec(HS * DH, 0, 0, True), tok_spec(HS * DH, 0, 1, True),
                 tok_spec(LANES, 0, 0, False), tok_spec(LANES, 0, 1, False)]
    qks = jax.ShapeDtypeStruct((B, S, 2 * H * DH), F32)
    vs = jax.ShapeDtypeStruct((B, S, H * DH), F32)
    gs = jax.ShapeDtypeStruct((B, S, LANES), F32)
    csf, csb, nsf, nsb, msf, msb = states
    return pl.pallas_call(
        body, name="mlstm_bwd", grid=(B, nc, H // HS), in_specs=in_specs, out_specs=out_specs,
        out_shape=[qks, qks, vs, vs, gs, gs],
        scratch_shapes=[pltpu.VMEM((2, H, DH, DH), F32), pltpu.VMEM((2, H, 1, DH), F32), pltpu.VMEM((2, H, 1, DH), F32)],
        compiler_params=_cparams(("parallel", "arbitrary", "arbitrary")),
    )(qk, qk, proj3, proj3, proj3, proj3, bias, csf, csb, nsf, nsb, msf, msb, dh, dh)


ROW_BLOCK = 256
FF_COLS = 512
FF_SHARD = D_FF // N_DEV
FF_SHARD_PAD = 384
FF_PAD = N_DEV * FF_SHARD_PAD


def _rms_fwd(name, x, g):
    T = x.shape[0]
    return _rowwise(name, lambda xv, gv: _rms(xv, gv), [_In(x), _In(g, rows=False)], [_Out(D_MODEL, BF16)], T, ROW_BLOCK)[0]


def _rms_bwd(name, x, g, dh, dres):
    T = x.shape[0]

    def fn(xv, gv, dhv, drv):
        _, vjp = jax.vjp(_rms, xv, gv)
        dx, dg = vjp(dhv)
        return drv + dx, dg

    return _rowwise(name, fn, [_In(x), _In(g, rows=False), _In(dh), _In(dres)],
                    [_Out(D_MODEL), _Out(D_MODEL, rows=False)], T, ROW_BLOCK)


def _mmw(name, a, w, mode, **kw):
    if isinstance(w, tuple):
        return _matmul(name, a, w[0], mode, bl=w[1], **kw)
    return _matmul(name, a, w, mode, **kw)


def _swiglu(gate, up):
    return _silu(gate) * up


def _ffn_in(name, h, wg, wu):
    (M, K), N = h.shape, wg.shape[0]
    tm, tn = _first_divisor(M, (1024, 512, 256, 128)), FF_COLS

    def body(h_ref, wg_ref, wu_ref, g_ref, u_ref, a_ref):
        hv = h_ref[...]
        gate = _mm(hv, wg_ref[...], _NT)
        up = _mm(hv, wu_ref[...], _NT)
        g_ref[...], u_ref[...] = gate.astype(BF16), up.astype(BF16)
        a_ref[...] = _swiglu(gate, up).astype(BF16)

    w_spec = pl.BlockSpec((tn, K), lambda i, j: (j, 0))
    o_spec = pl.BlockSpec((tm, tn), lambda i, j: (i, j))
    return pl.pallas_call(
        body, name=name, grid=(M // tm, N // tn), in_specs=[pl.BlockSpec((tm, K), lambda i, j: (i, 0)), w_spec, w_spec],
        out_specs=[o_spec, o_spec, o_spec],
        out_shape=[jax.ShapeDtypeStruct((M, N), BF16)] * 3,
        compiler_params=_cparams(("parallel", "parallel")),
    )(h, wg, wu)


def _ffn_dact(name, dx, wd, gate, up):
    (M, K), N = dx.shape, wd.shape[0]
    tm, tn = _first_divisor(M, (1024, 512, 256, 128)), FF_COLS

    def body(dx_ref, wd_ref, g_ref, u_ref, dg_ref, du_ref):
        dact = _mm(dx_ref[...], wd_ref[...], _NT) * 0.5
        gate, up = g_ref[...].astype(F32), u_ref[...].astype(F32)
        s = _sigmoid(gate)
        silu = gate * s
        dg_ref[...] = (dact * up * (s + silu * (1.0 - s))).astype(BF16)
        du_ref[...] = (dact * silu).astype(BF16)

    o_spec = pl.BlockSpec((tm, tn), lambda i, j: (i, j))
    return pl.pallas_call(
        body, name=name, grid=(M // tm, N // tn),
        in_specs=[pl.BlockSpec((tm, K), lambda i, j: (i, 0)), pl.BlockSpec((tn, K), lambda i, j: (j, 0)), o_spec, o_spec],
        out_specs=[o_spec, o_spec],
        out_shape=[jax.ShapeDtypeStruct((M, N), BF16), jax.ShapeDtypeStruct((M, N), BF16)],
        compiler_params=_cparams(("parallel", "parallel")),
    )(dx, wd, gate, up)


def _ffn_dh(name, dgate, dup, wg, wu, dep, x, gain, dres):
    (M, K), N = dgate.shape, wg.shape[1]
    tm, tk = _first_divisor(M, (512, 256, 128)), _first_divisor(K, (1024, 512, 384, 256, 128))
    nk = K // tk

    def body(dg_ref, du_ref, wg_ref, wu_ref, x_ref, gain_ref, dres_ref, dep_ref, o_ref, dgain_ref, acc):
        i, k = pl.program_id(0), pl.program_id(1)

        @pl.when(k == 0)
        def _():
            acc[...] = jnp.zeros_like(acc)

        acc[...] += _mm(dg_ref[...], wg_ref[...], _NN) + _mm(du_ref[...], wu_ref[...], _NN)

        @pl.when((k == nk - 1) & (i == 0))
        def _():
            dgain_ref[...] = jnp.zeros_like(dgain_ref)

        @pl.when(k == nk - 1)
        def _():
            _, vjp = jax.vjp(_rms, x_ref[...], gain_ref[...])
            dx, dgain = vjp(acc[...])
            o_ref[...] = dres_ref[...] + dx
            dgain_ref[...] += dgain

    a_spec = pl.BlockSpec((tm, tk), lambda i, k: (i, k))
    w_spec = pl.BlockSpec((tk, N), lambda i, k: (k, 0))
    row_spec = pl.BlockSpec((tm, N), lambda i, k: (i, 0))
    gain_spec = pl.BlockSpec((1, N), lambda i, k: (0, 0))
    return pl.pallas_call(
        body, name=name, grid=(M // tm, nk),
        in_specs=[a_spec, a_spec, w_spec, w_spec, row_spec, gain_spec, row_spec, pl.BlockSpec(memory_space=pl.ANY)],
        out_specs=[row_spec, gain_spec],
        out_shape=[jax.ShapeDtypeStruct((M, N), F32), jax.ShapeDtypeStruct((1, N), F32)],
        scratch_shapes=[pltpu.VMEM((tm, N), F32)], compiler_params=_cparams(("arbitrary", "arbitrary")),
    )(dgate, dup, wg, wu, x, gain, dres, dep)


def _ffn_fwd(tag, x, g, wg, wu, wd):
    h = _rms_fwd(tag + "_norm", x, g)
    gate, up, act = _ffn_in(tag + "_in", h, wg, wu)
    if callable(wd):
        wd = wd(act)
    out = _mmw(tag + "_down", act, wd, "nn", res=x, scale=0.5, whole_k=True)
    return out, (x, h, gate, up, act), wd


def _ffn_bwd(tag, saved, g, wg, wu, wd, dx, on_dw):
    x, h, gate, up, act = saved
    dgate, dup = _ffn_dact(tag + "_dact", dx, wd, gate, up)
    dwd = _matmul(tag + "_dwd", act, dx, "tn", scale=0.5, out_dtype=BF16)
    dwg = _matmul(tag + "_dwg", dgate, h, "tn", out_dtype=BF16, whole_k=True)
    dwu = _matmul(tag + "_dwu", dup, h, "tn", out_dtype=BF16, whole_k=True)
    token = on_dw({tag + "_w_gate": dwg, tag + "_w_up": dwu, tag + "_w_down": dwd}, dwu)
    return _ffn_dh(tag + "_dh", dgate, dup, wg, wu, token, x, g, dx)


def _rope_cos_sin(positions):
    half = ROPE_DIM // 2
    inv_freq = jnp.power(jnp.float32(ROPE_THETA), -jnp.arange(half, dtype=F32) * (2.0 / ROPE_DIM))
    head = jnp.zeros((ATT_HEAD_DIM,), F32).at[:ROPE_DIM].set(jnp.concatenate([inv_freq, inv_freq]))
    row = jnp.tile(head, LANES // ATT_HEAD_DIM)[None, :]
    T = positions.shape[0]
    return _rowwise("rope_tables", _rope_tables, [_In(positions), _In(row, rows=False)], [_Out(LANES), _Out(LANES)], T, 1024)


def _prep_fwd(name, src, width, base, g, cos, sin):
    return _rowwise(name, _qk_prep, [_In(src, width, base), _In(g, rows=False), _In(cos), _In(sin)],
                    [_Out(width)], src.shape[0], 512)[0]


def _prep_bwd(name, src, width, base, g, cos, sin, dout):
    def fn(tv, gv, cv, sv, dv):
        _, vjp = jax.vjp(lambda a, b: _qk_prep(a, b, cv, sv), tv, gv)
        return vjp(dv)

    return _rowwise(name, fn, [_In(src, width, base), _In(g, rows=False), _In(cos), _In(sin), _In(dout)],
                    [_Out(width, BF16), _Out(width, rows=False)], src.shape[0], 512)


def _to_heads(t, B, S, nh):
    return t.reshape(B, S, nh, ATT_HEAD_DIM).transpose(0, 2, 1, 3)


def _from_heads(t):
    B, nh, S, _ = t.shape
    return t.transpose(0, 2, 1, 3).reshape(B * S, nh * ATT_HEAD_DIM)


def _mix_fwd(x, cos, sin, B, S, p):
    T = B * S
    h = _rms_fwd("mix_norm", x, p["mix_norm"])
    proj = _matmul("mix_proj", h, p["w_in"], "nn")
    proj3 = proj.reshape(B, S, IN_PAD)
    q_gain = jnp.tile(p["attn_q_norm"], (1, ATT_HEADS))
    k_gain = jnp.tile(p["attn_k_norm"], (1, ATT_KV_HEADS))
    q_r = _prep_fwd("q_prep", proj, ATT_WIDTH, C_QA // ATT_WIDTH, q_gain, cos, sin)
    k_r = _prep_fwd("k_prep", proj, ATT_KV_WIDTH, C_KA // ATT_KV_WIDTH, k_gain, cos, sin)
    qh = q_r.reshape(B, S, ATT_WIDTH)
    kh = k_r.reshape(B, S, ATT_KV_WIDTH)
    sink = p["attn_sink"].reshape(ATT_KV_HEADS, ATT_GROUP, 1, 1)
    y_a = _attn_fwd(qh, kh, proj3, sink).reshape(T, ATT_WIDTH)

    qk_c = _conv_fwd(proj3, p["conv_w8"])
    hf, hb, *states = _mlstm_fwd(qk_c, proj3, p["gate_bias"])
    hf2, hb2 = hf.reshape(T, MLSTM_WIDTH), hb.reshape(T, MLSTM_WIDTH)
    DH = MLSTM_HEAD_DIM
    y_m = _rowwise("mlstm_out", _mlstm_combine,
                   [_In(hf2, DH, split=True), _In(hb2, DH, split=True), _In(proj, DH, C_OM // DH, split=True),
                    _In(p["mlstm_out_norm"], DH, split=True, rows=False)],
                   [_Out(MLSTM_WIDTH, BF16, DH, split=True)], T, 1024, ncol=MLSTM_HEADS)[0]

    za = _mmw("branch_a", y_a, p["w_branch_attn"], "nn")
    zm = _mmw("branch_m", y_m, p["w_branch_mlstm"], "nn")
    W = 512
    merged = _rowwise("merge", _merge,
                      [_In(proj, W, C_GMERGE // W, split=True), _In(proj, W, (C_GMERGE + D_MODEL) // W, split=True),
                       _In(za, W, split=True), _In(zm, W, split=True)],
                      [_Out(D_MODEL, BF16, W, split=True)], T, 512, ncol=D_MODEL // W)[0]
    out = _mmw("mix_out", merged, p["w_out"], "nn", res=x)
    saved = dict(x=x, h=h, proj=proj, q_gain=q_gain, k_gain=k_gain, qh=qh, kh=kh, sink=sink, y_a=y_a, qk_c=qk_c,
                 hf=hf2, hb=hb2, states=states, y_m=y_m, za=za, zm=zm, merged=merged)
    return out, saved


def _mix_bwd(sv, cos, sin, B, S, p, dx, on_dw):
    T = B * S
    DH = MLSTM_HEAD_DIM
    proj = sv["proj"]
    proj3 = proj.reshape(B, S, IN_PAD)
    g = {}
    dmerged = _mmw("mix_dmerged", dx, p["w_out"], "nt")
    g["w_out"] = _matmul("mix_dwout", sv["merged"], dx, "tn", out_dtype=BF16)
    W = 512

    def merge_bwd(ga, gm, za, zm, dm):
        _, vjp = jax.vjp(_merge, ga, gm, za, zm)
        return vjp(dm)

    dga, dgm, dza, dzm = _rowwise(
        "merge_bwd", merge_bwd,
        [_In(proj, W, C_GMERGE // W, split=True), _In(proj, W, (C_GMERGE + D_MODEL) // W, split=True),
         _In(sv["za"], W, split=True), _In(sv["zm"], W, split=True), _In(dmerged, W, split=True)],
        [_Out(D_MODEL, BF16, W, split=True), _Out(D_MODEL, BF16, W, split=True),
         _Out(D_MODEL, BF16, W, split=True), _Out(D_MODEL, BF16, W, split=True)], T, 512, ncol=D_MODEL // W)
    dya = _mmw("branch_a_dx", dza, p["w_branch_attn"], "nt")
    g["w_branch_attn"] = _matmul("branch_a_dw", sv["y_a"], dza, "tn", out_dtype=BF16)
    dym = _mmw("branch_m_dx", dzm, p["w_branch_mlstm"], "nt")
    g["w_branch_mlstm"] = _matmul("branch_m_dw", sv["y_m"], dzm, "tn", out_dtype=BF16)

    def combine_bwd(hf, hb, o_pre, gn, dy):
        _, vjp = jax.vjp(_mlstm_combine, hf, hb, o_pre, gn)
        dhf, _, do, dg = vjp(dy)
        return dhf, do, dg

    dh, dom, g["mlstm_out_norm"] = _rowwise(
        "mlstm_out_bwd", combine_bwd,
        [_In(sv["hf"], DH, split=True), _In(sv["hb"], DH, split=True), _In(proj, DH, C_OM // DH, split=True),
         _In(p["mlstm_out_norm"], DH, split=True, rows=False), _In(dym, DH, split=True)],
        [_Out(MLSTM_WIDTH, F32, DH, split=True), _Out(MLSTM_WIDTH, BF16, DH, split=True),
         _Out(MLSTM_WIDTH, F32, DH, split=True, rows=False)], T, 1024, ncol=MLSTM_HEADS)
    dqk_f, dqk_b, dv_f, dv_b, dg_f, dg_b = _mlstm_bwd(sv["qk_c"], proj3, p["gate_bias"], sv["states"],
                                                       dh.reshape(B, S, MLSTM_WIDTH))
    dgates, dvm, g["gate_bias"] = _rowwise(
        "mlstm_dsum", lambda a, b, c, d: (a + b, c + d, jnp.sum(a + b, axis=0, keepdims=True)),
        [_In(dg_f.reshape(T, LANES)), _In(dg_b.reshape(T, LANES)), _In(dv_f.reshape(T, MLSTM_WIDTH)), _In(dv_b.reshape(T, MLSTM_WIDTH))],
        [_Out(LANES, BF16), _Out(MLSTM_WIDTH, BF16), _Out(LANES, rows=False)], T, 1024)
    dqk, g["conv_w8"] = _conv_bwd(proj3, p["conv_w8"], dqk_f, dqk_b)

    dqh, dkh, dvh, dsink = _attn_bwd(sv["qh"], sv["kh"], proj3, sv["sink"], dya.reshape(B, S, ATT_WIDTH))
    g["attn_sink"] = dsink.reshape(1, ATT_HEADS)
    dva = dvh.reshape(T, ATT_KV_WIDTH)
    dqa, dq_gain = _prep_bwd("q_prep_bwd", proj, ATT_WIDTH, C_QA // ATT_WIDTH, sv["q_gain"], cos, sin,
                             dqh.reshape(T, ATT_WIDTH))
    dka, dk_gain = _prep_bwd("k_prep_bwd", proj, ATT_KV_WIDTH, C_KA // ATT_KV_WIDTH, sv["k_gain"], cos, sin,
                             dkh.reshape(T, ATT_KV_WIDTH))
    g["attn_q_norm"] = jnp.sum(dq_gain.reshape(ATT_HEADS, ATT_HEAD_DIM), axis=0, keepdims=True)
    g["attn_k_norm"] = jnp.sum(dk_gain.reshape(ATT_KV_HEADS, ATT_HEAD_DIM), axis=0, keepdims=True)

    dproj = jnp.concatenate(
        [dga, dgm, dqk.reshape(T, 2 * MLSTM_WIDTH), dvm, dom, dqa, dka, dva.astype(BF16), dgates], axis=1)
    dwin = _matmul("mix_dwin", sv["h"], dproj, "tn", out_dtype=BF16)
    token = on_dw({"w_in": _w_in_to_slots(dwin), "w_branch_attn": g.pop("w_branch_attn"),
                   "w_branch_mlstm": g.pop("w_branch_mlstm"), "w_out": g.pop("w_out")}, dwin)
    dh2 = _matmul("mix_dh", dproj, p["w_in"], "nt", dep=token)
    dx_new, g["mix_norm"] = _rms_bwd("mix_dnorm", sv["x"], p["mix_norm"], dh2, dx)
    return dx_new, g


def _loss_and_grad(x, g, target):
    T = x.shape[0]

    def loss_fn(xv, gv, tv):
        err = jnp.square(_rms(xv, gv) - tv)
        return 0.5 * jnp.sum(jnp.mean(err, axis=-1, keepdims=True), axis=0, keepdims=True)

    def fn(xv, gv, tv):
        val, vjp = jax.vjp(lambda a, b: loss_fn(a, b, tv), xv, gv)
        dx, dg = vjp(jnp.ones((1, 1), F32))
        return val, dx, dg

    return _rowwise("loss_head", fn, [_In(x), _In(g, rows=False), _In(target)],
                    [_Out(1, rows=False), _Out(D_MODEL), _Out(D_MODEL, rows=False)], T, ROW_BLOCK)


def _block_norm_fwd(x, g):
    T = x.shape[0]
    return _rowwise("block_norm", _rms, [_In(x), _In(g, rows=False)], [_Out(D_MODEL)], T, ROW_BLOCK)[0]


def _block_norm_bwd(x, g, dy):
    T = x.shape[0]

    def fn(xv, gv, dv):
        _, vjp = jax.vjp(_rms, xv, gv)
        return vjp(dv)

    return _rowwise("block_norm_bwd", fn, [_In(x), _In(g, rows=False), _In(dy)],
                    [_Out(D_MODEL), _Out(D_MODEL, rows=False)], T, ROW_BLOCK)


def _qk_perm_cols(t, axis):
    q, k = jnp.split(t, 2, axis=axis)
    parts = []
    for h in range(MLSTM_HEADS):
        sl = [slice(None)] * t.ndim
        sl[axis] = slice(h * MLSTM_HEAD_DIM, (h + 1) * MLSTM_HEAD_DIM)
        parts += [q[tuple(sl)], k[tuple(sl)]]
    return jnp.concatenate(parts, axis=axis)


def _qk_unperm_cols(t, axis):
    qs, ks = [], []
    for h in range(MLSTM_HEADS):
        sl = [slice(None)] * t.ndim
        sl[axis] = slice(2 * h * MLSTM_HEAD_DIM, (2 * h + 1) * MLSTM_HEAD_DIM)
        qs.append(t[tuple(sl)])
        sl[axis] = slice((2 * h + 1) * MLSTM_HEAD_DIM, (2 * h + 2) * MLSTM_HEAD_DIM)
        ks.append(t[tuple(sl)])
    return jnp.concatenate(qs + ks, axis=axis)


def _w_in_arrange(w):
    qa, ka, va, qm, km, vm, om, gm, gmerge = jnp.split(w, np.cumsum(
        (ATT_WIDTH, ATT_KV_WIDTH, ATT_KV_WIDTH, MLSTM_WIDTH, MLSTM_WIDTH, MLSTM_WIDTH, MLSTM_WIDTH, MLSTM_N_GATES))[:].tolist(), axis=1)
    qk = _qk_perm_cols(jnp.concatenate([qm, km], axis=1), 1)
    pad = jnp.zeros((w.shape[0], LANES - MLSTM_N_GATES), w.dtype)
    return jnp.concatenate([gmerge, qk, vm, om, qa, ka, va, gm, pad], axis=1)


def _w_in_restore(w):
    gmerge = w[:, C_GMERGE:C_GMERGE + 2 * D_MODEL]
    qk = _qk_unperm_cols(w[:, C_QK:C_QK + 2 * MLSTM_WIDTH], 1)
    vm, om = w[:, C_VM:C_VM + MLSTM_WIDTH], w[:, C_OM:C_OM + MLSTM_WIDTH]
    qa, ka, va = w[:, C_QA:C_QA + ATT_WIDTH], w[:, C_KA:C_KA + ATT_KV_WIDTH], w[:, C_VA:C_VA + ATT_KV_WIDTH]
    gm = w[:, C_GATES:C_GATES + MLSTM_N_GATES]
    return jnp.concatenate([qa, ka, va, qk, vm, om, gm, gmerge], axis=1)


BIG = ("ffn1_w_gate", "ffn1_w_up", "ffn1_w_down", "w_in", "mlstm_conv_w", "w_branch_attn", "w_branch_mlstm", "w_out",
       "ffn2_w_gate", "ffn2_w_up", "ffn2_w_down")
MATMUL_W = tuple(n for n in BIG if n != "mlstm_conv_w")
SMALL = ("ffn1_norm", "mix_norm", "mlstm_gate_bias", "attn_q_norm", "attn_k_norm", "attn_sink", "mlstm_conv_b",
         "mlstm_out_norm", "ffn2_norm", "block_out_norm")
WEIGHTS = ("ffn1_norm", "ffn1_w_gate", "ffn1_w_up", "ffn1_w_down", "mix_norm", "w_in", "mlstm_gate_bias", "attn_q_norm",
           "attn_k_norm", "attn_sink", "mlstm_conv_w", "mlstm_conv_b", "mlstm_out_norm", "w_branch_attn", "w_branch_mlstm",
           "w_out", "ffn2_norm", "ffn2_w_gate", "ffn2_w_up", "ffn2_w_down", "block_out_norm")
PACK_COLS = 1024


def _padded_rows(n_elems):
    return -(-n_elems // PACK_COLS)


def _pack_flat(arrs, dtype, row_multiple):
    parts = []
    for a in arrs:
        flat = a.reshape(-1).astype(dtype)
        pad = _padded_rows(flat.shape[0]) * PACK_COLS - flat.shape[0]
        parts.append(jnp.pad(flat, (0, pad)) if pad else flat)
    flat = jnp.concatenate(parts)
    rows = flat.shape[0] // PACK_COLS
    extra = (-rows) % row_multiple
    if extra:
        flat = jnp.pad(flat, (0, extra * PACK_COLS))
    return flat.reshape(-1, PACK_COLS)


def _pack_rows(name, pieces, total_rows):
    def body(*refs):
        o_ref = refs[-1]
        o_ref[...] = jnp.zeros_like(o_ref)
        at = 0
        for ref, (arr, r0, nr) in zip(refs[:-1], pieces):
            o_ref[at:at + nr, 0:arr.shape[1]] = ref[r0:r0 + nr, :].astype(F32)
            at += nr

    return pl.pallas_call(body, name=name, out_shape=jax.ShapeDtypeStruct((total_rows, PACK_COLS), F32))(
        *[p[0] for p in pieces])


def _unpack_flat(buf, shapes, lead=()):
    flat = buf.reshape(lead + (-1,))
    out, off = [], 0
    for s in shapes:
        n = int(np.prod(s))
        out.append(flat[..., off:off + n].reshape(lead + tuple(s)))
        off += _padded_rows(n) * PACK_COLS
    return out


class _Lay:
    def __init__(self, shard, axis, width):
        self.shard, self.axis, self.width = shard, axis, width
        self.padded = tuple(width if a == axis else s for a, s in enumerate(shard))
        self.whole = tuple(N_DEV * width if a == axis else s for a, s in enumerate(shard))

    def pad(self, t, lead=0):
        extra = self.width - self.shard[self.axis]
        if not extra:
            return t
        cfg = [(0, 0)] * t.ndim
        cfg[lead + self.axis] = (0, extra)
        return jnp.pad(t, cfg)

    def unpad(self, t, lead=0):
        idx = [slice(None)] * t.ndim
        idx[lead + self.axis] = slice(0, self.shard[self.axis])
        return t[tuple(idx)]


_FF_ROW = _Lay((FF_SHARD, D_MODEL), 0, FF_SHARD_PAD)
TRANSPOSED = ("ffn1_w_gate", "ffn1_w_up", "ffn2_w_gate", "ffn2_w_up")
LAYOUTS = {
    "ffn1_w_gate": _FF_ROW, "ffn1_w_up": _FF_ROW, "ffn1_w_down": _FF_ROW,
    "ffn2_w_gate": _FF_ROW, "ffn2_w_up": _FF_ROW, "ffn2_w_down": _FF_ROW,
    "w_in": _Lay((D_MODEL, IN_WIDTH // N_DEV), 0, D_MODEL),
    "mlstm_conv_w": _Lay((3, 2 * MLSTM_WIDTH // N_DEV), 1, 2 * MLSTM_WIDTH // N_DEV),
    "w_branch_attn": _Lay((ATT_WIDTH, D_MODEL // N_DEV), 1, D_MODEL // N_DEV),
    "w_branch_mlstm": _Lay((MLSTM_WIDTH, D_MODEL // N_DEV), 1, D_MODEL // N_DEV),
    "w_out": _Lay((D_MODEL // N_DEV, D_MODEL), 0, D_MODEL // N_DEV),
}


def _window(ref, axis, j, width):
    idx = [slice(None)] * len(ref.shape)
    idx[axis] = pl.ds(pl.multiple_of(j * width, width), width)
    return ref.at[tuple(idx)]


ANY = pl.BlockSpec(memory_space=pl.ANY)


def _mesh_pos():
    return lax.axis_index("x"), lax.axis_index("y"), lax.axis_index("c")


def _all_gather(name, shard, vmem=False):
    R, C = shard.shape
    space = pl.BlockSpec(memory_space=pltpu.VMEM) if vmem else ANY

    def body(x_ref, out_ref, send_sems, recv_sems, local_sem):
        x, y, c = _mesh_pos()
        me, sibling = (x, y, c), (x, y, 1 - c)
        chips = [(1 - x, y), (x, 1 - y), (1 - x, 1 - y)]

        def slot(px, py, pc):
            return out_ref.at[4 * px + 2 * py + pc]

        def copy(k, block, to, src=None):
            return pltpu.make_async_remote_copy(
                src_ref=slot(*block) if src is None else src, dst_ref=slot(*block),
                send_sem=send_sems.at[k], recv_sem=recv_sems.at[k], device_id=to, device_id_type=MESH)

        mine = pltpu.make_async_copy(x_ref, slot(*me), local_sem)
        mine.start()
        first = [copy(0, me, sibling, src=x_ref)]
        first += [copy(1 + j, me, (*chip, c), src=x_ref) for j, chip in enumerate(chips)]
        for cp in first:
            cp.start()
        passed = [copy(4 + j, (*chip, c), sibling) for j, chip in enumerate(chips)]
        for j, chip in enumerate(chips):
            copy(1 + j, (*chip, c), me).wait_recv()
            passed[j].start()
        copy(0, sibling, me).wait_recv()
        for j, chip in enumerate(chips):
            copy(4 + j, (*chip, 1 - c), me).wait_recv()
        for cp in first + passed:
            cp.wait_send()
        mine.wait()

    return pl.pallas_call(
        body, name=name, out_shape=jax.ShapeDtypeStruct((N_DEV, R, C), shard.dtype),
        in_specs=[space], out_specs=space,
        scratch_shapes=[pltpu.SemaphoreType.DMA((7,)), pltpu.SemaphoreType.DMA((7,)), pltpu.SemaphoreType.DMA],
    )(shard)


HBM = pl.BlockSpec(memory_space=pltpu.HBM)
SEM = pl.BlockSpec(memory_space=pltpu.SEMAPHORE)
SPLIT_COPY = pltpu.CompilerParams(has_side_effects=pltpu.SideEffectType.DATAFLOW_SIDE_EFFECTING)
N_PEERS = N_DEV - 1


def _peers(x, y, c):
    return [(x, y, 1 - c), (1 - x, y, c), (x, 1 - y, c), (1 - x, 1 - y, c),
            (1 - x, y, 1 - c), (x, 1 - y, 1 - c), (1 - x, 1 - y, 1 - c)]


def _dev_index(pos):
    return 4 * pos[0] + 2 * pos[1] + pos[2]


def _place_own(name, shards, lays):
    nt = len(shards)
    me = _dev_index(_mesh_pos())

    def body(me_ref, *refs):
        for x_ref, o_ref in zip(refs[:nt], refs[nt:]):
            o_ref[...] = x_ref[...]

    def window_spec(lay):
        if lay.axis == 0:
            return pl.BlockSpec(lay.padded, lambda i, me_ref: (me_ref[0], 0))
        return pl.BlockSpec(lay.padded, lambda i, me_ref: (0, me_ref[0]))

    return pl.pallas_call(
        body, name=name,
        grid_spec=pltpu.PrefetchScalarGridSpec(
            num_scalar_prefetch=1, grid=(1,),
            in_specs=[pl.BlockSpec(lay.padded, lambda i, me_ref: (0, 0)) for lay in lays],
            out_specs=[window_spec(lay) for lay in lays]),
        out_shape=[jax.ShapeDtypeStruct(lay.whole, s.dtype) for s, lay in zip(shards, lays)],
        compiler_params=_cparams(("arbitrary",)),
    )(me.reshape(1).astype(jnp.int32), *shards)


NEAR_PEERS = 4


def _gather_start(name, shards, lands, lays, groups, n_peers, after):
    nt, ng = len(shards), len(groups)

    def body(*refs):
        x_refs, land_refs = refs[:nt], refs[nt:2 * nt]
        sems = refs[2 * nt + 1:2 * nt + 1 + 2 * ng]
        pos = _mesh_pos()
        me = _dev_index(pos)
        for g, tens in enumerate(groups):
            for i, t in enumerate(tens):
                for k, peer in enumerate(_peers(*pos)[:n_peers[g]]):
                    pltpu.make_async_remote_copy(
                        src_ref=x_refs[t], dst_ref=_window(land_refs[t], lays[t].axis, me, lays[t].width),
                        send_sem=sems[2 * g].at[n_peers[g] * i + k], recv_sem=sems[2 * g + 1].at[n_peers[g] * i + k],
                        device_id=peer, device_id_type=MESH).start()

    sem_shapes = []
    for g, tens in enumerate(groups):
        sem_shapes += [pltpu.SemaphoreType.DMA((n_peers[g] * len(tens),))] * 2
    thru = [pltpu.HBM(s.shape, s.dtype) for s in shards] + [pltpu.HBM(lay.whole, s.dtype) for s, lay in zip(shards, lays)]
    args = [pltpu.with_memory_space_constraint(s, pltpu.HBM) for s in shards]
    args += [pltpu.with_memory_space_constraint(ld, pltpu.HBM) for ld in lands]
    res = pl.pallas_call(
        body, name=name, out_shape=tuple(sem_shapes + thru), in_specs=[HBM] * (2 * nt) + [ANY],
        out_specs=tuple([SEM] * (2 * ng) + [HBM] * (2 * nt)),
        input_output_aliases={t: 2 * ng + t for t in range(2 * nt)}, compiler_params=SPLIT_COPY,
    )(*args, after)
    sems = [(res[2 * g], res[2 * g + 1]) for g in range(ng)]
    return sems, list(res[2 * ng:2 * ng + nt]), list(res[2 * ng + nt:])


def _gather_wait(name, sems, shards, lands, lays, n_peers, after):
    nt = len(shards)
    send_sems, recv_sems = sems

    def body(*refs):
        x_refs, land_refs = refs[:nt], refs[nt:2 * nt]
        send_ref, recv_ref = refs[2 * nt], refs[2 * nt + 1]
        pos = _mesh_pos()
        for t in range(nt):
            for k, peer in enumerate(_peers(*pos)[:n_peers]):
                cp = pltpu.make_async_remote_copy(
                    src_ref=x_refs[t], dst_ref=_window(land_refs[t], lays[t].axis, _dev_index(peer), lays[t].width),
                    send_sem=send_ref.at[n_peers * t + k], recv_sem=recv_ref.at[n_peers * t + k],
                    device_id=peer, device_id_type=MESH)
                cp.wait_send()
                cp.wait_recv()

    thru = [pltpu.HBM(s.shape, s.dtype) for s in shards] + [pltpu.HBM(ld.shape, ld.dtype) for ld in lands]
    res = pl.pallas_call(
        body, name=name, out_shape=tuple(thru), in_specs=[HBM] * (2 * nt) + [SEM, SEM, ANY],
        out_specs=tuple([HBM] * (2 * nt)), input_output_aliases={t: t for t in range(2 * nt)},
        compiler_params=SPLIT_COPY,
    )(*shards, *lands, send_sems, recv_sems, after)
    return list(res[nt:])


def _forward_to_sibling(name, lands, lays):
    nt = len(lands)

    def body(*refs):
        land_refs = refs[nt:2 * nt]
        send_sems, recv_sems = refs[2 * nt:]
        x, y, c = _mesh_pos()
        chips = [(1 - x, y), (x, 1 - y), (1 - x, 1 - y)]

        def copy(t, j, core):
            win = _window(land_refs[t], lays[t].axis, _dev_index((*chips[j], core)), lays[t].width)
            return pltpu.make_async_remote_copy(
                src_ref=win, dst_ref=win, send_sem=send_sems.at[3 * t + j], recv_sem=recv_sems.at[3 * t + j],
                device_id=(x, y, 1 - c), device_id_type=MESH)

        sends = [copy(t, j, c) for t in range(nt) for j in range(3)]
        for cp in sends:
            cp.start()
        for t in range(nt):
            for j in range(3):
                copy(t, j, 1 - c).wait_recv()
        for cp in sends:
            cp.wait_send()

    return pl.pallas_call(
        body, name=name, out_shape=[jax.ShapeDtypeStruct(ld.shape, ld.dtype) for ld in lands],
        in_specs=[ANY] * nt, out_specs=[ANY] * nt, input_output_aliases={t: t for t in range(nt)},
        scratch_shapes=[pltpu.SemaphoreType.DMA((3 * nt,)), pltpu.SemaphoreType.DMA((3 * nt,))],
    )(*lands)


def _pair_exchange(name, grads, lays):
    nt = len(grads)

    def body(*refs):
        g_refs, land_refs = refs[:nt], refs[nt:2 * nt]
        send_sems, recv_sems = refs[2 * nt:]
        x, y, c = _mesh_pos()
        copies = []
        for t in range(nt):
            for chip in range(4):
                copies.append(pltpu.make_async_remote_copy(
                    src_ref=_window(g_refs[t], lays[t].axis, 2 * chip + (1 - c), lays[t].width), dst_ref=land_refs[t].at[chip],
                    send_sem=send_sems.at[4 * t + chip], recv_sem=recv_sems.at[4 * t + chip],
                    device_id=(x, y, 1 - c), device_id_type=MESH))
        for cp in copies:
            cp.start()
        for cp in copies:
            cp.wait_recv()
        for cp in copies:
            cp.wait_send()

    out_shape = [jax.ShapeDtypeStruct((4,) + lay.padded, g.dtype) for g, lay in zip(grads, lays)]
    return pl.pallas_call(
        body, name=name, out_shape=out_shape, in_specs=[ANY] * nt, out_specs=[ANY] * nt,
        scratch_shapes=[pltpu.SemaphoreType.DMA((4 * nt,)), pltpu.SemaphoreType.DMA((4 * nt,))],
    )(*grads)


def _pair_sum(name, whole, landed, lay, out_dtype):
    R, C = lay.padded
    br = _first_divisor(R, (512, 384, 256, 128, 64, 32, 16, 8))
    nb = R // br
    if lay.axis == 0:
        mine_spec = pl.BlockSpec((br, C), lambda k, i, c_ref: ((2 * k + c_ref[0]) * nb + i, 0))
    else:
        mine_spec = pl.BlockSpec((br, C), lambda k, i, c_ref: (i, 2 * k + c_ref[0]))

    def body(c_ref, mine_ref, sib_ref, o_ref):
        o_ref[0] = (mine_ref[...].astype(F32) + sib_ref[0].astype(F32)).astype(out_dtype)

    c = lax.axis_index("c")
    return pl.pallas_call(
        body, name=name,
        grid_spec=pltpu.PrefetchScalarGridSpec(
            num_scalar_prefetch=1, grid=(4, nb),
            in_specs=[mine_spec, pl.BlockSpec((1, br, C), lambda k, i, c_ref: (k, i, 0))],
            out_specs=pl.BlockSpec((1, br, C), lambda k, i, c_ref: (k, i, 0))),
        out_shape=jax.ShapeDtypeStruct((4, R, C), out_dtype),
        compiler_params=_cparams(("parallel", "parallel")),
    )(c.reshape(1).astype(jnp.int32), whole, landed)


def _chip_exchange(name, sums):
    nt = len(sums)

    def body(*refs):
        s_refs, land_refs = refs[:nt], refs[nt:2 * nt]
        send_sems, recv_sems, local_sems = refs[2 * nt:]
        x, y, c = _mesh_pos()
        my_chip = 2 * x + y
        mine = [pltpu.make_async_copy(s_refs[t].at[my_chip], land_refs[t].at[my_chip], local_sems.at[t]) for t in range(nt)]
        for cp in mine:
            cp.start()
        chips = [(1 - x, y), (x, 1 - y), (1 - x, 1 - y)]
        copies = []
        for t in range(nt):
            for j, (px, py) in enumerate(chips):
                copies.append(pltpu.make_async_remote_copy(
                    src_ref=s_refs[t].at[2 * px + py], dst_ref=land_refs[t].at[my_chip],
                    send_sem=send_sems.at[3 * t + j], recv_sem=recv_sems.at[3 * t + j],
                    device_id=(px, py, c), device_id_type=MESH))
        for cp in copies:
            cp.start()
        for t in range(nt):
            for j, (px, py) in enumerate(chips):
                pltpu.make_async_remote_copy(
                    src_ref=s_refs[t].at[my_chip], dst_ref=land_refs[t].at[2 * px + py],
                    send_sem=send_sems.at[3 * t + j], recv_sem=recv_sems.at[3 * t + j],
                    device_id=(px, py, c), device_id_type=MESH).wait_recv()
        for cp in copies:
            cp.wait_send()
        for cp in mine:
            cp.wait()

    return pl.pallas_call(
        body, name=name, out_shape=[jax.ShapeDtypeStruct(s.shape, s.dtype) for s in sums],
        in_specs=[ANY] * nt, out_specs=[ANY] * nt,
        scratch_shapes=[pltpu.SemaphoreType.DMA((3 * nt,)), pltpu.SemaphoreType.DMA((3 * nt,)), pltpu.SemaphoreType.DMA((nt,))],
    )(*sums)


def _chip_start(name, sums):
    nt = len(sums)

    def body(*refs):
        s_refs, land_refs = refs[:nt], refs[nt:2 * nt]
        send_sems, recv_sems = refs[2 * nt], refs[2 * nt + 1]
        x, y, c = _mesh_pos()
        my_chip = 2 * x + y
        for t in range(nt):
            for j, (px, py) in enumerate([(1 - x, y), (x, 1 - y), (1 - x, 1 - y)]):
                pltpu.make_async_remote_copy(
                    src_ref=s_refs[t].at[2 * px + py], dst_ref=land_refs[t].at[my_chip],
                    send_sem=send_sems.at[3 * t + j], recv_sem=recv_sems.at[3 * t + j],
                    device_id=(px, py, c), device_id_type=MESH).start()

    thru = [pltpu.HBM(s.shape, s.dtype) for s in sums] * 2
    args = [pltpu.with_memory_space_constraint(s, pltpu.HBM) for s in sums]
    args += [pltpu.with_memory_space_constraint(lax.empty(s.shape, s.dtype), pltpu.HBM) for s in sums]
    res = pl.pallas_call(
        body, name=name, out_shape=tuple([pltpu.SemaphoreType.DMA((3 * nt,))] * 2 + thru), in_specs=[HBM] * (2 * nt),
        out_specs=tuple([SEM, SEM] + [HBM] * (2 * nt)), input_output_aliases={t: 2 + t for t in range(2 * nt)},
        compiler_params=SPLIT_COPY,
    )(*args)
    return (res[0], res[1]), list(res[2:2 + nt]), list(res[2 + nt:])


def _chip_wait(name, sems, sums, lands, after):
    nt = len(sums)

    def body(*refs):
        s_refs, land_refs = refs[:nt], refs[nt:2 * nt]
        send_sems, recv_sems = refs[2 * nt], refs[2 * nt + 1]
        x, y, c = _mesh_pos()
        my_chip = 2 * x + y
        for t in range(nt):
            for j, (px, py) in enumerate([(1 - x, y), (x, 1 - y), (1 - x, 1 - y)]):
                cp = pltpu.make_async_remote_copy(
                    src_ref=s_refs[t].at[my_chip], dst_ref=land_refs[t].at[2 * px + py],
                    send_sem=send_sems.at[3 * t + j], recv_sem=recv_sems.at[3 * t + j],
                    device_id=(px, py, c), device_id_type=MESH)
                cp.wait_send()
                cp.wait_recv()

    thru = [pltpu.HBM(s.shape, s.dtype) for s in sums] * 2
    res = pl.pallas_call(
        body, name=name, out_shape=tuple(thru), in_specs=[HBM] * (2 * nt) + [SEM, SEM, ANY],
        out_specs=tuple([HBM] * (2 * nt)), input_output_aliases={t: t for t in range(2 * nt)},
        compiler_params=SPLIT_COPY,
    )(*sums, *lands, sems[0], sems[1], after)
    return list(res[:nt]), list(res[nt:])


def _sum_chips(name, own, landed):
    _, R, C = own.shape
    br = _first_divisor(R, (512, 384, 256, 128, 64, 32, 16, 8))
    x, y, _ = _mesh_pos()
    slots = jnp.stack([2 * x + y, 2 * (1 - x) + y, 2 * x + (1 - y), 2 * (1 - x) + (1 - y)]).astype(jnp.int32)

    def body(slot_ref, mine_ref, a_ref, b_ref, c_ref, o_ref):
        o_ref[...] = ((mine_ref[0].astype(F32) + a_ref[0].astype(F32)) + b_ref[0].astype(F32)) + c_ref[0].astype(F32)

    def slot_spec(j):
        return pl.BlockSpec((1, br, C), lambda i, slot_ref: (slot_ref[j], i, 0))

    return pl.pallas_call(
        body, name=name,
        grid_spec=pltpu.PrefetchScalarGridSpec(
            num_scalar_prefetch=1, grid=(R // br,), in_specs=[slot_spec(0), slot_spec(1), slot_spec(2), slot_spec(3)],
            out_specs=pl.BlockSpec((br, C), lambda i, slot_ref: (i, 0))),
        out_shape=jax.ShapeDtypeStruct((R, C), F32), compiler_params=_cparams(("parallel",)),
    )(slots, own, landed, landed, landed)


def _sum_slots(name, slots, n):
    _, R, C = slots.shape
    br = _first_divisor(R, (512, 384, 256, 128, 64, 32, 16, 8))

    def body(s_ref, o_ref):
        acc = s_ref[0].astype(F32)
        for k in range(1, n):
            acc = acc + s_ref[k].astype(F32)
        o_ref[...] = acc

    return pl.pallas_call(
        body, name=name, grid=(R // br,), in_specs=[pl.BlockSpec((n, br, C), lambda i: (0, i, 0))],
        out_specs=pl.BlockSpec((br, C), lambda i: (i, 0)), out_shape=jax.ShapeDtypeStruct((R, C), F32),
        compiler_params=_cparams(("parallel",)),
    )(slots)


def _reduce_scatter_start(tag, names, grads):
    lays = [LAYOUTS[n] for n in names]
    landed = _pair_exchange("grads_pair_" + names[0], grads, lays)
    sums = [_pair_sum("grads_pairsum_" + n, g, ld, lay, BF16) for n, g, ld, lay in zip(names, grads, landed, lays)]
    sems, sums, lands = _chip_start(tag + "_chips_start", sums)
    return tag, names, sems, sums, lands


def _reduce_scatter_finish(pending, after):
    tag, names, sems, sums, lands = pending
    own, got = _chip_wait(tag + "_chips_wait", sems, sums, lands, after)
    return [_sum_chips("grads_sum_" + n, o, s) for n, o, s in zip(names, own, got)]


def _adamw_math(w, g, m, v):
    m = ADAM_B1 * m + (1.0 - ADAM_B1) * g
    v = ADAM_B2 * v + (1.0 - ADAM_B2) * jnp.square(g)
    m_hat = m / (1.0 - ADAM_B1 ** ADAM_STEP)
    v_hat = v / (1.0 - ADAM_B2 ** ADAM_STEP)
    delta = -ADAM_LR * (m_hat / (jnp.sqrt(v_hat) + ADAM_EPS) + ADAM_WD * w)
    return delta, m, v


def _adamw_layers(name, w, totals, m, v):
    _, R, C = w.shape
    br = _first_divisor(R, (512, 176, 128, 64, 32, 16, 8))
    Cp = totals[0].shape[1]

    def body(w_ref, g0_ref, g1_ref, m_ref, v_ref, g_out, d_out, m_out, v_out):
        g = jnp.where(pl.program_id(0) == 0, g0_ref[:, 0:C], g1_ref[:, 0:C])
        delta, m_new, v_new = _adamw_math(w_ref[0], g, m_ref[0], v_ref[0])
        g_out[0], d_out[0], m_out[0], v_out[0] = g, delta, m_new, v_new

    blk = pl.BlockSpec((1, br, C), lambda l, i: (l, i, 0))
    g_spec = pl.BlockSpec((br, Cp), lambda l, i: (i, 0))
    return pl.pallas_call(
        body, name=name, grid=(DEPTH, R // br), in_specs=[blk, g_spec, g_spec, blk, blk], out_specs=[blk] * 4,
        out_shape=[jax.ShapeDtypeStruct(w.shape, F32)] * 4, compiler_params=_cparams(("parallel", "parallel")),
    )(w, totals[0], totals[1], m, v)


def _adamw(name, w, g, m, v):
    shape = w.shape
    cols = shape[-1]
    rows = int(np.prod(shape[:-1]))
    br = _first_divisor(rows, (512, 352, 256, 128, 64, 32, 16, 8))
    args = [_In(a.reshape(rows, cols)) for a in (w, g, m, v)]
    outs = _rowwise(name, _adamw_math, args, [_Out(cols), _Out(cols), _Out(cols)], rows, br)
    return [o.reshape(shape) for o in outs]


GROUPS = {"ffn1": ("ffn1_w_gate", "ffn1_w_up", "ffn1_w_down"),
          "mix": ("w_in", "w_branch_attn", "w_branch_mlstm", "w_out"),
          "ffn2": ("ffn2_w_gate", "ffn2_w_up", "ffn2_w_down")}
GATHER_GROUPS = {"ffn1_in": ("ffn1_w_gate", "ffn1_w_up"), "ffn1_out": ("ffn1_w_down",),
                 "mix": ("w_in", "w_branch_attn", "w_branch_mlstm", "w_out"),
                 "ffn2_in": ("ffn2_w_gate", "ffn2_w_up"), "ffn2_out": ("ffn2_w_down",)}


def _small_params(small, conv_w, l):
    p = {}
    for n in ("ffn1_norm", "mix_norm", "ffn2_norm", "block_out_norm", "mlstm_out_norm", "attn_q_norm", "attn_k_norm"):
        p[n] = small[n][l][None, :]
    p["attn_sink"] = small["attn_sink"][l]
    p["gate_bias"] = jnp.pad(small["mlstm_gate_bias"][l], (0, LANES - MLSTM_N_GATES))[None, :]
    taps = _qk_perm_cols(conv_w[l], 1)
    conv_b = _qk_perm_cols(small["mlstm_conv_b"][l][None, :], 1)
    p["conv_w8"] = jnp.concatenate([taps, conv_b, jnp.zeros((4, 2 * MLSTM_WIDTH), F32)], axis=0)
    return p


def _w_in_from_slots(slots):
    w_in = slots.reshape(N_DEV, D_MODEL, IN_WIDTH // N_DEV).transpose(1, 0, 2).reshape(D_MODEL, IN_WIDTH)
    return _w_in_arrange(w_in)


def _w_in_to_slots(g):
    return _w_in_restore(g).reshape(D_MODEL, N_DEV, IN_WIDTH // N_DEV).transpose(1, 0, 2).reshape(
        N_DEV * D_MODEL, IN_WIDTH // N_DEV)


def _local_step(x, positions, target, weights_of, small, conv_w, on_grads):
    B, S, _ = x.shape
    T = B * S
    cos, sin = _rope_cos_sin(positions.reshape(T, 1))
    params = [_small_params(small, conv_w, l) for l in range(DEPTH)]
    xs = x.reshape(T, D_MODEL)
    tgt = target.reshape(T, D_MODEL)

    saved = []
    for l, p in enumerate(params):
        p.update(weights_of(l, "ffn1_in", xs))
        x1, s1, p["ffn1_w_down"] = _ffn_fwd("ffn1", xs, p["ffn1_norm"], p["ffn1_w_gate"], p["ffn1_w_up"],
                                            lambda after, l=l: weights_of(l, "ffn1_out", after)["ffn1_w_down"])
        p.update(weights_of(l, "mix", x1))
        p["w_in"] = _w_in_from_slots(p["w_in"])
        x2, s2 = _mix_fwd(x1, cos, sin, B, S, p)
        p.update(weights_of(l, "ffn2_in", x2))
        x3, s3, p["ffn2_w_down"] = _ffn_fwd("ffn2", x2, p["ffn2_norm"], p["ffn2_w_gate"], p["ffn2_w_up"],
                                            lambda after, l=l: weights_of(l, "ffn2_out", after)["ffn2_w_down"])
        saved.append((s1, s2, s3, x3))
        if l + 1 < DEPTH:
            xs = _block_norm_fwd(x3, p["block_out_norm"])

    sm = {}
    loss = None
    dx = None
    for l in reversed(range(DEPTH)):
        p = params[l]
        s1, s2, s3, x3 = saved[l]
        if l == DEPTH - 1:
            loss, dx, dgn = _loss_and_grad(x3, p["block_out_norm"], tgt)
        else:
            dx, dgn = _block_norm_bwd(x3, p["block_out_norm"], dx)
        sm["block_out_norm", l] = (dgn, 0, 1)
        dx, dg = _ffn_bwd("ffn2", s3, p["ffn2_norm"], p["ffn2_w_gate"], p["ffn2_w_up"], p["ffn2_w_down"], dx,
                          functools.partial(on_grads, l, "ffn2"))
        sm["ffn2_norm", l] = (dg, 0, 1)
        dx, g = _mix_bwd(s2, cos, sin, B, S, p, dx, functools.partial(on_grads, l, "mix"))
        dconv = _qk_unperm_cols(g["conv_w8"], 1)
        sm["mlstm_conv_w", l] = (dconv, 0, 3)
        sm["mlstm_conv_b", l] = (dconv, 3, 1)
        for n, key in (("mix_norm", "mix_norm"), ("mlstm_gate_bias", "gate_bias"), ("attn_q_norm", "attn_q_norm"),
                       ("attn_k_norm", "attn_k_norm"), ("attn_sink", "attn_sink"), ("mlstm_out_norm", "mlstm_out_norm")):
            sm[n, l] = (g[key], 0, 1)
        dx, dg = _ffn_bwd("ffn1", s1, p["ffn1_norm"], p["ffn1_w_gate"], p["ffn1_w_up"], p["ffn1_w_down"], dx,
                          functools.partial(on_grads, l, "ffn1"))
        sm["ffn1_norm", l] = (dg, 0, 1)
    return loss, dx.reshape(B, S, D_MODEL), sm


def kernel(x, positions, ffn1_norm, ffn1_w_gate, ffn1_w_up, ffn1_w_down, mix_norm, w_in, mlstm_gate_bias, attn_q_norm, attn_k_norm, attn_sink, mlstm_conv_w, mlstm_conv_b, mlstm_out_norm, w_branch_attn, w_branch_mlstm, w_out, ffn2_norm, ffn2_w_gate, ffn2_w_up, ffn2_w_down, block_out_norm, loss_target, m_ffn1_norm, m_ffn1_w_gate, m_ffn1_w_up, m_ffn1_w_down, m_mix_norm, m_w_in, m_mlstm_gate_bias, m_attn_q_norm, m_attn_k_norm, m_attn_sink, m_mlstm_conv_w, m_mlstm_conv_b, m_mlstm_out_norm, m_w_branch_attn, m_w_branch_mlstm, m_w_out, m_ffn2_norm, m_ffn2_w_gate, m_ffn2_w_up, m_ffn2_w_down, m_block_out_norm, v_ffn1_norm, v_ffn1_w_gate, v_ffn1_w_up, v_ffn1_w_down, v_mix_norm, v_w_in, v_mlstm_gate_bias, v_attn_q_norm, v_attn_k_norm, v_attn_sink, v_mlstm_conv_w, v_mlstm_conv_b, v_mlstm_out_norm, v_w_branch_attn, v_w_branch_mlstm, v_w_out, v_ffn2_norm, v_ffn2_w_gate, v_ffn2_w_up, v_ffn2_w_down, v_block_out_norm):
    args = locals()
    def stored(n, t):
        return t.transpose(0, 2, 1) if n in TRANSPOSED else t

    w = {n: stored(n, args[n]) for n in WEIGHTS}
    m = {n: stored(n, args["m_" + n]) for n in WEIGHTS}
    v = {n: stored(n, args["v_" + n]) for n in WEIGHTS}

    order = [(l, grp) for l in range(DEPTH) for grp in GATHER_GROUPS]
    keys = [(l, n) for l, grp in order for n in GATHER_GROUPS[grp]]
    lays = [LAYOUTS[n] for _, n in keys]
    shards = [lay.pad(w[n][l].astype(BF16)) for (l, n), lay in zip(keys, lays)]
    group_idx, at = {}, 0
    for l, grp in order:
        group_idx[(l, grp)] = list(range(at, at + len(GATHER_GROUPS[grp])))
        at += len(GATHER_GROUPS[grp])
    conv_shape = w["mlstm_conv_w"].shape
    conv_all = _all_gather("conv_all_gather", _pack_flat([w["mlstm_conv_w"]], F32, 8), vmem=True)
    conv_parts = _unpack_flat(conv_all, [conv_shape], lead=(N_DEV,))[0]
    conv_w = jnp.concatenate([conv_parts[j] for j in range(N_DEV)], axis=2)
    small = {n: w[n] for n in SMALL}

    lands = []
    for l, grp in order:
        idx = group_idx[(l, grp)]
        lands += _place_own("weights_place_" + grp, [shards[i] for i in idx], [lays[i] for i in idx])
    n_peers = [NEAR_PEERS if (l, grp) in ((0, "ffn1_in"), (0, "ffn1_out"), (0, "mix")) else N_PEERS for l, grp in order]
    sems, shards, lands = _gather_start("weights_gather_start", shards, lands, lays, [group_idx[k] for k in order],
                                        n_peers, conv_all)

    def weights_of(l, grp, after):
        idx, g = group_idx[(l, grp)], order.index((l, grp))
        group_lays = [lays[i] for i in idx]
        whole = _gather_wait(f"weights_gather_wait_{l}_{grp}", sems[g], [shards[i] for i in idx],
                             [lands[i] for i in idx], group_lays, n_peers[g], after)
        if n_peers[g] == NEAR_PEERS:
            whole = _forward_to_sibling("weights_forward_" + grp, whole, group_lays)
        return dict(zip(GATHER_GROUPS[grp], whole))

    totals, pending = {}, []

    def finish(after):
        tag, names = pending[0][0], pending[0][1]
        for n, t in zip(names, _reduce_scatter_finish(pending.pop(0), after)):
            totals[(tag, n)] = t

    def on_grads(l, grp, g, after):
        if pending:
            finish(after)
        names = GROUPS[grp]
        pending.append(_reduce_scatter_start(f"grads_{l}_{grp}", names, [g[n] for n in names]))
        return pending[-1][3][0]

    loss, grad_x, small_g = _local_step(x, positions, loss_target, weights_of, small, conv_w, on_grads)
    finish(grad_x)
    grads, deltas, new_m, new_v = {}, {}, {}, {}
    for grp, names in GROUPS.items():
        for n in names:
            grads[n], deltas[n], new_m[n], new_v[n] = _adamw_layers(
                "adamw_" + n, w[n], [totals[(f"grads_{l}_{grp}", n)] for l in range(DEPTH)], m[n], v[n])

    n_small = DEPTH * len(SMALL)
    taps_at, loss_at, rows = n_small, n_small + 3 * DEPTH, 32
    pieces = [small_g[n, l] for n in SMALL for l in range(DEPTH)]
    pieces += [small_g["mlstm_conv_w", l] for l in range(DEPTH)] + [(loss, 0, 1)]
    small_all = _all_gather("small_all_gather", _pack_rows("small_pack", pieces, rows), vmem=True)
    small_sum = _sum_slots("small_sum", small_all, N_DEV)
    loss_total = small_sum[loss_at, 0]
    x_pos, y_pos, c_pos = _mesh_pos()
    grads["mlstm_conv_w"] = lax.dynamic_slice_in_dim(
        small_sum[taps_at:loss_at].reshape(DEPTH, 3, 2 * MLSTM_WIDTH),
        (4 * x_pos + 2 * y_pos + c_pos) * conv_shape[2], conv_shape[2], axis=2)

    n = "mlstm_conv_w"
    deltas[n], new_m[n], new_v[n] = _adamw("adamw_" + n, w[n], grads[n], m[n], v[n])
    sw, smm, sv = (_pack_rows("small_pack_" + tag, [(d[n], 0, DEPTH) for n in SMALL], rows)
                   for tag, d in (("w", w), ("m", m), ("v", v)))
    sd, snm, snv = _adamw("adamw_small", sw, small_sum, smm, sv)
    for i, n in enumerate(SMALL):
        rows_n, width = slice(DEPTH * i, DEPTH * (i + 1)), w[n].shape[1]
        grads[n], deltas[n], new_m[n], new_v[n] = (buf[rows_n, :width] for buf in (small_sum, sd, snm, snv))

    return (loss_total.reshape(()), grad_x, *[stored(n, d[n]) for d in (grads, deltas, new_m, new_v) for n in WEIGHTS])
```

```python
import functools

import numpy as np
import jax
import jax.numpy as jnp
from jax import lax
from jax.experimental import pallas as pl
from jax.experimental.pallas import tpu as pltpu

F32 = jnp.float32
BF16 = jnp.bfloat16

D_MODEL = 1024
D_FF = 2816
ATT_HEAD_DIM = 64
ATT_HEADS = 8
ATT_KV_HEADS = 2
ATT_GROUP = ATT_HEADS // ATT_KV_HEADS
ATT_WIDTH = ATT_HEADS * ATT_HEAD_DIM
ATT_KV_WIDTH = ATT_KV_HEADS * ATT_HEAD_DIM
WINDOW = 128
ATT_BLOCK = 128
ROPE_DIM = 16
ROPE_THETA = 500000.0
MLSTM_HEADS = 4
MLSTM_HEAD_DIM = 128
MLSTM_WIDTH = MLSTM_HEADS * MLSTM_HEAD_DIM
MLSTM_CHUNK = 128
MLSTM_N_GATES = 4 * MLSTM_HEADS
NORM_EPS = 1e-6
IN_WIDTH = 4880
DEPTH = 2
N_DEV = 8

ADAM_LR = 0.001
ADAM_B1 = 0.9
ADAM_B2 = 0.999
ADAM_EPS = 1e-08
ADAM_WD = 0.01
ADAM_STEP = 10

LANES = 128
C_GMERGE = 0
C_QK = 2048
C_VM = 3072
C_OM = 3584
C_QA = 4096
C_KA = 4608
C_VA = 4736
C_GATES = 4864
IN_PAD = 4992

VMEM_LIMIT = 48 * 1024 * 1024

MESH = pl.DeviceIdType.MESH


def _cparams(sem):
    return pltpu.CompilerParams(dimension_semantics=sem, vmem_limit_bytes=VMEM_LIMIT)


def _first_divisor(n, cands):
    for c in cands:
        if n % c == 0:
            return c
    return n


_NN = ((1,), (0,))
_NT = ((1,), (1,))
_TN = ((0,), (0,))


def _mm(a, b, dims):
    return lax.dot_general(a.astype(BF16), b.astype(BF16), (dims, ((), ())), preferred_element_type=F32)


@jax.custom_vjp
def mm_nn(a, b):
    return _mm(a, b, _NN)


def _mm_nn_fwd(a, b):
    return _mm(a, b, _NN), (a, b)


def _mm_nn_bwd(res, g):
    a, b = res
    return _mm(g, b, _NT).astype(a.dtype), _mm(a, g, _TN).astype(b.dtype)


mm_nn.defvjp(_mm_nn_fwd, _mm_nn_bwd)


@jax.custom_vjp
def mm_nt(a, b):
    return _mm(a, b, _NT)


def _mm_nt_fwd(a, b):
    return _mm(a, b, _NT), (a, b)


def _mm_nt_bwd(res, g):
    a, b = res
    return _mm(g, b, _NN).astype(a.dtype), _mm(g, a, _TN).astype(b.dtype)


mm_nt.defvjp(_mm_nt_fwd, _mm_nt_bwd)


@jax.custom_vjp
def mm_tn(a, b):
    return _mm(a, b, _TN)


def _mm_tn_fwd(a, b):
    return _mm(a, b, _TN), (a, b)


def _mm_tn_bwd(res, g):
    a, b = res
    return _mm(b, g, _NT).astype(a.dtype), _mm(a, g, _NN).astype(b.dtype)


mm_tn.defvjp(_mm_tn_fwd, _mm_tn_bwd)


def _matmul(name, a, b, mode, out_dtype=F32, res=None, scale=1.0, bl=None, dep=None, whole_k=False):
    b_shape = b.shape if bl is None else b.shape[1:]
    if mode == "nn":
        (M, K), (K2, N) = a.shape, b_shape
    elif mode == "nt":
        (M, K), (N, K2) = a.shape, b_shape
    else:
        (K, M), (K2, N) = a.shape, b_shape
    assert K == K2, (name, a.shape, b.shape)
    tm = _first_divisor(M, (1024, 512, 384, 256, 128))
    tn = _first_divisor(N, (1024, 1664, 512, 384, 256, 128))
    tk = K if whole_k else _first_divisor(K, (1024, 1664, 512, 256, 128))
    if whole_k:
        tn = min(tn, 512)
    nk = K // tk
    if mode == "tn":
        a_spec = pl.BlockSpec((tk, tm), lambda i, j, k: (k, i))
    else:
        a_spec = pl.BlockSpec((tm, tk), lambda i, j, k: (i, k))
    if mode == "nt":
        b_blk, b_idx = (tn, tk), (lambda i, j, k: (j, k))
    else:
        b_blk, b_idx = (tk, tn), (lambda i, j, k: (k, j))
    if bl is None:
        b_spec = pl.BlockSpec(b_blk, b_idx)
    else:
        b_spec = pl.BlockSpec((None,) + b_blk, lambda i, j, k: (bl,) + b_idx(i, j, k))
    o_spec = pl.BlockSpec((tm, tn), lambda i, j, k: (i, j))
    dims = {"nn": _NN, "nt": _NT, "tn": _TN}[mode]
    has_res = res is not None

    def body(*refs):
        a_ref, b_ref = refs[:2]
        r_ref = refs[2] if has_res else None

        def finish(out):
            if scale != 1.0:
                out = out * scale
            if has_res:
                out = r_ref[...].astype(F32) + out
            o_ref[...] = out.astype(out_dtype)

        if nk == 1:
            o_ref = refs[-1]
            finish(_mm(a_ref[...], b_ref[...], dims))
            return
        o_ref, acc = refs[-2:]
        k = pl.program_id(2)

        @pl.when(k == 0)
        def _():
            acc[...] = jnp.zeros_like(acc)

        acc[...] += _mm(a_ref[...], b_ref[...], dims)

        @pl.when(k == nk - 1)
        def _():
            finish(acc[...])

    in_specs = [a_spec, b_spec] + ([o_spec] if has_res else [])
    args = (a, b) + ((res,) if has_res else ())
    if dep is not None:
        in_specs.append(pl.BlockSpec(memory_space=pl.ANY))
        args += (dep,)
    return pl.pallas_call(
        body, name=name, grid=(M // tm, N // tn, nk), in_specs=in_specs, out_specs=o_spec,
        out_shape=jax.ShapeDtypeStruct((M, N), out_dtype),
        scratch_shapes=[pltpu.VMEM((tm, tn), F32)] if nk > 1 else [],
        compiler_params=_cparams(("parallel", "parallel", "arbitrary")),
    )(*args)


class _In:
    def __init__(self, arr, width=None, base=0, split=False, rows=True):
        self.arr, self.base, self.split, self.rows = arr, base, split, rows
        self.width = arr.shape[1] if width is None else width


class _Out:
    def __init__(self, cols, dtype=F32, width=None, split=False, rows=True, nrows=1):
        self.cols, self.dtype, self.split, self.rows, self.nrows = cols, dtype, split, rows, nrows
        self.width = cols if width is None else width


def _rowwise(name, fn, ins, outs, n_rows, br, ncol=1):
    br = min(br, n_rows)
    assert n_rows % br == 0, (name, n_rows, br)
    nrow_blocks = n_rows // br

    def in_spec(d):
        nb = br if d.rows else d.arr.shape[0]
        if d.rows and d.split:
            im = lambda j, i, base=d.base: (i, base + j)
        elif d.rows:
            im = lambda j, i, base=d.base: (i, base)
        elif d.split:
            im = lambda j, i, base=d.base: (0, base + j)
        else:
            im = lambda j, i, base=d.base: (0, base)
        return pl.BlockSpec((nb, d.width), im)

    def out_spec(d):
        nb = br if d.rows else d.nrows
        if d.rows and d.split:
            im = lambda j, i: (i, j)
        elif d.rows:
            im = lambda j, i: (i, 0)
        elif d.split:
            im = lambda j, i: (0, j)
        else:
            im = lambda j, i: (0, 0)
        return pl.BlockSpec((nb, d.width), im)

    n_in = len(ins)

    def body(*refs):
        i = pl.program_id(1)
        vals = [r[...] for r in refs[:n_in]]
        res = fn(*vals)
        if not isinstance(res, (tuple, list)):
            res = (res,)
        for d, ref, val in zip(outs, refs[n_in:], res):
            if d.rows:
                ref[...] = val.astype(d.dtype)
            else:
                @pl.when(i == 0)
                def _(ref=ref):
                    ref[...] = jnp.zeros_like(ref)

                ref[...] += val.astype(d.dtype)

    out_shape = [jax.ShapeDtypeStruct((n_rows if d.rows else d.nrows, d.cols), d.dtype) for d in outs]
    res = pl.pallas_call(
        body, name=name, grid=(ncol, nrow_blocks), in_specs=[in_spec(d) for d in ins],
        out_specs=[out_spec(d) for d in outs], out_shape=out_shape,
        compiler_params=_cparams(("parallel", "arbitrary")),
    )(*[d.arr for d in ins])
    return res


def _rms(x, g):
    return x * lax.rsqrt(jnp.mean(x * x, axis=-1, keepdims=True) + NORM_EPS) * g


def _sigmoid(x):
    return 0.5 * jnp.tanh(0.5 * x) + 0.5


def _silu(x):
    return x * _sigmoid(x)


def _log_sigmoid(x):
    return jnp.minimum(x, 0.0) - jnp.log(1.0 + jnp.exp(-jnp.abs(x)))


def _rope_tables(pos, inv_freq_row):
    ang = pos.astype(F32) * inv_freq_row
    return jnp.cos(ang), jnp.sin(ang)


def _head_sums_impl(v):
    w = v.shape[-1]
    shift = ATT_HEAD_DIM.bit_length() - 1
    r = lax.shift_right_logical(lax.broadcasted_iota(jnp.int32, (w, w), 0), shift)
    c = lax.shift_right_logical(lax.broadcasted_iota(jnp.int32, (w, w), 1), shift)
    ones = (r == c).astype(BF16)
    hi = v.astype(BF16)
    lo = (v - hi.astype(F32)).astype(BF16)
    dn = (_NN, ((), ()))
    return (lax.dot_general(hi, ones, dn, preferred_element_type=F32)
            + lax.dot_general(lo, ones, dn, preferred_element_type=F32))


@jax.custom_vjp
def _head_sums(v):
    return _head_sums_impl(v)


_head_sums.defvjp(lambda v: (_head_sums_impl(v), None), lambda _, g: (_head_sums_impl(g),))


def _rotate_half_impl(y):
    w = y.shape[-1]
    half = ROPE_DIM // 2
    lane = lax.broadcasted_iota(jnp.int32, y.shape, 1) & (ATT_HEAD_DIM - 1)
    above = pltpu.roll(y, w - half, axis=1)
    below = pltpu.roll(y, half, axis=1)
    return jnp.where(lane < half, -above, jnp.where(lane < ROPE_DIM, below, 0.0))


@jax.custom_vjp
def _rotate_half(y):
    return _rotate_half_impl(y)


_rotate_half.defvjp(lambda y: (_rotate_half_impl(y), None), lambda _, g: (-_rotate_half_impl(g),))


def _qk_prep(t, g, cos, sin):
    reps = t.shape[-1] // cos.shape[-1]
    if reps > 1:
        cos, sin = jnp.tile(cos, (1, reps)), jnp.tile(sin, (1, reps))
    y = t * lax.rsqrt(_head_sums(t * t) * (1.0 / ATT_HEAD_DIM) + NORM_EPS) * g
    return y * cos + _rotate_half(y) * sin


def _attn_head(q, kb, vb, sink, valid):
    s = mm_nt(q, kb) * (ATT_HEAD_DIM ** -0.5)
    s = jnp.where(valid, s, -jnp.inf)
    m = jnp.maximum(jnp.max(s, axis=-1, keepdims=True), sink)
    p = jnp.exp(s - m)
    den = jnp.sum(p, axis=-1, keepdims=True) + jnp.exp(sink - m)
    return mm_nn(p * (1.0 / den), vb)


def _mlstm_chunk(q, k, v, li, lf, C, n, m, incl, incl_t, eye):
    k = k * (MLSTM_HEAD_DIM ** -0.5)
    lf_row = jnp.sum(eye * lf, axis=0, keepdims=True)
    li_row = jnp.sum(eye * li, axis=0, keepdims=True)
    b = jnp.sum(incl * lf_row, axis=1, keepdims=True)
    b_row = jnp.sum(incl_t * lf, axis=0, keepdims=True)
    b_tot = jnp.sum(lf, axis=0, keepdims=True)
    a = b_tot - b + li
    a_max = jnp.max(a, axis=0, keepdims=True)
    kw = k * jnp.exp(a - a_max)
    c_loc = mm_tn(kw, v)
    n_loc = jnp.sum(kw, axis=0, keepdims=True)

    dmat = jnp.where(incl > 0.5, b - b_row + li_row, -jnp.inf)
    inter = b + m
    m_t = jnp.maximum(inter, jnp.max(dmat, axis=1, keepdims=True))
    sc = mm_nt(q, k) * jnp.exp(dmat - m_t)
    scale_in = jnp.exp(inter - m_t)
    num = mm_nn(sc, v) + scale_in * mm_nn(q, C)
    den = jnp.sum(sc, axis=1, keepdims=True) + scale_in * jnp.sum(q * n, axis=1, keepdims=True)
    h = num * (1.0 / jnp.maximum(jnp.abs(den), jnp.exp(-m_t)))

    m_new = jnp.maximum(b_tot + m, a_max)
    s_p = jnp.exp(b_tot + m - m_new)
    s_l = jnp.exp(a_max - m_new)
    return h, s_p * C + s_l * c_loc, s_p * n + s_l * n_loc, m_new


def _mlstm_combine(hf, hb, o_pre, g):
    h = hf + hb
    mu = jnp.mean(h, axis=-1, keepdims=True)
    var = jnp.mean(jnp.square(h - mu), axis=-1, keepdims=True)
    return _sigmoid(o_pre) * ((h - mu) * lax.rsqrt(var + NORM_EPS) * g)


def _merge(ga, gm, za, zm):
    return _sigmoid(ga) * za + _sigmoid(gm) * zm


def _attn_mask(n, seq):
    shape = (ATT_GROUP * ATT_BLOCK, 3 * ATT_BLOCK)
    qi = n * ATT_BLOCK + (lax.broadcasted_iota(jnp.int32, shape, 0) & (ATT_BLOCK - 1))
    kj = (n - 1) * ATT_BLOCK + lax.broadcasted_iota(jnp.int32, shape, 1)
    return (jnp.abs(qi - kj) <= WINDOW) & (kj >= 0) & (kj < seq)


def _attn_specs(nq, v_base):
    q_spec = pl.BlockSpec((1, ATT_BLOCK, ATT_WIDTH), lambda b, n: (b, n, 0))

    def kv_spec(off, base=0):
        return pl.BlockSpec((1, ATT_BLOCK, ATT_KV_WIDTH), lambda b, n: (b, jnp.clip(n + off, 0, nq - 1), base))

    sink_spec = pl.BlockSpec((ATT_KV_HEADS, ATT_GROUP, 1, 1), lambda b, n: (0, 0, 0, 0))
    specs = [q_spec, kv_spec(-1), kv_spec(0), kv_spec(1), kv_spec(-1, v_base), kv_spec(0, v_base), kv_spec(1, v_base), sink_spec]
    return q_spec, specs, sink_spec


def _head(h):
    return slice(h * ATT_HEAD_DIM, (h + 1) * ATT_HEAD_DIM)


def _group_rows(q_ref, s_ref, h):
    q4 = jnp.concatenate([q_ref[0, :, _head(h * ATT_GROUP + g)] for g in range(ATT_GROUP)], axis=0)
    sink4 = jnp.concatenate([jnp.broadcast_to(s_ref[h, g], (ATT_BLOCK, 1)) for g in range(ATT_GROUP)], axis=0)
    return q4, sink4


def _attn_fwd(q, k, proj3, sink):
    B, S, _ = q.shape
    nq = S // ATT_BLOCK
    q_spec, specs, _ = _attn_specs(nq, C_VA // ATT_KV_WIDTH)

    def body(q_ref, kp, kc, kn, vp, vc, vn, s_ref, o_ref):
        valid = _attn_mask(pl.program_id(1), S)
        for h in range(ATT_KV_HEADS):
            kb = jnp.concatenate([kp[0, :, _head(h)], kc[0, :, _head(h)], kn[0, :, _head(h)]], axis=0)
            vb = jnp.concatenate([vp[0, :, _head(h)], vc[0, :, _head(h)], vn[0, :, _head(h)]], axis=0)
            q4, sink4 = _group_rows(q_ref, s_ref, h)
            o4 = _attn_head(q4, kb, vb, sink4, valid).astype(BF16)
            for g in range(ATT_GROUP):
                o_ref[0, :, _head(h * ATT_GROUP + g)] = o4[g * ATT_BLOCK:(g + 1) * ATT_BLOCK]

    return pl.pallas_call(
        body, name="attn_fwd", grid=(B, nq), in_specs=specs,
        out_specs=q_spec, out_shape=jax.ShapeDtypeStruct(q.shape, BF16),
        compiler_params=_cparams(("parallel", "arbitrary")),
    )(q, k, k, k, proj3, proj3, proj3, sink)


def _attn_bwd(q, k, proj3, sink, dy):
    B, S, _ = q.shape
    nq = S // ATT_BLOCK
    q_spec, specs, sink_spec = _attn_specs(nq, C_VA // ATT_KV_WIDTH)
    kv_full = pl.BlockSpec((1, S, ATT_KV_WIDTH), lambda b, n: (b, 0, 0))

    def body(q_ref, kp, kc, kn, vp, vc, vn, s_ref, dy_ref, dq_ref, dk_ref, dv_ref, ds_ref):
        b, n = pl.program_id(0), pl.program_id(1)
        valid = _attn_mask(n, S)

        @pl.when(n == 0)
        def _():
            dk_ref[...] = jnp.zeros_like(dk_ref)
            dv_ref[...] = jnp.zeros_like(dv_ref)

        @pl.when((n == 0) & (b == 0))
        def _():
            ds_ref[...] = jnp.zeros_like(ds_ref)

        for h in range(ATT_KV_HEADS):
            kb = jnp.concatenate([kp[0, :, _head(h)], kc[0, :, _head(h)], kn[0, :, _head(h)]], axis=0)
            vb = jnp.concatenate([vp[0, :, _head(h)], vc[0, :, _head(h)], vn[0, :, _head(h)]], axis=0)
            q4, sink4 = _group_rows(q_ref, s_ref, h)
            dy4 = jnp.concatenate([dy_ref[0, :, _head(h * ATT_GROUP + g)] for g in range(ATT_GROUP)], axis=0)
            _, vjp = jax.vjp(functools.partial(_attn_head, valid=valid), q4, kb, vb, sink4)
            dq4, dkb, dvb, dsink4 = vjp(dy4)
            for g in range(ATT_GROUP):
                rows = slice(g * ATT_BLOCK, (g + 1) * ATT_BLOCK)
                dq_ref[0, :, _head(h * ATT_GROUP + g)] = dq4[rows]
                ds_ref[h, g] += jnp.sum(dsink4[rows], axis=0, keepdims=True)
            for j, off in enumerate((-1, 0, 1)):
                start = pl.multiple_of(jnp.clip(n + off, 0, nq - 1) * ATT_BLOCK, ATT_BLOCK)
                rows = pl.ds(start, ATT_BLOCK)
                dk_ref[0, rows, _head(h)] += dkb[j * ATT_BLOCK:(j + 1) * ATT_BLOCK]
                dv_ref[0, rows, _head(h)] += dvb[j * ATT_BLOCK:(j + 1) * ATT_BLOCK]

    kv_shape = jax.ShapeDtypeStruct(k.shape, F32)
    return pl.pallas_call(
        body, name="attn_bwd", grid=(B, nq), in_specs=specs + [q_spec],
        out_specs=[q_spec, kv_full, kv_full, sink_spec],
        out_shape=[jax.ShapeDtypeStruct(q.shape, F32), kv_shape, kv_shape, jax.ShapeDtypeStruct(sink.shape, F32)],
        compiler_params=_cparams(("arbitrary", "arbitrary")),
    )(q, k, k, k, proj3, proj3, proj3, sink, dy)


CONV_COLS = 256


def _conv_taps(u, seq):
    row = lax.broadcasted_iota(jnp.int32, u.shape, 0)
    prev = jnp.where(row == 0, 0.0, pltpu.roll(u, 1, axis=0))
    nxt = jnp.where(row == seq - 1, 0.0, pltpu.roll(u, seq - 1, axis=0))
    return prev, nxt


def _conv_fwd(proj3, w8):
    B, S, _ = proj3.shape
    ncb = 2 * MLSTM_WIDTH // CONV_COLS

    def body(u_ref, w_ref, o_ref):
        u = u_ref[0]
        prev, nxt = _conv_taps(u, S)
        o_ref[0] = _silu(prev * w_ref[0:1, :] + u * w_ref[1:2, :] + nxt * w_ref[2:3, :] + w_ref[3:4, :])

    return pl.pallas_call(
        body, name="conv_fwd", grid=(B, ncb),
        in_specs=[pl.BlockSpec((1, S, CONV_COLS), lambda b, c: (b, 0, C_QK // CONV_COLS + c)),
                  pl.BlockSpec((8, CONV_COLS), lambda b, c: (0, c))],
        out_specs=pl.BlockSpec((1, S, CONV_COLS), lambda b, c: (b, 0, c)),
        out_shape=jax.ShapeDtypeStruct((B, S, 2 * MLSTM_WIDTH), F32),
        compiler_params=_cparams(("parallel", "parallel")),
    )(proj3, w8)


def _conv_bwd(proj3, w8, dout_f, dout_b):
    B, S, _ = proj3.shape
    ncb = 2 * MLSTM_WIDTH // CONV_COLS

    def body(u_ref, w_ref, df_ref, db_ref, du_ref, dw_ref):
        b = pl.program_id(1)
        u = u_ref[0]
        prev, nxt = _conv_taps(u, S)
        w0, w1, w2 = w_ref[0:1, :], w_ref[1:2, :], w_ref[2:3, :]
        pre = prev * w0 + u * w1 + nxt * w2 + w_ref[3:4, :]
        sig = _sigmoid(pre)
        dpre = (df_ref[0] + db_ref[0]) * (sig * (1.0 + pre * (1.0 - sig)))
        dprev, dnxt = _conv_taps(dpre, S)
        du_ref[0] = (dnxt * w0 + dpre * w1 + dprev * w2).astype(BF16)

        @pl.when(b == 0)
        def _():
            dw_ref[...] = jnp.zeros_like(dw_ref)

        dw_ref[0:1, :] += jnp.sum(dpre * prev, axis=0, keepdims=True)
        dw_ref[1:2, :] += jnp.sum(dpre * u, axis=0, keepdims=True)
        dw_ref[2:3, :] += jnp.sum(dpre * nxt, axis=0, keepdims=True)
        dw_ref[3:4, :] += jnp.sum(dpre, axis=0, keepdims=True)

    blk = pl.BlockSpec((1, S, CONV_COLS), lambda c, b: (b, 0, c))
    return pl.pallas_call(
        body, name="conv_bwd", grid=(ncb, B),
        in_specs=[pl.BlockSpec((1, S, CONV_COLS), lambda c, b: (b, 0, C_QK // CONV_COLS + c)),
                  pl.BlockSpec((8, CONV_COLS), lambda c, b: (0, c)), blk, blk],
        out_specs=[blk, pl.BlockSpec((8, CONV_COLS), lambda c, b: (0, c))],
        out_shape=[jax.ShapeDtypeStruct((B, S, 2 * MLSTM_WIDTH), BF16), jax.ShapeDtypeStruct((8, 2 * MLSTM_WIDTH), F32)],
        compiler_params=_cparams(("parallel", "arbitrary")),
    )(proj3, w8, dout_f, dout_b)


MLSTM_HEADS_PER_STEP = 4


def _chunk_masks(direction):
    t = lax.broadcasted_iota(jnp.int32, (MLSTM_CHUNK, MLSTM_CHUNK), 0)
    s = lax.broadcasted_iota(jnp.int32, (MLSTM_CHUNK, MLSTM_CHUNK), 1)
    le, ge = (s <= t).astype(F32), (s >= t).astype(F32)
    eye = (s == t).astype(F32)
    return (le, ge, eye) if direction == 0 else (ge, le, eye)


def _gate_cols(gates, direction, head):
    lane = lax.broadcasted_iota(jnp.int32, gates.shape, 1)
    sel_i = (lane == (2 * direction) * MLSTM_HEADS + head).astype(F32)
    sel_f = (lane == (2 * direction + 1) * MLSTM_HEADS + head).astype(F32)
    return sel_i, sel_f


def _mlstm_fwd(qk, proj3, bias):
    B, S, _ = qk.shape
    nc = S // MLSTM_CHUNK
    H, L, DH = MLSTM_HEADS, MLSTM_CHUNK, MLSTM_HEAD_DIM

    def chunk_of(d, c):
        return c if d == 0 else nc - 1 - c

    HS = MLSTM_HEADS_PER_STEP

    def body(qkf, qkb, vf, vb, gf, gb, bias_ref, hf, hb, csf, csb, nsf, nsb, msf, msb, c_st, n_st, m_st):
        c, hg = pl.program_id(1), pl.program_id(2)

        @pl.when(c == 0)
        def _():
            for d in range(2):
                for j in range(HS):
                    c_st[d, hg * HS + j] = jnp.zeros((DH, DH), F32)
                    n_st[d, hg * HS + j] = jnp.zeros((1, DH), F32)
                    m_st[d, hg * HS + j] = jnp.zeros((1, DH), F32)

        for d, (qk_ref, v_ref, g_ref, h_ref, cs, ns, ms) in enumerate(
                ((qkf, vf, gf, hf, csf, nsf, msf), (qkb, vb, gb, hb, csb, nsb, msb))):
            incl, incl_t, eye = _chunk_masks(d)
            gates = g_ref[0] + bias_ref[...]
            log_f = _log_sigmoid(gates)
            for j in range(HS):
                h = hg * HS + j
                sel_i, sel_f = _gate_cols(gates, d, h)
                li = jnp.sum(gates * sel_i, axis=1, keepdims=True)
                lf = jnp.sum(log_f * sel_f, axis=1, keepdims=True)
                c_in, n_in, m_in = c_st[d, h], n_st[d, h], m_st[d, h]
                cs[0, 0, j], ns[0, 0, j], ms[0, 0, j] = c_in, n_in, m_in
                hh, c_new, n_new, m_new = _mlstm_chunk(
                    qk_ref[0, :, 2 * j * DH:(2 * j + 1) * DH], qk_ref[0, :, (2 * j + 1) * DH:(2 * j + 2) * DH],
                    v_ref[0, :, j * DH:(j + 1) * DH], li, lf, c_in, n_in,
                    jnp.max(m_in, axis=1, keepdims=True), incl, incl_t, eye)
                h_ref[0, :, j * DH:(j + 1) * DH] = hh
                c_st[d, h], n_st[d, h] = c_new, n_new
                m_st[d, h] = jnp.broadcast_to(m_new, (1, DH))

    def tok_spec(width, base, d, per_head):
        return pl.BlockSpec((1, L, width), lambda b, c, h: (b, chunk_of(d, c), base + (h if per_head else 0)))

    def st_spec(shape, d):
        return pl.BlockSpec((1, 1, HS) + shape, lambda b, c, h: (b, chunk_of(d, c), h, 0, 0))

    in_specs = [tok_spec(2 * HS * DH, 0, 0, True), tok_spec(2 * HS * DH, 0, 1, True),
                tok_spec(HS * DH, C_VM // (HS * DH), 0, True), tok_spec(HS * DH, C_VM // (HS * DH), 1, True),
                tok_spec(LANES, C_GATES // LANES, 0, False), tok_spec(LANES, C_GATES // LANES, 1, False),
                pl.BlockSpec((1, LANES), lambda b, c, h: (0, 0))]
    out_specs = [tok_spec(HS * DH, 0, 0, True), tok_spec(HS * DH, 0, 1, True),
                 st_spec((DH, DH), 0), st_spec((DH, DH), 1), st_spec((1, DH), 0), st_spec((1, DH), 1),
                 st_spec((1, DH), 0), st_spec((1, DH), 1)]
    hs = jax.ShapeDtypeStruct((B, S, H * DH), F32)
    cs = jax.ShapeDtypeStruct((B, nc, H, DH, DH), F32)
    vs = jax.ShapeDtypeStruct((B, nc, H, 1, DH), F32)
    return pl.pallas_call(
        body, name="mlstm_fwd", grid=(B, nc, H // HS), in_specs=in_specs, out_specs=out_specs,
        out_shape=[hs, hs, cs, cs, vs, vs, vs, vs],
        scratch_shapes=[pltpu.VMEM((2, H, DH, DH), F32), pltpu.VMEM((2, H, 1, DH), F32), pltpu.VMEM((2, H, 1, DH), F32)],
        compiler_params=_cparams(("parallel", "arbitrary", "arbitrary")),
    )(qk, qk, proj3, proj3, proj3, proj3, bias)


def _mlstm_bwd(qk, proj3, bias, states, dh):
    B, S, _ = qk.shape
    nc = S // MLSTM_CHUNK
    H, L, DH = MLSTM_HEADS, MLSTM_CHUNK, MLSTM_HEAD_DIM

    def chunk_of(d, c):
        return nc - 1 - c if d == 0 else c

    HS = MLSTM_HEADS_PER_STEP

    def body(qkf, qkb, vf, vb, gf, gb, bias_ref, csf, csb, nsf, nsb, msf, msb, dhf, dhb,
             dqkf, dqkb, dvf, dvb, dgf, dgb, dc_st, dn_st, dm_st):
        c, hg = pl.program_id(1), pl.program_id(2)

        @pl.when(c == 0)
        def _():
            for d in range(2):
                for j in range(HS):
                    dc_st[d, hg * HS + j] = jnp.zeros((DH, DH), F32)
                    dn_st[d, hg * HS + j] = jnp.zeros((1, DH), F32)
                    dm_st[d, hg * HS + j] = jnp.zeros((1, DH), F32)

        @pl.when(hg == 0)
        def _():
            dgf[...] = jnp.zeros_like(dgf)
            dgb[...] = jnp.zeros_like(dgb)

        for d, (qk_ref, v_ref, g_ref, cs, ns, ms, dh_ref, dqk_ref, dv_ref, dg_ref) in enumerate(
                ((qkf, vf, gf, csf, nsf, msf, dhf, dqkf, dvf, dgf), (qkb, vb, gb, csb, nsb, msb, dhb, dqkb, dvb, dgb))):
            incl, incl_t, eye = _chunk_masks(d)
            gates = g_ref[0] + bias_ref[...]
            log_f = _log_sigmoid(gates)
            d_li = jnp.zeros_like(gates)
            d_lf = jnp.zeros_like(gates)
            for j in range(HS):
                h = hg * HS + j
                sel_i, sel_f = _gate_cols(gates, d, h)
                li = jnp.sum(gates * sel_i, axis=1, keepdims=True)
                lf = jnp.sum(log_f * sel_f, axis=1, keepdims=True)
                m_in = jnp.max(ms[0, 0, j], axis=1, keepdims=True)
                _, vjp = jax.vjp(
                    functools.partial(_mlstm_chunk, incl=incl, incl_t=incl_t, eye=eye),
                    qk_ref[0, :, 2 * j * DH:(2 * j + 1) * DH], qk_ref[0, :, (2 * j + 1) * DH:(2 * j + 2) * DH],
                    v_ref[0, :, j * DH:(j + 1) * DH], li, lf, cs[0, 0, j], ns[0, 0, j], m_in)
                dm_out = jnp.max(dm_st[d, h], axis=1, keepdims=True)
                dq, dk, dv, dli, dlf, dc, dn, dm = vjp((dh_ref[0, :, j * DH:(j + 1) * DH], dc_st[d, h], dn_st[d, h], dm_out))
                dqk_ref[0, :, 2 * j * DH:(2 * j + 1) * DH] = dq
                dqk_ref[0, :, (2 * j + 1) * DH:(2 * j + 2) * DH] = dk
                dv_ref[0, :, j * DH:(j + 1) * DH] = dv
                d_li += dli * sel_i
                d_lf += dlf * sel_f
                dc_st[d, h], dn_st[d, h] = dc, dn
                dm_st[d, h] = jnp.broadcast_to(dm, (1, DH))
            dg_ref[0] += d_li + d_lf * _sigmoid(-gates)

    def tok_spec(width, base, d, per_head):
        return pl.BlockSpec((1, L, width), lambda b, c, h: (b, chunk_of(d, c), base + (h if per_head else 0)))

    def st_spec(shape, d):
        return pl.BlockSpec((1, 1, HS) + shape, lambda b, c, h: (b, chunk_of(d, c), h, 0, 0))

    in_specs = [tok_spec(2 * HS * DH, 0, 0, True), tok_spec(2 * HS * DH, 0, 1, True),
                tok_spec(HS * DH, C_VM // (HS * DH), 0, True), tok_spec(HS * DH, C_VM // (HS * DH), 1, True),
                tok_spec(LANES, C_GATES // LANES, 0, False), tok_spec(LANES, C_GATES // LANES, 1, False),
                pl.BlockSpec((1, LANES), lambda b, c, h: (0, 0)),
                st_spec((DH, DH), 0), st_spec((DH, DH), 1), st_spec((1, DH), 0), st_spec((1, DH), 1),
                st_spec((1, DH), 0), st_spec((1, DH), 1), tok_spec(HS * DH, 0, 0, True), tok_spec(HS * DH, 0, 1, True)]
    out_specs = [tok_spec(2 * HS * DH, 0, 0, True), tok_spec(2 * HS * DH, 0, 1, True),
                 tok_spec(HS * DH, 0, 0, True), tok_spec(HS * DH, 0, 1, True),
                 tok_spec(LANES, 0, 0, False), tok_spec(LANES, 0, 1, False)]
    qks = jax.ShapeDtypeStruct((B, S, 2 * H * DH), F32)
    vs = jax.ShapeDtypeStruct((B, S, H * DH), F32)
    gs = jax.ShapeDtypeStruct((B, S, LANES), F32)
    csf, csb, nsf, nsb, msf, msb = states
    return pl.pallas_call(
        body, name="mlstm_bwd", grid=(B, nc, H // HS), in_specs=in_specs, out_specs=out_specs,
        out_shape=[qks, qks, vs, vs, gs, gs],
        scratch_shapes=[pltpu.VMEM((2, H, DH, DH), F32), pltpu.VMEM((2, H, 1, DH), F32), pltpu.VMEM((2, H, 1, DH), F32)],
        compiler_params=_cparams(("parallel", "arbitrary", "arbitrary")),
    )(qk, qk, proj3, proj3, proj3, proj3, bias, csf, csb, nsf, nsb, msf, msb, dh, dh)


ROW_BLOCK = 256
FF_COLS = 512
FF_SHARD = D_FF // N_DEV
FF_SHARD_PAD = 384
FF_PAD = N_DEV * FF_SHARD_PAD


def _rms_fwd(name, x, g):
    T = x.shape[0]
    return _rowwise(name, lambda xv, gv: _rms(xv, gv), [_In(x), _In(g, rows=False)], [_Out(D_MODEL, BF16)], T, ROW_BLOCK)[0]


def _rms_bwd(name, x, g, dh, dres):
    T = x.shape[0]

    def fn(xv, gv, dhv, drv):
        _, vjp = jax.vjp(_rms, xv, gv)
        dx, dg = vjp(dhv)
        return drv + dx, dg

    return _rowwise(name, fn, [_In(x), _In(g, rows=False), _In(dh), _In(dres)],
                    [_Out(D_MODEL), _Out(D_MODEL, rows=False)], T, ROW_BLOCK)


def _mmw(name, a, w, mode, **kw):
    if isinstance(w, tuple):
        return _matmul(name, a, w[0], mode, bl=w[1], **kw)
    return _matmul(name, a, w, mode, **kw)


def _swiglu(gate, up):
    return _silu(gate) * up


def _ffn_in(name, h, wg, wu):
    (M, K), N = h.shape, wg.shape[0]
    tm, tn = _first_divisor(M, (1024, 512, 256, 128)), FF_COLS

    def body(h_ref, wg_ref, wu_ref, g_ref, u_ref, a_ref):
        hv = h_ref[...]
        gate = _mm(hv, wg_ref[...], _NT)
        up = _mm(hv, wu_ref[...], _NT)
        g_ref[...], u_ref[...] = gate.astype(BF16), up.astype(BF16)
        a_ref[...] = _swiglu(gate, up).astype(BF16)

    w_spec = pl.BlockSpec((tn, K), lambda i, j: (j, 0))
    o_spec = pl.BlockSpec((tm, tn), lambda i, j: (i, j))
    return pl.pallas_call(
        body, name=name, grid=(M // tm, N // tn), in_specs=[pl.BlockSpec((tm, K), lambda i, j: (i, 0)), w_spec, w_spec],
        out_specs=[o_spec, o_spec, o_spec],
        out_shape=[jax.ShapeDtypeStruct((M, N), BF16)] * 3,
        compiler_params=_cparams(("parallel", "parallel")),
    )(h, wg, wu)


def _ffn_dact(name, dx, wd, gate, up):
    (M, K), N = dx.shape, wd.shape[0]
    tm, tn = _first_divisor(M, (1024, 512, 256, 128)), FF_COLS

    def body(dx_ref, wd_ref, g_ref, u_ref, dg_ref, du_ref):
        dact = _mm(dx_ref[...], wd_ref[...], _NT) * 0.5
        gate, up = g_ref[...].astype(F32), u_ref[...].astype(F32)
        s = _sigmoid(gate)
        silu = gate * s
        dg_ref[...] = (dact * up * (s + silu * (1.0 - s))).astype(BF16)
        du_ref[...] = (dact * silu).astype(BF16)

    o_spec = pl.BlockSpec((tm, tn), lambda i, j: (i, j))
    return pl.pallas_call(
        body, name=name, grid=(M // tm, N // tn),
        in_specs=[pl.BlockSpec((tm, K), lambda i, j: (i, 0)), pl.BlockSpec((tn, K), lambda i, j: (j, 0)), o_spec, o_spec],
        out_specs=[o_spec, o_spec],
        out_shape=[jax.ShapeDtypeStruct((M, N), BF16), jax.ShapeDtypeStruct((M, N), BF16)],
        compiler_params=_cparams(("parallel", "parallel")),
    )(dx, wd, gate, up)


def _ffn_dh(name, dgate, dup, wg, wu, dep, x, gain, dres):
    (M, K), N = dgate.shape, wg.shape[1]
    tm, tk = _first_divisor(M, (512, 256, 128)), _first_divisor(K, (1024, 512, 384, 256, 128))
    nk = K // tk

    def body(dg_ref, du_ref, wg_ref, wu_ref, x_ref, gain_ref, dres_ref, dep_ref, o_ref, dgain_ref, acc):
        i, k = pl.program_id(0), pl.program_id(1)

        @pl.when(k == 0)
        def _():
            acc[...] = jnp.zeros_like(acc)

        acc[...] += _mm(dg_ref[...], wg_ref[...], _NN) + _mm(du_ref[...], wu_ref[...], _NN)

        @pl.when((k == nk - 1) & (i == 0))
        def _():
            dgain_ref[...] = jnp.zeros_like(dgain_ref)

        @pl.when(k == nk - 1)
        def _():
            _, vjp = jax.vjp(_rms, x_ref[...], gain_ref[...])
            dx, dgain = vjp(acc[...])
            o_ref[...] = dres_ref[...] + dx
            dgain_ref[...] += dgain

    a_spec = pl.BlockSpec((tm, tk), lambda i, k: (i, k))
    w_spec = pl.BlockSpec((tk, N), lambda i, k: (k, 0))
    row_spec = pl.BlockSpec((tm, N), lambda i, k: (i, 0))
    gain_spec = pl.BlockSpec((1, N), lambda i, k: (0, 0))
    return pl.pallas_call(
        body, name=name, grid=(M // tm, nk),
        in_specs=[a_spec, a_spec, w_spec, w_spec, row_spec, gain_spec, row_spec, pl.BlockSpec(memory_space=pl.ANY)],
        out_specs=[row_spec, gain_spec],
        out_shape=[jax.ShapeDtypeStruct((M, N), F32), jax.ShapeDtypeStruct((1, N), F32)],
        scratch_shapes=[pltpu.VMEM((tm, N), F32)], compiler_params=_cparams(("arbitrary", "arbitrary")),
    )(dgate, dup, wg, wu, x, gain, dres, dep)


def _ffn_fwd(tag, x, g, wg, wu, wd):
    h = _rms_fwd(tag + "_norm", x, g)
    gate, up, act = _ffn_in(tag + "_in", h, wg, wu)
    if callable(wd):
        wd = wd(act)
    out = _mmw(tag + "_down", act, wd, "nn", res=x, scale=0.5, whole_k=True)
    return out, (x, h, gate, up, act), wd


def _ffn_bwd(tag, saved, g, wg, wu, wd, dx, on_dw):
    x, h, gate, up, act = saved
    dgate, dup = _ffn_dact(tag + "_dact", dx, wd, gate, up)
    dwd = _matmul(tag + "_dwd", act, dx, "tn", scale=0.5, out_dtype=BF16)
    dwg = _matmul(tag + "_dwg", dgate, h, "tn", out_dtype=BF16, whole_k=True)
    dwu = _matmul(tag + "_dwu", dup, h, "tn", out_dtype=BF16, whole_k=True)
    token = on_dw({tag + "_w_gate": dwg, tag + "_w_up": dwu, tag + "_w_down": dwd}, dwu)
    return _ffn_dh(tag + "_dh", dgate, dup, wg, wu, token, x, g, dx)


def _rope_cos_sin(positions):
    half = ROPE_DIM // 2
    inv_freq = jnp.power(jnp.float32(ROPE_THETA), -jnp.arange(half, dtype=F32) * (2.0 / ROPE_DIM))
    head = jnp.zeros((ATT_HEAD_DIM,), F32).at[:ROPE_DIM].set(jnp.concatenate([inv_freq, inv_freq]))
    row = jnp.tile(head, LANES // ATT_HEAD_DIM)[None, :]
    T = positions.shape[0]
    return _rowwise("rope_tables", _rope_tables, [_In(positions), _In(row, rows=False)], [_Out(LANES), _Out(LANES)], T, 1024)


def _prep_fwd(name, src, width, base, g, cos, sin):
    return _rowwise(name, _qk_prep, [_In(src, width, base), _In(g, rows=False), _In(cos), _In(sin)],
                    [_Out(width)], src.shape[0], 512)[0]


def _prep_bwd(name, src, width, base, g, cos, sin, dout):
    def fn(tv, gv, cv, sv, dv):
        _, vjp = jax.vjp(lambda a, b: _qk_prep(a, b, cv, sv), tv, gv)
        return vjp(dv)

    return _rowwise(name, fn, [_In(src, width, base), _In(g, rows=False), _In(cos), _In(sin), _In(dout)],
                    [_Out(width, BF16), _Out(width, rows=False)], src.shape[0], 512)


def _to_heads(t, B, S, nh):
    return t.reshape(B, S, nh, ATT_HEAD_DIM).transpose(0, 2, 1, 3)


def _from_heads(t):
    B, nh, S, _ = t.shape
    return t.transpose(0, 2, 1, 3).reshape(B * S, nh * ATT_HEAD_DIM)


def _mix_fwd(x, cos, sin, B, S, p):
    T = B * S
    h = _rms_fwd("mix_norm", x, p["mix_norm"])
    proj = _matmul("mix_proj", h, p["w_in"], "nn")
    proj3 = proj.reshape(B, S, IN_PAD)
    q_gain = jnp.tile(p["attn_q_norm"], (1, ATT_HEADS))
    k_gain = jnp.tile(p["attn_k_norm"], (1, ATT_KV_HEADS))
    q_r = _prep_fwd("q_prep", proj, ATT_WIDTH, C_QA // ATT_WIDTH, q_gain, cos, sin)
    k_r = _prep_fwd("k_prep", proj, ATT_KV_WIDTH, C_KA // ATT_KV_WIDTH, k_gain, cos, sin)
    qh = q_r.reshape(B, S, ATT_WIDTH)
    kh = k_r.reshape(B, S, ATT_KV_WIDTH)
    sink = p["attn_sink"].reshape(ATT_KV_HEADS, ATT_GROUP, 1, 1)
    y_a = _attn_fwd(qh, kh, proj3, sink).reshape(T, ATT_WIDTH)

    qk_c = _conv_fwd(proj3, p["conv_w8"])
    hf, hb, *states = _mlstm_fwd(qk_c, proj3, p["gate_bias"])
    hf2, hb2 = hf.reshape(T, MLSTM_WIDTH), hb.reshape(T, MLSTM_WIDTH)
    DH = MLSTM_HEAD_DIM
    y_m = _rowwise("mlstm_out", _mlstm_combine,
                   [_In(hf2, DH, split=True), _In(hb2, DH, split=True), _In(proj, DH, C_OM // DH, split=True),
                    _In(p["mlstm_out_norm"], DH, split=True, rows=False)],
                   [_Out(MLSTM_WIDTH, BF16, DH, split=True)], T, 1024, ncol=MLSTM_HEADS)[0]

    za = _mmw("branch_a", y_a, p["w_branch_attn"], "nn")
    zm = _mmw("branch_m", y_m, p["w_branch_mlstm"], "nn")
    W = 512
    merged = _rowwise("merge", _merge,
                      [_In(proj, W, C_GMERGE // W, split=True), _In(proj, W, (C_GMERGE + D_MODEL) // W, split=True),
                       _In(za, W, split=True), _In(zm, W, split=True)],
                      [_Out(D_MODEL, BF16, W, split=True)], T, 512, ncol=D_MODEL // W)[0]
    out = _mmw("mix_out", merged, p["w_out"], "nn", res=x)
    saved = dict(x=x, h=h, proj=proj, q_gain=q_gain, k_gain=k_gain, qh=qh, kh=kh, sink=sink, y_a=y_a, qk_c=qk_c,
                 hf=hf2, hb=hb2, states=states, y_m=y_m, za=za, zm=zm, merged=merged)
    return out, saved


def _mix_bwd(sv, cos, sin, B, S, p, dx, on_dw):
    T = B * S
    DH = MLSTM_HEAD_DIM
    proj = sv["proj"]
    proj3 = proj.reshape(B, S, IN_PAD)
    g = {}
    dmerged = _mmw("mix_dmerged", dx, p["w_out"], "nt")
    g["w_out"] = _matmul("mix_dwout", sv["merged"], dx, "tn", out_dtype=BF16)
    W = 512

    def merge_bwd(ga, gm, za, zm, dm):
        _, vjp = jax.vjp(_merge, ga, gm, za, zm)
        return vjp(dm)

    dga, dgm, dza, dzm = _rowwise(
        "merge_bwd", merge_bwd,
        [_In(proj, W, C_GMERGE // W, split=True), _In(proj, W, (C_GMERGE + D_MODEL) // W, split=True),
         _In(sv["za"], W, split=True), _In(sv["zm"], W, split=True), _In(dmerged, W, split=True)],
        [_Out(D_MODEL, BF16, W, split=True), _Out(D_MODEL, BF16, W, split=True),
         _Out(D_MODEL, BF16, W, split=True), _Out(D_MODEL, BF16, W, split=True)], T, 512, ncol=D_MODEL // W)
    dya = _mmw("branch_a_dx", dza, p["w_branch_attn"], "nt")
    g["w_branch_attn"] = _matmul("branch_a_dw", sv["y_a"], dza, "tn", out_dtype=BF16)
    dym = _mmw("branch_m_dx", dzm, p["w_branch_mlstm"], "nt")
    g["w_branch_mlstm"] = _matmul("branch_m_dw", sv["y_m"], dzm, "tn", out_dtype=BF16)

    def combine_bwd(hf, hb, o_pre, gn, dy):
        _, vjp = jax.vjp(_mlstm_combine, hf, hb, o_pre, gn)
        dhf, _, do, dg = vjp(dy)
        return dhf, do, dg

    dh, dom, g["mlstm_out_norm"] = _rowwise(
        "mlstm_out_bwd", combine_bwd,
        [_In(sv["hf"], DH, split=True), _In(sv["hb"], DH, split=True), _In(proj, DH, C_OM // DH, split=True),
         _In(p["mlstm_out_norm"], DH, split=True, rows=False), _In(dym, DH, split=True)],
        [_Out(MLSTM_WIDTH, F32, DH, split=True), _Out(MLSTM_WIDTH, BF16, DH, split=True),
         _Out(MLSTM_WIDTH, F32, DH, split=True, rows=False)], T, 1024, ncol=MLSTM_HEADS)
    dqk_f, dqk_b, dv_f, dv_b, dg_f, dg_b = _mlstm_bwd(sv["qk_c"], proj3, p["gate_bias"], sv["states"],
                                                       dh.reshape(B, S, MLSTM_WIDTH))
    dgates, dvm, g["gate_bias"] = _rowwise(
        "mlstm_dsum", lambda a, b, c, d: (a + b, c + d, jnp.sum(a + b, axis=0, keepdims=True)),
        [_In(dg_f.reshape(T, LANES)), _In(dg_b.reshape(T, LANES)), _In(dv_f.reshape(T, MLSTM_WIDTH)), _In(dv_b.reshape(T, MLSTM_WIDTH))],
        [_Out(LANES, BF16), _Out(MLSTM_WIDTH, BF16), _Out(LANES, rows=False)], T, 1024)
    dqk, g["conv_w8"] = _conv_bwd(proj3, p["conv_w8"], dqk_f, dqk_b)

    dqh, dkh, dvh, dsink = _attn_bwd(sv["qh"], sv["kh"], proj3, sv["sink"], dya.reshape(B, S, ATT_WIDTH))
    g["attn_sink"] = dsink.reshape(1, ATT_HEADS)
    dva = dvh.reshape(T, ATT_KV_WIDTH)
    dqa, dq_gain = _prep_bwd("q_prep_bwd", proj, ATT_WIDTH, C_QA // ATT_WIDTH, sv["q_gain"], cos, sin,
                             dqh.reshape(T, ATT_WIDTH))
    dka, dk_gain = _prep_bwd("k_prep_bwd", proj, ATT_KV_WIDTH, C_KA // ATT_KV_WIDTH, sv["k_gain"], cos, sin,
                             dkh.reshape(T, ATT_KV_WIDTH))
    g["attn_q_norm"] = jnp.sum(dq_gain.reshape(ATT_HEADS, ATT_HEAD_DIM), axis=0, keepdims=True)
    g["attn_k_norm"] = jnp.sum(dk_gain.reshape(ATT_KV_HEADS, ATT_HEAD_DIM), axis=0, keepdims=True)

    dproj = jnp.concatenate(
        [dga, dgm, dqk.reshape(T, 2 * MLSTM_WIDTH), dvm, dom, dqa, dka, dva.astype(BF16), dgates], axis=1)
    dwin = _matmul("mix_dwin", sv["h"], dproj, "tn", out_dtype=BF16)
    token = on_dw({"w_in": _w_in_to_slots(dwin), "w_branch_attn": g.pop("w_branch_attn"),
                   "w_branch_mlstm": g.pop("w_branch_mlstm"), "w_out": g.pop("w_out")}, dwin)
    dh2 = _matmul("mix_dh", dproj, p["w_in"], "nt", dep=token)
    dx_new, g["mix_norm"] = _rms_bwd("mix_dnorm", sv["x"], p["mix_norm"], dh2, dx)
    return dx_new, g


def _loss_and_grad(x, g, target):
    T = x.shape[0]

    def loss_fn(xv, gv, tv):
        err = jnp.square(_rms(xv, gv) - tv)
        return 0.5 * jnp.sum(jnp.mean(err, axis=-1, keepdims=True), axis=0, keepdims=True)

    def fn(xv, gv, tv):
        val, vjp = jax.vjp(lambda a, b: loss_fn(a, b, tv), xv, gv)
        dx, dg = vjp(jnp.ones((1, 1), F32))
        return val, dx, dg

    return _rowwise("loss_head", fn, [_In(x), _In(g, rows=False), _In(target)],
                    [_Out(1, rows=False), _Out(D_MODEL), _Out(D_MODEL, rows=False)], T, ROW_BLOCK)


def _block_norm_fwd(x, g):
    T = x.shape[0]
    return _rowwise("block_norm", _rms, [_In(x), _In(g, rows=False)], [_Out(D_MODEL)], T, ROW_BLOCK)[0]


def _block_norm_bwd(x, g, dy):
    T = x.shape[0]

    def fn(xv, gv, dv):
        _, vjp = jax.vjp(_rms, xv, gv)
        return vjp(dv)

    return _rowwise("block_norm_bwd", fn, [_In(x), _In(g, rows=False), _In(dy)],
                    [_Out(D_MODEL), _Out(D_MODEL, rows=False)], T, ROW_BLOCK)


def _qk_perm_cols(t, axis):
    q, k = jnp.split(t, 2, axis=axis)
    parts = []
    for h in range(MLSTM_HEADS):
        sl = [slice(None)] * t.ndim
        sl[axis] = slice(h * MLSTM_HEAD_DIM, (h + 1) * MLSTM_HEAD_DIM)
        parts += [q[tuple(sl)], k[tuple(sl)]]
    return jnp.concatenate(parts, axis=axis)


def _qk_unperm_cols(t, axis):
    qs, ks = [], []
    for h in range(MLSTM_HEADS):
        sl = [slice(None)] * t.ndim
        sl[axis] = slice(2 * h * MLSTM_HEAD_DIM, (2 * h + 1) * MLSTM_HEAD_DIM)
        qs.append(t[tuple(sl)])
        sl[axis] = slice((2 * h + 1) * MLSTM_HEAD_DIM, (2 * h + 2) * MLSTM_HEAD_DIM)
        ks.append(t[tuple(sl)])
    return jnp.concatenate(qs + ks, axis=axis)


def _w_in_arrange(w):
    qa, ka, va, qm, km, vm, om, gm, gmerge = jnp.split(w, np.cumsum(
        (ATT_WIDTH, ATT_KV_WIDTH, ATT_KV_WIDTH, MLSTM_WIDTH, MLSTM_WIDTH, MLSTM_WIDTH, MLSTM_WIDTH, MLSTM_N_GATES))[:].tolist(), axis=1)
    qk = _qk_perm_cols(jnp.concatenate([qm, km], axis=1), 1)
    pad = jnp.zeros((w.shape[0], LANES - MLSTM_N_GATES), w.dtype)
    return jnp.concatenate([gmerge, qk, vm, om, qa, ka, va, gm, pad], axis=1)


def _w_in_restore(w):
    gmerge = w[:, C_GMERGE:C_GMERGE + 2 * D_MODEL]
    qk = _qk_unperm_cols(w[:, C_QK:C_QK + 2 * MLSTM_WIDTH], 1)
    vm, om = w[:, C_VM:C_VM + MLSTM_WIDTH], w[:, C_OM:C_OM + MLSTM_WIDTH]
    qa, ka, va = w[:, C_QA:C_QA + ATT_WIDTH], w[:, C_KA:C_KA + ATT_KV_WIDTH], w[:, C_VA:C_VA + ATT_KV_WIDTH]
    gm = w[:, C_GATES:C_GATES + MLSTM_N_GATES]
    return jnp.concatenate([qa, ka, va, qk, vm, om, gm, gmerge], axis=1)


BIG = ("ffn1_w_gate", "ffn1_w_up", "ffn1_w_down", "w_in", "mlstm_conv_w", "w_branch_attn", "w_branch_mlstm", "w_out",
       "ffn2_w_gate", "ffn2_w_up", "ffn2_w_down")
MATMUL_W = tuple(n for n in BIG if n != "mlstm_conv_w")
SMALL = ("ffn1_norm", "mix_norm", "mlstm_gate_bias", "attn_q_norm", "attn_k_norm", "attn_sink", "mlstm_conv_b",
         "mlstm_out_norm", "ffn2_norm", "block_out_norm")
WEIGHTS = ("ffn1_norm", "ffn1_w_gate", "ffn1_w_up", "ffn1_w_down", "mix_norm", "w_in", "mlstm_gate_bias", "attn_q_norm",
           "attn_k_norm", "attn_sink", "mlstm_conv_w", "mlstm_conv_b", "mlstm_out_norm", "w_branch_attn", "w_branch_mlstm",
           "w_out", "ffn2_norm", "ffn2_w_gate", "ffn2_w_up", "ffn2_w_down", "block_out_norm")
PACK_COLS = 1024


def _padded_rows(n_elems):
    return -(-n_elems // PACK_COLS)


def _pack_flat(arrs, dtype, row_multiple):
    parts = []
    for a in arrs:
        flat = a.reshape(-1).astype(dtype)
        pad = _padded_rows(flat.shape[0]) * PACK_COLS - flat.shape[0]
        parts.append(jnp.pad(flat, (0, pad)) if pad else flat)
    flat = jnp.concatenate(parts)
    rows = flat.shape[0] // PACK_COLS
    extra = (-rows) % row_multiple
    if extra:
        flat = jnp.pad(flat, (0, extra * PACK_COLS))
    return flat.reshape(-1, PACK_COLS)


def _pack_rows(name, pieces, total_rows):
    def body(*refs):
        o_ref = refs[-1]
        o_ref[...] = jnp.zeros_like(o_ref)
        at = 0
        for ref, (arr, r0, nr) in zip(refs[:-1], pieces):
            o_ref[at:at + nr, 0:arr.shape[1]] = ref[r0:r0 + nr, :].astype(F32)
            at += nr

    return pl.pallas_call(body, name=name, out_shape=jax.ShapeDtypeStruct((total_rows, PACK_COLS), F32))(
        *[p[0] for p in pieces])


def _unpack_flat(buf, shapes, lead=()):
    flat = buf.reshape(lead + (-1,))
    out, off = [], 0
    for s in shapes:
        n = int(np.prod(s))
        out.append(flat[..., off:off + n].reshape(lead + tuple(s)))
        off += _padded_rows(n) * PACK_COLS
    return out


class _Lay:
    def __init__(self, shard, axis, width):
        self.shard, self.axis, self.width = shard, axis, width
        self.padded = tuple(width if a == axis else s for a, s in enumerate(shard))
        self.whole = tuple(N_DEV * width if a == axis else s for a, s in enumerate(shard))

    def pad(self, t, lead=0):
        extra = self.width - self.shard[self.axis]
        if not extra:
            return t
        cfg = [(0, 0)] * t.ndim
        cfg[lead + self.axis] = (0, extra)
        return jnp.pad(t, cfg)

    def unpad(self, t, lead=0):
        idx = [slice(None)] * t.ndim
        idx[lead + self.axis] = slice(0, self.shard[self.axis])
        return t[tuple(idx)]


_FF_ROW = _Lay((FF_SHARD, D_MODEL), 0, FF_SHARD_PAD)
TRANSPOSED = ("ffn1_w_gate", "ffn1_w_up", "ffn2_w_gate", "ffn2_w_up")
LAYOUTS = {
    "ffn1_w_gate": _FF_ROW, "ffn1_w_up": _FF_ROW, "ffn1_w_down": _FF_ROW,
    "ffn2_w_gate": _FF_ROW, "ffn2_w_up": _FF_ROW, "ffn2_w_down": _FF_ROW,
    "w_in": _Lay((D_MODEL, IN_WIDTH // N_DEV), 0, D_MODEL),
    "mlstm_conv_w": _Lay((3, 2 * MLSTM_WIDTH // N_DEV), 1, 2 * MLSTM_WIDTH // N_DEV),
    "w_branch_attn": _Lay((ATT_WIDTH, D_MODEL // N_DEV), 1, D_MODEL // N_DEV),
    "w_branch_mlstm": _Lay((MLSTM_WIDTH, D_MODEL // N_DEV), 1, D_MODEL // N_DEV),
    "w_out": _Lay((D_MODEL // N_DEV, D_MODEL), 0, D_MODEL // N_DEV),
}


def _window(ref, axis, j, width):
    idx = [slice(None)] * len(ref.shape)
    idx[axis] = pl.ds(pl.multiple_of(j * width, width), width)
    return ref.at[tuple(idx)]


ANY = pl.BlockSpec(memory_space=pl.ANY)


def _mesh_pos():
    return lax.axis_index("x"), lax.axis_index("y"), lax.axis_index("c")


def _all_gather(name, shard, vmem=False):
    R, C = shard.shape
    space = pl.BlockSpec(memory_space=pltpu.VMEM) if vmem else ANY

    def body(x_ref, out_ref, send_sems, recv_sems, local_sem):
        x, y, c = _mesh_pos()
        me, sibling = (x, y, c), (x, y, 1 - c)
        chips = [(1 - x, y), (x, 1 - y), (1 - x, 1 - y)]

        def slot(px, py, pc):
            return out_ref.at[4 * px + 2 * py + pc]

        def copy(k, block, to, src=None):
            return pltpu.make_async_remote_copy(
                src_ref=slot(*block) if src is None else src, dst_ref=slot(*block),
                send_sem=send_sems.at[k], recv_sem=recv_sems.at[k], device_id=to, device_id_type=MESH)

        mine = pltpu.make_async_copy(x_ref, slot(*me), local_sem)
        mine.start()
        first = [copy(0, me, sibling, src=x_ref)]
        first += [copy(1 + j, me, (*chip, c), src=x_ref) for j, chip in enumerate(chips)]
        for cp in first:
            cp.start()
        passed = [copy(4 + j, (*chip, c), sibling) for j, chip in enumerate(chips)]
        for j, chip in enumerate(chips):
            copy(1 + j, (*chip, c), me).wait_recv()
            passed[j].start()
        copy(0, sibling, me).wait_recv()
        for j, chip in enumerate(chips):
            copy(4 + j, (*chip, 1 - c), me).wait_recv()
        for cp in first + passed:
            cp.wait_send()
        mine.wait()

    return pl.pallas_call(
        body, name=name, out_shape=jax.ShapeDtypeStruct((N_DEV, R, C), shard.dtype),
        in_specs=[space], out_specs=space,
        scratch_shapes=[pltpu.SemaphoreType.DMA((7,)), pltpu.SemaphoreType.DMA((7,)), pltpu.SemaphoreType.DMA],
    )(shard)


HBM = pl.BlockSpec(memory_space=pltpu.HBM)
SEM = pl.BlockSpec(memory_space=pltpu.SEMAPHORE)
SPLIT_COPY = pltpu.CompilerParams(has_side_effects=pltpu.SideEffectType.DATAFLOW_SIDE_EFFECTING)
N_PEERS = N_DEV - 1


def _peers(x, y, c):
    return [(x, y, 1 - c), (1 - x, y, c), (x, 1 - y, c), (1 - x, 1 - y, c),
            (1 - x, y, 1 - c), (x, 1 - y, 1 - c), (1 - x, 1 - y, 1 - c)]


def _dev_index(pos):
    return 4 * pos[0] + 2 * pos[1] + pos[2]


def _place_own(name, stacks, layer, lays):
    nt = len(stacks)
    me = _dev_index(_mesh_pos())

    def body(me_ref, *refs):
        for x_ref, s_ref, o_ref, lay in zip(refs[:nt], refs[nt:2 * nt], refs[2 * nt:], lays):
            rows = lay.shard[0]
            if lay.padded != lay.shard:
                s_ref[...] = jnp.zeros_like(s_ref)
            s_ref[0:rows, :] = x_ref[...].astype(BF16)
            o_ref[...] = s_ref[...]

    def window_spec(lay):
        if lay.axis == 0:
            return pl.BlockSpec(lay.padded, lambda i, me_ref: (me_ref[0], 0))
        return pl.BlockSpec(lay.padded, lambda i, me_ref: (0, me_ref[0]))

    for lay in lays:
        assert lay.padded[1] == lay.shard[1], "only rows are padded"
    res = pl.pallas_call(
        body, name=name,
        grid_spec=pltpu.PrefetchScalarGridSpec(
            num_scalar_prefetch=1, grid=(1,),
            in_specs=[pl.BlockSpec((None,) + lay.shard, lambda i, me_ref: (layer, 0, 0)) for lay in lays],
            out_specs=[pl.BlockSpec(lay.padded, lambda i, me_ref: (0, 0)) for lay in lays] + [window_spec(lay) for lay in lays]),
        out_shape=[jax.ShapeDtypeStruct(lay.padded, BF16) for lay in lays] + [jax.ShapeDtypeStruct(lay.whole, BF16) for lay in lays],
        compiler_params=_cparams(("arbitrary",)),
    )(me.reshape(1).astype(jnp.int32), *stacks)
    return list(res[:nt]), list(res[nt:])


NEAR_PEERS = 4


def _gather_start(name, shards, lands, lays, groups, n_peers, after):
    nt, ng = len(shards), len(groups)

    def body(*refs):
        x_refs, land_refs = refs[:nt], refs[nt:2 * nt]
        sems = refs[2 * nt + 1:2 * nt + 1 + 2 * ng]
        pos = _mesh_pos()
        me = _dev_index(pos)
        for g, tens in enumerate(groups):
            for i, t in enumerate(tens):
                for k, peer in enumerate(_peers(*pos)[:n_peers[g]]):
                    pltpu.make_async_remote_copy(
                        src_ref=x_refs[t], dst_ref=_window(land_refs[t], lays[t].axis, me, lays[t].width),
                        send_sem=sems[2 * g].at[n_peers[g] * i + k], recv_sem=sems[2 * g + 1].at[n_peers[g] * i + k],
                        device_id=peer, device_id_type=MESH).start()

    sem_shapes = []
    for g, tens in enumerate(groups):
        sem_shapes += [pltpu.SemaphoreType.DMA((n_peers[g] * len(tens),))] * 2
    thru = [pltpu.HBM(s.shape, s.dtype) for s in shards] + [pltpu.HBM(lay.whole, s.dtype) for s, lay in zip(shards, lays)]
    args = [pltpu.with_memory_space_constraint(s, pltpu.HBM) for s in shards]
    args += [pltpu.with_memory_space_constraint(ld, pltpu.HBM) for ld in lands]
    res = pl.pallas_call(
        body, name=name, out_shape=tuple(sem_shapes + thru), in_specs=[HBM] * (2 * nt) + [ANY],
        out_specs=tuple([SEM] * (2 * ng) + [HBM] * (2 * nt)),
        input_output_aliases={t: 2 * ng + t for t in range(2 * nt)}, compiler_params=SPLIT_COPY,
    )(*args, after)
    sems = [(res[2 * g], res[2 * g + 1]) for g in range(ng)]
    return sems, list(res[2 * ng:2 * ng + nt]), list(res[2 * ng + nt:])


def _gather_wait(name, sems, shards, lands, lays, n_peers, after):
    nt = len(shards)
    send_sems, recv_sems = sems

    def body(*refs):
        x_refs, land_refs = refs[:nt], refs[nt:2 * nt]
        send_ref, recv_ref = refs[2 * nt], refs[2 * nt + 1]
        pos = _mesh_pos()
        for t in range(nt):
            for k, peer in enumerate(_peers(*pos)[:n_peers]):
                cp = pltpu.make_async_remote_copy(
                    src_ref=x_refs[t], dst_ref=_window(land_refs[t], lays[t].axis, _dev_index(peer), lays[t].width),
                    send_sem=send_ref.at[n_peers * t + k], recv_sem=recv_ref.at[n_peers * t + k],
                    device_id=peer, device_id_type=MESH)
                cp.wait_send()
                cp.wait_recv()

    thru = [pltpu.HBM(s.shape, s.dtype) for s in shards] + [pltpu.HBM(ld.shape, ld.dtype) for ld in lands]
    res = pl.pallas_call(
        body, name=name, out_shape=tuple(thru), in_specs=[HBM] * (2 * nt) + [SEM, SEM, ANY],
        out_specs=tuple([HBM] * (2 * nt)), input_output_aliases={t: t for t in range(2 * nt)},
        compiler_params=SPLIT_COPY,
    )(*shards, *lands, send_sems, recv_sems, after)
    return list(res[nt:])


def _forward_to_sibling(name, lands, lays):
    nt = len(lands)

    def body(*refs):
        land_refs = refs[nt:2 * nt]
        send_sems, recv_sems = refs[2 * nt:]
        x, y, c = _mesh_pos()
        chips = [(1 - x, y), (x, 1 - y), (1 - x, 1 - y)]

        def copy(t, j, core):
            win = _window(land_refs[t], lays[t].axis, _dev_index((*chips[j], core)), lays[t].width)
            return pltpu.make_async_remote_copy(
                src_ref=win, dst_ref=win, send_sem=send_sems.at[3 * t + j], recv_sem=recv_sems.at[3 * t + j],
                device_id=(x, y, 1 - c), device_id_type=MESH)

        sends = [copy(t, j, c) for t in range(nt) for j in range(3)]
        for cp in sends:
            cp.start()
        for t in range(nt):
            for j in range(3):
                copy(t, j, 1 - c).wait_recv()
        for cp in sends:
            cp.wait_send()

    return pl.pallas_call(
        body, name=name, out_shape=[jax.ShapeDtypeStruct(ld.shape, ld.dtype) for ld in lands],
        in_specs=[ANY] * nt, out_specs=[ANY] * nt, input_output_aliases={t: t for t in range(nt)},
        scratch_shapes=[pltpu.SemaphoreType.DMA((3 * nt,)), pltpu.SemaphoreType.DMA((3 * nt,))],
    )(*lands)


def _pair_exchange(name, grads, lays):
    nt = len(grads)

    def body(*refs):
        g_refs, land_refs = refs[:nt], refs[nt:2 * nt]
        send_sems, recv_sems = refs[2 * nt:]
        x, y, c = _mesh_pos()
        copies = []
        for t in range(nt):
            for chip in range(4):
                copies.append(pltpu.make_async_remote_copy(
                    src_ref=_window(g_refs[t], lays[t].axis, 2 * chip + (1 - c), lays[t].width), dst_ref=land_refs[t].at[chip],
                    send_sem=send_sems.at[4 * t + chip], recv_sem=recv_sems.at[4 * t + chip],
                    device_id=(x, y, 1 - c), device_id_type=MESH))
        for cp in copies:
            cp.start()
        for cp in copies:
            cp.wait_recv()
        for cp in copies:
            cp.wait_send()

    out_shape = [jax.ShapeDtypeStruct((4,) + lay.padded, g.dtype) for g, lay in zip(grads, lays)]
    return pl.pallas_call(
        body, name=name, out_shape=out_shape, in_specs=[ANY] * nt, out_specs=[ANY] * nt,
        scratch_shapes=[pltpu.SemaphoreType.DMA((4 * nt,)), pltpu.SemaphoreType.DMA((4 * nt,))],
    )(*grads)


def _pair_sum(name, whole, landed, lay, out_dtype):
    R, C = lay.padded
    br = _first_divisor(R, (512, 384, 256, 128, 64, 32, 16, 8))
    nb = R // br
    if lay.axis == 0:
        mine_spec = pl.BlockSpec((br, C), lambda k, i, c_ref: ((2 * k + c_ref[0]) * nb + i, 0))
    else:
        mine_spec = pl.BlockSpec((br, C), lambda k, i, c_ref: (i, 2 * k + c_ref[0]))

    def body(c_ref, mine_ref, sib_ref, o_ref):
        o_ref[0] = (mine_ref[...].astype(F32) + sib_ref[0].astype(F32)).astype(out_dtype)

    c = lax.axis_index("c")
    return pl.pallas_call(
        body, name=name,
        grid_spec=pltpu.PrefetchScalarGridSpec(
            num_scalar_prefetch=1, grid=(4, nb),
            in_specs=[mine_spec, pl.BlockSpec((1, br, C), lambda k, i, c_ref: (k, i, 0))],
            out_specs=pl.BlockSpec((1, br, C), lambda k, i, c_ref: (k, i, 0))),
        out_shape=jax.ShapeDtypeStruct((4, R, C), out_dtype),
        compiler_params=_cparams(("parallel", "parallel")),
    )(c.reshape(1).astype(jnp.int32), whole, landed)


def _chip_exchange(name, sums):
    nt = len(sums)

    def body(*refs):
        s_refs, land_refs = refs[:nt], refs[nt:2 * nt]
        send_sems, recv_sems, local_sems = refs[2 * nt:]
        x, y, c = _mesh_pos()
        my_chip = 2 * x + y
        mine = [pltpu.make_async_copy(s_refs[t].at[my_chip], land_refs[t].at[my_chip], local_sems.at[t]) for t in range(nt)]
        for cp in mine:
            cp.start()
        chips = [(1 - x, y), (x, 1 - y), (1 - x, 1 - y)]
        copies = []
        for t in range(nt):
            for j, (px, py) in enumerate(chips):
                copies.append(pltpu.make_async_remote_copy(
                    src_ref=s_refs[t].at[2 * px + py], dst_ref=land_refs[t].at[my_chip],
                    send_sem=send_sems.at[3 * t + j], recv_sem=recv_sems.at[3 * t + j],
                    device_id=(px, py, c), device_id_type=MESH))
        for cp in copies:
            cp.start()
        for t in range(nt):
            for j, (px, py) in enumerate(chips):
                pltpu.make_async_remote_copy(
                    src_ref=s_refs[t].at[my_chip], dst_ref=land_refs[t].at[2 * px + py],
                    send_sem=send_sems.at[3 * t + j], recv_sem=recv_sems.at[3 * t + j],
                    device_id=(px, py, c), device_id_type=MESH).wait_recv()
        for cp in copies:
            cp.wait_send()
        for cp in mine:
            cp.wait()

    return pl.pallas_call(
        body, name=name, out_shape=[jax.ShapeDtypeStruct(s.shape, s.dtype) for s in sums],
        in_specs=[ANY] * nt, out_specs=[ANY] * nt,
        scratch_shapes=[pltpu.SemaphoreType.DMA((3 * nt,)), pltpu.SemaphoreType.DMA((3 * nt,)), pltpu.SemaphoreType.DMA((nt,))],
    )(*sums)


def _chip_start(name, sums):
    nt = len(sums)

    def body(*refs):
        s_refs, land_refs = refs[:nt], refs[nt:2 * nt]
        send_sems, recv_sems = refs[2 * nt], refs[2 * nt + 1]
        x, y, c = _mesh_pos()
        my_chip = 2 * x + y
        for t in range(nt):
            for j, (px, py) in enumerate([(1 - x, y), (x, 1 - y), (1 - x, 1 - y)]):
                pltpu.make_async_remote_copy(
                    src_ref=s_refs[t].at[2 * px + py], dst_ref=land_refs[t].at[my_chip],
                    send_sem=send_sems.at[3 * t + j], recv_sem=recv_sems.at[3 * t + j],
                    device_id=(px, py, c), device_id_type=MESH).start()

    thru = [pltpu.HBM(s.shape, s.dtype) for s in sums] * 2
    args = [pltpu.with_memory_space_constraint(s, pltpu.HBM) for s in sums]
    args += [pltpu.with_memory_space_constraint(lax.empty(s.shape, s.dtype), pltpu.HBM) for s in sums]
    res = pl.pallas_call(
        body, name=name, out_shape=tuple([pltpu.SemaphoreType.DMA((3 * nt,))] * 2 + thru), in_specs=[HBM] * (2 * nt),
        out_specs=tuple([SEM, SEM] + [HBM] * (2 * nt)), input_output_aliases={t: 2 + t for t in range(2 * nt)},
        compiler_params=SPLIT_COPY,
    )(*args)
    return (res[0], res[1]), list(res[2:2 + nt]), list(res[2 + nt:])


def _chip_wait(name, sems, sums, lands, after):
    nt = len(sums)

    def body(*refs):
        s_refs, land_refs = refs[:nt], refs[nt:2 * nt]
        send_sems, recv_sems = refs[2 * nt], refs[2 * nt + 1]
        x, y, c = _mesh_pos()
        my_chip = 2 * x + y
        for t in range(nt):
            for j, (px, py) in enumerate([(1 - x, y), (x, 1 - y), (1 - x, 1 - y)]):
                cp = pltpu.make_async_remote_copy(
                    src_ref=s_refs[t].at[my_chip], dst_ref=land_refs[t].at[2 * px + py],
                    send_sem=send_sems.at[3 * t + j], recv_sem=recv_sems.at[3 * t + j],
                    device_id=(px, py, c), device_id_type=MESH)
                cp.wait_send()
                cp.wait_recv()

    thru = [pltpu.HBM(s.shape, s.dtype) for s in sums] * 2
    res = pl.pallas_call(
        body, name=name, out_shape=tuple(thru), in_specs=[HBM] * (2 * nt) + [SEM, SEM, ANY],
        out_specs=tuple([HBM] * (2 * nt)), input_output_aliases={t: t for t in range(2 * nt)},
        compiler_params=SPLIT_COPY,
    )(*sums, *lands, sems[0], sems[1], after)
    return list(res[:nt]), list(res[nt:])


def _sum_chips(name, own, landed):
    _, R, C = own.shape
    br = _first_divisor(R, (512, 384, 256, 128, 64, 32, 16, 8))
    x, y, _ = _mesh_pos()
    slots = jnp.stack([2 * x + y, 2 * (1 - x) + y, 2 * x + (1 - y), 2 * (1 - x) + (1 - y)]).astype(jnp.int32)

    def body(slot_ref, mine_ref, a_ref, b_ref, c_ref, o_ref):
        o_ref[...] = ((mine_ref[0].astype(F32) + a_ref[0].astype(F32)) + b_ref[0].astype(F32)) + c_ref[0].astype(F32)

    def slot_spec(j):
        return pl.BlockSpec((1, br, C), lambda i, slot_ref: (slot_ref[j], i, 0))

    return pl.pallas_call(
        body, name=name,
        grid_spec=pltpu.PrefetchScalarGridSpec(
            num_scalar_prefetch=1, grid=(R // br,), in_specs=[slot_spec(0), slot_spec(1), slot_spec(2), slot_spec(3)],
            out_specs=pl.BlockSpec((br, C), lambda i, slot_ref: (i, 0))),
        out_shape=jax.ShapeDtypeStruct((R, C), F32), compiler_params=_cparams(("parallel",)),
    )(slots, own, landed, landed, landed)


def _sum_slots(name, slots, n):
    _, R, C = slots.shape
    br = _first_divisor(R, (512, 384, 256, 128, 64, 32, 16, 8))

    def body(s_ref, o_ref):
        acc = s_ref[0].astype(F32)
        for k in range(1, n):
            acc = acc + s_ref[k].astype(F32)
        o_ref[...] = acc

    return pl.pallas_call(
        body, name=name, grid=(R // br,), in_specs=[pl.BlockSpec((n, br, C), lambda i: (0, i, 0))],
        out_specs=pl.BlockSpec((br, C), lambda i: (i, 0)), out_shape=jax.ShapeDtypeStruct((R, C), F32),
        compiler_params=_cparams(("parallel",)),
    )(slots)


def _reduce_scatter_start(tag, names, grads):
    lays = [LAYOUTS[n] for n in names]
    landed = _pair_exchange("grads_pair_" + names[0], grads, lays)
    sums = [_pair_sum("grads_pairsum_" + n, g, ld, lay, BF16) for n, g, ld, lay in zip(names, grads, landed, lays)]
    sems, sums, lands = _chip_start(tag + "_chips_start", sums)
    return tag, names, sems, sums, lands


def _reduce_scatter_finish(pending, after):
    tag, names, sems, sums, lands = pending
    own, got = _chip_wait(tag + "_chips_wait", sems, sums, lands, after)
    return [_sum_chips("grads_sum_" + n, o, s) for n, o, s in zip(names, own, got)]


def _adamw_math(w, g, m, v):
    m = ADAM_B1 * m + (1.0 - ADAM_B1) * g
    v = ADAM_B2 * v + (1.0 - ADAM_B2) * jnp.square(g)
    m_hat = m / (1.0 - ADAM_B1 ** ADAM_STEP)
    v_hat = v / (1.0 - ADAM_B2 ** ADAM_STEP)
    delta = -ADAM_LR * (m_hat / (jnp.sqrt(v_hat) + ADAM_EPS) + ADAM_WD * w)
    return delta, m, v


def _adamw_layers(name, w, totals, m, v):
    _, R, C = w.shape
    br = _first_divisor(R, (512, 176, 128, 64, 32, 16, 8))
    Cp = totals[0].shape[1]

    def body(w_ref, g0_ref, g1_ref, m_ref, v_ref, g_out, d_out, m_out, v_out):
        g = jnp.where(pl.program_id(0) == 0, g0_ref[:, 0:C], g1_ref[:, 0:C])
        delta, m_new, v_new = _adamw_math(w_ref[0], g, m_ref[0], v_ref[0])
        g_out[0], d_out[0], m_out[0], v_out[0] = g, delta, m_new, v_new

    blk = pl.BlockSpec((1, br, C), lambda l, i: (l, i, 0))
    g_spec = pl.BlockSpec((br, Cp), lambda l, i: (i, 0))
    return pl.pallas_call(
        body, name=name, grid=(DEPTH, R // br), in_specs=[blk, g_spec, g_spec, blk, blk], out_specs=[blk] * 4,
        out_shape=[jax.ShapeDtypeStruct(w.shape, F32)] * 4, compiler_params=_cparams(("parallel", "parallel")),
    )(w, totals[0], totals[1], m, v)


def _adamw(name, w, g, m, v):
    shape = w.shape
    cols = shape[-1]
    rows = int(np.prod(shape[:-1]))
    br = _first_divisor(rows, (512, 352, 256, 128, 64, 32, 16, 8))
    args = [_In(a.reshape(rows, cols)) for a in (w, g, m, v)]
    outs = _rowwise(name, _adamw_math, args, [_Out(cols), _Out(cols), _Out(cols)], rows, br)
    return [o.reshape(shape) for o in outs]


GROUPS = {"ffn1": ("ffn1_w_gate", "ffn1_w_up", "ffn1_w_down"),
          "mix": ("w_in", "w_branch_attn", "w_branch_mlstm", "w_out"),
          "ffn2": ("ffn2_w_gate", "ffn2_w_up", "ffn2_w_down")}
GATHER_GROUPS = {"ffn1_in": ("ffn1_w_gate", "ffn1_w_up"), "ffn1_out": ("ffn1_w_down",),
                 "mix": ("w_in", "w_branch_attn", "w_branch_mlstm", "w_out"),
                 "ffn2_in": ("ffn2_w_gate", "ffn2_w_up"), "ffn2_out": ("ffn2_w_down",)}


def _small_params(small, conv_w, l):
    p = {}
    for n in ("ffn1_norm", "mix_norm", "ffn2_norm", "block_out_norm", "mlstm_out_norm", "attn_q_norm", "attn_k_norm"):
        p[n] = small[n][l][None, :]
    p["attn_sink"] = small["attn_sink"][l]
    p["gate_bias"] = jnp.pad(small["mlstm_gate_bias"][l], (0, LANES - MLSTM_N_GATES))[None, :]
    taps = _qk_perm_cols(conv_w[l], 1)
    conv_b = _qk_perm_cols(small["mlstm_conv_b"][l][None, :], 1)
    p["conv_w8"] = jnp.concatenate([taps, conv_b, jnp.zeros((4, 2 * MLSTM_WIDTH), F32)], axis=0)
    return p


def _w_in_from_slots(slots):
    w_in = slots.reshape(N_DEV, D_MODEL, IN_WIDTH // N_DEV).transpose(1, 0, 2).reshape(D_MODEL, IN_WIDTH)
    return _w_in_arrange(w_in)


def _w_in_to_slots(g):
    return _w_in_restore(g).reshape(D_MODEL, N_DEV, IN_WIDTH // N_DEV).transpose(1, 0, 2).reshape(
        N_DEV * D_MODEL, IN_WIDTH // N_DEV)


def _local_step(x, positions, target, weights_of, small, conv_w, on_grads):
    B, S, _ = x.shape
    T = B * S
    cos, sin = _rope_cos_sin(positions.reshape(T, 1))
    params = [_small_params(small, conv_w, l) for l in range(DEPTH)]
    xs = x.reshape(T, D_MODEL)
    tgt = target.reshape(T, D_MODEL)

    saved = []
    for l, p in enumerate(params):
        p.update(weights_of(l, "ffn1_in", xs))
        x1, s1, p["ffn1_w_down"] = _ffn_fwd("ffn1", xs, p["ffn1_norm"], p["ffn1_w_gate"], p["ffn1_w_up"],
                                            lambda after, l=l: weights_of(l, "ffn1_out", after)["ffn1_w_down"])
        p.update(weights_of(l, "mix", x1))
        p["w_in"] = _w_in_from_slots(p["w_in"])
        x2, s2 = _mix_fwd(x1, cos, sin, B, S, p)
        p.update(weights_of(l, "ffn2_in", x2))
        x3, s3, p["ffn2_w_down"] = _ffn_fwd("ffn2", x2, p["ffn2_norm"], p["ffn2_w_gate"], p["ffn2_w_up"],
                                            lambda after, l=l: weights_of(l, "ffn2_out", after)["ffn2_w_down"])
        saved.append((s1, s2, s3, x3))
        if l + 1 < DEPTH:
            xs = _block_norm_fwd(x3, p["block_out_norm"])

    sm = {}
    loss = None
    dx = None
    for l in reversed(range(DEPTH)):
        p = params[l]
        s1, s2, s3, x3 = saved[l]
        if l == DEPTH - 1:
            loss, dx, dgn = _loss_and_grad(x3, p["block_out_norm"], tgt)
        else:
            dx, dgn = _block_norm_bwd(x3, p["block_out_norm"], dx)
        sm["block_out_norm", l] = (dgn, 0, 1)
        dx, dg = _ffn_bwd("ffn2", s3, p["ffn2_norm"], p["ffn2_w_gate"], p["ffn2_w_up"], p["ffn2_w_down"], dx,
                          functools.partial(on_grads, l, "ffn2"))
        sm["ffn2_norm", l] = (dg, 0, 1)
        dx, g = _mix_bwd(s2, cos, sin, B, S, p, dx, functools.partial(on_grads, l, "mix"))
        dconv = _qk_unperm_cols(g["conv_w8"], 1)
        sm["mlstm_conv_w", l] = (dconv, 0, 3)
        sm["mlstm_conv_b", l] = (dconv, 3, 1)
        for n, key in (("mix_norm", "mix_norm"), ("mlstm_gate_bias", "gate_bias"), ("attn_q_norm", "attn_q_norm"),
                       ("attn_k_norm", "attn_k_norm"), ("attn_sink", "attn_sink"), ("mlstm_out_norm", "mlstm_out_norm")):
            sm[n, l] = (g[key], 0, 1)
        dx, dg = _ffn_bwd("ffn1", s1, p["ffn1_norm"], p["ffn1_w_gate"], p["ffn1_w_up"], p["ffn1_w_down"], dx,
                          functools.partial(on_grads, l, "ffn1"))
        sm["ffn1_norm", l] = (dg, 0, 1)
    return loss, dx.reshape(B, S, D_MODEL), sm


def kernel(x, positions, ffn1_norm, ffn1_w_gate, ffn1_w_up, ffn1_w_down, mix_norm, w_in, mlstm_gate_bias, attn_q_norm, attn_k_norm, attn_sink, mlstm_conv_w, mlstm_conv_b, mlstm_out_norm, w_branch_attn, w_branch_mlstm, w_out, ffn2_norm, ffn2_w_gate, ffn2_w_up, ffn2_w_down, block_out_norm, loss_target, m_ffn1_norm, m_ffn1_w_gate, m_ffn1_w_up, m_ffn1_w_down, m_mix_norm, m_w_in, m_mlstm_gate_bias, m_attn_q_norm, m_attn_k_norm, m_attn_sink, m_mlstm_conv_w, m_mlstm_conv_b, m_mlstm_out_norm, m_w_branch_attn, m_w_branch_mlstm, m_w_out, m_ffn2_norm, m_ffn2_w_gate, m_ffn2_w_up, m_ffn2_w_down, m_block_out_norm, v_ffn1_norm, v_ffn1_w_gate, v_ffn1_w_up, v_ffn1_w_down, v_mix_norm, v_w_in, v_mlstm_gate_bias, v_attn_q_norm, v_attn_k_norm, v_attn_sink, v_mlstm_conv_w, v_mlstm_conv_b, v_mlstm_out_norm, v_w_branch_attn, v_w_branch_mlstm, v_w_out, v_ffn2_norm, v_ffn2_w_gate, v_ffn2_w_up, v_ffn2_w_down, v_block_out_norm):
    args = locals()
    def stored(n, t):
        return t.transpose(0, 2, 1) if n in TRANSPOSED else t

    w = {n: stored(n, args[n]) for n in WEIGHTS}
    m = {n: stored(n, args["m_" + n]) for n in WEIGHTS}
    v = {n: stored(n, args["v_" + n]) for n in WEIGHTS}

    order = [(l, grp) for l in range(DEPTH) for grp in GATHER_GROUPS]
    keys = [(l, n) for l, grp in order for n in GATHER_GROUPS[grp]]
    lays = [LAYOUTS[n] for _, n in keys]
    group_idx, at = {}, 0
    for l, grp in order:
        group_idx[(l, grp)] = list(range(at, at + len(GATHER_GROUPS[grp])))
        at += len(GATHER_GROUPS[grp])
    conv_shape = w["mlstm_conv_w"].shape
    conv_all = _all_gather("conv_all_gather", _pack_flat([w["mlstm_conv_w"]], F32, 8), vmem=True)
    conv_parts = _unpack_flat(conv_all, [conv_shape], lead=(N_DEV,))[0]
    conv_w = jnp.concatenate([conv_parts[j] for j in range(N_DEV)], axis=2)
    small = {n: w[n] for n in SMALL}

    shards, lands = [], []
    for l, grp in order:
        own, whole = _place_own("weights_place_" + grp, [w[n] for n in GATHER_GROUPS[grp]], l,
                                [lays[i] for i in group_idx[(l, grp)]])
        shards += own
        lands += whole
    n_peers = [NEAR_PEERS if (l, grp) in ((0, "ffn1_in"), (0, "ffn1_out"), (0, "mix")) else N_PEERS for l, grp in order]
    sems, shards, lands = _gather_start("weights_gather_start", shards, lands, lays, [group_idx[k] for k in order],
                                        n_peers, conv_all)

    def weights_of(l, grp, after):
        idx, g = group_idx[(l, grp)], order.index((l, grp))
        group_lays = [lays[i] for i in idx]
        whole = _gather_wait(f"weights_gather_wait_{l}_{grp}", sems[g], [shards[i] for i in idx],
                             [lands[i] for i in idx], group_lays, n_peers[g], after)
        if n_peers[g] == NEAR_PEERS:
            whole = _forward_to_sibling("weights_forward_" + grp, whole, group_lays)
        return dict(zip(GATHER_GROUPS[grp], whole))

    totals, pending = {}, []

    def finish(after):
        tag, names = pending[0][0], pending[0][1]
        for n, t in zip(names, _reduce_scatter_finish(pending.pop(0), after)):
            totals[(tag, n)] = t

    def on_grads(l, grp, g, after):
        if pending:
            finish(after)
        names = GROUPS[grp]
        pending.append(_reduce_scatter_start(f"grads_{l}_{grp}", names, [g[n] for n in names]))
        return pending[-1][3][0]

    loss, grad_x, small_g = _local_step(x, positions, loss_target, weights_of, small, conv_w, on_grads)
    finish(grad_x)
    grads, deltas, new_m, new_v = {}, {}, {}, {}
    for grp, names in GROUPS.items():
        for n in names:
            grads[n], deltas[n], new_m[n], new_v[n] = _adamw_layers(
                "adamw_" + n, w[n], [totals[(f"grads_{l}_{grp}", n)] for l in range(DEPTH)], m[n], v[n])

    n_small = DEPTH * len(SMALL)
    taps_at, loss_at, rows = n_small, n_small + 3 * DEPTH, 32
    pieces = [small_g[n, l] for n in SMALL for l in range(DEPTH)]
    pieces += [small_g["mlstm_conv_w", l] for l in range(DEPTH)] + [(loss, 0, 1)]
    small_all = _all_gather("small_all_gather", _pack_rows("small_pack", pieces, rows), vmem=True)
    small_sum = _sum_slots("small_sum", small_all, N_DEV)
    loss_total = small_sum[loss_at, 0]
    x_pos, y_pos, c_pos = _mesh_pos()
    grads["mlstm_conv_w"] = lax.dynamic_slice_in_dim(
        small_sum[taps_at:loss_at].reshape(DEPTH, 3, 2 * MLSTM_WIDTH),
        (4 * x_pos + 2 * y_pos + c_pos) * conv_shape[2], conv_shape[2], axis=2)

    n = "mlstm_conv_w"
    deltas[n], new_m[n], new_v[n] = _adamw("adamw_" + n, w[n], grads[n], m[n], v[n])
    sw, smm, sv = (_pack_rows("small_pack_" + tag, [(d[n], 0, DEPTH) for n in SMALL], rows)
                   for tag, d in (("w", w), ("m", m), ("v", v)))
    sd, snm, snv = _adamw("adamw_small", sw, small_sum, smm, sv)
    for i, n in enumerate(SMALL):
        rows_n, width = slice(DEPTH * i, DEPTH * (i + 1)), w[n].shape[1]
        grads[n], deltas[n], new_m[n], new_v[n] = (buf[rows_n, :width] for buf in (small_sum, sd, snm, snv))

    return (loss_total.reshape(()), grad_x, *[stored(n, d[n]) for d in (grads, deltas, new_m, new_v) for n in WEIGHTS])
```

```python
import functools

import numpy as np
import jax
import jax.numpy as jnp
from jax import lax
from jax.experimental import pallas as pl
from jax.experimental.pallas import tpu as pltpu

F32 = jnp.float32
BF16 = jnp.bfloat16

D_MODEL = 1024
D_FF = 2816
ATT_HEAD_DIM = 64
ATT_HEADS = 8
ATT_KV_HEADS = 2
ATT_GROUP = ATT_HEADS // ATT_KV_HEADS
ATT_WIDTH = ATT_HEADS * ATT_HEAD_DIM
ATT_KV_WIDTH = ATT_KV_HEADS * ATT_HEAD_DIM
WINDOW = 128
ATT_BLOCK = 128
ROPE_DIM = 16
ROPE_THETA = 500000.0
MLSTM_HEADS = 4
MLSTM_HEAD_DIM = 128
MLSTM_WIDTH = MLSTM_HEADS * MLSTM_HEAD_DIM
MLSTM_CHUNK = 128
MLSTM_N_GATES = 4 * MLSTM_HEADS
NORM_EPS = 1e-6
IN_WIDTH = 4880
DEPTH = 2
N_DEV = 8

ADAM_LR = 0.001
ADAM_B1 = 0.9
ADAM_B2 = 0.999
ADAM_EPS = 1e-08
ADAM_WD = 0.01
ADAM_STEP = 10

LANES = 128
C_GMERGE = 0
C_QK = 2048
C_VM = 3072
C_OM = 3584
C_QA = 4096
C_KA = 4608
C_VA = 4736
C_GATES = 4864
IN_PAD = 4992

VMEM_LIMIT = 48 * 1024 * 1024

MESH = pl.DeviceIdType.MESH


def _cparams(sem):
    return pltpu.CompilerParams(dimension_semantics=sem, vmem_limit_bytes=VMEM_LIMIT)


def _first_divisor(n, cands):
    for c in cands:
        if n % c == 0:
            return c
    return n


_NN = ((1,), (0,))
_NT = ((1,), (1,))
_TN = ((0,), (0,))


def _mm(a, b, dims):
    return lax.dot_general(a.astype(BF16), b.astype(BF16), (dims, ((), ())), preferred_element_type=F32)


@jax.custom_vjp
def mm_nn(a, b):
    return _mm(a, b, _NN)


def _mm_nn_fwd(a, b):
    return _mm(a, b, _NN), (a, b)


def _mm_nn_bwd(res, g):
    a, b = res
    return _mm(g, b, _NT).astype(a.dtype), _mm(a, g, _TN).astype(b.dtype)


mm_nn.defvjp(_mm_nn_fwd, _mm_nn_bwd)


@jax.custom_vjp
def mm_nt(a, b):
    return _mm(a, b, _NT)


def _mm_nt_fwd(a, b):
    return _mm(a, b, _NT), (a, b)


def _mm_nt_bwd(res, g):
    a, b = res
    return _mm(g, b, _NN).astype(a.dtype), _mm(g, a, _TN).astype(b.dtype)


mm_nt.defvjp(_mm_nt_fwd, _mm_nt_bwd)


@jax.custom_vjp
def mm_tn(a, b):
    return _mm(a, b, _TN)


def _mm_tn_fwd(a, b):
    return _mm(a, b, _TN), (a, b)


def _mm_tn_bwd(res, g):
    a, b = res
    return _mm(b, g, _NT).astype(a.dtype), _mm(a, g, _NN).astype(b.dtype)


mm_tn.defvjp(_mm_tn_fwd, _mm_tn_bwd)


def _matmul(name, a, b, mode, out_dtype=F32, res=None, scale=1.0, bl=None, dep=None, whole_k=False):
    b_shape = b.shape if bl is None else b.shape[1:]
    if mode == "nn":
        (M, K), (K2, N) = a.shape, b_shape
    elif mode == "nt":
        (M, K), (N, K2) = a.shape, b_shape
    else:
        (K, M), (K2, N) = a.shape, b_shape
    assert K == K2, (name, a.shape, b.shape)
    tm = _first_divisor(M, (1024, 512, 384, 256, 128))
    tn = _first_divisor(N, (1024, 1664, 512, 384, 256, 128))
    tk = K if whole_k else _first_divisor(K, (1024, 1664, 512, 256, 128))
    if whole_k:
        tn = min(tn, 512)
    nk = K // tk
    if mode == "tn":
        a_spec = pl.BlockSpec((tk, tm), lambda i, j, k: (k, i))
    else:
        a_spec = pl.BlockSpec((tm, tk), lambda i, j, k: (i, k))
    if mode == "nt":
        b_blk, b_idx = (tn, tk), (lambda i, j, k: (j, k))
    else:
        b_blk, b_idx = (tk, tn), (lambda i, j, k: (k, j))
    if bl is None:
        b_spec = pl.BlockSpec(b_blk, b_idx)
    else:
        b_spec = pl.BlockSpec((None,) + b_blk, lambda i, j, k: (bl,) + b_idx(i, j, k))
    o_spec = pl.BlockSpec((tm, tn), lambda i, j, k: (i, j))
    dims = {"nn": _NN, "nt": _NT, "tn": _TN}[mode]
    has_res = res is not None

    def body(*refs):
        a_ref, b_ref = refs[:2]
        r_ref = refs[2] if has_res else None

        def finish(out):
            if scale != 1.0:
                out = out * scale
            if has_res:
                out = r_ref[...].astype(F32) + out
            o_ref[...] = out.astype(out_dtype)

        if nk == 1:
            o_ref = refs[-1]
            finish(_mm(a_ref[...], b_ref[...], dims))
            return
        o_ref, acc = refs[-2:]
        k = pl.program_id(2)

        @pl.when(k == 0)
        def _():
            acc[...] = jnp.zeros_like(acc)

        acc[...] += _mm(a_ref[...], b_ref[...], dims)

        @pl.when(k == nk - 1)
        def _():
            finish(acc[...])

    in_specs = [a_spec, b_spec] + ([o_spec] if has_res else [])
    args = (a, b) + ((res,) if has_res else ())
    if dep is not None:
        in_specs.append(pl.BlockSpec(memory_space=pl.ANY))
        args += (dep,)
    return pl.pallas_call(
        body, name=name, grid=(M // tm, N // tn, nk), in_specs=in_specs, out_specs=o_spec,
        out_shape=jax.ShapeDtypeStruct((M, N), out_dtype),
        scratch_shapes=[pltpu.VMEM((tm, tn), F32)] if nk > 1 else [],
        compiler_params=_cparams(("parallel", "parallel", "arbitrary")),
    )(*args)


class _In:
    def __init__(self, arr, width=None, base=0, split=False, rows=True):
        self.arr, self.base, self.split, self.rows = arr, base, split, rows
        self.width = arr.shape[1] if width is None else width


class _Out:
    def __init__(self, cols, dtype=F32, width=None, split=False, rows=True, nrows=1, into=None, base=0):
        self.cols, self.dtype, self.split, self.rows, self.nrows = cols, dtype, split, rows, nrows
        self.width = cols if width is None else width
        self.into, self.base = into, base
        if into is not None:
            self.cols, self.dtype = into.shape[1], into.dtype


def _rowwise(name, fn, ins, outs, n_rows, br, ncol=1):
    br = min(br, n_rows)
    assert n_rows % br == 0, (name, n_rows, br)
    nrow_blocks = n_rows // br

    def in_spec(d):
        nb = br if d.rows else d.arr.shape[0]
        if d.rows and d.split:
            im = lambda j, i, base=d.base: (i, base + j)
        elif d.rows:
            im = lambda j, i, base=d.base: (i, base)
        elif d.split:
            im = lambda j, i, base=d.base: (0, base + j)
        else:
            im = lambda j, i, base=d.base: (0, base)
        return pl.BlockSpec((nb, d.width), im)

    def out_spec(d):
        nb = br if d.rows else d.nrows
        if d.rows and d.split:
            im = lambda j, i, base=d.base: (i, base + j)
        elif d.rows:
            im = lambda j, i, base=d.base: (i, base)
        elif d.split:
            im = lambda j, i: (0, j)
        else:
            im = lambda j, i: (0, 0)
        return pl.BlockSpec((nb, d.width), im)

    n_in = len(ins)
    targets = [(k, d.into) for k, d in enumerate(outs) if d.into is not None]

    def body(*refs):
        i = pl.program_id(1)
        vals = [r[...] for r in refs[:n_in]]
        res = fn(*vals)
        if not isinstance(res, (tuple, list)):
            res = (res,)
        for d, ref, val in zip(outs, refs[n_in + len(targets):], res):
            if d.rows:
                ref[...] = val.astype(d.dtype)
            else:
                @pl.when(i == 0)
                def _(ref=ref):
                    ref[...] = jnp.zeros_like(ref)

                ref[...] += val.astype(d.dtype)

    out_shape = [jax.ShapeDtypeStruct((n_rows if d.rows else d.nrows, d.cols), d.dtype) for d in outs]
    res = pl.pallas_call(
        body, name=name, grid=(ncol, nrow_blocks),
        in_specs=[in_spec(d) for d in ins] + [pl.BlockSpec(memory_space=pl.ANY)] * len(targets),
        out_specs=[out_spec(d) for d in outs], out_shape=out_shape,
        input_output_aliases={n_in + t: k for t, (k, _) in enumerate(targets)},
        compiler_params=_cparams(("parallel", "arbitrary")),
    )(*[d.arr for d in ins], *[arr for _, arr in targets])
    return res


def _rms(x, g):
    return x * lax.rsqrt(jnp.mean(x * x, axis=-1, keepdims=True) + NORM_EPS) * g


def _sigmoid(x):
    return 0.5 * jnp.tanh(0.5 * x) + 0.5


def _silu(x):
    return x * _sigmoid(x)


def _log_sigmoid(x):
    return jnp.minimum(x, 0.0) - jnp.log(1.0 + jnp.exp(-jnp.abs(x)))


def _rope_tables(pos, inv_freq_row):
    ang = pos.astype(F32) * inv_freq_row
    return jnp.cos(ang), jnp.sin(ang)


def _head_sums_impl(v):
    w = v.shape[-1]
    shift = ATT_HEAD_DIM.bit_length() - 1
    r = lax.shift_right_logical(lax.broadcasted_iota(jnp.int32, (w, w), 0), shift)
    c = lax.shift_right_logical(lax.broadcasted_iota(jnp.int32, (w, w), 1), shift)
    ones = (r == c).astype(BF16)
    hi = v.astype(BF16)
    lo = (v - hi.astype(F32)).astype(BF16)
    dn = (_NN, ((), ()))
    return (lax.dot_general(hi, ones, dn, preferred_element_type=F32)
            + lax.dot_general(lo, ones, dn, preferred_element_type=F32))


@jax.custom_vjp
def _head_sums(v):
    return _head_sums_impl(v)


_head_sums.defvjp(lambda v: (_head_sums_impl(v), None), lambda _, g: (_head_sums_impl(g),))


def _rotate_half_impl(y):
    w = y.shape[-1]
    half = ROPE_DIM // 2
    lane = lax.broadcasted_iota(jnp.int32, y.shape, 1) & (ATT_HEAD_DIM - 1)
    above = pltpu.roll(y, w - half, axis=1)
    below = pltpu.roll(y, half, axis=1)
    return jnp.where(lane < half, -above, jnp.where(lane < ROPE_DIM, below, 0.0))


@jax.custom_vjp
def _rotate_half(y):
    return _rotate_half_impl(y)


_rotate_half.defvjp(lambda y: (_rotate_half_impl(y), None), lambda _, g: (-_rotate_half_impl(g),))


def _qk_prep(t, g, cos, sin):
    reps = t.shape[-1] // cos.shape[-1]
    if reps > 1:
        cos, sin = jnp.tile(cos, (1, reps)), jnp.tile(sin, (1, reps))
    y = t * lax.rsqrt(_head_sums(t * t) * (1.0 / ATT_HEAD_DIM) + NORM_EPS) * g
    return y * cos + _rotate_half(y) * sin


def _attn_head(q, kb, vb, sink, valid):
    s = mm_nt(q, kb) * (ATT_HEAD_DIM ** -0.5)
    s = jnp.where(valid, s, -jnp.inf)
    m = jnp.maximum(jnp.max(s, axis=-1, keepdims=True), sink)
    p = jnp.exp(s - m)
    den = jnp.sum(p, axis=-1, keepdims=True) + jnp.exp(sink - m)
    return mm_nn(p * (1.0 / den), vb)


def _mlstm_chunk(q, k, v, li, lf, C, n, m, incl, incl_t, eye):
    k = k * (MLSTM_HEAD_DIM ** -0.5)
    lf_row = jnp.sum(eye * lf, axis=0, keepdims=True)
    li_row = jnp.sum(eye * li, axis=0, keepdims=True)
    b = jnp.sum(incl * lf_row, axis=1, keepdims=True)
    b_row = jnp.sum(incl_t * lf, axis=0, keepdims=True)
    b_tot = jnp.sum(lf, axis=0, keepdims=True)
    a = b_tot - b + li
    a_max = jnp.max(a, axis=0, keepdims=True)
    kw = k * jnp.exp(a - a_max)
    c_loc = mm_tn(kw, v)
    n_loc = jnp.sum(kw, axis=0, keepdims=True)

    dmat = jnp.where(incl > 0.5, b - b_row + li_row, -jnp.inf)
    inter = b + m
    m_t = jnp.maximum(inter, jnp.max(dmat, axis=1, keepdims=True))
    sc = mm_nt(q, k) * jnp.exp(dmat - m_t)
    scale_in = jnp.exp(inter - m_t)
    num = mm_nn(sc, v) + scale_in * mm_nn(q, C)
    den = jnp.sum(sc, axis=1, keepdims=True) + scale_in * jnp.sum(q * n, axis=1, keepdims=True)
    h = num * (1.0 / jnp.maximum(jnp.abs(den), jnp.exp(-m_t)))

    m_new = jnp.maximum(b_tot + m, a_max)
    s_p = jnp.exp(b_tot + m - m_new)
    s_l = jnp.exp(a_max - m_new)
    return h, s_p * C + s_l * c_loc, s_p * n + s_l * n_loc, m_new


def _mlstm_combine(hf, hb, o_pre, g):
    h = hf + hb
    mu = jnp.mean(h, axis=-1, keepdims=True)
    var = jnp.mean(jnp.square(h - mu), axis=-1, keepdims=True)
    return _sigmoid(o_pre) * ((h - mu) * lax.rsqrt(var + NORM_EPS) * g)


def _merge(ga, gm, za, zm):
    return _sigmoid(ga) * za + _sigmoid(gm) * zm


def _attn_mask(n, seq):
    shape = (ATT_GROUP * ATT_BLOCK, 3 * ATT_BLOCK)
    qi = n * ATT_BLOCK + (lax.broadcasted_iota(jnp.int32, shape, 0) & (ATT_BLOCK - 1))
    kj = (n - 1) * ATT_BLOCK + lax.broadcasted_iota(jnp.int32, shape, 1)
    return (jnp.abs(qi - kj) <= WINDOW) & (kj >= 0) & (kj < seq)


def _attn_specs(nq, v_base):
    q_spec = pl.BlockSpec((1, ATT_BLOCK, ATT_WIDTH), lambda b, n: (b, n, 0))

    def kv_spec(off, base=0):
        return pl.BlockSpec((1, ATT_BLOCK, ATT_KV_WIDTH), lambda b, n: (b, jnp.clip(n + off, 0, nq - 1), base))

    sink_spec = pl.BlockSpec((ATT_KV_HEADS, ATT_GROUP, 1, 1), lambda b, n: (0, 0, 0, 0))
    specs = [q_spec, kv_spec(-1), kv_spec(0), kv_spec(1), kv_spec(-1, v_base), kv_spec(0, v_base), kv_spec(1, v_base), sink_spec]
    return q_spec, specs, sink_spec


def _head(h):
    return slice(h * ATT_HEAD_DIM, (h + 1) * ATT_HEAD_DIM)


def _group_rows(q_ref, s_ref, h):
    q4 = jnp.concatenate([q_ref[0, :, _head(h * ATT_GROUP + g)] for g in range(ATT_GROUP)], axis=0)
    sink4 = jnp.concatenate([jnp.broadcast_to(s_ref[h, g], (ATT_BLOCK, 1)) for g in range(ATT_GROUP)], axis=0)
    return q4, sink4


def _attn_fwd(q, k, proj3, sink):
    B, S, _ = q.shape
    nq = S // ATT_BLOCK
    q_spec, specs, _ = _attn_specs(nq, C_VA // ATT_KV_WIDTH)

    def body(q_ref, kp, kc, kn, vp, vc, vn, s_ref, o_ref):
        valid = _attn_mask(pl.program_id(1), S)
        for h in range(ATT_KV_HEADS):
            kb = jnp.concatenate([kp[0, :, _head(h)], kc[0, :, _head(h)], kn[0, :, _head(h)]], axis=0)
            vb = jnp.concatenate([vp[0, :, _head(h)], vc[0, :, _head(h)], vn[0, :, _head(h)]], axis=0)
            q4, sink4 = _group_rows(q_ref, s_ref, h)
            o4 = _attn_head(q4, kb, vb, sink4, valid).astype(BF16)
            for g in range(ATT_GROUP):
                o_ref[0, :, _head(h * ATT_GROUP + g)] = o4[g * ATT_BLOCK:(g + 1) * ATT_BLOCK]

    return pl.pallas_call(
        body, name="attn_fwd", grid=(B, nq), in_specs=specs,
        out_specs=q_spec, out_shape=jax.ShapeDtypeStruct(q.shape, BF16),
        compiler_params=_cparams(("parallel", "arbitrary")),
    )(q, k, k, k, proj3, proj3, proj3, sink)


def _attn_bwd(q, k, proj3, sink, dy):
    B, S, _ = q.shape
    nq = S // ATT_BLOCK
    q_spec, specs, sink_spec = _attn_specs(nq, C_VA // ATT_KV_WIDTH)
    kv_full = pl.BlockSpec((1, S, ATT_KV_WIDTH), lambda b, n: (b, 0, 0))

    def body(q_ref, kp, kc, kn, vp, vc, vn, s_ref, dy_ref, dq_ref, dk_ref, dv_ref, ds_ref):
        b, n = pl.program_id(0), pl.program_id(1)
        valid = _attn_mask(n, S)

        @pl.when(n == 0)
        def _():
            dk_ref[...] = jnp.zeros_like(dk_ref)
            dv_ref[...] = jnp.zeros_like(dv_ref)

        @pl.when((n == 0) & (b == 0))
        def _():
            ds_ref[...] = jnp.zeros_like(ds_ref)

        for h in range(ATT_KV_HEADS):
            kb = jnp.concatenate([kp[0, :, _head(h)], kc[0, :, _head(h)], kn[0, :, _head(h)]], axis=0)
            vb = jnp.concatenate([vp[0, :, _head(h)], vc[0, :, _head(h)], vn[0, :, _head(h)]], axis=0)
            q4, sink4 = _group_rows(q_ref, s_ref, h)
            dy4 = jnp.concatenate([dy_ref[0, :, _head(h * ATT_GROUP + g)] for g in range(ATT_GROUP)], axis=0)
            _, vjp = jax.vjp(functools.partial(_attn_head, valid=valid), q4, kb, vb, sink4)
            dq4, dkb, dvb, dsink4 = vjp(dy4)
            for g in range(ATT_GROUP):
                rows = slice(g * ATT_BLOCK, (g + 1) * ATT_BLOCK)
                dq_ref[0, :, _head(h * ATT_GROUP + g)] = dq4[rows]
                ds_ref[h, g] += jnp.sum(dsink4[rows], axis=0, keepdims=True)
            for j, off in enumerate((-1, 0, 1)):
                start = pl.multiple_of(jnp.clip(n + off, 0, nq - 1) * ATT_BLOCK, ATT_BLOCK)
                rows = pl.ds(start, ATT_BLOCK)
                dk_ref[0, rows, _head(h)] += dkb[j * ATT_BLOCK:(j + 1) * ATT_BLOCK]
                dv_ref[0, rows, _head(h)] += dvb[j * ATT_BLOCK:(j + 1) * ATT_BLOCK]

    kv_shape = jax.ShapeDtypeStruct(k.shape, F32)
    return pl.pallas_call(
        body, name="attn_bwd", grid=(B, nq), in_specs=specs + [q_spec],
        out_specs=[q_spec, kv_full, kv_full, sink_spec],
        out_shape=[jax.ShapeDtypeStruct(q.shape, F32), kv_shape, kv_shape, jax.ShapeDtypeStruct(sink.shape, F32)],
        compiler_params=_cparams(("arbitrary", "arbitrary")),
    )(q, k, k, k, proj3, proj3, proj3, sink, dy)


CONV_COLS = 256


def _conv_taps(u, seq):
    row = lax.broadcasted_iota(jnp.int32, u.shape, 0)
    prev = jnp.where(row == 0, 0.0, pltpu.roll(u, 1, axis=0))
    nxt = jnp.where(row == seq - 1, 0.0, pltpu.roll(u, seq - 1, axis=0))
    return prev, nxt


def _conv_fwd(proj3, w8):
    B, S, _ = proj3.shape
    ncb = 2 * MLSTM_WIDTH // CONV_COLS

    def body(u_ref, w_ref, o_ref):
        u = u_ref[0]
        prev, nxt = _conv_taps(u, S)
        o_ref[0] = _silu(prev * w_ref[0:1, :] + u * w_ref[1:2, :] + nxt * w_ref[2:3, :] + w_ref[3:4, :])

    return pl.pallas_call(
        body, name="conv_fwd", grid=(B, ncb),
        in_specs=[pl.BlockSpec((1, S, CONV_COLS), lambda b, c: (b, 0, C_QK // CONV_COLS + c)),
                  pl.BlockSpec((8, CONV_COLS), lambda b, c: (0, c))],
        out_specs=pl.BlockSpec((1, S, CONV_COLS), lambda b, c: (b, 0, c)),
        out_shape=jax.ShapeDtypeStruct((B, S, 2 * MLSTM_WIDTH), F32),
        compiler_params=_cparams(("parallel", "parallel")),
    )(proj3, w8)


def _conv_bwd(proj3, w8, dout_f, dout_b):
    B, S, _ = proj3.shape
    ncb = 2 * MLSTM_WIDTH // CONV_COLS

    def body(u_ref, w_ref, df_ref, db_ref, du_ref, dw_ref):
        b = pl.program_id(1)
        u = u_ref[0]
        prev, nxt = _conv_taps(u, S)
        w0, w1, w2 = w_ref[0:1, :], w_ref[1:2, :], w_ref[2:3, :]
        pre = prev * w0 + u * w1 + nxt * w2 + w_ref[3:4, :]
        sig = _sigmoid(pre)
        dpre = (df_ref[0] + db_ref[0]) * (sig * (1.0 + pre * (1.0 - sig)))
        dprev, dnxt = _conv_taps(dpre, S)
        du_ref[0] = (dnxt * w0 + dpre * w1 + dprev * w2).astype(BF16)

        @pl.when(b == 0)
        def _():
            dw_ref[...] = jnp.zeros_like(dw_ref)

        dw_ref[0:1, :] += jnp.sum(dpre * prev, axis=0, keepdims=True)
        dw_ref[1:2, :] += jnp.sum(dpre * u, axis=0, keepdims=True)
        dw_ref[2:3, :] += jnp.sum(dpre * nxt, axis=0, keepdims=True)
        dw_ref[3:4, :] += jnp.sum(dpre, axis=0, keepdims=True)

    blk = pl.BlockSpec((1, S, CONV_COLS), lambda c, b: (b, 0, c))
    return pl.pallas_call(
        body, name="conv_bwd", grid=(ncb, B),
        in_specs=[pl.BlockSpec((1, S, CONV_COLS), lambda c, b: (b, 0, C_QK // CONV_COLS + c)),
                  pl.BlockSpec((8, CONV_COLS), lambda c, b: (0, c)), blk, blk],
        out_specs=[blk, pl.BlockSpec((8, CONV_COLS), lambda c, b: (0, c))],
        out_shape=[jax.ShapeDtypeStruct((B, S, 2 * MLSTM_WIDTH), BF16), jax.ShapeDtypeStruct((8, 2 * MLSTM_WIDTH), F32)],
        compiler_params=_cparams(("parallel", "arbitrary")),
    )(proj3, w8, dout_f, dout_b)


MLSTM_HEADS_PER_STEP = 4


def _chunk_masks(direction):
    t = lax.broadcasted_iota(jnp.int32, (MLSTM_CHUNK, MLSTM_CHUNK), 0)
    s = lax.broadcasted_iota(jnp.int32, (MLSTM_CHUNK, MLSTM_CHUNK), 1)
    le, ge = (s <= t).astype(F32), (s >= t).astype(F32)
    eye = (s == t).astype(F32)
    return (le, ge, eye) if direction == 0 else (ge, le, eye)


def _gate_cols(gates, direction, head):
    lane = lax.broadcasted_iota(jnp.int32, gates.shape, 1)
    sel_i = (lane == (2 * direction) * MLSTM_HEADS + head).astype(F32)
    sel_f = (lane == (2 * direction + 1) * MLSTM_HEADS + head).astype(F32)
    return sel_i, sel_f


def _mlstm_fwd(qk, proj3, bias):
    B, S, _ = qk.shape
    nc = S // MLSTM_CHUNK
    H, L, DH = MLSTM_HEADS, MLSTM_CHUNK, MLSTM_HEAD_DIM

    def chunk_of(d, c):
        return c if d == 0 else nc - 1 - c

    HS = MLSTM_HEADS_PER_STEP

    def body(qkf, qkb, vf, vb, gf, gb, bias_ref, hf, hb, csf, csb, nsf, nsb, msf, msb, c_st, n_st, m_st):
        c, hg = pl.program_id(1), pl.program_id(2)

        @pl.when(c == 0)
        def _():
            for d in range(2):
                for j in range(HS):
                    c_st[d, hg * HS + j] = jnp.zeros((DH, DH), F32)
                    n_st[d, hg * HS + j] = jnp.zeros((1, DH), F32)
                    m_st[d, hg * HS + j] = jnp.zeros((1, DH), F32)

        for d, (qk_ref, v_ref, g_ref, h_ref, cs, ns, ms) in enumerate(
                ((qkf, vf, gf, hf, csf, nsf, msf), (qkb, vb, gb, hb, csb, nsb, msb))):
            incl, incl_t, eye = _chunk_masks(d)
            gates = g_ref[0] + bias_ref[...]
            log_f = _log_sigmoid(gates)
            for j in range(HS):
                h = hg * HS + j
                sel_i, sel_f = _gate_cols(gates, d, h)
                li = jnp.sum(gates * sel_i, axis=1, keepdims=True)
                lf = jnp.sum(log_f * sel_f, axis=1, keepdims=True)
                c_in, n_in, m_in = c_st[d, h], n_st[d, h], m_st[d, h]
                cs[0, 0, j], ns[0, 0, j], ms[0, 0, j] = c_in, n_in, m_in
                hh, c_new, n_new, m_new = _mlstm_chunk(
                    qk_ref[0, :, 2 * j * DH:(2 * j + 1) * DH], qk_ref[0, :, (2 * j + 1) * DH:(2 * j + 2) * DH],
                    v_ref[0, :, j * DH:(j + 1) * DH], li, lf, c_in, n_in,
                    jnp.max(m_in, axis=1, keepdims=True), incl, incl_t, eye)
                h_ref[0, :, j * DH:(j + 1) * DH] = hh
                c_st[d, h], n_st[d, h] = c_new, n_new
                m_st[d, h] = jnp.broadcast_to(m_new, (1, DH))

    def tok_spec(width, base, d, per_head):
        return pl.BlockSpec((1, L, width), lambda b, c, h: (b, chunk_of(d, c), base + (h if per_head else 0)))

    def st_spec(shape, d):
        return pl.BlockSpec((1, 1, HS) + shape, lambda b, c, h: (b, chunk_of(d, c), h, 0, 0))

    in_specs = [tok_spec(2 * HS * DH, 0, 0, True), tok_spec(2 * HS * DH, 0, 1, True),
                tok_spec(HS * DH, C_VM // (HS * DH), 0, True), tok_spec(HS * DH, C_VM // (HS * DH), 1, True),
                tok_spec(LANES, C_GATES // LANES, 0, False), tok_spec(LANES, C_GATES // LANES, 1, False),
                pl.BlockSpec((1, LANES), lambda b, c, h: (0, 0))]
    out_specs = [tok_spec(HS * DH, 0, 0, True), tok_spec(HS * DH, 0, 1, True),
                 st_spec((DH, DH), 0), st_spec((DH, DH), 1), st_spec((1, DH), 0), st_spec((1, DH), 1),
                 st_spec((1, DH), 0), st_spec((1, DH), 1)]
    hs = jax.ShapeDtypeStruct((B, S, H * DH), F32)
    cs = jax.ShapeDtypeStruct((B, nc, H, DH, DH), F32)
    vs = jax.ShapeDtypeStruct((B, nc, H, 1, DH), F32)
    return pl.pallas_call(
        body, name="mlstm_fwd", grid=(B, nc, H // HS), in_specs=in_specs, out_specs=out_specs,
        out_shape=[hs, hs, cs, cs, vs, vs, vs, vs],
        scratch_shapes=[pltpu.VMEM((2, H, DH, DH), F32), pltpu.VMEM((2, H, 1, DH), F32), pltpu.VMEM((2, H, 1, DH), F32)],
        compiler_params=_cparams(("parallel", "arbitrary", "arbitrary")),
    )(qk, qk, proj3, proj3, proj3, proj3, bias)


def _mlstm_bwd(qk, proj3, bias, states, dh):
    B, S, _ = qk.shape
    nc = S // MLSTM_CHUNK
    H, L, DH = MLSTM_HEADS, MLSTM_CHUNK, MLSTM_HEAD_DIM

    def chunk_of(d, c):
        return nc - 1 - c if d == 0 else c

    HS = MLSTM_HEADS_PER_STEP

    def body(qkf, qkb, vf, vb, gf, gb, bias_ref, csf, csb, nsf, nsb, msf, msb, dhf, dhb,
             dqkf, dqkb, dvf, dvb, dgf, dgb, dc_st, dn_st, dm_st):
        c, hg = pl.program_id(1), pl.program_id(2)

        @pl.when(c == 0)
        def _():
            for d in range(2):
                for j in range(HS):
                    dc_st[d, hg * HS + j] = jnp.zeros((DH, DH), F32)
                    dn_st[d, hg * HS + j] = jnp.zeros((1, DH), F32)
                    dm_st[d, hg * HS + j] = jnp.zeros((1, DH), F32)

        @pl.when(hg == 0)
        def _():
            dgf[...] = jnp.zeros_like(dgf)
            dgb[...] = jnp.zeros_like(dgb)

        for d, (qk_ref, v_ref, g_ref, cs, ns, ms, dh_ref, dqk_ref, dv_ref, dg_ref) in enumerate(
                ((qkf, vf, gf, csf, nsf, msf, dhf, dqkf, dvf, dgf), (qkb, vb, gb, csb, nsb, msb, dhb, dqkb, dvb, dgb))):
            incl, incl_t, eye = _chunk_masks(d)
            gates = g_ref[0] + bias_ref[...]
            log_f = _log_sigmoid(gates)
            d_li = jnp.zeros_like(gates)
            d_lf = jnp.zeros_like(gates)
            for j in range(HS):
                h = hg * HS + j
                sel_i, sel_f = _gate_cols(gates, d, h)
                li = jnp.sum(gates * sel_i, axis=1, keepdims=True)
                lf = jnp.sum(log_f * sel_f, axis=1, keepdims=True)
                m_in = jnp.max(ms[0, 0, j], axis=1, keepdims=True)
                _, vjp = jax.vjp(
                    functools.partial(_mlstm_chunk, incl=incl, incl_t=incl_t, eye=eye),
                    qk_ref[0, :, 2 * j * DH:(2 * j + 1) * DH], qk_ref[0, :, (2 * j + 1) * DH:(2 * j + 2) * DH],
                    v_ref[0, :, j * DH:(j + 1) * DH], li, lf, cs[0, 0, j], ns[0, 0, j], m_in)
                dm_out = jnp.max(dm_st[d, h], axis=1, keepdims=True)
                dq, dk, dv, dli, dlf, dc, dn, dm = vjp((dh_ref[0, :, j * DH:(j + 1) * DH], dc_st[d, h], dn_st[d, h], dm_out))
                dqk_ref[0, :, 2 * j * DH:(2 * j + 1) * DH] = dq
                dqk_ref[0, :, (2 * j + 1) * DH:(2 * j + 2) * DH] = dk
                dv_ref[0, :, j * DH:(j + 1) * DH] = dv
                d_li += dli * sel_i
                d_lf += dlf * sel_f
                dc_st[d, h], dn_st[d, h] = dc, dn
                dm_st[d, h] = jnp.broadcast_to(dm, (1, DH))
            dg_ref[0] += d_li + d_lf * _sigmoid(-gates)

    def tok_spec(width, base, d, per_head):
        return pl.BlockSpec((1, L, width), lambda b, c, h: (b, chunk_of(d, c), base + (h if per_head else 0)))

    def st_spec(shape, d):
        return pl.BlockSpec((1, 1, HS) + shape, lambda b, c, h: (b, chunk_of(d, c), h, 0, 0))

    in_specs = [tok_spec(2 * HS * DH, 0, 0, True), tok_spec(2 * HS * DH, 0, 1, True),
                tok_spec(HS * DH, C_VM // (HS * DH), 0, True), tok_spec(HS * DH, C_VM // (HS * DH), 1, True),
                tok_spec(LANES, C_GATES // LANES, 0, False), tok_spec(LANES, C_GATES // LANES, 1, False),
                pl.BlockSpec((1, LANES), lambda b, c, h: (0, 0)),
                st_spec((DH, DH), 0), st_spec((DH, DH), 1), st_spec((1, DH), 0), st_spec((1, DH), 1),
                st_spec((1, DH), 0), st_spec((1, DH), 1), tok_spec(HS * DH, 0, 0, True), tok_spec(HS * DH, 0, 1, True)]
    out_specs = [tok_spec(2 * HS * DH, 0, 0, True), tok_spec(2 * HS * DH, 0, 1, True),
                 tok_spec(HS * DH, 0, 0, True), tok_spec(HS * DH, 0, 1, True),
                 tok_spec(LANES, 0, 0, False), tok_spec(LANES, 0, 1, False)]
    qks = jax.ShapeDtypeStruct((B, S, 2 * H * DH), F32)
    vs = jax.ShapeDtypeStruct((B, S, H * DH), F32)
    gs = jax.ShapeDtypeStruct((B, S, LANES), F32)
    csf, csb, nsf, nsb, msf, msb = states
    return pl.pallas_call(
        body, name="mlstm_bwd", grid=(B, nc, H // HS), in_specs=in_specs, out_specs=out_specs,
        out_shape=[qks, qks, vs, vs, gs, gs],
        scratch_shapes=[pltpu.VMEM((2, H, DH, DH), F32), pltpu.VMEM((2, H, 1, DH), F32), pltpu.VMEM((2, H, 1, DH), F32)],
        compiler_params=_cparams(("parallel", "arbitrary", "arbitrary")),
    )(qk, qk, proj3, proj3, proj3, proj3, bias, csf, csb, nsf, nsb, msf, msb, dh, dh)


ROW_BLOCK = 256
FF_COLS = 512
FF_SHARD = D_FF // N_DEV
FF_SHARD_PAD = 384
FF_PAD = N_DEV * FF_SHARD_PAD


def _rms_fwd(name, x, g):
    T = x.shape[0]
    return _rowwise(name, lambda xv, gv: _rms(xv, gv), [_In(x), _In(g, rows=False)], [_Out(D_MODEL, BF16)], T, ROW_BLOCK)[0]


def _rms_bwd(name, x, g, dh, dres):
    T = x.shape[0]

    def fn(xv, gv, dhv, drv):
        _, vjp = jax.vjp(_rms, xv, gv)
        dx, dg = vjp(dhv)
        return drv + dx, dg

    return _rowwise(name, fn, [_In(x), _In(g, rows=False), _In(dh), _In(dres)],
                    [_Out(D_MODEL), _Out(D_MODEL, rows=False)], T, ROW_BLOCK)


def _mmw(name, a, w, mode, **kw):
    if isinstance(w, tuple):
        return _matmul(name, a, w[0], mode, bl=w[1], **kw)
    return _matmul(name, a, w, mode, **kw)


def _swiglu(gate, up):
    return _silu(gate) * up


def _ffn_in(name, h, wg, wu):
    (M, K), N = h.shape, wg.shape[0]
    tm, tn = _first_divisor(M, (1024, 512, 256, 128)), FF_COLS

    def body(h_ref, wg_ref, wu_ref, g_ref, u_ref, a_ref):
        hv = h_ref[...]
        gate = _mm(hv, wg_ref[...], _NT)
        up = _mm(hv, wu_ref[...], _NT)
        g_ref[...], u_ref[...] = gate.astype(BF16), up.astype(BF16)
        a_ref[...] = _swiglu(gate, up).astype(BF16)

    w_spec = pl.BlockSpec((tn, K), lambda i, j: (j, 0))
    o_spec = pl.BlockSpec((tm, tn), lambda i, j: (i, j))
    return pl.pallas_call(
        body, name=name, grid=(M // tm, N // tn), in_specs=[pl.BlockSpec((tm, K), lambda i, j: (i, 0)), w_spec, w_spec],
        out_specs=[o_spec, o_spec, o_spec],
        out_shape=[jax.ShapeDtypeStruct((M, N), BF16)] * 3,
        compiler_params=_cparams(("parallel", "parallel")),
    )(h, wg, wu)


def _ffn_dact(name, dx, wd, gate, up):
    (M, K), N = dx.shape, wd.shape[0]
    tm, tn = _first_divisor(M, (1024, 512, 256, 128)), FF_COLS

    def body(dx_ref, wd_ref, g_ref, u_ref, dg_ref, du_ref):
        dact = _mm(dx_ref[...], wd_ref[...], _NT) * 0.5
        gate, up = g_ref[...].astype(F32), u_ref[...].astype(F32)
        s = _sigmoid(gate)
        silu = gate * s
        dg_ref[...] = (dact * up * (s + silu * (1.0 - s))).astype(BF16)
        du_ref[...] = (dact * silu).astype(BF16)

    o_spec = pl.BlockSpec((tm, tn), lambda i, j: (i, j))
    return pl.pallas_call(
        body, name=name, grid=(M // tm, N // tn),
        in_specs=[pl.BlockSpec((tm, K), lambda i, j: (i, 0)), pl.BlockSpec((tn, K), lambda i, j: (j, 0)), o_spec, o_spec],
        out_specs=[o_spec, o_spec],
        out_shape=[jax.ShapeDtypeStruct((M, N), BF16), jax.ShapeDtypeStruct((M, N), BF16)],
        compiler_params=_cparams(("parallel", "parallel")),
    )(dx, wd, gate, up)


def _ffn_dh(name, dgate, dup, wg, wu, dep, x, gain, dres):
    (M, K), N = dgate.shape, wg.shape[1]
    tm, tk = _first_divisor(M, (512, 256, 128)), _first_divisor(K, (1024, 512, 384, 256, 128))
    nk = K // tk

    def body(dg_ref, du_ref, wg_ref, wu_ref, x_ref, gain_ref, dres_ref, dep_ref, o_ref, dgain_ref, acc):
        i, k = pl.program_id(0), pl.program_id(1)

        @pl.when(k == 0)
        def _():
            acc[...] = jnp.zeros_like(acc)

        acc[...] += _mm(dg_ref[...], wg_ref[...], _NN) + _mm(du_ref[...], wu_ref[...], _NN)

        @pl.when((k == nk - 1) & (i == 0))
        def _():
            dgain_ref[...] = jnp.zeros_like(dgain_ref)

        @pl.when(k == nk - 1)
        def _():
            _, vjp = jax.vjp(_rms, x_ref[...], gain_ref[...])
            dx, dgain = vjp(acc[...])
            o_ref[...] = dres_ref[...] + dx
            dgain_ref[...] += dgain

    a_spec = pl.BlockSpec((tm, tk), lambda i, k: (i, k))
    w_spec = pl.BlockSpec((tk, N), lambda i, k: (k, 0))
    row_spec = pl.BlockSpec((tm, N), lambda i, k: (i, 0))
    gain_spec = pl.BlockSpec((1, N), lambda i, k: (0, 0))
    return pl.pallas_call(
        body, name=name, grid=(M // tm, nk),
        in_specs=[a_spec, a_spec, w_spec, w_spec, row_spec, gain_spec, row_spec, pl.BlockSpec(memory_space=pl.ANY)],
        out_specs=[row_spec, gain_spec],
        out_shape=[jax.ShapeDtypeStruct((M, N), F32), jax.ShapeDtypeStruct((1, N), F32)],
        scratch_shapes=[pltpu.VMEM((tm, N), F32)], compiler_params=_cparams(("arbitrary", "arbitrary")),
    )(dgate, dup, wg, wu, x, gain, dres, dep)


def _ffn_fwd(tag, x, g, wg, wu, wd):
    h = _rms_fwd(tag + "_norm", x, g)
    gate, up, act = _ffn_in(tag + "_in", h, wg, wu)
    if callable(wd):
        wd = wd(act)
    out = _mmw(tag + "_down", act, wd, "nn", res=x, scale=0.5, whole_k=True)
    return out, (x, h, gate, up, act), wd


def _ffn_bwd(tag, saved, g, wg, wu, wd, dx, on_dw):
    x, h, gate, up, act = saved
    dgate, dup = _ffn_dact(tag + "_dact", dx, wd, gate, up)
    dwd = _matmul(tag + "_dwd", act, dx, "tn", scale=0.5, out_dtype=BF16)
    dwg = _matmul(tag + "_dwg", dgate, h, "tn", out_dtype=BF16, whole_k=True)
    dwu = _matmul(tag + "_dwu", dup, h, "tn", out_dtype=BF16, whole_k=True)
    token = on_dw({tag + "_w_gate": dwg, tag + "_w_up": dwu, tag + "_w_down": dwd}, dwu)
    return _ffn_dh(tag + "_dh", dgate, dup, wg, wu, token, x, g, dx)


def _rope_cos_sin(positions):
    half = ROPE_DIM // 2
    inv_freq = jnp.power(jnp.float32(ROPE_THETA), -jnp.arange(half, dtype=F32) * (2.0 / ROPE_DIM))
    head = jnp.zeros((ATT_HEAD_DIM,), F32).at[:ROPE_DIM].set(jnp.concatenate([inv_freq, inv_freq]))
    row = jnp.tile(head, LANES // ATT_HEAD_DIM)[None, :]
    T = positions.shape[0]
    return _rowwise("rope_tables", _rope_tables, [_In(positions), _In(row, rows=False)], [_Out(LANES), _Out(LANES)], T, 1024)


def _prep_fwd(name, src, width, base, g, cos, sin):
    return _rowwise(name, _qk_prep, [_In(src, width, base), _In(g, rows=False), _In(cos), _In(sin)],
                    [_Out(width)], src.shape[0], 512)[0]


def _prep_bwd(name, src, width, base, g, cos, sin, dout, into=None):
    def fn(tv, gv, cv, sv, dv):
        _, vjp = jax.vjp(lambda a, b: _qk_prep(a, b, cv, sv), tv, gv)
        return vjp(dv)

    dsrc = _Out(width, BF16) if into is None else _Out(0, width=width, into=into, base=base)
    return _rowwise(name, fn, [_In(src, width, base), _In(g, rows=False), _In(cos), _In(sin), _In(dout)],
                    [dsrc, _Out(width, rows=False)], src.shape[0], 512)


def _to_heads(t, B, S, nh):
    return t.reshape(B, S, nh, ATT_HEAD_DIM).transpose(0, 2, 1, 3)


def _from_heads(t):
    B, nh, S, _ = t.shape
    return t.transpose(0, 2, 1, 3).reshape(B * S, nh * ATT_HEAD_DIM)


def _mix_fwd(x, cos, sin, B, S, p):
    T = B * S
    h = _rms_fwd("mix_norm", x, p["mix_norm"])
    proj = _matmul("mix_proj", h, p["w_in"], "nn")
    proj3 = proj.reshape(B, S, IN_PAD)
    q_gain = jnp.tile(p["attn_q_norm"], (1, ATT_HEADS))
    k_gain = jnp.tile(p["attn_k_norm"], (1, ATT_KV_HEADS))
    q_r = _prep_fwd("q_prep", proj, ATT_WIDTH, C_QA // ATT_WIDTH, q_gain, cos, sin)
    k_r = _prep_fwd("k_prep", proj, ATT_KV_WIDTH, C_KA // ATT_KV_WIDTH, k_gain, cos, sin)
    qh = q_r.reshape(B, S, ATT_WIDTH)
    kh = k_r.reshape(B, S, ATT_KV_WIDTH)
    sink = p["attn_sink"].reshape(ATT_KV_HEADS, ATT_GROUP, 1, 1)
    y_a = _attn_fwd(qh, kh, proj3, sink).reshape(T, ATT_WIDTH)

    qk_c = _conv_fwd(proj3, p["conv_w8"])
    hf, hb, *states = _mlstm_fwd(qk_c, proj3, p["gate_bias"])
    hf2, hb2 = hf.reshape(T, MLSTM_WIDTH), hb.reshape(T, MLSTM_WIDTH)
    DH = MLSTM_HEAD_DIM
    y_m = _rowwise("mlstm_out", _mlstm_combine,
                   [_In(hf2, DH, split=True), _In(hb2, DH, split=True), _In(proj, DH, C_OM // DH, split=True),
                    _In(p["mlstm_out_norm"], DH, split=True, rows=False)],
                   [_Out(MLSTM_WIDTH, BF16, DH, split=True)], T, 1024, ncol=MLSTM_HEADS)[0]

    za = _mmw("branch_a", y_a, p["w_branch_attn"], "nn")
    zm = _mmw("branch_m", y_m, p["w_branch_mlstm"], "nn")
    W = 512
    merged = _rowwise("merge", _merge,
                      [_In(proj, W, C_GMERGE // W, split=True), _In(proj, W, (C_GMERGE + D_MODEL) // W, split=True),
                       _In(za, W, split=True), _In(zm, W, split=True)],
                      [_Out(D_MODEL, BF16, W, split=True)], T, 512, ncol=D_MODEL // W)[0]
    out = _mmw("mix_out", merged, p["w_out"], "nn", res=x)
    saved = dict(x=x, h=h, proj=proj, q_gain=q_gain, k_gain=k_gain, qh=qh, kh=kh, sink=sink, y_a=y_a, qk_c=qk_c,
                 hf=hf2, hb=hb2, states=states, y_m=y_m, za=za, zm=zm, merged=merged)
    return out, saved


def _mix_bwd(sv, cos, sin, B, S, p, dx, on_dw):
    T = B * S
    DH = MLSTM_HEAD_DIM
    proj = sv["proj"]
    proj3 = proj.reshape(B, S, IN_PAD)
    g = {}
    dmerged = _mmw("mix_dmerged", dx, p["w_out"], "nt")
    g["w_out"] = _matmul("mix_dwout", sv["merged"], dx, "tn", out_dtype=BF16)
    dproj = lax.empty((T, IN_PAD), BF16)

    def merge_bwd(ga, gm, za, zm, dm):
        _, vjp = jax.vjp(_merge, ga, gm, za, zm)
        dga, dgm, dza, dzm = vjp(dm)
        return jnp.concatenate([dga, dgm], axis=1), dza, dzm

    dproj, dza, dzm = _rowwise(
        "merge_bwd", merge_bwd,
        [_In(proj, D_MODEL, C_GMERGE // D_MODEL), _In(proj, D_MODEL, C_GMERGE // D_MODEL + 1),
         _In(sv["za"]), _In(sv["zm"]), _In(dmerged)],
        [_Out(0, width=2 * D_MODEL, into=dproj, base=C_GMERGE // (2 * D_MODEL)), _Out(D_MODEL, BF16), _Out(D_MODEL, BF16)],
        T, ROW_BLOCK)
    dya = _mmw("branch_a_dx", dza, p["w_branch_attn"], "nt")
    g["w_branch_attn"] = _matmul("branch_a_dw", sv["y_a"], dza, "tn", out_dtype=BF16)
    dym = _mmw("branch_m_dx", dzm, p["w_branch_mlstm"], "nt")
    g["w_branch_mlstm"] = _matmul("branch_m_dw", sv["y_m"], dzm, "tn", out_dtype=BF16)

    def combine_bwd(hf, hb, o_pre, gn, dy):
        _, vjp = jax.vjp(_mlstm_combine, hf, hb, o_pre, gn)
        dhf, _, do, dg = vjp(dy)
        return dhf, do, dg

    dh, dproj, g["mlstm_out_norm"] = _rowwise(
        "mlstm_out_bwd", combine_bwd,
        [_In(sv["hf"], DH, split=True), _In(sv["hb"], DH, split=True), _In(proj, DH, C_OM // DH, split=True),
         _In(p["mlstm_out_norm"], DH, split=True, rows=False), _In(dym, DH, split=True)],
        [_Out(MLSTM_WIDTH, F32, DH, split=True), _Out(0, width=DH, split=True, into=dproj, base=C_OM // DH),
         _Out(MLSTM_WIDTH, F32, DH, split=True, rows=False)], T, 1024, ncol=MLSTM_HEADS)
    dqk_f, dqk_b, dv_f, dv_b, dg_f, dg_b = _mlstm_bwd(sv["qk_c"], proj3, p["gate_bias"], sv["states"],
                                                       dh.reshape(B, S, MLSTM_WIDTH))
    dproj, g["gate_bias"] = _rowwise(
        "mlstm_dsum_gates", lambda a, b: (a + b, jnp.sum(a + b, axis=0, keepdims=True)),
        [_In(dg_f.reshape(T, LANES)), _In(dg_b.reshape(T, LANES))],
        [_Out(0, width=LANES, into=dproj, base=C_GATES // LANES), _Out(LANES, rows=False)], T, 1024)
    dproj = _rowwise(
        "mlstm_dsum_v", lambda a, b: a + b, [_In(dv_f.reshape(T, MLSTM_WIDTH)), _In(dv_b.reshape(T, MLSTM_WIDTH))],
        [_Out(0, width=MLSTM_WIDTH, into=dproj, base=C_VM // MLSTM_WIDTH)], T, 1024)[0]
    dqk, g["conv_w8"] = _conv_bwd(proj3, p["conv_w8"], dqk_f, dqk_b)

    dqh, dkh, dvh, dsink = _attn_bwd(sv["qh"], sv["kh"], proj3, sv["sink"], dya.reshape(B, S, ATT_WIDTH))
    g["attn_sink"] = dsink.reshape(1, ATT_HEADS)
    dva = dvh.reshape(T, ATT_KV_WIDTH)
    dproj, dq_gain = _prep_bwd("q_prep_bwd", proj, ATT_WIDTH, C_QA // ATT_WIDTH, sv["q_gain"], cos, sin,
                               dqh.reshape(T, ATT_WIDTH), into=dproj)
    dka, dk_gain = _prep_bwd("k_prep_bwd", proj, ATT_KV_WIDTH, C_KA // ATT_KV_WIDTH, sv["k_gain"], cos, sin,
                             dkh.reshape(T, ATT_KV_WIDTH))
    g["attn_q_norm"] = jnp.sum(dq_gain.reshape(ATT_HEADS, ATT_HEAD_DIM), axis=0, keepdims=True)
    g["attn_k_norm"] = jnp.sum(dk_gain.reshape(ATT_KV_HEADS, ATT_HEAD_DIM), axis=0, keepdims=True)

    dproj = dproj.at[:, C_QK:C_QK + 2 * MLSTM_WIDTH].set(dqk.reshape(T, 2 * MLSTM_WIDTH))
    dproj = dproj.at[:, C_KA:C_KA + ATT_KV_WIDTH].set(dka)
    dproj = dproj.at[:, C_VA:C_VA + ATT_KV_WIDTH].set(dva.astype(BF16))
    dwin = _matmul("mix_dwin", sv["h"], dproj, "tn", out_dtype=BF16)
    token = on_dw({"w_in": _w_in_to_slots(dwin), "w_branch_attn": g.pop("w_branch_attn"),
                   "w_branch_mlstm": g.pop("w_branch_mlstm"), "w_out": g.pop("w_out")}, dwin)
    dh2 = _matmul("mix_dh", dproj, p["w_in"], "nt", dep=token)
    dx_new, g["mix_norm"] = _rms_bwd("mix_dnorm", sv["x"], p["mix_norm"], dh2, dx)
    return dx_new, g


def _loss_and_grad(x, g, target):
    T = x.shape[0]

    def loss_fn(xv, gv, tv):
        err = jnp.square(_rms(xv, gv) - tv)
        return 0.5 * jnp.sum(jnp.mean(err, axis=-1, keepdims=True), axis=0, keepdims=True)

    def fn(xv, gv, tv):
        val, vjp = jax.vjp(lambda a, b: loss_fn(a, b, tv), xv, gv)
        dx, dg = vjp(jnp.ones((1, 1), F32))
        return val, dx, dg

    return _rowwise("loss_head", fn, [_In(x), _In(g, rows=False), _In(target)],
                    [_Out(1, rows=False), _Out(D_MODEL), _Out(D_MODEL, rows=False)], T, ROW_BLOCK)


def _block_norm_fwd(x, g):
    T = x.shape[0]
    return _rowwise("block_norm", _rms, [_In(x), _In(g, rows=False)], [_Out(D_MODEL)], T, ROW_BLOCK)[0]


def _block_norm_bwd(x, g, dy):
    T = x.shape[0]

    def fn(xv, gv, dv):
        _, vjp = jax.vjp(_rms, xv, gv)
        return vjp(dv)

    return _rowwise("block_norm_bwd", fn, [_In(x), _In(g, rows=False), _In(dy)],
                    [_Out(D_MODEL), _Out(D_MODEL, rows=False)], T, ROW_BLOCK)


def _qk_perm_cols(t, axis):
    q, k = jnp.split(t, 2, axis=axis)
    parts = []
    for h in range(MLSTM_HEADS):
        sl = [slice(None)] * t.ndim
        sl[axis] = slice(h * MLSTM_HEAD_DIM, (h + 1) * MLSTM_HEAD_DIM)
        parts += [q[tuple(sl)], k[tuple(sl)]]
    return jnp.concatenate(parts, axis=axis)


def _qk_unperm_cols(t, axis):
    qs, ks = [], []
    for h in range(MLSTM_HEADS):
        sl = [slice(None)] * t.ndim
        sl[axis] = slice(2 * h * MLSTM_HEAD_DIM, (2 * h + 1) * MLSTM_HEAD_DIM)
        qs.append(t[tuple(sl)])
        sl[axis] = slice((2 * h + 1) * MLSTM_HEAD_DIM, (2 * h + 2) * MLSTM_HEAD_DIM)
        ks.append(t[tuple(sl)])
    return jnp.concatenate(qs + ks, axis=axis)


def _w_in_arrange(w):
    qa, ka, va, qm, km, vm, om, gm, gmerge = jnp.split(w, np.cumsum(
        (ATT_WIDTH, ATT_KV_WIDTH, ATT_KV_WIDTH, MLSTM_WIDTH, MLSTM_WIDTH, MLSTM_WIDTH, MLSTM_WIDTH, MLSTM_N_GATES))[:].tolist(), axis=1)
    qk = _qk_perm_cols(jnp.concatenate([qm, km], axis=1), 1)
    pad = jnp.zeros((w.shape[0], LANES - MLSTM_N_GATES), w.dtype)
    return jnp.concatenate([gmerge, qk, vm, om, qa, ka, va, gm, pad], axis=1)


def _w_in_restore(w):
    gmerge = w[:, C_GMERGE:C_GMERGE + 2 * D_MODEL]
    qk = _qk_unperm_cols(w[:, C_QK:C_QK + 2 * MLSTM_WIDTH], 1)
    vm, om = w[:, C_VM:C_VM + MLSTM_WIDTH], w[:, C_OM:C_OM + MLSTM_WIDTH]
    qa, ka, va = w[:, C_QA:C_QA + ATT_WIDTH], w[:, C_KA:C_KA + ATT_KV_WIDTH], w[:, C_VA:C_VA + ATT_KV_WIDTH]
    gm = w[:, C_GATES:C_GATES + MLSTM_N_GATES]
    return jnp.concatenate([qa, ka, va, qk, vm, om, gm, gmerge], axis=1)


BIG = ("ffn1_w_gate", "ffn1_w_up", "ffn1_w_down", "w_in", "mlstm_conv_w", "w_branch_attn", "w_branch_mlstm", "w_out",
       "ffn2_w_gate", "ffn2_w_up", "ffn2_w_down")
MATMUL_W = tuple(n for n in BIG if n != "mlstm_conv_w")
SMALL = ("ffn1_norm", "mix_norm", "mlstm_gate_bias", "attn_q_norm", "attn_k_norm", "attn_sink", "mlstm_conv_b",
         "mlstm_out_norm", "ffn2_norm", "block_out_norm")
WEIGHTS = ("ffn1_norm", "ffn1_w_gate", "ffn1_w_up", "ffn1_w_down", "mix_norm", "w_in", "mlstm_gate_bias", "attn_q_norm",
           "attn_k_norm", "attn_sink", "mlstm_conv_w", "mlstm_conv_b", "mlstm_out_norm", "w_branch_attn", "w_branch_mlstm",
           "w_out", "ffn2_norm", "ffn2_w_gate", "ffn2_w_up", "ffn2_w_down", "block_out_norm")
PACK_COLS = 1024


def _padded_rows(n_elems):
    return -(-n_elems // PACK_COLS)


def _pack_flat(arrs, dtype, row_multiple):
    parts = []
    for a in arrs:
        flat = a.reshape(-1).astype(dtype)
        pad = _padded_rows(flat.shape[0]) * PACK_COLS - flat.shape[0]
        parts.append(jnp.pad(flat, (0, pad)) if pad else flat)
    flat = jnp.concatenate(parts)
    rows = flat.shape[0] // PACK_COLS
    extra = (-rows) % row_multiple
    if extra:
        flat = jnp.pad(flat, (0, extra * PACK_COLS))
    return flat.reshape(-1, PACK_COLS)


def _pack_rows(name, pieces, total_rows):
    def body(*refs):
        o_ref = refs[-1]
        o_ref[...] = jnp.zeros_like(o_ref)
        at = 0
        for ref, (arr, r0, nr) in zip(refs[:-1], pieces):
            o_ref[at:at + nr, 0:arr.shape[1]] = ref[r0:r0 + nr, :].astype(F32)
            at += nr

    return pl.pallas_call(body, name=name, out_shape=jax.ShapeDtypeStruct((total_rows, PACK_COLS), F32))(
        *[p[0] for p in pieces])


def _unpack_flat(buf, shapes, lead=()):
    flat = buf.reshape(lead + (-1,))
    out, off = [], 0
    for s in shapes:
        n = int(np.prod(s))
        out.append(flat[..., off:off + n].reshape(lead + tuple(s)))
        off += _padded_rows(n) * PACK_COLS
    return out


class _Lay:
    def __init__(self, shard, axis, width):
        self.shard, self.axis, self.width = shard, axis, width
        self.padded = tuple(width if a == axis else s for a, s in enumerate(shard))
        self.whole = tuple(N_DEV * width if a == axis else s for a, s in enumerate(shard))

    def pad(self, t, lead=0):
        extra = self.width - self.shard[self.axis]
        if not extra:
            return t
        cfg = [(0, 0)] * t.ndim
        cfg[lead + self.axis] = (0, extra)
        return jnp.pad(t, cfg)

    def unpad(self, t, lead=0):
        idx = [slice(None)] * t.ndim
        idx[lead + self.axis] = slice(0, self.shard[self.axis])
        return t[tuple(idx)]


_FF_ROW = _Lay((FF_SHARD, D_MODEL), 0, FF_SHARD_PAD)
TRANSPOSED = ("ffn1_w_gate", "ffn1_w_up", "ffn2_w_gate", "ffn2_w_up")
LAYOUTS = {
    "ffn1_w_gate": _FF_ROW, "ffn1_w_up": _FF_ROW, "ffn1_w_down": _FF_ROW,
    "ffn2_w_gate": _FF_ROW, "ffn2_w_up": _FF_ROW, "ffn2_w_down": _FF_ROW,
    "w_in": _Lay((D_MODEL, IN_WIDTH // N_DEV), 0, D_MODEL),
    "mlstm_conv_w": _Lay((3, 2 * MLSTM_WIDTH // N_DEV), 1, 2 * MLSTM_WIDTH // N_DEV),
    "w_branch_attn": _Lay((ATT_WIDTH, D_MODEL // N_DEV), 1, D_MODEL // N_DEV),
    "w_branch_mlstm": _Lay((MLSTM_WIDTH, D_MODEL // N_DEV), 1, D_MODEL // N_DEV),
    "w_out": _Lay((D_MODEL // N_DEV, D_MODEL), 0, D_MODEL // N_DEV),
}


def _window(ref, axis, j, width):
    idx = [slice(None)] * len(ref.shape)
    idx[axis] = pl.ds(pl.multiple_of(j * width, width), width)
    return ref.at[tuple(idx)]


ANY = pl.BlockSpec(memory_space=pl.ANY)


def _mesh_pos():
    return lax.axis_index("x"), lax.axis_index("y"), lax.axis_index("c")


def _all_gather(name, shard, vmem=False):
    R, C = shard.shape
    space = pl.BlockSpec(memory_space=pltpu.VMEM) if vmem else ANY

    def body(x_ref, out_ref, send_sems, recv_sems, local_sem):
        x, y, c = _mesh_pos()
        me, sibling = (x, y, c), (x, y, 1 - c)
        chips = [(1 - x, y), (x, 1 - y), (1 - x, 1 - y)]

        def slot(px, py, pc):
            return out_ref.at[4 * px + 2 * py + pc]

        def copy(k, block, to, src=None):
            return pltpu.make_async_remote_copy(
                src_ref=slot(*block) if src is None else src, dst_ref=slot(*block),
                send_sem=send_sems.at[k], recv_sem=recv_sems.at[k], device_id=to, device_id_type=MESH)

        mine = pltpu.make_async_copy(x_ref, slot(*me), local_sem)
        mine.start()
        first = [copy(0, me, sibling, src=x_ref)]
        first += [copy(1 + j, me, (*chip, c), src=x_ref) for j, chip in enumerate(chips)]
        for cp in first:
            cp.start()
        passed = [copy(4 + j, (*chip, c), sibling) for j, chip in enumerate(chips)]
        for j, chip in enumerate(chips):
            copy(1 + j, (*chip, c), me).wait_recv()
            passed[j].start()
        copy(0, sibling, me).wait_recv()
        for j, chip in enumerate(chips):
            copy(4 + j, (*chip, 1 - c), me).wait_recv()
        for cp in first + passed:
            cp.wait_send()
        mine.wait()

    return pl.pallas_call(
        body, name=name, out_shape=jax.ShapeDtypeStruct((N_DEV, R, C), shard.dtype),
        in_specs=[space], out_specs=space,
        scratch_shapes=[pltpu.SemaphoreType.DMA((7,)), pltpu.SemaphoreType.DMA((7,)), pltpu.SemaphoreType.DMA],
    )(shard)


HBM = pl.BlockSpec(memory_space=pltpu.HBM)
SEM = pl.BlockSpec(memory_space=pltpu.SEMAPHORE)
SPLIT_COPY = pltpu.CompilerParams(has_side_effects=pltpu.SideEffectType.DATAFLOW_SIDE_EFFECTING)
N_PEERS = N_DEV - 1


def _peers(x, y, c):
    return [(x, y, 1 - c), (1 - x, y, c), (x, 1 - y, c), (1 - x, 1 - y, c),
            (1 - x, y, 1 - c), (x, 1 - y, 1 - c), (1 - x, 1 - y, 1 - c)]


def _dev_index(pos):
    return 4 * pos[0] + 2 * pos[1] + pos[2]


def _place_own(name, stacks, layer, lays):
    nt = len(stacks)
    me = _dev_index(_mesh_pos())

    def body(me_ref, *refs):
        for x_ref, s_ref, o_ref, lay in zip(refs[:nt], refs[nt:2 * nt], refs[2 * nt:], lays):
            rows = lay.shard[0]
            if lay.padded != lay.shard:
                s_ref[...] = jnp.zeros_like(s_ref)
            s_ref[0:rows, :] = x_ref[...].astype(BF16)
            o_ref[...] = s_ref[...]

    def window_spec(lay):
        if lay.axis == 0:
            return pl.BlockSpec(lay.padded, lambda i, me_ref: (me_ref[0], 0))
        return pl.BlockSpec(lay.padded, lambda i, me_ref: (0, me_ref[0]))

    for lay in lays:
        assert lay.padded[1] == lay.shard[1], "only rows are padded"
    res = pl.pallas_call(
        body, name=name,
        grid_spec=pltpu.PrefetchScalarGridSpec(
            num_scalar_prefetch=1, grid=(1,),
            in_specs=[pl.BlockSpec((None,) + lay.shard, lambda i, me_ref: (layer, 0, 0)) for lay in lays],
            out_specs=[pl.BlockSpec(lay.padded, lambda i, me_ref: (0, 0)) for lay in lays] + [window_spec(lay) for lay in lays]),
        out_shape=[jax.ShapeDtypeStruct(lay.padded, BF16) for lay in lays] + [jax.ShapeDtypeStruct(lay.whole, BF16) for lay in lays],
        compiler_params=_cparams(("arbitrary",)),
    )(me.reshape(1).astype(jnp.int32), *stacks)
    return list(res[:nt]), list(res[nt:])


NEAR_PEERS = 4


def _gather_start(name, shards, lands, lays, groups, n_peers, after):
    nt, ng = len(shards), len(groups)

    def body(*refs):
        x_refs, land_refs = refs[:nt], refs[nt:2 * nt]
        sems = refs[2 * nt + 1:2 * nt + 1 + 2 * ng]
        pos = _mesh_pos()
        me = _dev_index(pos)
        for g, tens in enumerate(groups):
            for i, t in enumerate(tens):
                for k, peer in enumerate(_peers(*pos)[:n_peers[g]]):
                    pltpu.make_async_remote_copy(
                        src_ref=x_refs[t], dst_ref=_window(land_refs[t], lays[t].axis, me, lays[t].width),
                        send_sem=sems[2 * g].at[n_peers[g] * i + k], recv_sem=sems[2 * g + 1].at[n_peers[g] * i + k],
                        device_id=peer, device_id_type=MESH).start()

    sem_shapes = []
    for g, tens in enumerate(groups):
        sem_shapes += [pltpu.SemaphoreType.DMA((n_peers[g] * len(tens),))] * 2
    thru = [pltpu.HBM(s.shape, s.dtype) for s in shards] + [pltpu.HBM(lay.whole, s.dtype) for s, lay in zip(shards, lays)]
    args = [pltpu.with_memory_space_constraint(s, pltpu.HBM) for s in shards]
    args += [pltpu.with_memory_space_constraint(ld, pltpu.HBM) for ld in lands]
    res = pl.pallas_call(
        body, name=name, out_shape=tuple(sem_shapes + thru), in_specs=[HBM] * (2 * nt) + [ANY],
        out_specs=tuple([SEM] * (2 * ng) + [HBM] * (2 * nt)),
        input_output_aliases={t: 2 * ng + t for t in range(2 * nt)}, compiler_params=SPLIT_COPY,
    )(*args, after)
    sems = [(res[2 * g], res[2 * g + 1]) for g in range(ng)]
    return sems, list(res[2 * ng:2 * ng + nt]), list(res[2 * ng + nt:])


def _gather_wait(name, sems, shards, lands, lays, n_peers, after):
    nt = len(shards)
    send_sems, recv_sems = sems

    def body(*refs):
        x_refs, land_refs = refs[:nt], refs[nt:2 * nt]
        send_ref, recv_ref = refs[2 * nt], refs[2 * nt + 1]
        pos = _mesh_pos()
        for t in range(nt):
            for k, peer in enumerate(_peers(*pos)[:n_peers]):
                cp = pltpu.make_async_remote_copy(
                    src_ref=x_refs[t], dst_ref=_window(land_refs[t], lays[t].axis, _dev_index(peer), lays[t].width),
                    send_sem=send_ref.at[n_peers * t + k], recv_sem=recv_ref.at[n_peers * t + k],
                    device_id=peer, device_id_type=MESH)
                cp.wait_send()
                cp.wait_recv()

    thru = [pltpu.HBM(s.shape, s.dtype) for s in shards] + [pltpu.HBM(ld.shape, ld.dtype) for ld in lands]
    res = pl.pallas_call(
        body, name=name, out_shape=tuple(thru), in_specs=[HBM] * (2 * nt) + [SEM, SEM, ANY],
        out_specs=tuple([HBM] * (2 * nt)), input_output_aliases={t: t for t in range(2 * nt)},
        compiler_params=SPLIT_COPY,
    )(*shards, *lands, send_sems, recv_sems, after)
    return list(res[nt:])


def _forward_to_sibling(name, lands, lays):
    nt = len(lands)

    def body(*refs):
        land_refs = refs[nt:2 * nt]
        send_sems, recv_sems = refs[2 * nt:]
        x, y, c = _mesh_pos()
        chips = [(1 - x, y), (x, 1 - y), (1 - x, 1 - y)]

        def copy(t, j, core):
            win = _window(land_refs[t], lays[t].axis, _dev_index((*chips[j], core)), lays[t].width)
            return pltpu.make_async_remote_copy(
                src_ref=win, dst_ref=win, send_sem=send_sems.at[3 * t + j], recv_sem=recv_sems.at[3 * t + j],
                device_id=(x, y, 1 - c), device_id_type=MESH)

        sends = [copy(t, j, c) for t in range(nt) for j in range(3)]
        for cp in sends:
            cp.start()
        for t in range(nt):
            for j in range(3):
                copy(t, j, 1 - c).wait_recv()
        for cp in sends:
            cp.wait_send()

    return pl.pallas_call(
        body, name=name, out_shape=[jax.ShapeDtypeStruct(ld.shape, ld.dtype) for ld in lands],
        in_specs=[ANY] * nt, out_specs=[ANY] * nt, input_output_aliases={t: t for t in range(nt)},
        scratch_shapes=[pltpu.SemaphoreType.DMA((3 * nt,)), pltpu.SemaphoreType.DMA((3 * nt,))],
    )(*lands)


def _pair_exchange(name, grads, lays):
    nt = len(grads)

    def body(*refs):
        g_refs, land_refs = refs[:nt], refs[nt:2 * nt]
        send_sems, recv_sems = refs[2 * nt:]
        x, y, c = _mesh_pos()
        copies = []
        for t in range(nt):
            for chip in range(4):
                copies.append(pltpu.make_async_remote_copy(
                    src_ref=_window(g_refs[t], lays[t].axis, 2 * chip + (1 - c), lays[t].width), dst_ref=land_refs[t].at[chip],
                    send_sem=send_sems.at[4 * t + chip], recv_sem=recv_sems.at[4 * t + chip],
                    device_id=(x, y, 1 - c), device_id_type=MESH))
        for cp in copies:
            cp.start()
        for cp in copies:
            cp.wait_recv()
        for cp in copies:
            cp.wait_send()

    out_shape = [jax.ShapeDtypeStruct((4,) + lay.padded, g.dtype) for g, lay in zip(grads, lays)]
    return pl.pallas_call(
        body, name=name, out_shape=out_shape, in_specs=[ANY] * nt, out_specs=[ANY] * nt,
        scratch_shapes=[pltpu.SemaphoreType.DMA((4 * nt,)), pltpu.SemaphoreType.DMA((4 * nt,))],
    )(*grads)


def _pair_sum(name, whole, landed, lay, out_dtype):
    R, C = lay.padded
    br = _first_divisor(R, (512, 384, 256, 128, 64, 32, 16, 8))
    nb = R // br
    if lay.axis == 0:
        mine_spec = pl.BlockSpec((br, C), lambda k, i, c_ref: ((2 * k + c_ref[0]) * nb + i, 0))
    else:
        mine_spec = pl.BlockSpec((br, C), lambda k, i, c_ref: (i, 2 * k + c_ref[0]))

    def body(c_ref, mine_ref, sib_ref, o_ref):
        o_ref[0] = (mine_ref[...].astype(F32) + sib_ref[0].astype(F32)).astype(out_dtype)

    c = lax.axis_index("c")
    return pl.pallas_call(
        body, name=name,
        grid_spec=pltpu.PrefetchScalarGridSpec(
            num_scalar_prefetch=1, grid=(4, nb),
            in_specs=[mine_spec, pl.BlockSpec((1, br, C), lambda k, i, c_ref: (k, i, 0))],
            out_specs=pl.BlockSpec((1, br, C), lambda k, i, c_ref: (k, i, 0))),
        out_shape=jax.ShapeDtypeStruct((4, R, C), out_dtype),
        compiler_params=_cparams(("parallel", "parallel")),
    )(c.reshape(1).astype(jnp.int32), whole, landed)


def _chip_exchange(name, sums):
    nt = len(sums)

    def body(*refs):
        s_refs, land_refs = refs[:nt], refs[nt:2 * nt]
        send_sems, recv_sems, local_sems = refs[2 * nt:]
        x, y, c = _mesh_pos()
        my_chip = 2 * x + y
        mine = [pltpu.make_async_copy(s_refs[t].at[my_chip], land_refs[t].at[my_chip], local_sems.at[t]) for t in range(nt)]
        for cp in mine:
            cp.start()
        chips = [(1 - x, y), (x, 1 - y), (1 - x, 1 - y)]
        copies = []
        for t in range(nt):
            for j, (px, py) in enumerate(chips):
                copies.append(pltpu.make_async_remote_copy(
                    src_ref=s_refs[t].at[2 * px + py], dst_ref=land_refs[t].at[my_chip],
                    send_sem=send_sems.at[3 * t + j], recv_sem=recv_sems.at[3 * t + j],
                    device_id=(px, py, c), device_id_type=MESH))
        for cp in copies:
            cp.start()
        for t in range(nt):
            for j, (px, py) in enumerate(chips):
                pltpu.make_async_remote_copy(
                    src_ref=s_refs[t].at[my_chip], dst_ref=land_refs[t].at[2 * px + py],
                    send_sem=send_sems.at[3 * t + j], recv_sem=recv_sems.at[3 * t + j],
                    device_id=(px, py, c), device_id_type=MESH).wait_recv()
        for cp in copies:
            cp.wait_send()
        for cp in mine:
            cp.wait()

    return pl.pallas_call(
        body, name=name, out_shape=[jax.ShapeDtypeStruct(s.shape, s.dtype) for s in sums],
        in_specs=[ANY] * nt, out_specs=[ANY] * nt,
        scratch_shapes=[pltpu.SemaphoreType.DMA((3 * nt,)), pltpu.SemaphoreType.DMA((3 * nt,)), pltpu.SemaphoreType.DMA((nt,))],
    )(*sums)


def _chip_start(name, sums):
    nt = len(sums)

    def body(*refs):
        s_refs, land_refs = refs[:nt], refs[nt:2 * nt]
        send_sems, recv_sems = refs[2 * nt], refs[2 * nt + 1]
        x, y, c = _mesh_pos()
        my_chip = 2 * x + y
        for t in range(nt):
            for j, (px, py) in enumerate([(1 - x, y), (x, 1 - y), (1 - x, 1 - y)]):
                pltpu.make_async_remote_copy(
                    src_ref=s_refs[t].at[2 * px + py], dst_ref=land_refs[t].at[my_chip],
                    send_sem=send_sems.at[3 * t + j], recv_sem=recv_sems.at[3 * t + j],
                    device_id=(px, py, c), device_id_type=MESH).start()

    thru = [pltpu.HBM(s.shape, s.dtype) for s in sums] * 2
    args = [pltpu.with_memory_space_constraint(s, pltpu.HBM) for s in sums]
    args += [pltpu.with_memory_space_constraint(lax.empty(s.shape, s.dtype), pltpu.HBM) for s in sums]
    res = pl.pallas_call(
        body, name=name, out_shape=tuple([pltpu.SemaphoreType.DMA((3 * nt,))] * 2 + thru), in_specs=[HBM] * (2 * nt),
        out_specs=tuple([SEM, SEM] + [HBM] * (2 * nt)), input_output_aliases={t: 2 + t for t in range(2 * nt)},
        compiler_params=SPLIT_COPY,
    )(*args)
    return (res[0], res[1]), list(res[2:2 + nt]), list(res[2 + nt:])


def _chip_wait(name, sems, sums, lands, after):
    nt = len(sums)

    def body(*refs):
        s_refs, land_refs = refs[:nt], refs[nt:2 * nt]
        send_sems, recv_sems = refs[2 * nt], refs[2 * nt + 1]
        x, y, c = _mesh_pos()
        my_chip = 2 * x + y
        for t in range(nt):
            for j, (px, py) in enumerate([(1 - x, y), (x, 1 - y), (1 - x, 1 - y)]):
                cp = pltpu.make_async_remote_copy(
                    src_ref=s_refs[t].at[my_chip], dst_ref=land_refs[t].at[2 * px + py],
                    send_sem=send_sems.at[3 * t + j], recv_sem=recv_sems.at[3 * t + j],
                    device_id=(px, py, c), device_id_type=MESH)
                cp.wait_send()
                cp.wait_recv()

    thru = [pltpu.HBM(s.shape, s.dtype) for s in sums] * 2
    res = pl.pallas_call(
        body, name=name, out_shape=tuple(thru), in_specs=[HBM] * (2 * nt) + [SEM, SEM, ANY],
        out_specs=tuple([HBM] * (2 * nt)), input_output_aliases={t: t for t in range(2 * nt)},
        compiler_params=SPLIT_COPY,
    )(*sums, *lands, sems[0], sems[1], after)
    return list(res[:nt]), list(res[nt:])


def _sum_chips(name, own, landed):
    _, R, C = own.shape
    br = _first_divisor(R, (512, 384, 256, 128, 64, 32, 16, 8))
    x, y, _ = _mesh_pos()
    slots = jnp.stack([2 * x + y, 2 * (1 - x) + y, 2 * x + (1 - y), 2 * (1 - x) + (1 - y)]).astype(jnp.int32)

    def body(slot_ref, mine_ref, a_ref, b_ref, c_ref, o_ref):
        o_ref[...] = ((mine_ref[0].astype(F32) + a_ref[0].astype(F32)) + b_ref[0].astype(F32)) + c_ref[0].astype(F32)

    def slot_spec(j):
        return pl.BlockSpec((1, br, C), lambda i, slot_ref: (slot_ref[j], i, 0))

    return pl.pallas_call(
        body, name=name,
        grid_spec=pltpu.PrefetchScalarGridSpec(
            num_scalar_prefetch=1, grid=(R // br,), in_specs=[slot_spec(0), slot_spec(1), slot_spec(2), slot_spec(3)],
            out_specs=pl.BlockSpec((br, C), lambda i, slot_ref: (i, 0))),
        out_shape=jax.ShapeDtypeStruct((R, C), F32), compiler_params=_cparams(("parallel",)),
    )(slots, own, landed, landed, landed)


def _sum_slots(name, slots, n):
    _, R, C = slots.shape
    br = _first_divisor(R, (512, 384, 256, 128, 64, 32, 16, 8))

    def body(s_ref, o_ref):
        acc = s_ref[0].astype(F32)
        for k in range(1, n):
            acc = acc + s_ref[k].astype(F32)
        o_ref[...] = acc

    return pl.pallas_call(
        body, name=name, grid=(R // br,), in_specs=[pl.BlockSpec((n, br, C), lambda i: (0, i, 0))],
        out_specs=pl.BlockSpec((br, C), lambda i: (i, 0)), out_shape=jax.ShapeDtypeStruct((R, C), F32),
        compiler_params=_cparams(("parallel",)),
    )(slots)


def _reduce_scatter_start(tag, names, grads):
    lays = [LAYOUTS[n] for n in names]
    landed = _pair_exchange("grads_pair_" + names[0], grads, lays)
    sums = [_pair_sum("grads_pairsum_" + n, g, ld, lay, BF16) for n, g, ld, lay in zip(names, grads, landed, lays)]
    sems, sums, lands = _chip_start(tag + "_chips_start", sums)
    return tag, names, sems, sums, lands


def _reduce_scatter_finish(pending, after):
    tag, names, sems, sums, lands = pending
    own, got = _chip_wait(tag + "_chips_wait", sems, sums, lands, after)
    return [_sum_chips("grads_sum_" + n, o, s) for n, o, s in zip(names, own, got)]


def _adamw_math(w, g, m, v):
    m = ADAM_B1 * m + (1.0 - ADAM_B1) * g
    v = ADAM_B2 * v + (1.0 - ADAM_B2) * jnp.square(g)
    m_hat = m / (1.0 - ADAM_B1 ** ADAM_STEP)
    v_hat = v / (1.0 - ADAM_B2 ** ADAM_STEP)
    delta = -ADAM_LR * (m_hat / (jnp.sqrt(v_hat) + ADAM_EPS) + ADAM_WD * w)
    return delta, m, v


def _adamw_layers(name, w, totals, m, v):
    _, R, C = w.shape
    br = _first_divisor(R, (512, 176, 128, 64, 32, 16, 8))
    Cp = totals[0].shape[1]

    def body(w_ref, g0_ref, g1_ref, m_ref, v_ref, g_out, d_out, m_out, v_out):
        g = jnp.where(pl.program_id(0) == 0, g0_ref[:, 0:C], g1_ref[:, 0:C])
        delta, m_new, v_new = _adamw_math(w_ref[0], g, m_ref[0], v_ref[0])
        g_out[0], d_out[0], m_out[0], v_out[0] = g, delta, m_new, v_new

    blk = pl.BlockSpec((1, br, C), lambda l, i: (l, i, 0))
    g_spec = pl.BlockSpec((br, Cp), lambda l, i: (i, 0))
    return pl.pallas_call(
        body, name=name, grid=(DEPTH, R // br), in_specs=[blk, g_spec, g_spec, blk, blk], out_specs=[blk] * 4,
        out_shape=[jax.ShapeDtypeStruct(w.shape, F32)] * 4, compiler_params=_cparams(("parallel", "parallel")),
    )(w, totals[0], totals[1], m, v)


def _adamw(name, w, g, m, v):
    shape = w.shape
    cols = shape[-1]
    rows = int(np.prod(shape[:-1]))
    br = _first_divisor(rows, (512, 352, 256, 128, 64, 32, 16, 8))
    args = [_In(a.reshape(rows, cols)) for a in (w, g, m, v)]
    outs = _rowwise(name, _adamw_math, args, [_Out(cols), _Out(cols), _Out(cols)], rows, br)
    return [o.reshape(shape) for o in outs]


GROUPS = {"ffn1": ("ffn1_w_gate", "ffn1_w_up", "ffn1_w_down"),
          "mix": ("w_in", "w_branch_attn", "w_branch_mlstm", "w_out"),
          "ffn2": ("ffn2_w_gate", "ffn2_w_up", "ffn2_w_down")}
GATHER_GROUPS = {"ffn1_in": ("ffn1_w_gate", "ffn1_w_up"), "ffn1_out": ("ffn1_w_down",),
                 "mix": ("w_in", "w_branch_attn", "w_branch_mlstm", "w_out"),
                 "ffn2_in": ("ffn2_w_gate", "ffn2_w_up"), "ffn2_out": ("ffn2_w_down",)}


def _small_params(small, conv_w, l):
    p = {}
    for n in ("ffn1_norm", "mix_norm", "ffn2_norm", "block_out_norm", "mlstm_out_norm", "attn_q_norm", "attn_k_norm"):
        p[n] = small[n][l][None, :]
    p["attn_sink"] = small["attn_sink"][l]
    p["gate_bias"] = jnp.pad(small["mlstm_gate_bias"][l], (0, LANES - MLSTM_N_GATES))[None, :]
    taps = _qk_perm_cols(conv_w[l], 1)
    conv_b = _qk_perm_cols(small["mlstm_conv_b"][l][None, :], 1)
    p["conv_w8"] = jnp.concatenate([taps, conv_b, jnp.zeros((4, 2 * MLSTM_WIDTH), F32)], axis=0)
    return p


def _w_in_from_slots(slots):
    w_in = slots.reshape(N_DEV, D_MODEL, IN_WIDTH // N_DEV).transpose(1, 0, 2).reshape(D_MODEL, IN_WIDTH)
    return _w_in_arrange(w_in)


def _w_in_to_slots(g):
    return _w_in_restore(g).reshape(D_MODEL, N_DEV, IN_WIDTH // N_DEV).transpose(1, 0, 2).reshape(
        N_DEV * D_MODEL, IN_WIDTH // N_DEV)


def _local_step(x, positions, target, weights_of, small, conv_w, on_grads):
    B, S, _ = x.shape
    T = B * S
    cos, sin = _rope_cos_sin(positions.reshape(T, 1))
    params = [_small_params(small, conv_w, l) for l in range(DEPTH)]
    xs = x.reshape(T, D_MODEL)
    tgt = target.reshape(T, D_MODEL)

    saved = []
    for l, p in enumerate(params):
        p.update(weights_of(l, "ffn1_in", xs))
        x1, s1, p["ffn1_w_down"] = _ffn_fwd("ffn1", xs, p["ffn1_norm"], p["ffn1_w_gate"], p["ffn1_w_up"],
                                            lambda after, l=l: weights_of(l, "ffn1_out", after)["ffn1_w_down"])
        p.update(weights_of(l, "mix", x1))
        p["w_in"] = _w_in_from_slots(p["w_in"])
        x2, s2 = _mix_fwd(x1, cos, sin, B, S, p)
        p.update(weights_of(l, "ffn2_in", x2))
        x3, s3, p["ffn2_w_down"] = _ffn_fwd("ffn2", x2, p["ffn2_norm"], p["ffn2_w_gate"], p["ffn2_w_up"],
                                            lambda after, l=l: weights_of(l, "ffn2_out", after)["ffn2_w_down"])
        saved.append((s1, s2, s3, x3))
        if l + 1 < DEPTH:
            xs = _block_norm_fwd(x3, p["block_out_norm"])

    sm = {}
    loss = None
    dx = None
    for l in reversed(range(DEPTH)):
        p = params[l]
        s1, s2, s3, x3 = saved[l]
        if l == DEPTH - 1:
            loss, dx, dgn = _loss_and_grad(x3, p["block_out_norm"], tgt)
        else:
            dx, dgn = _block_norm_bwd(x3, p["block_out_norm"], dx)
        sm["block_out_norm", l] = (dgn, 0, 1)
        dx, dg = _ffn_bwd("ffn2", s3, p["ffn2_norm"], p["ffn2_w_gate"], p["ffn2_w_up"], p["ffn2_w_down"], dx,
                          functools.partial(on_grads, l, "ffn2"))
        sm["ffn2_norm", l] = (dg, 0, 1)
        dx, g = _mix_bwd(s2, cos, sin, B, S, p, dx, functools.partial(on_grads, l, "mix"))
        dconv = _qk_unperm_cols(g["conv_w8"], 1)
        sm["mlstm_conv_w", l] = (dconv, 0, 3)
        sm["mlstm_conv_b", l] = (dconv, 3, 1)
        for n, key in (("mix_norm", "mix_norm"), ("mlstm_gate_bias", "gate_bias"), ("attn_q_norm", "attn_q_norm"),
                       ("attn_k_norm", "attn_k_norm"), ("attn_sink", "attn_sink"), ("mlstm_out_norm", "mlstm_out_norm")):
            sm[n, l] = (g[key], 0, 1)
        dx, dg = _ffn_bwd("ffn1", s1, p["ffn1_norm"], p["ffn1_w_gate"], p["ffn1_w_up"], p["ffn1_w_down"], dx,
                          functools.partial(on_grads, l, "ffn1"))
        sm["ffn1_norm", l] = (dg, 0, 1)
    return loss, dx.reshape(B, S, D_MODEL), sm


def kernel(x, positions, ffn1_norm, ffn1_w_gate, ffn1_w_up, ffn1_w_down, mix_norm, w_in, mlstm_gate_bias, attn_q_norm, attn_k_norm, attn_sink, mlstm_conv_w, mlstm_conv_b, mlstm_out_norm, w_branch_attn, w_branch_mlstm, w_out, ffn2_norm, ffn2_w_gate, ffn2_w_up, ffn2_w_down, block_out_norm, loss_target, m_ffn1_norm, m_ffn1_w_gate, m_ffn1_w_up, m_ffn1_w_down, m_mix_norm, m_w_in, m_mlstm_gate_bias, m_attn_q_norm, m_attn_k_norm, m_attn_sink, m_mlstm_conv_w, m_mlstm_conv_b, m_mlstm_out_norm, m_w_branch_attn, m_w_branch_mlstm, m_w_out, m_ffn2_norm, m_ffn2_w_gate, m_ffn2_w_up, m_ffn2_w_down, m_block_out_norm, v_ffn1_norm, v_ffn1_w_gate, v_ffn1_w_up, v_ffn1_w_down, v_mix_norm, v_w_in, v_mlstm_gate_bias, v_attn_q_norm, v_attn_k_norm, v_attn_sink, v_mlstm_conv_w, v_mlstm_conv_b, v_mlstm_out_norm, v_w_branch_attn, v_w_branch_mlstm, v_w_out, v_ffn2_norm, v_ffn2_w_gate, v_ffn2_w_up, v_ffn2_w_down, v_block_out_norm):
    args = locals()
    def stored(n, t):
        return t.transpose(0, 2, 1) if n in TRANSPOSED else t

    w = {n: stored(n, args[n]) for n in WEIGHTS}
    m = {n: stored(n, args["m_" + n]) for n in WEIGHTS}
    v = {n: stored(n, args["v_" + n]) for n in WEIGHTS}

    order = [(l, grp) for l in range(DEPTH) for grp in GATHER_GROUPS]
    keys = [(l, n) for l, grp in order for n in GATHER_GROUPS[grp]]
    lays = [LAYOUTS[n] for _, n in keys]
    group_idx, at = {}, 0
    for l, grp in order:
        group_idx[(l, grp)] = list(range(at, at + len(GATHER_GROUPS[grp])))
        at += len(GATHER_GROUPS[grp])
    conv_shape = w["mlstm_conv_w"].shape
    conv_all = _all_gather("conv_all_gather", _pack_flat([w["mlstm_conv_w"]], F32, 8), vmem=True)
    conv_parts = _unpack_flat(conv_all, [conv_shape], lead=(N_DEV,))[0]
    conv_w = jnp.concatenate([conv_parts[j] for j in range(N_DEV)], axis=2)
    small = {n: w[n] for n in SMALL}

    shards, lands = [], []
    for l, grp in order:
        own, whole = _place_own("weights_place_" + grp, [w[n] for n in GATHER_GROUPS[grp]], l,
                                [lays[i] for i in group_idx[(l, grp)]])
        shards += own
        lands += whole
    n_peers = [NEAR_PEERS if (l, grp) in ((0, "ffn1_in"), (0, "ffn1_out"), (0, "mix")) else N_PEERS for l, grp in order]
    sems, shards, lands = _gather_start("weights_gather_start", shards, lands, lays, [group_idx[k] for k in order],
                                        n_peers, conv_all)

    def weights_of(l, grp, after):
        idx, g = group_idx[(l, grp)], order.index((l, grp))
        group_lays = [lays[i] for i in idx]
        whole = _gather_wait(f"weights_gather_wait_{l}_{grp}", sems[g], [shards[i] for i in idx],
                             [lands[i] for i in idx], group_lays, n_peers[g], after)
        if n_peers[g] == NEAR_PEERS:
            whole = _forward_to_sibling("weights_forward_" + grp, whole, group_lays)
        return dict(zip(GATHER_GROUPS[grp], whole))

    totals, pending = {}, []

    def finish(after):
        tag, names = pending[0][0], pending[0][1]
        for n, t in zip(names, _reduce_scatter_finish(pending.pop(0), after)):
            totals[(tag, n)] = t

    def on_grads(l, grp, g, after):
        if pending:
            finish(after)
        names = GROUPS[grp]
        pending.append(_reduce_scatter_start(f"grads_{l}_{grp}", names, [g[n] for n in names]))
        return pending[-1][3][0]

    loss, grad_x, small_g = _local_step(x, positions, loss_target, weights_of, small, conv_w, on_grads)
    finish(grad_x)
    grads, deltas, new_m, new_v = {}, {}, {}, {}
    for grp, names in GROUPS.items():
        for n in names:
            grads[n], deltas[n], new_m[n], new_v[n] = _adamw_layers(
                "adamw_" + n, w[n], [totals[(f"grads_{l}_{grp}", n)] for l in range(DEPTH)], m[n], v[n])

    n_small = DEPTH * len(SMALL)
    taps_at, loss_at, rows = n_small, n_small + 3 * DEPTH, 32
    pieces = [small_g[n, l] for n in SMALL for l in range(DEPTH)]
    pieces += [small_g["mlstm_conv_w", l] for l in range(DEPTH)] + [(loss, 0, 1)]
    small_all = _all_gather("small_all_gather", _pack_rows("small_pack", pieces, rows), vmem=True)
    small_sum = _sum_slots("small_sum", small_all, N_DEV)
    loss_total = small_sum[loss_at, 0]
    x_pos, y_pos, c_pos = _mesh_pos()
    grads["mlstm_conv_w"] = lax.dynamic_slice_in_dim(
        small_sum[taps_at:loss_at].reshape(DEPTH, 3, 2 * MLSTM_WIDTH),
        (4 * x_pos + 2 * y_pos + c_pos) * conv_shape[2], conv_shape[2], axis=2)

    n = "mlstm_conv_w"
    deltas[n], new_m[n], new_v[n] = _adamw("adamw_" + n, w[n], grads[n], m[n], v[n])
    sw, smm, sv = (_pack_rows("small_pack_" + tag, [(d[n], 0, DEPTH) for n in SMALL], rows)
                   for tag, d in (("w", w), ("m", m), ("v", v)))
    sd, snm, snv = _adamw("adamw_small", sw, small_sum, smm, sv)
    for i, n in enumerate(SMALL):
        rows_n, width = slice(DEPTH * i, DEPTH * (i + 1)), w[n].shape[1]
        grads[n], deltas[n], new_m[n], new_v[n] = (buf[rows_n, :width] for buf in (small_sum, sd, snm, snv))

    return (loss_total.reshape(()), grad_x, *[stored(n, d[n]) for d in (grads, deltas, new_m, new_v) for n in WEIGHTS])
```

```python
import functools

import numpy as np
import jax
import jax.numpy as jnp
from jax import lax
from jax.experimental import pallas as pl
from jax.experimental.pallas import tpu as pltpu

F32 = jnp.float32
BF16 = jnp.bfloat16

D_MODEL = 1024
D_FF = 2816
ATT_HEAD_DIM = 64
ATT_HEADS = 8
ATT_KV_HEADS = 2
ATT_GROUP = ATT_HEADS // ATT_KV_HEADS
ATT_WIDTH = ATT_HEADS * ATT_HEAD_DIM
ATT_KV_WIDTH = ATT_KV_HEADS * ATT_HEAD_DIM
WINDOW = 128
ATT_BLOCK = 128
ROPE_DIM = 16
ROPE_THETA = 500000.0
MLSTM_HEADS = 4
MLSTM_HEAD_DIM = 128
MLSTM_WIDTH = MLSTM_HEADS * MLSTM_HEAD_DIM
MLSTM_CHUNK = 128
MLSTM_N_GATES = 4 * MLSTM_HEADS
NORM_EPS = 1e-6
IN_WIDTH = 4880
DEPTH = 2
N_DEV = 8

ADAM_LR = 0.001
ADAM_B1 = 0.9
ADAM_B2 = 0.999
ADAM_EPS = 1e-08
ADAM_WD = 0.01
ADAM_STEP = 10

LANES = 128
C_GMERGE = 0
C_QK = 2048
C_VM = 3072
C_OM = 3584
C_QA = 4096
C_KA = 4608
C_VA = 4736
C_GATES = 4864
IN_PAD = 4992

VMEM_LIMIT = 48 * 1024 * 1024

MESH = pl.DeviceIdType.MESH


def _cparams(sem):
    return pltpu.CompilerParams(dimension_semantics=sem, vmem_limit_bytes=VMEM_LIMIT)


def _first_divisor(n, cands):
    for c in cands:
        if n % c == 0:
            return c
    return n


_NN = ((1,), (0,))
_NT = ((1,), (1,))
_TN = ((0,), (0,))


def _mm(a, b, dims):
    return lax.dot_general(a.astype(BF16), b.astype(BF16), (dims, ((), ())), preferred_element_type=F32)


@jax.custom_vjp
def mm_nn(a, b):
    return _mm(a, b, _NN)


def _mm_nn_fwd(a, b):
    return _mm(a, b, _NN), (a, b)


def _mm_nn_bwd(res, g):
    a, b = res
    return _mm(g, b, _NT).astype(a.dtype), _mm(a, g, _TN).astype(b.dtype)


mm_nn.defvjp(_mm_nn_fwd, _mm_nn_bwd)


@jax.custom_vjp
def mm_nt(a, b):
    return _mm(a, b, _NT)


def _mm_nt_fwd(a, b):
    return _mm(a, b, _NT), (a, b)


def _mm_nt_bwd(res, g):
    a, b = res
    return _mm(g, b, _NN).astype(a.dtype), _mm(g, a, _TN).astype(b.dtype)


mm_nt.defvjp(_mm_nt_fwd, _mm_nt_bwd)


@jax.custom_vjp
def mm_tn(a, b):
    return _mm(a, b, _TN)


def _mm_tn_fwd(a, b):
    return _mm(a, b, _TN), (a, b)


def _mm_tn_bwd(res, g):
    a, b = res
    return _mm(b, g, _NT).astype(a.dtype), _mm(a, g, _NN).astype(b.dtype)


mm_tn.defvjp(_mm_tn_fwd, _mm_tn_bwd)


def _matmul(name, a, b, mode, out_dtype=F32, res=None, scale=1.0, bl=None, dep=None, whole_k=False):
    b_shape = b.shape if bl is None else b.shape[1:]
    if mode == "nn":
        (M, K), (K2, N) = a.shape, b_shape
    elif mode == "nt":
        (M, K), (N, K2) = a.shape, b_shape
    else:
        (K, M), (K2, N) = a.shape, b_shape
    assert K == K2, (name, a.shape, b.shape)
    tm = _first_divisor(M, (1024, 512, 384, 256, 128))
    tn = _first_divisor(N, (1024, 1664, 512, 384, 256, 128))
    tk = K if whole_k else _first_divisor(K, (1024, 1664, 512, 256, 128))
    if whole_k:
        tn = min(tn, 512)
    nk = K // tk
    if mode == "tn":
        a_spec = pl.BlockSpec((tk, tm), lambda i, j, k: (k, i))
    else:
        a_spec = pl.BlockSpec((tm, tk), lambda i, j, k: (i, k))
    if mode == "nt":
        b_blk, b_idx = (tn, tk), (lambda i, j, k: (j, k))
    else:
        b_blk, b_idx = (tk, tn), (lambda i, j, k: (k, j))
    if bl is None:
        b_spec = pl.BlockSpec(b_blk, b_idx)
    else:
        b_spec = pl.BlockSpec((None,) + b_blk, lambda i, j, k: (bl,) + b_idx(i, j, k))
    o_spec = pl.BlockSpec((tm, tn), lambda i, j, k: (i, j))
    dims = {"nn": _NN, "nt": _NT, "tn": _TN}[mode]
    has_res = res is not None

    def body(*refs):
        a_ref, b_ref = refs[:2]
        r_ref = refs[2] if has_res else None

        def finish(out):
            if scale != 1.0:
                out = out * scale
            if has_res:
                out = r_ref[...].astype(F32) + out
            o_ref[...] = out.astype(out_dtype)

        if nk == 1:
            o_ref = refs[-1]
            finish(_mm(a_ref[...], b_ref[...], dims))
            return
        o_ref, acc = refs[-2:]
        k = pl.program_id(2)

        @pl.when(k == 0)
        def _():
            acc[...] = jnp.zeros_like(acc)

        acc[...] += _mm(a_ref[...], b_ref[...], dims)

        @pl.when(k == nk - 1)
        def _():
            finish(acc[...])

    in_specs = [a_spec, b_spec] + ([o_spec] if has_res else [])
    args = (a, b) + ((res,) if has_res else ())
    if dep is not None:
        in_specs.append(pl.BlockSpec(memory_space=pl.ANY))
        args += (dep,)
    return pl.pallas_call(
        body, name=name, grid=(M // tm, N // tn, nk), in_specs=in_specs, out_specs=o_spec,
        out_shape=jax.ShapeDtypeStruct((M, N), out_dtype),
        scratch_shapes=[pltpu.VMEM((tm, tn), F32)] if nk > 1 else [],
        compiler_params=_cparams(("parallel", "parallel", "arbitrary")),
    )(*args)


class _In:
    def __init__(self, arr, width=None, base=0, split=False, rows=True):
        self.arr, self.base, self.split, self.rows = arr, base, split, rows
        self.width = arr.shape[1] if width is None else width


class _Out:
    def __init__(self, cols, dtype=F32, width=None, split=False, rows=True, nrows=1, into=None, base=0):
        self.cols, self.dtype, self.split, self.rows, self.nrows = cols, dtype, split, rows, nrows
        self.width = cols if width is None else width
        self.into, self.base = into, base
        if into is not None:
            self.cols, self.dtype = into.shape[1], into.dtype


def _rowwise(name, fn, ins, outs, n_rows, br, ncol=1):
    br = min(br, n_rows)
    assert n_rows % br == 0, (name, n_rows, br)
    nrow_blocks = n_rows // br

    def in_spec(d):
        nb = br if d.rows else d.arr.shape[0]
        if d.rows and d.split:
            im = lambda j, i, base=d.base: (i, base + j)
        elif d.rows:
            im = lambda j, i, base=d.base: (i, base)
        elif d.split:
            im = lambda j, i, base=d.base: (0, base + j)
        else:
            im = lambda j, i, base=d.base: (0, base)
        return pl.BlockSpec((nb, d.width), im)

    def out_spec(d):
        nb = br if d.rows else d.nrows
        if d.rows and d.split:
            im = lambda j, i, base=d.base: (i, base + j)
        elif d.rows:
            im = lambda j, i, base=d.base: (i, base)
        elif d.split:
            im = lambda j, i: (0, j)
        else:
            im = lambda j, i: (0, 0)
        return pl.BlockSpec((nb, d.width), im)

    n_in = len(ins)
    targets = [(k, d.into) for k, d in enumerate(outs) if d.into is not None]

    def body(*refs):
        i = pl.program_id(1)
        vals = [r[...] for r in refs[:n_in]]
        res = fn(*vals)
        if not isinstance(res, (tuple, list)):
            res = (res,)
        for d, ref, val in zip(outs, refs[n_in + len(targets):], res):
            if d.rows:
                ref[...] = val.astype(d.dtype)
            else:
                @pl.when(i == 0)
                def _(ref=ref):
                    ref[...] = jnp.zeros_like(ref)

                ref[...] += val.astype(d.dtype)

    out_shape = [jax.ShapeDtypeStruct((n_rows if d.rows else d.nrows, d.cols), d.dtype) for d in outs]
    res = pl.pallas_call(
        body, name=name, grid=(ncol, nrow_blocks),
        in_specs=[in_spec(d) for d in ins] + [pl.BlockSpec(memory_space=pl.ANY)] * len(targets),
        out_specs=[out_spec(d) for d in outs], out_shape=out_shape,
        input_output_aliases={n_in + t: k for t, (k, _) in enumerate(targets)},
        compiler_params=_cparams(("parallel", "arbitrary")),
    )(*[d.arr for d in ins], *[arr for _, arr in targets])
    return res


def _rms(x, g):
    return x * lax.rsqrt(jnp.mean(x * x, axis=-1, keepdims=True) + NORM_EPS) * g


def _sigmoid(x):
    return 0.5 * jnp.tanh(0.5 * x) + 0.5


def _silu(x):
    return x * _sigmoid(x)


def _log_sigmoid(x):
    return jnp.minimum(x, 0.0) - jnp.log(1.0 + jnp.exp(-jnp.abs(x)))


def _rope_tables(pos, inv_freq_row):
    ang = pos.astype(F32) * inv_freq_row
    return jnp.cos(ang), jnp.sin(ang)


def _head_sums_impl(v):
    w = v.shape[-1]
    shift = ATT_HEAD_DIM.bit_length() - 1
    r = lax.shift_right_logical(lax.broadcasted_iota(jnp.int32, (w, w), 0), shift)
    c = lax.shift_right_logical(lax.broadcasted_iota(jnp.int32, (w, w), 1), shift)
    ones = (r == c).astype(BF16)
    hi = v.astype(BF16)
    lo = (v - hi.astype(F32)).astype(BF16)
    dn = (_NN, ((), ()))
    return (lax.dot_general(hi, ones, dn, preferred_element_type=F32)
            + lax.dot_general(lo, ones, dn, preferred_element_type=F32))


@jax.custom_vjp
def _head_sums(v):
    return _head_sums_impl(v)


_head_sums.defvjp(lambda v: (_head_sums_impl(v), None), lambda _, g: (_head_sums_impl(g),))


def _rotate_half_impl(y):
    w = y.shape[-1]
    half = ROPE_DIM // 2
    lane = lax.broadcasted_iota(jnp.int32, y.shape, 1) & (ATT_HEAD_DIM - 1)
    above = pltpu.roll(y, w - half, axis=1)
    below = pltpu.roll(y, half, axis=1)
    return jnp.where(lane < half, -above, jnp.where(lane < ROPE_DIM, below, 0.0))


@jax.custom_vjp
def _rotate_half(y):
    return _rotate_half_impl(y)


_rotate_half.defvjp(lambda y: (_rotate_half_impl(y), None), lambda _, g: (-_rotate_half_impl(g),))


def _qk_prep(t, g, cos, sin):
    reps = t.shape[-1] // cos.shape[-1]
    if reps > 1:
        cos, sin = jnp.tile(cos, (1, reps)), jnp.tile(sin, (1, reps))
    y = t * lax.rsqrt(_head_sums(t * t) * (1.0 / ATT_HEAD_DIM) + NORM_EPS) * g
    return y * cos + _rotate_half(y) * sin


def _attn_head(q, kb, vb, sink, valid):
    s = mm_nt(q, kb) * (ATT_HEAD_DIM ** -0.5)
    s = jnp.where(valid, s, -jnp.inf)
    m = jnp.maximum(jnp.max(s, axis=-1, keepdims=True), sink)
    p = jnp.exp(s - m)
    den = jnp.sum(p, axis=-1, keepdims=True) + jnp.exp(sink - m)
    return mm_nn(p * (1.0 / den), vb)


def _mlstm_chunk(q, k, v, li, lf, C, n, m, incl, incl_t, eye):
    k = k * (MLSTM_HEAD_DIM ** -0.5)
    lf_row = jnp.sum(eye * lf, axis=0, keepdims=True)
    li_row = jnp.sum(eye * li, axis=0, keepdims=True)
    b = jnp.sum(incl * lf_row, axis=1, keepdims=True)
    b_row = jnp.sum(incl_t * lf, axis=0, keepdims=True)
    b_tot = jnp.sum(lf, axis=0, keepdims=True)
    a = b_tot - b + li
    a_max = jnp.max(a, axis=0, keepdims=True)
    kw = k * jnp.exp(a - a_max)
    c_loc = mm_tn(kw, v)
    n_loc = jnp.sum(kw, axis=0, keepdims=True)

    dmat = jnp.where(incl > 0.5, b - b_row + li_row, -jnp.inf)
    inter = b + m
    m_t = jnp.maximum(inter, jnp.max(dmat, axis=1, keepdims=True))
    sc = mm_nt(q, k) * jnp.exp(dmat - m_t)
    scale_in = jnp.exp(inter - m_t)
    num = mm_nn(sc, v) + scale_in * mm_nn(q, C)
    den = jnp.sum(sc, axis=1, keepdims=True) + scale_in * jnp.sum(q * n, axis=1, keepdims=True)
    h = num * (1.0 / jnp.maximum(jnp.abs(den), jnp.exp(-m_t)))

    m_new = jnp.maximum(b_tot + m, a_max)
    s_p = jnp.exp(b_tot + m - m_new)
    s_l = jnp.exp(a_max - m_new)
    return h, s_p * C + s_l * c_loc, s_p * n + s_l * n_loc, m_new


def _mlstm_combine(hf, hb, o_pre, g):
    h = hf + hb
    mu = jnp.mean(h, axis=-1, keepdims=True)
    var = jnp.mean(jnp.square(h - mu), axis=-1, keepdims=True)
    return _sigmoid(o_pre) * ((h - mu) * lax.rsqrt(var + NORM_EPS) * g)


def _merge(ga, gm, za, zm):
    return _sigmoid(ga) * za + _sigmoid(gm) * zm


def _attn_mask(n, seq):
    shape = (ATT_GROUP * ATT_BLOCK, 3 * ATT_BLOCK)
    qi = n * ATT_BLOCK + (lax.broadcasted_iota(jnp.int32, shape, 0) & (ATT_BLOCK - 1))
    kj = (n - 1) * ATT_BLOCK + lax.broadcasted_iota(jnp.int32, shape, 1)
    return (jnp.abs(qi - kj) <= WINDOW) & (kj >= 0) & (kj < seq)


def _attn_specs(nq, v_base):
    q_spec = pl.BlockSpec((1, ATT_BLOCK, ATT_WIDTH), lambda b, n: (b, n, 0))

    def kv_spec(off, base=0):
        return pl.BlockSpec((1, ATT_BLOCK, ATT_KV_WIDTH), lambda b, n: (b, jnp.clip(n + off, 0, nq - 1), base))

    sink_spec = pl.BlockSpec((ATT_KV_HEADS, ATT_GROUP, 1, 1), lambda b, n: (0, 0, 0, 0))
    specs = [q_spec, kv_spec(-1), kv_spec(0), kv_spec(1), kv_spec(-1, v_base), kv_spec(0, v_base), kv_spec(1, v_base), sink_spec]
    return q_spec, specs, sink_spec


def _head(h):
    return slice(h * ATT_HEAD_DIM, (h + 1) * ATT_HEAD_DIM)


def _group_rows(q_ref, s_ref, h):
    q4 = jnp.concatenate([q_ref[0, :, _head(h * ATT_GROUP + g)] for g in range(ATT_GROUP)], axis=0)
    sink4 = jnp.concatenate([jnp.broadcast_to(s_ref[h, g], (ATT_BLOCK, 1)) for g in range(ATT_GROUP)], axis=0)
    return q4, sink4


def _attn_fwd(q, k, proj3, sink):
    B, S, _ = q.shape
    nq = S // ATT_BLOCK
    q_spec, specs, _ = _attn_specs(nq, C_VA // ATT_KV_WIDTH)

    def body(q_ref, kp, kc, kn, vp, vc, vn, s_ref, o_ref):
        valid = _attn_mask(pl.program_id(1), S)
        for h in range(ATT_KV_HEADS):
            kb = jnp.concatenate([kp[0, :, _head(h)], kc[0, :, _head(h)], kn[0, :, _head(h)]], axis=0)
            vb = jnp.concatenate([vp[0, :, _head(h)], vc[0, :, _head(h)], vn[0, :, _head(h)]], axis=0)
            q4, sink4 = _group_rows(q_ref, s_ref, h)
            o4 = _attn_head(q4, kb, vb, sink4, valid).astype(BF16)
            for g in range(ATT_GROUP):
                o_ref[0, :, _head(h * ATT_GROUP + g)] = o4[g * ATT_BLOCK:(g + 1) * ATT_BLOCK]

    return pl.pallas_call(
        body, name="attn_fwd", grid=(B, nq), in_specs=specs,
        out_specs=q_spec, out_shape=jax.ShapeDtypeStruct(q.shape, BF16),
        compiler_params=_cparams(("parallel", "arbitrary")),
    )(q, k, k, k, proj3, proj3, proj3, sink)


def _attn_bwd(q, k, proj3, sink, dy):
    B, S, _ = q.shape
    nq = S // ATT_BLOCK
    q_spec, specs, sink_spec = _attn_specs(nq, C_VA // ATT_KV_WIDTH)
    kv_full = pl.BlockSpec((1, S, ATT_KV_WIDTH), lambda b, n: (b, 0, 0))

    def body(q_ref, kp, kc, kn, vp, vc, vn, s_ref, dy_ref, dq_ref, dk_ref, dv_ref, ds_ref):
        b, n = pl.program_id(0), pl.program_id(1)
        valid = _attn_mask(n, S)

        @pl.when(n == 0)
        def _():
            dk_ref[...] = jnp.zeros_like(dk_ref)
            dv_ref[...] = jnp.zeros_like(dv_ref)

        @pl.when((n == 0) & (b == 0))
        def _():
            ds_ref[...] = jnp.zeros_like(ds_ref)

        for h in range(ATT_KV_HEADS):
            kb = jnp.concatenate([kp[0, :, _head(h)], kc[0, :, _head(h)], kn[0, :, _head(h)]], axis=0)
            vb = jnp.concatenate([vp[0, :, _head(h)], vc[0, :, _head(h)], vn[0, :, _head(h)]], axis=0)
            q4, sink4 = _group_rows(q_ref, s_ref, h)
            dy4 = jnp.concatenate([dy_ref[0, :, _head(h * ATT_GROUP + g)] for g in range(ATT_GROUP)], axis=0)
            _, vjp = jax.vjp(functools.partial(_attn_head, valid=valid), q4, kb, vb, sink4)
            dq4, dkb, dvb, dsink4 = vjp(dy4)
            for g in range(ATT_GROUP):
                rows = slice(g * ATT_BLOCK, (g + 1) * ATT_BLOCK)
                dq_ref[0, :, _head(h * ATT_GROUP + g)] = dq4[rows]
                ds_ref[h, g] += jnp.sum(dsink4[rows], axis=0, keepdims=True)
            for j, off in enumerate((-1, 0, 1)):
                start = pl.multiple_of(jnp.clip(n + off, 0, nq - 1) * ATT_BLOCK, ATT_BLOCK)
                rows = pl.ds(start, ATT_BLOCK)
                dk_ref[0, rows, _head(h)] += dkb[j * ATT_BLOCK:(j + 1) * ATT_BLOCK]
                dv_ref[0, rows, _head(h)] += dvb[j * ATT_BLOCK:(j + 1) * ATT_BLOCK]

    kv_shape = jax.ShapeDtypeStruct(k.shape, F32)
    return pl.pallas_call(
        body, name="attn_bwd", grid=(B, nq), in_specs=specs + [q_spec],
        out_specs=[q_spec, kv_full, kv_full, sink_spec],
        out_shape=[jax.ShapeDtypeStruct(q.shape, F32), kv_shape, kv_shape, jax.ShapeDtypeStruct(sink.shape, F32)],
        compiler_params=_cparams(("arbitrary", "arbitrary")),
    )(q, k, k, k, proj3, proj3, proj3, sink, dy)


CONV_COLS = 256


def _conv_taps(u, seq):
    row = lax.broadcasted_iota(jnp.int32, u.shape, 0)
    prev = jnp.where(row == 0, 0.0, pltpu.roll(u, 1, axis=0))
    nxt = jnp.where(row == seq - 1, 0.0, pltpu.roll(u, seq - 1, axis=0))
    return prev, nxt


def _conv_fwd(proj3, w8):
    B, S, _ = proj3.shape
    ncb = 2 * MLSTM_WIDTH // CONV_COLS

    def body(u_ref, w_ref, o_ref):
        u = u_ref[0]
        prev, nxt = _conv_taps(u, S)
        o_ref[0] = _silu(prev * w_ref[0:1, :] + u * w_ref[1:2, :] + nxt * w_ref[2:3, :] + w_ref[3:4, :])

    return pl.pallas_call(
        body, name="conv_fwd", grid=(B, ncb),
        in_specs=[pl.BlockSpec((1, S, CONV_COLS), lambda b, c: (b, 0, C_QK // CONV_COLS + c)),
                  pl.BlockSpec((8, CONV_COLS), lambda b, c: (0, c))],
        out_specs=pl.BlockSpec((1, S, CONV_COLS), lambda b, c: (b, 0, c)),
        out_shape=jax.ShapeDtypeStruct((B, S, 2 * MLSTM_WIDTH), F32),
        compiler_params=_cparams(("parallel", "parallel")),
    )(proj3, w8)


def _conv_bwd(proj3, w8, dout_f, dout_b):
    B, S, _ = proj3.shape
    ncb = 2 * MLSTM_WIDTH // CONV_COLS

    def body(u_ref, w_ref, df_ref, db_ref, du_ref, dw_ref):
        b = pl.program_id(1)
        u = u_ref[0]
        prev, nxt = _conv_taps(u, S)
        w0, w1, w2 = w_ref[0:1, :], w_ref[1:2, :], w_ref[2:3, :]
        pre = prev * w0 + u * w1 + nxt * w2 + w_ref[3:4, :]
        sig = _sigmoid(pre)
        dpre = (df_ref[0] + db_ref[0]) * (sig * (1.0 + pre * (1.0 - sig)))
        dprev, dnxt = _conv_taps(dpre, S)
        du_ref[0] = (dnxt * w0 + dpre * w1 + dprev * w2).astype(BF16)

        @pl.when(b == 0)
        def _():
            dw_ref[...] = jnp.zeros_like(dw_ref)

        dw_ref[0:1, :] += jnp.sum(dpre * prev, axis=0, keepdims=True)
        dw_ref[1:2, :] += jnp.sum(dpre * u, axis=0, keepdims=True)
        dw_ref[2:3, :] += jnp.sum(dpre * nxt, axis=0, keepdims=True)
        dw_ref[3:4, :] += jnp.sum(dpre, axis=0, keepdims=True)

    blk = pl.BlockSpec((1, S, CONV_COLS), lambda c, b: (b, 0, c))
    return pl.pallas_call(
        body, name="conv_bwd", grid=(ncb, B),
        in_specs=[pl.BlockSpec((1, S, CONV_COLS), lambda c, b: (b, 0, C_QK // CONV_COLS + c)),
                  pl.BlockSpec((8, CONV_COLS), lambda c, b: (0, c)), blk, blk],
        out_specs=[blk, pl.BlockSpec((8, CONV_COLS), lambda c, b: (0, c))],
        out_shape=[jax.ShapeDtypeStruct((B, S, 2 * MLSTM_WIDTH), BF16), jax.ShapeDtypeStruct((8, 2 * MLSTM_WIDTH), F32)],
        compiler_params=_cparams(("parallel", "arbitrary")),
    )(proj3, w8, dout_f, dout_b)


MLSTM_HEADS_PER_STEP = 4


def _chunk_masks(direction):
    t = lax.broadcasted_iota(jnp.int32, (MLSTM_CHUNK, MLSTM_CHUNK), 0)
    s = lax.broadcasted_iota(jnp.int32, (MLSTM_CHUNK, MLSTM_CHUNK), 1)
    le, ge = (s <= t).astype(F32), (s >= t).astype(F32)
    eye = (s == t).astype(F32)
    return (le, ge, eye) if direction == 0 else (ge, le, eye)


def _gate_cols(gates, direction, head):
    lane = lax.broadcasted_iota(jnp.int32, gates.shape, 1)
    sel_i = (lane == (2 * direction) * MLSTM_HEADS + head).astype(F32)
    sel_f = (lane == (2 * direction + 1) * MLSTM_HEADS + head).astype(F32)
    return sel_i, sel_f


def _mlstm_fwd(qk, proj3, bias):
    B, S, _ = qk.shape
    nc = S // MLSTM_CHUNK
    H, L, DH = MLSTM_HEADS, MLSTM_CHUNK, MLSTM_HEAD_DIM

    def chunk_of(d, c):
        return c if d == 0 else nc - 1 - c

    HS = MLSTM_HEADS_PER_STEP

    def body(qkf, qkb, vf, vb, gf, gb, bias_ref, hf, hb, csf, csb, nsf, nsb, msf, msb, c_st, n_st, m_st):
        c, hg = pl.program_id(1), pl.program_id(2)

        @pl.when(c == 0)
        def _():
            for d in range(2):
                for j in range(HS):
                    c_st[d, hg * HS + j] = jnp.zeros((DH, DH), F32)
                    n_st[d, hg * HS + j] = jnp.zeros((1, DH), F32)
                    m_st[d, hg * HS + j] = jnp.zeros((1, DH), F32)

        for d, (qk_ref, v_ref, g_ref, h_ref, cs, ns, ms) in enumerate(
                ((qkf, vf, gf, hf, csf, nsf, msf), (qkb, vb, gb, hb, csb, nsb, msb))):
            incl, incl_t, eye = _chunk_masks(d)
            gates = g_ref[0] + bias_ref[...]
            log_f = _log_sigmoid(gates)
            for j in range(HS):
                h = hg * HS + j
                sel_i, sel_f = _gate_cols(gates, d, h)
                li = jnp.sum(gates * sel_i, axis=1, keepdims=True)
                lf = jnp.sum(log_f * sel_f, axis=1, keepdims=True)
                c_in, n_in, m_in = c_st[d, h], n_st[d, h], m_st[d, h]
                cs[0, 0, j], ns[0, 0, j], ms[0, 0, j] = c_in, n_in, m_in
                hh, c_new, n_new, m_new = _mlstm_chunk(
                    qk_ref[0, :, 2 * j * DH:(2 * j + 1) * DH], qk_ref[0, :, (2 * j + 1) * DH:(2 * j + 2) * DH],
                    v_ref[0, :, j * DH:(j + 1) * DH], li, lf, c_in, n_in,
                    jnp.max(m_in, axis=1, keepdims=True), incl, incl_t, eye)
                h_ref[0, :, j * DH:(j + 1) * DH] = hh
                c_st[d, h], n_st[d, h] = c_new, n_new
                m_st[d, h] = jnp.broadcast_to(m_new, (1, DH))

    def tok_spec(width, base, d, per_head):
        return pl.BlockSpec((1, L, width), lambda b, c, h: (b, chunk_of(d, c), base + (h if per_head else 0)))

    def st_spec(shape, d):
        return pl.BlockSpec((1, 1, HS) + shape, lambda b, c, h: (b, chunk_of(d, c), h, 0, 0))

    in_specs = [tok_spec(2 * HS * DH, 0, 0, True), tok_spec(2 * HS * DH, 0, 1, True),
                tok_spec(HS * DH, C_VM // (HS * DH), 0, True), tok_spec(HS * DH, C_VM // (HS * DH), 1, True),
                tok_spec(LANES, C_GATES // LANES, 0, False), tok_spec(LANES, C_GATES // LANES, 1, False),
                pl.BlockSpec((1, LANES), lambda b, c, h: (0, 0))]
    out_specs = [tok_spec(HS * DH, 0, 0, True), tok_spec(HS * DH, 0, 1, True),
                 st_spec((DH, DH), 0), st_spec((DH, DH), 1), st_spec((1, DH), 0), st_spec((1, DH), 1),
                 st_spec((1, DH), 0), st_spec((1, DH), 1)]
    hs = jax.ShapeDtypeStruct((B, S, H * DH), F32)
    cs = jax.ShapeDtypeStruct((B, nc, H, DH, DH), F32)
    vs = jax.ShapeDtypeStruct((B, nc, H, 1, DH), F32)
    return pl.pallas_call(
        body, name="mlstm_fwd", grid=(B, nc, H // HS), in_specs=in_specs, out_specs=out_specs,
        out_shape=[hs, hs, cs, cs, vs, vs, vs, vs],
        scratch_shapes=[pltpu.VMEM((2, H, DH, DH), F32), pltpu.VMEM((2, H, 1, DH), F32), pltpu.VMEM((2, H, 1, DH), F32)],
        compiler_params=_cparams(("parallel", "arbitrary", "arbitrary")),
    )(qk, qk, proj3, proj3, proj3, proj3, bias)


def _mlstm_bwd(qk, proj3, bias, states, dh):
    B, S, _ = qk.shape
    nc = S // MLSTM_CHUNK
    H, L, DH = MLSTM_HEADS, MLSTM_CHUNK, MLSTM_HEAD_DIM

    def chunk_of(d, c):
        return nc - 1 - c if d == 0 else c

    HS = MLSTM_HEADS_PER_STEP

    def body(qkf, qkb, vf, vb, gf, gb, bias_ref, csf, csb, nsf, nsb, msf, msb, dhf, dhb,
             dqkf, dqkb, dvf, dvb, dgf, dgb, dc_st, dn_st, dm_st):
        c, hg = pl.program_id(1), pl.program_id(2)

        @pl.when(c == 0)
        def _():
            for d in range(2):
                for j in range(HS):
                    dc_st[d, hg * HS + j] = jnp.zeros((DH, DH), F32)
                    dn_st[d, hg * HS + j] = jnp.zeros((1, DH), F32)
                    dm_st[d, hg * HS + j] = jnp.zeros((1, DH), F32)

        @pl.when(hg == 0)
        def _():
            dgf[...] = jnp.zeros_like(dgf)
            dgb[...] = jnp.zeros_like(dgb)

        for d, (qk_ref, v_ref, g_ref, cs, ns, ms, dh_ref, dqk_ref, dv_ref, dg_ref) in enumerate(
                ((qkf, vf, gf, csf, nsf, msf, dhf, dqkf, dvf, dgf), (qkb, vb, gb, csb, nsb, msb, dhb, dqkb, dvb, dgb))):
            incl, incl_t, eye = _chunk_masks(d)
            gates = g_ref[0] + bias_ref[...]
            log_f = _log_sigmoid(gates)
            d_li = jnp.zeros_like(gates)
            d_lf = jnp.zeros_like(gates)
            for j in range(HS):
                h = hg * HS + j
                sel_i, sel_f = _gate_cols(gates, d, h)
                li = jnp.sum(gates * sel_i, axis=1, keepdims=True)
                lf = jnp.sum(log_f * sel_f, axis=1, keepdims=True)
                m_in = jnp.max(ms[0, 0, j], axis=1, keepdims=True)
                _, vjp = jax.vjp(
                    functools.partial(_mlstm_chunk, incl=incl, incl_t=incl_t, eye=eye),
                    qk_ref[0, :, 2 * j * DH:(2 * j + 1) * DH], qk_ref[0, :, (2 * j + 1) * DH:(2 * j + 2) * DH],
                    v_ref[0, :, j * DH:(j + 1) * DH], li, lf, cs[0, 0, j], ns[0, 0, j], m_in)
                dm_out = jnp.max(dm_st[d, h], axis=1, keepdims=True)
                dq, dk, dv, dli, dlf, dc, dn, dm = vjp((dh_ref[0, :, j * DH:(j + 1) * DH], dc_st[d, h], dn_st[d, h], dm_out))
                dqk_ref[0, :, 2 * j * DH:(2 * j + 1) * DH] = dq
                dqk_ref[0, :, (2 * j + 1) * DH:(2 * j + 2) * DH] = dk
                dv_ref[0, :, j * DH:(j + 1) * DH] = dv
                d_li += dli * sel_i
                d_lf += dlf * sel_f
                dc_st[d, h], dn_st[d, h] = dc, dn
                dm_st[d, h] = jnp.broadcast_to(dm, (1, DH))
            dg_ref[0] += d_li + d_lf * _sigmoid(-gates)

    def tok_spec(width, base, d, per_head):
        return pl.BlockSpec((1, L, width), lambda b, c, h: (b, chunk_of(d, c), base + (h if per_head else 0)))

    def st_spec(shape, d):
        return pl.BlockSpec((1, 1, HS) + shape, lambda b, c, h: (b, chunk_of(d, c), h, 0, 0))

    in_specs = [tok_spec(2 * HS * DH, 0, 0, True), tok_spec(2 * HS * DH, 0, 1, True),
                tok_spec(HS * DH, C_VM // (HS * DH), 0, True), tok_spec(HS * DH, C_VM // (HS * DH), 1, True),
                tok_spec(LANES, C_GATES // LANES, 0, False), tok_spec(LANES, C_GATES // LANES, 1, False),
                pl.BlockSpec((1, LANES), lambda b, c, h: (0, 0)),
                st_spec((DH, DH), 0), st_spec((DH, DH), 1), st_spec((1, DH), 0), st_spec((1, DH), 1),
                st_spec((1, DH), 0), st_spec((1, DH), 1), tok_spec(HS * DH, 0, 0, True), tok_spec(HS * DH, 0, 1, True)]
    out_specs = [tok_spec(2 * HS * DH, 0, 0, True), tok_spec(2 * HS * DH, 0, 1, True),
                 tok_spec(HS * DH, 0, 0, True), tok_spec(HS * DH, 0, 1, True),
                 tok_spec(LANES, 0, 0, False), tok_spec(LANES, 0, 1, False)]
    qks = jax.ShapeDtypeStruct((B, S, 2 * H * DH), F32)
    vs = jax.ShapeDtypeStruct((B, S, H * DH), F32)
    gs = jax.ShapeDtypeStruct((B, S, LANES), F32)
    csf, csb, nsf, nsb, msf, msb = states
    return pl.pallas_call(
        body, name="mlstm_bwd", grid=(B, nc, H // HS), in_specs=in_specs, out_specs=out_specs,
        out_shape=[qks, qks, vs, vs, gs, gs],
        scratch_shapes=[pltpu.VMEM((2, H, DH, DH), F32), pltpu.VMEM((2, H, 1, DH), F32), pltpu.VMEM((2, H, 1, DH), F32)],
        compiler_params=_cparams(("parallel", "arbitrary", "arbitrary")),
    )(qk, qk, proj3, proj3, proj3, proj3, bias, csf, csb, nsf, nsb, msf, msb, dh, dh)


ROW_BLOCK = 256
FF_COLS = 512
FF_SHARD = D_FF // N_DEV
FF_SHARD_PAD = 384
FF_PAD = N_DEV * FF_SHARD_PAD


def _rms_fwd(name, x, g):
    T = x.shape[0]
    return _rowwise(name, lambda xv, gv: _rms(xv, gv), [_In(x), _In(g, rows=False)], [_Out(D_MODEL, BF16)], T, ROW_BLOCK)[0]


def _rms_bwd(name, x, g, dh, dres):
    T = x.shape[0]

    def fn(xv, gv, dhv, drv):
        _, vjp = jax.vjp(_rms, xv, gv)
        dx, dg = vjp(dhv)
        return drv + dx, dg

    return _rowwise(name, fn, [_In(x), _In(g, rows=False), _In(dh), _In(dres)],
                    [_Out(D_MODEL), _Out(D_MODEL, rows=False)], T, ROW_BLOCK)


def _mmw(name, a, w, mode, **kw):
    if isinstance(w, tuple):
        return _matmul(name, a, w[0], mode, bl=w[1], **kw)
    return _matmul(name, a, w, mode, **kw)


def _swiglu(gate, up):
    return _silu(gate) * up


def _ffn_in(name, x, gain, wg, wu):
    (M, K), N = x.shape, wg.shape[0]
    tm, tn = _first_divisor(M, (1024, 512, 256, 128)), FF_COLS

    def body(x_ref, gain_ref, wg_ref, wu_ref, h_ref, g_ref, u_ref, a_ref):
        @pl.when(pl.program_id(1) == 0)
        def _():
            h_ref[...] = _rms(x_ref[...], gain_ref[...]).astype(BF16)

        hv = h_ref[...]
        gate = _mm(hv, wg_ref[...], _NT)
        up = _mm(hv, wu_ref[...], _NT)
        g_ref[...], u_ref[...] = gate.astype(BF16), up.astype(BF16)
        a_ref[...] = _swiglu(gate, up).astype(BF16)

    row_spec = pl.BlockSpec((tm, K), lambda i, j: (i, 0))
    w_spec = pl.BlockSpec((tn, K), lambda i, j: (j, 0))
    o_spec = pl.BlockSpec((tm, tn), lambda i, j: (i, j))
    return pl.pallas_call(
        body, name=name, grid=(M // tm, N // tn),
        in_specs=[row_spec, pl.BlockSpec((1, K), lambda i, j: (0, 0)), w_spec, w_spec],
        out_specs=[row_spec, o_spec, o_spec, o_spec],
        out_shape=[jax.ShapeDtypeStruct((M, K), BF16)] + [jax.ShapeDtypeStruct((M, N), BF16)] * 3,
        compiler_params=_cparams(("parallel", "arbitrary")),
    )(x, gain, wg, wu)


def _ffn_dact(name, dx, wd, gate, up):
    (M, K), N = dx.shape, wd.shape[0]
    tm, tn = _first_divisor(M, (1024, 512, 256, 128)), FF_COLS

    def body(dx_ref, wd_ref, g_ref, u_ref, dg_ref, du_ref):
        dact = _mm(dx_ref[...], wd_ref[...], _NT) * 0.5
        gate, up = g_ref[...].astype(F32), u_ref[...].astype(F32)
        s = _sigmoid(gate)
        silu = gate * s
        dg_ref[...] = (dact * up * (s + silu * (1.0 - s))).astype(BF16)
        du_ref[...] = (dact * silu).astype(BF16)

    o_spec = pl.BlockSpec((tm, tn), lambda i, j: (i, j))
    return pl.pallas_call(
        body, name=name, grid=(M // tm, N // tn),
        in_specs=[pl.BlockSpec((tm, K), lambda i, j: (i, 0)), pl.BlockSpec((tn, K), lambda i, j: (j, 0)), o_spec, o_spec],
        out_specs=[o_spec, o_spec],
        out_shape=[jax.ShapeDtypeStruct((M, N), BF16), jax.ShapeDtypeStruct((M, N), BF16)],
        compiler_params=_cparams(("parallel", "parallel")),
    )(dx, wd, gate, up)


def _ffn_dh(name, dgate, dup, wg, wu, dep, x, gain, dres):
    (M, K), N = dgate.shape, wg.shape[1]
    tm, tk = _first_divisor(M, (512, 256, 128)), _first_divisor(K, (1024, 512, 384, 256, 128))
    nk = K // tk

    def body(dg_ref, du_ref, wg_ref, wu_ref, x_ref, gain_ref, dres_ref, dep_ref, o_ref, dgain_ref, acc):
        i, k = pl.program_id(0), pl.program_id(1)

        @pl.when(k == 0)
        def _():
            acc[...] = jnp.zeros_like(acc)

        acc[...] += _mm(dg_ref[...], wg_ref[...], _NN) + _mm(du_ref[...], wu_ref[...], _NN)

        @pl.when((k == nk - 1) & (i == 0))
        def _():
            dgain_ref[...] = jnp.zeros_like(dgain_ref)

        @pl.when(k == nk - 1)
        def _():
            _, vjp = jax.vjp(_rms, x_ref[...], gain_ref[...])
            dx, dgain = vjp(acc[...])
            o_ref[...] = dres_ref[...] + dx
            dgain_ref[...] += dgain

    a_spec = pl.BlockSpec((tm, tk), lambda i, k: (i, k))
    w_spec = pl.BlockSpec((tk, N), lambda i, k: (k, 0))
    row_spec = pl.BlockSpec((tm, N), lambda i, k: (i, 0))
    gain_spec = pl.BlockSpec((1, N), lambda i, k: (0, 0))
    return pl.pallas_call(
        body, name=name, grid=(M // tm, nk),
        in_specs=[a_spec, a_spec, w_spec, w_spec, row_spec, gain_spec, row_spec, pl.BlockSpec(memory_space=pl.ANY)],
        out_specs=[row_spec, gain_spec],
        out_shape=[jax.ShapeDtypeStruct((M, N), F32), jax.ShapeDtypeStruct((1, N), F32)],
        scratch_shapes=[pltpu.VMEM((tm, N), F32)], compiler_params=_cparams(("arbitrary", "arbitrary")),
    )(dgate, dup, wg, wu, x, gain, dres, dep)


def _ffn_fwd(tag, x, g, wg, wu, wd):
    h, gate, up, act = _ffn_in(tag + "_in", x, g, wg, wu)
    if callable(wd):
        wd = wd(act)
    out = _mmw(tag + "_down", act, wd, "nn", res=x, scale=0.5, whole_k=True)
    return out, (x, h, gate, up, act), wd


def _ffn_bwd(tag, saved, g, wg, wu, wd, dx, on_dw):
    x, h, gate, up, act = saved
    dgate, dup = _ffn_dact(tag + "_dact", dx, wd, gate, up)
    dwd = _matmul(tag + "_dwd", act, dx, "tn", scale=0.5, out_dtype=BF16)
    dwg = _matmul(tag + "_dwg", dgate, h, "tn", out_dtype=BF16, whole_k=True)
    dwu = _matmul(tag + "_dwu", dup, h, "tn", out_dtype=BF16, whole_k=True)
    token = on_dw({tag + "_w_gate": dwg, tag + "_w_up": dwu, tag + "_w_down": dwd}, dwu)
    return _ffn_dh(tag + "_dh", dgate, dup, wg, wu, token, x, g, dx)


def _rope_cos_sin(positions):
    half = ROPE_DIM // 2
    inv_freq = jnp.power(jnp.float32(ROPE_THETA), -jnp.arange(half, dtype=F32) * (2.0 / ROPE_DIM))
    head = jnp.zeros((ATT_HEAD_DIM,), F32).at[:ROPE_DIM].set(jnp.concatenate([inv_freq, inv_freq]))
    row = jnp.tile(head, LANES // ATT_HEAD_DIM)[None, :]
    T = positions.shape[0]
    return _rowwise("rope_tables", _rope_tables, [_In(positions), _In(row, rows=False)], [_Out(LANES), _Out(LANES)], T, 1024)


def _prep_fwd(name, src, width, base, g, cos, sin):
    return _rowwise(name, _qk_prep, [_In(src, width, base), _In(g, rows=False), _In(cos), _In(sin)],
                    [_Out(width)], src.shape[0], 512)[0]


def _prep_bwd(name, src, width, base, g, cos, sin, dout, into=None):
    def fn(tv, gv, cv, sv, dv):
        _, vjp = jax.vjp(lambda a, b: _qk_prep(a, b, cv, sv), tv, gv)
        return vjp(dv)

    dsrc = _Out(width, BF16) if into is None else _Out(0, width=width, into=into, base=base)
    return _rowwise(name, fn, [_In(src, width, base), _In(g, rows=False), _In(cos), _In(sin), _In(dout)],
                    [dsrc, _Out(width, rows=False)], src.shape[0], 512)


def _mix_fwd(x, cos, sin, B, S, p):
    T = B * S
    h = _rms_fwd("mix_norm", x, p["mix_norm"])
    proj = _matmul("mix_proj", h, p["w_in"], "nn")
    proj3 = proj.reshape(B, S, IN_PAD)
    q_gain = jnp.tile(p["attn_q_norm"], (1, ATT_HEADS))
    k_gain = jnp.tile(p["attn_k_norm"], (1, ATT_KV_HEADS))
    q_r = _prep_fwd("q_prep", proj, ATT_WIDTH, C_QA // ATT_WIDTH, q_gain, cos, sin)
    k_r = _prep_fwd("k_prep", proj, ATT_KV_WIDTH, C_KA // ATT_KV_WIDTH, k_gain, cos, sin)
    qh = q_r.reshape(B, S, ATT_WIDTH)
    kh = k_r.reshape(B, S, ATT_KV_WIDTH)
    sink = p["attn_sink"].reshape(ATT_KV_HEADS, ATT_GROUP, 1, 1)
    y_a = _attn_fwd(qh, kh, proj3, sink).reshape(T, ATT_WIDTH)

    qk_c = _conv_fwd(proj3, p["conv_w8"])
    hf, hb, *states = _mlstm_fwd(qk_c, proj3, p["gate_bias"])
    hf2, hb2 = hf.reshape(T, MLSTM_WIDTH), hb.reshape(T, MLSTM_WIDTH)
    DH = MLSTM_HEAD_DIM
    y_m = _rowwise("mlstm_out", _mlstm_combine,
                   [_In(hf2, DH, split=True), _In(hb2, DH, split=True), _In(proj, DH, C_OM // DH, split=True),
                    _In(p["mlstm_out_norm"], DH, split=True, rows=False)],
                   [_Out(MLSTM_WIDTH, BF16, DH, split=True)], T, 1024, ncol=MLSTM_HEADS)[0]

    za = _mmw("branch_a", y_a, p["w_branch_attn"], "nn")
    zm = _mmw("branch_m", y_m, p["w_branch_mlstm"], "nn")
    W = 512
    merged = _rowwise("merge", _merge,
                      [_In(proj, W, C_GMERGE // W, split=True), _In(proj, W, (C_GMERGE + D_MODEL) // W, split=True),
                       _In(za, W, split=True), _In(zm, W, split=True)],
                      [_Out(D_MODEL, BF16, W, split=True)], T, 512, ncol=D_MODEL // W)[0]
    out = _mmw("mix_out", merged, p["w_out"], "nn", res=x)
    saved = dict(x=x, h=h, proj=proj, q_gain=q_gain, k_gain=k_gain, qh=qh, kh=kh, sink=sink, y_a=y_a, qk_c=qk_c,
                 hf=hf2, hb=hb2, states=states, y_m=y_m, za=za, zm=zm, merged=merged)
    return out, saved


def _mix_bwd(sv, cos, sin, B, S, p, dx, on_dw):
    T = B * S
    DH = MLSTM_HEAD_DIM
    proj = sv["proj"]
    proj3 = proj.reshape(B, S, IN_PAD)
    g = {}
    dmerged = _mmw("mix_dmerged", dx, p["w_out"], "nt")
    g["w_out"] = _matmul("mix_dwout", sv["merged"], dx, "tn", out_dtype=BF16)
    dproj = lax.empty((T, IN_PAD), BF16)

    def merge_bwd(ga, gm, za, zm, dm):
        _, vjp = jax.vjp(_merge, ga, gm, za, zm)
        dga, dgm, dza, dzm = vjp(dm)
        return jnp.concatenate([dga, dgm], axis=1), dza, dzm

    dproj, dza, dzm = _rowwise(
        "merge_bwd", merge_bwd,
        [_In(proj, D_MODEL, C_GMERGE // D_MODEL), _In(proj, D_MODEL, C_GMERGE // D_MODEL + 1),
         _In(sv["za"]), _In(sv["zm"]), _In(dmerged)],
        [_Out(0, width=2 * D_MODEL, into=dproj, base=C_GMERGE // (2 * D_MODEL)), _Out(D_MODEL, BF16), _Out(D_MODEL, BF16)],
        T, ROW_BLOCK)
    dya = _mmw("branch_a_dx", dza, p["w_branch_attn"], "nt")
    g["w_branch_attn"] = _matmul("branch_a_dw", sv["y_a"], dza, "tn", out_dtype=BF16)
    dym = _mmw("branch_m_dx", dzm, p["w_branch_mlstm"], "nt")
    g["w_branch_mlstm"] = _matmul("branch_m_dw", sv["y_m"], dzm, "tn", out_dtype=BF16)

    def combine_bwd(hf, hb, o_pre, gn, dy):
        _, vjp = jax.vjp(_mlstm_combine, hf, hb, o_pre, gn)
        dhf, _, do, dg = vjp(dy)
        return dhf, do, dg

    dh, dproj, g["mlstm_out_norm"] = _rowwise(
        "mlstm_out_bwd", combine_bwd,
        [_In(sv["hf"], DH, split=True), _In(sv["hb"], DH, split=True), _In(proj, DH, C_OM // DH, split=True),
         _In(p["mlstm_out_norm"], DH, split=True, rows=False), _In(dym, DH, split=True)],
        [_Out(MLSTM_WIDTH, F32, DH, split=True), _Out(0, width=DH, split=True, into=dproj, base=C_OM // DH),
         _Out(MLSTM_WIDTH, F32, DH, split=True, rows=False)], T, 1024, ncol=MLSTM_HEADS)
    dqk_f, dqk_b, dv_f, dv_b, dg_f, dg_b = _mlstm_bwd(sv["qk_c"], proj3, p["gate_bias"], sv["states"],
                                                       dh.reshape(B, S, MLSTM_WIDTH))
    dproj, g["gate_bias"] = _rowwise(
        "mlstm_dsum_gates", lambda a, b: (a + b, jnp.sum(a + b, axis=0, keepdims=True)),
        [_In(dg_f.reshape(T, LANES)), _In(dg_b.reshape(T, LANES))],
        [_Out(0, width=LANES, into=dproj, base=C_GATES // LANES), _Out(LANES, rows=False)], T, 1024)
    dproj = _rowwise(
        "mlstm_dsum_v", lambda a, b: a + b, [_In(dv_f.reshape(T, MLSTM_WIDTH)), _In(dv_b.reshape(T, MLSTM_WIDTH))],
        [_Out(0, width=MLSTM_WIDTH, into=dproj, base=C_VM // MLSTM_WIDTH)], T, 1024)[0]
    dqk, g["conv_w8"] = _conv_bwd(proj3, p["conv_w8"], dqk_f, dqk_b)

    dqh, dkh, dvh, dsink = _attn_bwd(sv["qh"], sv["kh"], proj3, sv["sink"], dya.reshape(B, S, ATT_WIDTH))
    g["attn_sink"] = dsink.reshape(1, ATT_HEADS)
    dva = dvh.reshape(T, ATT_KV_WIDTH)
    dproj, dq_gain = _prep_bwd("q_prep_bwd", proj, ATT_WIDTH, C_QA // ATT_WIDTH, sv["q_gain"], cos, sin,
                               dqh.reshape(T, ATT_WIDTH), into=dproj)
    dka, dk_gain = _prep_bwd("k_prep_bwd", proj, ATT_KV_WIDTH, C_KA // ATT_KV_WIDTH, sv["k_gain"], cos, sin,
                             dkh.reshape(T, ATT_KV_WIDTH))
    g["attn_q_norm"] = jnp.sum(dq_gain.reshape(ATT_HEADS, ATT_HEAD_DIM), axis=0, keepdims=True)
    g["attn_k_norm"] = jnp.sum(dk_gain.reshape(ATT_KV_HEADS, ATT_HEAD_DIM), axis=0, keepdims=True)

    dproj = dproj.at[:, C_QK:C_QK + 2 * MLSTM_WIDTH].set(dqk.reshape(T, 2 * MLSTM_WIDTH))
    dproj = dproj.at[:, C_KA:C_KA + ATT_KV_WIDTH].set(dka)
    dproj = dproj.at[:, C_VA:C_VA + ATT_KV_WIDTH].set(dva.astype(BF16))
    dwin = _matmul("mix_dwin", sv["h"], dproj, "tn", out_dtype=BF16)
    token = on_dw({"w_in": _w_in_to_slots(dwin), "w_branch_attn": g.pop("w_branch_attn"),
                   "w_branch_mlstm": g.pop("w_branch_mlstm"), "w_out": g.pop("w_out")}, dwin)
    dh2 = _matmul("mix_dh", dproj, p["w_in"], "nt", dep=token)
    dx_new, g["mix_norm"] = _rms_bwd("mix_dnorm", sv["x"], p["mix_norm"], dh2, dx)
    return dx_new, g


def _loss_and_grad(x, g, target):
    T = x.shape[0]

    def loss_fn(xv, gv, tv):
        err = jnp.square(_rms(xv, gv) - tv)
        return 0.5 * jnp.sum(jnp.mean(err, axis=-1, keepdims=True), axis=0, keepdims=True)

    def fn(xv, gv, tv):
        val, vjp = jax.vjp(lambda a, b: loss_fn(a, b, tv), xv, gv)
        dx, dg = vjp(jnp.ones((1, 1), F32))
        return val, dx, dg

    return _rowwise("loss_head", fn, [_In(x), _In(g, rows=False), _In(target)],
                    [_Out(1, rows=False), _Out(D_MODEL), _Out(D_MODEL, rows=False)], T, ROW_BLOCK)


def _block_norm_fwd(x, g):
    T = x.shape[0]
    return _rowwise("block_norm", _rms, [_In(x), _In(g, rows=False)], [_Out(D_MODEL)], T, ROW_BLOCK)[0]


def _block_norm_bwd(x, g, dy):
    T = x.shape[0]

    def fn(xv, gv, dv):
        _, vjp = jax.vjp(_rms, xv, gv)
        return vjp(dv)

    return _rowwise("block_norm_bwd", fn, [_In(x), _In(g, rows=False), _In(dy)],
                    [_Out(D_MODEL), _Out(D_MODEL, rows=False)], T, ROW_BLOCK)


def _qk_perm_cols(t, axis):
    q, k = jnp.split(t, 2, axis=axis)
    parts = []
    for h in range(MLSTM_HEADS):
        sl = [slice(None)] * t.ndim
        sl[axis] = slice(h * MLSTM_HEAD_DIM, (h + 1) * MLSTM_HEAD_DIM)
        parts += [q[tuple(sl)], k[tuple(sl)]]
    return jnp.concatenate(parts, axis=axis)


def _qk_unperm_cols(t, axis):
    qs, ks = [], []
    for h in range(MLSTM_HEADS):
        sl = [slice(None)] * t.ndim
        sl[axis] = slice(2 * h * MLSTM_HEAD_DIM, (2 * h + 1) * MLSTM_HEAD_DIM)
        qs.append(t[tuple(sl)])
        sl[axis] = slice((2 * h + 1) * MLSTM_HEAD_DIM, (2 * h + 2) * MLSTM_HEAD_DIM)
        ks.append(t[tuple(sl)])
    return jnp.concatenate(qs + ks, axis=axis)


def _w_in_arrange(w):
    qa, ka, va, qm, km, vm, om, gm, gmerge = jnp.split(w, np.cumsum(
        (ATT_WIDTH, ATT_KV_WIDTH, ATT_KV_WIDTH, MLSTM_WIDTH, MLSTM_WIDTH, MLSTM_WIDTH, MLSTM_WIDTH, MLSTM_N_GATES))[:].tolist(), axis=1)
    qk = _qk_perm_cols(jnp.concatenate([qm, km], axis=1), 1)
    pad = jnp.zeros((w.shape[0], LANES - MLSTM_N_GATES), w.dtype)
    return jnp.concatenate([gmerge, qk, vm, om, qa, ka, va, gm, pad], axis=1)


def _w_in_restore(w):
    gmerge = w[:, C_GMERGE:C_GMERGE + 2 * D_MODEL]
    qk = _qk_unperm_cols(w[:, C_QK:C_QK + 2 * MLSTM_WIDTH], 1)
    vm, om = w[:, C_VM:C_VM + MLSTM_WIDTH], w[:, C_OM:C_OM + MLSTM_WIDTH]
    qa, ka, va = w[:, C_QA:C_QA + ATT_WIDTH], w[:, C_KA:C_KA + ATT_KV_WIDTH], w[:, C_VA:C_VA + ATT_KV_WIDTH]
    gm = w[:, C_GATES:C_GATES + MLSTM_N_GATES]
    return jnp.concatenate([qa, ka, va, qk, vm, om, gm, gmerge], axis=1)


BIG = ("ffn1_w_gate", "ffn1_w_up", "ffn1_w_down", "w_in", "mlstm_conv_w", "w_branch_attn", "w_branch_mlstm", "w_out",
       "ffn2_w_gate", "ffn2_w_up", "ffn2_w_down")
MATMUL_W = tuple(n for n in BIG if n != "mlstm_conv_w")
SMALL = ("ffn1_norm", "mix_norm", "mlstm_gate_bias", "attn_q_norm", "attn_k_norm", "attn_sink", "mlstm_conv_b",
         "mlstm_out_norm", "ffn2_norm", "block_out_norm")
WEIGHTS = ("ffn1_norm", "ffn1_w_gate", "ffn1_w_up", "ffn1_w_down", "mix_norm", "w_in", "mlstm_gate_bias", "attn_q_norm",
           "attn_k_norm", "attn_sink", "mlstm_conv_w", "mlstm_conv_b", "mlstm_out_norm", "w_branch_attn", "w_branch_mlstm",
           "w_out", "ffn2_norm", "ffn2_w_gate", "ffn2_w_up", "ffn2_w_down", "block_out_norm")
PACK_COLS = 1024


def _padded_rows(n_elems):
    return -(-n_elems // PACK_COLS)


def _pack_flat(arrs, dtype, row_multiple):
    parts = []
    for a in arrs:
        flat = a.reshape(-1).astype(dtype)
        pad = _padded_rows(flat.shape[0]) * PACK_COLS - flat.shape[0]
        parts.append(jnp.pad(flat, (0, pad)) if pad else flat)
    flat = jnp.concatenate(parts)
    rows = flat.shape[0] // PACK_COLS
    extra = (-rows) % row_multiple
    if extra:
        flat = jnp.pad(flat, (0, extra * PACK_COLS))
    return flat.reshape(-1, PACK_COLS)


def _pack_rows(name, pieces, total_rows):
    def body(*refs):
        o_ref = refs[-1]
        o_ref[...] = jnp.zeros_like(o_ref)
        at = 0
        for ref, (arr, r0, nr) in zip(refs[:-1], pieces):
            o_ref[at:at + nr, 0:arr.shape[1]] = ref[r0:r0 + nr, :].astype(F32)
            at += nr

    return pl.pallas_call(body, name=name, out_shape=jax.ShapeDtypeStruct((total_rows, PACK_COLS), F32))(
        *[p[0] for p in pieces])


def _unpack_flat(buf, shapes, lead=()):
    flat = buf.reshape(lead + (-1,))
    out, off = [], 0
    for s in shapes:
        n = int(np.prod(s))
        out.append(flat[..., off:off + n].reshape(lead + tuple(s)))
        off += _padded_rows(n) * PACK_COLS
    return out


class _Lay:
    def __init__(self, shard, axis, width):
        self.shard, self.axis, self.width = shard, axis, width
        self.padded = tuple(width if a == axis else s for a, s in enumerate(shard))
        self.whole = tuple(N_DEV * width if a == axis else s for a, s in enumerate(shard))


_FF_ROW = _Lay((FF_SHARD, D_MODEL), 0, FF_SHARD_PAD)
TRANSPOSED = ("ffn1_w_gate", "ffn1_w_up", "ffn2_w_gate", "ffn2_w_up")
LAYOUTS = {
    "ffn1_w_gate": _FF_ROW, "ffn1_w_up": _FF_ROW, "ffn1_w_down": _FF_ROW,
    "ffn2_w_gate": _FF_ROW, "ffn2_w_up": _FF_ROW, "ffn2_w_down": _FF_ROW,
    "w_in": _Lay((D_MODEL, IN_WIDTH // N_DEV), 0, D_MODEL),
    "mlstm_conv_w": _Lay((3, 2 * MLSTM_WIDTH // N_DEV), 1, 2 * MLSTM_WIDTH // N_DEV),
    "w_branch_attn": _Lay((ATT_WIDTH, D_MODEL // N_DEV), 1, D_MODEL // N_DEV),
    "w_branch_mlstm": _Lay((MLSTM_WIDTH, D_MODEL // N_DEV), 1, D_MODEL // N_DEV),
    "w_out": _Lay((D_MODEL // N_DEV, D_MODEL), 0, D_MODEL // N_DEV),
}


def _window(ref, axis, j, width):
    idx = [slice(None)] * len(ref.shape)
    idx[axis] = pl.ds(pl.multiple_of(j * width, width), width)
    return ref.at[tuple(idx)]


ANY = pl.BlockSpec(memory_space=pl.ANY)


def _mesh_pos():
    return lax.axis_index("x"), lax.axis_index("y"), lax.axis_index("c")


def _all_gather(name, shard, vmem=False):
    R, C = shard.shape
    space = pl.BlockSpec(memory_space=pltpu.VMEM) if vmem else ANY

    def body(x_ref, out_ref, send_sems, recv_sems, local_sem):
        x, y, c = _mesh_pos()
        me, sibling = (x, y, c), (x, y, 1 - c)
        chips = [(1 - x, y), (x, 1 - y), (1 - x, 1 - y)]

        def slot(px, py, pc):
            return out_ref.at[4 * px + 2 * py + pc]

        def copy(k, block, to, src=None):
            return pltpu.make_async_remote_copy(
                src_ref=slot(*block) if src is None else src, dst_ref=slot(*block),
                send_sem=send_sems.at[k], recv_sem=recv_sems.at[k], device_id=to, device_id_type=MESH)

        mine = pltpu.make_async_copy(x_ref, slot(*me), local_sem)
        mine.start()
        first = [copy(0, me, sibling, src=x_ref)]
        first += [copy(1 + j, me, (*chip, c), src=x_ref) for j, chip in enumerate(chips)]
        for cp in first:
            cp.start()
        passed = [copy(4 + j, (*chip, c), sibling) for j, chip in enumerate(chips)]
        for j, chip in enumerate(chips):
            copy(1 + j, (*chip, c), me).wait_recv()
            passed[j].start()
        copy(0, sibling, me).wait_recv()
        for j, chip in enumerate(chips):
            copy(4 + j, (*chip, 1 - c), me).wait_recv()
        for cp in first + passed:
            cp.wait_send()
        mine.wait()

    return pl.pallas_call(
        body, name=name, out_shape=jax.ShapeDtypeStruct((N_DEV, R, C), shard.dtype),
        in_specs=[space], out_specs=space,
        scratch_shapes=[pltpu.SemaphoreType.DMA((7,)), pltpu.SemaphoreType.DMA((7,)), pltpu.SemaphoreType.DMA],
    )(shard)


HBM = pl.BlockSpec(memory_space=pltpu.HBM)
SEM = pl.BlockSpec(memory_space=pltpu.SEMAPHORE)
SPLIT_COPY = pltpu.CompilerParams(has_side_effects=pltpu.SideEffectType.DATAFLOW_SIDE_EFFECTING)
N_PEERS = N_DEV - 1


def _peers(x, y, c):
    return [(x, y, 1 - c), (1 - x, y, c), (x, 1 - y, c), (1 - x, 1 - y, c),
            (1 - x, y, 1 - c), (x, 1 - y, 1 - c), (1 - x, 1 - y, 1 - c)]


def _dev_index(pos):
    return 4 * pos[0] + 2 * pos[1] + pos[2]


def _place_own(name, stacks, layer, lays):
    nt = len(stacks)
    me = _dev_index(_mesh_pos())

    def body(me_ref, *refs):
        for x_ref, s_ref, o_ref, lay in zip(refs[:nt], refs[nt:2 * nt], refs[2 * nt:], lays):
            rows = lay.shard[0]
            if lay.padded != lay.shard:
                s_ref[...] = jnp.zeros_like(s_ref)
            s_ref[0:rows, :] = x_ref[...].astype(BF16)
            o_ref[...] = s_ref[...]

    def window_spec(lay):
        if lay.axis == 0:
            return pl.BlockSpec(lay.padded, lambda i, me_ref: (me_ref[0], 0))
        return pl.BlockSpec(lay.padded, lambda i, me_ref: (0, me_ref[0]))

    for lay in lays:
        assert lay.padded[1] == lay.shard[1], "only rows are padded"
    res = pl.pallas_call(
        body, name=name,
        grid_spec=pltpu.PrefetchScalarGridSpec(
            num_scalar_prefetch=1, grid=(1,),
            in_specs=[pl.BlockSpec((None,) + lay.shard, lambda i, me_ref: (layer, 0, 0)) for lay in lays],
            out_specs=[pl.BlockSpec(lay.padded, lambda i, me_ref: (0, 0)) for lay in lays] + [window_spec(lay) for lay in lays]),
        out_shape=[jax.ShapeDtypeStruct(lay.padded, BF16) for lay in lays] + [jax.ShapeDtypeStruct(lay.whole, BF16) for lay in lays],
        compiler_params=_cparams(("arbitrary",)),
    )(me.reshape(1).astype(jnp.int32), *stacks)
    return list(res[:nt]), list(res[nt:])


NEAR_PEERS = 4


def _gather_start(name, shards, lands, lays, groups, n_peers, after):
    nt, ng = len(shards), len(groups)

    def body(*refs):
        x_refs, land_refs = refs[:nt], refs[nt:2 * nt]
        sems = refs[2 * nt + 1:2 * nt + 1 + 2 * ng]
        pos = _mesh_pos()
        me = _dev_index(pos)
        for g, tens in enumerate(groups):
            for i, t in enumerate(tens):
                for k, peer in enumerate(_peers(*pos)[:n_peers[g]]):
                    pltpu.make_async_remote_copy(
                        src_ref=x_refs[t], dst_ref=_window(land_refs[t], lays[t].axis, me, lays[t].width),
                        send_sem=sems[2 * g].at[n_peers[g] * i + k], recv_sem=sems[2 * g + 1].at[n_peers[g] * i + k],
                        device_id=peer, device_id_type=MESH).start()

    sem_shapes = []
    for g, tens in enumerate(groups):
        sem_shapes += [pltpu.SemaphoreType.DMA((n_peers[g] * len(tens),))] * 2
    thru = [pltpu.HBM(s.shape, s.dtype) for s in shards] + [pltpu.HBM(lay.whole, s.dtype) for s, lay in zip(shards, lays)]
    args = [pltpu.with_memory_space_constraint(s, pltpu.HBM) for s in shards]
    args += [pltpu.with_memory_space_constraint(ld, pltpu.HBM) for ld in lands]
    res = pl.pallas_call(
        body, name=name, out_shape=tuple(sem_shapes + thru), in_specs=[HBM] * (2 * nt) + [ANY],
        out_specs=tuple([SEM] * (2 * ng) + [HBM] * (2 * nt)),
        input_output_aliases={t: 2 * ng + t for t in range(2 * nt)}, compiler_params=SPLIT_COPY,
    )(*args, after)
    sems = [(res[2 * g], res[2 * g + 1]) for g in range(ng)]
    return sems, list(res[2 * ng:2 * ng + nt]), list(res[2 * ng + nt:])


def _gather_wait(name, sems, shards, lands, lays, n_peers, after):
    nt = len(shards)
    send_sems, recv_sems = sems

    def body(*refs):
        x_refs, land_refs = refs[:nt], refs[nt:2 * nt]
        send_ref, recv_ref = refs[2 * nt], refs[2 * nt + 1]
        pos = _mesh_pos()
        for t in range(nt):
            for k, peer in enumerate(_peers(*pos)[:n_peers]):
                cp = pltpu.make_async_remote_copy(
                    src_ref=x_refs[t], dst_ref=_window(land_refs[t], lays[t].axis, _dev_index(peer), lays[t].width),
                    send_sem=send_ref.at[n_peers * t + k], recv_sem=recv_ref.at[n_peers * t + k],
                    device_id=peer, device_id_type=MESH)
                cp.wait_send()
                cp.wait_recv()

    thru = [pltpu.HBM(s.shape, s.dtype) for s in shards] + [pltpu.HBM(ld.shape, ld.dtype) for ld in lands]
    res = pl.pallas_call(
        body, name=name, out_shape=tuple(thru), in_specs=[HBM] * (2 * nt) + [SEM, SEM, ANY],
        out_specs=tuple([HBM] * (2 * nt)), input_output_aliases={t: t for t in range(2 * nt)},
        compiler_params=SPLIT_COPY,
    )(*shards, *lands, send_sems, recv_sems, after)
    return list(res[nt:])


def _forward_to_sibling(name, lands, lays):
    nt = len(lands)

    def body(*refs):
        land_refs = refs[nt:2 * nt]
        send_sems, recv_sems = refs[2 * nt:]
        x, y, c = _mesh_pos()
        chips = [(1 - x, y), (x, 1 - y), (1 - x, 1 - y)]

        def copy(t, j, core):
            win = _window(land_refs[t], lays[t].axis, _dev_index((*chips[j], core)), lays[t].width)
            return pltpu.make_async_remote_copy(
                src_ref=win, dst_ref=win, send_sem=send_sems.at[3 * t + j], recv_sem=recv_sems.at[3 * t + j],
                device_id=(x, y, 1 - c), device_id_type=MESH)

        sends = [copy(t, j, c) for t in range(nt) for j in range(3)]
        for cp in sends:
            cp.start()
        for t in range(nt):
            for j in range(3):
                copy(t, j, 1 - c).wait_recv()
        for cp in sends:
            cp.wait_send()

    return pl.pallas_call(
        body, name=name, out_shape=[jax.ShapeDtypeStruct(ld.shape, ld.dtype) for ld in lands],
        in_specs=[ANY] * nt, out_specs=[ANY] * nt, input_output_aliases={t: t for t in range(nt)},
        scratch_shapes=[pltpu.SemaphoreType.DMA((3 * nt,)), pltpu.SemaphoreType.DMA((3 * nt,))],
    )(*lands)


def _pair_exchange(name, grads, lays):
    nt = len(grads)

    def body(*refs):
        g_refs, land_refs = refs[:nt], refs[nt:2 * nt]
        send_sems, recv_sems = refs[2 * nt:]
        x, y, c = _mesh_pos()
        copies = []
        for t in range(nt):
            for chip in range(4):
                copies.append(pltpu.make_async_remote_copy(
                    src_ref=_window(g_refs[t], lays[t].axis, 2 * chip + (1 - c), lays[t].width), dst_ref=land_refs[t].at[chip],
                    send_sem=send_sems.at[4 * t + chip], recv_sem=recv_sems.at[4 * t + chip],
                    device_id=(x, y, 1 - c), device_id_type=MESH))
        for cp in copies:
            cp.start()
        for cp in copies:
            cp.wait_recv()
        for cp in copies:
            cp.wait_send()

    out_shape = [jax.ShapeDtypeStruct((4,) + lay.padded, g.dtype) for g, lay in zip(grads, lays)]
    return pl.pallas_call(
        body, name=name, out_shape=out_shape, in_specs=[ANY] * nt, out_specs=[ANY] * nt,
        scratch_shapes=[pltpu.SemaphoreType.DMA((4 * nt,)), pltpu.SemaphoreType.DMA((4 * nt,))],
    )(*grads)


def _pair_sum(name, whole, landed, lay, out_dtype):
    R, C = lay.padded
    br = _first_divisor(R, (512, 384, 256, 128, 64, 32, 16, 8))
    nb = R // br
    if lay.axis == 0:
        mine_spec = pl.BlockSpec((br, C), lambda k, i, c_ref: ((2 * k + c_ref[0]) * nb + i, 0))
    else:
        mine_spec = pl.BlockSpec((br, C), lambda k, i, c_ref: (i, 2 * k + c_ref[0]))

    def body(c_ref, mine_ref, sib_ref, o_ref):
        o_ref[0] = (mine_ref[...].astype(F32) + sib_ref[0].astype(F32)).astype(out_dtype)

    c = lax.axis_index("c")
    return pl.pallas_call(
        body, name=name,
        grid_spec=pltpu.PrefetchScalarGridSpec(
            num_scalar_prefetch=1, grid=(4, nb),
            in_specs=[mine_spec, pl.BlockSpec((1, br, C), lambda k, i, c_ref: (k, i, 0))],
            out_specs=pl.BlockSpec((1, br, C), lambda k, i, c_ref: (k, i, 0))),
        out_shape=jax.ShapeDtypeStruct((4, R, C), out_dtype),
        compiler_params=_cparams(("parallel", "parallel")),
    )(c.reshape(1).astype(jnp.int32), whole, landed)


def _chip_start(name, sums):
    nt = len(sums)

    def body(*refs):
        s_refs, land_refs = refs[:nt], refs[nt:2 * nt]
        send_sems, recv_sems = refs[2 * nt], refs[2 * nt + 1]
        x, y, c = _mesh_pos()
        my_chip = 2 * x + y
        for t in range(nt):
            for j, (px, py) in enumerate([(1 - x, y), (x, 1 - y), (1 - x, 1 - y)]):
                pltpu.make_async_remote_copy(
                    src_ref=s_refs[t].at[2 * px + py], dst_ref=land_refs[t].at[my_chip],
                    send_sem=send_sems.at[3 * t + j], recv_sem=recv_sems.at[3 * t + j],
                    device_id=(px, py, c), device_id_type=MESH).start()

    thru = [pltpu.HBM(s.shape, s.dtype) for s in sums] * 2
    args = [pltpu.with_memory_space_constraint(s, pltpu.HBM) for s in sums]
    args += [pltpu.with_memory_space_constraint(lax.empty(s.shape, s.dtype), pltpu.HBM) for s in sums]
    res = pl.pallas_call(
        body, name=name, out_shape=tuple([pltpu.SemaphoreType.DMA((3 * nt,))] * 2 + thru), in_specs=[HBM] * (2 * nt),
        out_specs=tuple([SEM, SEM] + [HBM] * (2 * nt)), input_output_aliases={t: 2 + t for t in range(2 * nt)},
        compiler_params=SPLIT_COPY,
    )(*args)
    return (res[0], res[1]), list(res[2:2 + nt]), list(res[2 + nt:])


def _chip_wait(name, sems, sums, lands, after):
    nt = len(sums)

    def body(*refs):
        s_refs, land_refs = refs[:nt], refs[nt:2 * nt]
        send_sems, recv_sems = refs[2 * nt], refs[2 * nt + 1]
        x, y, c = _mesh_pos()
        my_chip = 2 * x + y
        for t in range(nt):
            for j, (px, py) in enumerate([(1 - x, y), (x, 1 - y), (1 - x, 1 - y)]):
                cp = pltpu.make_async_remote_copy(
                    src_ref=s_refs[t].at[my_chip], dst_ref=land_refs[t].at[2 * px + py],
                    send_sem=send_sems.at[3 * t + j], recv_sem=recv_sems.at[3 * t + j],
                    device_id=(px, py, c), device_id_type=MESH)
                cp.wait_send()
                cp.wait_recv()

    thru = [pltpu.HBM(s.shape, s.dtype) for s in sums] * 2
    res = pl.pallas_call(
        body, name=name, out_shape=tuple(thru), in_specs=[HBM] * (2 * nt) + [SEM, SEM, ANY],
        out_specs=tuple([HBM] * (2 * nt)), input_output_aliases={t: t for t in range(2 * nt)},
        compiler_params=SPLIT_COPY,
    )(*sums, *lands, sems[0], sems[1], after)
    return list(res[:nt]), list(res[nt:])


def _sum_chips(name, own, landed):
    _, R, C = own.shape
    br = _first_divisor(R, (512, 384, 256, 128, 64, 32, 16, 8))
    x, y, _ = _mesh_pos()
    slots = jnp.stack([2 * x + y, 2 * (1 - x) + y, 2 * x + (1 - y), 2 * (1 - x) + (1 - y)]).astype(jnp.int32)

    def body(slot_ref, mine_ref, a_ref, b_ref, c_ref, o_ref):
        o_ref[...] = ((mine_ref[0].astype(F32) + a_ref[0].astype(F32)) + b_ref[0].astype(F32)) + c_ref[0].astype(F32)

    def slot_spec(j):
        return pl.BlockSpec((1, br, C), lambda i, slot_ref: (slot_ref[j], i, 0))

    return pl.pallas_call(
        body, name=name,
        grid_spec=pltpu.PrefetchScalarGridSpec(
            num_scalar_prefetch=1, grid=(R // br,), in_specs=[slot_spec(0), slot_spec(1), slot_spec(2), slot_spec(3)],
            out_specs=pl.BlockSpec((br, C), lambda i, slot_ref: (i, 0))),
        out_shape=jax.ShapeDtypeStruct((R, C), F32), compiler_params=_cparams(("parallel",)),
    )(slots, own, landed, landed, landed)


def _sum_slots(name, slots, n):
    _, R, C = slots.shape
    br = _first_divisor(R, (512, 384, 256, 128, 64, 32, 16, 8))

    def body(s_ref, o_ref):
        acc = s_ref[0].astype(F32)
        for k in range(1, n):
            acc = acc + s_ref[k].astype(F32)
        o_ref[...] = acc

    return pl.pallas_call(
        body, name=name, grid=(R // br,), in_specs=[pl.BlockSpec((n, br, C), lambda i: (0, i, 0))],
        out_specs=pl.BlockSpec((br, C), lambda i: (i, 0)), out_shape=jax.ShapeDtypeStruct((R, C), F32),
        compiler_params=_cparams(("parallel",)),
    )(slots)


def _reduce_scatter_start(tag, names, grads):
    lays = [LAYOUTS[n] for n in names]
    landed = _pair_exchange("grads_pair_" + names[0], grads, lays)
    sums = [_pair_sum("grads_pairsum_" + n, g, ld, lay, BF16) for n, g, ld, lay in zip(names, grads, landed, lays)]
    sems, sums, lands = _chip_start(tag + "_chips_start", sums)
    return tag, names, sems, sums, lands


def _reduce_scatter_finish(pending, after):
    tag, names, sems, sums, lands = pending
    own, got = _chip_wait(tag + "_chips_wait", sems, sums, lands, after)
    return [_sum_chips("grads_sum_" + n, o, s) for n, o, s in zip(names, own, got)]


def _adamw_math(w, g, m, v):
    m = ADAM_B1 * m + (1.0 - ADAM_B1) * g
    v = ADAM_B2 * v + (1.0 - ADAM_B2) * jnp.square(g)
    m_hat = m / (1.0 - ADAM_B1 ** ADAM_STEP)
    v_hat = v / (1.0 - ADAM_B2 ** ADAM_STEP)
    delta = -ADAM_LR * (m_hat / (jnp.sqrt(v_hat) + ADAM_EPS) + ADAM_WD * w)
    return delta, m, v


def _adamw_layers(name, w, totals, m, v):
    _, R, C = w.shape
    br = _first_divisor(R, (512, 176, 128, 64, 32, 16, 8))
    Cp = totals[0].shape[1]

    def body(w_ref, g0_ref, g1_ref, m_ref, v_ref, g_out, d_out, m_out, v_out):
        g = jnp.where(pl.program_id(0) == 0, g0_ref[:, 0:C], g1_ref[:, 0:C])
        delta, m_new, v_new = _adamw_math(w_ref[0], g, m_ref[0], v_ref[0])
        g_out[0], d_out[0], m_out[0], v_out[0] = g, delta, m_new, v_new

    blk = pl.BlockSpec((1, br, C), lambda l, i: (l, i, 0))
    g_spec = pl.BlockSpec((br, Cp), lambda l, i: (i, 0))
    return pl.pallas_call(
        body, name=name, grid=(DEPTH, R // br), in_specs=[blk, g_spec, g_spec, blk, blk], out_specs=[blk] * 4,
        out_shape=[jax.ShapeDtypeStruct(w.shape, F32)] * 4, compiler_params=_cparams(("parallel", "parallel")),
    )(w, totals[0], totals[1], m, v)


def _adamw(name, w, g, m, v):
    shape = w.shape
    cols = shape[-1]
    rows = int(np.prod(shape[:-1]))
    br = _first_divisor(rows, (512, 352, 256, 128, 64, 32, 16, 8))
    args = [_In(a.reshape(rows, cols)) for a in (w, g, m, v)]
    outs = _rowwise(name, _adamw_math, args, [_Out(cols), _Out(cols), _Out(cols)], rows, br)
    return [o.reshape(shape) for o in outs]


GROUPS = {"ffn1": ("ffn1_w_gate", "ffn1_w_up", "ffn1_w_down"),
          "mix": ("w_in", "w_branch_attn", "w_branch_mlstm", "w_out"),
          "ffn2": ("ffn2_w_gate", "ffn2_w_up", "ffn2_w_down")}
GATHER_GROUPS = {"ffn1_in": ("ffn1_w_gate", "ffn1_w_up"), "ffn1_out": ("ffn1_w_down",),
                 "mix": ("w_in", "w_branch_attn", "w_branch_mlstm", "w_out"),
                 "ffn2_in": ("ffn2_w_gate", "ffn2_w_up"), "ffn2_out": ("ffn2_w_down",)}


def _small_params(small, conv_w, l):
    p = {}
    for n in ("ffn1_norm", "mix_norm", "ffn2_norm", "block_out_norm", "mlstm_out_norm", "attn_q_norm", "attn_k_norm"):
        p[n] = small[n][l][None, :]
    p["attn_sink"] = small["attn_sink"][l]
    p["gate_bias"] = jnp.pad(small["mlstm_gate_bias"][l], (0, LANES - MLSTM_N_GATES))[None, :]
    taps = _qk_perm_cols(conv_w[l], 1)
    conv_b = _qk_perm_cols(small["mlstm_conv_b"][l][None, :], 1)
    p["conv_w8"] = jnp.concatenate([taps, conv_b, jnp.zeros((4, 2 * MLSTM_WIDTH), F32)], axis=0)
    return p


def _w_in_from_slots(slots):
    w_in = slots.reshape(N_DEV, D_MODEL, IN_WIDTH // N_DEV).transpose(1, 0, 2).reshape(D_MODEL, IN_WIDTH)
    return _w_in_arrange(w_in)


def _w_in_to_slots(g):
    return _w_in_restore(g).reshape(D_MODEL, N_DEV, IN_WIDTH // N_DEV).transpose(1, 0, 2).reshape(
        N_DEV * D_MODEL, IN_WIDTH // N_DEV)


def _local_step(x, positions, target, weights_of, small, conv_w, on_grads):
    B, S, _ = x.shape
    T = B * S
    cos, sin = _rope_cos_sin(positions.reshape(T, 1))
    params = [_small_params(small, conv_w, l) for l in range(DEPTH)]
    xs = x.reshape(T, D_MODEL)
    tgt = target.reshape(T, D_MODEL)

    saved = []
    for l, p in enumerate(params):
        p.update(weights_of(l, "ffn1_in", xs))
        x1, s1, p["ffn1_w_down"] = _ffn_fwd("ffn1", xs, p["ffn1_norm"], p["ffn1_w_gate"], p["ffn1_w_up"],
                                            lambda after, l=l: weights_of(l, "ffn1_out", after)["ffn1_w_down"])
        p.update(weights_of(l, "mix", x1))
        p["w_in"] = _w_in_from_slots(p["w_in"])
        x2, s2 = _mix_fwd(x1, cos, sin, B, S, p)
        p.update(weights_of(l, "ffn2_in", x2))
        x3, s3, p["ffn2_w_down"] = _ffn_fwd("ffn2", x2, p["ffn2_norm"], p["ffn2_w_gate"], p["ffn2_w_up"],
                                            lambda after, l=l: weights_of(l, "ffn2_out", after)["ffn2_w_down"])
        saved.append((s1, s2, s3, x3))
        if l + 1 < DEPTH:
            xs = _block_norm_fwd(x3, p["block_out_norm"])

    sm = {}
    loss = None
    dx = None
    for l in reversed(range(DEPTH)):
        p = params[l]
        s1, s2, s3, x3 = saved[l]
        if l == DEPTH - 1:
            loss, dx, dgn = _loss_and_grad(x3, p["block_out_norm"], tgt)
        else:
            dx, dgn = _block_norm_bwd(x3, p["block_out_norm"], dx)
        sm["block_out_norm", l] = (dgn, 0, 1)
        dx, dg = _ffn_bwd("ffn2", s3, p["ffn2_norm"], p["ffn2_w_gate"], p["ffn2_w_up"], p["ffn2_w_down"], dx,
                          functools.partial(on_grads, l, "ffn2"))
        sm["ffn2_norm", l] = (dg, 0, 1)
        dx, g = _mix_bwd(s2, cos, sin, B, S, p, dx, functools.partial(on_grads, l, "mix"))
        dconv = _qk_unperm_cols(g["conv_w8"], 1)
        sm["mlstm_conv_w", l] = (dconv, 0, 3)
        sm["mlstm_conv_b", l] = (dconv, 3, 1)
        for n, key in (("mix_norm", "mix_norm"), ("mlstm_gate_bias", "gate_bias"), ("attn_q_norm", "attn_q_norm"),
                       ("attn_k_norm", "attn_k_norm"), ("attn_sink", "attn_sink"), ("mlstm_out_norm", "mlstm_out_norm")):
            sm[n, l] = (g[key], 0, 1)
        dx, dg = _ffn_bwd("ffn1", s1, p["ffn1_norm"], p["ffn1_w_gate"], p["ffn1_w_up"], p["ffn1_w_down"], dx,
                          functools.partial(on_grads, l, "ffn1"))
        sm["ffn1_norm", l] = (dg, 0, 1)
    return loss, dx.reshape(B, S, D_MODEL), sm


def kernel(x, positions, ffn1_norm, ffn1_w_gate, ffn1_w_up, ffn1_w_down, mix_norm, w_in, mlstm_gate_bias, attn_q_norm, attn_k_norm, attn_sink, mlstm_conv_w, mlstm_conv_b, mlstm_out_norm, w_branch_attn, w_branch_mlstm, w_out, ffn2_norm, ffn2_w_gate, ffn2_w_up, ffn2_w_down, block_out_norm, loss_target, m_ffn1_norm, m_ffn1_w_gate, m_ffn1_w_up, m_ffn1_w_down, m_mix_norm, m_w_in, m_mlstm_gate_bias, m_attn_q_norm, m_attn_k_norm, m_attn_sink, m_mlstm_conv_w, m_mlstm_conv_b, m_mlstm_out_norm, m_w_branch_attn, m_w_branch_mlstm, m_w_out, m_ffn2_norm, m_ffn2_w_gate, m_ffn2_w_up, m_ffn2_w_down, m_block_out_norm, v_ffn1_norm, v_ffn1_w_gate, v_ffn1_w_up, v_ffn1_w_down, v_mix_norm, v_w_in, v_mlstm_gate_bias, v_attn_q_norm, v_attn_k_norm, v_attn_sink, v_mlstm_conv_w, v_mlstm_conv_b, v_mlstm_out_norm, v_w_branch_attn, v_w_branch_mlstm, v_w_out, v_ffn2_norm, v_ffn2_w_gate, v_ffn2_w_up, v_ffn2_w_down, v_block_out_norm):
    args = locals()
    def stored(n, t):
        return t.transpose(0, 2, 1) if n in TRANSPOSED else t

    w = {n: stored(n, args[n]) for n in WEIGHTS}
    m = {n: stored(n, args["m_" + n]) for n in WEIGHTS}
    v = {n: stored(n, args["v_" + n]) for n in WEIGHTS}

    order = [(l, grp) for l in range(DEPTH) for grp in GATHER_GROUPS]
    keys = [(l, n) for l, grp in order for n in GATHER_GROUPS[grp]]
    lays = [LAYOUTS[n] for _, n in keys]
    group_idx, at = {}, 0
    for l, grp in order:
        group_idx[(l, grp)] = list(range(at, at + len(GATHER_GROUPS[grp])))
        at += len(GATHER_GROUPS[grp])
    conv_shape = w["mlstm_conv_w"].shape
    conv_all = _all_gather("conv_all_gather", _pack_flat([w["mlstm_conv_w"]], F32, 8), vmem=True)
    conv_parts = _unpack_flat(conv_all, [conv_shape], lead=(N_DEV,))[0]
    conv_w = jnp.concatenate([conv_parts[j] for j in range(N_DEV)], axis=2)
    small = {n: w[n] for n in SMALL}

    shards, lands = [], []
    for l, grp in order:
        own, whole = _place_own("weights_place_" + grp, [w[n] for n in GATHER_GROUPS[grp]], l,
                                [lays[i] for i in group_idx[(l, grp)]])
        shards += own
        lands += whole
    n_peers = [NEAR_PEERS if (l, grp) in ((0, "ffn1_in"), (0, "ffn1_out"), (0, "mix")) else N_PEERS for l, grp in order]
    sems, shards, lands = _gather_start("weights_gather_start", shards, lands, lays, [group_idx[k] for k in order],
                                        n_peers, conv_all)

    def weights_of(l, grp, after):
        idx, g = group_idx[(l, grp)], order.index((l, grp))
        group_lays = [lays[i] for i in idx]
        whole = _gather_wait(f"weights_gather_wait_{l}_{grp}", sems[g], [shards[i] for i in idx],
                             [lands[i] for i in idx], group_lays, n_peers[g], after)
        if n_peers[g] == NEAR_PEERS:
            whole = _forward_to_sibling("weights_forward_" + grp, whole, group_lays)
        return dict(zip(GATHER_GROUPS[grp], whole))

    totals, pending = {}, []

    def finish(after):
        tag, names = pending[0][0], pending[0][1]
        for n, t in zip(names, _reduce_scatter_finish(pending.pop(0), after)):
            totals[(tag, n)] = t

    def on_grads(l, grp, g, after):
        if pending:
            finish(after)
        names = GROUPS[grp]
        pending.append(_reduce_scatter_start(f"grads_{l}_{grp}", names, [g[n] for n in names]))
        return pending[-1][3][0]

    loss, grad_x, small_g = _local_step(x, positions, loss_target, weights_of, small, conv_w, on_grads)
    finish(grad_x)
    grads, deltas, new_m, new_v = {}, {}, {}, {}
    for grp, names in GROUPS.items():
        for n in names:
            grads[n], deltas[n], new_m[n], new_v[n] = _adamw_layers(
                "adamw_" + n, w[n], [totals[(f"grads_{l}_{grp}", n)] for l in range(DEPTH)], m[n], v[n])

    n_small = DEPTH * len(SMALL)
    taps_at, loss_at, rows = n_small, n_small + 3 * DEPTH, 32
    pieces = [small_g[n, l] for n in SMALL for l in range(DEPTH)]
    pieces += [small_g["mlstm_conv_w", l] for l in range(DEPTH)] + [(loss, 0, 1)]
    small_all = _all_gather("small_all_gather", _pack_rows("small_pack", pieces, rows), vmem=True)
    small_sum = _sum_slots("small_sum", small_all, N_DEV)
    loss_total = small_sum[loss_at, 0]
    x_pos, y_pos, c_pos = _mesh_pos()
    grads["mlstm_conv_w"] = lax.dynamic_slice_in_dim(
        small_sum[taps_at:loss_at].reshape(DEPTH, 3, 2 * MLSTM_WIDTH),
        (4 * x_pos + 2 * y_pos + c_pos) * conv_shape[2], conv_shape[2], axis=2)

    n = "mlstm_conv_w"
    deltas[n], new_m[n], new_v[n] = _adamw("adamw_" + n, w[n], grads[n], m[n], v[n])
    sw, smm, sv = (_pack_rows("small_pack_" + tag, [(d[n], 0, DEPTH) for n in SMALL], rows)
                   for tag, d in (("w", w), ("m", m), ("v", v)))
    sd, snm, snv = _adamw("adamw_small", sw, small_sum, smm, sv)
    for i, n in enumerate(SMALL):
        rows_n, width = slice(DEPTH * i, DEPTH * (i + 1)), w[n].shape[1]
        grads[n], deltas[n], new_m[n], new_v[n] = (buf[rows_n, :width] for buf in (small_sum, sd, snm, snv))

    return (loss_total.reshape(()), grad_x, *[stored(n, d[n]) for d in (grads, deltas, new_m, new_v) for n in WEIGHTS])
```

```python
import functools

import numpy as np
import jax
import jax.numpy as jnp
from jax import lax
from jax.experimental import pallas as pl
from jax.experimental.pallas import tpu as pltpu

F32 = jnp.float32
BF16 = jnp.bfloat16

D_MODEL = 1024
D_FF = 2816
ATT_HEAD_DIM = 64
ATT_HEADS = 8
ATT_KV_HEADS = 2
ATT_GROUP = ATT_HEADS // ATT_KV_HEADS
ATT_WIDTH = ATT_HEADS * ATT_HEAD_DIM
ATT_KV_WIDTH = ATT_KV_HEADS * ATT_HEAD_DIM
WINDOW = 128
ATT_BLOCK = 128
ROPE_DIM = 16
ROPE_THETA = 500000.0
MLSTM_HEADS = 4
MLSTM_HEAD_DIM = 128
MLSTM_WIDTH = MLSTM_HEADS * MLSTM_HEAD_DIM
MLSTM_CHUNK = 128
MLSTM_N_GATES = 4 * MLSTM_HEADS
NORM_EPS = 1e-6
IN_WIDTH = 4880
DEPTH = 2
N_DEV = 8

ADAM_LR = 0.001
ADAM_B1 = 0.9
ADAM_B2 = 0.999
ADAM_EPS = 1e-08
ADAM_WD = 0.01
ADAM_STEP = 10

LANES = 128
C_GMERGE = 0
C_QK = 2048
C_VM = 3072
C_OM = 3584
C_QA = 4096
C_KA = 4608
C_VA = 4736
C_GATES = 4864
IN_PAD = 4992

VMEM_LIMIT = 48 * 1024 * 1024

MESH = pl.DeviceIdType.MESH


def _cparams(sem):
    return pltpu.CompilerParams(dimension_semantics=sem, vmem_limit_bytes=VMEM_LIMIT)


def _first_divisor(n, cands):
    for c in cands:
        if n % c == 0:
            return c
    return n


_NN = ((1,), (0,))
_NT = ((1,), (1,))
_TN = ((0,), (0,))


def _mm(a, b, dims):
    return lax.dot_general(a.astype(BF16), b.astype(BF16), (dims, ((), ())), preferred_element_type=F32)


@jax.custom_vjp
def mm_nn(a, b):
    return _mm(a, b, _NN)


def _mm_nn_fwd(a, b):
    return _mm(a, b, _NN), (a, b)


def _mm_nn_bwd(res, g):
    a, b = res
    return _mm(g, b, _NT).astype(a.dtype), _mm(a, g, _TN).astype(b.dtype)


mm_nn.defvjp(_mm_nn_fwd, _mm_nn_bwd)


@jax.custom_vjp
def mm_nt(a, b):
    return _mm(a, b, _NT)


def _mm_nt_fwd(a, b):
    return _mm(a, b, _NT), (a, b)


def _mm_nt_bwd(res, g):
    a, b = res
    return _mm(g, b, _NN).astype(a.dtype), _mm(g, a, _TN).astype(b.dtype)


mm_nt.defvjp(_mm_nt_fwd, _mm_nt_bwd)


@jax.custom_vjp
def mm_tn(a, b):
    return _mm(a, b, _TN)


def _mm_tn_fwd(a, b):
    return _mm(a, b, _TN), (a, b)


def _mm_tn_bwd(res, g):
    a, b = res
    return _mm(b, g, _NT).astype(a.dtype), _mm(a, g, _NN).astype(b.dtype)


mm_tn.defvjp(_mm_tn_fwd, _mm_tn_bwd)


def _matmul(name, a, b, mode, out_dtype=F32, res=None, scale=1.0, bl=None, dep=None, whole_k=False):
    b_shape = b.shape if bl is None else b.shape[1:]
    if mode == "nn":
        (M, K), (K2, N) = a.shape, b_shape
    elif mode == "nt":
        (M, K), (N, K2) = a.shape, b_shape
    else:
        (K, M), (K2, N) = a.shape, b_shape
    assert K == K2, (name, a.shape, b.shape)
    tm = _first_divisor(M, (1024, 512, 384, 256, 128))
    tn = _first_divisor(N, (1024, 1664, 512, 384, 256, 128))
    tk = K if whole_k else _first_divisor(K, (1024, 1664, 512, 256, 128))
    if whole_k:
        tn = min(tn, 512)
    nk = K // tk
    if mode == "tn":
        a_spec = pl.BlockSpec((tk, tm), lambda i, j, k: (k, i))
    else:
        a_spec = pl.BlockSpec((tm, tk), lambda i, j, k: (i, k))
    if mode == "nt":
        b_blk, b_idx = (tn, tk), (lambda i, j, k: (j, k))
    else:
        b_blk, b_idx = (tk, tn), (lambda i, j, k: (k, j))
    if bl is None:
        b_spec = pl.BlockSpec(b_blk, b_idx)
    else:
        b_spec = pl.BlockSpec((None,) + b_blk, lambda i, j, k: (bl,) + b_idx(i, j, k))
    o_spec = pl.BlockSpec((tm, tn), lambda i, j, k: (i, j))
    dims = {"nn": _NN, "nt": _NT, "tn": _TN}[mode]
    has_res = res is not None

    def body(*refs):
        a_ref, b_ref = refs[:2]
        r_ref = refs[2] if has_res else None

        def finish(out):
            if scale != 1.0:
                out = out * scale
            if has_res:
                out = r_ref[...].astype(F32) + out
            o_ref[...] = out.astype(out_dtype)

        if nk == 1:
            o_ref = refs[-1]
            finish(_mm(a_ref[...], b_ref[...], dims))
            return
        o_ref, acc = refs[-2:]
        k = pl.program_id(2)

        @pl.when(k == 0)
        def _():
            acc[...] = jnp.zeros_like(acc)

        acc[...] += _mm(a_ref[...], b_ref[...], dims)

        @pl.when(k == nk - 1)
        def _():
            finish(acc[...])

    in_specs = [a_spec, b_spec] + ([o_spec] if has_res else [])
    args = (a, b) + ((res,) if has_res else ())
    if dep is not None:
        in_specs.append(pl.BlockSpec(memory_space=pl.ANY))
        args += (dep,)
    return pl.pallas_call(
        body, name=name, grid=(M // tm, N // tn, nk), in_specs=in_specs, out_specs=o_spec,
        out_shape=jax.ShapeDtypeStruct((M, N), out_dtype),
        scratch_shapes=[pltpu.VMEM((tm, tn), F32)] if nk > 1 else [],
        compiler_params=_cparams(("parallel", "parallel", "arbitrary")),
    )(*args)


class _In:
    def __init__(self, arr, width=None, base=0, split=False, rows=True):
        self.arr, self.base, self.split, self.rows = arr, base, split, rows
        self.width = arr.shape[1] if width is None else width


class _Out:
    def __init__(self, cols, dtype=F32, width=None, split=False, rows=True, nrows=1, into=None, base=0):
        self.cols, self.dtype, self.split, self.rows, self.nrows = cols, dtype, split, rows, nrows
        self.width = cols if width is None else width
        self.into, self.base = into, base
        if into is not None:
            self.cols, self.dtype = into.shape[1], into.dtype


def _rowwise(name, fn, ins, outs, n_rows, br, ncol=1):
    br = min(br, n_rows)
    assert n_rows % br == 0, (name, n_rows, br)
    nrow_blocks = n_rows // br

    def in_spec(d):
        nb = br if d.rows else d.arr.shape[0]
        if d.rows and d.split:
            im = lambda j, i, base=d.base: (i, base + j)
        elif d.rows:
            im = lambda j, i, base=d.base: (i, base)
        elif d.split:
            im = lambda j, i, base=d.base: (0, base + j)
        else:
            im = lambda j, i, base=d.base: (0, base)
        return pl.BlockSpec((nb, d.width), im)

    def out_spec(d):
        nb = br if d.rows else d.nrows
        if d.rows and d.split:
            im = lambda j, i, base=d.base: (i, base + j)
        elif d.rows:
            im = lambda j, i, base=d.base: (i, base)
        elif d.split:
            im = lambda j, i: (0, j)
        else:
            im = lambda j, i: (0, 0)
        return pl.BlockSpec((nb, d.width), im)

    n_in = len(ins)
    targets = [(k, d.into) for k, d in enumerate(outs) if d.into is not None]

    def body(*refs):
        i = pl.program_id(1)
        vals = [r[...] for r in refs[:n_in]]
        res = fn(*vals)
        if not isinstance(res, (tuple, list)):
            res = (res,)
        for d, ref, val in zip(outs, refs[n_in + len(targets):], res):
            if d.rows:
                ref[...] = val.astype(d.dtype)
            else:
                @pl.when(i == 0)
                def _(ref=ref):
                    ref[...] = jnp.zeros_like(ref)

                ref[...] += val.astype(d.dtype)

    out_shape = [jax.ShapeDtypeStruct((n_rows if d.rows else d.nrows, d.cols), d.dtype) for d in outs]
    res = pl.pallas_call(
        body, name=name, grid=(ncol, nrow_blocks),
        in_specs=[in_spec(d) for d in ins] + [pl.BlockSpec(memory_space=pl.ANY)] * len(targets),
        out_specs=[out_spec(d) for d in outs], out_shape=out_shape,
        input_output_aliases={n_in + t: k for t, (k, _) in enumerate(targets)},
        compiler_params=_cparams(("parallel", "arbitrary")),
    )(*[d.arr for d in ins], *[arr for _, arr in targets])
    return res


def _rms(x, g):
    return x * lax.rsqrt(jnp.mean(x * x, axis=-1, keepdims=True) + NORM_EPS) * g


def _sigmoid(x):
    return 0.5 * jnp.tanh(0.5 * x) + 0.5


def _silu(x):
    return x * _sigmoid(x)


def _log_sigmoid(x):
    return jnp.minimum(x, 0.0) - jnp.log(1.0 + jnp.exp(-jnp.abs(x)))


def _rope_tables(pos, inv_freq_row):
    ang = pos.astype(F32) * inv_freq_row
    return jnp.cos(ang), jnp.sin(ang)


def _head_sums_impl(v):
    w = v.shape[-1]
    shift = ATT_HEAD_DIM.bit_length() - 1
    r = lax.shift_right_logical(lax.broadcasted_iota(jnp.int32, (w, w), 0), shift)
    c = lax.shift_right_logical(lax.broadcasted_iota(jnp.int32, (w, w), 1), shift)
    ones = (r == c).astype(BF16)
    hi = v.astype(BF16)
    lo = (v - hi.astype(F32)).astype(BF16)
    dn = (_NN, ((), ()))
    return (lax.dot_general(hi, ones, dn, preferred_element_type=F32)
            + lax.dot_general(lo, ones, dn, preferred_element_type=F32))


@jax.custom_vjp
def _head_sums(v):
    return _head_sums_impl(v)


_head_sums.defvjp(lambda v: (_head_sums_impl(v), None), lambda _, g: (_head_sums_impl(g),))


def _rotate_half_impl(y):
    w = y.shape[-1]
    half = ROPE_DIM // 2
    lane = lax.broadcasted_iota(jnp.int32, y.shape, 1) & (ATT_HEAD_DIM - 1)
    above = pltpu.roll(y, w - half, axis=1)
    below = pltpu.roll(y, half, axis=1)
    return jnp.where(lane < half, -above, jnp.where(lane < ROPE_DIM, below, 0.0))


@jax.custom_vjp
def _rotate_half(y):
    return _rotate_half_impl(y)


_rotate_half.defvjp(lambda y: (_rotate_half_impl(y), None), lambda _, g: (-_rotate_half_impl(g),))


def _qk_prep(t, g, cos, sin):
    reps = t.shape[-1] // cos.shape[-1]
    if reps > 1:
        cos, sin = jnp.tile(cos, (1, reps)), jnp.tile(sin, (1, reps))
    y = t * lax.rsqrt(_head_sums(t * t) * (1.0 / ATT_HEAD_DIM) + NORM_EPS) * g
    return y * cos + _rotate_half(y) * sin


def _attn_head(q, kb, vb, sink, valid):
    s = mm_nt(q, kb) * (ATT_HEAD_DIM ** -0.5)
    s = jnp.where(valid, s, -jnp.inf)
    m = jnp.maximum(jnp.max(s, axis=-1, keepdims=True), sink)
    p = jnp.exp(s - m)
    den = jnp.sum(p, axis=-1, keepdims=True) + jnp.exp(sink - m)
    return mm_nn(p * (1.0 / den), vb)


def _mlstm_chunk(q, k, v, li, lf, C, n, m, incl, incl_t, eye):
    k = k * (MLSTM_HEAD_DIM ** -0.5)
    lf_row = jnp.sum(eye * lf, axis=0, keepdims=True)
    li_row = jnp.sum(eye * li, axis=0, keepdims=True)
    b = jnp.sum(incl * lf_row, axis=1, keepdims=True)
    b_row = jnp.sum(incl_t * lf, axis=0, keepdims=True)
    b_tot = jnp.sum(lf, axis=0, keepdims=True)
    a = b_tot - b + li
    a_max = jnp.max(a, axis=0, keepdims=True)
    kw = k * jnp.exp(a - a_max)
    c_loc = mm_tn(kw, v)
    n_loc = jnp.sum(kw, axis=0, keepdims=True)

    dmat = jnp.where(incl > 0.5, b - b_row + li_row, -jnp.inf)
    inter = b + m
    m_t = jnp.maximum(inter, jnp.max(dmat, axis=1, keepdims=True))
    sc = mm_nt(q, k) * jnp.exp(dmat - m_t)
    scale_in = jnp.exp(inter - m_t)
    num = mm_nn(sc, v) + scale_in * mm_nn(q, C)
    den = jnp.sum(sc, axis=1, keepdims=True) + scale_in * jnp.sum(q * n, axis=1, keepdims=True)
    h = num * (1.0 / jnp.maximum(jnp.abs(den), jnp.exp(-m_t)))

    m_new = jnp.maximum(b_tot + m, a_max)
    s_p = jnp.exp(b_tot + m - m_new)
    s_l = jnp.exp(a_max - m_new)
    return h, s_p * C + s_l * c_loc, s_p * n + s_l * n_loc, m_new


def _mlstm_combine(hf, hb, o_pre, g):
    h = hf + hb
    mu = jnp.mean(h, axis=-1, keepdims=True)
    var = jnp.mean(jnp.square(h - mu), axis=-1, keepdims=True)
    return _sigmoid(o_pre) * ((h - mu) * lax.rsqrt(var + NORM_EPS) * g)


def _merge(ga, gm, za, zm):
    return _sigmoid(ga) * za + _sigmoid(gm) * zm


def _attn_mask(n, seq):
    shape = (ATT_GROUP * ATT_BLOCK, 3 * ATT_BLOCK)
    qi = n * ATT_BLOCK + (lax.broadcasted_iota(jnp.int32, shape, 0) & (ATT_BLOCK - 1))
    kj = (n - 1) * ATT_BLOCK + lax.broadcasted_iota(jnp.int32, shape, 1)
    return (jnp.abs(qi - kj) <= WINDOW) & (kj >= 0) & (kj < seq)


def _attn_specs(nq, v_base):
    q_spec = pl.BlockSpec((1, ATT_BLOCK, ATT_WIDTH), lambda b, n: (b, n, 0))

    def kv_spec(off, base=0):
        return pl.BlockSpec((1, ATT_BLOCK, ATT_KV_WIDTH), lambda b, n: (b, jnp.clip(n + off, 0, nq - 1), base))

    sink_spec = pl.BlockSpec((ATT_KV_HEADS, ATT_GROUP, 1, 1), lambda b, n: (0, 0, 0, 0))
    specs = [q_spec, kv_spec(-1), kv_spec(0), kv_spec(1), kv_spec(-1, v_base), kv_spec(0, v_base), kv_spec(1, v_base), sink_spec]
    return q_spec, specs, sink_spec


def _head(h):
    return slice(h * ATT_HEAD_DIM, (h + 1) * ATT_HEAD_DIM)


def _group_rows(q_ref, s_ref, h):
    q4 = jnp.concatenate([q_ref[0, :, _head(h * ATT_GROUP + g)] for g in range(ATT_GROUP)], axis=0)
    sink4 = jnp.concatenate([jnp.broadcast_to(s_ref[h, g], (ATT_BLOCK, 1)) for g in range(ATT_GROUP)], axis=0)
    return q4, sink4


def _attn_fwd(q, k, proj3, sink):
    B, S, _ = q.shape
    nq = S // ATT_BLOCK
    q_spec, specs, _ = _attn_specs(nq, C_VA // ATT_KV_WIDTH)

    def body(q_ref, kp, kc, kn, vp, vc, vn, s_ref, o_ref):
        valid = _attn_mask(pl.program_id(1), S)
        for h in range(ATT_KV_HEADS):
            kb = jnp.concatenate([kp[0, :, _head(h)], kc[0, :, _head(h)], kn[0, :, _head(h)]], axis=0)
            vb = jnp.concatenate([vp[0, :, _head(h)], vc[0, :, _head(h)], vn[0, :, _head(h)]], axis=0)
            q4, sink4 = _group_rows(q_ref, s_ref, h)
            o4 = _attn_head(q4, kb, vb, sink4, valid).astype(BF16)
            for g in range(ATT_GROUP):
                o_ref[0, :, _head(h * ATT_GROUP + g)] = o4[g * ATT_BLOCK:(g + 1) * ATT_BLOCK]

    return pl.pallas_call(
        body, name="attn_fwd", grid=(B, nq), in_specs=specs,
        out_specs=q_spec, out_shape=jax.ShapeDtypeStruct(q.shape, BF16),
        compiler_params=_cparams(("parallel", "arbitrary")),
    )(q, k, k, k, proj3, proj3, proj3, sink)


def _attn_bwd(q, k, proj3, sink, dy):
    B, S, _ = q.shape
    nq = S // ATT_BLOCK
    q_spec, specs, sink_spec = _attn_specs(nq, C_VA // ATT_KV_WIDTH)
    kv_full = pl.BlockSpec((1, S, ATT_KV_WIDTH), lambda b, n: (b, 0, 0))

    def body(q_ref, kp, kc, kn, vp, vc, vn, s_ref, dy_ref, dq_ref, dk_ref, dv_ref, ds_ref):
        b, n = pl.program_id(0), pl.program_id(1)
        valid = _attn_mask(n, S)

        @pl.when(n == 0)
        def _():
            dk_ref[...] = jnp.zeros_like(dk_ref)
            dv_ref[...] = jnp.zeros_like(dv_ref)

        @pl.when((n == 0) & (b == 0))
        def _():
            ds_ref[...] = jnp.zeros_like(ds_ref)

        for h in range(ATT_KV_HEADS):
            kb = jnp.concatenate([kp[0, :, _head(h)], kc[0, :, _head(h)], kn[0, :, _head(h)]], axis=0)
            vb = jnp.concatenate([vp[0, :, _head(h)], vc[0, :, _head(h)], vn[0, :, _head(h)]], axis=0)
            q4, sink4 = _group_rows(q_ref, s_ref, h)
            dy4 = jnp.concatenate([dy_ref[0, :, _head(h * ATT_GROUP + g)] for g in range(ATT_GROUP)], axis=0)
            _, vjp = jax.vjp(functools.partial(_attn_head, valid=valid), q4, kb, vb, sink4)
            dq4, dkb, dvb, dsink4 = vjp(dy4)
            for g in range(ATT_GROUP):
                rows = slice(g * ATT_BLOCK, (g + 1) * ATT_BLOCK)
                dq_ref[0, :, _head(h * ATT_GROUP + g)] = dq4[rows]
                ds_ref[h, g] += jnp.sum(dsink4[rows], axis=0, keepdims=True)
            for j, off in enumerate((-1, 0, 1)):
                start = pl.multiple_of(jnp.clip(n + off, 0, nq - 1) * ATT_BLOCK, ATT_BLOCK)
                rows = pl.ds(start, ATT_BLOCK)
                dk_ref[0, rows, _head(h)] += dkb[j * ATT_BLOCK:(j + 1) * ATT_BLOCK]
                dv_ref[0, rows, _head(h)] += dvb[j * ATT_BLOCK:(j + 1) * ATT_BLOCK]

    kv_shape = jax.ShapeDtypeStruct(k.shape, F32)
    return pl.pallas_call(
        body, name="attn_bwd", grid=(B, nq), in_specs=specs + [q_spec],
        out_specs=[q_spec, kv_full, kv_full, sink_spec],
        out_shape=[jax.ShapeDtypeStruct(q.shape, F32), kv_shape, kv_shape, jax.ShapeDtypeStruct(sink.shape, F32)],
        compiler_params=_cparams(("arbitrary", "arbitrary")),
    )(q, k, k, k, proj3, proj3, proj3, sink, dy)


CONV_COLS = 256


def _conv_taps(u, seq):
    row = lax.broadcasted_iota(jnp.int32, u.shape, 0)
    prev = jnp.where(row == 0, 0.0, pltpu.roll(u, 1, axis=0))
    nxt = jnp.where(row == seq - 1, 0.0, pltpu.roll(u, seq - 1, axis=0))
    return prev, nxt


def _conv_fwd(proj3, w8):
    B, S, _ = proj3.shape
    ncb = 2 * MLSTM_WIDTH // CONV_COLS

    def body(u_ref, w_ref, o_ref):
        u = u_ref[0]
        prev, nxt = _conv_taps(u, S)
        o_ref[0] = _silu(prev * w_ref[0:1, :] + u * w_ref[1:2, :] + nxt * w_ref[2:3, :] + w_ref[3:4, :])

    return pl.pallas_call(
        body, name="conv_fwd", grid=(B, ncb),
        in_specs=[pl.BlockSpec((1, S, CONV_COLS), lambda b, c: (b, 0, C_QK // CONV_COLS + c)),
                  pl.BlockSpec((8, CONV_COLS), lambda b, c: (0, c))],
        out_specs=pl.BlockSpec((1, S, CONV_COLS), lambda b, c: (b, 0, c)),
        out_shape=jax.ShapeDtypeStruct((B, S, 2 * MLSTM_WIDTH), F32),
        compiler_params=_cparams(("parallel", "parallel")),
    )(proj3, w8)


def _conv_bwd(proj3, w8, dout_f, dout_b):
    B, S, _ = proj3.shape
    ncb = 2 * MLSTM_WIDTH // CONV_COLS

    def body(u_ref, w_ref, df_ref, db_ref, du_ref, dw_ref):
        b = pl.program_id(1)
        u = u_ref[0]
        prev, nxt = _conv_taps(u, S)
        w0, w1, w2 = w_ref[0:1, :], w_ref[1:2, :], w_ref[2:3, :]
        pre = prev * w0 + u * w1 + nxt * w2 + w_ref[3:4, :]
        sig = _sigmoid(pre)
        dpre = (df_ref[0] + db_ref[0]) * (sig * (1.0 + pre * (1.0 - sig)))
        dprev, dnxt = _conv_taps(dpre, S)
        du_ref[0] = (dnxt * w0 + dpre * w1 + dprev * w2).astype(BF16)

        @pl.when(b == 0)
        def _():
            dw_ref[...] = jnp.zeros_like(dw_ref)

        dw_ref[0:1, :] += jnp.sum(dpre * prev, axis=0, keepdims=True)
        dw_ref[1:2, :] += jnp.sum(dpre * u, axis=0, keepdims=True)
        dw_ref[2:3, :] += jnp.sum(dpre * nxt, axis=0, keepdims=True)
        dw_ref[3:4, :] += jnp.sum(dpre, axis=0, keepdims=True)

    blk = pl.BlockSpec((1, S, CONV_COLS), lambda c, b: (b, 0, c))
    return pl.pallas_call(
        body, name="conv_bwd", grid=(ncb, B),
        in_specs=[pl.BlockSpec((1, S, CONV_COLS), lambda c, b: (b, 0, C_QK // CONV_COLS + c)),
                  pl.BlockSpec((8, CONV_COLS), lambda c, b: (0, c)), blk, blk],
        out_specs=[blk, pl.BlockSpec((8, CONV_COLS), lambda c, b: (0, c))],
        out_shape=[jax.ShapeDtypeStruct((B, S, 2 * MLSTM_WIDTH), BF16), jax.ShapeDtypeStruct((8, 2 * MLSTM_WIDTH), F32)],
        compiler_params=_cparams(("parallel", "arbitrary")),
    )(proj3, w8, dout_f, dout_b)


MLSTM_HEADS_PER_STEP = 4


def _chunk_masks(direction):
    t = lax.broadcasted_iota(jnp.int32, (MLSTM_CHUNK, MLSTM_CHUNK), 0)
    s = lax.broadcasted_iota(jnp.int32, (MLSTM_CHUNK, MLSTM_CHUNK), 1)
    le, ge = (s <= t).astype(F32), (s >= t).astype(F32)
    eye = (s == t).astype(F32)
    return (le, ge, eye) if direction == 0 else (ge, le, eye)


def _gate_cols(gates, direction, head):
    lane = lax.broadcasted_iota(jnp.int32, gates.shape, 1)
    sel_i = (lane == (2 * direction) * MLSTM_HEADS + head).astype(F32)
    sel_f = (lane == (2 * direction + 1) * MLSTM_HEADS + head).astype(F32)
    return sel_i, sel_f


def _mlstm_fwd(qk, proj3, bias):
    B, S, _ = qk.shape
    nc = S // MLSTM_CHUNK
    H, L, DH = MLSTM_HEADS, MLSTM_CHUNK, MLSTM_HEAD_DIM

    def chunk_of(d, c):
        return c if d == 0 else nc - 1 - c

    HS = MLSTM_HEADS_PER_STEP

    def body(qkf, qkb, vf, vb, gf, gb, bias_ref, hf, hb, csf, csb, nsf, nsb, msf, msb, c_st, n_st, m_st):
        c, hg = pl.program_id(1), pl.program_id(2)

        @pl.when(c == 0)
        def _():
            for d in range(2):
                for j in range(HS):
                    c_st[d, hg * HS + j] = jnp.zeros((DH, DH), F32)
                    n_st[d, hg * HS + j] = jnp.zeros((1, DH), F32)
                    m_st[d, hg * HS + j] = jnp.zeros((1, DH), F32)

        for d, (qk_ref, v_ref, g_ref, h_ref, cs, ns, ms) in enumerate(
                ((qkf, vf, gf, hf, csf, nsf, msf), (qkb, vb, gb, hb, csb, nsb, msb))):
            incl, incl_t, eye = _chunk_masks(d)
            gates = g_ref[0] + bias_ref[...]
            log_f = _log_sigmoid(gates)
            for j in range(HS):
                h = hg * HS + j
                sel_i, sel_f = _gate_cols(gates, d, h)
                li = jnp.sum(gates * sel_i, axis=1, keepdims=True)
                lf = jnp.sum(log_f * sel_f, axis=1, keepdims=True)
                c_in, n_in, m_in = c_st[d, h], n_st[d, h], m_st[d, h]
                cs[0, 0, j], ns[0, 0, j], ms[0, 0, j] = c_in, n_in, m_in
                hh, c_new, n_new, m_new = _mlstm_chunk(
                    qk_ref[0, :, 2 * j * DH:(2 * j + 1) * DH], qk_ref[0, :, (2 * j + 1) * DH:(2 * j + 2) * DH],
                    v_ref[0, :, j * DH:(j + 1) * DH], li, lf, c_in, n_in,
                    jnp.max(m_in, axis=1, keepdims=True), incl, incl_t, eye)
                h_ref[0, :, j * DH:(j + 1) * DH] = hh
                c_st[d, h], n_st[d, h] = c_new, n_new
                m_st[d, h] = jnp.broadcast_to(m_new, (1, DH))

    def tok_spec(width, base, d, per_head):
        return pl.BlockSpec((1, L, width), lambda b, c, h: (b, chunk_of(d, c), base + (h if per_head else 0)))

    def st_spec(shape, d):
        return pl.BlockSpec((1, 1, HS) + shape, lambda b, c, h: (b, chunk_of(d, c), h, 0, 0))

    in_specs = [tok_spec(2 * HS * DH, 0, 0, True), tok_spec(2 * HS * DH, 0, 1, True),
                tok_spec(HS * DH, C_VM // (HS * DH), 0, True), tok_spec(HS * DH, C_VM // (HS * DH), 1, True),
                tok_spec(LANES, C_GATES // LANES, 0, False), tok_spec(LANES, C_GATES // LANES, 1, False),
                pl.BlockSpec((1, LANES), lambda b, c, h: (0, 0))]
    out_specs = [tok_spec(HS * DH, 0, 0, True), tok_spec(HS * DH, 0, 1, True),
                 st_spec((DH, DH), 0), st_spec((DH, DH), 1), st_spec((1, DH), 0), st_spec((1, DH), 1),
                 st_spec((1, DH), 0), st_spec((1, DH), 1)]
    hs = jax.ShapeDtypeStruct((B, S, H * DH), F32)
    cs = jax.ShapeDtypeStruct((B, nc, H, DH, DH), F32)
    vs = jax.ShapeDtypeStruct((B, nc, H, 1, DH), F32)
    return pl.pallas_call(
        body, name="mlstm_fwd", grid=(B, nc, H // HS), in_specs=in_specs, out_specs=out_specs,
        out_shape=[hs, hs, cs, cs, vs, vs, vs, vs],
        scratch_shapes=[pltpu.VMEM((2, H, DH, DH), F32), pltpu.VMEM((2, H, 1, DH), F32), pltpu.VMEM((2, H, 1, DH), F32)],
        compiler_params=_cparams(("parallel", "arbitrary", "arbitrary")),
    )(qk, qk, proj3, proj3, proj3, proj3, bias)


def _mlstm_bwd(qk, proj3, bias, states, dh):
    B, S, _ = qk.shape
    nc = S // MLSTM_CHUNK
    H, L, DH = MLSTM_HEADS, MLSTM_CHUNK, MLSTM_HEAD_DIM

    def chunk_of(d, c):
        return nc - 1 - c if d == 0 else c

    HS = MLSTM_HEADS_PER_STEP

    def body(qkf, qkb, vf, vb, gf, gb, bias_ref, csf, csb, nsf, nsb, msf, msb, dhf, dhb,
             dqkf, dqkb, dvf, dvb, dgf, dgb, dc_st, dn_st, dm_st):
        c, hg = pl.program_id(1), pl.program_id(2)

        @pl.when(c == 0)
        def _():
            for d in range(2):
                for j in range(HS):
                    dc_st[d, hg * HS + j] = jnp.zeros((DH, DH), F32)
                    dn_st[d, hg * HS + j] = jnp.zeros((1, DH), F32)
                    dm_st[d, hg * HS + j] = jnp.zeros((1, DH), F32)

        @pl.when(hg == 0)
        def _():
            dgf[...] = jnp.zeros_like(dgf)
            dgb[...] = jnp.zeros_like(dgb)

        for d, (qk_ref, v_ref, g_ref, cs, ns, ms, dh_ref, dqk_ref, dv_ref, dg_ref) in enumerate(
                ((qkf, vf, gf, csf, nsf, msf, dhf, dqkf, dvf, dgf), (qkb, vb, gb, csb, nsb, msb, dhb, dqkb, dvb, dgb))):
            incl, incl_t, eye = _chunk_masks(d)
            gates = g_ref[0] + bias_ref[...]
            log_f = _log_sigmoid(gates)
            d_li = jnp.zeros_like(gates)
            d_lf = jnp.zeros_like(gates)
            for j in range(HS):
                h = hg * HS + j
                sel_i, sel_f = _gate_cols(gates, d, h)
                li = jnp.sum(gates * sel_i, axis=1, keepdims=True)
                lf = jnp.sum(log_f * sel_f, axis=1, keepdims=True)
                m_in = jnp.max(ms[0, 0, j], axis=1, keepdims=True)
                _, vjp = jax.vjp(
                    functools.partial(_mlstm_chunk, incl=incl, incl_t=incl_t, eye=eye),
                    qk_ref[0, :, 2 * j * DH:(2 * j + 1) * DH], qk_ref[0, :, (2 * j + 1) * DH:(2 * j + 2) * DH],
                    v_ref[0, :, j * DH:(j + 1) * DH], li, lf, cs[0, 0, j], ns[0, 0, j], m_in)
                dm_out = jnp.max(dm_st[d, h], axis=1, keepdims=True)
                dq, dk, dv, dli, dlf, dc, dn, dm = vjp((dh_ref[0, :, j * DH:(j + 1) * DH], dc_st[d, h], dn_st[d, h], dm_out))
                dqk_ref[0, :, 2 * j * DH:(2 * j + 1) * DH] = dq
                dqk_ref[0, :, (2 * j + 1) * DH:(2 * j + 2) * DH] = dk
                dv_ref[0, :, j * DH:(j + 1) * DH] = dv
                d_li += dli * sel_i
                d_lf += dlf * sel_f
                dc_st[d, h], dn_st[d, h] = dc, dn
                dm_st[d, h] = jnp.broadcast_to(dm, (1, DH))
            dg_ref[0] += d_li + d_lf * _sigmoid(-gates)

    def tok_spec(width, base, d, per_head):
        return pl.BlockSpec((1, L, width), lambda b, c, h: (b, chunk_of(d, c), base + (h if per_head else 0)))

    def st_spec(shape, d):
        return pl.BlockSpec((1, 1, HS) + shape, lambda b, c, h: (b, chunk_of(d, c), h, 0, 0))

    in_specs = [tok_spec(2 * HS * DH, 0, 0, True), tok_spec(2 * HS * DH, 0, 1, True),
                tok_spec(HS * DH, C_VM // (HS * DH), 0, True), tok_spec(HS * DH, C_VM // (HS * DH), 1, True),
                tok_spec(LANES, C_GATES // LANES, 0, False), tok_spec(LANES, C_GATES // LANES, 1, False),
                pl.BlockSpec((1, LANES), lambda b, c, h: (0, 0)),
                st_spec((DH, DH), 0), st_spec((DH, DH), 1), st_spec((1, DH), 0), st_spec((1, DH), 1),
                st_spec((1, DH), 0), st_spec((1, DH), 1), tok_spec(HS * DH, 0, 0, True), tok_spec(HS * DH, 0, 1, True)]
    out_specs = [tok_spec(2 * HS * DH, 0, 0, True), tok_spec(2 * HS * DH, 0, 1, True),
                 tok_spec(HS * DH, 0, 0, True), tok_spec(HS * DH, 0, 1, True),
                 tok_spec(LANES, 0, 0, False), tok_spec(LANES, 0, 1, False)]
    qks = jax.ShapeDtypeStruct((B, S, 2 * H * DH), F32)
    vs = jax.ShapeDtypeStruct((B, S, H * DH), F32)
    gs = jax.ShapeDtypeStruct((B, S, LANES), F32)
    csf, csb, nsf, nsb, msf, msb = states
    return pl.pallas_call(
        body, name="mlstm_bwd", grid=(B, nc, H // HS), in_specs=in_specs, out_specs=out_specs,
        out_shape=[qks, qks, vs, vs, gs, gs],
        scratch_shapes=[pltpu.VMEM((2, H, DH, DH), F32), pltpu.VMEM((2, H, 1, DH), F32), pltpu.VMEM((2, H, 1, DH), F32)],
        compiler_params=_cparams(("parallel", "arbitrary", "arbitrary")),
    )(qk, qk, proj3, proj3, proj3, proj3, bias, csf, csb, nsf, nsb, msf, msb, dh, dh)


ROW_BLOCK = 256
FF_COLS = 512
FF_SHARD = D_FF // N_DEV
FF_SHARD_PAD = 384
FF_PAD = N_DEV * FF_SHARD_PAD


def _rms_bwd(name, x, g, dh, dres):
    T = x.shape[0]

    def fn(xv, gv, dhv, drv):
        _, vjp = jax.vjp(_rms, xv, gv)
        dx, dg = vjp(dhv)
        return drv + dx, dg

    return _rowwise(name, fn, [_In(x), _In(g, rows=False), _In(dh), _In(dres)],
                    [_Out(D_MODEL), _Out(D_MODEL, rows=False)], T, ROW_BLOCK)


def _mmw(name, a, w, mode, **kw):
    if isinstance(w, tuple):
        return _matmul(name, a, w[0], mode, bl=w[1], **kw)
    return _matmul(name, a, w, mode, **kw)


def _swiglu(gate, up):
    return _silu(gate) * up


def _ffn_in(name, x, gain, wg, wu):
    (M, K), N = x.shape, wg.shape[0]
    tm, tn = _first_divisor(M, (1024, 512, 256, 128)), FF_COLS

    def body(x_ref, gain_ref, wg_ref, wu_ref, h_ref, g_ref, u_ref, a_ref):
        @pl.when(pl.program_id(1) == 0)
        def _():
            h_ref[...] = _rms(x_ref[...], gain_ref[...]).astype(BF16)

        hv = h_ref[...]
        gate = _mm(hv, wg_ref[...], _NT)
        up = _mm(hv, wu_ref[...], _NT)
        g_ref[...], u_ref[...] = gate.astype(BF16), up.astype(BF16)
        a_ref[...] = _swiglu(gate, up).astype(BF16)

    row_spec = pl.BlockSpec((tm, K), lambda i, j: (i, 0))
    w_spec = pl.BlockSpec((tn, K), lambda i, j: (j, 0))
    o_spec = pl.BlockSpec((tm, tn), lambda i, j: (i, j))
    return pl.pallas_call(
        body, name=name, grid=(M // tm, N // tn),
        in_specs=[row_spec, pl.BlockSpec((1, K), lambda i, j: (0, 0)), w_spec, w_spec],
        out_specs=[row_spec, o_spec, o_spec, o_spec],
        out_shape=[jax.ShapeDtypeStruct((M, K), BF16)] + [jax.ShapeDtypeStruct((M, N), BF16)] * 3,
        compiler_params=_cparams(("parallel", "arbitrary")),
    )(x, gain, wg, wu)


def _norm_matmul(name, x, gain, w):
    (M, K), N = x.shape, w.shape[1]
    tm = _first_divisor(M, (1024, 512, 256, 128))
    tn = _first_divisor(N, (1664, 1024, 512, 384, 256, 128))

    def body(x_ref, gain_ref, w_ref, h_ref, o_ref):
        @pl.when(pl.program_id(1) == 0)
        def _():
            h_ref[...] = _rms(x_ref[...], gain_ref[...]).astype(BF16)

        o_ref[...] = _mm(h_ref[...], w_ref[...], _NN)

    row_spec = pl.BlockSpec((tm, K), lambda i, j: (i, 0))
    return pl.pallas_call(
        body, name=name, grid=(M // tm, N // tn),
        in_specs=[row_spec, pl.BlockSpec((1, K), lambda i, j: (0, 0)), pl.BlockSpec((K, tn), lambda i, j: (0, j))],
        out_specs=[row_spec, pl.BlockSpec((tm, tn), lambda i, j: (i, j))],
        out_shape=[jax.ShapeDtypeStruct((M, K), BF16), jax.ShapeDtypeStruct((M, N), F32)],
        compiler_params=_cparams(("parallel", "arbitrary")),
    )(x, gain, w)


def _ffn_dact(name, dx, wd, gate, up):
    (M, K), N = dx.shape, wd.shape[0]
    tm, tn = _first_divisor(M, (1024, 512, 256, 128)), FF_COLS

    def body(dx_ref, wd_ref, g_ref, u_ref, dg_ref, du_ref):
        dact = _mm(dx_ref[...], wd_ref[...], _NT) * 0.5
        gate, up = g_ref[...].astype(F32), u_ref[...].astype(F32)
        s = _sigmoid(gate)
        silu = gate * s
        dg_ref[...] = (dact * up * (s + silu * (1.0 - s))).astype(BF16)
        du_ref[...] = (dact * silu).astype(BF16)

    o_spec = pl.BlockSpec((tm, tn), lambda i, j: (i, j))
    return pl.pallas_call(
        body, name=name, grid=(M // tm, N // tn),
        in_specs=[pl.BlockSpec((tm, K), lambda i, j: (i, 0)), pl.BlockSpec((tn, K), lambda i, j: (j, 0)), o_spec, o_spec],
        out_specs=[o_spec, o_spec],
        out_shape=[jax.ShapeDtypeStruct((M, N), BF16), jax.ShapeDtypeStruct((M, N), BF16)],
        compiler_params=_cparams(("parallel", "parallel")),
    )(dx, wd, gate, up)


def _ffn_dh(name, dgate, dup, wg, wu, dep, x, gain, dres):
    (M, K), N = dgate.shape, wg.shape[1]
    tm, tk = _first_divisor(M, (512, 256, 128)), _first_divisor(K, (1024, 512, 384, 256, 128))
    nk = K // tk

    def body(dg_ref, du_ref, wg_ref, wu_ref, x_ref, gain_ref, dres_ref, dep_ref, o_ref, dgain_ref, acc):
        i, k = pl.program_id(0), pl.program_id(1)

        @pl.when(k == 0)
        def _():
            acc[...] = jnp.zeros_like(acc)

        acc[...] += _mm(dg_ref[...], wg_ref[...], _NN) + _mm(du_ref[...], wu_ref[...], _NN)

        @pl.when((k == nk - 1) & (i == 0))
        def _():
            dgain_ref[...] = jnp.zeros_like(dgain_ref)

        @pl.when(k == nk - 1)
        def _():
            _, vjp = jax.vjp(_rms, x_ref[...], gain_ref[...])
            dx, dgain = vjp(acc[...])
            o_ref[...] = dres_ref[...] + dx
            dgain_ref[...] += dgain

    a_spec = pl.BlockSpec((tm, tk), lambda i, k: (i, k))
    w_spec = pl.BlockSpec((tk, N), lambda i, k: (k, 0))
    row_spec = pl.BlockSpec((tm, N), lambda i, k: (i, 0))
    gain_spec = pl.BlockSpec((1, N), lambda i, k: (0, 0))
    return pl.pallas_call(
        body, name=name, grid=(M // tm, nk),
        in_specs=[a_spec, a_spec, w_spec, w_spec, row_spec, gain_spec, row_spec, pl.BlockSpec(memory_space=pl.ANY)],
        out_specs=[row_spec, gain_spec],
        out_shape=[jax.ShapeDtypeStruct((M, N), F32), jax.ShapeDtypeStruct((1, N), F32)],
        scratch_shapes=[pltpu.VMEM((tm, N), F32)], compiler_params=_cparams(("arbitrary", "arbitrary")),
    )(dgate, dup, wg, wu, x, gain, dres, dep)


def _ffn_fwd(tag, x, g, wg, wu, wd):
    h, gate, up, act = _ffn_in(tag + "_in", x, g, wg, wu)
    if callable(wd):
        wd = wd(act)
    out = _mmw(tag + "_down", act, wd, "nn", res=x, scale=0.5, whole_k=True)
    return out, (x, h, gate, up, act), wd


def _ffn_bwd(tag, saved, g, wg, wu, wd, dx, on_dw):
    x, h, gate, up, act = saved
    dgate, dup = _ffn_dact(tag + "_dact", dx, wd, gate, up)
    dwd = _matmul(tag + "_dwd", act, dx, "tn", scale=0.5, out_dtype=BF16)
    dwg = _matmul(tag + "_dwg", dgate, h, "tn", out_dtype=BF16, whole_k=True)
    dwu = _matmul(tag + "_dwu", dup, h, "tn", out_dtype=BF16, whole_k=True)
    token = on_dw({tag + "_w_gate": dwg, tag + "_w_up": dwu, tag + "_w_down": dwd}, dwu)
    return _ffn_dh(tag + "_dh", dgate, dup, wg, wu, token, x, g, dx)


def _rope_cos_sin(positions):
    half = ROPE_DIM // 2
    inv_freq = jnp.power(jnp.float32(ROPE_THETA), -jnp.arange(half, dtype=F32) * (2.0 / ROPE_DIM))
    head = jnp.zeros((ATT_HEAD_DIM,), F32).at[:ROPE_DIM].set(jnp.concatenate([inv_freq, inv_freq]))
    row = jnp.tile(head, LANES // ATT_HEAD_DIM)[None, :]
    T = positions.shape[0]
    return _rowwise("rope_tables", _rope_tables, [_In(positions), _In(row, rows=False)], [_Out(LANES), _Out(LANES)], T, 1024)


def _prep_fwd(name, src, width, base, g, cos, sin):
    return _rowwise(name, _qk_prep, [_In(src, width, base), _In(g, rows=False), _In(cos), _In(sin)],
                    [_Out(width)], src.shape[0], 512)[0]


def _prep_bwd(name, src, width, base, g, cos, sin, dout, into=None):
    def fn(tv, gv, cv, sv, dv):
        _, vjp = jax.vjp(lambda a, b: _qk_prep(a, b, cv, sv), tv, gv)
        return vjp(dv)

    dsrc = _Out(width, BF16) if into is None else _Out(0, width=width, into=into, base=base)
    return _rowwise(name, fn, [_In(src, width, base), _In(g, rows=False), _In(cos), _In(sin), _In(dout)],
                    [dsrc, _Out(width, rows=False)], src.shape[0], 512)


def _mix_fwd(x, cos, sin, B, S, p):
    T = B * S
    h, proj = _norm_matmul("mix_proj", x, p["mix_norm"], p["w_in"])
    proj3 = proj.reshape(B, S, IN_PAD)
    q_gain = jnp.tile(p["attn_q_norm"], (1, ATT_HEADS))
    k_gain = jnp.tile(p["attn_k_norm"], (1, ATT_KV_HEADS))
    q_r = _prep_fwd("q_prep", proj, ATT_WIDTH, C_QA // ATT_WIDTH, q_gain, cos, sin)
    k_r = _prep_fwd("k_prep", proj, ATT_KV_WIDTH, C_KA // ATT_KV_WIDTH, k_gain, cos, sin)
    qh = q_r.reshape(B, S, ATT_WIDTH)
    kh = k_r.reshape(B, S, ATT_KV_WIDTH)
    sink = p["attn_sink"].reshape(ATT_KV_HEADS, ATT_GROUP, 1, 1)
    y_a = _attn_fwd(qh, kh, proj3, sink).reshape(T, ATT_WIDTH)

    qk_c = _conv_fwd(proj3, p["conv_w8"])
    hf, hb, *states = _mlstm_fwd(qk_c, proj3, p["gate_bias"])
    hf2, hb2 = hf.reshape(T, MLSTM_WIDTH), hb.reshape(T, MLSTM_WIDTH)
    DH = MLSTM_HEAD_DIM
    y_m = _rowwise("mlstm_out", _mlstm_combine,
                   [_In(hf2, DH, split=True), _In(hb2, DH, split=True), _In(proj, DH, C_OM // DH, split=True),
                    _In(p["mlstm_out_norm"], DH, split=True, rows=False)],
                   [_Out(MLSTM_WIDTH, BF16, DH, split=True)], T, 1024, ncol=MLSTM_HEADS)[0]

    za = _mmw("branch_a", y_a, p["w_branch_attn"], "nn")
    zm = _mmw("branch_m", y_m, p["w_branch_mlstm"], "nn")
    W = 512
    merged = _rowwise("merge", _merge,
                      [_In(proj, W, C_GMERGE // W, split=True), _In(proj, W, (C_GMERGE + D_MODEL) // W, split=True),
                       _In(za, W, split=True), _In(zm, W, split=True)],
                      [_Out(D_MODEL, BF16, W, split=True)], T, 512, ncol=D_MODEL // W)[0]
    out = _mmw("mix_out", merged, p["w_out"], "nn", res=x)
    saved = dict(x=x, h=h, proj=proj, q_gain=q_gain, k_gain=k_gain, qh=qh, kh=kh, sink=sink, y_a=y_a, qk_c=qk_c,
                 hf=hf2, hb=hb2, states=states, y_m=y_m, za=za, zm=zm, merged=merged)
    return out, saved


def _mix_bwd(sv, cos, sin, B, S, p, dx, on_dw):
    T = B * S
    DH = MLSTM_HEAD_DIM
    proj = sv["proj"]
    proj3 = proj.reshape(B, S, IN_PAD)
    g = {}
    dmerged = _mmw("mix_dmerged", dx, p["w_out"], "nt")
    g["w_out"] = _matmul("mix_dwout", sv["merged"], dx, "tn", out_dtype=BF16)
    dproj = lax.empty((T, IN_PAD), BF16)

    def merge_bwd(ga, gm, za, zm, dm):
        _, vjp = jax.vjp(_merge, ga, gm, za, zm)
        dga, dgm, dza, dzm = vjp(dm)
        return jnp.concatenate([dga, dgm], axis=1), dza, dzm

    dproj, dza, dzm = _rowwise(
        "merge_bwd", merge_bwd,
        [_In(proj, D_MODEL, C_GMERGE // D_MODEL), _In(proj, D_MODEL, C_GMERGE // D_MODEL + 1),
         _In(sv["za"]), _In(sv["zm"]), _In(dmerged)],
        [_Out(0, width=2 * D_MODEL, into=dproj, base=C_GMERGE // (2 * D_MODEL)), _Out(D_MODEL, BF16), _Out(D_MODEL, BF16)],
        T, ROW_BLOCK)
    dya = _mmw("branch_a_dx", dza, p["w_branch_attn"], "nt")
    g["w_branch_attn"] = _matmul("branch_a_dw", sv["y_a"], dza, "tn", out_dtype=BF16)
    dym = _mmw("branch_m_dx", dzm, p["w_branch_mlstm"], "nt")
    g["w_branch_mlstm"] = _matmul("branch_m_dw", sv["y_m"], dzm, "tn", out_dtype=BF16)

    def combine_bwd(hf, hb, o_pre, gn, dy):
        _, vjp = jax.vjp(_mlstm_combine, hf, hb, o_pre, gn)
        dhf, _, do, dg = vjp(dy)
        return dhf, do, dg

    dh, dproj, g["mlstm_out_norm"] = _rowwise(
        "mlstm_out_bwd", combine_bwd,
        [_In(sv["hf"], DH, split=True), _In(sv["hb"], DH, split=True), _In(proj, DH, C_OM // DH, split=True),
         _In(p["mlstm_out_norm"], DH, split=True, rows=False), _In(dym, DH, split=True)],
        [_Out(MLSTM_WIDTH, F32, DH, split=True), _Out(0, width=DH, split=True, into=dproj, base=C_OM // DH),
         _Out(MLSTM_WIDTH, F32, DH, split=True, rows=False)], T, 1024, ncol=MLSTM_HEADS)
    dqk_f, dqk_b, dv_f, dv_b, dg_f, dg_b = _mlstm_bwd(sv["qk_c"], proj3, p["gate_bias"], sv["states"],
                                                       dh.reshape(B, S, MLSTM_WIDTH))
    dproj, g["gate_bias"] = _rowwise(
        "mlstm_dsum_gates", lambda a, b: (a + b, jnp.sum(a + b, axis=0, keepdims=True)),
        [_In(dg_f.reshape(T, LANES)), _In(dg_b.reshape(T, LANES))],
        [_Out(0, width=LANES, into=dproj, base=C_GATES // LANES), _Out(LANES, rows=False)], T, 1024)
    dproj = _rowwise(
        "mlstm_dsum_v", lambda a, b: a + b, [_In(dv_f.reshape(T, MLSTM_WIDTH)), _In(dv_b.reshape(T, MLSTM_WIDTH))],
        [_Out(0, width=MLSTM_WIDTH, into=dproj, base=C_VM // MLSTM_WIDTH)], T, 1024)[0]
    dqk, g["conv_w8"] = _conv_bwd(proj3, p["conv_w8"], dqk_f, dqk_b)

    dqh, dkh, dvh, dsink = _attn_bwd(sv["qh"], sv["kh"], proj3, sv["sink"], dya.reshape(B, S, ATT_WIDTH))
    g["attn_sink"] = dsink.reshape(1, ATT_HEADS)
    dva = dvh.reshape(T, ATT_KV_WIDTH)
    dproj, dq_gain = _prep_bwd("q_prep_bwd", proj, ATT_WIDTH, C_QA // ATT_WIDTH, sv["q_gain"], cos, sin,
                               dqh.reshape(T, ATT_WIDTH), into=dproj)
    dka, dk_gain = _prep_bwd("k_prep_bwd", proj, ATT_KV_WIDTH, C_KA // ATT_KV_WIDTH, sv["k_gain"], cos, sin,
                             dkh.reshape(T, ATT_KV_WIDTH))
    g["attn_q_norm"] = jnp.sum(dq_gain.reshape(ATT_HEADS, ATT_HEAD_DIM), axis=0, keepdims=True)
    g["attn_k_norm"] = jnp.sum(dk_gain.reshape(ATT_KV_HEADS, ATT_HEAD_DIM), axis=0, keepdims=True)

    dproj = dproj.at[:, C_QK:C_QK + 2 * MLSTM_WIDTH].set(dqk.reshape(T, 2 * MLSTM_WIDTH))
    dproj = dproj.at[:, C_KA:C_KA + ATT_KV_WIDTH].set(dka)
    dproj = dproj.at[:, C_VA:C_VA + ATT_KV_WIDTH].set(dva.astype(BF16))
    dwin = _matmul("mix_dwin", sv["h"], dproj, "tn", out_dtype=BF16)
    token = on_dw({"w_in": _w_in_to_slots(dwin), "w_branch_attn": g.pop("w_branch_attn"),
                   "w_branch_mlstm": g.pop("w_branch_mlstm"), "w_out": g.pop("w_out")}, dwin)
    dh2 = _matmul("mix_dh", dproj, p["w_in"], "nt", dep=token)
    dx_new, g["mix_norm"] = _rms_bwd("mix_dnorm", sv["x"], p["mix_norm"], dh2, dx)
    return dx_new, g


def _loss_and_grad(x, g, target):
    T = x.shape[0]

    def loss_fn(xv, gv, tv):
        err = jnp.square(_rms(xv, gv) - tv)
        return 0.5 * jnp.sum(jnp.mean(err, axis=-1, keepdims=True), axis=0, keepdims=True)

    def fn(xv, gv, tv):
        val, vjp = jax.vjp(lambda a, b: loss_fn(a, b, tv), xv, gv)
        dx, dg = vjp(jnp.ones((1, 1), F32))
        return val, dx, dg

    return _rowwise("loss_head", fn, [_In(x), _In(g, rows=False), _In(target)],
                    [_Out(1, rows=False), _Out(D_MODEL), _Out(D_MODEL, rows=False)], T, ROW_BLOCK)


def _block_norm_fwd(x, g):
    T = x.shape[0]
    return _rowwise("block_norm", _rms, [_In(x), _In(g, rows=False)], [_Out(D_MODEL)], T, ROW_BLOCK)[0]


def _block_norm_bwd(x, g, dy):
    T = x.shape[0]

    def fn(xv, gv, dv):
        _, vjp = jax.vjp(_rms, xv, gv)
        return vjp(dv)

    return _rowwise("block_norm_bwd", fn, [_In(x), _In(g, rows=False), _In(dy)],
                    [_Out(D_MODEL), _Out(D_MODEL, rows=False)], T, ROW_BLOCK)


def _qk_perm_cols(t, axis):
    q, k = jnp.split(t, 2, axis=axis)
    parts = []
    for h in range(MLSTM_HEADS):
        sl = [slice(None)] * t.ndim
        sl[axis] = slice(h * MLSTM_HEAD_DIM, (h + 1) * MLSTM_HEAD_DIM)
        parts += [q[tuple(sl)], k[tuple(sl)]]
    return jnp.concatenate(parts, axis=axis)


def _qk_unperm_cols(t, axis):
    qs, ks = [], []
    for h in range(MLSTM_HEADS):
        sl = [slice(None)] * t.ndim
        sl[axis] = slice(2 * h * MLSTM_HEAD_DIM, (2 * h + 1) * MLSTM_HEAD_DIM)
        qs.append(t[tuple(sl)])
        sl[axis] = slice((2 * h + 1) * MLSTM_HEAD_DIM, (2 * h + 2) * MLSTM_HEAD_DIM)
        ks.append(t[tuple(sl)])
    return jnp.concatenate(qs + ks, axis=axis)


def _w_in_arrange(w):
    qa, ka, va, qm, km, vm, om, gm, gmerge = jnp.split(w, np.cumsum(
        (ATT_WIDTH, ATT_KV_WIDTH, ATT_KV_WIDTH, MLSTM_WIDTH, MLSTM_WIDTH, MLSTM_WIDTH, MLSTM_WIDTH, MLSTM_N_GATES))[:].tolist(), axis=1)
    qk = _qk_perm_cols(jnp.concatenate([qm, km], axis=1), 1)
    pad = jnp.zeros((w.shape[0], LANES - MLSTM_N_GATES), w.dtype)
    return jnp.concatenate([gmerge, qk, vm, om, qa, ka, va, gm, pad], axis=1)


def _w_in_restore(w):
    gmerge = w[:, C_GMERGE:C_GMERGE + 2 * D_MODEL]
    qk = _qk_unperm_cols(w[:, C_QK:C_QK + 2 * MLSTM_WIDTH], 1)
    vm, om = w[:, C_VM:C_VM + MLSTM_WIDTH], w[:, C_OM:C_OM + MLSTM_WIDTH]
    qa, ka, va = w[:, C_QA:C_QA + ATT_WIDTH], w[:, C_KA:C_KA + ATT_KV_WIDTH], w[:, C_VA:C_VA + ATT_KV_WIDTH]
    gm = w[:, C_GATES:C_GATES + MLSTM_N_GATES]
    return jnp.concatenate([qa, ka, va, qk, vm, om, gm, gmerge], axis=1)


BIG = ("ffn1_w_gate", "ffn1_w_up", "ffn1_w_down", "w_in", "mlstm_conv_w", "w_branch_attn", "w_branch_mlstm", "w_out",
       "ffn2_w_gate", "ffn2_w_up", "ffn2_w_down")
MATMUL_W = tuple(n for n in BIG if n != "mlstm_conv_w")
SMALL = ("ffn1_norm", "mix_norm", "mlstm_gate_bias", "attn_q_norm", "attn_k_norm", "attn_sink", "mlstm_conv_b",
         "mlstm_out_norm", "ffn2_norm", "block_out_norm")
WEIGHTS = ("ffn1_norm", "ffn1_w_gate", "ffn1_w_up", "ffn1_w_down", "mix_norm", "w_in", "mlstm_gate_bias", "attn_q_norm",
           "attn_k_norm", "attn_sink", "mlstm_conv_w", "mlstm_conv_b", "mlstm_out_norm", "w_branch_attn", "w_branch_mlstm",
           "w_out", "ffn2_norm", "ffn2_w_gate", "ffn2_w_up", "ffn2_w_down", "block_out_norm")
PACK_COLS = 1024


def _padded_rows(n_elems):
    return -(-n_elems // PACK_COLS)


def _pack_flat(arrs, dtype, row_multiple):
    parts = []
    for a in arrs:
        flat = a.reshape(-1).astype(dtype)
        pad = _padded_rows(flat.shape[0]) * PACK_COLS - flat.shape[0]
        parts.append(jnp.pad(flat, (0, pad)) if pad else flat)
    flat = jnp.concatenate(parts)
    rows = flat.shape[0] // PACK_COLS
    extra = (-rows) % row_multiple
    if extra:
        flat = jnp.pad(flat, (0, extra * PACK_COLS))
    return flat.reshape(-1, PACK_COLS)


def _pack_rows(name, pieces, total_rows):
    def body(*refs):
        o_ref = refs[-1]
        o_ref[...] = jnp.zeros_like(o_ref)
        at = 0
        for ref, (arr, r0, nr) in zip(refs[:-1], pieces):
            o_ref[at:at + nr, 0:arr.shape[1]] = ref[r0:r0 + nr, :].astype(F32)
            at += nr

    return pl.pallas_call(body, name=name, out_shape=jax.ShapeDtypeStruct((total_rows, PACK_COLS), F32))(
        *[p[0] for p in pieces])


def _unpack_flat(buf, shapes, lead=()):
    flat = buf.reshape(lead + (-1,))
    out, off = [], 0
    for s in shapes:
        n = int(np.prod(s))
        out.append(flat[..., off:off + n].reshape(lead + tuple(s)))
        off += _padded_rows(n) * PACK_COLS
    return out


class _Lay:
    def __init__(self, shard, axis, width):
        self.shard, self.axis, self.width = shard, axis, width
        self.padded = tuple(width if a == axis else s for a, s in enumerate(shard))
        self.whole = tuple(N_DEV * width if a == axis else s for a, s in enumerate(shard))


_FF_ROW = _Lay((FF_SHARD, D_MODEL), 0, FF_SHARD_PAD)
TRANSPOSED = ("ffn1_w_gate", "ffn1_w_up", "ffn2_w_gate", "ffn2_w_up")
LAYOUTS = {
    "ffn1_w_gate": _FF_ROW, "ffn1_w_up": _FF_ROW, "ffn1_w_down": _FF_ROW,
    "ffn2_w_gate": _FF_ROW, "ffn2_w_up": _FF_ROW, "ffn2_w_down": _FF_ROW,
    "w_in": _Lay((D_MODEL, IN_WIDTH // N_DEV), 0, D_MODEL),
    "mlstm_conv_w": _Lay((3, 2 * MLSTM_WIDTH // N_DEV), 1, 2 * MLSTM_WIDTH // N_DEV),
    "w_branch_attn": _Lay((ATT_WIDTH, D_MODEL // N_DEV), 1, D_MODEL // N_DEV),
    "w_branch_mlstm": _Lay((MLSTM_WIDTH, D_MODEL // N_DEV), 1, D_MODEL // N_DEV),
    "w_out": _Lay((D_MODEL // N_DEV, D_MODEL), 0, D_MODEL // N_DEV),
}


def _window(ref, axis, j, width):
    idx = [slice(None)] * len(ref.shape)
    idx[axis] = pl.ds(pl.multiple_of(j * width, width), width)
    return ref.at[tuple(idx)]


ANY = pl.BlockSpec(memory_space=pl.ANY)


def _mesh_pos():
    return lax.axis_index("x"), lax.axis_index("y"), lax.axis_index("c")


def _all_gather(name, shard, vmem=False):
    R, C = shard.shape
    space = pl.BlockSpec(memory_space=pltpu.VMEM) if vmem else ANY

    def body(x_ref, out_ref, send_sems, recv_sems, local_sem):
        x, y, c = _mesh_pos()
        me, sibling = (x, y, c), (x, y, 1 - c)
        chips = [(1 - x, y), (x, 1 - y), (1 - x, 1 - y)]

        def slot(px, py, pc):
            return out_ref.at[4 * px + 2 * py + pc]

        def copy(k, block, to, src=None):
            return pltpu.make_async_remote_copy(
                src_ref=slot(*block) if src is None else src, dst_ref=slot(*block),
                send_sem=send_sems.at[k], recv_sem=recv_sems.at[k], device_id=to, device_id_type=MESH)

        mine = pltpu.make_async_copy(x_ref, slot(*me), local_sem)
        mine.start()
        first = [copy(0, me, sibling, src=x_ref)]
        first += [copy(1 + j, me, (*chip, c), src=x_ref) for j, chip in enumerate(chips)]
        for cp in first:
            cp.start()
        passed = [copy(4 + j, (*chip, c), sibling) for j, chip in enumerate(chips)]
        for j, chip in enumerate(chips):
            copy(1 + j, (*chip, c), me).wait_recv()
            passed[j].start()
        copy(0, sibling, me).wait_recv()
        for j, chip in enumerate(chips):
            copy(4 + j, (*chip, 1 - c), me).wait_recv()
        for cp in first + passed:
            cp.wait_send()
        mine.wait()

    return pl.pallas_call(
        body, name=name, out_shape=jax.ShapeDtypeStruct((N_DEV, R, C), shard.dtype),
        in_specs=[space], out_specs=space,
        scratch_shapes=[pltpu.SemaphoreType.DMA((7,)), pltpu.SemaphoreType.DMA((7,)), pltpu.SemaphoreType.DMA],
    )(shard)


HBM = pl.BlockSpec(memory_space=pltpu.HBM)
SEM = pl.BlockSpec(memory_space=pltpu.SEMAPHORE)
SPLIT_COPY = pltpu.CompilerParams(has_side_effects=pltpu.SideEffectType.DATAFLOW_SIDE_EFFECTING)
N_PEERS = N_DEV - 1


def _peers(x, y, c):
    return [(x, y, 1 - c), (1 - x, y, c), (x, 1 - y, c), (1 - x, 1 - y, c),
            (1 - x, y, 1 - c), (x, 1 - y, 1 - c), (1 - x, 1 - y, 1 - c)]


def _dev_index(pos):
    return 4 * pos[0] + 2 * pos[1] + pos[2]


def _place_own(name, stacks, layer, lays):
    nt = len(stacks)
    me = _dev_index(_mesh_pos())

    def body(me_ref, *refs):
        for x_ref, s_ref, o_ref, lay in zip(refs[:nt], refs[nt:2 * nt], refs[2 * nt:], lays):
            rows = lay.shard[0]
            if lay.padded != lay.shard:
                s_ref[...] = jnp.zeros_like(s_ref)
            s_ref[0:rows, :] = x_ref[...].astype(BF16)
            o_ref[...] = s_ref[...]

    def window_spec(lay):
        if lay.axis == 0:
            return pl.BlockSpec(lay.padded, lambda i, me_ref: (me_ref[0], 0))
        return pl.BlockSpec(lay.padded, lambda i, me_ref: (0, me_ref[0]))

    for lay in lays:
        assert lay.padded[1] == lay.shard[1], "only rows are padded"
    res = pl.pallas_call(
        body, name=name,
        grid_spec=pltpu.PrefetchScalarGridSpec(
            num_scalar_prefetch=1, grid=(1,),
            in_specs=[pl.BlockSpec((None,) + lay.shard, lambda i, me_ref: (layer, 0, 0)) for lay in lays],
            out_specs=[pl.BlockSpec(lay.padded, lambda i, me_ref: (0, 0)) for lay in lays] + [window_spec(lay) for lay in lays]),
        out_shape=[jax.ShapeDtypeStruct(lay.padded, BF16) for lay in lays] + [jax.ShapeDtypeStruct(lay.whole, BF16) for lay in lays],
        compiler_params=_cparams(("arbitrary",)),
    )(me.reshape(1).astype(jnp.int32), *stacks)
    return list(res[:nt]), list(res[nt:])


NEAR_PEERS = 4


def _gather_start(name, shards, lands, lays, groups, n_peers, after):
    nt, ng = len(shards), len(groups)

    def body(*refs):
        x_refs, land_refs = refs[:nt], refs[nt:2 * nt]
        sems = refs[2 * nt + 1:2 * nt + 1 + 2 * ng]
        pos = _mesh_pos()
        me = _dev_index(pos)
        for g, tens in enumerate(groups):
            for i, t in enumerate(tens):
                for k, peer in enumerate(_peers(*pos)[:n_peers[g]]):
                    pltpu.make_async_remote_copy(
                        src_ref=x_refs[t], dst_ref=_window(land_refs[t], lays[t].axis, me, lays[t].width),
                        send_sem=sems[2 * g].at[n_peers[g] * i + k], recv_sem=sems[2 * g + 1].at[n_peers[g] * i + k],
                        device_id=peer, device_id_type=MESH).start()

    sem_shapes = []
    for g, tens in enumerate(groups):
        sem_shapes += [pltpu.SemaphoreType.DMA((n_peers[g] * len(tens),))] * 2
    thru = [pltpu.HBM(s.shape, s.dtype) for s in shards] + [pltpu.HBM(lay.whole, s.dtype) for s, lay in zip(shards, lays)]
    args = [pltpu.with_memory_space_constraint(s, pltpu.HBM) for s in shards]
    args += [pltpu.with_memory_space_constraint(ld, pltpu.HBM) for ld in lands]
    res = pl.pallas_call(
        body, name=name, out_shape=tuple(sem_shapes + thru), in_specs=[HBM] * (2 * nt) + [ANY],
        out_specs=tuple([SEM] * (2 * ng) + [HBM] * (2 * nt)),
        input_output_aliases={t: 2 * ng + t for t in range(2 * nt)}, compiler_params=SPLIT_COPY,
    )(*args, after)
    sems = [(res[2 * g], res[2 * g + 1]) for g in range(ng)]
    return sems, list(res[2 * ng:2 * ng + nt]), list(res[2 * ng + nt:])


def _gather_wait(name, sems, shards, lands, lays, n_peers, after):
    nt = len(shards)
    send_sems, recv_sems = sems

    def body(*refs):
        x_refs, land_refs = refs[:nt], refs[nt:2 * nt]
        send_ref, recv_ref = refs[2 * nt], refs[2 * nt + 1]
        pos = _mesh_pos()
        for t in range(nt):
            for k, peer in enumerate(_peers(*pos)[:n_peers]):
                cp = pltpu.make_async_remote_copy(
                    src_ref=x_refs[t], dst_ref=_window(land_refs[t], lays[t].axis, _dev_index(peer), lays[t].width),
                    send_sem=send_ref.at[n_peers * t + k], recv_sem=recv_ref.at[n_peers * t + k],
                    device_id=peer, device_id_type=MESH)
                cp.wait_send()
                cp.wait_recv()

    thru = [pltpu.HBM(s.shape, s.dtype) for s in shards] + [pltpu.HBM(ld.shape, ld.dtype) for ld in lands]
    res = pl.pallas_call(
        body, name=name, out_shape=tuple(thru), in_specs=[HBM] * (2 * nt) + [SEM, SEM, ANY],
        out_specs=tuple([HBM] * (2 * nt)), input_output_aliases={t: t for t in range(2 * nt)},
        compiler_params=SPLIT_COPY,
    )(*shards, *lands, send_sems, recv_sems, after)
    return list(res[nt:])


def _forward_to_sibling(name, lands, lays):
    nt = len(lands)

    def body(*refs):
        land_refs = refs[nt:2 * nt]
        send_sems, recv_sems = refs[2 * nt:]
        x, y, c = _mesh_pos()
        chips = [(1 - x, y), (x, 1 - y), (1 - x, 1 - y)]

        def copy(t, j, core):
            win = _window(land_refs[t], lays[t].axis, _dev_index((*chips[j], core)), lays[t].width)
            return pltpu.make_async_remote_copy(
                src_ref=win, dst_ref=win, send_sem=send_sems.at[3 * t + j], recv_sem=recv_sems.at[3 * t + j],
                device_id=(x, y, 1 - c), device_id_type=MESH)

        sends = [copy(t, j, c) for t in range(nt) for j in range(3)]
        for cp in sends:
            cp.start()
        for t in range(nt):
            for j in range(3):
                copy(t, j, 1 - c).wait_recv()
        for cp in sends:
            cp.wait_send()

    return pl.pallas_call(
        body, name=name, out_shape=[jax.ShapeDtypeStruct(ld.shape, ld.dtype) for ld in lands],
        in_specs=[ANY] * nt, out_specs=[ANY] * nt, input_output_aliases={t: t for t in range(nt)},
        scratch_shapes=[pltpu.SemaphoreType.DMA((3 * nt,)), pltpu.SemaphoreType.DMA((3 * nt,))],
    )(*lands)


def _pair_exchange(name, grads, lays):
    nt = len(grads)

    def body(*refs):
        g_refs, land_refs = refs[:nt], refs[nt:2 * nt]
        send_sems, recv_sems = refs[2 * nt:]
        x, y, c = _mesh_pos()
        copies = []
        for t in range(nt):
            for chip in range(4):
                copies.append(pltpu.make_async_remote_copy(
                    src_ref=_window(g_refs[t], lays[t].axis, 2 * chip + (1 - c), lays[t].width), dst_ref=land_refs[t].at[chip],
                    send_sem=send_sems.at[4 * t + chip], recv_sem=recv_sems.at[4 * t + chip],
                    device_id=(x, y, 1 - c), device_id_type=MESH))
        for cp in copies:
            cp.start()
        for cp in copies:
            cp.wait_recv()
        for cp in copies:
            cp.wait_send()

    out_shape = [jax.ShapeDtypeStruct((4,) + lay.padded, g.dtype) for g, lay in zip(grads, lays)]
    return pl.pallas_call(
        body, name=name, out_shape=out_shape, in_specs=[ANY] * nt, out_specs=[ANY] * nt,
        scratch_shapes=[pltpu.SemaphoreType.DMA((4 * nt,)), pltpu.SemaphoreType.DMA((4 * nt,))],
    )(*grads)


def _pair_sum(name, whole, landed, lay, out_dtype):
    R, C = lay.padded
    br = _first_divisor(R, (512, 384, 256, 128, 64, 32, 16, 8))
    nb = R // br
    if lay.axis == 0:
        mine_spec = pl.BlockSpec((br, C), lambda k, i, c_ref: ((2 * k + c_ref[0]) * nb + i, 0))
    else:
        mine_spec = pl.BlockSpec((br, C), lambda k, i, c_ref: (i, 2 * k + c_ref[0]))

    def body(c_ref, mine_ref, sib_ref, o_ref):
        o_ref[0] = (mine_ref[...].astype(F32) + sib_ref[0].astype(F32)).astype(out_dtype)

    c = lax.axis_index("c")
    return pl.pallas_call(
        body, name=name,
        grid_spec=pltpu.PrefetchScalarGridSpec(
            num_scalar_prefetch=1, grid=(4, nb),
            in_specs=[mine_spec, pl.BlockSpec((1, br, C), lambda k, i, c_ref: (k, i, 0))],
            out_specs=pl.BlockSpec((1, br, C), lambda k, i, c_ref: (k, i, 0))),
        out_shape=jax.ShapeDtypeStruct((4, R, C), out_dtype),
        compiler_params=_cparams(("parallel", "parallel")),
    )(c.reshape(1).astype(jnp.int32), whole, landed)


def _chip_start(name, sums):
    nt = len(sums)

    def body(*refs):
        s_refs, land_refs = refs[:nt], refs[nt:2 * nt]
        send_sems, recv_sems = refs[2 * nt], refs[2 * nt + 1]
        x, y, c = _mesh_pos()
        my_chip = 2 * x + y
        for t in range(nt):
            for j, (px, py) in enumerate([(1 - x, y), (x, 1 - y), (1 - x, 1 - y)]):
                pltpu.make_async_remote_copy(
                    src_ref=s_refs[t].at[2 * px + py], dst_ref=land_refs[t].at[my_chip],
                    send_sem=send_sems.at[3 * t + j], recv_sem=recv_sems.at[3 * t + j],
                    device_id=(px, py, c), device_id_type=MESH).start()

    thru = [pltpu.HBM(s.shape, s.dtype) for s in sums] * 2
    args = [pltpu.with_memory_space_constraint(s, pltpu.HBM) for s in sums]
    args += [pltpu.with_memory_space_constraint(lax.empty(s.shape, s.dtype), pltpu.HBM) for s in sums]
    res = pl.pallas_call(
        body, name=name, out_shape=tuple([pltpu.SemaphoreType.DMA((3 * nt,))] * 2 + thru), in_specs=[HBM] * (2 * nt),
        out_specs=tuple([SEM, SEM] + [HBM] * (2 * nt)), input_output_aliases={t: 2 + t for t in range(2 * nt)},
        compiler_params=SPLIT_COPY,
    )(*args)
    return (res[0], res[1]), list(res[2:2 + nt]), list(res[2 + nt:])


def _chip_wait(name, sems, sums, lands, after):
    nt = len(sums)

    def body(*refs):
        s_refs, land_refs = refs[:nt], refs[nt:2 * nt]
        send_sems, recv_sems = refs[2 * nt], refs[2 * nt + 1]
        x, y, c = _mesh_pos()
        my_chip = 2 * x + y
        for t in range(nt):
            for j, (px, py) in enumerate([(1 - x, y), (x, 1 - y), (1 - x, 1 - y)]):
                cp = pltpu.make_async_remote_copy(
                    src_ref=s_refs[t].at[my_chip], dst_ref=land_refs[t].at[2 * px + py],
                    send_sem=send_sems.at[3 * t + j], recv_sem=recv_sems.at[3 * t + j],
                    device_id=(px, py, c), device_id_type=MESH)
                cp.wait_send()
                cp.wait_recv()

    thru = [pltpu.HBM(s.shape, s.dtype) for s in sums] * 2
    res = pl.pallas_call(
        body, name=name, out_shape=tuple(thru), in_specs=[HBM] * (2 * nt) + [SEM, SEM, ANY],
        out_specs=tuple([HBM] * (2 * nt)), input_output_aliases={t: t for t in range(2 * nt)},
        compiler_params=SPLIT_COPY,
    )(*sums, *lands, sems[0], sems[1], after)
    return list(res[:nt]), list(res[nt:])


def _sum_chips(name, own, landed):
    _, R, C = own.shape
    br = _first_divisor(R, (512, 384, 256, 128, 64, 32, 16, 8))
    x, y, _ = _mesh_pos()
    slots = jnp.stack([2 * x + y, 2 * (1 - x) + y, 2 * x + (1 - y), 2 * (1 - x) + (1 - y)]).astype(jnp.int32)

    def body(slot_ref, mine_ref, a_ref, b_ref, c_ref, o_ref):
        o_ref[...] = ((mine_ref[0].astype(F32) + a_ref[0].astype(F32)) + b_ref[0].astype(F32)) + c_ref[0].astype(F32)

    def slot_spec(j):
        return pl.BlockSpec((1, br, C), lambda i, slot_ref: (slot_ref[j], i, 0))

    return pl.pallas_call(
        body, name=name,
        grid_spec=pltpu.PrefetchScalarGridSpec(
            num_scalar_prefetch=1, grid=(R // br,), in_specs=[slot_spec(0), slot_spec(1), slot_spec(2), slot_spec(3)],
            out_specs=pl.BlockSpec((br, C), lambda i, slot_ref: (i, 0))),
        out_shape=jax.ShapeDtypeStruct((R, C), F32), compiler_params=_cparams(("parallel",)),
    )(slots, own, landed, landed, landed)


def _sum_slots(name, slots, n):
    _, R, C = slots.shape
    br = _first_divisor(R, (512, 384, 256, 128, 64, 32, 16, 8))

    def body(s_ref, o_ref):
        acc = s_ref[0].astype(F32)
        for k in range(1, n):
            acc = acc + s_ref[k].astype(F32)
        o_ref[...] = acc

    return pl.pallas_call(
        body, name=name, grid=(R // br,), in_specs=[pl.BlockSpec((n, br, C), lambda i: (0, i, 0))],
        out_specs=pl.BlockSpec((br, C), lambda i: (i, 0)), out_shape=jax.ShapeDtypeStruct((R, C), F32),
        compiler_params=_cparams(("parallel",)),
    )(slots)


def _reduce_scatter_start(tag, names, grads):
    lays = [LAYOUTS[n] for n in names]
    landed = _pair_exchange("grads_pair_" + names[0], grads, lays)
    sums = [_pair_sum("grads_pairsum_" + n, g, ld, lay, BF16) for n, g, ld, lay in zip(names, grads, landed, lays)]
    sems, sums, lands = _chip_start(tag + "_chips_start", sums)
    return tag, names, sems, sums, lands


def _reduce_scatter_finish(pending, after):
    tag, names, sems, sums, lands = pending
    own, got = _chip_wait(tag + "_chips_wait", sems, sums, lands, after)
    return [_sum_chips("grads_sum_" + n, o, s) for n, o, s in zip(names, own, got)]


def _adamw_math(w, g, m, v):
    m = ADAM_B1 * m + (1.0 - ADAM_B1) * g
    v = ADAM_B2 * v + (1.0 - ADAM_B2) * jnp.square(g)
    m_hat = m / (1.0 - ADAM_B1 ** ADAM_STEP)
    v_hat = v / (1.0 - ADAM_B2 ** ADAM_STEP)
    delta = -ADAM_LR * (m_hat / (jnp.sqrt(v_hat) + ADAM_EPS) + ADAM_WD * w)
    return delta, m, v


def _adamw_layers(name, w, totals, m, v):
    _, R, C = w.shape
    br = _first_divisor(R, (512, 176, 128, 64, 32, 16, 8))
    Cp = totals[0].shape[1]

    def body(w_ref, g0_ref, g1_ref, m_ref, v_ref, g_out, d_out, m_out, v_out):
        g = jnp.where(pl.program_id(0) == 0, g0_ref[:, 0:C], g1_ref[:, 0:C])
        delta, m_new, v_new = _adamw_math(w_ref[0], g, m_ref[0], v_ref[0])
        g_out[0], d_out[0], m_out[0], v_out[0] = g, delta, m_new, v_new

    blk = pl.BlockSpec((1, br, C), lambda l, i: (l, i, 0))
    g_spec = pl.BlockSpec((br, Cp), lambda l, i: (i, 0))
    return pl.pallas_call(
        body, name=name, grid=(DEPTH, R // br), in_specs=[blk, g_spec, g_spec, blk, blk], out_specs=[blk] * 4,
        out_shape=[jax.ShapeDtypeStruct(w.shape, F32)] * 4, compiler_params=_cparams(("parallel", "parallel")),
    )(w, totals[0], totals[1], m, v)


def _adamw(name, w, g, m, v):
    shape = w.shape
    cols = shape[-1]
    rows = int(np.prod(shape[:-1]))
    br = _first_divisor(rows, (512, 352, 256, 128, 64, 32, 16, 8))
    args = [_In(a.reshape(rows, cols)) for a in (w, g, m, v)]
    outs = _rowwise(name, _adamw_math, args, [_Out(cols), _Out(cols), _Out(cols)], rows, br)
    return [o.reshape(shape) for o in outs]


GROUPS = {"ffn1": ("ffn1_w_gate", "ffn1_w_up", "ffn1_w_down"),
          "mix": ("w_in", "w_branch_attn", "w_branch_mlstm", "w_out"),
          "ffn2": ("ffn2_w_gate", "ffn2_w_up", "ffn2_w_down")}
GATHER_GROUPS = {"ffn1_in": ("ffn1_w_gate", "ffn1_w_up"), "ffn1_out": ("ffn1_w_down",),
                 "mix": ("w_in", "w_branch_attn", "w_branch_mlstm", "w_out"),
                 "ffn2_in": ("ffn2_w_gate", "ffn2_w_up"), "ffn2_out": ("ffn2_w_down",)}


def _small_params(small, conv_w, l):
    p = {}
    for n in ("ffn1_norm", "mix_norm", "ffn2_norm", "block_out_norm", "mlstm_out_norm", "attn_q_norm", "attn_k_norm"):
        p[n] = small[n][l][None, :]
    p["attn_sink"] = small["attn_sink"][l]
    p["gate_bias"] = jnp.pad(small["mlstm_gate_bias"][l], (0, LANES - MLSTM_N_GATES))[None, :]
    taps = _qk_perm_cols(conv_w[l], 1)
    conv_b = _qk_perm_cols(small["mlstm_conv_b"][l][None, :], 1)
    p["conv_w8"] = jnp.concatenate([taps, conv_b, jnp.zeros((4, 2 * MLSTM_WIDTH), F32)], axis=0)
    return p


def _w_in_from_slots(slots):
    w_in = slots.reshape(N_DEV, D_MODEL, IN_WIDTH // N_DEV).transpose(1, 0, 2).reshape(D_MODEL, IN_WIDTH)
    return _w_in_arrange(w_in)


def _w_in_to_slots(g):
    return _w_in_restore(g).reshape(D_MODEL, N_DEV, IN_WIDTH // N_DEV).transpose(1, 0, 2).reshape(
        N_DEV * D_MODEL, IN_WIDTH // N_DEV)


def _local_step(x, positions, target, weights_of, small, conv_w, on_grads):
    B, S, _ = x.shape
    T = B * S
    cos, sin = _rope_cos_sin(positions.reshape(T, 1))
    params = [_small_params(small, conv_w, l) for l in range(DEPTH)]
    xs = x.reshape(T, D_MODEL)
    tgt = target.reshape(T, D_MODEL)

    saved = []
    for l, p in enumerate(params):
        p.update(weights_of(l, "ffn1_in", xs))
        x1, s1, p["ffn1_w_down"] = _ffn_fwd("ffn1", xs, p["ffn1_norm"], p["ffn1_w_gate"], p["ffn1_w_up"],
                                            lambda after, l=l: weights_of(l, "ffn1_out", after)["ffn1_w_down"])
        p.update(weights_of(l, "mix", x1))
        p["w_in"] = _w_in_from_slots(p["w_in"])
        x2, s2 = _mix_fwd(x1, cos, sin, B, S, p)
        p.update(weights_of(l, "ffn2_in", x2))
        x3, s3, p["ffn2_w_down"] = _ffn_fwd("ffn2", x2, p["ffn2_norm"], p["ffn2_w_gate"], p["ffn2_w_up"],
                                            lambda after, l=l: weights_of(l, "ffn2_out", after)["ffn2_w_down"])
        saved.append((s1, s2, s3, x3))
        if l + 1 < DEPTH:
            xs = _block_norm_fwd(x3, p["block_out_norm"])

    sm = {}
    loss = None
    dx = None
    for l in reversed(range(DEPTH)):
        p = params[l]
        s1, s2, s3, x3 = saved[l]
        if l == DEPTH - 1:
            loss, dx, dgn = _loss_and_grad(x3, p["block_out_norm"], tgt)
        else:
            dx, dgn = _block_norm_bwd(x3, p["block_out_norm"], dx)
        sm["block_out_norm", l] = (dgn, 0, 1)
        dx, dg = _ffn_bwd("ffn2", s3, p["ffn2_norm"], p["ffn2_w_gate"], p["ffn2_w_up"], p["ffn2_w_down"], dx,
                          functools.partial(on_grads, l, "ffn2"))
        sm["ffn2_norm", l] = (dg, 0, 1)
        dx, g = _mix_bwd(s2, cos, sin, B, S, p, dx, functools.partial(on_grads, l, "mix"))
        dconv = _qk_unperm_cols(g["conv_w8"], 1)
        sm["mlstm_conv_w", l] = (dconv, 0, 3)
        sm["mlstm_conv_b", l] = (dconv, 3, 1)
        for n, key in (("mix_norm", "mix_norm"), ("mlstm_gate_bias", "gate_bias"), ("attn_q_norm", "attn_q_norm"),
                       ("attn_k_norm", "attn_k_norm"), ("attn_sink", "attn_sink"), ("mlstm_out_norm", "mlstm_out_norm")):
            sm[n, l] = (g[key], 0, 1)
        dx, dg = _ffn_bwd("ffn1", s1, p["ffn1_norm"], p["ffn1_w_gate"], p["ffn1_w_up"], p["ffn1_w_down"], dx,
                          functools.partial(on_grads, l, "ffn1"))
        sm["ffn1_norm", l] = (dg, 0, 1)
    return loss, dx.reshape(B, S, D_MODEL), sm


def kernel(x, positions, ffn1_norm, ffn1_w_gate, ffn1_w_up, ffn1_w_down, mix_norm, w_in, mlstm_gate_bias, attn_q_norm, attn_k_norm, attn_sink, mlstm_conv_w, mlstm_conv_b, mlstm_out_norm, w_branch_attn, w_branch_mlstm, w_out, ffn2_norm, ffn2_w_gate, ffn2_w_up, ffn2_w_down, block_out_norm, loss_target, m_ffn1_norm, m_ffn1_w_gate, m_ffn1_w_up, m_ffn1_w_down, m_mix_norm, m_w_in, m_mlstm_gate_bias, m_attn_q_norm, m_attn_k_norm, m_attn_sink, m_mlstm_conv_w, m_mlstm_conv_b, m_mlstm_out_norm, m_w_branch_attn, m_w_branch_mlstm, m_w_out, m_ffn2_norm, m_ffn2_w_gate, m_ffn2_w_up, m_ffn2_w_down, m_block_out_norm, v_ffn1_norm, v_ffn1_w_gate, v_ffn1_w_up, v_ffn1_w_down, v_mix_norm, v_w_in, v_mlstm_gate_bias, v_attn_q_norm, v_attn_k_norm, v_attn_sink, v_mlstm_conv_w, v_mlstm_conv_b, v_mlstm_out_norm, v_w_branch_attn, v_w_branch_mlstm, v_w_out, v_ffn2_norm, v_ffn2_w_gate, v_ffn2_w_up, v_ffn2_w_down, v_block_out_norm):
    args = locals()
    def stored(n, t):
        return t.transpose(0, 2, 1) if n in TRANSPOSED else t

    w = {n: stored(n, args[n]) for n in WEIGHTS}
    m = {n: stored(n, args["m_" + n]) for n in WEIGHTS}
    v = {n: stored(n, args["v_" + n]) for n in WEIGHTS}

    order = [(l, grp) for l in range(DEPTH) for grp in GATHER_GROUPS]
    keys = [(l, n) for l, grp in order for n in GATHER_GROUPS[grp]]
    lays = [LAYOUTS[n] for _, n in keys]
    group_idx, at = {}, 0
    for l, grp in order:
        group_idx[(l, grp)] = list(range(at, at + len(GATHER_GROUPS[grp])))
        at += len(GATHER_GROUPS[grp])
    conv_shape = w["mlstm_conv_w"].shape
    conv_all = _all_gather("conv_all_gather", _pack_flat([w["mlstm_conv_w"]], F32, 8), vmem=True)
    conv_parts = _unpack_flat(conv_all, [conv_shape], lead=(N_DEV,))[0]
    conv_w = jnp.concatenate([conv_parts[j] for j in range(N_DEV)], axis=2)
    small = {n: w[n] for n in SMALL}

    shards, lands = [], []
    for l, grp in order:
        own, whole = _place_own("weights_place_" + grp, [w[n] for n in GATHER_GROUPS[grp]], l,
                                [lays[i] for i in group_idx[(l, grp)]])
        shards += own
        lands += whole
    n_peers = [NEAR_PEERS if (l, grp) in ((0, "ffn1_in"), (0, "ffn1_out"), (0, "mix")) else N_PEERS for l, grp in order]
    sems, shards, lands = _gather_start("weights_gather_start", shards, lands, lays, [group_idx[k] for k in order],
                                        n_peers, conv_all)

    def weights_of(l, grp, after):
        idx, g = group_idx[(l, grp)], order.index((l, grp))
        group_lays = [lays[i] for i in idx]
        whole = _gather_wait(f"weights_gather_wait_{l}_{grp}", sems[g], [shards[i] for i in idx],
                             [lands[i] for i in idx], group_lays, n_peers[g], after)
        if n_peers[g] == NEAR_PEERS:
            whole = _forward_to_sibling("weights_forward_" + grp, whole, group_lays)
        return dict(zip(GATHER_GROUPS[grp], whole))

    totals, pending = {}, []

    def finish(after):
        tag, names = pending[0][0], pending[0][1]
        for n, t in zip(names, _reduce_scatter_finish(pending.pop(0), after)):
            totals[(tag, n)] = t

    def on_grads(l, grp, g, after):
        if pending:
            finish(after)
        names = GROUPS[grp]
        pending.append(_reduce_scatter_start(f"grads_{l}_{grp}", names, [g[n] for n in names]))
        return pending[-1][3][0]

    loss, grad_x, small_g = _local_step(x, positions, loss_target, weights_of, small, conv_w, on_grads)
    finish(grad_x)
    grads, deltas, new_m, new_v = {}, {}, {}, {}
    for grp, names in GROUPS.items():
        for n in names:
            grads[n], deltas[n], new_m[n], new_v[n] = _adamw_layers(
                "adamw_" + n, w[n], [totals[(f"grads_{l}_{grp}", n)] for l in range(DEPTH)], m[n], v[n])

    n_small = DEPTH * len(SMALL)
    taps_at, loss_at, rows = n_small, n_small + 3 * DEPTH, 32
    pieces = [small_g[n, l] for n in SMALL for l in range(DEPTH)]
    pieces += [small_g["mlstm_conv_w", l] for l in range(DEPTH)] + [(loss, 0, 1)]
    small_all = _all_gather("small_all_gather", _pack_rows("small_pack", pieces, rows), vmem=True)
    small_sum = _sum_slots("small_sum", small_all, N_DEV)
    loss_total = small_sum[loss_at, 0]
    x_pos, y_pos, c_pos = _mesh_pos()
    grads["mlstm_conv_w"] = lax.dynamic_slice_in_dim(
        small_sum[taps_at:loss_at].reshape(DEPTH, 3, 2 * MLSTM_WIDTH),
        (4 * x_pos + 2 * y_pos + c_pos) * conv_shape[2], conv_shape[2], axis=2)

    n = "mlstm_conv_w"
    deltas[n], new_m[n], new_v[n] = _adamw("adamw_" + n, w[n], grads[n], m[n], v[n])
    sw, smm, sv = (_pack_rows("small_pack_" + tag, [(d[n], 0, DEPTH) for n in SMALL], rows)
                   for tag, d in (("w", w), ("m", m), ("v", v)))
    sd, snm, snv = _adamw("adamw_small", sw, small_sum, smm, sv)
    for i, n in enumerate(SMALL):
        rows_n, width = slice(DEPTH * i, DEPTH * (i + 1)), w[n].shape[1]
        grads[n], deltas[n], new_m[n], new_v[n] = (buf[rows_n, :width] for buf in (small_sum, sd, snm, snv))

    return (loss_total.reshape(()), grad_x, *[stored(n, d[n]) for d in (grads, deltas, new_m, new_v) for n in WEIGHTS])
```

```python
import functools

import numpy as np
import jax
import jax.numpy as jnp
from jax import lax
from jax.experimental import pallas as pl
from jax.experimental.pallas import tpu as pltpu

F32 = jnp.float32
BF16 = jnp.bfloat16

D_MODEL = 1024
D_FF = 2816
ATT_HEAD_DIM = 64
ATT_HEADS = 8
ATT_KV_HEADS = 2
ATT_GROUP = ATT_HEADS // ATT_KV_HEADS
ATT_WIDTH = ATT_HEADS * ATT_HEAD_DIM
ATT_KV_WIDTH = ATT_KV_HEADS * ATT_HEAD_DIM
WINDOW = 128
ATT_BLOCK = 128
ROPE_DIM = 16
ROPE_THETA = 500000.0
MLSTM_HEADS = 4
MLSTM_HEAD_DIM = 128
MLSTM_WIDTH = MLSTM_HEADS * MLSTM_HEAD_DIM
MLSTM_CHUNK = 128
MLSTM_N_GATES = 4 * MLSTM_HEADS
NORM_EPS = 1e-6
IN_WIDTH = 4880
DEPTH = 2
N_DEV = 8

ADAM_LR = 0.001
ADAM_B1 = 0.9
ADAM_B2 = 0.999
ADAM_EPS = 1e-08
ADAM_WD = 0.01
ADAM_STEP = 10

LANES = 128
C_GMERGE = 0
C_QK = 2048
C_VM = 3072
C_OM = 3584
C_QA = 4096
C_KA = 4608
C_VA = 4736
C_GATES = 4864
IN_PAD = 4992

VMEM_LIMIT = 48 * 1024 * 1024

MESH = pl.DeviceIdType.MESH


def _cparams(sem):
    return pltpu.CompilerParams(dimension_semantics=sem, vmem_limit_bytes=VMEM_LIMIT)


def _first_divisor(n, cands):
    for c in cands:
        if n % c == 0:
            return c
    return n


_NN = ((1,), (0,))
_NT = ((1,), (1,))
_TN = ((0,), (0,))


def _mm(a, b, dims):
    return lax.dot_general(a.astype(BF16), b.astype(BF16), (dims, ((), ())), preferred_element_type=F32)


@jax.custom_vjp
def mm_nn(a, b):
    return _mm(a, b, _NN)


def _mm_nn_fwd(a, b):
    return _mm(a, b, _NN), (a, b)


def _mm_nn_bwd(res, g):
    a, b = res
    return _mm(g, b, _NT).astype(a.dtype), _mm(a, g, _TN).astype(b.dtype)


mm_nn.defvjp(_mm_nn_fwd, _mm_nn_bwd)


@jax.custom_vjp
def mm_nt(a, b):
    return _mm(a, b, _NT)


def _mm_nt_fwd(a, b):
    return _mm(a, b, _NT), (a, b)


def _mm_nt_bwd(res, g):
    a, b = res
    return _mm(g, b, _NN).astype(a.dtype), _mm(g, a, _TN).astype(b.dtype)


mm_nt.defvjp(_mm_nt_fwd, _mm_nt_bwd)


@jax.custom_vjp
def mm_tn(a, b):
    return _mm(a, b, _TN)


def _mm_tn_fwd(a, b):
    return _mm(a, b, _TN), (a, b)


def _mm_tn_bwd(res, g):
    a, b = res
    return _mm(b, g, _NT).astype(a.dtype), _mm(a, g, _NN).astype(b.dtype)


mm_tn.defvjp(_mm_tn_fwd, _mm_tn_bwd)


def _matmul(name, a, b, mode, out_dtype=F32, res=None, scale=1.0, bl=None, dep=None, whole_k=False):
    b_shape = b.shape if bl is None else b.shape[1:]
    if mode == "nn":
        (M, K), (K2, N) = a.shape, b_shape
    elif mode == "nt":
        (M, K), (N, K2) = a.shape, b_shape
    else:
        (K, M), (K2, N) = a.shape, b_shape
    assert K == K2, (name, a.shape, b.shape)
    tm = _first_divisor(M, (1024, 512, 384, 256, 128))
    tn = _first_divisor(N, (1024, 1664, 512, 384, 256, 128))
    tk = K if whole_k else _first_divisor(K, (1024, 1664, 512, 256, 128))
    if whole_k:
        tn = min(tn, 512)
    nk = K // tk
    if mode == "tn":
        a_spec = pl.BlockSpec((tk, tm), lambda i, j, k: (k, i))
    else:
        a_spec = pl.BlockSpec((tm, tk), lambda i, j, k: (i, k))
    if mode == "nt":
        b_blk, b_idx = (tn, tk), (lambda i, j, k: (j, k))
    else:
        b_blk, b_idx = (tk, tn), (lambda i, j, k: (k, j))
    if bl is None:
        b_spec = pl.BlockSpec(b_blk, b_idx)
    else:
        b_spec = pl.BlockSpec((None,) + b_blk, lambda i, j, k: (bl,) + b_idx(i, j, k))
    o_spec = pl.BlockSpec((tm, tn), lambda i, j, k: (i, j))
    dims = {"nn": _NN, "nt": _NT, "tn": _TN}[mode]
    has_res = res is not None

    def body(*refs):
        a_ref, b_ref = refs[:2]
        r_ref = refs[2] if has_res else None

        def finish(out):
            if scale != 1.0:
                out = out * scale
            if has_res:
                out = r_ref[...].astype(F32) + out
            o_ref[...] = out.astype(out_dtype)

        if nk == 1:
            o_ref = refs[-1]
            finish(_mm(a_ref[...], b_ref[...], dims))
            return
        o_ref, acc = refs[-2:]
        k = pl.program_id(2)

        @pl.when(k == 0)
        def _():
            acc[...] = jnp.zeros_like(acc)

        acc[...] += _mm(a_ref[...], b_ref[...], dims)

        @pl.when(k == nk - 1)
        def _():
            finish(acc[...])

    in_specs = [a_spec, b_spec] + ([o_spec] if has_res else [])
    args = (a, b) + ((res,) if has_res else ())
    if dep is not None:
        in_specs.append(pl.BlockSpec(memory_space=pl.ANY))
        args += (dep,)
    return pl.pallas_call(
        body, name=name, grid=(M // tm, N // tn, nk), in_specs=in_specs, out_specs=o_spec,
        out_shape=jax.ShapeDtypeStruct((M, N), out_dtype),
        scratch_shapes=[pltpu.VMEM((tm, tn), F32)] if nk > 1 else [],
        compiler_params=_cparams(("parallel", "parallel", "arbitrary")),
    )(*args)


def _in_hbm(arr):
    return pltpu.with_memory_space_constraint(arr, pltpu.HBM)


class _In:
    def __init__(self, arr, width=None, base=0, split=False, rows=True):
        self.arr, self.base, self.split, self.rows = arr, base, split, rows
        self.width = arr.shape[1] if width is None else width


class _Out:
    def __init__(self, cols, dtype=F32, width=None, split=False, rows=True, nrows=1, into=None, base=0):
        self.cols, self.dtype, self.split, self.rows, self.nrows = cols, dtype, split, rows, nrows
        self.width = cols if width is None else width
        self.into, self.base = into, base
        if into is not None:
            self.cols, self.dtype = into.shape[1], into.dtype


def _rowwise(name, fn, ins, outs, n_rows, br, ncol=1):
    br = min(br, n_rows)
    assert n_rows % br == 0, (name, n_rows, br)
    nrow_blocks = n_rows // br

    def in_spec(d):
        nb = br if d.rows else d.arr.shape[0]
        if d.rows and d.split:
            im = lambda j, i, base=d.base: (i, base + j)
        elif d.rows:
            im = lambda j, i, base=d.base: (i, base)
        elif d.split:
            im = lambda j, i, base=d.base: (0, base + j)
        else:
            im = lambda j, i, base=d.base: (0, base)
        return pl.BlockSpec((nb, d.width), im)

    def out_spec(d):
        nb = br if d.rows else d.nrows
        if d.rows and d.split:
            im = lambda j, i, base=d.base: (i, base + j)
        elif d.rows:
            im = lambda j, i, base=d.base: (i, base)
        elif d.split:
            im = lambda j, i: (0, j)
        else:
            im = lambda j, i: (0, 0)
        return pl.BlockSpec((nb, d.width), im)

    n_in = len(ins)
    targets = [(k, d.into) for k, d in enumerate(outs) if d.into is not None]

    def body(*refs):
        i = pl.program_id(1)
        vals = [r[...] for r in refs[:n_in]]
        res = fn(*vals)
        if not isinstance(res, (tuple, list)):
            res = (res,)
        for d, ref, val in zip(outs, refs[n_in + len(targets):], res):
            if d.rows:
                ref[...] = val.astype(d.dtype)
            else:
                @pl.when(i == 0)
                def _(ref=ref):
                    ref[...] = jnp.zeros_like(ref)

                ref[...] += val.astype(d.dtype)

    out_shape = [jax.ShapeDtypeStruct((n_rows if d.rows else d.nrows, d.cols), d.dtype) for d in outs]
    res = pl.pallas_call(
        body, name=name, grid=(ncol, nrow_blocks),
        in_specs=[in_spec(d) for d in ins] + [pl.BlockSpec(memory_space=pl.ANY)] * len(targets),
        out_specs=[out_spec(d) for d in outs], out_shape=out_shape,
        input_output_aliases={n_in + t: k for t, (k, _) in enumerate(targets)},
        compiler_params=_cparams(("parallel", "arbitrary")),
    )(*[_in_hbm(d.arr) for d in ins], *[arr for _, arr in targets])
    return res


def _rms(x, g):
    return x * lax.rsqrt(jnp.mean(x * x, axis=-1, keepdims=True) + NORM_EPS) * g


def _sigmoid(x):
    return 0.5 * jnp.tanh(0.5 * x) + 0.5


def _silu(x):
    return x * _sigmoid(x)


def _log_sigmoid(x):
    return jnp.minimum(x, 0.0) - jnp.log(1.0 + jnp.exp(-jnp.abs(x)))


def _rope_tables(pos, inv_freq_row):
    ang = pos.astype(F32) * inv_freq_row
    return jnp.cos(ang), jnp.sin(ang)


def _head_sums_impl(v):
    w = v.shape[-1]
    shift = ATT_HEAD_DIM.bit_length() - 1
    r = lax.shift_right_logical(lax.broadcasted_iota(jnp.int32, (w, w), 0), shift)
    c = lax.shift_right_logical(lax.broadcasted_iota(jnp.int32, (w, w), 1), shift)
    ones = (r == c).astype(BF16)
    hi = v.astype(BF16)
    lo = (v - hi.astype(F32)).astype(BF16)
    dn = (_NN, ((), ()))
    return (lax.dot_general(hi, ones, dn, preferred_element_type=F32)
            + lax.dot_general(lo, ones, dn, preferred_element_type=F32))


@jax.custom_vjp
def _head_sums(v):
    return _head_sums_impl(v)


_head_sums.defvjp(lambda v: (_head_sums_impl(v), None), lambda _, g: (_head_sums_impl(g),))


def _rotate_half_impl(y):
    w = y.shape[-1]
    half = ROPE_DIM // 2
    lane = lax.broadcasted_iota(jnp.int32, y.shape, 1) & (ATT_HEAD_DIM - 1)
    above = pltpu.roll(y, w - half, axis=1)
    below = pltpu.roll(y, half, axis=1)
    return jnp.where(lane < half, -above, jnp.where(lane < ROPE_DIM, below, 0.0))


@jax.custom_vjp
def _rotate_half(y):
    return _rotate_half_impl(y)


_rotate_half.defvjp(lambda y: (_rotate_half_impl(y), None), lambda _, g: (-_rotate_half_impl(g),))


def _qk_prep(t, g, cos, sin):
    reps = t.shape[-1] // cos.shape[-1]
    if reps > 1:
        cos, sin = jnp.tile(cos, (1, reps)), jnp.tile(sin, (1, reps))
    y = t * lax.rsqrt(_head_sums(t * t) * (1.0 / ATT_HEAD_DIM) + NORM_EPS) * g
    return y * cos + _rotate_half(y) * sin


def _attn_head(q, kb, vb, sink, valid):
    s = mm_nt(q, kb) * (ATT_HEAD_DIM ** -0.5)
    s = jnp.where(valid, s, -jnp.inf)
    m = jnp.maximum(jnp.max(s, axis=-1, keepdims=True), sink)
    p = jnp.exp(s - m)
    den = jnp.sum(p, axis=-1, keepdims=True) + jnp.exp(sink - m)
    return mm_nn(p * (1.0 / den), vb)


def _mlstm_chunk(q, k, v, li, lf, C, n, m, incl, incl_t, eye):
    k = k * (MLSTM_HEAD_DIM ** -0.5)
    lf_row = jnp.sum(eye * lf, axis=0, keepdims=True)
    li_row = jnp.sum(eye * li, axis=0, keepdims=True)
    b = jnp.sum(incl * lf_row, axis=1, keepdims=True)
    b_row = jnp.sum(incl_t * lf, axis=0, keepdims=True)
    b_tot = jnp.sum(lf, axis=0, keepdims=True)
    a = b_tot - b + li
    a_max = jnp.max(a, axis=0, keepdims=True)
    kw = k * jnp.exp(a - a_max)
    c_loc = mm_tn(kw, v)
    n_loc = jnp.sum(kw, axis=0, keepdims=True)

    dmat = jnp.where(incl > 0.5, b - b_row + li_row, -jnp.inf)
    inter = b + m
    m_t = jnp.maximum(inter, jnp.max(dmat, axis=1, keepdims=True))
    sc = mm_nt(q, k) * jnp.exp(dmat - m_t)
    scale_in = jnp.exp(inter - m_t)
    num = mm_nn(sc, v) + scale_in * mm_nn(q, C)
    den = jnp.sum(sc, axis=1, keepdims=True) + scale_in * jnp.sum(q * n, axis=1, keepdims=True)
    h = num * (1.0 / jnp.maximum(jnp.abs(den), jnp.exp(-m_t)))

    m_new = jnp.maximum(b_tot + m, a_max)
    s_p = jnp.exp(b_tot + m - m_new)
    s_l = jnp.exp(a_max - m_new)
    return h, s_p * C + s_l * c_loc, s_p * n + s_l * n_loc, m_new


def _mlstm_combine(hf, hb, o_pre, g):
    h = hf + hb
    mu = jnp.mean(h, axis=-1, keepdims=True)
    var = jnp.mean(jnp.square(h - mu), axis=-1, keepdims=True)
    return _sigmoid(o_pre) * ((h - mu) * lax.rsqrt(var + NORM_EPS) * g)


def _merge(ga, gm, za, zm):
    return _sigmoid(ga) * za + _sigmoid(gm) * zm


def _attn_mask(n, seq):
    shape = (ATT_GROUP * ATT_BLOCK, 3 * ATT_BLOCK)
    qi = n * ATT_BLOCK + (lax.broadcasted_iota(jnp.int32, shape, 0) & (ATT_BLOCK - 1))
    kj = (n - 1) * ATT_BLOCK + lax.broadcasted_iota(jnp.int32, shape, 1)
    return (jnp.abs(qi - kj) <= WINDOW) & (kj >= 0) & (kj < seq)


def _attn_specs(nq, v_base):
    q_spec = pl.BlockSpec((1, ATT_BLOCK, ATT_WIDTH), lambda b, n: (b, n, 0))

    def kv_spec(off, base=0):
        return pl.BlockSpec((1, ATT_BLOCK, ATT_KV_WIDTH), lambda b, n: (b, jnp.clip(n + off, 0, nq - 1), base))

    sink_spec = pl.BlockSpec((ATT_KV_HEADS, ATT_GROUP, 1, 1), lambda b, n: (0, 0, 0, 0))
    specs = [q_spec, kv_spec(-1), kv_spec(0), kv_spec(1), kv_spec(-1, v_base), kv_spec(0, v_base), kv_spec(1, v_base), sink_spec]
    return q_spec, specs, sink_spec


def _head(h):
    return slice(h * ATT_HEAD_DIM, (h + 1) * ATT_HEAD_DIM)


def _group_rows(q_ref, s_ref, h):
    q4 = jnp.concatenate([q_ref[0, :, _head(h * ATT_GROUP + g)] for g in range(ATT_GROUP)], axis=0)
    sink4 = jnp.concatenate([jnp.broadcast_to(s_ref[h, g], (ATT_BLOCK, 1)) for g in range(ATT_GROUP)], axis=0)
    return q4, sink4


def _attn_fwd(q, k, proj3, sink):
    B, S, _ = q.shape
    nq = S // ATT_BLOCK
    q_spec, specs, _ = _attn_specs(nq, C_VA // ATT_KV_WIDTH)

    def body(q_ref, kp, kc, kn, vp, vc, vn, s_ref, o_ref):
        valid = _attn_mask(pl.program_id(1), S)
        for h in range(ATT_KV_HEADS):
            kb = jnp.concatenate([kp[0, :, _head(h)], kc[0, :, _head(h)], kn[0, :, _head(h)]], axis=0)
            vb = jnp.concatenate([vp[0, :, _head(h)], vc[0, :, _head(h)], vn[0, :, _head(h)]], axis=0)
            q4, sink4 = _group_rows(q_ref, s_ref, h)
            o4 = _attn_head(q4, kb, vb, sink4, valid).astype(BF16)
            for g in range(ATT_GROUP):
                o_ref[0, :, _head(h * ATT_GROUP + g)] = o4[g * ATT_BLOCK:(g + 1) * ATT_BLOCK]

    return pl.pallas_call(
        body, name="attn_fwd", grid=(B, nq), in_specs=specs,
        out_specs=q_spec, out_shape=jax.ShapeDtypeStruct(q.shape, BF16),
        compiler_params=_cparams(("parallel", "arbitrary")),
    )(q, k, k, k, proj3, proj3, proj3, sink)


def _attn_bwd(q, k, proj3, sink, dy):
    B, S, _ = q.shape
    nq = S // ATT_BLOCK
    q_spec, specs, sink_spec = _attn_specs(nq, C_VA // ATT_KV_WIDTH)
    kv_full = pl.BlockSpec((1, S, ATT_KV_WIDTH), lambda b, n: (b, 0, 0))

    def body(q_ref, kp, kc, kn, vp, vc, vn, s_ref, dy_ref, dq_ref, dk_ref, dv_ref, ds_ref):
        b, n = pl.program_id(0), pl.program_id(1)
        valid = _attn_mask(n, S)

        @pl.when(n == 0)
        def _():
            dk_ref[...] = jnp.zeros_like(dk_ref)
            dv_ref[...] = jnp.zeros_like(dv_ref)

        @pl.when((n == 0) & (b == 0))
        def _():
            ds_ref[...] = jnp.zeros_like(ds_ref)

        for h in range(ATT_KV_HEADS):
            kb = jnp.concatenate([kp[0, :, _head(h)], kc[0, :, _head(h)], kn[0, :, _head(h)]], axis=0)
            vb = jnp.concatenate([vp[0, :, _head(h)], vc[0, :, _head(h)], vn[0, :, _head(h)]], axis=0)
            q4, sink4 = _group_rows(q_ref, s_ref, h)
            dy4 = jnp.concatenate([dy_ref[0, :, _head(h * ATT_GROUP + g)] for g in range(ATT_GROUP)], axis=0)
            _, vjp = jax.vjp(functools.partial(_attn_head, valid=valid), q4, kb, vb, sink4)
            dq4, dkb, dvb, dsink4 = vjp(dy4)
            for g in range(ATT_GROUP):
                rows = slice(g * ATT_BLOCK, (g + 1) * ATT_BLOCK)
                dq_ref[0, :, _head(h * ATT_GROUP + g)] = dq4[rows]
                ds_ref[h, g] += jnp.sum(dsink4[rows], axis=0, keepdims=True)
            for j, off in enumerate((-1, 0, 1)):
                start = pl.multiple_of(jnp.clip(n + off, 0, nq - 1) * ATT_BLOCK, ATT_BLOCK)
                rows = pl.ds(start, ATT_BLOCK)
                dk_ref[0, rows, _head(h)] += dkb[j * ATT_BLOCK:(j + 1) * ATT_BLOCK]
                dv_ref[0, rows, _head(h)] += dvb[j * ATT_BLOCK:(j + 1) * ATT_BLOCK]

    kv_shape = jax.ShapeDtypeStruct(k.shape, F32)
    return pl.pallas_call(
        body, name="attn_bwd", grid=(B, nq), in_specs=specs + [q_spec],
        out_specs=[q_spec, kv_full, kv_full, sink_spec],
        out_shape=[jax.ShapeDtypeStruct(q.shape, F32), kv_shape, kv_shape, jax.ShapeDtypeStruct(sink.shape, F32)],
        compiler_params=_cparams(("arbitrary", "arbitrary")),
    )(q, k, k, k, proj3, proj3, proj3, sink, dy)


CONV_COLS = 256


def _conv_taps(u, seq):
    row = lax.broadcasted_iota(jnp.int32, u.shape, 0)
    prev = jnp.where(row == 0, 0.0, pltpu.roll(u, 1, axis=0))
    nxt = jnp.where(row == seq - 1, 0.0, pltpu.roll(u, seq - 1, axis=0))
    return prev, nxt


def _conv_fwd(proj3, w8):
    B, S, _ = proj3.shape
    ncb = 2 * MLSTM_WIDTH // CONV_COLS

    def body(u_ref, w_ref, o_ref):
        u = u_ref[0]
        prev, nxt = _conv_taps(u, S)
        o_ref[0] = _silu(prev * w_ref[0:1, :] + u * w_ref[1:2, :] + nxt * w_ref[2:3, :] + w_ref[3:4, :])

    return pl.pallas_call(
        body, name="conv_fwd", grid=(B, ncb),
        in_specs=[pl.BlockSpec((1, S, CONV_COLS), lambda b, c: (b, 0, C_QK // CONV_COLS + c)),
                  pl.BlockSpec((8, CONV_COLS), lambda b, c: (0, c))],
        out_specs=pl.BlockSpec((1, S, CONV_COLS), lambda b, c: (b, 0, c)),
        out_shape=jax.ShapeDtypeStruct((B, S, 2 * MLSTM_WIDTH), F32),
        compiler_params=_cparams(("parallel", "parallel")),
    )(proj3, w8)


def _conv_bwd(proj3, w8, dout_f, dout_b):
    B, S, _ = proj3.shape
    ncb = 2 * MLSTM_WIDTH // CONV_COLS

    def body(u_ref, w_ref, df_ref, db_ref, du_ref, dw_ref):
        b = pl.program_id(1)
        u = u_ref[0]
        prev, nxt = _conv_taps(u, S)
        w0, w1, w2 = w_ref[0:1, :], w_ref[1:2, :], w_ref[2:3, :]
        pre = prev * w0 + u * w1 + nxt * w2 + w_ref[3:4, :]
        sig = _sigmoid(pre)
        dpre = (df_ref[0] + db_ref[0]) * (sig * (1.0 + pre * (1.0 - sig)))
        dprev, dnxt = _conv_taps(dpre, S)
        du_ref[0] = (dnxt * w0 + dpre * w1 + dprev * w2).astype(BF16)

        @pl.when(b == 0)
        def _():
            dw_ref[...] = jnp.zeros_like(dw_ref)

        dw_ref[0:1, :] += jnp.sum(dpre * prev, axis=0, keepdims=True)
        dw_ref[1:2, :] += jnp.sum(dpre * u, axis=0, keepdims=True)
        dw_ref[2:3, :] += jnp.sum(dpre * nxt, axis=0, keepdims=True)
        dw_ref[3:4, :] += jnp.sum(dpre, axis=0, keepdims=True)

    blk = pl.BlockSpec((1, S, CONV_COLS), lambda c, b: (b, 0, c))
    return pl.pallas_call(
        body, name="conv_bwd", grid=(ncb, B),
        in_specs=[pl.BlockSpec((1, S, CONV_COLS), lambda c, b: (b, 0, C_QK // CONV_COLS + c)),
                  pl.BlockSpec((8, CONV_COLS), lambda c, b: (0, c)), blk, blk],
        out_specs=[blk, pl.BlockSpec((8, CONV_COLS), lambda c, b: (0, c))],
        out_shape=[jax.ShapeDtypeStruct((B, S, 2 * MLSTM_WIDTH), BF16), jax.ShapeDtypeStruct((8, 2 * MLSTM_WIDTH), F32)],
        compiler_params=_cparams(("parallel", "arbitrary")),
    )(proj3, w8, dout_f, dout_b)


MLSTM_HEADS_PER_STEP = 4


def _chunk_masks(direction):
    t = lax.broadcasted_iota(jnp.int32, (MLSTM_CHUNK, MLSTM_CHUNK), 0)
    s = lax.broadcasted_iota(jnp.int32, (MLSTM_CHUNK, MLSTM_CHUNK), 1)
    le, ge = (s <= t).astype(F32), (s >= t).astype(F32)
    eye = (s == t).astype(F32)
    return (le, ge, eye) if direction == 0 else (ge, le, eye)


def _gate_cols(gates, direction, head):
    lane = lax.broadcasted_iota(jnp.int32, gates.shape, 1)
    sel_i = (lane == (2 * direction) * MLSTM_HEADS + head).astype(F32)
    sel_f = (lane == (2 * direction + 1) * MLSTM_HEADS + head).astype(F32)
    return sel_i, sel_f


def _mlstm_fwd(qk, proj3, bias):
    B, S, _ = qk.shape
    nc = S // MLSTM_CHUNK
    H, L, DH = MLSTM_HEADS, MLSTM_CHUNK, MLSTM_HEAD_DIM

    def chunk_of(d, c):
        return c if d == 0 else nc - 1 - c

    HS = MLSTM_HEADS_PER_STEP

    def body(qkf, qkb, vf, vb, gf, gb, bias_ref, hf, hb, csf, csb, nsf, nsb, msf, msb, c_st, n_st, m_st):
        c, hg = pl.program_id(1), pl.program_id(2)

        @pl.when(c == 0)
        def _():
            for d in range(2):
                for j in range(HS):
                    c_st[d, hg * HS + j] = jnp.zeros((DH, DH), F32)
                    n_st[d, hg * HS + j] = jnp.zeros((1, DH), F32)
                    m_st[d, hg * HS + j] = jnp.zeros((1, DH), F32)

        for d, (qk_ref, v_ref, g_ref, h_ref, cs, ns, ms) in enumerate(
                ((qkf, vf, gf, hf, csf, nsf, msf), (qkb, vb, gb, hb, csb, nsb, msb))):
            incl, incl_t, eye = _chunk_masks(d)
            gates = g_ref[0] + bias_ref[...]
            log_f = _log_sigmoid(gates)
            for j in range(HS):
                h = hg * HS + j
                sel_i, sel_f = _gate_cols(gates, d, h)
                li = jnp.sum(gates * sel_i, axis=1, keepdims=True)
                lf = jnp.sum(log_f * sel_f, axis=1, keepdims=True)
                c_in, n_in, m_in = c_st[d, h], n_st[d, h], m_st[d, h]
                cs[0, 0, j], ns[0, 0, j], ms[0, 0, j] = c_in, n_in, m_in
                hh, c_new, n_new, m_new = _mlstm_chunk(
                    qk_ref[0, :, 2 * j * DH:(2 * j + 1) * DH], qk_ref[0, :, (2 * j + 1) * DH:(2 * j + 2) * DH],
                    v_ref[0, :, j * DH:(j + 1) * DH], li, lf, c_in, n_in,
                    jnp.max(m_in, axis=1, keepdims=True), incl, incl_t, eye)
                h_ref[0, :, j * DH:(j + 1) * DH] = hh
                c_st[d, h], n_st[d, h] = c_new, n_new
                m_st[d, h] = jnp.broadcast_to(m_new, (1, DH))

    def tok_spec(width, base, d, per_head):
        return pl.BlockSpec((1, L, width), lambda b, c, h: (b, chunk_of(d, c), base + (h if per_head else 0)))

    def st_spec(shape, d):
        return pl.BlockSpec((1, 1, HS) + shape, lambda b, c, h: (b, chunk_of(d, c), h, 0, 0))

    in_specs = [tok_spec(2 * HS * DH, 0, 0, True), tok_spec(2 * HS * DH, 0, 1, True),
                tok_spec(HS * DH, C_VM // (HS * DH), 0, True), tok_spec(HS * DH, C_VM // (HS * DH), 1, True),
                tok_spec(LANES, C_GATES // LANES, 0, False), tok_spec(LANES, C_GATES // LANES, 1, False),
                pl.BlockSpec((1, LANES), lambda b, c, h: (0, 0))]
    out_specs = [tok_spec(HS * DH, 0, 0, True), tok_spec(HS * DH, 0, 1, True),
                 st_spec((DH, DH), 0), st_spec((DH, DH), 1), st_spec((1, DH), 0), st_spec((1, DH), 1),
                 st_spec((1, DH), 0), st_spec((1, DH), 1)]
    hs = jax.ShapeDtypeStruct((B, S, H * DH), F32)
    cs = jax.ShapeDtypeStruct((B, nc, H, DH, DH), F32)
    vs = jax.ShapeDtypeStruct((B, nc, H, 1, DH), F32)
    return pl.pallas_call(
        body, name="mlstm_fwd", grid=(B, nc, H // HS), in_specs=in_specs, out_specs=out_specs,
        out_shape=[hs, hs, cs, cs, vs, vs, vs, vs],
        scratch_shapes=[pltpu.VMEM((2, H, DH, DH), F32), pltpu.VMEM((2, H, 1, DH), F32), pltpu.VMEM((2, H, 1, DH), F32)],
        compiler_params=_cparams(("parallel", "arbitrary", "arbitrary")),
    )(qk, qk, proj3, proj3, proj3, proj3, bias)


def _mlstm_bwd(qk, proj3, bias, states, dh):
    B, S, _ = qk.shape
    nc = S // MLSTM_CHUNK
    H, L, DH = MLSTM_HEADS, MLSTM_CHUNK, MLSTM_HEAD_DIM

    def chunk_of(d, c):
        return nc - 1 - c if d == 0 else c

    HS = MLSTM_HEADS_PER_STEP

    def body(qkf, qkb, vf, vb, gf, gb, bias_ref, csf, csb, nsf, nsb, msf, msb, dhf, dhb,
             dqkf, dqkb, dvf, dvb, dgf, dgb, dc_st, dn_st, dm_st):
        c, hg = pl.program_id(1), pl.program_id(2)

        @pl.when(c == 0)
        def _():
            for d in range(2):
                for j in range(HS):
                    dc_st[d, hg * HS + j] = jnp.zeros((DH, DH), F32)
                    dn_st[d, hg * HS + j] = jnp.zeros((1, DH), F32)
                    dm_st[d, hg * HS + j] = jnp.zeros((1, DH), F32)

        @pl.when(hg == 0)
        def _():
            dgf[...] = jnp.zeros_like(dgf)
            dgb[...] = jnp.zeros_like(dgb)

        for d, (qk_ref, v_ref, g_ref, cs, ns, ms, dh_ref, dqk_ref, dv_ref, dg_ref) in enumerate(
                ((qkf, vf, gf, csf, nsf, msf, dhf, dqkf, dvf, dgf), (qkb, vb, gb, csb, nsb, msb, dhb, dqkb, dvb, dgb))):
            incl, incl_t, eye = _chunk_masks(d)
            gates = g_ref[0] + bias_ref[...]
            log_f = _log_sigmoid(gates)
            d_li = jnp.zeros_like(gates)
            d_lf = jnp.zeros_like(gates)
            for j in range(HS):
                h = hg * HS + j
                sel_i, sel_f = _gate_cols(gates, d, h)
                li = jnp.sum(gates * sel_i, axis=1, keepdims=True)
                lf = jnp.sum(log_f * sel_f, axis=1, keepdims=True)
                m_in = jnp.max(ms[0, 0, j], axis=1, keepdims=True)
                _, vjp = jax.vjp(
                    functools.partial(_mlstm_chunk, incl=incl, incl_t=incl_t, eye=eye),
                    qk_ref[0, :, 2 * j * DH:(2 * j + 1) * DH], qk_ref[0, :, (2 * j + 1) * DH:(2 * j + 2) * DH],
                    v_ref[0, :, j * DH:(j + 1) * DH], li, lf, cs[0, 0, j], ns[0, 0, j], m_in)
                dm_out = jnp.max(dm_st[d, h], axis=1, keepdims=True)
                dq, dk, dv, dli, dlf, dc, dn, dm = vjp((dh_ref[0, :, j * DH:(j + 1) * DH], dc_st[d, h], dn_st[d, h], dm_out))
                dqk_ref[0, :, 2 * j * DH:(2 * j + 1) * DH] = dq
                dqk_ref[0, :, (2 * j + 1) * DH:(2 * j + 2) * DH] = dk
                dv_ref[0, :, j * DH:(j + 1) * DH] = dv
                d_li += dli * sel_i
                d_lf += dlf * sel_f
                dc_st[d, h], dn_st[d, h] = dc, dn
                dm_st[d, h] = jnp.broadcast_to(dm, (1, DH))
            dg_ref[0] += d_li + d_lf * _sigmoid(-gates)

    def tok_spec(width, base, d, per_head):
        return pl.BlockSpec((1, L, width), lambda b, c, h: (b, chunk_of(d, c), base + (h if per_head else 0)))

    def st_spec(shape, d):
        return pl.BlockSpec((1, 1, HS) + shape, lambda b, c, h: (b, chunk_of(d, c), h, 0, 0))

    in_specs = [tok_spec(2 * HS * DH, 0, 0, True), tok_spec(2 * HS * DH, 0, 1, True),
                tok_spec(HS * DH, C_VM // (HS * DH), 0, True), tok_spec(HS * DH, C_VM // (HS * DH), 1, True),
                tok_spec(LANES, C_GATES // LANES, 0, False), tok_spec(LANES, C_GATES // LANES, 1, False),
                pl.BlockSpec((1, LANES), lambda b, c, h: (0, 0)),
                st_spec((DH, DH), 0), st_spec((DH, DH), 1), st_spec((1, DH), 0), st_spec((1, DH), 1),
                st_spec((1, DH), 0), st_spec((1, DH), 1), tok_spec(HS * DH, 0, 0, True), tok_spec(HS * DH, 0, 1, True)]
    out_specs = [tok_spec(2 * HS * DH, 0, 0, True), tok_spec(2 * HS * DH, 0, 1, True),
                 tok_spec(HS * DH, 0, 0, True), tok_spec(HS * DH, 0, 1, True),
                 tok_spec(LANES, 0, 0, False), tok_spec(LANES, 0, 1, False)]
    qks = jax.ShapeDtypeStruct((B, S, 2 * H * DH), F32)
    vs = jax.ShapeDtypeStruct((B, S, H * DH), F32)
    gs = jax.ShapeDtypeStruct((B, S, LANES), F32)
    csf, csb, nsf, nsb, msf, msb = states
    return pl.pallas_call(
        body, name="mlstm_bwd", grid=(B, nc, H // HS), in_specs=in_specs, out_specs=out_specs,
        out_shape=[qks, qks, vs, vs, gs, gs],
        scratch_shapes=[pltpu.VMEM((2, H, DH, DH), F32), pltpu.VMEM((2, H, 1, DH), F32), pltpu.VMEM((2, H, 1, DH), F32)],
        compiler_params=_cparams(("parallel", "arbitrary", "arbitrary")),
    )(qk, qk, proj3, proj3, proj3, proj3, bias, csf, csb, nsf, nsb, msf, msb, dh, dh)


ROW_BLOCK = 256
FF_COLS = 512
FF_SHARD = D_FF // N_DEV
FF_SHARD_PAD = 384
FF_PAD = N_DEV * FF_SHARD_PAD


def _rms_bwd(name, x, g, dh, dres):
    T = x.shape[0]

    def fn(xv, gv, dhv, drv):
        _, vjp = jax.vjp(_rms, xv, gv)
        dx, dg = vjp(dhv)
        return drv + dx, dg

    return _rowwise(name, fn, [_In(x), _In(g, rows=False), _In(dh), _In(dres)],
                    [_Out(D_MODEL), _Out(D_MODEL, rows=False)], T, ROW_BLOCK)


def _mmw(name, a, w, mode, **kw):
    if isinstance(w, tuple):
        return _matmul(name, a, w[0], mode, bl=w[1], **kw)
    return _matmul(name, a, w, mode, **kw)


def _swiglu(gate, up):
    return _silu(gate) * up


def _ffn_in(name, x, gain, wg, wu):
    (M, K), N = x.shape, wg.shape[0]
    tm, tn = _first_divisor(M, (1024, 512, 256, 128)), FF_COLS

    def body(x_ref, gain_ref, wg_ref, wu_ref, h_ref, g_ref, u_ref, a_ref):
        @pl.when(pl.program_id(1) == 0)
        def _():
            h_ref[...] = _rms(x_ref[...], gain_ref[...]).astype(BF16)

        hv = h_ref[...]
        gate = _mm(hv, wg_ref[...], _NT)
        up = _mm(hv, wu_ref[...], _NT)
        g_ref[...], u_ref[...] = gate.astype(BF16), up.astype(BF16)
        a_ref[...] = _swiglu(gate, up).astype(BF16)

    row_spec = pl.BlockSpec((tm, K), lambda i, j: (i, 0))
    w_spec = pl.BlockSpec((tn, K), lambda i, j: (j, 0))
    o_spec = pl.BlockSpec((tm, tn), lambda i, j: (i, j))
    return pl.pallas_call(
        body, name=name, grid=(M // tm, N // tn),
        in_specs=[row_spec, pl.BlockSpec((1, K), lambda i, j: (0, 0)), w_spec, w_spec],
        out_specs=[row_spec, o_spec, o_spec, o_spec],
        out_shape=[jax.ShapeDtypeStruct((M, K), BF16)] + [jax.ShapeDtypeStruct((M, N), BF16)] * 3,
        compiler_params=_cparams(("parallel", "arbitrary")),
    )(x, gain, wg, wu)


def _norm_matmul(name, x, gain, w):
    (M, K), N = x.shape, w.shape[1]
    tm = _first_divisor(M, (1024, 512, 256, 128))
    tn = _first_divisor(N, (1664, 1024, 512, 384, 256, 128))

    def body(x_ref, gain_ref, w_ref, h_ref, o_ref):
        @pl.when(pl.program_id(1) == 0)
        def _():
            h_ref[...] = _rms(x_ref[...], gain_ref[...]).astype(BF16)

        o_ref[...] = _mm(h_ref[...], w_ref[...], _NN)

    row_spec = pl.BlockSpec((tm, K), lambda i, j: (i, 0))
    return pl.pallas_call(
        body, name=name, grid=(M // tm, N // tn),
        in_specs=[row_spec, pl.BlockSpec((1, K), lambda i, j: (0, 0)), pl.BlockSpec((K, tn), lambda i, j: (0, j))],
        out_specs=[row_spec, pl.BlockSpec((tm, tn), lambda i, j: (i, j))],
        out_shape=[jax.ShapeDtypeStruct((M, K), BF16), jax.ShapeDtypeStruct((M, N), F32)],
        compiler_params=_cparams(("parallel", "arbitrary")),
    )(x, gain, w)


def _ffn_dact(name, dx, wd, gate, up):
    (M, K), N = dx.shape, wd.shape[0]
    tm, tn = _first_divisor(M, (1024, 512, 256, 128)), FF_COLS

    def body(dx_ref, wd_ref, g_ref, u_ref, dg_ref, du_ref):
        dact = _mm(dx_ref[...], wd_ref[...], _NT) * 0.5
        gate, up = g_ref[...].astype(F32), u_ref[...].astype(F32)
        s = _sigmoid(gate)
        silu = gate * s
        dg_ref[...] = (dact * up * (s + silu * (1.0 - s))).astype(BF16)
        du_ref[...] = (dact * silu).astype(BF16)

    o_spec = pl.BlockSpec((tm, tn), lambda i, j: (i, j))
    return pl.pallas_call(
        body, name=name, grid=(M // tm, N // tn),
        in_specs=[pl.BlockSpec((tm, K), lambda i, j: (i, 0)), pl.BlockSpec((tn, K), lambda i, j: (j, 0)), o_spec, o_spec],
        out_specs=[o_spec, o_spec],
        out_shape=[jax.ShapeDtypeStruct((M, N), BF16), jax.ShapeDtypeStruct((M, N), BF16)],
        compiler_params=_cparams(("parallel", "parallel")),
    )(dx, wd, gate, up)


def _ffn_dh(name, dgate, dup, wg, wu, dep, x, gain, dres):
    (M, K), N = dgate.shape, wg.shape[1]
    tm, tk = _first_divisor(M, (512, 256, 128)), _first_divisor(K, (1024, 512, 384, 256, 128))
    nk = K // tk

    def body(dg_ref, du_ref, wg_ref, wu_ref, x_ref, gain_ref, dres_ref, dep_ref, o_ref, dgain_ref, acc):
        i, k = pl.program_id(0), pl.program_id(1)

        @pl.when(k == 0)
        def _():
            acc[...] = jnp.zeros_like(acc)

        acc[...] += _mm(dg_ref[...], wg_ref[...], _NN) + _mm(du_ref[...], wu_ref[...], _NN)

        @pl.when((k == nk - 1) & (i == 0))
        def _():
            dgain_ref[...] = jnp.zeros_like(dgain_ref)

        @pl.when(k == nk - 1)
        def _():
            _, vjp = jax.vjp(_rms, x_ref[...], gain_ref[...])
            dx, dgain = vjp(acc[...])
            o_ref[...] = dres_ref[...] + dx
            dgain_ref[...] += dgain

    a_spec = pl.BlockSpec((tm, tk), lambda i, k: (i, k))
    w_spec = pl.BlockSpec((tk, N), lambda i, k: (k, 0))
    row_spec = pl.BlockSpec((tm, N), lambda i, k: (i, 0))
    gain_spec = pl.BlockSpec((1, N), lambda i, k: (0, 0))
    return pl.pallas_call(
        body, name=name, grid=(M // tm, nk),
        in_specs=[a_spec, a_spec, w_spec, w_spec, row_spec, gain_spec, row_spec, pl.BlockSpec(memory_space=pl.ANY)],
        out_specs=[row_spec, gain_spec],
        out_shape=[jax.ShapeDtypeStruct((M, N), F32), jax.ShapeDtypeStruct((1, N), F32)],
        scratch_shapes=[pltpu.VMEM((tm, N), F32)], compiler_params=_cparams(("arbitrary", "arbitrary")),
    )(dgate, dup, wg, wu, x, gain, dres, dep)


def _ffn_fwd(tag, x, g, wg, wu, wd):
    h, gate, up, act = _ffn_in(tag + "_in", x, g, wg, wu)
    if callable(wd):
        wd = wd(act)
    out = _mmw(tag + "_down", act, wd, "nn", res=x, scale=0.5, whole_k=True)
    return out, (x, h, gate, up, act), wd


def _ffn_bwd(tag, saved, g, wg, wu, wd, dx, on_dw):
    x, h, gate, up, act = saved
    dgate, dup = _ffn_dact(tag + "_dact", dx, wd, gate, up)
    dwd = _matmul(tag + "_dwd", act, dx, "tn", scale=0.5, out_dtype=BF16)
    dwg = _matmul(tag + "_dwg", dgate, h, "tn", out_dtype=BF16, whole_k=True)
    dwu = _matmul(tag + "_dwu", dup, h, "tn", out_dtype=BF16, whole_k=True)
    token = on_dw({tag + "_w_gate": dwg, tag + "_w_up": dwu, tag + "_w_down": dwd}, dwu)
    return _ffn_dh(tag + "_dh", dgate, dup, wg, wu, token, x, g, dx)


def _rope_cos_sin(positions):
    half = ROPE_DIM // 2
    inv_freq = jnp.power(jnp.float32(ROPE_THETA), -jnp.arange(half, dtype=F32) * (2.0 / ROPE_DIM))
    head = jnp.zeros((ATT_HEAD_DIM,), F32).at[:ROPE_DIM].set(jnp.concatenate([inv_freq, inv_freq]))
    row = jnp.tile(head, LANES // ATT_HEAD_DIM)[None, :]
    T = positions.shape[0]
    return _rowwise("rope_tables", _rope_tables, [_In(positions), _In(row, rows=False)], [_Out(LANES), _Out(LANES)], T, 1024)


def _prep_fwd(name, src, width, base, g, cos, sin):
    return _rowwise(name, _qk_prep, [_In(src, width, base), _In(g, rows=False), _In(cos), _In(sin)],
                    [_Out(width)], src.shape[0], 512)[0]


def _prep_bwd(name, src, width, base, g, cos, sin, dout, into=None):
    def fn(tv, gv, cv, sv, dv):
        _, vjp = jax.vjp(lambda a, b: _qk_prep(a, b, cv, sv), tv, gv)
        return vjp(dv)

    dsrc = _Out(width, BF16) if into is None else _Out(0, width=width, into=into, base=base)
    return _rowwise(name, fn, [_In(src, width, base), _In(g, rows=False), _In(cos), _In(sin), _In(dout)],
                    [dsrc, _Out(width, rows=False)], src.shape[0], 512)


def _mix_fwd(x, cos, sin, B, S, p):
    T = B * S
    h, proj = _norm_matmul("mix_proj", x, p["mix_norm"], p["w_in"])
    proj3 = proj.reshape(B, S, IN_PAD)
    q_gain = jnp.tile(p["attn_q_norm"], (1, ATT_HEADS))
    k_gain = jnp.tile(p["attn_k_norm"], (1, ATT_KV_HEADS))
    q_r = _prep_fwd("q_prep", proj, ATT_WIDTH, C_QA // ATT_WIDTH, q_gain, cos, sin)
    k_r = _prep_fwd("k_prep", proj, ATT_KV_WIDTH, C_KA // ATT_KV_WIDTH, k_gain, cos, sin)
    qh = q_r.reshape(B, S, ATT_WIDTH)
    kh = k_r.reshape(B, S, ATT_KV_WIDTH)
    sink = p["attn_sink"].reshape(ATT_KV_HEADS, ATT_GROUP, 1, 1)
    y_a = _attn_fwd(qh, kh, proj3, sink).reshape(T, ATT_WIDTH)

    qk_c = _conv_fwd(proj3, p["conv_w8"])
    hf, hb, *states = _mlstm_fwd(qk_c, proj3, p["gate_bias"])
    hf2, hb2 = hf.reshape(T, MLSTM_WIDTH), hb.reshape(T, MLSTM_WIDTH)
    DH = MLSTM_HEAD_DIM
    y_m = _rowwise("mlstm_out", _mlstm_combine,
                   [_In(hf2, DH, split=True), _In(hb2, DH, split=True), _In(proj, DH, C_OM // DH, split=True),
                    _In(p["mlstm_out_norm"], DH, split=True, rows=False)],
                   [_Out(MLSTM_WIDTH, BF16, DH, split=True)], T, 1024, ncol=MLSTM_HEADS)[0]

    za = _mmw("branch_a", y_a, p["w_branch_attn"], "nn")
    zm = _mmw("branch_m", y_m, p["w_branch_mlstm"], "nn")
    W = 512
    merged = _rowwise("merge", _merge,
                      [_In(proj, W, C_GMERGE // W, split=True), _In(proj, W, (C_GMERGE + D_MODEL) // W, split=True),
                       _In(za, W, split=True), _In(zm, W, split=True)],
                      [_Out(D_MODEL, BF16, W, split=True)], T, 512, ncol=D_MODEL // W)[0]
    out = _mmw("mix_out", merged, p["w_out"], "nn", res=x)
    saved = dict(x=x, h=h, proj=proj, q_gain=q_gain, k_gain=k_gain, qh=qh, kh=kh, sink=sink, y_a=y_a, qk_c=qk_c,
                 hf=hf2, hb=hb2, states=states, y_m=y_m, za=za, zm=zm, merged=merged)
    return out, saved


def _mix_bwd(sv, cos, sin, B, S, p, dx, on_dw):
    T = B * S
    DH = MLSTM_HEAD_DIM
    proj = sv["proj"]
    proj3 = proj.reshape(B, S, IN_PAD)
    g = {}
    dmerged = _mmw("mix_dmerged", dx, p["w_out"], "nt")
    g["w_out"] = _matmul("mix_dwout", sv["merged"], dx, "tn", out_dtype=BF16)
    dproj = lax.empty((T, IN_PAD), BF16)

    def merge_bwd(ga, gm, za, zm, dm):
        _, vjp = jax.vjp(_merge, ga, gm, za, zm)
        dga, dgm, dza, dzm = vjp(dm)
        return jnp.concatenate([dga, dgm], axis=1), dza, dzm

    dproj, dza, dzm = _rowwise(
        "merge_bwd", merge_bwd,
        [_In(proj, D_MODEL, C_GMERGE // D_MODEL), _In(proj, D_MODEL, C_GMERGE // D_MODEL + 1),
         _In(sv["za"]), _In(sv["zm"]), _In(dmerged)],
        [_Out(0, width=2 * D_MODEL, into=dproj, base=C_GMERGE // (2 * D_MODEL)), _Out(D_MODEL, BF16), _Out(D_MODEL, BF16)],
        T, ROW_BLOCK)
    dya = _mmw("branch_a_dx", dza, p["w_branch_attn"], "nt")
    g["w_branch_attn"] = _matmul("branch_a_dw", sv["y_a"], dza, "tn", out_dtype=BF16)
    dym = _mmw("branch_m_dx", dzm, p["w_branch_mlstm"], "nt")
    g["w_branch_mlstm"] = _matmul("branch_m_dw", sv["y_m"], dzm, "tn", out_dtype=BF16)

    def combine_bwd(hf, hb, o_pre, gn, dy):
        _, vjp = jax.vjp(_mlstm_combine, hf, hb, o_pre, gn)
        dhf, _, do, dg = vjp(dy)
        return dhf, do, dg

    dh, dproj, g["mlstm_out_norm"] = _rowwise(
        "mlstm_out_bwd", combine_bwd,
        [_In(sv["hf"], DH, split=True), _In(sv["hb"], DH, split=True), _In(proj, DH, C_OM // DH, split=True),
         _In(p["mlstm_out_norm"], DH, split=True, rows=False), _In(dym, DH, split=True)],
        [_Out(MLSTM_WIDTH, F32, DH, split=True), _Out(0, width=DH, split=True, into=dproj, base=C_OM // DH),
         _Out(MLSTM_WIDTH, F32, DH, split=True, rows=False)], T, 1024, ncol=MLSTM_HEADS)
    dqk_f, dqk_b, dv_f, dv_b, dg_f, dg_b = _mlstm_bwd(sv["qk_c"], proj3, p["gate_bias"], sv["states"],
                                                       dh.reshape(B, S, MLSTM_WIDTH))
    dproj, g["gate_bias"] = _rowwise(
        "mlstm_dsum_gates", lambda a, b: (a + b, jnp.sum(a + b, axis=0, keepdims=True)),
        [_In(dg_f.reshape(T, LANES)), _In(dg_b.reshape(T, LANES))],
        [_Out(0, width=LANES, into=dproj, base=C_GATES // LANES), _Out(LANES, rows=False)], T, 1024)
    dproj = _rowwise(
        "mlstm_dsum_v", lambda a, b: a + b, [_In(dv_f.reshape(T, MLSTM_WIDTH)), _In(dv_b.reshape(T, MLSTM_WIDTH))],
        [_Out(0, width=MLSTM_WIDTH, into=dproj, base=C_VM // MLSTM_WIDTH)], T, 1024)[0]
    dqk, g["conv_w8"] = _conv_bwd(proj3, p["conv_w8"], dqk_f, dqk_b)

    dqh, dkh, dvh, dsink = _attn_bwd(sv["qh"], sv["kh"], proj3, sv["sink"], dya.reshape(B, S, ATT_WIDTH))
    g["attn_sink"] = dsink.reshape(1, ATT_HEADS)
    dva = dvh.reshape(T, ATT_KV_WIDTH)
    dproj, dq_gain = _prep_bwd("q_prep_bwd", proj, ATT_WIDTH, C_QA // ATT_WIDTH, sv["q_gain"], cos, sin,
                               dqh.reshape(T, ATT_WIDTH), into=dproj)
    dka, dk_gain = _prep_bwd("k_prep_bwd", proj, ATT_KV_WIDTH, C_KA // ATT_KV_WIDTH, sv["k_gain"], cos, sin,
                             dkh.reshape(T, ATT_KV_WIDTH))
    g["attn_q_norm"] = jnp.sum(dq_gain.reshape(ATT_HEADS, ATT_HEAD_DIM), axis=0, keepdims=True)
    g["attn_k_norm"] = jnp.sum(dk_gain.reshape(ATT_KV_HEADS, ATT_HEAD_DIM), axis=0, keepdims=True)

    dproj = dproj.at[:, C_QK:C_QK + 2 * MLSTM_WIDTH].set(dqk.reshape(T, 2 * MLSTM_WIDTH))
    dproj = dproj.at[:, C_KA:C_KA + ATT_KV_WIDTH].set(dka)
    dproj = dproj.at[:, C_VA:C_VA + ATT_KV_WIDTH].set(dva.astype(BF16))
    dwin = _matmul("mix_dwin", sv["h"], dproj, "tn", out_dtype=BF16)
    token = on_dw({"w_in": _w_in_to_slots(dwin), "w_branch_attn": g.pop("w_branch_attn"),
                   "w_branch_mlstm": g.pop("w_branch_mlstm"), "w_out": g.pop("w_out")}, dwin)
    dh2 = _matmul("mix_dh", dproj, p["w_in"], "nt", dep=token)
    dx_new, g["mix_norm"] = _rms_bwd("mix_dnorm", sv["x"], p["mix_norm"], dh2, dx)
    return dx_new, g


def _loss_and_grad(x, g, target):
    T = x.shape[0]

    def loss_fn(xv, gv, tv):
        err = jnp.square(_rms(xv, gv) - tv)
        return 0.5 * jnp.sum(jnp.mean(err, axis=-1, keepdims=True), axis=0, keepdims=True)

    def fn(xv, gv, tv):
        val, vjp = jax.vjp(lambda a, b: loss_fn(a, b, tv), xv, gv)
        dx, dg = vjp(jnp.ones((1, 1), F32))
        return val, dx, dg

    return _rowwise("loss_head", fn, [_In(x), _In(g, rows=False), _In(target)],
                    [_Out(1, rows=False), _Out(D_MODEL), _Out(D_MODEL, rows=False)], T, ROW_BLOCK)


def _block_norm_fwd(x, g):
    T = x.shape[0]
    return _rowwise("block_norm", _rms, [_In(x), _In(g, rows=False)], [_Out(D_MODEL)], T, ROW_BLOCK)[0]


def _block_norm_bwd(x, g, dy):
    T = x.shape[0]

    def fn(xv, gv, dv):
        _, vjp = jax.vjp(_rms, xv, gv)
        return vjp(dv)

    return _rowwise("block_norm_bwd", fn, [_In(x), _In(g, rows=False), _In(dy)],
                    [_Out(D_MODEL), _Out(D_MODEL, rows=False)], T, ROW_BLOCK)


def _qk_perm_cols(t, axis):
    q, k = jnp.split(t, 2, axis=axis)
    parts = []
    for h in range(MLSTM_HEADS):
        sl = [slice(None)] * t.ndim
        sl[axis] = slice(h * MLSTM_HEAD_DIM, (h + 1) * MLSTM_HEAD_DIM)
        parts += [q[tuple(sl)], k[tuple(sl)]]
    return jnp.concatenate(parts, axis=axis)


def _qk_unperm_cols(t, axis):
    qs, ks = [], []
    for h in range(MLSTM_HEADS):
        sl = [slice(None)] * t.ndim
        sl[axis] = slice(2 * h * MLSTM_HEAD_DIM, (2 * h + 1) * MLSTM_HEAD_DIM)
        qs.append(t[tuple(sl)])
        sl[axis] = slice((2 * h + 1) * MLSTM_HEAD_DIM, (2 * h + 2) * MLSTM_HEAD_DIM)
        ks.append(t[tuple(sl)])
    return jnp.concatenate(qs + ks, axis=axis)


def _w_in_arrange(w):
    qa, ka, va, qm, km, vm, om, gm, gmerge = jnp.split(w, np.cumsum(
        (ATT_WIDTH, ATT_KV_WIDTH, ATT_KV_WIDTH, MLSTM_WIDTH, MLSTM_WIDTH, MLSTM_WIDTH, MLSTM_WIDTH, MLSTM_N_GATES))[:].tolist(), axis=1)
    qk = _qk_perm_cols(jnp.concatenate([qm, km], axis=1), 1)
    pad = jnp.zeros((w.shape[0], LANES - MLSTM_N_GATES), w.dtype)
    return jnp.concatenate([gmerge, qk, vm, om, qa, ka, va, gm, pad], axis=1)


def _w_in_restore(w):
    gmerge = w[:, C_GMERGE:C_GMERGE + 2 * D_MODEL]
    qk = _qk_unperm_cols(w[:, C_QK:C_QK + 2 * MLSTM_WIDTH], 1)
    vm, om = w[:, C_VM:C_VM + MLSTM_WIDTH], w[:, C_OM:C_OM + MLSTM_WIDTH]
    qa, ka, va = w[:, C_QA:C_QA + ATT_WIDTH], w[:, C_KA:C_KA + ATT_KV_WIDTH], w[:, C_VA:C_VA + ATT_KV_WIDTH]
    gm = w[:, C_GATES:C_GATES + MLSTM_N_GATES]
    return jnp.concatenate([qa, ka, va, qk, vm, om, gm, gmerge], axis=1)


BIG = ("ffn1_w_gate", "ffn1_w_up", "ffn1_w_down", "w_in", "mlstm_conv_w", "w_branch_attn", "w_branch_mlstm", "w_out",
       "ffn2_w_gate", "ffn2_w_up", "ffn2_w_down")
MATMUL_W = tuple(n for n in BIG if n != "mlstm_conv_w")
SMALL = ("ffn1_norm", "mix_norm", "mlstm_gate_bias", "attn_q_norm", "attn_k_norm", "attn_sink", "mlstm_conv_b",
         "mlstm_out_norm", "ffn2_norm", "block_out_norm")
WEIGHTS = ("ffn1_norm", "ffn1_w_gate", "ffn1_w_up", "ffn1_w_down", "mix_norm", "w_in", "mlstm_gate_bias", "attn_q_norm",
           "attn_k_norm", "attn_sink", "mlstm_conv_w", "mlstm_conv_b", "mlstm_out_norm", "w_branch_attn", "w_branch_mlstm",
           "w_out", "ffn2_norm", "ffn2_w_gate", "ffn2_w_up", "ffn2_w_down", "block_out_norm")
PACK_COLS = 1024


def _padded_rows(n_elems):
    return -(-n_elems // PACK_COLS)


def _pack_flat(arrs, dtype, row_multiple):
    parts = []
    for a in arrs:
        flat = a.reshape(-1).astype(dtype)
        pad = _padded_rows(flat.shape[0]) * PACK_COLS - flat.shape[0]
        parts.append(jnp.pad(flat, (0, pad)) if pad else flat)
    flat = jnp.concatenate(parts)
    rows = flat.shape[0] // PACK_COLS
    extra = (-rows) % row_multiple
    if extra:
        flat = jnp.pad(flat, (0, extra * PACK_COLS))
    return flat.reshape(-1, PACK_COLS)


def _pack_rows(name, pieces, total_rows):
    def body(*refs):
        o_ref = refs[-1]
        o_ref[...] = jnp.zeros_like(o_ref)
        at = 0
        for ref, (arr, r0, nr) in zip(refs[:-1], pieces):
            o_ref[at:at + nr, 0:arr.shape[1]] = ref[r0:r0 + nr, :].astype(F32)
            at += nr

    return pl.pallas_call(body, name=name, out_shape=jax.ShapeDtypeStruct((total_rows, PACK_COLS), F32))(
        *[p[0] for p in pieces])


def _unpack_flat(buf, shapes, lead=()):
    flat = buf.reshape(lead + (-1,))
    out, off = [], 0
    for s in shapes:
        n = int(np.prod(s))
        out.append(flat[..., off:off + n].reshape(lead + tuple(s)))
        off += _padded_rows(n) * PACK_COLS
    return out


class _Lay:
    def __init__(self, shard, axis, width):
        self.shard, self.axis, self.width = shard, axis, width
        self.padded = tuple(width if a == axis else s for a, s in enumerate(shard))
        self.whole = tuple(N_DEV * width if a == axis else s for a, s in enumerate(shard))


_FF_ROW = _Lay((FF_SHARD, D_MODEL), 0, FF_SHARD_PAD)
TRANSPOSED = ("ffn1_w_gate", "ffn1_w_up", "ffn2_w_gate", "ffn2_w_up")
LAYOUTS = {
    "ffn1_w_gate": _FF_ROW, "ffn1_w_up": _FF_ROW, "ffn1_w_down": _FF_ROW,
    "ffn2_w_gate": _FF_ROW, "ffn2_w_up": _FF_ROW, "ffn2_w_down": _FF_ROW,
    "w_in": _Lay((D_MODEL, IN_WIDTH // N_DEV), 0, D_MODEL),
    "mlstm_conv_w": _Lay((3, 2 * MLSTM_WIDTH // N_DEV), 1, 2 * MLSTM_WIDTH // N_DEV),
    "w_branch_attn": _Lay((ATT_WIDTH, D_MODEL // N_DEV), 1, D_MODEL // N_DEV),
    "w_branch_mlstm": _Lay((MLSTM_WIDTH, D_MODEL // N_DEV), 1, D_MODEL // N_DEV),
    "w_out": _Lay((D_MODEL // N_DEV, D_MODEL), 0, D_MODEL // N_DEV),
}


def _window(ref, axis, j, width):
    idx = [slice(None)] * len(ref.shape)
    idx[axis] = pl.ds(pl.multiple_of(j * width, width), width)
    return ref.at[tuple(idx)]


ANY = pl.BlockSpec(memory_space=pl.ANY)


def _mesh_pos():
    return lax.axis_index("x"), lax.axis_index("y"), lax.axis_index("c")


def _all_gather(name, shard, vmem=False):
    R, C = shard.shape
    space = pl.BlockSpec(memory_space=pltpu.VMEM) if vmem else ANY

    def body(x_ref, out_ref, send_sems, recv_sems, local_sem):
        x, y, c = _mesh_pos()
        me, sibling = (x, y, c), (x, y, 1 - c)
        chips = [(1 - x, y), (x, 1 - y), (1 - x, 1 - y)]

        def slot(px, py, pc):
            return out_ref.at[4 * px + 2 * py + pc]

        def copy(k, block, to, src=None):
            return pltpu.make_async_remote_copy(
                src_ref=slot(*block) if src is None else src, dst_ref=slot(*block),
                send_sem=send_sems.at[k], recv_sem=recv_sems.at[k], device_id=to, device_id_type=MESH)

        mine = pltpu.make_async_copy(x_ref, slot(*me), local_sem)
        mine.start()
        first = [copy(0, me, sibling, src=x_ref)]
        first += [copy(1 + j, me, (*chip, c), src=x_ref) for j, chip in enumerate(chips)]
        for cp in first:
            cp.start()
        passed = [copy(4 + j, (*chip, c), sibling) for j, chip in enumerate(chips)]
        for j, chip in enumerate(chips):
            copy(1 + j, (*chip, c), me).wait_recv()
            passed[j].start()
        copy(0, sibling, me).wait_recv()
        for j, chip in enumerate(chips):
            copy(4 + j, (*chip, 1 - c), me).wait_recv()
        for cp in first + passed:
            cp.wait_send()
        mine.wait()

    return pl.pallas_call(
        body, name=name, out_shape=jax.ShapeDtypeStruct((N_DEV, R, C), shard.dtype),
        in_specs=[space], out_specs=space,
        scratch_shapes=[pltpu.SemaphoreType.DMA((7,)), pltpu.SemaphoreType.DMA((7,)), pltpu.SemaphoreType.DMA],
    )(shard)


HBM = pl.BlockSpec(memory_space=pltpu.HBM)
SEM = pl.BlockSpec(memory_space=pltpu.SEMAPHORE)
SPLIT_COPY = pltpu.CompilerParams(has_side_effects=pltpu.SideEffectType.DATAFLOW_SIDE_EFFECTING)
N_PEERS = N_DEV - 1


def _peers(x, y, c):
    return [(x, y, 1 - c), (1 - x, y, c), (x, 1 - y, c), (1 - x, 1 - y, c),
            (1 - x, y, 1 - c), (x, 1 - y, 1 - c), (1 - x, 1 - y, 1 - c)]


def _dev_index(pos):
    return 4 * pos[0] + 2 * pos[1] + pos[2]


def _place_own(name, stacks, layer, lays):
    nt = len(stacks)
    me = _dev_index(_mesh_pos())

    def body(me_ref, *refs):
        for x_ref, s_ref, o_ref, lay in zip(refs[:nt], refs[nt:2 * nt], refs[2 * nt:], lays):
            rows = lay.shard[0]
            if lay.padded != lay.shard:
                s_ref[...] = jnp.zeros_like(s_ref)
            s_ref[0:rows, :] = x_ref[...].astype(BF16)
            o_ref[...] = s_ref[...]

    def window_spec(lay):
        if lay.axis == 0:
            return pl.BlockSpec(lay.padded, lambda i, me_ref: (me_ref[0], 0))
        return pl.BlockSpec(lay.padded, lambda i, me_ref: (0, me_ref[0]))

    for lay in lays:
        assert lay.padded[1] == lay.shard[1], "only rows are padded"
    res = pl.pallas_call(
        body, name=name,
        grid_spec=pltpu.PrefetchScalarGridSpec(
            num_scalar_prefetch=1, grid=(1,),
            in_specs=[pl.BlockSpec((None,) + lay.shard, lambda i, me_ref: (layer, 0, 0)) for lay in lays],
            out_specs=[pl.BlockSpec(lay.padded, lambda i, me_ref: (0, 0)) for lay in lays] + [window_spec(lay) for lay in lays]),
        out_shape=[jax.ShapeDtypeStruct(lay.padded, BF16) for lay in lays] + [jax.ShapeDtypeStruct(lay.whole, BF16) for lay in lays],
        compiler_params=_cparams(("arbitrary",)),
    )(me.reshape(1).astype(jnp.int32), *stacks)
    return list(res[:nt]), list(res[nt:])


NEAR_PEERS = 4


def _gather_start(name, shards, lands, lays, groups, n_peers, after):
    nt, ng = len(shards), len(groups)

    def body(*refs):
        x_refs, land_refs = refs[:nt], refs[nt:2 * nt]
        sems = refs[2 * nt + 1:2 * nt + 1 + 2 * ng]
        pos = _mesh_pos()
        me = _dev_index(pos)
        for g, tens in enumerate(groups):
            for i, t in enumerate(tens):
                for k, peer in enumerate(_peers(*pos)[:n_peers[g]]):
                    pltpu.make_async_remote_copy(
                        src_ref=x_refs[t], dst_ref=_window(land_refs[t], lays[t].axis, me, lays[t].width),
                        send_sem=sems[2 * g].at[n_peers[g] * i + k], recv_sem=sems[2 * g + 1].at[n_peers[g] * i + k],
                        device_id=peer, device_id_type=MESH).start()

    sem_shapes = []
    for g, tens in enumerate(groups):
        sem_shapes += [pltpu.SemaphoreType.DMA((n_peers[g] * len(tens),))] * 2
    thru = [pltpu.HBM(s.shape, s.dtype) for s in shards] + [pltpu.HBM(lay.whole, s.dtype) for s, lay in zip(shards, lays)]
    args = [pltpu.with_memory_space_constraint(s, pltpu.HBM) for s in shards]
    args += [pltpu.with_memory_space_constraint(ld, pltpu.HBM) for ld in lands]
    res = pl.pallas_call(
        body, name=name, out_shape=tuple(sem_shapes + thru), in_specs=[HBM] * (2 * nt) + [ANY],
        out_specs=tuple([SEM] * (2 * ng) + [HBM] * (2 * nt)),
        input_output_aliases={t: 2 * ng + t for t in range(2 * nt)}, compiler_params=SPLIT_COPY,
    )(*args, after)
    sems = [(res[2 * g], res[2 * g + 1]) for g in range(ng)]
    return sems, list(res[2 * ng:2 * ng + nt]), list(res[2 * ng + nt:])


def _gather_wait(name, sems, shards, lands, lays, n_peers, after):
    nt = len(shards)
    send_sems, recv_sems = sems

    def body(*refs):
        x_refs, land_refs = refs[:nt], refs[nt:2 * nt]
        send_ref, recv_ref = refs[2 * nt], refs[2 * nt + 1]
        pos = _mesh_pos()
        for t in range(nt):
            for k, peer in enumerate(_peers(*pos)[:n_peers]):
                cp = pltpu.make_async_remote_copy(
                    src_ref=x_refs[t], dst_ref=_window(land_refs[t], lays[t].axis, _dev_index(peer), lays[t].width),
                    send_sem=send_ref.at[n_peers * t + k], recv_sem=recv_ref.at[n_peers * t + k],
                    device_id=peer, device_id_type=MESH)
                cp.wait_send()
                cp.wait_recv()

    thru = [pltpu.HBM(s.shape, s.dtype) for s in shards] + [pltpu.HBM(ld.shape, ld.dtype) for ld in lands]
    res = pl.pallas_call(
        body, name=name, out_shape=tuple(thru), in_specs=[HBM] * (2 * nt) + [SEM, SEM, ANY],
        out_specs=tuple([HBM] * (2 * nt)), input_output_aliases={t: t for t in range(2 * nt)},
        compiler_params=SPLIT_COPY,
    )(*shards, *lands, send_sems, recv_sems, after)
    return list(res[nt:])


def _forward_to_sibling(name, lands, lays):
    nt = len(lands)

    def body(*refs):
        land_refs = refs[nt:2 * nt]
        send_sems, recv_sems = refs[2 * nt:]
        x, y, c = _mesh_pos()
        chips = [(1 - x, y), (x, 1 - y), (1 - x, 1 - y)]

        def copy(t, j, core):
            win = _window(land_refs[t], lays[t].axis, _dev_index((*chips[j], core)), lays[t].width)
            return pltpu.make_async_remote_copy(
                src_ref=win, dst_ref=win, send_sem=send_sems.at[3 * t + j], recv_sem=recv_sems.at[3 * t + j],
                device_id=(x, y, 1 - c), device_id_type=MESH)

        sends = [copy(t, j, c) for t in range(nt) for j in range(3)]
        for cp in sends:
            cp.start()
        for t in range(nt):
            for j in range(3):
                copy(t, j, 1 - c).wait_recv()
        for cp in sends:
            cp.wait_send()

    return pl.pallas_call(
        body, name=name, out_shape=[jax.ShapeDtypeStruct(ld.shape, ld.dtype) for ld in lands],
        in_specs=[ANY] * nt, out_specs=[ANY] * nt, input_output_aliases={t: t for t in range(nt)},
        scratch_shapes=[pltpu.SemaphoreType.DMA((3 * nt,)), pltpu.SemaphoreType.DMA((3 * nt,))],
    )(*lands)


def _pair_exchange(name, grads, lays):
    nt = len(grads)

    def body(*refs):
        g_refs, land_refs = refs[:nt], refs[nt:2 * nt]
        send_sems, recv_sems = refs[2 * nt:]
        x, y, c = _mesh_pos()
        copies = []
        for t in range(nt):
            for chip in range(4):
                copies.append(pltpu.make_async_remote_copy(
                    src_ref=_window(g_refs[t], lays[t].axis, 2 * chip + (1 - c), lays[t].width), dst_ref=land_refs[t].at[chip],
                    send_sem=send_sems.at[4 * t + chip], recv_sem=recv_sems.at[4 * t + chip],
                    device_id=(x, y, 1 - c), device_id_type=MESH))
        for cp in copies:
            cp.start()
        for cp in copies:
            cp.wait_recv()
        for cp in copies:
            cp.wait_send()

    out_shape = [jax.ShapeDtypeStruct((4,) + lay.padded, g.dtype) for g, lay in zip(grads, lays)]
    return pl.pallas_call(
        body, name=name, out_shape=out_shape, in_specs=[ANY] * nt, out_specs=[ANY] * nt,
        scratch_shapes=[pltpu.SemaphoreType.DMA((4 * nt,)), pltpu.SemaphoreType.DMA((4 * nt,))],
    )(*grads)


def _pair_sum(name, whole, landed, lay, out_dtype):
    R, C = lay.padded
    br = _first_divisor(R, (512, 384, 256, 128, 64, 32, 16, 8))
    nb = R // br
    if lay.axis == 0:
        mine_spec = pl.BlockSpec((br, C), lambda k, i, c_ref: ((2 * k + c_ref[0]) * nb + i, 0))
    else:
        mine_spec = pl.BlockSpec((br, C), lambda k, i, c_ref: (i, 2 * k + c_ref[0]))

    def body(c_ref, mine_ref, sib_ref, o_ref):
        o_ref[0] = (mine_ref[...].astype(F32) + sib_ref[0].astype(F32)).astype(out_dtype)

    c = lax.axis_index("c")
    return pl.pallas_call(
        body, name=name,
        grid_spec=pltpu.PrefetchScalarGridSpec(
            num_scalar_prefetch=1, grid=(4, nb),
            in_specs=[mine_spec, pl.BlockSpec((1, br, C), lambda k, i, c_ref: (k, i, 0))],
            out_specs=pl.BlockSpec((1, br, C), lambda k, i, c_ref: (k, i, 0))),
        out_shape=jax.ShapeDtypeStruct((4, R, C), out_dtype),
        compiler_params=_cparams(("parallel", "parallel")),
    )(c.reshape(1).astype(jnp.int32), whole, landed)


def _chip_start(name, sums):
    nt = len(sums)

    def body(*refs):
        s_refs, land_refs = refs[:nt], refs[nt:2 * nt]
        send_sems, recv_sems = refs[2 * nt], refs[2 * nt + 1]
        x, y, c = _mesh_pos()
        my_chip = 2 * x + y
        for t in range(nt):
            for j, (px, py) in enumerate([(1 - x, y), (x, 1 - y), (1 - x, 1 - y)]):
                pltpu.make_async_remote_copy(
                    src_ref=s_refs[t].at[2 * px + py], dst_ref=land_refs[t].at[my_chip],
                    send_sem=send_sems.at[3 * t + j], recv_sem=recv_sems.at[3 * t + j],
                    device_id=(px, py, c), device_id_type=MESH).start()

    thru = [pltpu.HBM(s.shape, s.dtype) for s in sums] * 2
    args = [pltpu.with_memory_space_constraint(s, pltpu.HBM) for s in sums]
    args += [pltpu.with_memory_space_constraint(lax.empty(s.shape, s.dtype), pltpu.HBM) for s in sums]
    res = pl.pallas_call(
        body, name=name, out_shape=tuple([pltpu.SemaphoreType.DMA((3 * nt,))] * 2 + thru), in_specs=[HBM] * (2 * nt),
        out_specs=tuple([SEM, SEM] + [HBM] * (2 * nt)), input_output_aliases={t: 2 + t for t in range(2 * nt)},
        compiler_params=SPLIT_COPY,
    )(*args)
    return (res[0], res[1]), list(res[2:2 + nt]), list(res[2 + nt:])


def _chip_wait(name, sems, sums, lands, after):
    nt = len(sums)

    def body(*refs):
        s_refs, land_refs = refs[:nt], refs[nt:2 * nt]
        send_sems, recv_sems = refs[2 * nt], refs[2 * nt + 1]
        x, y, c = _mesh_pos()
        my_chip = 2 * x + y
        for t in range(nt):
            for j, (px, py) in enumerate([(1 - x, y), (x, 1 - y), (1 - x, 1 - y)]):
                cp = pltpu.make_async_remote_copy(
                    src_ref=s_refs[t].at[my_chip], dst_ref=land_refs[t].at[2 * px + py],
                    send_sem=send_sems.at[3 * t + j], recv_sem=recv_sems.at[3 * t + j],
                    device_id=(px, py, c), device_id_type=MESH)
                cp.wait_send()
                cp.wait_recv()

    thru = [pltpu.HBM(s.shape, s.dtype) for s in sums] * 2
    res = pl.pallas_call(
        body, name=name, out_shape=tuple(thru), in_specs=[HBM] * (2 * nt) + [SEM, SEM, ANY],
        out_specs=tuple([HBM] * (2 * nt)), input_output_aliases={t: t for t in range(2 * nt)},
        compiler_params=SPLIT_COPY,
    )(*sums, *lands, sems[0], sems[1], after)
    return list(res[:nt]), list(res[nt:])


def _sum_chips(name, own, landed):
    _, R, C = own.shape
    br = _first_divisor(R, (512, 384, 256, 128, 64, 32, 16, 8))
    x, y, _ = _mesh_pos()
    slots = jnp.stack([2 * x + y, 2 * (1 - x) + y, 2 * x + (1 - y), 2 * (1 - x) + (1 - y)]).astype(jnp.int32)

    def body(slot_ref, mine_ref, a_ref, b_ref, c_ref, o_ref):
        o_ref[...] = ((mine_ref[0].astype(F32) + a_ref[0].astype(F32)) + b_ref[0].astype(F32)) + c_ref[0].astype(F32)

    def slot_spec(j):
        return pl.BlockSpec((1, br, C), lambda i, slot_ref: (slot_ref[j], i, 0))

    return pl.pallas_call(
        body, name=name,
        grid_spec=pltpu.PrefetchScalarGridSpec(
            num_scalar_prefetch=1, grid=(R // br,), in_specs=[slot_spec(0), slot_spec(1), slot_spec(2), slot_spec(3)],
            out_specs=pl.BlockSpec((br, C), lambda i, slot_ref: (i, 0))),
        out_shape=jax.ShapeDtypeStruct((R, C), F32), compiler_params=_cparams(("parallel",)),
    )(slots, own, landed, landed, landed)


def _sum_slots(name, slots, n):
    _, R, C = slots.shape
    br = _first_divisor(R, (512, 384, 256, 128, 64, 32, 16, 8))

    def body(s_ref, o_ref):
        acc = s_ref[0].astype(F32)
        for k in range(1, n):
            acc = acc + s_ref[k].astype(F32)
        o_ref[...] = acc

    return pl.pallas_call(
        body, name=name, grid=(R // br,), in_specs=[pl.BlockSpec((n, br, C), lambda i: (0, i, 0))],
        out_specs=pl.BlockSpec((br, C), lambda i: (i, 0)), out_shape=jax.ShapeDtypeStruct((R, C), F32),
        compiler_params=_cparams(("parallel",)),
    )(slots)


def _reduce_scatter_start(tag, names, grads):
    lays = [LAYOUTS[n] for n in names]
    landed = _pair_exchange("grads_pair_" + names[0], grads, lays)
    sums = [_pair_sum("grads_pairsum_" + n, g, ld, lay, BF16) for n, g, ld, lay in zip(names, grads, landed, lays)]
    sems, sums, lands = _chip_start(tag + "_chips_start", sums)
    return tag, names, sems, sums, lands


def _reduce_scatter_finish(pending, after):
    tag, names, sems, sums, lands = pending
    own, got = _chip_wait(tag + "_chips_wait", sems, sums, lands, after)
    return [_sum_chips("grads_sum_" + n, o, s) for n, o, s in zip(names, own, got)]


def _adamw_math(w, g, m, v):
    m = ADAM_B1 * m + (1.0 - ADAM_B1) * g
    v = ADAM_B2 * v + (1.0 - ADAM_B2) * jnp.square(g)
    m_hat = m / (1.0 - ADAM_B1 ** ADAM_STEP)
    v_hat = v / (1.0 - ADAM_B2 ** ADAM_STEP)
    delta = -ADAM_LR * (m_hat / (jnp.sqrt(v_hat) + ADAM_EPS) + ADAM_WD * w)
    return delta, m, v


def _adamw_layers(name, w, totals, m, v):
    _, R, C = w.shape
    br = _first_divisor(R, (512, 176, 128, 64, 32, 16, 8))
    Cp = totals[0].shape[1]

    def body(w_ref, g0_ref, g1_ref, m_ref, v_ref, g_out, d_out, m_out, v_out):
        g = jnp.where(pl.program_id(0) == 0, g0_ref[:, 0:C], g1_ref[:, 0:C])
        delta, m_new, v_new = _adamw_math(w_ref[0], g, m_ref[0], v_ref[0])
        g_out[0], d_out[0], m_out[0], v_out[0] = g, delta, m_new, v_new

    blk = pl.BlockSpec((1, br, C), lambda l, i: (l, i, 0))
    g_spec = pl.BlockSpec((br, Cp), lambda l, i: (i, 0))
    return pl.pallas_call(
        body, name=name, grid=(DEPTH, R // br), in_specs=[blk, g_spec, g_spec, blk, blk], out_specs=[blk] * 4,
        out_shape=[jax.ShapeDtypeStruct(w.shape, F32)] * 4, compiler_params=_cparams(("parallel", "parallel")),
    )(w, totals[0], totals[1], m, v)


def _adamw(name, w, g, m, v):
    shape = w.shape
    cols = shape[-1]
    rows = int(np.prod(shape[:-1]))
    br = _first_divisor(rows, (512, 352, 256, 128, 64, 32, 16, 8))
    args = [_In(a.reshape(rows, cols)) for a in (w, g, m, v)]
    outs = _rowwise(name, _adamw_math, args, [_Out(cols), _Out(cols), _Out(cols)], rows, br)
    return [o.reshape(shape) for o in outs]


GROUPS = {"ffn1": ("ffn1_w_gate", "ffn1_w_up", "ffn1_w_down"),
          "mix": ("w_in", "w_branch_attn", "w_branch_mlstm", "w_out"),
          "ffn2": ("ffn2_w_gate", "ffn2_w_up", "ffn2_w_down")}
GATHER_GROUPS = {"ffn1_in": ("ffn1_w_gate", "ffn1_w_up"), "ffn1_out": ("ffn1_w_down",),
                 "mix": ("w_in", "w_branch_attn", "w_branch_mlstm", "w_out"),
                 "ffn2_in": ("ffn2_w_gate", "ffn2_w_up"), "ffn2_out": ("ffn2_w_down",)}


def _small_params(small, conv_w, l):
    p = {}
    for n in ("ffn1_norm", "mix_norm", "ffn2_norm", "block_out_norm", "mlstm_out_norm", "attn_q_norm", "attn_k_norm"):
        p[n] = small[n][l][None, :]
    p["attn_sink"] = small["attn_sink"][l]
    p["gate_bias"] = jnp.pad(small["mlstm_gate_bias"][l], (0, LANES - MLSTM_N_GATES))[None, :]
    taps = _qk_perm_cols(conv_w[l], 1)
    conv_b = _qk_perm_cols(small["mlstm_conv_b"][l][None, :], 1)
    p["conv_w8"] = jnp.concatenate([taps, conv_b, jnp.zeros((4, 2 * MLSTM_WIDTH), F32)], axis=0)
    return p


def _w_in_from_slots(slots):
    w_in = slots.reshape(N_DEV, D_MODEL, IN_WIDTH // N_DEV).transpose(1, 0, 2).reshape(D_MODEL, IN_WIDTH)
    return _w_in_arrange(w_in)


def _w_in_to_slots(g):
    return _w_in_restore(g).reshape(D_MODEL, N_DEV, IN_WIDTH // N_DEV).transpose(1, 0, 2).reshape(
        N_DEV * D_MODEL, IN_WIDTH // N_DEV)


def _local_step(x, positions, target, weights_of, small, conv_w, on_grads):
    B, S, _ = x.shape
    T = B * S
    cos, sin = _rope_cos_sin(positions.reshape(T, 1))
    params = [_small_params(small, conv_w, l) for l in range(DEPTH)]
    xs = x.reshape(T, D_MODEL)
    tgt = target.reshape(T, D_MODEL)

    saved = []
    for l, p in enumerate(params):
        p.update(weights_of(l, "ffn1_in", xs))
        x1, s1, p["ffn1_w_down"] = _ffn_fwd("ffn1", xs, p["ffn1_norm"], p["ffn1_w_gate"], p["ffn1_w_up"],
                                            lambda after, l=l: weights_of(l, "ffn1_out", after)["ffn1_w_down"])
        p.update(weights_of(l, "mix", x1))
        p["w_in"] = _w_in_from_slots(p["w_in"])
        x2, s2 = _mix_fwd(x1, cos, sin, B, S, p)
        p.update(weights_of(l, "ffn2_in", x2))
        x3, s3, p["ffn2_w_down"] = _ffn_fwd("ffn2", x2, p["ffn2_norm"], p["ffn2_w_gate"], p["ffn2_w_up"],
                                            lambda after, l=l: weights_of(l, "ffn2_out", after)["ffn2_w_down"])
        saved.append((s1, s2, s3, x3))
        if l + 1 < DEPTH:
            xs = _block_norm_fwd(x3, p["block_out_norm"])

    sm = {}
    loss = None
    dx = None
    for l in reversed(range(DEPTH)):
        p = params[l]
        s1, s2, s3, x3 = saved[l]
        if l == DEPTH - 1:
            loss, dx, dgn = _loss_and_grad(x3, p["block_out_norm"], tgt)
        else:
            dx, dgn = _block_norm_bwd(x3, p["block_out_norm"], dx)
        sm["block_out_norm", l] = (dgn, 0, 1)
        dx, dg = _ffn_bwd("ffn2", s3, p["ffn2_norm"], p["ffn2_w_gate"], p["ffn2_w_up"], p["ffn2_w_down"], dx,
                          functools.partial(on_grads, l, "ffn2"))
        sm["ffn2_norm", l] = (dg, 0, 1)
        dx, g = _mix_bwd(s2, cos, sin, B, S, p, dx, functools.partial(on_grads, l, "mix"))
        dconv = _qk_unperm_cols(g["conv_w8"], 1)
        sm["mlstm_conv_w", l] = (dconv, 0, 3)
        sm["mlstm_conv_b", l] = (dconv, 3, 1)
        for n, key in (("mix_norm", "mix_norm"), ("mlstm_gate_bias", "gate_bias"), ("attn_q_norm", "attn_q_norm"),
                       ("attn_k_norm", "attn_k_norm"), ("attn_sink", "attn_sink"), ("mlstm_out_norm", "mlstm_out_norm")):
            sm[n, l] = (g[key], 0, 1)
        dx, dg = _ffn_bwd("ffn1", s1, p["ffn1_norm"], p["ffn1_w_gate"], p["ffn1_w_up"], p["ffn1_w_down"], dx,
                          functools.partial(on_grads, l, "ffn1"))
        sm["ffn1_norm", l] = (dg, 0, 1)
    return loss, dx.reshape(B, S, D_MODEL), sm


def kernel(x, positions, ffn1_norm, ffn1_w_gate, ffn1_w_up, ffn1_w_down, mix_norm, w_in, mlstm_gate_bias, attn_q_norm, attn_k_norm, attn_sink, mlstm_conv_w, mlstm_conv_b, mlstm_out_norm, w_branch_attn, w_branch_mlstm, w_out, ffn2_norm, ffn2_w_gate, ffn2_w_up, ffn2_w_down, block_out_norm, loss_target, m_ffn1_norm, m_ffn1_w_gate, m_ffn1_w_up, m_ffn1_w_down, m_mix_norm, m_w_in, m_mlstm_gate_bias, m_attn_q_norm, m_attn_k_norm, m_attn_sink, m_mlstm_conv_w, m_mlstm_conv_b, m_mlstm_out_norm, m_w_branch_attn, m_w_branch_mlstm, m_w_out, m_ffn2_norm, m_ffn2_w_gate, m_ffn2_w_up, m_ffn2_w_down, m_block_out_norm, v_ffn1_norm, v_ffn1_w_gate, v_ffn1_w_up, v_ffn1_w_down, v_mix_norm, v_w_in, v_mlstm_gate_bias, v_attn_q_norm, v_attn_k_norm, v_attn_sink, v_mlstm_conv_w, v_mlstm_conv_b, v_mlstm_out_norm, v_w_branch_attn, v_w_branch_mlstm, v_w_out, v_ffn2_norm, v_ffn2_w_gate, v_ffn2_w_up, v_ffn2_w_down, v_block_out_norm):
    args = locals()
    def stored(n, t):
        return t.transpose(0, 2, 1) if n in TRANSPOSED else t

    w = {n: stored(n, args[n]) for n in WEIGHTS}
    m = {n: stored(n, args["m_" + n]) for n in WEIGHTS}
    v = {n: stored(n, args["v_" + n]) for n in WEIGHTS}

    order = [(l, grp) for l in range(DEPTH) for grp in GATHER_GROUPS]
    keys = [(l, n) for l, grp in order for n in GATHER_GROUPS[grp]]
    lays = [LAYOUTS[n] for _, n in keys]
    group_idx, at = {}, 0
    for l, grp in order:
        group_idx[(l, grp)] = list(range(at, at + len(GATHER_GROUPS[grp])))
        at += len(GATHER_GROUPS[grp])
    conv_shape = w["mlstm_conv_w"].shape
    conv_all = _all_gather("conv_all_gather", _pack_flat([w["mlstm_conv_w"]], F32, 8), vmem=True)
    conv_parts = _unpack_flat(conv_all, [conv_shape], lead=(N_DEV,))[0]
    conv_w = jnp.concatenate([conv_parts[j] for j in range(N_DEV)], axis=2)
    small = {n: w[n] for n in SMALL}

    shards, lands = [], []
    for l, grp in order:
        own, whole = _place_own("weights_place_" + grp, [w[n] for n in GATHER_GROUPS[grp]], l,
                                [lays[i] for i in group_idx[(l, grp)]])
        shards += own
        lands += whole
    n_peers = [NEAR_PEERS if (l, grp) in ((0, "ffn1_in"), (0, "ffn1_out"), (0, "mix")) else N_PEERS for l, grp in order]
    sems, shards, lands = _gather_start("weights_gather_start", shards, lands, lays, [group_idx[k] for k in order],
                                        n_peers, conv_all)

    def weights_of(l, grp, after):
        idx, g = group_idx[(l, grp)], order.index((l, grp))
        group_lays = [lays[i] for i in idx]
        whole = _gather_wait(f"weights_gather_wait_{l}_{grp}", sems[g], [shards[i] for i in idx],
                             [lands[i] for i in idx], group_lays, n_peers[g], after)
        if n_peers[g] == NEAR_PEERS:
            whole = _forward_to_sibling("weights_forward_" + grp, whole, group_lays)
        return dict(zip(GATHER_GROUPS[grp], whole))

    totals, pending = {}, []

    def finish(after):
        tag, names = pending[0][0], pending[0][1]
        for n, t in zip(names, _reduce_scatter_finish(pending.pop(0), after)):
            totals[(tag, n)] = t

    def on_grads(l, grp, g, after):
        if pending:
            finish(after)
        names = GROUPS[grp]
        pending.append(_reduce_scatter_start(f"grads_{l}_{grp}", names, [g[n] for n in names]))
        return pending[-1][3][0]

    loss, grad_x, small_g = _local_step(x, positions, loss_target, weights_of, small, conv_w, on_grads)
    finish(grad_x)
    grads, deltas, new_m, new_v = {}, {}, {}, {}
    for grp, names in GROUPS.items():
        for n in names:
            grads[n], deltas[n], new_m[n], new_v[n] = _adamw_layers(
                "adamw_" + n, w[n], [totals[(f"grads_{l}_{grp}", n)] for l in range(DEPTH)], m[n], v[n])

    n_small = DEPTH * len(SMALL)
    taps_at, loss_at, rows = n_small, n_small + 3 * DEPTH, 32
    pieces = [small_g[n, l] for n in SMALL for l in range(DEPTH)]
    pieces += [small_g["mlstm_conv_w", l] for l in range(DEPTH)] + [(loss, 0, 1)]
    small_all = _all_gather("small_all_gather", _pack_rows("small_pack", pieces, rows), vmem=True)
    small_sum = _sum_slots("small_sum", small_all, N_DEV)
    loss_total = small_sum[loss_at, 0]
    x_pos, y_pos, c_pos = _mesh_pos()
    grads["mlstm_conv_w"] = lax.dynamic_slice_in_dim(
        small_sum[taps_at:loss_at].reshape(DEPTH, 3, 2 * MLSTM_WIDTH),
        (4 * x_pos + 2 * y_pos + c_pos) * conv_shape[2], conv_shape[2], axis=2)

    n = "mlstm_conv_w"
    deltas[n], new_m[n], new_v[n] = _adamw("adamw_" + n, w[n], grads[n], m[n], v[n])
    sw, smm, sv = (_pack_rows("small_pack_" + tag, [(d[n], 0, DEPTH) for n in SMALL], rows)
                   for tag, d in (("w", w), ("m", m), ("v", v)))
    sd, snm, snv = _adamw("adamw_small", sw, small_sum, smm, sv)
    for i, n in enumerate(SMALL):
        rows_n, width = slice(DEPTH * i, DEPTH * (i + 1)), w[n].shape[1]
        grads[n], deltas[n], new_m[n], new_v[n] = (buf[rows_n, :width] for buf in (small_sum, sd, snm, snv))

    return (loss_total.reshape(()), grad_x, *[stored(n, d[n]) for d in (grads, deltas, new_m, new_v) for n in WEIGHTS])
```

```python
import functools

import numpy as np
import jax
import jax.numpy as jnp
from jax import lax
from jax.experimental import pallas as pl
from jax.experimental.pallas import tpu as pltpu

F32 = jnp.float32
BF16 = jnp.bfloat16

D_MODEL = 1024
D_FF = 2816
ATT_HEAD_DIM = 64
ATT_HEADS = 8
ATT_KV_HEADS = 2
ATT_GROUP = ATT_HEADS // ATT_KV_HEADS
ATT_WIDTH = ATT_HEADS * ATT_HEAD_DIM
ATT_KV_WIDTH = ATT_KV_HEADS * ATT_HEAD_DIM
WINDOW = 128
ATT_BLOCK = 128
ROPE_DIM = 16
ROPE_THETA = 500000.0
MLSTM_HEADS = 4
MLSTM_HEAD_DIM = 128
MLSTM_WIDTH = MLSTM_HEADS * MLSTM_HEAD_DIM
MLSTM_CHUNK = 128
MLSTM_N_GATES = 4 * MLSTM_HEADS
NORM_EPS = 1e-6
IN_WIDTH = 4880
DEPTH = 2
N_DEV = 8

ADAM_LR = 0.001
ADAM_B1 = 0.9
ADAM_B2 = 0.999
ADAM_EPS = 1e-08
ADAM_WD = 0.01
ADAM_STEP = 10

LANES = 128
C_GMERGE = 0
C_QK = 2048
C_VM = 3072
C_OM = 3584
C_QA = 4096
C_KA = 4608
C_VA = 4736
C_GATES = 4864
IN_PAD = 4992

VMEM_LIMIT = 48 * 1024 * 1024

MESH = pl.DeviceIdType.MESH


def _cparams(sem):
    return pltpu.CompilerParams(dimension_semantics=sem, vmem_limit_bytes=VMEM_LIMIT)


def _first_divisor(n, cands):
    for c in cands:
        if n % c == 0:
            return c
    return n


_NN = ((1,), (0,))
_NT = ((1,), (1,))
_TN = ((0,), (0,))


def _mm(a, b, dims):
    return lax.dot_general(a.astype(BF16), b.astype(BF16), (dims, ((), ())), preferred_element_type=F32)


@jax.custom_vjp
def mm_nn(a, b):
    return _mm(a, b, _NN)


def _mm_nn_fwd(a, b):
    return _mm(a, b, _NN), (a, b)


def _mm_nn_bwd(res, g):
    a, b = res
    return _mm(g, b, _NT).astype(a.dtype), _mm(a, g, _TN).astype(b.dtype)


mm_nn.defvjp(_mm_nn_fwd, _mm_nn_bwd)


@jax.custom_vjp
def mm_nt(a, b):
    return _mm(a, b, _NT)


def _mm_nt_fwd(a, b):
    return _mm(a, b, _NT), (a, b)


def _mm_nt_bwd(res, g):
    a, b = res
    return _mm(g, b, _NN).astype(a.dtype), _mm(g, a, _TN).astype(b.dtype)


mm_nt.defvjp(_mm_nt_fwd, _mm_nt_bwd)


@jax.custom_vjp
def mm_tn(a, b):
    return _mm(a, b, _TN)


def _mm_tn_fwd(a, b):
    return _mm(a, b, _TN), (a, b)


def _mm_tn_bwd(res, g):
    a, b = res
    return _mm(b, g, _NT).astype(a.dtype), _mm(a, g, _NN).astype(b.dtype)


mm_tn.defvjp(_mm_tn_fwd, _mm_tn_bwd)


def _matmul(name, a, b, mode, out_dtype=F32, res=None, scale=1.0, bl=None, dep=None, whole_k=False):
    b_shape = b.shape if bl is None else b.shape[1:]
    if mode == "nn":
        (M, K), (K2, N) = a.shape, b_shape
    elif mode == "nt":
        (M, K), (N, K2) = a.shape, b_shape
    else:
        (K, M), (K2, N) = a.shape, b_shape
    assert K == K2, (name, a.shape, b.shape)
    tm = _first_divisor(M, (1024, 512, 384, 256, 128))
    tn = _first_divisor(N, (1024, 1664, 512, 384, 256, 128))
    tk = K if whole_k else _first_divisor(K, (1024, 1664, 512, 256, 128))
    if whole_k:
        tn = min(tn, 512)
    nk = K // tk
    if mode == "tn":
        a_spec = pl.BlockSpec((tk, tm), lambda i, j, k: (k, i))
    else:
        a_spec = pl.BlockSpec((tm, tk), lambda i, j, k: (i, k))
    if mode == "nt":
        b_blk, b_idx = (tn, tk), (lambda i, j, k: (j, k))
    else:
        b_blk, b_idx = (tk, tn), (lambda i, j, k: (k, j))
    if bl is None:
        b_spec = pl.BlockSpec(b_blk, b_idx)
    else:
        b_spec = pl.BlockSpec((None,) + b_blk, lambda i, j, k: (bl,) + b_idx(i, j, k))
    o_spec = pl.BlockSpec((tm, tn), lambda i, j, k: (i, j))
    dims = {"nn": _NN, "nt": _NT, "tn": _TN}[mode]
    has_res = res is not None

    def body(*refs):
        a_ref, b_ref = refs[:2]
        r_ref = refs[2] if has_res else None

        def finish(out):
            if scale != 1.0:
                out = out * scale
            if has_res:
                out = r_ref[...].astype(F32) + out
            o_ref[...] = out.astype(out_dtype)

        if nk == 1:
            o_ref = refs[-1]
            finish(_mm(a_ref[...], b_ref[...], dims))
            return
        o_ref, acc = refs[-2:]
        k = pl.program_id(2)

        @pl.when(k == 0)
        def _():
            acc[...] = jnp.zeros_like(acc)

        acc[...] += _mm(a_ref[...], b_ref[...], dims)

        @pl.when(k == nk - 1)
        def _():
            finish(acc[...])

    in_specs = [a_spec, b_spec] + ([o_spec] if has_res else [])
    args = tuple(_in_hbm(t) for t in (a, b) + ((res,) if has_res else ()))
    if dep is not None:
        in_specs.append(pl.BlockSpec(memory_space=pl.ANY))
        args += (dep,)
    return pl.pallas_call(
        body, name=name, grid=(M // tm, N // tn, nk), in_specs=in_specs, out_specs=o_spec,
        out_shape=jax.ShapeDtypeStruct((M, N), out_dtype),
        scratch_shapes=[pltpu.VMEM((tm, tn), F32)] if nk > 1 else [],
        compiler_params=_cparams(("parallel", "parallel", "arbitrary")),
    )(*args)


def _in_hbm(arr):
    return pltpu.with_memory_space_constraint(arr, pltpu.HBM)


class _In:
    def __init__(self, arr, width=None, base=0, split=False, rows=True):
        self.arr, self.base, self.split, self.rows = arr, base, split, rows
        self.width = arr.shape[1] if width is None else width


class _Out:
    def __init__(self, cols, dtype=F32, width=None, split=False, rows=True, nrows=1, into=None, base=0):
        self.cols, self.dtype, self.split, self.rows, self.nrows = cols, dtype, split, rows, nrows
        self.width = cols if width is None else width
        self.into, self.base = into, base
        if into is not None:
            self.cols, self.dtype = into.shape[1], into.dtype


def _rowwise(name, fn, ins, outs, n_rows, br, ncol=1):
    br = min(br, n_rows)
    assert n_rows % br == 0, (name, n_rows, br)
    nrow_blocks = n_rows // br

    def in_spec(d):
        nb = br if d.rows else d.arr.shape[0]
        if d.rows and d.split:
            im = lambda j, i, base=d.base: (i, base + j)
        elif d.rows:
            im = lambda j, i, base=d.base: (i, base)
        elif d.split:
            im = lambda j, i, base=d.base: (0, base + j)
        else:
            im = lambda j, i, base=d.base: (0, base)
        return pl.BlockSpec((nb, d.width), im)

    def out_spec(d):
        nb = br if d.rows else d.nrows
        if d.rows and d.split:
            im = lambda j, i, base=d.base: (i, base + j)
        elif d.rows:
            im = lambda j, i, base=d.base: (i, base)
        elif d.split:
            im = lambda j, i: (0, j)
        else:
            im = lambda j, i: (0, 0)
        return pl.BlockSpec((nb, d.width), im)

    n_in = len(ins)
    targets = [(k, d.into) for k, d in enumerate(outs) if d.into is not None]

    def body(*refs):
        i = pl.program_id(1)
        vals = [r[...] for r in refs[:n_in]]
        res = fn(*vals)
        if not isinstance(res, (tuple, list)):
            res = (res,)
        for d, ref, val in zip(outs, refs[n_in + len(targets):], res):
            if d.rows:
                ref[...] = val.astype(d.dtype)
            else:
                @pl.when(i == 0)
                def _(ref=ref):
                    ref[...] = jnp.zeros_like(ref)

                ref[...] += val.astype(d.dtype)

    out_shape = [jax.ShapeDtypeStruct((n_rows if d.rows else d.nrows, d.cols), d.dtype) for d in outs]
    res = pl.pallas_call(
        body, name=name, grid=(ncol, nrow_blocks),
        in_specs=[in_spec(d) for d in ins] + [pl.BlockSpec(memory_space=pl.ANY)] * len(targets),
        out_specs=[out_spec(d) for d in outs], out_shape=out_shape,
        input_output_aliases={n_in + t: k for t, (k, _) in enumerate(targets)},
        compiler_params=_cparams(("parallel", "arbitrary")),
    )(*[_in_hbm(d.arr) for d in ins], *[arr for _, arr in targets])
    return res


def _rms(x, g):
    return x * lax.rsqrt(jnp.mean(x * x, axis=-1, keepdims=True) + NORM_EPS) * g


def _sigmoid(x):
    return 0.5 * jnp.tanh(0.5 * x) + 0.5


def _silu(x):
    return x * _sigmoid(x)


def _log_sigmoid(x):
    return jnp.minimum(x, 0.0) - jnp.log(1.0 + jnp.exp(-jnp.abs(x)))


def _rope_tables(pos, inv_freq_row):
    ang = pos.astype(F32) * inv_freq_row
    return jnp.cos(ang), jnp.sin(ang)


def _head_sums_impl(v):
    w = v.shape[-1]
    shift = ATT_HEAD_DIM.bit_length() - 1
    r = lax.shift_right_logical(lax.broadcasted_iota(jnp.int32, (w, w), 0), shift)
    c = lax.shift_right_logical(lax.broadcasted_iota(jnp.int32, (w, w), 1), shift)
    ones = (r == c).astype(BF16)
    hi = v.astype(BF16)
    lo = (v - hi.astype(F32)).astype(BF16)
    dn = (_NN, ((), ()))
    return (lax.dot_general(hi, ones, dn, preferred_element_type=F32)
            + lax.dot_general(lo, ones, dn, preferred_element_type=F32))


@jax.custom_vjp
def _head_sums(v):
    return _head_sums_impl(v)


_head_sums.defvjp(lambda v: (_head_sums_impl(v), None), lambda _, g: (_head_sums_impl(g),))


def _rotate_half_impl(y):
    w = y.shape[-1]
    half = ROPE_DIM // 2
    lane = lax.broadcasted_iota(jnp.int32, y.shape, 1) & (ATT_HEAD_DIM - 1)
    above = pltpu.roll(y, w - half, axis=1)
    below = pltpu.roll(y, half, axis=1)
    return jnp.where(lane < half, -above, jnp.where(lane < ROPE_DIM, below, 0.0))


@jax.custom_vjp
def _rotate_half(y):
    return _rotate_half_impl(y)


_rotate_half.defvjp(lambda y: (_rotate_half_impl(y), None), lambda _, g: (-_rotate_half_impl(g),))


def _qk_prep(t, g, cos, sin):
    reps = t.shape[-1] // cos.shape[-1]
    if reps > 1:
        cos, sin = jnp.tile(cos, (1, reps)), jnp.tile(sin, (1, reps))
    y = t * lax.rsqrt(_head_sums(t * t) * (1.0 / ATT_HEAD_DIM) + NORM_EPS) * g
    return y * cos + _rotate_half(y) * sin


def _attn_head(q, kb, vb, sink, valid):
    s = mm_nt(q, kb) * (ATT_HEAD_DIM ** -0.5)
    s = jnp.where(valid, s, -jnp.inf)
    m = jnp.maximum(jnp.max(s, axis=-1, keepdims=True), sink)
    p = jnp.exp(s - m)
    den = jnp.sum(p, axis=-1, keepdims=True) + jnp.exp(sink - m)
    return mm_nn(p * (1.0 / den), vb)


def _mlstm_chunk(q, k, v, li, lf, C, n, m, incl, incl_t, eye):
    k = k * (MLSTM_HEAD_DIM ** -0.5)
    lf_row = jnp.sum(eye * lf, axis=0, keepdims=True)
    li_row = jnp.sum(eye * li, axis=0, keepdims=True)
    b = jnp.sum(incl * lf_row, axis=1, keepdims=True)
    b_row = jnp.sum(incl_t * lf, axis=0, keepdims=True)
    b_tot = jnp.sum(lf, axis=0, keepdims=True)
    a = b_tot - b + li
    a_max = jnp.max(a, axis=0, keepdims=True)
    kw = k * jnp.exp(a - a_max)
    c_loc = mm_tn(kw, v)
    n_loc = jnp.sum(kw, axis=0, keepdims=True)

    dmat = jnp.where(incl > 0.5, b - b_row + li_row, -jnp.inf)
    inter = b + m
    m_t = jnp.maximum(inter, jnp.max(dmat, axis=1, keepdims=True))
    sc = mm_nt(q, k) * jnp.exp(dmat - m_t)
    scale_in = jnp.exp(inter - m_t)
    num = mm_nn(sc, v) + scale_in * mm_nn(q, C)
    den = jnp.sum(sc, axis=1, keepdims=True) + scale_in * jnp.sum(q * n, axis=1, keepdims=True)
    h = num * (1.0 / jnp.maximum(jnp.abs(den), jnp.exp(-m_t)))

    m_new = jnp.maximum(b_tot + m, a_max)
    s_p = jnp.exp(b_tot + m - m_new)
    s_l = jnp.exp(a_max - m_new)
    return h, s_p * C + s_l * c_loc, s_p * n + s_l * n_loc, m_new


def _mlstm_combine(hf, hb, o_pre, g):
    h = hf + hb
    mu = jnp.mean(h, axis=-1, keepdims=True)
    var = jnp.mean(jnp.square(h - mu), axis=-1, keepdims=True)
    return _sigmoid(o_pre) * ((h - mu) * lax.rsqrt(var + NORM_EPS) * g)


def _merge(ga, gm, za, zm):
    return _sigmoid(ga) * za + _sigmoid(gm) * zm


def _attn_mask(n, seq):
    shape = (ATT_GROUP * ATT_BLOCK, 3 * ATT_BLOCK)
    qi = n * ATT_BLOCK + (lax.broadcasted_iota(jnp.int32, shape, 0) & (ATT_BLOCK - 1))
    kj = (n - 1) * ATT_BLOCK + lax.broadcasted_iota(jnp.int32, shape, 1)
    return (jnp.abs(qi - kj) <= WINDOW) & (kj >= 0) & (kj < seq)


def _attn_specs(nq, v_base):
    q_spec = pl.BlockSpec((1, ATT_BLOCK, ATT_WIDTH), lambda b, n: (b, n, 0))

    def kv_spec(off, base=0):
        return pl.BlockSpec((1, ATT_BLOCK, ATT_KV_WIDTH), lambda b, n: (b, jnp.clip(n + off, 0, nq - 1), base))

    sink_spec = pl.BlockSpec((ATT_KV_HEADS, ATT_GROUP, 1, 1), lambda b, n: (0, 0, 0, 0))
    specs = [q_spec, kv_spec(-1), kv_spec(0), kv_spec(1), kv_spec(-1, v_base), kv_spec(0, v_base), kv_spec(1, v_base), sink_spec]
    return q_spec, specs, sink_spec


def _head(h):
    return slice(h * ATT_HEAD_DIM, (h + 1) * ATT_HEAD_DIM)


def _group_rows(q_ref, s_ref, h):
    q4 = jnp.concatenate([q_ref[0, :, _head(h * ATT_GROUP + g)] for g in range(ATT_GROUP)], axis=0)
    sink4 = jnp.concatenate([jnp.broadcast_to(s_ref[h, g], (ATT_BLOCK, 1)) for g in range(ATT_GROUP)], axis=0)
    return q4, sink4


def _attn_fwd(q, k, proj3, sink):
    B, S, _ = q.shape
    nq = S // ATT_BLOCK
    q_spec, specs, _ = _attn_specs(nq, C_VA // ATT_KV_WIDTH)

    def body(q_ref, kp, kc, kn, vp, vc, vn, s_ref, o_ref):
        valid = _attn_mask(pl.program_id(1), S)
        for h in range(ATT_KV_HEADS):
            kb = jnp.concatenate([kp[0, :, _head(h)], kc[0, :, _head(h)], kn[0, :, _head(h)]], axis=0)
            vb = jnp.concatenate([vp[0, :, _head(h)], vc[0, :, _head(h)], vn[0, :, _head(h)]], axis=0)
            q4, sink4 = _group_rows(q_ref, s_ref, h)
            o4 = _attn_head(q4, kb, vb, sink4, valid).astype(BF16)
            for g in range(ATT_GROUP):
                o_ref[0, :, _head(h * ATT_GROUP + g)] = o4[g * ATT_BLOCK:(g + 1) * ATT_BLOCK]

    return pl.pallas_call(
        body, name="attn_fwd", grid=(B, nq), in_specs=specs,
        out_specs=q_spec, out_shape=jax.ShapeDtypeStruct(q.shape, BF16),
        compiler_params=_cparams(("parallel", "arbitrary")),
    )(q, k, k, k, proj3, proj3, proj3, sink)


def _attn_bwd(q, k, proj3, sink, dy):
    B, S, _ = q.shape
    nq = S // ATT_BLOCK
    q_spec, specs, sink_spec = _attn_specs(nq, C_VA // ATT_KV_WIDTH)
    kv_full = pl.BlockSpec((1, S, ATT_KV_WIDTH), lambda b, n: (b, 0, 0))

    def body(q_ref, kp, kc, kn, vp, vc, vn, s_ref, dy_ref, dq_ref, dk_ref, dv_ref, ds_ref):
        b, n = pl.program_id(0), pl.program_id(1)
        valid = _attn_mask(n, S)

        @pl.when(n == 0)
        def _():
            dk_ref[...] = jnp.zeros_like(dk_ref)
            dv_ref[...] = jnp.zeros_like(dv_ref)

        @pl.when((n == 0) & (b == 0))
        def _():
            ds_ref[...] = jnp.zeros_like(ds_ref)

        for h in range(ATT_KV_HEADS):
            kb = jnp.concatenate([kp[0, :, _head(h)], kc[0, :, _head(h)], kn[0, :, _head(h)]], axis=0)
            vb = jnp.concatenate([vp[0, :, _head(h)], vc[0, :, _head(h)], vn[0, :, _head(h)]], axis=0)
            q4, sink4 = _group_rows(q_ref, s_ref, h)
            dy4 = jnp.concatenate([dy_ref[0, :, _head(h * ATT_GROUP + g)] for g in range(ATT_GROUP)], axis=0)
            _, vjp = jax.vjp(functools.partial(_attn_head, valid=valid), q4, kb, vb, sink4)
            dq4, dkb, dvb, dsink4 = vjp(dy4)
            for g in range(ATT_GROUP):
                rows = slice(g * ATT_BLOCK, (g + 1) * ATT_BLOCK)
                dq_ref[0, :, _head(h * ATT_GROUP + g)] = dq4[rows]
                ds_ref[h, g] += jnp.sum(dsink4[rows], axis=0, keepdims=True)
            for j, off in enumerate((-1, 0, 1)):
                start = pl.multiple_of(jnp.clip(n + off, 0, nq - 1) * ATT_BLOCK, ATT_BLOCK)
                rows = pl.ds(start, ATT_BLOCK)
                dk_ref[0, rows, _head(h)] += dkb[j * ATT_BLOCK:(j + 1) * ATT_BLOCK]
                dv_ref[0, rows, _head(h)] += dvb[j * ATT_BLOCK:(j + 1) * ATT_BLOCK]

    kv_shape = jax.ShapeDtypeStruct(k.shape, F32)
    return pl.pallas_call(
        body, name="attn_bwd", grid=(B, nq), in_specs=specs + [q_spec],
        out_specs=[q_spec, kv_full, kv_full, sink_spec],
        out_shape=[jax.ShapeDtypeStruct(q.shape, F32), kv_shape, kv_shape, jax.ShapeDtypeStruct(sink.shape, F32)],
        compiler_params=_cparams(("arbitrary", "arbitrary")),
    )(q, k, k, k, proj3, proj3, proj3, sink, dy)


CONV_COLS = 256


def _conv_taps(u, seq):
    row = lax.broadcasted_iota(jnp.int32, u.shape, 0)
    prev = jnp.where(row == 0, 0.0, pltpu.roll(u, 1, axis=0))
    nxt = jnp.where(row == seq - 1, 0.0, pltpu.roll(u, seq - 1, axis=0))
    return prev, nxt


def _conv_fwd(proj3, w8):
    B, S, _ = proj3.shape
    ncb = 2 * MLSTM_WIDTH // CONV_COLS

    def body(u_ref, w_ref, o_ref):
        u = u_ref[0]
        prev, nxt = _conv_taps(u, S)
        o_ref[0] = _silu(prev * w_ref[0:1, :] + u * w_ref[1:2, :] + nxt * w_ref[2:3, :] + w_ref[3:4, :])

    return pl.pallas_call(
        body, name="conv_fwd", grid=(B, ncb),
        in_specs=[pl.BlockSpec((1, S, CONV_COLS), lambda b, c: (b, 0, C_QK // CONV_COLS + c)),
                  pl.BlockSpec((8, CONV_COLS), lambda b, c: (0, c))],
        out_specs=pl.BlockSpec((1, S, CONV_COLS), lambda b, c: (b, 0, c)),
        out_shape=jax.ShapeDtypeStruct((B, S, 2 * MLSTM_WIDTH), F32),
        compiler_params=_cparams(("parallel", "parallel")),
    )(proj3, w8)


def _conv_bwd(proj3, w8, dout_f, dout_b):
    B, S, _ = proj3.shape
    ncb = 2 * MLSTM_WIDTH // CONV_COLS

    def body(u_ref, w_ref, df_ref, db_ref, du_ref, dw_ref):
        b = pl.program_id(1)
        u = u_ref[0]
        prev, nxt = _conv_taps(u, S)
        w0, w1, w2 = w_ref[0:1, :], w_ref[1:2, :], w_ref[2:3, :]
        pre = prev * w0 + u * w1 + nxt * w2 + w_ref[3:4, :]
        sig = _sigmoid(pre)
        dpre = (df_ref[0] + db_ref[0]) * (sig * (1.0 + pre * (1.0 - sig)))
        dprev, dnxt = _conv_taps(dpre, S)
        du_ref[0] = (dnxt * w0 + dpre * w1 + dprev * w2).astype(BF16)

        @pl.when(b == 0)
        def _():
            dw_ref[...] = jnp.zeros_like(dw_ref)

        dw_ref[0:1, :] += jnp.sum(dpre * prev, axis=0, keepdims=True)
        dw_ref[1:2, :] += jnp.sum(dpre * u, axis=0, keepdims=True)
        dw_ref[2:3, :] += jnp.sum(dpre * nxt, axis=0, keepdims=True)
        dw_ref[3:4, :] += jnp.sum(dpre, axis=0, keepdims=True)

    blk = pl.BlockSpec((1, S, CONV_COLS), lambda c, b: (b, 0, c))
    return pl.pallas_call(
        body, name="conv_bwd", grid=(ncb, B),
        in_specs=[pl.BlockSpec((1, S, CONV_COLS), lambda c, b: (b, 0, C_QK // CONV_COLS + c)),
                  pl.BlockSpec((8, CONV_COLS), lambda c, b: (0, c)), blk, blk],
        out_specs=[blk, pl.BlockSpec((8, CONV_COLS), lambda c, b: (0, c))],
        out_shape=[jax.ShapeDtypeStruct((B, S, 2 * MLSTM_WIDTH), BF16), jax.ShapeDtypeStruct((8, 2 * MLSTM_WIDTH), F32)],
        compiler_params=_cparams(("parallel", "arbitrary")),
    )(proj3, w8, dout_f, dout_b)


MLSTM_HEADS_PER_STEP = 4


def _chunk_masks(direction):
    t = lax.broadcasted_iota(jnp.int32, (MLSTM_CHUNK, MLSTM_CHUNK), 0)
    s = lax.broadcasted_iota(jnp.int32, (MLSTM_CHUNK, MLSTM_CHUNK), 1)
    le, ge = (s <= t).astype(F32), (s >= t).astype(F32)
    eye = (s == t).astype(F32)
    return (le, ge, eye) if direction == 0 else (ge, le, eye)


def _gate_cols(gates, direction, head):
    lane = lax.broadcasted_iota(jnp.int32, gates.shape, 1)
    sel_i = (lane == (2 * direction) * MLSTM_HEADS + head).astype(F32)
    sel_f = (lane == (2 * direction + 1) * MLSTM_HEADS + head).astype(F32)
    return sel_i, sel_f


def _mlstm_fwd(qk, proj3, bias):
    B, S, _ = qk.shape
    nc = S // MLSTM_CHUNK
    H, L, DH = MLSTM_HEADS, MLSTM_CHUNK, MLSTM_HEAD_DIM

    def chunk_of(d, c):
        return c if d == 0 else nc - 1 - c

    HS = MLSTM_HEADS_PER_STEP

    def body(qkf, qkb, vf, vb, gf, gb, bias_ref, hf, hb, csf, csb, nsf, nsb, msf, msb, c_st, n_st, m_st):
        c, hg = pl.program_id(1), pl.program_id(2)

        @pl.when(c == 0)
        def _():
            for d in range(2):
                for j in range(HS):
                    c_st[d, hg * HS + j] = jnp.zeros((DH, DH), F32)
                    n_st[d, hg * HS + j] = jnp.zeros((1, DH), F32)
                    m_st[d, hg * HS + j] = jnp.zeros((1, DH), F32)

        for d, (qk_ref, v_ref, g_ref, h_ref, cs, ns, ms) in enumerate(
                ((qkf, vf, gf, hf, csf, nsf, msf), (qkb, vb, gb, hb, csb, nsb, msb))):
            incl, incl_t, eye = _chunk_masks(d)
            gates = g_ref[0] + bias_ref[...]
            log_f = _log_sigmoid(gates)
            for j in range(HS):
                h = hg * HS + j
                sel_i, sel_f = _gate_cols(gates, d, h)
                li = jnp.sum(gates * sel_i, axis=1, keepdims=True)
                lf = jnp.sum(log_f * sel_f, axis=1, keepdims=True)
                c_in, n_in, m_in = c_st[d, h], n_st[d, h], m_st[d, h]
                cs[0, 0, j], ns[0, 0, j], ms[0, 0, j] = c_in, n_in, m_in
                hh, c_new, n_new, m_new = _mlstm_chunk(
                    qk_ref[0, :, 2 * j * DH:(2 * j + 1) * DH], qk_ref[0, :, (2 * j + 1) * DH:(2 * j + 2) * DH],
                    v_ref[0, :, j * DH:(j + 1) * DH], li, lf, c_in, n_in,
                    jnp.max(m_in, axis=1, keepdims=True), incl, incl_t, eye)
                h_ref[0, :, j * DH:(j + 1) * DH] = hh
                c_st[d, h], n_st[d, h] = c_new, n_new
                m_st[d, h] = jnp.broadcast_to(m_new, (1, DH))

    def tok_spec(width, base, d, per_head):
        return pl.BlockSpec((1, L, width), lambda b, c, h: (b, chunk_of(d, c), base + (h if per_head else 0)))

    def st_spec(shape, d):
        return pl.BlockSpec((1, 1, HS) + shape, lambda b, c, h: (b, chunk_of(d, c), h, 0, 0))

    in_specs = [tok_spec(2 * HS * DH, 0, 0, True), tok_spec(2 * HS * DH, 0, 1, True),
                tok_spec(HS * DH, C_VM // (HS * DH), 0, True), tok_spec(HS * DH, C_VM // (HS * DH), 1, True),
                tok_spec(LANES, C_GATES // LANES, 0, False), tok_spec(LANES, C_GATES // LANES, 1, False),
                pl.BlockSpec((1, LANES), lambda b, c, h: (0, 0))]
    out_specs = [tok_spec(HS * DH, 0, 0, True), tok_spec(HS * DH, 0, 1, True),
                 st_spec((DH, DH), 0), st_spec((DH, DH), 1), st_spec((1, DH), 0), st_spec((1, DH), 1),
                 st_spec((1, DH), 0), st_spec((1, DH), 1)]
    hs = jax.ShapeDtypeStruct((B, S, H * DH), F32)
    cs = jax.ShapeDtypeStruct((B, nc, H, DH, DH), F32)
    vs = jax.ShapeDtypeStruct((B, nc, H, 1, DH), F32)
    return pl.pallas_call(
        body, name="mlstm_fwd", grid=(B, nc, H // HS), in_specs=in_specs, out_specs=out_specs,
        out_shape=[hs, hs, cs, cs, vs, vs, vs, vs],
        scratch_shapes=[pltpu.VMEM((2, H, DH, DH), F32), pltpu.VMEM((2, H, 1, DH), F32), pltpu.VMEM((2, H, 1, DH), F32)],
        compiler_params=_cparams(("parallel", "arbitrary", "arbitrary")),
    )(qk, qk, proj3, proj3, proj3, proj3, bias)


def _mlstm_bwd(qk, proj3, bias, states, dh):
    B, S, _ = qk.shape
    nc = S // MLSTM_CHUNK
    H, L, DH = MLSTM_HEADS, MLSTM_CHUNK, MLSTM_HEAD_DIM

    def chunk_of(d, c):
        return nc - 1 - c if d == 0 else c

    HS = MLSTM_HEADS_PER_STEP

    def body(qkf, qkb, vf, vb, gf, gb, bias_ref, csf, csb, nsf, nsb, msf, msb, dhf, dhb,
             dqkf, dqkb, dvf, dvb, dgf, dgb, dc_st, dn_st, dm_st):
        c, hg = pl.program_id(1), pl.program_id(2)

        @pl.when(c == 0)
        def _():
            for d in range(2):
                for j in range(HS):
                    dc_st[d, hg * HS + j] = jnp.zeros((DH, DH), F32)
                    dn_st[d, hg * HS + j] = jnp.zeros((1, DH), F32)
                    dm_st[d, hg * HS + j] = jnp.zeros((1, DH), F32)

        @pl.when(hg == 0)
        def _():
            dgf[...] = jnp.zeros_like(dgf)
            dgb[...] = jnp.zeros_like(dgb)

        for d, (qk_ref, v_ref, g_ref, cs, ns, ms, dh_ref, dqk_ref, dv_ref, dg_ref) in enumerate(
                ((qkf, vf, gf, csf, nsf, msf, dhf, dqkf, dvf, dgf), (qkb, vb, gb, csb, nsb, msb, dhb, dqkb, dvb, dgb))):
            incl, incl_t, eye = _chunk_masks(d)
            gates = g_ref[0] + bias_ref[...]
            log_f = _log_sigmoid(gates)
            d_li = jnp.zeros_like(gates)
            d_lf = jnp.zeros_like(gates)
            for j in range(HS):
                h = hg * HS + j
                sel_i, sel_f = _gate_cols(gates, d, h)
                li = jnp.sum(gates * sel_i, axis=1, keepdims=True)
                lf = jnp.sum(log_f * sel_f, axis=1, keepdims=True)
                m_in = jnp.max(ms[0, 0, j], axis=1, keepdims=True)
                _, vjp = jax.vjp(
                    functools.partial(_mlstm_chunk, incl=incl, incl_t=incl_t, eye=eye),
                    qk_ref[0, :, 2 * j * DH:(2 * j + 1) * DH], qk_ref[0, :, (2 * j + 1) * DH:(2 * j + 2) * DH],
                    v_ref[0, :, j * DH:(j + 1) * DH], li, lf, cs[0, 0, j], ns[0, 0, j], m_in)
                dm_out = jnp.max(dm_st[d, h], axis=1, keepdims=True)
                dq, dk, dv, dli, dlf, dc, dn, dm = vjp((dh_ref[0, :, j * DH:(j + 1) * DH], dc_st[d, h], dn_st[d, h], dm_out))
                dqk_ref[0, :, 2 * j * DH:(2 * j + 1) * DH] = dq
                dqk_ref[0, :, (2 * j + 1) * DH:(2 * j + 2) * DH] = dk
                dv_ref[0, :, j * DH:(j + 1) * DH] = dv
                d_li += dli * sel_i
                d_lf += dlf * sel_f
                dc_st[d, h], dn_st[d, h] = dc, dn
                dm_st[d, h] = jnp.broadcast_to(dm, (1, DH))
            dg_ref[0] += d_li + d_lf * _sigmoid(-gates)

    def tok_spec(width, base, d, per_head):
        return pl.BlockSpec((1, L, width), lambda b, c, h: (b, chunk_of(d, c), base + (h if per_head else 0)))

    def st_spec(shape, d):
        return pl.BlockSpec((1, 1, HS) + shape, lambda b, c, h: (b, chunk_of(d, c), h, 0, 0))

    in_specs = [tok_spec(2 * HS * DH, 0, 0, True), tok_spec(2 * HS * DH, 0, 1, True),
                tok_spec(HS * DH, C_VM // (HS * DH), 0, True), tok_spec(HS * DH, C_VM // (HS * DH), 1, True),
                tok_spec(LANES, C_GATES // LANES, 0, False), tok_spec(LANES, C_GATES // LANES, 1, False),
                pl.BlockSpec((1, LANES), lambda b, c, h: (0, 0)),
                st_spec((DH, DH), 0), st_spec((DH, DH), 1), st_spec((1, DH), 0), st_spec((1, DH), 1),
                st_spec((1, DH), 0), st_spec((1, DH), 1), tok_spec(HS * DH, 0, 0, True), tok_spec(HS * DH, 0, 1, True)]
    out_specs = [tok_spec(2 * HS * DH, 0, 0, True), tok_spec(2 * HS * DH, 0, 1, True),
                 tok_spec(HS * DH, 0, 0, True), tok_spec(HS * DH, 0, 1, True),
                 tok_spec(LANES, 0, 0, False), tok_spec(LANES, 0, 1, False)]
    qks = jax.ShapeDtypeStruct((B, S, 2 * H * DH), F32)
    vs = jax.ShapeDtypeStruct((B, S, H * DH), F32)
    gs = jax.ShapeDtypeStruct((B, S, LANES), F32)
    csf, csb, nsf, nsb, msf, msb = states
    return pl.pallas_call(
        body, name="mlstm_bwd", grid=(B, nc, H // HS), in_specs=in_specs, out_specs=out_specs,
        out_shape=[qks, qks, vs, vs, gs, gs],
        scratch_shapes=[pltpu.VMEM((2, H, DH, DH), F32), pltpu.VMEM((2, H, 1, DH), F32), pltpu.VMEM((2, H, 1, DH), F32)],
        compiler_params=_cparams(("parallel", "arbitrary", "arbitrary")),
    )(qk, qk, proj3, proj3, proj3, proj3, bias, csf, csb, nsf, nsb, msf, msb, dh, dh)


ROW_BLOCK = 256
FF_COLS = 512
FF_SHARD = D_FF // N_DEV
FF_SHARD_PAD = 384
FF_PAD = N_DEV * FF_SHARD_PAD


def _rms_bwd(name, x, g, dh, dres):
    T = x.shape[0]

    def fn(xv, gv, dhv, drv):
        _, vjp = jax.vjp(_rms, xv, gv)
        dx, dg = vjp(dhv)
        return drv + dx, dg

    return _rowwise(name, fn, [_In(x), _In(g, rows=False), _In(dh), _In(dres)],
                    [_Out(D_MODEL), _Out(D_MODEL, rows=False)], T, ROW_BLOCK)


def _mmw(name, a, w, mode, **kw):
    if isinstance(w, tuple):
        return _matmul(name, a, w[0], mode, bl=w[1], **kw)
    return _matmul(name, a, w, mode, **kw)


def _swiglu(gate, up):
    return _silu(gate) * up


def _ffn_in(name, x, gain, wg, wu):
    (M, K), N = x.shape, wg.shape[0]
    tm, tn = _first_divisor(M, (1024, 512, 256, 128)), FF_COLS

    def body(x_ref, gain_ref, wg_ref, wu_ref, h_ref, g_ref, u_ref, a_ref):
        @pl.when(pl.program_id(1) == 0)
        def _():
            h_ref[...] = _rms(x_ref[...], gain_ref[...]).astype(BF16)

        hv = h_ref[...]
        gate = _mm(hv, wg_ref[...], _NT)
        up = _mm(hv, wu_ref[...], _NT)
        g_ref[...], u_ref[...] = gate.astype(BF16), up.astype(BF16)
        a_ref[...] = _swiglu(gate, up).astype(BF16)

    row_spec = pl.BlockSpec((tm, K), lambda i, j: (i, 0))
    w_spec = pl.BlockSpec((tn, K), lambda i, j: (j, 0))
    o_spec = pl.BlockSpec((tm, tn), lambda i, j: (i, j))
    return pl.pallas_call(
        body, name=name, grid=(M // tm, N // tn),
        in_specs=[row_spec, pl.BlockSpec((1, K), lambda i, j: (0, 0)), w_spec, w_spec],
        out_specs=[row_spec, o_spec, o_spec, o_spec],
        out_shape=[jax.ShapeDtypeStruct((M, K), BF16)] + [jax.ShapeDtypeStruct((M, N), BF16)] * 3,
        compiler_params=_cparams(("parallel", "arbitrary")),
    )(x, gain, wg, wu)


def _norm_matmul(name, x, gain, w):
    (M, K), N = x.shape, w.shape[1]
    tm = _first_divisor(M, (1024, 512, 256, 128))
    tn = _first_divisor(N, (1664, 1024, 512, 384, 256, 128))

    def body(x_ref, gain_ref, w_ref, h_ref, o_ref):
        @pl.when(pl.program_id(1) == 0)
        def _():
            h_ref[...] = _rms(x_ref[...], gain_ref[...]).astype(BF16)

        o_ref[...] = _mm(h_ref[...], w_ref[...], _NN)

    row_spec = pl.BlockSpec((tm, K), lambda i, j: (i, 0))
    return pl.pallas_call(
        body, name=name, grid=(M // tm, N // tn),
        in_specs=[row_spec, pl.BlockSpec((1, K), lambda i, j: (0, 0)), pl.BlockSpec((K, tn), lambda i, j: (0, j))],
        out_specs=[row_spec, pl.BlockSpec((tm, tn), lambda i, j: (i, j))],
        out_shape=[jax.ShapeDtypeStruct((M, K), BF16), jax.ShapeDtypeStruct((M, N), F32)],
        compiler_params=_cparams(("parallel", "arbitrary")),
    )(x, gain, w)


def _ffn_dact(name, dx, wd, gate, up):
    (M, K), N = dx.shape, wd.shape[0]
    tm, tn = _first_divisor(M, (1024, 512, 256, 128)), FF_COLS

    def body(dx_ref, wd_ref, g_ref, u_ref, dg_ref, du_ref):
        dact = _mm(dx_ref[...], wd_ref[...], _NT) * 0.5
        gate, up = g_ref[...].astype(F32), u_ref[...].astype(F32)
        s = _sigmoid(gate)
        silu = gate * s
        dg_ref[...] = (dact * up * (s + silu * (1.0 - s))).astype(BF16)
        du_ref[...] = (dact * silu).astype(BF16)

    o_spec = pl.BlockSpec((tm, tn), lambda i, j: (i, j))
    return pl.pallas_call(
        body, name=name, grid=(M // tm, N // tn),
        in_specs=[pl.BlockSpec((tm, K), lambda i, j: (i, 0)), pl.BlockSpec((tn, K), lambda i, j: (j, 0)), o_spec, o_spec],
        out_specs=[o_spec, o_spec],
        out_shape=[jax.ShapeDtypeStruct((M, N), BF16), jax.ShapeDtypeStruct((M, N), BF16)],
        compiler_params=_cparams(("parallel", "parallel")),
    )(dx, wd, gate, up)


def _ffn_dh(name, dgate, dup, wg, wu, dep, x, gain, dres):
    (M, K), N = dgate.shape, wg.shape[1]
    tm, tk = _first_divisor(M, (512, 256, 128)), _first_divisor(K, (1024, 512, 384, 256, 128))
    nk = K // tk

    def body(dg_ref, du_ref, wg_ref, wu_ref, x_ref, gain_ref, dres_ref, dep_ref, o_ref, dgain_ref, acc):
        i, k = pl.program_id(0), pl.program_id(1)

        @pl.when(k == 0)
        def _():
            acc[...] = jnp.zeros_like(acc)

        acc[...] += _mm(dg_ref[...], wg_ref[...], _NN) + _mm(du_ref[...], wu_ref[...], _NN)

        @pl.when((k == nk - 1) & (i == 0))
        def _():
            dgain_ref[...] = jnp.zeros_like(dgain_ref)

        @pl.when(k == nk - 1)
        def _():
            _, vjp = jax.vjp(_rms, x_ref[...], gain_ref[...])
            dx, dgain = vjp(acc[...])
            o_ref[...] = dres_ref[...] + dx
            dgain_ref[...] += dgain

    a_spec = pl.BlockSpec((tm, tk), lambda i, k: (i, k))
    w_spec = pl.BlockSpec((tk, N), lambda i, k: (k, 0))
    row_spec = pl.BlockSpec((tm, N), lambda i, k: (i, 0))
    gain_spec = pl.BlockSpec((1, N), lambda i, k: (0, 0))
    return pl.pallas_call(
        body, name=name, grid=(M // tm, nk),
        in_specs=[a_spec, a_spec, w_spec, w_spec, row_spec, gain_spec, row_spec, pl.BlockSpec(memory_space=pl.ANY)],
        out_specs=[row_spec, gain_spec],
        out_shape=[jax.ShapeDtypeStruct((M, N), F32), jax.ShapeDtypeStruct((1, N), F32)],
        scratch_shapes=[pltpu.VMEM((tm, N), F32)], compiler_params=_cparams(("arbitrary", "arbitrary")),
    )(dgate, dup, wg, wu, x, gain, dres, dep)


def _ffn_fwd(tag, x, g, wg, wu, wd):
    h, gate, up, act = _ffn_in(tag + "_in", x, g, wg, wu)
    if callable(wd):
        wd = wd(act)
    out = _mmw(tag + "_down", act, wd, "nn", res=x, scale=0.5, whole_k=True)
    return out, (x, h, gate, up, act), wd


def _ffn_bwd(tag, saved, g, wg, wu, wd, dx, on_dw):
    x, h, gate, up, act = saved
    dgate, dup = _ffn_dact(tag + "_dact", dx, wd, gate, up)
    dwd = _matmul(tag + "_dwd", act, dx, "tn", scale=0.5, out_dtype=BF16)
    dwg = _matmul(tag + "_dwg", dgate, h, "tn", out_dtype=BF16, whole_k=True)
    dwu = _matmul(tag + "_dwu", dup, h, "tn", out_dtype=BF16, whole_k=True)
    token = on_dw({tag + "_w_gate": dwg, tag + "_w_up": dwu, tag + "_w_down": dwd}, dwu)
    return _ffn_dh(tag + "_dh", dgate, dup, wg, wu, token, x, g, dx)


def _rope_cos_sin(positions):
    half = ROPE_DIM // 2
    inv_freq = jnp.power(jnp.float32(ROPE_THETA), -jnp.arange(half, dtype=F32) * (2.0 / ROPE_DIM))
    head = jnp.zeros((ATT_HEAD_DIM,), F32).at[:ROPE_DIM].set(jnp.concatenate([inv_freq, inv_freq]))
    row = jnp.tile(head, LANES // ATT_HEAD_DIM)[None, :]
    T = positions.shape[0]
    return _rowwise("rope_tables", _rope_tables, [_In(positions), _In(row, rows=False)], [_Out(LANES), _Out(LANES)], T, 1024)


def _prep_fwd(name, src, width, base, g, cos, sin):
    return _rowwise(name, _qk_prep, [_In(src, width, base), _In(g, rows=False), _In(cos), _In(sin)],
                    [_Out(width)], src.shape[0], 512)[0]


def _prep_bwd(name, src, width, base, g, cos, sin, dout, into=None):
    def fn(tv, gv, cv, sv, dv):
        _, vjp = jax.vjp(lambda a, b: _qk_prep(a, b, cv, sv), tv, gv)
        return vjp(dv)

    dsrc = _Out(width, BF16) if into is None else _Out(0, width=width, into=into, base=base)
    return _rowwise(name, fn, [_In(src, width, base), _In(g, rows=False), _In(cos), _In(sin), _In(dout)],
                    [dsrc, _Out(width, rows=False)], src.shape[0], 512)


def _mix_fwd(x, cos, sin, B, S, p):
    T = B * S
    h, proj = _norm_matmul("mix_proj", x, p["mix_norm"], p["w_in"])
    proj3 = proj.reshape(B, S, IN_PAD)
    q_gain = jnp.tile(p["attn_q_norm"], (1, ATT_HEADS))
    k_gain = jnp.tile(p["attn_k_norm"], (1, ATT_KV_HEADS))
    q_r = _prep_fwd("q_prep", proj, ATT_WIDTH, C_QA // ATT_WIDTH, q_gain, cos, sin)
    k_r = _prep_fwd("k_prep", proj, ATT_KV_WIDTH, C_KA // ATT_KV_WIDTH, k_gain, cos, sin)
    qh = q_r.reshape(B, S, ATT_WIDTH)
    kh = k_r.reshape(B, S, ATT_KV_WIDTH)
    sink = p["attn_sink"].reshape(ATT_KV_HEADS, ATT_GROUP, 1, 1)
    y_a = _attn_fwd(qh, kh, proj3, sink).reshape(T, ATT_WIDTH)

    qk_c = _conv_fwd(proj3, p["conv_w8"])
    hf, hb, *states = _mlstm_fwd(qk_c, proj3, p["gate_bias"])
    hf2, hb2 = hf.reshape(T, MLSTM_WIDTH), hb.reshape(T, MLSTM_WIDTH)
    DH = MLSTM_HEAD_DIM
    y_m = _rowwise("mlstm_out", _mlstm_combine,
                   [_In(hf2, DH, split=True), _In(hb2, DH, split=True), _In(proj, DH, C_OM // DH, split=True),
                    _In(p["mlstm_out_norm"], DH, split=True, rows=False)],
                   [_Out(MLSTM_WIDTH, BF16, DH, split=True)], T, 1024, ncol=MLSTM_HEADS)[0]

    za = _mmw("branch_a", y_a, p["w_branch_attn"], "nn")
    zm = _mmw("branch_m", y_m, p["w_branch_mlstm"], "nn")
    W = 512
    merged = _rowwise("merge", _merge,
                      [_In(proj, W, C_GMERGE // W, split=True), _In(proj, W, (C_GMERGE + D_MODEL) // W, split=True),
                       _In(za, W, split=True), _In(zm, W, split=True)],
                      [_Out(D_MODEL, BF16, W, split=True)], T, 512, ncol=D_MODEL // W)[0]
    out = _mmw("mix_out", merged, p["w_out"], "nn", res=x)
    saved = dict(x=x, h=h, proj=proj, q_gain=q_gain, k_gain=k_gain, qh=qh, kh=kh, sink=sink, y_a=y_a, qk_c=qk_c,
                 hf=hf2, hb=hb2, states=states, y_m=y_m, za=za, zm=zm, merged=merged)
    return out, saved


def _mix_bwd(sv, cos, sin, B, S, p, dx, on_dw):
    T = B * S
    DH = MLSTM_HEAD_DIM
    proj = sv["proj"]
    proj3 = proj.reshape(B, S, IN_PAD)
    g = {}
    dmerged = _mmw("mix_dmerged", dx, p["w_out"], "nt")
    g["w_out"] = _matmul("mix_dwout", sv["merged"], dx, "tn", out_dtype=BF16)
    dproj = lax.empty((T, IN_PAD), BF16)

    def merge_bwd(ga, gm, za, zm, dm):
        _, vjp = jax.vjp(_merge, ga, gm, za, zm)
        dga, dgm, dza, dzm = vjp(dm)
        return jnp.concatenate([dga, dgm], axis=1), dza, dzm

    dproj, dza, dzm = _rowwise(
        "merge_bwd", merge_bwd,
        [_In(proj, D_MODEL, C_GMERGE // D_MODEL), _In(proj, D_MODEL, C_GMERGE // D_MODEL + 1),
         _In(sv["za"]), _In(sv["zm"]), _In(dmerged)],
        [_Out(0, width=2 * D_MODEL, into=dproj, base=C_GMERGE // (2 * D_MODEL)), _Out(D_MODEL, BF16), _Out(D_MODEL, BF16)],
        T, ROW_BLOCK)
    dya = _mmw("branch_a_dx", dza, p["w_branch_attn"], "nt")
    g["w_branch_attn"] = _matmul("branch_a_dw", sv["y_a"], dza, "tn", out_dtype=BF16)
    dym = _mmw("branch_m_dx", dzm, p["w_branch_mlstm"], "nt")
    g["w_branch_mlstm"] = _matmul("branch_m_dw", sv["y_m"], dzm, "tn", out_dtype=BF16)

    def combine_bwd(hf, hb, o_pre, gn, dy):
        _, vjp = jax.vjp(_mlstm_combine, hf, hb, o_pre, gn)
        dhf, _, do, dg = vjp(dy)
        return dhf, do, dg

    dh, dproj, g["mlstm_out_norm"] = _rowwise(
        "mlstm_out_bwd", combine_bwd,
        [_In(sv["hf"], DH, split=True), _In(sv["hb"], DH, split=True), _In(proj, DH, C_OM // DH, split=True),
         _In(p["mlstm_out_norm"], DH, split=True, rows=False), _In(dym, DH, split=True)],
        [_Out(MLSTM_WIDTH, F32, DH, split=True), _Out(0, width=DH, split=True, into=dproj, base=C_OM // DH),
         _Out(MLSTM_WIDTH, F32, DH, split=True, rows=False)], T, 1024, ncol=MLSTM_HEADS)
    dqk_f, dqk_b, dv_f, dv_b, dg_f, dg_b = _mlstm_bwd(sv["qk_c"], proj3, p["gate_bias"], sv["states"],
                                                       dh.reshape(B, S, MLSTM_WIDTH))
    dproj, g["gate_bias"] = _rowwise(
        "mlstm_dsum_gates", lambda a, b: (a + b, jnp.sum(a + b, axis=0, keepdims=True)),
        [_In(dg_f.reshape(T, LANES)), _In(dg_b.reshape(T, LANES))],
        [_Out(0, width=LANES, into=dproj, base=C_GATES // LANES), _Out(LANES, rows=False)], T, 1024)
    dproj = _rowwise(
        "mlstm_dsum_v", lambda a, b: a + b, [_In(dv_f.reshape(T, MLSTM_WIDTH)), _In(dv_b.reshape(T, MLSTM_WIDTH))],
        [_Out(0, width=MLSTM_WIDTH, into=dproj, base=C_VM // MLSTM_WIDTH)], T, 1024)[0]
    dqk, g["conv_w8"] = _conv_bwd(proj3, p["conv_w8"], dqk_f, dqk_b)

    dqh, dkh, dvh, dsink = _attn_bwd(sv["qh"], sv["kh"], proj3, sv["sink"], dya.reshape(B, S, ATT_WIDTH))
    g["attn_sink"] = dsink.reshape(1, ATT_HEADS)
    dva = dvh.reshape(T, ATT_KV_WIDTH)
    dproj, dq_gain = _prep_bwd("q_prep_bwd", proj, ATT_WIDTH, C_QA // ATT_WIDTH, sv["q_gain"], cos, sin,
                               dqh.reshape(T, ATT_WIDTH), into=dproj)
    dka, dk_gain = _prep_bwd("k_prep_bwd", proj, ATT_KV_WIDTH, C_KA // ATT_KV_WIDTH, sv["k_gain"], cos, sin,
                             dkh.reshape(T, ATT_KV_WIDTH))
    g["attn_q_norm"] = jnp.sum(dq_gain.reshape(ATT_HEADS, ATT_HEAD_DIM), axis=0, keepdims=True)
    g["attn_k_norm"] = jnp.sum(dk_gain.reshape(ATT_KV_HEADS, ATT_HEAD_DIM), axis=0, keepdims=True)

    dproj = dproj.at[:, C_QK:C_QK + 2 * MLSTM_WIDTH].set(dqk.reshape(T, 2 * MLSTM_WIDTH))
    dproj = dproj.at[:, C_KA:C_KA + ATT_KV_WIDTH].set(dka)
    dproj = dproj.at[:, C_VA:C_VA + ATT_KV_WIDTH].set(dva.astype(BF16))
    dwin = _matmul("mix_dwin", sv["h"], dproj, "tn", out_dtype=BF16)
    token = on_dw({"w_in": _w_in_to_slots(dwin), "w_branch_attn": g.pop("w_branch_attn"),
                   "w_branch_mlstm": g.pop("w_branch_mlstm"), "w_out": g.pop("w_out")}, dwin)
    dh2 = _matmul("mix_dh", dproj, p["w_in"], "nt", dep=token)
    dx_new, g["mix_norm"] = _rms_bwd("mix_dnorm", sv["x"], p["mix_norm"], dh2, dx)
    return dx_new, g


def _loss_and_grad(x, g, target):
    T = x.shape[0]

    def loss_fn(xv, gv, tv):
        err = jnp.square(_rms(xv, gv) - tv)
        return 0.5 * jnp.sum(jnp.mean(err, axis=-1, keepdims=True), axis=0, keepdims=True)

    def fn(xv, gv, tv):
        val, vjp = jax.vjp(lambda a, b: loss_fn(a, b, tv), xv, gv)
        dx, dg = vjp(jnp.ones((1, 1), F32))
        return val, dx, dg

    return _rowwise("loss_head", fn, [_In(x), _In(g, rows=False), _In(target)],
                    [_Out(1, rows=False), _Out(D_MODEL), _Out(D_MODEL, rows=False)], T, ROW_BLOCK)


def _block_norm_fwd(x, g):
    T = x.shape[0]
    return _rowwise("block_norm", _rms, [_In(x), _In(g, rows=False)], [_Out(D_MODEL)], T, ROW_BLOCK)[0]


def _block_norm_bwd(x, g, dy):
    T = x.shape[0]

    def fn(xv, gv, dv):
        _, vjp = jax.vjp(_rms, xv, gv)
        return vjp(dv)

    return _rowwise("block_norm_bwd", fn, [_In(x), _In(g, rows=False), _In(dy)],
                    [_Out(D_MODEL), _Out(D_MODEL, rows=False)], T, ROW_BLOCK)


def _qk_perm_cols(t, axis):
    q, k = jnp.split(t, 2, axis=axis)
    parts = []
    for h in range(MLSTM_HEADS):
        sl = [slice(None)] * t.ndim
        sl[axis] = slice(h * MLSTM_HEAD_DIM, (h + 1) * MLSTM_HEAD_DIM)
        parts += [q[tuple(sl)], k[tuple(sl)]]
    return jnp.concatenate(parts, axis=axis)


def _qk_unperm_cols(t, axis):
    qs, ks = [], []
    for h in range(MLSTM_HEADS):
        sl = [slice(None)] * t.ndim
        sl[axis] = slice(2 * h * MLSTM_HEAD_DIM, (2 * h + 1) * MLSTM_HEAD_DIM)
        qs.append(t[tuple(sl)])
        sl[axis] = slice((2 * h + 1) * MLSTM_HEAD_DIM, (2 * h + 2) * MLSTM_HEAD_DIM)
        ks.append(t[tuple(sl)])
    return jnp.concatenate(qs + ks, axis=axis)


def _w_in_arrange(w):
    qa, ka, va, qm, km, vm, om, gm, gmerge = jnp.split(w, np.cumsum(
        (ATT_WIDTH, ATT_KV_WIDTH, ATT_KV_WIDTH, MLSTM_WIDTH, MLSTM_WIDTH, MLSTM_WIDTH, MLSTM_WIDTH, MLSTM_N_GATES))[:].tolist(), axis=1)
    qk = _qk_perm_cols(jnp.concatenate([qm, km], axis=1), 1)
    pad = jnp.zeros((w.shape[0], LANES - MLSTM_N_GATES), w.dtype)
    return jnp.concatenate([gmerge, qk, vm, om, qa, ka, va, gm, pad], axis=1)


def _w_in_restore(w):
    gmerge = w[:, C_GMERGE:C_GMERGE + 2 * D_MODEL]
    qk = _qk_unperm_cols(w[:, C_QK:C_QK + 2 * MLSTM_WIDTH], 1)
    vm, om = w[:, C_VM:C_VM + MLSTM_WIDTH], w[:, C_OM:C_OM + MLSTM_WIDTH]
    qa, ka, va = w[:, C_QA:C_QA + ATT_WIDTH], w[:, C_KA:C_KA + ATT_KV_WIDTH], w[:, C_VA:C_VA + ATT_KV_WIDTH]
    gm = w[:, C_GATES:C_GATES + MLSTM_N_GATES]
    return jnp.concatenate([qa, ka, va, qk, vm, om, gm, gmerge], axis=1)


BIG = ("ffn1_w_gate", "ffn1_w_up", "ffn1_w_down", "w_in", "mlstm_conv_w", "w_branch_attn", "w_branch_mlstm", "w_out",
       "ffn2_w_gate", "ffn2_w_up", "ffn2_w_down")
MATMUL_W = tuple(n for n in BIG if n != "mlstm_conv_w")
SMALL = ("ffn1_norm", "mix_norm", "mlstm_gate_bias", "attn_q_norm", "attn_k_norm", "attn_sink", "mlstm_conv_b",
         "mlstm_out_norm", "ffn2_norm", "block_out_norm")
WEIGHTS = ("ffn1_norm", "ffn1_w_gate", "ffn1_w_up", "ffn1_w_down", "mix_norm", "w_in", "mlstm_gate_bias", "attn_q_norm",
           "attn_k_norm", "attn_sink", "mlstm_conv_w", "mlstm_conv_b", "mlstm_out_norm", "w_branch_attn", "w_branch_mlstm",
           "w_out", "ffn2_norm", "ffn2_w_gate", "ffn2_w_up", "ffn2_w_down", "block_out_norm")
PACK_COLS = 1024


def _padded_rows(n_elems):
    return -(-n_elems // PACK_COLS)


def _pack_flat(arrs, dtype, row_multiple):
    parts = []
    for a in arrs:
        flat = a.reshape(-1).astype(dtype)
        pad = _padded_rows(flat.shape[0]) * PACK_COLS - flat.shape[0]
        parts.append(jnp.pad(flat, (0, pad)) if pad else flat)
    flat = jnp.concatenate(parts)
    rows = flat.shape[0] // PACK_COLS
    extra = (-rows) % row_multiple
    if extra:
        flat = jnp.pad(flat, (0, extra * PACK_COLS))
    return flat.reshape(-1, PACK_COLS)


def _pack_rows(name, pieces, total_rows):
    def body(*refs):
        o_ref = refs[-1]
        o_ref[...] = jnp.zeros_like(o_ref)
        at = 0
        for ref, (arr, r0, nr) in zip(refs[:-1], pieces):
            o_ref[at:at + nr, 0:arr.shape[1]] = ref[r0:r0 + nr, :].astype(F32)
            at += nr

    return pl.pallas_call(body, name=name, out_shape=jax.ShapeDtypeStruct((total_rows, PACK_COLS), F32))(
        *[p[0] for p in pieces])


def _unpack_flat(buf, shapes, lead=()):
    flat = buf.reshape(lead + (-1,))
    out, off = [], 0
    for s in shapes:
        n = int(np.prod(s))
        out.append(flat[..., off:off + n].reshape(lead + tuple(s)))
        off += _padded_rows(n) * PACK_COLS
    return out


class _Lay:
    def __init__(self, shard, axis, width):
        self.shard, self.axis, self.width = shard, axis, width
        self.padded = tuple(width if a == axis else s for a, s in enumerate(shard))
        self.whole = tuple(N_DEV * width if a == axis else s for a, s in enumerate(shard))


_FF_ROW = _Lay((FF_SHARD, D_MODEL), 0, FF_SHARD_PAD)
TRANSPOSED = ("ffn1_w_gate", "ffn1_w_up", "ffn2_w_gate", "ffn2_w_up")
LAYOUTS = {
    "ffn1_w_gate": _FF_ROW, "ffn1_w_up": _FF_ROW, "ffn1_w_down": _FF_ROW,
    "ffn2_w_gate": _FF_ROW, "ffn2_w_up": _FF_ROW, "ffn2_w_down": _FF_ROW,
    "w_in": _Lay((D_MODEL, IN_WIDTH // N_DEV), 0, D_MODEL),
    "mlstm_conv_w": _Lay((3, 2 * MLSTM_WIDTH // N_DEV), 1, 2 * MLSTM_WIDTH // N_DEV),
    "w_branch_attn": _Lay((ATT_WIDTH, D_MODEL // N_DEV), 1, D_MODEL // N_DEV),
    "w_branch_mlstm": _Lay((MLSTM_WIDTH, D_MODEL // N_DEV), 1, D_MODEL // N_DEV),
    "w_out": _Lay((D_MODEL // N_DEV, D_MODEL), 0, D_MODEL // N_DEV),
}


def _window(ref, axis, j, width):
    idx = [slice(None)] * len(ref.shape)
    idx[axis] = pl.ds(pl.multiple_of(j * width, width), width)
    return ref.at[tuple(idx)]


ANY = pl.BlockSpec(memory_space=pl.ANY)


def _mesh_pos():
    return lax.axis_index("x"), lax.axis_index("y"), lax.axis_index("c")


def _all_gather(name, shard, vmem=False):
    R, C = shard.shape
    space = pl.BlockSpec(memory_space=pltpu.VMEM) if vmem else ANY

    def body(x_ref, out_ref, send_sems, recv_sems, local_sem):
        x, y, c = _mesh_pos()
        me, sibling = (x, y, c), (x, y, 1 - c)
        chips = [(1 - x, y), (x, 1 - y), (1 - x, 1 - y)]

        def slot(px, py, pc):
            return out_ref.at[4 * px + 2 * py + pc]

        def copy(k, block, to, src=None):
            return pltpu.make_async_remote_copy(
                src_ref=slot(*block) if src is None else src, dst_ref=slot(*block),
                send_sem=send_sems.at[k], recv_sem=recv_sems.at[k], device_id=to, device_id_type=MESH)

        mine = pltpu.make_async_copy(x_ref, slot(*me), local_sem)
        mine.start()
        first = [copy(0, me, sibling, src=x_ref)]
        first += [copy(1 + j, me, (*chip, c), src=x_ref) for j, chip in enumerate(chips)]
        for cp in first:
            cp.start()
        passed = [copy(4 + j, (*chip, c), sibling) for j, chip in enumerate(chips)]
        for j, chip in enumerate(chips):
            copy(1 + j, (*chip, c), me).wait_recv()
            passed[j].start()
        copy(0, sibling, me).wait_recv()
        for j, chip in enumerate(chips):
            copy(4 + j, (*chip, 1 - c), me).wait_recv()
        for cp in first + passed:
            cp.wait_send()
        mine.wait()

    return pl.pallas_call(
        body, name=name, out_shape=jax.ShapeDtypeStruct((N_DEV, R, C), shard.dtype),
        in_specs=[space], out_specs=space,
        scratch_shapes=[pltpu.SemaphoreType.DMA((7,)), pltpu.SemaphoreType.DMA((7,)), pltpu.SemaphoreType.DMA],
    )(shard)


HBM = pl.BlockSpec(memory_space=pltpu.HBM)
SEM = pl.BlockSpec(memory_space=pltpu.SEMAPHORE)
SPLIT_COPY = pltpu.CompilerParams(has_side_effects=pltpu.SideEffectType.DATAFLOW_SIDE_EFFECTING)
N_PEERS = N_DEV - 1


def _peers(x, y, c):
    return [(x, y, 1 - c), (1 - x, y, c), (x, 1 - y, c), (1 - x, 1 - y, c),
            (1 - x, y, 1 - c), (x, 1 - y, 1 - c), (1 - x, 1 - y, 1 - c)]


def _dev_index(pos):
    return 4 * pos[0] + 2 * pos[1] + pos[2]


def _place_own(name, stacks, layer, lays):
    nt = len(stacks)
    me = _dev_index(_mesh_pos())

    def body(me_ref, *refs):
        for x_ref, s_ref, o_ref, lay in zip(refs[:nt], refs[nt:2 * nt], refs[2 * nt:], lays):
            rows = lay.shard[0]
            if lay.padded != lay.shard:
                s_ref[...] = jnp.zeros_like(s_ref)
            s_ref[0:rows, :] = x_ref[...].astype(BF16)
            o_ref[...] = s_ref[...]

    def window_spec(lay):
        if lay.axis == 0:
            return pl.BlockSpec(lay.padded, lambda i, me_ref: (me_ref[0], 0))
        return pl.BlockSpec(lay.padded, lambda i, me_ref: (0, me_ref[0]))

    for lay in lays:
        assert lay.padded[1] == lay.shard[1], "only rows are padded"
    res = pl.pallas_call(
        body, name=name,
        grid_spec=pltpu.PrefetchScalarGridSpec(
            num_scalar_prefetch=1, grid=(1,),
            in_specs=[pl.BlockSpec((None,) + lay.shard, lambda i, me_ref: (layer, 0, 0)) for lay in lays],
            out_specs=[pl.BlockSpec(lay.padded, lambda i, me_ref: (0, 0)) for lay in lays] + [window_spec(lay) for lay in lays]),
        out_shape=[jax.ShapeDtypeStruct(lay.padded, BF16) for lay in lays] + [jax.ShapeDtypeStruct(lay.whole, BF16) for lay in lays],
        compiler_params=_cparams(("arbitrary",)),
    )(me.reshape(1).astype(jnp.int32), *stacks)
    return list(res[:nt]), list(res[nt:])


NEAR_PEERS = 4


def _gather_start(name, shards, lands, lays, groups, n_peers, after):
    nt, ng = len(shards), len(groups)

    def body(*refs):
        x_refs, land_refs = refs[:nt], refs[nt:2 * nt]
        sems = refs[2 * nt + 1:2 * nt + 1 + 2 * ng]
        pos = _mesh_pos()
        me = _dev_index(pos)
        for g, tens in enumerate(groups):
            for i, t in enumerate(tens):
                for k, peer in enumerate(_peers(*pos)[:n_peers[g]]):
                    pltpu.make_async_remote_copy(
                        src_ref=x_refs[t], dst_ref=_window(land_refs[t], lays[t].axis, me, lays[t].width),
                        send_sem=sems[2 * g].at[n_peers[g] * i + k], recv_sem=sems[2 * g + 1].at[n_peers[g] * i + k],
                        device_id=peer, device_id_type=MESH).start()

    sem_shapes = []
    for g, tens in enumerate(groups):
        sem_shapes += [pltpu.SemaphoreType.DMA((n_peers[g] * len(tens),))] * 2
    thru = [pltpu.HBM(s.shape, s.dtype) for s in shards] + [pltpu.HBM(lay.whole, s.dtype) for s, lay in zip(shards, lays)]
    args = [pltpu.with_memory_space_constraint(s, pltpu.HBM) for s in shards]
    args += [pltpu.with_memory_space_constraint(ld, pltpu.HBM) for ld in lands]
    res = pl.pallas_call(
        body, name=name, out_shape=tuple(sem_shapes + thru), in_specs=[HBM] * (2 * nt) + [ANY],
        out_specs=tuple([SEM] * (2 * ng) + [HBM] * (2 * nt)),
        input_output_aliases={t: 2 * ng + t for t in range(2 * nt)}, compiler_params=SPLIT_COPY,
    )(*args, after)
    sems = [(res[2 * g], res[2 * g + 1]) for g in range(ng)]
    return sems, list(res[2 * ng:2 * ng + nt]), list(res[2 * ng + nt:])


def _gather_wait(name, sems, shards, lands, lays, n_peers, after):
    nt = len(shards)
    send_sems, recv_sems = sems

    def body(*refs):
        x_refs, land_refs = refs[:nt], refs[nt:2 * nt]
        send_ref, recv_ref = refs[2 * nt], refs[2 * nt + 1]
        pos = _mesh_pos()
        for t in range(nt):
            for k, peer in enumerate(_peers(*pos)[:n_peers]):
                cp = pltpu.make_async_remote_copy(
                    src_ref=x_refs[t], dst_ref=_window(land_refs[t], lays[t].axis, _dev_index(peer), lays[t].width),
                    send_sem=send_ref.at[n_peers * t + k], recv_sem=recv_ref.at[n_peers * t + k],
                    device_id=peer, device_id_type=MESH)
                cp.wait_send()
                cp.wait_recv()

    thru = [pltpu.HBM(s.shape, s.dtype) for s in shards] + [pltpu.HBM(ld.shape, ld.dtype) for ld in lands]
    res = pl.pallas_call(
        body, name=name, out_shape=tuple(thru), in_specs=[HBM] * (2 * nt) + [SEM, SEM, ANY],
        out_specs=tuple([HBM] * (2 * nt)), input_output_aliases={t: t for t in range(2 * nt)},
        compiler_params=SPLIT_COPY,
    )(*shards, *lands, send_sems, recv_sems, after)
    return list(res[nt:])


def _forward_to_sibling(name, lands, lays):
    nt = len(lands)

    def body(*refs):
        land_refs = refs[nt:2 * nt]
        send_sems, recv_sems = refs[2 * nt:]
        x, y, c = _mesh_pos()
        chips = [(1 - x, y), (x, 1 - y), (1 - x, 1 - y)]

        def copy(t, j, core):
            win = _window(land_refs[t], lays[t].axis, _dev_index((*chips[j], core)), lays[t].width)
            return pltpu.make_async_remote_copy(
                src_ref=win, dst_ref=win, send_sem=send_sems.at[3 * t + j], recv_sem=recv_sems.at[3 * t + j],
                device_id=(x, y, 1 - c), device_id_type=MESH)

        sends = [copy(t, j, c) for t in range(nt) for j in range(3)]
        for cp in sends:
            cp.start()
        for t in range(nt):
            for j in range(3):
                copy(t, j, 1 - c).wait_recv()
        for cp in sends:
            cp.wait_send()

    return pl.pallas_call(
        body, name=name, out_shape=[jax.ShapeDtypeStruct(ld.shape, ld.dtype) for ld in lands],
        in_specs=[ANY] * nt, out_specs=[ANY] * nt, input_output_aliases={t: t for t in range(nt)},
        scratch_shapes=[pltpu.SemaphoreType.DMA((3 * nt,)), pltpu.SemaphoreType.DMA((3 * nt,))],
    )(*lands)


def _pair_exchange(name, grads, lays):
    nt = len(grads)

    def body(*refs):
        g_refs, land_refs = refs[:nt], refs[nt:2 * nt]
        send_sems, recv_sems = refs[2 * nt:]
        x, y, c = _mesh_pos()
        copies = []
        for t in range(nt):
            for chip in range(4):
                copies.append(pltpu.make_async_remote_copy(
                    src_ref=_window(g_refs[t], lays[t].axis, 2 * chip + (1 - c), lays[t].width), dst_ref=land_refs[t].at[chip],
                    send_sem=send_sems.at[4 * t + chip], recv_sem=recv_sems.at[4 * t + chip],
                    device_id=(x, y, 1 - c), device_id_type=MESH))
        for cp in copies:
            cp.start()
        for cp in copies:
            cp.wait_recv()
        for cp in copies:
            cp.wait_send()

    out_shape = [jax.ShapeDtypeStruct((4,) + lay.padded, g.dtype) for g, lay in zip(grads, lays)]
    return pl.pallas_call(
        body, name=name, out_shape=out_shape, in_specs=[ANY] * nt, out_specs=[ANY] * nt,
        scratch_shapes=[pltpu.SemaphoreType.DMA((4 * nt,)), pltpu.SemaphoreType.DMA((4 * nt,))],
    )(*grads)


def _pair_sum(name, whole, landed, lay, out_dtype):
    R, C = lay.padded
    br = _first_divisor(R, (512, 384, 256, 128, 64, 32, 16, 8))
    nb = R // br
    if lay.axis == 0:
        mine_spec = pl.BlockSpec((br, C), lambda k, i, c_ref: ((2 * k + c_ref[0]) * nb + i, 0))
    else:
        mine_spec = pl.BlockSpec((br, C), lambda k, i, c_ref: (i, 2 * k + c_ref[0]))

    def body(c_ref, mine_ref, sib_ref, o_ref):
        o_ref[0] = (mine_ref[...].astype(F32) + sib_ref[0].astype(F32)).astype(out_dtype)

    c = lax.axis_index("c")
    return pl.pallas_call(
        body, name=name,
        grid_spec=pltpu.PrefetchScalarGridSpec(
            num_scalar_prefetch=1, grid=(4, nb),
            in_specs=[mine_spec, pl.BlockSpec((1, br, C), lambda k, i, c_ref: (k, i, 0))],
            out_specs=pl.BlockSpec((1, br, C), lambda k, i, c_ref: (k, i, 0))),
        out_shape=jax.ShapeDtypeStruct((4, R, C), out_dtype),
        compiler_params=_cparams(("parallel", "parallel")),
    )(c.reshape(1).astype(jnp.int32), whole, landed)


def _chip_start(name, sums):
    nt = len(sums)

    def body(*refs):
        s_refs, land_refs = refs[:nt], refs[nt:2 * nt]
        send_sems, recv_sems = refs[2 * nt], refs[2 * nt + 1]
        x, y, c = _mesh_pos()
        my_chip = 2 * x + y
        for t in range(nt):
            for j, (px, py) in enumerate([(1 - x, y), (x, 1 - y), (1 - x, 1 - y)]):
                pltpu.make_async_remote_copy(
                    src_ref=s_refs[t].at[2 * px + py], dst_ref=land_refs[t].at[my_chip],
                    send_sem=send_sems.at[3 * t + j], recv_sem=recv_sems.at[3 * t + j],
                    device_id=(px, py, c), device_id_type=MESH).start()

    thru = [pltpu.HBM(s.shape, s.dtype) for s in sums] * 2
    args = [pltpu.with_memory_space_constraint(s, pltpu.HBM) for s in sums]
    args += [pltpu.with_memory_space_constraint(lax.empty(s.shape, s.dtype), pltpu.HBM) for s in sums]
    res = pl.pallas_call(
        body, name=name, out_shape=tuple([pltpu.SemaphoreType.DMA((3 * nt,))] * 2 + thru), in_specs=[HBM] * (2 * nt),
        out_specs=tuple([SEM, SEM] + [HBM] * (2 * nt)), input_output_aliases={t: 2 + t for t in range(2 * nt)},
        compiler_params=SPLIT_COPY,
    )(*args)
    return (res[0], res[1]), list(res[2:2 + nt]), list(res[2 + nt:])


def _chip_wait(name, sems, sums, lands, after):
    nt = len(sums)

    def body(*refs):
        s_refs, land_refs = refs[:nt], refs[nt:2 * nt]
        send_sems, recv_sems = refs[2 * nt], refs[2 * nt + 1]
        x, y, c = _mesh_pos()
        my_chip = 2 * x + y
        for t in range(nt):
            for j, (px, py) in enumerate([(1 - x, y), (x, 1 - y), (1 - x, 1 - y)]):
                cp = pltpu.make_async_remote_copy(
                    src_ref=s_refs[t].at[my_chip], dst_ref=land_refs[t].at[2 * px + py],
                    send_sem=send_sems.at[3 * t + j], recv_sem=recv_sems.at[3 * t + j],
                    device_id=(px, py, c), device_id_type=MESH)
                cp.wait_send()
                cp.wait_recv()

    thru = [pltpu.HBM(s.shape, s.dtype) for s in sums] * 2
    res = pl.pallas_call(
        body, name=name, out_shape=tuple(thru), in_specs=[HBM] * (2 * nt) + [SEM, SEM, ANY],
        out_specs=tuple([HBM] * (2 * nt)), input_output_aliases={t: t for t in range(2 * nt)},
        compiler_params=SPLIT_COPY,
    )(*sums, *lands, sems[0], sems[1], after)
    return list(res[:nt]), list(res[nt:])


def _sum_chips(name, own, landed):
    _, R, C = own.shape
    br = _first_divisor(R, (512, 384, 256, 128, 64, 32, 16, 8))
    x, y, _ = _mesh_pos()
    slots = jnp.stack([2 * x + y, 2 * (1 - x) + y, 2 * x + (1 - y), 2 * (1 - x) + (1 - y)]).astype(jnp.int32)

    def body(slot_ref, mine_ref, a_ref, b_ref, c_ref, o_ref):
        o_ref[...] = ((mine_ref[0].astype(F32) + a_ref[0].astype(F32)) + b_ref[0].astype(F32)) + c_ref[0].astype(F32)

    def slot_spec(j):
        return pl.BlockSpec((1, br, C), lambda i, slot_ref: (slot_ref[j], i, 0))

    return pl.pallas_call(
        body, name=name,
        grid_spec=pltpu.PrefetchScalarGridSpec(
            num_scalar_prefetch=1, grid=(R // br,), in_specs=[slot_spec(0), slot_spec(1), slot_spec(2), slot_spec(3)],
            out_specs=pl.BlockSpec((br, C), lambda i, slot_ref: (i, 0))),
        out_shape=jax.ShapeDtypeStruct((R, C), F32), compiler_params=_cparams(("parallel",)),
    )(slots, own, landed, landed, landed)


def _sum_slots(name, slots, n):
    _, R, C = slots.shape
    br = _first_divisor(R, (512, 384, 256, 128, 64, 32, 16, 8))

    def body(s_ref, o_ref):
        acc = s_ref[0].astype(F32)
        for k in range(1, n):
            acc = acc + s_ref[k].astype(F32)
        o_ref[...] = acc

    return pl.pallas_call(
        body, name=name, grid=(R // br,), in_specs=[pl.BlockSpec((n, br, C), lambda i: (0, i, 0))],
        out_specs=pl.BlockSpec((br, C), lambda i: (i, 0)), out_shape=jax.ShapeDtypeStruct((R, C), F32),
        compiler_params=_cparams(("parallel",)),
    )(slots)


def _reduce_scatter_start(tag, names, grads):
    lays = [LAYOUTS[n] for n in names]
    landed = _pair_exchange("grads_pair_" + names[0], grads, lays)
    sums = [_pair_sum("grads_pairsum_" + n, g, ld, lay, BF16) for n, g, ld, lay in zip(names, grads, landed, lays)]
    sems, sums, lands = _chip_start(tag + "_chips_start", sums)
    return tag, names, sems, sums, lands


def _reduce_scatter_finish(pending, after):
    tag, names, sems, sums, lands = pending
    own, got = _chip_wait(tag + "_chips_wait", sems, sums, lands, after)
    return [_sum_chips("grads_sum_" + n, o, s) for n, o, s in zip(names, own, got)]


def _adamw_math(w, g, m, v):
    m = ADAM_B1 * m + (1.0 - ADAM_B1) * g
    v = ADAM_B2 * v + (1.0 - ADAM_B2) * jnp.square(g)
    m_hat = m / (1.0 - ADAM_B1 ** ADAM_STEP)
    v_hat = v / (1.0 - ADAM_B2 ** ADAM_STEP)
    delta = -ADAM_LR * (m_hat / (jnp.sqrt(v_hat) + ADAM_EPS) + ADAM_WD * w)
    return delta, m, v


def _adamw_layers(name, w, totals, m, v):
    _, R, C = w.shape
    br = _first_divisor(R, (512, 176, 128, 64, 32, 16, 8))
    Cp = totals[0].shape[1]

    def body(w_ref, g0_ref, g1_ref, m_ref, v_ref, g_out, d_out, m_out, v_out):
        g = jnp.where(pl.program_id(0) == 0, g0_ref[:, 0:C], g1_ref[:, 0:C])
        delta, m_new, v_new = _adamw_math(w_ref[0], g, m_ref[0], v_ref[0])
        g_out[0], d_out[0], m_out[0], v_out[0] = g, delta, m_new, v_new

    blk = pl.BlockSpec((1, br, C), lambda l, i: (l, i, 0))
    g_spec = pl.BlockSpec((br, Cp), lambda l, i: (i, 0))
    return pl.pallas_call(
        body, name=name, grid=(DEPTH, R // br), in_specs=[blk, g_spec, g_spec, blk, blk], out_specs=[blk] * 4,
        out_shape=[jax.ShapeDtypeStruct(w.shape, F32)] * 4, compiler_params=_cparams(("parallel", "parallel")),
    )(w, totals[0], totals[1], m, v)


def _adamw(name, w, g, m, v):
    shape = w.shape
    cols = shape[-1]
    rows = int(np.prod(shape[:-1]))
    br = _first_divisor(rows, (512, 352, 256, 128, 64, 32, 16, 8))
    args = [_In(a.reshape(rows, cols)) for a in (w, g, m, v)]
    outs = _rowwise(name, _adamw_math, args, [_Out(cols), _Out(cols), _Out(cols)], rows, br)
    return [o.reshape(shape) for o in outs]


GROUPS = {"ffn1": ("ffn1_w_gate", "ffn1_w_up", "ffn1_w_down"),
          "mix": ("w_in", "w_branch_attn", "w_branch_mlstm", "w_out"),
          "ffn2": ("ffn2_w_gate", "ffn2_w_up", "ffn2_w_down")}
GATHER_GROUPS = {"ffn1_in": ("ffn1_w_gate", "ffn1_w_up"), "ffn1_out": ("ffn1_w_down",),
                 "mix": ("w_in", "w_branch_attn", "w_branch_mlstm", "w_out"),
                 "ffn2_in": ("ffn2_w_gate", "ffn2_w_up"), "ffn2_out": ("ffn2_w_down",)}


def _small_params(small, conv_w, l):
    p = {}
    for n in ("ffn1_norm", "mix_norm", "ffn2_norm", "block_out_norm", "mlstm_out_norm", "attn_q_norm", "attn_k_norm"):
        p[n] = small[n][l][None, :]
    p["attn_sink"] = small["attn_sink"][l]
    p["gate_bias"] = jnp.pad(small["mlstm_gate_bias"][l], (0, LANES - MLSTM_N_GATES))[None, :]
    taps = _qk_perm_cols(conv_w[l], 1)
    conv_b = _qk_perm_cols(small["mlstm_conv_b"][l][None, :], 1)
    p["conv_w8"] = jnp.concatenate([taps, conv_b, jnp.zeros((4, 2 * MLSTM_WIDTH), F32)], axis=0)
    return p


def _w_in_from_slots(slots):
    w_in = slots.reshape(N_DEV, D_MODEL, IN_WIDTH // N_DEV).transpose(1, 0, 2).reshape(D_MODEL, IN_WIDTH)
    return _w_in_arrange(w_in)


def _w_in_to_slots(g):
    return _w_in_restore(g).reshape(D_MODEL, N_DEV, IN_WIDTH // N_DEV).transpose(1, 0, 2).reshape(
        N_DEV * D_MODEL, IN_WIDTH // N_DEV)


def _local_step(x, positions, target, weights_of, small, conv_w, on_grads):
    B, S, _ = x.shape
    T = B * S
    cos, sin = _rope_cos_sin(positions.reshape(T, 1))
    params = [_small_params(small, conv_w, l) for l in range(DEPTH)]
    xs = x.reshape(T, D_MODEL)
    tgt = target.reshape(T, D_MODEL)

    saved = []
    for l, p in enumerate(params):
        p.update(weights_of(l, "ffn1_in", xs))
        x1, s1, p["ffn1_w_down"] = _ffn_fwd("ffn1", xs, p["ffn1_norm"], p["ffn1_w_gate"], p["ffn1_w_up"],
                                            lambda after, l=l: weights_of(l, "ffn1_out", after)["ffn1_w_down"])
        p.update(weights_of(l, "mix", x1))
        p["w_in"] = _w_in_from_slots(p["w_in"])
        x2, s2 = _mix_fwd(x1, cos, sin, B, S, p)
        p.update(weights_of(l, "ffn2_in", x2))
        x3, s3, p["ffn2_w_down"] = _ffn_fwd("ffn2", x2, p["ffn2_norm"], p["ffn2_w_gate"], p["ffn2_w_up"],
                                            lambda after, l=l: weights_of(l, "ffn2_out", after)["ffn2_w_down"])
        saved.append((s1, s2, s3, x3))
        if l + 1 < DEPTH:
            xs = _block_norm_fwd(x3, p["block_out_norm"])

    sm = {}
    loss = None
    dx = None
    for l in reversed(range(DEPTH)):
        p = params[l]
        s1, s2, s3, x3 = saved[l]
        if l == DEPTH - 1:
            loss, dx, dgn = _loss_and_grad(x3, p["block_out_norm"], tgt)
        else:
            dx, dgn = _block_norm_bwd(x3, p["block_out_norm"], dx)
        sm["block_out_norm", l] = (dgn, 0, 1)
        dx, dg = _ffn_bwd("ffn2", s3, p["ffn2_norm"], p["ffn2_w_gate"], p["ffn2_w_up"], p["ffn2_w_down"], dx,
                          functools.partial(on_grads, l, "ffn2"))
        sm["ffn2_norm", l] = (dg, 0, 1)
        dx, g = _mix_bwd(s2, cos, sin, B, S, p, dx, functools.partial(on_grads, l, "mix"))
        dconv = _qk_unperm_cols(g["conv_w8"], 1)
        sm["mlstm_conv_w", l] = (dconv, 0, 3)
        sm["mlstm_conv_b", l] = (dconv, 3, 1)
        for n, key in (("mix_norm", "mix_norm"), ("mlstm_gate_bias", "gate_bias"), ("attn_q_norm", "attn_q_norm"),
                       ("attn_k_norm", "attn_k_norm"), ("attn_sink", "attn_sink"), ("mlstm_out_norm", "mlstm_out_norm")):
            sm[n, l] = (g[key], 0, 1)
        dx, dg = _ffn_bwd("ffn1", s1, p["ffn1_norm"], p["ffn1_w_gate"], p["ffn1_w_up"], p["ffn1_w_down"], dx,
                          functools.partial(on_grads, l, "ffn1"))
        sm["ffn1_norm", l] = (dg, 0, 1)
    return loss, dx.reshape(B, S, D_MODEL), sm


def kernel(x, positions, ffn1_norm, ffn1_w_gate, ffn1_w_up, ffn1_w_down, mix_norm, w_in, mlstm_gate_bias, attn_q_norm, attn_k_norm, attn_sink, mlstm_conv_w, mlstm_conv_b, mlstm_out_norm, w_branch_attn, w_branch_mlstm, w_out, ffn2_norm, ffn2_w_gate, ffn2_w_up, ffn2_w_down, block_out_norm, loss_target, m_ffn1_norm, m_ffn1_w_gate, m_ffn1_w_up, m_ffn1_w_down, m_mix_norm, m_w_in, m_mlstm_gate_bias, m_attn_q_norm, m_attn_k_norm, m_attn_sink, m_mlstm_conv_w, m_mlstm_conv_b, m_mlstm_out_norm, m_w_branch_attn, m_w_branch_mlstm, m_w_out, m_ffn2_norm, m_ffn2_w_gate, m_ffn2_w_up, m_ffn2_w_down, m_block_out_norm, v_ffn1_norm, v_ffn1_w_gate, v_ffn1_w_up, v_ffn1_w_down, v_mix_norm, v_w_in, v_mlstm_gate_bias, v_attn_q_norm, v_attn_k_norm, v_attn_sink, v_mlstm_conv_w, v_mlstm_conv_b, v_mlstm_out_norm, v_w_branch_attn, v_w_branch_mlstm, v_w_out, v_ffn2_norm, v_ffn2_w_gate, v_ffn2_w_up, v_ffn2_w_down, v_block_out_norm):
    args = locals()
    def stored(n, t):
        return t.transpose(0, 2, 1) if n in TRANSPOSED else t

    w = {n: stored(n, args[n]) for n in WEIGHTS}
    m = {n: stored(n, args["m_" + n]) for n in WEIGHTS}
    v = {n: stored(n, args["v_" + n]) for n in WEIGHTS}

    order = [(l, grp) for l in range(DEPTH) for grp in GATHER_GROUPS]
    keys = [(l, n) for l, grp in order for n in GATHER_GROUPS[grp]]
    lays = [LAYOUTS[n] for _, n in keys]
    group_idx, at = {}, 0
    for l, grp in order:
        group_idx[(l, grp)] = list(range(at, at + len(GATHER_GROUPS[grp])))
        at += len(GATHER_GROUPS[grp])
    conv_shape = w["mlstm_conv_w"].shape
    conv_all = _all_gather("conv_all_gather", _pack_flat([w["mlstm_conv_w"]], F32, 8), vmem=True)
    conv_parts = _unpack_flat(conv_all, [conv_shape], lead=(N_DEV,))[0]
    conv_w = jnp.concatenate([conv_parts[j] for j in range(N_DEV)], axis=2)
    small = {n: w[n] for n in SMALL}

    shards, lands = [], []
    for l, grp in order:
        own, whole = _place_own("weights_place_" + grp, [w[n] for n in GATHER_GROUPS[grp]], l,
                                [lays[i] for i in group_idx[(l, grp)]])
        shards += own
        lands += whole
    n_peers = [NEAR_PEERS if (l, grp) in ((0, "ffn1_in"), (0, "ffn1_out"), (0, "mix")) else N_PEERS for l, grp in order]
    sems, shards, lands = _gather_start("weights_gather_start", shards, lands, lays, [group_idx[k] for k in order],
                                        n_peers, conv_all)

    def weights_of(l, grp, after):
        idx, g = group_idx[(l, grp)], order.index((l, grp))
        group_lays = [lays[i] for i in idx]
        whole = _gather_wait(f"weights_gather_wait_{l}_{grp}", sems[g], [shards[i] for i in idx],
                             [lands[i] for i in idx], group_lays, n_peers[g], after)
        if n_peers[g] == NEAR_PEERS:
            whole = _forward_to_sibling("weights_forward_" + grp, whole, group_lays)
        return dict(zip(GATHER_GROUPS[grp], whole))

    totals, pending = {}, []

    def finish(after):
        tag, names = pending[0][0], pending[0][1]
        for n, t in zip(names, _reduce_scatter_finish(pending.pop(0), after)):
            totals[(tag, n)] = t

    def on_grads(l, grp, g, after):
        if pending:
            finish(after)
        names = GROUPS[grp]
        pending.append(_reduce_scatter_start(f"grads_{l}_{grp}", names, [g[n] for n in names]))
        return pending[-1][3][0]

    loss, grad_x, small_g = _local_step(x, positions, loss_target, weights_of, small, conv_w, on_grads)
    finish(grad_x)
    grads, deltas, new_m, new_v = {}, {}, {}, {}
    for grp, names in GROUPS.items():
        for n in names:
            grads[n], deltas[n], new_m[n], new_v[n] = _adamw_layers(
                "adamw_" + n, w[n], [totals[(f"grads_{l}_{grp}", n)] for l in range(DEPTH)], m[n], v[n])

    n_small = DEPTH * len(SMALL)
    taps_at, loss_at, rows = n_small, n_small + 3 * DEPTH, 32
    pieces = [small_g[n, l] for n in SMALL for l in range(DEPTH)]
    pieces += [small_g["mlstm_conv_w", l] for l in range(DEPTH)] + [(loss, 0, 1)]
    small_all = _all_gather("small_all_gather", _pack_rows("small_pack", pieces, rows), vmem=True)
    small_sum = _sum_slots("small_sum", small_all, N_DEV)
    loss_total = small_sum[loss_at, 0]
    x_pos, y_pos, c_pos = _mesh_pos()
    grads["mlstm_conv_w"] = lax.dynamic_slice_in_dim(
        small_sum[taps_at:loss_at].reshape(DEPTH, 3, 2 * MLSTM_WIDTH),
        (4 * x_pos + 2 * y_pos + c_pos) * conv_shape[2], conv_shape[2], axis=2)

    n = "mlstm_conv_w"
    deltas[n], new_m[n], new_v[n] = _adamw("adamw_" + n, w[n], grads[n], m[n], v[n])
    sw, smm, sv = (_pack_rows("small_pack_" + tag, [(d[n], 0, DEPTH) for n in SMALL], rows)
                   for tag, d in (("w", w), ("m", m), ("v", v)))
    sd, snm, snv = _adamw("adamw_small", sw, small_sum, smm, sv)
    for i, n in enumerate(SMALL):
        rows_n, width = slice(DEPTH * i, DEPTH * (i + 1)), w[n].shape[1]
        grads[n], deltas[n], new_m[n], new_v[n] = (buf[rows_n, :width] for buf in (small_sum, sd, snm, snv))

    return (loss_total.reshape(()), grad_x, *[stored(n, d[n]) for d in (grads, deltas, new_m, new_v) for n in WEIGHTS])
```
